```python
import jax, jax.numpy as jnp
from jax import lax
import numpy as np

D_MODEL = 1024
BATCH = 8
SEQ = 4096
DEPTH = 1

N_MLA_HEADS = 8
MLA_Q_RANK = 256
MLA_KV_RANK = 128
MLA_NOPE_DIM = 64
MLA_ROPE_DIM = 32
MLA_V_DIM = 64
ROPE_THETA = 10000.0
Q_BLOCK = 128

N_GDN_HEADS = 8
GDN_HEAD_DIM = 64
GDN_CONV = 4
GDN_CHUNK = 64

MLA_W = N_MLA_HEADS * MLA_V_DIM
GDN_W = N_GDN_HEADS * GDN_HEAD_DIM
D_MIX = MLA_W + GDN_W
D_IN = MLA_Q_RANK + MLA_KV_RANK + MLA_ROPE_DIM + 3 * GDN_W + 2 * N_GDN_HEADS + GDN_W

D_FF = 2816
EPS = 1e-6

kernel_name = "hybrid_mla_gdn_macaron_sandwich"


def rmsnorm(x, g):
    xf = x.astype(jnp.float32)
    y = xf * lax.rsqrt(jnp.mean(xf * xf, axis=-1, keepdims=True) + EPS)
    return (y * g.astype(jnp.float32)).astype(x.dtype)


def l2norm(x):
    xf = x.astype(jnp.float32)
    return xf * lax.rsqrt(jnp.sum(xf * xf, axis=-1, keepdims=True) + EPS)


def swiglu(x, w_gate, w_up, w_down):
    return (jax.nn.silu(x @ w_gate) * (x @ w_up)) @ w_down


def rope_tables(positions):
    half = MLA_ROPE_DIM // 2
    freqs = ROPE_THETA ** (-jnp.arange(half, dtype=jnp.float32) / half)
    ang = positions.astype(jnp.float32)[..., None] * freqs
    return jnp.cos(ang), jnp.sin(ang)


def apply_rope(x, cos, sin):
    x1, x2 = jnp.split(x.astype(jnp.float32), 2, axis=-1)
    return jnp.concatenate([x1 * cos - x2 * sin, x1 * sin + x2 * cos], axis=-1).astype(x.dtype)


def mla_group(c_q, c_kv, k_rope_raw, positions, q_norm_g, w_uq, kv_norm_g, w_ukv):
    B, T, _ = c_q.shape
    H = N_MLA_HEADS
    q = (rmsnorm(c_q, q_norm_g) @ w_uq).reshape(B, T, H, MLA_NOPE_DIM + MLA_ROPE_DIM)
    q_nope, q_pe = q[..., :MLA_NOPE_DIM], q[..., MLA_NOPE_DIM:]
    kv = (rmsnorm(c_kv, kv_norm_g) @ w_ukv).reshape(B, T, H, MLA_NOPE_DIM + MLA_V_DIM)
    k_nope, v = kv[..., :MLA_NOPE_DIM], kv[..., MLA_NOPE_DIM:]
    cos, sin = rope_tables(positions)
    q_pe = apply_rope(q_pe, cos[:, :, None], sin[:, :, None])
    k_pe = apply_rope(k_rope_raw, cos, sin)
    scale = (MLA_NOPE_DIM + MLA_ROPE_DIM) ** -0.5
    nb = T // Q_BLOCK
    qn_b = q_nope.reshape(B, nb, Q_BLOCK, H, MLA_NOPE_DIM).transpose(1, 0, 2, 3, 4)
    qp_b = q_pe.reshape(B, nb, Q_BLOCK, H, MLA_ROPE_DIM).transpose(1, 0, 2, 3, 4)
    key_pos = jnp.arange(T)

    def attend(args):
        qn, qp, blk = args
        s = (jnp.einsum('bqhd,bkhd->bhqk', qn, k_nope)
             + jnp.einsum('bqhr,bkr->bhqk', qp, k_pe)).astype(jnp.float32) * scale
        q_pos = blk * Q_BLOCK + jnp.arange(Q_BLOCK)
        causal = key_pos[None, :] <= q_pos[:, None]
        s = jnp.where(causal, s, -jnp.inf)
        p = jax.nn.softmax(s, axis=-1).astype(v.dtype)
        return jnp.einsum('bhqk,bkhd->bqhd', p, v)

    o = lax.map(attend, (qn_b, qp_b, jnp.arange(nb)))
    return o.transpose(1, 0, 2, 3, 4).reshape(B, T, H * MLA_V_DIM)


def causal_conv(x, w):
    K, C = w.shape
    return lax.conv_general_dilated(
        x, w[:, None, :], window_strides=(1,), padding=[(K - 1, 0)],
        dimension_numbers=('NWC', 'WIO', 'NWC'), feature_group_count=C)


def gated_delta_rule(q, k, v, g, beta):
    out_dtype = v.dtype
    B, T, H, dk = q.shape
    dv = v.shape[-1]
    C = GDN_CHUNK
    N = T // C
    f32 = jnp.float32
    q = q.astype(f32) * dk ** -0.5
    k, v, g, beta = k.astype(f32), v.astype(f32), g.astype(f32), beta.astype(f32)

    def to_chunks(t):
        return t.reshape((B, N, C, H) + t.shape[3:]).swapaxes(2, 3)

    qc, kc, vc, gc, bc = map(to_chunks, (q, k, v, g, beta))
    gc = jnp.cumsum(gc, axis=-1)
    tril = jnp.tril(jnp.ones((C, C), dtype=bool))
    strict = jnp.tril(jnp.ones((C, C), dtype=bool), -1)
    diff = gc[..., :, None] - gc[..., None, :]
    decay = jnp.exp(jnp.where(tril, diff, -jnp.inf))
    kb = kc * bc[..., None]
    L = jnp.where(strict, jnp.einsum('bnhid,bnhjd->bnhij', kb, kc) * decay, 0.0)
    A = jnp.eye(C, dtype=f32) + L
    w = lax.linalg.triangular_solve(A, kb * jnp.exp(gc)[..., None],
                                    left_side=True, lower=True, unit_diagonal=True)
    u = lax.linalg.triangular_solve(A, vc * bc[..., None],
                                    left_side=True, lower=True, unit_diagonal=True)
    attn = jnp.einsum('bnhid,bnhjd->bnhij', qc, kc) * decay
    q_dec = qc * jnp.exp(gc)[..., None]
    k_dec = kc * jnp.exp(gc[..., -1:] - gc)[..., None]
    g_last = jnp.exp(gc[..., -1])

    def step(S, xs):
        q_d, w_i, u_i, attn_i, k_d, gl = xs
        v_new = u_i - jnp.einsum('bhcd,bhde->bhce', w_i, S)
        o = jnp.einsum('bhcd,bhde->bhce', q_d, S) + jnp.einsum('bhij,bhje->bhie', attn_i, v_new)
        S = S * gl[..., None, None] + jnp.einsum('bhcd,bhce->bhde', k_d, v_new)
        return S, o

    xs = tuple(jnp.moveaxis(t, 1, 0) for t in (q_dec, w, u, attn, k_dec, g_last))
    S0 = jnp.zeros((B, H, dk, dv), f32)
    _, o = lax.scan(step, S0, xs)
    return o.transpose(1, 0, 3, 2, 4).reshape(B, T, H, dv).astype(out_dtype)


def _fwd_setup_inputs(seed: int = 0) -> dict:
    key = jax.random.key(seed)
    ks = jax.random.split(key, 32)
    f32 = jnp.float32

    def nrm(k, shape, fan_in):
        return jax.random.normal(k, shape, f32) * fan_in ** -0.5

    def gain(k, shape):
        return 1.0 + 0.02 * jax.random.normal(k, shape, f32)

    L = DEPTH
    x = jax.random.normal(ks[0], (BATCH, SEQ, D_MODEL), f32)
    offset = jax.random.randint(ks[1], (BATCH, 1), 0, 2048, dtype=jnp.int32)
    positions = offset + jnp.arange(SEQ, dtype=jnp.int32)[None, :]
    a_log = jnp.log(jax.random.uniform(ks[17], (L, N_GDN_HEADS), f32, 1.0, 16.0))
    dt = jnp.exp(jax.random.uniform(ks[18], (L, N_GDN_HEADS), f32, np.log(1e-3), np.log(1e-1)))
    dt_bias = dt + jnp.log(-jnp.expm1(-dt))
    return {
        "x": x,
        "positions": positions,
        "ffn1_pre_g": gain(ks[2], (L, D_MODEL)),
        "ffn1_w_gate": nrm(ks[3], (L, D_MODEL, D_FF), D_MODEL),
        "ffn1_w_up": nrm(ks[4], (L, D_MODEL, D_FF), D_MODEL),
        "ffn1_w_down": nrm(ks[5], (L, D_FF, D_MODEL), D_FF),
        "ffn1_post_g": gain(ks[6], (L, D_MODEL)),
        "mix_pre_g": gain(ks[7], (L, D_MODEL)),
        "w_in": nrm(ks[8], (L, D_MODEL, D_IN), D_MODEL),
        "mla_q_norm_g": gain(ks[9], (L, MLA_Q_RANK)),
        "mla_w_uq": nrm(ks[10], (L, MLA_Q_RANK, N_MLA_HEADS * (MLA_NOPE_DIM + MLA_ROPE_DIM)), MLA_Q_RANK),
        "mla_kv_norm_g": gain(ks[11], (L, MLA_KV_RANK)),
        "mla_w_ukv": nrm(ks[12], (L, MLA_KV_RANK, N_MLA_HEADS * (MLA_NOPE_DIM + MLA_V_DIM)), MLA_KV_RANK),
        "mla_out_g": gain(ks[13], (L, MLA_W)),
        "gdn_conv_w": nrm(ks[14], (L, GDN_CONV, 3 * GDN_W), GDN_CONV),
        "gdn_a_log": a_log,
        "gdn_dt_bias": dt_bias,
        "gdn_norm_g": gain(ks[15], (L, GDN_HEAD_DIM)),
        "w_out": nrm(ks[16], (L, D_MIX, D_MODEL), D_MIX),
        "mix_post_g": gain(ks[19], (L, D_MODEL)),
        "ffn2_pre_g": gain(ks[20], (L, D_MODEL)),
        "ffn2_w_gate": nrm(ks[21], (L, D_MODEL, D_FF), D_MODEL),
        "ffn2_w_up": nrm(ks[22], (L, D_MODEL, D_FF), D_MODEL),
        "ffn2_w_down": nrm(ks[23], (L, D_FF, D_MODEL), D_FF),
        "ffn2_post_g": gain(ks[24], (L, D_MODEL)),
    }


def _fwd_reference(x, positions, ffn1_pre_g, ffn1_w_gate, ffn1_w_up, ffn1_w_down, ffn1_post_g,
              mix_pre_g, w_in, mla_q_norm_g, mla_w_uq, mla_kv_norm_g, mla_w_ukv, mla_out_g,
              gdn_conv_w, gdn_a_log, gdn_dt_bias, gdn_norm_g, w_out, mix_post_g,
              ffn2_pre_g, ffn2_w_gate, ffn2_w_up, ffn2_w_down, ffn2_post_g):
    B, T, _ = x.shape
    H, dh = N_GDN_HEADS, GDN_HEAD_DIM
    sizes = (MLA_Q_RANK, MLA_KV_RANK, MLA_ROPE_DIM, 3 * GDN_W, N_GDN_HEADS, N_GDN_HEADS, GDN_W)
    cuts = []
    acc = 0
    for s in sizes[:-1]:
        acc += s
        cuts.append(acc)

    for l in range(DEPTH):
        h = swiglu(rmsnorm(x, ffn1_pre_g[l]), ffn1_w_gate[l], ffn1_w_up[l], ffn1_w_down[l])
        x = x + 0.5 * rmsnorm(h, ffn1_post_g[l])

        hn = rmsnorm(x, mix_pre_g[l])
        proj = hn @ w_in[l]
        c_q, c_kv, k_pe_raw, qkv, a, b, gate = jnp.split(proj, cuts, axis=-1)

        mla_o = mla_group(c_q, c_kv, k_pe_raw, positions,
                          mla_q_norm_g[l], mla_w_uq[l], mla_kv_norm_g[l], mla_w_ukv[l])
        mla_o = rmsnorm(mla_o, mla_out_g[l])

        qkv = jax.nn.silu(causal_conv(qkv, gdn_conv_w[l]))
        q, k, v = jnp.split(qkv, 3, axis=-1)
        q = l2norm(q.reshape(B, T, H, dh))
        k = l2norm(k.reshape(B, T, H, dh))
        v = v.reshape(B, T, H, dh)
        g = -jnp.exp(gdn_a_log[l].astype(jnp.float32)) * jax.nn.softplus(
            a.astype(jnp.float32) + gdn_dt_bias[l].astype(jnp.float32))
        beta = jax.nn.sigmoid(b.astype(jnp.float32))
        o = gated_delta_rule(q, k, v, g, beta)
        o = rmsnorm(o, gdn_norm_g[l]) * jax.nn.silu(gate.reshape(B, T, H, dh))
        gdn_o = o.reshape(B, T, GDN_W)

        mixed = jnp.concatenate([mla_o, gdn_o], axis=-1) @ w_out[l]
        x = x + rmsnorm(mixed, mix_post_g[l])

        h = swiglu(rmsnorm(x, ffn2_pre_g[l]), ffn2_w_gate[l], ffn2_w_up[l], ffn2_w_down[l])
        x = x + 0.5 * rmsnorm(h, ffn2_post_g[l])
    return x


import jax as _jax
import jax.numpy as _jnp

TWIN_FORMAT = 'train_step'
FWD_PARAMS = ['x', 'positions', 'ffn1_pre_g', 'ffn1_w_gate', 'ffn1_w_up', 'ffn1_w_down', 'ffn1_post_g', 'mix_pre_g', 'w_in', 'mla_q_norm_g', 'mla_w_uq', 'mla_kv_norm_g', 'mla_w_ukv', 'mla_out_g', 'gdn_conv_w', 'gdn_a_log', 'gdn_dt_bias', 'gdn_norm_g', 'w_out', 'mix_post_g', 'ffn2_pre_g', 'ffn2_w_gate', 'ffn2_w_up', 'ffn2_w_down', 'ffn2_post_g']
TWIN_WEIGHTS = ['ffn1_pre_g', 'ffn1_w_gate', 'ffn1_w_up', 'ffn1_w_down', 'ffn1_post_g', 'mix_pre_g', 'w_in', 'mla_q_norm_g', 'mla_w_uq', 'mla_kv_norm_g', 'mla_w_ukv', 'mla_out_g', 'gdn_conv_w', 'gdn_a_log', 'gdn_dt_bias', 'gdn_norm_g', 'w_out', 'mix_post_g', 'ffn2_pre_g', 'ffn2_w_gate', 'ffn2_w_up', 'ffn2_w_down', 'ffn2_post_g']
TWIN_DIFF_INPUT = 'x'
TWIN_INPUTS = ['x', 'positions', 'ffn1_pre_g', 'ffn1_w_gate', 'ffn1_w_up', 'ffn1_w_down', 'ffn1_post_g', 'mix_pre_g', 'w_in', 'mla_q_norm_g', 'mla_w_uq', 'mla_kv_norm_g', 'mla_w_ukv', 'mla_out_g', 'gdn_conv_w', 'gdn_a_log', 'gdn_dt_bias', 'gdn_norm_g', 'w_out', 'mix_post_g', 'ffn2_pre_g', 'ffn2_w_gate', 'ffn2_w_up', 'ffn2_w_down', 'ffn2_post_g', 'loss_target', 'm_ffn1_pre_g', 'm_ffn1_w_gate', 'm_ffn1_w_up', 'm_ffn1_w_down', 'm_ffn1_post_g', 'm_mix_pre_g', 'm_w_in', 'm_mla_q_norm_g', 'm_mla_w_uq', 'm_mla_kv_norm_g', 'm_mla_w_ukv', 'm_mla_out_g', 'm_gdn_conv_w', 'm_gdn_a_log', 'm_gdn_dt_bias', 'm_gdn_norm_g', 'm_w_out', 'm_mix_post_g', 'm_ffn2_pre_g', 'm_ffn2_w_gate', 'm_ffn2_w_up', 'm_ffn2_w_down', 'm_ffn2_post_g', 'v_ffn1_pre_g', 'v_ffn1_w_gate', 'v_ffn1_w_up', 'v_ffn1_w_down', 'v_ffn1_post_g', 'v_mix_pre_g', 'v_w_in', 'v_mla_q_norm_g', 'v_mla_w_uq', 'v_mla_kv_norm_g', 'v_mla_w_ukv', 'v_mla_out_g', 'v_gdn_conv_w', 'v_gdn_a_log', 'v_gdn_dt_bias', 'v_gdn_norm_g', 'v_w_out', 'v_mix_post_g', 'v_ffn2_pre_g', 'v_ffn2_w_gate', 'v_ffn2_w_up', 'v_ffn2_w_down', 'v_ffn2_post_g']
TWIN_OUTPUTS = ['loss', 'grad_x', 'grad_ffn1_pre_g', 'grad_ffn1_w_gate', 'grad_ffn1_w_up', 'grad_ffn1_w_down', 'grad_ffn1_post_g', 'grad_mix_pre_g', 'grad_w_in', 'grad_mla_q_norm_g', 'grad_mla_w_uq', 'grad_mla_kv_norm_g', 'grad_mla_w_ukv', 'grad_mla_out_g', 'grad_gdn_conv_w', 'grad_gdn_a_log', 'grad_gdn_dt_bias', 'grad_gdn_norm_g', 'grad_w_out', 'grad_mix_post_g', 'grad_ffn2_pre_g', 'grad_ffn2_w_gate', 'grad_ffn2_w_up', 'grad_ffn2_w_down', 'grad_ffn2_post_g', 'delta_ffn1_pre_g', 'delta_ffn1_w_gate', 'delta_ffn1_w_up', 'delta_ffn1_w_down', 'delta_ffn1_post_g', 'delta_mix_pre_g', 'delta_w_in', 'delta_mla_q_norm_g', 'delta_mla_w_uq', 'delta_mla_kv_norm_g', 'delta_mla_w_ukv', 'delta_mla_out_g', 'delta_gdn_conv_w', 'delta_gdn_a_log', 'delta_gdn_dt_bias', 'delta_gdn_norm_g', 'delta_w_out', 'delta_mix_post_g', 'delta_ffn2_pre_g', 'delta_ffn2_w_gate', 'delta_ffn2_w_up', 'delta_ffn2_w_down', 'delta_ffn2_post_g', 'new_m_ffn1_pre_g', 'new_m_ffn1_w_gate', 'new_m_ffn1_w_up', 'new_m_ffn1_w_down', 'new_m_ffn1_post_g', 'new_m_mix_pre_g', 'new_m_w_in', 'new_m_mla_q_norm_g', 'new_m_mla_w_uq', 'new_m_mla_kv_norm_g', 'new_m_mla_w_ukv', 'new_m_mla_out_g', 'new_m_gdn_conv_w', 'new_m_gdn_a_log', 'new_m_gdn_dt_bias', 'new_m_gdn_norm_g', 'new_m_w_out', 'new_m_mix_post_g', 'new_m_ffn2_pre_g', 'new_m_ffn2_w_gate', 'new_m_ffn2_w_up', 'new_m_ffn2_w_down', 'new_m_ffn2_post_g', 'new_v_ffn1_pre_g', 'new_v_ffn1_w_gate', 'new_v_ffn1_w_up', 'new_v_ffn1_w_down', 'new_v_ffn1_post_g', 'new_v_mix_pre_g', 'new_v_w_in', 'new_v_mla_q_norm_g', 'new_v_mla_w_uq', 'new_v_mla_kv_norm_g', 'new_v_mla_w_ukv', 'new_v_mla_out_g', 'new_v_gdn_conv_w', 'new_v_gdn_a_log', 'new_v_gdn_dt_bias', 'new_v_gdn_norm_g', 'new_v_w_out', 'new_v_mix_post_g', 'new_v_ffn2_pre_g', 'new_v_ffn2_w_gate', 'new_v_ffn2_w_up', 'new_v_ffn2_w_down', 'new_v_ffn2_post_g']
TWIN_LEAF_KINDS = {'loss': 'loss', 'grad_x': 'grad_x', 'grad_ffn1_pre_g': 'grad_w', 'grad_ffn1_w_gate': 'grad_w', 'grad_ffn1_w_up': 'grad_w', 'grad_ffn1_w_down': 'grad_w', 'grad_ffn1_post_g': 'grad_w', 'grad_mix_pre_g': 'grad_w', 'grad_w_in': 'grad_w', 'grad_mla_q_norm_g': 'grad_w', 'grad_mla_w_uq': 'grad_w', 'grad_mla_kv_norm_g': 'grad_w', 'grad_mla_w_ukv': 'grad_w', 'grad_mla_out_g': 'grad_w', 'grad_gdn_conv_w': 'grad_w', 'grad_gdn_a_log': 'grad_w', 'grad_gdn_dt_bias': 'grad_w', 'grad_gdn_norm_g': 'grad_w', 'grad_w_out': 'grad_w', 'grad_mix_post_g': 'grad_w', 'grad_ffn2_pre_g': 'grad_w', 'grad_ffn2_w_gate': 'grad_w', 'grad_ffn2_w_up': 'grad_w', 'grad_ffn2_w_down': 'grad_w', 'grad_ffn2_post_g': 'grad_w', 'delta_ffn1_pre_g': 'delta_w', 'delta_ffn1_w_gate': 'delta_w', 'delta_ffn1_w_up': 'delta_w', 'delta_ffn1_w_down': 'delta_w', 'delta_ffn1_post_g': 'delta_w', 'delta_mix_pre_g': 'delta_w', 'delta_w_in': 'delta_w', 'delta_mla_q_norm_g': 'delta_w', 'delta_mla_w_uq': 'delta_w', 'delta_mla_kv_norm_g': 'delta_w', 'delta_mla_w_ukv': 'delta_w', 'delta_mla_out_g': 'delta_w', 'delta_gdn_conv_w': 'delta_w', 'delta_gdn_a_log': 'delta_w', 'delta_gdn_dt_bias': 'delta_w', 'delta_gdn_norm_g': 'delta_w', 'delta_w_out': 'delta_w', 'delta_mix_post_g': 'delta_w', 'delta_ffn2_pre_g': 'delta_w', 'delta_ffn2_w_gate': 'delta_w', 'delta_ffn2_w_up': 'delta_w', 'delta_ffn2_w_down': 'delta_w', 'delta_ffn2_post_g': 'delta_w', 'new_m_ffn1_pre_g': 'new_m', 'new_m_ffn1_w_gate': 'new_m', 'new_m_ffn1_w_up': 'new_m', 'new_m_ffn1_w_down': 'new_m', 'new_m_ffn1_post_g': 'new_m', 'new_m_mix_pre_g': 'new_m', 'new_m_w_in': 'new_m', 'new_m_mla_q_norm_g': 'new_m', 'new_m_mla_w_uq': 'new_m', 'new_m_mla_kv_norm_g': 'new_m', 'new_m_mla_w_ukv': 'new_m', 'new_m_mla_out_g': 'new_m', 'new_m_gdn_conv_w': 'new_m', 'new_m_gdn_a_log': 'new_m', 'new_m_gdn_dt_bias': 'new_m', 'new_m_gdn_norm_g': 'new_m', 'new_m_w_out': 'new_m', 'new_m_mix_post_g': 'new_m', 'new_m_ffn2_pre_g': 'new_m', 'new_m_ffn2_w_gate': 'new_m', 'new_m_ffn2_w_up': 'new_m', 'new_m_ffn2_w_down': 'new_m', 'new_m_ffn2_post_g': 'new_m', 'new_v_ffn1_pre_g': 'new_v', 'new_v_ffn1_w_gate': 'new_v', 'new_v_ffn1_w_up': 'new_v', 'new_v_ffn1_w_down': 'new_v', 'new_v_ffn1_post_g': 'new_v', 'new_v_mix_pre_g': 'new_v', 'new_v_w_in': 'new_v', 'new_v_mla_q_norm_g': 'new_v', 'new_v_mla_w_uq': 'new_v', 'new_v_mla_kv_norm_g': 'new_v', 'new_v_mla_w_ukv': 'new_v', 'new_v_mla_out_g': 'new_v', 'new_v_gdn_conv_w': 'new_v', 'new_v_gdn_a_log': 'new_v', 'new_v_gdn_dt_bias': 'new_v', 'new_v_gdn_norm_g': 'new_v', 'new_v_w_out': 'new_v', 'new_v_mix_post_g': 'new_v', 'new_v_ffn2_pre_g': 'new_v', 'new_v_ffn2_w_gate': 'new_v', 'new_v_ffn2_w_up': 'new_v', 'new_v_ffn2_w_down': 'new_v', 'new_v_ffn2_post_g': 'new_v'}


def _forward(args):
    return _fwd_reference(*[args[k] for k in FWD_PARAMS])


def _output_shape():
    out = _jax.eval_shape(lambda: _forward(_fwd_setup_inputs(0)))
    return out.shape, out.dtype

N_MICROBATCH = 1
ADAM_LR = 0.001
ADAM_B1 = 0.9
ADAM_B2 = 0.999
ADAM_EPS = 1e-08
ADAM_WD = 0.01
ADAM_STEP = 10
PER_EXAMPLE_BATCH_AXIS = {'x': 0, 'positions': 0, 'loss_target': 0}
SHARED_INPUTS = []
_WEIGHT_DTYPES = {'ffn1_pre_g': _jnp.float32, 'ffn1_w_gate': _jnp.float32, 'ffn1_w_up': _jnp.float32, 'ffn1_w_down': _jnp.float32, 'ffn1_post_g': _jnp.float32, 'mix_pre_g': _jnp.float32, 'w_in': _jnp.float32, 'mla_q_norm_g': _jnp.float32, 'mla_w_uq': _jnp.float32, 'mla_kv_norm_g': _jnp.float32, 'mla_w_ukv': _jnp.float32, 'mla_out_g': _jnp.float32, 'gdn_conv_w': _jnp.float32, 'gdn_a_log': _jnp.float32, 'gdn_dt_bias': _jnp.float32, 'gdn_norm_g': _jnp.float32, 'w_out': _jnp.float32, 'mix_post_g': _jnp.float32, 'ffn2_pre_g': _jnp.float32, 'ffn2_w_gate': _jnp.float32, 'ffn2_w_up': _jnp.float32, 'ffn2_w_down': _jnp.float32, 'ffn2_post_g': _jnp.float32}
MOMENT_SCALE = {'ffn1_pre_g': 4.498600e-01, 'ffn1_w_gate': 1.889439e-01, 'ffn1_w_up': 1.970042e-01, 'ffn1_w_down': 3.336414e-01, 'ffn1_post_g': 7.863599e+00, 'mix_pre_g': 7.387690e-01, 'w_in': 4.497904e-01, 'mla_q_norm_g': 1.129047e+00, 'mla_w_uq': 6.494593e-01, 'mla_kv_norm_g': 2.959366e+00, 'mla_w_ukv': 8.465409e-01, 'mla_out_g': 9.603138e-01, 'gdn_conv_w': 2.611682e-01, 'gdn_a_log': 6.364789e-01, 'gdn_dt_bias': 6.112184e-01, 'gdn_norm_g': 8.614228e-01, 'w_out': 6.770869e-01, 'mix_post_g': 3.204612e+01, 'ffn2_pre_g': 4.333868e-01, 'ffn2_w_gate': 1.314709e-01, 'ffn2_w_up': 2.057549e-01, 'ffn2_w_down': 3.400755e-01, 'ffn2_post_g': 7.986411e+00}


def _to_microbatches(a, axis):
    t = _jnp.moveaxis(a, axis, 0)
    t = t.reshape((N_MICROBATCH, t.shape[0] // N_MICROBATCH) + t.shape[1:])
    return _jnp.moveaxis(t, 1, axis + 1)


def setup_inputs(seed: int = 0) -> dict:
    inp = _fwd_setup_inputs(seed)
    key = _jax.random.fold_in(_jax.random.key(seed), 7919)
    shape, _ = _output_shape()
    out = dict(inp)
    out["loss_target"] = _jax.random.normal(_jax.random.fold_in(key, 0), shape, _jnp.float32)
    for i, name in enumerate(TWIN_WEIGHTS):
        w = inp[name].astype(_jnp.float32)
        if MOMENT_SCALE is None:
            s = _jnp.sqrt(_jnp.mean(_jnp.square(w)) + 1e-30)
        else:
            s = MOMENT_SCALE[name]
        km, kv = _jax.random.split(_jax.random.fold_in(key, i + 1))
        out[name] = w
        out["m_" + name] = s * _jax.random.normal(km, w.shape, _jnp.float32)
        out["v_" + name] = (s * s) * _jax.random.uniform(kv, w.shape, _jnp.float32, 0.5, 1.5)
    if N_MICROBATCH > 1:
        for name, axis in PER_EXAMPLE_BATCH_AXIS.items():
            out[name] = _to_microbatches(out[name], axis)
    return {'x': out['x'], 'positions': out['positions'], 'ffn1_pre_g': out['ffn1_pre_g'], 'ffn1_w_gate': out['ffn1_w_gate'], 'ffn1_w_up': out['ffn1_w_up'], 'ffn1_w_down': out['ffn1_w_down'], 'ffn1_post_g': out['ffn1_post_g'], 'mix_pre_g': out['mix_pre_g'], 'w_in': out['w_in'], 'mla_q_norm_g': out['mla_q_norm_g'], 'mla_w_uq': out['mla_w_uq'], 'mla_kv_norm_g': out['mla_kv_norm_g'], 'mla_w_ukv': out['mla_w_ukv'], 'mla_out_g': out['mla_out_g'], 'gdn_conv_w': out['gdn_conv_w'], 'gdn_a_log': out['gdn_a_log'], 'gdn_dt_bias': out['gdn_dt_bias'], 'gdn_norm_g': out['gdn_norm_g'], 'w_out': out['w_out'], 'mix_post_g': out['mix_post_g'], 'ffn2_pre_g': out['ffn2_pre_g'], 'ffn2_w_gate': out['ffn2_w_gate'], 'ffn2_w_up': out['ffn2_w_up'], 'ffn2_w_down': out['ffn2_w_down'], 'ffn2_post_g': out['ffn2_post_g'], 'loss_target': out['loss_target'], 'm_ffn1_pre_g': out['m_ffn1_pre_g'], 'm_ffn1_w_gate': out['m_ffn1_w_gate'], 'm_ffn1_w_up': out['m_ffn1_w_up'], 'm_ffn1_w_down': out['m_ffn1_w_down'], 'm_ffn1_post_g': out['m_ffn1_post_g'], 'm_mix_pre_g': out['m_mix_pre_g'], 'm_w_in': out['m_w_in'], 'm_mla_q_norm_g': out['m_mla_q_norm_g'], 'm_mla_w_uq': out['m_mla_w_uq'], 'm_mla_kv_norm_g': out['m_mla_kv_norm_g'], 'm_mla_w_ukv': out['m_mla_w_ukv'], 'm_mla_out_g': out['m_mla_out_g'], 'm_gdn_conv_w': out['m_gdn_conv_w'], 'm_gdn_a_log': out['m_gdn_a_log'], 'm_gdn_dt_bias': out['m_gdn_dt_bias'], 'm_gdn_norm_g': out['m_gdn_norm_g'], 'm_w_out': out['m_w_out'], 'm_mix_post_g': out['m_mix_post_g'], 'm_ffn2_pre_g': out['m_ffn2_pre_g'], 'm_ffn2_w_gate': out['m_ffn2_w_gate'], 'm_ffn2_w_up': out['m_ffn2_w_up'], 'm_ffn2_w_down': out['m_ffn2_w_down'], 'm_ffn2_post_g': out['m_ffn2_post_g'], 'v_ffn1_pre_g': out['v_ffn1_pre_g'], 'v_ffn1_w_gate': out['v_ffn1_w_gate'], 'v_ffn1_w_up': out['v_ffn1_w_up'], 'v_ffn1_w_down': out['v_ffn1_w_down'], 'v_ffn1_post_g': out['v_ffn1_post_g'], 'v_mix_pre_g': out['v_mix_pre_g'], 'v_w_in': out['v_w_in'], 'v_mla_q_norm_g': out['v_mla_q_norm_g'], 'v_mla_w_uq': out['v_mla_w_uq'], 'v_mla_kv_norm_g': out['v_mla_kv_norm_g'], 'v_mla_w_ukv': out['v_mla_w_ukv'], 'v_mla_out_g': out['v_mla_out_g'], 'v_gdn_conv_w': out['v_gdn_conv_w'], 'v_gdn_a_log': out['v_gdn_a_log'], 'v_gdn_dt_bias': out['v_gdn_dt_bias'], 'v_gdn_norm_g': out['v_gdn_norm_g'], 'v_w_out': out['v_w_out'], 'v_mix_post_g': out['v_mix_post_g'], 'v_ffn2_pre_g': out['v_ffn2_pre_g'], 'v_ffn2_w_gate': out['v_ffn2_w_gate'], 'v_ffn2_w_up': out['v_ffn2_w_up'], 'v_ffn2_w_down': out['v_ffn2_w_down'], 'v_ffn2_post_g': out['v_ffn2_post_g']}


def _loss(weights, diff, rest, loss_target):
    with _jax.named_scope("forward"):
        args = {**rest, TWIN_DIFF_INPUT: diff, **{k: w.astype(_WEIGHT_DTYPES[k]) for k, w in weights.items()}}
        y = _forward(args)
    with _jax.named_scope("loss_head"):
        err = _jnp.square(y.astype(_jnp.float32) - loss_target)
        return 0.5 * _jnp.sum(_jnp.mean(err, axis=-1)) if err.ndim else 0.5 * err


def _adamw(w, g, m, v):
    m = ADAM_B1 * m + (1.0 - ADAM_B1) * g
    v = ADAM_B2 * v + (1.0 - ADAM_B2) * _jnp.square(g)
    m_hat = m / (1.0 - ADAM_B1 ** ADAM_STEP)
    v_hat = v / (1.0 - ADAM_B2 ** ADAM_STEP)
    delta = -ADAM_LR * (m_hat / (_jnp.sqrt(v_hat) + ADAM_EPS) + ADAM_WD * w)
    return delta, m, v


def reference(x, positions, ffn1_pre_g, ffn1_w_gate, ffn1_w_up, ffn1_w_down, ffn1_post_g, mix_pre_g, w_in, mla_q_norm_g, mla_w_uq, mla_kv_norm_g, mla_w_ukv, mla_out_g, gdn_conv_w, gdn_a_log, gdn_dt_bias, gdn_norm_g, w_out, mix_post_g, ffn2_pre_g, ffn2_w_gate, ffn2_w_up, ffn2_w_down, ffn2_post_g, loss_target, m_ffn1_pre_g, m_ffn1_w_gate, m_ffn1_w_up, m_ffn1_w_down, m_ffn1_post_g, m_mix_pre_g, m_w_in, m_mla_q_norm_g, m_mla_w_uq, m_mla_kv_norm_g, m_mla_w_ukv, m_mla_out_g, m_gdn_conv_w, m_gdn_a_log, m_gdn_dt_bias, m_gdn_norm_g, m_w_out, m_mix_post_g, m_ffn2_pre_g, m_ffn2_w_gate, m_ffn2_w_up, m_ffn2_w_down, m_ffn2_post_g, v_ffn1_pre_g, v_ffn1_w_gate, v_ffn1_w_up, v_ffn1_w_down, v_ffn1_post_g, v_mix_pre_g, v_w_in, v_mla_q_norm_g, v_mla_w_uq, v_mla_kv_norm_g, v_mla_w_ukv, v_mla_out_g, v_gdn_conv_w, v_gdn_a_log, v_gdn_dt_bias, v_gdn_norm_g, v_w_out, v_mix_post_g, v_ffn2_pre_g, v_ffn2_w_gate, v_ffn2_w_up, v_ffn2_w_down, v_ffn2_post_g):
    given = dict(x=x, positions=positions, ffn1_pre_g=ffn1_pre_g, ffn1_w_gate=ffn1_w_gate, ffn1_w_up=ffn1_w_up, ffn1_w_down=ffn1_w_down, ffn1_post_g=ffn1_post_g, mix_pre_g=mix_pre_g, w_in=w_in, mla_q_norm_g=mla_q_norm_g, mla_w_uq=mla_w_uq, mla_kv_norm_g=mla_kv_norm_g, mla_w_ukv=mla_w_ukv, mla_out_g=mla_out_g, gdn_conv_w=gdn_conv_w, gdn_a_log=gdn_a_log, gdn_dt_bias=gdn_dt_bias, gdn_norm_g=gdn_norm_g, w_out=w_out, mix_post_g=mix_post_g, ffn2_pre_g=ffn2_pre_g, ffn2_w_gate=ffn2_w_gate, ffn2_w_up=ffn2_w_up, ffn2_w_down=ffn2_w_down, ffn2_post_g=ffn2_post_g, loss_target=loss_target, m_ffn1_pre_g=m_ffn1_pre_g, m_ffn1_w_gate=m_ffn1_w_gate, m_ffn1_w_up=m_ffn1_w_up, m_ffn1_w_down=m_ffn1_w_down, m_ffn1_post_g=m_ffn1_post_g, m_mix_pre_g=m_mix_pre_g, m_w_in=m_w_in, m_mla_q_norm_g=m_mla_q_norm_g, m_mla_w_uq=m_mla_w_uq, m_mla_kv_norm_g=m_mla_kv_norm_g, m_mla_w_ukv=m_mla_w_ukv, m_mla_out_g=m_mla_out_g, m_gdn_conv_w=m_gdn_conv_w, m_gdn_a_log=m_gdn_a_log, m_gdn_dt_bias=m_gdn_dt_bias, m_gdn_norm_g=m_gdn_norm_g, m_w_out=m_w_out, m_mix_post_g=m_mix_post_g, m_ffn2_pre_g=m_ffn2_pre_g, m_ffn2_w_gate=m_ffn2_w_gate, m_ffn2_w_up=m_ffn2_w_up, m_ffn2_w_down=m_ffn2_w_down, m_ffn2_post_g=m_ffn2_post_g, v_ffn1_pre_g=v_ffn1_pre_g, v_ffn1_w_gate=v_ffn1_w_gate, v_ffn1_w_up=v_ffn1_w_up, v_ffn1_w_down=v_ffn1_w_down, v_ffn1_post_g=v_ffn1_post_g, v_mix_pre_g=v_mix_pre_g, v_w_in=v_w_in, v_mla_q_norm_g=v_mla_q_norm_g, v_mla_w_uq=v_mla_w_uq, v_mla_kv_norm_g=v_mla_kv_norm_g, v_mla_w_ukv=v_mla_w_ukv, v_mla_out_g=v_mla_out_g, v_gdn_conv_w=v_gdn_conv_w, v_gdn_a_log=v_gdn_a_log, v_gdn_dt_bias=v_gdn_dt_bias, v_gdn_norm_g=v_gdn_norm_g, v_w_out=v_w_out, v_mix_post_g=v_mix_post_g, v_ffn2_pre_g=v_ffn2_pre_g, v_ffn2_w_gate=v_ffn2_w_gate, v_ffn2_w_up=v_ffn2_w_up, v_ffn2_w_down=v_ffn2_w_down, v_ffn2_post_g=v_ffn2_post_g)
    weights = {n: given[n] for n in TWIN_WEIGHTS}
    shared = {n: given[n] for n in SHARED_INPUTS}
    per_example = {n: given[n] for n in ['x', 'positions']}
    grad_fn = _jax.value_and_grad(_loss, argnums=(0, 1))

    def one_microbatch(ex, loss_target):
        ex = dict(ex)
        diff = ex.pop(TWIN_DIFF_INPUT)
        return grad_fn(weights, diff, {**shared, **ex}, loss_target)

    if N_MICROBATCH == 1:
        loss, (grad_w, grad_x) = one_microbatch(per_example, given["loss_target"])
    else:
        def body(carry, xs):
            loss_sum, grad_sum = carry
            l_k, (gw_k, gx_k) = one_microbatch(xs[0], xs[1])
            with _jax.named_scope("update"):
                return (loss_sum + l_k, _jax.tree.map(_jnp.add, grad_sum, gw_k)), gx_k

        init = (_jnp.zeros((), _jnp.float32), _jax.tree.map(_jnp.zeros_like, weights))
        (loss, grad_w), grad_x = _jax.lax.scan(body, init, (per_example, given["loss_target"]))
    with _jax.named_scope("update"):
        delta_w, new_m, new_v = {}, {}, {}
        for n in TWIN_WEIGHTS:
            delta_w[n], new_m[n], new_v[n] = _adamw(weights[n], grad_w[n], given["m_" + n], given["v_" + n])
    return (loss, grad_x, *[grad_w[n] for n in TWIN_WEIGHTS], *[delta_w[n] for n in TWIN_WEIGHTS],
            *[new_m[n] for n in TWIN_WEIGHTS], *[new_v[n] for n in TWIN_WEIGHTS])
```

```python
import functools

import jax
import jax.numpy as jnp
import numpy as np
from jax import lax
from jax.experimental import pallas as pl
from jax.experimental.pallas import tpu as pltpu

F32 = jnp.float32
BF16 = jnp.bfloat16
HI = lax.Precision.HIGHEST

N_DEV = 8
D_MODEL = 1024
D_FF = 2816
N_HEADS = 8
SLOT = 128
MLA_Q_RANK = 256
MLA_KV_RANK = 128
MLA_NOPE = 64
MLA_ROPE = 32
MLA_V = 64
GDN_D = 64
GDN_CONV = 4
GDN_CHUNK = 64
ROPE_THETA = 10000.0
EPS = 1e-6
ADAM_LR, ADAM_B1, ADAM_B2, ADAM_EPS, ADAM_WD, ADAM_STEP = 0.001, 0.9, 0.999, 1e-08, 0.01, 10


def _dot(a, b, dims, precision=None):
    return lax.dot_general(a, b, (dims, ((), ())), precision=precision, preferred_element_type=F32)


def _nn(a, b, precision=None):
    return _dot(a, b, ((1,), (0,)), precision)


def _nt(a, b, precision=None):
    return _dot(a, b, ((1,), (1,)), precision)


def _tn(a, b, precision=None):
    return _dot(a, b, ((0,), (0,)), precision)


def _sigmoid(x):
    return 1.0 / (1.0 + jnp.exp(-x))


def _silu(x):
    return x * _sigmoid(x)


def _rms(x, g, n):
    ms = jnp.sum(x * x, axis=-1, keepdims=True) * (1.0 / n)
    return x * lax.rsqrt(ms + EPS) * g


def _chunk_masks():
    c = GDN_CHUNK
    i = lax.broadcasted_iota(jnp.int32, (c, c), 0)
    j = lax.broadcasted_iota(jnp.int32, (c, c), 1)
    lower = i >= j
    strict = i > j
    eye = (i == j).astype(F32)
    blocks = []
    b = 1
    while b < c:
        same = (i // (2 * b)) == (j // (2 * b))
        blocks.append(same & ((i % (2 * b)) >= b) & ((j % (2 * b)) < b))
        b *= 2
    return lower, strict, eye, blocks


def _unit_lower_inverse(low, eye, blocks):
    t = eye
    for m in blocks:
        lo = jnp.where(m, low, 0.0)
        t = t - _nn(t, _nn(lo, t, HI), HI)
    return t


def _gdn_chunk(q, k, v, gb, bb, s, masks):
    lower, strict, eye, blocks = masks
    qs = q * (GDN_D ** -0.5)
    gc = _nn(lower.astype(F32), gb, HI)
    gct = _nt(eye, gc, HI)
    decay = jnp.exp(jnp.where(lower, gc - gct, -1e30))
    kb = k * bb
    low = jnp.where(strict, _nt(kb, k, HI) * decay, 0.0)
    tinv = _unit_lower_inverse(low, eye, blocks)
    eg = jnp.exp(gc)
    w = _nn(tinv, kb * eg, HI)
    u = _nn(tinv, v * bb, HI)
    attn = _nt(qs, k, HI) * decay
    g_end = jnp.sum(gb, axis=0, keepdims=True)
    k_dec = k * jnp.exp(g_end - gc)
    v_new = u - _nn(w, s, HI)
    o = _nn(qs * eg, s, HI) + _nn(attn, v_new, HI)
    s_new = s * jnp.exp(g_end) + _tn(k_dec, v_new, HI)
    return o, s_new


def _slot_spec(rows, kind=0):
    return pl.BlockSpec((rows, SLOT), lambda h, n: (n, kind * N_HEADS + h))


def _gdn_fwd(qkv, gb, bb):
    t = qkv.shape[0]
    n_chunks = t // GDN_CHUNK
    d = GDN_D

    def body(q_ref, k_ref, v_ref, g_ref, b_ref, o_ref, keep_ref, s_ref):
        @pl.when(pl.program_id(1) == 0)
        def _():
            s_ref[...] = jnp.zeros_like(s_ref)

        s = s_ref[...]
        keep_ref[0, 0] = s
        o, s_new = _gdn_chunk(q_ref[:, :d], k_ref[:, :d], v_ref[:, :d], g_ref[:, :d], b_ref[:, :d], s, _chunk_masks())
        s_ref[...] = s_new
        o_ref[:, :d] = o
        o_ref[:, d:] = jnp.zeros((GDN_CHUNK, SLOT - d), F32)

    return pl.pallas_call(
        body, name="gdn_fwd",
        grid=(N_HEADS, n_chunks),
        in_specs=[_slot_spec(GDN_CHUNK, 0), _slot_spec(GDN_CHUNK, 1), _slot_spec(GDN_CHUNK, 2), _slot_spec(GDN_CHUNK), _slot_spec(GDN_CHUNK)],
        out_specs=[_slot_spec(GDN_CHUNK), pl.BlockSpec((1, 1, d, d), lambda h, n: (h, n, 0, 0))],
        out_shape=[jax.ShapeDtypeStruct((t, N_HEADS * SLOT), F32), jax.ShapeDtypeStruct((N_HEADS, n_chunks, d, d), F32)],
        scratch_shapes=[pltpu.VMEM((d, d), F32)],
        compiler_params=pltpu.CompilerParams(dimension_semantics=("parallel", "arbitrary")),
    )(qkv, qkv, qkv, gb, bb)


def _gdn_bwd(qkv, gb, bb, keep, do):
    t = qkv.shape[0]
    n_chunks = t // GDN_CHUNK
    d = GDN_D

    def body(q_ref, k_ref, v_ref, g_ref, b_ref, keep_ref, do_ref, dqkv_ref, dg_ref, db_ref, ds_ref):
        @pl.when(pl.program_id(1) == 0)
        def _():
            ds_ref[...] = jnp.zeros_like(ds_ref)

        masks = _chunk_masks()
        _, pull = jax.vjp(lambda *a: _gdn_chunk(*a, masks), q_ref[:, :d], k_ref[:, :d], v_ref[:, :d], g_ref[:, :d],
                          b_ref[:, :d], keep_ref[0, 0])
        dq, dk, dv, dg, db, ds = pull((do_ref[:, :d], ds_ref[...]))
        ds_ref[...] = ds
        pad = jnp.zeros((GDN_CHUNK, SLOT - d), F32)
        for j, val in enumerate((dq, dk, dv)):
            dqkv_ref[j, :, :d] = val
            dqkv_ref[j, :, d:] = pad
        for ref, val in ((dg_ref, dg), (db_ref, db)):
            ref[:, :d] = val
            ref[:, d:] = pad

    def spec(kind=0):
        return pl.BlockSpec((GDN_CHUNK, SLOT), lambda h, n: (n_chunks - 1 - n, kind * N_HEADS + h))

    return pl.pallas_call(
        body, name="gdn_bwd",
        grid=(N_HEADS, n_chunks),
        in_specs=[spec(0), spec(1), spec(2), spec(), spec(),
                  pl.BlockSpec((1, 1, d, d), lambda h, n: (h, n_chunks - 1 - n, 0, 0)), spec()],
        out_specs=[pl.BlockSpec((3, GDN_CHUNK, SLOT), lambda h, n: (0, n_chunks - 1 - n, h)), spec(), spec()],
        out_shape=[jax.ShapeDtypeStruct((3, t, N_HEADS * SLOT), F32)] + [jax.ShapeDtypeStruct((t, N_HEADS * SLOT), F32)] * 2,
        scratch_shapes=[pltpu.VMEM((d, d), F32)],
        compiler_params=pltpu.CompilerParams(dimension_semantics=("parallel", "arbitrary")),
    )(qkv, qkv, qkv, gb, bb, keep, do)


def _rowwise(name, fn, rows, consts, outs, sums=(), tm=256):
    rows = [x if isinstance(x, tuple) else (x, x.shape[1], 0) for x in rows]
    t = rows[0][0].shape[0]
    tm = min(tm, t)
    steps = t // tm
    n_r, n_c, n_o, n_s = len(rows), len(consts), len(outs), len(sums)

    def window(width, block):
        return pl.BlockSpec((tm, width), lambda i: (i, block))

    def body(*refs):
        r, c = refs[:n_r], refs[n_r:n_r + n_c]
        o, s = refs[n_r + n_c:n_r + n_c + n_o], refs[n_r + n_c + n_o:]
        vals, tot = fn([x[...] for x in r], [x[...] for x in c])
        for ref, val in zip(o, vals):
            ref[...] = val.astype(ref.dtype)
        if n_s:
            @pl.when(pl.program_id(0) == 0)
            def _():
                for ref in s:
                    ref[...] = jnp.zeros_like(ref)

            for ref, val in zip(s, tot):
                ref[...] += val

    return pl.pallas_call(
        body, name=name,
        grid=(steps,),
        in_specs=[window(w, b) for _, w, b in rows] + [pl.BlockSpec(x.shape, lambda i: (0, 0)) for x in consts],
        out_specs=[pl.BlockSpec((tm, w), lambda i: (i, 0)) for w, _ in outs]
        + [pl.BlockSpec((1, w), lambda i: (0, 0)) for w in sums],
        out_shape=[jax.ShapeDtypeStruct((t, w), dt) for w, dt in outs]
        + [jax.ShapeDtypeStruct((1, w), F32) for w in sums],
        compiler_params=pltpu.CompilerParams(dimension_semantics=("arbitrary",)),
    )(*[x for x, _, _ in rows], *consts)


def _tile(dim, target):
    if dim <= target:
        return dim
    best = None
    for cand in range(128, target + 1, 128):
        if dim % cand == 0:
            best = cand
    assert best is not None, (dim, target)
    return best


def _matmul(name, a, b, mode, out_dtype=F32, tm=512, tn=1024, tk=1024):
    if mode == "nn":
        (m, k), n = a.shape, b.shape[1]
    elif mode == "nt":
        (m, k), n = a.shape, b.shape[0]
    else:
        (k, m), n = a.shape, b.shape[1]
    tm, tn, tk = _tile(m, tm), _tile(n, tn), _tile(k, tk)
    k_steps = k // tk
    product = {"nn": _nn, "nt": _nt, "tn": _tn}[mode]

    def body(a_ref, b_ref, o_ref, acc_ref):
        part = product(a_ref[...].astype(BF16), b_ref[...].astype(BF16))
        if k_steps == 1:
            o_ref[...] = part.astype(o_ref.dtype)
        else:
            kk = pl.program_id(2)

            @pl.when(kk == 0)
            def _():
                acc_ref[...] = part

            @pl.when(kk > 0)
            def _():
                acc_ref[...] += part

            @pl.when(kk == k_steps - 1)
            def _():
                o_ref[...] = acc_ref[...].astype(o_ref.dtype)

    a_spec = pl.BlockSpec((tk, tm), lambda i, j, kk: (kk, i)) if mode == "tn" else pl.BlockSpec((tm, tk), lambda i, j, kk: (i, kk))
    b_spec = pl.BlockSpec((tn, tk), lambda i, j, kk: (j, kk)) if mode == "nt" else pl.BlockSpec((tk, tn), lambda i, j, kk: (kk, j))
    return pl.pallas_call(
        body, name=name,
        grid=(m // tm, n // tn, k_steps),
        in_specs=[a_spec, b_spec],
        out_specs=pl.BlockSpec((tm, tn), lambda i, j, kk: (i, j)),
        out_shape=jax.ShapeDtypeStruct((m, n), out_dtype),
        scratch_shapes=[pltpu.VMEM((tm, tn) if k_steps > 1 else (8, 128), F32)],
        compiler_params=pltpu.CompilerParams(dimension_semantics=("parallel", "parallel", "arbitrary")),
    )(a, b)


FFN_TM = 512
FFN_BWD_TM = 256
FFN_TF = 1408


def _ffn_fwd(name, x, g_pre, w_gate, w_up, w_down, g_post):
    t, dm = x.shape
    f = w_gate.shape[1]
    tm, tf = min(FFN_TM, t), _tile(f, FFN_TF)
    f_steps = f // tf

    def body(x_ref, gpre_ref, wg_ref, wu_ref, wd_ref, gpost_ref, h_ref, y_ref, xn_ref, acc_ref):
        j = pl.program_id(1)

        @pl.when(j == 0)
        def _():
            xn_ref[...] = _rms(x_ref[...], gpre_ref[...], dm).astype(BF16)
            acc_ref[...] = jnp.zeros_like(acc_ref)

        xn = xn_ref[...]
        a = _silu(_nn(xn, wg_ref[...])) * _nn(xn, wu_ref[...])
        acc_ref[...] += _nn(a.astype(BF16), wd_ref[...])

        @pl.when(j == f_steps - 1)
        def _():
            h = acc_ref[...]
            h_ref[...] = h
            y_ref[...] = x_ref[...] + 0.5 * _rms(h, gpost_ref[...], dm)

    row = pl.BlockSpec((tm, dm), lambda i, j: (i, 0))
    vec = pl.BlockSpec((1, dm), lambda i, j: (0, 0))
    return pl.pallas_call(
        body, name=name,
        grid=(t // tm, f_steps),
        in_specs=[row, vec, pl.BlockSpec((dm, tf), lambda i, j: (0, j)), pl.BlockSpec((dm, tf), lambda i, j: (0, j)),
                  pl.BlockSpec((tf, dm), lambda i, j: (j, 0)), vec],
        out_specs=[row, row],
        out_shape=[jax.ShapeDtypeStruct((t, dm), F32)] * 2,
        scratch_shapes=[pltpu.VMEM((tm, dm), BF16), pltpu.VMEM((tm, dm), F32)],
        compiler_params=pltpu.CompilerParams(dimension_semantics=("parallel", "arbitrary")),
    )(x, g_pre, w_gate, w_up, w_down, g_post)


def _ffn_bwd(name, x, h, dy, g_pre, w_gate, w_up, w_down, g_post):
    t, dm = x.shape
    f = w_gate.shape[1]
    tm, tf = min(FFN_BWD_TM, t), _tile(f, FFN_TF)
    f_steps = f // tf

    def post(hv, g):
        return 0.5 * _rms(hv, g, dm)

    def pre(xv, g):
        return _rms(xv, g, dm)

    def body(x_ref, h_ref, dy_ref, gpre_ref, wg_ref, wu_ref, wd_ref, gpost_ref,
             dx_ref, xn_ref, dh_ref, a_ref, dhg_ref, dhu_ref, dgpre_ref, dgpost_ref, acc_ref):
        i, j = pl.program_id(0), pl.program_id(1)

        @pl.when((i == 0) & (j == 0))
        def _():
            dgpre_ref[...] = jnp.zeros_like(dgpre_ref)
            dgpost_ref[...] = jnp.zeros_like(dgpost_ref)

        @pl.when(j == 0)
        def _():
            xn_ref[...] = pre(x_ref[...], gpre_ref[...]).astype(BF16)
            _, pull = jax.vjp(post, h_ref[...], gpost_ref[...])
            dh, dg = pull(dy_ref[...])
            dh_ref[...] = dh.astype(BF16)
            dgpost_ref[...] += dg
            acc_ref[...] = jnp.zeros_like(acc_ref)

        xn = xn_ref[...]
        hg = _nn(xn, wg_ref[...])
        hu = _nn(xn, wu_ref[...])
        da = _nt(dh_ref[...], wd_ref[...])
        sig = _sigmoid(hg)
        act = hg * sig
        dhu = (da * act).astype(BF16)
        dhg = (da * hu * (sig * (1.0 + hg * (1.0 - sig)))).astype(BF16)
        a_ref[...] = (act * hu).astype(BF16)
        dhg_ref[...] = dhg
        dhu_ref[...] = dhu
        acc_ref[...] += _nt(dhg, wg_ref[...]) + _nt(dhu, wu_ref[...])

        @pl.when(j == f_steps - 1)
        def _():
            _, pull = jax.vjp(pre, x_ref[...], gpre_ref[...])
            dx, dg = pull(acc_ref[...])
            dx_ref[...] = dy_ref[...] + dx
            dgpre_ref[...] += dg

    row = pl.BlockSpec((tm, dm), lambda i, j: (i, 0))
    vec = pl.BlockSpec((1, dm), lambda i, j: (0, 0))
    wide = pl.BlockSpec((tm, tf), lambda i, j: (i, j))
    return pl.pallas_call(
        body, name=name,
        grid=(t // tm, f_steps),
        in_specs=[row, row, row, vec, pl.BlockSpec((dm, tf), lambda i, j: (0, j)), pl.BlockSpec((dm, tf), lambda i, j: (0, j)),
                  pl.BlockSpec((tf, dm), lambda i, j: (j, 0)), vec],
        out_specs=[row, row, row, wide, wide, wide, vec, vec],
        out_shape=[jax.ShapeDtypeStruct((t, dm), F32), jax.ShapeDtypeStruct((t, dm), BF16), jax.ShapeDtypeStruct((t, dm), BF16),
                   jax.ShapeDtypeStruct((t, f), BF16), jax.ShapeDtypeStruct((t, f), BF16), jax.ShapeDtypeStruct((t, f), BF16),
                   jax.ShapeDtypeStruct((1, dm), F32), jax.ShapeDtypeStruct((1, dm), F32)],
        scratch_shapes=[pltpu.VMEM((tm, dm), F32)],
        compiler_params=pltpu.CompilerParams(dimension_semantics=("arbitrary", "arbitrary")),
    )(x, h, dy, g_pre, w_gate, w_up, w_down, g_post)


ATT_T = 512
ATT_SCALE = (MLA_NOPE + MLA_ROPE) ** -0.5


def _causal_scores(q, k, qi, ki, tile):
    s = _nt(q, k) * ATT_SCALE
    row = lax.broadcasted_iota(jnp.int32, s.shape, 0) + qi * tile
    col = lax.broadcasted_iota(jnp.int32, s.shape, 1) + ki * tile
    return jnp.where(col <= row, s, -1e30)


def _attn_fwd(q, k, v):
    t = q.shape[0]
    tile = min(ATT_T, t)
    steps = t // tile

    def body(q_ref, k_ref, v_ref, o_ref, lse_ref, m_ref, l_ref, acc_ref):
        qi, ki = pl.program_id(1), pl.program_id(2)

        @pl.when(ki == 0)
        def _():
            m_ref[...] = jnp.full_like(m_ref, -1e30)
            l_ref[...] = jnp.zeros_like(l_ref)
            acc_ref[...] = jnp.zeros_like(acc_ref)

        @pl.when(ki <= qi)
        def _():
            s = _causal_scores(q_ref[...], k_ref[...], qi, ki, tile)
            m_old = m_ref[...]
            m_new = jnp.maximum(m_old, jnp.max(s, axis=-1, keepdims=True))
            p = jnp.exp(s - m_new)
            alpha = jnp.exp(m_old - m_new)
            l_ref[...] = alpha * l_ref[...] + jnp.sum(p, axis=-1, keepdims=True)
            acc_ref[...] = alpha * acc_ref[...] + _nn(p.astype(BF16), v_ref[...])
            m_ref[...] = m_new

        @pl.when(ki == qi)
        def _():
            o_ref[...] = acc_ref[...] / l_ref[...]
            lse_ref[...] = jnp.broadcast_to(m_ref[...] + jnp.log(l_ref[...]), lse_ref.shape)

    q_spec = pl.BlockSpec((tile, SLOT), lambda h, qi, ki: (qi, h))
    k_spec = pl.BlockSpec((tile, SLOT), lambda h, qi, ki: (jnp.minimum(ki, qi), h))
    return pl.pallas_call(
        body, name="attn_fwd",
        grid=(N_HEADS, steps, steps),
        in_specs=[q_spec, k_spec, k_spec],
        out_specs=[q_spec, q_spec],
        out_shape=[jax.ShapeDtypeStruct((t, N_HEADS * SLOT), F32)] * 2,
        scratch_shapes=[pltpu.VMEM((tile, 1), F32), pltpu.VMEM((tile, 1), F32), pltpu.VMEM((tile, SLOT), F32)],
        compiler_params=pltpu.CompilerParams(dimension_semantics=("parallel", "parallel", "arbitrary")),
    )(q, k, v)


def _attn_probs(q, k, lse_ref, qi, ki, tile):
    return jnp.exp(_causal_scores(q, k, qi, ki, tile) - lse_ref[:, 0:1])


def _attn_bwd_q(q, k, v, do, lse, delta):
    t = q.shape[0]
    tile = min(ATT_T, t)
    steps = t // tile

    def body(q_ref, k_ref, v_ref, do_ref, lse_ref, delta_ref, dq_ref, acc_ref):
        qi, ki = pl.program_id(1), pl.program_id(2)

        @pl.when(ki == 0)
        def _():
            acc_ref[...] = jnp.zeros_like(acc_ref)

        @pl.when(ki <= qi)
        def _():
            p = _attn_probs(q_ref[...], k_ref[...], lse_ref, qi, ki, tile)
            dp = _nt(do_ref[...].astype(BF16), v_ref[...])
            ds = p * (dp - delta_ref[:, 0:1]) * ATT_SCALE
            acc_ref[...] += _nn(ds.astype(BF16), k_ref[...])

        @pl.when(ki == qi)
        def _():
            dq_ref[...] = acc_ref[...]

    q_spec = pl.BlockSpec((tile, SLOT), lambda h, qi, ki: (qi, h))
    k_spec = pl.BlockSpec((tile, SLOT), lambda h, qi, ki: (jnp.minimum(ki, qi), h))
    return pl.pallas_call(
        body, name="attn_bwd_q",
        grid=(N_HEADS, steps, steps),
        in_specs=[q_spec, k_spec, k_spec, q_spec, q_spec, q_spec],
        out_specs=q_spec,
        out_shape=jax.ShapeDtypeStruct((t, N_HEADS * SLOT), F32),
        scratch_shapes=[pltpu.VMEM((tile, SLOT), F32)],
        compiler_params=pltpu.CompilerParams(dimension_semantics=("parallel", "parallel", "arbitrary")),
    )(q, k, v, do, lse, delta)


def _attn_bwd_kv(q, k, v, do, lse, delta):
    t = q.shape[0]
    tile = min(ATT_T, t)
    steps = t // tile

    def body(q_ref, k_ref, v_ref, do_ref, lse_ref, delta_ref, dk_ref, dv_ref, dk_acc, dv_acc):
        ki, qi = pl.program_id(1), pl.program_id(2)

        @pl.when(qi == 0)
        def _():
            dk_acc[...] = jnp.zeros_like(dk_acc)
            dv_acc[...] = jnp.zeros_like(dv_acc)

        @pl.when(qi >= ki)
        def _():
            p = _attn_probs(q_ref[...], k_ref[...], lse_ref, qi, ki, tile)
            do_b = do_ref[...].astype(BF16)
            dv_acc[...] += _tn(p.astype(BF16), do_b)
            dp = _nt(do_b, v_ref[...])
            ds = p * (dp - delta_ref[:, 0:1]) * ATT_SCALE
            dk_acc[...] += _tn(ds.astype(BF16), q_ref[...])

        @pl.when(qi == steps - 1)
        def _():
            dk_ref[...] = dk_acc[...]
            dv_ref[...] = dv_acc[...]

    q_spec = pl.BlockSpec((tile, SLOT), lambda h, ki, qi: (jnp.maximum(qi, ki), h))
    k_spec = pl.BlockSpec((tile, SLOT), lambda h, ki, qi: (ki, h))
    return pl.pallas_call(
        body, name="attn_bwd_kv",
        grid=(N_HEADS, steps, steps),
        in_specs=[q_spec, k_spec, k_spec, q_spec, q_spec, q_spec],
        out_specs=[k_spec, k_spec],
        out_shape=[jax.ShapeDtypeStruct((t, N_HEADS * SLOT), F32)] * 2,
        scratch_shapes=[pltpu.VMEM((tile, SLOT), F32), pltpu.VMEM((tile, SLOT), F32)],
        compiler_params=pltpu.CompilerParams(dimension_semantics=("parallel", "parallel", "arbitrary")),
    )(q, k, v, do, lse, delta)


def _shift_down(x, s):
    if s == 0:
        return x
    row = lax.broadcasted_iota(jnp.int32, x.shape, 0)
    return jnp.where(row >= s, pltpu.roll(x, s, 0), 0.0)


def _shift_up(x, s):
    if s == 0:
        return x
    n = x.shape[0]
    row = lax.broadcasted_iota(jnp.int32, x.shape, 0)
    return jnp.where(row < n - s, pltpu.roll(x, n - s, 0), 0.0)


def _l2norm(x):
    return x * lax.rsqrt(jnp.sum(x * x, axis=-1, keepdims=True) + EPS)


def _conv_pre(x, w):
    y = w[GDN_CONV - 1:GDN_CONV, :] * x
    for s in range(1, GDN_CONV):
        y = y + w[GDN_CONV - 1 - s:GDN_CONV - s, :] * _shift_down(x, s)
    return y


def _gdn_conv_fwd(x, w):
    t, width = x.shape

    def body(x_ref, w_ref, o_ref):
        act = _silu(_conv_pre(x_ref[...], w_ref[...]))
        normed = pl.program_id(0) < 2 * N_HEADS
        o_ref[...] = jnp.where(normed, _l2norm(act), act)

    return pl.pallas_call(
        body, name="gdn_conv_fwd",
        grid=(width // SLOT,),
        in_specs=[pl.BlockSpec((t, SLOT), lambda j: (0, j)), pl.BlockSpec((GDN_CONV, SLOT), lambda j: (0, j))],
        out_specs=pl.BlockSpec((t, SLOT), lambda j: (0, j)),
        out_shape=jax.ShapeDtypeStruct((t, width), F32),
        compiler_params=pltpu.CompilerParams(dimension_semantics=("parallel",)),
    )(x, w)


def _gdn_conv_bwd(x, w, dout):
    t, width = x.shape

    def body(x_ref, w_ref, do_ref, dx_ref, dw_ref):
        xv, wv = x_ref[...], w_ref[...]
        y = _conv_pre(xv, wv)
        sig = _sigmoid(y)
        act = y * sig
        _, pull = jax.vjp(_l2norm, act)
        normed = pl.program_id(0) < 2 * N_HEADS
        dact = jnp.where(normed, pull(do_ref[0])[0], do_ref[0])
        dy = dact * (sig * (1.0 + y * (1.0 - sig)))
        dx = wv[GDN_CONV - 1:GDN_CONV, :] * dy
        for s in range(1, GDN_CONV):
            dx = dx + wv[GDN_CONV - 1 - s:GDN_CONV - s, :] * _shift_up(dy, s)
        dx_ref[...] = dx.astype(BF16)
        for s in range(GDN_CONV):
            dw_ref[GDN_CONV - 1 - s:GDN_CONV - s, :] = jnp.sum(dy * _shift_down(xv, s), axis=0, keepdims=True)

    col = pl.BlockSpec((t, SLOT), lambda j: (0, j))
    tap = pl.BlockSpec((GDN_CONV, SLOT), lambda j: (0, j))
    return pl.pallas_call(
        body, name="gdn_conv_bwd",
        grid=(width // SLOT,),
        in_specs=[col, tap, pl.BlockSpec((1, t, SLOT), lambda j: (j // N_HEADS, 0, j % N_HEADS))],
        out_specs=[col, tap],
        out_shape=[jax.ShapeDtypeStruct((t, width), BF16), jax.ShapeDtypeStruct((GDN_CONV, width), F32)],
        compiler_params=pltpu.CompilerParams(dimension_semantics=("parallel",)),
    )(x, w, dout)


def _softplus(x):
    e = jnp.exp(-jnp.abs(x))
    u = 1.0 + e
    log1p = jnp.where(u == 1.0, e, jnp.log(u) * e / jnp.where(u == 1.0, 1.0, u - 1.0))
    return jnp.maximum(x, 0.0) + log1p


def _gates_fwd(ab, a_log, dt_bias):
    def fn(rows, consts):
        (abv,), (alog, dtb) = rows, consts
        g = -jnp.exp(alog) * _softplus(abv + dtb)
        beta = _sigmoid(abv)
        shape = (abv.shape[0], SLOT)
        g_slots = [jnp.broadcast_to(g[:, h:h + 1], shape) for h in range(N_HEADS)]
        b_slots = [jnp.broadcast_to(beta[:, N_HEADS + h:N_HEADS + h + 1], shape) for h in range(N_HEADS)]
        return [jnp.concatenate(g_slots, axis=1), jnp.concatenate(b_slots, axis=1)], []

    width = N_HEADS * SLOT
    return _rowwise("gdn_gates_fwd", fn, [ab], [a_log, dt_bias], [(width, F32), (width, F32)])


def _gates_bwd(ab, a_log, dt_bias, dg, dbeta):
    def fn(rows, consts):
        (abv, dgv, dbv), (alog, dtb) = rows, consts
        lane = lax.broadcasted_iota(jnp.int32, abv.shape, 1)
        dg_tok = jnp.zeros_like(abv)
        db_tok = jnp.zeros_like(abv)
        for h in range(N_HEADS):
            dg_tok = dg_tok + jnp.where(lane == h, jnp.sum(dgv[:, h * SLOT:(h + 1) * SLOT], axis=1, keepdims=True), 0.0)
            db_tok = db_tok + jnp.where(lane == N_HEADS + h, jnp.sum(dbv[:, h * SLOT:(h + 1) * SLOT], axis=1, keepdims=True), 0.0)
        xa = abv + dtb
        g = -jnp.exp(alog) * _softplus(xa)
        da = dg_tok * (-jnp.exp(alog)) * _sigmoid(xa)
        beta = _sigmoid(abv)
        dab = jnp.where(lane < N_HEADS, da, db_tok * beta * (1.0 - beta))
        dab = jnp.where(lane < 2 * N_HEADS, dab, 0.0)
        d_alog = jnp.sum(jnp.where(lane < N_HEADS, dg_tok * g, 0.0), axis=0, keepdims=True)
        d_dtb = jnp.sum(jnp.where(lane < N_HEADS, da, 0.0), axis=0, keepdims=True)
        return [dab], [d_alog, d_dtb]

    return _rowwise("gdn_gates_bwd", fn, [ab, dg, dbeta], [a_log, dt_bias], [(SLOT, F32)], sums=[SLOT, SLOT])


ROPE_HALF = MLA_ROPE // 2


def _rope_tables(positions):
    freqs = ROPE_THETA ** (-jnp.arange(ROPE_HALF, dtype=F32) / ROPE_HALF)
    ang = positions.astype(F32)[:, None] * freqs
    cos, sin = jnp.cos(ang), jnp.sin(ang)
    t = positions.shape[0]
    ones, zeros = jnp.ones((t, MLA_NOPE), F32), jnp.zeros((t, MLA_NOPE), F32)
    tail = jnp.zeros((t, SLOT - MLA_NOPE - MLA_ROPE), F32)
    half0 = jnp.zeros((t, ROPE_HALF), F32)
    same = jnp.concatenate([ones, cos, cos, tail], axis=1)
    from_low = jnp.concatenate([zeros, half0, sin, tail], axis=1)
    from_high = jnp.concatenate([zeros, -sin, half0, tail], axis=1)
    return same, from_low, from_high


def _rope(x, tabs):
    same, from_low, from_high = tabs
    width = x.shape[1]
    return x * same + pltpu.roll(x, ROPE_HALF, 1) * from_low + pltpu.roll(x, width - ROPE_HALF, 1) * from_high


def _rope_transposed(dy, tabs):
    same, from_low, from_high = tabs
    width = dy.shape[1]
    return dy * same + pltpu.roll(dy * from_low, width - ROPE_HALF, 1) + pltpu.roll(dy * from_high, ROPE_HALF, 1)


def _tile_slots(tab):
    return jnp.concatenate([tab] * N_HEADS, axis=1)


A_WIDTH = MLA_Q_RANK + MLA_KV_RANK + 2 * SLOT
A_KPE = MLA_Q_RANK + MLA_KV_RANK
A_AB = A_KPE + SLOT
WIDE = N_HEADS * SLOT


def _mla_pre_fwd(proj_a, tabs, g_q, g_kv):
    def fn(rows, consts):
        pa, *tb = rows
        gq, gkv = consts
        return [_rms(pa[:, :MLA_Q_RANK], gq, MLA_Q_RANK), _rms(pa[:, MLA_Q_RANK:A_KPE], gkv, MLA_KV_RANK),
                _rope(pa[:, A_KPE:A_AB], tb)], []

    return _rowwise("mla_pre_fwd", fn, [proj_a, *tabs], [g_q, g_kv], [(MLA_Q_RANK, BF16), (MLA_KV_RANK, BF16), (SLOT, F32)])


def _mla_pre_bwd(proj_a, tabs, g_q, g_kv, dcqn, dckvn, dkpe, dab):
    def fn(rows, consts):
        pa, t0, t1, t2, dq, dkv, dk, da = rows
        gq, gkv = consts
        _, pull_q = jax.vjp(lambda x, g: _rms(x, g, MLA_Q_RANK), pa[:, :MLA_Q_RANK], gq)
        _, pull_kv = jax.vjp(lambda x, g: _rms(x, g, MLA_KV_RANK), pa[:, MLA_Q_RANK:A_KPE], gkv)
        dcq, dgq = pull_q(dq)
        dckv, dgkv = pull_kv(dkv)
        return [jnp.concatenate([dcq, dckv, _rope_transposed(dk, (t0, t1, t2)), da], axis=1)], [dgq, dgkv]

    return _rowwise("mla_pre_bwd", fn, [proj_a, *tabs, dcqn, dckvn, dkpe, dab], [g_q, g_kv], [(A_WIDTH, BF16)],
                    sums=[MLA_Q_RANK, MLA_KV_RANK])


def _mla_qkv_fwd(q_p, kv_p, kpe, tabs):
    def fn(rows, consts):
        qv, kvv, kp, *tb = rows
        q = _rope(qv, [_tile_slots(x) for x in tb])
        k = kvv[:, :WIDE] + _tile_slots(kp)
        return [q, k, kvv[:, WIDE:]], []

    return _rowwise("mla_qkv_fwd", fn, [q_p, kv_p, kpe, *tabs], [], [(WIDE, BF16)] * 3)


def _mla_qkv_bwd(dq, dk, dv, tabs):
    def fn(rows, consts):
        dqv, dkv, dvv, *tb = rows
        dkpe = dkv[:, :SLOT]
        for h in range(1, N_HEADS):
            dkpe = dkpe + dkv[:, h * SLOT:(h + 1) * SLOT]
        return [_rope_transposed(dqv, [_tile_slots(x) for x in tb]), jnp.concatenate([dkv, dvv], axis=1), dkpe], []

    return _rowwise("mla_qkv_bwd", fn, [dq, dk, dv, *tabs], [], [(WIDE, BF16), (2 * WIDE, BF16), (SLOT, F32)])


def _slot_sum(x):
    parts = [jnp.broadcast_to(jnp.sum(x[:, h * SLOT:(h + 1) * SLOT], axis=1, keepdims=True), (x.shape[0], SLOT))
             for h in range(N_HEADS)]
    return jnp.concatenate(parts, axis=1)


def _mix_join(o_mla, o_gdn, gate, g_mla, g_gdn):
    mla = _rms(o_mla, g_mla, N_HEADS * MLA_V)
    gdn = o_gdn * lax.rsqrt(_slot_sum(o_gdn * o_gdn) * (1.0 / GDN_D) + EPS) * g_gdn * _silu(gate)
    return mla, gdn


def _mix_join_fwd(o_mla, o_gdn, gate, g_mla, g_gdn):
    def fn(rows, consts):
        return [jnp.concatenate(_mix_join(*rows, *consts), axis=1)], []

    return _rowwise("mix_join_fwd", fn, [o_mla, o_gdn, gate], [g_mla, g_gdn], [(2 * WIDE, BF16)])


def _mix_join_bwd(o_mla, o_gdn, gate, g_mla, g_gdn, dcat):
    def fn(rows, consts):
        om, og, gt, dc = rows
        gm, gg = consts
        _, pull = jax.vjp(lambda x, g: _rms(x, g, N_HEADS * MLA_V), om, gm)
        dom, dgm = pull(dc[:, :WIDE])
        dy = dc[:, WIDE:]
        r = lax.rsqrt(_slot_sum(og * og) * (1.0 / GDN_D) + EPS)
        sig = _sigmoid(gt)
        normed = og * r
        dn = dy * gg * (gt * sig)
        dog = r * dn - normed * (r * r) * _slot_sum(dn * og) * (1.0 / GDN_D)
        dgt = dy * normed * gg * (sig * (1.0 + gt * (1.0 - sig)))
        dgg = jnp.sum(dy * normed * (gt * sig), axis=0, keepdims=True)
        return [dom, _slot_sum(dom * om), dog, dgt], [dgm, dgg]

    return _rowwise("mix_join_bwd", fn, [o_mla, o_gdn, gate, dcat], [g_mla, g_gdn],
                    [(WIDE, F32), (WIDE, F32), (WIDE, F32), (WIDE, BF16)], sums=[WIDE, WIDE])


def _norm_residual_fwd(name, x, h, g, out_dtypes):
    dm = x.shape[1]

    def fn(rows, consts):
        y = rows[0] + _rms(rows[1], consts[0], dm)
        return [y] + [_rms(y, gg, dm) for gg in consts[1:]], []

    return _rowwise(name, fn, [x, h], list(g), [(dm, dt) for dt in out_dtypes])


def _norm_residual_bwd(name, h, g, dy):
    dm = h.shape[1]

    def fn(rows, consts):
        _, pull = jax.vjp(lambda hv, gv: _rms(hv, gv, dm), rows[0], consts[0])
        dh, dg = pull(rows[1])
        return [dh], [dg]

    return _rowwise(name, fn, [h, dy], [g], [(dm, BF16)], sums=[dm])


def _norm_bwd_add(name, x, g, dns, dy):
    dm = x.shape[1]

    def fn(rows, consts):
        xv, dyv, *parts = rows
        dn = parts[0]
        for p in parts[1:]:
            dn = dn + p
        _, pull = jax.vjp(lambda a, gv: _rms(a, gv, dm), xv, consts[0])
        dx, dg = pull(dn)
        return [dyv + dx], [dg]

    return _rowwise(name, fn, [x, dy, *dns], [g], [(dm, F32)], sums=[dm])


def _loss_fwd(y, target):
    dm = y.shape[1]

    def fn(rows, consts):
        err = rows[0] - rows[1]
        sq = err * err
        lanes = sq[:, :SLOT]
        for j in range(1, dm // SLOT):
            lanes = lanes + sq[:, j * SLOT:(j + 1) * SLOT]
        return [err * (1.0 / dm)], [jnp.sum(lanes, axis=0, keepdims=True) * (0.5 / dm)]

    return _rowwise("loss", fn, [y, target], [], [(dm, F32)], sums=[SLOT])


def _norm_fwd(name, x, g):
    dm = x.shape[1]
    return _rowwise(name, lambda rows, consts: ([_rms(rows[0], consts[0], dm)], []), [x], [g], [(dm, BF16)])[0]


W_IN_CUTS = (0, 256, 384, 416, 1952, 1960, 1968, 2480)


def _heads_out(w, per_head, axis=-1):
    w = jnp.moveaxis(w, axis, -1)
    lead = w.shape[:-1]
    w = w.reshape(lead + (w.shape[-1] // per_head, per_head))
    w = jnp.pad(w, [(0, 0)] * len(lead) + [(0, 0), (0, SLOT - per_head)])
    return jnp.moveaxis(w.reshape(lead + (-1,)), -1, axis)


def _heads_in(w, per_head, axis=-1):
    w = jnp.moveaxis(w, axis, -1)
    lead = w.shape[:-1]
    w = w.reshape(lead + (w.shape[-1] // SLOT, SLOT))[..., :per_head]
    return jnp.moveaxis(w.reshape(lead + (-1,)), -1, axis)


def _pad_lanes(v, lo, width=SLOT):
    return jnp.pad(v, [(0, 0)] * (v.ndim - 1) + [(lo, width - lo - v.shape[-1])])


def _layout_weights(w):
    c = W_IN_CUTS
    w_in = w["w_in"]
    p = {}
    p["w_a"] = jnp.concatenate([w_in[:, c[0]:c[2]], _pad_lanes(w_in[:, c[2]:c[3]], MLA_NOPE),
                                _pad_lanes(w_in[:, c[4]:c[6]], 0)], axis=1)
    p["w_qkv"] = _heads_out(w_in[:, c[3]:c[4]], GDN_D)
    p["w_gate"] = _heads_out(w_in[:, c[6]:c[7]], GDN_D)
    p["w_uq"] = _heads_out(w["mla_w_uq"], MLA_NOPE + MLA_ROPE)
    ukv = w["mla_w_ukv"].reshape(MLA_KV_RANK, N_HEADS, MLA_NOPE + MLA_V)
    p["w_kv"] = jnp.concatenate([_heads_out(ukv[:, :, :MLA_NOPE].reshape(MLA_KV_RANK, -1), MLA_NOPE),
                                 _heads_out(ukv[:, :, MLA_NOPE:].reshape(MLA_KV_RANK, -1), MLA_V)], axis=1)
    p["w_out"] = _heads_out(w["w_out"], GDN_D, axis=0)
    p["conv"] = _heads_out(w["gdn_conv_w"], GDN_D)
    p["g_mla_out"] = _heads_out(w["mla_out_g"], MLA_V)
    p["g_gdn"] = jnp.tile(_pad_lanes(w["gdn_norm_g"], 0), (1, N_HEADS))
    p["a_log"] = _pad_lanes(w["gdn_a_log"], 0)
    p["dt_bias"] = _pad_lanes(w["gdn_dt_bias"], 0)
    return p


def _unlayout_grads(d):
    c = W_IN_CUTS
    g = {}
    da = d["w_a"]
    kpe0 = A_KPE + MLA_NOPE
    g["w_in"] = jnp.concatenate([da[:, :A_KPE], da[:, kpe0:kpe0 + MLA_ROPE], _heads_in(d["w_qkv"], GDN_D),
                                 da[:, A_AB:A_AB + 2 * N_HEADS], _heads_in(d["w_gate"], GDN_D)], axis=1)
    assert g["w_in"].shape[1] == c[-1]
    g["mla_w_uq"] = _heads_in(d["w_uq"], MLA_NOPE + MLA_ROPE)
    dk = _heads_in(d["w_kv"][:, :WIDE], MLA_NOPE).reshape(MLA_KV_RANK, N_HEADS, MLA_NOPE)
    dv = _heads_in(d["w_kv"][:, WIDE:], MLA_V).reshape(MLA_KV_RANK, N_HEADS, MLA_V)
    g["mla_w_ukv"] = jnp.concatenate([dk, dv], axis=2).reshape(MLA_KV_RANK, -1)
    g["w_out"] = _heads_in(d["w_out"], GDN_D, axis=0)
    g["gdn_conv_w"] = _heads_in(d["conv"], GDN_D)
    g["mla_out_g"] = _heads_in(d["g_mla_out"], MLA_V)
    g["gdn_norm_g"] = jnp.sum(d["g_gdn"].reshape(N_HEADS, SLOT), axis=0, keepdims=True)[:, :GDN_D]
    g["gdn_a_log"] = d["a_log"][:, :N_HEADS]
    g["gdn_dt_bias"] = d["dt_bias"][:, :N_HEADS]
    return g


def _weight_grad(name, acts, cots, tm=1024, tn=1408, tk=512):
    return _matmul(name, acts, cots, "tn", tm=tm, tn=tn, tk=tk)


def _local_step(x, positions, target, w):
    p = _layout_weights(w)
    tabs = _rope_tables(positions)

    h1, x1 = _ffn_fwd("ffn1_fwd", x, w["ffn1_pre_g"], w["ffn1_w_gate"], w["ffn1_w_up"], w["ffn1_w_down"], w["ffn1_post_g"])
    hn = _norm_fwd("mix_pre_norm", x1, w["mix_pre_g"])
    proj_a = _matmul("proj_a", hn, p["w_a"], "nn")
    proj_qkv = _matmul("proj_qkv", hn, p["w_qkv"], "nn")
    proj_gate = _matmul("proj_gate", hn, p["w_gate"], "nn")
    cqn, ckvn, kpe = _mla_pre_fwd(proj_a, tabs, w["mla_q_norm_g"], w["mla_kv_norm_g"])
    q_p = _matmul("mla_q", cqn, p["w_uq"], "nn")
    kv_p = _matmul("mla_kv", ckvn, p["w_kv"], "nn")
    q, k, v = _mla_qkv_fwd(q_p, kv_p, kpe, tabs)
    o_mla, lse = _attn_fwd(q, k, v)
    ab = (proj_a, SLOT, A_AB // SLOT)
    qkv_n = _gdn_conv_fwd(proj_qkv, p["conv"])
    gb, bb = _gates_fwd(ab, p["a_log"], p["dt_bias"])
    o_gdn, keep = _gdn_fwd(qkv_n, gb, bb)
    cat = _mix_join_fwd(o_mla, o_gdn, proj_gate, p["g_mla_out"], p["g_gdn"])[0]
    mixed = _matmul("mix_out", cat, p["w_out"], "nn")
    x2 = _norm_residual_fwd("mix_post", x1, mixed, [w["mix_post_g"]], [F32])[0]
    h2, y = _ffn_fwd("ffn2_fwd", x2, w["ffn2_pre_g"], w["ffn2_w_gate"], w["ffn2_w_up"], w["ffn2_w_down"], w["ffn2_post_g"])
    dy, loss_lanes = _loss_fwd(y, target)

    g = {}
    dx2, xn2, dh2, a2, dhg2, dhu2, g["ffn2_pre_g"], g["ffn2_post_g"] = _ffn_bwd(
        "ffn2_bwd", x2, h2, dy, w["ffn2_pre_g"], w["ffn2_w_gate"], w["ffn2_w_up"], w["ffn2_w_down"], w["ffn2_post_g"])
    g["ffn2_w_gate"] = _weight_grad("ffn2_dw_gate", xn2, dhg2)
    g["ffn2_w_up"] = _weight_grad("ffn2_dw_up", xn2, dhu2)
    g["ffn2_w_down"] = _weight_grad("ffn2_dw_down", a2, dh2, tm=1408, tn=1024)
    dmixed, g["mix_post_g"] = _norm_residual_bwd("mix_post_bwd", mixed, w["mix_post_g"], dx2)
    dcat = _matmul("mix_out_dx", dmixed, p["w_out"], "nt")
    d = {}
    d["w_out"] = _weight_grad("mix_out_dw", cat, dmixed, tn=1024)
    do_mla, delta, do_gdn, dgate, d["g_mla_out"], d["g_gdn"] = _mix_join_bwd(o_mla, o_gdn, proj_gate, p["g_mla_out"], p["g_gdn"], dcat)
    dq = _attn_bwd_q(q, k, v, do_mla, lse, delta)
    dk, dv = _attn_bwd_kv(q, k, v, do_mla, lse, delta)
    dq_p, dkv_p, dkpe = _mla_qkv_bwd(dq, dk, dv, tabs)
    dcqn = _matmul("mla_q_dx", dq_p, p["w_uq"], "nt")
    d["w_uq"] = _weight_grad("mla_q_dw", cqn, dq_p, tn=1024)
    dckvn = _matmul("mla_kv_dx", dkv_p, p["w_kv"], "nt")
    d["w_kv"] = _weight_grad("mla_kv_dw", ckvn, dkv_p, tn=1024)
    dqkv_n, dgb, dbb = _gdn_bwd(qkv_n, gb, bb, keep, do_gdn)
    dab, d["a_log"], d["dt_bias"] = _gates_bwd(ab, p["a_log"], p["dt_bias"], dgb, dbb)
    dproj_qkv, d["conv"] = _gdn_conv_bwd(proj_qkv, p["conv"], dqkv_n)
    dproj_a, g["mla_q_norm_g"], g["mla_kv_norm_g"] = _mla_pre_bwd(
        proj_a, tabs, w["mla_q_norm_g"], w["mla_kv_norm_g"], dcqn, dckvn, dkpe, dab)
    dhn = [_matmul("proj_a_dx", dproj_a, p["w_a"], "nt"), _matmul("proj_qkv_dx", dproj_qkv, p["w_qkv"], "nt"),
           _matmul("proj_gate_dx", dgate, p["w_gate"], "nt")]
    d["w_a"] = _weight_grad("proj_a_dw", hn, dproj_a, tn=640)
    d["w_qkv"] = _weight_grad("proj_qkv_dw", hn, dproj_qkv, tn=1024)
    d["w_gate"] = _weight_grad("proj_gate_dw", hn, dgate, tn=1024)
    dx1, g["mix_pre_g"] = _norm_bwd_add("mix_pre_bwd", x1, w["mix_pre_g"], dhn, dx2)
    dx, xn1, dh1, a1, dhg1, dhu1, g["ffn1_pre_g"], g["ffn1_post_g"] = _ffn_bwd(
        "ffn1_bwd", x, h1, dx1, w["ffn1_pre_g"], w["ffn1_w_gate"], w["ffn1_w_up"], w["ffn1_w_down"], w["ffn1_post_g"])
    g["ffn1_w_gate"] = _weight_grad("ffn1_dw_gate", xn1, dhg1)
    g["ffn1_w_up"] = _weight_grad("ffn1_dw_up", xn1, dhu1)
    g["ffn1_w_down"] = _weight_grad("ffn1_dw_down", a1, dh1, tm=1408, tn=1024)
    g.update(_unlayout_grads(d))
    return loss_lanes, dx, g


MESH_AXES = ("x", "y", "c")


def _peer(r):
    x, y, c = (lax.axis_index(a) for a in MESH_AXES)
    return (1 - x if r & 4 else x, 1 - y if r & 2 else y, 1 - c if r & 1 else c)


def _block_of(dev):
    x, y, c = dev
    return 4 * x + 2 * y + c


def _exchange(name, buf, gather):
    shape = buf.shape[-2:]

    def body(x_ref, out_ref, send_sems, recv_sems, local_sem):
        me = _block_of(_peer(0))
        mine = x_ref if gather else x_ref.at[me]
        local = pltpu.make_async_copy(mine, out_ref.at[me], local_sem)
        local.start()

        def copy(r):
            src = x_ref if gather else x_ref.at[_block_of(_peer(r))]
            return pltpu.make_async_remote_copy(
                src_ref=src, dst_ref=out_ref.at[me], send_sem=send_sems.at[r - 1], recv_sem=recv_sems.at[r - 1],
                device_id=_peer(r), device_id_type=pl.DeviceIdType.MESH)

        def arrival(r):
            return pltpu.make_async_remote_copy(
                src_ref=mine, dst_ref=out_ref.at[_block_of(_peer(r))], send_sem=send_sems.at[r - 1],
                recv_sem=recv_sems.at[r - 1], device_id=_peer(r), device_id_type=pl.DeviceIdType.MESH)

        sends = [copy(r) for r in range(1, N_DEV)]
        for cp in sends:
            cp.start()
        for r in range(1, N_DEV):
            arrival(r).wait_recv()
        for cp in sends:
            cp.wait_send()
        local.wait()

    return pl.pallas_call(
        body, name=name,
        in_specs=[pl.BlockSpec(memory_space=pl.ANY)],
        out_specs=pl.BlockSpec(memory_space=pl.ANY),
        out_shape=jax.ShapeDtypeStruct((N_DEV,) + shape, buf.dtype),
        scratch_shapes=[pltpu.SemaphoreType.DMA((N_DEV - 1,)), pltpu.SemaphoreType.DMA((N_DEV - 1,)), pltpu.SemaphoreType.DMA(())],
    )(buf)


def _sum_blocks(name, blocks, tm):
    _, rows, width = blocks.shape
    assert rows % tm == 0

    def body(x_ref, o_ref):
        acc = x_ref[0].astype(F32)
        for d in range(1, N_DEV):
            acc = acc + x_ref[d].astype(F32)
        o_ref[...] = acc

    return pl.pallas_call(
        body, name=name,
        grid=(rows // tm,),
        in_specs=[pl.BlockSpec((N_DEV, tm, width), lambda i: (0, i, 0))],
        out_specs=pl.BlockSpec((tm, width), lambda i: (i, 0)),
        out_shape=jax.ShapeDtypeStruct((rows, width), F32),
        compiler_params=pltpu.CompilerParams(dimension_semantics=("parallel",)),
    )(blocks)


def _all_reduce_small(name, vec):
    rows, width = vec.shape

    def body(x_ref, o_ref, all_ref, send_sems, recv_sems):
        me = _block_of(_peer(0))
        all_ref[me] = x_ref[...]

        def copy(r, block):
            return pltpu.make_async_remote_copy(
                src_ref=x_ref, dst_ref=all_ref.at[block], send_sem=send_sems.at[r - 1], recv_sem=recv_sems.at[r - 1],
                device_id=_peer(r), device_id_type=pl.DeviceIdType.MESH)

        sends = [copy(r, me) for r in range(1, N_DEV)]
        for cp in sends:
            cp.start()
        for r in range(1, N_DEV):
            copy(r, _block_of(_peer(r))).wait_recv()
        for cp in sends:
            cp.wait_send()
        acc = all_ref[0]
        for d in range(1, N_DEV):
            acc = acc + all_ref[d]
        o_ref[...] = acc

    return pl.pallas_call(
        body, name=name,
        in_specs=[pl.BlockSpec(memory_space=pltpu.VMEM)],
        out_specs=pl.BlockSpec(memory_space=pltpu.VMEM),
        out_shape=jax.ShapeDtypeStruct((rows, width), F32),
        scratch_shapes=[pltpu.VMEM((N_DEV, rows, width), F32), pltpu.SemaphoreType.DMA((N_DEV - 1,)), pltpu.SemaphoreType.DMA((N_DEV - 1,))],
    )(vec)


def _adamw(name, w, g, m, v, tm):
    def fn(rows, consts):
        wv, gv, mv, vv = rows
        m2 = ADAM_B1 * mv + (1.0 - ADAM_B1) * gv
        v2 = ADAM_B2 * vv + (1.0 - ADAM_B2) * jnp.square(gv)
        m_hat = m2 / (1.0 - ADAM_B1 ** ADAM_STEP)
        v_hat = v2 / (1.0 - ADAM_B2 ** ADAM_STEP)
        return [-ADAM_LR * (m_hat / (jnp.sqrt(v_hat) + ADAM_EPS) + ADAM_WD * wv), m2, v2], []

    width = w.shape[1]
    return _rowwise(name, fn, [w, g, m, v], [], [(width, F32)] * 3, tm=tm)


ROW = 1024
BIG = {
    "ffn1_w_gate": ((D_MODEL, D_FF), 1), "ffn1_w_up": ((D_MODEL, D_FF), 1), "ffn1_w_down": ((D_FF, D_MODEL), 0),
    "w_in": ((D_MODEL, W_IN_CUTS[-1]), 1), "mla_w_uq": ((MLA_Q_RANK, N_HEADS * (MLA_NOPE + MLA_ROPE)), 1),
    "mla_w_ukv": ((MLA_KV_RANK, N_HEADS * (MLA_NOPE + MLA_V)), 1), "w_out": ((2 * N_HEADS * GDN_D, D_MODEL), 0),
    "ffn2_w_gate": ((D_MODEL, D_FF), 1), "ffn2_w_up": ((D_MODEL, D_FF), 1), "ffn2_w_down": ((D_FF, D_MODEL), 0),
}
PART_ROWS = 16
BIG_TM = 240
BIG_ROWS = 11 * BIG_TM
SMALL = {
    "ffn1_pre_g": (1024, 1024), "ffn1_post_g": (1024, 1024), "mix_pre_g": (1024, 1024), "mla_q_norm_g": (256, 256),
    "mla_kv_norm_g": (128, 128), "mla_out_g": (512, 512), "gdn_a_log": (8, 128), "gdn_dt_bias": (8, 128),
    "gdn_norm_g": (64, 128), "mix_post_g": (1024, 1024), "ffn2_pre_g": (1024, 1024), "ffn2_post_g": (1024, 1024),
}
SMALL_LANES = sum(r for _, r in SMALL.values())
CONV_SHAPE = (GDN_CONV, 3 * N_HEADS * GDN_D)
CONV_SHARD = (GDN_CONV, CONV_SHAPE[1] // N_DEV)
CONV_LANES = CONV_SHAPE[0] * CONV_SHAPE[1]
SMALL_ROWS = 8
REDUCE_ROWS = 16


def _shard_shape(name):
    shape, axis = BIG[name]
    return tuple(s // N_DEV if i == axis else s for i, s in enumerate(shape))


def _part_rows(name):
    shape = _shard_shape(name)
    rows = shape[0] * shape[1] // ROW
    return rows, -(-rows // PART_ROWS) * PART_ROWS


def _pad_rows(a, rows, axis):
    pad = [(0, 0)] * a.ndim
    pad[axis] = (0, rows - a.shape[axis])
    return jnp.pad(a, pad)


def _pack_big(shards):
    parts = [_pad_rows(shards[n].reshape(-1, ROW), _part_rows(n)[1], 0) for n in BIG]
    rows = sum(p.shape[0] for p in parts)
    return jnp.concatenate(parts + [jnp.zeros((BIG_ROWS - rows, ROW), parts[0].dtype)], axis=0)


def _unpack_big(buf):
    out, at = {}, 0
    for n in BIG:
        rows, taken = _part_rows(n)
        out[n] = buf[at:at + rows].reshape(_shard_shape(n))
        at += taken
    return out


def _join_big(gathered):
    out, at = {}, 0
    for n, (shape, axis) in BIG.items():
        rows, taken = _part_rows(n)
        blocks = gathered[:, at:at + rows].reshape((N_DEV,) + _shard_shape(n))
        out[n] = blocks.reshape(shape) if axis == 0 else blocks.transpose(1, 0, 2).reshape(shape)
        at += taken
    return out


def _split_big(full, dtype):
    parts = []
    for n, (shape, axis) in BIG.items():
        sh = _shard_shape(n)
        blocks = full[n].astype(dtype).reshape((N_DEV,) + sh) if axis == 0 else full[n].astype(dtype).reshape(
            sh[0], N_DEV, sh[1]).transpose(1, 0, 2)
        parts.append(_pad_rows(blocks.reshape(N_DEV, -1, ROW), _part_rows(n)[1], 1))
    rows = sum(p.shape[1] for p in parts)
    return jnp.concatenate(parts + [jnp.zeros((N_DEV, BIG_ROWS - rows, ROW), dtype)], axis=1)


def _pack_small(vecs, conv, rows):
    parts = [_pad_lanes(vecs[n].reshape(1, -1), 0, r) for n, (_, r) in SMALL.items()]
    if conv is not None:
        parts.append(conv.reshape(1, -1))
    flat = jnp.concatenate(parts, axis=1)
    return _pad_lanes(flat, 0, rows * ROW).reshape(rows, ROW)


def _unpack_small(buf):
    flat = buf.reshape(1, -1)
    out, at = {}, 0
    for n, (w, r) in SMALL.items():
        out[n] = flat[:, at:at + w]
        at += r
    return out, flat[0, at:]


def kernel(x, positions, ffn1_pre_g, ffn1_w_gate, ffn1_w_up, ffn1_w_down, ffn1_post_g, mix_pre_g, w_in, mla_q_norm_g, mla_w_uq, mla_kv_norm_g, mla_w_ukv, mla_out_g, gdn_conv_w, gdn_a_log, gdn_dt_bias, gdn_norm_g, w_out, mix_post_g, ffn2_pre_g, ffn2_w_gate, ffn2_w_up, ffn2_w_down, ffn2_post_g, loss_target, m_ffn1_pre_g, m_ffn1_w_gate, m_ffn1_w_up, m_ffn1_w_down, m_ffn1_post_g, m_mix_pre_g, m_w_in, m_mla_q_norm_g, m_mla_w_uq, m_mla_kv_norm_g, m_mla_w_ukv, m_mla_out_g, m_gdn_conv_w, m_gdn_a_log, m_gdn_dt_bias, m_gdn_norm_g, m_w_out, m_mix_post_g, m_ffn2_pre_g, m_ffn2_w_gate, m_ffn2_w_up, m_ffn2_w_down, m_ffn2_post_g, v_ffn1_pre_g, v_ffn1_w_gate, v_ffn1_w_up, v_ffn1_w_down, v_ffn1_post_g, v_mix_pre_g, v_w_in, v_mla_q_norm_g, v_mla_w_uq, v_mla_kv_norm_g, v_mla_w_ukv, v_mla_out_g, v_gdn_conv_w, v_gdn_a_log, v_gdn_dt_bias, v_gdn_norm_g, v_w_out, v_mix_post_g, v_ffn2_pre_g, v_ffn2_w_gate, v_ffn2_w_up, v_ffn2_w_down, v_ffn2_post_g):
    given = dict(locals())
    names = list(BIG) + list(SMALL) + ["gdn_conv_w"]
    order = ["ffn1_pre_g", "ffn1_w_gate", "ffn1_w_up", "ffn1_w_down", "ffn1_post_g", "mix_pre_g", "w_in", "mla_q_norm_g",
             "mla_w_uq", "mla_kv_norm_g", "mla_w_ukv", "mla_out_g", "gdn_conv_w", "gdn_a_log", "gdn_dt_bias", "gdn_norm_g",
             "w_out", "mix_post_g", "ffn2_pre_g", "ffn2_w_gate", "ffn2_w_up", "ffn2_w_down", "ffn2_post_g"]
    assert sorted(names) == sorted(order)
    def drop_depth(a):
        return a[0] if a.ndim == 3 else a

    wts = {n: drop_depth(given[n]) for n in order}
    mom = {n: drop_depth(given["m_" + n]) for n in order}
    var = {n: drop_depth(given["v_" + n]) for n in order}
    me = _block_of(_peer(0))

    gathered = _exchange("gather_weights", _pack_big({n: wts[n].astype(BF16) for n in BIG}), gather=True)
    conv_at = lax.dynamic_update_slice(jnp.zeros((N_DEV, CONV_SHARD[0] * CONV_SHARD[1]), F32),
                                       wts["gdn_conv_w"].reshape(1, -1), (me, 0))
    conv_all = _all_reduce_small("gather_conv", _pad_lanes(conv_at.reshape(1, -1), 0, SMALL_ROWS * ROW).reshape(SMALL_ROWS, ROW))
    conv_full = conv_all.reshape(-1)[:CONV_LANES].reshape((N_DEV,) + CONV_SHARD).transpose(1, 0, 2).reshape(CONV_SHAPE)
    full = _join_big(gathered)
    full.update({n: wts[n] for n in SMALL})
    full["gdn_conv_w"] = conv_full

    loss_lanes, dx, grads = _local_step(x[0], positions[0], loss_target[0], full)
    loss = lax.psum(jnp.sum(loss_lanes), MESH_AXES)

    landed = _exchange("scatter_grads", _split_big(grads, BF16), gather=False)
    big_grad = _sum_blocks("sum_grads", landed, BIG_TM)
    small_sum = _all_reduce_small("reduce_small", _pack_small(grads, grads["gdn_conv_w"].reshape(-1), REDUCE_ROWS))
    small_grad, conv_grad_full = _unpack_small(small_sum)
    conv_grad = lax.dynamic_slice(conv_grad_full[:CONV_LANES].reshape(CONV_SHAPE), (0, me * CONV_SHARD[1]), CONV_SHARD)

    big = [_pack_big({n: s[n] for n in BIG}) for s in (wts, mom, var)]
    big_out = _adamw("adamw_big", big[0], big_grad, big[1], big[2], BIG_TM)
    small_g = dict(small_grad)
    small = [_pack_small(s, s["gdn_conv_w"].reshape(-1), SMALL_ROWS) for s in (wts, {**small_g, "gdn_conv_w": conv_grad}, mom, var)]
    small_out = _adamw("adamw_small", *small, SMALL_ROWS)

    outs = {"grad": {**_unpack_big(big_grad), **small_g, "gdn_conv_w": conv_grad}}
    for kind, b, s in zip(("delta", "new_m", "new_v"), big_out, small_out):
        vecs, conv = _unpack_small(s)
        outs[kind] = {**_unpack_big(b), **vecs, "gdn_conv_w": conv[:CONV_SHARD[0] * CONV_SHARD[1]].reshape(CONV_SHARD)}
    result = [loss, dx[None]]
    for kind in ("grad", "delta", "new_m", "new_v"):
        result += [outs[kind][n].reshape(given[n].shape) for n in order]
    return tuple(result)
```

```python
import functools

import jax
import jax.numpy as jnp
import numpy as np
from jax import lax
from jax.experimental import pallas as pl
from jax.experimental.pallas import tpu as pltpu

F32 = jnp.float32
BF16 = jnp.bfloat16
HI = lax.Precision.HIGHEST

N_DEV = 8
D_MODEL = 1024
D_FF = 2816
N_HEADS = 8
SLOT = 128
MLA_Q_RANK = 256
MLA_KV_RANK = 128
MLA_NOPE = 64
MLA_ROPE = 32
MLA_V = 64
GDN_D = 64
GDN_CONV = 4
GDN_CHUNK = 64
ROPE_THETA = 10000.0
EPS = 1e-6
ADAM_LR, ADAM_B1, ADAM_B2, ADAM_EPS, ADAM_WD, ADAM_STEP = 0.001, 0.9, 0.999, 1e-08, 0.01, 10


def _dot(a, b, ca, cb, precision=None):
    lead = a.ndim - 2
    batch = tuple(range(lead))
    return lax.dot_general(a, b, (((lead + ca,), (lead + cb,)), (batch, batch)), precision=precision,
                           preferred_element_type=F32)


def _nn(a, b, precision=None):
    return _dot(a, b, 1, 0, precision)


def _nt(a, b, precision=None):
    return _dot(a, b, 1, 1, precision)


def _tn(a, b, precision=None):
    return _dot(a, b, 0, 0, precision)


def _sigmoid(x):
    return 1.0 / (1.0 + jnp.exp(-x))


def _silu(x):
    return x * _sigmoid(x)


def _rms(x, g, n):
    ms = jnp.sum(x * x, axis=-1, keepdims=True) * (1.0 / n)
    return x * lax.rsqrt(ms + EPS) * g


def _chunk_masks():
    c = GDN_CHUNK
    i = lax.broadcasted_iota(jnp.int32, (c, c), 0)
    j = lax.broadcasted_iota(jnp.int32, (c, c), 1)
    lower = i >= j
    strict = i > j
    eye = (i == j).astype(F32)
    blocks = []
    b = 1
    while b < c:
        same = (i // (2 * b)) == (j // (2 * b))
        blocks.append(same & ((i % (2 * b)) >= b) & ((j % (2 * b)) < b))
        b *= 2
    return lower, strict, eye, blocks


def _unit_lower_inverse(low, eye, blocks):
    t = jnp.broadcast_to(eye, low.shape)
    for m in blocks:
        lo = jnp.where(m, low, 0.0)
        t = t - _nn(t, _nn(lo, t, HI), HI)
    return t


def _gdn_chunk(q, k, v, gb, bb, s, masks):
    lower, strict, eye, blocks = masks
    qs = q * (GDN_D ** -0.5)
    gc = _nn(jnp.broadcast_to(lower.astype(F32), gb.shape), gb, HI)
    gct = _nt(jnp.broadcast_to(eye, gb.shape), gc, HI)
    decay = jnp.exp(jnp.where(lower, gc - gct, -1e30))
    kb = k * bb
    low = jnp.where(strict, _nt(kb, k, HI) * decay, 0.0)
    tinv = _unit_lower_inverse(low, eye, blocks)
    eg = jnp.exp(gc)
    w = _nn(tinv, kb * eg, HI)
    u = _nn(tinv, v * bb, HI)
    attn = _nt(qs, k, HI) * decay
    g_end = jnp.sum(gb, axis=-2, keepdims=True)
    k_dec = k * jnp.exp(g_end - gc)
    v_new = u - _nn(w, s, HI)
    o = _nn(qs * eg, s, HI) + _nn(attn, v_new, HI)
    s_new = s * jnp.exp(g_end) + _tn(k_dec, v_new, HI)
    return o, s_new


GDN_GROUP = 8
GDN_GROUPS = N_HEADS // GDN_GROUP


def _group_heads(ref):
    return jnp.stack([ref[:, pl.ds(j * SLOT, GDN_D)] for j in range(GDN_GROUP)])


def _ungroup_heads(ref, val):
    pad = jnp.zeros((GDN_CHUNK, SLOT - GDN_D), F32)
    for j in range(GDN_GROUP):
        ref[:, pl.ds(j * SLOT, GDN_D)] = val[j]
        ref[:, pl.ds(j * SLOT + GDN_D, SLOT - GDN_D)] = pad


def _gdn_fwd(qkv, gb, bb):
    t = qkv.shape[0]
    n_chunks = t // GDN_CHUNK
    d = GDN_D

    def body(q_ref, k_ref, v_ref, g_ref, b_ref, o_ref, keep_ref, s_ref):
        @pl.when(pl.program_id(1) == 0)
        def _():
            s_ref[...] = jnp.zeros_like(s_ref)

        s = s_ref[...]
        keep_ref[:, 0] = s
        o, s_new = _gdn_chunk(*[_group_heads(r) for r in (q_ref, k_ref, v_ref, g_ref, b_ref)], s, _chunk_masks())
        s_ref[...] = s_new
        _ungroup_heads(o_ref, o)

    def spec(kind=0):
        return pl.BlockSpec((GDN_CHUNK, GDN_GROUP * SLOT), lambda h, n: (n, kind * GDN_GROUPS + h))

    return pl.pallas_call(
        body, name="gdn_fwd",
        grid=(GDN_GROUPS, n_chunks),
        in_specs=[spec(0), spec(1), spec(2), spec(), spec()],
        out_specs=[spec(), pl.BlockSpec((GDN_GROUP, 1, d, d), lambda h, n: (h, n, 0, 0))],
        out_shape=[jax.ShapeDtypeStruct((t, N_HEADS * SLOT), F32), jax.ShapeDtypeStruct((N_HEADS, n_chunks, d, d), F32)],
        scratch_shapes=[pltpu.VMEM((GDN_GROUP, d, d), F32)],
        compiler_params=pltpu.CompilerParams(dimension_semantics=("parallel", "arbitrary")),
    )(qkv, qkv, qkv, gb, bb)


def _gdn_bwd(qkv, gb, bb, keep, do):
    t = qkv.shape[0]
    n_chunks = t // GDN_CHUNK
    d = GDN_D

    def body(q_ref, k_ref, v_ref, g_ref, b_ref, keep_ref, do_ref, dqkv_ref, dg_ref, db_ref, ds_ref):
        @pl.when(pl.program_id(1) == 0)
        def _():
            ds_ref[...] = jnp.zeros_like(ds_ref)

        masks = _chunk_masks()
        _, pull = jax.vjp(lambda *a: _gdn_chunk(*a, masks), *[_group_heads(r) for r in (q_ref, k_ref, v_ref, g_ref, b_ref)],
                          keep_ref[:, 0])
        dq, dk, dv, dg, db, ds = pull((_group_heads(do_ref), ds_ref[...]))
        ds_ref[...] = ds
        for i, val in enumerate((dq, dk, dv)):
            _ungroup_heads(dqkv_ref.at[i], val)
        _ungroup_heads(dg_ref, dg)
        _ungroup_heads(db_ref, db)

    def spec(kind=0):
        return pl.BlockSpec((GDN_CHUNK, GDN_GROUP * SLOT), lambda h, n: (n_chunks - 1 - n, kind * GDN_GROUPS + h))

    return pl.pallas_call(
        body, name="gdn_bwd",
        grid=(GDN_GROUPS, n_chunks),
        in_specs=[spec(0), spec(1), spec(2), spec(), spec(),
                  pl.BlockSpec((GDN_GROUP, 1, d, d), lambda h, n: (h, n_chunks - 1 - n, 0, 0)), spec()],
        out_specs=[pl.BlockSpec((3, GDN_CHUNK, GDN_GROUP * SLOT), lambda h, n: (0, n_chunks - 1 - n, h)), spec(), spec()],
        out_shape=[jax.ShapeDtypeStruct((3, t, N_HEADS * SLOT), F32)] + [jax.ShapeDtypeStruct((t, N_HEADS * SLOT), F32)] * 2,
        scratch_shapes=[pltpu.VMEM((GDN_GROUP, d, d), F32)],
        compiler_params=pltpu.CompilerParams(dimension_semantics=("parallel", "arbitrary")),
    )(qkv, qkv, qkv, gb, bb, keep, do)


def _rowwise(name, fn, rows, consts, outs, sums=(), tm=256):
    rows = [x if isinstance(x, tuple) else (x, x.shape[1], 0) for x in rows]
    t = rows[0][0].shape[0]
    tm = min(tm, t)
    steps = t // tm
    n_r, n_c, n_o, n_s = len(rows), len(consts), len(outs), len(sums)

    def window(width, block):
        return pl.BlockSpec((tm, width), lambda i: (i, block))

    def body(*refs):
        r, c = refs[:n_r], refs[n_r:n_r + n_c]
        o, s = refs[n_r + n_c:n_r + n_c + n_o], refs[n_r + n_c + n_o:]
        vals, tot = fn([x[...] for x in r], [x[...] for x in c])
        for ref, val in zip(o, vals):
            ref[...] = val.astype(ref.dtype)
        if n_s:
            @pl.when(pl.program_id(0) == 0)
            def _():
                for ref in s:
                    ref[...] = jnp.zeros_like(ref)

            for ref, val in zip(s, tot):
                ref[...] += val

    return pl.pallas_call(
        body, name=name,
        grid=(steps,),
        in_specs=[window(w, b) for _, w, b in rows] + [pl.BlockSpec(x.shape, lambda i: (0, 0)) for x in consts],
        out_specs=[pl.BlockSpec((tm, w), lambda i: (i, 0)) for w, _ in outs]
        + [pl.BlockSpec((1, w), lambda i: (0, 0)) for w in sums],
        out_shape=[jax.ShapeDtypeStruct((t, w), dt) for w, dt in outs]
        + [jax.ShapeDtypeStruct((1, w), F32) for w in sums],
        compiler_params=pltpu.CompilerParams(dimension_semantics=("arbitrary",)),
    )(*[x for x, _, _ in rows], *consts)


def _tile(dim, target):
    if dim <= target:
        return dim
    best = None
    for cand in range(128, target + 1, 128):
        if dim % cand == 0:
            best = cand
    assert best is not None, (dim, target)
    return best


def _matmul(name, a, b, mode, out_dtype=F32, tm=512, tn=1024, tk=1024):
    if mode == "nn":
        (m, k), n = a.shape, b.shape[1]
    elif mode == "nt":
        (m, k), n = a.shape, b.shape[0]
    else:
        (k, m), n = a.shape, b.shape[1]
    tm, tn, tk = _tile(m, tm), _tile(n, tn), _tile(k, tk)
    k_steps = k // tk
    product = {"nn": _nn, "nt": _nt, "tn": _tn}[mode]

    def body(a_ref, b_ref, o_ref, acc_ref):
        part = product(a_ref[...].astype(BF16), b_ref[...].astype(BF16))
        if k_steps == 1:
            o_ref[...] = part.astype(o_ref.dtype)
        else:
            kk = pl.program_id(2)

            @pl.when(kk == 0)
            def _():
                acc_ref[...] = part

            @pl.when(kk > 0)
            def _():
                acc_ref[...] += part

            @pl.when(kk == k_steps - 1)
            def _():
                o_ref[...] = acc_ref[...].astype(o_ref.dtype)

    a_spec = pl.BlockSpec((tk, tm), lambda i, j, kk: (kk, i)) if mode == "tn" else pl.BlockSpec((tm, tk), lambda i, j, kk: (i, kk))
    b_spec = pl.BlockSpec((tn, tk), lambda i, j, kk: (j, kk)) if mode == "nt" else pl.BlockSpec((tk, tn), lambda i, j, kk: (kk, j))
    return pl.pallas_call(
        body, name=name,
        grid=(m // tm, n // tn, k_steps),
        in_specs=[a_spec, b_spec],
        out_specs=pl.BlockSpec((tm, tn), lambda i, j, kk: (i, j)),
        out_shape=jax.ShapeDtypeStruct((m, n), out_dtype),
        scratch_shapes=[pltpu.VMEM((tm, tn) if k_steps > 1 else (8, 128), F32)],
        compiler_params=pltpu.CompilerParams(dimension_semantics=("parallel", "parallel", "arbitrary")),
    )(a, b)


FFN_TM = 512
FFN_BWD_TM = 256
FFN_TF = 1408


def _ffn_fwd(name, x, g_pre, w_gate, w_up, w_down, g_post):
    t, dm = x.shape
    f = w_gate.shape[1]
    tm, tf = min(FFN_TM, t), _tile(f, FFN_TF)
    f_steps = f // tf

    def body(x_ref, gpre_ref, wg_ref, wu_ref, wd_ref, gpost_ref, h_ref, y_ref, xn_ref, acc_ref):
        j = pl.program_id(1)

        @pl.when(j == 0)
        def _():
            xn_ref[...] = _rms(x_ref[...], gpre_ref[...], dm).astype(BF16)
            acc_ref[...] = jnp.zeros_like(acc_ref)

        xn = xn_ref[...]
        a = _silu(_nn(xn, wg_ref[...])) * _nn(xn, wu_ref[...])
        acc_ref[...] += _nn(a.astype(BF16), wd_ref[...])

        @pl.when(j == f_steps - 1)
        def _():
            h = acc_ref[...]
            h_ref[...] = h
            y_ref[...] = x_ref[...] + 0.5 * _rms(h, gpost_ref[...], dm)

    row = pl.BlockSpec((tm, dm), lambda i, j: (i, 0))
    vec = pl.BlockSpec((1, dm), lambda i, j: (0, 0))
    return pl.pallas_call(
        body, name=name,
        grid=(t // tm, f_steps),
        in_specs=[row, vec, pl.BlockSpec((dm, tf), lambda i, j: (0, j)), pl.BlockSpec((dm, tf), lambda i, j: (0, j)),
                  pl.BlockSpec((tf, dm), lambda i, j: (j, 0)), vec],
        out_specs=[row, row],
        out_shape=[jax.ShapeDtypeStruct((t, dm), F32)] * 2,
        scratch_shapes=[pltpu.VMEM((tm, dm), BF16), pltpu.VMEM((tm, dm), F32)],
        compiler_params=pltpu.CompilerParams(dimension_semantics=("parallel", "arbitrary")),
    )(x, g_pre, w_gate, w_up, w_down, g_post)


def _ffn_bwd(name, x, h, dy, g_pre, w_gate, w_up, w_down, g_post):
    t, dm = x.shape
    f = w_gate.shape[1]
    tm, tf = min(FFN_BWD_TM, t), _tile(f, FFN_TF)
    f_steps = f // tf

    def post(hv, g):
        return 0.5 * _rms(hv, g, dm)

    def pre(xv, g):
        return _rms(xv, g, dm)

    def body(x_ref, h_ref, dy_ref, gpre_ref, wg_ref, wu_ref, wd_ref, gpost_ref,
             dx_ref, xn_ref, dh_ref, a_ref, dhg_ref, dhu_ref, dgpre_ref, dgpost_ref, acc_ref):
        i, j = pl.program_id(0), pl.program_id(1)

        @pl.when((i == 0) & (j == 0))
        def _():
            dgpre_ref[...] = jnp.zeros_like(dgpre_ref)
            dgpost_ref[...] = jnp.zeros_like(dgpost_ref)

        @pl.when(j == 0)
        def _():
            xn_ref[...] = pre(x_ref[...], gpre_ref[...]).astype(BF16)
            _, pull = jax.vjp(post, h_ref[...], gpost_ref[...])
            dh, dg = pull(dy_ref[...])
            dh_ref[...] = dh.astype(BF16)
            dgpost_ref[...] += dg
            acc_ref[...] = jnp.zeros_like(acc_ref)

        xn = xn_ref[...]
        hg = _nn(xn, wg_ref[...])
        hu = _nn(xn, wu_ref[...])
        da = _nt(dh_ref[...], wd_ref[...])
        sig = _sigmoid(hg)
        act = hg * sig
        dhu = (da * act).astype(BF16)
        dhg = (da * hu * (sig * (1.0 + hg * (1.0 - sig)))).astype(BF16)
        a_ref[...] = (act * hu).astype(BF16)
        dhg_ref[...] = dhg
        dhu_ref[...] = dhu
        acc_ref[...] += _nt(dhg, wg_ref[...]) + _nt(dhu, wu_ref[...])

        @pl.when(j == f_steps - 1)
        def _():
            _, pull = jax.vjp(pre, x_ref[...], gpre_ref[...])
            dx, dg = pull(acc_ref[...])
            dx_ref[...] = dy_ref[...] + dx
            dgpre_ref[...] += dg

    row = pl.BlockSpec((tm, dm), lambda i, j: (i, 0))
    vec = pl.BlockSpec((1, dm), lambda i, j: (0, 0))
    wide = pl.BlockSpec((tm, tf), lambda i, j: (i, j))
    return pl.pallas_call(
        body, name=name,
        grid=(t // tm, f_steps),
        in_specs=[row, row, row, vec, pl.BlockSpec((dm, tf), lambda i, j: (0, j)), pl.BlockSpec((dm, tf), lambda i, j: (0, j)),
                  pl.BlockSpec((tf, dm), lambda i, j: (j, 0)), vec],
        out_specs=[row, row, row, wide, wide, wide, vec, vec],
        out_shape=[jax.ShapeDtypeStruct((t, dm), F32), jax.ShapeDtypeStruct((t, dm), BF16), jax.ShapeDtypeStruct((t, dm), BF16),
                   jax.ShapeDtypeStruct((t, f), BF16), jax.ShapeDtypeStruct((t, f), BF16), jax.ShapeDtypeStruct((t, f), BF16),
                   jax.ShapeDtypeStruct((1, dm), F32), jax.ShapeDtypeStruct((1, dm), F32)],
        scratch_shapes=[pltpu.VMEM((tm, dm), F32)],
        compiler_params=pltpu.CompilerParams(dimension_semantics=("arbitrary", "arbitrary")),
    )(x, h, dy, g_pre, w_gate, w_up, w_down, g_post)


ATT_T = 512
ATT_SCALE = (MLA_NOPE + MLA_ROPE) ** -0.5


def _causal_scores(q, k, qi, ki, tile):
    s = _nt(q, k) * ATT_SCALE
    row = lax.broadcasted_iota(jnp.int32, s.shape, 0) + qi * tile
    col = lax.broadcasted_iota(jnp.int32, s.shape, 1) + ki * tile
    return jnp.where(col <= row, s, -1e30)


def _attn_fwd(q, k, v):
    t = q.shape[0]
    tile = min(ATT_T, t)
    steps = t // tile

    def body(q_ref, k_ref, v_ref, o_ref, lse_ref, m_ref, l_ref, acc_ref):
        qi, ki = pl.program_id(1), pl.program_id(2)

        @pl.when(ki == 0)
        def _():
            m_ref[...] = jnp.full_like(m_ref, -1e30)
            l_ref[...] = jnp.zeros_like(l_ref)
            acc_ref[...] = jnp.zeros_like(acc_ref)

        @pl.when(ki <= qi)
        def _():
            s = _causal_scores(q_ref[...], k_ref[...], qi, ki, tile)
            m_old = m_ref[...]
            m_new = jnp.maximum(m_old, jnp.max(s, axis=-1, keepdims=True))
            p = jnp.exp(s - m_new)
            alpha = jnp.exp(m_old - m_new)
            l_ref[...] = alpha * l_ref[...] + jnp.sum(p, axis=-1, keepdims=True)
            acc_ref[...] = alpha * acc_ref[...] + _nn(p.astype(BF16), v_ref[...])
            m_ref[...] = m_new

        @pl.when(ki == qi)
        def _():
            o_ref[...] = acc_ref[...] / l_ref[...]
            lse_ref[...] = jnp.broadcast_to(m_ref[...] + jnp.log(l_ref[...]), lse_ref.shape)

    q_spec = pl.BlockSpec((tile, SLOT), lambda h, qi, ki: (qi, h))
    k_spec = pl.BlockSpec((tile, SLOT), lambda h, qi, ki: (jnp.minimum(ki, qi), h))
    return pl.pallas_call(
        body, name="attn_fwd",
        grid=(N_HEADS, steps, steps),
        in_specs=[q_spec, k_spec, k_spec],
        out_specs=[q_spec, q_spec],
        out_shape=[jax.ShapeDtypeStruct((t, N_HEADS * SLOT), F32)] * 2,
        scratch_shapes=[pltpu.VMEM((tile, 1), F32), pltpu.VMEM((tile, 1), F32), pltpu.VMEM((tile, SLOT), F32)],
        compiler_params=pltpu.CompilerParams(dimension_semantics=("parallel", "parallel", "arbitrary")),
    )(q, k, v)


def _attn_probs(q, k, lse_ref, qi, ki, tile):
    return jnp.exp(_causal_scores(q, k, qi, ki, tile) - lse_ref[:, 0:1])


def _attn_bwd_q(q, k, v, do, lse, delta):
    t = q.shape[0]
    tile = min(ATT_T, t)
    steps = t // tile

    def body(q_ref, k_ref, v_ref, do_ref, lse_ref, delta_ref, dq_ref, acc_ref):
        qi, ki = pl.program_id(1), pl.program_id(2)

        @pl.when(ki == 0)
        def _():
            acc_ref[...] = jnp.zeros_like(acc_ref)

        @pl.when(ki <= qi)
        def _():
            p = _attn_probs(q_ref[...], k_ref[...], lse_ref, qi, ki, tile)
            dp = _nt(do_ref[...].astype(BF16), v_ref[...])
            ds = p * (dp - delta_ref[:, 0:1]) * ATT_SCALE
            acc_ref[...] += _nn(ds.astype(BF16), k_ref[...])

        @pl.when(ki == qi)
        def _():
            dq_ref[...] = acc_ref[...]

    q_spec = pl.BlockSpec((tile, SLOT), lambda h, qi, ki: (qi, h))
    k_spec = pl.BlockSpec((tile, SLOT), lambda h, qi, ki: (jnp.minimum(ki, qi), h))
    return pl.pallas_call(
        body, name="attn_bwd_q",
        grid=(N_HEADS, steps, steps),
        in_specs=[q_spec, k_spec, k_spec, q_spec, q_spec, q_spec],
        out_specs=q_spec,
        out_shape=jax.ShapeDtypeStruct((t, N_HEADS * SLOT), F32),
        scratch_shapes=[pltpu.VMEM((tile, SLOT), F32)],
        compiler_params=pltpu.CompilerParams(dimension_semantics=("parallel", "parallel", "arbitrary")),
    )(q, k, v, do, lse, delta)


def _attn_bwd_kv(q, k, v, do, lse, delta):
    t = q.shape[0]
    tile = min(ATT_T, t)
    steps = t // tile

    def body(q_ref, k_ref, v_ref, do_ref, lse_ref, delta_ref, dk_ref, dv_ref, dk_acc, dv_acc):
        ki, qi = pl.program_id(1), pl.program_id(2)

        @pl.when(qi == 0)
        def _():
            dk_acc[...] = jnp.zeros_like(dk_acc)
            dv_acc[...] = jnp.zeros_like(dv_acc)

        @pl.when(qi >= ki)
        def _():
            p = _attn_probs(q_ref[...], k_ref[...], lse_ref, qi, ki, tile)
            do_b = do_ref[...].astype(BF16)
            dv_acc[...] += _tn(p.astype(BF16), do_b)
            dp = _nt(do_b, v_ref[...])
            ds = p * (dp - delta_ref[:, 0:1]) * ATT_SCALE
            dk_acc[...] += _tn(ds.astype(BF16), q_ref[...])

        @pl.when(qi == steps - 1)
        def _():
            dk_ref[...] = dk_acc[...]
            dv_ref[...] = dv_acc[...]

    q_spec = pl.BlockSpec((tile, SLOT), lambda h, ki, qi: (jnp.maximum(qi, ki), h))
    k_spec = pl.BlockSpec((tile, SLOT), lambda h, ki, qi: (ki, h))
    return pl.pallas_call(
        body, name="attn_bwd_kv",
        grid=(N_HEADS, steps, steps),
        in_specs=[q_spec, k_spec, k_spec, q_spec, q_spec, q_spec],
        out_specs=[k_spec, k_spec],
        out_shape=[jax.ShapeDtypeStruct((t, N_HEADS * SLOT), F32)] * 2,
        scratch_shapes=[pltpu.VMEM((tile, SLOT), F32), pltpu.VMEM((tile, SLOT), F32)],
        compiler_params=pltpu.CompilerParams(dimension_semantics=("parallel", "parallel", "arbitrary")),
    )(q, k, v, do, lse, delta)


def _shift_down(x, s):
    if s == 0:
        return x
    row = lax.broadcasted_iota(jnp.int32, x.shape, 0)
    return jnp.where(row >= s, pltpu.roll(x, s, 0), 0.0)


def _shift_up(x, s):
    if s == 0:
        return x
    n = x.shape[0]
    row = lax.broadcasted_iota(jnp.int32, x.shape, 0)
    return jnp.where(row < n - s, pltpu.roll(x, n - s, 0), 0.0)


def _l2norm(x):
    return x * lax.rsqrt(jnp.sum(x * x, axis=-1, keepdims=True) + EPS)


def _conv_pre(x, w):
    y = w[GDN_CONV - 1:GDN_CONV, :] * x
    for s in range(1, GDN_CONV):
        y = y + w[GDN_CONV - 1 - s:GDN_CONV - s, :] * _shift_down(x, s)
    return y


def _gdn_conv_fwd(x, w):
    t, width = x.shape

    def body(x_ref, w_ref, o_ref):
        act = _silu(_conv_pre(x_ref[...], w_ref[...]))
        normed = pl.program_id(0) < 2 * N_HEADS
        o_ref[...] = jnp.where(normed, _l2norm(act), act)

    return pl.pallas_call(
        body, name="gdn_conv_fwd",
        grid=(width // SLOT,),
        in_specs=[pl.BlockSpec((t, SLOT), lambda j: (0, j)), pl.BlockSpec((GDN_CONV, SLOT), lambda j: (0, j))],
        out_specs=pl.BlockSpec((t, SLOT), lambda j: (0, j)),
        out_shape=jax.ShapeDtypeStruct((t, width), F32),
        compiler_params=pltpu.CompilerParams(dimension_semantics=("parallel",)),
    )(x, w)


def _gdn_conv_bwd(x, w, dout):
    t, width = x.shape

    def body(x_ref, w_ref, do_ref, dx_ref, dw_ref):
        xv, wv = x_ref[...], w_ref[...]
        y = _conv_pre(xv, wv)
        sig = _sigmoid(y)
        act = y * sig
        _, pull = jax.vjp(_l2norm, act)
        normed = pl.program_id(0) < 2 * N_HEADS
        dact = jnp.where(normed, pull(do_ref[0])[0], do_ref[0])
        dy = dact * (sig * (1.0 + y * (1.0 - sig)))
        dx = wv[GDN_CONV - 1:GDN_CONV, :] * dy
        for s in range(1, GDN_CONV):
            dx = dx + wv[GDN_CONV - 1 - s:GDN_CONV - s, :] * _shift_up(dy, s)
        dx_ref[...] = dx.astype(BF16)
        for s in range(GDN_CONV):
            dw_ref[GDN_CONV - 1 - s:GDN_CONV - s, :] = jnp.sum(dy * _shift_down(xv, s), axis=0, keepdims=True)

    col = pl.BlockSpec((t, SLOT), lambda j: (0, j))
    tap = pl.BlockSpec((GDN_CONV, SLOT), lambda j: (0, j))
    return pl.pallas_call(
        body, name="gdn_conv_bwd",
        grid=(width // SLOT,),
        in_specs=[col, tap, pl.BlockSpec((1, t, SLOT), lambda j: (j // N_HEADS, 0, j % N_HEADS))],
        out_specs=[col, tap],
        out_shape=[jax.ShapeDtypeStruct((t, width), BF16), jax.ShapeDtypeStruct((GDN_CONV, width), F32)],
        compiler_params=pltpu.CompilerParams(dimension_semantics=("parallel",)),
    )(x, w, dout)


def _softplus(x):
    e = jnp.exp(-jnp.abs(x))
    u = 1.0 + e
    log1p = jnp.where(u == 1.0, e, jnp.log(u) * e / jnp.where(u == 1.0, 1.0, u - 1.0))
    return jnp.maximum(x, 0.0) + log1p


def _gates_fwd(ab, a_log, dt_bias):
    def fn(rows, consts):
        (abv,), (alog, dtb) = rows, consts
        g = -jnp.exp(alog) * _softplus(abv + dtb)
        beta = _sigmoid(abv)
        shape = (abv.shape[0], SLOT)
        g_slots = [jnp.broadcast_to(g[:, h:h + 1], shape) for h in range(N_HEADS)]
        b_slots = [jnp.broadcast_to(beta[:, N_HEADS + h:N_HEADS + h + 1], shape) for h in range(N_HEADS)]
        return [jnp.concatenate(g_slots, axis=1), jnp.concatenate(b_slots, axis=1)], []

    width = N_HEADS * SLOT
    return _rowwise("gdn_gates_fwd", fn, [ab], [a_log, dt_bias], [(width, F32), (width, F32)])


def _gates_bwd(ab, a_log, dt_bias, dg, dbeta):
    def fn(rows, consts):
        (abv, dgv, dbv), (alog, dtb) = rows, consts
        lane = lax.broadcasted_iota(jnp.int32, abv.shape, 1)
        dg_tok = jnp.zeros_like(abv)
        db_tok = jnp.zeros_like(abv)
        for h in range(N_HEADS):
            dg_tok = dg_tok + jnp.where(lane == h, jnp.sum(dgv[:, h * SLOT:(h + 1) * SLOT], axis=1, keepdims=True), 0.0)
            db_tok = db_tok + jnp.where(lane == N_HEADS + h, jnp.sum(dbv[:, h * SLOT:(h + 1) * SLOT], axis=1, keepdims=True), 0.0)
        xa = abv + dtb
        g = -jnp.exp(alog) * _softplus(xa)
        da = dg_tok * (-jnp.exp(alog)) * _sigmoid(xa)
        beta = _sigmoid(abv)
        dab = jnp.where(lane < N_HEADS, da, db_tok * beta * (1.0 - beta))
        dab = jnp.where(lane < 2 * N_HEADS, dab, 0.0)
        d_alog = jnp.sum(jnp.where(lane < N_HEADS, dg_tok * g, 0.0), axis=0, keepdims=True)
        d_dtb = jnp.sum(jnp.where(lane < N_HEADS, da, 0.0), axis=0, keepdims=True)
        return [dab], [d_alog, d_dtb]

    return _rowwise("gdn_gates_bwd", fn, [ab, dg, dbeta], [a_log, dt_bias], [(SLOT, F32)], sums=[SLOT, SLOT])


ROPE_HALF = MLA_ROPE // 2


def _rope_tables(positions):
    freqs = ROPE_THETA ** (-jnp.arange(ROPE_HALF, dtype=F32) / ROPE_HALF)
    ang = positions.astype(F32)[:, None] * freqs
    cos, sin = jnp.cos(ang), jnp.sin(ang)
    t = positions.shape[0]
    ones, zeros = jnp.ones((t, MLA_NOPE), F32), jnp.zeros((t, MLA_NOPE), F32)
    tail = jnp.zeros((t, SLOT - MLA_NOPE - MLA_ROPE), F32)
    half0 = jnp.zeros((t, ROPE_HALF), F32)
    same = jnp.concatenate([ones, cos, cos, tail], axis=1)
    from_low = jnp.concatenate([zeros, half0, sin, tail], axis=1)
    from_high = jnp.concatenate([zeros, -sin, half0, tail], axis=1)
    return same, from_low, from_high


def _rope(x, tabs):
    same, from_low, from_high = tabs
    width = x.shape[1]
    return x * same + pltpu.roll(x, ROPE_HALF, 1) * from_low + pltpu.roll(x, width - ROPE_HALF, 1) * from_high


def _rope_transposed(dy, tabs):
    same, from_low, from_high = tabs
    width = dy.shape[1]
    return dy * same + pltpu.roll(dy * from_low, width - ROPE_HALF, 1) + pltpu.roll(dy * from_high, ROPE_HALF, 1)


def _tile_slots(tab):
    return jnp.concatenate([tab] * N_HEADS, axis=1)


A_WIDTH = MLA_Q_RANK + MLA_KV_RANK + 2 * SLOT
A_KPE = MLA_Q_RANK + MLA_KV_RANK
A_AB = A_KPE + SLOT
WIDE = N_HEADS * SLOT


def _mla_pre_fwd(proj_a, tabs, g_q, g_kv):
    def fn(rows, consts):
        pa, *tb = rows
        gq, gkv = consts
        return [_rms(pa[:, :MLA_Q_RANK], gq, MLA_Q_RANK), _rms(pa[:, MLA_Q_RANK:A_KPE], gkv, MLA_KV_RANK),
                _rope(pa[:, A_KPE:A_AB], tb)], []

    return _rowwise("mla_pre_fwd", fn, [proj_a, *tabs], [g_q, g_kv], [(MLA_Q_RANK, BF16), (MLA_KV_RANK, BF16), (SLOT, F32)])


def _mla_pre_bwd(proj_a, tabs, g_q, g_kv, dcqn, dckvn, dkpe, dab):
    def fn(rows, consts):
        pa, t0, t1, t2, dq, dkv, dk, da = rows
        gq, gkv = consts
        _, pull_q = jax.vjp(lambda x, g: _rms(x, g, MLA_Q_RANK), pa[:, :MLA_Q_RANK], gq)
        _, pull_kv = jax.vjp(lambda x, g: _rms(x, g, MLA_KV_RANK), pa[:, MLA_Q_RANK:A_KPE], gkv)
        dcq, dgq = pull_q(dq)
        dckv, dgkv = pull_kv(dkv)
        return [jnp.concatenate([dcq, dckv, _rope_transposed(dk, (t0, t1, t2)), da], axis=1)], [dgq, dgkv]

    return _rowwise("mla_pre_bwd", fn, [proj_a, *tabs, dcqn, dckvn, dkpe, dab], [g_q, g_kv], [(A_WIDTH, BF16)],
                    sums=[MLA_Q_RANK, MLA_KV_RANK])


def _mla_qkv_fwd(q_p, kv_p, kpe, tabs):
    def fn(rows, consts):
        qv, kvv, kp, *tb = rows
        q = _rope(qv, [_tile_slots(x) for x in tb])
        k = kvv[:, :WIDE] + _tile_slots(kp)
        return [q, k, kvv[:, WIDE:]], []

    return _rowwise("mla_qkv_fwd", fn, [q_p, kv_p, kpe, *tabs], [], [(WIDE, BF16)] * 3)


def _mla_qkv_bwd(dq, dk, dv, tabs):
    def fn(rows, consts):
        dqv, dkv, dvv, *tb = rows
        dkpe = dkv[:, :SLOT]
        for h in range(1, N_HEADS):
            dkpe = dkpe + dkv[:, h * SLOT:(h + 1) * SLOT]
        return [_rope_transposed(dqv, [_tile_slots(x) for x in tb]), jnp.concatenate([dkv, dvv], axis=1), dkpe], []

    return _rowwise("mla_qkv_bwd", fn, [dq, dk, dv, *tabs], [], [(WIDE, BF16), (2 * WIDE, BF16), (SLOT, F32)])


def _slot_sum(x):
    parts = [jnp.broadcast_to(jnp.sum(x[:, h * SLOT:(h + 1) * SLOT], axis=1, keepdims=True), (x.shape[0], SLOT))
             for h in range(N_HEADS)]
    return jnp.concatenate(parts, axis=1)


def _mix_join(o_mla, o_gdn, gate, g_mla, g_gdn):
    mla = _rms(o_mla, g_mla, N_HEADS * MLA_V)
    gdn = o_gdn * lax.rsqrt(_slot_sum(o_gdn * o_gdn) * (1.0 / GDN_D) + EPS) * g_gdn * _silu(gate)
    return mla, gdn


def _mix_join_fwd(o_mla, o_gdn, gate, g_mla, g_gdn):
    def fn(rows, consts):
        return [jnp.concatenate(_mix_join(*rows, *consts), axis=1)], []

    return _rowwise("mix_join_fwd", fn, [o_mla, o_gdn, gate], [g_mla, g_gdn], [(2 * WIDE, BF16)])


def _mix_join_bwd(o_mla, o_gdn, gate, g_mla, g_gdn, dcat):
    def fn(rows, consts):
        om, og, gt, dc = rows
        gm, gg = consts
        _, pull = jax.vjp(lambda x, g: _rms(x, g, N_HEADS * MLA_V), om, gm)
        dom, dgm = pull(dc[:, :WIDE])
        dy = dc[:, WIDE:]
        r = lax.rsqrt(_slot_sum(og * og) * (1.0 / GDN_D) + EPS)
        sig = _sigmoid(gt)
        normed = og * r
        dn = dy * gg * (gt * sig)
        dog = r * dn - normed * (r * r) * _slot_sum(dn * og) * (1.0 / GDN_D)
        dgt = dy * normed * gg * (sig * (1.0 + gt * (1.0 - sig)))
        dgg = jnp.sum(dy * normed * (gt * sig), axis=0, keepdims=True)
        return [dom, _slot_sum(dom * om), dog, dgt], [dgm, dgg]

    return _rowwise("mix_join_bwd", fn, [o_mla, o_gdn, gate, dcat], [g_mla, g_gdn],
                    [(WIDE, F32), (WIDE, F32), (WIDE, F32), (WIDE, BF16)], sums=[WIDE, WIDE])


def _norm_residual_fwd(name, x, h, g, out_dtypes):
    dm = x.shape[1]

    def fn(rows, consts):
        y = rows[0] + _rms(rows[1], consts[0], dm)
        return [y] + [_rms(y, gg, dm) for gg in consts[1:]], []

    return _rowwise(name, fn, [x, h], list(g), [(dm, dt) for dt in out_dtypes])


def _norm_residual_bwd(name, h, g, dy):
    dm = h.shape[1]

    def fn(rows, consts):
        _, pull = jax.vjp(lambda hv, gv: _rms(hv, gv, dm), rows[0], consts[0])
        dh, dg = pull(rows[1])
        return [dh], [dg]

    return _rowwise(name, fn, [h, dy], [g], [(dm, BF16)], sums=[dm])


def _norm_bwd_add(name, x, g, dns, dy):
    dm = x.shape[1]

    def fn(rows, consts):
        xv, dyv, *parts = rows
        dn = parts[0]
        for p in parts[1:]:
            dn = dn + p
        _, pull = jax.vjp(lambda a, gv: _rms(a, gv, dm), xv, consts[0])
        dx, dg = pull(dn)
        return [dyv + dx], [dg]

    return _rowwise(name, fn, [x, dy, *dns], [g], [(dm, F32)], sums=[dm])


def _loss_fwd(y, target):
    dm = y.shape[1]

    def fn(rows, consts):
        err = rows[0] - rows[1]
        sq = err * err
        lanes = sq[:, :SLOT]
        for j in range(1, dm // SLOT):
            lanes = lanes + sq[:, j * SLOT:(j + 1) * SLOT]
        return [err * (1.0 / dm)], [jnp.sum(lanes, axis=0, keepdims=True) * (0.5 / dm)]

    return _rowwise("loss", fn, [y, target], [], [(dm, F32)], sums=[SLOT])


def _norm_fwd(name, x, g):
    dm = x.shape[1]
    return _rowwise(name, lambda rows, consts: ([_rms(rows[0], consts[0], dm)], []), [x], [g], [(dm, BF16)])[0]


W_IN_CUTS = (0, 256, 384, 416, 1952, 1960, 1968, 2480)


def _heads_out(w, per_head, axis=-1):
    w = jnp.moveaxis(w, axis, -1)
    lead = w.shape[:-1]
    w = w.reshape(lead + (w.shape[-1] // per_head, per_head))
    w = jnp.pad(w, [(0, 0)] * len(lead) + [(0, 0), (0, SLOT - per_head)])
    return jnp.moveaxis(w.reshape(lead + (-1,)), -1, axis)


def _heads_in(w, per_head, axis=-1):
    w = jnp.moveaxis(w, axis, -1)
    lead = w.shape[:-1]
    w = w.reshape(lead + (w.shape[-1] // SLOT, SLOT))[..., :per_head]
    return jnp.moveaxis(w.reshape(lead + (-1,)), -1, axis)


def _pad_lanes(v, lo, width=SLOT):
    return jnp.pad(v, [(0, 0)] * (v.ndim - 1) + [(lo, width - lo - v.shape[-1])])


def _layout_weights(w):
    c = W_IN_CUTS
    w_in = w["w_in"]
    p = {}
    p["w_a"] = jnp.concatenate([w_in[:, c[0]:c[2]], _pad_lanes(w_in[:, c[2]:c[3]], MLA_NOPE),
                                _pad_lanes(w_in[:, c[4]:c[6]], 0)], axis=1)
    p["w_qkv"] = _heads_out(w_in[:, c[3]:c[4]], GDN_D)
    p["w_gate"] = _heads_out(w_in[:, c[6]:c[7]], GDN_D)
    p["w_uq"] = _heads_out(w["mla_w_uq"], MLA_NOPE + MLA_ROPE)
    ukv = w["mla_w_ukv"].reshape(MLA_KV_RANK, N_HEADS, MLA_NOPE + MLA_V)
    p["w_kv"] = jnp.concatenate([_heads_out(ukv[:, :, :MLA_NOPE].reshape(MLA_KV_RANK, -1), MLA_NOPE),
                                 _heads_out(ukv[:, :, MLA_NOPE:].reshape(MLA_KV_RANK, -1), MLA_V)], axis=1)
    p["w_out"] = _heads_out(w["w_out"], GDN_D, axis=0)
    p["conv"] = _heads_out(w["gdn_conv_w"], GDN_D)
    p["g_mla_out"] = _heads_out(w["mla_out_g"], MLA_V)
    p["g_gdn"] = jnp.tile(_pad_lanes(w["gdn_norm_g"], 0), (1, N_HEADS))
    p["a_log"] = _pad_lanes(w["gdn_a_log"], 0)
    p["dt_bias"] = _pad_lanes(w["gdn_dt_bias"], 0)
    return p


def _unlayout_grads(d):
    c = W_IN_CUTS
    g = {}
    da = d["w_a"]
    kpe0 = A_KPE + MLA_NOPE
    g["w_in"] = jnp.concatenate([da[:, :A_KPE], da[:, kpe0:kpe0 + MLA_ROPE], _heads_in(d["w_qkv"], GDN_D),
                                 da[:, A_AB:A_AB + 2 * N_HEADS], _heads_in(d["w_gate"], GDN_D)], axis=1)
    assert g["w_in"].shape[1] == c[-1]
    g["mla_w_uq"] = _heads_in(d["w_uq"], MLA_NOPE + MLA_ROPE)
    dk = _heads_in(d["w_kv"][:, :WIDE], MLA_NOPE).reshape(MLA_KV_RANK, N_HEADS, MLA_NOPE)
    dv = _heads_in(d["w_kv"][:, WIDE:], MLA_V).reshape(MLA_KV_RANK, N_HEADS, MLA_V)
    g["mla_w_ukv"] = jnp.concatenate([dk, dv], axis=2).reshape(MLA_KV_RANK, -1)
    g["w_out"] = _heads_in(d["w_out"], GDN_D, axis=0)
    g["gdn_conv_w"] = _heads_in(d["conv"], GDN_D)
    g["mla_out_g"] = _heads_in(d["g_mla_out"], MLA_V)
    g["gdn_norm_g"] = jnp.sum(d["g_gdn"].reshape(N_HEADS, SLOT), axis=0, keepdims=True)[:, :GDN_D]
    g["gdn_a_log"] = d["a_log"][:, :N_HEADS]
    g["gdn_dt_bias"] = d["dt_bias"][:, :N_HEADS]
    return g


def _weight_grad(name, acts, cots, tm=1024, tn=1408, tk=512):
    return _matmul(name, acts, cots, "tn", tm=tm, tn=tn, tk=tk)


def _local_step(x, positions, target, w):
    p = _layout_weights(w)
    tabs = _rope_tables(positions)

    h1, x1 = _ffn_fwd("ffn1_fwd", x, w["ffn1_pre_g"], w["ffn1_w_gate"], w["ffn1_w_up"], w["ffn1_w_down"], w["ffn1_post_g"])
    hn = _norm_fwd("mix_pre_norm", x1, w["mix_pre_g"])
    proj_a = _matmul("proj_a", hn, p["w_a"], "nn")
    proj_qkv = _matmul("proj_qkv", hn, p["w_qkv"], "nn")
    proj_gate = _matmul("proj_gate", hn, p["w_gate"], "nn")
    cqn, ckvn, kpe = _mla_pre_fwd(proj_a, tabs, w["mla_q_norm_g"], w["mla_kv_norm_g"])
    q_p = _matmul("mla_q", cqn, p["w_uq"], "nn")
    kv_p = _matmul("mla_kv", ckvn, p["w_kv"], "nn")
    q, k, v = _mla_qkv_fwd(q_p, kv_p, kpe, tabs)
    o_mla, lse = _attn_fwd(q, k, v)
    ab = (proj_a, SLOT, A_AB // SLOT)
    qkv_n = _gdn_conv_fwd(proj_qkv, p["conv"])
    gb, bb = _gates_fwd(ab, p["a_log"], p["dt_bias"])
    o_gdn, keep = _gdn_fwd(qkv_n, gb, bb)
    cat = _mix_join_fwd(o_mla, o_gdn, proj_gate, p["g_mla_out"], p["g_gdn"])[0]
    mixed = _matmul("mix_out", cat, p["w_out"], "nn")
    x2 = _norm_residual_fwd("mix_post", x1, mixed, [w["mix_post_g"]], [F32])[0]
    h2, y = _ffn_fwd("ffn2_fwd", x2, w["ffn2_pre_g"], w["ffn2_w_gate"], w["ffn2_w_up"], w["ffn2_w_down"], w["ffn2_post_g"])
    dy, loss_lanes = _loss_fwd(y, target)

    g = {}
    dx2, xn2, dh2, a2, dhg2, dhu2, g["ffn2_pre_g"], g["ffn2_post_g"] = _ffn_bwd(
        "ffn2_bwd", x2, h2, dy, w["ffn2_pre_g"], w["ffn2_w_gate"], w["ffn2_w_up"], w["ffn2_w_down"], w["ffn2_post_g"])
    g["ffn2_w_gate"] = _weight_grad("ffn2_dw_gate", xn2, dhg2)
    g["ffn2_w_up"] = _weight_grad("ffn2_dw_up", xn2, dhu2)
    g["ffn2_w_down"] = _weight_grad("ffn2_dw_down", a2, dh2, tm=1408, tn=1024)
    dmixed, g["mix_post_g"] = _norm_residual_bwd("mix_post_bwd", mixed, w["mix_post_g"], dx2)
    dcat = _matmul("mix_out_dx", dmixed, p["w_out"], "nt")
    d = {}
    d["w_out"] = _weight_grad("mix_out_dw", cat, dmixed, tn=1024)
    do_mla, delta, do_gdn, dgate, d["g_mla_out"], d["g_gdn"] = _mix_join_bwd(o_mla, o_gdn, proj_gate, p["g_mla_out"], p["g_gdn"], dcat)
    dq = _attn_bwd_q(q, k, v, do_mla, lse, delta)
    dk, dv = _attn_bwd_kv(q, k, v, do_mla, lse, delta)
    dq_p, dkv_p, dkpe = _mla_qkv_bwd(dq, dk, dv, tabs)
    dcqn = _matmul("mla_q_dx", dq_p, p["w_uq"], "nt")
    d["w_uq"] = _weight_grad("mla_q_dw", cqn, dq_p, tn=1024)
    dckvn = _matmul("mla_kv_dx", dkv_p, p["w_kv"], "nt")
    d["w_kv"] = _weight_grad("mla_kv_dw", ckvn, dkv_p, tn=1024)
    dqkv_n, dgb, dbb = _gdn_bwd(qkv_n, gb, bb, keep, do_gdn)
    dab, d["a_log"], d["dt_bias"] = _gates_bwd(ab, p["a_log"], p["dt_bias"], dgb, dbb)
    dproj_qkv, d["conv"] = _gdn_conv_bwd(proj_qkv, p["conv"], dqkv_n)
    dproj_a, g["mla_q_norm_g"], g["mla_kv_norm_g"] = _mla_pre_bwd(
        proj_a, tabs, w["mla_q_norm_g"], w["mla_kv_norm_g"], dcqn, dckvn, dkpe, dab)
    dhn = [_matmul("proj_a_dx", dproj_a, p["w_a"], "nt"), _matmul("proj_qkv_dx", dproj_qkv, p["w_qkv"], "nt"),
           _matmul("proj_gate_dx", dgate, p["w_gate"], "nt")]
    d["w_a"] = _weight_grad("proj_a_dw", hn, dproj_a, tn=640)
    d["w_qkv"] = _weight_grad("proj_qkv_dw", hn, dproj_qkv, tn=1024)
    d["w_gate"] = _weight_grad("proj_gate_dw", hn, dgate, tn=1024)
    dx1, g["mix_pre_g"] = _norm_bwd_add("mix_pre_bwd", x1, w["mix_pre_g"], dhn, dx2)
    dx, xn1, dh1, a1, dhg1, dhu1, g["ffn1_pre_g"], g["ffn1_post_g"] = _ffn_bwd(
        "ffn1_bwd", x, h1, dx1, w["ffn1_pre_g"], w["ffn1_w_gate"], w["ffn1_w_up"], w["ffn1_w_down"], w["ffn1_post_g"])
    g["ffn1_w_gate"] = _weight_grad("ffn1_dw_gate", xn1, dhg1)
    g["ffn1_w_up"] = _weight_grad("ffn1_dw_up", xn1, dhu1)
    g["ffn1_w_down"] = _weight_grad("ffn1_dw_down", a1, dh1, tm=1408, tn=1024)
    g.update(_unlayout_grads(d))
    return loss_lanes, dx, g


MESH_AXES = ("x", "y", "c")


def _peer(r):
    x, y, c = (lax.axis_index(a) for a in MESH_AXES)
    return (1 - x if r & 4 else x, 1 - y if r & 2 else y, 1 - c if r & 1 else c)


def _block_of(dev):
    x, y, c = dev
    return 4 * x + 2 * y + c


def _exchange(name, buf, gather):
    shape = buf.shape[-2:]

    def body(x_ref, out_ref, send_sems, recv_sems, local_sem):
        me = _block_of(_peer(0))
        mine = x_ref if gather else x_ref.at[me]
        local = pltpu.make_async_copy(mine, out_ref.at[me], local_sem)
        local.start()

        def copy(r):
            src = x_ref if gather else x_ref.at[_block_of(_peer(r))]
            return pltpu.make_async_remote_copy(
                src_ref=src, dst_ref=out_ref.at[me], send_sem=send_sems.at[r - 1], recv_sem=recv_sems.at[r - 1],
                device_id=_peer(r), device_id_type=pl.DeviceIdType.MESH)

        def arrival(r):
            return pltpu.make_async_remote_copy(
                src_ref=mine, dst_ref=out_ref.at[_block_of(_peer(r))], send_sem=send_sems.at[r - 1],
                recv_sem=recv_sems.at[r - 1], device_id=_peer(r), device_id_type=pl.DeviceIdType.MESH)

        sends = [copy(r) for r in range(1, N_DEV)]
        for cp in sends:
            cp.start()
        for r in range(1, N_DEV):
            arrival(r).wait_recv()
        for cp in sends:
            cp.wait_send()
        local.wait()

    return pl.pallas_call(
        body, name=name,
        in_specs=[pl.BlockSpec(memory_space=pl.ANY)],
        out_specs=pl.BlockSpec(memory_space=pl.ANY),
        out_shape=jax.ShapeDtypeStruct((N_DEV,) + shape, buf.dtype),
        scratch_shapes=[pltpu.SemaphoreType.DMA((N_DEV - 1,)), pltpu.SemaphoreType.DMA((N_DEV - 1,)), pltpu.SemaphoreType.DMA(())],
    )(buf)


def _sum_blocks(name, blocks, tm):
    _, rows, width = blocks.shape
    assert rows % tm == 0

    def body(x_ref, o_ref):
        acc = x_ref[0].astype(F32)
        for d in range(1, N_DEV):
            acc = acc + x_ref[d].astype(F32)
        o_ref[...] = acc

    return pl.pallas_call(
        body, name=name,
        grid=(rows // tm,),
        in_specs=[pl.BlockSpec((N_DEV, tm, width), lambda i: (0, i, 0))],
        out_specs=pl.BlockSpec((tm, width), lambda i: (i, 0)),
        out_shape=jax.ShapeDtypeStruct((rows, width), F32),
        compiler_params=pltpu.CompilerParams(dimension_semantics=("parallel",)),
    )(blocks)


def _all_reduce_small(name, vec):
    rows, width = vec.shape

    def body(x_ref, o_ref, all_ref, send_sems, recv_sems):
        me = _block_of(_peer(0))
        all_ref[me] = x_ref[...]

        def copy(r, block):
            return pltpu.make_async_remote_copy(
                src_ref=x_ref, dst_ref=all_ref.at[block], send_sem=send_sems.at[r - 1], recv_sem=recv_sems.at[r - 1],
                device_id=_peer(r), device_id_type=pl.DeviceIdType.MESH)

        sends = [copy(r, me) for r in range(1, N_DEV)]
        for cp in sends:
            cp.start()
        for r in range(1, N_DEV):
            copy(r, _block_of(_peer(r))).wait_recv()
        for cp in sends:
            cp.wait_send()
        acc = all_ref[0]
        for d in range(1, N_DEV):
            acc = acc + all_ref[d]
        o_ref[...] = acc

    return pl.pallas_call(
        body, name=name,
        in_specs=[pl.BlockSpec(memory_space=pltpu.VMEM)],
        out_specs=pl.BlockSpec(memory_space=pltpu.VMEM),
        out_shape=jax.ShapeDtypeStruct((rows, width), F32),
        scratch_shapes=[pltpu.VMEM((N_DEV, rows, width), F32), pltpu.SemaphoreType.DMA((N_DEV - 1,)), pltpu.SemaphoreType.DMA((N_DEV - 1,))],
    )(vec)


def _adamw(name, w, g, m, v, tm):
    def fn(rows, consts):
        wv, gv, mv, vv = rows
        m2 = ADAM_B1 * mv + (1.0 - ADAM_B1) * gv
        v2 = ADAM_B2 * vv + (1.0 - ADAM_B2) * jnp.square(gv)
        m_hat = m2 / (1.0 - ADAM_B1 ** ADAM_STEP)
        v_hat = v2 / (1.0 - ADAM_B2 ** ADAM_STEP)
        return [-ADAM_LR * (m_hat / (jnp.sqrt(v_hat) + ADAM_EPS) + ADAM_WD * wv), m2, v2], []

    width = w.shape[1]
    return _rowwise(name, fn, [w, g, m, v], [], [(width, F32)] * 3, tm=tm)


ROW = 1024
BIG = {
    "ffn1_w_gate": ((D_MODEL, D_FF), 1), "ffn1_w_up": ((D_MODEL, D_FF), 1), "ffn1_w_down": ((D_FF, D_MODEL), 0),
    "w_in": ((D_MODEL, W_IN_CUTS[-1]), 1), "mla_w_uq": ((MLA_Q_RANK, N_HEADS * (MLA_NOPE + MLA_ROPE)), 1),
    "mla_w_ukv": ((MLA_KV_RANK, N_HEADS * (MLA_NOPE + MLA_V)), 1), "w_out": ((2 * N_HEADS * GDN_D, D_MODEL), 0),
    "ffn2_w_gate": ((D_MODEL, D_FF), 1), "ffn2_w_up": ((D_MODEL, D_FF), 1), "ffn2_w_down": ((D_FF, D_MODEL), 0),
}
PART_ROWS = 16
BIG_TM = 240
BIG_ROWS = 11 * BIG_TM
SMALL = {
    "ffn1_pre_g": (1024, 1024), "ffn1_post_g": (1024, 1024), "mix_pre_g": (1024, 1024), "mla_q_norm_g": (256, 256),
    "mla_kv_norm_g": (128, 128), "mla_out_g": (512, 512), "gdn_a_log": (8, 128), "gdn_dt_bias": (8, 128),
    "gdn_norm_g": (64, 128), "mix_post_g": (1024, 1024), "ffn2_pre_g": (1024, 1024), "ffn2_post_g": (1024, 1024),
}
SMALL_LANES = sum(r for _, r in SMALL.values())
CONV_SHAPE = (GDN_CONV, 3 * N_HEADS * GDN_D)
CONV_SHARD = (GDN_CONV, CONV_SHAPE[1] // N_DEV)
CONV_LANES = CONV_SHAPE[0] * CONV_SHAPE[1]
SMALL_ROWS = 8
REDUCE_ROWS = 16


def _shard_shape(name):
    shape, axis = BIG[name]
    return tuple(s // N_DEV if i == axis else s for i, s in enumerate(shape))


def _part_rows(name):
    shape = _shard_shape(name)
    rows = shape[0] * shape[1] // ROW
    return rows, -(-rows // PART_ROWS) * PART_ROWS


def _pad_rows(a, rows, axis):
    pad = [(0, 0)] * a.ndim
    pad[axis] = (0, rows - a.shape[axis])
    return jnp.pad(a, pad)


def _pack_big(shards):
    parts = [_pad_rows(shards[n].reshape(-1, ROW), _part_rows(n)[1], 0) for n in BIG]
    rows = sum(p.shape[0] for p in parts)
    return jnp.concatenate(parts + [jnp.zeros((BIG_ROWS - rows, ROW), parts[0].dtype)], axis=0)


def _unpack_big(buf):
    out, at = {}, 0
    for n in BIG:
        rows, taken = _part_rows(n)
        out[n] = buf[at:at + rows].reshape(_shard_shape(n))
        at += taken
    return out


def _join_big(gathered):
    out, at = {}, 0
    for n, (shape, axis) in BIG.items():
        rows, taken = _part_rows(n)
        blocks = gathered[:, at:at + rows].reshape((N_DEV,) + _shard_shape(n))
        out[n] = blocks.reshape(shape) if axis == 0 else blocks.transpose(1, 0, 2).reshape(shape)
        at += taken
    return out


def _split_big(full, dtype):
    parts = []
    for n, (shape, axis) in BIG.items():
        sh = _shard_shape(n)
        blocks = full[n].astype(dtype).reshape((N_DEV,) + sh) if axis == 0 else full[n].astype(dtype).reshape(
            sh[0], N_DEV, sh[1]).transpose(1, 0, 2)
        parts.append(_pad_rows(blocks.reshape(N_DEV, -1, ROW), _part_rows(n)[1], 1))
    rows = sum(p.shape[1] for p in parts)
    return jnp.concatenate(parts + [jnp.zeros((N_DEV, BIG_ROWS - rows, ROW), dtype)], axis=1)


def _pack_small(vecs, conv, rows):
    parts = [_pad_lanes(vecs[n].reshape(1, -1), 0, r) for n, (_, r) in SMALL.items()]
    if conv is not None:
        parts.append(conv.reshape(1, -1))
    flat = jnp.concatenate(parts, axis=1)
    return _pad_lanes(flat, 0, rows * ROW).reshape(rows, ROW)


def _unpack_small(buf):
    flat = buf.reshape(1, -1)
    out, at = {}, 0
    for n, (w, r) in SMALL.items():
        out[n] = flat[:, at:at + w]
        at += r
    return out, flat[0, at:]


def kernel(x, positions, ffn1_pre_g, ffn1_w_gate, ffn1_w_up, ffn1_w_down, ffn1_post_g, mix_pre_g, w_in, mla_q_norm_g, mla_w_uq, mla_kv_norm_g, mla_w_ukv, mla_out_g, gdn_conv_w, gdn_a_log, gdn_dt_bias, gdn_norm_g, w_out, mix_post_g, ffn2_pre_g, ffn2_w_gate, ffn2_w_up, ffn2_w_down, ffn2_post_g, loss_target, m_ffn1_pre_g, m_ffn1_w_gate, m_ffn1_w_up, m_ffn1_w_down, m_ffn1_post_g, m_mix_pre_g, m_w_in, m_mla_q_norm_g, m_mla_w_uq, m_mla_kv_norm_g, m_mla_w_ukv, m_mla_out_g, m_gdn_conv_w, m_gdn_a_log, m_gdn_dt_bias, m_gdn_norm_g, m_w_out, m_mix_post_g, m_ffn2_pre_g, m_ffn2_w_gate, m_ffn2_w_up, m_ffn2_w_down, m_ffn2_post_g, v_ffn1_pre_g, v_ffn1_w_gate, v_ffn1_w_up, v_ffn1_w_down, v_ffn1_post_g, v_mix_pre_g, v_w_in, v_mla_q_norm_g, v_mla_w_uq, v_mla_kv_norm_g, v_mla_w_ukv, v_mla_out_g, v_gdn_conv_w, v_gdn_a_log, v_gdn_dt_bias, v_gdn_norm_g, v_w_out, v_mix_post_g, v_ffn2_pre_g, v_ffn2_w_gate, v_ffn2_w_up, v_ffn2_w_down, v_ffn2_post_g):
    given = dict(locals())
    names = list(BIG) + list(SMALL) + ["gdn_conv_w"]
    order = ["ffn1_pre_g", "ffn1_w_gate", "ffn1_w_up", "ffn1_w_down", "ffn1_post_g", "mix_pre_g", "w_in", "mla_q_norm_g",
             "mla_w_uq", "mla_kv_norm_g", "mla_w_ukv", "mla_out_g", "gdn_conv_w", "gdn_a_log", "gdn_dt_bias", "gdn_norm_g",
             "w_out", "mix_post_g", "ffn2_pre_g", "ffn2_w_gate", "ffn2_w_up", "ffn2_w_down", "ffn2_post_g"]
    assert sorted(names) == sorted(order)
    def drop_depth(a):
        return a[0] if a.ndim == 3 else a

    wts = {n: drop_depth(given[n]) for n in order}
    mom = {n: drop_depth(given["m_" + n]) for n in order}
    var = {n: drop_depth(given["v_" + n]) for n in order}
    me = _block_of(_peer(0))

    gathered = _exchange("gather_weights", _pack_big({n: wts[n].astype(BF16) for n in BIG}), gather=True)
    conv_at = lax.dynamic_update_slice(jnp.zeros((N_DEV, CONV_SHARD[0] * CONV_SHARD[1]), F32),
                                       wts["gdn_conv_w"].reshape(1, -1), (me, 0))
    conv_all = _all_reduce_small("gather_conv", _pad_lanes(conv_at.reshape(1, -1), 0, SMALL_ROWS * ROW).reshape(SMALL_ROWS, ROW))
    conv_full = conv_all.reshape(-1)[:CONV_LANES].reshape((N_DEV,) + CONV_SHARD).transpose(1, 0, 2).reshape(CONV_SHAPE)
    full = _join_big(gathered)
    full.update({n: wts[n] for n in SMALL})
    full["gdn_conv_w"] = conv_full

    loss_lanes, dx, grads = _local_step(x[0], positions[0], loss_target[0], full)
    loss = lax.psum(jnp.sum(loss_lanes), MESH_AXES)

    landed = _exchange("scatter_grads", _split_big(grads, BF16), gather=False)
    big_grad = _sum_blocks("sum_grads", landed, BIG_TM)
    small_sum = _all_reduce_small("reduce_small", _pack_small(grads, grads["gdn_conv_w"].reshape(-1), REDUCE_ROWS))
    small_grad, conv_grad_full = _unpack_small(small_sum)
    conv_grad = lax.dynamic_slice(conv_grad_full[:CONV_LANES].reshape(CONV_SHAPE), (0, me * CONV_SHARD[1]), CONV_SHARD)

    big = [_pack_big({n: s[n] for n in BIG}) for s in (wts, mom, var)]
    big_out = _adamw("adamw_big", big[0], big_grad, big[1], big[2], BIG_TM)
    small_g = dict(small_grad)
    small = [_pack_small(s, s["gdn_conv_w"].reshape(-1), SMALL_ROWS) for s in (wts, {**small_g, "gdn_conv_w": conv_grad}, mom, var)]
    small_out = _adamw("adamw_small", *small, SMALL_ROWS)

    outs = {"grad": {**_unpack_big(big_grad), **small_g, "gdn_conv_w": conv_grad}}
    for kind, b, s in zip(("delta", "new_m", "new_v"), big_out, small_out):
        vecs, conv = _unpack_small(s)
        outs[kind] = {**_unpack_big(b), **vecs, "gdn_conv_w": conv[:CONV_SHARD[0] * CONV_SHARD[1]].reshape(CONV_SHARD)}
    result = [loss, dx[None]]
    for kind in ("grad", "delta", "new_m", "new_v"):
        result += [outs[kind][n].reshape(given[n].shape) for n in order]
    return tuple(result)
```

```python
import functools

import jax
import jax.numpy as jnp
import numpy as np
from jax import lax
from jax.experimental import pallas as pl
from jax.experimental.pallas import tpu as pltpu

F32 = jnp.float32
BF16 = jnp.bfloat16
HI = lax.Precision.HIGH
EXACT = lax.Precision.HIGHEST

N_DEV = 8
D_MODEL = 1024
D_FF = 2816
N_HEADS = 8
SLOT = 128
MLA_Q_RANK = 256
MLA_KV_RANK = 128
MLA_NOPE = 64
MLA_ROPE = 32
MLA_V = 64
GDN_D = 64
GDN_CONV = 4
GDN_CHUNK = 64
ROPE_THETA = 10000.0
EPS = 1e-6
ADAM_LR, ADAM_B1, ADAM_B2, ADAM_EPS, ADAM_WD, ADAM_STEP = 0.001, 0.9, 0.999, 1e-08, 0.01, 10


def _dot(a, b, ca, cb, precision=None):
    lead = a.ndim - 2
    batch = tuple(range(lead))
    return lax.dot_general(a, b, (((lead + ca,), (lead + cb,)), (batch, batch)), precision=precision,
                           preferred_element_type=F32)


def _nn(a, b, precision=None):
    return _dot(a, b, 1, 0, precision)


def _nt(a, b, precision=None):
    return _dot(a, b, 1, 1, precision)


def _tn(a, b, precision=None):
    return _dot(a, b, 0, 0, precision)


def _sigmoid(x):
    return 1.0 / (1.0 + jnp.exp(-x))


def _silu(x):
    return x * _sigmoid(x)


def _rms(x, g, n):
    ms = jnp.sum(x * x, axis=-1, keepdims=True) * (1.0 / n)
    return x * lax.rsqrt(ms + EPS) * g


def _chunk_masks():
    c = GDN_CHUNK
    i = lax.broadcasted_iota(jnp.int32, (c, c), 0)
    j = lax.broadcasted_iota(jnp.int32, (c, c), 1)
    lower = i >= j
    strict = i > j
    eye = (i == j).astype(F32)
    blocks = []
    b = 1
    while b < c:
        same = (i // (2 * b)) == (j // (2 * b))
        blocks.append(same & ((i % (2 * b)) >= b) & ((j % (2 * b)) < b))
        b *= 2
    return lower, strict, eye, blocks


def _unit_lower_inverse(low, eye, blocks):
    t = jnp.broadcast_to(eye, low.shape)
    for m in blocks:
        lo = jnp.where(m, low, 0.0)
        t = t - _nn(t, _nn(lo, t, HI), HI)
    return t


@jax.custom_vjp
def _known_inverse(low, tinv):
    return tinv


def _known_inverse_fwd(low, tinv):
    return tinv, tinv


def _known_inverse_bwd(tinv, dt):
    return -_tn(tinv, _nt(dt, tinv, HI), HI), jnp.zeros_like(tinv)


_known_inverse.defvjp(_known_inverse_fwd, _known_inverse_bwd)


def _gdn_chunk(q, k, v, gb, bb, s, masks, tinv=None):
    lower, strict, eye, blocks = masks
    qs = q * (GDN_D ** -0.5)
    gc = _nn(jnp.broadcast_to(lower.astype(F32), gb.shape), gb, EXACT)
    gct = _nt(jnp.broadcast_to(eye, gb.shape), gc, EXACT)
    decay = jnp.exp(jnp.where(lower, gc - gct, -1e30))
    kb = k * bb
    low = jnp.where(strict, _nt(kb, k, HI) * decay, 0.0)
    tinv = _unit_lower_inverse(low, eye, blocks) if tinv is None else _known_inverse(low, tinv)
    eg = jnp.exp(gc)
    w = _nn(tinv, kb * eg, HI)
    u = _nn(tinv, v * bb, HI)
    attn = _nt(qs, k, HI) * decay
    g_end = jnp.sum(gb, axis=-2, keepdims=True)
    k_dec = k * jnp.exp(g_end - gc)
    v_new = u - _nn(w, s, HI)
    o = _nn(qs * eg, s, HI) + _nn(attn, v_new, HI)
    s_new = s * jnp.exp(g_end) + _tn(k_dec, v_new, HI)
    return o, s_new, tinv


GDN_GROUP = 8
GDN_GROUPS = N_HEADS // GDN_GROUP


def _group_heads(ref):
    return jnp.stack([ref[:, pl.ds(j * SLOT, GDN_D)] for j in range(GDN_GROUP)])


def _ungroup_heads(ref, val):
    pad = jnp.zeros((GDN_CHUNK, SLOT - GDN_D), F32)
    for j in range(GDN_GROUP):
        ref[:, pl.ds(j * SLOT, GDN_D)] = val[j]
        ref[:, pl.ds(j * SLOT + GDN_D, SLOT - GDN_D)] = pad


def _gdn_fwd(qkv, gb, bb):
    t = qkv.shape[0]
    n_chunks = t // GDN_CHUNK
    d = GDN_D

    def body(q_ref, k_ref, v_ref, g_ref, b_ref, o_ref, keep_ref, s_ref):
        @pl.when(pl.program_id(1) == 0)
        def _():
            s_ref[...] = jnp.zeros_like(s_ref)

        s = s_ref[...]
        keep_ref[:, 0, 0] = s
        o, s_new, tinv = _gdn_chunk(*[_group_heads(r) for r in (q_ref, k_ref, v_ref, g_ref, b_ref)], s, _chunk_masks())
        keep_ref[:, 0, 1] = tinv
        s_ref[...] = s_new
        _ungroup_heads(o_ref, o)

    def spec(kind=0):
        return pl.BlockSpec((GDN_CHUNK, GDN_GROUP * SLOT), lambda h, n: (n, kind * GDN_GROUPS + h))

    return pl.pallas_call(
        body, name="gdn_fwd",
        grid=(GDN_GROUPS, n_chunks),
        in_specs=[spec(0), spec(1), spec(2), spec(), spec()],
        out_specs=[spec(), pl.BlockSpec((GDN_GROUP, 1, 2, d, d), lambda h, n: (h, n, 0, 0, 0))],
        out_shape=[jax.ShapeDtypeStruct((t, N_HEADS * SLOT), F32), jax.ShapeDtypeStruct((N_HEADS, n_chunks, 2, d, d), F32)],
        scratch_shapes=[pltpu.VMEM((GDN_GROUP, d, d), F32)],
        compiler_params=pltpu.CompilerParams(dimension_semantics=("parallel", "arbitrary")),
    )(qkv, qkv, qkv, gb, bb)


def _gdn_bwd(qkv, gb, bb, keep, do):
    t = qkv.shape[0]
    n_chunks = t // GDN_CHUNK
    d = GDN_D

    def body(q_ref, k_ref, v_ref, g_ref, b_ref, keep_ref, do_ref, dqkv_ref, dg_ref, db_ref, ds_ref):
        @pl.when(pl.program_id(1) == 0)
        def _():
            ds_ref[...] = jnp.zeros_like(ds_ref)

        masks = _chunk_masks()
        tinv = keep_ref[:, 0, 1]
        _, pull = jax.vjp(lambda *a: _gdn_chunk(*a, masks, tinv)[:2],
                          *[_group_heads(r) for r in (q_ref, k_ref, v_ref, g_ref, b_ref)], keep_ref[:, 0, 0])
        dq, dk, dv, dg, db, ds = pull((_group_heads(do_ref), ds_ref[...]))
        ds_ref[...] = ds
        for i, val in enumerate((dq, dk, dv)):
            _ungroup_heads(dqkv_ref.at[i], val)
        _ungroup_heads(dg_ref, dg)
        _ungroup_heads(db_ref, db)

    def spec(kind=0):
        return pl.BlockSpec((GDN_CHUNK, GDN_GROUP * SLOT), lambda h, n: (n_chunks - 1 - n, kind * GDN_GROUPS + h))

    return pl.pallas_call(
        body, name="gdn_bwd",
        grid=(GDN_GROUPS, n_chunks),
        in_specs=[spec(0), spec(1), spec(2), spec(), spec(),
                  pl.BlockSpec((GDN_GROUP, 1, 2, d, d), lambda h, n: (h, n_chunks - 1 - n, 0, 0, 0)), spec()],
        out_specs=[pl.BlockSpec((3, GDN_CHUNK, GDN_GROUP * SLOT), lambda h, n: (0, n_chunks - 1 - n, h)), spec(), spec()],
        out_shape=[jax.ShapeDtypeStruct((3, t, N_HEADS * SLOT), F32)] + [jax.ShapeDtypeStruct((t, N_HEADS * SLOT), F32)] * 2,
        scratch_shapes=[pltpu.VMEM((GDN_GROUP, d, d), F32)],
        compiler_params=pltpu.CompilerParams(dimension_semantics=("parallel", "arbitrary")),
    )(qkv, qkv, qkv, gb, bb, keep, do)


def _rowwise(name, fn, rows, consts, outs, sums=(), tm=256):
    rows = [x if isinstance(x, tuple) else (x, x.shape[1], 0) for x in rows]
    t = rows[0][0].shape[0]
    tm = min(tm, t)
    steps = t // tm
    n_r, n_c, n_o, n_s = len(rows), len(consts), len(outs), len(sums)

    def window(width, block):
        return pl.BlockSpec((tm, width), lambda i: (i, block))

    def body(*refs):
        r, c = refs[:n_r], refs[n_r:n_r + n_c]
        o, s = refs[n_r + n_c:n_r + n_c + n_o], refs[n_r + n_c + n_o:]
        vals, tot = fn([x[...] for x in r], [x[...] for x in c])
        for ref, val in zip(o, vals):
            ref[...] = val.astype(ref.dtype)
        if n_s:
            @pl.when(pl.program_id(0) == 0)
            def _():
                for ref in s:
                    ref[...] = jnp.zeros_like(ref)

            for ref, val in zip(s, tot):
                ref[...] += val

    return pl.pallas_call(
        body, name=name,
        grid=(steps,),
        in_specs=[window(w, b) for _, w, b in rows] + [pl.BlockSpec(x.shape, lambda i: (0, 0)) for x in consts],
        out_specs=[pl.BlockSpec((tm, w), lambda i: (i, 0)) for w, _ in outs]
        + [pl.BlockSpec((1, w), lambda i: (0, 0)) for w in sums],
        out_shape=[jax.ShapeDtypeStruct((t, w), dt) for w, dt in outs]
        + [jax.ShapeDtypeStruct((1, w), F32) for w in sums],
        compiler_params=pltpu.CompilerParams(dimension_semantics=("arbitrary",)),
    )(*[x for x, _, _ in rows], *consts)


def _tile(dim, target):
    if dim <= target:
        return dim
    best = None
    for cand in range(128, target + 1, 128):
        if dim % cand == 0:
            best = cand
    assert best is not None, (dim, target)
    return best


def _matmul(name, a, b, mode, out_dtype=F32, tm=512, tn=1024, tk=1024):
    if mode == "nn":
        (m, k), n = a.shape, b.shape[1]
    elif mode == "nt":
        (m, k), n = a.shape, b.shape[0]
    else:
        (k, m), n = a.shape, b.shape[1]
    tm, tn, tk = _tile(m, tm), _tile(n, tn), _tile(k, tk)
    k_steps = k // tk
    product = {"nn": _nn, "nt": _nt, "tn": _tn}[mode]

    def body(a_ref, b_ref, o_ref, acc_ref):
        part = product(a_ref[...].astype(BF16), b_ref[...].astype(BF16))
        if k_steps == 1:
            o_ref[...] = part.astype(o_ref.dtype)
        else:
            kk = pl.program_id(2)

            @pl.when(kk == 0)
            def _():
                acc_ref[...] = part

            @pl.when(kk > 0)
            def _():
                acc_ref[...] += part

            @pl.when(kk == k_steps - 1)
            def _():
                o_ref[...] = acc_ref[...].astype(o_ref.dtype)

    a_spec = pl.BlockSpec((tk, tm), lambda i, j, kk: (kk, i)) if mode == "tn" else pl.BlockSpec((tm, tk), lambda i, j, kk: (i, kk))
    b_spec = pl.BlockSpec((tn, tk), lambda i, j, kk: (j, kk)) if mode == "nt" else pl.BlockSpec((tk, tn), lambda i, j, kk: (kk, j))
    return pl.pallas_call(
        body, name=name,
        grid=(m // tm, n // tn, k_steps),
        in_specs=[a_spec, b_spec],
        out_specs=pl.BlockSpec((tm, tn), lambda i, j, kk: (i, j)),
        out_shape=jax.ShapeDtypeStruct((m, n), out_dtype),
        scratch_shapes=[pltpu.VMEM((tm, tn) if k_steps > 1 else (8, 128), F32)],
        compiler_params=pltpu.CompilerParams(dimension_semantics=("parallel", "parallel", "arbitrary")),
    )(a, b)


FFN_TM = 512
FFN_BWD_TM = 256
FFN_TF = 1408


def _ffn_fwd(name, x, g_pre, w_gate, w_up, w_down, g_post):
    t, dm = x.shape
    f = w_gate.shape[1]
    tm, tf = min(FFN_TM, t), _tile(f, FFN_TF)
    f_steps = f // tf

    def body(x_ref, gpre_ref, wg_ref, wu_ref, wd_ref, gpost_ref, h_ref, y_ref, xn_ref, acc_ref):
        j = pl.program_id(1)

        @pl.when(j == 0)
        def _():
            xn_ref[...] = _rms(x_ref[...], gpre_ref[...], dm).astype(BF16)
            acc_ref[...] = jnp.zeros_like(acc_ref)

        xn = xn_ref[...]
        a = _silu(_nn(xn, wg_ref[...])) * _nn(xn, wu_ref[...])
        acc_ref[...] += _nn(a.astype(BF16), wd_ref[...])

        @pl.when(j == f_steps - 1)
        def _():
            h = acc_ref[...]
            h_ref[...] = h
            y_ref[...] = x_ref[...] + 0.5 * _rms(h, gpost_ref[...], dm)

    row = pl.BlockSpec((tm, dm), lambda i, j: (i, 0))
    vec = pl.BlockSpec((1, dm), lambda i, j: (0, 0))
    return pl.pallas_call(
        body, name=name,
        grid=(t // tm, f_steps),
        in_specs=[row, vec, pl.BlockSpec((dm, tf), lambda i, j: (0, j)), pl.BlockSpec((dm, tf), lambda i, j: (0, j)),
                  pl.BlockSpec((tf, dm), lambda i, j: (j, 0)), vec],
        out_specs=[row, row],
        out_shape=[jax.ShapeDtypeStruct((t, dm), F32)] * 2,
        scratch_shapes=[pltpu.VMEM((tm, dm), BF16), pltpu.VMEM((tm, dm), F32)],
        compiler_params=pltpu.CompilerParams(dimension_semantics=("parallel", "arbitrary")),
    )(x, g_pre, w_gate, w_up, w_down, g_post)


def _ffn_bwd(name, x, h, dy, g_pre, w_gate, w_up, w_down, g_post):
    t, dm = x.shape
    f = w_gate.shape[1]
    tm, tf = min(FFN_BWD_TM, t), _tile(f, FFN_TF)
    f_steps = f // tf

    def post(hv, g):
        return 0.5 * _rms(hv, g, dm)

    def pre(xv, g):
        return _rms(xv, g, dm)

    def body(x_ref, h_ref, dy_ref, gpre_ref, wg_ref, wu_ref, wd_ref, gpost_ref,
             dx_ref, xn_ref, dh_ref, a_ref, dhg_ref, dhu_ref, dgpre_ref, dgpost_ref, acc_ref):
        i, j = pl.program_id(0), pl.program_id(1)

        @pl.when((i == 0) & (j == 0))
        def _():
            dgpre_ref[...] = jnp.zeros_like(dgpre_ref)
            dgpost_ref[...] = jnp.zeros_like(dgpost_ref)

        @pl.when(j == 0)
        def _():
            xn_ref[...] = pre(x_ref[...], gpre_ref[...]).astype(BF16)
            _, pull = jax.vjp(post, h_ref[...], gpost_ref[...])
            dh, dg = pull(dy_ref[...])
            dh_ref[...] = dh.astype(BF16)
            dgpost_ref[...] += dg
            acc_ref[...] = jnp.zeros_like(acc_ref)

        xn = xn_ref[...]
        hg = _nn(xn, wg_ref[...])
        hu = _nn(xn, wu_ref[...])
        da = _nt(dh_ref[...], wd_ref[...])
        sig = _sigmoid(hg)
        act = hg * sig
        dhu = (da * act).astype(BF16)
        dhg = (da * hu * (sig * (1.0 + hg * (1.0 - sig)))).astype(BF16)
        a_ref[...] = (act * hu).astype(BF16)
        dhg_ref[...] = dhg
        dhu_ref[...] = dhu
        acc_ref[...] += _nt(dhg, wg_ref[...]) + _nt(dhu, wu_ref[...])

        @pl.when(j == f_steps - 1)
        def _():
            _, pull = jax.vjp(pre, x_ref[...], gpre_ref[...])
            dx, dg = pull(acc_ref[...])
            dx_ref[...] = dy_ref[...] + dx
            dgpre_ref[...] += dg

    row = pl.BlockSpec((tm, dm), lambda i, j: (i, 0))
    vec = pl.BlockSpec((1, dm), lambda i, j: (0, 0))
    wide = pl.BlockSpec((tm, tf), lambda i, j: (i, j))
    return pl.pallas_call(
        body, name=name,
        grid=(t // tm, f_steps),
        in_specs=[row, row, row, vec, pl.BlockSpec((dm, tf), lambda i, j: (0, j)), pl.BlockSpec((dm, tf), lambda i, j: (0, j)),
                  pl.BlockSpec((tf, dm), lambda i, j: (j, 0)), vec],
        out_specs=[row, row, row, wide, wide, wide, vec, vec],
        out_shape=[jax.ShapeDtypeStruct((t, dm), F32), jax.ShapeDtypeStruct((t, dm), BF16), jax.ShapeDtypeStruct((t, dm), BF16),
                   jax.ShapeDtypeStruct((t, f), BF16), jax.ShapeDtypeStruct((t, f), BF16), jax.ShapeDtypeStruct((t, f), BF16),
                   jax.ShapeDtypeStruct((1, dm), F32), jax.ShapeDtypeStruct((1, dm), F32)],
        scratch_shapes=[pltpu.VMEM((tm, dm), F32)],
        compiler_params=pltpu.CompilerParams(dimension_semantics=("arbitrary", "arbitrary")),
    )(x, h, dy, g_pre, w_gate, w_up, w_down, g_post)


ATT_T = 512
ATT_SCALE = (MLA_NOPE + MLA_ROPE) ** -0.5


def _causal_scores(q, k, qi, ki, tile):
    s = _nt(q, k) * ATT_SCALE
    row = lax.broadcasted_iota(jnp.int32, s.shape, 0) + qi * tile
    col = lax.broadcasted_iota(jnp.int32, s.shape, 1) + ki * tile
    return jnp.where(col <= row, s, -1e30)


def _attn_fwd(q, k, v):
    t = q.shape[0]
    tile = min(ATT_T, t)
    steps = t // tile

    def body(q_ref, k_ref, v_ref, o_ref, lse_ref, m_ref, l_ref, acc_ref):
        qi, ki = pl.program_id(1), pl.program_id(2)

        @pl.when(ki == 0)
        def _():
            m_ref[...] = jnp.full_like(m_ref, -1e30)
            l_ref[...] = jnp.zeros_like(l_ref)
            acc_ref[...] = jnp.zeros_like(acc_ref)

        @pl.when(ki <= qi)
        def _():
            s = _causal_scores(q_ref[...], k_ref[...], qi, ki, tile)
            m_old = m_ref[...]
            m_new = jnp.maximum(m_old, jnp.max(s, axis=-1, keepdims=True))
            p = jnp.exp(s - m_new)
            alpha = jnp.exp(m_old - m_new)
            l_ref[...] = alpha * l_ref[...] + jnp.sum(p, axis=-1, keepdims=True)
            acc_ref[...] = alpha * acc_ref[...] + _nn(p.astype(BF16), v_ref[...])
            m_ref[...] = m_new

        @pl.when(ki == qi)
        def _():
            o_ref[...] = acc_ref[...] / l_ref[...]
            lse_ref[...] = jnp.broadcast_to(m_ref[...] + jnp.log(l_ref[...]), lse_ref.shape)

    q_spec = pl.BlockSpec((tile, SLOT), lambda h, qi, ki: (qi, h))
    k_spec = pl.BlockSpec((tile, SLOT), lambda h, qi, ki: (jnp.minimum(ki, qi), h))
    return pl.pallas_call(
        body, name="attn_fwd",
        grid=(N_HEADS, steps, steps),
        in_specs=[q_spec, k_spec, k_spec],
        out_specs=[q_spec, q_spec],
        out_shape=[jax.ShapeDtypeStruct((t, N_HEADS * SLOT), F32)] * 2,
        scratch_shapes=[pltpu.VMEM((tile, 1), F32), pltpu.VMEM((tile, 1), F32), pltpu.VMEM((tile, SLOT), F32)],
        compiler_params=pltpu.CompilerParams(dimension_semantics=("parallel", "parallel", "arbitrary")),
    )(q, k, v)


def _attn_probs(q, k, lse_ref, qi, ki, tile):
    return jnp.exp(_causal_scores(q, k, qi, ki, tile) - lse_ref[:, 0:1])


def _attn_bwd_q(q, k, v, do, lse, delta):
    t = q.shape[0]
    tile = min(ATT_T, t)
    steps = t // tile

    def body(q_ref, k_ref, v_ref, do_ref, lse_ref, delta_ref, dq_ref, acc_ref):
        qi, ki = pl.program_id(1), pl.program_id(2)

        @pl.when(ki == 0)
        def _():
            acc_ref[...] = jnp.zeros_like(acc_ref)

        @pl.when(ki <= qi)
        def _():
            p = _attn_probs(q_ref[...], k_ref[...], lse_ref, qi, ki, tile)
            dp = _nt(do_ref[...].astype(BF16), v_ref[...])
            ds = p * (dp - delta_ref[:, 0:1]) * ATT_SCALE
            acc_ref[...] += _nn(ds.astype(BF16), k_ref[...])

        @pl.when(ki == qi)
        def _():
            dq_ref[...] = acc_ref[...]

    q_spec = pl.BlockSpec((tile, SLOT), lambda h, qi, ki: (qi, h))
    k_spec = pl.BlockSpec((tile, SLOT), lambda h, qi, ki: (jnp.minimum(ki, qi), h))
    return pl.pallas_call(
        body, name="attn_bwd_q",
        grid=(N_HEADS, steps, steps),
        in_specs=[q_spec, k_spec, k_spec, q_spec, q_spec, q_spec],
        out_specs=q_spec,
        out_shape=jax.ShapeDtypeStruct((t, N_HEADS * SLOT), F32),
        scratch_shapes=[pltpu.VMEM((tile, SLOT), F32)],
        compiler_params=pltpu.CompilerParams(dimension_semantics=("parallel", "parallel", "arbitrary")),
    )(q, k, v, do, lse, delta)


def _attn_bwd_kv(q, k, v, do, lse, delta):
    t = q.shape[0]
    tile = min(ATT_T, t)
    steps = t // tile

    def body(q_ref, k_ref, v_ref, do_ref, lse_ref, delta_ref, dk_ref, dv_ref, dk_acc, dv_acc):
        ki, qi = pl.program_id(1), pl.program_id(2)

        @pl.when(qi == 0)
        def _():
            dk_acc[...] = jnp.zeros_like(dk_acc)
            dv_acc[...] = jnp.zeros_like(dv_acc)

        @pl.when(qi >= ki)
        def _():
            p = _attn_probs(q_ref[...], k_ref[...], lse_ref, qi, ki, tile)
            do_b = do_ref[...].astype(BF16)
            dv_acc[...] += _tn(p.astype(BF16), do_b)
            dp = _nt(do_b, v_ref[...])
            ds = p * (dp - delta_ref[:, 0:1]) * ATT_SCALE
            dk_acc[...] += _tn(ds.astype(BF16), q_ref[...])

        @pl.when(qi == steps - 1)
        def _():
            dk_ref[...] = dk_acc[...]
            dv_ref[...] = dv_acc[...]

    q_spec = pl.BlockSpec((tile, SLOT), lambda h, ki, qi: (jnp.maximum(qi, ki), h))
    k_spec = pl.BlockSpec((tile, SLOT), lambda h, ki, qi: (ki, h))
    return pl.pallas_call(
        body, name="attn_bwd_kv",
        grid=(N_HEADS, steps, steps),
        in_specs=[q_spec, k_spec, k_spec, q_spec, q_spec, q_spec],
        out_specs=[k_spec, k_spec],
        out_shape=[jax.ShapeDtypeStruct((t, N_HEADS * SLOT), F32)] * 2,
        scratch_shapes=[pltpu.VMEM((tile, SLOT), F32), pltpu.VMEM((tile, SLOT), F32)],
        compiler_params=pltpu.CompilerParams(dimension_semantics=("parallel", "parallel", "arbitrary")),
    )(q, k, v, do, lse, delta)


def _shift_down(x, s):
    if s == 0:
        return x
    row = lax.broadcasted_iota(jnp.int32, x.shape, 0)
    return jnp.where(row >= s, pltpu.roll(x, s, 0), 0.0)


def _shift_up(x, s):
    if s == 0:
        return x
    n = x.shape[0]
    row = lax.broadcasted_iota(jnp.int32, x.shape, 0)
    return jnp.where(row < n - s, pltpu.roll(x, n - s, 0), 0.0)


def _l2norm(x):
    return x * lax.rsqrt(jnp.sum(x * x, axis=-1, keepdims=True) + EPS)


def _conv_pre(x, w):
    y = w[GDN_CONV - 1:GDN_CONV, :] * x
    for s in range(1, GDN_CONV):
        y = y + w[GDN_CONV - 1 - s:GDN_CONV - s, :] * _shift_down(x, s)
    return y


def _gdn_conv_fwd(x, w):
    t, width = x.shape

    def body(x_ref, w_ref, o_ref):
        act = _silu(_conv_pre(x_ref[...], w_ref[...]))
        normed = pl.program_id(0) < 2 * N_HEADS
        o_ref[...] = jnp.where(normed, _l2norm(act), act)

    return pl.pallas_call(
        body, name="gdn_conv_fwd",
        grid=(width // SLOT,),
        in_specs=[pl.BlockSpec((t, SLOT), lambda j: (0, j)), pl.BlockSpec((GDN_CONV, SLOT), lambda j: (0, j))],
        out_specs=pl.BlockSpec((t, SLOT), lambda j: (0, j)),
        out_shape=jax.ShapeDtypeStruct((t, width), F32),
        compiler_params=pltpu.CompilerParams(dimension_semantics=("parallel",)),
    )(x, w)


def _gdn_conv_bwd(x, w, dout):
    t, width = x.shape

    def body(x_ref, w_ref, do_ref, dx_ref, dw_ref):
        xv, wv = x_ref[...], w_ref[...]
        y = _conv_pre(xv, wv)
        sig = _sigmoid(y)
        act = y * sig
        _, pull = jax.vjp(_l2norm, act)
        normed = pl.program_id(0) < 2 * N_HEADS
        dact = jnp.where(normed, pull(do_ref[0])[0], do_ref[0])
        dy = dact * (sig * (1.0 + y * (1.0 - sig)))
        dx = wv[GDN_CONV - 1:GDN_CONV, :] * dy
        for s in range(1, GDN_CONV):
            dx = dx + wv[GDN_CONV - 1 - s:GDN_CONV - s, :] * _shift_up(dy, s)
        dx_ref[...] = dx.astype(BF16)
        for s in range(GDN_CONV):
            dw_ref[GDN_CONV - 1 - s:GDN_CONV - s, :] = jnp.sum(dy * _shift_down(xv, s), axis=0, keepdims=True)

    col = pl.BlockSpec((t, SLOT), lambda j: (0, j))
    tap = pl.BlockSpec((GDN_CONV, SLOT), lambda j: (0, j))
    return pl.pallas_call(
        body, name="gdn_conv_bwd",
        grid=(width // SLOT,),
        in_specs=[col, tap, pl.BlockSpec((1, t, SLOT), lambda j: (j // N_HEADS, 0, j % N_HEADS))],
        out_specs=[col, tap],
        out_shape=[jax.ShapeDtypeStruct((t, width), BF16), jax.ShapeDtypeStruct((GDN_CONV, width), F32)],
        compiler_params=pltpu.CompilerParams(dimension_semantics=("parallel",)),
    )(x, w, dout)


def _softplus(x):
    e = jnp.exp(-jnp.abs(x))
    u = 1.0 + e
    log1p = jnp.where(u == 1.0, e, jnp.log(u) * e / jnp.where(u == 1.0, 1.0, u - 1.0))
    return jnp.maximum(x, 0.0) + log1p


def _gates_fwd(ab, a_log, dt_bias):
    def fn(rows, consts):
        (abv,), (alog, dtb) = rows, consts
        g = -jnp.exp(alog) * _softplus(abv + dtb)
        beta = _sigmoid(abv)
        shape = (abv.shape[0], SLOT)
        g_slots = [jnp.broadcast_to(g[:, h:h + 1], shape) for h in range(N_HEADS)]
        b_slots = [jnp.broadcast_to(beta[:, N_HEADS + h:N_HEADS + h + 1], shape) for h in range(N_HEADS)]
        return [jnp.concatenate(g_slots, axis=1), jnp.concatenate(b_slots, axis=1)], []

    width = N_HEADS * SLOT
    return _rowwise("gdn_gates_fwd", fn, [ab], [a_log, dt_bias], [(width, F32), (width, F32)])


def _gates_bwd(ab, a_log, dt_bias, dg, dbeta):
    def fn(rows, consts):
        (abv, dgv, dbv), (alog, dtb) = rows, consts
        lane = lax.broadcasted_iota(jnp.int32, abv.shape, 1)
        dg_tok = jnp.zeros_like(abv)
        db_tok = jnp.zeros_like(abv)
        for h in range(N_HEADS):
            dg_tok = dg_tok + jnp.where(lane == h, jnp.sum(dgv[:, h * SLOT:(h + 1) * SLOT], axis=1, keepdims=True), 0.0)
            db_tok = db_tok + jnp.where(lane == N_HEADS + h, jnp.sum(dbv[:, h * SLOT:(h + 1) * SLOT], axis=1, keepdims=True), 0.0)
        xa = abv + dtb
        g = -jnp.exp(alog) * _softplus(xa)
        da = dg_tok * (-jnp.exp(alog)) * _sigmoid(xa)
        beta = _sigmoid(abv)
        dab = jnp.where(lane < N_HEADS, da, db_tok * beta * (1.0 - beta))
        dab = jnp.where(lane < 2 * N_HEADS, dab, 0.0)
        d_alog = jnp.sum(jnp.where(lane < N_HEADS, dg_tok * g, 0.0), axis=0, keepdims=True)
        d_dtb = jnp.sum(jnp.where(lane < N_HEADS, da, 0.0), axis=0, keepdims=True)
        return [dab], [d_alog, d_dtb]

    return _rowwise("gdn_gates_bwd", fn, [ab, dg, dbeta], [a_log, dt_bias], [(SLOT, F32)], sums=[SLOT, SLOT])


ROPE_HALF = MLA_ROPE // 2


def _rope_tables(positions):
    freqs = ROPE_THETA ** (-jnp.arange(ROPE_HALF, dtype=F32) / ROPE_HALF)
    ang = positions.astype(F32)[:, None] * freqs
    cos, sin = jnp.cos(ang), jnp.sin(ang)
    t = positions.shape[0]
    ones, zeros = jnp.ones((t, MLA_NOPE), F32), jnp.zeros((t, MLA_NOPE), F32)
    tail = jnp.zeros((t, SLOT - MLA_NOPE - MLA_ROPE), F32)
    half0 = jnp.zeros((t, ROPE_HALF), F32)
    same = jnp.concatenate([ones, cos, cos, tail], axis=1)
    from_low = jnp.concatenate([zeros, half0, sin, tail], axis=1)
    from_high = jnp.concatenate([zeros, -sin, half0, tail], axis=1)
    return same, from_low, from_high


def _rope(x, tabs):
    same, from_low, from_high = tabs
    width = x.shape[1]
    return x * same + pltpu.roll(x, ROPE_HALF, 1) * from_low + pltpu.roll(x, width - ROPE_HALF, 1) * from_high


def _rope_transposed(dy, tabs):
    same, from_low, from_high = tabs
    width = dy.shape[1]
    return dy * same + pltpu.roll(dy * from_low, width - ROPE_HALF, 1) + pltpu.roll(dy * from_high, ROPE_HALF, 1)


def _tile_slots(tab):
    return jnp.concatenate([tab] * N_HEADS, axis=1)


A_WIDTH = MLA_Q_RANK + MLA_KV_RANK + 2 * SLOT
A_KPE = MLA_Q_RANK + MLA_KV_RANK
A_AB = A_KPE + SLOT
WIDE = N_HEADS * SLOT


def _mla_pre_fwd(proj_a, tabs, g_q, g_kv):
    def fn(rows, consts):
        pa, *tb = rows
        gq, gkv = consts
        return [_rms(pa[:, :MLA_Q_RANK], gq, MLA_Q_RANK), _rms(pa[:, MLA_Q_RANK:A_KPE], gkv, MLA_KV_RANK),
                _rope(pa[:, A_KPE:A_AB], tb)], []

    return _rowwise("mla_pre_fwd", fn, [proj_a, *tabs], [g_q, g_kv], [(MLA_Q_RANK, BF16), (MLA_KV_RANK, BF16), (SLOT, F32)])


def _mla_pre_bwd(proj_a, tabs, g_q, g_kv, dcqn, dckvn, dkpe, dab):
    def fn(rows, consts):
        pa, t0, t1, t2, dq, dkv, dk, da = rows
        gq, gkv = consts
        _, pull_q = jax.vjp(lambda x, g: _rms(x, g, MLA_Q_RANK), pa[:, :MLA_Q_RANK], gq)
        _, pull_kv = jax.vjp(lambda x, g: _rms(x, g, MLA_KV_RANK), pa[:, MLA_Q_RANK:A_KPE], gkv)
        dcq, dgq = pull_q(dq)
        dckv, dgkv = pull_kv(dkv)
        return [jnp.concatenate([dcq, dckv, _rope_transposed(dk, (t0, t1, t2)), da], axis=1)], [dgq, dgkv]

    return _rowwise("mla_pre_bwd", fn, [proj_a, *tabs, dcqn, dckvn, dkpe, dab], [g_q, g_kv], [(A_WIDTH, BF16)],
                    sums=[MLA_Q_RANK, MLA_KV_RANK])


def _mla_qkv_fwd(q_p, kv_p, kpe, tabs):
    def fn(rows, consts):
        qv, kvv, kp, *tb = rows
        q = _rope(qv, [_tile_slots(x) for x in tb])
        k = kvv[:, :WIDE] + _tile_slots(kp)
        return [q, k, kvv[:, WIDE:]], []

    return _rowwise("mla_qkv_fwd", fn, [q_p, kv_p, kpe, *tabs], [], [(WIDE, BF16)] * 3)


def _mla_qkv_bwd(dq, dk, dv, tabs):
    def fn(rows, consts):
        dqv, dkv, dvv, *tb = rows
        dkpe = dkv[:, :SLOT]
        for h in range(1, N_HEADS):
            dkpe = dkpe + dkv[:, h * SLOT:(h + 1) * SLOT]
        return [_rope_transposed(dqv, [_tile_slots(x) for x in tb]), jnp.concatenate([dkv, dvv], axis=1), dkpe], []

    return _rowwise("mla_qkv_bwd", fn, [dq, dk, dv, *tabs], [], [(WIDE, BF16), (2 * WIDE, BF16), (SLOT, F32)])


def _slot_sum(x):
    parts = [jnp.broadcast_to(jnp.sum(x[:, h * SLOT:(h + 1) * SLOT], axis=1, keepdims=True), (x.shape[0], SLOT))
             for h in range(N_HEADS)]
    return jnp.concatenate(parts, axis=1)


def _mix_join(o_mla, o_gdn, gate, g_mla, g_gdn):
    mla = _rms(o_mla, g_mla, N_HEADS * MLA_V)
    gdn = o_gdn * lax.rsqrt(_slot_sum(o_gdn * o_gdn) * (1.0 / GDN_D) + EPS) * g_gdn * _silu(gate)
    return mla, gdn


def _mix_join_fwd(o_mla, o_gdn, gate, g_mla, g_gdn):
    def fn(rows, consts):
        return [jnp.concatenate(_mix_join(*rows, *consts), axis=1)], []

    return _rowwise("mix_join_fwd", fn, [o_mla, o_gdn, gate], [g_mla, g_gdn], [(2 * WIDE, BF16)])


def _mix_join_bwd(o_mla, o_gdn, gate, g_mla, g_gdn, dcat):
    def fn(rows, consts):
        om, og, gt, dc = rows
        gm, gg = consts
        _, pull = jax.vjp(lambda x, g: _rms(x, g, N_HEADS * MLA_V), om, gm)
        dom, dgm = pull(dc[:, :WIDE])
        dy = dc[:, WIDE:]
        r = lax.rsqrt(_slot_sum(og * og) * (1.0 / GDN_D) + EPS)
        sig = _sigmoid(gt)
        normed = og * r
        dn = dy * gg * (gt * sig)
        dog = r * dn - normed * (r * r) * _slot_sum(dn * og) * (1.0 / GDN_D)
        dgt = dy * normed * gg * (sig * (1.0 + gt * (1.0 - sig)))
        dgg = jnp.sum(dy * normed * (gt * sig), axis=0, keepdims=True)
        return [dom, _slot_sum(dom * om), dog, dgt], [dgm, dgg]

    return _rowwise("mix_join_bwd", fn, [o_mla, o_gdn, gate, dcat], [g_mla, g_gdn],
                    [(WIDE, F32), (WIDE, F32), (WIDE, F32), (WIDE, BF16)], sums=[WIDE, WIDE])


def _norm_residual_fwd(name, x, h, g, out_dtypes):
    dm = x.shape[1]

    def fn(rows, consts):
        y = rows[0] + _rms(rows[1], consts[0], dm)
        return [y] + [_rms(y, gg, dm) for gg in consts[1:]], []

    return _rowwise(name, fn, [x, h], list(g), [(dm, dt) for dt in out_dtypes])


def _norm_residual_bwd(name, h, g, dy):
    dm = h.shape[1]

    def fn(rows, consts):
        _, pull = jax.vjp(lambda hv, gv: _rms(hv, gv, dm), rows[0], consts[0])
        dh, dg = pull(rows[1])
        return [dh], [dg]

    return _rowwise(name, fn, [h, dy], [g], [(dm, BF16)], sums=[dm])


def _norm_bwd_add(name, x, g, dns, dy):
    dm = x.shape[1]

    def fn(rows, consts):
        xv, dyv, *parts = rows
        dn = parts[0]
        for p in parts[1:]:
            dn = dn + p
        _, pull = jax.vjp(lambda a, gv: _rms(a, gv, dm), xv, consts[0])
        dx, dg = pull(dn)
        return [dyv + dx], [dg]

    return _rowwise(name, fn, [x, dy, *dns], [g], [(dm, F32)], sums=[dm])


def _loss_fwd(y, target):
    dm = y.shape[1]

    def fn(rows, consts):
        err = rows[0] - rows[1]
        sq = err * err
        lanes = sq[:, :SLOT]
        for j in range(1, dm // SLOT):
            lanes = lanes + sq[:, j * SLOT:(j + 1) * SLOT]
        return [err * (1.0 / dm)], [jnp.sum(lanes, axis=0, keepdims=True) * (0.5 / dm)]

    return _rowwise("loss", fn, [y, target], [], [(dm, F32)], sums=[SLOT])


def _norm_fwd(name, x, g):
    dm = x.shape[1]
    return _rowwise(name, lambda rows, consts: ([_rms(rows[0], consts[0], dm)], []), [x], [g], [(dm, BF16)])[0]


W_IN_CUTS = (0, 256, 384, 416, 1952, 1960, 1968, 2480)


def _heads_out(w, per_head, axis=-1):
    w = jnp.moveaxis(w, axis, -1)
    lead = w.shape[:-1]
    w = w.reshape(lead + (w.shape[-1] // per_head, per_head))
    w = jnp.pad(w, [(0, 0)] * len(lead) + [(0, 0), (0, SLOT - per_head)])
    return jnp.moveaxis(w.reshape(lead + (-1,)), -1, axis)


def _heads_in(w, per_head, axis=-1):
    w = jnp.moveaxis(w, axis, -1)
    lead = w.shape[:-1]
    w = w.reshape(lead + (w.shape[-1] // SLOT, SLOT))[..., :per_head]
    return jnp.moveaxis(w.reshape(lead + (-1,)), -1, axis)


def _pad_lanes(v, lo, width=SLOT):
    return jnp.pad(v, [(0, 0)] * (v.ndim - 1) + [(lo, width - lo - v.shape[-1])])


def _layout_weights(w):
    c = W_IN_CUTS
    w_in = w["w_in"]
    p = {}
    p["w_a"] = jnp.concatenate([w_in[:, c[0]:c[2]], _pad_lanes(w_in[:, c[2]:c[3]], MLA_NOPE),
                                _pad_lanes(w_in[:, c[4]:c[6]], 0)], axis=1)
    p["w_qkv"] = _heads_out(w_in[:, c[3]:c[4]], GDN_D)
    p["w_gate"] = _heads_out(w_in[:, c[6]:c[7]], GDN_D)
    p["w_uq"] = _heads_out(w["mla_w_uq"], MLA_NOPE + MLA_ROPE)
    ukv = w["mla_w_ukv"].reshape(MLA_KV_RANK, N_HEADS, MLA_NOPE + MLA_V)
    p["w_kv"] = jnp.concatenate([_heads_out(ukv[:, :, :MLA_NOPE].reshape(MLA_KV_RANK, -1), MLA_NOPE),
                                 _heads_out(ukv[:, :, MLA_NOPE:].reshape(MLA_KV_RANK, -1), MLA_V)], axis=1)
    p["w_out"] = _heads_out(w["w_out"], GDN_D, axis=0)
    p["conv"] = _heads_out(w["gdn_conv_w"], GDN_D)
    p["g_mla_out"] = _heads_out(w["mla_out_g"], MLA_V)
    p["g_gdn"] = jnp.tile(_pad_lanes(w["gdn_norm_g"], 0), (1, N_HEADS))
    p["a_log"] = _pad_lanes(w["gdn_a_log"], 0)
    p["dt_bias"] = _pad_lanes(w["gdn_dt_bias"], 0)
    return p


def _unlayout_grads(d):
    c = W_IN_CUTS
    g = {}
    da = d["w_a"]
    kpe0 = A_KPE + MLA_NOPE
    g["w_in"] = jnp.concatenate([da[:, :A_KPE], da[:, kpe0:kpe0 + MLA_ROPE], _heads_in(d["w_qkv"], GDN_D),
                                 da[:, A_AB:A_AB + 2 * N_HEADS], _heads_in(d["w_gate"], GDN_D)], axis=1)
    assert g["w_in"].shape[1] == c[-1]
    g["mla_w_uq"] = _heads_in(d["w_uq"], MLA_NOPE + MLA_ROPE)
    dk = _heads_in(d["w_kv"][:, :WIDE], MLA_NOPE).reshape(MLA_KV_RANK, N_HEADS, MLA_NOPE)
    dv = _heads_in(d["w_kv"][:, WIDE:], MLA_V).reshape(MLA_KV_RANK, N_HEADS, MLA_V)
    g["mla_w_ukv"] = jnp.concatenate([dk, dv], axis=2).reshape(MLA_KV_RANK, -1)
    g["w_out"] = _heads_in(d["w_out"], GDN_D, axis=0)
    g["gdn_conv_w"] = _heads_in(d["conv"], GDN_D)
    g["mla_out_g"] = _heads_in(d["g_mla_out"], MLA_V)
    g["gdn_norm_g"] = jnp.sum(d["g_gdn"].reshape(N_HEADS, SLOT), axis=0, keepdims=True)[:, :GDN_D]
    g["gdn_a_log"] = d["a_log"][:, :N_HEADS]
    g["gdn_dt_bias"] = d["dt_bias"][:, :N_HEADS]
    return g


def _weight_grad(name, acts, cots, tm=1024, tn=1408, tk=512):
    return _matmul(name, acts, cots, "tn", tm=tm, tn=tn, tk=tk)


def _local_step(x, positions, target, w):
    p = _layout_weights(w)
    tabs = _rope_tables(positions)

    h1, x1 = _ffn_fwd("ffn1_fwd", x, w["ffn1_pre_g"], w["ffn1_w_gate"], w["ffn1_w_up"], w["ffn1_w_down"], w["ffn1_post_g"])
    hn = _norm_fwd("mix_pre_norm", x1, w["mix_pre_g"])
    proj_a = _matmul("proj_a", hn, p["w_a"], "nn")
    proj_qkv = _matmul("proj_qkv", hn, p["w_qkv"], "nn")
    proj_gate = _matmul("proj_gate", hn, p["w_gate"], "nn")
    cqn, ckvn, kpe = _mla_pre_fwd(proj_a, tabs, w["mla_q_norm_g"], w["mla_kv_norm_g"])
    q_p = _matmul("mla_q", cqn, p["w_uq"], "nn")
    kv_p = _matmul("mla_kv", ckvn, p["w_kv"], "nn")
    q, k, v = _mla_qkv_fwd(q_p, kv_p, kpe, tabs)
    o_mla, lse = _attn_fwd(q, k, v)
    ab = (proj_a, SLOT, A_AB // SLOT)
    qkv_n = _gdn_conv_fwd(proj_qkv, p["conv"])
    gb, bb = _gates_fwd(ab, p["a_log"], p["dt_bias"])
    o_gdn, keep = _gdn_fwd(qkv_n, gb, bb)
    cat = _mix_join_fwd(o_mla, o_gdn, proj_gate, p["g_mla_out"], p["g_gdn"])[0]
    mixed = _matmul("mix_out", cat, p["w_out"], "nn")
    x2 = _norm_residual_fwd("mix_post", x1, mixed, [w["mix_post_g"]], [F32])[0]
    h2, y = _ffn_fwd("ffn2_fwd", x2, w["ffn2_pre_g"], w["ffn2_w_gate"], w["ffn2_w_up"], w["ffn2_w_down"], w["ffn2_post_g"])
    dy, loss_lanes = _loss_fwd(y, target)

    g = {}
    dx2, xn2, dh2, a2, dhg2, dhu2, g["ffn2_pre_g"], g["ffn2_post_g"] = _ffn_bwd(
        "ffn2_bwd", x2, h2, dy, w["ffn2_pre_g"], w["ffn2_w_gate"], w["ffn2_w_up"], w["ffn2_w_down"], w["ffn2_post_g"])
    g["ffn2_w_gate"] = _weight_grad("ffn2_dw_gate", xn2, dhg2)
    g["ffn2_w_up"] = _weight_grad("ffn2_dw_up", xn2, dhu2)
    g["ffn2_w_down"] = _weight_grad("ffn2_dw_down", a2, dh2, tm=1408, tn=1024)
    dmixed, g["mix_post_g"] = _norm_residual_bwd("mix_post_bwd", mixed, w["mix_post_g"], dx2)
    dcat = _matmul("mix_out_dx", dmixed, p["w_out"], "nt")
    d = {}
    d["w_out"] = _weight_grad("mix_out_dw", cat, dmixed, tn=1024)
    do_mla, delta, do_gdn, dgate, d["g_mla_out"], d["g_gdn"] = _mix_join_bwd(o_mla, o_gdn, proj_gate, p["g_mla_out"], p["g_gdn"], dcat)
    dq = _attn_bwd_q(q, k, v, do_mla, lse, delta)
    dk, dv = _attn_bwd_kv(q, k, v, do_mla, lse, delta)
    dq_p, dkv_p, dkpe = _mla_qkv_bwd(dq, dk, dv, tabs)
    dcqn = _matmul("mla_q_dx", dq_p, p["w_uq"], "nt")
    d["w_uq"] = _weight_grad("mla_q_dw", cqn, dq_p, tn=1024)
    dckvn = _matmul("mla_kv_dx", dkv_p, p["w_kv"], "nt")
    d["w_kv"] = _weight_grad("mla_kv_dw", ckvn, dkv_p, tn=1024)
    dqkv_n, dgb, dbb = _gdn_bwd(qkv_n, gb, bb, keep, do_gdn)
    dab, d["a_log"], d["dt_bias"] = _gates_bwd(ab, p["a_log"], p["dt_bias"], dgb, dbb)
    dproj_qkv, d["conv"] = _gdn_conv_bwd(proj_qkv, p["conv"], dqkv_n)
    dproj_a, g["mla_q_norm_g"], g["mla_kv_norm_g"] = _mla_pre_bwd(
        proj_a, tabs, w["mla_q_norm_g"], w["mla_kv_norm_g"], dcqn, dckvn, dkpe, dab)
    dhn = [_matmul("proj_a_dx", dproj_a, p["w_a"], "nt"), _matmul("proj_qkv_dx", dproj_qkv, p["w_qkv"], "nt"),
           _matmul("proj_gate_dx", dgate, p["w_gate"], "nt")]
    d["w_a"] = _weight_grad("proj_a_dw", hn, dproj_a, tn=640)
    d["w_qkv"] = _weight_grad("proj_qkv_dw", hn, dproj_qkv, tn=1024)
    d["w_gate"] = _weight_grad("proj_gate_dw", hn, dgate, tn=1024)
    dx1, g["mix_pre_g"] = _norm_bwd_add("mix_pre_bwd", x1, w["mix_pre_g"], dhn, dx2)
    dx, xn1, dh1, a1, dhg1, dhu1, g["ffn1_pre_g"], g["ffn1_post_g"] = _ffn_bwd(
        "ffn1_bwd", x, h1, dx1, w["ffn1_pre_g"], w["ffn1_w_gate"], w["ffn1_w_up"], w["ffn1_w_down"], w["ffn1_post_g"])
    g["ffn1_w_gate"] = _weight_grad("ffn1_dw_gate", xn1, dhg1)
    g["ffn1_w_up"] = _weight_grad("ffn1_dw_up", xn1, dhu1)
    g["ffn1_w_down"] = _weight_grad("ffn1_dw_down", a1, dh1, tm=1408, tn=1024)
    g.update(_unlayout_grads(d))
    return loss_lanes, dx, g


MESH_AXES = ("x", "y", "c")


def _peer(r):
    x, y, c = (lax.axis_index(a) for a in MESH_AXES)
    return (1 - x if r & 4 else x, 1 - y if r & 2 else y, 1 - c if r & 1 else c)


def _block_of(dev):
    x, y, c = dev
    return 4 * x + 2 * y + c


def _exchange(name, buf, gather):
    shape = buf.shape[-2:]

    def body(x_ref, out_ref, send_sems, recv_sems, local_sem):
        me = _block_of(_peer(0))
        mine = x_ref if gather else x_ref.at[me]
        local = pltpu.make_async_copy(mine, out_ref.at[me], local_sem)
        local.start()

        def copy(r):
            src = x_ref if gather else x_ref.at[_block_of(_peer(r))]
            return pltpu.make_async_remote_copy(
                src_ref=src, dst_ref=out_ref.at[me], send_sem=send_sems.at[r - 1], recv_sem=recv_sems.at[r - 1],
                device_id=_peer(r), device_id_type=pl.DeviceIdType.MESH)

        def arrival(r):
            return pltpu.make_async_remote_copy(
                src_ref=mine, dst_ref=out_ref.at[_block_of(_peer(r))], send_sem=send_sems.at[r - 1],
                recv_sem=recv_sems.at[r - 1], device_id=_peer(r), device_id_type=pl.DeviceIdType.MESH)

        sends = [copy(r) for r in range(1, N_DEV)]
        for cp in sends:
            cp.start()
        for r in range(1, N_DEV):
            arrival(r).wait_recv()
        for cp in sends:
            cp.wait_send()
        local.wait()

    return pl.pallas_call(
        body, name=name,
        in_specs=[pl.BlockSpec(memory_space=pl.ANY)],
        out_specs=pl.BlockSpec(memory_space=pl.ANY),
        out_shape=jax.ShapeDtypeStruct((N_DEV,) + shape, buf.dtype),
        scratch_shapes=[pltpu.SemaphoreType.DMA((N_DEV - 1,)), pltpu.SemaphoreType.DMA((N_DEV - 1,)), pltpu.SemaphoreType.DMA(())],
    )(buf)


def _sum_blocks(name, blocks, tm):
    _, rows, width = blocks.shape
    assert rows % tm == 0

    def body(x_ref, o_ref):
        acc = x_ref[0].astype(F32)
        for d in range(1, N_DEV):
            acc = acc + x_ref[d].astype(F32)
        o_ref[...] = acc

    return pl.pallas_call(
        body, name=name,
        grid=(rows // tm,),
        in_specs=[pl.BlockSpec((N_DEV, tm, width), lambda i: (0, i, 0))],
        out_specs=pl.BlockSpec((tm, width), lambda i: (i, 0)),
        out_shape=jax.ShapeDtypeStruct((rows, width), F32),
        compiler_params=pltpu.CompilerParams(dimension_semantics=("parallel",)),
    )(blocks)


def _all_reduce_small(name, vec):
    rows, width = vec.shape

    def body(x_ref, o_ref, all_ref, send_sems, recv_sems):
        me = _block_of(_peer(0))
        all_ref[me] = x_ref[...]

        def copy(r, block):
            return pltpu.make_async_remote_copy(
                src_ref=x_ref, dst_ref=all_ref.at[block], send_sem=send_sems.at[r - 1], recv_sem=recv_sems.at[r - 1],
                device_id=_peer(r), device_id_type=pl.DeviceIdType.MESH)

        sends = [copy(r, me) for r in range(1, N_DEV)]
        for cp in sends:
            cp.start()
        for r in range(1, N_DEV):
            copy(r, _block_of(_peer(r))).wait_recv()
        for cp in sends:
            cp.wait_send()
        acc = all_ref[0]
        for d in range(1, N_DEV):
            acc = acc + all_ref[d]
        o_ref[...] = acc

    return pl.pallas_call(
        body, name=name,
        in_specs=[pl.BlockSpec(memory_space=pltpu.VMEM)],
        out_specs=pl.BlockSpec(memory_space=pltpu.VMEM),
        out_shape=jax.ShapeDtypeStruct((rows, width), F32),
        scratch_shapes=[pltpu.VMEM((N_DEV, rows, width), F32), pltpu.SemaphoreType.DMA((N_DEV - 1,)), pltpu.SemaphoreType.DMA((N_DEV - 1,))],
    )(vec)


def _adamw(name, w, g, m, v, tm):
    def fn(rows, consts):
        wv, gv, mv, vv = rows
        m2 = ADAM_B1 * mv + (1.0 - ADAM_B1) * gv
        v2 = ADAM_B2 * vv + (1.0 - ADAM_B2) * jnp.square(gv)
        m_hat = m2 / (1.0 - ADAM_B1 ** ADAM_STEP)
        v_hat = v2 / (1.0 - ADAM_B2 ** ADAM_STEP)
        return [-ADAM_LR * (m_hat / (jnp.sqrt(v_hat) + ADAM_EPS) + ADAM_WD * wv), m2, v2], []

    width = w.shape[1]
    return _rowwise(name, fn, [w, g, m, v], [], [(width, F32)] * 3, tm=tm)


ROW = 1024
BIG = {
    "ffn1_w_gate": ((D_MODEL, D_FF), 1), "ffn1_w_up": ((D_MODEL, D_FF), 1), "ffn1_w_down": ((D_FF, D_MODEL), 0),
    "w_in": ((D_MODEL, W_IN_CUTS[-1]), 1), "mla_w_uq": ((MLA_Q_RANK, N_HEADS * (MLA_NOPE + MLA_ROPE)), 1),
    "mla_w_ukv": ((MLA_KV_RANK, N_HEADS * (MLA_NOPE + MLA_V)), 1), "w_out": ((2 * N_HEADS * GDN_D, D_MODEL), 0),
    "ffn2_w_gate": ((D_MODEL, D_FF), 1), "ffn2_w_up": ((D_MODEL, D_FF), 1), "ffn2_w_down": ((D_FF, D_MODEL), 0),
}
PART_ROWS = 16
BIG_TM = 240
BIG_ROWS = 11 * BIG_TM
SMALL = {
    "ffn1_pre_g": (1024, 1024), "ffn1_post_g": (1024, 1024), "mix_pre_g": (1024, 1024), "mla_q_norm_g": (256, 256),
    "mla_kv_norm_g": (128, 128), "mla_out_g": (512, 512), "gdn_a_log": (8, 128), "gdn_dt_bias": (8, 128),
    "gdn_norm_g": (64, 128), "mix_post_g": (1024, 1024), "ffn2_pre_g": (1024, 1024), "ffn2_post_g": (1024, 1024),
}
SMALL_LANES = sum(r for _, r in SMALL.values())
CONV_SHAPE = (GDN_CONV, 3 * N_HEADS * GDN_D)
CONV_SHARD = (GDN_CONV, CONV_SHAPE[1] // N_DEV)
CONV_LANES = CONV_SHAPE[0] * CONV_SHAPE[1]
SMALL_ROWS = 8
REDUCE_ROWS = 16


def _shard_shape(name):
    shape, axis = BIG[name]
    return tuple(s // N_DEV if i == axis else s for i, s in enumerate(shape))


def _part_rows(name):
    shape = _shard_shape(name)
    rows = shape[0] * shape[1] // ROW
    return rows, -(-rows // PART_ROWS) * PART_ROWS


def _pad_rows(a, rows, axis):
    pad = [(0, 0)] * a.ndim
    pad[axis] = (0, rows - a.shape[axis])
    return jnp.pad(a, pad)


def _pack_big(shards):
    parts = [_pad_rows(shards[n].reshape(-1, ROW), _part_rows(n)[1], 0) for n in BIG]
    rows = sum(p.shape[0] for p in parts)
    return jnp.concatenate(parts + [jnp.zeros((BIG_ROWS - rows, ROW), parts[0].dtype)], axis=0)


def _unpack_big(buf):
    out, at = {}, 0
    for n in BIG:
        rows, taken = _part_rows(n)
        out[n] = buf[at:at + rows].reshape(_shard_shape(n))
        at += taken
    return out


def _join_big(gathered):
    out, at = {}, 0
    for n, (shape, axis) in BIG.items():
        rows, taken = _part_rows(n)
        blocks = gathered[:, at:at + rows].reshape((N_DEV,) + _shard_shape(n))
        out[n] = blocks.reshape(shape) if axis == 0 else blocks.transpose(1, 0, 2).reshape(shape)
        at += taken
    return out


def _split_big(full, dtype):
    parts = []
    for n, (shape, axis) in BIG.items():
        sh = _shard_shape(n)
        blocks = full[n].astype(dtype).reshape((N_DEV,) + sh) if axis == 0 else full[n].astype(dtype).reshape(
            sh[0], N_DEV, sh[1]).transpose(1, 0, 2)
        parts.append(_pad_rows(blocks.reshape(N_DEV, -1, ROW), _part_rows(n)[1], 1))
    rows = sum(p.shape[1] for p in parts)
    return jnp.concatenate(parts + [jnp.zeros((N_DEV, BIG_ROWS - rows, ROW), dtype)], axis=1)


def _pack_small(vecs, conv, rows):
    parts = [_pad_lanes(vecs[n].reshape(1, -1), 0, r) for n, (_, r) in SMALL.items()]
    if conv is not None:
        parts.append(conv.reshape(1, -1))
    flat = jnp.concatenate(parts, axis=1)
    return _pad_lanes(flat, 0, rows * ROW).reshape(rows, ROW)


def _unpack_small(buf):
    flat = buf.reshape(1, -1)
    out, at = {}, 0
    for n, (w, r) in SMALL.items():
        out[n] = flat[:, at:at + w]
        at += r
    return out, flat[0, at:]


def kernel(x, positions, ffn1_pre_g, ffn1_w_gate, ffn1_w_up, ffn1_w_down, ffn1_post_g, mix_pre_g, w_in, mla_q_norm_g, mla_w_uq, mla_kv_norm_g, mla_w_ukv, mla_out_g, gdn_conv_w, gdn_a_log, gdn_dt_bias, gdn_norm_g, w_out, mix_post_g, ffn2_pre_g, ffn2_w_gate, ffn2_w_up, ffn2_w_down, ffn2_post_g, loss_target, m_ffn1_pre_g, m_ffn1_w_gate, m_ffn1_w_up, m_ffn1_w_down, m_ffn1_post_g, m_mix_pre_g, m_w_in, m_mla_q_norm_g, m_mla_w_uq, m_mla_kv_norm_g, m_mla_w_ukv, m_mla_out_g, m_gdn_conv_w, m_gdn_a_log, m_gdn_dt_bias, m_gdn_norm_g, m_w_out, m_mix_post_g, m_ffn2_pre_g, m_ffn2_w_gate, m_ffn2_w_up, m_ffn2_w_down, m_ffn2_post_g, v_ffn1_pre_g, v_ffn1_w_gate, v_ffn1_w_up, v_ffn1_w_down, v_ffn1_post_g, v_mix_pre_g, v_w_in, v_mla_q_norm_g, v_mla_w_uq, v_mla_kv_norm_g, v_mla_w_ukv, v_mla_out_g, v_gdn_conv_w, v_gdn_a_log, v_gdn_dt_bias, v_gdn_norm_g, v_w_out, v_mix_post_g, v_ffn2_pre_g, v_ffn2_w_gate, v_ffn2_w_up, v_ffn2_w_down, v_ffn2_post_g):
    given = dict(locals())
    names = list(BIG) + list(SMALL) + ["gdn_conv_w"]
    order = ["ffn1_pre_g", "ffn1_w_gate", "ffn1_w_up", "ffn1_w_down", "ffn1_post_g", "mix_pre_g", "w_in", "mla_q_norm_g",
             "mla_w_uq", "mla_kv_norm_g", "mla_w_ukv", "mla_out_g", "gdn_conv_w", "gdn_a_log", "gdn_dt_bias", "gdn_norm_g",
             "w_out", "mix_post_g", "ffn2_pre_g", "ffn2_w_gate", "ffn2_w_up", "ffn2_w_down", "ffn2_post_g"]
    assert sorted(names) == sorted(order)
    def drop_depth(a):
        return a[0] if a.ndim == 3 else a

    wts = {n: drop_depth(given[n]) for n in order}
    mom = {n: drop_depth(given["m_" + n]) for n in order}
    var = {n: drop_depth(given["v_" + n]) for n in order}
    me = _block_of(_peer(0))

    gathered = _exchange("gather_weights", _pack_big({n: wts[n].astype(BF16) for n in BIG}), gather=True)
    conv_at = lax.dynamic_update_slice(jnp.zeros((N_DEV, CONV_SHARD[0] * CONV_SHARD[1]), F32),
                                       wts["gdn_conv_w"].reshape(1, -1), (me, 0))
    conv_all = _all_reduce_small("gather_conv", _pad_lanes(conv_at.reshape(1, -1), 0, SMALL_ROWS * ROW).reshape(SMALL_ROWS, ROW))
    conv_full = conv_all.reshape(-1)[:CONV_LANES].reshape((N_DEV,) + CONV_SHARD).transpose(1, 0, 2).reshape(CONV_SHAPE)
    full = _join_big(gathered)
    full.update({n: wts[n] for n in SMALL})
    full["gdn_conv_w"] = conv_full

    loss_lanes, dx, grads = _local_step(x[0], positions[0], loss_target[0], full)
    loss = lax.psum(jnp.sum(loss_lanes), MESH_AXES)

    landed = _exchange("scatter_grads", _split_big(grads, BF16), gather=False)
    big_grad = _sum_blocks("sum_grads", landed, BIG_TM)
    small_sum = _all_reduce_small("reduce_small", _pack_small(grads, grads["gdn_conv_w"].reshape(-1), REDUCE_ROWS))
    small_grad, conv_grad_full = _unpack_small(small_sum)
    conv_grad = lax.dynamic_slice(conv_grad_full[:CONV_LANES].reshape(CONV_SHAPE), (0, me * CONV_SHARD[1]), CONV_SHARD)

    big = [_pack_big({n: s[n] for n in BIG}) for s in (wts, mom, var)]
    big_out = _adamw("adamw_big", big[0], big_grad, big[1], big[2], BIG_TM)
    small_g = dict(small_grad)
    small = [_pack_small(s, s["gdn_conv_w"].reshape(-1), SMALL_ROWS) for s in (wts, {**small_g, "gdn_conv_w": conv_grad}, mom, var)]
    small_out = _adamw("adamw_small", *small, SMALL_ROWS)

    outs = {"grad": {**_unpack_big(big_grad), **small_g, "gdn_conv_w": conv_grad}}
    for kind, b, s in zip(("delta", "new_m", "new_v"), big_out, small_out):
        vecs, conv = _unpack_small(s)
        outs[kind] = {**_unpack_big(b), **vecs, "gdn_conv_w": conv[:CONV_SHARD[0] * CONV_SHARD[1]].reshape(CONV_SHARD)}
    result = [loss, dx[None]]
    for kind in ("grad", "delta", "new_m", "new_v"):
        result += [outs[kind][n].reshape(given[n].shape) for n in order]
    return tuple(result)
```

```python
import functools

import jax
import jax.numpy as jnp
import numpy as np
from jax import lax
from jax.experimental import pallas as pl
from jax.experimental.pallas import tpu as pltpu

F32 = jnp.float32
BF16 = jnp.bfloat16
HI = lax.Precision.HIGH
EXACT = lax.Precision.HIGHEST

N_DEV = 8
D_MODEL = 1024
D_FF = 2816
N_HEADS = 8
SLOT = 128
MLA_Q_RANK = 256
MLA_KV_RANK = 128
MLA_NOPE = 64
MLA_ROPE = 32
MLA_V = 64
GDN_D = 64
GDN_CONV = 4
GDN_CHUNK = 64
ROPE_THETA = 10000.0
EPS = 1e-6
ADAM_LR, ADAM_B1, ADAM_B2, ADAM_EPS, ADAM_WD, ADAM_STEP = 0.001, 0.9, 0.999, 1e-08, 0.01, 10


def _dot(a, b, ca, cb, precision=None):
    lead = a.ndim - 2
    batch = tuple(range(lead))
    return lax.dot_general(a, b, (((lead + ca,), (lead + cb,)), (batch, batch)), precision=precision,
                           preferred_element_type=F32)


def _nn(a, b, precision=None):
    return _dot(a, b, 1, 0, precision)


def _nt(a, b, precision=None):
    return _dot(a, b, 1, 1, precision)


def _tn(a, b, precision=None):
    return _dot(a, b, 0, 0, precision)


def _sigmoid(x):
    return 1.0 / (1.0 + jnp.exp(-x))


def _silu(x):
    return x * _sigmoid(x)


def _rms(x, g, n):
    ms = jnp.sum(x * x, axis=-1, keepdims=True) * (1.0 / n)
    return x * lax.rsqrt(ms + EPS) * g


def _chunk_masks():
    c = GDN_CHUNK
    i = lax.broadcasted_iota(jnp.int32, (c, c), 0)
    j = lax.broadcasted_iota(jnp.int32, (c, c), 1)
    lower = i >= j
    strict = i > j
    eye = (i == j).astype(F32)
    blocks = []
    b = 1
    while b < c:
        same = (i // (2 * b)) == (j // (2 * b))
        blocks.append(same & ((i % (2 * b)) >= b) & ((j % (2 * b)) < b))
        b *= 2
    return lower, strict, eye, blocks


def _unit_lower_inverse(low, eye, blocks):
    t = jnp.broadcast_to(eye, low.shape)
    for m in blocks:
        lo = jnp.where(m, low, 0.0)
        t = t - _nn(t, _nn(lo, t, HI), HI)
    return t


@jax.custom_vjp
def _known_inverse(low, tinv):
    return tinv


def _known_inverse_fwd(low, tinv):
    return tinv, tinv


def _known_inverse_bwd(tinv, dt):
    return -_tn(tinv, _nt(dt, tinv, HI), HI), jnp.zeros_like(tinv)


_known_inverse.defvjp(_known_inverse_fwd, _known_inverse_bwd)


def _gdn_chunk(q, k, v, gb, bb, s, masks, tinv=None):
    lower, strict, eye, blocks = masks
    qs = q * (GDN_D ** -0.5)
    gc = _nn(jnp.broadcast_to(lower.astype(F32), gb.shape), gb, EXACT)
    gct = _nt(jnp.broadcast_to(eye, gb.shape), gc, EXACT)
    decay = jnp.exp(jnp.where(lower, gc - gct, -1e30))
    kb = k * bb
    low = jnp.where(strict, _nt(kb, k, HI) * decay, 0.0)
    tinv = _unit_lower_inverse(low, eye, blocks) if tinv is None else _known_inverse(low, tinv)
    eg = jnp.exp(gc)
    w = _nn(tinv, kb * eg, HI)
    u = _nn(tinv, v * bb, HI)
    attn = _nt(qs, k, HI) * decay
    g_end = jnp.sum(gb, axis=-2, keepdims=True)
    k_dec = k * jnp.exp(g_end - gc)
    v_new = u - _nn(w, s, HI)
    o = _nn(qs * eg, s, HI) + _nn(attn, v_new, HI)
    s_new = s * jnp.exp(g_end) + _tn(k_dec, v_new, HI)
    return o, s_new, tinv


GDN_GROUP = 8
GDN_GROUPS = N_HEADS // GDN_GROUP


def _group_heads(ref):
    return jnp.stack([ref[:, pl.ds(j * SLOT, GDN_D)] for j in range(GDN_GROUP)])


def _ungroup_heads(ref, val):
    pad = jnp.zeros((GDN_CHUNK, SLOT - GDN_D), F32)
    for j in range(GDN_GROUP):
        ref[:, pl.ds(j * SLOT, GDN_D)] = val[j]
        ref[:, pl.ds(j * SLOT + GDN_D, SLOT - GDN_D)] = pad


def _gdn_fwd(qkv, gb, bb):
    t = qkv.shape[0]
    n_chunks = t // GDN_CHUNK
    d = GDN_D

    def body(q_ref, k_ref, v_ref, g_ref, b_ref, o_ref, keep_ref, s_ref):
        @pl.when(pl.program_id(1) == 0)
        def _():
            s_ref[...] = jnp.zeros_like(s_ref)

        s = s_ref[...]
        keep_ref[:, 0, 0] = s
        o, s_new, tinv = _gdn_chunk(*[_group_heads(r) for r in (q_ref, k_ref, v_ref, g_ref, b_ref)], s, _chunk_masks())
        keep_ref[:, 0, 1] = tinv
        s_ref[...] = s_new
        _ungroup_heads(o_ref, o)

    def spec(kind=0):
        return pl.BlockSpec((GDN_CHUNK, GDN_GROUP * SLOT), lambda h, n: (n, kind * GDN_GROUPS + h))

    return pl.pallas_call(
        body, name="gdn_fwd",
        grid=(GDN_GROUPS, n_chunks),
        in_specs=[spec(0), spec(1), spec(2), spec(), spec()],
        out_specs=[spec(), pl.BlockSpec((GDN_GROUP, 1, 2, d, d), lambda h, n: (h, n, 0, 0, 0))],
        out_shape=[jax.ShapeDtypeStruct((t, N_HEADS * SLOT), F32), jax.ShapeDtypeStruct((N_HEADS, n_chunks, 2, d, d), F32)],
        scratch_shapes=[pltpu.VMEM((GDN_GROUP, d, d), F32)],
        compiler_params=pltpu.CompilerParams(dimension_semantics=("parallel", "arbitrary")),
    )(qkv, qkv, qkv, gb, bb)


def _gdn_bwd(qkv, gb, bb, keep, do):
    t = qkv.shape[0]
    n_chunks = t // GDN_CHUNK
    d = GDN_D

    def body(q_ref, k_ref, v_ref, g_ref, b_ref, keep_ref, do_ref, dqkv_ref, dg_ref, db_ref, ds_ref):
        @pl.when(pl.program_id(1) == 0)
        def _():
            ds_ref[...] = jnp.zeros_like(ds_ref)

        masks = _chunk_masks()
        tinv = keep_ref[:, 0, 1]
        _, pull = jax.vjp(lambda *a: _gdn_chunk(*a, masks, tinv)[:2],
                          *[_group_heads(r) for r in (q_ref, k_ref, v_ref, g_ref, b_ref)], keep_ref[:, 0, 0])
        dq, dk, dv, dg, db, ds = pull((_group_heads(do_ref), ds_ref[...]))
        ds_ref[...] = ds
        for i, val in enumerate((dq, dk, dv)):
            _ungroup_heads(dqkv_ref.at[i], val)
        _ungroup_heads(dg_ref, dg)
        _ungroup_heads(db_ref, db)

    def spec(kind=0):
        return pl.BlockSpec((GDN_CHUNK, GDN_GROUP * SLOT), lambda h, n: (n_chunks - 1 - n, kind * GDN_GROUPS + h))

    return pl.pallas_call(
        body, name="gdn_bwd",
        grid=(GDN_GROUPS, n_chunks),
        in_specs=[spec(0), spec(1), spec(2), spec(), spec(),
                  pl.BlockSpec((GDN_GROUP, 1, 2, d, d), lambda h, n: (h, n_chunks - 1 - n, 0, 0, 0)), spec()],
        out_specs=[pl.BlockSpec((3, GDN_CHUNK, GDN_GROUP * SLOT), lambda h, n: (0, n_chunks - 1 - n, h)), spec(), spec()],
        out_shape=[jax.ShapeDtypeStruct((3, t, N_HEADS * SLOT), F32)] + [jax.ShapeDtypeStruct((t, N_HEADS * SLOT), F32)] * 2,
        scratch_shapes=[pltpu.VMEM((GDN_GROUP, d, d), F32)],
        compiler_params=pltpu.CompilerParams(dimension_semantics=("parallel", "arbitrary")),
    )(qkv, qkv, qkv, gb, bb, keep, do)


def _rowwise(name, fn, rows, consts, outs, sums=(), tm=256):
    rows = [x if isinstance(x, tuple) else (x, x.shape[1], 0) for x in rows]
    t = rows[0][0].shape[0]
    tm = min(tm, t)
    steps = t // tm
    n_r, n_c, n_o, n_s = len(rows), len(consts), len(outs), len(sums)

    def window(width, block):
        return pl.BlockSpec((tm, width), lambda i: (i, block))

    def body(*refs):
        r, c = refs[:n_r], refs[n_r:n_r + n_c]
        o, s = refs[n_r + n_c:n_r + n_c + n_o], refs[n_r + n_c + n_o:]
        vals, tot = fn([x[...] for x in r], [x[...] for x in c])
        for ref, val in zip(o, vals):
            ref[...] = val.astype(ref.dtype)
        if n_s:
            @pl.when(pl.program_id(0) == 0)
            def _():
                for ref in s:
                    ref[...] = jnp.zeros_like(ref)

            for ref, val in zip(s, tot):
                ref[...] += val

    return pl.pallas_call(
        body, name=name,
        grid=(steps,),
        in_specs=[window(w, b) for _, w, b in rows] + [pl.BlockSpec(x.shape, lambda i: (0, 0)) for x in consts],
        out_specs=[pl.BlockSpec((tm, w), lambda i: (i, 0)) for w, _ in outs]
        + [pl.BlockSpec((1, w), lambda i: (0, 0)) for w in sums],
        out_shape=[jax.ShapeDtypeStruct((t, w), dt) for w, dt in outs]
        + [jax.ShapeDtypeStruct((1, w), F32) for w in sums],
        compiler_params=pltpu.CompilerParams(dimension_semantics=("arbitrary",)),
    )(*[x for x, _, _ in rows], *consts)


def _tile(dim, target):
    if dim <= target:
        return dim
    best = None
    for cand in range(128, target + 1, 128):
        if dim % cand == 0:
            best = cand
    assert best is not None, (dim, target)
    return best


def _matmul(name, a, b, mode, out_dtype=F32, tm=512, tn=1024, tk=1024):
    if mode == "nn":
        (m, k), n = a.shape, b.shape[1]
    elif mode == "nt":
        (m, k), n = a.shape, b.shape[0]
    else:
        (k, m), n = a.shape, b.shape[1]
    tm, tn, tk = _tile(m, tm), _tile(n, tn), _tile(k, tk)
    k_steps = k // tk
    product = {"nn": _nn, "nt": _nt, "tn": _tn}[mode]

    def body(a_ref, b_ref, o_ref, acc_ref):
        part = product(a_ref[...].astype(BF16), b_ref[...].astype(BF16))
        if k_steps == 1:
            o_ref[...] = part.astype(o_ref.dtype)
        else:
            kk = pl.program_id(2)

            @pl.when(kk == 0)
            def _():
                acc_ref[...] = part

            @pl.when(kk > 0)
            def _():
                acc_ref[...] += part

            @pl.when(kk == k_steps - 1)
            def _():
                o_ref[...] = acc_ref[...].astype(o_ref.dtype)

    a_spec = pl.BlockSpec((tk, tm), lambda i, j, kk: (kk, i)) if mode == "tn" else pl.BlockSpec((tm, tk), lambda i, j, kk: (i, kk))
    b_spec = pl.BlockSpec((tn, tk), lambda i, j, kk: (j, kk)) if mode == "nt" else pl.BlockSpec((tk, tn), lambda i, j, kk: (kk, j))
    return pl.pallas_call(
        body, name=name,
        grid=(m // tm, n // tn, k_steps),
        in_specs=[a_spec, b_spec],
        out_specs=pl.BlockSpec((tm, tn), lambda i, j, kk: (i, j)),
        out_shape=jax.ShapeDtypeStruct((m, n), out_dtype),
        scratch_shapes=[pltpu.VMEM((tm, tn) if k_steps > 1 else (8, 128), F32)],
        compiler_params=pltpu.CompilerParams(dimension_semantics=("parallel", "parallel", "arbitrary")),
    )(a, b)


FFN_TM = 512
FFN_BWD_TM = 256
FFN_TF = 1408


def _ffn_fwd(name, x, g_pre, w_gate, w_up, w_down, g_post):
    t, dm = x.shape
    f = w_gate.shape[1]
    tm, tf = min(FFN_TM, t), _tile(f, FFN_TF)
    f_steps = f // tf

    def body(x_ref, gpre_ref, wg_ref, wu_ref, wd_ref, gpost_ref, h_ref, y_ref, xn_ref, acc_ref):
        j = pl.program_id(1)

        @pl.when(j == 0)
        def _():
            xn_ref[...] = _rms(x_ref[...], gpre_ref[...], dm).astype(BF16)
            acc_ref[...] = jnp.zeros_like(acc_ref)

        xn = xn_ref[...]
        a = _silu(_nn(xn, wg_ref[...])) * _nn(xn, wu_ref[...])
        acc_ref[...] += _nn(a.astype(BF16), wd_ref[...])

        @pl.when(j == f_steps - 1)
        def _():
            h = acc_ref[...]
            h_ref[...] = h
            y_ref[...] = x_ref[...] + 0.5 * _rms(h, gpost_ref[...], dm)

    row = pl.BlockSpec((tm, dm), lambda i, j: (i, 0))
    vec = pl.BlockSpec((1, dm), lambda i, j: (0, 0))
    return pl.pallas_call(
        body, name=name,
        grid=(t // tm, f_steps),
        in_specs=[row, vec, pl.BlockSpec((dm, tf), lambda i, j: (0, j)), pl.BlockSpec((dm, tf), lambda i, j: (0, j)),
                  pl.BlockSpec((tf, dm), lambda i, j: (j, 0)), vec],
        out_specs=[row, row],
        out_shape=[jax.ShapeDtypeStruct((t, dm), F32)] * 2,
        scratch_shapes=[pltpu.VMEM((tm, dm), BF16), pltpu.VMEM((tm, dm), F32)],
        compiler_params=pltpu.CompilerParams(dimension_semantics=("parallel", "arbitrary")),
    )(x, g_pre, w_gate, w_up, w_down, g_post)


def _ffn_bwd(name, x, h, dy, g_pre, w_gate, w_up, w_down, g_post):
    t, dm = x.shape
    f = w_gate.shape[1]
    tm, tf = min(FFN_BWD_TM, t), _tile(f, FFN_TF)
    f_steps = f // tf

    def post(hv, g):
        return 0.5 * _rms(hv, g, dm)

    def pre(xv, g):
        return _rms(xv, g, dm)

    def body(x_ref, h_ref, dy_ref, gpre_ref, wg_ref, wu_ref, wd_ref, gpost_ref,
             dx_ref, xn_ref, dh_ref, a_ref, dhg_ref, dhu_ref, dgpre_ref, dgpost_ref, acc_ref):
        i, j = pl.program_id(0), pl.program_id(1)

        @pl.when((i == 0) & (j == 0))
        def _():
            dgpre_ref[...] = jnp.zeros_like(dgpre_ref)
            dgpost_ref[...] = jnp.zeros_like(dgpost_ref)

        @pl.when(j == 0)
        def _():
            xn_ref[...] = pre(x_ref[...], gpre_ref[...]).astype(BF16)
            _, pull = jax.vjp(post, h_ref[...], gpost_ref[...])
            dh, dg = pull(dy_ref[...])
            dh_ref[...] = dh.astype(BF16)
            dgpost_ref[...] += dg
            acc_ref[...] = jnp.zeros_like(acc_ref)

        xn = xn_ref[...]
        hg = _nn(xn, wg_ref[...])
        hu = _nn(xn, wu_ref[...])
        da = _nt(dh_ref[...], wd_ref[...])
        sig = _sigmoid(hg)
        act = hg * sig
        dhu = (da * act).astype(BF16)
        dhg = (da * hu * (sig * (1.0 + hg * (1.0 - sig)))).astype(BF16)
        a_ref[...] = (act * hu).astype(BF16)
        dhg_ref[...] = dhg
        dhu_ref[...] = dhu
        acc_ref[...] += _nt(dhg, wg_ref[...]) + _nt(dhu, wu_ref[...])

        @pl.when(j == f_steps - 1)
        def _():
            _, pull = jax.vjp(pre, x_ref[...], gpre_ref[...])
            dx, dg = pull(acc_ref[...])
            dx_ref[...] = dy_ref[...] + dx
            dgpre_ref[...] += dg

    row = pl.BlockSpec((tm, dm), lambda i, j: (i, 0))
    vec = pl.BlockSpec((1, dm), lambda i, j: (0, 0))
    wide = pl.BlockSpec((tm, tf), lambda i, j: (i, j))
    return pl.pallas_call(
        body, name=name,
        grid=(t // tm, f_steps),
        in_specs=[row, row, row, vec, pl.BlockSpec((dm, tf), lambda i, j: (0, j)), pl.BlockSpec((dm, tf), lambda i, j: (0, j)),
                  pl.BlockSpec((tf, dm), lambda i, j: (j, 0)), vec],
        out_specs=[row, row, row, wide, wide, wide, vec, vec],
        out_shape=[jax.ShapeDtypeStruct((t, dm), F32), jax.ShapeDtypeStruct((t, dm), BF16), jax.ShapeDtypeStruct((t, dm), BF16),
                   jax.ShapeDtypeStruct((t, f), BF16), jax.ShapeDtypeStruct((t, f), BF16), jax.ShapeDtypeStruct((t, f), BF16),
                   jax.ShapeDtypeStruct((1, dm), F32), jax.ShapeDtypeStruct((1, dm), F32)],
        scratch_shapes=[pltpu.VMEM((tm, dm), F32)],
        compiler_params=pltpu.CompilerParams(dimension_semantics=("arbitrary", "arbitrary")),
    )(x, h, dy, g_pre, w_gate, w_up, w_down, g_post)


ATT_T = 512
ATT_GROUP = 2
ATT_SCALE = (MLA_NOPE + MLA_ROPE) ** -0.5


def _stack_slots(ref, group):
    return jnp.stack([ref[:, pl.ds(j * SLOT, SLOT)] for j in range(group)])


def _unstack_slots(ref, val):
    for j in range(val.shape[0]):
        ref[:, pl.ds(j * SLOT, SLOT)] = val[j].astype(ref.dtype)


def _scores(q, k, diagonal):
    s = _nt(q, k) * ATT_SCALE
    if diagonal:
        row = lax.broadcasted_iota(jnp.int32, s.shape[1:], 0)
        col = lax.broadcasted_iota(jnp.int32, s.shape[1:], 1)
        s = jnp.where(col <= row, s, -1e30)
    return s


def _attn_specs(tile, q_major):
    width = ATT_GROUP * SLOT
    if q_major:
        return (pl.BlockSpec((tile, width), lambda h, qi, ki: (qi, h)),
                pl.BlockSpec((tile, width), lambda h, qi, ki: (jnp.minimum(ki, qi), h)))
    return (pl.BlockSpec((tile, width), lambda h, ki, qi: (jnp.maximum(qi, ki), h)),
            pl.BlockSpec((tile, width), lambda h, ki, qi: (ki, h)))


def _attn_fwd(q, k, v):
    t = q.shape[0]
    tile = min(ATT_T, t)
    steps = t // tile
    g = ATT_GROUP

    def body(q_ref, k_ref, v_ref, o_ref, lse_ref, m_ref, l_ref, acc_ref):
        qi, ki = pl.program_id(1), pl.program_id(2)

        @pl.when(ki == 0)
        def _():
            m_ref[...] = jnp.full_like(m_ref, -1e30)
            l_ref[...] = jnp.zeros_like(l_ref)
            acc_ref[...] = jnp.zeros_like(acc_ref)

        def step(diagonal):
            s = _scores(_stack_slots(q_ref, g), _stack_slots(k_ref, g), diagonal)
            m_old = m_ref[...]
            m_new = jnp.maximum(m_old, jnp.max(s, axis=-1, keepdims=True))
            p = jnp.exp(s - m_new)
            alpha = jnp.exp(m_old - m_new)
            l_ref[...] = alpha * l_ref[...] + jnp.sum(p, axis=-1, keepdims=True)
            acc_ref[...] = alpha * acc_ref[...] + _nn(p.astype(BF16), _stack_slots(v_ref, g))
            m_ref[...] = m_new

        @pl.when(ki < qi)
        def _():
            step(False)

        @pl.when(ki == qi)
        def _():
            step(True)
            _unstack_slots(o_ref, acc_ref[...] / l_ref[...])
            _unstack_slots(lse_ref, jnp.broadcast_to(m_ref[...] + jnp.log(l_ref[...]), acc_ref.shape))

    q_spec, k_spec = _attn_specs(tile, True)
    return pl.pallas_call(
        body, name="attn_fwd",
        grid=(N_HEADS // g, steps, steps),
        in_specs=[q_spec, k_spec, k_spec],
        out_specs=[q_spec, q_spec],
        out_shape=[jax.ShapeDtypeStruct((t, N_HEADS * SLOT), F32)] * 2,
        scratch_shapes=[pltpu.VMEM((g, tile, 1), F32), pltpu.VMEM((g, tile, 1), F32), pltpu.VMEM((g, tile, SLOT), F32)],
        compiler_params=pltpu.CompilerParams(dimension_semantics=("parallel", "parallel", "arbitrary")),
    )(q, k, v)


def _attn_grad_scores(q, k, v, do, lse_ref, delta_ref, diagonal):
    g = ATT_GROUP
    p = jnp.exp(_scores(q, k, diagonal) - _stack_slots(lse_ref, g)[:, :, 0:1])
    dp = _nt(do, v)
    return p, p * (dp - _stack_slots(delta_ref, g)[:, :, 0:1]) * ATT_SCALE


def _attn_bwd_q(q, k, v, do, lse, delta):
    t = q.shape[0]
    tile = min(ATT_T, t)
    steps = t // tile
    g = ATT_GROUP

    def body(q_ref, k_ref, v_ref, do_ref, lse_ref, delta_ref, dq_ref, acc_ref):
        qi, ki = pl.program_id(1), pl.program_id(2)

        @pl.when(ki == 0)
        def _():
            acc_ref[...] = jnp.zeros_like(acc_ref)

        def step(diagonal):
            kk = _stack_slots(k_ref, g)
            _, ds = _attn_grad_scores(_stack_slots(q_ref, g), kk, _stack_slots(v_ref, g),
                                      _stack_slots(do_ref, g).astype(BF16), lse_ref, delta_ref, diagonal)
            acc_ref[...] += _nn(ds.astype(BF16), kk)

        @pl.when(ki < qi)
        def _():
            step(False)

        @pl.when(ki == qi)
        def _():
            step(True)
            _unstack_slots(dq_ref, acc_ref[...])

    q_spec, k_spec = _attn_specs(tile, True)
    return pl.pallas_call(
        body, name="attn_bwd_q",
        grid=(N_HEADS // g, steps, steps),
        in_specs=[q_spec, k_spec, k_spec, q_spec, q_spec, q_spec],
        out_specs=q_spec,
        out_shape=jax.ShapeDtypeStruct((t, N_HEADS * SLOT), F32),
        scratch_shapes=[pltpu.VMEM((g, tile, SLOT), F32)],
        compiler_params=pltpu.CompilerParams(dimension_semantics=("parallel", "parallel", "arbitrary")),
    )(q, k, v, do, lse, delta)


def _attn_bwd_kv(q, k, v, do, lse, delta):
    t = q.shape[0]
    tile = min(ATT_T, t)
    steps = t // tile
    g = ATT_GROUP

    def body(q_ref, k_ref, v_ref, do_ref, lse_ref, delta_ref, dk_ref, dv_ref, dk_acc, dv_acc):
        ki, qi = pl.program_id(1), pl.program_id(2)

        @pl.when(qi == 0)
        def _():
            dk_acc[...] = jnp.zeros_like(dk_acc)
            dv_acc[...] = jnp.zeros_like(dv_acc)

        def step(diagonal):
            qq = _stack_slots(q_ref, g)
            do_b = _stack_slots(do_ref, g).astype(BF16)
            p, ds = _attn_grad_scores(qq, _stack_slots(k_ref, g), _stack_slots(v_ref, g), do_b, lse_ref, delta_ref, diagonal)
            dv_acc[...] += _tn(p.astype(BF16), do_b)
            dk_acc[...] += _tn(ds.astype(BF16), qq)

        @pl.when(qi > ki)
        def _():
            step(False)

        @pl.when(qi == ki)
        def _():
            step(True)

        @pl.when(qi == steps - 1)
        def _():
            _unstack_slots(dk_ref, dk_acc[...])
            _unstack_slots(dv_ref, dv_acc[...])

    q_spec, k_spec = _attn_specs(tile, False)
    return pl.pallas_call(
        body, name="attn_bwd_kv",
        grid=(N_HEADS // g, steps, steps),
        in_specs=[q_spec, k_spec, k_spec, q_spec, q_spec, q_spec],
        out_specs=[k_spec, k_spec],
        out_shape=[jax.ShapeDtypeStruct((t, N_HEADS * SLOT), F32)] * 2,
        scratch_shapes=[pltpu.VMEM((g, tile, SLOT), F32), pltpu.VMEM((g, tile, SLOT), F32)],
        compiler_params=pltpu.CompilerParams(dimension_semantics=("parallel", "parallel", "arbitrary")),
    )(q, k, v, do, lse, delta)


def _shift_down(x, s):
    if s == 0:
        return x
    row = lax.broadcasted_iota(jnp.int32, x.shape, 0)
    return jnp.where(row >= s, pltpu.roll(x, s, 0), 0.0)


def _shift_up(x, s):
    if s == 0:
        return x
    n = x.shape[0]
    row = lax.broadcasted_iota(jnp.int32, x.shape, 0)
    return jnp.where(row < n - s, pltpu.roll(x, n - s, 0), 0.0)


def _l2norm(x):
    return x * lax.rsqrt(jnp.sum(x * x, axis=-1, keepdims=True) + EPS)


def _conv_pre(x, w):
    y = w[GDN_CONV - 1:GDN_CONV, :] * x
    for s in range(1, GDN_CONV):
        y = y + w[GDN_CONV - 1 - s:GDN_CONV - s, :] * _shift_down(x, s)
    return y


def _gdn_conv_fwd(x, w):
    t, width = x.shape

    def body(x_ref, w_ref, o_ref):
        act = _silu(_conv_pre(x_ref[...], w_ref[...]))
        normed = pl.program_id(0) < 2 * N_HEADS
        o_ref[...] = jnp.where(normed, _l2norm(act), act)

    return pl.pallas_call(
        body, name="gdn_conv_fwd",
        grid=(width // SLOT,),
        in_specs=[pl.BlockSpec((t, SLOT), lambda j: (0, j)), pl.BlockSpec((GDN_CONV, SLOT), lambda j: (0, j))],
        out_specs=pl.BlockSpec((t, SLOT), lambda j: (0, j)),
        out_shape=jax.ShapeDtypeStruct((t, width), F32),
        compiler_params=pltpu.CompilerParams(dimension_semantics=("parallel",)),
    )(x, w)


def _gdn_conv_bwd(x, w, dout):
    t, width = x.shape

    def body(x_ref, w_ref, do_ref, dx_ref, dw_ref):
        xv, wv = x_ref[...], w_ref[...]
        y = _conv_pre(xv, wv)
        sig = _sigmoid(y)
        act = y * sig
        _, pull = jax.vjp(_l2norm, act)
        normed = pl.program_id(0) < 2 * N_HEADS
        dact = jnp.where(normed, pull(do_ref[0])[0], do_ref[0])
        dy = dact * (sig * (1.0 + y * (1.0 - sig)))
        dx = wv[GDN_CONV - 1:GDN_CONV, :] * dy
        for s in range(1, GDN_CONV):
            dx = dx + wv[GDN_CONV - 1 - s:GDN_CONV - s, :] * _shift_up(dy, s)
        dx_ref[...] = dx.astype(BF16)
        for s in range(GDN_CONV):
            dw_ref[GDN_CONV - 1 - s:GDN_CONV - s, :] = jnp.sum(dy * _shift_down(xv, s), axis=0, keepdims=True)

    col = pl.BlockSpec((t, SLOT), lambda j: (0, j))
    tap = pl.BlockSpec((GDN_CONV, SLOT), lambda j: (0, j))
    return pl.pallas_call(
        body, name="gdn_conv_bwd",
        grid=(width // SLOT,),
        in_specs=[col, tap, pl.BlockSpec((1, t, SLOT), lambda j: (j // N_HEADS, 0, j % N_HEADS))],
        out_specs=[col, tap],
        out_shape=[jax.ShapeDtypeStruct((t, width), BF16), jax.ShapeDtypeStruct((GDN_CONV, width), F32)],
        compiler_params=pltpu.CompilerParams(dimension_semantics=("parallel",)),
    )(x, w, dout)


def _softplus(x):
    e = jnp.exp(-jnp.abs(x))
    u = 1.0 + e
    log1p = jnp.where(u == 1.0, e, jnp.log(u) * e / jnp.where(u == 1.0, 1.0, u - 1.0))
    return jnp.maximum(x, 0.0) + log1p


def _gates_fwd(ab, a_log, dt_bias):
    def fn(rows, consts):
        (abv,), (alog, dtb) = rows, consts
        g = -jnp.exp(alog) * _softplus(abv + dtb)
        beta = _sigmoid(abv)
        shape = (abv.shape[0], SLOT)
        g_slots = [jnp.broadcast_to(g[:, h:h + 1], shape) for h in range(N_HEADS)]
        b_slots = [jnp.broadcast_to(beta[:, N_HEADS + h:N_HEADS + h + 1], shape) for h in range(N_HEADS)]
        return [jnp.concatenate(g_slots, axis=1), jnp.concatenate(b_slots, axis=1)], []

    width = N_HEADS * SLOT
    return _rowwise("gdn_gates_fwd", fn, [ab], [a_log, dt_bias], [(width, F32), (width, F32)])


def _gates_bwd(ab, a_log, dt_bias, dg, dbeta):
    def fn(rows, consts):
        (abv, dgv, dbv), (alog, dtb) = rows, consts
        lane = lax.broadcasted_iota(jnp.int32, abv.shape, 1)
        dg_tok = jnp.zeros_like(abv)
        db_tok = jnp.zeros_like(abv)
        for h in range(N_HEADS):
            dg_tok = dg_tok + jnp.where(lane == h, jnp.sum(dgv[:, h * SLOT:(h + 1) * SLOT], axis=1, keepdims=True), 0.0)
            db_tok = db_tok + jnp.where(lane == N_HEADS + h, jnp.sum(dbv[:, h * SLOT:(h + 1) * SLOT], axis=1, keepdims=True), 0.0)
        xa = abv + dtb
        g = -jnp.exp(alog) * _softplus(xa)
        da = dg_tok * (-jnp.exp(alog)) * _sigmoid(xa)
        beta = _sigmoid(abv)
        dab = jnp.where(lane < N_HEADS, da, db_tok * beta * (1.0 - beta))
        dab = jnp.where(lane < 2 * N_HEADS, dab, 0.0)
        d_alog = jnp.sum(jnp.where(lane < N_HEADS, dg_tok * g, 0.0), axis=0, keepdims=True)
        d_dtb = jnp.sum(jnp.where(lane < N_HEADS, da, 0.0), axis=0, keepdims=True)
        return [dab], [d_alog, d_dtb]

    return _rowwise("gdn_gates_bwd", fn, [ab, dg, dbeta], [a_log, dt_bias], [(SLOT, F32)], sums=[SLOT, SLOT])


ROPE_HALF = MLA_ROPE // 2


def _rope_tables(positions):
    freqs = ROPE_THETA ** (-jnp.arange(ROPE_HALF, dtype=F32) / ROPE_HALF)
    ang = positions.astype(F32)[:, None] * freqs
    cos, sin = jnp.cos(ang), jnp.sin(ang)
    t = positions.shape[0]
    ones, zeros = jnp.ones((t, MLA_NOPE), F32), jnp.zeros((t, MLA_NOPE), F32)
    tail = jnp.zeros((t, SLOT - MLA_NOPE - MLA_ROPE), F32)
    half0 = jnp.zeros((t, ROPE_HALF), F32)
    same = jnp.concatenate([ones, cos, cos, tail], axis=1)
    from_low = jnp.concatenate([zeros, half0, sin, tail], axis=1)
    from_high = jnp.concatenate([zeros, -sin, half0, tail], axis=1)
    return same, from_low, from_high


def _rope(x, tabs):
    same, from_low, from_high = tabs
    width = x.shape[1]
    return x * same + pltpu.roll(x, ROPE_HALF, 1) * from_low + pltpu.roll(x, width - ROPE_HALF, 1) * from_high


def _rope_transposed(dy, tabs):
    same, from_low, from_high = tabs
    width = dy.shape[1]
    return dy * same + pltpu.roll(dy * from_low, width - ROPE_HALF, 1) + pltpu.roll(dy * from_high, ROPE_HALF, 1)


def _tile_slots(tab):
    return jnp.concatenate([tab] * N_HEADS, axis=1)


A_WIDTH = MLA_Q_RANK + MLA_KV_RANK + 2 * SLOT
A_KPE = MLA_Q_RANK + MLA_KV_RANK
A_AB = A_KPE + SLOT
WIDE = N_HEADS * SLOT


def _mla_pre_fwd(proj_a, tabs, g_q, g_kv):
    def fn(rows, consts):
        pa, *tb = rows
        gq, gkv = consts
        return [_rms(pa[:, :MLA_Q_RANK], gq, MLA_Q_RANK), _rms(pa[:, MLA_Q_RANK:A_KPE], gkv, MLA_KV_RANK),
                _rope(pa[:, A_KPE:A_AB], tb)], []

    return _rowwise("mla_pre_fwd", fn, [proj_a, *tabs], [g_q, g_kv], [(MLA_Q_RANK, BF16), (MLA_KV_RANK, BF16), (SLOT, F32)])


def _mla_pre_bwd(proj_a, tabs, g_q, g_kv, dcqn, dckvn, dkpe, dab):
    def fn(rows, consts):
        pa, t0, t1, t2, dq, dkv, dk, da = rows
        gq, gkv = consts
        _, pull_q = jax.vjp(lambda x, g: _rms(x, g, MLA_Q_RANK), pa[:, :MLA_Q_RANK], gq)
        _, pull_kv = jax.vjp(lambda x, g: _rms(x, g, MLA_KV_RANK), pa[:, MLA_Q_RANK:A_KPE], gkv)
        dcq, dgq = pull_q(dq)
        dckv, dgkv = pull_kv(dkv)
        return [jnp.concatenate([dcq, dckv, _rope_transposed(dk, (t0, t1, t2)), da], axis=1)], [dgq, dgkv]

    return _rowwise("mla_pre_bwd", fn, [proj_a, *tabs, dcqn, dckvn, dkpe, dab], [g_q, g_kv], [(A_WIDTH, BF16)],
                    sums=[MLA_Q_RANK, MLA_KV_RANK])


def _mla_qkv_fwd(q_p, kv_p, kpe, tabs):
    def fn(rows, consts):
        qv, kvv, kp, *tb = rows
        q = _rope(qv, [_tile_slots(x) for x in tb])
        k = kvv[:, :WIDE] + _tile_slots(kp)
        return [q, k, kvv[:, WIDE:]], []

    return _rowwise("mla_qkv_fwd", fn, [q_p, kv_p, kpe, *tabs], [], [(WIDE, BF16)] * 3)


def _mla_qkv_bwd(dq, dk, dv, tabs):
    def fn(rows, consts):
        dqv, dkv, dvv, *tb = rows
        dkpe = dkv[:, :SLOT]
        for h in range(1, N_HEADS):
            dkpe = dkpe + dkv[:, h * SLOT:(h + 1) * SLOT]
        return [_rope_transposed(dqv, [_tile_slots(x) for x in tb]), jnp.concatenate([dkv, dvv], axis=1), dkpe], []

    return _rowwise("mla_qkv_bwd", fn, [dq, dk, dv, *tabs], [], [(WIDE, BF16), (2 * WIDE, BF16), (SLOT, F32)])


def _slot_sum(x):
    parts = [jnp.broadcast_to(jnp.sum(x[:, h * SLOT:(h + 1) * SLOT], axis=1, keepdims=True), (x.shape[0], SLOT))
             for h in range(N_HEADS)]
    return jnp.concatenate(parts, axis=1)


def _mix_join(o_mla, o_gdn, gate, g_mla, g_gdn):
    mla = _rms(o_mla, g_mla, N_HEADS * MLA_V)
    gdn = o_gdn * lax.rsqrt(_slot_sum(o_gdn * o_gdn) * (1.0 / GDN_D) + EPS) * g_gdn * _silu(gate)
    return mla, gdn


def _mix_join_fwd(o_mla, o_gdn, gate, g_mla, g_gdn):
    def fn(rows, consts):
        return [jnp.concatenate(_mix_join(*rows, *consts), axis=1)], []

    return _rowwise("mix_join_fwd", fn, [o_mla, o_gdn, gate], [g_mla, g_gdn], [(2 * WIDE, BF16)])


def _mix_join_bwd(o_mla, o_gdn, gate, g_mla, g_gdn, dcat):
    def fn(rows, consts):
        om, og, gt, dc = rows
        gm, gg = consts
        _, pull = jax.vjp(lambda x, g: _rms(x, g, N_HEADS * MLA_V), om, gm)
        dom, dgm = pull(dc[:, :WIDE])
        dy = dc[:, WIDE:]
        r = lax.rsqrt(_slot_sum(og * og) * (1.0 / GDN_D) + EPS)
        sig = _sigmoid(gt)
        normed = og * r
        dn = dy * gg * (gt * sig)
        dog = r * dn - normed * (r * r) * _slot_sum(dn * og) * (1.0 / GDN_D)
        dgt = dy * normed * gg * (sig * (1.0 + gt * (1.0 - sig)))
        dgg = jnp.sum(dy * normed * (gt * sig), axis=0, keepdims=True)
        return [dom, _slot_sum(dom * om), dog, dgt], [dgm, dgg]

    return _rowwise("mix_join_bwd", fn, [o_mla, o_gdn, gate, dcat], [g_mla, g_gdn],
                    [(WIDE, F32), (WIDE, F32), (WIDE, F32), (WIDE, BF16)], sums=[WIDE, WIDE])


def _norm_residual_fwd(name, x, h, g, out_dtypes):
    dm = x.shape[1]

    def fn(rows, consts):
        y = rows[0] + _rms(rows[1], consts[0], dm)
        return [y] + [_rms(y, gg, dm) for gg in consts[1:]], []

    return _rowwise(name, fn, [x, h], list(g), [(dm, dt) for dt in out_dtypes])


def _norm_residual_bwd(name, h, g, dy):
    dm = h.shape[1]

    def fn(rows, consts):
        _, pull = jax.vjp(lambda hv, gv: _rms(hv, gv, dm), rows[0], consts[0])
        dh, dg = pull(rows[1])
        return [dh], [dg]

    return _rowwise(name, fn, [h, dy], [g], [(dm, BF16)], sums=[dm])


def _norm_bwd_add(name, x, g, dns, dy):
    dm = x.shape[1]

    def fn(rows, consts):
        xv, dyv, *parts = rows
        dn = parts[0]
        for p in parts[1:]:
            dn = dn + p
        _, pull = jax.vjp(lambda a, gv: _rms(a, gv, dm), xv, consts[0])
        dx, dg = pull(dn)
        return [dyv + dx], [dg]

    return _rowwise(name, fn, [x, dy, *dns], [g], [(dm, F32)], sums=[dm])


def _loss_fwd(y, target):
    dm = y.shape[1]

    def fn(rows, consts):
        err = rows[0] - rows[1]
        sq = err * err
        lanes = sq[:, :SLOT]
        for j in range(1, dm // SLOT):
            lanes = lanes + sq[:, j * SLOT:(j + 1) * SLOT]
        return [err * (1.0 / dm)], [jnp.sum(lanes, axis=0, keepdims=True) * (0.5 / dm)]

    return _rowwise("loss", fn, [y, target], [], [(dm, F32)], sums=[SLOT])


def _norm_fwd(name, x, g):
    dm = x.shape[1]
    return _rowwise(name, lambda rows, consts: ([_rms(rows[0], consts[0], dm)], []), [x], [g], [(dm, BF16)])[0]


W_IN_CUTS = (0, 256, 384, 416, 1952, 1960, 1968, 2480)


def _heads_out(w, per_head, axis=-1):
    w = jnp.moveaxis(w, axis, -1)
    lead = w.shape[:-1]
    w = w.reshape(lead + (w.shape[-1] // per_head, per_head))
    w = jnp.pad(w, [(0, 0)] * len(lead) + [(0, 0), (0, SLOT - per_head)])
    return jnp.moveaxis(w.reshape(lead + (-1,)), -1, axis)


def _heads_in(w, per_head, axis=-1):
    w = jnp.moveaxis(w, axis, -1)
    lead = w.shape[:-1]
    w = w.reshape(lead + (w.shape[-1] // SLOT, SLOT))[..., :per_head]
    return jnp.moveaxis(w.reshape(lead + (-1,)), -1, axis)


def _pad_lanes(v, lo, width=SLOT):
    return jnp.pad(v, [(0, 0)] * (v.ndim - 1) + [(lo, width - lo - v.shape[-1])])


def _layout_weights(w):
    c = W_IN_CUTS
    w_in = w["w_in"]
    p = {}
    p["w_a"] = jnp.concatenate([w_in[:, c[0]:c[2]], _pad_lanes(w_in[:, c[2]:c[3]], MLA_NOPE),
                                _pad_lanes(w_in[:, c[4]:c[6]], 0)], axis=1)
    p["w_qkv"] = _heads_out(w_in[:, c[3]:c[4]], GDN_D)
    p["w_gate"] = _heads_out(w_in[:, c[6]:c[7]], GDN_D)
    p["w_uq"] = _heads_out(w["mla_w_uq"], MLA_NOPE + MLA_ROPE)
    ukv = w["mla_w_ukv"].reshape(MLA_KV_RANK, N_HEADS, MLA_NOPE + MLA_V)
    p["w_kv"] = jnp.concatenate([_heads_out(ukv[:, :, :MLA_NOPE].reshape(MLA_KV_RANK, -1), MLA_NOPE),
                                 _heads_out(ukv[:, :, MLA_NOPE:].reshape(MLA_KV_RANK, -1), MLA_V)], axis=1)
    p["w_out"] = _heads_out(w["w_out"], GDN_D, axis=0)
    p["conv"] = _heads_out(w["gdn_conv_w"], GDN_D)
    p["g_mla_out"] = _heads_out(w["mla_out_g"], MLA_V)
    p["g_gdn"] = jnp.tile(_pad_lanes(w["gdn_norm_g"], 0), (1, N_HEADS))
    p["a_log"] = _pad_lanes(w["gdn_a_log"], 0)
    p["dt_bias"] = _pad_lanes(w["gdn_dt_bias"], 0)
    return p


def _unlayout_grads(d):
    c = W_IN_CUTS
    g = {}
    da = d["w_a"]
    kpe0 = A_KPE + MLA_NOPE
    g["w_in"] = jnp.concatenate([da[:, :A_KPE], da[:, kpe0:kpe0 + MLA_ROPE], _heads_in(d["w_qkv"], GDN_D),
                                 da[:, A_AB:A_AB + 2 * N_HEADS], _heads_in(d["w_gate"], GDN_D)], axis=1)
    assert g["w_in"].shape[1] == c[-1]
    g["mla_w_uq"] = _heads_in(d["w_uq"], MLA_NOPE + MLA_ROPE)
    dk = _heads_in(d["w_kv"][:, :WIDE], MLA_NOPE).reshape(MLA_KV_RANK, N_HEADS, MLA_NOPE)
    dv = _heads_in(d["w_kv"][:, WIDE:], MLA_V).reshape(MLA_KV_RANK, N_HEADS, MLA_V)
    g["mla_w_ukv"] = jnp.concatenate([dk, dv], axis=2).reshape(MLA_KV_RANK, -1)
    g["w_out"] = _heads_in(d["w_out"], GDN_D, axis=0)
    g["gdn_conv_w"] = _heads_in(d["conv"], GDN_D)
    g["mla_out_g"] = _heads_in(d["g_mla_out"], MLA_V)
    g["gdn_norm_g"] = jnp.sum(d["g_gdn"].reshape(N_HEADS, SLOT), axis=0, keepdims=True)[:, :GDN_D]
    g["gdn_a_log"] = d["a_log"][:, :N_HEADS]
    g["gdn_dt_bias"] = d["dt_bias"][:, :N_HEADS]
    return g


def _weight_grad(name, acts, cots, tm=1024, tn=1408, tk=512):
    return _matmul(name, acts, cots, "tn", tm=tm, tn=tn, tk=tk)


def _local_step(x, positions, target, w):
    p = _layout_weights(w)
    tabs = _rope_tables(positions)

    h1, x1 = _ffn_fwd("ffn1_fwd", x, w["ffn1_pre_g"], w["ffn1_w_gate"], w["ffn1_w_up"], w["ffn1_w_down"], w["ffn1_post_g"])
    hn = _norm_fwd("mix_pre_norm", x1, w["mix_pre_g"])
    proj_a = _matmul("proj_a", hn, p["w_a"], "nn")
    proj_qkv = _matmul("proj_qkv", hn, p["w_qkv"], "nn")
    proj_gate = _matmul("proj_gate", hn, p["w_gate"], "nn")
    cqn, ckvn, kpe = _mla_pre_fwd(proj_a, tabs, w["mla_q_norm_g"], w["mla_kv_norm_g"])
    q_p = _matmul("mla_q", cqn, p["w_uq"], "nn")
    kv_p = _matmul("mla_kv", ckvn, p["w_kv"], "nn")
    q, k, v = _mla_qkv_fwd(q_p, kv_p, kpe, tabs)
    o_mla, lse = _attn_fwd(q, k, v)
    ab = (proj_a, SLOT, A_AB // SLOT)
    qkv_n = _gdn_conv_fwd(proj_qkv, p["conv"])
    gb, bb = _gates_fwd(ab, p["a_log"], p["dt_bias"])
    o_gdn, keep = _gdn_fwd(qkv_n, gb, bb)
    cat = _mix_join_fwd(o_mla, o_gdn, proj_gate, p["g_mla_out"], p["g_gdn"])[0]
    mixed = _matmul("mix_out", cat, p["w_out"], "nn")
    x2 = _norm_residual_fwd("mix_post", x1, mixed, [w["mix_post_g"]], [F32])[0]
    h2, y = _ffn_fwd("ffn2_fwd", x2, w["ffn2_pre_g"], w["ffn2_w_gate"], w["ffn2_w_up"], w["ffn2_w_down"], w["ffn2_post_g"])
    dy, loss_lanes = _loss_fwd(y, target)

    g = {}
    dx2, xn2, dh2, a2, dhg2, dhu2, g["ffn2_pre_g"], g["ffn2_post_g"] = _ffn_bwd(
        "ffn2_bwd", x2, h2, dy, w["ffn2_pre_g"], w["ffn2_w_gate"], w["ffn2_w_up"], w["ffn2_w_down"], w["ffn2_post_g"])
    g["ffn2_w_gate"] = _weight_grad("ffn2_dw_gate", xn2, dhg2)
    g["ffn2_w_up"] = _weight_grad("ffn2_dw_up", xn2, dhu2)
    g["ffn2_w_down"] = _weight_grad("ffn2_dw_down", a2, dh2, tm=1408, tn=1024)
    dmixed, g["mix_post_g"] = _norm_residual_bwd("mix_post_bwd", mixed, w["mix_post_g"], dx2)
    dcat = _matmul("mix_out_dx", dmixed, p["w_out"], "nt")
    d = {}
    d["w_out"] = _weight_grad("mix_out_dw", cat, dmixed, tn=1024)
    do_mla, delta, do_gdn, dgate, d["g_mla_out"], d["g_gdn"] = _mix_join_bwd(o_mla, o_gdn, proj_gate, p["g_mla_out"], p["g_gdn"], dcat)
    dq = _attn_bwd_q(q, k, v, do_mla, lse, delta)
    dk, dv = _attn_bwd_kv(q, k, v, do_mla, lse, delta)
    dq_p, dkv_p, dkpe = _mla_qkv_bwd(dq, dk, dv, tabs)
    dcqn = _matmul("mla_q_dx", dq_p, p["w_uq"], "nt")
    d["w_uq"] = _weight_grad("mla_q_dw", cqn, dq_p, tn=1024)
    dckvn = _matmul("mla_kv_dx", dkv_p, p["w_kv"], "nt")
    d["w_kv"] = _weight_grad("mla_kv_dw", ckvn, dkv_p, tn=1024)
    dqkv_n, dgb, dbb = _gdn_bwd(qkv_n, gb, bb, keep, do_gdn)
    dab, d["a_log"], d["dt_bias"] = _gates_bwd(ab, p["a_log"], p["dt_bias"], dgb, dbb)
    dproj_qkv, d["conv"] = _gdn_conv_bwd(proj_qkv, p["conv"], dqkv_n)
    dproj_a, g["mla_q_norm_g"], g["mla_kv_norm_g"] = _mla_pre_bwd(
        proj_a, tabs, w["mla_q_norm_g"], w["mla_kv_norm_g"], dcqn, dckvn, dkpe, dab)
    dhn = [_matmul("proj_a_dx", dproj_a, p["w_a"], "nt"), _matmul("proj_qkv_dx", dproj_qkv, p["w_qkv"], "nt"),
           _matmul("proj_gate_dx", dgate, p["w_gate"], "nt")]
    d["w_a"] = _weight_grad("proj_a_dw", hn, dproj_a, tn=640)
    d["w_qkv"] = _weight_grad("proj_qkv_dw", hn, dproj_qkv, tn=1024)
    d["w_gate"] = _weight_grad("proj_gate_dw", hn, dgate, tn=1024)
    dx1, g["mix_pre_g"] = _norm_bwd_add("mix_pre_bwd", x1, w["mix_pre_g"], dhn, dx2)
    dx, xn1, dh1, a1, dhg1, dhu1, g["ffn1_pre_g"], g["ffn1_post_g"] = _ffn_bwd(
        "ffn1_bwd", x, h1, dx1, w["ffn1_pre_g"], w["ffn1_w_gate"], w["ffn1_w_up"], w["ffn1_w_down"], w["ffn1_post_g"])
    g["ffn1_w_gate"] = _weight_grad("ffn1_dw_gate", xn1, dhg1)
    g["ffn1_w_up"] = _weight_grad("ffn1_dw_up", xn1, dhu1)
    g["ffn1_w_down"] = _weight_grad("ffn1_dw_down", a1, dh1, tm=1408, tn=1024)
    g.update(_unlayout_grads(d))
    return loss_lanes, dx, g


MESH_AXES = ("x", "y", "c")


def _peer(r):
    x, y, c = (lax.axis_index(a) for a in MESH_AXES)
    return (1 - x if r & 4 else x, 1 - y if r & 2 else y, 1 - c if r & 1 else c)


def _block_of(dev):
    x, y, c = dev
    return 4 * x + 2 * y + c


def _exchange(name, buf, gather):
    shape = buf.shape[-2:]

    def body(x_ref, out_ref, send_sems, recv_sems, local_sem):
        me = _block_of(_peer(0))
        mine = x_ref if gather else x_ref.at[me]
        local = pltpu.make_async_copy(mine, out_ref.at[me], local_sem)
        local.start()

        def copy(r):
            src = x_ref if gather else x_ref.at[_block_of(_peer(r))]
            return pltpu.make_async_remote_copy(
                src_ref=src, dst_ref=out_ref.at[me], send_sem=send_sems.at[r - 1], recv_sem=recv_sems.at[r - 1],
                device_id=_peer(r), device_id_type=pl.DeviceIdType.MESH)

        def arrival(r):
            return pltpu.make_async_remote_copy(
                src_ref=mine, dst_ref=out_ref.at[_block_of(_peer(r))], send_sem=send_sems.at[r - 1],
                recv_sem=recv_sems.at[r - 1], device_id=_peer(r), device_id_type=pl.DeviceIdType.MESH)

        sends = [copy(r) for r in range(1, N_DEV)]
        for cp in sends:
            cp.start()
        for r in range(1, N_DEV):
            arrival(r).wait_recv()
        for cp in sends:
            cp.wait_send()
        local.wait()

    return pl.pallas_call(
        body, name=name,
        in_specs=[pl.BlockSpec(memory_space=pl.ANY)],
        out_specs=pl.BlockSpec(memory_space=pl.ANY),
        out_shape=jax.ShapeDtypeStruct((N_DEV,) + shape, buf.dtype),
        scratch_shapes=[pltpu.SemaphoreType.DMA((N_DEV - 1,)), pltpu.SemaphoreType.DMA((N_DEV - 1,)), pltpu.SemaphoreType.DMA(())],
    )(buf)


def _sum_blocks(name, blocks, tm):
    _, rows, width = blocks.shape
    assert rows % tm == 0

    def body(x_ref, o_ref):
        acc = x_ref[0].astype(F32)
        for d in range(1, N_DEV):
            acc = acc + x_ref[d].astype(F32)
        o_ref[...] = acc

    return pl.pallas_call(
        body, name=name,
        grid=(rows // tm,),
        in_specs=[pl.BlockSpec((N_DEV, tm, width), lambda i: (0, i, 0))],
        out_specs=pl.BlockSpec((tm, width), lambda i: (i, 0)),
        out_shape=jax.ShapeDtypeStruct((rows, width), F32),
        compiler_params=pltpu.CompilerParams(dimension_semantics=("parallel",)),
    )(blocks)


def _all_reduce_small(name, vec):
    rows, width = vec.shape

    def body(x_ref, o_ref, all_ref, send_sems, recv_sems):
        me = _block_of(_peer(0))
        all_ref[me] = x_ref[...]

        def copy(r, block):
            return pltpu.make_async_remote_copy(
                src_ref=x_ref, dst_ref=all_ref.at[block], send_sem=send_sems.at[r - 1], recv_sem=recv_sems.at[r - 1],
                device_id=_peer(r), device_id_type=pl.DeviceIdType.MESH)

        sends = [copy(r, me) for r in range(1, N_DEV)]
        for cp in sends:
            cp.start()
        for r in range(1, N_DEV):
            copy(r, _block_of(_peer(r))).wait_recv()
        for cp in sends:
            cp.wait_send()
        acc = all_ref[0]
        for d in range(1, N_DEV):
            acc = acc + all_ref[d]
        o_ref[...] = acc

    return pl.pallas_call(
        body, name=name,
        in_specs=[pl.BlockSpec(memory_space=pltpu.VMEM)],
        out_specs=pl.BlockSpec(memory_space=pltpu.VMEM),
        out_shape=jax.ShapeDtypeStruct((rows, width), F32),
        scratch_shapes=[pltpu.VMEM((N_DEV, rows, width), F32), pltpu.SemaphoreType.DMA((N_DEV - 1,)), pltpu.SemaphoreType.DMA((N_DEV - 1,))],
    )(vec)


def _adamw(name, w, g, m, v, tm):
    def fn(rows, consts):
        wv, gv, mv, vv = rows
        m2 = ADAM_B1 * mv + (1.0 - ADAM_B1) * gv
        v2 = ADAM_B2 * vv + (1.0 - ADAM_B2) * jnp.square(gv)
        m_hat = m2 / (1.0 - ADAM_B1 ** ADAM_STEP)
        v_hat = v2 / (1.0 - ADAM_B2 ** ADAM_STEP)
        return [-ADAM_LR * (m_hat / (jnp.sqrt(v_hat) + ADAM_EPS) + ADAM_WD * wv), m2, v2], []

    width = w.shape[1]
    return _rowwise(name, fn, [w, g, m, v], [], [(width, F32)] * 3, tm=tm)


ROW = 1024
BIG = {
    "ffn1_w_gate": ((D_MODEL, D_FF), 1), "ffn1_w_up": ((D_MODEL, D_FF), 1), "ffn1_w_down": ((D_FF, D_MODEL), 0),
    "w_in": ((D_MODEL, W_IN_CUTS[-1]), 1), "mla_w_uq": ((MLA_Q_RANK, N_HEADS * (MLA_NOPE + MLA_ROPE)), 1),
    "mla_w_ukv": ((MLA_KV_RANK, N_HEADS * (MLA_NOPE + MLA_V)), 1), "w_out": ((2 * N_HEADS * GDN_D, D_MODEL), 0),
    "ffn2_w_gate": ((D_MODEL, D_FF), 1), "ffn2_w_up": ((D_MODEL, D_FF), 1), "ffn2_w_down": ((D_FF, D_MODEL), 0),
}
PART_ROWS = 16
BIG_TM = 240
BIG_ROWS = 11 * BIG_TM
SMALL = {
    "ffn1_pre_g": (1024, 1024), "ffn1_post_g": (1024, 1024), "mix_pre_g": (1024, 1024), "mla_q_norm_g": (256, 256),
    "mla_kv_norm_g": (128, 128), "mla_out_g": (512, 512), "gdn_a_log": (8, 128), "gdn_dt_bias": (8, 128),
    "gdn_norm_g": (64, 128), "mix_post_g": (1024, 1024), "ffn2_pre_g": (1024, 1024), "ffn2_post_g": (1024, 1024),
}
SMALL_LANES = sum(r for _, r in SMALL.values())
CONV_SHAPE = (GDN_CONV, 3 * N_HEADS * GDN_D)
CONV_SHARD = (GDN_CONV, CONV_SHAPE[1] // N_DEV)
CONV_LANES = CONV_SHAPE[0] * CONV_SHAPE[1]
SMALL_ROWS = 8
REDUCE_ROWS = 16


def _shard_shape(name):
    shape, axis = BIG[name]
    return tuple(s // N_DEV if i == axis else s for i, s in enumerate(shape))


def _part_rows(name):
    shape = _shard_shape(name)
    rows = shape[0] * shape[1] // ROW
    return rows, -(-rows // PART_ROWS) * PART_ROWS


def _pad_rows(a, rows, axis):
    pad = [(0, 0)] * a.ndim
    pad[axis] = (0, rows - a.shape[axis])
    return jnp.pad(a, pad)


def _pack_big(shards):
    parts = [_pad_rows(shards[n].reshape(-1, ROW), _part_rows(n)[1], 0) for n in BIG]
    rows = sum(p.shape[0] for p in parts)
    return jnp.concatenate(parts + [jnp.zeros((BIG_ROWS - rows, ROW), parts[0].dtype)], axis=0)


def _unpack_big(buf):
    out, at = {}, 0
    for n in BIG:
        rows, taken = _part_rows(n)
        out[n] = buf[at:at + rows].reshape(_shard_shape(n))
        at += taken
    return out


def _join_big(gathered):
    out, at = {}, 0
    for n, (shape, axis) in BIG.items():
        rows, taken = _part_rows(n)
        blocks = gathered[:, at:at + rows].reshape((N_DEV,) + _shard_shape(n))
        out[n] = blocks.reshape(shape) if axis == 0 else blocks.transpose(1, 0, 2).reshape(shape)
        at += taken
    return out


def _split_big(full, dtype):
    parts = []
    for n, (shape, axis) in BIG.items():
        sh = _shard_shape(n)
        blocks = full[n].astype(dtype).reshape((N_DEV,) + sh) if axis == 0 else full[n].astype(dtype).reshape(
            sh[0], N_DEV, sh[1]).transpose(1, 0, 2)
        parts.append(_pad_rows(blocks.reshape(N_DEV, -1, ROW), _part_rows(n)[1], 1))
    rows = sum(p.shape[1] for p in parts)
    return jnp.concatenate(parts + [jnp.zeros((N_DEV, BIG_ROWS - rows, ROW), dtype)], axis=1)


def _pack_small(vecs, conv, rows):
    parts = [_pad_lanes(vecs[n].reshape(1, -1), 0, r) for n, (_, r) in SMALL.items()]
    if conv is not None:
        parts.append(conv.reshape(1, -1))
    flat = jnp.concatenate(parts, axis=1)
    return _pad_lanes(flat, 0, rows * ROW).reshape(rows, ROW)


def _unpack_small(buf):
    flat = buf.reshape(1, -1)
    out, at = {}, 0
    for n, (w, r) in SMALL.items():
        out[n] = flat[:, at:at + w]
        at += r
    return out, flat[0, at:]


def kernel(x, positions, ffn1_pre_g, ffn1_w_gate, ffn1_w_up, ffn1_w_down, ffn1_post_g, mix_pre_g, w_in, mla_q_norm_g, mla_w_uq, mla_kv_norm_g, mla_w_ukv, mla_out_g, gdn_conv_w, gdn_a_log, gdn_dt_bias, gdn_norm_g, w_out, mix_post_g, ffn2_pre_g, ffn2_w_gate, ffn2_w_up, ffn2_w_down, ffn2_post_g, loss_target, m_ffn1_pre_g, m_ffn1_w_gate, m_ffn1_w_up, m_ffn1_w_down, m_ffn1_post_g, m_mix_pre_g, m_w_in, m_mla_q_norm_g, m_mla_w_uq, m_mla_kv_norm_g, m_mla_w_ukv, m_mla_out_g, m_gdn_conv_w, m_gdn_a_log, m_gdn_dt_bias, m_gdn_norm_g, m_w_out, m_mix_post_g, m_ffn2_pre_g, m_ffn2_w_gate, m_ffn2_w_up, m_ffn2_w_down, m_ffn2_post_g, v_ffn1_pre_g, v_ffn1_w_gate, v_ffn1_w_up, v_ffn1_w_down, v_ffn1_post_g, v_mix_pre_g, v_w_in, v_mla_q_norm_g, v_mla_w_uq, v_mla_kv_norm_g, v_mla_w_ukv, v_mla_out_g, v_gdn_conv_w, v_gdn_a_log, v_gdn_dt_bias, v_gdn_norm_g, v_w_out, v_mix_post_g, v_ffn2_pre_g, v_ffn2_w_gate, v_ffn2_w_up, v_ffn2_w_down, v_ffn2_post_g):
    given = dict(locals())
    names = list(BIG) + list(SMALL) + ["gdn_conv_w"]
    order = ["ffn1_pre_g", "ffn1_w_gate", "ffn1_w_up", "ffn1_w_down", "ffn1_post_g", "mix_pre_g", "w_in", "mla_q_norm_g",
             "mla_w_uq", "mla_kv_norm_g", "mla_w_ukv", "mla_out_g", "gdn_conv_w", "gdn_a_log", "gdn_dt_bias", "gdn_norm_g",
             "w_out", "mix_post_g", "ffn2_pre_g", "ffn2_w_gate", "ffn2_w_up", "ffn2_w_down", "ffn2_post_g"]
    assert sorted(names) == sorted(order)
    def drop_depth(a):
        return a[0] if a.ndim == 3 else a

    wts = {n: drop_depth(given[n]) for n in order}
    mom = {n: drop_depth(given["m_" + n]) for n in order}
    var = {n: drop_depth(given["v_" + n]) for n in order}
    me = _block_of(_peer(0))

    gathered = _exchange("gather_weights", _pack_big({n: wts[n].astype(BF16) for n in BIG}), gather=True)
    conv_at = lax.dynamic_update_slice(jnp.zeros((N_DEV, CONV_SHARD[0] * CONV_SHARD[1]), F32),
                                       wts["gdn_conv_w"].reshape(1, -1), (me, 0))
    conv_all = _all_reduce_small("gather_conv", _pad_lanes(conv_at.reshape(1, -1), 0, SMALL_ROWS * ROW).reshape(SMALL_ROWS, ROW))
    conv_full = conv_all.reshape(-1)[:CONV_LANES].reshape((N_DEV,) + CONV_SHARD).transpose(1, 0, 2).reshape(CONV_SHAPE)
    full = _join_big(gathered)
    full.update({n: wts[n] for n in SMALL})
    full["gdn_conv_w"] = conv_full

    loss_lanes, dx, grads = _local_step(x[0], positions[0], loss_target[0], full)
    loss = lax.psum(jnp.sum(loss_lanes), MESH_AXES)

    landed = _exchange("scatter_grads", _split_big(grads, BF16), gather=False)
    big_grad = _sum_blocks("sum_grads", landed, BIG_TM)
    small_sum = _all_reduce_small("reduce_small", _pack_small(grads, grads["gdn_conv_w"].reshape(-1), REDUCE_ROWS))
    small_grad, conv_grad_full = _unpack_small(small_sum)
    conv_grad = lax.dynamic_slice(conv_grad_full[:CONV_LANES].reshape(CONV_SHAPE), (0, me * CONV_SHARD[1]), CONV_SHARD)

    big = [_pack_big({n: s[n] for n in BIG}) for s in (wts, mom, var)]
    big_out = _adamw("adamw_big", big[0], big_grad, big[1], big[2], BIG_TM)
    small_g = dict(small_grad)
    small = [_pack_small(s, s["gdn_conv_w"].reshape(-1), SMALL_ROWS) for s in (wts, {**small_g, "gdn_conv_w": conv_grad}, mom, var)]
    small_out = _adamw("adamw_small", *small, SMALL_ROWS)

    outs = {"grad": {**_unpack_big(big_grad), **small_g, "gdn_conv_w": conv_grad}}
    for kind, b, s in zip(("delta", "new_m", "new_v"), big_out, small_out):
        vecs, conv = _unpack_small(s)
        outs[kind] = {**_unpack_big(b), **vecs, "gdn_conv_w": conv[:CONV_SHARD[0] * CONV_SHARD[1]].reshape(CONV_SHARD)}
    result = [loss, dx[None]]
    for kind in ("grad", "delta", "new_m", "new_v"):
        result += [outs[kind][n].reshape(given[n].shape) for n in order]
    return tuple(result)
```

```python
import jax
import jax.numpy as jnp
from jax import lax
from jax.experimental import pallas as pl
from jax.experimental.pallas import tpu as pltpu

F32 = jnp.float32
BF16 = jnp.bfloat16
HI = lax.Precision.HIGH
EXACT = lax.Precision.HIGHEST

N_DEV = 8
D_MODEL = 1024
D_FF = 2816
N_HEADS = 8
SLOT = 128
MLA_Q_RANK = 256
MLA_KV_RANK = 128
MLA_NOPE = 64
MLA_ROPE = 32
MLA_V = 64
GDN_D = 64
GDN_CONV = 4
GDN_CHUNK = 64
ROPE_THETA = 10000.0
EPS = 1e-6
ADAM_LR, ADAM_B1, ADAM_B2, ADAM_EPS, ADAM_WD, ADAM_STEP = 0.001, 0.9, 0.999, 1e-08, 0.01, 10


def _dot(a, b, ca, cb, precision=None):
    lead = a.ndim - 2
    batch = tuple(range(lead))
    return lax.dot_general(a, b, (((lead + ca,), (lead + cb,)), (batch, batch)), precision=precision,
                           preferred_element_type=F32)


def _nn(a, b, precision=None):
    return _dot(a, b, 1, 0, precision)


def _nt(a, b, precision=None):
    return _dot(a, b, 1, 1, precision)


def _tn(a, b, precision=None):
    return _dot(a, b, 0, 0, precision)


def _sigmoid(x):
    return 1.0 / (1.0 + jnp.exp(-x))


def _silu(x):
    return x * _sigmoid(x)


def _rms(x, g, n):
    ms = jnp.sum(x * x, axis=-1, keepdims=True) * (1.0 / n)
    return x * lax.rsqrt(ms + EPS) * g


def _chunk_masks():
    c = GDN_CHUNK
    i = lax.broadcasted_iota(jnp.int32, (c, c), 0)
    j = lax.broadcasted_iota(jnp.int32, (c, c), 1)
    lower = i >= j
    strict = i > j
    eye = (i == j).astype(F32)
    blocks = []
    b = 1
    while b < c:
        same = (i // (2 * b)) == (j // (2 * b))
        blocks.append(same & ((i % (2 * b)) >= b) & ((j % (2 * b)) < b))
        b *= 2
    return lower, strict, eye, blocks


def _unit_lower_inverse(low, eye, blocks):
    t = jnp.broadcast_to(eye, low.shape)
    for m in blocks:
        lo = jnp.where(m, low, 0.0)
        t = t - _nn(t, _nn(lo, t, HI), HI)
    return t


@jax.custom_vjp
def _known_inverse(low, tinv):
    return tinv


def _known_inverse_fwd(low, tinv):
    return tinv, tinv


def _known_inverse_bwd(tinv, dt):
    return -_tn(tinv, _nt(dt, tinv, HI), HI), jnp.zeros_like(tinv)


_known_inverse.defvjp(_known_inverse_fwd, _known_inverse_bwd)


def _gdn_chunk(q, k, v, gb, bb, s, masks, tinv=None):
    lower, strict, eye, blocks = masks
    qs = q * (GDN_D ** -0.5)
    gc = _nn(jnp.broadcast_to(lower.astype(F32), gb.shape), gb, EXACT)
    gct = _nt(jnp.broadcast_to(eye, gb.shape), gc, EXACT)
    decay = jnp.exp(jnp.where(lower, gc - gct, -1e30))
    kb = k * bb
    low = jnp.where(strict, _nt(kb, k, HI) * decay, 0.0)
    tinv = _unit_lower_inverse(low, eye, blocks) if tinv is None else _known_inverse(low, tinv)
    eg = jnp.exp(gc)
    w = _nn(tinv, kb * eg, HI)
    u = _nn(tinv, v * bb, HI)
    attn = _nt(qs, k, HI) * decay
    g_end = jnp.sum(gb, axis=-2, keepdims=True)
    k_dec = k * jnp.exp(g_end - gc)
    v_new = u - _nn(w, s, HI)
    o = _nn(qs * eg, s, HI) + _nn(attn, v_new, HI)
    s_new = s * jnp.exp(g_end) + _tn(k_dec, v_new, HI)
    return o, s_new, tinv


GDN_GROUP = 8
GDN_GROUPS = N_HEADS // GDN_GROUP


def _group_heads(ref):
    return jnp.stack([ref[:, pl.ds(j * SLOT, GDN_D)] for j in range(GDN_GROUP)])


def _ungroup_heads(ref, val):
    pad = jnp.zeros((GDN_CHUNK, SLOT - GDN_D), F32)
    for j in range(GDN_GROUP):
        ref[:, pl.ds(j * SLOT, GDN_D)] = val[j]
        ref[:, pl.ds(j * SLOT + GDN_D, SLOT - GDN_D)] = pad


def _gdn_fwd(qkv, gb, bb):
    t = qkv.shape[0]
    n_chunks = t // GDN_CHUNK
    d = GDN_D

    def body(q_ref, k_ref, v_ref, g_ref, b_ref, o_ref, keep_ref, s_ref):
        @pl.when(pl.program_id(1) == 0)
        def _():
            s_ref[...] = jnp.zeros_like(s_ref)

        s = s_ref[...]
        keep_ref[:, 0, 0] = s
        o, s_new, tinv = _gdn_chunk(*[_group_heads(r) for r in (q_ref, k_ref, v_ref, g_ref, b_ref)], s, _chunk_masks())
        keep_ref[:, 0, 1] = tinv
        s_ref[...] = s_new
        _ungroup_heads(o_ref, o)

    def spec(kind=0):
        return pl.BlockSpec((GDN_CHUNK, GDN_GROUP * SLOT), lambda h, n: (n, kind * GDN_GROUPS + h))

    return pl.pallas_call(
        body, name="gdn_fwd",
        grid=(GDN_GROUPS, n_chunks),
        in_specs=[spec(0), spec(1), spec(2), spec(), spec()],
        out_specs=[spec(), pl.BlockSpec((GDN_GROUP, 1, 2, d, d), lambda h, n: (h, n, 0, 0, 0))],
        out_shape=[jax.ShapeDtypeStruct((t, N_HEADS * SLOT), F32), jax.ShapeDtypeStruct((N_HEADS, n_chunks, 2, d, d), F32)],
        scratch_shapes=[pltpu.VMEM((GDN_GROUP, d, d), F32)],
        compiler_params=pltpu.CompilerParams(dimension_semantics=("parallel", "arbitrary")),
    )(qkv, qkv, qkv, gb, bb)


def _gdn_bwd(qkv, gb, bb, keep, do):
    t = qkv.shape[0]
    n_chunks = t // GDN_CHUNK
    d = GDN_D

    def body(q_ref, k_ref, v_ref, g_ref, b_ref, keep_ref, do_ref, dqkv_ref, dg_ref, db_ref, ds_ref):
        @pl.when(pl.program_id(1) == 0)
        def _():
            ds_ref[...] = jnp.zeros_like(ds_ref)

        masks = _chunk_masks()
        tinv = keep_ref[:, 0, 1]
        _, pull = jax.vjp(lambda *a: _gdn_chunk(*a, masks, tinv)[:2],
                          *[_group_heads(r) for r in (q_ref, k_ref, v_ref, g_ref, b_ref)], keep_ref[:, 0, 0])
        dq, dk, dv, dg, db, ds = pull((_group_heads(do_ref), ds_ref[...]))
        ds_ref[...] = ds
        for i, val in enumerate((dq, dk, dv)):
            _ungroup_heads(dqkv_ref.at[i], val)
        _ungroup_heads(dg_ref, dg)
        _ungroup_heads(db_ref, db)

    def spec(kind=0):
        return pl.BlockSpec((GDN_CHUNK, GDN_GROUP * SLOT), lambda h, n: (n_chunks - 1 - n, kind * GDN_GROUPS + h))

    return pl.pallas_call(
        body, name="gdn_bwd",
        grid=(GDN_GROUPS, n_chunks),
        in_specs=[spec(0), spec(1), spec(2), spec(), spec(),
                  pl.BlockSpec((GDN_GROUP, 1, 2, d, d), lambda h, n: (h, n_chunks - 1 - n, 0, 0, 0)), spec()],
        out_specs=[pl.BlockSpec((3, GDN_CHUNK, GDN_GROUP * SLOT), lambda h, n: (0, n_chunks - 1 - n, h)), spec(), spec()],
        out_shape=[jax.ShapeDtypeStruct((3, t, N_HEADS * SLOT), F32)] + [jax.ShapeDtypeStruct((t, N_HEADS * SLOT), F32)] * 2,
        scratch_shapes=[pltpu.VMEM((GDN_GROUP, d, d), F32)],
        compiler_params=pltpu.CompilerParams(dimension_semantics=("parallel", "arbitrary")),
    )(qkv, qkv, qkv, gb, bb, keep, do)


def _rowwise(name, fn, rows, consts, outs, sums=(), tm=256):
    rows = [x if isinstance(x, tuple) else (x, x.shape[1], 0) for x in rows]
    t = rows[0][0].shape[0]
    tm = min(tm, t)
    steps = t // tm
    n_r, n_c, n_o, n_s = len(rows), len(consts), len(outs), len(sums)

    def window(width, block):
        return pl.BlockSpec((tm, width), lambda i: (i, block))

    def body(*refs):
        r, c = refs[:n_r], refs[n_r:n_r + n_c]
        o, s = refs[n_r + n_c:n_r + n_c + n_o], refs[n_r + n_c + n_o:]
        vals, tot = fn([x[...] for x in r], [x[...] for x in c])
        for ref, val in zip(o, vals):
            ref[...] = val.astype(ref.dtype)
        if n_s:
            @pl.when(pl.program_id(0) == 0)
            def _():
                for ref in s:
                    ref[...] = jnp.zeros_like(ref)

            for ref, val in zip(s, tot):
                ref[...] += val

    return pl.pallas_call(
        body, name=name,
        grid=(steps,),
        in_specs=[window(w, b) for _, w, b in rows] + [pl.BlockSpec(x.shape, lambda i: (0, 0)) for x in consts],
        out_specs=[pl.BlockSpec((tm, w), lambda i: (i, 0)) for w, _ in outs]
        + [pl.BlockSpec((1, w), lambda i: (0, 0)) for w in sums],
        out_shape=[jax.ShapeDtypeStruct((t, w), dt) for w, dt in outs]
        + [jax.ShapeDtypeStruct((1, w), F32) for w in sums],
        compiler_params=pltpu.CompilerParams(dimension_semantics=("arbitrary",)),
    )(*[x for x, _, _ in rows], *consts)


def _tile(dim, target):
    if dim <= target:
        return dim
    best = None
    for cand in range(128, target + 1, 128):
        if dim % cand == 0:
            best = cand
    assert best is not None, (dim, target)
    return best


def _matmul(name, a, b, mode, out_dtype=F32, tm=512, tn=1024, tk=1024):
    if mode == "nn":
        (m, k), n = a.shape, b.shape[1]
    elif mode == "nt":
        (m, k), n = a.shape, b.shape[0]
    else:
        (k, m), n = a.shape, b.shape[1]
    tm, tn, tk = _tile(m, tm), _tile(n, tn), _tile(k, tk)
    k_steps = k // tk
    product = {"nn": _nn, "nt": _nt, "tn": _tn}[mode]

    def body(a_ref, b_ref, o_ref, acc_ref):
        part = product(a_ref[...].astype(BF16), b_ref[...].astype(BF16))
        if k_steps == 1:
            o_ref[...] = part.astype(o_ref.dtype)
        else:
            kk = pl.program_id(2)

            @pl.when(kk == 0)
            def _():
                acc_ref[...] = part

            @pl.when(kk > 0)
            def _():
                acc_ref[...] += part

            @pl.when(kk == k_steps - 1)
            def _():
                o_ref[...] = acc_ref[...].astype(o_ref.dtype)

    a_spec = pl.BlockSpec((tk, tm), lambda i, j, kk: (kk, i)) if mode == "tn" else pl.BlockSpec((tm, tk), lambda i, j, kk: (i, kk))
    b_spec = pl.BlockSpec((tn, tk), lambda i, j, kk: (j, kk)) if mode == "nt" else pl.BlockSpec((tk, tn), lambda i, j, kk: (kk, j))
    return pl.pallas_call(
        body, name=name,
        grid=(m // tm, n // tn, k_steps),
        in_specs=[a_spec, b_spec],
        out_specs=pl.BlockSpec((tm, tn), lambda i, j, kk: (i, j)),
        out_shape=jax.ShapeDtypeStruct((m, n), out_dtype),
        scratch_shapes=[pltpu.VMEM((tm, tn) if k_steps > 1 else (8, 128), F32)],
        compiler_params=pltpu.CompilerParams(dimension_semantics=("parallel", "parallel", "arbitrary")),
    )(a, b)


FFN_TM = 512
FFN_BWD_TM = 256
FFN_BLOCKS = 4
FFN_GATE, FFN_UP, FFN_DOWN = 0, 1, 2


def _ffn_weight_specs(ffn_w, first):
    _, _, rows, dm = ffn_w.shape

    def spec(k):
        return pl.BlockSpec((FFN_BLOCKS, None, rows, dm), lambda i, j: (j, first + k, 0, 0))

    return [spec(FFN_GATE), spec(FFN_UP), spec(FFN_DOWN)], FFN_BLOCKS * rows


def _ffn_fwd(name, x, g_pre, ffn_w, first, g_post):
    t, dm = x.shape
    tm = min(FFN_TM, t)
    w_specs, tf = _ffn_weight_specs(ffn_w, first)
    f_steps = N_DEV // FFN_BLOCKS

    def body(x_ref, gpre_ref, wg_ref, wu_ref, wd_ref, gpost_ref, h_ref, y_ref, xn_ref, acc_ref):
        j = pl.program_id(1)

        @pl.when(j == 0)
        def _():
            xn_ref[...] = _rms(x_ref[...], gpre_ref[...], dm).astype(BF16)
            acc_ref[...] = jnp.zeros_like(acc_ref)

        xn = xn_ref[...]
        wg, wu, wd = (r[...].reshape(tf, dm) for r in (wg_ref, wu_ref, wd_ref))
        a = _silu(_nt(xn, wg)) * _nt(xn, wu)
        acc_ref[...] += _nn(a.astype(BF16), wd)

        @pl.when(j == f_steps - 1)
        def _():
            h = acc_ref[...]
            h_ref[...] = h
            y_ref[...] = x_ref[...] + 0.5 * _rms(h, gpost_ref[...], dm)

    row = pl.BlockSpec((tm, dm), lambda i, j: (i, 0))
    vec = pl.BlockSpec((1, dm), lambda i, j: (0, 0))
    return pl.pallas_call(
        body, name=name,
        grid=(t // tm, f_steps),
        in_specs=[row, vec, *w_specs, vec],
        out_specs=[row, row],
        out_shape=[jax.ShapeDtypeStruct((t, dm), F32)] * 2,
        scratch_shapes=[pltpu.VMEM((tm, dm), BF16), pltpu.VMEM((tm, dm), F32)],
        compiler_params=pltpu.CompilerParams(dimension_semantics=("parallel", "arbitrary")),
    )(x, g_pre, ffn_w, ffn_w, ffn_w, g_post)


def _ffn_bwd(name, x, h, dy, g_pre, ffn_w, first, g_post):
    t, dm = x.shape
    tm = min(FFN_BWD_TM, t)
    w_specs, tf = _ffn_weight_specs(ffn_w, first)
    f_steps = N_DEV // FFN_BLOCKS
    f = f_steps * tf

    def post(hv, g):
        return 0.5 * _rms(hv, g, dm)

    def pre(xv, g):
        return _rms(xv, g, dm)

    def body(x_ref, h_ref, dy_ref, gpre_ref, wg_ref, wu_ref, wd_ref, gpost_ref,
             dx_ref, xn_ref, dh_ref, a_ref, dhg_ref, dhu_ref, dgpre_ref, dgpost_ref, acc_ref):
        i, j = pl.program_id(0), pl.program_id(1)

        @pl.when((i == 0) & (j == 0))
        def _():
            dgpre_ref[...] = jnp.zeros_like(dgpre_ref)
            dgpost_ref[...] = jnp.zeros_like(dgpost_ref)

        @pl.when(j == 0)
        def _():
            xn_ref[...] = pre(x_ref[...], gpre_ref[...]).astype(BF16)
            _, pull = jax.vjp(post, h_ref[...], gpost_ref[...])
            dh, dg = pull(dy_ref[...])
            dh_ref[...] = dh.astype(BF16)
            dgpost_ref[...] += dg
            acc_ref[...] = jnp.zeros_like(acc_ref)

        xn = xn_ref[...]
        wg, wu, wd = (r[...].reshape(tf, dm) for r in (wg_ref, wu_ref, wd_ref))
        hg = _nt(xn, wg)
        hu = _nt(xn, wu)
        da = _nt(dh_ref[...], wd)
        sig = _sigmoid(hg)
        act = hg * sig
        dhu = (da * act).astype(BF16)
        dhg = (da * hu * (sig * (1.0 + hg * (1.0 - sig)))).astype(BF16)
        a_ref[...] = (act * hu).astype(BF16)
        dhg_ref[...] = dhg
        dhu_ref[...] = dhu
        acc_ref[...] += _nn(dhg, wg) + _nn(dhu, wu)

        @pl.when(j == f_steps - 1)
        def _():
            _, pull = jax.vjp(pre, x_ref[...], gpre_ref[...])
            dx, dg = pull(acc_ref[...])
            dx_ref[...] = dy_ref[...] + dx
            dgpre_ref[...] += dg

    row = pl.BlockSpec((tm, dm), lambda i, j: (i, 0))
    vec = pl.BlockSpec((1, dm), lambda i, j: (0, 0))
    wide = pl.BlockSpec((tm, tf), lambda i, j: (i, j))
    return pl.pallas_call(
        body, name=name,
        grid=(t // tm, f_steps),
        in_specs=[row, row, row, vec, *w_specs, vec],
        out_specs=[row, row, row, wide, wide, wide, vec, vec],
        out_shape=[jax.ShapeDtypeStruct((t, dm), F32), jax.ShapeDtypeStruct((t, dm), BF16), jax.ShapeDtypeStruct((t, dm), BF16),
                   jax.ShapeDtypeStruct((t, f), BF16), jax.ShapeDtypeStruct((t, f), BF16), jax.ShapeDtypeStruct((t, f), BF16),
                   jax.ShapeDtypeStruct((1, dm), F32), jax.ShapeDtypeStruct((1, dm), F32)],
        scratch_shapes=[pltpu.VMEM((tm, dm), F32)],
        compiler_params=pltpu.CompilerParams(dimension_semantics=("arbitrary", "arbitrary")),
    )(x, h, dy, g_pre, ffn_w, ffn_w, ffn_w, g_post)


ATT_T = 512
ATT_GROUP = 2
ATT_SCALE = (MLA_NOPE + MLA_ROPE) ** -0.5


def _stack_slots(ref, group):
    return jnp.stack([ref[:, pl.ds(j * SLOT, SLOT)] for j in range(group)])


def _unstack_slots(ref, val):
    for j in range(val.shape[0]):
        ref[:, pl.ds(j * SLOT, SLOT)] = val[j].astype(ref.dtype)


def _scores(q, k, diagonal):
    s = _nt(q, k) * ATT_SCALE
    if diagonal:
        row = lax.broadcasted_iota(jnp.int32, s.shape[1:], 0)
        col = lax.broadcasted_iota(jnp.int32, s.shape[1:], 1)
        s = jnp.where(col <= row, s, -1e30)
    return s


def _attn_specs(tile, q_major):
    width = ATT_GROUP * SLOT
    if q_major:
        return (pl.BlockSpec((tile, width), lambda h, qi, ki: (qi, h)),
                pl.BlockSpec((tile, width), lambda h, qi, ki: (jnp.minimum(ki, qi), h)))
    return (pl.BlockSpec((tile, width), lambda h, ki, qi: (jnp.maximum(qi, ki), h)),
            pl.BlockSpec((tile, width), lambda h, ki, qi: (ki, h)))


def _attn_fwd(q, k, v):
    t = q.shape[0]
    tile = min(ATT_T, t)
    steps = t // tile
    g = ATT_GROUP

    def body(q_ref, k_ref, v_ref, o_ref, lse_ref, m_ref, l_ref, acc_ref):
        qi, ki = pl.program_id(1), pl.program_id(2)

        @pl.when(ki == 0)
        def _():
            m_ref[...] = jnp.full_like(m_ref, -1e30)
            l_ref[...] = jnp.zeros_like(l_ref)
            acc_ref[...] = jnp.zeros_like(acc_ref)

        def step(diagonal):
            s = _scores(_stack_slots(q_ref, g), _stack_slots(k_ref, g), diagonal)
            m_old = m_ref[...]
            m_new = jnp.maximum(m_old, jnp.max(s, axis=-1, keepdims=True))
            p = jnp.exp(s - m_new)
            alpha = jnp.exp(m_old - m_new)
            l_ref[...] = alpha * l_ref[...] + jnp.sum(p, axis=-1, keepdims=True)
            acc_ref[...] = alpha * acc_ref[...] + _nn(p.astype(BF16), _stack_slots(v_ref, g))
            m_ref[...] = m_new

        @pl.when(ki < qi)
        def _():
            step(False)

        @pl.when(ki == qi)
        def _():
            step(True)
            _unstack_slots(o_ref, acc_ref[...] / l_ref[...])
            _unstack_slots(lse_ref, jnp.broadcast_to(m_ref[...] + jnp.log(l_ref[...]), acc_ref.shape))

    q_spec, k_spec = _attn_specs(tile, True)
    return pl.pallas_call(
        body, name="attn_fwd",
        grid=(N_HEADS // g, steps, steps),
        in_specs=[q_spec, k_spec, k_spec],
        out_specs=[q_spec, q_spec],
        out_shape=[jax.ShapeDtypeStruct((t, N_HEADS * SLOT), F32)] * 2,
        scratch_shapes=[pltpu.VMEM((g, tile, 1), F32), pltpu.VMEM((g, tile, 1), F32), pltpu.VMEM((g, tile, SLOT), F32)],
        compiler_params=pltpu.CompilerParams(dimension_semantics=("parallel", "parallel", "arbitrary")),
    )(q, k, v)


def _attn_grad_scores(q, k, v, do, lse_ref, delta_ref, diagonal):
    g = ATT_GROUP
    p = jnp.exp(_scores(q, k, diagonal) - _stack_slots(lse_ref, g)[:, :, 0:1])
    dp = _nt(do, v)
    return p, p * (dp - _stack_slots(delta_ref, g)[:, :, 0:1]) * ATT_SCALE


def _attn_bwd_q(q, k, v, do, lse, delta):
    t = q.shape[0]
    tile = min(ATT_T, t)
    steps = t // tile
    g = ATT_GROUP

    def body(q_ref, k_ref, v_ref, do_ref, lse_ref, delta_ref, dq_ref, acc_ref):
        qi, ki = pl.program_id(1), pl.program_id(2)

        @pl.when(ki == 0)
        def _():
            acc_ref[...] = jnp.zeros_like(acc_ref)

        def step(diagonal):
            kk = _stack_slots(k_ref, g)
            _, ds = _attn_grad_scores(_stack_slots(q_ref, g), kk, _stack_slots(v_ref, g),
                                      _stack_slots(do_ref, g).astype(BF16), lse_ref, delta_ref, diagonal)
            acc_ref[...] += _nn(ds.astype(BF16), kk)

        @pl.when(ki < qi)
        def _():
            step(False)

        @pl.when(ki == qi)
        def _():
            step(True)
            _unstack_slots(dq_ref, acc_ref[...])

    q_spec, k_spec = _attn_specs(tile, True)
    return pl.pallas_call(
        body, name="attn_bwd_q",
        grid=(N_HEADS // g, steps, steps),
        in_specs=[q_spec, k_spec, k_spec, q_spec, q_spec, q_spec],
        out_specs=q_spec,
        out_shape=jax.ShapeDtypeStruct((t, N_HEADS * SLOT), F32),
        scratch_shapes=[pltpu.VMEM((g, tile, SLOT), F32)],
        compiler_params=pltpu.CompilerParams(dimension_semantics=("parallel", "parallel", "arbitrary")),
    )(q, k, v, do, lse, delta)


def _attn_bwd_kv(q, k, v, do, lse, delta):
    t = q.shape[0]
    tile = min(ATT_T, t)
    steps = t // tile
    g = ATT_GROUP

    def body(q_ref, k_ref, v_ref, do_ref, lse_ref, delta_ref, dk_ref, dv_ref, dk_acc, dv_acc):
        ki, qi = pl.program_id(1), pl.program_id(2)

        @pl.when(qi == 0)
        def _():
            dk_acc[...] = jnp.zeros_like(dk_acc)
            dv_acc[...] = jnp.zeros_like(dv_acc)

        def step(diagonal):
            qq = _stack_slots(q_ref, g)
            do_b = _stack_slots(do_ref, g).astype(BF16)
            p, ds = _attn_grad_scores(qq, _stack_slots(k_ref, g), _stack_slots(v_ref, g), do_b, lse_ref, delta_ref, diagonal)
            dv_acc[...] += _tn(p.astype(BF16), do_b)
            dk_acc[...] += _tn(ds.astype(BF16), qq)

        @pl.when(qi > ki)
        def _():
            step(False)

        @pl.when(qi == ki)
        def _():
            step(True)

        @pl.when(qi == steps - 1)
        def _():
            _unstack_slots(dk_ref, dk_acc[...])
            _unstack_slots(dv_ref, dv_acc[...])

    q_spec, k_spec = _attn_specs(tile, False)
    return pl.pallas_call(
        body, name="attn_bwd_kv",
        grid=(N_HEADS // g, steps, steps),
        in_specs=[q_spec, k_spec, k_spec, q_spec, q_spec, q_spec],
        out_specs=[k_spec, k_spec],
        out_shape=[jax.ShapeDtypeStruct((t, N_HEADS * SLOT), F32)] * 2,
        scratch_shapes=[pltpu.VMEM((g, tile, SLOT), F32), pltpu.VMEM((g, tile, SLOT), F32)],
        compiler_params=pltpu.CompilerParams(dimension_semantics=("parallel", "parallel", "arbitrary")),
    )(q, k, v, do, lse, delta)


def _shift_down(x, s):
    if s == 0:
        return x
    row = lax.broadcasted_iota(jnp.int32, x.shape, 0)
    return jnp.where(row >= s, pltpu.roll(x, s, 0), 0.0)


def _shift_up(x, s):
    if s == 0:
        return x
    n = x.shape[0]
    row = lax.broadcasted_iota(jnp.int32, x.shape, 0)
    return jnp.where(row < n - s, pltpu.roll(x, n - s, 0), 0.0)


def _l2norm(x):
    return x * lax.rsqrt(jnp.sum(x * x, axis=-1, keepdims=True) + EPS)


def _conv_pre(x, w):
    y = w[GDN_CONV - 1:GDN_CONV, :] * x
    for s in range(1, GDN_CONV):
        y = y + w[GDN_CONV - 1 - s:GDN_CONV - s, :] * _shift_down(x, s)
    return y


def _gdn_conv_fwd(x, w):
    t, width = x.shape

    def body(x_ref, w_ref, o_ref):
        act = _silu(_conv_pre(x_ref[...], w_ref[...]))
        normed = pl.program_id(0) < 2 * N_HEADS
        o_ref[...] = jnp.where(normed, _l2norm(act), act)

    return pl.pallas_call(
        body, name="gdn_conv_fwd",
        grid=(width // SLOT,),
        in_specs=[pl.BlockSpec((t, SLOT), lambda j: (0, j)), pl.BlockSpec((GDN_CONV, SLOT), lambda j: (0, j))],
        out_specs=pl.BlockSpec((t, SLOT), lambda j: (0, j)),
        out_shape=jax.ShapeDtypeStruct((t, width), F32),
        compiler_params=pltpu.CompilerParams(dimension_semantics=("parallel",)),
    )(x, w)


def _gdn_conv_bwd(x, w, dout):
    t, width = x.shape

    def body(x_ref, w_ref, do_ref, dx_ref, dw_ref):
        xv, wv = x_ref[...], w_ref[...]
        y = _conv_pre(xv, wv)
        sig = _sigmoid(y)
        act = y * sig
        _, pull = jax.vjp(_l2norm, act)
        normed = pl.program_id(0) < 2 * N_HEADS
        dact = jnp.where(normed, pull(do_ref[0])[0], do_ref[0])
        dy = dact * (sig * (1.0 + y * (1.0 - sig)))
        dx = wv[GDN_CONV - 1:GDN_CONV, :] * dy
        for s in range(1, GDN_CONV):
            dx = dx + wv[GDN_CONV - 1 - s:GDN_CONV - s, :] * _shift_up(dy, s)
        dx_ref[...] = dx.astype(BF16)
        for s in range(GDN_CONV):
            dw_ref[GDN_CONV - 1 - s:GDN_CONV - s, :] = jnp.sum(dy * _shift_down(xv, s), axis=0, keepdims=True)

    col = pl.BlockSpec((t, SLOT), lambda j: (0, j))
    tap = pl.BlockSpec((GDN_CONV, SLOT), lambda j: (0, j))
    return pl.pallas_call(
        body, name="gdn_conv_bwd",
        grid=(width // SLOT,),
        in_specs=[col, tap, pl.BlockSpec((1, t, SLOT), lambda j: (j // N_HEADS, 0, j % N_HEADS))],
        out_specs=[col, tap],
        out_shape=[jax.ShapeDtypeStruct((t, width), BF16), jax.ShapeDtypeStruct((GDN_CONV, width), F32)],
        compiler_params=pltpu.CompilerParams(dimension_semantics=("parallel",)),
    )(x, w, dout)


def _softplus(x):
    e = jnp.exp(-jnp.abs(x))
    u = 1.0 + e
    log1p = jnp.where(u == 1.0, e, jnp.log(u) * e / jnp.where(u == 1.0, 1.0, u - 1.0))
    return jnp.maximum(x, 0.0) + log1p


def _gates_fwd(ab, a_log, dt_bias):
    def fn(rows, consts):
        (abv,), (alog, dtb) = rows, consts
        g = -jnp.exp(alog) * _softplus(abv + dtb)
        beta = _sigmoid(abv)
        shape = (abv.shape[0], SLOT)
        g_slots = [jnp.broadcast_to(g[:, h:h + 1], shape) for h in range(N_HEADS)]
        b_slots = [jnp.broadcast_to(beta[:, N_HEADS + h:N_HEADS + h + 1], shape) for h in range(N_HEADS)]
        return [jnp.concatenate(g_slots, axis=1), jnp.concatenate(b_slots, axis=1)], []

    width = N_HEADS * SLOT
    return _rowwise("gdn_gates_fwd", fn, [ab], [a_log, dt_bias], [(width, F32), (width, F32)])


def _gates_bwd(ab, a_log, dt_bias, dg, dbeta):
    def fn(rows, consts):
        (abv, dgv, dbv), (alog, dtb) = rows, consts
        lane = lax.broadcasted_iota(jnp.int32, abv.shape, 1)
        dg_tok = jnp.zeros_like(abv)
        db_tok = jnp.zeros_like(abv)
        for h in range(N_HEADS):
            dg_tok = dg_tok + jnp.where(lane == h, jnp.sum(dgv[:, h * SLOT:(h + 1) * SLOT], axis=1, keepdims=True), 0.0)
            db_tok = db_tok + jnp.where(lane == N_HEADS + h, jnp.sum(dbv[:, h * SLOT:(h + 1) * SLOT], axis=1, keepdims=True), 0.0)
        xa = abv + dtb
        g = -jnp.exp(alog) * _softplus(xa)
        da = dg_tok * (-jnp.exp(alog)) * _sigmoid(xa)
        beta = _sigmoid(abv)
        dab = jnp.where(lane < N_HEADS, da, db_tok * beta * (1.0 - beta))
        dab = jnp.where(lane < 2 * N_HEADS, dab, 0.0)
        d_alog = jnp.sum(jnp.where(lane < N_HEADS, dg_tok * g, 0.0), axis=0, keepdims=True)
        d_dtb = jnp.sum(jnp.where(lane < N_HEADS, da, 0.0), axis=0, keepdims=True)
        return [dab], [d_alog, d_dtb]

    return _rowwise("gdn_gates_bwd", fn, [ab, dg, dbeta], [a_log, dt_bias], [(SLOT, F32)], sums=[SLOT, SLOT])


ROPE_HALF = MLA_ROPE // 2


def _rope_tables(positions):
    freqs = ROPE_THETA ** (-jnp.arange(ROPE_HALF, dtype=F32) / ROPE_HALF)
    ang = positions.astype(F32)[:, None] * freqs
    cos, sin = jnp.cos(ang), jnp.sin(ang)
    t = positions.shape[0]
    ones, zeros = jnp.ones((t, MLA_NOPE), F32), jnp.zeros((t, MLA_NOPE), F32)
    tail = jnp.zeros((t, SLOT - MLA_NOPE - MLA_ROPE), F32)
    half0 = jnp.zeros((t, ROPE_HALF), F32)
    same = jnp.concatenate([ones, cos, cos, tail], axis=1)
    from_low = jnp.concatenate([zeros, half0, sin, tail], axis=1)
    from_high = jnp.concatenate([zeros, -sin, half0, tail], axis=1)
    return same, from_low, from_high


def _rope(x, tabs):
    same, from_low, from_high = tabs
    width = x.shape[1]
    return x * same + pltpu.roll(x, ROPE_HALF, 1) * from_low + pltpu.roll(x, width - ROPE_HALF, 1) * from_high


def _rope_transposed(dy, tabs):
    same, from_low, from_high = tabs
    width = dy.shape[1]
    return dy * same + pltpu.roll(dy * from_low, width - ROPE_HALF, 1) + pltpu.roll(dy * from_high, ROPE_HALF, 1)


def _tile_slots(tab):
    return jnp.concatenate([tab] * N_HEADS, axis=1)


A_WIDTH = MLA_Q_RANK + MLA_KV_RANK + 2 * SLOT
A_KPE = MLA_Q_RANK + MLA_KV_RANK
A_AB = A_KPE + SLOT
WIDE = N_HEADS * SLOT


def _mla_pre_fwd(proj_a, tabs, g_q, g_kv):
    def fn(rows, consts):
        pa, *tb = rows
        gq, gkv = consts
        return [_rms(pa[:, :MLA_Q_RANK], gq, MLA_Q_RANK), _rms(pa[:, MLA_Q_RANK:A_KPE], gkv, MLA_KV_RANK),
                _rope(pa[:, A_KPE:A_AB], tb)], []

    return _rowwise("mla_pre_fwd", fn, [proj_a, *tabs], [g_q, g_kv], [(MLA_Q_RANK, BF16), (MLA_KV_RANK, BF16), (SLOT, F32)])


def _mla_pre_bwd(proj_a, tabs, g_q, g_kv, dcqn, dckvn, dkpe, dab):
    def fn(rows, consts):
        pa, t0, t1, t2, dq, dkv, dk, da = rows
        gq, gkv = consts
        _, pull_q = jax.vjp(lambda x, g: _rms(x, g, MLA_Q_RANK), pa[:, :MLA_Q_RANK], gq)
        _, pull_kv = jax.vjp(lambda x, g: _rms(x, g, MLA_KV_RANK), pa[:, MLA_Q_RANK:A_KPE], gkv)
        dcq, dgq = pull_q(dq)
        dckv, dgkv = pull_kv(dkv)
        return [jnp.concatenate([dcq, dckv, _rope_transposed(dk, (t0, t1, t2)), da], axis=1)], [dgq, dgkv]

    return _rowwise("mla_pre_bwd", fn, [proj_a, *tabs, dcqn, dckvn, dkpe, dab], [g_q, g_kv], [(A_WIDTH, BF16)],
                    sums=[MLA_Q_RANK, MLA_KV_RANK])


def _mla_qkv_fwd(q_p, kv_p, kpe, tabs):
    def fn(rows, consts):
        qv, kvv, kp, *tb = rows
        q = _rope(qv, [_tile_slots(x) for x in tb])
        k = kvv[:, :WIDE] + _tile_slots(kp)
        return [q, k, kvv[:, WIDE:]], []

    return _rowwise("mla_qkv_fwd", fn, [q_p, kv_p, kpe, *tabs], [], [(WIDE, BF16)] * 3)


def _mla_qkv_bwd(dq, dk, dv, tabs):
    def fn(rows, consts):
        dqv, dkv, dvv, *tb = rows
        dkpe = dkv[:, :SLOT]
        for h in range(1, N_HEADS):
            dkpe = dkpe + dkv[:, h * SLOT:(h + 1) * SLOT]
        return [_rope_transposed(dqv, [_tile_slots(x) for x in tb]), jnp.concatenate([dkv, dvv], axis=1), dkpe], []

    return _rowwise("mla_qkv_bwd", fn, [dq, dk, dv, *tabs], [], [(WIDE, BF16), (2 * WIDE, BF16), (SLOT, F32)])


def _slot_sum(x):
    parts = [jnp.broadcast_to(jnp.sum(x[:, h * SLOT:(h + 1) * SLOT], axis=1, keepdims=True), (x.shape[0], SLOT))
             for h in range(N_HEADS)]
    return jnp.concatenate(parts, axis=1)


def _mix_join(o_mla, o_gdn, gate, g_mla, g_gdn):
    mla = _rms(o_mla, g_mla, N_HEADS * MLA_V)
    gdn = o_gdn * lax.rsqrt(_slot_sum(o_gdn * o_gdn) * (1.0 / GDN_D) + EPS) * g_gdn * _silu(gate)
    return mla, gdn


def _mix_join_fwd(o_mla, o_gdn, gate, g_mla, g_gdn):
    def fn(rows, consts):
        return [jnp.concatenate(_mix_join(*rows, *consts), axis=1)], []

    return _rowwise("mix_join_fwd", fn, [o_mla, o_gdn, gate], [g_mla, g_gdn], [(2 * WIDE, BF16)])


def _mix_join_bwd(o_mla, o_gdn, gate, g_mla, g_gdn, dcat):
    def fn(rows, consts):
        om, og, gt, dc = rows
        gm, gg = consts
        _, pull = jax.vjp(lambda x, g: _rms(x, g, N_HEADS * MLA_V), om, gm)
        dom, dgm = pull(dc[:, :WIDE])
        dy = dc[:, WIDE:]
        r = lax.rsqrt(_slot_sum(og * og) * (1.0 / GDN_D) + EPS)
        sig = _sigmoid(gt)
        normed = og * r
        dn = dy * gg * (gt * sig)
        dog = r * dn - normed * (r * r) * _slot_sum(dn * og) * (1.0 / GDN_D)
        dgt = dy * normed * gg * (sig * (1.0 + gt * (1.0 - sig)))
        dgg = jnp.sum(dy * normed * (gt * sig), axis=0, keepdims=True)
        return [dom, _slot_sum(dom * om), dog, dgt], [dgm, dgg]

    return _rowwise("mix_join_bwd", fn, [o_mla, o_gdn, gate, dcat], [g_mla, g_gdn],
                    [(WIDE, F32), (WIDE, F32), (WIDE, F32), (WIDE, BF16)], sums=[WIDE, WIDE])


def _norm_residual_fwd(name, x, h, g, out_dtypes):
    dm = x.shape[1]

    def fn(rows, consts):
        y = rows[0] + _rms(rows[1], consts[0], dm)
        return [y] + [_rms(y, gg, dm) for gg in consts[1:]], []

    return _rowwise(name, fn, [x, h], list(g), [(dm, dt) for dt in out_dtypes])


def _norm_residual_bwd(name, h, g, dy):
    dm = h.shape[1]

    def fn(rows, consts):
        _, pull = jax.vjp(lambda hv, gv: _rms(hv, gv, dm), rows[0], consts[0])
        dh, dg = pull(rows[1])
        return [dh], [dg]

    return _rowwise(name, fn, [h, dy], [g], [(dm, BF16)], sums=[dm])


def _norm_bwd_add(name, x, g, dns, dy):
    dm = x.shape[1]

    def fn(rows, consts):
        xv, dyv, *parts = rows
        dn = parts[0]
        for p in parts[1:]:
            dn = dn + p
        _, pull = jax.vjp(lambda a, gv: _rms(a, gv, dm), xv, consts[0])
        dx, dg = pull(dn)
        return [dyv + dx], [dg]

    return _rowwise(name, fn, [x, dy, *dns], [g], [(dm, F32)], sums=[dm])


def _loss_fwd(y, target):
    dm = y.shape[1]

    def fn(rows, consts):
        err = rows[0] - rows[1]
        sq = err * err
        lanes = sq[:, :SLOT]
        for j in range(1, dm // SLOT):
            lanes = lanes + sq[:, j * SLOT:(j + 1) * SLOT]
        return [err * (1.0 / dm)], [jnp.sum(lanes, axis=0, keepdims=True) * (0.5 / dm)]

    return _rowwise("loss", fn, [y, target], [], [(dm, F32)], sums=[SLOT])


def _norm_fwd(name, x, g):
    dm = x.shape[1]
    return _rowwise(name, lambda rows, consts: ([_rms(rows[0], consts[0], dm)], []), [x], [g], [(dm, BF16)])[0]


W_IN_CUTS = (0, 256, 384, 416, 1952, 1960, 1968, 2480)


def _heads_out(w, per_head, axis=-1):
    axis = axis % w.ndim
    shape = w.shape
    n = shape[axis] // per_head
    w = w.reshape(shape[:axis] + (n, per_head) + shape[axis + 1:])
    pad = [(0, 0)] * w.ndim
    pad[axis + 1] = (0, SLOT - per_head)
    return jnp.pad(w, pad).reshape(shape[:axis] + (n * SLOT,) + shape[axis + 1:])


def _heads_in(w, per_head, axis=-1):
    axis = axis % w.ndim
    shape = w.shape
    n = shape[axis] // SLOT
    w = w.reshape(shape[:axis] + (n, SLOT) + shape[axis + 1:])
    w = lax.slice_in_dim(w, 0, per_head, axis=axis + 1)
    return w.reshape(shape[:axis] + (n * per_head,) + shape[axis + 1:])


def _pad_lanes(v, lo, width=SLOT):
    return jnp.pad(v, [(0, 0)] * (v.ndim - 1) + [(lo, width - lo - v.shape[-1])])


def _pad_rows(v, lo, rows=SLOT):
    return jnp.pad(v, [(lo, rows - lo - v.shape[0])] + [(0, 0)] * (v.ndim - 1))


def _layout_weights(w):
    c = W_IN_CUTS
    w_in = w["w_in_t"]
    p = {}
    p["w_a"] = jnp.concatenate([w_in[c[0]:c[2]], _pad_rows(w_in[c[2]:c[3]], MLA_NOPE), _pad_rows(w_in[c[4]:c[6]], 0)], axis=0)
    p["w_qkv"] = _heads_out(w_in[c[3]:c[4]], GDN_D, axis=0)
    p["w_gate"] = _heads_out(w_in[c[6]:c[7]], GDN_D, axis=0)
    p["w_uq"] = _heads_out(w["uq_t"], MLA_NOPE + MLA_ROPE, axis=0)
    ukv = w["ukv_t"].reshape(N_HEADS, MLA_NOPE + MLA_V, MLA_KV_RANK)
    p["w_kv"] = jnp.concatenate([_heads_out(ukv[:, :MLA_NOPE].reshape(-1, MLA_KV_RANK), MLA_NOPE, axis=0),
                                 _heads_out(ukv[:, MLA_NOPE:].reshape(-1, MLA_KV_RANK), MLA_V, axis=0)], axis=0)
    p["w_out"] = _heads_out(w["w_out"], GDN_D, axis=0)
    p["conv"] = _heads_out(w["gdn_conv_w"], GDN_D)
    p["g_mla_out"] = _heads_out(w["mla_out_g"], MLA_V)
    p["g_gdn"] = jnp.tile(_pad_lanes(w["gdn_norm_g"], 0), (1, N_HEADS))
    p["a_log"] = _pad_lanes(w["gdn_a_log"], 0)
    p["dt_bias"] = _pad_lanes(w["gdn_dt_bias"], 0)
    return p


def _unlayout_grads(d):
    c = W_IN_CUTS
    g = {}
    da = d["w_a"]
    kpe0 = A_KPE + MLA_NOPE
    g["w_in_t"] = jnp.concatenate([da[:A_KPE], da[kpe0:kpe0 + MLA_ROPE], _heads_in(d["w_qkv"], GDN_D, axis=0),
                                   da[A_AB:A_AB + 2 * N_HEADS], _heads_in(d["w_gate"], GDN_D, axis=0)], axis=0)
    assert g["w_in_t"].shape[0] == c[-1]
    g["uq_t"] = _heads_in(d["w_uq"], MLA_NOPE + MLA_ROPE, axis=0)
    dk = _heads_in(d["w_kv"][:WIDE], MLA_NOPE, axis=0).reshape(N_HEADS, MLA_NOPE, MLA_KV_RANK)
    dv = _heads_in(d["w_kv"][WIDE:], MLA_V, axis=0).reshape(N_HEADS, MLA_V, MLA_KV_RANK)
    g["ukv_t"] = jnp.concatenate([dk, dv], axis=1).reshape(-1, MLA_KV_RANK)
    g["w_out"] = _heads_in(d["w_out"], GDN_D, axis=0)
    g["gdn_conv_w"] = _heads_in(d["conv"], GDN_D)
    g["mla_out_g"] = _heads_in(d["g_mla_out"], MLA_V)
    g["gdn_norm_g"] = jnp.sum(d["g_gdn"].reshape(N_HEADS, SLOT), axis=0, keepdims=True)[:, :GDN_D]
    g["gdn_a_log"] = d["a_log"][:, :N_HEADS]
    g["gdn_dt_bias"] = d["dt_bias"][:, :N_HEADS]
    return g


def _weight_grad(name, cots, acts, out_dtype=F32, tm=1024, tn=1024, tk=512):
    return _matmul(name, cots, acts, "tn", out_dtype=out_dtype, tm=tm, tn=tn, tk=tk)


def _local_step(x, positions, target, w):
    p = _layout_weights(w)
    tabs = _rope_tables(positions)
    ffn = w["ffn"]

    h1, x1 = _ffn_fwd("ffn1_fwd", x, w["ffn1_pre_g"], ffn, 0, w["ffn1_post_g"])
    hn = _norm_fwd("mix_pre_norm", x1, w["mix_pre_g"])
    proj_a = _matmul("proj_a", hn, p["w_a"], "nt")
    proj_qkv = _matmul("proj_qkv", hn, p["w_qkv"], "nt")
    proj_gate = _matmul("proj_gate", hn, p["w_gate"], "nt")
    cqn, ckvn, kpe = _mla_pre_fwd(proj_a, tabs, w["mla_q_norm_g"], w["mla_kv_norm_g"])
    q_p = _matmul("mla_q", cqn, p["w_uq"], "nt")
    kv_p = _matmul("mla_kv", ckvn, p["w_kv"], "nt")
    q, k, v = _mla_qkv_fwd(q_p, kv_p, kpe, tabs)
    o_mla, lse = _attn_fwd(q, k, v)
    ab = (proj_a, SLOT, A_AB // SLOT)
    qkv_n = _gdn_conv_fwd(proj_qkv, p["conv"])
    gb, bb = _gates_fwd(ab, p["a_log"], p["dt_bias"])
    o_gdn, keep = _gdn_fwd(qkv_n, gb, bb)
    cat = _mix_join_fwd(o_mla, o_gdn, proj_gate, p["g_mla_out"], p["g_gdn"])[0]
    mixed = _matmul("mix_out", cat, p["w_out"], "nn")
    x2 = _norm_residual_fwd("mix_post", x1, mixed, [w["mix_post_g"]], [F32])[0]
    h2, y = _ffn_fwd("ffn2_fwd", x2, w["ffn2_pre_g"], ffn, 3, w["ffn2_post_g"])
    dy, loss_lanes = _loss_fwd(y, target)

    g = {}
    dx2, xn2, dh2, a2, dhg2, dhu2, g["ffn2_pre_g"], g["ffn2_post_g"] = _ffn_bwd(
        "ffn2_bwd", x2, h2, dy, w["ffn2_pre_g"], ffn, 3, w["ffn2_post_g"])
    ffn2_grads = [_weight_grad("ffn2_dw_gate", dhg2, xn2, BF16, tm=1408), _weight_grad("ffn2_dw_up", dhu2, xn2, BF16, tm=1408),
                  _weight_grad("ffn2_dw_down", a2, dh2, BF16, tm=1408)]
    dmixed, g["mix_post_g"] = _norm_residual_bwd("mix_post_bwd", mixed, w["mix_post_g"], dx2)
    dcat = _matmul("mix_out_dx", dmixed, p["w_out"], "nt")
    d = {}
    d["w_out"] = _weight_grad("mix_out_dw", cat, dmixed)
    do_mla, delta, do_gdn, dgate, d["g_mla_out"], d["g_gdn"] = _mix_join_bwd(o_mla, o_gdn, proj_gate, p["g_mla_out"], p["g_gdn"], dcat)
    dq = _attn_bwd_q(q, k, v, do_mla, lse, delta)
    dk, dv = _attn_bwd_kv(q, k, v, do_mla, lse, delta)
    dq_p, dkv_p, dkpe = _mla_qkv_bwd(dq, dk, dv, tabs)
    dcqn = _matmul("mla_q_dx", dq_p, p["w_uq"], "nn")
    d["w_uq"] = _weight_grad("mla_q_dw", dq_p, cqn)
    dckvn = _matmul("mla_kv_dx", dkv_p, p["w_kv"], "nn")
    d["w_kv"] = _weight_grad("mla_kv_dw", dkv_p, ckvn)
    dqkv_n, dgb, dbb = _gdn_bwd(qkv_n, gb, bb, keep, do_gdn)
    dab, d["a_log"], d["dt_bias"] = _gates_bwd(ab, p["a_log"], p["dt_bias"], dgb, dbb)
    dproj_qkv, d["conv"] = _gdn_conv_bwd(proj_qkv, p["conv"], dqkv_n)
    dproj_a, g["mla_q_norm_g"], g["mla_kv_norm_g"] = _mla_pre_bwd(
        proj_a, tabs, w["mla_q_norm_g"], w["mla_kv_norm_g"], dcqn, dckvn, dkpe, dab)
    dhn = [_matmul("proj_a_dx", dproj_a, p["w_a"], "nn"), _matmul("proj_qkv_dx", dproj_qkv, p["w_qkv"], "nn"),
           _matmul("proj_gate_dx", dgate, p["w_gate"], "nn")]
    d["w_a"] = _weight_grad("proj_a_dw", dproj_a, hn, tm=640)
    d["w_qkv"] = _weight_grad("proj_qkv_dw", dproj_qkv, hn)
    d["w_gate"] = _weight_grad("proj_gate_dw", dgate, hn)
    dx1, g["mix_pre_g"] = _norm_bwd_add("mix_pre_bwd", x1, w["mix_pre_g"], dhn, dx2)
    dx, xn1, dh1, a1, dhg1, dhu1, g["ffn1_pre_g"], g["ffn1_post_g"] = _ffn_bwd(
        "ffn1_bwd", x, h1, dx1, w["ffn1_pre_g"], ffn, 0, w["ffn1_post_g"])
    g["ffn"] = [_weight_grad("ffn1_dw_gate", dhg1, xn1, BF16, tm=1408), _weight_grad("ffn1_dw_up", dhu1, xn1, BF16, tm=1408),
                _weight_grad("ffn1_dw_down", a1, dh1, BF16, tm=1408)] + ffn2_grads
    g.update(_unlayout_grads(d))
    return loss_lanes, dx, g


MESH_AXES = ("x", "y", "c")
N_LINKS = N_DEV - 1


def _place():
    return tuple(lax.axis_index(a) for a in MESH_AXES)


def _block_of(dev):
    x, y, c = dev
    return 4 * x + 2 * y + c


def _remote_copy(src, dst, sems, k, to):
    send_sems, recv_sems = sems
    return pltpu.make_async_remote_copy(src_ref=src, dst_ref=dst, send_sem=send_sems.at[k], recv_sem=recv_sems.at[k],
                                        device_id=to, device_id_type=pl.DeviceIdType.MESH)


def _gather(name, arrays):
    n = len(arrays)

    def body(*refs):
        ins, outs = refs[:n], refs[n:2 * n]
        sems, local_sems = refs[2 * n:2 * n + 2], refs[2 * n + 2]
        x, y, c = _place()
        me, sibling = (x, y, c), (x, y, 1 - c)
        chips = [(1 - x, y), (x, 1 - y), (1 - x, 1 - y)]

        def copy(a, k, block, to, mine=False):
            src = ins[a] if mine else outs[a].at[_block_of(block)]
            return _remote_copy(src, outs[a].at[_block_of(block)], sems, a * N_LINKS + k, to)

        local = [pltpu.make_async_copy(ins[a], outs[a].at[_block_of(me)], local_sems.at[a]) for a in range(n)]
        for cp in local:
            cp.start()
        sends = []
        for a in range(n):
            sends.append(copy(a, 0, me, sibling, mine=True))
            sends += [copy(a, 1 + j, me, (*chip, c), mine=True) for j, chip in enumerate(chips)]
        for cp in sends:
            cp.start()
        for j, chip in enumerate(chips):
            for a in range(n):
                copy(a, 1 + j, (*chip, c), me).wait_recv()
                passed = copy(a, 4 + j, (*chip, c), sibling)
                passed.start()
                sends.append(passed)
        for a in range(n):
            copy(a, 0, sibling, me).wait_recv()
            for j, chip in enumerate(chips):
                copy(a, 4 + j, (*chip, 1 - c), me).wait_recv()
        for cp in sends:
            cp.wait_send()
        for cp in local:
            cp.wait()

    hbm = pl.BlockSpec(memory_space=pl.ANY)
    return pl.pallas_call(
        body, name=name,
        in_specs=[hbm] * n,
        out_specs=[hbm] * n,
        out_shape=[jax.ShapeDtypeStruct((N_DEV,) + a.shape, a.dtype) for a in arrays],
        scratch_shapes=[pltpu.SemaphoreType.DMA((n * N_LINKS,)), pltpu.SemaphoreType.DMA((n * N_LINKS,)),
                        pltpu.SemaphoreType.DMA((n,))],
    )(*arrays)


def _scatter(name, arrays):
    n = len(arrays)

    def body(*refs):
        ins, outs = refs[:n], refs[n:2 * n]
        sems, local_sems = refs[2 * n:2 * n + 2], refs[2 * n + 2]
        x, y, c = _place()
        me = _block_of((x, y, c))

        def peer(r):
            return (1 - x if r & 4 else x, 1 - y if r & 2 else y, 1 - c if r & 1 else c)

        local = [pltpu.make_async_copy(ins[a].at[me], outs[a].at[me], local_sems.at[a]) for a in range(n)]
        for cp in local:
            cp.start()
        sends = [_remote_copy(ins[a].at[_block_of(peer(r))], outs[a].at[me], sems, a * N_LINKS + r - 1, peer(r))
                 for a in range(n) for r in range(1, N_DEV)]
        for cp in sends:
            cp.start()
        for a in range(n):
            for r in range(1, N_DEV):
                _remote_copy(ins[a].at[me], outs[a].at[_block_of(peer(r))], sems, a * N_LINKS + r - 1, peer(r)).wait_recv()
        for cp in sends:
            cp.wait_send()
        for cp in local:
            cp.wait()

    hbm = pl.BlockSpec(memory_space=pl.ANY)
    return pl.pallas_call(
        body, name=name,
        in_specs=[hbm] * n,
        out_specs=[hbm] * n,
        out_shape=[jax.ShapeDtypeStruct(a.shape, a.dtype) for a in arrays],
        scratch_shapes=[pltpu.SemaphoreType.DMA((n * N_LINKS,)), pltpu.SemaphoreType.DMA((n * N_LINKS,)),
                        pltpu.SemaphoreType.DMA((n,))],
    )(*arrays)


def _row_tile(rows, target=256):
    best = rows
    for cand in range(16, min(rows, target) + 1, 16):
        if rows % cand == 0:
            best = cand
    return best


def _sum_blocks(name, blocks):
    rows, width = blocks.shape[-2:]
    tm = _row_tile(rows)

    def body(x_ref, o_ref):
        acc = x_ref[0].astype(F32)
        for d in range(1, N_DEV):
            acc = acc + x_ref[d].astype(F32)
        o_ref[...] = acc

    return pl.pallas_call(
        body, name=name,
        grid=(rows // tm,),
        in_specs=[pl.BlockSpec((N_DEV, tm, width), lambda i: (0, i, 0))],
        out_specs=pl.BlockSpec((tm, width), lambda i: (i, 0)),
        out_shape=jax.ShapeDtypeStruct((rows, width), F32),
        compiler_params=pltpu.CompilerParams(dimension_semantics=("parallel",)),
    )(blocks)


def _all_reduce_small(name, vec):
    rows, width = vec.shape

    def body(x_ref, o_ref, all_ref, send_sems, recv_sems):
        x, y, c = _place()
        me = _block_of((x, y, c))
        all_ref[me] = x_ref[...]

        def peer(r):
            return (1 - x if r & 4 else x, 1 - y if r & 2 else y, 1 - c if r & 1 else c)

        def copy(r, block):
            return _remote_copy(x_ref, all_ref.at[block], (send_sems, recv_sems), r - 1, peer(r))

        sends = [copy(r, me) for r in range(1, N_DEV)]
        for cp in sends:
            cp.start()
        for r in range(1, N_DEV):
            copy(r, _block_of(peer(r))).wait_recv()
        for cp in sends:
            cp.wait_send()
        acc = all_ref[0]
        for d in range(1, N_DEV):
            acc = acc + all_ref[d]
        o_ref[...] = acc

    return pl.pallas_call(
        body, name=name,
        in_specs=[pl.BlockSpec(memory_space=pltpu.VMEM)],
        out_specs=pl.BlockSpec(memory_space=pltpu.VMEM),
        out_shape=jax.ShapeDtypeStruct((rows, width), F32),
        scratch_shapes=[pltpu.VMEM((N_DEV, rows, width), F32), pltpu.SemaphoreType.DMA((N_LINKS,)), pltpu.SemaphoreType.DMA((N_LINKS,))],
    )(vec)


def _adamw(name, w, g, m, v):
    def fn(rows, consts):
        wv, gv, mv, vv = rows
        m2 = ADAM_B1 * mv + (1.0 - ADAM_B1) * gv
        v2 = ADAM_B2 * vv + (1.0 - ADAM_B2) * jnp.square(gv)
        m_hat = m2 / (1.0 - ADAM_B1 ** ADAM_STEP)
        v_hat = v2 / (1.0 - ADAM_B2 ** ADAM_STEP)
        return [-ADAM_LR * (m_hat / (jnp.sqrt(v_hat) + ADAM_EPS) + ADAM_WD * wv), m2, v2], []

    return _rowwise(name, fn, [w, g, m, v], [], [(w.shape[1], F32)] * 3, tm=_row_tile(w.shape[0]))


ROW = 1024
FFN_NAMES = ("ffn1_w_gate", "ffn1_w_up", "ffn1_w_down", "ffn2_w_gate", "ffn2_w_up", "ffn2_w_down")
OTHER = {"w_in": ("w_in_t", True), "mla_w_uq": ("uq_t", True), "mla_w_ukv": ("ukv_t", True), "w_out": ("w_out", False)}
BY_COLUMNS = ("ffn1_w_gate", "ffn1_w_up", "ffn2_w_gate", "ffn2_w_up", "w_in", "mla_w_uq", "mla_w_ukv")
SMALL = {
    "ffn1_pre_g": (1024, 1024), "ffn1_post_g": (1024, 1024), "mix_pre_g": (1024, 1024), "mla_q_norm_g": (256, 256),
    "mla_kv_norm_g": (128, 128), "mla_out_g": (512, 512), "gdn_a_log": (8, 128), "gdn_dt_bias": (8, 128),
    "gdn_norm_g": (64, 128), "mix_post_g": (1024, 1024), "ffn2_pre_g": (1024, 1024), "ffn2_post_g": (1024, 1024),
}
CONV_SHAPE = (GDN_CONV, 3 * N_HEADS * GDN_D)
CONV_SHARD = (GDN_CONV, CONV_SHAPE[1] // N_DEV)
CONV_LANES = CONV_SHAPE[0] * CONV_SHAPE[1]
SMALL_ROWS = 8
REDUCE_ROWS = 16


def _pack_small(vecs, conv, rows):
    parts = [_pad_lanes(vecs[n].reshape(1, -1), 0, r) for n, (_, r) in SMALL.items()]
    parts.append(conv.reshape(1, -1))
    flat = jnp.concatenate(parts, axis=1)
    return _pad_lanes(flat, 0, rows * ROW).reshape(rows, ROW)


def _unpack_small(buf):
    flat = buf.reshape(1, -1)
    out, at = {}, 0
    for n, (w, r) in SMALL.items():
        out[n] = flat[:, at:at + w]
        at += r
    return out, flat[0, at:]


def kernel(x, positions, ffn1_pre_g, ffn1_w_gate, ffn1_w_up, ffn1_w_down, ffn1_post_g, mix_pre_g, w_in, mla_q_norm_g, mla_w_uq, mla_kv_norm_g, mla_w_ukv, mla_out_g, gdn_conv_w, gdn_a_log, gdn_dt_bias, gdn_norm_g, w_out, mix_post_g, ffn2_pre_g, ffn2_w_gate, ffn2_w_up, ffn2_w_down, ffn2_post_g, loss_target, m_ffn1_pre_g, m_ffn1_w_gate, m_ffn1_w_up, m_ffn1_w_down, m_ffn1_post_g, m_mix_pre_g, m_w_in, m_mla_q_norm_g, m_mla_w_uq, m_mla_kv_norm_g, m_mla_w_ukv, m_mla_out_g, m_gdn_conv_w, m_gdn_a_log, m_gdn_dt_bias, m_gdn_norm_g, m_w_out, m_mix_post_g, m_ffn2_pre_g, m_ffn2_w_gate, m_ffn2_w_up, m_ffn2_w_down, m_ffn2_post_g, v_ffn1_pre_g, v_ffn1_w_gate, v_ffn1_w_up, v_ffn1_w_down, v_ffn1_post_g, v_mix_pre_g, v_w_in, v_mla_q_norm_g, v_mla_w_uq, v_mla_kv_norm_g, v_mla_w_ukv, v_mla_out_g, v_gdn_conv_w, v_gdn_a_log, v_gdn_dt_bias, v_gdn_norm_g, v_w_out, v_mix_post_g, v_ffn2_pre_g, v_ffn2_w_gate, v_ffn2_w_up, v_ffn2_w_down, v_ffn2_post_g):
    given = dict(locals())
    order = ["ffn1_pre_g", "ffn1_w_gate", "ffn1_w_up", "ffn1_w_down", "ffn1_post_g", "mix_pre_g", "w_in", "mla_q_norm_g",
             "mla_w_uq", "mla_kv_norm_g", "mla_w_ukv", "mla_out_g", "gdn_conv_w", "gdn_a_log", "gdn_dt_bias", "gdn_norm_g",
             "w_out", "mix_post_g", "ffn2_pre_g", "ffn2_w_gate", "ffn2_w_up", "ffn2_w_down", "ffn2_post_g"]
    assert sorted(order) == sorted(list(FFN_NAMES) + list(OTHER) + list(SMALL) + ["gdn_conv_w"])

    def drop_depth(a):
        return a[0] if a.ndim == 3 else a

    wts = {n: drop_depth(given[n]) for n in order}
    mom = {n: drop_depth(given["m_" + n]) for n in order}
    var = {n: drop_depth(given["v_" + n]) for n in order}
    me = _block_of(_place())

    def wire(n):
        return (wts[n].T if n in BY_COLUMNS else wts[n]).astype(BF16)

    gathered = _gather("gather_weights", [jnp.stack([wire(n) for n in FFN_NAMES])] + [wire(n) for n in OTHER])
    conv_at = lax.dynamic_update_slice(jnp.zeros((N_DEV, CONV_SHARD[0] * CONV_SHARD[1]), F32),
                                       wts["gdn_conv_w"].reshape(1, -1), (me, 0))
    conv_all = _all_reduce_small("gather_conv", _pad_lanes(conv_at.reshape(1, -1), 0, SMALL_ROWS * ROW).reshape(SMALL_ROWS, ROW))
    full = {n: wts[n] for n in SMALL}
    full["ffn"] = gathered[0]
    for (name, _), blocks in zip(OTHER.values(), gathered[1:]):
        full[name] = blocks.reshape((-1,) + blocks.shape[2:])
    full["gdn_conv_w"] = conv_all.reshape(-1)[:CONV_LANES].reshape((N_DEV,) + CONV_SHARD).transpose(1, 0, 2).reshape(CONV_SHAPE)

    loss_lanes, dx, grads = _local_step(x[0], positions[0], loss_target[0], full)
    loss = lax.psum(jnp.sum(loss_lanes), MESH_AXES)

    def by_device(a):
        return a.astype(BF16).reshape((N_DEV, a.shape[0] // N_DEV) + a.shape[1:])

    landed = _scatter("scatter_grads", [by_device(a) for a in grads["ffn"]] + [by_device(grads[t]) for t, _ in OTHER.values()])
    sums = {n: _sum_blocks("sum_" + n, blocks) for n, blocks in zip(list(FFN_NAMES) + list(OTHER), landed)}
    grad = {n: (sums[n].T if n in BY_COLUMNS else sums[n]) for n in sums}
    small_sum = _all_reduce_small("reduce_small", _pack_small(grads, grads["gdn_conv_w"].reshape(-1), REDUCE_ROWS))
    small_grad, conv_grad_full = _unpack_small(small_sum)
    grad.update(small_grad)
    grad["gdn_conv_w"] = lax.dynamic_slice(conv_grad_full[:CONV_LANES].reshape(CONV_SHAPE), (0, me * CONV_SHARD[1]), CONV_SHARD)

    outs = {"grad": grad, "delta": {}, "new_m": {}, "new_v": {}}
    for n in list(FFN_NAMES) + list(OTHER):
        outs["delta"][n], outs["new_m"][n], outs["new_v"][n] = _adamw("adamw_" + n, wts[n], grad[n], mom[n], var[n])
    small = [_pack_small(s, s["gdn_conv_w"].reshape(-1), SMALL_ROWS) for s in (wts, grad, mom, var)]
    for kind, s in zip(("delta", "new_m", "new_v"), _adamw("adamw_small", *small)):
        vecs, conv = _unpack_small(s)
        outs[kind].update(vecs)
        outs[kind]["gdn_conv_w"] = conv[:CONV_SHARD[0] * CONV_SHARD[1]].reshape(CONV_SHARD)
    result = [loss, dx[None]]
    for kind in ("grad", "delta", "new_m", "new_v"):
        result += [outs[kind][n].reshape(given[n].shape) for n in order]
    return tuple(result)
```

```python
import jax
import jax.numpy as jnp
from jax import lax
from jax.experimental import pallas as pl
from jax.experimental.pallas import tpu as pltpu

F32 = jnp.float32
BF16 = jnp.bfloat16
HI = lax.Precision.HIGH
EXACT = lax.Precision.HIGHEST

N_DEV = 8
D_MODEL = 1024
D_FF = 2816
N_HEADS = 8
SLOT = 128
MLA_Q_RANK = 256
MLA_KV_RANK = 128
MLA_NOPE = 64
MLA_ROPE = 32
MLA_V = 64
GDN_D = 64
GDN_CONV = 4
GDN_CHUNK = 64
ROPE_THETA = 10000.0
EPS = 1e-6
ADAM_LR, ADAM_B1, ADAM_B2, ADAM_EPS, ADAM_WD, ADAM_STEP = 0.001, 0.9, 0.999, 1e-08, 0.01, 10


def _dot(a, b, ca, cb, precision=None):
    lead = a.ndim - 2
    batch = tuple(range(lead))
    return lax.dot_general(a, b, (((lead + ca,), (lead + cb,)), (batch, batch)), precision=precision,
                           preferred_element_type=F32)


def _nn(a, b, precision=None):
    return _dot(a, b, 1, 0, precision)


def _nt(a, b, precision=None):
    return _dot(a, b, 1, 1, precision)


def _tn(a, b, precision=None):
    return _dot(a, b, 0, 0, precision)


def _sigmoid(x):
    return 1.0 / (1.0 + jnp.exp(-x))


def _silu(x):
    return x * _sigmoid(x)


def _rms(x, g, n):
    ms = jnp.sum(x * x, axis=-1, keepdims=True) * (1.0 / n)
    return x * lax.rsqrt(ms + EPS) * g


def _chunk_masks():
    c = GDN_CHUNK
    i = lax.broadcasted_iota(jnp.int32, (c, c), 0)
    j = lax.broadcasted_iota(jnp.int32, (c, c), 1)
    lower = i >= j
    strict = i > j
    eye = (i == j).astype(F32)
    blocks = []
    b = 1
    while b < c:
        same = (i // (2 * b)) == (j // (2 * b))
        blocks.append(same & ((i % (2 * b)) >= b) & ((j % (2 * b)) < b))
        b *= 2
    return lower, strict, eye, blocks


def _unit_lower_inverse(low, eye, blocks):
    t = jnp.broadcast_to(eye, low.shape)
    for m in blocks:
        lo = jnp.where(m, low, 0.0)
        t = t - _nn(t, _nn(lo, t, HI), HI)
    return t


@jax.custom_vjp
def _known_inverse(low, tinv):
    return tinv


def _known_inverse_fwd(low, tinv):
    return tinv, tinv


def _known_inverse_bwd(tinv, dt):
    return -_tn(tinv, _nt(dt, tinv, HI), HI), jnp.zeros_like(tinv)


_known_inverse.defvjp(_known_inverse_fwd, _known_inverse_bwd)


def _gdn_chunk(q, k, v, gb, bb, s, masks, tinv=None):
    lower, strict, eye, blocks = masks
    qs = q * (GDN_D ** -0.5)
    gc = _nn(jnp.broadcast_to(lower.astype(F32), gb.shape), gb, EXACT)
    gct = _nt(jnp.broadcast_to(eye, gb.shape), gc, EXACT)
    decay = jnp.exp(jnp.where(lower, gc - gct, -1e30))
    kb = k * bb
    low = jnp.where(strict, _nt(kb, k, HI) * decay, 0.0)
    tinv = _unit_lower_inverse(low, eye, blocks) if tinv is None else _known_inverse(low, tinv)
    eg = jnp.exp(gc)
    w = _nn(tinv, kb * eg, HI)
    u = _nn(tinv, v * bb, HI)
    attn = _nt(qs, k, HI) * decay
    g_end = jnp.sum(gb, axis=-2, keepdims=True)
    k_dec = k * jnp.exp(g_end - gc)
    v_new = u - _nn(w, s, HI)
    o = _nn(qs * eg, s, HI) + _nn(attn, v_new, HI)
    s_new = s * jnp.exp(g_end) + _tn(k_dec, v_new, HI)
    return o, s_new, tinv


GDN_GROUP = 8
GDN_GROUPS = N_HEADS // GDN_GROUP


def _group_heads(ref):
    return jnp.stack([ref[:, pl.ds(j * SLOT, GDN_D)] for j in range(GDN_GROUP)])


def _ungroup_heads(ref, val):
    pad = jnp.zeros((GDN_CHUNK, SLOT - GDN_D), F32)
    for j in range(GDN_GROUP):
        ref[:, pl.ds(j * SLOT, GDN_D)] = val[j]
        ref[:, pl.ds(j * SLOT + GDN_D, SLOT - GDN_D)] = pad


def _gdn_fwd(qkv, gb, bb):
    t = qkv.shape[0]
    n_chunks = t // GDN_CHUNK
    d = GDN_D

    def body(q_ref, k_ref, v_ref, g_ref, b_ref, o_ref, keep_ref, s_ref):
        @pl.when(pl.program_id(1) == 0)
        def _():
            s_ref[...] = jnp.zeros_like(s_ref)

        s = s_ref[...]
        keep_ref[:, 0, 0] = s
        o, s_new, tinv = _gdn_chunk(*[_group_heads(r) for r in (q_ref, k_ref, v_ref, g_ref, b_ref)], s, _chunk_masks())
        keep_ref[:, 0, 1] = tinv
        s_ref[...] = s_new
        _ungroup_heads(o_ref, o)

    def spec(kind=0):
        return pl.BlockSpec((GDN_CHUNK, GDN_GROUP * SLOT), lambda h, n: (n, kind * GDN_GROUPS + h))

    return pl.pallas_call(
        body, name="gdn_fwd",
        grid=(GDN_GROUPS, n_chunks),
        in_specs=[spec(0), spec(1), spec(2), spec(), spec()],
        out_specs=[spec(), pl.BlockSpec((GDN_GROUP, 1, 2, d, d), lambda h, n: (h, n, 0, 0, 0))],
        out_shape=[jax.ShapeDtypeStruct((t, N_HEADS * SLOT), F32), jax.ShapeDtypeStruct((N_HEADS, n_chunks, 2, d, d), F32)],
        scratch_shapes=[pltpu.VMEM((GDN_GROUP, d, d), F32)],
        compiler_params=pltpu.CompilerParams(dimension_semantics=("parallel", "arbitrary")),
    )(qkv, qkv, qkv, gb, bb)


def _gdn_bwd(qkv, gb, bb, keep, do, carry=None):
    t = qkv.shape[0]
    n_chunks = t // GDN_CHUNK
    d = GDN_D

    def body(q_ref, k_ref, v_ref, g_ref, b_ref, keep_ref, do_ref, dqkv_ref, dg_ref, db_ref, ds_ref):
        @pl.when(pl.program_id(1) == 0)
        def _():
            ds_ref[...] = jnp.zeros_like(ds_ref)

        masks = _chunk_masks()
        tinv = keep_ref[:, 0, 1]
        _, pull = jax.vjp(lambda *a: _gdn_chunk(*a, masks, tinv)[:2],
                          *[_group_heads(r) for r in (q_ref, k_ref, v_ref, g_ref, b_ref)], keep_ref[:, 0, 0])
        dq, dk, dv, dg, db, ds = pull((_group_heads(do_ref), ds_ref[...]))
        ds_ref[...] = ds
        for i, val in enumerate((dq, dk, dv)):
            _ungroup_heads(dqkv_ref.at[i], val)
        _ungroup_heads(dg_ref, dg)
        _ungroup_heads(db_ref, db)

    def spec(kind=0):
        return pl.BlockSpec((GDN_CHUNK, GDN_GROUP * SLOT), lambda h, n: (n_chunks - 1 - n, kind * GDN_GROUPS + h))

    return _call_carrying(
        body, carry, (qkv, qkv, qkv, gb, bb, keep, do), name="gdn_bwd",
        grid=(GDN_GROUPS, n_chunks),
        in_specs=[spec(0), spec(1), spec(2), spec(), spec(),
                  pl.BlockSpec((GDN_GROUP, 1, 2, d, d), lambda h, n: (h, n_chunks - 1 - n, 0, 0, 0)), spec()],
        out_specs=[pl.BlockSpec((3, GDN_CHUNK, GDN_GROUP * SLOT), lambda h, n: (0, n_chunks - 1 - n, h)), spec(), spec()],
        out_shape=[jax.ShapeDtypeStruct((3, t, N_HEADS * SLOT), F32)] + [jax.ShapeDtypeStruct((t, N_HEADS * SLOT), F32)] * 2,
        scratch_shapes=[pltpu.VMEM((GDN_GROUP, d, d), F32)],
        compiler_params=pltpu.CompilerParams(dimension_semantics=("arbitrary", "arbitrary")),
    )


def _rowwise(name, fn, rows, consts, outs, sums=(), tm=256):
    rows = [x if isinstance(x, tuple) else (x, x.shape[1], 0) for x in rows]
    t = rows[0][0].shape[0]
    tm = min(tm, t)
    steps = t // tm
    n_r, n_c, n_o, n_s = len(rows), len(consts), len(outs), len(sums)

    def window(width, block):
        return pl.BlockSpec((tm, width), lambda i: (i, block))

    def body(*refs):
        r, c = refs[:n_r], refs[n_r:n_r + n_c]
        o, s = refs[n_r + n_c:n_r + n_c + n_o], refs[n_r + n_c + n_o:]
        vals, tot = fn([x[...] for x in r], [x[...] for x in c])
        for ref, val in zip(o, vals):
            ref[...] = val.astype(ref.dtype)
        if n_s:
            @pl.when(pl.program_id(0) == 0)
            def _():
                for ref in s:
                    ref[...] = jnp.zeros_like(ref)

            for ref, val in zip(s, tot):
                ref[...] += val

    return pl.pallas_call(
        body, name=name,
        grid=(steps,),
        in_specs=[window(w, b) for _, w, b in rows] + [pl.BlockSpec(x.shape, lambda i: (0, 0)) for x in consts],
        out_specs=[pl.BlockSpec((tm, w), lambda i: (i, 0)) for w, _ in outs]
        + [pl.BlockSpec((1, w), lambda i: (0, 0)) for w in sums],
        out_shape=[jax.ShapeDtypeStruct((t, w), dt) for w, dt in outs]
        + [jax.ShapeDtypeStruct((1, w), F32) for w in sums],
        compiler_params=pltpu.CompilerParams(dimension_semantics=("arbitrary",)),
    )(*[x for x, _, _ in rows], *consts)


def _tile(dim, target):
    if dim <= target:
        return dim
    best = None
    for cand in range(128, target + 1, 128):
        if dim % cand == 0:
            best = cand
    assert best is not None, (dim, target)
    return best


def _matmul(name, a, b, mode, out_dtype=F32, tm=512, tn=1024, tk=1024):
    if mode == "nn":
        (m, k), n = a.shape, b.shape[1]
    elif mode == "nt":
        (m, k), n = a.shape, b.shape[0]
    else:
        (k, m), n = a.shape, b.shape[1]
    tm, tn, tk = _tile(m, tm), _tile(n, tn), _tile(k, tk)
    k_steps = k // tk
    product = {"nn": _nn, "nt": _nt, "tn": _tn}[mode]

    def body(a_ref, b_ref, o_ref, acc_ref):
        part = product(a_ref[...].astype(BF16), b_ref[...].astype(BF16))
        if k_steps == 1:
            o_ref[...] = part.astype(o_ref.dtype)
        else:
            kk = pl.program_id(2)

            @pl.when(kk == 0)
            def _():
                acc_ref[...] = part

            @pl.when(kk > 0)
            def _():
                acc_ref[...] += part

            @pl.when(kk == k_steps - 1)
            def _():
                o_ref[...] = acc_ref[...].astype(o_ref.dtype)

    a_spec = pl.BlockSpec((tk, tm), lambda i, j, kk: (kk, i)) if mode == "tn" else pl.BlockSpec((tm, tk), lambda i, j, kk: (i, kk))
    b_spec = pl.BlockSpec((tn, tk), lambda i, j, kk: (j, kk)) if mode == "nt" else pl.BlockSpec((tk, tn), lambda i, j, kk: (kk, j))
    return pl.pallas_call(
        body, name=name,
        grid=(m // tm, n // tn, k_steps),
        in_specs=[a_spec, b_spec],
        out_specs=pl.BlockSpec((tm, tn), lambda i, j, kk: (i, j)),
        out_shape=jax.ShapeDtypeStruct((m, n), out_dtype),
        scratch_shapes=[pltpu.VMEM((tm, tn) if k_steps > 1 else (8, 128), F32)],
        compiler_params=pltpu.CompilerParams(dimension_semantics=("parallel", "parallel", "arbitrary")),
    )(a, b)


FFN_TM = 512
FFN_BWD_TM = 256
FFN_BLOCKS = 4
FFN_GATE, FFN_UP, FFN_DOWN = 0, 1, 2


def _ffn_weight_specs(ffn_w, first):
    _, _, rows, dm = ffn_w.shape

    def spec(k):
        return pl.BlockSpec((FFN_BLOCKS, None, rows, dm), lambda i, j: (j, first + k, 0, 0))

    return [spec(FFN_GATE), spec(FFN_UP), spec(FFN_DOWN)], FFN_BLOCKS * rows


def _ffn_fwd(name, x, g_pre, ffn_w, first, g_post, carry=None):
    t, dm = x.shape
    tm = min(FFN_TM, t)
    w_specs, tf = _ffn_weight_specs(ffn_w, first)
    f_steps = N_DEV // FFN_BLOCKS

    def body(x_ref, gpre_ref, wg_ref, wu_ref, wd_ref, gpost_ref, h_ref, y_ref, xn_ref, acc_ref):
        j = pl.program_id(1)

        @pl.when(j == 0)
        def _():
            xn_ref[...] = _rms(x_ref[...], gpre_ref[...], dm).astype(BF16)
            acc_ref[...] = jnp.zeros_like(acc_ref)

        xn = xn_ref[...]
        wg, wu, wd = (r[...].reshape(tf, dm) for r in (wg_ref, wu_ref, wd_ref))
        a = _silu(_nt(xn, wg)) * _nt(xn, wu)
        acc_ref[...] += _nn(a.astype(BF16), wd)

        @pl.when(j == f_steps - 1)
        def _():
            h = acc_ref[...]
            h_ref[...] = h
            y_ref[...] = x_ref[...] + 0.5 * _rms(h, gpost_ref[...], dm)

    row = pl.BlockSpec((tm, dm), lambda i, j: (i, 0))
    vec = pl.BlockSpec((1, dm), lambda i, j: (0, 0))
    return _call_carrying(
        body, carry, (x, g_pre, ffn_w, ffn_w, ffn_w, g_post), name=name,
        grid=(t // tm, f_steps),
        in_specs=[row, vec, *w_specs, vec],
        out_specs=[row, row],
        out_shape=[jax.ShapeDtypeStruct((t, dm), F32)] * 2,
        scratch_shapes=[pltpu.VMEM((tm, dm), BF16), pltpu.VMEM((tm, dm), F32)],
        compiler_params=pltpu.CompilerParams(dimension_semantics=("arbitrary", "arbitrary")),
    )


def _ffn_bwd(name, x, h, dy, g_pre, ffn_w, first, g_post, carry=None):
    t, dm = x.shape
    tm = min(FFN_BWD_TM, t)
    w_specs, tf = _ffn_weight_specs(ffn_w, first)
    f_steps = N_DEV // FFN_BLOCKS
    f = f_steps * tf

    def post(hv, g):
        return 0.5 * _rms(hv, g, dm)

    def pre(xv, g):
        return _rms(xv, g, dm)

    def body(x_ref, h_ref, dy_ref, gpre_ref, wg_ref, wu_ref, wd_ref, gpost_ref,
             dx_ref, xn_ref, dh_ref, a_ref, dhg_ref, dhu_ref, dgpre_ref, dgpost_ref, acc_ref):
        i, j = pl.program_id(0), pl.program_id(1)

        @pl.when((i == 0) & (j == 0))
        def _():
            dgpre_ref[...] = jnp.zeros_like(dgpre_ref)
            dgpost_ref[...] = jnp.zeros_like(dgpost_ref)

        @pl.when(j == 0)
        def _():
            xn_ref[...] = pre(x_ref[...], gpre_ref[...]).astype(BF16)
            _, pull = jax.vjp(post, h_ref[...], gpost_ref[...])
            dh, dg = pull(dy_ref[...])
            dh_ref[...] = dh.astype(BF16)
            dgpost_ref[...] += dg
            acc_ref[...] = jnp.zeros_like(acc_ref)

        xn = xn_ref[...]
        wg, wu, wd = (r[...].reshape(tf, dm) for r in (wg_ref, wu_ref, wd_ref))
        hg = _nt(xn, wg)
        hu = _nt(xn, wu)
        da = _nt(dh_ref[...], wd)
        sig = _sigmoid(hg)
        act = hg * sig
        dhu = (da * act).astype(BF16)
        dhg = (da * hu * (sig * (1.0 + hg * (1.0 - sig)))).astype(BF16)
        a_ref[...] = (act * hu).astype(BF16)
        dhg_ref[...] = dhg
        dhu_ref[...] = dhu
        acc_ref[...] += _nn(dhg, wg) + _nn(dhu, wu)

        @pl.when(j == f_steps - 1)
        def _():
            _, pull = jax.vjp(pre, x_ref[...], gpre_ref[...])
            dx, dg = pull(acc_ref[...])
            dx_ref[...] = dy_ref[...] + dx
            dgpre_ref[...] += dg

    row = pl.BlockSpec((tm, dm), lambda i, j: (i, 0))
    vec = pl.BlockSpec((1, dm), lambda i, j: (0, 0))
    wide = pl.BlockSpec((tm, tf), lambda i, j: (i, j))
    return _call_carrying(
        body, carry, (x, h, dy, g_pre, ffn_w, ffn_w, ffn_w, g_post), name=name,
        grid=(t // tm, f_steps),
        in_specs=[row, row, row, vec, *w_specs, vec],
        out_specs=[row, row, row, wide, wide, wide, vec, vec],
        out_shape=[jax.ShapeDtypeStruct((t, dm), F32), jax.ShapeDtypeStruct((t, dm), BF16), jax.ShapeDtypeStruct((t, dm), BF16),
                   jax.ShapeDtypeStruct((t, f), BF16), jax.ShapeDtypeStruct((t, f), BF16), jax.ShapeDtypeStruct((t, f), BF16),
                   jax.ShapeDtypeStruct((1, dm), F32), jax.ShapeDtypeStruct((1, dm), F32)],
        scratch_shapes=[pltpu.VMEM((tm, dm), F32)],
        compiler_params=pltpu.CompilerParams(dimension_semantics=("arbitrary", "arbitrary")),
    )


ATT_T = 512
ATT_GROUP = 2
ATT_SCALE = (MLA_NOPE + MLA_ROPE) ** -0.5


def _stack_slots(ref, group):
    return jnp.stack([ref[:, pl.ds(j * SLOT, SLOT)] for j in range(group)])


def _unstack_slots(ref, val):
    for j in range(val.shape[0]):
        ref[:, pl.ds(j * SLOT, SLOT)] = val[j].astype(ref.dtype)


def _scores(q, k, diagonal):
    s = _nt(q, k) * ATT_SCALE
    if diagonal:
        row = lax.broadcasted_iota(jnp.int32, s.shape[1:], 0)
        col = lax.broadcasted_iota(jnp.int32, s.shape[1:], 1)
        s = jnp.where(col <= row, s, -1e30)
    return s


def _attn_specs(tile, q_major):
    width = ATT_GROUP * SLOT
    if q_major:
        return (pl.BlockSpec((tile, width), lambda h, qi, ki: (qi, h)),
                pl.BlockSpec((tile, width), lambda h, qi, ki: (jnp.minimum(ki, qi), h)))
    return (pl.BlockSpec((tile, width), lambda h, ki, qi: (jnp.maximum(qi, ki), h)),
            pl.BlockSpec((tile, width), lambda h, ki, qi: (ki, h)))


def _attn_fwd(q, k, v):
    t = q.shape[0]
    tile = min(ATT_T, t)
    steps = t // tile
    g = ATT_GROUP

    def body(q_ref, k_ref, v_ref, o_ref, lse_ref, m_ref, l_ref, acc_ref):
        qi, ki = pl.program_id(1), pl.program_id(2)

        @pl.when(ki == 0)
        def _():
            m_ref[...] = jnp.full_like(m_ref, -1e30)
            l_ref[...] = jnp.zeros_like(l_ref)
            acc_ref[...] = jnp.zeros_like(acc_ref)

        def step(diagonal):
            s = _scores(_stack_slots(q_ref, g), _stack_slots(k_ref, g), diagonal)
            m_old = m_ref[...]
            m_new = jnp.maximum(m_old, jnp.max(s, axis=-1, keepdims=True))
            p = jnp.exp(s - m_new)
            alpha = jnp.exp(m_old - m_new)
            l_ref[...] = alpha * l_ref[...] + jnp.sum(p, axis=-1, keepdims=True)
            acc_ref[...] = alpha * acc_ref[...] + _nn(p.astype(BF16), _stack_slots(v_ref, g))
            m_ref[...] = m_new

        @pl.when(ki < qi)
        def _():
            step(False)

        @pl.when(ki == qi)
        def _():
            step(True)
            _unstack_slots(o_ref, acc_ref[...] / l_ref[...])
            _unstack_slots(lse_ref, jnp.broadcast_to(m_ref[...] + jnp.log(l_ref[...]), acc_ref.shape))

    q_spec, k_spec = _attn_specs(tile, True)
    return pl.pallas_call(
        body, name="attn_fwd",
        grid=(N_HEADS // g, steps, steps),
        in_specs=[q_spec, k_spec, k_spec],
        out_specs=[q_spec, q_spec],
        out_shape=[jax.ShapeDtypeStruct((t, N_HEADS * SLOT), F32)] * 2,
        scratch_shapes=[pltpu.VMEM((g, tile, 1), F32), pltpu.VMEM((g, tile, 1), F32), pltpu.VMEM((g, tile, SLOT), F32)],
        compiler_params=pltpu.CompilerParams(dimension_semantics=("parallel", "parallel", "arbitrary")),
    )(q, k, v)


def _attn_grad_scores(q, k, v, do, lse_ref, delta_ref, diagonal):
    g = ATT_GROUP
    p = jnp.exp(_scores(q, k, diagonal) - _stack_slots(lse_ref, g)[:, :, 0:1])
    dp = _nt(do, v)
    return p, p * (dp - _stack_slots(delta_ref, g)[:, :, 0:1]) * ATT_SCALE


def _attn_bwd_q(q, k, v, do, lse, delta):
    t = q.shape[0]
    tile = min(ATT_T, t)
    steps = t // tile
    g = ATT_GROUP

    def body(q_ref, k_ref, v_ref, do_ref, lse_ref, delta_ref, dq_ref, acc_ref):
        qi, ki = pl.program_id(1), pl.program_id(2)

        @pl.when(ki == 0)
        def _():
            acc_ref[...] = jnp.zeros_like(acc_ref)

        def step(diagonal):
            kk = _stack_slots(k_ref, g)
            _, ds = _attn_grad_scores(_stack_slots(q_ref, g), kk, _stack_slots(v_ref, g),
                                      _stack_slots(do_ref, g).astype(BF16), lse_ref, delta_ref, diagonal)
            acc_ref[...] += _nn(ds.astype(BF16), kk)

        @pl.when(ki < qi)
        def _():
            step(False)

        @pl.when(ki == qi)
        def _():
            step(True)
            _unstack_slots(dq_ref, acc_ref[...])

    q_spec, k_spec = _attn_specs(tile, True)
    return pl.pallas_call(
        body, name="attn_bwd_q",
        grid=(N_HEADS // g, steps, steps),
        in_specs=[q_spec, k_spec, k_spec, q_spec, q_spec, q_spec],
        out_specs=q_spec,
        out_shape=jax.ShapeDtypeStruct((t, N_HEADS * SLOT), F32),
        scratch_shapes=[pltpu.VMEM((g, tile, SLOT), F32)],
        compiler_params=pltpu.CompilerParams(dimension_semantics=("parallel", "parallel", "arbitrary")),
    )(q, k, v, do, lse, delta)


def _attn_bwd_kv(q, k, v, do, lse, delta):
    t = q.shape[0]
    tile = min(ATT_T, t)
    steps = t // tile
    g = ATT_GROUP

    def body(q_ref, k_ref, v_ref, do_ref, lse_ref, delta_ref, dk_ref, dv_ref, dk_acc, dv_acc):
        ki, qi = pl.program_id(1), pl.program_id(2)

        @pl.when(qi == 0)
        def _():
            dk_acc[...] = jnp.zeros_like(dk_acc)
            dv_acc[...] = jnp.zeros_like(dv_acc)

        def step(diagonal):
            qq = _stack_slots(q_ref, g)
            do_b = _stack_slots(do_ref, g).astype(BF16)
            p, ds = _attn_grad_scores(qq, _stack_slots(k_ref, g), _stack_slots(v_ref, g), do_b, lse_ref, delta_ref, diagonal)
            dv_acc[...] += _tn(p.astype(BF16), do_b)
            dk_acc[...] += _tn(ds.astype(BF16), qq)

        @pl.when(qi > ki)
        def _():
            step(False)

        @pl.when(qi == ki)
        def _():
            step(True)

        @pl.when(qi == steps - 1)
        def _():
            _unstack_slots(dk_ref, dk_acc[...])
            _unstack_slots(dv_ref, dv_acc[...])

    q_spec, k_spec = _attn_specs(tile, False)
    return pl.pallas_call(
        body, name="attn_bwd_kv",
        grid=(N_HEADS // g, steps, steps),
        in_specs=[q_spec, k_spec, k_spec, q_spec, q_spec, q_spec],
        out_specs=[k_spec, k_spec],
        out_shape=[jax.ShapeDtypeStruct((t, N_HEADS * SLOT), F32)] * 2,
        scratch_shapes=[pltpu.VMEM((g, tile, SLOT), F32), pltpu.VMEM((g, tile, SLOT), F32)],
        compiler_params=pltpu.CompilerParams(dimension_semantics=("parallel", "parallel", "arbitrary")),
    )(q, k, v, do, lse, delta)


def _shift_down(x, s):
    if s == 0:
        return x
    row = lax.broadcasted_iota(jnp.int32, x.shape, 0)
    return jnp.where(row >= s, pltpu.roll(x, s, 0), 0.0)


def _shift_up(x, s):
    if s == 0:
        return x
    n = x.shape[0]
    row = lax.broadcasted_iota(jnp.int32, x.shape, 0)
    return jnp.where(row < n - s, pltpu.roll(x, n - s, 0), 0.0)


def _l2norm(x):
    return x * lax.rsqrt(jnp.sum(x * x, axis=-1, keepdims=True) + EPS)


def _conv_pre(x, w):
    y = w[GDN_CONV - 1:GDN_CONV, :] * x
    for s in range(1, GDN_CONV):
        y = y + w[GDN_CONV - 1 - s:GDN_CONV - s, :] * _shift_down(x, s)
    return y


def _gdn_conv_fwd(x, w):
    t, width = x.shape

    def body(x_ref, w_ref, o_ref):
        act = _silu(_conv_pre(x_ref[...], w_ref[...]))
        normed = pl.program_id(0) < 2 * N_HEADS
        o_ref[...] = jnp.where(normed, _l2norm(act), act)

    return pl.pallas_call(
        body, name="gdn_conv_fwd",
        grid=(width // SLOT,),
        in_specs=[pl.BlockSpec((t, SLOT), lambda j: (0, j)), pl.BlockSpec((GDN_CONV, SLOT), lambda j: (0, j))],
        out_specs=pl.BlockSpec((t, SLOT), lambda j: (0, j)),
        out_shape=jax.ShapeDtypeStruct((t, width), F32),
        compiler_params=pltpu.CompilerParams(dimension_semantics=("parallel",)),
    )(x, w)


def _gdn_conv_bwd(x, w, dout):
    t, width = x.shape

    def body(x_ref, w_ref, do_ref, dx_ref, dw_ref):
        xv, wv = x_ref[...], w_ref[...]
        y = _conv_pre(xv, wv)
        sig = _sigmoid(y)
        act = y * sig
        _, pull = jax.vjp(_l2norm, act)
        normed = pl.program_id(0) < 2 * N_HEADS
        dact = jnp.where(normed, pull(do_ref[0])[0], do_ref[0])
        dy = dact * (sig * (1.0 + y * (1.0 - sig)))
        dx = wv[GDN_CONV - 1:GDN_CONV, :] * dy
        for s in range(1, GDN_CONV):
            dx = dx + wv[GDN_CONV - 1 - s:GDN_CONV - s, :] * _shift_up(dy, s)
        dx_ref[...] = dx.astype(BF16)
        for s in range(GDN_CONV):
            dw_ref[GDN_CONV - 1 - s:GDN_CONV - s, :] = jnp.sum(dy * _shift_down(xv, s), axis=0, keepdims=True)

    col = pl.BlockSpec((t, SLOT), lambda j: (0, j))
    tap = pl.BlockSpec((GDN_CONV, SLOT), lambda j: (0, j))
    return pl.pallas_call(
        body, name="gdn_conv_bwd",
        grid=(width // SLOT,),
        in_specs=[col, tap, pl.BlockSpec((1, t, SLOT), lambda j: (j // N_HEADS, 0, j % N_HEADS))],
        out_specs=[col, tap],
        out_shape=[jax.ShapeDtypeStruct((t, width), BF16), jax.ShapeDtypeStruct((GDN_CONV, width), F32)],
        compiler_params=pltpu.CompilerParams(dimension_semantics=("parallel",)),
    )(x, w, dout)


def _softplus(x):
    e = jnp.exp(-jnp.abs(x))
    u = 1.0 + e
    log1p = jnp.where(u == 1.0, e, jnp.log(u) * e / jnp.where(u == 1.0, 1.0, u - 1.0))
    return jnp.maximum(x, 0.0) + log1p


def _gates_fwd(ab, a_log, dt_bias):
    def fn(rows, consts):
        (abv,), (alog, dtb) = rows, consts
        g = -jnp.exp(alog) * _softplus(abv + dtb)
        beta = _sigmoid(abv)
        shape = (abv.shape[0], SLOT)
        g_slots = [jnp.broadcast_to(g[:, h:h + 1], shape) for h in range(N_HEADS)]
        b_slots = [jnp.broadcast_to(beta[:, N_HEADS + h:N_HEADS + h + 1], shape) for h in range(N_HEADS)]
        return [jnp.concatenate(g_slots, axis=1), jnp.concatenate(b_slots, axis=1)], []

    width = N_HEADS * SLOT
    return _rowwise("gdn_gates_fwd", fn, [ab], [a_log, dt_bias], [(width, F32), (width, F32)])


def _gates_bwd(ab, a_log, dt_bias, dg, dbeta):
    def fn(rows, consts):
        (abv, dgv, dbv), (alog, dtb) = rows, consts
        lane = lax.broadcasted_iota(jnp.int32, abv.shape, 1)
        dg_tok = jnp.zeros_like(abv)
        db_tok = jnp.zeros_like(abv)
        for h in range(N_HEADS):
            dg_tok = dg_tok + jnp.where(lane == h, jnp.sum(dgv[:, h * SLOT:(h + 1) * SLOT], axis=1, keepdims=True), 0.0)
            db_tok = db_tok + jnp.where(lane == N_HEADS + h, jnp.sum(dbv[:, h * SLOT:(h + 1) * SLOT], axis=1, keepdims=True), 0.0)
        xa = abv + dtb
        g = -jnp.exp(alog) * _softplus(xa)
        da = dg_tok * (-jnp.exp(alog)) * _sigmoid(xa)
        beta = _sigmoid(abv)
        dab = jnp.where(lane < N_HEADS, da, db_tok * beta * (1.0 - beta))
        dab = jnp.where(lane < 2 * N_HEADS, dab, 0.0)
        d_alog = jnp.sum(jnp.where(lane < N_HEADS, dg_tok * g, 0.0), axis=0, keepdims=True)
        d_dtb = jnp.sum(jnp.where(lane < N_HEADS, da, 0.0), axis=0, keepdims=True)
        return [dab], [d_alog, d_dtb]

    return _rowwise("gdn_gates_bwd", fn, [ab, dg, dbeta], [a_log, dt_bias], [(SLOT, F32)], sums=[SLOT, SLOT])


ROPE_HALF = MLA_ROPE // 2


def _rope_tables(positions):
    freqs = ROPE_THETA ** (-jnp.arange(ROPE_HALF, dtype=F32) / ROPE_HALF)
    ang = positions.astype(F32)[:, None] * freqs
    cos, sin = jnp.cos(ang), jnp.sin(ang)
    t = positions.shape[0]
    ones, zeros = jnp.ones((t, MLA_NOPE), F32), jnp.zeros((t, MLA_NOPE), F32)
    tail = jnp.zeros((t, SLOT - MLA_NOPE - MLA_ROPE), F32)
    half0 = jnp.zeros((t, ROPE_HALF), F32)
    same = jnp.concatenate([ones, cos, cos, tail], axis=1)
    from_low = jnp.concatenate([zeros, half0, sin, tail], axis=1)
    from_high = jnp.concatenate([zeros, -sin, half0, tail], axis=1)
    return same, from_low, from_high


def _rope(x, tabs):
    same, from_low, from_high = tabs
    width = x.shape[1]
    return x * same + pltpu.roll(x, ROPE_HALF, 1) * from_low + pltpu.roll(x, width - ROPE_HALF, 1) * from_high


def _rope_transposed(dy, tabs):
    same, from_low, from_high = tabs
    width = dy.shape[1]
    return dy * same + pltpu.roll(dy * from_low, width - ROPE_HALF, 1) + pltpu.roll(dy * from_high, ROPE_HALF, 1)


def _tile_slots(tab):
    return jnp.concatenate([tab] * N_HEADS, axis=1)


A_WIDTH = MLA_Q_RANK + MLA_KV_RANK + 2 * SLOT
A_KPE = MLA_Q_RANK + MLA_KV_RANK
A_AB = A_KPE + SLOT
WIDE = N_HEADS * SLOT


def _mla_pre_fwd(proj_a, tabs, g_q, g_kv):
    def fn(rows, consts):
        pa, *tb = rows
        gq, gkv = consts
        return [_rms(pa[:, :MLA_Q_RANK], gq, MLA_Q_RANK), _rms(pa[:, MLA_Q_RANK:A_KPE], gkv, MLA_KV_RANK),
                _rope(pa[:, A_KPE:A_AB], tb)], []

    return _rowwise("mla_pre_fwd", fn, [proj_a, *tabs], [g_q, g_kv], [(MLA_Q_RANK, BF16), (MLA_KV_RANK, BF16), (SLOT, F32)])


def _mla_pre_bwd(proj_a, tabs, g_q, g_kv, dcqn, dckvn, dkpe, dab):
    def fn(rows, consts):
        pa, t0, t1, t2, dq, dkv, dk, da = rows
        gq, gkv = consts
        _, pull_q = jax.vjp(lambda x, g: _rms(x, g, MLA_Q_RANK), pa[:, :MLA_Q_RANK], gq)
        _, pull_kv = jax.vjp(lambda x, g: _rms(x, g, MLA_KV_RANK), pa[:, MLA_Q_RANK:A_KPE], gkv)
        dcq, dgq = pull_q(dq)
        dckv, dgkv = pull_kv(dkv)
        return [jnp.concatenate([dcq, dckv, _rope_transposed(dk, (t0, t1, t2)), da], axis=1)], [dgq, dgkv]

    return _rowwise("mla_pre_bwd", fn, [proj_a, *tabs, dcqn, dckvn, dkpe, dab], [g_q, g_kv], [(A_WIDTH, BF16)],
                    sums=[MLA_Q_RANK, MLA_KV_RANK])


def _mla_qkv_fwd(q_p, kv_p, kpe, tabs):
    def fn(rows, consts):
        qv, kvv, kp, *tb = rows
        q = _rope(qv, [_tile_slots(x) for x in tb])
        k = kvv[:, :WIDE] + _tile_slots(kp)
        return [q, k, kvv[:, WIDE:]], []

    return _rowwise("mla_qkv_fwd", fn, [q_p, kv_p, kpe, *tabs], [], [(WIDE, BF16)] * 3)


def _mla_qkv_bwd(dq, dk, dv, tabs):
    def fn(rows, consts):
        dqv, dkv, dvv, *tb = rows
        dkpe = dkv[:, :SLOT]
        for h in range(1, N_HEADS):
            dkpe = dkpe + dkv[:, h * SLOT:(h + 1) * SLOT]
        return [_rope_transposed(dqv, [_tile_slots(x) for x in tb]), jnp.concatenate([dkv, dvv], axis=1), dkpe], []

    return _rowwise("mla_qkv_bwd", fn, [dq, dk, dv, *tabs], [], [(WIDE, BF16), (2 * WIDE, BF16), (SLOT, F32)])


def _slot_sum(x):
    parts = [jnp.broadcast_to(jnp.sum(x[:, h * SLOT:(h + 1) * SLOT], axis=1, keepdims=True), (x.shape[0], SLOT))
             for h in range(N_HEADS)]
    return jnp.concatenate(parts, axis=1)


def _mix_join(o_mla, o_gdn, gate, g_mla, g_gdn):
    mla = _rms(o_mla, g_mla, N_HEADS * MLA_V)
    gdn = o_gdn * lax.rsqrt(_slot_sum(o_gdn * o_gdn) * (1.0 / GDN_D) + EPS) * g_gdn * _silu(gate)
    return mla, gdn


def _mix_join_fwd(o_mla, o_gdn, gate, g_mla, g_gdn):
    def fn(rows, consts):
        return [jnp.concatenate(_mix_join(*rows, *consts), axis=1)], []

    return _rowwise("mix_join_fwd", fn, [o_mla, o_gdn, gate], [g_mla, g_gdn], [(2 * WIDE, BF16)])


def _mix_join_bwd(o_mla, o_gdn, gate, g_mla, g_gdn, dcat):
    def fn(rows, consts):
        om, og, gt, dc = rows
        gm, gg = consts
        _, pull = jax.vjp(lambda x, g: _rms(x, g, N_HEADS * MLA_V), om, gm)
        dom, dgm = pull(dc[:, :WIDE])
        dy = dc[:, WIDE:]
        r = lax.rsqrt(_slot_sum(og * og) * (1.0 / GDN_D) + EPS)
        sig = _sigmoid(gt)
        normed = og * r
        dn = dy * gg * (gt * sig)
        dog = r * dn - normed * (r * r) * _slot_sum(dn * og) * (1.0 / GDN_D)
        dgt = dy * normed * gg * (sig * (1.0 + gt * (1.0 - sig)))
        dgg = jnp.sum(dy * normed * (gt * sig), axis=0, keepdims=True)
        return [dom, _slot_sum(dom * om), dog, dgt], [dgm, dgg]

    return _rowwise("mix_join_bwd", fn, [o_mla, o_gdn, gate, dcat], [g_mla, g_gdn],
                    [(WIDE, F32), (WIDE, F32), (WIDE, F32), (WIDE, BF16)], sums=[WIDE, WIDE])


def _norm_residual_fwd(name, x, h, g, out_dtypes):
    dm = x.shape[1]

    def fn(rows, consts):
        y = rows[0] + _rms(rows[1], consts[0], dm)
        return [y] + [_rms(y, gg, dm) for gg in consts[1:]], []

    return _rowwise(name, fn, [x, h], list(g), [(dm, dt) for dt in out_dtypes])


def _norm_residual_bwd(name, h, g, dy):
    dm = h.shape[1]

    def fn(rows, consts):
        _, pull = jax.vjp(lambda hv, gv: _rms(hv, gv, dm), rows[0], consts[0])
        dh, dg = pull(rows[1])
        return [dh], [dg]

    return _rowwise(name, fn, [h, dy], [g], [(dm, BF16)], sums=[dm])


def _norm_bwd_add(name, x, g, dns, dy):
    dm = x.shape[1]

    def fn(rows, consts):
        xv, dyv, *parts = rows
        dn = parts[0]
        for p in parts[1:]:
            dn = dn + p
        _, pull = jax.vjp(lambda a, gv: _rms(a, gv, dm), xv, consts[0])
        dx, dg = pull(dn)
        return [dyv + dx], [dg]

    return _rowwise(name, fn, [x, dy, *dns], [g], [(dm, F32)], sums=[dm])


def _loss_fwd(y, target):
    dm = y.shape[1]

    def fn(rows, consts):
        err = rows[0] - rows[1]
        sq = err * err
        lanes = sq[:, :SLOT]
        for j in range(1, dm // SLOT):
            lanes = lanes + sq[:, j * SLOT:(j + 1) * SLOT]
        return [err * (1.0 / dm)], [jnp.sum(lanes, axis=0, keepdims=True) * (0.5 / dm)]

    return _rowwise("loss", fn, [y, target], [], [(dm, F32)], sums=[SLOT])


def _norm_fwd(name, x, g):
    dm = x.shape[1]
    return _rowwise(name, lambda rows, consts: ([_rms(rows[0], consts[0], dm)], []), [x], [g], [(dm, BF16)])[0]


W_IN_CUTS = (0, 256, 384, 416, 1952, 1960, 1968, 2480)


def _heads_out(w, per_head, axis=-1):
    axis = axis % w.ndim
    shape = w.shape
    n = shape[axis] // per_head
    w = w.reshape(shape[:axis] + (n, per_head) + shape[axis + 1:])
    pad = [(0, 0)] * w.ndim
    pad[axis + 1] = (0, SLOT - per_head)
    return jnp.pad(w, pad).reshape(shape[:axis] + (n * SLOT,) + shape[axis + 1:])


def _heads_in(w, per_head, axis=-1):
    axis = axis % w.ndim
    shape = w.shape
    n = shape[axis] // SLOT
    w = w.reshape(shape[:axis] + (n, SLOT) + shape[axis + 1:])
    w = lax.slice_in_dim(w, 0, per_head, axis=axis + 1)
    return w.reshape(shape[:axis] + (n * per_head,) + shape[axis + 1:])


def _pad_lanes(v, lo, width=SLOT):
    return jnp.pad(v, [(0, 0)] * (v.ndim - 1) + [(lo, width - lo - v.shape[-1])])


def _pad_rows(v, lo, rows=SLOT):
    return jnp.pad(v, [(lo, rows - lo - v.shape[0])] + [(0, 0)] * (v.ndim - 1))


def _layout_weights(w):
    c = W_IN_CUTS
    w_in = w["w_in_t"]
    p = {}
    p["w_a"] = jnp.concatenate([w_in[c[0]:c[2]], _pad_rows(w_in[c[2]:c[3]], MLA_NOPE), _pad_rows(w_in[c[4]:c[6]], 0)], axis=0)
    p["w_qkv"] = _heads_out(w_in[c[3]:c[4]], GDN_D, axis=0)
    p["w_gate"] = _heads_out(w_in[c[6]:c[7]], GDN_D, axis=0)
    p["w_uq"] = _heads_out(w["uq_t"], MLA_NOPE + MLA_ROPE, axis=0)
    ukv = w["ukv_t"].reshape(N_HEADS, MLA_NOPE + MLA_V, MLA_KV_RANK)
    p["w_kv"] = jnp.concatenate([_heads_out(ukv[:, :MLA_NOPE].reshape(-1, MLA_KV_RANK), MLA_NOPE, axis=0),
                                 _heads_out(ukv[:, MLA_NOPE:].reshape(-1, MLA_KV_RANK), MLA_V, axis=0)], axis=0)
    p["conv"] = _heads_out(w["gdn_conv_w"], GDN_D)
    p["g_mla_out"] = _heads_out(w["mla_out_g"], MLA_V)
    p["g_gdn"] = jnp.tile(_pad_lanes(w["gdn_norm_g"], 0), (1, N_HEADS))
    p["a_log"] = _pad_lanes(w["gdn_a_log"], 0)
    p["dt_bias"] = _pad_lanes(w["gdn_dt_bias"], 0)
    return p


def _unlayout_grads(d):
    c = W_IN_CUTS
    g = {}
    da = d["w_a"]
    kpe0 = A_KPE + MLA_NOPE
    g["w_in_t"] = jnp.concatenate([da[:A_KPE], da[kpe0:kpe0 + MLA_ROPE], _heads_in(d["w_qkv"], GDN_D, axis=0),
                                   da[A_AB:A_AB + 2 * N_HEADS], _heads_in(d["w_gate"], GDN_D, axis=0)], axis=0)
    assert g["w_in_t"].shape[0] == c[-1]
    g["uq_t"] = _heads_in(d["w_uq"], MLA_NOPE + MLA_ROPE, axis=0)
    dk = _heads_in(d["w_kv"][:WIDE], MLA_NOPE, axis=0).reshape(N_HEADS, MLA_NOPE, MLA_KV_RANK)
    dv = _heads_in(d["w_kv"][WIDE:], MLA_V, axis=0).reshape(N_HEADS, MLA_V, MLA_KV_RANK)
    g["ukv_t"] = jnp.concatenate([dk, dv], axis=1).reshape(-1, MLA_KV_RANK)
    g["w_out"] = _heads_in(d["w_out"], GDN_D, axis=0)
    g["gdn_conv_w"] = _heads_in(d["conv"], GDN_D)
    g["mla_out_g"] = _heads_in(d["g_mla_out"], MLA_V)
    g["gdn_norm_g"] = jnp.sum(d["g_gdn"].reshape(N_HEADS, SLOT), axis=0, keepdims=True)[:, :GDN_D]
    g["gdn_a_log"] = d["a_log"][:, :N_HEADS]
    g["gdn_dt_bias"] = d["dt_bias"][:, :N_HEADS]
    return g


def _weight_grad(name, cots, acts, out_dtype=F32, tm=1024, tn=1024, tk=512):
    return _matmul(name, cots, acts, "tn", out_dtype=out_dtype, tm=tm, tn=tn, tk=tk)


def _by_device(a):
    return a.astype(BF16).reshape((N_DEV, a.shape[0] // N_DEV) + a.shape[1:])


def _local_step(x, positions, target, w, late):
    p = _layout_weights(w)
    tabs = _rope_tables(positions)

    (h1, x1), (ffn2, w_out) = _ffn_fwd("ffn1_fwd", x, w["ffn1_pre_g"], w["ffn1"], 0, w["ffn1_post_g"], carry=late)
    p["w_out"] = _heads_out(w_out.reshape((-1,) + w_out.shape[2:]), GDN_D, axis=0)
    hn = _norm_fwd("mix_pre_norm", x1, w["mix_pre_g"])
    proj_a = _matmul("proj_a", hn, p["w_a"], "nt")
    proj_qkv = _matmul("proj_qkv", hn, p["w_qkv"], "nt")
    proj_gate = _matmul("proj_gate", hn, p["w_gate"], "nt")
    cqn, ckvn, kpe = _mla_pre_fwd(proj_a, tabs, w["mla_q_norm_g"], w["mla_kv_norm_g"])
    q_p = _matmul("mla_q", cqn, p["w_uq"], "nt")
    kv_p = _matmul("mla_kv", ckvn, p["w_kv"], "nt")
    q, k, v = _mla_qkv_fwd(q_p, kv_p, kpe, tabs)
    o_mla, lse = _attn_fwd(q, k, v)
    ab = (proj_a, SLOT, A_AB // SLOT)
    qkv_n = _gdn_conv_fwd(proj_qkv, p["conv"])
    gb, bb = _gates_fwd(ab, p["a_log"], p["dt_bias"])
    o_gdn, keep = _gdn_fwd(qkv_n, gb, bb)
    cat = _mix_join_fwd(o_mla, o_gdn, proj_gate, p["g_mla_out"], p["g_gdn"])[0]
    mixed = _matmul("mix_out", cat, p["w_out"], "nn")
    x2 = _norm_residual_fwd("mix_post", x1, mixed, [w["mix_post_g"]], [F32])[0]
    (h2, y), _ = _ffn_fwd("ffn2_fwd", x2, w["ffn2_pre_g"], ffn2, 0, w["ffn2_post_g"])
    dy, loss_lanes = _loss_fwd(y, target)

    g = {}
    (dx2, xn2, dh2, a2, dhg2, dhu2, g["ffn2_pre_g"], g["ffn2_post_g"]), _ = _ffn_bwd(
        "ffn2_bwd", x2, h2, dy, w["ffn2_pre_g"], ffn2, 0, w["ffn2_post_g"])
    ffn2_grads = _Scatter([_by_device(_weight_grad("ffn2_dw_gate", dhg2, xn2, BF16, tm=1408)),
                           _by_device(_weight_grad("ffn2_dw_up", dhu2, xn2, BF16, tm=1408)),
                           _by_device(_weight_grad("ffn2_dw_down", a2, dh2, BF16, tm=1408))])
    dmixed, g["mix_post_g"] = _norm_residual_bwd("mix_post_bwd", mixed, w["mix_post_g"], dx2)
    dcat = _matmul("mix_out_dx", dmixed, p["w_out"], "nt")
    d = {}
    d["w_out"] = _weight_grad("mix_out_dw", cat, dmixed)
    do_mla, delta, do_gdn, dgate, d["g_mla_out"], d["g_gdn"] = _mix_join_bwd(o_mla, o_gdn, proj_gate, p["g_mla_out"], p["g_gdn"], dcat)
    dq = _attn_bwd_q(q, k, v, do_mla, lse, delta)
    dk, dv = _attn_bwd_kv(q, k, v, do_mla, lse, delta)
    dq_p, dkv_p, dkpe = _mla_qkv_bwd(dq, dk, dv, tabs)
    dcqn = _matmul("mla_q_dx", dq_p, p["w_uq"], "nn")
    d["w_uq"] = _weight_grad("mla_q_dw", dq_p, cqn)
    dckvn = _matmul("mla_kv_dx", dkv_p, p["w_kv"], "nn")
    d["w_kv"] = _weight_grad("mla_kv_dw", dkv_p, ckvn)
    (dqkv_n, dgb, dbb), landed_ffn2 = _gdn_bwd(qkv_n, gb, bb, keep, do_gdn, carry=ffn2_grads)
    dab, d["a_log"], d["dt_bias"] = _gates_bwd(ab, p["a_log"], p["dt_bias"], dgb, dbb)
    dproj_qkv, d["conv"] = _gdn_conv_bwd(proj_qkv, p["conv"], dqkv_n)
    dproj_a, g["mla_q_norm_g"], g["mla_kv_norm_g"] = _mla_pre_bwd(
        proj_a, tabs, w["mla_q_norm_g"], w["mla_kv_norm_g"], dcqn, dckvn, dkpe, dab)
    dhn = [_matmul("proj_a_dx", dproj_a, p["w_a"], "nn"), _matmul("proj_qkv_dx", dproj_qkv, p["w_qkv"], "nn"),
           _matmul("proj_gate_dx", dgate, p["w_gate"], "nn")]
    d["w_a"] = _weight_grad("proj_a_dw", dproj_a, hn, tm=640)
    d["w_qkv"] = _weight_grad("proj_qkv_dw", dproj_qkv, hn)
    d["w_gate"] = _weight_grad("proj_gate_dw", dgate, hn)
    dx1, g["mix_pre_g"] = _norm_bwd_add("mix_pre_bwd", x1, w["mix_pre_g"], dhn, dx2)
    g.update(_unlayout_grads(d))
    others = [t for t, _ in OTHER.values()]
    (dx, xn1, dh1, a1, dhg1, dhu1, g["ffn1_pre_g"], g["ffn1_post_g"]), landed_others = _ffn_bwd(
        "ffn1_bwd", x, h1, dx1, w["ffn1_pre_g"], w["ffn1"], 0, w["ffn1_post_g"], carry=_Scatter([_by_device(g.pop(t)) for t in others]))
    landed_ffn1 = _exchange("scatter_ffn1", _Scatter([_by_device(_weight_grad("ffn1_dw_gate", dhg1, xn1, BF16, tm=1408)),
                                                      _by_device(_weight_grad("ffn1_dw_up", dhu1, xn1, BF16, tm=1408)),
                                                      _by_device(_weight_grad("ffn1_dw_down", a1, dh1, BF16, tm=1408))]))
    landed = dict(zip(list(FFN_NAMES) + list(OTHER), list(landed_ffn1) + list(landed_ffn2) + list(landed_others)))
    return loss_lanes, dx, g, landed


MESH_AXES = ("x", "y", "c")
N_LINKS = N_DEV - 1


def _place():
    return tuple(lax.axis_index(a) for a in MESH_AXES)


def _block_of(dev):
    x, y, c = dev
    return 4 * x + 2 * y + c


def _remote_copy(src, dst, sems, k, to):
    send_sems, recv_sems = sems
    return pltpu.make_async_remote_copy(src_ref=src, dst_ref=dst, send_sem=send_sems.at[k], recv_sem=recv_sems.at[k],
                                        device_id=to, device_id_type=pl.DeviceIdType.MESH)


class _Exchange:
    def __init__(self, arrays):
        self.arrays = list(arrays)
        self.n = len(self.arrays)
        self.specs = [pl.BlockSpec(memory_space=pl.ANY)] * self.n
        self.scratch = [pltpu.SemaphoreType.DMA((self.n * N_LINKS,)), pltpu.SemaphoreType.DMA((self.n * N_LINKS,)),
                        pltpu.SemaphoreType.DMA((self.n,))]

    def split(self, refs):
        n = self.n
        return refs[:n], refs[n:2 * n], (refs[2 * n], refs[2 * n + 1]), refs[2 * n + 2]


class _Gather(_Exchange):
    def out_shape(self):
        return [jax.ShapeDtypeStruct((N_DEV,) + a.shape, a.dtype) for a in self.arrays]

    def _plan(self, ins, outs, sems, local_sems):
        x, y, c = _place()
        me, sibling = (x, y, c), (x, y, 1 - c)
        chips = [(1 - x, y), (x, 1 - y), (1 - x, 1 - y)]

        def copy(a, k, block, to, mine=False):
            src = ins[a] if mine else outs[a].at[_block_of(block)]
            return _remote_copy(src, outs[a].at[_block_of(block)], sems, a * N_LINKS + k, to)

        local = [pltpu.make_async_copy(ins[a], outs[a].at[_block_of(me)], local_sems.at[a]) for a in range(self.n)]
        first = []
        for a in range(self.n):
            first.append(copy(a, 0, me, sibling, mine=True))
            first += [copy(a, 1 + j, me, (*chip, c), mine=True) for j, chip in enumerate(chips)]
        return me, sibling, chips, c, copy, local, first

    def start(self, ins, outs, sems, local_sems):
        *_, local, first = self._plan(ins, outs, sems, local_sems)
        for cp in local + first:
            cp.start()

    def finish(self, ins, outs, sems, local_sems):
        me, sibling, chips, c, copy, local, first = self._plan(ins, outs, sems, local_sems)
        passed = []
        for j, chip in enumerate(chips):
            for a in range(self.n):
                copy(a, 1 + j, (*chip, c), me).wait_recv()
                passed.append(copy(a, 4 + j, (*chip, c), sibling))
                passed[-1].start()
        for a in range(self.n):
            copy(a, 0, sibling, me).wait_recv()
            for j, chip in enumerate(chips):
                copy(a, 4 + j, (*chip, 1 - c), me).wait_recv()
        for cp in first + passed:
            cp.wait_send()
        for cp in local:
            cp.wait()


class _Scatter(_Exchange):
    def out_shape(self):
        return [jax.ShapeDtypeStruct(a.shape, a.dtype) for a in self.arrays]

    def _plan(self, ins, outs, sems, local_sems):
        x, y, c = _place()
        me = _block_of((x, y, c))

        def peer(r):
            return (1 - x if r & 4 else x, 1 - y if r & 2 else y, 1 - c if r & 1 else c)

        local = [pltpu.make_async_copy(ins[a].at[me], outs[a].at[me], local_sems.at[a]) for a in range(self.n)]
        sends = [_remote_copy(ins[a].at[_block_of(peer(r))], outs[a].at[me], sems, a * N_LINKS + r - 1, peer(r))
                 for a in range(self.n) for r in range(1, N_DEV)]
        arrivals = [_remote_copy(ins[a].at[me], outs[a].at[_block_of(peer(r))], sems, a * N_LINKS + r - 1, peer(r))
                    for a in range(self.n) for r in range(1, N_DEV)]
        return local, sends, arrivals

    def start(self, ins, outs, sems, local_sems):
        local, sends, _ = self._plan(ins, outs, sems, local_sems)
        for cp in local + sends:
            cp.start()

    def finish(self, ins, outs, sems, local_sems):
        local, sends, arrivals = self._plan(ins, outs, sems, local_sems)
        for cp in arrivals:
            cp.wait_recv()
        for cp in sends:
            cp.wait_send()
        for cp in local:
            cp.wait()


def _exchange(name, plan):
    def body(*refs):
        parts = plan.split(refs)
        plan.start(*parts)
        plan.finish(*parts)

    return pl.pallas_call(
        body, name=name,
        in_specs=plan.specs,
        out_specs=plan.specs,
        out_shape=plan.out_shape(),
        scratch_shapes=plan.scratch,
    )(*plan.arrays)


def _call_carrying(body, plan, operands, *, name, grid, in_specs, out_specs, out_shape, scratch_shapes, compiler_params):
    if plan is None:
        outs = pl.pallas_call(body, name=name, grid=grid, in_specs=in_specs, out_specs=out_specs, out_shape=out_shape,
                              scratch_shapes=scratch_shapes, compiler_params=compiler_params)(*operands)
        return outs, []
    n_i, n_o, n_s, k = len(in_specs), len(out_specs), len(scratch_shapes), plan.n

    def whole(*refs):
        cut = [n_i, n_i + k, n_i + k + n_o, n_i + 2 * k + n_o, n_i + 2 * k + n_o + n_s]
        own_in, ex_in, own_out, ex_out, own_scr, ex_scr = (refs[a:b] for a, b in zip([0] + cut, cut + [len(refs)]))
        parts = plan.split(ex_in + ex_out + ex_scr)
        first = last = True
        for axis, size in enumerate(grid):
            first = first & (pl.program_id(axis) == 0)
            last = last & (pl.program_id(axis) == size - 1)

        @pl.when(first)
        def _():
            plan.start(*parts)

        body(*own_in, *own_out, *own_scr)

        @pl.when(last)
        def _():
            plan.finish(*parts)

    outs = pl.pallas_call(
        whole, name=name, grid=grid,
        in_specs=list(in_specs) + plan.specs, out_specs=list(out_specs) + plan.specs,
        out_shape=list(out_shape) + plan.out_shape(), scratch_shapes=list(scratch_shapes) + plan.scratch,
        compiler_params=compiler_params,
    )(*operands, *plan.arrays)
    return outs[:n_o], outs[n_o:]


def _row_tile(rows, target=256):
    best = rows
    for cand in range(16, min(rows, target) + 1, 16):
        if rows % cand == 0:
            best = cand
    return best


def _sum_blocks(name, blocks):
    rows, width = blocks.shape[-2:]
    tm = _row_tile(rows)

    def body(x_ref, o_ref):
        acc = x_ref[0].astype(F32)
        for d in range(1, N_DEV):
            acc = acc + x_ref[d].astype(F32)
        o_ref[...] = acc

    return pl.pallas_call(
        body, name=name,
        grid=(rows // tm,),
        in_specs=[pl.BlockSpec((N_DEV, tm, width), lambda i: (0, i, 0))],
        out_specs=pl.BlockSpec((tm, width), lambda i: (i, 0)),
        out_shape=jax.ShapeDtypeStruct((rows, width), F32),
        compiler_params=pltpu.CompilerParams(dimension_semantics=("parallel",)),
    )(blocks)


def _all_reduce_small(name, vec):
    rows, width = vec.shape

    def body(x_ref, o_ref, all_ref, send_sems, recv_sems):
        x, y, c = _place()
        me = _block_of((x, y, c))
        all_ref[me] = x_ref[...]

        def peer(r):
            return (1 - x if r & 4 else x, 1 - y if r & 2 else y, 1 - c if r & 1 else c)

        def copy(r, block):
            return _remote_copy(x_ref, all_ref.at[block], (send_sems, recv_sems), r - 1, peer(r))

        sends = [copy(r, me) for r in range(1, N_DEV)]
        for cp in sends:
            cp.start()
        for r in range(1, N_DEV):
            copy(r, _block_of(peer(r))).wait_recv()
        for cp in sends:
            cp.wait_send()
        acc = all_ref[0]
        for d in range(1, N_DEV):
            acc = acc + all_ref[d]
        o_ref[...] = acc

    return pl.pallas_call(
        body, name=name,
        in_specs=[pl.BlockSpec(memory_space=pltpu.VMEM)],
        out_specs=pl.BlockSpec(memory_space=pltpu.VMEM),
        out_shape=jax.ShapeDtypeStruct((rows, width), F32),
        scratch_shapes=[pltpu.VMEM((N_DEV, rows, width), F32), pltpu.SemaphoreType.DMA((N_LINKS,)), pltpu.SemaphoreType.DMA((N_LINKS,))],
    )(vec)


def _adamw(name, w, g, m, v):
    def fn(rows, consts):
        wv, gv, mv, vv = rows
        m2 = ADAM_B1 * mv + (1.0 - ADAM_B1) * gv
        v2 = ADAM_B2 * vv + (1.0 - ADAM_B2) * jnp.square(gv)
        m_hat = m2 / (1.0 - ADAM_B1 ** ADAM_STEP)
        v_hat = v2 / (1.0 - ADAM_B2 ** ADAM_STEP)
        return [-ADAM_LR * (m_hat / (jnp.sqrt(v_hat) + ADAM_EPS) + ADAM_WD * wv), m2, v2], []

    return _rowwise(name, fn, [w, g, m, v], [], [(w.shape[1], F32)] * 3, tm=_row_tile(w.shape[0]))


ROW = 1024
FFN_NAMES = ("ffn1_w_gate", "ffn1_w_up", "ffn1_w_down", "ffn2_w_gate", "ffn2_w_up", "ffn2_w_down")
OTHER = {"w_in": ("w_in_t", True), "mla_w_uq": ("uq_t", True), "mla_w_ukv": ("ukv_t", True), "w_out": ("w_out", False)}
BY_COLUMNS = ("ffn1_w_gate", "ffn1_w_up", "ffn2_w_gate", "ffn2_w_up", "w_in", "mla_w_uq", "mla_w_ukv")
SMALL = {
    "ffn1_pre_g": (1024, 1024), "ffn1_post_g": (1024, 1024), "mix_pre_g": (1024, 1024), "mla_q_norm_g": (256, 256),
    "mla_kv_norm_g": (128, 128), "mla_out_g": (512, 512), "gdn_a_log": (8, 128), "gdn_dt_bias": (8, 128),
    "gdn_norm_g": (64, 128), "mix_post_g": (1024, 1024), "ffn2_pre_g": (1024, 1024), "ffn2_post_g": (1024, 1024),
}
CONV_SHAPE = (GDN_CONV, 3 * N_HEADS * GDN_D)
CONV_SHARD = (GDN_CONV, CONV_SHAPE[1] // N_DEV)
CONV_LANES = CONV_SHAPE[0] * CONV_SHAPE[1]
SMALL_ROWS = 8
REDUCE_ROWS = 16


def _pack_small(vecs, conv, rows):
    parts = [_pad_lanes(vecs[n].reshape(1, -1), 0, r) for n, (_, r) in SMALL.items()]
    parts.append(conv.reshape(1, -1))
    flat = jnp.concatenate(parts, axis=1)
    return _pad_lanes(flat, 0, rows * ROW).reshape(rows, ROW)


def _unpack_small(buf):
    flat = buf.reshape(1, -1)
    out, at = {}, 0
    for n, (w, r) in SMALL.items():
        out[n] = flat[:, at:at + w]
        at += r
    return out, flat[0, at:]


def kernel(x, positions, ffn1_pre_g, ffn1_w_gate, ffn1_w_up, ffn1_w_down, ffn1_post_g, mix_pre_g, w_in, mla_q_norm_g, mla_w_uq, mla_kv_norm_g, mla_w_ukv, mla_out_g, gdn_conv_w, gdn_a_log, gdn_dt_bias, gdn_norm_g, w_out, mix_post_g, ffn2_pre_g, ffn2_w_gate, ffn2_w_up, ffn2_w_down, ffn2_post_g, loss_target, m_ffn1_pre_g, m_ffn1_w_gate, m_ffn1_w_up, m_ffn1_w_down, m_ffn1_post_g, m_mix_pre_g, m_w_in, m_mla_q_norm_g, m_mla_w_uq, m_mla_kv_norm_g, m_mla_w_ukv, m_mla_out_g, m_gdn_conv_w, m_gdn_a_log, m_gdn_dt_bias, m_gdn_norm_g, m_w_out, m_mix_post_g, m_ffn2_pre_g, m_ffn2_w_gate, m_ffn2_w_up, m_ffn2_w_down, m_ffn2_post_g, v_ffn1_pre_g, v_ffn1_w_gate, v_ffn1_w_up, v_ffn1_w_down, v_ffn1_post_g, v_mix_pre_g, v_w_in, v_mla_q_norm_g, v_mla_w_uq, v_mla_kv_norm_g, v_mla_w_ukv, v_mla_out_g, v_gdn_conv_w, v_gdn_a_log, v_gdn_dt_bias, v_gdn_norm_g, v_w_out, v_mix_post_g, v_ffn2_pre_g, v_ffn2_w_gate, v_ffn2_w_up, v_ffn2_w_down, v_ffn2_post_g):
    given = dict(locals())
    order = ["ffn1_pre_g", "ffn1_w_gate", "ffn1_w_up", "ffn1_w_down", "ffn1_post_g", "mix_pre_g", "w_in", "mla_q_norm_g",
             "mla_w_uq", "mla_kv_norm_g", "mla_w_ukv", "mla_out_g", "gdn_conv_w", "gdn_a_log", "gdn_dt_bias", "gdn_norm_g",
             "w_out", "mix_post_g", "ffn2_pre_g", "ffn2_w_gate", "ffn2_w_up", "ffn2_w_down", "ffn2_post_g"]
    assert sorted(order) == sorted(list(FFN_NAMES) + list(OTHER) + list(SMALL) + ["gdn_conv_w"])

    def drop_depth(a):
        return a[0] if a.ndim == 3 else a

    wts = {n: drop_depth(given[n]) for n in order}
    mom = {n: drop_depth(given["m_" + n]) for n in order}
    var = {n: drop_depth(given["v_" + n]) for n in order}
    me = _block_of(_place())

    def wire(n):
        return (wts[n].T if n in BY_COLUMNS else wts[n]).astype(BF16)

    first = ["w_in", "mla_w_uq", "mla_w_ukv"]
    gathered = _exchange("gather_first", _Gather([jnp.stack([wire(n) for n in FFN_NAMES[:3]])] + [wire(n) for n in first]))
    late = _Gather([jnp.stack([wire(n) for n in FFN_NAMES[3:]]), wire("w_out")])
    conv_at = lax.dynamic_update_slice(jnp.zeros((N_DEV, CONV_SHARD[0] * CONV_SHARD[1]), F32),
                                       wts["gdn_conv_w"].reshape(1, -1), (me, 0))
    conv_all = _all_reduce_small("gather_conv", _pad_lanes(conv_at.reshape(1, -1), 0, SMALL_ROWS * ROW).reshape(SMALL_ROWS, ROW))
    full = {n: wts[n] for n in SMALL}
    full["ffn1"] = gathered[0]
    for n, blocks in zip(first, gathered[1:]):
        full[OTHER[n][0]] = blocks.reshape((-1,) + blocks.shape[2:])
    full["gdn_conv_w"] = conv_all.reshape(-1)[:CONV_LANES].reshape((N_DEV,) + CONV_SHARD).transpose(1, 0, 2).reshape(CONV_SHAPE)

    loss_lanes, dx, grads, landed = _local_step(x[0], positions[0], loss_target[0], full, late)
    loss = lax.psum(jnp.sum(loss_lanes), MESH_AXES)

    sums = {n: _sum_blocks("sum_" + n, blocks) for n, blocks in landed.items()}
    grad = {n: (sums[n].T if n in BY_COLUMNS else sums[n]) for n in sums}
    small_sum = _all_reduce_small("reduce_small", _pack_small(grads, grads["gdn_conv_w"].reshape(-1), REDUCE_ROWS))
    small_grad, conv_grad_full = _unpack_small(small_sum)
    grad.update(small_grad)
    grad["gdn_conv_w"] = lax.dynamic_slice(conv_grad_full[:CONV_LANES].reshape(CONV_SHAPE), (0, me * CONV_SHARD[1]), CONV_SHARD)

    outs = {"grad": grad, "delta": {}, "new_m": {}, "new_v": {}}
    for n in list(FFN_NAMES) + list(OTHER):
        outs["delta"][n], outs["new_m"][n], outs["new_v"][n] = _adamw("adamw_" + n, wts[n], grad[n], mom[n], var[n])
    small = [_pack_small(s, s["gdn_conv_w"].reshape(-1), SMALL_ROWS) for s in (wts, grad, mom, var)]
    for kind, s in zip(("delta", "new_m", "new_v"), _adamw("adamw_small", *small)):
        vecs, conv = _unpack_small(s)
        outs[kind].update(vecs)
        outs[kind]["gdn_conv_w"] = conv[:CONV_SHARD[0] * CONV_SHARD[1]].reshape(CONV_SHARD)
    result = [loss, dx[None]]
    for kind in ("grad", "delta", "new_m", "new_v"):
        result += [outs[kind][n].reshape(given[n].shape) for n in order]
    return tuple(result)
```

```python
import jax
import jax.numpy as jnp
from jax import lax
from jax.experimental import pallas as pl
from jax.experimental.pallas import tpu as pltpu

F32 = jnp.float32
BF16 = jnp.bfloat16
HI = lax.Precision.HIGH
EXACT = lax.Precision.HIGHEST

N_DEV = 8
D_MODEL = 1024
D_FF = 2816
N_HEADS = 8
SLOT = 128
MLA_Q_RANK = 256
MLA_KV_RANK = 128
MLA_NOPE = 64
MLA_ROPE = 32
MLA_V = 64
GDN_D = 64
GDN_CONV = 4
GDN_CHUNK = 64
ROPE_THETA = 10000.0
EPS = 1e-6
ADAM_LR, ADAM_B1, ADAM_B2, ADAM_EPS, ADAM_WD, ADAM_STEP = 0.001, 0.9, 0.999, 1e-08, 0.01, 10


def _dot(a, b, ca, cb, precision=None):
    lead = a.ndim - 2
    batch = tuple(range(lead))
    return lax.dot_general(a, b, (((lead + ca,), (lead + cb,)), (batch, batch)), precision=precision,
                           preferred_element_type=F32)


def _nn(a, b, precision=None):
    return _dot(a, b, 1, 0, precision)


def _nt(a, b, precision=None):
    return _dot(a, b, 1, 1, precision)


def _tn(a, b, precision=None):
    return _dot(a, b, 0, 0, precision)


def _sigmoid(x):
    return 1.0 / (1.0 + jnp.exp(-x))


def _silu(x):
    return x * _sigmoid(x)


def _rms(x, g, n):
    ms = jnp.sum(x * x, axis=-1, keepdims=True) * (1.0 / n)
    return x * lax.rsqrt(ms + EPS) * g


def _chunk_masks():
    c = GDN_CHUNK
    i = lax.broadcasted_iota(jnp.int32, (c, c), 0)
    j = lax.broadcasted_iota(jnp.int32, (c, c), 1)
    lower = i >= j
    strict = i > j
    eye = (i == j).astype(F32)
    blocks = []
    b = 1
    while b < c:
        same = (i // (2 * b)) == (j // (2 * b))
        blocks.append(same & ((i % (2 * b)) >= b) & ((j % (2 * b)) < b))
        b *= 2
    return lower, strict, eye, blocks


def _unit_lower_inverse(low, eye, blocks):
    t = jnp.broadcast_to(eye, low.shape)
    for m in blocks:
        lo = jnp.where(m, low, 0.0)
        t = t - _nn(t, _nn(lo, t, HI), HI)
    return t


@jax.custom_vjp
def _known_inverse(low, tinv):
    return tinv


def _known_inverse_fwd(low, tinv):
    return tinv, tinv


def _known_inverse_bwd(tinv, dt):
    return -_tn(tinv, _nt(dt, tinv, HI), HI), jnp.zeros_like(tinv)


_known_inverse.defvjp(_known_inverse_fwd, _known_inverse_bwd)


def _gdn_chunk(q, k, v, gb, bb, s, masks, tinv=None):
    lower, strict, eye, blocks = masks
    qs = q * (GDN_D ** -0.5)
    gc = _nn(jnp.broadcast_to(lower.astype(F32), gb.shape), gb, EXACT)
    gct = _nt(jnp.broadcast_to(eye, gb.shape), gc, EXACT)
    decay = jnp.exp(jnp.where(lower, gc - gct, -1e30))
    kb = k * bb
    low = jnp.where(strict, _nt(kb, k, HI) * decay, 0.0)
    tinv = _unit_lower_inverse(low, eye, blocks) if tinv is None else _known_inverse(low, tinv)
    eg = jnp.exp(gc)
    w = _nn(tinv, kb * eg, HI)
    u = _nn(tinv, v * bb, HI)
    attn = _nt(qs, k, HI) * decay
    g_end = jnp.sum(gb, axis=-2, keepdims=True)
    k_dec = k * jnp.exp(g_end - gc)
    v_new = u - _nn(w, s, HI)
    o = _nn(qs * eg, s, HI) + _nn(attn, v_new, HI)
    s_new = s * jnp.exp(g_end) + _tn(k_dec, v_new, HI)
    return o, s_new, tinv


GDN_GROUP = 8
GDN_GROUPS = N_HEADS // GDN_GROUP


def _group_heads(ref):
    return jnp.stack([ref[:, pl.ds(j * SLOT, GDN_D)] for j in range(GDN_GROUP)])


def _ungroup_heads(ref, val):
    pad = jnp.zeros((GDN_CHUNK, SLOT - GDN_D), F32)
    for j in range(GDN_GROUP):
        ref[:, pl.ds(j * SLOT, GDN_D)] = val[j]
        ref[:, pl.ds(j * SLOT + GDN_D, SLOT - GDN_D)] = pad


def _gdn_fwd(qkv, gb, bb, carry=None):
    t = qkv.shape[0]
    n_chunks = t // GDN_CHUNK
    d = GDN_D

    def body(q_ref, k_ref, v_ref, g_ref, b_ref, o_ref, keep_ref, s_ref):
        @pl.when(pl.program_id(1) == 0)
        def _():
            s_ref[...] = jnp.zeros_like(s_ref)

        s = s_ref[...]
        keep_ref[:, 0, 0] = s
        o, s_new, tinv = _gdn_chunk(*[_group_heads(r) for r in (q_ref, k_ref, v_ref, g_ref, b_ref)], s, _chunk_masks())
        keep_ref[:, 0, 1] = tinv
        s_ref[...] = s_new
        _ungroup_heads(o_ref, o)

    def spec(kind=0):
        return pl.BlockSpec((GDN_CHUNK, GDN_GROUP * SLOT), lambda h, n: (n, kind * GDN_GROUPS + h))

    return _call_carrying(
        body, carry, (qkv, qkv, qkv, gb, bb), name="gdn_fwd",
        grid=(GDN_GROUPS, n_chunks),
        in_specs=[spec(0), spec(1), spec(2), spec(), spec()],
        out_specs=[spec(), pl.BlockSpec((GDN_GROUP, 1, 2, d, d), lambda h, n: (h, n, 0, 0, 0))],
        out_shape=[jax.ShapeDtypeStruct((t, N_HEADS * SLOT), F32), jax.ShapeDtypeStruct((N_HEADS, n_chunks, 2, d, d), F32)],
        scratch_shapes=[pltpu.VMEM((GDN_GROUP, d, d), F32)],
        compiler_params=pltpu.CompilerParams(dimension_semantics=("arbitrary", "arbitrary")),
    )


def _gdn_bwd(qkv, gb, bb, keep, do, carry=None):
    t = qkv.shape[0]
    n_chunks = t // GDN_CHUNK
    d = GDN_D

    def body(q_ref, k_ref, v_ref, g_ref, b_ref, keep_ref, do_ref, dqkv_ref, dg_ref, db_ref, ds_ref):
        @pl.when(pl.program_id(1) == 0)
        def _():
            ds_ref[...] = jnp.zeros_like(ds_ref)

        masks = _chunk_masks()
        tinv = keep_ref[:, 0, 1]
        _, pull = jax.vjp(lambda *a: _gdn_chunk(*a, masks, tinv)[:2],
                          *[_group_heads(r) for r in (q_ref, k_ref, v_ref, g_ref, b_ref)], keep_ref[:, 0, 0])
        dq, dk, dv, dg, db, ds = pull((_group_heads(do_ref), ds_ref[...]))
        ds_ref[...] = ds
        for i, val in enumerate((dq, dk, dv)):
            _ungroup_heads(dqkv_ref.at[i], val)
        _ungroup_heads(dg_ref, dg)
        _ungroup_heads(db_ref, db)

    def spec(kind=0):
        return pl.BlockSpec((GDN_CHUNK, GDN_GROUP * SLOT), lambda h, n: (n_chunks - 1 - n, kind * GDN_GROUPS + h))

    return _call_carrying(
        body, carry, (qkv, qkv, qkv, gb, bb, keep, do), name="gdn_bwd",
        grid=(GDN_GROUPS, n_chunks),
        in_specs=[spec(0), spec(1), spec(2), spec(), spec(),
                  pl.BlockSpec((GDN_GROUP, 1, 2, d, d), lambda h, n: (h, n_chunks - 1 - n, 0, 0, 0)), spec()],
        out_specs=[pl.BlockSpec((3, GDN_CHUNK, GDN_GROUP * SLOT), lambda h, n: (0, n_chunks - 1 - n, h)), spec(), spec()],
        out_shape=[jax.ShapeDtypeStruct((3, t, N_HEADS * SLOT), F32)] + [jax.ShapeDtypeStruct((t, N_HEADS * SLOT), F32)] * 2,
        scratch_shapes=[pltpu.VMEM((GDN_GROUP, d, d), F32)],
        compiler_params=pltpu.CompilerParams(dimension_semantics=("arbitrary", "arbitrary")),
    )


def _rowwise(name, fn, rows, consts, outs, sums=(), tm=256):
    rows = [x if isinstance(x, tuple) else (x, x.shape[1], 0) for x in rows]
    t = rows[0][0].shape[0]
    tm = min(tm, t)
    steps = t // tm
    n_r, n_c, n_o, n_s = len(rows), len(consts), len(outs), len(sums)

    def window(width, block):
        return pl.BlockSpec((tm, width), lambda i: (i, block))

    def body(*refs):
        r, c = refs[:n_r], refs[n_r:n_r + n_c]
        o, s = refs[n_r + n_c:n_r + n_c + n_o], refs[n_r + n_c + n_o:]
        vals, tot = fn([x[...] for x in r], [x[...] for x in c])
        for ref, val in zip(o, vals):
            ref[...] = val.astype(ref.dtype)
        if n_s:
            @pl.when(pl.program_id(0) == 0)
            def _():
                for ref in s:
                    ref[...] = jnp.zeros_like(ref)

            for ref, val in zip(s, tot):
                ref[...] += val

    return pl.pallas_call(
        body, name=name,
        grid=(steps,),
        in_specs=[window(w, b) for _, w, b in rows] + [pl.BlockSpec(x.shape, lambda i: (0, 0)) for x in consts],
        out_specs=[pl.BlockSpec((tm, w), lambda i: (i, 0)) for w, _ in outs]
        + [pl.BlockSpec((1, w), lambda i: (0, 0)) for w in sums],
        out_shape=[jax.ShapeDtypeStruct((t, w), dt) for w, dt in outs]
        + [jax.ShapeDtypeStruct((1, w), F32) for w in sums],
        compiler_params=pltpu.CompilerParams(dimension_semantics=("arbitrary",)),
    )(*[x for x, _, _ in rows], *consts)


def _tile(dim, target):
    if dim <= target:
        return dim
    best = None
    for cand in range(128, target + 1, 128):
        if dim % cand == 0:
            best = cand
    assert best is not None, (dim, target)
    return best


def _matmul(name, a, b, mode, out_dtype=F32, tm=512, tn=1024, tk=1024, carry=None):
    if mode == "nn":
        (m, k), n = a.shape, b.shape[1]
    elif mode == "nt":
        (m, k), n = a.shape, b.shape[0]
    else:
        (k, m), n = a.shape, b.shape[1]
    tm, tn, tk = _tile(m, tm), _tile(n, tn), _tile(k, tk)
    k_steps = k // tk
    product = {"nn": _nn, "nt": _nt, "tn": _tn}[mode]

    def body(a_ref, b_ref, o_ref, acc_ref):
        part = product(a_ref[...].astype(BF16), b_ref[...].astype(BF16))
        if k_steps == 1:
            o_ref[...] = part.astype(o_ref.dtype)
        else:
            kk = pl.program_id(2)

            @pl.when(kk == 0)
            def _():
                acc_ref[...] = part

            @pl.when(kk > 0)
            def _():
                acc_ref[...] += part

            @pl.when(kk == k_steps - 1)
            def _():
                o_ref[...] = acc_ref[...].astype(o_ref.dtype)

    a_spec = pl.BlockSpec((tk, tm), lambda i, j, kk: (kk, i)) if mode == "tn" else pl.BlockSpec((tm, tk), lambda i, j, kk: (i, kk))
    b_spec = pl.BlockSpec((tn, tk), lambda i, j, kk: (j, kk)) if mode == "nt" else pl.BlockSpec((tk, tn), lambda i, j, kk: (kk, j))
    (out,), carried = _call_carrying(
        body, carry, (a, b), name=name,
        grid=(m // tm, n // tn, k_steps),
        in_specs=[a_spec, b_spec],
        out_specs=[pl.BlockSpec((tm, tn), lambda i, j, kk: (i, j))],
        out_shape=[jax.ShapeDtypeStruct((m, n), out_dtype)],
        scratch_shapes=[pltpu.VMEM((tm, tn) if k_steps > 1 else (8, 128), F32)],
        compiler_params=pltpu.CompilerParams(dimension_semantics=("arbitrary", "arbitrary", "arbitrary")),
    )
    return out if carry is None else (out, carried)


FFN_TM = 512
FFN_BWD_TM = 256
FFN_BLOCKS = 4
FFN_GATE, FFN_UP, FFN_DOWN = 0, 1, 2


def _ffn_weight_specs(ffn_w, first):
    _, _, rows, dm = ffn_w.shape

    def spec(k):
        return pl.BlockSpec((FFN_BLOCKS, None, rows, dm), lambda i, j: (j, first + k, 0, 0))

    return [spec(FFN_GATE), spec(FFN_UP), spec(FFN_DOWN)], FFN_BLOCKS * rows


def _ffn_fwd(name, x, g_pre, ffn_w, first, g_post, carry=None):
    t, dm = x.shape
    tm = min(FFN_TM, t)
    w_specs, tf = _ffn_weight_specs(ffn_w, first)
    f_steps = N_DEV // FFN_BLOCKS

    def body(x_ref, gpre_ref, wg_ref, wu_ref, wd_ref, gpost_ref, h_ref, y_ref, xn_ref, acc_ref):
        j = pl.program_id(1)

        @pl.when(j == 0)
        def _():
            xn_ref[...] = _rms(x_ref[...], gpre_ref[...], dm).astype(BF16)
            acc_ref[...] = jnp.zeros_like(acc_ref)

        xn = xn_ref[...]
        wg, wu, wd = (r[...].reshape(tf, dm) for r in (wg_ref, wu_ref, wd_ref))
        a = _silu(_nt(xn, wg)) * _nt(xn, wu)
        acc_ref[...] += _nn(a.astype(BF16), wd)

        @pl.when(j == f_steps - 1)
        def _():
            h = acc_ref[...]
            h_ref[...] = h
            y_ref[...] = x_ref[...] + 0.5 * _rms(h, gpost_ref[...], dm)

    row = pl.BlockSpec((tm, dm), lambda i, j: (i, 0))
    vec = pl.BlockSpec((1, dm), lambda i, j: (0, 0))
    return _call_carrying(
        body, carry, (x, g_pre, ffn_w, ffn_w, ffn_w, g_post), name=name,
        grid=(t // tm, f_steps),
        in_specs=[row, vec, *w_specs, vec],
        out_specs=[row, row],
        out_shape=[jax.ShapeDtypeStruct((t, dm), F32)] * 2,
        scratch_shapes=[pltpu.VMEM((tm, dm), BF16), pltpu.VMEM((tm, dm), F32)],
        compiler_params=pltpu.CompilerParams(dimension_semantics=("arbitrary", "arbitrary")),
    )


def _ffn_bwd(name, x, h, dy, g_pre, ffn_w, first, g_post, carry=None):
    t, dm = x.shape
    tm = min(FFN_BWD_TM, t)
    w_specs, tf = _ffn_weight_specs(ffn_w, first)
    f_steps = N_DEV // FFN_BLOCKS
    f = f_steps * tf

    def post(hv, g):
        return 0.5 * _rms(hv, g, dm)

    def pre(xv, g):
        return _rms(xv, g, dm)

    def body(x_ref, h_ref, dy_ref, gpre_ref, wg_ref, wu_ref, wd_ref, gpost_ref,
             dx_ref, xn_ref, dh_ref, a_ref, dhg_ref, dhu_ref, dgpre_ref, dgpost_ref, acc_ref):
        i, j = pl.program_id(0), pl.program_id(1)

        @pl.when((i == 0) & (j == 0))
        def _():
            dgpre_ref[...] = jnp.zeros_like(dgpre_ref)
            dgpost_ref[...] = jnp.zeros_like(dgpost_ref)

        @pl.when(j == 0)
        def _():
            xn_ref[...] = pre(x_ref[...], gpre_ref[...]).astype(BF16)
            _, pull = jax.vjp(post, h_ref[...], gpost_ref[...])
            dh, dg = pull(dy_ref[...])
            dh_ref[...] = dh.astype(BF16)
            dgpost_ref[...] += dg
            acc_ref[...] = jnp.zeros_like(acc_ref)

        xn = xn_ref[...]
        wg, wu, wd = (r[...].reshape(tf, dm) for r in (wg_ref, wu_ref, wd_ref))
        hg = _nt(xn, wg)
        hu = _nt(xn, wu)
        da = _nt(dh_ref[...], wd)
        sig = _sigmoid(hg)
        act = hg * sig
        dhu = (da * act).astype(BF16)
        dhg = (da * hu * (sig * (1.0 + hg * (1.0 - sig)))).astype(BF16)
        a_ref[...] = (act * hu).astype(BF16)
        dhg_ref[...] = dhg
        dhu_ref[...] = dhu
        acc_ref[...] += _nn(dhg, wg) + _nn(dhu, wu)

        @pl.when(j == f_steps - 1)
        def _():
            _, pull = jax.vjp(pre, x_ref[...], gpre_ref[...])
            dx, dg = pull(acc_ref[...])
            dx_ref[...] = dy_ref[...] + dx
            dgpre_ref[...] += dg

    row = pl.BlockSpec((tm, dm), lambda i, j: (i, 0))
    vec = pl.BlockSpec((1, dm), lambda i, j: (0, 0))
    wide = pl.BlockSpec((tm, tf), lambda i, j: (i, j))
    return _call_carrying(
        body, carry, (x, h, dy, g_pre, ffn_w, ffn_w, ffn_w, g_post), name=name,
        grid=(t // tm, f_steps),
        in_specs=[row, row, row, vec, *w_specs, vec],
        out_specs=[row, row, row, wide, wide, wide, vec, vec],
        out_shape=[jax.ShapeDtypeStruct((t, dm), F32), jax.ShapeDtypeStruct((t, dm), BF16), jax.ShapeDtypeStruct((t, dm), BF16),
                   jax.ShapeDtypeStruct((t, f), BF16), jax.ShapeDtypeStruct((t, f), BF16), jax.ShapeDtypeStruct((t, f), BF16),
                   jax.ShapeDtypeStruct((1, dm), F32), jax.ShapeDtypeStruct((1, dm), F32)],
        scratch_shapes=[pltpu.VMEM((tm, dm), F32)],
        compiler_params=pltpu.CompilerParams(dimension_semantics=("arbitrary", "arbitrary")),
    )


ATT_T = 512
ATT_GROUP = 2
ATT_SCALE = (MLA_NOPE + MLA_ROPE) ** -0.5


def _stack_slots(ref, group):
    return jnp.stack([ref[:, pl.ds(j * SLOT, SLOT)] for j in range(group)])


def _unstack_slots(ref, val):
    for j in range(val.shape[0]):
        ref[:, pl.ds(j * SLOT, SLOT)] = val[j].astype(ref.dtype)


def _scores(q, k, diagonal):
    s = _nt(q, k) * ATT_SCALE
    if diagonal:
        row = lax.broadcasted_iota(jnp.int32, s.shape[1:], 0)
        col = lax.broadcasted_iota(jnp.int32, s.shape[1:], 1)
        s = jnp.where(col <= row, s, -1e30)
    return s


def _attn_specs(tile, q_major):
    width = ATT_GROUP * SLOT
    if q_major:
        return (pl.BlockSpec((tile, width), lambda h, qi, ki: (qi, h)),
                pl.BlockSpec((tile, width), lambda h, qi, ki: (jnp.minimum(ki, qi), h)))
    return (pl.BlockSpec((tile, width), lambda h, ki, qi: (jnp.maximum(qi, ki), h)),
            pl.BlockSpec((tile, width), lambda h, ki, qi: (ki, h)))


def _attn_fwd(q, k, v):
    t = q.shape[0]
    tile = min(ATT_T, t)
    steps = t // tile
    g = ATT_GROUP

    strip = min(SLOT, tile)

    def body(q_ref, k_ref, v_ref, o_ref, lse_ref, m_ref, l_ref, alpha_ref, acc_ref, s_ref, p_ref):
        qi, ki = pl.program_id(1), pl.program_id(2)

        @pl.when(ki == 0)
        def _():
            m_ref[...] = jnp.full_like(m_ref, -1e30)
            l_ref[...] = jnp.zeros_like(l_ref)
            acc_ref[...] = jnp.zeros_like(acc_ref)

        def step(diagonal):
            s_ref[...] = _nt(_stack_slots(k_ref, g), _stack_slots(q_ref, g))
            for j in range(tile // strip):
                c = pl.ds(j * strip, strip)
                s = s_ref[:, :, c] * ATT_SCALE
                if diagonal:
                    key = lax.broadcasted_iota(jnp.int32, s.shape[1:], 0)
                    query = lax.broadcasted_iota(jnp.int32, s.shape[1:], 1) + j * strip
                    s = jnp.where(key <= query, s, -1e30)
                m_old = m_ref[:, :, c]
                m_new = jnp.maximum(m_old, jnp.max(s, axis=1, keepdims=True))
                p = jnp.exp(s - m_new)
                alpha = jnp.exp(m_old - m_new)
                l_ref[:, :, c] = alpha * l_ref[:, :, c] + jnp.sum(p, axis=1, keepdims=True)
                alpha_ref[:, :, c] = alpha
                m_ref[:, :, c] = m_new
                p_ref[:, :, c] = p.astype(BF16)
            acc_ref[...] = acc_ref[...] * alpha_ref[...] + _tn(_stack_slots(v_ref, g), p_ref[...])

        @pl.when(ki < qi)
        def _():
            step(False)

        @pl.when(ki == qi)
        def _():
            step(True)
            out = acc_ref[...] / l_ref[...]
            lse = jnp.broadcast_to(m_ref[...] + jnp.log(l_ref[...]), out.shape)
            for j in range(g):
                o_ref[:, pl.ds(j * SLOT, SLOT)] = out[j].T
                lse_ref[:, pl.ds(j * SLOT, SLOT)] = lse[j].T

    q_spec, k_spec = _attn_specs(tile, True)
    return pl.pallas_call(
        body, name="attn_fwd",
        grid=(N_HEADS // g, steps, steps),
        in_specs=[q_spec, k_spec, k_spec],
        out_specs=[q_spec, q_spec],
        out_shape=[jax.ShapeDtypeStruct((t, N_HEADS * SLOT), F32)] * 2,
        scratch_shapes=[pltpu.VMEM((g, 1, tile), F32), pltpu.VMEM((g, 1, tile), F32), pltpu.VMEM((g, 1, tile), F32),
                        pltpu.VMEM((g, SLOT, tile), F32), pltpu.VMEM((g, tile, tile), F32), pltpu.VMEM((g, tile, tile), BF16)],
        compiler_params=pltpu.CompilerParams(dimension_semantics=("parallel", "parallel", "arbitrary")),
    )(q, k, v)


def _attn_grad_scores(q, k, v, do, lse_ref, delta_ref, diagonal):
    g = ATT_GROUP
    p = jnp.exp(_scores(q, k, diagonal) - _stack_slots(lse_ref, g)[:, :, 0:1])
    dp = _nt(do, v)
    return p, p * (dp - _stack_slots(delta_ref, g)[:, :, 0:1]) * ATT_SCALE


def _attn_bwd_q(q, k, v, do, lse, delta):
    t = q.shape[0]
    tile = min(ATT_T, t)
    steps = t // tile
    g = ATT_GROUP

    def body(q_ref, k_ref, v_ref, do_ref, lse_ref, delta_ref, dq_ref, acc_ref):
        qi, ki = pl.program_id(1), pl.program_id(2)

        @pl.when(ki == 0)
        def _():
            acc_ref[...] = jnp.zeros_like(acc_ref)

        def step(diagonal):
            kk = _stack_slots(k_ref, g)
            _, ds = _attn_grad_scores(_stack_slots(q_ref, g), kk, _stack_slots(v_ref, g),
                                      _stack_slots(do_ref, g).astype(BF16), lse_ref, delta_ref, diagonal)
            acc_ref[...] += _nn(ds.astype(BF16), kk)

        @pl.when(ki < qi)
        def _():
            step(False)

        @pl.when(ki == qi)
        def _():
            step(True)
            _unstack_slots(dq_ref, acc_ref[...])

    q_spec, k_spec = _attn_specs(tile, True)
    return pl.pallas_call(
        body, name="attn_bwd_q",
        grid=(N_HEADS // g, steps, steps),
        in_specs=[q_spec, k_spec, k_spec, q_spec, q_spec, q_spec],
        out_specs=q_spec,
        out_shape=jax.ShapeDtypeStruct((t, N_HEADS * SLOT), F32),
        scratch_shapes=[pltpu.VMEM((g, tile, SLOT), F32)],
        compiler_params=pltpu.CompilerParams(dimension_semantics=("parallel", "parallel", "arbitrary")),
    )(q, k, v, do, lse, delta)


def _attn_bwd_kv(q, k, v, do, lse, delta):
    t = q.shape[0]
    tile = min(ATT_T, t)
    steps = t // tile
    g = ATT_GROUP

    def body(q_ref, k_ref, v_ref, do_ref, lse_ref, delta_ref, dk_ref, dv_ref, dk_acc, dv_acc):
        ki, qi = pl.program_id(1), pl.program_id(2)

        @pl.when(qi == 0)
        def _():
            dk_acc[...] = jnp.zeros_like(dk_acc)
            dv_acc[...] = jnp.zeros_like(dv_acc)

        def step(diagonal):
            qq = _stack_slots(q_ref, g)
            do_b = _stack_slots(do_ref, g).astype(BF16)
            p, ds = _attn_grad_scores(qq, _stack_slots(k_ref, g), _stack_slots(v_ref, g), do_b, lse_ref, delta_ref, diagonal)
            dv_acc[...] += _tn(p.astype(BF16), do_b)
            dk_acc[...] += _tn(ds.astype(BF16), qq)

        @pl.when(qi > ki)
        def _():
            step(False)

        @pl.when(qi == ki)
        def _():
            step(True)

        @pl.when(qi == steps - 1)
        def _():
            _unstack_slots(dk_ref, dk_acc[...])
            _unstack_slots(dv_ref, dv_acc[...])

    q_spec, k_spec = _attn_specs(tile, False)
    return pl.pallas_call(
        body, name="attn_bwd_kv",
        grid=(N_HEADS // g, steps, steps),
        in_specs=[q_spec, k_spec, k_spec, q_spec, q_spec, q_spec],
        out_specs=[k_spec, k_spec],
        out_shape=[jax.ShapeDtypeStruct((t, N_HEADS * SLOT), F32)] * 2,
        scratch_shapes=[pltpu.VMEM((g, tile, SLOT), F32), pltpu.VMEM((g, tile, SLOT), F32)],
        compiler_params=pltpu.CompilerParams(dimension_semantics=("parallel", "parallel", "arbitrary")),
    )(q, k, v, do, lse, delta)


def _shift_down(x, s):
    if s == 0:
        return x
    row = lax.broadcasted_iota(jnp.int32, x.shape, 0)
    return jnp.where(row >= s, pltpu.roll(x, s, 0), 0.0)


def _shift_up(x, s):
    if s == 0:
        return x
    n = x.shape[0]
    row = lax.broadcasted_iota(jnp.int32, x.shape, 0)
    return jnp.where(row < n - s, pltpu.roll(x, n - s, 0), 0.0)


def _l2norm(x):
    return x * lax.rsqrt(jnp.sum(x * x, axis=-1, keepdims=True) + EPS)


def _conv_pre(x, w):
    y = w[GDN_CONV - 1:GDN_CONV, :] * x
    for s in range(1, GDN_CONV):
        y = y + w[GDN_CONV - 1 - s:GDN_CONV - s, :] * _shift_down(x, s)
    return y


def _gdn_conv_fwd(x, w):
    t, width = x.shape

    def body(x_ref, w_ref, o_ref):
        act = _silu(_conv_pre(x_ref[...], w_ref[...]))
        normed = pl.program_id(0) < 2 * N_HEADS
        o_ref[...] = jnp.where(normed, _l2norm(act), act)

    return pl.pallas_call(
        body, name="gdn_conv_fwd",
        grid=(width // SLOT,),
        in_specs=[pl.BlockSpec((t, SLOT), lambda j: (0, j)), pl.BlockSpec((GDN_CONV, SLOT), lambda j: (0, j))],
        out_specs=pl.BlockSpec((t, SLOT), lambda j: (0, j)),
        out_shape=jax.ShapeDtypeStruct((t, width), F32),
        compiler_params=pltpu.CompilerParams(dimension_semantics=("parallel",)),
    )(x, w)


def _gdn_conv_bwd(x, w, dout):
    t, width = x.shape

    def body(x_ref, w_ref, do_ref, dx_ref, dw_ref):
        xv, wv = x_ref[...], w_ref[...]
        y = _conv_pre(xv, wv)
        sig = _sigmoid(y)
        act = y * sig
        _, pull = jax.vjp(_l2norm, act)
        normed = pl.program_id(0) < 2 * N_HEADS
        dact = jnp.where(normed, pull(do_ref[0])[0], do_ref[0])
        dy = dact * (sig * (1.0 + y * (1.0 - sig)))
        dx = wv[GDN_CONV - 1:GDN_CONV, :] * dy
        for s in range(1, GDN_CONV):
            dx = dx + wv[GDN_CONV - 1 - s:GDN_CONV - s, :] * _shift_up(dy, s)
        dx_ref[...] = dx.astype(BF16)
        for s in range(GDN_CONV):
            dw_ref[GDN_CONV - 1 - s:GDN_CONV - s, :] = jnp.sum(dy * _shift_down(xv, s), axis=0, keepdims=True)

    col = pl.BlockSpec((t, SLOT), lambda j: (0, j))
    tap = pl.BlockSpec((GDN_CONV, SLOT), lambda j: (0, j))
    return pl.pallas_call(
        body, name="gdn_conv_bwd",
        grid=(width // SLOT,),
        in_specs=[col, tap, pl.BlockSpec((1, t, SLOT), lambda j: (j // N_HEADS, 0, j % N_HEADS))],
        out_specs=[col, tap],
        out_shape=[jax.ShapeDtypeStruct((t, width), BF16), jax.ShapeDtypeStruct((GDN_CONV, width), F32)],
        compiler_params=pltpu.CompilerParams(dimension_semantics=("parallel",)),
    )(x, w, dout)


def _softplus(x):
    e = jnp.exp(-jnp.abs(x))
    u = 1.0 + e
    log1p = jnp.where(u == 1.0, e, jnp.log(u) * e / jnp.where(u == 1.0, 1.0, u - 1.0))
    return jnp.maximum(x, 0.0) + log1p


def _gates_fwd(ab, a_log, dt_bias):
    def fn(rows, consts):
        (abv,), (alog, dtb) = rows, consts
        g = -jnp.exp(alog) * _softplus(abv + dtb)
        beta = _sigmoid(abv)
        shape = (abv.shape[0], SLOT)
        g_slots = [jnp.broadcast_to(g[:, h:h + 1], shape) for h in range(N_HEADS)]
        b_slots = [jnp.broadcast_to(beta[:, N_HEADS + h:N_HEADS + h + 1], shape) for h in range(N_HEADS)]
        return [jnp.concatenate(g_slots, axis=1), jnp.concatenate(b_slots, axis=1)], []

    width = N_HEADS * SLOT
    return _rowwise("gdn_gates_fwd", fn, [ab], [a_log, dt_bias], [(width, F32), (width, F32)])


def _gates_bwd(ab, a_log, dt_bias, dg, dbeta):
    def fn(rows, consts):
        (abv, dgv, dbv), (alog, dtb) = rows, consts
        lane = lax.broadcasted_iota(jnp.int32, abv.shape, 1)
        dg_tok = jnp.zeros_like(abv)
        db_tok = jnp.zeros_like(abv)
        for h in range(N_HEADS):
            dg_tok = dg_tok + jnp.where(lane == h, jnp.sum(dgv[:, h * SLOT:(h + 1) * SLOT], axis=1, keepdims=True), 0.0)
            db_tok = db_tok + jnp.where(lane == N_HEADS + h, jnp.sum(dbv[:, h * SLOT:(h + 1) * SLOT], axis=1, keepdims=True), 0.0)
        xa = abv + dtb
        g = -jnp.exp(alog) * _softplus(xa)
        da = dg_tok * (-jnp.exp(alog)) * _sigmoid(xa)
        beta = _sigmoid(abv)
        dab = jnp.where(lane < N_HEADS, da, db_tok * beta * (1.0 - beta))
        dab = jnp.where(lane < 2 * N_HEADS, dab, 0.0)
        d_alog = jnp.sum(jnp.where(lane < N_HEADS, dg_tok * g, 0.0), axis=0, keepdims=True)
        d_dtb = jnp.sum(jnp.where(lane < N_HEADS, da, 0.0), axis=0, keepdims=True)
        return [dab], [d_alog, d_dtb]

    return _rowwise("gdn_gates_bwd", fn, [ab, dg, dbeta], [a_log, dt_bias], [(SLOT, F32)], sums=[SLOT, SLOT])


ROPE_HALF = MLA_ROPE // 2


def _rope_tables(positions):
    freqs = ROPE_THETA ** (-jnp.arange(ROPE_HALF, dtype=F32) / ROPE_HALF)
    ang = positions.astype(F32)[:, None] * freqs
    cos, sin = jnp.cos(ang), jnp.sin(ang)
    t = positions.shape[0]
    ones, zeros = jnp.ones((t, MLA_NOPE), F32), jnp.zeros((t, MLA_NOPE), F32)
    tail = jnp.zeros((t, SLOT - MLA_NOPE - MLA_ROPE), F32)
    half0 = jnp.zeros((t, ROPE_HALF), F32)
    same = jnp.concatenate([ones, cos, cos, tail], axis=1)
    from_low = jnp.concatenate([zeros, half0, sin, tail], axis=1)
    from_high = jnp.concatenate([zeros, -sin, half0, tail], axis=1)
    return same, from_low, from_high


def _rope(x, tabs):
    same, from_low, from_high = tabs
    width = x.shape[1]
    return x * same + pltpu.roll(x, ROPE_HALF, 1) * from_low + pltpu.roll(x, width - ROPE_HALF, 1) * from_high


def _rope_transposed(dy, tabs):
    same, from_low, from_high = tabs
    width = dy.shape[1]
    return dy * same + pltpu.roll(dy * from_low, width - ROPE_HALF, 1) + pltpu.roll(dy * from_high, ROPE_HALF, 1)


def _tile_slots(tab):
    return jnp.concatenate([tab] * N_HEADS, axis=1)


A_WIDTH = MLA_Q_RANK + MLA_KV_RANK + 2 * SLOT
A_KPE = MLA_Q_RANK + MLA_KV_RANK
A_AB = A_KPE + SLOT
WIDE = N_HEADS * SLOT


def _mla_pre_fwd(proj_a, tabs, g_q, g_kv):
    def fn(rows, consts):
        pa, *tb = rows
        gq, gkv = consts
        return [_rms(pa[:, :MLA_Q_RANK], gq, MLA_Q_RANK), _rms(pa[:, MLA_Q_RANK:A_KPE], gkv, MLA_KV_RANK),
                _rope(pa[:, A_KPE:A_AB], tb)], []

    return _rowwise("mla_pre_fwd", fn, [proj_a, *tabs], [g_q, g_kv], [(MLA_Q_RANK, BF16), (MLA_KV_RANK, BF16), (SLOT, F32)])


def _mla_pre_bwd(proj_a, tabs, g_q, g_kv, dcqn, dckvn, dkpe, dab):
    def fn(rows, consts):
        pa, t0, t1, t2, dq, dkv, dk, da = rows
        gq, gkv = consts
        _, pull_q = jax.vjp(lambda x, g: _rms(x, g, MLA_Q_RANK), pa[:, :MLA_Q_RANK], gq)
        _, pull_kv = jax.vjp(lambda x, g: _rms(x, g, MLA_KV_RANK), pa[:, MLA_Q_RANK:A_KPE], gkv)
        dcq, dgq = pull_q(dq)
        dckv, dgkv = pull_kv(dkv)
        return [jnp.concatenate([dcq, dckv, _rope_transposed(dk, (t0, t1, t2)), da], axis=1)], [dgq, dgkv]

    return _rowwise("mla_pre_bwd", fn, [proj_a, *tabs, dcqn, dckvn, dkpe, dab], [g_q, g_kv], [(A_WIDTH, BF16)],
                    sums=[MLA_Q_RANK, MLA_KV_RANK])


def _mla_qkv_fwd(q_p, kv_p, kpe, tabs):
    def fn(rows, consts):
        qv, kvv, kp, *tb = rows
        q = _rope(qv, [_tile_slots(x) for x in tb])
        k = kvv[:, :WIDE] + _tile_slots(kp)
        return [q, k, kvv[:, WIDE:]], []

    return _rowwise("mla_qkv_fwd", fn, [q_p, kv_p, kpe, *tabs], [], [(WIDE, BF16)] * 3)


def _mla_qkv_bwd(dq, dk, dv, tabs):
    def fn(rows, consts):
        dqv, dkv, dvv, *tb = rows
        dkpe = dkv[:, :SLOT]
        for h in range(1, N_HEADS):
            dkpe = dkpe + dkv[:, h * SLOT:(h + 1) * SLOT]
        return [_rope_transposed(dqv, [_tile_slots(x) for x in tb]), jnp.concatenate([dkv, dvv], axis=1), dkpe], []

    return _rowwise("mla_qkv_bwd", fn, [dq, dk, dv, *tabs], [], [(WIDE, BF16), (2 * WIDE, BF16), (SLOT, F32)])


def _slot_sum(x):
    parts = [jnp.broadcast_to(jnp.sum(x[:, h * SLOT:(h + 1) * SLOT], axis=1, keepdims=True), (x.shape[0], SLOT))
             for h in range(N_HEADS)]
    return jnp.concatenate(parts, axis=1)


def _mix_join(o_mla, o_gdn, gate, g_mla, g_gdn):
    mla = _rms(o_mla, g_mla, N_HEADS * MLA_V)
    gdn = o_gdn * lax.rsqrt(_slot_sum(o_gdn * o_gdn) * (1.0 / GDN_D) + EPS) * g_gdn * _silu(gate)
    return mla, gdn


def _mix_join_fwd(o_mla, o_gdn, gate, g_mla, g_gdn):
    def fn(rows, consts):
        return [jnp.concatenate(_mix_join(*rows, *consts), axis=1)], []

    return _rowwise("mix_join_fwd", fn, [o_mla, o_gdn, gate], [g_mla, g_gdn], [(2 * WIDE, BF16)])


def _mix_join_bwd(o_mla, o_gdn, gate, g_mla, g_gdn, dcat):
    def fn(rows, consts):
        om, og, gt, dc = rows
        gm, gg = consts
        _, pull = jax.vjp(lambda x, g: _rms(x, g, N_HEADS * MLA_V), om, gm)
        dom, dgm = pull(dc[:, :WIDE])
        dy = dc[:, WIDE:]
        r = lax.rsqrt(_slot_sum(og * og) * (1.0 / GDN_D) + EPS)
        sig = _sigmoid(gt)
        normed = og * r
        dn = dy * gg * (gt * sig)
        dog = r * dn - normed * (r * r) * _slot_sum(dn * og) * (1.0 / GDN_D)
        dgt = dy * normed * gg * (sig * (1.0 + gt * (1.0 - sig)))
        dgg = jnp.sum(dy * normed * (gt * sig), axis=0, keepdims=True)
        return [dom, _slot_sum(dom * om), dog, dgt], [dgm, dgg]

    return _rowwise("mix_join_bwd", fn, [o_mla, o_gdn, gate, dcat], [g_mla, g_gdn],
                    [(WIDE, F32), (WIDE, F32), (WIDE, F32), (WIDE, BF16)], sums=[WIDE, WIDE])


def _norm_residual_fwd(name, x, h, g, out_dtypes):
    dm = x.shape[1]

    def fn(rows, consts):
        y = rows[0] + _rms(rows[1], consts[0], dm)
        return [y] + [_rms(y, gg, dm) for gg in consts[1:]], []

    return _rowwise(name, fn, [x, h], list(g), [(dm, dt) for dt in out_dtypes])


def _norm_residual_bwd(name, h, g, dy):
    dm = h.shape[1]

    def fn(rows, consts):
        _, pull = jax.vjp(lambda hv, gv: _rms(hv, gv, dm), rows[0], consts[0])
        dh, dg = pull(rows[1])
        return [dh], [dg]

    return _rowwise(name, fn, [h, dy], [g], [(dm, BF16)], sums=[dm])


def _norm_bwd_add(name, x, g, dns, dy):
    dm = x.shape[1]

    def fn(rows, consts):
        xv, dyv, *parts = rows
        dn = parts[0]
        for p in parts[1:]:
            dn = dn + p
        _, pull = jax.vjp(lambda a, gv: _rms(a, gv, dm), xv, consts[0])
        dx, dg = pull(dn)
        return [dyv + dx], [dg]

    return _rowwise(name, fn, [x, dy, *dns], [g], [(dm, F32)], sums=[dm])


def _loss_fwd(y, target):
    dm = y.shape[1]

    def fn(rows, consts):
        err = rows[0] - rows[1]
        sq = err * err
        lanes = sq[:, :SLOT]
        for j in range(1, dm // SLOT):
            lanes = lanes + sq[:, j * SLOT:(j + 1) * SLOT]
        return [err * (1.0 / dm)], [jnp.sum(lanes, axis=0, keepdims=True) * (0.5 / dm)]

    return _rowwise("loss", fn, [y, target], [], [(dm, F32)], sums=[SLOT])


def _norm_fwd(name, x, g):
    dm = x.shape[1]
    return _rowwise(name, lambda rows, consts: ([_rms(rows[0], consts[0], dm)], []), [x], [g], [(dm, BF16)])[0]


W_IN_CUTS = (0, 256, 384, 416, 1952, 1960, 1968, 2480)


def _heads_out(w, per_head, axis=-1):
    axis = axis % w.ndim
    shape = w.shape
    n = shape[axis] // per_head
    w = w.reshape(shape[:axis] + (n, per_head) + shape[axis + 1:])
    pad = [(0, 0)] * w.ndim
    pad[axis + 1] = (0, SLOT - per_head)
    return jnp.pad(w, pad).reshape(shape[:axis] + (n * SLOT,) + shape[axis + 1:])


def _heads_in(w, per_head, axis=-1):
    axis = axis % w.ndim
    shape = w.shape
    n = shape[axis] // SLOT
    w = w.reshape(shape[:axis] + (n, SLOT) + shape[axis + 1:])
    w = lax.slice_in_dim(w, 0, per_head, axis=axis + 1)
    return w.reshape(shape[:axis] + (n * per_head,) + shape[axis + 1:])


def _pad_lanes(v, lo, width=SLOT):
    return jnp.pad(v, [(0, 0)] * (v.ndim - 1) + [(lo, width - lo - v.shape[-1])])


def _pad_rows(v, lo, rows=SLOT):
    return jnp.pad(v, [(lo, rows - lo - v.shape[0])] + [(0, 0)] * (v.ndim - 1))


def _layout_weights(w):
    c = W_IN_CUTS
    w_in = w["w_in_t"]
    p = {}
    p["w_a"] = jnp.concatenate([w_in[c[0]:c[2]], _pad_rows(w_in[c[2]:c[3]], MLA_NOPE), _pad_rows(w_in[c[4]:c[6]], 0)], axis=0)
    p["w_qkv"] = _heads_out(w_in[c[3]:c[4]], GDN_D, axis=0)
    p["w_gate"] = _heads_out(w_in[c[6]:c[7]], GDN_D, axis=0)
    p["w_uq"] = _heads_out(w["uq_t"], MLA_NOPE + MLA_ROPE, axis=0)
    ukv = w["ukv_t"].reshape(N_HEADS, MLA_NOPE + MLA_V, MLA_KV_RANK)
    p["w_kv"] = jnp.concatenate([_heads_out(ukv[:, :MLA_NOPE].reshape(-1, MLA_KV_RANK), MLA_NOPE, axis=0),
                                 _heads_out(ukv[:, MLA_NOPE:].reshape(-1, MLA_KV_RANK), MLA_V, axis=0)], axis=0)
    p["conv"] = _heads_out(w["gdn_conv_w"], GDN_D)
    p["g_mla_out"] = _heads_out(w["mla_out_g"], MLA_V)
    p["g_gdn"] = jnp.tile(_pad_lanes(w["gdn_norm_g"], 0), (1, N_HEADS))
    p["a_log"] = _pad_lanes(w["gdn_a_log"], 0)
    p["dt_bias"] = _pad_lanes(w["gdn_dt_bias"], 0)
    return p


def _unlayout_grads(d):
    c = W_IN_CUTS
    g = {}
    da = d["w_a"]
    kpe0 = A_KPE + MLA_NOPE
    g["w_in_t"] = jnp.concatenate([da[:A_KPE], da[kpe0:kpe0 + MLA_ROPE], _heads_in(d["w_qkv"], GDN_D, axis=0),
                                   da[A_AB:A_AB + 2 * N_HEADS], _heads_in(d["w_gate"], GDN_D, axis=0)], axis=0)
    assert g["w_in_t"].shape[0] == c[-1]
    g["uq_t"] = _heads_in(d["w_uq"], MLA_NOPE + MLA_ROPE, axis=0)
    dk = _heads_in(d["w_kv"][:WIDE], MLA_NOPE, axis=0).reshape(N_HEADS, MLA_NOPE, MLA_KV_RANK)
    dv = _heads_in(d["w_kv"][WIDE:], MLA_V, axis=0).reshape(N_HEADS, MLA_V, MLA_KV_RANK)
    g["ukv_t"] = jnp.concatenate([dk, dv], axis=1).reshape(-1, MLA_KV_RANK)
    g["w_out"] = _heads_in(d["w_out"], GDN_D, axis=0)
    g["gdn_conv_w"] = _heads_in(d["conv"], GDN_D)
    g["mla_out_g"] = _heads_in(d["g_mla_out"], MLA_V)
    g["gdn_norm_g"] = jnp.sum(d["g_gdn"].reshape(N_HEADS, SLOT), axis=0, keepdims=True)[:, :GDN_D]
    g["gdn_a_log"] = d["a_log"][:, :N_HEADS]
    g["gdn_dt_bias"] = d["dt_bias"][:, :N_HEADS]
    return g


def _weight_grad(name, cots, acts, out_dtype=F32, tm=1024, tn=1024, tk=512, carry=None):
    return _matmul(name, cots, acts, "tn", out_dtype=out_dtype, tm=tm, tn=tn, tk=tk, carry=carry)


def _by_device(a):
    return a.astype(BF16).reshape((N_DEV, a.shape[0] // N_DEV) + a.shape[1:])


def _rows_of(blocks):
    return blocks.reshape((-1,) + blocks.shape[2:])


def _local_step(x, positions, target, w, mid, late):
    tabs = _rope_tables(positions)

    (h1, x1), gathered = _ffn_fwd("ffn1_fwd", x, w["ffn1_pre_g"], w["ffn1"], 0, w["ffn1_post_g"], carry=mid)
    w = dict(w, w_in_t=_rows_of(gathered[0]), uq_t=_rows_of(gathered[1]), ukv_t=_rows_of(gathered[2]))
    p = _layout_weights(w)
    hn = _norm_fwd("mix_pre_norm", x1, w["mix_pre_g"])
    proj_a = _matmul("proj_a", hn, p["w_a"], "nt")
    proj_qkv = _matmul("proj_qkv", hn, p["w_qkv"], "nt")
    proj_gate = _matmul("proj_gate", hn, p["w_gate"], "nt")
    cqn, ckvn, kpe = _mla_pre_fwd(proj_a, tabs, w["mla_q_norm_g"], w["mla_kv_norm_g"])
    q_p = _matmul("mla_q", cqn, p["w_uq"], "nt")
    kv_p = _matmul("mla_kv", ckvn, p["w_kv"], "nt")
    q, k, v = _mla_qkv_fwd(q_p, kv_p, kpe, tabs)
    o_mla, lse = _attn_fwd(q, k, v)
    ab = (proj_a, SLOT, A_AB // SLOT)
    qkv_n = _gdn_conv_fwd(proj_qkv, p["conv"])
    gb, bb = _gates_fwd(ab, p["a_log"], p["dt_bias"])
    (o_gdn, keep), (ffn2, w_out) = _gdn_fwd(qkv_n, gb, bb, carry=late)
    p["w_out"] = _heads_out(_rows_of(w_out), GDN_D, axis=0)
    cat = _mix_join_fwd(o_mla, o_gdn, proj_gate, p["g_mla_out"], p["g_gdn"])[0]
    mixed = _matmul("mix_out", cat, p["w_out"], "nn")
    x2 = _norm_residual_fwd("mix_post", x1, mixed, [w["mix_post_g"]], [F32])[0]
    (h2, y), _ = _ffn_fwd("ffn2_fwd", x2, w["ffn2_pre_g"], ffn2, 0, w["ffn2_post_g"])
    dy, loss_lanes = _loss_fwd(y, target)

    g = {}
    (dx2, xn2, dh2, a2, dhg2, dhu2, g["ffn2_pre_g"], g["ffn2_post_g"]), _ = _ffn_bwd(
        "ffn2_bwd", x2, h2, dy, w["ffn2_pre_g"], ffn2, 0, w["ffn2_post_g"])
    ffn2_grads = _Scatter([_by_device(_weight_grad("ffn2_dw_gate", dhg2, xn2, BF16, tm=1408)),
                           _by_device(_weight_grad("ffn2_dw_up", dhu2, xn2, BF16, tm=1408)),
                           _by_device(_weight_grad("ffn2_dw_down", a2, dh2, BF16, tm=1408))])
    dmixed, g["mix_post_g"] = _norm_residual_bwd("mix_post_bwd", mixed, w["mix_post_g"], dx2)
    dcat = _matmul("mix_out_dx", dmixed, p["w_out"], "nt")
    d = {}
    d["w_out"] = _weight_grad("mix_out_dw", cat, dmixed)
    do_mla, delta, do_gdn, dgate, d["g_mla_out"], d["g_gdn"] = _mix_join_bwd(o_mla, o_gdn, proj_gate, p["g_mla_out"], p["g_gdn"], dcat)
    dq = _attn_bwd_q(q, k, v, do_mla, lse, delta)
    dk, dv = _attn_bwd_kv(q, k, v, do_mla, lse, delta)
    dq_p, dkv_p, dkpe = _mla_qkv_bwd(dq, dk, dv, tabs)
    dcqn = _matmul("mla_q_dx", dq_p, p["w_uq"], "nn")
    d["w_uq"] = _weight_grad("mla_q_dw", dq_p, cqn)
    dckvn = _matmul("mla_kv_dx", dkv_p, p["w_kv"], "nn")
    d["w_kv"] = _weight_grad("mla_kv_dw", dkv_p, ckvn)
    (dqkv_n, dgb, dbb), landed_ffn2 = _gdn_bwd(qkv_n, gb, bb, keep, do_gdn, carry=ffn2_grads)
    dab, d["a_log"], d["dt_bias"] = _gates_bwd(ab, p["a_log"], p["dt_bias"], dgb, dbb)
    dproj_qkv, d["conv"] = _gdn_conv_bwd(proj_qkv, p["conv"], dqkv_n)
    dproj_a, g["mla_q_norm_g"], g["mla_kv_norm_g"] = _mla_pre_bwd(
        proj_a, tabs, w["mla_q_norm_g"], w["mla_kv_norm_g"], dcqn, dckvn, dkpe, dab)
    dhn = [_matmul("proj_a_dx", dproj_a, p["w_a"], "nn"), _matmul("proj_qkv_dx", dproj_qkv, p["w_qkv"], "nn"),
           _matmul("proj_gate_dx", dgate, p["w_gate"], "nn")]
    d["w_a"] = _weight_grad("proj_a_dw", dproj_a, hn, tm=640)
    d["w_qkv"] = _weight_grad("proj_qkv_dw", dproj_qkv, hn)
    d["w_gate"] = _weight_grad("proj_gate_dw", dgate, hn)
    dx1, g["mix_pre_g"] = _norm_bwd_add("mix_pre_bwd", x1, w["mix_pre_g"], dhn, dx2)
    g.update(_unlayout_grads(d))
    others = [t for t, _ in OTHER.values()]
    (dx, xn1, dh1, a1, dhg1, dhu1, g["ffn1_pre_g"], g["ffn1_post_g"]), landed_others = _ffn_bwd(
        "ffn1_bwd", x, h1, dx1, w["ffn1_pre_g"], w["ffn1"], 0, w["ffn1_post_g"], carry=_Scatter([_by_device(g.pop(t)) for t in others]))
    dw_down = _weight_grad("ffn1_dw_down", a1, dh1, BF16, tm=1408)
    dw_gate, (landed_down,) = _weight_grad("ffn1_dw_gate", dhg1, xn1, BF16, tm=1408, carry=_Scatter([_by_device(dw_down)]))
    dw_up, (landed_gate,) = _weight_grad("ffn1_dw_up", dhu1, xn1, BF16, tm=1408, carry=_Scatter([_by_device(dw_gate)]))
    (landed_up,) = _exchange("scatter_last", _Scatter([_by_device(dw_up)]))
    landed = dict(zip(list(FFN_NAMES) + list(OTHER),
                      [landed_gate, landed_up, landed_down] + list(landed_ffn2) + list(landed_others)))
    return loss_lanes, dx, g, landed


MESH_AXES = ("x", "y", "c")
N_LINKS = N_DEV - 1


def _place():
    return tuple(lax.axis_index(a) for a in MESH_AXES)


def _block_of(dev):
    x, y, c = dev
    return 4 * x + 2 * y + c


def _remote_copy(src, dst, sems, k, to):
    send_sems, recv_sems = sems
    return pltpu.make_async_remote_copy(src_ref=src, dst_ref=dst, send_sem=send_sems.at[k], recv_sem=recv_sems.at[k],
                                        device_id=to, device_id_type=pl.DeviceIdType.MESH)


class _Exchange:
    def __init__(self, arrays):
        self.arrays = list(arrays)
        self.n = len(self.arrays)
        self.specs = [pl.BlockSpec(memory_space=pl.ANY)] * self.n
        self.scratch = [pltpu.SemaphoreType.DMA((self.n * N_LINKS,)), pltpu.SemaphoreType.DMA((self.n * N_LINKS,)),
                        pltpu.SemaphoreType.DMA((self.n,))]

    def split(self, refs):
        n = self.n
        return refs[:n], refs[n:2 * n], (refs[2 * n], refs[2 * n + 1]), refs[2 * n + 2]


class _Gather(_Exchange):
    def out_shape(self):
        return [jax.ShapeDtypeStruct((N_DEV,) + a.shape, a.dtype) for a in self.arrays]

    def _plan(self, ins, outs, sems, local_sems):
        x, y, c = _place()
        me, sibling = (x, y, c), (x, y, 1 - c)
        chips = [(1 - x, y), (x, 1 - y), (1 - x, 1 - y)]

        def copy(a, k, block, to, mine=False):
            src = ins[a] if mine else outs[a].at[_block_of(block)]
            return _remote_copy(src, outs[a].at[_block_of(block)], sems, a * N_LINKS + k, to)

        local = [pltpu.make_async_copy(ins[a], outs[a].at[_block_of(me)], local_sems.at[a]) for a in range(self.n)]
        first = []
        for a in range(self.n):
            first.append(copy(a, 0, me, sibling, mine=True))
            first += [copy(a, 1 + j, me, (*chip, c), mine=True) for j, chip in enumerate(chips)]
        return me, sibling, chips, c, copy, local, first

    def start(self, ins, outs, sems, local_sems):
        *_, local, first = self._plan(ins, outs, sems, local_sems)
        for cp in local + first:
            cp.start()

    def finish(self, ins, outs, sems, local_sems):
        me, sibling, chips, c, copy, local, first = self._plan(ins, outs, sems, local_sems)
        passed = []
        for j, chip in enumerate(chips):
            for a in range(self.n):
                copy(a, 1 + j, (*chip, c), me).wait_recv()
                passed.append(copy(a, 4 + j, (*chip, c), sibling))
                passed[-1].start()
        for a in range(self.n):
            copy(a, 0, sibling, me).wait_recv()
            for j, chip in enumerate(chips):
                copy(a, 4 + j, (*chip, 1 - c), me).wait_recv()
        for cp in first + passed:
            cp.wait_send()
        for cp in local:
            cp.wait()


class _Scatter(_Exchange):
    def out_shape(self):
        return [jax.ShapeDtypeStruct(a.shape, a.dtype) for a in self.arrays]

    def _plan(self, ins, outs, sems, local_sems):
        x, y, c = _place()
        me = _block_of((x, y, c))

        def peer(r):
            return (1 - x if r & 4 else x, 1 - y if r & 2 else y, 1 - c if r & 1 else c)

        local = [pltpu.make_async_copy(ins[a].at[me], outs[a].at[me], local_sems.at[a]) for a in range(self.n)]
        sends = [_remote_copy(ins[a].at[_block_of(peer(r))], outs[a].at[me], sems, a * N_LINKS + r - 1, peer(r))
                 for a in range(self.n) for r in range(1, N_DEV)]
        arrivals = [_remote_copy(ins[a].at[me], outs[a].at[_block_of(peer(r))], sems, a * N_LINKS + r - 1, peer(r))
                    for a in range(self.n) for r in range(1, N_DEV)]
        return local, sends, arrivals

    def start(self, ins, outs, sems, local_sems):
        local, sends, _ = self._plan(ins, outs, sems, local_sems)
        for cp in local + sends:
            cp.start()

    def finish(self, ins, outs, sems, local_sems):
        local, sends, arrivals = self._plan(ins, outs, sems, local_sems)
        for cp in arrivals:
            cp.wait_recv()
        for cp in sends:
            cp.wait_send()
        for cp in local:
            cp.wait()


def _exchange(name, plan):
    def body(*refs):
        parts = plan.split(refs)
        plan.start(*parts)
        plan.finish(*parts)

    return pl.pallas_call(
        body, name=name,
        in_specs=plan.specs,
        out_specs=plan.specs,
        out_shape=plan.out_shape(),
        scratch_shapes=plan.scratch,
    )(*plan.arrays)


def _call_carrying(body, plan, operands, *, name, grid, in_specs, out_specs, out_shape, scratch_shapes, compiler_params):
    if plan is None:
        outs = pl.pallas_call(body, name=name, grid=grid, in_specs=in_specs, out_specs=out_specs, out_shape=out_shape,
                              scratch_shapes=scratch_shapes, compiler_params=compiler_params)(*operands)
        return outs, []
    n_i, n_o, n_s, k = len(in_specs), len(out_specs), len(scratch_shapes), plan.n

    def whole(*refs):
        cut = [n_i, n_i + k, n_i + k + n_o, n_i + 2 * k + n_o, n_i + 2 * k + n_o + n_s]
        own_in, ex_in, own_out, ex_out, own_scr, ex_scr = (refs[a:b] for a, b in zip([0] + cut, cut + [len(refs)]))
        parts = plan.split(ex_in + ex_out + ex_scr)
        first = last = True
        for axis, size in enumerate(grid):
            first = first & (pl.program_id(axis) == 0)
            last = last & (pl.program_id(axis) == size - 1)

        @pl.when(first)
        def _():
            plan.start(*parts)

        body(*own_in, *own_out, *own_scr)

        @pl.when(last)
        def _():
            plan.finish(*parts)

    outs = pl.pallas_call(
        whole, name=name, grid=grid,
        in_specs=list(in_specs) + plan.specs, out_specs=list(out_specs) + plan.specs,
        out_shape=list(out_shape) + plan.out_shape(), scratch_shapes=list(scratch_shapes) + plan.scratch,
        compiler_params=compiler_params,
    )(*operands, *plan.arrays)
    return outs[:n_o], outs[n_o:]


def _row_tile(rows, target=256):
    best = rows
    for cand in range(16, min(rows, target) + 1, 16):
        if rows % cand == 0:
            best = cand
    return best


def _sum_blocks(name, blocks):
    rows, width = blocks.shape[-2:]
    tm = _row_tile(rows)

    def body(x_ref, o_ref):
        acc = x_ref[0].astype(F32)
        for d in range(1, N_DEV):
            acc = acc + x_ref[d].astype(F32)
        o_ref[...] = acc

    return pl.pallas_call(
        body, name=name,
        grid=(rows // tm,),
        in_specs=[pl.BlockSpec((N_DEV, tm, width), lambda i: (0, i, 0))],
        out_specs=pl.BlockSpec((tm, width), lambda i: (i, 0)),
        out_shape=jax.ShapeDtypeStruct((rows, width), F32),
        compiler_params=pltpu.CompilerParams(dimension_semantics=("parallel",)),
    )(blocks)


def _all_reduce_small(name, vec):
    rows, width = vec.shape

    def body(x_ref, o_ref, all_ref, send_sems, recv_sems):
        x, y, c = _place()
        me = _block_of((x, y, c))
        all_ref[me] = x_ref[...]

        def peer(r):
            return (1 - x if r & 4 else x, 1 - y if r & 2 else y, 1 - c if r & 1 else c)

        def copy(r, block):
            return _remote_copy(x_ref, all_ref.at[block], (send_sems, recv_sems), r - 1, peer(r))

        sends = [copy(r, me) for r in range(1, N_DEV)]
        for cp in sends:
            cp.start()
        for r in range(1, N_DEV):
            copy(r, _block_of(peer(r))).wait_recv()
        for cp in sends:
            cp.wait_send()
        acc = all_ref[0]
        for d in range(1, N_DEV):
            acc = acc + all_ref[d]
        o_ref[...] = acc

    return pl.pallas_call(
        body, name=name,
        in_specs=[pl.BlockSpec(memory_space=pltpu.VMEM)],
        out_specs=pl.BlockSpec(memory_space=pltpu.VMEM),
        out_shape=jax.ShapeDtypeStruct((rows, width), F32),
        scratch_shapes=[pltpu.VMEM((N_DEV, rows, width), F32), pltpu.SemaphoreType.DMA((N_LINKS,)), pltpu.SemaphoreType.DMA((N_LINKS,))],
    )(vec)


def _adamw(name, w, g, m, v):
    def fn(rows, consts):
        wv, gv, mv, vv = rows
        m2 = ADAM_B1 * mv + (1.0 - ADAM_B1) * gv
        v2 = ADAM_B2 * vv + (1.0 - ADAM_B2) * jnp.square(gv)
        m_hat = m2 / (1.0 - ADAM_B1 ** ADAM_STEP)
        v_hat = v2 / (1.0 - ADAM_B2 ** ADAM_STEP)
        return [-ADAM_LR * (m_hat / (jnp.sqrt(v_hat) + ADAM_EPS) + ADAM_WD * wv), m2, v2], []

    return _rowwise(name, fn, [w, g, m, v], [], [(w.shape[1], F32)] * 3, tm=_row_tile(w.shape[0]))


ROW = 1024
FFN_NAMES = ("ffn1_w_gate", "ffn1_w_up", "ffn1_w_down", "ffn2_w_gate", "ffn2_w_up", "ffn2_w_down")
OTHER = {"w_in": ("w_in_t", True), "mla_w_uq": ("uq_t", True), "mla_w_ukv": ("ukv_t", True), "w_out": ("w_out", False)}
BY_COLUMNS = ("ffn1_w_gate", "ffn1_w_up", "ffn2_w_gate", "ffn2_w_up", "w_in", "mla_w_uq", "mla_w_ukv")
SMALL = {
    "ffn1_pre_g": (1024, 1024), "ffn1_post_g": (1024, 1024), "mix_pre_g": (1024, 1024), "mla_q_norm_g": (256, 256),
    "mla_kv_norm_g": (128, 128), "mla_out_g": (512, 512), "gdn_a_log": (8, 128), "gdn_dt_bias": (8, 128),
    "gdn_norm_g": (64, 128), "mix_post_g": (1024, 1024), "ffn2_pre_g": (1024, 1024), "ffn2_post_g": (1024, 1024),
}
CONV_SHAPE = (GDN_CONV, 3 * N_HEADS * GDN_D)
CONV_SHARD = (GDN_CONV, CONV_SHAPE[1] // N_DEV)
CONV_LANES = CONV_SHAPE[0] * CONV_SHAPE[1]
SMALL_ROWS = 8
REDUCE_ROWS = 16


def _pack_small(vecs, conv, rows):
    parts = [_pad_lanes(vecs[n].reshape(1, -1), 0, r) for n, (_, r) in SMALL.items()]
    parts.append(conv.reshape(1, -1))
    flat = jnp.concatenate(parts, axis=1)
    return _pad_lanes(flat, 0, rows * ROW).reshape(rows, ROW)


def _unpack_small(buf):
    flat = buf.reshape(1, -1)
    out, at = {}, 0
    for n, (w, r) in SMALL.items():
        out[n] = flat[:, at:at + w]
        at += r
    return out, flat[0, at:]


def kernel(x, positions, ffn1_pre_g, ffn1_w_gate, ffn1_w_up, ffn1_w_down, ffn1_post_g, mix_pre_g, w_in, mla_q_norm_g, mla_w_uq, mla_kv_norm_g, mla_w_ukv, mla_out_g, gdn_conv_w, gdn_a_log, gdn_dt_bias, gdn_norm_g, w_out, mix_post_g, ffn2_pre_g, ffn2_w_gate, ffn2_w_up, ffn2_w_down, ffn2_post_g, loss_target, m_ffn1_pre_g, m_ffn1_w_gate, m_ffn1_w_up, m_ffn1_w_down, m_ffn1_post_g, m_mix_pre_g, m_w_in, m_mla_q_norm_g, m_mla_w_uq, m_mla_kv_norm_g, m_mla_w_ukv, m_mla_out_g, m_gdn_conv_w, m_gdn_a_log, m_gdn_dt_bias, m_gdn_norm_g, m_w_out, m_mix_post_g, m_ffn2_pre_g, m_ffn2_w_gate, m_ffn2_w_up, m_ffn2_w_down, m_ffn2_post_g, v_ffn1_pre_g, v_ffn1_w_gate, v_ffn1_w_up, v_ffn1_w_down, v_ffn1_post_g, v_mix_pre_g, v_w_in, v_mla_q_norm_g, v_mla_w_uq, v_mla_kv_norm_g, v_mla_w_ukv, v_mla_out_g, v_gdn_conv_w, v_gdn_a_log, v_gdn_dt_bias, v_gdn_norm_g, v_w_out, v_mix_post_g, v_ffn2_pre_g, v_ffn2_w_gate, v_ffn2_w_up, v_ffn2_w_down, v_ffn2_post_g):
    given = dict(locals())
    order = ["ffn1_pre_g", "ffn1_w_gate", "ffn1_w_up", "ffn1_w_down", "ffn1_post_g", "mix_pre_g", "w_in", "mla_q_norm_g",
             "mla_w_uq", "mla_kv_norm_g", "mla_w_ukv", "mla_out_g", "gdn_conv_w", "gdn_a_log", "gdn_dt_bias", "gdn_norm_g",
             "w_out", "mix_post_g", "ffn2_pre_g", "ffn2_w_gate", "ffn2_w_up", "ffn2_w_down", "ffn2_post_g"]
    assert sorted(order) == sorted(list(FFN_NAMES) + list(OTHER) + list(SMALL) + ["gdn_conv_w"])

    def drop_depth(a):
        return a[0] if a.ndim == 3 else a

    wts = {n: drop_depth(given[n]) for n in order}
    mom = {n: drop_depth(given["m_" + n]) for n in order}
    var = {n: drop_depth(given["v_" + n]) for n in order}
    me = _block_of(_place())

    def wire(n):
        return (wts[n].T if n in BY_COLUMNS else wts[n]).astype(BF16)

    (ffn1,) = _exchange("gather_first", _Gather([jnp.stack([wire(n) for n in FFN_NAMES[:3]])]))
    mid = _Gather([wire(n) for n in ("w_in", "mla_w_uq", "mla_w_ukv")])
    late = _Gather([jnp.stack([wire(n) for n in FFN_NAMES[3:]]), wire("w_out")])
    conv_at = lax.dynamic_update_slice(jnp.zeros((N_DEV, CONV_SHARD[0] * CONV_SHARD[1]), F32),
                                       wts["gdn_conv_w"].reshape(1, -1), (me, 0))
    conv_all = _all_reduce_small("gather_conv", _pad_lanes(conv_at.reshape(1, -1), 0, SMALL_ROWS * ROW).reshape(SMALL_ROWS, ROW))
    full = {n: wts[n] for n in SMALL}
    full["ffn1"] = ffn1
    full["gdn_conv_w"] = conv_all.reshape(-1)[:CONV_LANES].reshape((N_DEV,) + CONV_SHARD).transpose(1, 0, 2).reshape(CONV_SHAPE)

    loss_lanes, dx, grads, landed = _local_step(x[0], positions[0], loss_target[0], full, mid, late)
    loss = lax.psum(jnp.sum(loss_lanes), MESH_AXES)

    sums = {n: _sum_blocks("sum_" + n, blocks) for n, blocks in landed.items()}
    grad = {n: (sums[n].T if n in BY_COLUMNS else sums[n]) for n in sums}
    small_sum = _all_reduce_small("reduce_small", _pack_small(grads, grads["gdn_conv_w"].reshape(-1), REDUCE_ROWS))
    small_grad, conv_grad_full = _unpack_small(small_sum)
    grad.update(small_grad)
    grad["gdn_conv_w"] = lax.dynamic_slice(conv_grad_full[:CONV_LANES].reshape(CONV_SHAPE), (0, me * CONV_SHARD[1]), CONV_SHARD)

    outs = {"grad": grad, "delta": {}, "new_m": {}, "new_v": {}}
    for n in list(FFN_NAMES) + list(OTHER):
        outs["delta"][n], outs["new_m"][n], outs["new_v"][n] = _adamw("adamw_" + n, wts[n], grad[n], mom[n], var[n])
    small = [_pack_small(s, s["gdn_conv_w"].reshape(-1), SMALL_ROWS) for s in (wts, grad, mom, var)]
    for kind, s in zip(("delta", "new_m", "new_v"), _adamw("adamw_small", *small)):
        vecs, conv = _unpack_small(s)
        outs[kind].update(vecs)
        outs[kind]["gdn_conv_w"] = conv[:CONV_SHARD[0] * CONV_SHARD[1]].reshape(CONV_SHARD)
    result = [loss, dx[None]]
    for kind in ("grad", "delta", "new_m", "new_v"):
        result += [outs[kind][n].reshape(given[n].shape) for n in order]
    return tuple(result)
```

```python
import jax
import jax.numpy as jnp
from jax import lax
from jax.experimental import pallas as pl
from jax.experimental.pallas import tpu as pltpu

F32 = jnp.float32
BF16 = jnp.bfloat16
HI = lax.Precision.HIGH
EXACT = lax.Precision.HIGHEST

N_DEV = 8
D_MODEL = 1024
D_FF = 2816
N_HEADS = 8
SLOT = 128
MLA_Q_RANK = 256
MLA_KV_RANK = 128
MLA_NOPE = 64
MLA_ROPE = 32
MLA_V = 64
GDN_D = 64
GDN_CONV = 4
GDN_CHUNK = 64
ROPE_THETA = 10000.0
EPS = 1e-6
ADAM_LR, ADAM_B1, ADAM_B2, ADAM_EPS, ADAM_WD, ADAM_STEP = 0.001, 0.9, 0.999, 1e-08, 0.01, 10


def _dot(a, b, ca, cb, precision=None):
    lead = a.ndim - 2
    batch = tuple(range(lead))
    return lax.dot_general(a, b, (((lead + ca,), (lead + cb,)), (batch, batch)), precision=precision,
                           preferred_element_type=F32)


def _nn(a, b, precision=None):
    return _dot(a, b, 1, 0, precision)


def _nt(a, b, precision=None):
    return _dot(a, b, 1, 1, precision)


def _tn(a, b, precision=None):
    return _dot(a, b, 0, 0, precision)


def _sigmoid(x):
    return 1.0 / (1.0 + jnp.exp(-x))


def _silu(x):
    return x * _sigmoid(x)


def _rms(x, g, n):
    ms = jnp.sum(x * x, axis=-1, keepdims=True) * (1.0 / n)
    return x * lax.rsqrt(ms + EPS) * g


def _chunk_masks():
    c = GDN_CHUNK
    i = lax.broadcasted_iota(jnp.int32, (c, c), 0)
    j = lax.broadcasted_iota(jnp.int32, (c, c), 1)
    lower = i >= j
    strict = i > j
    eye = (i == j).astype(F32)
    blocks = []
    b = 1
    while b < c:
        same = (i // (2 * b)) == (j // (2 * b))
        blocks.append(same & ((i % (2 * b)) >= b) & ((j % (2 * b)) < b))
        b *= 2
    return lower, strict, eye, blocks


def _unit_lower_inverse(low, eye, blocks):
    t = jnp.broadcast_to(eye, low.shape)
    for m in blocks:
        lo = jnp.where(m, low, 0.0)
        t = t - _nn(t, _nn(lo, t, HI), HI)
    return t


@jax.custom_vjp
def _known_inverse(low, tinv):
    return tinv


def _known_inverse_fwd(low, tinv):
    return tinv, tinv


def _known_inverse_bwd(tinv, dt):
    return -_tn(tinv, _nt(dt, tinv, HI), HI), jnp.zeros_like(tinv)


_known_inverse.defvjp(_known_inverse_fwd, _known_inverse_bwd)


def _gdn_chunk(q, k, v, gb, bb, s, masks, tinv=None):
    lower, strict, eye, blocks = masks
    qs = q * (GDN_D ** -0.5)
    gc = _nn(jnp.broadcast_to(lower.astype(F32), gb.shape), gb, EXACT)
    gct = _nt(jnp.broadcast_to(eye, gb.shape), gc, EXACT)
    decay = jnp.exp(jnp.where(lower, gc - gct, -1e30))
    kb = k * bb
    low = jnp.where(strict, _nt(kb, k, HI) * decay, 0.0)
    tinv = _unit_lower_inverse(low, eye, blocks) if tinv is None else _known_inverse(low, tinv)
    eg = jnp.exp(gc)
    w = _nn(tinv, kb * eg, HI)
    u = _nn(tinv, v * bb, HI)
    attn = _nt(qs, k, HI) * decay
    g_end = jnp.sum(gb, axis=-2, keepdims=True)
    k_dec = k * jnp.exp(g_end - gc)
    v_new = u - _nn(w, s, HI)
    o = _nn(qs * eg, s, HI) + _nn(attn, v_new, HI)
    s_new = s * jnp.exp(g_end) + _tn(k_dec, v_new, HI)
    return o, s_new, tinv


GDN_GROUP = 8
GDN_GROUPS = N_HEADS // GDN_GROUP


def _group_heads(ref):
    return jnp.stack([ref[:, pl.ds(j * SLOT, GDN_D)] for j in range(GDN_GROUP)])


def _ungroup_heads(ref, val):
    pad = jnp.zeros((GDN_CHUNK, SLOT - GDN_D), F32)
    for j in range(GDN_GROUP):
        ref[:, pl.ds(j * SLOT, GDN_D)] = val[j]
        ref[:, pl.ds(j * SLOT + GDN_D, SLOT - GDN_D)] = pad


def _gdn_fwd(qkv, gb, bb, carry=None):
    t = qkv.shape[0]
    n_chunks = t // GDN_CHUNK
    d = GDN_D

    def body(q_ref, k_ref, v_ref, g_ref, b_ref, o_ref, keep_ref, s_ref):
        @pl.when(pl.program_id(1) == 0)
        def _():
            s_ref[...] = jnp.zeros_like(s_ref)

        s = s_ref[...]
        keep_ref[:, 0, 0] = s
        o, s_new, tinv = _gdn_chunk(*[_group_heads(r) for r in (q_ref, k_ref, v_ref, g_ref, b_ref)], s, _chunk_masks())
        keep_ref[:, 0, 1] = tinv
        s_ref[...] = s_new
        _ungroup_heads(o_ref, o)

    def spec(kind=0):
        return pl.BlockSpec((GDN_CHUNK, GDN_GROUP * SLOT), lambda h, n: (n, kind * GDN_GROUPS + h))

    return _call_carrying(
        body, carry, (qkv, qkv, qkv, gb, bb), name="gdn_fwd",
        grid=(GDN_GROUPS, n_chunks),
        in_specs=[spec(0), spec(1), spec(2), spec(), spec()],
        out_specs=[spec(), pl.BlockSpec((GDN_GROUP, 1, 2, d, d), lambda h, n: (h, n, 0, 0, 0))],
        out_shape=[jax.ShapeDtypeStruct((t, N_HEADS * SLOT), F32), jax.ShapeDtypeStruct((N_HEADS, n_chunks, 2, d, d), F32)],
        scratch_shapes=[pltpu.VMEM((GDN_GROUP, d, d), F32)],
        compiler_params=pltpu.CompilerParams(dimension_semantics=("arbitrary", "arbitrary")),
    )


def _gdn_bwd(qkv, gb, bb, keep, do, carry=None):
    t = qkv.shape[0]
    n_chunks = t // GDN_CHUNK
    d = GDN_D

    def body(q_ref, k_ref, v_ref, g_ref, b_ref, keep_ref, do_ref, dqkv_ref, dg_ref, db_ref, ds_ref):
        @pl.when(pl.program_id(1) == 0)
        def _():
            ds_ref[...] = jnp.zeros_like(ds_ref)

        masks = _chunk_masks()
        tinv = keep_ref[:, 0, 1]
        _, pull = jax.vjp(lambda *a: _gdn_chunk(*a, masks, tinv)[:2],
                          *[_group_heads(r) for r in (q_ref, k_ref, v_ref, g_ref, b_ref)], keep_ref[:, 0, 0])
        dq, dk, dv, dg, db, ds = pull((_group_heads(do_ref), ds_ref[...]))
        ds_ref[...] = ds
        for i, val in enumerate((dq, dk, dv)):
            _ungroup_heads(dqkv_ref.at[i], val)
        _ungroup_heads(dg_ref, dg)
        _ungroup_heads(db_ref, db)

    def spec(kind=0):
        return pl.BlockSpec((GDN_CHUNK, GDN_GROUP * SLOT), lambda h, n: (n_chunks - 1 - n, kind * GDN_GROUPS + h))

    return _call_carrying(
        body, carry, (qkv, qkv, qkv, gb, bb, keep, do), name="gdn_bwd",
        grid=(GDN_GROUPS, n_chunks),
        in_specs=[spec(0), spec(1), spec(2), spec(), spec(),
                  pl.BlockSpec((GDN_GROUP, 1, 2, d, d), lambda h, n: (h, n_chunks - 1 - n, 0, 0, 0)), spec()],
        out_specs=[pl.BlockSpec((3, GDN_CHUNK, GDN_GROUP * SLOT), lambda h, n: (0, n_chunks - 1 - n, h)), spec(), spec()],
        out_shape=[jax.ShapeDtypeStruct((3, t, N_HEADS * SLOT), F32)] + [jax.ShapeDtypeStruct((t, N_HEADS * SLOT), F32)] * 2,
        scratch_shapes=[pltpu.VMEM((GDN_GROUP, d, d), F32)],
        compiler_params=pltpu.CompilerParams(dimension_semantics=("arbitrary", "arbitrary")),
    )


def _rowwise(name, fn, rows, consts, outs, sums=(), tm=256):
    rows = [x if isinstance(x, tuple) else (x, x.shape[1], 0) for x in rows]
    t = rows[0][0].shape[0]
    tm = min(tm, t)
    steps = t // tm
    n_r, n_c, n_o, n_s = len(rows), len(consts), len(outs), len(sums)

    def window(width, block):
        return pl.BlockSpec((tm, width), lambda i: (i, block))

    def body(*refs):
        r, c = refs[:n_r], refs[n_r:n_r + n_c]
        o, s = refs[n_r + n_c:n_r + n_c + n_o], refs[n_r + n_c + n_o:]
        vals, tot = fn([x[...] for x in r], [x[...] for x in c])
        for ref, val in zip(o, vals):
            ref[...] = val.astype(ref.dtype)
        if n_s:
            @pl.when(pl.program_id(0) == 0)
            def _():
                for ref in s:
                    ref[...] = jnp.zeros_like(ref)

            for ref, val in zip(s, tot):
                ref[...] += val

    return pl.pallas_call(
        body, name=name,
        grid=(steps,),
        in_specs=[window(w, b) for _, w, b in rows] + [pl.BlockSpec(x.shape, lambda i: (0, 0)) for x in consts],
        out_specs=[pl.BlockSpec((tm, w), lambda i: (i, 0)) for w, _ in outs]
        + [pl.BlockSpec((1, w), lambda i: (0, 0)) for w in sums],
        out_shape=[jax.ShapeDtypeStruct((t, w), dt) for w, dt in outs]
        + [jax.ShapeDtypeStruct((1, w), F32) for w in sums],
        compiler_params=pltpu.CompilerParams(dimension_semantics=("arbitrary",)),
    )(*[x for x, _, _ in rows], *consts)


def _tile(dim, target):
    if dim <= target:
        return dim
    best = None
    for cand in range(128, target + 1, 128):
        if dim % cand == 0:
            best = cand
    assert best is not None, (dim, target)
    return best


def _matmul(name, a, b, mode, out_dtype=F32, tm=512, tn=1024, tk=2048, carry=None):
    if mode == "nn":
        (m, k), n = a.shape, b.shape[1]
    elif mode == "nt":
        (m, k), n = a.shape, b.shape[0]
    else:
        (k, m), n = a.shape, b.shape[1]
    tm, tn, tk = _tile(m, tm), _tile(n, tn), _tile(k, tk)
    k_steps = k // tk
    product = {"nn": _nn, "nt": _nt, "tn": _tn}[mode]

    def body(a_ref, b_ref, o_ref, acc_ref):
        part = product(a_ref[...].astype(BF16), b_ref[...].astype(BF16))
        if k_steps == 1:
            o_ref[...] = part.astype(o_ref.dtype)
        else:
            kk = pl.program_id(2)

            @pl.when(kk == 0)
            def _():
                acc_ref[...] = part

            @pl.when(kk > 0)
            def _():
                acc_ref[...] += part

            @pl.when(kk == k_steps - 1)
            def _():
                o_ref[...] = acc_ref[...].astype(o_ref.dtype)

    a_spec = pl.BlockSpec((tk, tm), lambda i, j, kk: (kk, i)) if mode == "tn" else pl.BlockSpec((tm, tk), lambda i, j, kk: (i, kk))
    b_spec = pl.BlockSpec((tn, tk), lambda i, j, kk: (j, kk)) if mode == "nt" else pl.BlockSpec((tk, tn), lambda i, j, kk: (kk, j))
    (out,), carried = _call_carrying(
        body, carry, (a, b), name=name,
        grid=(m // tm, n // tn, k_steps),
        in_specs=[a_spec, b_spec],
        out_specs=[pl.BlockSpec((tm, tn), lambda i, j, kk: (i, j))],
        out_shape=[jax.ShapeDtypeStruct((m, n), out_dtype)],
        scratch_shapes=[pltpu.VMEM((tm, tn) if k_steps > 1 else (8, 128), F32)],
        compiler_params=pltpu.CompilerParams(dimension_semantics=("arbitrary", "arbitrary", "arbitrary")),
    )
    return out if carry is None else (out, carried)


FFN_TM = 512
FFN_BWD_TM = 256
FFN_BLOCKS = 4
FFN_GATE, FFN_UP, FFN_DOWN = 0, 1, 2


def _ffn_weight_specs(ffn_w, first):
    _, _, rows, dm = ffn_w.shape

    def spec(k):
        return pl.BlockSpec((FFN_BLOCKS, None, rows, dm), lambda i, j: (j, first + k, 0, 0))

    return [spec(FFN_GATE), spec(FFN_UP), spec(FFN_DOWN)], FFN_BLOCKS * rows


def _ffn_fwd(name, x, g_pre, ffn_w, first, g_post, carry=None):
    t, dm = x.shape
    tm = min(FFN_TM, t)
    w_specs, tf = _ffn_weight_specs(ffn_w, first)
    f_steps = N_DEV // FFN_BLOCKS

    def body(x_ref, gpre_ref, wg_ref, wu_ref, wd_ref, gpost_ref, h_ref, y_ref, xn_ref, acc_ref):
        j = pl.program_id(1)

        @pl.when(j == 0)
        def _():
            xn_ref[...] = _rms(x_ref[...], gpre_ref[...], dm).astype(BF16)
            acc_ref[...] = jnp.zeros_like(acc_ref)

        xn = xn_ref[...]
        wg, wu, wd = (r[...].reshape(tf, dm) for r in (wg_ref, wu_ref, wd_ref))
        a = _silu(_nt(xn, wg)) * _nt(xn, wu)
        acc_ref[...] += _nn(a.astype(BF16), wd)

        @pl.when(j == f_steps - 1)
        def _():
            h = acc_ref[...]
            h_ref[...] = h
            y_ref[...] = x_ref[...] + 0.5 * _rms(h, gpost_ref[...], dm)

    row = pl.BlockSpec((tm, dm), lambda i, j: (i, 0))
    vec = pl.BlockSpec((1, dm), lambda i, j: (0, 0))
    return _call_carrying(
        body, carry, (x, g_pre, ffn_w, ffn_w, ffn_w, g_post), name=name,
        grid=(t // tm, f_steps),
        in_specs=[row, vec, *w_specs, vec],
        out_specs=[row, row],
        out_shape=[jax.ShapeDtypeStruct((t, dm), F32)] * 2,
        scratch_shapes=[pltpu.VMEM((tm, dm), BF16), pltpu.VMEM((tm, dm), F32)],
        compiler_params=pltpu.CompilerParams(dimension_semantics=("arbitrary", "arbitrary")),
    )


def _ffn_bwd(name, x, h, dy, g_pre, ffn_w, first, g_post, carry=None):
    t, dm = x.shape
    tm = min(FFN_BWD_TM, t)
    w_specs, tf = _ffn_weight_specs(ffn_w, first)
    f_steps = N_DEV // FFN_BLOCKS
    f = f_steps * tf

    def post(hv, g):
        return 0.5 * _rms(hv, g, dm)

    def pre(xv, g):
        return _rms(xv, g, dm)

    def body(x_ref, h_ref, dy_ref, gpre_ref, wg_ref, wu_ref, wd_ref, gpost_ref,
             dx_ref, xn_ref, dh_ref, a_ref, dhg_ref, dhu_ref, dgpre_ref, dgpost_ref, acc_ref):
        i, j = pl.program_id(0), pl.program_id(1)

        @pl.when((i == 0) & (j == 0))
        def _():
            dgpre_ref[...] = jnp.zeros_like(dgpre_ref)
            dgpost_ref[...] = jnp.zeros_like(dgpost_ref)

        @pl.when(j == 0)
        def _():
            xn_ref[...] = pre(x_ref[...], gpre_ref[...]).astype(BF16)
            _, pull = jax.vjp(post, h_ref[...], gpost_ref[...])
            dh, dg = pull(dy_ref[...])
            dh_ref[...] = dh.astype(BF16)
            dgpost_ref[...] += dg
            acc_ref[...] = jnp.zeros_like(acc_ref)

        xn = xn_ref[...]
        wg, wu, wd = (r[...].reshape(tf, dm) for r in (wg_ref, wu_ref, wd_ref))
        hg = _nt(xn, wg)
        hu = _nt(xn, wu)
        da = _nt(dh_ref[...], wd)
        sig = _sigmoid(hg)
        act = hg * sig
        dhu = (da * act).astype(BF16)
        dhg = (da * hu * (sig * (1.0 + hg * (1.0 - sig)))).astype(BF16)
        a_ref[...] = (act * hu).astype(BF16)
        dhg_ref[...] = dhg
        dhu_ref[...] = dhu
        acc_ref[...] += _nn(dhg, wg) + _nn(dhu, wu)

        @pl.when(j == f_steps - 1)
        def _():
            _, pull = jax.vjp(pre, x_ref[...], gpre_ref[...])
            dx, dg = pull(acc_ref[...])
            dx_ref[...] = dy_ref[...] + dx
            dgpre_ref[...] += dg

    row = pl.BlockSpec((tm, dm), lambda i, j: (i, 0))
    vec = pl.BlockSpec((1, dm), lambda i, j: (0, 0))
    wide = pl.BlockSpec((tm, tf), lambda i, j: (i, j))
    return _call_carrying(
        body, carry, (x, h, dy, g_pre, ffn_w, ffn_w, ffn_w, g_post), name=name,
        grid=(t // tm, f_steps),
        in_specs=[row, row, row, vec, *w_specs, vec],
        out_specs=[row, row, row, wide, wide, wide, vec, vec],
        out_shape=[jax.ShapeDtypeStruct((t, dm), F32), jax.ShapeDtypeStruct((t, dm), BF16), jax.ShapeDtypeStruct((t, dm), BF16),
                   jax.ShapeDtypeStruct((t, f), BF16), jax.ShapeDtypeStruct((t, f), BF16), jax.ShapeDtypeStruct((t, f), BF16),
                   jax.ShapeDtypeStruct((1, dm), F32), jax.ShapeDtypeStruct((1, dm), F32)],
        scratch_shapes=[pltpu.VMEM((tm, dm), F32)],
        compiler_params=pltpu.CompilerParams(dimension_semantics=("arbitrary", "arbitrary")),
    )


ATT_T = 512
ATT_GROUP = 2
ATT_SCALE = (MLA_NOPE + MLA_ROPE) ** -0.5


def _stack_slots(ref, group):
    return jnp.stack([ref[:, pl.ds(j * SLOT, SLOT)] for j in range(group)])


def _unstack_slots(ref, val):
    for j in range(val.shape[0]):
        ref[:, pl.ds(j * SLOT, SLOT)] = val[j].astype(ref.dtype)


def _scores(q, k, diagonal):
    s = _nt(q, k) * ATT_SCALE
    if diagonal:
        row = lax.broadcasted_iota(jnp.int32, s.shape[1:], 0)
        col = lax.broadcasted_iota(jnp.int32, s.shape[1:], 1)
        s = jnp.where(col <= row, s, -1e30)
    return s


def _attn_specs(tile, q_major):
    width = ATT_GROUP * SLOT
    if q_major:
        return (pl.BlockSpec((tile, width), lambda h, qi, ki: (qi, h)),
                pl.BlockSpec((tile, width), lambda h, qi, ki: (jnp.minimum(ki, qi), h)))
    return (pl.BlockSpec((tile, width), lambda h, ki, qi: (jnp.maximum(qi, ki), h)),
            pl.BlockSpec((tile, width), lambda h, ki, qi: (ki, h)))


def _attn_fwd(q, k, v):
    t = q.shape[0]
    tile = min(ATT_T, t)
    steps = t // tile
    g = ATT_GROUP

    strip = min(SLOT, tile)

    def body(q_ref, k_ref, v_ref, o_ref, lse_ref, m_ref, l_ref, alpha_ref, acc_ref, s_ref, p_ref):
        qi, ki = pl.program_id(1), pl.program_id(2)

        @pl.when(ki == 0)
        def _():
            m_ref[...] = jnp.full_like(m_ref, -1e30)
            l_ref[...] = jnp.zeros_like(l_ref)
            acc_ref[...] = jnp.zeros_like(acc_ref)

        def step(diagonal):
            s_ref[...] = _nt(_stack_slots(k_ref, g), _stack_slots(q_ref, g))
            for j in range(tile // strip):
                c = pl.ds(j * strip, strip)
                s = s_ref[:, :, c] * ATT_SCALE
                if diagonal:
                    key = lax.broadcasted_iota(jnp.int32, s.shape[1:], 0)
                    query = lax.broadcasted_iota(jnp.int32, s.shape[1:], 1) + j * strip
                    s = jnp.where(key <= query, s, -1e30)
                m_old = m_ref[:, :, c]
                m_new = jnp.maximum(m_old, jnp.max(s, axis=1, keepdims=True))
                p = jnp.exp(s - m_new)
                alpha = jnp.exp(m_old - m_new)
                l_ref[:, :, c] = alpha * l_ref[:, :, c] + jnp.sum(p, axis=1, keepdims=True)
                alpha_ref[:, :, c] = alpha
                m_ref[:, :, c] = m_new
                p_ref[:, :, c] = p.astype(BF16)
            acc_ref[...] = acc_ref[...] * alpha_ref[...] + _tn(_stack_slots(v_ref, g), p_ref[...])

        @pl.when(ki < qi)
        def _():
            step(False)

        @pl.when(ki == qi)
        def _():
            step(True)
            out = acc_ref[...] / l_ref[...]
            lse = jnp.broadcast_to(m_ref[...] + jnp.log(l_ref[...]), out.shape)
            for j in range(g):
                o_ref[:, pl.ds(j * SLOT, SLOT)] = out[j].T
                lse_ref[:, pl.ds(j * SLOT, SLOT)] = lse[j].T

    q_spec, k_spec = _attn_specs(tile, True)
    return pl.pallas_call(
        body, name="attn_fwd",
        grid=(N_HEADS // g, steps, steps),
        in_specs=[q_spec, k_spec, k_spec],
        out_specs=[q_spec, q_spec],
        out_shape=[jax.ShapeDtypeStruct((t, N_HEADS * SLOT), F32)] * 2,
        scratch_shapes=[pltpu.VMEM((g, 1, tile), F32), pltpu.VMEM((g, 1, tile), F32), pltpu.VMEM((g, 1, tile), F32),
                        pltpu.VMEM((g, SLOT, tile), F32), pltpu.VMEM((g, tile, tile), F32), pltpu.VMEM((g, tile, tile), BF16)],
        compiler_params=pltpu.CompilerParams(dimension_semantics=("parallel", "parallel", "arbitrary")),
    )(q, k, v)


def _attn_grad_scores(q, k, v, do, lse_ref, delta_ref, diagonal):
    g = ATT_GROUP
    p = jnp.exp(_scores(q, k, diagonal) - _stack_slots(lse_ref, g)[:, :, 0:1])
    dp = _nt(do, v)
    return p, p * (dp - _stack_slots(delta_ref, g)[:, :, 0:1]) * ATT_SCALE


def _attn_bwd(q, k, v, do, lse, delta):
    t = q.shape[0]
    tile = min(ATT_T, t)
    steps = t // tile
    g = ATT_GROUP

    def body(q_ref, k_ref, v_ref, do_ref, lse_ref, delta_ref, dq_ref, dk_ref, dv_ref, dk_acc, dv_acc):
        ki, qi = pl.program_id(1), pl.program_id(2)

        @pl.when((ki == 0) & (qi == 0))
        def _():
            dq_ref[...] = jnp.zeros_like(dq_ref)

        @pl.when(qi == 0)
        def _():
            dk_acc[...] = jnp.zeros_like(dk_acc)
            dv_acc[...] = jnp.zeros_like(dv_acc)

        def step(diagonal):
            qq, kk = _stack_slots(q_ref, g), _stack_slots(k_ref, g)
            do_b = _stack_slots(do_ref, g).astype(BF16)
            p, ds = _attn_grad_scores(qq, kk, _stack_slots(v_ref, g), do_b, lse_ref, delta_ref, diagonal)
            ds = ds.astype(BF16)
            dv_acc[...] += _tn(p.astype(BF16), do_b)
            dk_acc[...] += _tn(ds, qq)
            dq = _nn(ds, kk)
            rows = pl.ds(pl.multiple_of(qi * tile, tile), tile)
            for j in range(g):
                dq_ref[rows, pl.ds(j * SLOT, SLOT)] += dq[j]

        @pl.when(qi > ki)
        def _():
            step(False)

        @pl.when(qi == ki)
        def _():
            step(True)

        @pl.when(qi == steps - 1)
        def _():
            _unstack_slots(dk_ref, dk_acc[...])
            _unstack_slots(dv_ref, dv_acc[...])

    q_spec, k_spec = _attn_specs(tile, False)
    return pl.pallas_call(
        body, name="attn_bwd",
        grid=(N_HEADS // g, steps, steps),
        in_specs=[q_spec, k_spec, k_spec, q_spec, q_spec, q_spec],
        out_specs=[pl.BlockSpec((t, g * SLOT), lambda h, ki, qi: (0, h)), k_spec, k_spec],
        out_shape=[jax.ShapeDtypeStruct((t, N_HEADS * SLOT), F32)] * 3,
        scratch_shapes=[pltpu.VMEM((g, tile, SLOT), F32), pltpu.VMEM((g, tile, SLOT), F32)],
        compiler_params=pltpu.CompilerParams(dimension_semantics=("parallel", "arbitrary", "arbitrary")),
    )(q, k, v, do, lse, delta)


def _shift_down(x, s):
    if s == 0:
        return x
    row = lax.broadcasted_iota(jnp.int32, x.shape, 0)
    return jnp.where(row >= s, pltpu.roll(x, s, 0), 0.0)


def _shift_up(x, s):
    if s == 0:
        return x
    n = x.shape[0]
    row = lax.broadcasted_iota(jnp.int32, x.shape, 0)
    return jnp.where(row < n - s, pltpu.roll(x, n - s, 0), 0.0)


def _l2norm(x):
    return x * lax.rsqrt(jnp.sum(x * x, axis=-1, keepdims=True) + EPS)


def _conv_pre(x, w):
    y = w[GDN_CONV - 1:GDN_CONV, :] * x
    for s in range(1, GDN_CONV):
        y = y + w[GDN_CONV - 1 - s:GDN_CONV - s, :] * _shift_down(x, s)
    return y


def _gdn_conv_fwd(x, w):
    t, width = x.shape

    def body(x_ref, w_ref, o_ref):
        act = _silu(_conv_pre(x_ref[...], w_ref[...]))
        normed = pl.program_id(0) < 2 * N_HEADS
        o_ref[...] = jnp.where(normed, _l2norm(act), act)

    return pl.pallas_call(
        body, name="gdn_conv_fwd",
        grid=(width // SLOT,),
        in_specs=[pl.BlockSpec((t, SLOT), lambda j: (0, j)), pl.BlockSpec((GDN_CONV, SLOT), lambda j: (0, j))],
        out_specs=pl.BlockSpec((t, SLOT), lambda j: (0, j)),
        out_shape=jax.ShapeDtypeStruct((t, width), F32),
        compiler_params=pltpu.CompilerParams(dimension_semantics=("parallel",)),
    )(x, w)


def _gdn_conv_bwd(x, w, dout):
    t, width = x.shape

    def body(x_ref, w_ref, do_ref, dx_ref, dw_ref):
        xv, wv = x_ref[...], w_ref[...]
        y = _conv_pre(xv, wv)
        sig = _sigmoid(y)
        act = y * sig
        _, pull = jax.vjp(_l2norm, act)
        normed = pl.program_id(0) < 2 * N_HEADS
        dact = jnp.where(normed, pull(do_ref[0])[0], do_ref[0])
        dy = dact * (sig * (1.0 + y * (1.0 - sig)))
        dx = wv[GDN_CONV - 1:GDN_CONV, :] * dy
        for s in range(1, GDN_CONV):
            dx = dx + wv[GDN_CONV - 1 - s:GDN_CONV - s, :] * _shift_up(dy, s)
        dx_ref[...] = dx.astype(BF16)
        for s in range(GDN_CONV):
            dw_ref[GDN_CONV - 1 - s:GDN_CONV - s, :] = jnp.sum(dy * _shift_down(xv, s), axis=0, keepdims=True)

    col = pl.BlockSpec((t, SLOT), lambda j: (0, j))
    tap = pl.BlockSpec((GDN_CONV, SLOT), lambda j: (0, j))
    return pl.pallas_call(
        body, name="gdn_conv_bwd",
        grid=(width // SLOT,),
        in_specs=[col, tap, pl.BlockSpec((1, t, SLOT), lambda j: (j // N_HEADS, 0, j % N_HEADS))],
        out_specs=[col, tap],
        out_shape=[jax.ShapeDtypeStruct((t, width), BF16), jax.ShapeDtypeStruct((GDN_CONV, width), F32)],
        compiler_params=pltpu.CompilerParams(dimension_semantics=("parallel",)),
    )(x, w, dout)


def _softplus(x):
    e = jnp.exp(-jnp.abs(x))
    u = 1.0 + e
    log1p = jnp.where(u == 1.0, e, jnp.log(u) * e / jnp.where(u == 1.0, 1.0, u - 1.0))
    return jnp.maximum(x, 0.0) + log1p


def _gates_fwd(ab, a_log, dt_bias):
    def fn(rows, consts):
        (abv,), (alog, dtb) = rows, consts
        g = -jnp.exp(alog) * _softplus(abv + dtb)
        beta = _sigmoid(abv)
        shape = (abv.shape[0], SLOT)
        g_slots = [jnp.broadcast_to(g[:, h:h + 1], shape) for h in range(N_HEADS)]
        b_slots = [jnp.broadcast_to(beta[:, N_HEADS + h:N_HEADS + h + 1], shape) for h in range(N_HEADS)]
        return [jnp.concatenate(g_slots, axis=1), jnp.concatenate(b_slots, axis=1)], []

    width = N_HEADS * SLOT
    return _rowwise("gdn_gates_fwd", fn, [ab], [a_log, dt_bias], [(width, F32), (width, F32)])


def _gates_bwd(ab, a_log, dt_bias, dg, dbeta):
    def fn(rows, consts):
        (abv, dgv, dbv), (alog, dtb) = rows, consts
        lane = lax.broadcasted_iota(jnp.int32, abv.shape, 1)
        dg_tok = jnp.zeros_like(abv)
        db_tok = jnp.zeros_like(abv)
        for h in range(N_HEADS):
            dg_tok = dg_tok + jnp.where(lane == h, jnp.sum(dgv[:, h * SLOT:(h + 1) * SLOT], axis=1, keepdims=True), 0.0)
            db_tok = db_tok + jnp.where(lane == N_HEADS + h, jnp.sum(dbv[:, h * SLOT:(h + 1) * SLOT], axis=1, keepdims=True), 0.0)
        xa = abv + dtb
        g = -jnp.exp(alog) * _softplus(xa)
        da = dg_tok * (-jnp.exp(alog)) * _sigmoid(xa)
        beta = _sigmoid(abv)
        dab = jnp.where(lane < N_HEADS, da, db_tok * beta * (1.0 - beta))
        dab = jnp.where(lane < 2 * N_HEADS, dab, 0.0)
        d_alog = jnp.sum(jnp.where(lane < N_HEADS, dg_tok * g, 0.0), axis=0, keepdims=True)
        d_dtb = jnp.sum(jnp.where(lane < N_HEADS, da, 0.0), axis=0, keepdims=True)
        return [dab], [d_alog, d_dtb]

    return _rowwise("gdn_gates_bwd", fn, [ab, dg, dbeta], [a_log, dt_bias], [(SLOT, F32)], sums=[SLOT, SLOT])


ROPE_HALF = MLA_ROPE // 2


def _rope_tables(positions):
    freqs = ROPE_THETA ** (-jnp.arange(ROPE_HALF, dtype=F32) / ROPE_HALF)
    ang = positions.astype(F32)[:, None] * freqs
    cos, sin = jnp.cos(ang), jnp.sin(ang)
    t = positions.shape[0]
    ones, zeros = jnp.ones((t, MLA_NOPE), F32), jnp.zeros((t, MLA_NOPE), F32)
    tail = jnp.zeros((t, SLOT - MLA_NOPE - MLA_ROPE), F32)
    half0 = jnp.zeros((t, ROPE_HALF), F32)
    same = jnp.concatenate([ones, cos, cos, tail], axis=1)
    from_low = jnp.concatenate([zeros, half0, sin, tail], axis=1)
    from_high = jnp.concatenate([zeros, -sin, half0, tail], axis=1)
    return same, from_low, from_high


def _rope(x, tabs):
    same, from_low, from_high = tabs
    width = x.shape[1]
    return x * same + pltpu.roll(x, ROPE_HALF, 1) * from_low + pltpu.roll(x, width - ROPE_HALF, 1) * from_high


def _rope_transposed(dy, tabs):
    same, from_low, from_high = tabs
    width = dy.shape[1]
    return dy * same + pltpu.roll(dy * from_low, width - ROPE_HALF, 1) + pltpu.roll(dy * from_high, ROPE_HALF, 1)


def _tile_slots(tab):
    return jnp.concatenate([tab] * N_HEADS, axis=1)


A_WIDTH = MLA_Q_RANK + MLA_KV_RANK + 2 * SLOT
A_KPE = MLA_Q_RANK + MLA_KV_RANK
A_AB = A_KPE + SLOT
WIDE = N_HEADS * SLOT


def _mla_pre_fwd(proj_a, tabs, g_q, g_kv):
    def fn(rows, consts):
        pa, *tb = rows
        gq, gkv = consts
        return [_rms(pa[:, :MLA_Q_RANK], gq, MLA_Q_RANK), _rms(pa[:, MLA_Q_RANK:A_KPE], gkv, MLA_KV_RANK),
                _rope(pa[:, A_KPE:A_AB], tb)], []

    return _rowwise("mla_pre_fwd", fn, [proj_a, *tabs], [g_q, g_kv], [(MLA_Q_RANK, BF16), (MLA_KV_RANK, BF16), (SLOT, F32)])


def _mla_pre_bwd(proj_a, tabs, g_q, g_kv, dcqn, dckvn, dkpe, dab):
    def fn(rows, consts):
        pa, t0, t1, t2, dq, dkv, dk, da = rows
        gq, gkv = consts
        _, pull_q = jax.vjp(lambda x, g: _rms(x, g, MLA_Q_RANK), pa[:, :MLA_Q_RANK], gq)
        _, pull_kv = jax.vjp(lambda x, g: _rms(x, g, MLA_KV_RANK), pa[:, MLA_Q_RANK:A_KPE], gkv)
        dcq, dgq = pull_q(dq)
        dckv, dgkv = pull_kv(dkv)
        return [jnp.concatenate([dcq, dckv, _rope_transposed(dk, (t0, t1, t2)), da], axis=1)], [dgq, dgkv]

    return _rowwise("mla_pre_bwd", fn, [proj_a, *tabs, dcqn, dckvn, dkpe, dab], [g_q, g_kv], [(A_WIDTH, BF16)],
                    sums=[MLA_Q_RANK, MLA_KV_RANK])


def _mla_qkv_fwd(q_p, kv_p, kpe, tabs):
    def fn(rows, consts):
        qv, kvv, kp, *tb = rows
        q = _rope(qv, [_tile_slots(x) for x in tb])
        k = kvv[:, :WIDE] + _tile_slots(kp)
        return [q, k, kvv[:, WIDE:]], []

    return _rowwise("mla_qkv_fwd", fn, [q_p, kv_p, kpe, *tabs], [], [(WIDE, BF16)] * 3)


def _mla_qkv_bwd(dq, dk, dv, tabs):
    def fn(rows, consts):
        dqv, dkv, dvv, *tb = rows
        dkpe = dkv[:, :SLOT]
        for h in range(1, N_HEADS):
            dkpe = dkpe + dkv[:, h * SLOT:(h + 1) * SLOT]
        return [_rope_transposed(dqv, [_tile_slots(x) for x in tb]), jnp.concatenate([dkv, dvv], axis=1), dkpe], []

    return _rowwise("mla_qkv_bwd", fn, [dq, dk, dv, *tabs], [], [(WIDE, BF16), (2 * WIDE, BF16), (SLOT, F32)])


def _slot_sum(x):
    parts = [jnp.broadcast_to(jnp.sum(x[:, h * SLOT:(h + 1) * SLOT], axis=1, keepdims=True), (x.shape[0], SLOT))
             for h in range(N_HEADS)]
    return jnp.concatenate(parts, axis=1)


def _mix_join(o_mla, o_gdn, gate, g_mla, g_gdn):
    mla = _rms(o_mla, g_mla, N_HEADS * MLA_V)
    gdn = o_gdn * lax.rsqrt(_slot_sum(o_gdn * o_gdn) * (1.0 / GDN_D) + EPS) * g_gdn * _silu(gate)
    return mla, gdn


def _mix_join_fwd(o_mla, o_gdn, gate, g_mla, g_gdn):
    def fn(rows, consts):
        return [jnp.concatenate(_mix_join(*rows, *consts), axis=1)], []

    return _rowwise("mix_join_fwd", fn, [o_mla, o_gdn, gate], [g_mla, g_gdn], [(2 * WIDE, BF16)])


def _mix_join_bwd(o_mla, o_gdn, gate, g_mla, g_gdn, dcat):
    def fn(rows, consts):
        om, og, gt, dc = rows
        gm, gg = consts
        _, pull = jax.vjp(lambda x, g: _rms(x, g, N_HEADS * MLA_V), om, gm)
        dom, dgm = pull(dc[:, :WIDE])
        dy = dc[:, WIDE:]
        r = lax.rsqrt(_slot_sum(og * og) * (1.0 / GDN_D) + EPS)
        sig = _sigmoid(gt)
        normed = og * r
        dn = dy * gg * (gt * sig)
        dog = r * dn - normed * (r * r) * _slot_sum(dn * og) * (1.0 / GDN_D)
        dgt = dy * normed * gg * (sig * (1.0 + gt * (1.0 - sig)))
        dgg = jnp.sum(dy * normed * (gt * sig), axis=0, keepdims=True)
        return [dom, _slot_sum(dom * om), dog, dgt], [dgm, dgg]

    return _rowwise("mix_join_bwd", fn, [o_mla, o_gdn, gate, dcat], [g_mla, g_gdn],
                    [(WIDE, F32), (WIDE, F32), (WIDE, F32), (WIDE, BF16)], sums=[WIDE, WIDE])


def _norm_residual_fwd(name, x, h, g, out_dtypes):
    dm = x.shape[1]

    def fn(rows, consts):
        y = rows[0] + _rms(rows[1], consts[0], dm)
        return [y] + [_rms(y, gg, dm) for gg in consts[1:]], []

    return _rowwise(name, fn, [x, h], list(g), [(dm, dt) for dt in out_dtypes])


def _norm_residual_bwd(name, h, g, dy):
    dm = h.shape[1]

    def fn(rows, consts):
        _, pull = jax.vjp(lambda hv, gv: _rms(hv, gv, dm), rows[0], consts[0])
        dh, dg = pull(rows[1])
        return [dh], [dg]

    return _rowwise(name, fn, [h, dy], [g], [(dm, BF16)], sums=[dm])


def _norm_bwd_add(name, x, g, dns, dy):
    dm = x.shape[1]

    def fn(rows, consts):
        xv, dyv, *parts = rows
        dn = parts[0]
        for p in parts[1:]:
            dn = dn + p
        _, pull = jax.vjp(lambda a, gv: _rms(a, gv, dm), xv, consts[0])
        dx, dg = pull(dn)
        return [dyv + dx], [dg]

    return _rowwise(name, fn, [x, dy, *dns], [g], [(dm, F32)], sums=[dm])


def _loss_fwd(y, target):
    dm = y.shape[1]

    def fn(rows, consts):
        err = rows[0] - rows[1]
        sq = err * err
        lanes = sq[:, :SLOT]
        for j in range(1, dm // SLOT):
            lanes = lanes + sq[:, j * SLOT:(j + 1) * SLOT]
        return [err * (1.0 / dm)], [jnp.sum(lanes, axis=0, keepdims=True) * (0.5 / dm)]

    return _rowwise("loss", fn, [y, target], [], [(dm, F32)], sums=[SLOT])


def _norm_fwd(name, x, g):
    dm = x.shape[1]
    return _rowwise(name, lambda rows, consts: ([_rms(rows[0], consts[0], dm)], []), [x], [g], [(dm, BF16)])[0]


W_IN_CUTS = (0, 256, 384, 416, 1952, 1960, 1968, 2480)


def _heads_out(w, per_head, axis=-1):
    axis = axis % w.ndim
    shape = w.shape
    n = shape[axis] // per_head
    w = w.reshape(shape[:axis] + (n, per_head) + shape[axis + 1:])
    pad = [(0, 0)] * w.ndim
    pad[axis + 1] = (0, SLOT - per_head)
    return jnp.pad(w, pad).reshape(shape[:axis] + (n * SLOT,) + shape[axis + 1:])


def _heads_in(w, per_head, axis=-1):
    axis = axis % w.ndim
    shape = w.shape
    n = shape[axis] // SLOT
    w = w.reshape(shape[:axis] + (n, SLOT) + shape[axis + 1:])
    w = lax.slice_in_dim(w, 0, per_head, axis=axis + 1)
    return w.reshape(shape[:axis] + (n * per_head,) + shape[axis + 1:])


def _pad_lanes(v, lo, width=SLOT):
    return jnp.pad(v, [(0, 0)] * (v.ndim - 1) + [(lo, width - lo - v.shape[-1])])


def _pad_rows(v, lo, rows=SLOT):
    return jnp.pad(v, [(lo, rows - lo - v.shape[0])] + [(0, 0)] * (v.ndim - 1))


def _layout_weights(w):
    c = W_IN_CUTS
    w_in = w["w_in_t"]
    p = {}
    p["w_a"] = jnp.concatenate([w_in[c[0]:c[2]], _pad_rows(w_in[c[2]:c[3]], MLA_NOPE), _pad_rows(w_in[c[4]:c[6]], 0)], axis=0)
    p["w_qkv"] = _heads_out(w_in[c[3]:c[4]], GDN_D, axis=0)
    p["w_gate"] = _heads_out(w_in[c[6]:c[7]], GDN_D, axis=0)
    p["w_uq"] = _heads_out(w["uq_t"], MLA_NOPE + MLA_ROPE, axis=0)
    ukv = w["ukv_t"].reshape(N_HEADS, MLA_NOPE + MLA_V, MLA_KV_RANK)
    p["w_kv"] = jnp.concatenate([_heads_out(ukv[:, :MLA_NOPE].reshape(-1, MLA_KV_RANK), MLA_NOPE, axis=0),
                                 _heads_out(ukv[:, MLA_NOPE:].reshape(-1, MLA_KV_RANK), MLA_V, axis=0)], axis=0)
    p["conv"] = _heads_out(w["gdn_conv_w"], GDN_D)
    p["g_mla_out"] = _heads_out(w["mla_out_g"], MLA_V)
    p["g_gdn"] = jnp.tile(_pad_lanes(w["gdn_norm_g"], 0), (1, N_HEADS))
    p["a_log"] = _pad_lanes(w["gdn_a_log"], 0)
    p["dt_bias"] = _pad_lanes(w["gdn_dt_bias"], 0)
    return p


def _unlayout_grads(d):
    c = W_IN_CUTS
    g = {}
    da = d["w_a"]
    kpe0 = A_KPE + MLA_NOPE
    g["w_in_t"] = jnp.concatenate([da[:A_KPE], da[kpe0:kpe0 + MLA_ROPE], _heads_in(d["w_qkv"], GDN_D, axis=0),
                                   da[A_AB:A_AB + 2 * N_HEADS], _heads_in(d["w_gate"], GDN_D, axis=0)], axis=0)
    assert g["w_in_t"].shape[0] == c[-1]
    g["uq_t"] = _heads_in(d["w_uq"], MLA_NOPE + MLA_ROPE, axis=0)
    dk = _heads_in(d["w_kv"][:WIDE], MLA_NOPE, axis=0).reshape(N_HEADS, MLA_NOPE, MLA_KV_RANK)
    dv = _heads_in(d["w_kv"][WIDE:], MLA_V, axis=0).reshape(N_HEADS, MLA_V, MLA_KV_RANK)
    g["ukv_t"] = jnp.concatenate([dk, dv], axis=1).reshape(-1, MLA_KV_RANK)
    g["w_out"] = _heads_in(d["w_out"], GDN_D, axis=0)
    g["gdn_conv_w"] = _heads_in(d["conv"], GDN_D)
    g["mla_out_g"] = _heads_in(d["g_mla_out"], MLA_V)
    g["gdn_norm_g"] = jnp.sum(d["g_gdn"].reshape(N_HEADS, SLOT), axis=0, keepdims=True)[:, :GDN_D]
    g["gdn_a_log"] = d["a_log"][:, :N_HEADS]
    g["gdn_dt_bias"] = d["dt_bias"][:, :N_HEADS]
    return g


def _weight_grad(name, cots, acts, out_dtype=F32, tm=1024, tn=1024, tk=2048, carry=None):
    return _matmul(name, cots, acts, "tn", out_dtype=out_dtype, tm=tm, tn=tn, tk=tk, carry=carry)


def _by_device(a):
    return a.astype(BF16).reshape((N_DEV, a.shape[0] // N_DEV) + a.shape[1:])


def _rows_of(blocks):
    return blocks.reshape((-1,) + blocks.shape[2:])


def _local_step(x, positions, target, w, mid, late):
    tabs = _rope_tables(positions)

    (h1, x1), gathered = _ffn_fwd("ffn1_fwd", x, w["ffn1_pre_g"], w["ffn1"], 0, w["ffn1_post_g"], carry=mid)
    w = dict(w, w_in_t=_rows_of(gathered[0]), uq_t=_rows_of(gathered[1]), ukv_t=_rows_of(gathered[2]))
    p = _layout_weights(w)
    hn = _norm_fwd("mix_pre_norm", x1, w["mix_pre_g"])
    proj_a = _matmul("proj_a", hn, p["w_a"], "nt")
    proj_qkv = _matmul("proj_qkv", hn, p["w_qkv"], "nt")
    proj_gate = _matmul("proj_gate", hn, p["w_gate"], "nt")
    cqn, ckvn, kpe = _mla_pre_fwd(proj_a, tabs, w["mla_q_norm_g"], w["mla_kv_norm_g"])
    q_p = _matmul("mla_q", cqn, p["w_uq"], "nt")
    kv_p = _matmul("mla_kv", ckvn, p["w_kv"], "nt")
    q, k, v = _mla_qkv_fwd(q_p, kv_p, kpe, tabs)
    o_mla, lse = _attn_fwd(q, k, v)
    ab = (proj_a, SLOT, A_AB // SLOT)
    qkv_n = _gdn_conv_fwd(proj_qkv, p["conv"])
    gb, bb = _gates_fwd(ab, p["a_log"], p["dt_bias"])
    (o_gdn, keep), (ffn2, w_out) = _gdn_fwd(qkv_n, gb, bb, carry=late)
    p["w_out"] = _heads_out(_rows_of(w_out), GDN_D, axis=0)
    cat = _mix_join_fwd(o_mla, o_gdn, proj_gate, p["g_mla_out"], p["g_gdn"])[0]
    mixed = _matmul("mix_out", cat, p["w_out"], "nn")
    x2 = _norm_residual_fwd("mix_post", x1, mixed, [w["mix_post_g"]], [F32])[0]
    (h2, y), _ = _ffn_fwd("ffn2_fwd", x2, w["ffn2_pre_g"], ffn2, 0, w["ffn2_post_g"])
    dy, loss_lanes = _loss_fwd(y, target)

    g = {}
    (dx2, xn2, dh2, a2, dhg2, dhu2, g["ffn2_pre_g"], g["ffn2_post_g"]), _ = _ffn_bwd(
        "ffn2_bwd", x2, h2, dy, w["ffn2_pre_g"], ffn2, 0, w["ffn2_post_g"])
    ffn2_grads = _Scatter([_by_device(_weight_grad("ffn2_dw_gate", dhg2, xn2, BF16, tm=1408)),
                           _by_device(_weight_grad("ffn2_dw_up", dhu2, xn2, BF16, tm=1408)),
                           _by_device(_weight_grad("ffn2_dw_down", a2, dh2, BF16, tm=1408))])
    dmixed, g["mix_post_g"] = _norm_residual_bwd("mix_post_bwd", mixed, w["mix_post_g"], dx2)
    dcat = _matmul("mix_out_dx", dmixed, p["w_out"], "nt")
    d = {}
    d["w_out"] = _weight_grad("mix_out_dw", cat, dmixed)
    do_mla, delta, do_gdn, dgate, d["g_mla_out"], d["g_gdn"] = _mix_join_bwd(o_mla, o_gdn, proj_gate, p["g_mla_out"], p["g_gdn"], dcat)
    dq, dk, dv = _attn_bwd(q, k, v, do_mla, lse, delta)
    dq_p, dkv_p, dkpe = _mla_qkv_bwd(dq, dk, dv, tabs)
    dcqn = _matmul("mla_q_dx", dq_p, p["w_uq"], "nn")
    d["w_uq"] = _weight_grad("mla_q_dw", dq_p, cqn)
    dckvn = _matmul("mla_kv_dx", dkv_p, p["w_kv"], "nn")
    d["w_kv"] = _weight_grad("mla_kv_dw", dkv_p, ckvn)
    (dqkv_n, dgb, dbb), landed_ffn2 = _gdn_bwd(qkv_n, gb, bb, keep, do_gdn, carry=ffn2_grads)
    dab, d["a_log"], d["dt_bias"] = _gates_bwd(ab, p["a_log"], p["dt_bias"], dgb, dbb)
    dproj_qkv, d["conv"] = _gdn_conv_bwd(proj_qkv, p["conv"], dqkv_n)
    dproj_a, g["mla_q_norm_g"], g["mla_kv_norm_g"] = _mla_pre_bwd(
        proj_a, tabs, w["mla_q_norm_g"], w["mla_kv_norm_g"], dcqn, dckvn, dkpe, dab)
    dhn = [_matmul("proj_a_dx", dproj_a, p["w_a"], "nn"), _matmul("proj_qkv_dx", dproj_qkv, p["w_qkv"], "nn"),
           _matmul("proj_gate_dx", dgate, p["w_gate"], "nn")]
    d["w_a"] = _weight_grad("proj_a_dw", dproj_a, hn, tm=640)
    d["w_qkv"] = _weight_grad("proj_qkv_dw", dproj_qkv, hn)
    d["w_gate"] = _weight_grad("proj_gate_dw", dgate, hn)
    dx1, g["mix_pre_g"] = _norm_bwd_add("mix_pre_bwd", x1, w["mix_pre_g"], dhn, dx2)
    g.update(_unlayout_grads(d))
    others = [t for t, _ in OTHER.values()]
    (dx, xn1, dh1, a1, dhg1, dhu1, g["ffn1_pre_g"], g["ffn1_post_g"]), landed_others = _ffn_bwd(
        "ffn1_bwd", x, h1, dx1, w["ffn1_pre_g"], w["ffn1"], 0, w["ffn1_post_g"], carry=_Scatter([_by_device(g.pop(t)) for t in others]))
    dw_down = _weight_grad("ffn1_dw_down", a1, dh1, BF16, tm=1408)
    dw_gate, (landed_down,) = _weight_grad("ffn1_dw_gate", dhg1, xn1, BF16, tm=1408, carry=_Scatter([_by_device(dw_down)]))
    dw_up, (landed_gate,) = _weight_grad("ffn1_dw_up", dhu1, xn1, BF16, tm=1408, carry=_Scatter([_by_device(dw_gate)]))
    (landed_up,) = _exchange("scatter_last", _Scatter([_by_device(dw_up)]))
    landed = dict(zip(list(FFN_NAMES) + list(OTHER),
                      [landed_gate, landed_up, landed_down] + list(landed_ffn2) + list(landed_others)))
    return loss_lanes, dx, g, landed


MESH_AXES = ("x", "y", "c")
N_LINKS = N_DEV - 1


def _place():
    return tuple(lax.axis_index(a) for a in MESH_AXES)


def _block_of(dev):
    x, y, c = dev
    return 4 * x + 2 * y + c


def _remote_copy(src, dst, sems, k, to):
    send_sems, recv_sems = sems
    return pltpu.make_async_remote_copy(src_ref=src, dst_ref=dst, send_sem=send_sems.at[k], recv_sem=recv_sems.at[k],
                                        device_id=to, device_id_type=pl.DeviceIdType.MESH)


class _Exchange:
    def __init__(self, arrays):
        self.arrays = list(arrays)
        self.n = len(self.arrays)
        self.specs = [pl.BlockSpec(memory_space=pl.ANY)] * self.n
        self.scratch = [pltpu.SemaphoreType.DMA((self.n * N_LINKS,)), pltpu.SemaphoreType.DMA((self.n * N_LINKS,)),
                        pltpu.SemaphoreType.DMA((self.n,))]

    def split(self, refs):
        n = self.n
        return refs[:n], refs[n:2 * n], (refs[2 * n], refs[2 * n + 1]), refs[2 * n + 2]


class _Gather(_Exchange):
    def out_shape(self):
        return [jax.ShapeDtypeStruct((N_DEV,) + a.shape, a.dtype) for a in self.arrays]

    def _plan(self, ins, outs, sems, local_sems):
        x, y, c = _place()
        me, sibling = (x, y, c), (x, y, 1 - c)
        chips = [(1 - x, y), (x, 1 - y), (1 - x, 1 - y)]

        def copy(a, k, block, to, mine=False):
            src = ins[a] if mine else outs[a].at[_block_of(block)]
            return _remote_copy(src, outs[a].at[_block_of(block)], sems, a * N_LINKS + k, to)

        local = [pltpu.make_async_copy(ins[a], outs[a].at[_block_of(me)], local_sems.at[a]) for a in range(self.n)]
        first = []
        for a in range(self.n):
            first.append(copy(a, 0, me, sibling, mine=True))
            first += [copy(a, 1 + j, me, (*chip, c), mine=True) for j, chip in enumerate(chips)]
        return me, sibling, chips, c, copy, local, first

    def start(self, ins, outs, sems, local_sems):
        *_, local, first = self._plan(ins, outs, sems, local_sems)
        for cp in local + first:
            cp.start()

    def finish(self, ins, outs, sems, local_sems):
        me, sibling, chips, c, copy, local, first = self._plan(ins, outs, sems, local_sems)
        passed = []
        for j, chip in enumerate(chips):
            for a in range(self.n):
                copy(a, 1 + j, (*chip, c), me).wait_recv()
                passed.append(copy(a, 4 + j, (*chip, c), sibling))
                passed[-1].start()
        for a in range(self.n):
            copy(a, 0, sibling, me).wait_recv()
            for j, chip in enumerate(chips):
                copy(a, 4 + j, (*chip, 1 - c), me).wait_recv()
        for cp in first + passed:
            cp.wait_send()
        for cp in local:
            cp.wait()


class _Scatter(_Exchange):
    def out_shape(self):
        return [jax.ShapeDtypeStruct(a.shape, a.dtype) for a in self.arrays]

    def _plan(self, ins, outs, sems, local_sems):
        x, y, c = _place()
        me = _block_of((x, y, c))

        def peer(r):
            return (1 - x if r & 4 else x, 1 - y if r & 2 else y, 1 - c if r & 1 else c)

        local = [pltpu.make_async_copy(ins[a].at[me], outs[a].at[me], local_sems.at[a]) for a in range(self.n)]
        sends = [_remote_copy(ins[a].at[_block_of(peer(r))], outs[a].at[me], sems, a * N_LINKS + r - 1, peer(r))
                 for a in range(self.n) for r in range(1, N_DEV)]
        arrivals = [_remote_copy(ins[a].at[me], outs[a].at[_block_of(peer(r))], sems, a * N_LINKS + r - 1, peer(r))
                    for a in range(self.n) for r in range(1, N_DEV)]
        return local, sends, arrivals

    def start(self, ins, outs, sems, local_sems):
        local, sends, _ = self._plan(ins, outs, sems, local_sems)
        for cp in local + sends:
            cp.start()

    def finish(self, ins, outs, sems, local_sems):
        local, sends, arrivals = self._plan(ins, outs, sems, local_sems)
        for cp in arrivals:
            cp.wait_recv()
        for cp in sends:
            cp.wait_send()
        for cp in local:
            cp.wait()


def _exchange(name, plan):
    def body(*refs):
        parts = plan.split(refs)
        plan.start(*parts)
        plan.finish(*parts)

    return pl.pallas_call(
        body, name=name,
        in_specs=plan.specs,
        out_specs=plan.specs,
        out_shape=plan.out_shape(),
        scratch_shapes=plan.scratch,
    )(*plan.arrays)


def _call_carrying(body, plan, operands, *, name, grid, in_specs, out_specs, out_shape, scratch_shapes, compiler_params):
    if plan is None:
        outs = pl.pallas_call(body, name=name, grid=grid, in_specs=in_specs, out_specs=out_specs, out_shape=out_shape,
                              scratch_shapes=scratch_shapes, compiler_params=compiler_params)(*operands)
        return outs, []
    n_i, n_o, n_s, k = len(in_specs), len(out_specs), len(scratch_shapes), plan.n

    def whole(*refs):
        cut = [n_i, n_i + k, n_i + k + n_o, n_i + 2 * k + n_o, n_i + 2 * k + n_o + n_s]
        own_in, ex_in, own_out, ex_out, own_scr, ex_scr = (refs[a:b] for a, b in zip([0] + cut, cut + [len(refs)]))
        parts = plan.split(ex_in + ex_out + ex_scr)
        first = last = True
        for axis, size in enumerate(grid):
            first = first & (pl.program_id(axis) == 0)
            last = last & (pl.program_id(axis) == size - 1)

        @pl.when(first)
        def _():
            plan.start(*parts)

        body(*own_in, *own_out, *own_scr)

        @pl.when(last)
        def _():
            plan.finish(*parts)

    outs = pl.pallas_call(
        whole, name=name, grid=grid,
        in_specs=list(in_specs) + plan.specs, out_specs=list(out_specs) + plan.specs,
        out_shape=list(out_shape) + plan.out_shape(), scratch_shapes=list(scratch_shapes) + plan.scratch,
        compiler_params=compiler_params,
    )(*operands, *plan.arrays)
    return outs[:n_o], outs[n_o:]


def _row_tile(rows, target=256):
    best = rows
    for cand in range(16, min(rows, target) + 1, 16):
        if rows % cand == 0:
            best = cand
    return best


def _sum_blocks(name, blocks):
    rows, width = blocks.shape[-2:]
    tm = _row_tile(rows)

    def body(x_ref, o_ref):
        acc = x_ref[0].astype(F32)
        for d in range(1, N_DEV):
            acc = acc + x_ref[d].astype(F32)
        o_ref[...] = acc

    return pl.pallas_call(
        body, name=name,
        grid=(rows // tm,),
        in_specs=[pl.BlockSpec((N_DEV, tm, width), lambda i: (0, i, 0))],
        out_specs=pl.BlockSpec((tm, width), lambda i: (i, 0)),
        out_shape=jax.ShapeDtypeStruct((rows, width), F32),
        compiler_params=pltpu.CompilerParams(dimension_semantics=("parallel",)),
    )(blocks)


def _all_reduce_small(name, vec):
    rows, width = vec.shape

    def body(x_ref, o_ref, all_ref, send_sems, recv_sems):
        x, y, c = _place()
        me = _block_of((x, y, c))
        all_ref[me] = x_ref[...]

        def peer(r):
            return (1 - x if r & 4 else x, 1 - y if r & 2 else y, 1 - c if r & 1 else c)

        def copy(r, block):
            return _remote_copy(x_ref, all_ref.at[block], (send_sems, recv_sems), r - 1, peer(r))

        sends = [copy(r, me) for r in range(1, N_DEV)]
        for cp in sends:
            cp.start()
        for r in range(1, N_DEV):
            copy(r, _block_of(peer(r))).wait_recv()
        for cp in sends:
            cp.wait_send()
        acc = all_ref[0]
        for d in range(1, N_DEV):
            acc = acc + all_ref[d]
        o_ref[...] = acc

    return pl.pallas_call(
        body, name=name,
        in_specs=[pl.BlockSpec(memory_space=pltpu.VMEM)],
        out_specs=pl.BlockSpec(memory_space=pltpu.VMEM),
        out_shape=jax.ShapeDtypeStruct((rows, width), F32),
        scratch_shapes=[pltpu.VMEM((N_DEV, rows, width), F32), pltpu.SemaphoreType.DMA((N_LINKS,)), pltpu.SemaphoreType.DMA((N_LINKS,))],
    )(vec)


def _adamw(name, w, g, m, v):
    def fn(rows, consts):
        wv, gv, mv, vv = rows
        m2 = ADAM_B1 * mv + (1.0 - ADAM_B1) * gv
        v2 = ADAM_B2 * vv + (1.0 - ADAM_B2) * jnp.square(gv)
        m_hat = m2 / (1.0 - ADAM_B1 ** ADAM_STEP)
        v_hat = v2 / (1.0 - ADAM_B2 ** ADAM_STEP)
        return [-ADAM_LR * (m_hat / (jnp.sqrt(v_hat) + ADAM_EPS) + ADAM_WD * wv), m2, v2], []

    return _rowwise(name, fn, [w, g, m, v], [], [(w.shape[1], F32)] * 3, tm=_row_tile(w.shape[0]))


ROW = 1024
FFN_NAMES = ("ffn1_w_gate", "ffn1_w_up", "ffn1_w_down", "ffn2_w_gate", "ffn2_w_up", "ffn2_w_down")
OTHER = {"w_in": ("w_in_t", True), "mla_w_uq": ("uq_t", True), "mla_w_ukv": ("ukv_t", True), "w_out": ("w_out", False)}
BY_COLUMNS = ("ffn1_w_gate", "ffn1_w_up", "ffn2_w_gate", "ffn2_w_up", "w_in", "mla_w_uq", "mla_w_ukv")
SMALL = {
    "ffn1_pre_g": (1024, 1024), "ffn1_post_g": (1024, 1024), "mix_pre_g": (1024, 1024), "mla_q_norm_g": (256, 256),
    "mla_kv_norm_g": (128, 128), "mla_out_g": (512, 512), "gdn_a_log": (8, 128), "gdn_dt_bias": (8, 128),
    "gdn_norm_g": (64, 128), "mix_post_g": (1024, 1024), "ffn2_pre_g": (1024, 1024), "ffn2_post_g": (1024, 1024),
}
CONV_SHAPE = (GDN_CONV, 3 * N_HEADS * GDN_D)
CONV_SHARD = (GDN_CONV, CONV_SHAPE[1] // N_DEV)
CONV_LANES = CONV_SHAPE[0] * CONV_SHAPE[1]
SMALL_ROWS = 8
REDUCE_ROWS = 16


def _pack_small(vecs, conv, rows):
    parts = [_pad_lanes(vecs[n].reshape(1, -1), 0, r) for n, (_, r) in SMALL.items()]
    parts.append(conv.reshape(1, -1))
    flat = jnp.concatenate(parts, axis=1)
    return _pad_lanes(flat, 0, rows * ROW).reshape(rows, ROW)


def _unpack_small(buf):
    flat = buf.reshape(1, -1)
    out, at = {}, 0
    for n, (w, r) in SMALL.items():
        out[n] = flat[:, at:at + w]
        at += r
    return out, flat[0, at:]


def kernel(x, positions, ffn1_pre_g, ffn1_w_gate, ffn1_w_up, ffn1_w_down, ffn1_post_g, mix_pre_g, w_in, mla_q_norm_g, mla_w_uq, mla_kv_norm_g, mla_w_ukv, mla_out_g, gdn_conv_w, gdn_a_log, gdn_dt_bias, gdn_norm_g, w_out, mix_post_g, ffn2_pre_g, ffn2_w_gate, ffn2_w_up, ffn2_w_down, ffn2_post_g, loss_target, m_ffn1_pre_g, m_ffn1_w_gate, m_ffn1_w_up, m_ffn1_w_down, m_ffn1_post_g, m_mix_pre_g, m_w_in, m_mla_q_norm_g, m_mla_w_uq, m_mla_kv_norm_g, m_mla_w_ukv, m_mla_out_g, m_gdn_conv_w, m_gdn_a_log, m_gdn_dt_bias, m_gdn_norm_g, m_w_out, m_mix_post_g, m_ffn2_pre_g, m_ffn2_w_gate, m_ffn2_w_up, m_ffn2_w_down, m_ffn2_post_g, v_ffn1_pre_g, v_ffn1_w_gate, v_ffn1_w_up, v_ffn1_w_down, v_ffn1_post_g, v_mix_pre_g, v_w_in, v_mla_q_norm_g, v_mla_w_uq, v_mla_kv_norm_g, v_mla_w_ukv, v_mla_out_g, v_gdn_conv_w, v_gdn_a_log, v_gdn_dt_bias, v_gdn_norm_g, v_w_out, v_mix_post_g, v_ffn2_pre_g, v_ffn2_w_gate, v_ffn2_w_up, v_ffn2_w_down, v_ffn2_post_g):
    given = dict(locals())
    order = ["ffn1_pre_g", "ffn1_w_gate", "ffn1_w_up", "ffn1_w_down", "ffn1_post_g", "mix_pre_g", "w_in", "mla_q_norm_g",
             "mla_w_uq", "mla_kv_norm_g", "mla_w_ukv", "mla_out_g", "gdn_conv_w", "gdn_a_log", "gdn_dt_bias", "gdn_norm_g",
             "w_out", "mix_post_g", "ffn2_pre_g", "ffn2_w_gate", "ffn2_w_up", "ffn2_w_down", "ffn2_post_g"]
    assert sorted(order) == sorted(list(FFN_NAMES) + list(OTHER) + list(SMALL) + ["gdn_conv_w"])

    def drop_depth(a):
        return a[0] if a.ndim == 3 else a

    wts = {n: drop_depth(given[n]) for n in order}
    mom = {n: drop_depth(given["m_" + n]) for n in order}
    var = {n: drop_depth(given["v_" + n]) for n in order}
    me = _block_of(_place())

    def wire(n):
        return (wts[n].T if n in BY_COLUMNS else wts[n]).astype(BF16)

    (ffn1,) = _exchange("gather_first", _Gather([jnp.stack([wire(n) for n in FFN_NAMES[:3]])]))
    mid = _Gather([wire(n) for n in ("w_in", "mla_w_uq", "mla_w_ukv")])
    late = _Gather([jnp.stack([wire(n) for n in FFN_NAMES[3:]]), wire("w_out")])
    conv_at = lax.dynamic_update_slice(jnp.zeros((N_DEV, CONV_SHARD[0] * CONV_SHARD[1]), F32),
                                       wts["gdn_conv_w"].reshape(1, -1), (me, 0))
    conv_all = _all_reduce_small("gather_conv", _pad_lanes(conv_at.reshape(1, -1), 0, SMALL_ROWS * ROW).reshape(SMALL_ROWS, ROW))
    full = {n: wts[n] for n in SMALL}
    full["ffn1"] = ffn1
    full["gdn_conv_w"] = conv_all.reshape(-1)[:CONV_LANES].reshape((N_DEV,) + CONV_SHARD).transpose(1, 0, 2).reshape(CONV_SHAPE)

    loss_lanes, dx, grads, landed = _local_step(x[0], positions[0], loss_target[0], full, mid, late)
    loss = lax.psum(jnp.sum(loss_lanes), MESH_AXES)

    sums = {n: _sum_blocks("sum_" + n, blocks) for n, blocks in landed.items()}
    grad = {n: (sums[n].T if n in BY_COLUMNS else sums[n]) for n in sums}
    small_sum = _all_reduce_small("reduce_small", _pack_small(grads, grads["gdn_conv_w"].reshape(-1), REDUCE_ROWS))
    small_grad, conv_grad_full = _unpack_small(small_sum)
    grad.update(small_grad)
    grad["gdn_conv_w"] = lax.dynamic_slice(conv_grad_full[:CONV_LANES].reshape(CONV_SHAPE), (0, me * CONV_SHARD[1]), CONV_SHARD)

    outs = {"grad": grad, "delta": {}, "new_m": {}, "new_v": {}}
    for n in list(FFN_NAMES) + list(OTHER):
        outs["delta"][n], outs["new_m"][n], outs["new_v"][n] = _adamw("adamw_" + n, wts[n], grad[n], mom[n], var[n])
    small = [_pack_small(s, s["gdn_conv_w"].reshape(-1), SMALL_ROWS) for s in (wts, grad, mom, var)]
    for kind, s in zip(("delta", "new_m", "new_v"), _adamw("adamw_small", *small)):
        vecs, conv = _unpack_small(s)
        outs[kind].update(vecs)
        outs[kind]["gdn_conv_w"] = conv[:CONV_SHARD[0] * CONV_SHARD[1]].reshape(CONV_SHARD)
    result = [loss, dx[None]]
    for kind in ("grad", "delta", "new_m", "new_v"):
        result += [outs[kind][n].reshape(given[n].shape) for n in order]
    return tuple(result)
```

```python
import jax
import jax.numpy as jnp
from jax import lax
from jax.experimental import pallas as pl
from jax.experimental.pallas import tpu as pltpu

F32 = jnp.float32
BF16 = jnp.bfloat16
HI = lax.Precision.HIGH

N_DEV = 8
D_MODEL = 1024
D_FF = 2816
N_HEADS = 8
SLOT = 128
MLA_Q_RANK = 256
MLA_KV_RANK = 128
MLA_NOPE = 64
MLA_ROPE = 32
MLA_V = 64
GDN_D = 64
GDN_CONV = 4
GDN_CHUNK = 64
ROPE_THETA = 10000.0
EPS = 1e-6
ADAM_LR, ADAM_B1, ADAM_B2, ADAM_EPS, ADAM_WD, ADAM_STEP = 0.001, 0.9, 0.999, 1e-08, 0.01, 10


def _dot(a, b, ca, cb, precision=None):
    lead = a.ndim - 2
    batch = tuple(range(lead))
    return lax.dot_general(a, b, (((lead + ca,), (lead + cb,)), (batch, batch)), precision=precision,
                           preferred_element_type=F32)


def _nn(a, b, precision=None):
    return _dot(a, b, 1, 0, precision)


def _nt(a, b, precision=None):
    return _dot(a, b, 1, 1, precision)


def _tn(a, b, precision=None):
    return _dot(a, b, 0, 0, precision)


def _sigmoid(x):
    return 1.0 / (1.0 + jnp.exp(-x))


def _silu(x):
    return x * _sigmoid(x)


def _rms(x, g, n):
    ms = jnp.sum(x * x, axis=-1, keepdims=True) * (1.0 / n)
    return x * lax.rsqrt(ms + EPS) * g


def _chunk_masks():
    c = GDN_CHUNK
    i = lax.broadcasted_iota(jnp.int32, (c, c), 0)
    j = lax.broadcasted_iota(jnp.int32, (c, c), 1)
    lower = i >= j
    strict = i > j
    eye = (i == j).astype(F32)
    blocks = []
    b = 1
    while b < c:
        same = (i // (2 * b)) == (j // (2 * b))
        blocks.append(same & ((i % (2 * b)) >= b) & ((j % (2 * b)) < b))
        b *= 2
    return lower, strict, eye, blocks


def _unit_lower_inverse(low, eye, blocks):
    t = jnp.broadcast_to(eye, low.shape)
    for m in blocks:
        lo = jnp.where(m, low, 0.0)
        t = t - _nn(t, _nn(lo, t, HI), HI)
    return t


@jax.custom_vjp
def _known_inverse(low, tinv):
    return tinv


def _known_inverse_fwd(low, tinv):
    return tinv, tinv


def _known_inverse_bwd(tinv, dt):
    return -_tn(tinv, _nt(dt, tinv, HI), HI), jnp.zeros_like(tinv)


_known_inverse.defvjp(_known_inverse_fwd, _known_inverse_bwd)


def _gdn_chunk(q, k, v, gc, bb, s, masks, tinv=None):
    lower, strict, eye, blocks = masks
    qs = q * (GDN_D ** -0.5)
    gct = jnp.swapaxes(gc, -1, -2)
    decay = jnp.exp(jnp.where(lower, gc - gct, -1e30))
    kb = k * bb
    low = jnp.where(strict, _nt(kb, k, HI) * decay, 0.0)
    tinv = _unit_lower_inverse(low, eye, blocks) if tinv is None else _known_inverse(low, tinv)
    eg = jnp.exp(gc)
    w = _nn(tinv, kb * eg, HI)
    u = _nn(tinv, v * bb, HI)
    attn = _nt(qs, k, HI) * decay
    last = lax.broadcasted_iota(jnp.int32, gc.shape[-2:], 0) == GDN_CHUNK - 1
    g_end = jnp.sum(jnp.where(last, gc, 0.0), axis=-2, keepdims=True)
    k_dec = k * jnp.exp(g_end - gc)
    v_new = u - _nn(w, s, HI)
    o = _nn(qs * eg, s, HI) + _nn(attn, v_new, HI)
    s_new = s * jnp.exp(g_end) + _tn(k_dec, v_new, HI)
    return o, s_new, tinv


GDN_GROUP = 8
GDN_GROUPS = N_HEADS // GDN_GROUP


def _group_heads(ref):
    return jnp.stack([ref[:, pl.ds(j * SLOT, GDN_D)] for j in range(GDN_GROUP)])


def _ungroup_heads(ref, val):
    pad = jnp.zeros((GDN_CHUNK, SLOT - GDN_D), F32)
    for j in range(GDN_GROUP):
        ref[:, pl.ds(j * SLOT, GDN_D)] = val[j]
        ref[:, pl.ds(j * SLOT + GDN_D, SLOT - GDN_D)] = pad


def _gdn_fwd(qkv, gb, bb, carry=None):
    t = qkv.shape[0]
    n_chunks = t // GDN_CHUNK
    d = GDN_D

    def body(q_ref, k_ref, v_ref, g_ref, b_ref, o_ref, keep_ref, s_ref):
        @pl.when(pl.program_id(1) == 0)
        def _():
            s_ref[...] = jnp.zeros_like(s_ref)

        s = s_ref[...]
        keep_ref[:, 0, 0] = s
        o, s_new, tinv = _gdn_chunk(*[_group_heads(r) for r in (q_ref, k_ref, v_ref, g_ref, b_ref)], s, _chunk_masks())
        keep_ref[:, 0, 1] = tinv
        s_ref[...] = s_new
        _ungroup_heads(o_ref, o)

    def spec(kind=0):
        return pl.BlockSpec((GDN_CHUNK, GDN_GROUP * SLOT), lambda h, n: (n, kind * GDN_GROUPS + h))

    return _call_carrying(
        body, carry, (qkv, qkv, qkv, gb, bb), name="gdn_fwd",
        grid=(GDN_GROUPS, n_chunks),
        in_specs=[spec(0), spec(1), spec(2), spec(), spec()],
        out_specs=[spec(), pl.BlockSpec((GDN_GROUP, 1, 2, d, d), lambda h, n: (h, n, 0, 0, 0))],
        out_shape=[jax.ShapeDtypeStruct((t, N_HEADS * SLOT), F32), jax.ShapeDtypeStruct((N_HEADS, n_chunks, 2, d, d), F32)],
        scratch_shapes=[pltpu.VMEM((GDN_GROUP, d, d), F32)],
        compiler_params=pltpu.CompilerParams(dimension_semantics=("arbitrary", "arbitrary")),
    )


def _gdn_bwd(qkv, gb, bb, keep, do, carry=None):
    t = qkv.shape[0]
    n_chunks = t // GDN_CHUNK
    d = GDN_D

    def body(q_ref, k_ref, v_ref, g_ref, b_ref, keep_ref, do_ref, dqkv_ref, dg_ref, db_ref, ds_ref):
        @pl.when(pl.program_id(1) == 0)
        def _():
            ds_ref[...] = jnp.zeros_like(ds_ref)

        masks = _chunk_masks()
        tinv = keep_ref[:, 0, 1]
        _, pull = jax.vjp(lambda *a: _gdn_chunk(*a, masks, tinv)[:2],
                          *[_group_heads(r) for r in (q_ref, k_ref, v_ref, g_ref, b_ref)], keep_ref[:, 0, 0])
        dq, dk, dv, dg, db, ds = pull((_group_heads(do_ref), ds_ref[...]))
        ds_ref[...] = ds
        for i, val in enumerate((dq, dk, dv)):
            _ungroup_heads(dqkv_ref.at[i], val)
        _ungroup_heads(dg_ref, dg)
        _ungroup_heads(db_ref, db)

    def spec(kind=0):
        return pl.BlockSpec((GDN_CHUNK, GDN_GROUP * SLOT), lambda h, n: (n_chunks - 1 - n, kind * GDN_GROUPS + h))

    return _call_carrying(
        body, carry, (qkv, qkv, qkv, gb, bb, keep, do), name="gdn_bwd",
        grid=(GDN_GROUPS, n_chunks),
        in_specs=[spec(0), spec(1), spec(2), spec(), spec(),
                  pl.BlockSpec((GDN_GROUP, 1, 2, d, d), lambda h, n: (h, n_chunks - 1 - n, 0, 0, 0)), spec()],
        out_specs=[pl.BlockSpec((3, GDN_CHUNK, GDN_GROUP * SLOT), lambda h, n: (0, n_chunks - 1 - n, h)), spec(), spec()],
        out_shape=[jax.ShapeDtypeStruct((3, t, N_HEADS * SLOT), F32)] + [jax.ShapeDtypeStruct((t, N_HEADS * SLOT), F32)] * 2,
        scratch_shapes=[pltpu.VMEM((GDN_GROUP, d, d), F32)],
        compiler_params=pltpu.CompilerParams(dimension_semantics=("arbitrary", "arbitrary")),
    )


def _rowwise(name, fn, rows, consts, outs, sums=(), tm=256):
    rows = [x if isinstance(x, tuple) else (x, x.shape[1], 0) for x in rows]
    t = rows[0][0].shape[0]
    tm = min(tm, t)
    steps = t // tm
    n_r, n_c, n_o, n_s = len(rows), len(consts), len(outs), len(sums)

    def window(width, block):
        return pl.BlockSpec((tm, width), lambda i: (i, block))

    def body(*refs):
        r, c = refs[:n_r], refs[n_r:n_r + n_c]
        o, s = refs[n_r + n_c:n_r + n_c + n_o], refs[n_r + n_c + n_o:]
        vals, tot = fn([x[...] for x in r], [x[...] for x in c])
        for ref, val in zip(o, vals):
            ref[...] = val.astype(ref.dtype)
        if n_s:
            @pl.when(pl.program_id(0) == 0)
            def _():
                for ref in s:
                    ref[...] = jnp.zeros_like(ref)

            for ref, val in zip(s, tot):
                ref[...] += val

    return pl.pallas_call(
        body, name=name,
        grid=(steps,),
        in_specs=[window(w, b) for _, w, b in rows] + [pl.BlockSpec(x.shape, lambda i: (0, 0)) for x in consts],
        out_specs=[pl.BlockSpec((tm, w), lambda i: (i, 0)) for w, _ in outs]
        + [pl.BlockSpec((1, w), lambda i: (0, 0)) for w in sums],
        out_shape=[jax.ShapeDtypeStruct((t, w), dt) for w, dt in outs]
        + [jax.ShapeDtypeStruct((1, w), F32) for w in sums],
        compiler_params=pltpu.CompilerParams(dimension_semantics=("arbitrary",)),
    )(*[x for x, _, _ in rows], *consts)


def _tile(dim, target):
    if dim <= target:
        return dim
    best = None
    for cand in range(128, target + 1, 128):
        if dim % cand == 0:
            best = cand
    assert best is not None, (dim, target)
    return best


def _matmul(name, a, b, mode, out_dtype=F32, tm=512, tn=1024, tk=2048, carry=None):
    if mode == "nn":
        (m, k), n = a.shape, b.shape[1]
    elif mode == "nt":
        (m, k), n = a.shape, b.shape[0]
    else:
        (k, m), n = a.shape, b.shape[1]
    tm, tn, tk = _tile(m, tm), _tile(n, tn), _tile(k, tk)
    k_steps = k // tk
    product = {"nn": _nn, "nt": _nt, "tn": _tn}[mode]

    def body(a_ref, b_ref, o_ref, acc_ref):
        part = product(a_ref[...].astype(BF16), b_ref[...].astype(BF16))
        if k_steps == 1:
            o_ref[...] = part.astype(o_ref.dtype)
        else:
            kk = pl.program_id(2)

            @pl.when(kk == 0)
            def _():
                acc_ref[...] = part

            @pl.when(kk > 0)
            def _():
                acc_ref[...] += part

            @pl.when(kk == k_steps - 1)
            def _():
                o_ref[...] = acc_ref[...].astype(o_ref.dtype)

    a_spec = pl.BlockSpec((tk, tm), lambda i, j, kk: (kk, i)) if mode == "tn" else pl.BlockSpec((tm, tk), lambda i, j, kk: (i, kk))
    b_spec = pl.BlockSpec((tn, tk), lambda i, j, kk: (j, kk)) if mode == "nt" else pl.BlockSpec((tk, tn), lambda i, j, kk: (kk, j))
    (out,), carried = _call_carrying(
        body, carry, (a, b), name=name,
        grid=(m // tm, n // tn, k_steps),
        in_specs=[a_spec, b_spec],
        out_specs=[pl.BlockSpec((tm, tn), lambda i, j, kk: (i, j))],
        out_shape=[jax.ShapeDtypeStruct((m, n), out_dtype)],
        scratch_shapes=[pltpu.VMEM((tm, tn) if k_steps > 1 else (8, 128), F32)],
        compiler_params=pltpu.CompilerParams(dimension_semantics=("arbitrary", "arbitrary", "arbitrary")),
    )
    return out if carry is None else (out, carried)


FFN_TM = 512
FFN_BWD_TM = 256
FFN_BLOCKS = 4
FFN_GATE, FFN_UP, FFN_DOWN = 0, 1, 2


def _ffn_weight_specs(ffn_w, first):
    _, _, rows, dm = ffn_w.shape

    def spec(k):
        return pl.BlockSpec((FFN_BLOCKS, None, rows, dm), lambda i, j: (j, first + k, 0, 0))

    return [spec(FFN_GATE), spec(FFN_UP), spec(FFN_DOWN)], FFN_BLOCKS * rows


def _ffn_fwd(name, x, g_pre, ffn_w, first, g_post, carry=None):
    t, dm = x.shape
    tm = min(FFN_TM, t)
    w_specs, tf = _ffn_weight_specs(ffn_w, first)
    f_steps = N_DEV // FFN_BLOCKS

    def body(x_ref, gpre_ref, wg_ref, wu_ref, wd_ref, gpost_ref, h_ref, y_ref, hg_ref, hu_ref, xn_ref, acc_ref):
        j = pl.program_id(1)

        @pl.when(j == 0)
        def _():
            xn_ref[...] = _rms(x_ref[...], gpre_ref[...], dm).astype(BF16)
            acc_ref[...] = jnp.zeros_like(acc_ref)

        xn = xn_ref[...]
        wg, wu, wd = (r[...].reshape(tf, dm) for r in (wg_ref, wu_ref, wd_ref))
        hg, hu = _nt(xn, wg), _nt(xn, wu)
        hg_ref[...] = hg.astype(BF16)
        hu_ref[...] = hu.astype(BF16)
        a = _silu(hg) * hu
        acc_ref[...] += _nn(a.astype(BF16), wd)

        @pl.when(j == f_steps - 1)
        def _():
            h = acc_ref[...]
            h_ref[...] = h
            y_ref[...] = x_ref[...] + 0.5 * _rms(h, gpost_ref[...], dm)

    row = pl.BlockSpec((tm, dm), lambda i, j: (i, 0))
    vec = pl.BlockSpec((1, dm), lambda i, j: (0, 0))
    wide = pl.BlockSpec((tm, tf), lambda i, j: (i, j))
    return _call_carrying(
        body, carry, (x, g_pre, ffn_w, ffn_w, ffn_w, g_post), name=name,
        grid=(t // tm, f_steps),
        in_specs=[row, vec, *w_specs, vec],
        out_specs=[row, row, wide, wide],
        out_shape=[jax.ShapeDtypeStruct((t, dm), F32)] * 2 + [jax.ShapeDtypeStruct((t, f_steps * tf), BF16)] * 2,
        scratch_shapes=[pltpu.VMEM((tm, dm), BF16), pltpu.VMEM((tm, dm), F32)],
        compiler_params=pltpu.CompilerParams(dimension_semantics=("arbitrary", "arbitrary")),
    )


def _ffn_bwd(name, x, h, hg, hu, dy, g_pre, ffn_w, first, g_post, carry=None):
    t, dm = x.shape
    tm = min(FFN_BWD_TM, t)
    w_specs, tf = _ffn_weight_specs(ffn_w, first)
    f_steps = N_DEV // FFN_BLOCKS
    f = f_steps * tf

    def post(hv, g):
        return 0.5 * _rms(hv, g, dm)

    def pre(xv, g):
        return _rms(xv, g, dm)

    def body(x_ref, h_ref, dy_ref, hg_ref, hu_ref, gpre_ref, wg_ref, wu_ref, wd_ref, gpost_ref,
             dx_ref, xn_ref, dh_ref, a_ref, dhg_ref, dhu_ref, dgpre_ref, dgpost_ref, acc_ref):
        i, j = pl.program_id(0), pl.program_id(1)

        @pl.when((i == 0) & (j == 0))
        def _():
            dgpre_ref[...] = jnp.zeros_like(dgpre_ref)
            dgpost_ref[...] = jnp.zeros_like(dgpost_ref)

        @pl.when(j == 0)
        def _():
            xn_ref[...] = pre(x_ref[...], gpre_ref[...]).astype(BF16)
            _, pull = jax.vjp(post, h_ref[...], gpost_ref[...])
            dh, dg = pull(dy_ref[...])
            dh_ref[...] = dh.astype(BF16)
            dgpost_ref[...] += dg
            acc_ref[...] = jnp.zeros_like(acc_ref)

        wg, wu, wd = (r[...].reshape(tf, dm) for r in (wg_ref, wu_ref, wd_ref))
        hg, hu = hg_ref[...].astype(F32), hu_ref[...].astype(F32)
        da = _nt(dh_ref[...], wd)
        sig = _sigmoid(hg)
        act = hg * sig
        dhu = (da * act).astype(BF16)
        dhg = (da * hu * (sig * (1.0 + hg * (1.0 - sig)))).astype(BF16)
        a_ref[...] = (act * hu).astype(BF16)
        dhg_ref[...] = dhg
        dhu_ref[...] = dhu
        acc_ref[...] += _nn(dhg, wg) + _nn(dhu, wu)

        @pl.when(j == f_steps - 1)
        def _():
            _, pull = jax.vjp(pre, x_ref[...], gpre_ref[...])
            dx, dg = pull(acc_ref[...])
            dx_ref[...] = dy_ref[...] + dx
            dgpre_ref[...] += dg

    row = pl.BlockSpec((tm, dm), lambda i, j: (i, 0))
    vec = pl.BlockSpec((1, dm), lambda i, j: (0, 0))
    wide = pl.BlockSpec((tm, tf), lambda i, j: (i, j))
    return _call_carrying(
        body, carry, (x, h, dy, hg, hu, g_pre, ffn_w, ffn_w, ffn_w, g_post), name=name,
        grid=(t // tm, f_steps),
        in_specs=[row, row, row, wide, wide, vec, *w_specs, vec],
        out_specs=[row, row, row, wide, wide, wide, vec, vec],
        out_shape=[jax.ShapeDtypeStruct((t, dm), F32), jax.ShapeDtypeStruct((t, dm), BF16), jax.ShapeDtypeStruct((t, dm), BF16),
                   jax.ShapeDtypeStruct((t, f), BF16), jax.ShapeDtypeStruct((t, f), BF16), jax.ShapeDtypeStruct((t, f), BF16),
                   jax.ShapeDtypeStruct((1, dm), F32), jax.ShapeDtypeStruct((1, dm), F32)],
        scratch_shapes=[pltpu.VMEM((tm, dm), F32)],
        compiler_params=pltpu.CompilerParams(dimension_semantics=("arbitrary", "arbitrary")),
    )


ATT_T = 512
ATT_GROUP = 2
ATT_SCALE = (MLA_NOPE + MLA_ROPE) ** -0.5


def _stack_slots(ref, group):
    return jnp.stack([ref[:, pl.ds(j * SLOT, SLOT)] for j in range(group)])


def _unstack_slots(ref, val):
    for j in range(val.shape[0]):
        ref[:, pl.ds(j * SLOT, SLOT)] = val[j].astype(ref.dtype)


def _scores(q, k, diagonal):
    s = _nt(q, k) * ATT_SCALE
    if diagonal:
        row = lax.broadcasted_iota(jnp.int32, s.shape[1:], 0)
        col = lax.broadcasted_iota(jnp.int32, s.shape[1:], 1)
        s = jnp.where(col <= row, s, -1e30)
    return s


def _attn_pairs(steps, q_major):
    pairs = ([(qi, ki) for qi in range(steps) for ki in range(qi + 1)] if q_major
             else [(qi, ki) for ki in range(steps) for qi in range(ki, steps)])
    return jnp.array([p[0] for p in pairs], jnp.int32), jnp.array([p[1] for p in pairs], jnp.int32)


def _attn_specs(tile):
    width = ATT_GROUP * SLOT
    return (pl.BlockSpec((tile, width), lambda h, p, qt, kt: (qt[p], h)),
            pl.BlockSpec((tile, width), lambda h, p, qt, kt: (kt[p], h)))


def _attn_fwd(q, k, v):
    t = q.shape[0]
    tile = min(ATT_T, t)
    steps = t // tile
    g = ATT_GROUP

    strip = min(SLOT, tile)

    def body(qt_ref, kt_ref, q_ref, k_ref, v_ref, o_ref, lse_ref, m_ref, l_ref, alpha_ref, acc_ref, s_ref, p_ref):
        qi, ki = qt_ref[pl.program_id(1)], kt_ref[pl.program_id(1)]

        @pl.when(ki == 0)
        def _():
            m_ref[...] = jnp.full_like(m_ref, -1e30)
            l_ref[...] = jnp.zeros_like(l_ref)
            acc_ref[...] = jnp.zeros_like(acc_ref)

        def step(diagonal):
            s_ref[...] = _nt(_stack_slots(k_ref, g), _stack_slots(q_ref, g))
            for j in range(tile // strip):
                c = pl.ds(j * strip, strip)
                s = s_ref[:, :, c] * ATT_SCALE
                if diagonal:
                    key = lax.broadcasted_iota(jnp.int32, s.shape[1:], 0)
                    query = lax.broadcasted_iota(jnp.int32, s.shape[1:], 1) + j * strip
                    s = jnp.where(key <= query, s, -1e30)
                m_old = m_ref[:, :, c]
                m_new = jnp.maximum(m_old, jnp.max(s, axis=1, keepdims=True))
                p = jnp.exp(s - m_new)
                alpha = jnp.exp(m_old - m_new)
                l_ref[:, :, c] = alpha * l_ref[:, :, c] + jnp.sum(p, axis=1, keepdims=True)
                alpha_ref[:, :, c] = alpha
                m_ref[:, :, c] = m_new
                p_ref[:, :, c] = p.astype(BF16)
            acc_ref[...] = acc_ref[...] * alpha_ref[...] + _tn(_stack_slots(v_ref, g), p_ref[...])

        @pl.when(ki < qi)
        def _():
            step(False)

        @pl.when(ki == qi)
        def _():
            step(True)
            out = acc_ref[...] / l_ref[...]
            lse = jnp.broadcast_to(m_ref[...] + jnp.log(l_ref[...]), out.shape)
            for j in range(g):
                o_ref[:, pl.ds(j * SLOT, SLOT)] = out[j].T
                lse_ref[:, pl.ds(j * SLOT, SLOT)] = lse[j].T

    q_spec, k_spec = _attn_specs(tile)
    tables = _attn_pairs(steps, True)
    return pl.pallas_call(
        body, name="attn_fwd",
        grid_spec=pltpu.PrefetchScalarGridSpec(
            num_scalar_prefetch=2, grid=(N_HEADS // g, tables[0].shape[0]),
            in_specs=[q_spec, k_spec, k_spec], out_specs=[q_spec, q_spec],
            scratch_shapes=[pltpu.VMEM((g, 1, tile), F32), pltpu.VMEM((g, 1, tile), F32), pltpu.VMEM((g, 1, tile), F32),
                            pltpu.VMEM((g, SLOT, tile), F32), pltpu.VMEM((g, tile, tile), F32), pltpu.VMEM((g, tile, tile), BF16)]),
        out_shape=[jax.ShapeDtypeStruct((t, N_HEADS * SLOT), F32)] * 2,
        compiler_params=pltpu.CompilerParams(dimension_semantics=("parallel", "arbitrary")),
    )(*tables, q, k, v)


def _attn_grad_scores(q, k, v, do, lse_ref, delta_ref, diagonal):
    g = ATT_GROUP
    p = jnp.exp(_scores(q, k, diagonal) - _stack_slots(lse_ref, g)[:, :, 0:1])
    dp = _nt(do, v)
    return p, p * (dp - _stack_slots(delta_ref, g)[:, :, 0:1]) * ATT_SCALE


def _attn_bwd(q, k, v, do, lse, delta):
    t = q.shape[0]
    tile = min(ATT_T, t)
    steps = t // tile
    g = ATT_GROUP

    def body(qt_ref, kt_ref, q_ref, k_ref, v_ref, do_ref, lse_ref, delta_ref, dq_ref, dk_ref, dv_ref, dk_acc, dv_acc):
        qi, ki = qt_ref[pl.program_id(1)], kt_ref[pl.program_id(1)]

        @pl.when(pl.program_id(1) == 0)
        def _():
            dq_ref[...] = jnp.zeros_like(dq_ref)

        def step(diagonal):
            qq, kk = _stack_slots(q_ref, g), _stack_slots(k_ref, g)
            do_b = _stack_slots(do_ref, g).astype(BF16)
            p, ds = _attn_grad_scores(qq, kk, _stack_slots(v_ref, g), do_b, lse_ref, delta_ref, diagonal)
            ds = ds.astype(BF16)
            dv_acc[...] += _tn(p.astype(BF16), do_b)
            dk_acc[...] += _tn(ds, qq)
            dq = _nn(ds, kk)
            rows = pl.ds(pl.multiple_of(qi * tile, tile), tile)
            for j in range(g):
                dq_ref[rows, pl.ds(j * SLOT, SLOT)] += dq[j]

        @pl.when(qi == ki)
        def _():
            dk_acc[...] = jnp.zeros_like(dk_acc)
            dv_acc[...] = jnp.zeros_like(dv_acc)
            step(True)

        @pl.when(qi > ki)
        def _():
            step(False)

        @pl.when(qi == steps - 1)
        def _():
            _unstack_slots(dk_ref, dk_acc[...])
            _unstack_slots(dv_ref, dv_acc[...])

    q_spec, k_spec = _attn_specs(tile)
    tables = _attn_pairs(steps, False)
    return pl.pallas_call(
        body, name="attn_bwd",
        grid_spec=pltpu.PrefetchScalarGridSpec(
            num_scalar_prefetch=2, grid=(N_HEADS // g, tables[0].shape[0]),
            in_specs=[q_spec, k_spec, k_spec, q_spec, q_spec, q_spec],
            out_specs=[pl.BlockSpec((t, g * SLOT), lambda h, p, qt, kt: (0, h)), k_spec, k_spec],
            scratch_shapes=[pltpu.VMEM((g, tile, SLOT), F32), pltpu.VMEM((g, tile, SLOT), F32)]),
        out_shape=[jax.ShapeDtypeStruct((t, N_HEADS * SLOT), F32)] * 3,
        compiler_params=pltpu.CompilerParams(dimension_semantics=("parallel", "arbitrary")),
    )(*tables, q, k, v, do, lse, delta)


def _shift_down(x, s):
    if s == 0:
        return x
    row = lax.broadcasted_iota(jnp.int32, x.shape, 0)
    return jnp.where(row >= s, pltpu.roll(x, s, 0), 0.0)


def _shift_up(x, s):
    if s == 0:
        return x
    n = x.shape[0]
    row = lax.broadcasted_iota(jnp.int32, x.shape, 0)
    return jnp.where(row < n - s, pltpu.roll(x, n - s, 0), 0.0)


def _l2norm(x):
    return x * lax.rsqrt(jnp.sum(x * x, axis=-1, keepdims=True) + EPS)


def _conv_pre(x, w):
    y = w[GDN_CONV - 1:GDN_CONV, :] * x
    for s in range(1, GDN_CONV):
        y = y + w[GDN_CONV - 1 - s:GDN_CONV - s, :] * _shift_down(x, s)
    return y


def _gdn_conv_fwd(x, w):
    t, width = x.shape

    def body(x_ref, w_ref, o_ref):
        act = _silu(_conv_pre(x_ref[...], w_ref[...]))
        normed = pl.program_id(0) < 2 * N_HEADS
        o_ref[...] = jnp.where(normed, _l2norm(act), act)

    return pl.pallas_call(
        body, name="gdn_conv_fwd",
        grid=(width // SLOT,),
        in_specs=[pl.BlockSpec((t, SLOT), lambda j: (0, j)), pl.BlockSpec((GDN_CONV, SLOT), lambda j: (0, j))],
        out_specs=pl.BlockSpec((t, SLOT), lambda j: (0, j)),
        out_shape=jax.ShapeDtypeStruct((t, width), F32),
        compiler_params=pltpu.CompilerParams(dimension_semantics=("parallel",)),
    )(x, w)


def _gdn_conv_bwd(x, w, dout):
    t, width = x.shape

    def body(x_ref, w_ref, do_ref, dx_ref, dw_ref):
        xv, wv = x_ref[...], w_ref[...]
        y = _conv_pre(xv, wv)
        sig = _sigmoid(y)
        act = y * sig
        _, pull = jax.vjp(_l2norm, act)
        normed = pl.program_id(0) < 2 * N_HEADS
        dact = jnp.where(normed, pull(do_ref[0])[0], do_ref[0])
        dy = dact * (sig * (1.0 + y * (1.0 - sig)))
        dx = wv[GDN_CONV - 1:GDN_CONV, :] * dy
        for s in range(1, GDN_CONV):
            dx = dx + wv[GDN_CONV - 1 - s:GDN_CONV - s, :] * _shift_up(dy, s)
        dx_ref[...] = dx.astype(BF16)
        for s in range(GDN_CONV):
            dw_ref[GDN_CONV - 1 - s:GDN_CONV - s, :] = jnp.sum(dy * _shift_down(xv, s), axis=0, keepdims=True)

    col = pl.BlockSpec((t, SLOT), lambda j: (0, j))
    tap = pl.BlockSpec((GDN_CONV, SLOT), lambda j: (0, j))
    return pl.pallas_call(
        body, name="gdn_conv_bwd",
        grid=(width // SLOT,),
        in_specs=[col, tap, pl.BlockSpec((1, t, SLOT), lambda j: (j // N_HEADS, 0, j % N_HEADS))],
        out_specs=[col, tap],
        out_shape=[jax.ShapeDtypeStruct((t, width), BF16), jax.ShapeDtypeStruct((GDN_CONV, width), F32)],
        compiler_params=pltpu.CompilerParams(dimension_semantics=("parallel",)),
    )(x, w, dout)


def _softplus(x):
    e = jnp.exp(-jnp.abs(x))
    u = 1.0 + e
    log1p = jnp.where(u == 1.0, e, jnp.log(u) * e / jnp.where(u == 1.0, 1.0, u - 1.0))
    return jnp.maximum(x, 0.0) + log1p


def _chunk_running_sum(x, reverse=False):
    tm = x.shape[0]
    at = lax.broadcasted_iota(jnp.int32, x.shape, 0) % GDN_CHUNK
    step = 1
    while step < GDN_CHUNK:
        if reverse:
            x = x + jnp.where(at < GDN_CHUNK - step, pltpu.roll(x, tm - step, 0), 0.0)
        else:
            x = x + jnp.where(at >= step, pltpu.roll(x, step, 0), 0.0)
        step *= 2
    return x


def _gates_fwd(ab, a_log, dt_bias):
    def fn(rows, consts):
        (abv,), (alog, dtb) = rows, consts
        g = _chunk_running_sum(-jnp.exp(alog) * _softplus(abv + dtb))
        beta = _sigmoid(abv)
        shape = (abv.shape[0], SLOT)
        g_slots = [jnp.broadcast_to(g[:, h:h + 1], shape) for h in range(N_HEADS)]
        b_slots = [jnp.broadcast_to(beta[:, N_HEADS + h:N_HEADS + h + 1], shape) for h in range(N_HEADS)]
        return [jnp.concatenate(g_slots, axis=1), jnp.concatenate(b_slots, axis=1)], []

    width = N_HEADS * SLOT
    return _rowwise("gdn_gates_fwd", fn, [ab], [a_log, dt_bias], [(width, F32), (width, F32)])


def _gates_bwd(ab, a_log, dt_bias, dg, dbeta):
    def fn(rows, consts):
        (abv, dgv, dbv), (alog, dtb) = rows, consts
        lane = lax.broadcasted_iota(jnp.int32, abv.shape, 1)
        dg_tok = jnp.zeros_like(abv)
        db_tok = jnp.zeros_like(abv)
        for h in range(N_HEADS):
            dg_tok = dg_tok + jnp.where(lane == h, jnp.sum(dgv[:, h * SLOT:(h + 1) * SLOT], axis=1, keepdims=True), 0.0)
            db_tok = db_tok + jnp.where(lane == N_HEADS + h, jnp.sum(dbv[:, h * SLOT:(h + 1) * SLOT], axis=1, keepdims=True), 0.0)
        dg_tok = _chunk_running_sum(dg_tok, reverse=True)
        xa = abv + dtb
        g = -jnp.exp(alog) * _softplus(xa)
        da = dg_tok * (-jnp.exp(alog)) * _sigmoid(xa)
        beta = _sigmoid(abv)
        dab = jnp.where(lane < N_HEADS, da, db_tok * beta * (1.0 - beta))
        dab = jnp.where(lane < 2 * N_HEADS, dab, 0.0)
        d_alog = jnp.sum(jnp.where(lane < N_HEADS, dg_tok * g, 0.0), axis=0, keepdims=True)
        d_dtb = jnp.sum(jnp.where(lane < N_HEADS, da, 0.0), axis=0, keepdims=True)
        return [dab], [d_alog, d_dtb]

    return _rowwise("gdn_gates_bwd", fn, [ab, dg, dbeta], [a_log, dt_bias], [(SLOT, F32)], sums=[SLOT, SLOT])


ROPE_HALF = MLA_ROPE // 2


def _rope_tables(positions):
    freqs = ROPE_THETA ** (-jnp.arange(ROPE_HALF, dtype=F32) / ROPE_HALF)
    ang = positions.astype(F32)[:, None] * freqs
    cos, sin = jnp.cos(ang), jnp.sin(ang)
    t = positions.shape[0]
    ones, zeros = jnp.ones((t, MLA_NOPE), F32), jnp.zeros((t, MLA_NOPE), F32)
    tail = jnp.zeros((t, SLOT - MLA_NOPE - MLA_ROPE), F32)
    half0 = jnp.zeros((t, ROPE_HALF), F32)
    same = jnp.concatenate([ones, cos, cos, tail], axis=1)
    from_low = jnp.concatenate([zeros, half0, sin, tail], axis=1)
    from_high = jnp.concatenate([zeros, -sin, half0, tail], axis=1)
    return same, from_low, from_high


def _rope(x, tabs):
    same, from_low, from_high = tabs
    width = x.shape[1]
    return x * same + pltpu.roll(x, ROPE_HALF, 1) * from_low + pltpu.roll(x, width - ROPE_HALF, 1) * from_high


def _rope_transposed(dy, tabs):
    same, from_low, from_high = tabs
    width = dy.shape[1]
    return dy * same + pltpu.roll(dy * from_low, width - ROPE_HALF, 1) + pltpu.roll(dy * from_high, ROPE_HALF, 1)


def _tile_slots(tab):
    return jnp.concatenate([tab] * N_HEADS, axis=1)


A_WIDTH = MLA_Q_RANK + MLA_KV_RANK + 2 * SLOT
A_KPE = MLA_Q_RANK + MLA_KV_RANK
A_AB = A_KPE + SLOT
WIDE = N_HEADS * SLOT


def _mla_pre_fwd(proj_a, tabs, g_q, g_kv):
    def fn(rows, consts):
        pa, *tb = rows
        gq, gkv = consts
        return [_rms(pa[:, :MLA_Q_RANK], gq, MLA_Q_RANK), _rms(pa[:, MLA_Q_RANK:A_KPE], gkv, MLA_KV_RANK),
                _rope(pa[:, A_KPE:A_AB], tb)], []

    return _rowwise("mla_pre_fwd", fn, [proj_a, *tabs], [g_q, g_kv], [(MLA_Q_RANK, BF16), (MLA_KV_RANK, BF16), (SLOT, F32)])


def _mla_pre_bwd(proj_a, tabs, g_q, g_kv, dcqn, dckvn, dkpe, dab):
    def fn(rows, consts):
        pa, t0, t1, t2, dq, dkv, dk, da = rows
        gq, gkv = consts
        _, pull_q = jax.vjp(lambda x, g: _rms(x, g, MLA_Q_RANK), pa[:, :MLA_Q_RANK], gq)
        _, pull_kv = jax.vjp(lambda x, g: _rms(x, g, MLA_KV_RANK), pa[:, MLA_Q_RANK:A_KPE], gkv)
        dcq, dgq = pull_q(dq)
        dckv, dgkv = pull_kv(dkv)
        return [jnp.concatenate([dcq, dckv, _rope_transposed(dk, (t0, t1, t2)), da], axis=1)], [dgq, dgkv]

    return _rowwise("mla_pre_bwd", fn, [proj_a, *tabs, dcqn, dckvn, dkpe, dab], [g_q, g_kv], [(A_WIDTH, BF16)],
                    sums=[MLA_Q_RANK, MLA_KV_RANK])


def _mla_qkv_fwd(q_p, kv_p, kpe, tabs):
    def fn(rows, consts):
        qv, kvv, kp, *tb = rows
        q = _rope(qv, [_tile_slots(x) for x in tb])
        k = kvv[:, :WIDE] + _tile_slots(kp)
        return [q, k, kvv[:, WIDE:]], []

    return _rowwise("mla_qkv_fwd", fn, [q_p, kv_p, kpe, *tabs], [], [(WIDE, BF16)] * 3)


def _mla_qkv_bwd(dq, dk, dv, tabs):
    def fn(rows, consts):
        dqv, dkv, dvv, *tb = rows
        dkpe = dkv[:, :SLOT]
        for h in range(1, N_HEADS):
            dkpe = dkpe + dkv[:, h * SLOT:(h + 1) * SLOT]
        return [_rope_transposed(dqv, [_tile_slots(x) for x in tb]), jnp.concatenate([dkv, dvv], axis=1), dkpe], []

    return _rowwise("mla_qkv_bwd", fn, [dq, dk, dv, *tabs], [], [(WIDE, BF16), (2 * WIDE, BF16), (SLOT, F32)])


def _slot_sum(x):
    parts = [jnp.broadcast_to(jnp.sum(x[:, h * SLOT:(h + 1) * SLOT], axis=1, keepdims=True), (x.shape[0], SLOT))
             for h in range(N_HEADS)]
    return jnp.concatenate(parts, axis=1)


def _mix_join(o_mla, o_gdn, gate, g_mla, g_gdn):
    mla = _rms(o_mla, g_mla, N_HEADS * MLA_V)
    gdn = o_gdn * lax.rsqrt(_slot_sum(o_gdn * o_gdn) * (1.0 / GDN_D) + EPS) * g_gdn * _silu(gate)
    return mla, gdn


def _mix_join_fwd(o_mla, o_gdn, gate, g_mla, g_gdn):
    def fn(rows, consts):
        return [jnp.concatenate(_mix_join(*rows, *consts), axis=1)], []

    return _rowwise("mix_join_fwd", fn, [o_mla, o_gdn, gate], [g_mla, g_gdn], [(2 * WIDE, BF16)])


def _mix_join_bwd(o_mla, o_gdn, gate, g_mla, g_gdn, dcat):
    def fn(rows, consts):
        om, og, gt, dc = rows
        gm, gg = consts
        _, pull = jax.vjp(lambda x, g: _rms(x, g, N_HEADS * MLA_V), om, gm)
        dom, dgm = pull(dc[:, :WIDE])
        dy = dc[:, WIDE:]
        r = lax.rsqrt(_slot_sum(og * og) * (1.0 / GDN_D) + EPS)
        sig = _sigmoid(gt)
        normed = og * r
        dn = dy * gg * (gt * sig)
        dog = r * dn - normed * (r * r) * _slot_sum(dn * og) * (1.0 / GDN_D)
        dgt = dy * normed * gg * (sig * (1.0 + gt * (1.0 - sig)))
        dgg = jnp.sum(dy * normed * (gt * sig), axis=0, keepdims=True)
        return [dom, _slot_sum(dom * om), dog, dgt], [dgm, dgg]

    return _rowwise("mix_join_bwd", fn, [o_mla, o_gdn, gate, dcat], [g_mla, g_gdn],
                    [(WIDE, F32), (WIDE, F32), (WIDE, F32), (WIDE, BF16)], sums=[WIDE, WIDE])


def _norm_residual_fwd(name, x, h, g, out_dtypes):
    dm = x.shape[1]

    def fn(rows, consts):
        y = rows[0] + _rms(rows[1], consts[0], dm)
        return [y] + [_rms(y, gg, dm) for gg in consts[1:]], []

    return _rowwise(name, fn, [x, h], list(g), [(dm, dt) for dt in out_dtypes])


def _norm_residual_bwd(name, h, g, dy):
    dm = h.shape[1]

    def fn(rows, consts):
        _, pull = jax.vjp(lambda hv, gv: _rms(hv, gv, dm), rows[0], consts[0])
        dh, dg = pull(rows[1])
        return [dh], [dg]

    return _rowwise(name, fn, [h, dy], [g], [(dm, BF16)], sums=[dm])


def _norm_bwd_add(name, x, g, dns, dy):
    dm = x.shape[1]

    def fn(rows, consts):
        xv, dyv, *parts = rows
        dn = parts[0]
        for p in parts[1:]:
            dn = dn + p
        _, pull = jax.vjp(lambda a, gv: _rms(a, gv, dm), xv, consts[0])
        dx, dg = pull(dn)
        return [dyv + dx], [dg]

    return _rowwise(name, fn, [x, dy, *dns], [g], [(dm, F32)], sums=[dm])


def _loss_fwd(y, target):
    dm = y.shape[1]

    def fn(rows, consts):
        err = rows[0] - rows[1]
        sq = err * err
        lanes = sq[:, :SLOT]
        for j in range(1, dm // SLOT):
            lanes = lanes + sq[:, j * SLOT:(j + 1) * SLOT]
        return [err * (1.0 / dm)], [jnp.sum(lanes, axis=0, keepdims=True) * (0.5 / dm)]

    return _rowwise("loss", fn, [y, target], [], [(dm, F32)], sums=[SLOT])


def _norm_fwd(name, x, g):
    dm = x.shape[1]
    return _rowwise(name, lambda rows, consts: ([_rms(rows[0], consts[0], dm)], []), [x], [g], [(dm, BF16)])[0]


W_IN_CUTS = (0, 256, 384, 416, 1952, 1960, 1968, 2480)


def _heads_out(w, per_head, axis=-1):
    axis = axis % w.ndim
    shape = w.shape
    n = shape[axis] // per_head
    w = w.reshape(shape[:axis] + (n, per_head) + shape[axis + 1:])
    pad = [(0, 0)] * w.ndim
    pad[axis + 1] = (0, SLOT - per_head)
    return jnp.pad(w, pad).reshape(shape[:axis] + (n * SLOT,) + shape[axis + 1:])


def _heads_in(w, per_head, axis=-1):
    axis = axis % w.ndim
    shape = w.shape
    n = shape[axis] // SLOT
    w = w.reshape(shape[:axis] + (n, SLOT) + shape[axis + 1:])
    w = lax.slice_in_dim(w, 0, per_head, axis=axis + 1)
    return w.reshape(shape[:axis] + (n * per_head,) + shape[axis + 1:])


def _pad_lanes(v, lo, width=SLOT):
    return jnp.pad(v, [(0, 0)] * (v.ndim - 1) + [(lo, width - lo - v.shape[-1])])


def _pad_rows(v, lo, rows=SLOT):
    return jnp.pad(v, [(lo, rows - lo - v.shape[0])] + [(0, 0)] * (v.ndim - 1))


def _layout_weights(w):
    c = W_IN_CUTS
    w_in = w["w_in_t"]
    p = {}
    p["w_a"] = jnp.concatenate([w_in[c[0]:c[2]], _pad_rows(w_in[c[2]:c[3]], MLA_NOPE), _pad_rows(w_in[c[4]:c[6]], 0)], axis=0)
    p["w_qkv"] = _heads_out(w_in[c[3]:c[4]], GDN_D, axis=0)
    p["w_gate"] = _heads_out(w_in[c[6]:c[7]], GDN_D, axis=0)
    p["w_uq"] = _heads_out(w["uq_t"], MLA_NOPE + MLA_ROPE, axis=0)
    ukv = w["ukv_t"].reshape(N_HEADS, MLA_NOPE + MLA_V, MLA_KV_RANK)
    p["w_kv"] = jnp.concatenate([_heads_out(ukv[:, :MLA_NOPE].reshape(-1, MLA_KV_RANK), MLA_NOPE, axis=0),
                                 _heads_out(ukv[:, MLA_NOPE:].reshape(-1, MLA_KV_RANK), MLA_V, axis=0)], axis=0)
    p["conv"] = _heads_out(w["gdn_conv_w"], GDN_D)
    p["g_mla_out"] = _heads_out(w["mla_out_g"], MLA_V)
    p["g_gdn"] = jnp.tile(_pad_lanes(w["gdn_norm_g"], 0), (1, N_HEADS))
    p["a_log"] = _pad_lanes(w["gdn_a_log"], 0)
    p["dt_bias"] = _pad_lanes(w["gdn_dt_bias"], 0)
    return p


def _unlayout_grads(d):
    c = W_IN_CUTS
    g = {}
    da = d["w_a"]
    kpe0 = A_KPE + MLA_NOPE
    g["w_in_t"] = jnp.concatenate([da[:A_KPE], da[kpe0:kpe0 + MLA_ROPE], _heads_in(d["w_qkv"], GDN_D, axis=0),
                                   da[A_AB:A_AB + 2 * N_HEADS], _heads_in(d["w_gate"], GDN_D, axis=0)], axis=0)
    assert g["w_in_t"].shape[0] == c[-1]
    g["uq_t"] = _heads_in(d["w_uq"], MLA_NOPE + MLA_ROPE, axis=0)
    dk = _heads_in(d["w_kv"][:WIDE], MLA_NOPE, axis=0).reshape(N_HEADS, MLA_NOPE, MLA_KV_RANK)
    dv = _heads_in(d["w_kv"][WIDE:], MLA_V, axis=0).reshape(N_HEADS, MLA_V, MLA_KV_RANK)
    g["ukv_t"] = jnp.concatenate([dk, dv], axis=1).reshape(-1, MLA_KV_RANK)
    g["w_out"] = _heads_in(d["w_out"], GDN_D, axis=0)
    g["gdn_conv_w"] = _heads_in(d["conv"], GDN_D)
    g["mla_out_g"] = _heads_in(d["g_mla_out"], MLA_V)
    g["gdn_norm_g"] = jnp.sum(d["g_gdn"].reshape(N_HEADS, SLOT), axis=0, keepdims=True)[:, :GDN_D]
    g["gdn_a_log"] = d["a_log"][:, :N_HEADS]
    g["gdn_dt_bias"] = d["dt_bias"][:, :N_HEADS]
    return g


def _weight_grad(name, cots, acts, out_dtype=F32, tm=1024, tn=1024, tk=2048, carry=None):
    return _matmul(name, cots, acts, "tn", out_dtype=out_dtype, tm=tm, tn=tn, tk=tk, carry=carry)


def _by_device(a):
    return a.astype(BF16).reshape((N_DEV, a.shape[0] // N_DEV) + a.shape[1:])


def _rows_of(blocks):
    return blocks.reshape((-1,) + blocks.shape[2:])


def _local_step(x, positions, target, w, mid, late):
    tabs = _rope_tables(positions)

    (h1, x1, hg1, hu1), gathered = _ffn_fwd("ffn1_fwd", x, w["ffn1_pre_g"], w["ffn1"], 0, w["ffn1_post_g"], carry=mid)
    w = dict(w, w_in_t=_rows_of(gathered[0]), uq_t=_rows_of(gathered[1]), ukv_t=_rows_of(gathered[2]))
    p = _layout_weights(w)
    hn = _norm_fwd("mix_pre_norm", x1, w["mix_pre_g"])
    proj_a = _matmul("proj_a", hn, p["w_a"], "nt")
    proj_qkv = _matmul("proj_qkv", hn, p["w_qkv"], "nt")
    proj_gate = _matmul("proj_gate", hn, p["w_gate"], "nt")
    cqn, ckvn, kpe = _mla_pre_fwd(proj_a, tabs, w["mla_q_norm_g"], w["mla_kv_norm_g"])
    q_p = _matmul("mla_q", cqn, p["w_uq"], "nt")
    kv_p = _matmul("mla_kv", ckvn, p["w_kv"], "nt")
    q, k, v = _mla_qkv_fwd(q_p, kv_p, kpe, tabs)
    o_mla, lse = _attn_fwd(q, k, v)
    ab = (proj_a, SLOT, A_AB // SLOT)
    qkv_n = _gdn_conv_fwd(proj_qkv, p["conv"])
    gb, bb = _gates_fwd(ab, p["a_log"], p["dt_bias"])
    (o_gdn, keep), (ffn2, w_out) = _gdn_fwd(qkv_n, gb, bb, carry=late)
    p["w_out"] = _heads_out(_rows_of(w_out), GDN_D, axis=0)
    cat = _mix_join_fwd(o_mla, o_gdn, proj_gate, p["g_mla_out"], p["g_gdn"])[0]
    mixed = _matmul("mix_out", cat, p["w_out"], "nn")
    x2 = _norm_residual_fwd("mix_post", x1, mixed, [w["mix_post_g"]], [F32])[0]
    (h2, y, hg2, hu2), _ = _ffn_fwd("ffn2_fwd", x2, w["ffn2_pre_g"], ffn2, 0, w["ffn2_post_g"])
    dy, loss_lanes = _loss_fwd(y, target)

    g = {}
    (dx2, xn2, dh2, a2, dhg2, dhu2, g["ffn2_pre_g"], g["ffn2_post_g"]), _ = _ffn_bwd(
        "ffn2_bwd", x2, h2, hg2, hu2, dy, w["ffn2_pre_g"], ffn2, 0, w["ffn2_post_g"])
    ffn2_grads = _Scatter([_by_device(_weight_grad("ffn2_dw_gate", dhg2, xn2, BF16, tm=1408)),
                           _by_device(_weight_grad("ffn2_dw_up", dhu2, xn2, BF16, tm=1408)),
                           _by_device(_weight_grad("ffn2_dw_down", a2, dh2, BF16, tm=1408))])
    dmixed, g["mix_post_g"] = _norm_residual_bwd("mix_post_bwd", mixed, w["mix_post_g"], dx2)
    dcat = _matmul("mix_out_dx", dmixed, p["w_out"], "nt")
    d = {}
    d["w_out"] = _weight_grad("mix_out_dw", cat, dmixed)
    do_mla, delta, do_gdn, dgate, d["g_mla_out"], d["g_gdn"] = _mix_join_bwd(o_mla, o_gdn, proj_gate, p["g_mla_out"], p["g_gdn"], dcat)
    dq, dk, dv = _attn_bwd(q, k, v, do_mla, lse, delta)
    dq_p, dkv_p, dkpe = _mla_qkv_bwd(dq, dk, dv, tabs)
    dcqn = _matmul("mla_q_dx", dq_p, p["w_uq"], "nn")
    d["w_uq"] = _weight_grad("mla_q_dw", dq_p, cqn)
    dckvn = _matmul("mla_kv_dx", dkv_p, p["w_kv"], "nn")
    d["w_kv"] = _weight_grad("mla_kv_dw", dkv_p, ckvn)
    (dqkv_n, dgb, dbb), landed_ffn2 = _gdn_bwd(qkv_n, gb, bb, keep, do_gdn, carry=ffn2_grads)
    dab, d["a_log"], d["dt_bias"] = _gates_bwd(ab, p["a_log"], p["dt_bias"], dgb, dbb)
    dproj_qkv, d["conv"] = _gdn_conv_bwd(proj_qkv, p["conv"], dqkv_n)
    dproj_a, g["mla_q_norm_g"], g["mla_kv_norm_g"] = _mla_pre_bwd(
        proj_a, tabs, w["mla_q_norm_g"], w["mla_kv_norm_g"], dcqn, dckvn, dkpe, dab)
    dhn = [_matmul("proj_a_dx", dproj_a, p["w_a"], "nn"), _matmul("proj_qkv_dx", dproj_qkv, p["w_qkv"], "nn"),
           _matmul("proj_gate_dx", dgate, p["w_gate"], "nn")]
    d["w_a"] = _weight_grad("proj_a_dw", dproj_a, hn, tm=640)
    d["w_qkv"] = _weight_grad("proj_qkv_dw", dproj_qkv, hn)
    d["w_gate"] = _weight_grad("proj_gate_dw", dgate, hn)
    dx1, g["mix_pre_g"] = _norm_bwd_add("mix_pre_bwd", x1, w["mix_pre_g"], dhn, dx2)
    g.update(_unlayout_grads(d))
    others = [t for t, _ in OTHER.values()]
    (dx, xn1, dh1, a1, dhg1, dhu1, g["ffn1_pre_g"], g["ffn1_post_g"]), landed_others = _ffn_bwd(
        "ffn1_bwd", x, h1, hg1, hu1, dx1, w["ffn1_pre_g"], w["ffn1"], 0, w["ffn1_post_g"], carry=_Scatter([_by_device(g.pop(t)) for t in others]))
    dw_down = _weight_grad("ffn1_dw_down", a1, dh1, BF16, tm=1408)
    dw_gate, (landed_down,) = _weight_grad("ffn1_dw_gate", dhg1, xn1, BF16, tm=1408, carry=_Scatter([_by_device(dw_down)]))
    dw_up, (landed_gate,) = _weight_grad("ffn1_dw_up", dhu1, xn1, BF16, tm=1408, carry=_Scatter([_by_device(dw_gate)]))
    (landed_up,) = _exchange("scatter_last", _Scatter([_by_device(dw_up)]))
    landed = dict(zip(list(FFN_NAMES) + list(OTHER),
                      [landed_gate, landed_up, landed_down] + list(landed_ffn2) + list(landed_others)))
    return loss_lanes, dx, g, landed


MESH_AXES = ("x", "y", "c")
N_LINKS = N_DEV - 1


def _place():
    return tuple(lax.axis_index(a) for a in MESH_AXES)


def _block_of(dev):
    x, y, c = dev
    return 4 * x + 2 * y + c


def _remote_copy(src, dst, sems, k, to):
    send_sems, recv_sems = sems
    return pltpu.make_async_remote_copy(src_ref=src, dst_ref=dst, send_sem=send_sems.at[k], recv_sem=recv_sems.at[k],
                                        device_id=to, device_id_type=pl.DeviceIdType.MESH)


class _Exchange:
    def __init__(self, arrays):
        self.arrays = list(arrays)
        self.n = len(self.arrays)
        self.specs = [pl.BlockSpec(memory_space=pl.ANY)] * self.n
        self.scratch = [pltpu.SemaphoreType.DMA((self.n * N_LINKS,)), pltpu.SemaphoreType.DMA((self.n * N_LINKS,)),
                        pltpu.SemaphoreType.DMA((self.n,))]

    def split(self, refs):
        n = self.n
        return refs[:n], refs[n:2 * n], (refs[2 * n], refs[2 * n + 1]), refs[2 * n + 2]


class _Gather(_Exchange):
    def out_shape(self):
        return [jax.ShapeDtypeStruct((N_DEV,) + a.shape, a.dtype) for a in self.arrays]

    def _plan(self, ins, outs, sems, local_sems):
        x, y, c = _place()
        me, sibling = (x, y, c), (x, y, 1 - c)
        chips = [(1 - x, y), (x, 1 - y), (1 - x, 1 - y)]

        def copy(a, k, block, to, mine=False):
            src = ins[a] if mine else outs[a].at[_block_of(block)]
            return _remote_copy(src, outs[a].at[_block_of(block)], sems, a * N_LINKS + k, to)

        local = [pltpu.make_async_copy(ins[a], outs[a].at[_block_of(me)], local_sems.at[a]) for a in range(self.n)]
        first = []
        for a in range(self.n):
            first.append(copy(a, 0, me, sibling, mine=True))
            first += [copy(a, 1 + j, me, (*chip, c), mine=True) for j, chip in enumerate(chips)]
        return me, sibling, chips, c, copy, local, first

    def start(self, ins, outs, sems, local_sems):
        *_, local, first = self._plan(ins, outs, sems, local_sems)
        for cp in local + first:
            cp.start()

    def finish(self, ins, outs, sems, local_sems):
        me, sibling, chips, c, copy, local, first = self._plan(ins, outs, sems, local_sems)
        passed = []
        for j, chip in enumerate(chips):
            for a in range(self.n):
                copy(a, 1 + j, (*chip, c), me).wait_recv()
                passed.append(copy(a, 4 + j, (*chip, c), sibling))
                passed[-1].start()
        for a in range(self.n):
            copy(a, 0, sibling, me).wait_recv()
            for j, chip in enumerate(chips):
                copy(a, 4 + j, (*chip, 1 - c), me).wait_recv()
        for cp in first + passed:
            cp.wait_send()
        for cp in local:
            cp.wait()


class _Scatter(_Exchange):
    def out_shape(self):
        return [jax.ShapeDtypeStruct(a.shape, a.dtype) for a in self.arrays]

    def _plan(self, ins, outs, sems, local_sems):
        x, y, c = _place()
        me = _block_of((x, y, c))

        def peer(r):
            return (1 - x if r & 4 else x, 1 - y if r & 2 else y, 1 - c if r & 1 else c)

        local = [pltpu.make_async_copy(ins[a].at[me], outs[a].at[me], local_sems.at[a]) for a in range(self.n)]
        sends = [_remote_copy(ins[a].at[_block_of(peer(r))], outs[a].at[me], sems, a * N_LINKS + r - 1, peer(r))
                 for a in range(self.n) for r in range(1, N_DEV)]
        arrivals = [_remote_copy(ins[a].at[me], outs[a].at[_block_of(peer(r))], sems, a * N_LINKS + r - 1, peer(r))
                    for a in range(self.n) for r in range(1, N_DEV)]
        return local, sends, arrivals

    def start(self, ins, outs, sems, local_sems):
        local, sends, _ = self._plan(ins, outs, sems, local_sems)
        for cp in local + sends:
            cp.start()

    def finish(self, ins, outs, sems, local_sems):
        local, sends, arrivals = self._plan(ins, outs, sems, local_sems)
        for cp in arrivals:
            cp.wait_recv()
        for cp in sends:
            cp.wait_send()
        for cp in local:
            cp.wait()


def _exchange(name, plan):
    def body(*refs):
        parts = plan.split(refs)
        plan.start(*parts)
        plan.finish(*parts)

    return pl.pallas_call(
        body, name=name,
        in_specs=plan.specs,
        out_specs=plan.specs,
        out_shape=plan.out_shape(),
        scratch_shapes=plan.scratch,
    )(*plan.arrays)


def _call_carrying(body, plan, operands, *, name, grid, in_specs, out_specs, out_shape, scratch_shapes, compiler_params):
    if plan is None:
        outs = pl.pallas_call(body, name=name, grid=grid, in_specs=in_specs, out_specs=out_specs, out_shape=out_shape,
                              scratch_shapes=scratch_shapes, compiler_params=compiler_params)(*operands)
        return outs, []
    n_i, n_o, n_s, k = len(in_specs), len(out_specs), len(scratch_shapes), plan.n

    def whole(*refs):
        cut = [n_i, n_i + k, n_i + k + n_o, n_i + 2 * k + n_o, n_i + 2 * k + n_o + n_s]
        own_in, ex_in, own_out, ex_out, own_scr, ex_scr = (refs[a:b] for a, b in zip([0] + cut, cut + [len(refs)]))
        parts = plan.split(ex_in + ex_out + ex_scr)
        first = last = True
        for axis, size in enumerate(grid):
            first = first & (pl.program_id(axis) == 0)
            last = last & (pl.program_id(axis) == size - 1)

        @pl.when(first)
        def _():
            plan.start(*parts)

        body(*own_in, *own_out, *own_scr)

        @pl.when(last)
        def _():
            plan.finish(*parts)

    outs = pl.pallas_call(
        whole, name=name, grid=grid,
        in_specs=list(in_specs) + plan.specs, out_specs=list(out_specs) + plan.specs,
        out_shape=list(out_shape) + plan.out_shape(), scratch_shapes=list(scratch_shapes) + plan.scratch,
        compiler_params=compiler_params,
    )(*operands, *plan.arrays)
    return outs[:n_o], outs[n_o:]


def _row_tile(rows, target=256):
    best = rows
    for cand in range(16, min(rows, target) + 1, 16):
        if rows % cand == 0:
            best = cand
    return best


def _sum_blocks(name, blocks):
    rows, width = blocks.shape[-2:]
    tm = _row_tile(rows)

    def body(x_ref, o_ref):
        acc = x_ref[0].astype(F32)
        for d in range(1, N_DEV):
            acc = acc + x_ref[d].astype(F32)
        o_ref[...] = acc

    return pl.pallas_call(
        body, name=name,
        grid=(rows // tm,),
        in_specs=[pl.BlockSpec((N_DEV, tm, width), lambda i: (0, i, 0))],
        out_specs=pl.BlockSpec((tm, width), lambda i: (i, 0)),
        out_shape=jax.ShapeDtypeStruct((rows, width), F32),
        compiler_params=pltpu.CompilerParams(dimension_semantics=("parallel",)),
    )(blocks)


def _all_reduce_small(name, vec):
    rows, width = vec.shape

    def body(x_ref, o_ref, all_ref, send_sems, recv_sems):
        x, y, c = _place()
        me = _block_of((x, y, c))
        all_ref[me] = x_ref[...]

        def peer(r):
            return (1 - x if r & 4 else x, 1 - y if r & 2 else y, 1 - c if r & 1 else c)

        def copy(r, block):
            return _remote_copy(x_ref, all_ref.at[block], (send_sems, recv_sems), r - 1, peer(r))

        sends = [copy(r, me) for r in range(1, N_DEV)]
        for cp in sends:
            cp.start()
        for r in range(1, N_DEV):
            copy(r, _block_of(peer(r))).wait_recv()
        for cp in sends:
            cp.wait_send()
        acc = all_ref[0]
        for d in range(1, N_DEV):
            acc = acc + all_ref[d]
        o_ref[...] = acc

    return pl.pallas_call(
        body, name=name,
        in_specs=[pl.BlockSpec(memory_space=pltpu.VMEM)],
        out_specs=pl.BlockSpec(memory_space=pltpu.VMEM),
        out_shape=jax.ShapeDtypeStruct((rows, width), F32),
        scratch_shapes=[pltpu.VMEM((N_DEV, rows, width), F32), pltpu.SemaphoreType.DMA((N_LINKS,)), pltpu.SemaphoreType.DMA((N_LINKS,))],
    )(vec)


def _adamw(name, w, g, m, v):
    def fn(rows, consts):
        wv, gv, mv, vv = rows
        m2 = ADAM_B1 * mv + (1.0 - ADAM_B1) * gv
        v2 = ADAM_B2 * vv + (1.0 - ADAM_B2) * jnp.square(gv)
        m_hat = m2 / (1.0 - ADAM_B1 ** ADAM_STEP)
        v_hat = v2 / (1.0 - ADAM_B2 ** ADAM_STEP)
        return [-ADAM_LR * (m_hat / (jnp.sqrt(v_hat) + ADAM_EPS) + ADAM_WD * wv), m2, v2], []

    return _rowwise(name, fn, [w, g, m, v], [], [(w.shape[1], F32)] * 3, tm=_row_tile(w.shape[0]))


ROW = 1024
FFN_NAMES = ("ffn1_w_gate", "ffn1_w_up", "ffn1_w_down", "ffn2_w_gate", "ffn2_w_up", "ffn2_w_down")
OTHER = {"w_in": ("w_in_t", True), "mla_w_uq": ("uq_t", True), "mla_w_ukv": ("ukv_t", True), "w_out": ("w_out", False)}
BY_COLUMNS = ("ffn1_w_gate", "ffn1_w_up", "ffn2_w_gate", "ffn2_w_up", "w_in", "mla_w_uq", "mla_w_ukv")
SMALL = {
    "ffn1_pre_g": (1024, 1024), "ffn1_post_g": (1024, 1024), "mix_pre_g": (1024, 1024), "mla_q_norm_g": (256, 256),
    "mla_kv_norm_g": (128, 128), "mla_out_g": (512, 512), "gdn_a_log": (8, 128), "gdn_dt_bias": (8, 128),
    "gdn_norm_g": (64, 128), "mix_post_g": (1024, 1024), "ffn2_pre_g": (1024, 1024), "ffn2_post_g": (1024, 1024),
}
CONV_SHAPE = (GDN_CONV, 3 * N_HEADS * GDN_D)
CONV_SHARD = (GDN_CONV, CONV_SHAPE[1] // N_DEV)
CONV_LANES = CONV_SHAPE[0] * CONV_SHAPE[1]
SMALL_ROWS = 8
REDUCE_ROWS = 16


def _pack_small(vecs, conv, rows):
    parts = [_pad_lanes(vecs[n].reshape(1, -1), 0, r) for n, (_, r) in SMALL.items()]
    parts.append(conv.reshape(1, -1))
    flat = jnp.concatenate(parts, axis=1)
    return _pad_lanes(flat, 0, rows * ROW).reshape(rows, ROW)


def _unpack_small(buf):
    flat = buf.reshape(1, -1)
    out, at = {}, 0
    for n, (w, r) in SMALL.items():
        out[n] = flat[:, at:at + w]
        at += r
    return out, flat[0, at:]


def kernel(x, positions, ffn1_pre_g, ffn1_w_gate, ffn1_w_up, ffn1_w_down, ffn1_post_g, mix_pre_g, w_in, mla_q_norm_g, mla_w_uq, mla_kv_norm_g, mla_w_ukv, mla_out_g, gdn_conv_w, gdn_a_log, gdn_dt_bias, gdn_norm_g, w_out, mix_post_g, ffn2_pre_g, ffn2_w_gate, ffn2_w_up, ffn2_w_down, ffn2_post_g, loss_target, m_ffn1_pre_g, m_ffn1_w_gate, m_ffn1_w_up, m_ffn1_w_down, m_ffn1_post_g, m_mix_pre_g, m_w_in, m_mla_q_norm_g, m_mla_w_uq, m_mla_kv_norm_g, m_mla_w_ukv, m_mla_out_g, m_gdn_conv_w, m_gdn_a_log, m_gdn_dt_bias, m_gdn_norm_g, m_w_out, m_mix_post_g, m_ffn2_pre_g, m_ffn2_w_gate, m_ffn2_w_up, m_ffn2_w_down, m_ffn2_post_g, v_ffn1_pre_g, v_ffn1_w_gate, v_ffn1_w_up, v_ffn1_w_down, v_ffn1_post_g, v_mix_pre_g, v_w_in, v_mla_q_norm_g, v_mla_w_uq, v_mla_kv_norm_g, v_mla_w_ukv, v_mla_out_g, v_gdn_conv_w, v_gdn_a_log, v_gdn_dt_bias, v_gdn_norm_g, v_w_out, v_mix_post_g, v_ffn2_pre_g, v_ffn2_w_gate, v_ffn2_w_up, v_ffn2_w_down, v_ffn2_post_g):
    given = dict(locals())
    order = ["ffn1_pre_g", "ffn1_w_gate", "ffn1_w_up", "ffn1_w_down", "ffn1_post_g", "mix_pre_g", "w_in", "mla_q_norm_g",
             "mla_w_uq", "mla_kv_norm_g", "mla_w_ukv", "mla_out_g", "gdn_conv_w", "gdn_a_log", "gdn_dt_bias", "gdn_norm_g",
             "w_out", "mix_post_g", "ffn2_pre_g", "ffn2_w_gate", "ffn2_w_up", "ffn2_w_down", "ffn2_post_g"]
    assert sorted(order) == sorted(list(FFN_NAMES) + list(OTHER) + list(SMALL) + ["gdn_conv_w"])

    def drop_depth(a):
        return a[0] if a.ndim == 3 else a

    wts = {n: drop_depth(given[n]) for n in order}
    mom = {n: drop_depth(given["m_" + n]) for n in order}
    var = {n: drop_depth(given["v_" + n]) for n in order}
    me = _block_of(_place())

    def wire(n):
        return (wts[n].T if n in BY_COLUMNS else wts[n]).astype(BF16)

    (ffn1,) = _exchange("gather_first", _Gather([jnp.stack([wire(n) for n in FFN_NAMES[:3]])]))
    mid = _Gather([wire(n) for n in ("w_in", "mla_w_uq", "mla_w_ukv")])
    late = _Gather([jnp.stack([wire(n) for n in FFN_NAMES[3:]]), wire("w_out")])
    conv_at = lax.dynamic_update_slice(jnp.zeros((N_DEV, CONV_SHARD[0] * CONV_SHARD[1]), F32),
                                       wts["gdn_conv_w"].reshape(1, -1), (me, 0))
    conv_all = _all_reduce_small("gather_conv", _pad_lanes(conv_at.reshape(1, -1), 0, SMALL_ROWS * ROW).reshape(SMALL_ROWS, ROW))
    full = {n: wts[n] for n in SMALL}
    full["ffn1"] = ffn1
    full["gdn_conv_w"] = conv_all.reshape(-1)[:CONV_LANES].reshape((N_DEV,) + CONV_SHARD).transpose(1, 0, 2).reshape(CONV_SHAPE)

    loss_lanes, dx, grads, landed = _local_step(x[0], positions[0], loss_target[0], full, mid, late)
    loss = lax.psum(jnp.sum(loss_lanes), MESH_AXES)

    sums = {n: _sum_blocks("sum_" + n, blocks) for n, blocks in landed.items()}
    grad = {n: (sums[n].T if n in BY_COLUMNS else sums[n]) for n in sums}
    small_sum = _all_reduce_small("reduce_small", _pack_small(grads, grads["gdn_conv_w"].reshape(-1), REDUCE_ROWS))
    small_grad, conv_grad_full = _unpack_small(small_sum)
    grad.update(small_grad)
    grad["gdn_conv_w"] = lax.dynamic_slice(conv_grad_full[:CONV_LANES].reshape(CONV_SHAPE), (0, me * CONV_SHARD[1]), CONV_SHARD)

    outs = {"grad": grad, "delta": {}, "new_m": {}, "new_v": {}}
    for n in list(FFN_NAMES) + list(OTHER):
        outs["delta"][n], outs["new_m"][n], outs["new_v"][n] = _adamw("adamw_" + n, wts[n], grad[n], mom[n], var[n])
    small = [_pack_small(s, s["gdn_conv_w"].reshape(-1), SMALL_ROWS) for s in (wts, grad, mom, var)]
    for kind, s in zip(("delta", "new_m", "new_v"), _adamw("adamw_small", *small)):
        vecs, conv = _unpack_small(s)
        outs[kind].update(vecs)
        outs[kind]["gdn_conv_w"] = conv[:CONV_SHARD[0] * CONV_SHARD[1]].reshape(CONV_SHARD)
    result = [loss, dx[None]]
    for kind in ("grad", "delta", "new_m", "new_v"):
        result += [outs[kind][n].reshape(given[n].shape) for n in order]
    return tuple(result)
```

```python
import jax
import jax.numpy as jnp
from jax import lax
from jax.experimental import pallas as pl
from jax.experimental.pallas import tpu as pltpu

F32 = jnp.float32
BF16 = jnp.bfloat16
HI = lax.Precision.HIGH

N_DEV = 8
D_MODEL = 1024
D_FF = 2816
N_HEADS = 8
SLOT = 128
MLA_Q_RANK = 256
MLA_KV_RANK = 128
MLA_NOPE = 64
MLA_ROPE = 32
MLA_V = 64
GDN_D = 64
GDN_CONV = 4
GDN_CHUNK = 64
ROPE_THETA = 10000.0
EPS = 1e-6
ADAM_LR, ADAM_B1, ADAM_B2, ADAM_EPS, ADAM_WD, ADAM_STEP = 0.001, 0.9, 0.999, 1e-08, 0.01, 10


def _dot(a, b, ca, cb, precision=None):
    lead = a.ndim - 2
    batch = tuple(range(lead))
    return lax.dot_general(a, b, (((lead + ca,), (lead + cb,)), (batch, batch)), precision=precision,
                           preferred_element_type=F32)


def _nn(a, b, precision=None):
    return _dot(a, b, 1, 0, precision)


def _nt(a, b, precision=None):
    return _dot(a, b, 1, 1, precision)


def _tn(a, b, precision=None):
    return _dot(a, b, 0, 0, precision)


def _sigmoid(x):
    return 1.0 / (1.0 + jnp.exp(-x))


def _silu(x):
    return x * _sigmoid(x)


def _rms(x, g, n):
    ms = jnp.sum(x * x, axis=-1, keepdims=True) * (1.0 / n)
    return x * lax.rsqrt(ms + EPS) * g


def _chunk_masks():
    c = GDN_CHUNK
    i = lax.broadcasted_iota(jnp.int32, (c, c), 0)
    j = lax.broadcasted_iota(jnp.int32, (c, c), 1)
    lower = i >= j
    strict = i > j
    eye = (i == j).astype(F32)
    blocks = []
    b = 1
    while b < c:
        same = (i // (2 * b)) == (j // (2 * b))
        blocks.append(same & ((i % (2 * b)) >= b) & ((j % (2 * b)) < b))
        b *= 2
    return lower, strict, eye, blocks


def _unit_lower_inverse(low, eye, blocks):
    t = jnp.broadcast_to(eye, low.shape)
    for m in blocks:
        lo = jnp.where(m, low, 0.0)
        t = t - _nn(t, _nn(lo, t, HI), HI)
    return t


@jax.custom_vjp
def _known_inverse(low, tinv):
    return tinv


def _known_inverse_fwd(low, tinv):
    return tinv, tinv


def _known_inverse_bwd(tinv, dt):
    return -_tn(tinv, _nt(dt, tinv, HI), HI), jnp.zeros_like(tinv)


_known_inverse.defvjp(_known_inverse_fwd, _known_inverse_bwd)


def _gdn_chunk(q, k, v, gc, bb, s, masks, tinv=None):
    lower, strict, eye, blocks = masks
    qs = q * (GDN_D ** -0.5)
    gct = jnp.swapaxes(gc, -1, -2)
    decay = jnp.exp(jnp.where(lower, gc - gct, -1e30))
    kb = k * bb
    low = jnp.where(strict, _nt(kb, k, HI) * decay, 0.0)
    tinv = _unit_lower_inverse(low, eye, blocks) if tinv is None else _known_inverse(low, tinv)
    eg = jnp.exp(gc)
    w = _nn(tinv, kb * eg, HI)
    u = _nn(tinv, v * bb, HI)
    attn = _nt(qs, k, HI) * decay
    last = lax.broadcasted_iota(jnp.int32, gc.shape[-2:], 0) == GDN_CHUNK - 1
    g_end = jnp.sum(jnp.where(last, gc, 0.0), axis=-2, keepdims=True)
    k_dec = k * jnp.exp(g_end - gc)
    v_new = u - _nn(w, s, HI)
    o = _nn(qs * eg, s, HI) + _nn(attn, v_new, HI)
    s_new = s * jnp.exp(g_end) + _tn(k_dec, v_new, HI)
    return o, s_new, tinv


GDN_GROUP = 8
GDN_GROUPS = N_HEADS // GDN_GROUP


def _group_heads(ref):
    return jnp.stack([ref[:, pl.ds(j * SLOT, GDN_D)] for j in range(GDN_GROUP)])


def _ungroup_heads(ref, val):
    pad = jnp.zeros((GDN_CHUNK, SLOT - GDN_D), F32)
    for j in range(GDN_GROUP):
        ref[:, pl.ds(j * SLOT, GDN_D)] = val[j]
        ref[:, pl.ds(j * SLOT + GDN_D, SLOT - GDN_D)] = pad


def _gdn_fwd(qkv, gb, bb, carry=None):
    t = qkv.shape[0]
    n_chunks = t // GDN_CHUNK
    d = GDN_D

    def body(q_ref, k_ref, v_ref, g_ref, b_ref, o_ref, keep_ref, s_ref):
        @pl.when(pl.program_id(1) == 0)
        def _():
            s_ref[...] = jnp.zeros_like(s_ref)

        s = s_ref[...]
        keep_ref[:, 0, 0] = s
        o, s_new, tinv = _gdn_chunk(*[_group_heads(r) for r in (q_ref, k_ref, v_ref, g_ref, b_ref)], s, _chunk_masks())
        keep_ref[:, 0, 1] = tinv
        s_ref[...] = s_new
        _ungroup_heads(o_ref, o)

    def spec(kind=0):
        return pl.BlockSpec((GDN_CHUNK, GDN_GROUP * SLOT), lambda h, n: (n, kind * GDN_GROUPS + h))

    return _call_carrying(
        body, carry, (qkv, qkv, qkv, gb, bb), name="gdn_fwd",
        grid=(GDN_GROUPS, n_chunks),
        in_specs=[spec(0), spec(1), spec(2), spec(), spec()],
        out_specs=[spec(), pl.BlockSpec((GDN_GROUP, 1, 2, d, d), lambda h, n: (h, n, 0, 0, 0))],
        out_shape=[jax.ShapeDtypeStruct((t, N_HEADS * SLOT), F32), jax.ShapeDtypeStruct((N_HEADS, n_chunks, 2, d, d), F32)],
        scratch_shapes=[pltpu.VMEM((GDN_GROUP, d, d), F32)],
        compiler_params=pltpu.CompilerParams(dimension_semantics=("arbitrary", "arbitrary")),
    )


def _gdn_bwd(qkv, gb, bb, keep, do, carry=None):
    t = qkv.shape[0]
    n_chunks = t // GDN_CHUNK
    d = GDN_D

    def body(q_ref, k_ref, v_ref, g_ref, b_ref, keep_ref, do_ref, dqkv_ref, dg_ref, db_ref, ds_ref):
        @pl.when(pl.program_id(1) == 0)
        def _():
            ds_ref[...] = jnp.zeros_like(ds_ref)

        masks = _chunk_masks()
        tinv = keep_ref[:, 0, 1]
        _, pull = jax.vjp(lambda *a: _gdn_chunk(*a, masks, tinv)[:2],
                          *[_group_heads(r) for r in (q_ref, k_ref, v_ref, g_ref, b_ref)], keep_ref[:, 0, 0])
        dq, dk, dv, dg, db, ds = pull((_group_heads(do_ref), ds_ref[...]))
        ds_ref[...] = ds
        for i, val in enumerate((dq, dk, dv)):
            _ungroup_heads(dqkv_ref.at[i], val)
        _ungroup_heads(dg_ref, dg)
        _ungroup_heads(db_ref, db)

    def spec(kind=0):
        return pl.BlockSpec((GDN_CHUNK, GDN_GROUP * SLOT), lambda h, n: (n_chunks - 1 - n, kind * GDN_GROUPS + h))

    return _call_carrying(
        body, carry, (qkv, qkv, qkv, gb, bb, keep, do), name="gdn_bwd",
        grid=(GDN_GROUPS, n_chunks),
        in_specs=[spec(0), spec(1), spec(2), spec(), spec(),
                  pl.BlockSpec((GDN_GROUP, 1, 2, d, d), lambda h, n: (h, n_chunks - 1 - n, 0, 0, 0)), spec()],
        out_specs=[pl.BlockSpec((3, GDN_CHUNK, GDN_GROUP * SLOT), lambda h, n: (0, n_chunks - 1 - n, h)), spec(), spec()],
        out_shape=[jax.ShapeDtypeStruct((3, t, N_HEADS * SLOT), F32)] + [jax.ShapeDtypeStruct((t, N_HEADS * SLOT), F32)] * 2,
        scratch_shapes=[pltpu.VMEM((GDN_GROUP, d, d), F32)],
        compiler_params=pltpu.CompilerParams(dimension_semantics=("arbitrary", "arbitrary")),
    )


def _rowwise(name, fn, rows, consts, outs, sums=(), tm=512):
    rows = [x if isinstance(x, tuple) else (x, x.shape[1], 0) for x in rows]
    t = rows[0][0].shape[0]
    tm = min(tm, t)
    steps = t // tm
    n_r, n_c, n_o, n_s = len(rows), len(consts), len(outs), len(sums)

    def window(width, block):
        return pl.BlockSpec((tm, width), lambda i: (i, block))

    def body(*refs):
        r, c = refs[:n_r], refs[n_r:n_r + n_c]
        o, s = refs[n_r + n_c:n_r + n_c + n_o], refs[n_r + n_c + n_o:]
        vals, tot = fn([x[...] for x in r], [x[...] for x in c])
        for ref, val in zip(o, vals):
            ref[...] = val.astype(ref.dtype)
        if n_s:
            @pl.when(pl.program_id(0) == 0)
            def _():
                for ref in s:
                    ref[...] = jnp.zeros_like(ref)

            for ref, val in zip(s, tot):
                ref[...] += val

    return pl.pallas_call(
        body, name=name,
        grid=(steps,),
        in_specs=[window(w, b) for _, w, b in rows] + [pl.BlockSpec(x.shape, lambda i: (0, 0)) for x in consts],
        out_specs=[pl.BlockSpec((tm, w), lambda i: (i, 0)) for w, _ in outs]
        + [pl.BlockSpec((1, w), lambda i: (0, 0)) for w in sums],
        out_shape=[jax.ShapeDtypeStruct((t, w), dt) for w, dt in outs]
        + [jax.ShapeDtypeStruct((1, w), F32) for w in sums],
        compiler_params=pltpu.CompilerParams(dimension_semantics=("arbitrary",)),
    )(*[x for x, _, _ in rows], *consts)


def _tile(dim, target):
    if dim <= target:
        return dim
    best = None
    for cand in range(128, target + 1, 128):
        if dim % cand == 0:
            best = cand
    assert best is not None, (dim, target)
    return best


def _matmul(name, a, b, mode, out_dtype=F32, tm=1024, tn=1024, tk=2048, carry=None):
    if mode == "nn":
        (m, k), n = a.shape, b.shape[1]
    elif mode == "nt":
        (m, k), n = a.shape, b.shape[0]
    else:
        (k, m), n = a.shape, b.shape[1]
    tm, tn, tk = _tile(m, tm), _tile(n, tn), _tile(k, tk)
    k_steps = k // tk
    product = {"nn": _nn, "nt": _nt, "tn": _tn}[mode]

    def body(a_ref, b_ref, o_ref, acc_ref):
        part = product(a_ref[...].astype(BF16), b_ref[...].astype(BF16))
        if k_steps == 1:
            o_ref[...] = part.astype(o_ref.dtype)
        else:
            kk = pl.program_id(2)

            @pl.when(kk == 0)
            def _():
                acc_ref[...] = part

            @pl.when(kk > 0)
            def _():
                acc_ref[...] += part

            @pl.when(kk == k_steps - 1)
            def _():
                o_ref[...] = acc_ref[...].astype(o_ref.dtype)

    a_spec = pl.BlockSpec((tk, tm), lambda i, j, kk: (kk, i)) if mode == "tn" else pl.BlockSpec((tm, tk), lambda i, j, kk: (i, kk))
    b_spec = pl.BlockSpec((tn, tk), lambda i, j, kk: (j, kk)) if mode == "nt" else pl.BlockSpec((tk, tn), lambda i, j, kk: (kk, j))
    (out,), carried = _call_carrying(
        body, carry, (a, b), name=name,
        grid=(m // tm, n // tn, k_steps),
        in_specs=[a_spec, b_spec],
        out_specs=[pl.BlockSpec((tm, tn), lambda i, j, kk: (i, j))],
        out_shape=[jax.ShapeDtypeStruct((m, n), out_dtype)],
        scratch_shapes=[pltpu.VMEM((tm, tn) if k_steps > 1 else (8, 128), F32)],
        compiler_params=pltpu.CompilerParams(dimension_semantics=("arbitrary", "arbitrary", "arbitrary")),
    )
    return out if carry is None else (out, carried)


FFN_TM = 512
FFN_BWD_TM = 256
FFN_BLOCKS = 4
FFN_GATE, FFN_UP, FFN_DOWN = 0, 1, 2


def _ffn_weight_specs(ffn_w, first):
    _, _, rows, dm = ffn_w.shape

    def spec(k):
        return pl.BlockSpec((FFN_BLOCKS, None, rows, dm), lambda i, j: (j, first + k, 0, 0))

    return [spec(FFN_GATE), spec(FFN_UP), spec(FFN_DOWN)], FFN_BLOCKS * rows


def _ffn_fwd(name, x, g_pre, ffn_w, first, g_post, carry=None):
    t, dm = x.shape
    tm = min(FFN_TM, t)
    w_specs, tf = _ffn_weight_specs(ffn_w, first)
    f_steps = N_DEV // FFN_BLOCKS

    def body(x_ref, gpre_ref, wg_ref, wu_ref, wd_ref, gpost_ref, h_ref, y_ref, hg_ref, hu_ref, xn_ref, acc_ref):
        j = pl.program_id(1)

        @pl.when(j == 0)
        def _():
            xn_ref[...] = _rms(x_ref[...], gpre_ref[...], dm).astype(BF16)
            acc_ref[...] = jnp.zeros_like(acc_ref)

        xn = xn_ref[...]
        wg, wu, wd = (r[...].reshape(tf, dm) for r in (wg_ref, wu_ref, wd_ref))
        hg, hu = _nt(xn, wg), _nt(xn, wu)
        hg_ref[...] = hg.astype(BF16)
        hu_ref[...] = hu.astype(BF16)
        a = _silu(hg) * hu
        acc_ref[...] += _nn(a.astype(BF16), wd)

        @pl.when(j == f_steps - 1)
        def _():
            h = acc_ref[...]
            h_ref[...] = h
            y_ref[...] = x_ref[...] + 0.5 * _rms(h, gpost_ref[...], dm)

    row = pl.BlockSpec((tm, dm), lambda i, j: (i, 0))
    vec = pl.BlockSpec((1, dm), lambda i, j: (0, 0))
    wide = pl.BlockSpec((tm, tf), lambda i, j: (i, j))
    return _call_carrying(
        body, carry, (x, g_pre, ffn_w, ffn_w, ffn_w, g_post), name=name,
        grid=(t // tm, f_steps),
        in_specs=[row, vec, *w_specs, vec],
        out_specs=[row, row, wide, wide],
        out_shape=[jax.ShapeDtypeStruct((t, dm), F32)] * 2 + [jax.ShapeDtypeStruct((t, f_steps * tf), BF16)] * 2,
        scratch_shapes=[pltpu.VMEM((tm, dm), BF16), pltpu.VMEM((tm, dm), F32)],
        compiler_params=pltpu.CompilerParams(dimension_semantics=("arbitrary", "arbitrary")),
    )


def _ffn_bwd(name, x, h, hg, hu, dy, g_pre, ffn_w, first, g_post, carry=None):
    t, dm = x.shape
    tm = min(FFN_BWD_TM, t)
    w_specs, tf = _ffn_weight_specs(ffn_w, first)
    f_steps = N_DEV // FFN_BLOCKS
    f = f_steps * tf

    def post(hv, g):
        return 0.5 * _rms(hv, g, dm)

    def pre(xv, g):
        return _rms(xv, g, dm)

    def body(x_ref, h_ref, dy_ref, hg_ref, hu_ref, gpre_ref, wg_ref, wu_ref, wd_ref, gpost_ref,
             dx_ref, xn_ref, dh_ref, a_ref, dhg_ref, dhu_ref, dgpre_ref, dgpost_ref, acc_ref):
        i, j = pl.program_id(0), pl.program_id(1)

        @pl.when((i == 0) & (j == 0))
        def _():
            dgpre_ref[...] = jnp.zeros_like(dgpre_ref)
            dgpost_ref[...] = jnp.zeros_like(dgpost_ref)

        @pl.when(j == 0)
        def _():
            xn_ref[...] = pre(x_ref[...], gpre_ref[...]).astype(BF16)
            _, pull = jax.vjp(post, h_ref[...], gpost_ref[...])
            dh, dg = pull(dy_ref[...])
            dh_ref[...] = dh.astype(BF16)
            dgpost_ref[...] += dg
            acc_ref[...] = jnp.zeros_like(acc_ref)

        wg, wu, wd = (r[...].reshape(tf, dm) for r in (wg_ref, wu_ref, wd_ref))
        hg, hu = hg_ref[...].astype(F32), hu_ref[...].astype(F32)
        da = _nt(dh_ref[...], wd)
        sig = _sigmoid(hg)
        act = hg * sig
        dhu = (da * act).astype(BF16)
        dhg = (da * hu * (sig * (1.0 + hg * (1.0 - sig)))).astype(BF16)
        a_ref[...] = (act * hu).astype(BF16)
        dhg_ref[...] = dhg
        dhu_ref[...] = dhu
        acc_ref[...] += _nn(dhg, wg) + _nn(dhu, wu)

        @pl.when(j == f_steps - 1)
        def _():
            _, pull = jax.vjp(pre, x_ref[...], gpre_ref[...])
            dx, dg = pull(acc_ref[...])
            dx_ref[...] = dy_ref[...] + dx
            dgpre_ref[...] += dg

    row = pl.BlockSpec((tm, dm), lambda i, j: (i, 0))
    vec = pl.BlockSpec((1, dm), lambda i, j: (0, 0))
    wide = pl.BlockSpec((tm, tf), lambda i, j: (i, j))
    return _call_carrying(
        body, carry, (x, h, dy, hg, hu, g_pre, ffn_w, ffn_w, ffn_w, g_post), name=name,
        grid=(t // tm, f_steps),
        in_specs=[row, row, row, wide, wide, vec, *w_specs, vec],
        out_specs=[row, row, row, wide, wide, wide, vec, vec],
        out_shape=[jax.ShapeDtypeStruct((t, dm), F32), jax.ShapeDtypeStruct((t, dm), BF16), jax.ShapeDtypeStruct((t, dm), BF16),
                   jax.ShapeDtypeStruct((t, f), BF16), jax.ShapeDtypeStruct((t, f), BF16), jax.ShapeDtypeStruct((t, f), BF16),
                   jax.ShapeDtypeStruct((1, dm), F32), jax.ShapeDtypeStruct((1, dm), F32)],
        scratch_shapes=[pltpu.VMEM((tm, dm), F32)],
        compiler_params=pltpu.CompilerParams(dimension_semantics=("arbitrary", "arbitrary")),
    )


ATT_T = 512
ATT_GROUP = 2
ATT_SCALE = (MLA_NOPE + MLA_ROPE) ** -0.5


def _stack_slots(ref, group):
    return jnp.stack([ref[:, pl.ds(j * SLOT, SLOT)] for j in range(group)])


def _unstack_slots(ref, val):
    for j in range(val.shape[0]):
        ref[:, pl.ds(j * SLOT, SLOT)] = val[j].astype(ref.dtype)


def _scores(q, k, diagonal):
    s = _nt(q, k) * ATT_SCALE
    if diagonal:
        row = lax.broadcasted_iota(jnp.int32, s.shape[1:], 0)
        col = lax.broadcasted_iota(jnp.int32, s.shape[1:], 1)
        s = jnp.where(col <= row, s, -1e30)
    return s


def _attn_pairs(steps, q_major):
    pairs = ([(qi, ki) for qi in range(steps) for ki in range(qi + 1)] if q_major
             else [(qi, ki) for ki in range(steps) for qi in range(ki, steps)])
    return jnp.array([p[0] for p in pairs], jnp.int32), jnp.array([p[1] for p in pairs], jnp.int32)


def _attn_specs(tile):
    width = ATT_GROUP * SLOT
    return (pl.BlockSpec((tile, width), lambda h, p, qt, kt: (qt[p], h)),
            pl.BlockSpec((tile, width), lambda h, p, qt, kt: (kt[p], h)))


def _attn_fwd(q, k, v):
    t = q.shape[0]
    tile = min(ATT_T, t)
    steps = t // tile
    g = ATT_GROUP

    strip = min(SLOT, tile)

    def body(qt_ref, kt_ref, q_ref, k_ref, v_ref, o_ref, lse_ref, m_ref, l_ref, alpha_ref, acc_ref, s_ref, p_ref):
        qi, ki = qt_ref[pl.program_id(1)], kt_ref[pl.program_id(1)]

        @pl.when(ki == 0)
        def _():
            m_ref[...] = jnp.full_like(m_ref, -1e30)
            l_ref[...] = jnp.zeros_like(l_ref)
            acc_ref[...] = jnp.zeros_like(acc_ref)

        def step(diagonal):
            s_ref[...] = _nt(_stack_slots(k_ref, g), _stack_slots(q_ref, g))
            for j in range(tile // strip):
                c = pl.ds(j * strip, strip)
                s = s_ref[:, :, c] * ATT_SCALE
                if diagonal:
                    key = lax.broadcasted_iota(jnp.int32, s.shape[1:], 0)
                    query = lax.broadcasted_iota(jnp.int32, s.shape[1:], 1) + j * strip
                    s = jnp.where(key <= query, s, -1e30)
                m_old = m_ref[:, :, c]
                m_new = jnp.maximum(m_old, jnp.max(s, axis=1, keepdims=True))
                p = jnp.exp(s - m_new)
                alpha = jnp.exp(m_old - m_new)
                l_ref[:, :, c] = alpha * l_ref[:, :, c] + jnp.sum(p, axis=1, keepdims=True)
                alpha_ref[:, :, c] = alpha
                m_ref[:, :, c] = m_new
                p_ref[:, :, c] = p.astype(BF16)
            acc_ref[...] = acc_ref[...] * alpha_ref[...] + _tn(_stack_slots(v_ref, g), p_ref[...])

        @pl.when(ki < qi)
        def _():
            step(False)

        @pl.when(ki == qi)
        def _():
            step(True)
            out = acc_ref[...] / l_ref[...]
            lse = jnp.broadcast_to(m_ref[...] + jnp.log(l_ref[...]), out.shape)
            for j in range(g):
                o_ref[:, pl.ds(j * SLOT, SLOT)] = out[j].T
                lse_ref[:, pl.ds(j * SLOT, SLOT)] = lse[j].T

    q_spec, k_spec = _attn_specs(tile)
    tables = _attn_pairs(steps, True)
    return pl.pallas_call(
        body, name="attn_fwd",
        grid_spec=pltpu.PrefetchScalarGridSpec(
            num_scalar_prefetch=2, grid=(N_HEADS // g, tables[0].shape[0]),
            in_specs=[q_spec, k_spec, k_spec], out_specs=[q_spec, q_spec],
            scratch_shapes=[pltpu.VMEM((g, 1, tile), F32), pltpu.VMEM((g, 1, tile), F32), pltpu.VMEM((g, 1, tile), F32),
                            pltpu.VMEM((g, SLOT, tile), F32), pltpu.VMEM((g, tile, tile), F32), pltpu.VMEM((g, tile, tile), BF16)]),
        out_shape=[jax.ShapeDtypeStruct((t, N_HEADS * SLOT), F32)] * 2,
        compiler_params=pltpu.CompilerParams(dimension_semantics=("parallel", "arbitrary")),
    )(*tables, q, k, v)


def _attn_grad_scores(q, k, v, do, lse_ref, delta_ref, diagonal):
    g = ATT_GROUP
    p = jnp.exp(_scores(q, k, diagonal) - _stack_slots(lse_ref, g)[:, :, 0:1])
    dp = _nt(do, v)
    return p, p * (dp - _stack_slots(delta_ref, g)[:, :, 0:1]) * ATT_SCALE


def _attn_bwd(q, k, v, do, lse, delta):
    t = q.shape[0]
    tile = min(ATT_T, t)
    steps = t // tile
    g = ATT_GROUP

    def body(qt_ref, kt_ref, q_ref, k_ref, v_ref, do_ref, lse_ref, delta_ref, dq_ref, dk_ref, dv_ref, dk_acc, dv_acc):
        qi, ki = qt_ref[pl.program_id(1)], kt_ref[pl.program_id(1)]

        @pl.when(pl.program_id(1) == 0)
        def _():
            dq_ref[...] = jnp.zeros_like(dq_ref)

        def step(diagonal):
            qq, kk = _stack_slots(q_ref, g), _stack_slots(k_ref, g)
            do_b = _stack_slots(do_ref, g).astype(BF16)
            p, ds = _attn_grad_scores(qq, kk, _stack_slots(v_ref, g), do_b, lse_ref, delta_ref, diagonal)
            ds = ds.astype(BF16)
            dv_acc[...] += _tn(p.astype(BF16), do_b)
            dk_acc[...] += _tn(ds, qq)
            dq = _nn(ds, kk)
            rows = pl.ds(pl.multiple_of(qi * tile, tile), tile)
            for j in range(g):
                dq_ref[rows, pl.ds(j * SLOT, SLOT)] += dq[j]

        @pl.when(qi == ki)
        def _():
            dk_acc[...] = jnp.zeros_like(dk_acc)
            dv_acc[...] = jnp.zeros_like(dv_acc)
            step(True)

        @pl.when(qi > ki)
        def _():
            step(False)

        @pl.when(qi == steps - 1)
        def _():
            _unstack_slots(dk_ref, dk_acc[...])
            _unstack_slots(dv_ref, dv_acc[...])

    q_spec, k_spec = _attn_specs(tile)
    tables = _attn_pairs(steps, False)
    return pl.pallas_call(
        body, name="attn_bwd",
        grid_spec=pltpu.PrefetchScalarGridSpec(
            num_scalar_prefetch=2, grid=(N_HEADS // g, tables[0].shape[0]),
            in_specs=[q_spec, k_spec, k_spec, q_spec, q_spec, q_spec],
            out_specs=[pl.BlockSpec((t, g * SLOT), lambda h, p, qt, kt: (0, h)), k_spec, k_spec],
            scratch_shapes=[pltpu.VMEM((g, tile, SLOT), F32), pltpu.VMEM((g, tile, SLOT), F32)]),
        out_shape=[jax.ShapeDtypeStruct((t, N_HEADS * SLOT), F32)] * 3,
        compiler_params=pltpu.CompilerParams(dimension_semantics=("parallel", "arbitrary")),
    )(*tables, q, k, v, do, lse, delta)


CONV_PAD = 8


def _fill_padded(ref, val):
    t = val.shape[0]
    zeros = jnp.zeros((CONV_PAD, val.shape[1]), val.dtype)
    ref[pl.ds(0, CONV_PAD)] = zeros
    ref[pl.ds(CONV_PAD + t, CONV_PAD)] = zeros
    ref[pl.ds(CONV_PAD, t)] = val


def _shifted(ref, s):
    return ref[pl.ds(CONV_PAD - s, ref.shape[0] - 2 * CONV_PAD)]


def _l2norm(x):
    return x * lax.rsqrt(jnp.sum(x * x, axis=-1, keepdims=True) + EPS)


def _conv_pre(x_pad, w):
    y = w[GDN_CONV - 1:GDN_CONV, :] * _shifted(x_pad, 0)
    for s in range(1, GDN_CONV):
        y = y + w[GDN_CONV - 1 - s:GDN_CONV - s, :] * _shifted(x_pad, s)
    return y


def _gdn_conv_fwd(x, w):
    t, width = x.shape

    def body(x_ref, w_ref, o_ref, x_pad):
        _fill_padded(x_pad, x_ref[...])
        act = _silu(_conv_pre(x_pad, w_ref[...]))
        normed = pl.program_id(0) < 2 * N_HEADS
        o_ref[...] = jnp.where(normed, _l2norm(act), act)

    return pl.pallas_call(
        body, name="gdn_conv_fwd",
        grid=(width // SLOT,),
        in_specs=[pl.BlockSpec((t, SLOT), lambda j: (0, j)), pl.BlockSpec((GDN_CONV, SLOT), lambda j: (0, j))],
        out_specs=pl.BlockSpec((t, SLOT), lambda j: (0, j)),
        out_shape=jax.ShapeDtypeStruct((t, width), F32),
        scratch_shapes=[pltpu.VMEM((t + 2 * CONV_PAD, SLOT), F32)],
        compiler_params=pltpu.CompilerParams(dimension_semantics=("parallel",)),
    )(x, w)


def _gdn_conv_bwd(x, w, dout):
    t, width = x.shape

    def body(x_ref, w_ref, do_ref, dx_ref, dw_ref, x_pad, dy_pad):
        wv = w_ref[...]
        _fill_padded(x_pad, x_ref[...])
        y = _conv_pre(x_pad, wv)
        sig = _sigmoid(y)
        act = y * sig
        _, pull = jax.vjp(_l2norm, act)
        normed = pl.program_id(0) < 2 * N_HEADS
        dact = jnp.where(normed, pull(do_ref[0])[0], do_ref[0])
        dy = dact * (sig * (1.0 + y * (1.0 - sig)))
        _fill_padded(dy_pad, dy)
        dx = wv[GDN_CONV - 1:GDN_CONV, :] * dy
        for s in range(1, GDN_CONV):
            dx = dx + wv[GDN_CONV - 1 - s:GDN_CONV - s, :] * _shifted(dy_pad, -s)
        dx_ref[...] = dx.astype(BF16)
        for s in range(GDN_CONV):
            dw_ref[GDN_CONV - 1 - s:GDN_CONV - s, :] = jnp.sum(dy * _shifted(x_pad, s), axis=0, keepdims=True)

    col = pl.BlockSpec((t, SLOT), lambda j: (0, j))
    tap = pl.BlockSpec((GDN_CONV, SLOT), lambda j: (0, j))
    return pl.pallas_call(
        body, name="gdn_conv_bwd",
        grid=(width // SLOT,),
        in_specs=[col, tap, pl.BlockSpec((1, t, SLOT), lambda j: (j // N_HEADS, 0, j % N_HEADS))],
        out_specs=[col, tap],
        out_shape=[jax.ShapeDtypeStruct((t, width), BF16), jax.ShapeDtypeStruct((GDN_CONV, width), F32)],
        scratch_shapes=[pltpu.VMEM((t + 2 * CONV_PAD, SLOT), F32)] * 2,
        compiler_params=pltpu.CompilerParams(dimension_semantics=("parallel",)),
    )(x, w, dout)


def _softplus(x):
    e = jnp.exp(-jnp.abs(x))
    u = 1.0 + e
    log1p = jnp.where(u == 1.0, e, jnp.log(u) * e / jnp.where(u == 1.0, 1.0, u - 1.0))
    return jnp.maximum(x, 0.0) + log1p


def _chunk_running_sum(x, reverse=False):
    tm = x.shape[0]
    at = lax.broadcasted_iota(jnp.int32, x.shape, 0) % GDN_CHUNK
    step = 1
    while step < GDN_CHUNK:
        if reverse:
            x = x + jnp.where(at < GDN_CHUNK - step, pltpu.roll(x, tm - step, 0), 0.0)
        else:
            x = x + jnp.where(at >= step, pltpu.roll(x, step, 0), 0.0)
        step *= 2
    return x


def _gates_fwd(ab, a_log, dt_bias):
    def fn(rows, consts):
        (abv,), (alog, dtb) = rows, consts
        g = _chunk_running_sum(-jnp.exp(alog) * _softplus(abv + dtb))
        beta = _sigmoid(abv)
        shape = (abv.shape[0], SLOT)
        g_slots = [jnp.broadcast_to(g[:, h:h + 1], shape) for h in range(N_HEADS)]
        b_slots = [jnp.broadcast_to(beta[:, N_HEADS + h:N_HEADS + h + 1], shape) for h in range(N_HEADS)]
        return [jnp.concatenate(g_slots, axis=1), jnp.concatenate(b_slots, axis=1)], []

    width = N_HEADS * SLOT
    return _rowwise("gdn_gates_fwd", fn, [ab], [a_log, dt_bias], [(width, F32), (width, F32)])


def _gates_bwd(ab, a_log, dt_bias, dg, dbeta):
    def fn(rows, consts):
        (abv, dgv, dbv), (alog, dtb) = rows, consts
        lane = lax.broadcasted_iota(jnp.int32, abv.shape, 1)
        dg_tok = jnp.zeros_like(abv)
        db_tok = jnp.zeros_like(abv)
        for h in range(N_HEADS):
            dg_tok = dg_tok + jnp.where(lane == h, jnp.sum(dgv[:, h * SLOT:(h + 1) * SLOT], axis=1, keepdims=True), 0.0)
            db_tok = db_tok + jnp.where(lane == N_HEADS + h, jnp.sum(dbv[:, h * SLOT:(h + 1) * SLOT], axis=1, keepdims=True), 0.0)
        dg_tok = _chunk_running_sum(dg_tok, reverse=True)
        xa = abv + dtb
        g = -jnp.exp(alog) * _softplus(xa)
        da = dg_tok * (-jnp.exp(alog)) * _sigmoid(xa)
        beta = _sigmoid(abv)
        dab = jnp.where(lane < N_HEADS, da, db_tok * beta * (1.0 - beta))
        dab = jnp.where(lane < 2 * N_HEADS, dab, 0.0)
        d_alog = jnp.sum(jnp.where(lane < N_HEADS, dg_tok * g, 0.0), axis=0, keepdims=True)
        d_dtb = jnp.sum(jnp.where(lane < N_HEADS, da, 0.0), axis=0, keepdims=True)
        return [dab], [d_alog, d_dtb]

    return _rowwise("gdn_gates_bwd", fn, [ab, dg, dbeta], [a_log, dt_bias], [(SLOT, F32)], sums=[SLOT, SLOT])


ROPE_HALF = MLA_ROPE // 2


def _rope_tables(positions):
    freqs = ROPE_THETA ** (-jnp.arange(ROPE_HALF, dtype=F32) / ROPE_HALF)
    ang = positions.astype(F32)[:, None] * freqs
    cos, sin = jnp.cos(ang), jnp.sin(ang)
    t = positions.shape[0]
    ones, zeros = jnp.ones((t, MLA_NOPE), F32), jnp.zeros((t, MLA_NOPE), F32)
    tail = jnp.zeros((t, SLOT - MLA_NOPE - MLA_ROPE), F32)
    half0 = jnp.zeros((t, ROPE_HALF), F32)
    same = jnp.concatenate([ones, cos, cos, tail], axis=1)
    from_low = jnp.concatenate([zeros, half0, sin, tail], axis=1)
    from_high = jnp.concatenate([zeros, -sin, half0, tail], axis=1)
    return same, from_low, from_high


def _rope(x, tabs):
    same, from_low, from_high = tabs
    width = x.shape[1]
    return x * same + pltpu.roll(x, ROPE_HALF, 1) * from_low + pltpu.roll(x, width - ROPE_HALF, 1) * from_high


def _rope_transposed(dy, tabs):
    same, from_low, from_high = tabs
    width = dy.shape[1]
    return dy * same + pltpu.roll(dy * from_low, width - ROPE_HALF, 1) + pltpu.roll(dy * from_high, ROPE_HALF, 1)


def _tile_slots(tab):
    return jnp.concatenate([tab] * N_HEADS, axis=1)


A_WIDTH = MLA_Q_RANK + MLA_KV_RANK + 2 * SLOT
A_KPE = MLA_Q_RANK + MLA_KV_RANK
A_AB = A_KPE + SLOT
WIDE = N_HEADS * SLOT


def _mla_pre_fwd(proj_a, tabs, g_q, g_kv):
    def fn(rows, consts):
        pa, *tb = rows
        gq, gkv = consts
        return [_rms(pa[:, :MLA_Q_RANK], gq, MLA_Q_RANK), _rms(pa[:, MLA_Q_RANK:A_KPE], gkv, MLA_KV_RANK),
                _rope(pa[:, A_KPE:A_AB], tb)], []

    return _rowwise("mla_pre_fwd", fn, [proj_a, *tabs], [g_q, g_kv], [(MLA_Q_RANK, BF16), (MLA_KV_RANK, BF16), (SLOT, F32)])


def _mla_pre_bwd(proj_a, tabs, g_q, g_kv, dcqn, dckvn, dkpe, dab):
    def fn(rows, consts):
        pa, t0, t1, t2, dq, dkv, dk, da = rows
        gq, gkv = consts
        _, pull_q = jax.vjp(lambda x, g: _rms(x, g, MLA_Q_RANK), pa[:, :MLA_Q_RANK], gq)
        _, pull_kv = jax.vjp(lambda x, g: _rms(x, g, MLA_KV_RANK), pa[:, MLA_Q_RANK:A_KPE], gkv)
        dcq, dgq = pull_q(dq)
        dckv, dgkv = pull_kv(dkv)
        return [jnp.concatenate([dcq, dckv, _rope_transposed(dk, (t0, t1, t2)), da], axis=1)], [dgq, dgkv]

    return _rowwise("mla_pre_bwd", fn, [proj_a, *tabs, dcqn, dckvn, dkpe, dab], [g_q, g_kv], [(A_WIDTH, BF16)],
                    sums=[MLA_Q_RANK, MLA_KV_RANK])


def _mla_qkv_fwd(q_p, kv_p, kpe, tabs):
    def fn(rows, consts):
        qv, kvv, kp, *tb = rows
        q = _rope(qv, [_tile_slots(x) for x in tb])
        k = kvv[:, :WIDE] + _tile_slots(kp)
        return [q, k, kvv[:, WIDE:]], []

    return _rowwise("mla_qkv_fwd", fn, [q_p, kv_p, kpe, *tabs], [], [(WIDE, BF16)] * 3)


def _mla_qkv_bwd(dq, dk, dv, tabs):
    def fn(rows, consts):
        dqv, dkv, dvv, *tb = rows
        dkpe = dkv[:, :SLOT]
        for h in range(1, N_HEADS):
            dkpe = dkpe + dkv[:, h * SLOT:(h + 1) * SLOT]
        return [_rope_transposed(dqv, [_tile_slots(x) for x in tb]), jnp.concatenate([dkv, dvv], axis=1), dkpe], []

    return _rowwise("mla_qkv_bwd", fn, [dq, dk, dv, *tabs], [], [(WIDE, BF16), (2 * WIDE, BF16), (SLOT, F32)])


def _slot_sum(x):
    parts = [jnp.broadcast_to(jnp.sum(x[:, h * SLOT:(h + 1) * SLOT], axis=1, keepdims=True), (x.shape[0], SLOT))
             for h in range(N_HEADS)]
    return jnp.concatenate(parts, axis=1)


def _mix_join(o_mla, o_gdn, gate, g_mla, g_gdn):
    mla = _rms(o_mla, g_mla, N_HEADS * MLA_V)
    gdn = o_gdn * lax.rsqrt(_slot_sum(o_gdn * o_gdn) * (1.0 / GDN_D) + EPS) * g_gdn * _silu(gate)
    return mla, gdn


def _mix_join_fwd(o_mla, o_gdn, gate, g_mla, g_gdn):
    def fn(rows, consts):
        return [jnp.concatenate(_mix_join(*rows, *consts), axis=1)], []

    return _rowwise("mix_join_fwd", fn, [o_mla, o_gdn, gate], [g_mla, g_gdn], [(2 * WIDE, BF16)])


def _mix_join_bwd(o_mla, o_gdn, gate, g_mla, g_gdn, dcat):
    def fn(rows, consts):
        om, og, gt, dc = rows
        gm, gg = consts
        _, pull = jax.vjp(lambda x, g: _rms(x, g, N_HEADS * MLA_V), om, gm)
        dom, dgm = pull(dc[:, :WIDE])
        dy = dc[:, WIDE:]
        r = lax.rsqrt(_slot_sum(og * og) * (1.0 / GDN_D) + EPS)
        sig = _sigmoid(gt)
        normed = og * r
        dn = dy * gg * (gt * sig)
        dog = r * dn - normed * (r * r) * _slot_sum(dn * og) * (1.0 / GDN_D)
        dgt = dy * normed * gg * (sig * (1.0 + gt * (1.0 - sig)))
        dgg = jnp.sum(dy * normed * (gt * sig), axis=0, keepdims=True)
        return [dom, _slot_sum(dom * om), dog, dgt], [dgm, dgg]

    return _rowwise("mix_join_bwd", fn, [o_mla, o_gdn, gate, dcat], [g_mla, g_gdn],
                    [(WIDE, F32), (WIDE, F32), (WIDE, F32), (WIDE, BF16)], sums=[WIDE, WIDE])


def _norm_residual_fwd(name, x, h, g, out_dtypes):
    dm = x.shape[1]

    def fn(rows, consts):
        y = rows[0] + _rms(rows[1], consts[0], dm)
        return [y] + [_rms(y, gg, dm) for gg in consts[1:]], []

    return _rowwise(name, fn, [x, h], list(g), [(dm, dt) for dt in out_dtypes])


def _norm_residual_bwd(name, h, g, dy):
    dm = h.shape[1]

    def fn(rows, consts):
        _, pull = jax.vjp(lambda hv, gv: _rms(hv, gv, dm), rows[0], consts[0])
        dh, dg = pull(rows[1])
        return [dh], [dg]

    return _rowwise(name, fn, [h, dy], [g], [(dm, BF16)], sums=[dm])


def _norm_bwd_add(name, x, g, dns, dy):
    dm = x.shape[1]

    def fn(rows, consts):
        xv, dyv, *parts = rows
        dn = parts[0]
        for p in parts[1:]:
            dn = dn + p
        _, pull = jax.vjp(lambda a, gv: _rms(a, gv, dm), xv, consts[0])
        dx, dg = pull(dn)
        return [dyv + dx], [dg]

    return _rowwise(name, fn, [x, dy, *dns], [g], [(dm, F32)], sums=[dm])


def _loss_fwd(y, target):
    dm = y.shape[1]

    def fn(rows, consts):
        err = rows[0] - rows[1]
        sq = err * err
        lanes = sq[:, :SLOT]
        for j in range(1, dm // SLOT):
            lanes = lanes + sq[:, j * SLOT:(j + 1) * SLOT]
        return [err * (1.0 / dm)], [jnp.sum(lanes, axis=0, keepdims=True) * (0.5 / dm)]

    return _rowwise("loss", fn, [y, target], [], [(dm, F32)], sums=[SLOT])


def _norm_fwd(name, x, g):
    dm = x.shape[1]
    return _rowwise(name, lambda rows, consts: ([_rms(rows[0], consts[0], dm)], []), [x], [g], [(dm, BF16)])[0]


W_IN_CUTS = (0, 256, 384, 416, 1952, 1960, 1968, 2480)


def _heads_out(w, per_head, axis=-1):
    axis = axis % w.ndim
    shape = w.shape
    n = shape[axis] // per_head
    w = w.reshape(shape[:axis] + (n, per_head) + shape[axis + 1:])
    pad = [(0, 0)] * w.ndim
    pad[axis + 1] = (0, SLOT - per_head)
    return jnp.pad(w, pad).reshape(shape[:axis] + (n * SLOT,) + shape[axis + 1:])


def _heads_in(w, per_head, axis=-1):
    axis = axis % w.ndim
    shape = w.shape
    n = shape[axis] // SLOT
    w = w.reshape(shape[:axis] + (n, SLOT) + shape[axis + 1:])
    w = lax.slice_in_dim(w, 0, per_head, axis=axis + 1)
    return w.reshape(shape[:axis] + (n * per_head,) + shape[axis + 1:])


def _pad_lanes(v, lo, width=SLOT):
    return jnp.pad(v, [(0, 0)] * (v.ndim - 1) + [(lo, width - lo - v.shape[-1])])


def _pad_rows(v, lo, rows=SLOT):
    return jnp.pad(v, [(lo, rows - lo - v.shape[0])] + [(0, 0)] * (v.ndim - 1))


def _layout_weights(w):
    c = W_IN_CUTS
    w_in = w["w_in_t"]
    p = {}
    p["w_a"] = jnp.concatenate([w_in[c[0]:c[2]], _pad_rows(w_in[c[2]:c[3]], MLA_NOPE), _pad_rows(w_in[c[4]:c[6]], 0)], axis=0)
    p["w_qkv"] = _heads_out(w_in[c[3]:c[4]], GDN_D, axis=0)
    p["w_gate"] = _heads_out(w_in[c[6]:c[7]], GDN_D, axis=0)
    p["w_uq"] = _heads_out(w["uq_t"], MLA_NOPE + MLA_ROPE, axis=0)
    ukv = w["ukv_t"].reshape(N_HEADS, MLA_NOPE + MLA_V, MLA_KV_RANK)
    p["w_kv"] = jnp.concatenate([_heads_out(ukv[:, :MLA_NOPE].reshape(-1, MLA_KV_RANK), MLA_NOPE, axis=0),
                                 _heads_out(ukv[:, MLA_NOPE:].reshape(-1, MLA_KV_RANK), MLA_V, axis=0)], axis=0)
    p["conv"] = _heads_out(w["gdn_conv_w"], GDN_D)
    p["g_mla_out"] = _heads_out(w["mla_out_g"], MLA_V)
    p["g_gdn"] = jnp.tile(_pad_lanes(w["gdn_norm_g"], 0), (1, N_HEADS))
    p["a_log"] = _pad_lanes(w["gdn_a_log"], 0)
    p["dt_bias"] = _pad_lanes(w["gdn_dt_bias"], 0)
    return p


def _unlayout_grads(d):
    c = W_IN_CUTS
    g = {}
    da = d["w_a"]
    kpe0 = A_KPE + MLA_NOPE
    g["w_in_t"] = jnp.concatenate([da[:A_KPE], da[kpe0:kpe0 + MLA_ROPE], _heads_in(d["w_qkv"], GDN_D, axis=0),
                                   da[A_AB:A_AB + 2 * N_HEADS], _heads_in(d["w_gate"], GDN_D, axis=0)], axis=0)
    assert g["w_in_t"].shape[0] == c[-1]
    g["uq_t"] = _heads_in(d["w_uq"], MLA_NOPE + MLA_ROPE, axis=0)
    dk = _heads_in(d["w_kv"][:WIDE], MLA_NOPE, axis=0).reshape(N_HEADS, MLA_NOPE, MLA_KV_RANK)
    dv = _heads_in(d["w_kv"][WIDE:], MLA_V, axis=0).reshape(N_HEADS, MLA_V, MLA_KV_RANK)
    g["ukv_t"] = jnp.concatenate([dk, dv], axis=1).reshape(-1, MLA_KV_RANK)
    g["w_out"] = _heads_in(d["w_out"], GDN_D, axis=0)
    g["gdn_conv_w"] = _heads_in(d["conv"], GDN_D)
    g["mla_out_g"] = _heads_in(d["g_mla_out"], MLA_V)
    g["gdn_norm_g"] = jnp.sum(d["g_gdn"].reshape(N_HEADS, SLOT), axis=0, keepdims=True)[:, :GDN_D]
    g["gdn_a_log"] = d["a_log"][:, :N_HEADS]
    g["gdn_dt_bias"] = d["dt_bias"][:, :N_HEADS]
    return g


def _weight_grad(name, cots, acts, out_dtype=F32, tm=1024, tn=1024, tk=2048, carry=None):
    return _matmul(name, cots, acts, "tn", out_dtype=out_dtype, tm=tm, tn=tn, tk=tk, carry=carry)


def _by_device(a):
    return a.astype(BF16).reshape((N_DEV, a.shape[0] // N_DEV) + a.shape[1:])


def _rows_of(blocks):
    return blocks.reshape((-1,) + blocks.shape[2:])


def _local_step(x, positions, target, w, mid, late):
    tabs = _rope_tables(positions)

    (h1, x1, hg1, hu1), gathered = _ffn_fwd("ffn1_fwd", x, w["ffn1_pre_g"], w["ffn1"], 0, w["ffn1_post_g"], carry=mid)
    w = dict(w, w_in_t=_rows_of(gathered[0]), uq_t=_rows_of(gathered[1]), ukv_t=_rows_of(gathered[2]))
    p = _layout_weights(w)
    hn = _norm_fwd("mix_pre_norm", x1, w["mix_pre_g"])
    proj_a = _matmul("proj_a", hn, p["w_a"], "nt")
    proj_qkv = _matmul("proj_qkv", hn, p["w_qkv"], "nt")
    proj_gate = _matmul("proj_gate", hn, p["w_gate"], "nt")
    cqn, ckvn, kpe = _mla_pre_fwd(proj_a, tabs, w["mla_q_norm_g"], w["mla_kv_norm_g"])
    q_p = _matmul("mla_q", cqn, p["w_uq"], "nt")
    kv_p = _matmul("mla_kv", ckvn, p["w_kv"], "nt")
    q, k, v = _mla_qkv_fwd(q_p, kv_p, kpe, tabs)
    o_mla, lse = _attn_fwd(q, k, v)
    ab = (proj_a, SLOT, A_AB // SLOT)
    qkv_n = _gdn_conv_fwd(proj_qkv, p["conv"])
    gb, bb = _gates_fwd(ab, p["a_log"], p["dt_bias"])
    (o_gdn, keep), (ffn2, w_out) = _gdn_fwd(qkv_n, gb, bb, carry=late)
    p["w_out"] = _heads_out(_rows_of(w_out), GDN_D, axis=0)
    cat = _mix_join_fwd(o_mla, o_gdn, proj_gate, p["g_mla_out"], p["g_gdn"])[0]
    mixed = _matmul("mix_out", cat, p["w_out"], "nn")
    x2 = _norm_residual_fwd("mix_post", x1, mixed, [w["mix_post_g"]], [F32])[0]
    (h2, y, hg2, hu2), _ = _ffn_fwd("ffn2_fwd", x2, w["ffn2_pre_g"], ffn2, 0, w["ffn2_post_g"])
    dy, loss_lanes = _loss_fwd(y, target)

    g = {}
    (dx2, xn2, dh2, a2, dhg2, dhu2, g["ffn2_pre_g"], g["ffn2_post_g"]), _ = _ffn_bwd(
        "ffn2_bwd", x2, h2, hg2, hu2, dy, w["ffn2_pre_g"], ffn2, 0, w["ffn2_post_g"])
    ffn2_grads = _Scatter([_by_device(_weight_grad("ffn2_dw_gate", dhg2, xn2, BF16, tm=1408)),
                           _by_device(_weight_grad("ffn2_dw_up", dhu2, xn2, BF16, tm=1408)),
                           _by_device(_weight_grad("ffn2_dw_down", a2, dh2, BF16, tm=1408))])
    dmixed, g["mix_post_g"] = _norm_residual_bwd("mix_post_bwd", mixed, w["mix_post_g"], dx2)
    dcat = _matmul("mix_out_dx", dmixed, p["w_out"], "nt")
    d = {}
    d["w_out"] = _weight_grad("mix_out_dw", cat, dmixed)
    do_mla, delta, do_gdn, dgate, d["g_mla_out"], d["g_gdn"] = _mix_join_bwd(o_mla, o_gdn, proj_gate, p["g_mla_out"], p["g_gdn"], dcat)
    dq, dk, dv = _attn_bwd(q, k, v, do_mla, lse, delta)
    dq_p, dkv_p, dkpe = _mla_qkv_bwd(dq, dk, dv, tabs)
    dcqn = _matmul("mla_q_dx", dq_p, p["w_uq"], "nn")
    d["w_uq"] = _weight_grad("mla_q_dw", dq_p, cqn)
    dckvn = _matmul("mla_kv_dx", dkv_p, p["w_kv"], "nn")
    d["w_kv"] = _weight_grad("mla_kv_dw", dkv_p, ckvn)
    (dqkv_n, dgb, dbb), landed_ffn2 = _gdn_bwd(qkv_n, gb, bb, keep, do_gdn, carry=ffn2_grads)
    dab, d["a_log"], d["dt_bias"] = _gates_bwd(ab, p["a_log"], p["dt_bias"], dgb, dbb)
    dproj_qkv, d["conv"] = _gdn_conv_bwd(proj_qkv, p["conv"], dqkv_n)
    dproj_a, g["mla_q_norm_g"], g["mla_kv_norm_g"] = _mla_pre_bwd(
        proj_a, tabs, w["mla_q_norm_g"], w["mla_kv_norm_g"], dcqn, dckvn, dkpe, dab)
    dhn = [_matmul("proj_a_dx", dproj_a, p["w_a"], "nn"), _matmul("proj_qkv_dx", dproj_qkv, p["w_qkv"], "nn"),
           _matmul("proj_gate_dx", dgate, p["w_gate"], "nn")]
    d["w_a"] = _weight_grad("proj_a_dw", dproj_a, hn, tm=640)
    d["w_qkv"] = _weight_grad("proj_qkv_dw", dproj_qkv, hn)
    d["w_gate"] = _weight_grad("proj_gate_dw", dgate, hn)
    dx1, g["mix_pre_g"] = _norm_bwd_add("mix_pre_bwd", x1, w["mix_pre_g"], dhn, dx2)
    g.update(_unlayout_grads(d))
    others = [t for t, _ in OTHER.values()]
    (dx, xn1, dh1, a1, dhg1, dhu1, g["ffn1_pre_g"], g["ffn1_post_g"]), landed_others = _ffn_bwd(
        "ffn1_bwd", x, h1, hg1, hu1, dx1, w["ffn1_pre_g"], w["ffn1"], 0, w["ffn1_post_g"], carry=_Scatter([_by_device(g.pop(t)) for t in others]))
    dw_down = _weight_grad("ffn1_dw_down", a1, dh1, BF16, tm=1408)
    dw_gate, (landed_down,) = _weight_grad("ffn1_dw_gate", dhg1, xn1, BF16, tm=1408, carry=_Scatter([_by_device(dw_down)]))
    dw_up, (landed_gate,) = _weight_grad("ffn1_dw_up", dhu1, xn1, BF16, tm=1408, carry=_Scatter([_by_device(dw_gate)]))
    (landed_up,) = _exchange("scatter_last", _Scatter([_by_device(dw_up)]))
    landed = dict(zip(list(FFN_NAMES) + list(OTHER),
                      [landed_gate, landed_up, landed_down] + list(landed_ffn2) + list(landed_others)))
    return loss_lanes, dx, g, landed


MESH_AXES = ("x", "y", "c")
N_LINKS = N_DEV - 1


def _place():
    return tuple(lax.axis_index(a) for a in MESH_AXES)


def _block_of(dev):
    x, y, c = dev
    return 4 * x + 2 * y + c


def _remote_copy(src, dst, sems, k, to):
    send_sems, recv_sems = sems
    return pltpu.make_async_remote_copy(src_ref=src, dst_ref=dst, send_sem=send_sems.at[k], recv_sem=recv_sems.at[k],
                                        device_id=to, device_id_type=pl.DeviceIdType.MESH)


class _Exchange:
    def __init__(self, arrays):
        self.arrays = list(arrays)
        self.n = len(self.arrays)
        self.specs = [pl.BlockSpec(memory_space=pl.ANY)] * self.n
        self.scratch = [pltpu.SemaphoreType.DMA((self.n * N_LINKS,)), pltpu.SemaphoreType.DMA((self.n * N_LINKS,)),
                        pltpu.SemaphoreType.DMA((self.n,))]

    def split(self, refs):
        n = self.n
        return refs[:n], refs[n:2 * n], (refs[2 * n], refs[2 * n + 1]), refs[2 * n + 2]


class _Gather(_Exchange):
    def out_shape(self):
        return [jax.ShapeDtypeStruct((N_DEV,) + a.shape, a.dtype) for a in self.arrays]

    def _plan(self, ins, outs, sems, local_sems):
        x, y, c = _place()
        me, sibling = (x, y, c), (x, y, 1 - c)
        chips = [(1 - x, y), (x, 1 - y), (1 - x, 1 - y)]

        def copy(a, k, block, to, mine=False):
            src = ins[a] if mine else outs[a].at[_block_of(block)]
            return _remote_copy(src, outs[a].at[_block_of(block)], sems, a * N_LINKS + k, to)

        local = [pltpu.make_async_copy(ins[a], outs[a].at[_block_of(me)], local_sems.at[a]) for a in range(self.n)]
        first = []
        for a in range(self.n):
            first.append(copy(a, 0, me, sibling, mine=True))
            first += [copy(a, 1 + j, me, (*chip, c), mine=True) for j, chip in enumerate(chips)]
        return me, sibling, chips, c, copy, local, first

    def start(self, ins, outs, sems, local_sems):
        *_, local, first = self._plan(ins, outs, sems, local_sems)
        for cp in local + first:
            cp.start()

    def finish(self, ins, outs, sems, local_sems):
        me, sibling, chips, c, copy, local, first = self._plan(ins, outs, sems, local_sems)
        passed = []
        for j, chip in enumerate(chips):
            for a in range(self.n):
                copy(a, 1 + j, (*chip, c), me).wait_recv()
                passed.append(copy(a, 4 + j, (*chip, c), sibling))
                passed[-1].start()
        for a in range(self.n):
            copy(a, 0, sibling, me).wait_recv()
            for j, chip in enumerate(chips):
                copy(a, 4 + j, (*chip, 1 - c), me).wait_recv()
        for cp in first + passed:
            cp.wait_send()
        for cp in local:
            cp.wait()


class _Scatter(_Exchange):
    def out_shape(self):
        return [jax.ShapeDtypeStruct(a.shape, a.dtype) for a in self.arrays]

    def _plan(self, ins, outs, sems, local_sems):
        x, y, c = _place()
        me = _block_of((x, y, c))

        def peer(r):
            return (1 - x if r & 4 else x, 1 - y if r & 2 else y, 1 - c if r & 1 else c)

        local = [pltpu.make_async_copy(ins[a].at[me], outs[a].at[me], local_sems.at[a]) for a in range(self.n)]
        sends = [_remote_copy(ins[a].at[_block_of(peer(r))], outs[a].at[me], sems, a * N_LINKS + r - 1, peer(r))
                 for a in range(self.n) for r in range(1, N_DEV)]
        arrivals = [_remote_copy(ins[a].at[me], outs[a].at[_block_of(peer(r))], sems, a * N_LINKS + r - 1, peer(r))
                    for a in range(self.n) for r in range(1, N_DEV)]
        return local, sends, arrivals

    def start(self, ins, outs, sems, local_sems):
        local, sends, _ = self._plan(ins, outs, sems, local_sems)
        for cp in local + sends:
            cp.start()

    def finish(self, ins, outs, sems, local_sems):
        local, sends, arrivals = self._plan(ins, outs, sems, local_sems)
        for cp in arrivals:
            cp.wait_recv()
        for cp in sends:
            cp.wait_send()
        for cp in local:
            cp.wait()


def _exchange(name, plan):
    def body(*refs):
        parts = plan.split(refs)
        plan.start(*parts)
        plan.finish(*parts)

    return pl.pallas_call(
        body, name=name,
        in_specs=plan.specs,
        out_specs=plan.specs,
        out_shape=plan.out_shape(),
        scratch_shapes=plan.scratch,
    )(*plan.arrays)


def _call_carrying(body, plan, operands, *, name, grid, in_specs, out_specs, out_shape, scratch_shapes, compiler_params):
    if plan is None:
        outs = pl.pallas_call(body, name=name, grid=grid, in_specs=in_specs, out_specs=out_specs, out_shape=out_shape,
                              scratch_shapes=scratch_shapes, compiler_params=compiler_params)(*operands)
        return outs, []
    n_i, n_o, n_s, k = len(in_specs), len(out_specs), len(scratch_shapes), plan.n

    def whole(*refs):
        cut = [n_i, n_i + k, n_i + k + n_o, n_i + 2 * k + n_o, n_i + 2 * k + n_o + n_s]
        own_in, ex_in, own_out, ex_out, own_scr, ex_scr = (refs[a:b] for a, b in zip([0] + cut, cut + [len(refs)]))
        parts = plan.split(ex_in + ex_out + ex_scr)
        first = last = True
        for axis, size in enumerate(grid):
            first = first & (pl.program_id(axis) == 0)
            last = last & (pl.program_id(axis) == size - 1)

        @pl.when(first)
        def _():
            plan.start(*parts)

        body(*own_in, *own_out, *own_scr)

        @pl.when(last)
        def _():
            plan.finish(*parts)

    outs = pl.pallas_call(
        whole, name=name, grid=grid,
        in_specs=list(in_specs) + plan.specs, out_specs=list(out_specs) + plan.specs,
        out_shape=list(out_shape) + plan.out_shape(), scratch_shapes=list(scratch_shapes) + plan.scratch,
        compiler_params=compiler_params,
    )(*operands, *plan.arrays)
    return outs[:n_o], outs[n_o:]


def _row_tile(rows, target=256):
    best = rows
    for cand in range(16, min(rows, target) + 1, 16):
        if rows % cand == 0:
            best = cand
    return best


def _sum_blocks(name, blocks):
    rows, width = blocks.shape[-2:]
    tm = _row_tile(rows)

    def body(x_ref, o_ref):
        acc = x_ref[0].astype(F32)
        for d in range(1, N_DEV):
            acc = acc + x_ref[d].astype(F32)
        o_ref[...] = acc

    return pl.pallas_call(
        body, name=name,
        grid=(rows // tm,),
        in_specs=[pl.BlockSpec((N_DEV, tm, width), lambda i: (0, i, 0))],
        out_specs=pl.BlockSpec((tm, width), lambda i: (i, 0)),
        out_shape=jax.ShapeDtypeStruct((rows, width), F32),
        compiler_params=pltpu.CompilerParams(dimension_semantics=("parallel",)),
    )(blocks)


def _all_reduce_small(name, vec):
    rows, width = vec.shape

    def body(x_ref, o_ref, all_ref, send_sems, recv_sems):
        x, y, c = _place()
        me = _block_of((x, y, c))
        all_ref[me] = x_ref[...]

        def peer(r):
            return (1 - x if r & 4 else x, 1 - y if r & 2 else y, 1 - c if r & 1 else c)

        def copy(r, block):
            return _remote_copy(x_ref, all_ref.at[block], (send_sems, recv_sems), r - 1, peer(r))

        sends = [copy(r, me) for r in range(1, N_DEV)]
        for cp in sends:
            cp.start()
        for r in range(1, N_DEV):
            copy(r, _block_of(peer(r))).wait_recv()
        for cp in sends:
            cp.wait_send()
        acc = all_ref[0]
        for d in range(1, N_DEV):
            acc = acc + all_ref[d]
        o_ref[...] = acc

    return pl.pallas_call(
        body, name=name,
        in_specs=[pl.BlockSpec(memory_space=pltpu.VMEM)],
        out_specs=pl.BlockSpec(memory_space=pltpu.VMEM),
        out_shape=jax.ShapeDtypeStruct((rows, width), F32),
        scratch_shapes=[pltpu.VMEM((N_DEV, rows, width), F32), pltpu.SemaphoreType.DMA((N_LINKS,)), pltpu.SemaphoreType.DMA((N_LINKS,))],
    )(vec)


def _adamw(name, w, g, m, v):
    def fn(rows, consts):
        wv, gv, mv, vv = rows
        m2 = ADAM_B1 * mv + (1.0 - ADAM_B1) * gv
        v2 = ADAM_B2 * vv + (1.0 - ADAM_B2) * jnp.square(gv)
        m_hat = m2 / (1.0 - ADAM_B1 ** ADAM_STEP)
        v_hat = v2 / (1.0 - ADAM_B2 ** ADAM_STEP)
        return [-ADAM_LR * (m_hat / (jnp.sqrt(v_hat) + ADAM_EPS) + ADAM_WD * wv), m2, v2], []

    return _rowwise(name, fn, [w, g, m, v], [], [(w.shape[1], F32)] * 3, tm=_row_tile(w.shape[0]))


ROW = 1024
FFN_NAMES = ("ffn1_w_gate", "ffn1_w_up", "ffn1_w_down", "ffn2_w_gate", "ffn2_w_up", "ffn2_w_down")
OTHER = {"w_in": ("w_in_t", True), "mla_w_uq": ("uq_t", True), "mla_w_ukv": ("ukv_t", True), "w_out": ("w_out", False)}
BY_COLUMNS = ("ffn1_w_gate", "ffn1_w_up", "ffn2_w_gate", "ffn2_w_up", "w_in", "mla_w_uq", "mla_w_ukv")
SMALL = {
    "ffn1_pre_g": (1024, 1024), "ffn1_post_g": (1024, 1024), "mix_pre_g": (1024, 1024), "mla_q_norm_g": (256, 256),
    "mla_kv_norm_g": (128, 128), "mla_out_g": (512, 512), "gdn_a_log": (8, 128), "gdn_dt_bias": (8, 128),
    "gdn_norm_g": (64, 128), "mix_post_g": (1024, 1024), "ffn2_pre_g": (1024, 1024), "ffn2_post_g": (1024, 1024),
}
CONV_SHAPE = (GDN_CONV, 3 * N_HEADS * GDN_D)
CONV_SHARD = (GDN_CONV, CONV_SHAPE[1] // N_DEV)
CONV_LANES = CONV_SHAPE[0] * CONV_SHAPE[1]
SMALL_ROWS = 8
REDUCE_ROWS = 16


def _pack_small(vecs, conv, rows):
    parts = [_pad_lanes(vecs[n].reshape(1, -1), 0, r) for n, (_, r) in SMALL.items()]
    parts.append(conv.reshape(1, -1))
    flat = jnp.concatenate(parts, axis=1)
    return _pad_lanes(flat, 0, rows * ROW).reshape(rows, ROW)


def _unpack_small(buf):
    flat = buf.reshape(1, -1)
    out, at = {}, 0
    for n, (w, r) in SMALL.items():
        out[n] = flat[:, at:at + w]
        at += r
    return out, flat[0, at:]


def kernel(x, positions, ffn1_pre_g, ffn1_w_gate, ffn1_w_up, ffn1_w_down, ffn1_post_g, mix_pre_g, w_in, mla_q_norm_g, mla_w_uq, mla_kv_norm_g, mla_w_ukv, mla_out_g, gdn_conv_w, gdn_a_log, gdn_dt_bias, gdn_norm_g, w_out, mix_post_g, ffn2_pre_g, ffn2_w_gate, ffn2_w_up, ffn2_w_down, ffn2_post_g, loss_target, m_ffn1_pre_g, m_ffn1_w_gate, m_ffn1_w_up, m_ffn1_w_down, m_ffn1_post_g, m_mix_pre_g, m_w_in, m_mla_q_norm_g, m_mla_w_uq, m_mla_kv_norm_g, m_mla_w_ukv, m_mla_out_g, m_gdn_conv_w, m_gdn_a_log, m_gdn_dt_bias, m_gdn_norm_g, m_w_out, m_mix_post_g, m_ffn2_pre_g, m_ffn2_w_gate, m_ffn2_w_up, m_ffn2_w_down, m_ffn2_post_g, v_ffn1_pre_g, v_ffn1_w_gate, v_ffn1_w_up, v_ffn1_w_down, v_ffn1_post_g, v_mix_pre_g, v_w_in, v_mla_q_norm_g, v_mla_w_uq, v_mla_kv_norm_g, v_mla_w_ukv, v_mla_out_g, v_gdn_conv_w, v_gdn_a_log, v_gdn_dt_bias, v_gdn_norm_g, v_w_out, v_mix_post_g, v_ffn2_pre_g, v_ffn2_w_gate, v_ffn2_w_up, v_ffn2_w_down, v_ffn2_post_g):
    given = dict(locals())
    order = ["ffn1_pre_g", "ffn1_w_gate", "ffn1_w_up", "ffn1_w_down", "ffn1_post_g", "mix_pre_g", "w_in", "mla_q_norm_g",
             "mla_w_uq", "mla_kv_norm_g", "mla_w_ukv", "mla_out_g", "gdn_conv_w", "gdn_a_log", "gdn_dt_bias", "gdn_norm_g",
             "w_out", "mix_post_g", "ffn2_pre_g", "ffn2_w_gate", "ffn2_w_up", "ffn2_w_down", "ffn2_post_g"]
    assert sorted(order) == sorted(list(FFN_NAMES) + list(OTHER) + list(SMALL) + ["gdn_conv_w"])

    def drop_depth(a):
        return a[0] if a.ndim == 3 else a

    wts = {n: drop_depth(given[n]) for n in order}
    mom = {n: drop_depth(given["m_" + n]) for n in order}
    var = {n: drop_depth(given["v_" + n]) for n in order}
    me = _block_of(_place())

    def wire(n):
        return (wts[n].T if n in BY_COLUMNS else wts[n]).astype(BF16)

    (ffn1,) = _exchange("gather_first", _Gather([jnp.stack([wire(n) for n in FFN_NAMES[:3]])]))
    mid = _Gather([wire(n) for n in ("w_in", "mla_w_uq", "mla_w_ukv")])
    late = _Gather([jnp.stack([wire(n) for n in FFN_NAMES[3:]]), wire("w_out")])
    conv_at = lax.dynamic_update_slice(jnp.zeros((N_DEV, CONV_SHARD[0] * CONV_SHARD[1]), F32),
                                       wts["gdn_conv_w"].reshape(1, -1), (me, 0))
    conv_all = _all_reduce_small("gather_conv", _pad_lanes(conv_at.reshape(1, -1), 0, SMALL_ROWS * ROW).reshape(SMALL_ROWS, ROW))
    full = {n: wts[n] for n in SMALL}
    full["ffn1"] = ffn1
    full["gdn_conv_w"] = conv_all.reshape(-1)[:CONV_LANES].reshape((N_DEV,) + CONV_SHARD).transpose(1, 0, 2).reshape(CONV_SHAPE)

    loss_lanes, dx, grads, landed = _local_step(x[0], positions[0], loss_target[0], full, mid, late)
    loss = lax.psum(jnp.sum(loss_lanes), MESH_AXES)

    sums = {n: _sum_blocks("sum_" + n, blocks) for n, blocks in landed.items()}
    grad = {n: (sums[n].T if n in BY_COLUMNS else sums[n]) for n in sums}
    small_sum = _all_reduce_small("reduce_small", _pack_small(grads, grads["gdn_conv_w"].reshape(-1), REDUCE_ROWS))
    small_grad, conv_grad_full = _unpack_small(small_sum)
    grad.update(small_grad)
    grad["gdn_conv_w"] = lax.dynamic_slice(conv_grad_full[:CONV_LANES].reshape(CONV_SHAPE), (0, me * CONV_SHARD[1]), CONV_SHARD)

    outs = {"grad": grad, "delta": {}, "new_m": {}, "new_v": {}}
    for n in list(FFN_NAMES) + list(OTHER):
        outs["delta"][n], outs["new_m"][n], outs["new_v"][n] = _adamw("adamw_" + n, wts[n], grad[n], mom[n], var[n])
    small = [_pack_small(s, s["gdn_conv_w"].reshape(-1), SMALL_ROWS) for s in (wts, grad, mom, var)]
    for kind, s in zip(("delta", "new_m", "new_v"), _adamw("adamw_small", *small)):
        vecs, conv = _unpack_small(s)
        outs[kind].update(vecs)
        outs[kind]["gdn_conv_w"] = conv[:CONV_SHARD[0] * CONV_SHARD[1]].reshape(CONV_SHARD)
    result = [loss, dx[None]]
    for kind in ("grad", "delta", "new_m", "new_v"):
        result += [outs[kind][n].reshape(given[n].shape) for n in order]
    return tuple(result)
```

```python
import jax
import jax.numpy as jnp
from jax import lax
from jax.experimental import pallas as pl
from jax.experimental.pallas import tpu as pltpu

F32 = jnp.float32
BF16 = jnp.bfloat16
HI = lax.Precision.HIGH

N_DEV = 8
D_MODEL = 1024
D_FF = 2816
N_HEADS = 8
SLOT = 128
MLA_Q_RANK = 256
MLA_KV_RANK = 128
MLA_NOPE = 64
MLA_ROPE = 32
MLA_V = 64
GDN_D = 64
GDN_CONV = 4
GDN_CHUNK = 64
ROPE_THETA = 10000.0
EPS = 1e-6
ADAM_LR, ADAM_B1, ADAM_B2, ADAM_EPS, ADAM_WD, ADAM_STEP = 0.001, 0.9, 0.999, 1e-08, 0.01, 10


def _dot(a, b, ca, cb, precision=None):
    lead = a.ndim - 2
    batch = tuple(range(lead))
    return lax.dot_general(a, b, (((lead + ca,), (lead + cb,)), (batch, batch)), precision=precision,
                           preferred_element_type=F32)


def _nn(a, b, precision=None):
    return _dot(a, b, 1, 0, precision)


def _nt(a, b, precision=None):
    return _dot(a, b, 1, 1, precision)


def _tn(a, b, precision=None):
    return _dot(a, b, 0, 0, precision)


def _sigmoid(x):
    return 1.0 / (1.0 + jnp.exp(-x))


def _silu(x):
    return x * _sigmoid(x)


def _rms(x, g, n):
    ms = jnp.sum(x * x, axis=-1, keepdims=True) * (1.0 / n)
    return x * lax.rsqrt(ms + EPS) * g


def _chunk_masks():
    c = GDN_CHUNK
    i = lax.broadcasted_iota(jnp.int32, (c, c), 0)
    j = lax.broadcasted_iota(jnp.int32, (c, c), 1)
    lower = i >= j
    strict = i > j
    eye = (i == j).astype(F32)
    blocks = []
    b = 1
    while b < c:
        same = (i // (2 * b)) == (j // (2 * b))
        blocks.append(same & ((i % (2 * b)) >= b) & ((j % (2 * b)) < b))
        b *= 2
    return lower, strict, eye, blocks


def _unit_lower_inverse(low, eye, blocks):
    t = eye - jnp.where(blocks[0], low, 0.0)
    for m in blocks[1:]:
        lo = jnp.where(m, low, 0.0)
        t = t - _nn(t, _nn(lo, t, HI), HI)
    return t


@jax.custom_vjp
def _known_inverse(low, tinv):
    return tinv


def _known_inverse_fwd(low, tinv):
    return tinv, tinv


def _known_inverse_bwd(tinv, dt):
    return -_tn(tinv, _nt(dt, tinv, HI), HI), jnp.zeros_like(tinv)


_known_inverse.defvjp(_known_inverse_fwd, _known_inverse_bwd)


def _gdn_chunk(q, k, v, gc, bb, s, masks, tinv=None):
    lower, strict, eye, blocks = masks
    qs = q * (GDN_D ** -0.5)
    gct = jnp.swapaxes(gc, -1, -2)
    decay = jnp.exp(jnp.where(lower, gc - gct, -1e30))
    kb = k * bb
    low = jnp.where(strict, _nt(kb, k, HI) * decay, 0.0)
    tinv = _unit_lower_inverse(low, eye, blocks) if tinv is None else _known_inverse(low, tinv)
    eg = jnp.exp(gc)
    w = _nn(tinv, kb * eg, HI)
    u = _nn(tinv, v * bb, HI)
    attn = _nt(qs, k, HI) * decay
    last = lax.broadcasted_iota(jnp.int32, gc.shape[-2:], 0) == GDN_CHUNK - 1
    g_end = jnp.sum(jnp.where(last, gc, 0.0), axis=-2, keepdims=True)
    k_dec = k * jnp.exp(g_end - gc)
    v_new = u - _nn(w, s, HI)
    o = _nn(qs * eg, s, HI) + _nn(attn, v_new, HI)
    s_new = s * jnp.exp(g_end) + _tn(k_dec, v_new, HI)
    return o, s_new, tinv


GDN_GROUP = 8
GDN_GROUPS = N_HEADS // GDN_GROUP


def _group_heads(ref):
    return jnp.stack([ref[:, pl.ds(j * SLOT, GDN_D)] for j in range(GDN_GROUP)])


def _ungroup_heads(ref, val):
    pad = jnp.zeros((GDN_CHUNK, SLOT - GDN_D), F32)
    for j in range(GDN_GROUP):
        ref[:, pl.ds(j * SLOT, GDN_D)] = val[j]
        ref[:, pl.ds(j * SLOT + GDN_D, SLOT - GDN_D)] = pad


def _gdn_fwd(qkv, gb, bb, carry=None):
    t = qkv.shape[0]
    n_chunks = t // GDN_CHUNK
    d = GDN_D

    def body(q_ref, k_ref, v_ref, g_ref, b_ref, o_ref, keep_ref, s_ref):
        @pl.when(pl.program_id(1) == 0)
        def _():
            s_ref[...] = jnp.zeros_like(s_ref)

        s = s_ref[...]
        keep_ref[:, 0, 0] = s
        o, s_new, tinv = _gdn_chunk(*[_group_heads(r) for r in (q_ref, k_ref, v_ref, g_ref, b_ref)], s, _chunk_masks())
        keep_ref[:, 0, 1] = tinv
        s_ref[...] = s_new
        _ungroup_heads(o_ref, o)

    def spec(kind=0):
        return pl.BlockSpec((GDN_CHUNK, GDN_GROUP * SLOT), lambda h, n: (n, kind * GDN_GROUPS + h))

    return _call_carrying(
        body, carry, (qkv, qkv, qkv, gb, bb), name="gdn_fwd",
        grid=(GDN_GROUPS, n_chunks),
        in_specs=[spec(0), spec(1), spec(2), spec(), spec()],
        out_specs=[spec(), pl.BlockSpec((GDN_GROUP, 1, 2, d, d), lambda h, n: (h, n, 0, 0, 0))],
        out_shape=[jax.ShapeDtypeStruct((t, N_HEADS * SLOT), F32), jax.ShapeDtypeStruct((N_HEADS, n_chunks, 2, d, d), F32)],
        scratch_shapes=[pltpu.VMEM((GDN_GROUP, d, d), F32)],
        compiler_params=pltpu.CompilerParams(dimension_semantics=("arbitrary", "arbitrary")),
    )


def _gdn_bwd(qkv, gb, bb, keep, do, carry=None):
    t = qkv.shape[0]
    n_chunks = t // GDN_CHUNK
    d = GDN_D

    def body(q_ref, k_ref, v_ref, g_ref, b_ref, keep_ref, do_ref, dqkv_ref, dg_ref, db_ref, ds_ref):
        @pl.when(pl.program_id(1) == 0)
        def _():
            ds_ref[...] = jnp.zeros_like(ds_ref)

        masks = _chunk_masks()
        tinv = keep_ref[:, 0, 1]
        _, pull = jax.vjp(lambda *a: _gdn_chunk(*a, masks, tinv)[:2],
                          *[_group_heads(r) for r in (q_ref, k_ref, v_ref, g_ref, b_ref)], keep_ref[:, 0, 0])
        dq, dk, dv, dg, db, ds = pull((_group_heads(do_ref), ds_ref[...]))
        ds_ref[...] = ds
        for i, val in enumerate((dq, dk, dv)):
            _ungroup_heads(dqkv_ref.at[i], val)
        _ungroup_heads(dg_ref, dg)
        _ungroup_heads(db_ref, db)

    def spec(kind=0):
        return pl.BlockSpec((GDN_CHUNK, GDN_GROUP * SLOT), lambda h, n: (n_chunks - 1 - n, kind * GDN_GROUPS + h))

    return _call_carrying(
        body, carry, (qkv, qkv, qkv, gb, bb, keep, do), name="gdn_bwd",
        grid=(GDN_GROUPS, n_chunks),
        in_specs=[spec(0), spec(1), spec(2), spec(), spec(),
                  pl.BlockSpec((GDN_GROUP, 1, 2, d, d), lambda h, n: (h, n_chunks - 1 - n, 0, 0, 0)), spec()],
        out_specs=[pl.BlockSpec((3, GDN_CHUNK, GDN_GROUP * SLOT), lambda h, n: (0, n_chunks - 1 - n, h)), spec(), spec()],
        out_shape=[jax.ShapeDtypeStruct((3, t, N_HEADS * SLOT), F32)] + [jax.ShapeDtypeStruct((t, N_HEADS * SLOT), F32)] * 2,
        scratch_shapes=[pltpu.VMEM((GDN_GROUP, d, d), F32)],
        compiler_params=pltpu.CompilerParams(dimension_semantics=("arbitrary", "arbitrary")),
    )


def _rowwise(name, fn, rows, consts, outs, sums=(), tm=512):
    rows = [x if isinstance(x, tuple) else (x, x.shape[1], 0) for x in rows]
    t = rows[0][0].shape[0]
    tm = min(tm, t)
    steps = t // tm
    n_r, n_c, n_o, n_s = len(rows), len(consts), len(outs), len(sums)

    def window(width, block):
        return pl.BlockSpec((tm, width), lambda i: (i, block))

    def body(*refs):
        r, c = refs[:n_r], refs[n_r:n_r + n_c]
        o, s = refs[n_r + n_c:n_r + n_c + n_o], refs[n_r + n_c + n_o:]
        vals, tot = fn([x[...] for x in r], [x[...] for x in c])
        for ref, val in zip(o, vals):
            ref[...] = val.astype(ref.dtype)
        if n_s:
            @pl.when(pl.program_id(0) == 0)
            def _():
                for ref in s:
                    ref[...] = jnp.zeros_like(ref)

            for ref, val in zip(s, tot):
                ref[...] += val

    return pl.pallas_call(
        body, name=name,
        grid=(steps,),
        in_specs=[window(w, b) for _, w, b in rows] + [pl.BlockSpec(x.shape, lambda i: (0, 0)) for x in consts],
        out_specs=[pl.BlockSpec((tm, w), lambda i: (i, 0)) for w, _ in outs]
        + [pl.BlockSpec((1, w), lambda i: (0, 0)) for w in sums],
        out_shape=[jax.ShapeDtypeStruct((t, w), dt) for w, dt in outs]
        + [jax.ShapeDtypeStruct((1, w), F32) for w in sums],
        compiler_params=pltpu.CompilerParams(dimension_semantics=("arbitrary",)),
    )(*[x for x, _, _ in rows], *consts)


def _tile(dim, target):
    if dim <= target:
        return dim
    best = None
    for cand in range(128, target + 1, 128):
        if dim % cand == 0:
            best = cand
    assert best is not None, (dim, target)
    return best


def _matmul(name, a, b, mode, out_dtype=F32, tm=1024, tn=1024, tk=2048, carry=None):
    if mode == "nn":
        (m, k), n = a.shape, b.shape[1]
    elif mode == "nt":
        (m, k), n = a.shape, b.shape[0]
    else:
        (k, m), n = a.shape, b.shape[1]
    tm, tn, tk = _tile(m, tm), _tile(n, tn), _tile(k, tk)
    k_steps = k // tk
    product = {"nn": _nn, "nt": _nt, "tn": _tn}[mode]

    def body(a_ref, b_ref, o_ref, acc_ref):
        part = product(a_ref[...].astype(BF16), b_ref[...].astype(BF16))
        if k_steps == 1:
            o_ref[...] = part.astype(o_ref.dtype)
        else:
            kk = pl.program_id(2)

            @pl.when(kk == 0)
            def _():
                acc_ref[...] = part

            @pl.when(kk > 0)
            def _():
                acc_ref[...] += part

            @pl.when(kk == k_steps - 1)
            def _():
                o_ref[...] = acc_ref[...].astype(o_ref.dtype)

    a_spec = pl.BlockSpec((tk, tm), lambda i, j, kk: (kk, i)) if mode == "tn" else pl.BlockSpec((tm, tk), lambda i, j, kk: (i, kk))
    b_spec = pl.BlockSpec((tn, tk), lambda i, j, kk: (j, kk)) if mode == "nt" else pl.BlockSpec((tk, tn), lambda i, j, kk: (kk, j))
    (out,), carried = _call_carrying(
        body, carry, (a, b), name=name,
        grid=(m // tm, n // tn, k_steps),
        in_specs=[a_spec, b_spec],
        out_specs=[pl.BlockSpec((tm, tn), lambda i, j, kk: (i, j))],
        out_shape=[jax.ShapeDtypeStruct((m, n), out_dtype)],
        scratch_shapes=[pltpu.VMEM((tm, tn) if k_steps > 1 else (8, 128), F32)],
        compiler_params=pltpu.CompilerParams(dimension_semantics=("arbitrary", "arbitrary", "arbitrary")),
    )
    return out if carry is None else (out, carried)


FFN_TM = 512
FFN_BWD_TM = 256
FFN_BLOCKS = 4
FFN_GATE, FFN_UP, FFN_DOWN = 0, 1, 2


def _ffn_weight_specs(ffn_w, first):
    _, _, rows, dm = ffn_w.shape

    def spec(k):
        return pl.BlockSpec((FFN_BLOCKS, None, rows, dm), lambda i, j: (j, first + k, 0, 0))

    return [spec(FFN_GATE), spec(FFN_UP), spec(FFN_DOWN)], FFN_BLOCKS * rows


def _ffn_fwd(name, x, g_pre, ffn_w, first, g_post, carry=None):
    t, dm = x.shape
    tm = min(FFN_TM, t)
    w_specs, tf = _ffn_weight_specs(ffn_w, first)
    f_steps = N_DEV // FFN_BLOCKS

    def body(x_ref, gpre_ref, wg_ref, wu_ref, wd_ref, gpost_ref, h_ref, y_ref, hg_ref, hu_ref, xn_ref, acc_ref):
        j = pl.program_id(1)

        @pl.when(j == 0)
        def _():
            xn_ref[...] = _rms(x_ref[...], gpre_ref[...], dm).astype(BF16)
            acc_ref[...] = jnp.zeros_like(acc_ref)

        xn = xn_ref[...]
        wg, wu, wd = (r[...].reshape(tf, dm) for r in (wg_ref, wu_ref, wd_ref))
        hg, hu = _nt(xn, wg), _nt(xn, wu)
        hg_ref[...] = hg.astype(BF16)
        hu_ref[...] = hu.astype(BF16)
        a = _silu(hg) * hu
        acc_ref[...] += _nn(a.astype(BF16), wd)

        @pl.when(j == f_steps - 1)
        def _():
            h = acc_ref[...]
            h_ref[...] = h
            y_ref[...] = x_ref[...] + 0.5 * _rms(h, gpost_ref[...], dm)

    row = pl.BlockSpec((tm, dm), lambda i, j: (i, 0))
    vec = pl.BlockSpec((1, dm), lambda i, j: (0, 0))
    wide = pl.BlockSpec((tm, tf), lambda i, j: (i, j))
    return _call_carrying(
        body, carry, (x, g_pre, ffn_w, ffn_w, ffn_w, g_post), name=name,
        grid=(t // tm, f_steps),
        in_specs=[row, vec, *w_specs, vec],
        out_specs=[row, row, wide, wide],
        out_shape=[jax.ShapeDtypeStruct((t, dm), F32)] * 2 + [jax.ShapeDtypeStruct((t, f_steps * tf), BF16)] * 2,
        scratch_shapes=[pltpu.VMEM((tm, dm), BF16), pltpu.VMEM((tm, dm), F32)],
        compiler_params=pltpu.CompilerParams(dimension_semantics=("arbitrary", "arbitrary")),
    )


def _ffn_bwd(name, x, h, hg, hu, dy, g_pre, ffn_w, first, g_post, carry=None):
    t, dm = x.shape
    tm = min(FFN_BWD_TM, t)
    w_specs, tf = _ffn_weight_specs(ffn_w, first)
    f_steps = N_DEV // FFN_BLOCKS
    f = f_steps * tf

    def post(hv, g):
        return 0.5 * _rms(hv, g, dm)

    def pre(xv, g):
        return _rms(xv, g, dm)

    def body(x_ref, h_ref, dy_ref, hg_ref, hu_ref, gpre_ref, wg_ref, wu_ref, wd_ref, gpost_ref,
             dx_ref, xn_ref, dh_ref, a_ref, dhg_ref, dhu_ref, dgpre_ref, dgpost_ref, acc_ref):
        i, j = pl.program_id(0), pl.program_id(1)

        @pl.when((i == 0) & (j == 0))
        def _():
            dgpre_ref[...] = jnp.zeros_like(dgpre_ref)
            dgpost_ref[...] = jnp.zeros_like(dgpost_ref)

        @pl.when(j == 0)
        def _():
            xn_ref[...] = pre(x_ref[...], gpre_ref[...]).astype(BF16)
            _, pull = jax.vjp(post, h_ref[...], gpost_ref[...])
            dh, dg = pull(dy_ref[...])
            dh_ref[...] = dh.astype(BF16)
            dgpost_ref[...] += dg
            acc_ref[...] = jnp.zeros_like(acc_ref)

        wg, wu, wd = (r[...].reshape(tf, dm) for r in (wg_ref, wu_ref, wd_ref))
        hg, hu = hg_ref[...].astype(F32), hu_ref[...].astype(F32)
        da = _nt(dh_ref[...], wd)
        sig = _sigmoid(hg)
        act = hg * sig
        dhu = (da * act).astype(BF16)
        dhg = (da * hu * (sig * (1.0 + hg * (1.0 - sig)))).astype(BF16)
        a_ref[...] = (act * hu).astype(BF16)
        dhg_ref[...] = dhg
        dhu_ref[...] = dhu
        acc_ref[...] += _nn(dhg, wg) + _nn(dhu, wu)

        @pl.when(j == f_steps - 1)
        def _():
            _, pull = jax.vjp(pre, x_ref[...], gpre_ref[...])
            dx, dg = pull(acc_ref[...])
            dx_ref[...] = dy_ref[...] + dx
            dgpre_ref[...] += dg

    row = pl.BlockSpec((tm, dm), lambda i, j: (i, 0))
    vec = pl.BlockSpec((1, dm), lambda i, j: (0, 0))
    wide = pl.BlockSpec((tm, tf), lambda i, j: (i, j))
    return _call_carrying(
        body, carry, (x, h, dy, hg, hu, g_pre, ffn_w, ffn_w, ffn_w, g_post), name=name,
        grid=(t // tm, f_steps),
        in_specs=[row, row, row, wide, wide, vec, *w_specs, vec],
        out_specs=[row, row, row, wide, wide, wide, vec, vec],
        out_shape=[jax.ShapeDtypeStruct((t, dm), F32), jax.ShapeDtypeStruct((t, dm), BF16), jax.ShapeDtypeStruct((t, dm), BF16),
                   jax.ShapeDtypeStruct((t, f), BF16), jax.ShapeDtypeStruct((t, f), BF16), jax.ShapeDtypeStruct((t, f), BF16),
                   jax.ShapeDtypeStruct((1, dm), F32), jax.ShapeDtypeStruct((1, dm), F32)],
        scratch_shapes=[pltpu.VMEM((tm, dm), F32)],
        compiler_params=pltpu.CompilerParams(dimension_semantics=("arbitrary", "arbitrary")),
    )


ATT_T = 512
ATT_GROUP = 2
ATT_SCALE = (MLA_NOPE + MLA_ROPE) ** -0.5


def _stack_slots(ref, group):
    return jnp.stack([ref[:, pl.ds(j * SLOT, SLOT)] for j in range(group)])


def _unstack_slots(ref, val):
    for j in range(val.shape[0]):
        ref[:, pl.ds(j * SLOT, SLOT)] = val[j].astype(ref.dtype)


def _scores(q, k, diagonal):
    s = _nt(q, k) * ATT_SCALE
    if diagonal:
        row = lax.broadcasted_iota(jnp.int32, s.shape[1:], 0)
        col = lax.broadcasted_iota(jnp.int32, s.shape[1:], 1)
        s = jnp.where(col <= row, s, -1e30)
    return s


def _attn_pairs(steps, q_major):
    pairs = ([(qi, ki) for qi in range(steps) for ki in range(qi + 1)] if q_major
             else [(qi, ki) for ki in range(steps) for qi in range(ki, steps)])
    return jnp.array([p[0] for p in pairs], jnp.int32), jnp.array([p[1] for p in pairs], jnp.int32)


def _attn_specs(tile):
    width = ATT_GROUP * SLOT
    return (pl.BlockSpec((tile, width), lambda h, p, qt, kt: (qt[p], h)),
            pl.BlockSpec((tile, width), lambda h, p, qt, kt: (kt[p], h)))


def _attn_fwd(q, k, v):
    t = q.shape[0]
    tile = min(ATT_T, t)
    steps = t // tile
    g = ATT_GROUP

    strip = min(SLOT, tile)

    def body(qt_ref, kt_ref, q_ref, k_ref, v_ref, o_ref, lse_ref, m_ref, l_ref, alpha_ref, acc_ref, s_ref, p_ref):
        qi, ki = qt_ref[pl.program_id(1)], kt_ref[pl.program_id(1)]

        @pl.when(ki == 0)
        def _():
            m_ref[...] = jnp.full_like(m_ref, -1e30)
            l_ref[...] = jnp.zeros_like(l_ref)
            acc_ref[...] = jnp.zeros_like(acc_ref)

        def step(diagonal):
            s_ref[...] = _nt(_stack_slots(k_ref, g), _stack_slots(q_ref, g))
            for j in range(tile // strip):
                c = pl.ds(j * strip, strip)
                s = s_ref[:, :, c] * ATT_SCALE
                if diagonal:
                    key = lax.broadcasted_iota(jnp.int32, s.shape[1:], 0)
                    query = lax.broadcasted_iota(jnp.int32, s.shape[1:], 1) + j * strip
                    s = jnp.where(key <= query, s, -1e30)
                m_old = m_ref[:, :, c]
                m_new = jnp.maximum(m_old, jnp.max(s, axis=1, keepdims=True))
                p = jnp.exp(s - m_new)
                alpha = jnp.exp(m_old - m_new)
                l_ref[:, :, c] = alpha * l_ref[:, :, c] + jnp.sum(p, axis=1, keepdims=True)
                alpha_ref[:, :, c] = alpha
                m_ref[:, :, c] = m_new
                p_ref[:, :, c] = p.astype(BF16)
            acc_ref[...] = acc_ref[...] * alpha_ref[...] + _tn(_stack_slots(v_ref, g), p_ref[...])

        @pl.when(ki < qi)
        def _():
            step(False)

        @pl.when(ki == qi)
        def _():
            step(True)
            out = acc_ref[...] / l_ref[...]
            lse = jnp.broadcast_to(m_ref[...] + jnp.log(l_ref[...]), out.shape)
            for j in range(g):
                o_ref[:, pl.ds(j * SLOT, SLOT)] = out[j].T
                lse_ref[:, pl.ds(j * SLOT, SLOT)] = lse[j].T

    q_spec, k_spec = _attn_specs(tile)
    tables = _attn_pairs(steps, True)
    return pl.pallas_call(
        body, name="attn_fwd",
        grid_spec=pltpu.PrefetchScalarGridSpec(
            num_scalar_prefetch=2, grid=(N_HEADS // g, tables[0].shape[0]),
            in_specs=[q_spec, k_spec, k_spec], out_specs=[q_spec, q_spec],
            scratch_shapes=[pltpu.VMEM((g, 1, tile), F32), pltpu.VMEM((g, 1, tile), F32), pltpu.VMEM((g, 1, tile), F32),
                            pltpu.VMEM((g, SLOT, tile), F32), pltpu.VMEM((g, tile, tile), F32), pltpu.VMEM((g, tile, tile), BF16)]),
        out_shape=[jax.ShapeDtypeStruct((t, N_HEADS * SLOT), F32)] * 2,
        compiler_params=pltpu.CompilerParams(dimension_semantics=("parallel", "arbitrary")),
    )(*tables, q, k, v)


def _attn_grad_scores(q, k, v, do, lse_ref, delta_ref, diagonal):
    g = ATT_GROUP
    p = jnp.exp(_scores(q, k, diagonal) - _stack_slots(lse_ref, g)[:, :, 0:1])
    dp = _nt(do, v)
    return p, p * (dp - _stack_slots(delta_ref, g)[:, :, 0:1]) * ATT_SCALE


def _attn_bwd(q, k, v, do, lse, delta):
    t = q.shape[0]
    tile = min(ATT_T, t)
    steps = t // tile
    g = ATT_GROUP

    def body(qt_ref, kt_ref, q_ref, k_ref, v_ref, do_ref, lse_ref, delta_ref, dq_ref, dk_ref, dv_ref, dk_acc, dv_acc):
        qi, ki = qt_ref[pl.program_id(1)], kt_ref[pl.program_id(1)]

        @pl.when(pl.program_id(1) == 0)
        def _():
            dq_ref[...] = jnp.zeros_like(dq_ref)

        def step(diagonal):
            qq, kk = _stack_slots(q_ref, g), _stack_slots(k_ref, g)
            do_b = _stack_slots(do_ref, g).astype(BF16)
            p, ds = _attn_grad_scores(qq, kk, _stack_slots(v_ref, g), do_b, lse_ref, delta_ref, diagonal)
            ds = ds.astype(BF16)
            dv_acc[...] += _tn(p.astype(BF16), do_b)
            dk_acc[...] += _tn(ds, qq)
            dq = _nn(ds, kk)
            rows = pl.ds(pl.multiple_of(qi * tile, tile), tile)
            for j in range(g):
                dq_ref[rows, pl.ds(j * SLOT, SLOT)] += dq[j]

        @pl.when(qi == ki)
        def _():
            dk_acc[...] = jnp.zeros_like(dk_acc)
            dv_acc[...] = jnp.zeros_like(dv_acc)
            step(True)

        @pl.when(qi > ki)
        def _():
            step(False)

        @pl.when(qi == steps - 1)
        def _():
            _unstack_slots(dk_ref, dk_acc[...])
            _unstack_slots(dv_ref, dv_acc[...])

    q_spec, k_spec = _attn_specs(tile)
    tables = _attn_pairs(steps, False)
    return pl.pallas_call(
        body, name="attn_bwd",
        grid_spec=pltpu.PrefetchScalarGridSpec(
            num_scalar_prefetch=2, grid=(N_HEADS // g, tables[0].shape[0]),
            in_specs=[q_spec, k_spec, k_spec, q_spec, q_spec, q_spec],
            out_specs=[pl.BlockSpec((t, g * SLOT), lambda h, p, qt, kt: (0, h)), k_spec, k_spec],
            scratch_shapes=[pltpu.VMEM((g, tile, SLOT), F32), pltpu.VMEM((g, tile, SLOT), F32)]),
        out_shape=[jax.ShapeDtypeStruct((t, N_HEADS * SLOT), F32)] * 3,
        compiler_params=pltpu.CompilerParams(dimension_semantics=("parallel", "arbitrary")),
    )(*tables, q, k, v, do, lse, delta)


CONV_PAD = 8


def _fill_padded(ref, val):
    t = val.shape[0]
    zeros = jnp.zeros((CONV_PAD, val.shape[1]), val.dtype)
    ref[pl.ds(0, CONV_PAD)] = zeros
    ref[pl.ds(CONV_PAD + t, CONV_PAD)] = zeros
    ref[pl.ds(CONV_PAD, t)] = val


def _shifted(ref, s):
    return ref[pl.ds(CONV_PAD - s, ref.shape[0] - 2 * CONV_PAD)]


def _l2norm(x):
    return x * lax.rsqrt(jnp.sum(x * x, axis=-1, keepdims=True) + EPS)


def _conv_pre(x_pad, w):
    y = w[GDN_CONV - 1:GDN_CONV, :] * _shifted(x_pad, 0)
    for s in range(1, GDN_CONV):
        y = y + w[GDN_CONV - 1 - s:GDN_CONV - s, :] * _shifted(x_pad, s)
    return y


def _gdn_conv_fwd(x, w):
    t, width = x.shape

    def body(x_ref, w_ref, o_ref, x_pad):
        _fill_padded(x_pad, x_ref[...])
        act = _silu(_conv_pre(x_pad, w_ref[...]))
        normed = pl.program_id(0) < 2 * N_HEADS
        o_ref[...] = jnp.where(normed, _l2norm(act), act)

    return pl.pallas_call(
        body, name="gdn_conv_fwd",
        grid=(width // SLOT,),
        in_specs=[pl.BlockSpec((t, SLOT), lambda j: (0, j)), pl.BlockSpec((GDN_CONV, SLOT), lambda j: (0, j))],
        out_specs=pl.BlockSpec((t, SLOT), lambda j: (0, j)),
        out_shape=jax.ShapeDtypeStruct((t, width), F32),
        scratch_shapes=[pltpu.VMEM((t + 2 * CONV_PAD, SLOT), F32)],
        compiler_params=pltpu.CompilerParams(dimension_semantics=("parallel",)),
    )(x, w)


def _gdn_conv_bwd(x, w, dout):
    t, width = x.shape

    def body(x_ref, w_ref, do_ref, dx_ref, dw_ref, x_pad, dy_pad):
        wv = w_ref[...]
        _fill_padded(x_pad, x_ref[...])
        y = _conv_pre(x_pad, wv)
        sig = _sigmoid(y)
        act = y * sig
        _, pull = jax.vjp(_l2norm, act)
        normed = pl.program_id(0) < 2 * N_HEADS
        dact = jnp.where(normed, pull(do_ref[0])[0], do_ref[0])
        dy = dact * (sig * (1.0 + y * (1.0 - sig)))
        _fill_padded(dy_pad, dy)
        dx = wv[GDN_CONV - 1:GDN_CONV, :] * dy
        for s in range(1, GDN_CONV):
            dx = dx + wv[GDN_CONV - 1 - s:GDN_CONV - s, :] * _shifted(dy_pad, -s)
        dx_ref[...] = dx.astype(BF16)
        for s in range(GDN_CONV):
            dw_ref[GDN_CONV - 1 - s:GDN_CONV - s, :] = jnp.sum(dy * _shifted(x_pad, s), axis=0, keepdims=True)

    col = pl.BlockSpec((t, SLOT), lambda j: (0, j))
    tap = pl.BlockSpec((GDN_CONV, SLOT), lambda j: (0, j))
    return pl.pallas_call(
        body, name="gdn_conv_bwd",
        grid=(width // SLOT,),
        in_specs=[col, tap, pl.BlockSpec((1, t, SLOT), lambda j: (j // N_HEADS, 0, j % N_HEADS))],
        out_specs=[col, tap],
        out_shape=[jax.ShapeDtypeStruct((t, width), BF16), jax.ShapeDtypeStruct((GDN_CONV, width), F32)],
        scratch_shapes=[pltpu.VMEM((t + 2 * CONV_PAD, SLOT), F32)] * 2,
        compiler_params=pltpu.CompilerParams(dimension_semantics=("parallel",)),
    )(x, w, dout)


def _softplus(x):
    e = jnp.exp(-jnp.abs(x))
    u = 1.0 + e
    log1p = jnp.where(u == 1.0, e, jnp.log(u) * e / jnp.where(u == 1.0, 1.0, u - 1.0))
    return jnp.maximum(x, 0.0) + log1p


def _chunk_running_sum(x, reverse=False):
    tm = x.shape[0]
    at = lax.broadcasted_iota(jnp.int32, x.shape, 0) % GDN_CHUNK
    step = 1
    while step < GDN_CHUNK:
        if reverse:
            x = x + jnp.where(at < GDN_CHUNK - step, pltpu.roll(x, tm - step, 0), 0.0)
        else:
            x = x + jnp.where(at >= step, pltpu.roll(x, step, 0), 0.0)
        step *= 2
    return x


def _gates_fwd(ab, a_log, dt_bias):
    def fn(rows, consts):
        (abv,), (alog, dtb) = rows, consts
        g = _chunk_running_sum(-jnp.exp(alog) * _softplus(abv + dtb))
        beta = _sigmoid(abv)
        shape = (abv.shape[0], SLOT)
        g_slots = [jnp.broadcast_to(g[:, h:h + 1], shape) for h in range(N_HEADS)]
        b_slots = [jnp.broadcast_to(beta[:, N_HEADS + h:N_HEADS + h + 1], shape) for h in range(N_HEADS)]
        return [jnp.concatenate(g_slots, axis=1), jnp.concatenate(b_slots, axis=1)], []

    width = N_HEADS * SLOT
    return _rowwise("gdn_gates_fwd", fn, [ab], [a_log, dt_bias], [(width, F32), (width, F32)])


def _gates_bwd(ab, a_log, dt_bias, dg, dbeta):
    def fn(rows, consts):
        (abv, dgv, dbv), (alog, dtb) = rows, consts
        lane = lax.broadcasted_iota(jnp.int32, abv.shape, 1)
        dg_tok = jnp.zeros_like(abv)
        db_tok = jnp.zeros_like(abv)
        for h in range(N_HEADS):
            dg_tok = dg_tok + jnp.where(lane == h, jnp.sum(dgv[:, h * SLOT:(h + 1) * SLOT], axis=1, keepdims=True), 0.0)
            db_tok = db_tok + jnp.where(lane == N_HEADS + h, jnp.sum(dbv[:, h * SLOT:(h + 1) * SLOT], axis=1, keepdims=True), 0.0)
        dg_tok = _chunk_running_sum(dg_tok, reverse=True)
        xa = abv + dtb
        g = -jnp.exp(alog) * _softplus(xa)
        da = dg_tok * (-jnp.exp(alog)) * _sigmoid(xa)
        beta = _sigmoid(abv)
        dab = jnp.where(lane < N_HEADS, da, db_tok * beta * (1.0 - beta))
        dab = jnp.where(lane < 2 * N_HEADS, dab, 0.0)
        d_alog = jnp.sum(jnp.where(lane < N_HEADS, dg_tok * g, 0.0), axis=0, keepdims=True)
        d_dtb = jnp.sum(jnp.where(lane < N_HEADS, da, 0.0), axis=0, keepdims=True)
        return [dab], [d_alog, d_dtb]

    return _rowwise("gdn_gates_bwd", fn, [ab, dg, dbeta], [a_log, dt_bias], [(SLOT, F32)], sums=[SLOT, SLOT])


ROPE_HALF = MLA_ROPE // 2


def _rope_tables(positions):
    freqs = ROPE_THETA ** (-jnp.arange(ROPE_HALF, dtype=F32) / ROPE_HALF)
    ang = positions.astype(F32)[:, None] * freqs
    cos, sin = jnp.cos(ang), jnp.sin(ang)
    t = positions.shape[0]
    ones, zeros = jnp.ones((t, MLA_NOPE), F32), jnp.zeros((t, MLA_NOPE), F32)
    tail = jnp.zeros((t, SLOT - MLA_NOPE - MLA_ROPE), F32)
    half0 = jnp.zeros((t, ROPE_HALF), F32)
    same = jnp.concatenate([ones, cos, cos, tail], axis=1)
    from_low = jnp.concatenate([zeros, half0, sin, tail], axis=1)
    from_high = jnp.concatenate([zeros, -sin, half0, tail], axis=1)
    return same, from_low, from_high


def _rope(x, tabs):
    same, from_low, from_high = tabs
    width = x.shape[1]
    return x * same + pltpu.roll(x, ROPE_HALF, 1) * from_low + pltpu.roll(x, width - ROPE_HALF, 1) * from_high


def _rope_transposed(dy, tabs):
    same, from_low, from_high = tabs
    width = dy.shape[1]
    return dy * same + pltpu.roll(dy * from_low, width - ROPE_HALF, 1) + pltpu.roll(dy * from_high, ROPE_HALF, 1)


def _tile_slots(tab):
    return jnp.concatenate([tab] * N_HEADS, axis=1)


A_WIDTH = MLA_Q_RANK + MLA_KV_RANK + 2 * SLOT
A_KPE = MLA_Q_RANK + MLA_KV_RANK
A_AB = A_KPE + SLOT
WIDE = N_HEADS * SLOT


def _mla_front_fwd(proj_a, tabs, g_q, g_kv, w_uq, w_kv):
    def fn(rows, consts):
        pa, *tb = rows
        gq, gkv, wuq, wkv = consts
        cqn = _rms(pa[:, :MLA_Q_RANK], gq, MLA_Q_RANK).astype(BF16)
        ckvn = _rms(pa[:, MLA_Q_RANK:A_KPE], gkv, MLA_KV_RANK).astype(BF16)
        kv = _nt(ckvn, wkv)
        q = _rope(_nt(cqn, wuq), [_tile_slots(x) for x in tb])
        k = kv[:, :WIDE] + _tile_slots(_rope(pa[:, A_KPE:A_AB], tb))
        return [cqn, ckvn, q, k, kv[:, WIDE:]], []

    return _rowwise("mla_front_fwd", fn, [proj_a, *tabs], [g_q, g_kv, w_uq, w_kv],
                    [(MLA_Q_RANK, BF16), (MLA_KV_RANK, BF16)] + [(WIDE, BF16)] * 3)


def _mla_front_bwd(proj_a, tabs, g_q, g_kv, w_uq, w_kv, dq, dk, dv, dab):
    def fn(rows, consts):
        pa, t0, t1, t2, dqv, dkv, dvv, da = rows
        gq, gkv, wuq, wkv = consts
        tb = (t0, t1, t2)
        dq_p = _rope_transposed(dqv, [_tile_slots(x) for x in tb]).astype(BF16)
        dkv_p = jnp.concatenate([dkv, dvv], axis=1).astype(BF16)
        dkpe = dkv[:, :SLOT]
        for h in range(1, N_HEADS):
            dkpe = dkpe + dkv[:, h * SLOT:(h + 1) * SLOT]
        _, pull_q = jax.vjp(lambda x, g: _rms(x, g, MLA_Q_RANK), pa[:, :MLA_Q_RANK], gq)
        _, pull_kv = jax.vjp(lambda x, g: _rms(x, g, MLA_KV_RANK), pa[:, MLA_Q_RANK:A_KPE], gkv)
        dcq, dgq = pull_q(_nn(dq_p, wuq))
        dckv, dgkv = pull_kv(_nn(dkv_p, wkv))
        return [jnp.concatenate([dcq, dckv, _rope_transposed(dkpe, tb), da], axis=1), dq_p, dkv_p], [dgq, dgkv]

    return _rowwise("mla_front_bwd", fn, [proj_a, *tabs, dq, dk, dv, dab], [g_q, g_kv, w_uq, w_kv],
                    [(A_WIDTH, BF16), (WIDE, BF16), (2 * WIDE, BF16)], sums=[MLA_Q_RANK, MLA_KV_RANK])


def _slot_sum(x):
    parts = [jnp.broadcast_to(jnp.sum(x[:, h * SLOT:(h + 1) * SLOT], axis=1, keepdims=True), (x.shape[0], SLOT))
             for h in range(N_HEADS)]
    return jnp.concatenate(parts, axis=1)


def _mix_join(o_mla, o_gdn, gate, g_mla, g_gdn):
    mla = _rms(o_mla, g_mla, N_HEADS * MLA_V)
    gdn = o_gdn * lax.rsqrt(_slot_sum(o_gdn * o_gdn) * (1.0 / GDN_D) + EPS) * g_gdn * _silu(gate)
    return mla, gdn


MIX_TM = 256


def _mix_fwd(o_mla, o_gdn, gate, x, g_mla, g_gdn, w_out, g_post):
    dm = x.shape[1]

    def fn(rows, consts):
        om, og, gt, xv = rows
        gm, gg, wo, gp = consts
        cat = jnp.concatenate(_mix_join(om, og, gt, gm, gg), axis=1).astype(BF16)
        mixed = _nn(cat, wo)
        return [cat, mixed, xv + _rms(mixed, gp, dm)], []

    return _rowwise("mix_fwd", fn, [o_mla, o_gdn, gate, x], [g_mla, g_gdn, w_out, g_post],
                    [(2 * WIDE, BF16), (dm, F32), (dm, F32)], tm=MIX_TM)


def _mix_bwd(o_mla, o_gdn, gate, mixed, dy, g_mla, g_gdn, w_out, g_post):
    dm = mixed.shape[1]

    def fn(rows, consts):
        om, og, gt, mx, dyv = rows
        gm, gg, wo, gp = consts
        _, pull_post = jax.vjp(lambda hv, gv: _rms(hv, gv, dm), mx, gp)
        dmixed, dgp = pull_post(dyv)
        dmixed = dmixed.astype(BF16)
        dc = _nt(dmixed, wo)
        _, pull = jax.vjp(lambda x, g: _rms(x, g, N_HEADS * MLA_V), om, gm)
        dom, dgm = pull(dc[:, :WIDE])
        dn_out = dc[:, WIDE:]
        r = lax.rsqrt(_slot_sum(og * og) * (1.0 / GDN_D) + EPS)
        sig = _sigmoid(gt)
        normed = og * r
        dn = dn_out * gg * (gt * sig)
        dog = r * dn - normed * (r * r) * _slot_sum(dn * og) * (1.0 / GDN_D)
        dgt = dn_out * normed * gg * (sig * (1.0 + gt * (1.0 - sig)))
        dgg = jnp.sum(dn_out * normed * (gt * sig), axis=0, keepdims=True)
        return [dmixed, dom, _slot_sum(dom * om), dog, dgt], [dgp, dgm, dgg]

    return _rowwise("mix_bwd", fn, [o_mla, o_gdn, gate, mixed, dy], [g_mla, g_gdn, w_out, g_post],
                    [(dm, BF16), (WIDE, F32), (WIDE, F32), (WIDE, F32), (WIDE, BF16)], sums=[dm, WIDE, WIDE], tm=MIX_TM)


def _norm_bwd_add(name, x, g, dns, dy):
    dm = x.shape[1]

    def fn(rows, consts):
        xv, dyv, *parts = rows
        dn = parts[0]
        for p in parts[1:]:
            dn = dn + p
        _, pull = jax.vjp(lambda a, gv: _rms(a, gv, dm), xv, consts[0])
        dx, dg = pull(dn)
        return [dyv + dx], [dg]

    return _rowwise(name, fn, [x, dy, *dns], [g], [(dm, F32)], sums=[dm])


def _loss_fwd(y, target):
    dm = y.shape[1]

    def fn(rows, consts):
        err = rows[0] - rows[1]
        sq = err * err
        lanes = sq[:, :SLOT]
        for j in range(1, dm // SLOT):
            lanes = lanes + sq[:, j * SLOT:(j + 1) * SLOT]
        return [err * (1.0 / dm)], [jnp.sum(lanes, axis=0, keepdims=True) * (0.5 / dm)]

    return _rowwise("loss", fn, [y, target], [], [(dm, F32)], sums=[SLOT])


def _norm_fwd(name, x, g):
    dm = x.shape[1]
    return _rowwise(name, lambda rows, consts: ([_rms(rows[0], consts[0], dm)], []), [x], [g], [(dm, BF16)])[0]


W_IN_CUTS = (0, 256, 384, 416, 1952, 1960, 1968, 2480)


def _heads_out(w, per_head, axis=-1):
    axis = axis % w.ndim
    shape = w.shape
    n = shape[axis] // per_head
    w = w.reshape(shape[:axis] + (n, per_head) + shape[axis + 1:])
    pad = [(0, 0)] * w.ndim
    pad[axis + 1] = (0, SLOT - per_head)
    return jnp.pad(w, pad).reshape(shape[:axis] + (n * SLOT,) + shape[axis + 1:])


def _heads_in(w, per_head, axis=-1):
    axis = axis % w.ndim
    shape = w.shape
    n = shape[axis] // SLOT
    w = w.reshape(shape[:axis] + (n, SLOT) + shape[axis + 1:])
    w = lax.slice_in_dim(w, 0, per_head, axis=axis + 1)
    return w.reshape(shape[:axis] + (n * per_head,) + shape[axis + 1:])


def _pad_lanes(v, lo, width=SLOT):
    return jnp.pad(v, [(0, 0)] * (v.ndim - 1) + [(lo, width - lo - v.shape[-1])])


def _pad_rows(v, lo, rows=SLOT):
    return jnp.pad(v, [(lo, rows - lo - v.shape[0])] + [(0, 0)] * (v.ndim - 1))


def _layout_weights(w):
    c = W_IN_CUTS
    w_in = w["w_in_t"]
    p = {}
    p["w_a"] = jnp.concatenate([w_in[c[0]:c[2]], _pad_rows(w_in[c[2]:c[3]], MLA_NOPE), _pad_rows(w_in[c[4]:c[6]], 0)], axis=0)
    p["w_qkv"] = _heads_out(w_in[c[3]:c[4]], GDN_D, axis=0)
    p["w_gate"] = _heads_out(w_in[c[6]:c[7]], GDN_D, axis=0)
    p["w_uq"] = _heads_out(w["uq_t"], MLA_NOPE + MLA_ROPE, axis=0)
    ukv = w["ukv_t"].reshape(N_HEADS, MLA_NOPE + MLA_V, MLA_KV_RANK)
    p["w_kv"] = jnp.concatenate([_heads_out(ukv[:, :MLA_NOPE].reshape(-1, MLA_KV_RANK), MLA_NOPE, axis=0),
                                 _heads_out(ukv[:, MLA_NOPE:].reshape(-1, MLA_KV_RANK), MLA_V, axis=0)], axis=0)
    p["conv"] = _heads_out(w["gdn_conv_w"], GDN_D)
    p["g_mla_out"] = _heads_out(w["mla_out_g"], MLA_V)
    p["g_gdn"] = jnp.tile(_pad_lanes(w["gdn_norm_g"], 0), (1, N_HEADS))
    p["a_log"] = _pad_lanes(w["gdn_a_log"], 0)
    p["dt_bias"] = _pad_lanes(w["gdn_dt_bias"], 0)
    return p


def _unlayout_grads(d):
    c = W_IN_CUTS
    g = {}
    da = d["w_a"]
    kpe0 = A_KPE + MLA_NOPE
    g["w_in_t"] = jnp.concatenate([da[:A_KPE], da[kpe0:kpe0 + MLA_ROPE], _heads_in(d["w_qkv"], GDN_D, axis=0),
                                   da[A_AB:A_AB + 2 * N_HEADS], _heads_in(d["w_gate"], GDN_D, axis=0)], axis=0)
    assert g["w_in_t"].shape[0] == c[-1]
    g["uq_t"] = _heads_in(d["w_uq"], MLA_NOPE + MLA_ROPE, axis=0)
    dk = _heads_in(d["w_kv"][:WIDE], MLA_NOPE, axis=0).reshape(N_HEADS, MLA_NOPE, MLA_KV_RANK)
    dv = _heads_in(d["w_kv"][WIDE:], MLA_V, axis=0).reshape(N_HEADS, MLA_V, MLA_KV_RANK)
    g["ukv_t"] = jnp.concatenate([dk, dv], axis=1).reshape(-1, MLA_KV_RANK)
    g["w_out"] = _heads_in(d["w_out"], GDN_D, axis=0)
    g["gdn_conv_w"] = _heads_in(d["conv"], GDN_D)
    g["mla_out_g"] = _heads_in(d["g_mla_out"], MLA_V)
    g["gdn_norm_g"] = jnp.sum(d["g_gdn"].reshape(N_HEADS, SLOT), axis=0, keepdims=True)[:, :GDN_D]
    g["gdn_a_log"] = d["a_log"][:, :N_HEADS]
    g["gdn_dt_bias"] = d["dt_bias"][:, :N_HEADS]
    return g


def _weight_grad(name, cots, acts, out_dtype=F32, tm=1024, tn=1024, tk=2048, carry=None):
    return _matmul(name, cots, acts, "tn", out_dtype=out_dtype, tm=tm, tn=tn, tk=tk, carry=carry)


def _by_device(a):
    return a.astype(BF16).reshape((N_DEV, a.shape[0] // N_DEV) + a.shape[1:])


def _rows_of(blocks):
    return blocks.reshape((-1,) + blocks.shape[2:])


def _local_step(x, positions, target, w, mid, late):
    tabs = _rope_tables(positions)

    (h1, x1, hg1, hu1), gathered = _ffn_fwd("ffn1_fwd", x, w["ffn1_pre_g"], w["ffn1"], 0, w["ffn1_post_g"], carry=mid)
    w = dict(w, w_in_t=_rows_of(gathered[0]), uq_t=_rows_of(gathered[1]), ukv_t=_rows_of(gathered[2]))
    p = _layout_weights(w)
    hn = _norm_fwd("mix_pre_norm", x1, w["mix_pre_g"])
    proj_a = _matmul("proj_a", hn, p["w_a"], "nt")
    proj_qkv = _matmul("proj_qkv", hn, p["w_qkv"], "nt")
    proj_gate = _matmul("proj_gate", hn, p["w_gate"], "nt")
    cqn, ckvn, q, k, v = _mla_front_fwd(proj_a, tabs, w["mla_q_norm_g"], w["mla_kv_norm_g"], p["w_uq"], p["w_kv"])
    o_mla, lse = _attn_fwd(q, k, v)
    ab = (proj_a, SLOT, A_AB // SLOT)
    qkv_n = _gdn_conv_fwd(proj_qkv, p["conv"])
    gb, bb = _gates_fwd(ab, p["a_log"], p["dt_bias"])
    (o_gdn, keep), (ffn2, w_out) = _gdn_fwd(qkv_n, gb, bb, carry=late)
    p["w_out"] = _heads_out(_rows_of(w_out), GDN_D, axis=0)
    cat, mixed, x2 = _mix_fwd(o_mla, o_gdn, proj_gate, x1, p["g_mla_out"], p["g_gdn"], p["w_out"], w["mix_post_g"])
    (h2, y, hg2, hu2), _ = _ffn_fwd("ffn2_fwd", x2, w["ffn2_pre_g"], ffn2, 0, w["ffn2_post_g"])
    dy, loss_lanes = _loss_fwd(y, target)

    g = {}
    (dx2, xn2, dh2, a2, dhg2, dhu2, g["ffn2_pre_g"], g["ffn2_post_g"]), _ = _ffn_bwd(
        "ffn2_bwd", x2, h2, hg2, hu2, dy, w["ffn2_pre_g"], ffn2, 0, w["ffn2_post_g"])
    ffn2_grads = _Scatter([_by_device(_weight_grad("ffn2_dw_gate", dhg2, xn2, BF16, tm=1408)),
                           _by_device(_weight_grad("ffn2_dw_up", dhu2, xn2, BF16, tm=1408)),
                           _by_device(_weight_grad("ffn2_dw_down", a2, dh2, BF16, tm=1408))])
    d = {}
    dmixed, do_mla, delta, do_gdn, dgate, g["mix_post_g"], d["g_mla_out"], d["g_gdn"] = _mix_bwd(
        o_mla, o_gdn, proj_gate, mixed, dx2, p["g_mla_out"], p["g_gdn"], p["w_out"], w["mix_post_g"])
    d["w_out"] = _weight_grad("mix_out_dw", cat, dmixed)
    dq, dk, dv = _attn_bwd(q, k, v, do_mla, lse, delta)
    (dqkv_n, dgb, dbb), landed_ffn2 = _gdn_bwd(qkv_n, gb, bb, keep, do_gdn, carry=ffn2_grads)
    dab, d["a_log"], d["dt_bias"] = _gates_bwd(ab, p["a_log"], p["dt_bias"], dgb, dbb)
    dproj_qkv, d["conv"] = _gdn_conv_bwd(proj_qkv, p["conv"], dqkv_n)
    dproj_a, dq_p, dkv_p, g["mla_q_norm_g"], g["mla_kv_norm_g"] = _mla_front_bwd(
        proj_a, tabs, w["mla_q_norm_g"], w["mla_kv_norm_g"], p["w_uq"], p["w_kv"], dq, dk, dv, dab)
    d["w_uq"] = _weight_grad("mla_q_dw", dq_p, cqn)
    d["w_kv"] = _weight_grad("mla_kv_dw", dkv_p, ckvn)
    dhn = [_matmul("proj_a_dx", dproj_a, p["w_a"], "nn"), _matmul("proj_qkv_dx", dproj_qkv, p["w_qkv"], "nn"),
           _matmul("proj_gate_dx", dgate, p["w_gate"], "nn")]
    d["w_a"] = _weight_grad("proj_a_dw", dproj_a, hn, tm=640)
    d["w_qkv"] = _weight_grad("proj_qkv_dw", dproj_qkv, hn)
    d["w_gate"] = _weight_grad("proj_gate_dw", dgate, hn)
    dx1, g["mix_pre_g"] = _norm_bwd_add("mix_pre_bwd", x1, w["mix_pre_g"], dhn, dx2)
    g.update(_unlayout_grads(d))
    others = [t for t, _ in OTHER.values()]
    (dx, xn1, dh1, a1, dhg1, dhu1, g["ffn1_pre_g"], g["ffn1_post_g"]), landed_others = _ffn_bwd(
        "ffn1_bwd", x, h1, hg1, hu1, dx1, w["ffn1_pre_g"], w["ffn1"], 0, w["ffn1_post_g"], carry=_Scatter([_by_device(g.pop(t)) for t in others]))
    dw_down = _weight_grad("ffn1_dw_down", a1, dh1, BF16, tm=1408)
    dw_gate, (landed_down,) = _weight_grad("ffn1_dw_gate", dhg1, xn1, BF16, tm=1408, carry=_Scatter([_by_device(dw_down)]))
    dw_up, (landed_gate,) = _weight_grad("ffn1_dw_up", dhu1, xn1, BF16, tm=1408, carry=_Scatter([_by_device(dw_gate)]))
    (landed_up,) = _exchange("scatter_last", _Scatter([_by_device(dw_up)]))
    landed = dict(zip(list(FFN_NAMES) + list(OTHER),
                      [landed_gate, landed_up, landed_down] + list(landed_ffn2) + list(landed_others)))
    return loss_lanes, dx, g, landed


MESH_AXES = ("x", "y", "c")
N_LINKS = N_DEV - 1


def _place():
    return tuple(lax.axis_index(a) for a in MESH_AXES)


def _block_of(dev):
    x, y, c = dev
    return 4 * x + 2 * y + c


def _remote_copy(src, dst, sems, k, to):
    send_sems, recv_sems = sems
    return pltpu.make_async_remote_copy(src_ref=src, dst_ref=dst, send_sem=send_sems.at[k], recv_sem=recv_sems.at[k],
                                        device_id=to, device_id_type=pl.DeviceIdType.MESH)


class _Exchange:
    def __init__(self, arrays):
        self.arrays = list(arrays)
        self.n = len(self.arrays)
        self.specs = [pl.BlockSpec(memory_space=pl.ANY)] * self.n
        self.scratch = [pltpu.SemaphoreType.DMA((self.n * N_LINKS,)), pltpu.SemaphoreType.DMA((self.n * N_LINKS,)),
                        pltpu.SemaphoreType.DMA((self.n,))]

    def split(self, refs):
        n = self.n
        return refs[:n], refs[n:2 * n], (refs[2 * n], refs[2 * n + 1]), refs[2 * n + 2]


class _Gather(_Exchange):
    def out_shape(self):
        return [jax.ShapeDtypeStruct((N_DEV,) + a.shape, a.dtype) for a in self.arrays]

    def _plan(self, ins, outs, sems, local_sems):
        x, y, c = _place()
        me, sibling = (x, y, c), (x, y, 1 - c)
        chips = [(1 - x, y), (x, 1 - y), (1 - x, 1 - y)]

        def copy(a, k, block, to, mine=False):
            src = ins[a] if mine else outs[a].at[_block_of(block)]
            return _remote_copy(src, outs[a].at[_block_of(block)], sems, a * N_LINKS + k, to)

        local = [pltpu.make_async_copy(ins[a], outs[a].at[_block_of(me)], local_sems.at[a]) for a in range(self.n)]
        first = []
        for a in range(self.n):
            first.append(copy(a, 0, me, sibling, mine=True))
            first += [copy(a, 1 + j, me, (*chip, c), mine=True) for j, chip in enumerate(chips)]
        return me, sibling, chips, c, copy, local, first

    def start(self, ins, outs, sems, local_sems):
        *_, local, first = self._plan(ins, outs, sems, local_sems)
        for cp in local + first:
            cp.start()

    def finish(self, ins, outs, sems, local_sems):
        me, sibling, chips, c, copy, local, first = self._plan(ins, outs, sems, local_sems)
        passed = []
        for j, chip in enumerate(chips):
            for a in range(self.n):
                copy(a, 1 + j, (*chip, c), me).wait_recv()
                passed.append(copy(a, 4 + j, (*chip, c), sibling))
                passed[-1].start()
        for a in range(self.n):
            copy(a, 0, sibling, me).wait_recv()
            for j, chip in enumerate(chips):
                copy(a, 4 + j, (*chip, 1 - c), me).wait_recv()
        for cp in first + passed:
            cp.wait_send()
        for cp in local:
            cp.wait()


class _Scatter(_Exchange):
    def out_shape(self):
        return [jax.ShapeDtypeStruct(a.shape, a.dtype) for a in self.arrays]

    def _plan(self, ins, outs, sems, local_sems):
        x, y, c = _place()
        me = _block_of((x, y, c))

        def peer(r):
            return (1 - x if r & 4 else x, 1 - y if r & 2 else y, 1 - c if r & 1 else c)

        local = [pltpu.make_async_copy(ins[a].at[me], outs[a].at[me], local_sems.at[a]) for a in range(self.n)]
        sends = [_remote_copy(ins[a].at[_block_of(peer(r))], outs[a].at[me], sems, a * N_LINKS + r - 1, peer(r))
                 for a in range(self.n) for r in range(1, N_DEV)]
        arrivals = [_remote_copy(ins[a].at[me], outs[a].at[_block_of(peer(r))], sems, a * N_LINKS + r - 1, peer(r))
                    for a in range(self.n) for r in range(1, N_DEV)]
        return local, sends, arrivals

    def start(self, ins, outs, sems, local_sems):
        local, sends, _ = self._plan(ins, outs, sems, local_sems)
        for cp in local + sends:
            cp.start()

    def finish(self, ins, outs, sems, local_sems):
        local, sends, arrivals = self._plan(ins, outs, sems, local_sems)
        for cp in arrivals:
            cp.wait_recv()
        for cp in sends:
            cp.wait_send()
        for cp in local:
            cp.wait()


def _exchange(name, plan):
    def body(*refs):
        parts = plan.split(refs)
        plan.start(*parts)
        plan.finish(*parts)

    return pl.pallas_call(
        body, name=name,
        in_specs=plan.specs,
        out_specs=plan.specs,
        out_shape=plan.out_shape(),
        scratch_shapes=plan.scratch,
    )(*plan.arrays)


def _call_carrying(body, plan, operands, *, name, grid, in_specs, out_specs, out_shape, scratch_shapes, compiler_params):
    if plan is None:
        outs = pl.pallas_call(body, name=name, grid=grid, in_specs=in_specs, out_specs=out_specs, out_shape=out_shape,
                              scratch_shapes=scratch_shapes, compiler_params=compiler_params)(*operands)
        return outs, []
    n_i, n_o, n_s, k = len(in_specs), len(out_specs), len(scratch_shapes), plan.n

    def whole(*refs):
        cut = [n_i, n_i + k, n_i + k + n_o, n_i + 2 * k + n_o, n_i + 2 * k + n_o + n_s]
        own_in, ex_in, own_out, ex_out, own_scr, ex_scr = (refs[a:b] for a, b in zip([0] + cut, cut + [len(refs)]))
        parts = plan.split(ex_in + ex_out + ex_scr)
        first = last = True
        for axis, size in enumerate(grid):
            first = first & (pl.program_id(axis) == 0)
            last = last & (pl.program_id(axis) == size - 1)

        @pl.when(first)
        def _():
            plan.start(*parts)

        body(*own_in, *own_out, *own_scr)

        @pl.when(last)
        def _():
            plan.finish(*parts)

    outs = pl.pallas_call(
        whole, name=name, grid=grid,
        in_specs=list(in_specs) + plan.specs, out_specs=list(out_specs) + plan.specs,
        out_shape=list(out_shape) + plan.out_shape(), scratch_shapes=list(scratch_shapes) + plan.scratch,
        compiler_params=compiler_params,
    )(*operands, *plan.arrays)
    return outs[:n_o], outs[n_o:]


def _row_tile(rows, target=256):
    best = rows
    for cand in range(16, min(rows, target) + 1, 16):
        if rows % cand == 0:
            best = cand
    return best


def _sum_blocks(name, blocks):
    rows, width = blocks.shape[-2:]
    tm = _row_tile(rows)

    def body(x_ref, o_ref):
        acc = x_ref[0].astype(F32)
        for d in range(1, N_DEV):
            acc = acc + x_ref[d].astype(F32)
        o_ref[...] = acc

    return pl.pallas_call(
        body, name=name,
        grid=(rows // tm,),
        in_specs=[pl.BlockSpec((N_DEV, tm, width), lambda i: (0, i, 0))],
        out_specs=pl.BlockSpec((tm, width), lambda i: (i, 0)),
        out_shape=jax.ShapeDtypeStruct((rows, width), F32),
        compiler_params=pltpu.CompilerParams(dimension_semantics=("parallel",)),
    )(blocks)


def _all_reduce_small(name, vec):
    rows, width = vec.shape

    def body(x_ref, o_ref, all_ref, send_sems, recv_sems):
        x, y, c = _place()
        me = _block_of((x, y, c))
        all_ref[me] = x_ref[...]

        def peer(r):
            return (1 - x if r & 4 else x, 1 - y if r & 2 else y, 1 - c if r & 1 else c)

        def copy(r, block):
            return _remote_copy(x_ref, all_ref.at[block], (send_sems, recv_sems), r - 1, peer(r))

        sends = [copy(r, me) for r in range(1, N_DEV)]
        for cp in sends:
            cp.start()
        for r in range(1, N_DEV):
            copy(r, _block_of(peer(r))).wait_recv()
        for cp in sends:
            cp.wait_send()
        acc = all_ref[0]
        for d in range(1, N_DEV):
            acc = acc + all_ref[d]
        o_ref[...] = acc

    return pl.pallas_call(
        body, name=name,
        in_specs=[pl.BlockSpec(memory_space=pltpu.VMEM)],
        out_specs=pl.BlockSpec(memory_space=pltpu.VMEM),
        out_shape=jax.ShapeDtypeStruct((rows, width), F32),
        scratch_shapes=[pltpu.VMEM((N_DEV, rows, width), F32), pltpu.SemaphoreType.DMA((N_LINKS,)), pltpu.SemaphoreType.DMA((N_LINKS,))],
    )(vec)


def _adamw(name, w, g, m, v):
    def fn(rows, consts):
        wv, gv, mv, vv = rows
        m2 = ADAM_B1 * mv + (1.0 - ADAM_B1) * gv
        v2 = ADAM_B2 * vv + (1.0 - ADAM_B2) * jnp.square(gv)
        m_hat = m2 / (1.0 - ADAM_B1 ** ADAM_STEP)
        v_hat = v2 / (1.0 - ADAM_B2 ** ADAM_STEP)
        return [-ADAM_LR * (m_hat / (jnp.sqrt(v_hat) + ADAM_EPS) + ADAM_WD * wv), m2, v2], []

    return _rowwise(name, fn, [w, g, m, v], [], [(w.shape[1], F32)] * 3, tm=_row_tile(w.shape[0]))


ROW = 1024
FFN_NAMES = ("ffn1_w_gate", "ffn1_w_up", "ffn1_w_down", "ffn2_w_gate", "ffn2_w_up", "ffn2_w_down")
OTHER = {"w_in": ("w_in_t", True), "mla_w_uq": ("uq_t", True), "mla_w_ukv": ("ukv_t", True), "w_out": ("w_out", False)}
BY_COLUMNS = ("ffn1_w_gate", "ffn1_w_up", "ffn2_w_gate", "ffn2_w_up", "w_in", "mla_w_uq", "mla_w_ukv")
SMALL = {
    "ffn1_pre_g": (1024, 1024), "ffn1_post_g": (1024, 1024), "mix_pre_g": (1024, 1024), "mla_q_norm_g": (256, 256),
    "mla_kv_norm_g": (128, 128), "mla_out_g": (512, 512), "gdn_a_log": (8, 128), "gdn_dt_bias": (8, 128),
    "gdn_norm_g": (64, 128), "mix_post_g": (1024, 1024), "ffn2_pre_g": (1024, 1024), "ffn2_post_g": (1024, 1024),
}
CONV_SHAPE = (GDN_CONV, 3 * N_HEADS * GDN_D)
CONV_SHARD = (GDN_CONV, CONV_SHAPE[1] // N_DEV)
CONV_LANES = CONV_SHAPE[0] * CONV_SHAPE[1]
SMALL_ROWS = 8
REDUCE_ROWS = 16


def _pack_small(vecs, conv, rows):
    parts = [_pad_lanes(vecs[n].reshape(1, -1), 0, r) for n, (_, r) in SMALL.items()]
    parts.append(conv.reshape(1, -1))
    flat = jnp.concatenate(parts, axis=1)
    return _pad_lanes(flat, 0, rows * ROW).reshape(rows, ROW)


def _unpack_small(buf):
    flat = buf.reshape(1, -1)
    out, at = {}, 0
    for n, (w, r) in SMALL.items():
        out[n] = flat[:, at:at + w]
        at += r
    return out, flat[0, at:]


def kernel(x, positions, ffn1_pre_g, ffn1_w_gate, ffn1_w_up, ffn1_w_down, ffn1_post_g, mix_pre_g, w_in, mla_q_norm_g, mla_w_uq, mla_kv_norm_g, mla_w_ukv, mla_out_g, gdn_conv_w, gdn_a_log, gdn_dt_bias, gdn_norm_g, w_out, mix_post_g, ffn2_pre_g, ffn2_w_gate, ffn2_w_up, ffn2_w_down, ffn2_post_g, loss_target, m_ffn1_pre_g, m_ffn1_w_gate, m_ffn1_w_up, m_ffn1_w_down, m_ffn1_post_g, m_mix_pre_g, m_w_in, m_mla_q_norm_g, m_mla_w_uq, m_mla_kv_norm_g, m_mla_w_ukv, m_mla_out_g, m_gdn_conv_w, m_gdn_a_log, m_gdn_dt_bias, m_gdn_norm_g, m_w_out, m_mix_post_g, m_ffn2_pre_g, m_ffn2_w_gate, m_ffn2_w_up, m_ffn2_w_down, m_ffn2_post_g, v_ffn1_pre_g, v_ffn1_w_gate, v_ffn1_w_up, v_ffn1_w_down, v_ffn1_post_g, v_mix_pre_g, v_w_in, v_mla_q_norm_g, v_mla_w_uq, v_mla_kv_norm_g, v_mla_w_ukv, v_mla_out_g, v_gdn_conv_w, v_gdn_a_log, v_gdn_dt_bias, v_gdn_norm_g, v_w_out, v_mix_post_g, v_ffn2_pre_g, v_ffn2_w_gate, v_ffn2_w_up, v_ffn2_w_down, v_ffn2_post_g):
    given = dict(locals())
    order = ["ffn1_pre_g", "ffn1_w_gate", "ffn1_w_up", "ffn1_w_down", "ffn1_post_g", "mix_pre_g", "w_in", "mla_q_norm_g",
             "mla_w_uq", "mla_kv_norm_g", "mla_w_ukv", "mla_out_g", "gdn_conv_w", "gdn_a_log", "gdn_dt_bias", "gdn_norm_g",
             "w_out", "mix_post_g", "ffn2_pre_g", "ffn2_w_gate", "ffn2_w_up", "ffn2_w_down", "ffn2_post_g"]
    assert sorted(order) == sorted(list(FFN_NAMES) + list(OTHER) + list(SMALL) + ["gdn_conv_w"])

    def drop_depth(a):
        return a[0] if a.ndim == 3 else a

    wts = {n: drop_depth(given[n]) for n in order}
    mom = {n: drop_depth(given["m_" + n]) for n in order}
    var = {n: drop_depth(given["v_" + n]) for n in order}
    me = _block_of(_place())

    def wire(n):
        return (wts[n].T if n in BY_COLUMNS else wts[n]).astype(BF16)

    (ffn1,) = _exchange("gather_first", _Gather([jnp.stack([wire(n) for n in FFN_NAMES[:3]])]))
    mid = _Gather([wire(n) for n in ("w_in", "mla_w_uq", "mla_w_ukv")])
    late = _Gather([jnp.stack([wire(n) for n in FFN_NAMES[3:]]), wire("w_out")])
    conv_at = lax.dynamic_update_slice(jnp.zeros((N_DEV, CONV_SHARD[0] * CONV_SHARD[1]), F32),
                                       wts["gdn_conv_w"].reshape(1, -1), (me, 0))
    conv_all = _all_reduce_small("gather_conv", _pad_lanes(conv_at.reshape(1, -1), 0, SMALL_ROWS * ROW).reshape(SMALL_ROWS, ROW))
    full = {n: wts[n] for n in SMALL}
    full["ffn1"] = ffn1
    full["gdn_conv_w"] = conv_all.reshape(-1)[:CONV_LANES].reshape((N_DEV,) + CONV_SHARD).transpose(1, 0, 2).reshape(CONV_SHAPE)

    loss_lanes, dx, grads, landed = _local_step(x[0], positions[0], loss_target[0], full, mid, late)
    loss = lax.psum(jnp.sum(loss_lanes), MESH_AXES)

    sums = {n: _sum_blocks("sum_" + n, blocks) for n, blocks in landed.items()}
    grad = {n: (sums[n].T if n in BY_COLUMNS else sums[n]) for n in sums}
    small_sum = _all_reduce_small("reduce_small", _pack_small(grads, grads["gdn_conv_w"].reshape(-1), REDUCE_ROWS))
    small_grad, conv_grad_full = _unpack_small(small_sum)
    grad.update(small_grad)
    grad["gdn_conv_w"] = lax.dynamic_slice(conv_grad_full[:CONV_LANES].reshape(CONV_SHAPE), (0, me * CONV_SHARD[1]), CONV_SHARD)

    outs = {"grad": grad, "delta": {}, "new_m": {}, "new_v": {}}
    for n in list(FFN_NAMES) + list(OTHER):
        outs["delta"][n], outs["new_m"][n], outs["new_v"][n] = _adamw("adamw_" + n, wts[n], grad[n], mom[n], var[n])
    small = [_pack_small(s, s["gdn_conv_w"].reshape(-1), SMALL_ROWS) for s in (wts, grad, mom, var)]
    for kind, s in zip(("delta", "new_m", "new_v"), _adamw("adamw_small", *small)):
        vecs, conv = _unpack_small(s)
        outs[kind].update(vecs)
        outs[kind]["gdn_conv_w"] = conv[:CONV_SHARD[0] * CONV_SHARD[1]].reshape(CONV_SHARD)
    result = [loss, dx[None]]
    for kind in ("grad", "delta", "new_m", "new_v"):
        result += [outs[kind][n].reshape(given[n].shape) for n in order]
    return tuple(result)
```

```python
import jax
import jax.numpy as jnp
from jax import lax
from jax.experimental import pallas as pl
from jax.experimental.pallas import tpu as pltpu

F32 = jnp.float32
BF16 = jnp.bfloat16
HI = lax.Precision.HIGH

N_DEV = 8
D_MODEL = 1024
D_FF = 2816
N_HEADS = 8
SLOT = 128
MLA_Q_RANK = 256
MLA_KV_RANK = 128
MLA_NOPE = 64
MLA_ROPE = 32
MLA_V = 64
GDN_D = 64
GDN_CONV = 4
GDN_CHUNK = 64
ROPE_THETA = 10000.0
EPS = 1e-6
ADAM_LR, ADAM_B1, ADAM_B2, ADAM_EPS, ADAM_WD, ADAM_STEP = 0.001, 0.9, 0.999, 1e-08, 0.01, 10


def _dot(a, b, ca, cb, precision=None):
    lead = a.ndim - 2
    batch = tuple(range(lead))
    return lax.dot_general(a, b, (((lead + ca,), (lead + cb,)), (batch, batch)), precision=precision,
                           preferred_element_type=F32)


def _nn(a, b, precision=None):
    return _dot(a, b, 1, 0, precision)


def _nt(a, b, precision=None):
    return _dot(a, b, 1, 1, precision)


def _tn(a, b, precision=None):
    return _dot(a, b, 0, 0, precision)


def _sigmoid(x):
    return 1.0 / (1.0 + jnp.exp(-x))


def _silu(x):
    return x * _sigmoid(x)


def _rms(x, g, n):
    ms = jnp.sum(x * x, axis=-1, keepdims=True) * (1.0 / n)
    return x * lax.rsqrt(ms + EPS) * g


def _chunk_masks():
    c = GDN_CHUNK
    i = lax.broadcasted_iota(jnp.int32, (c, c), 0)
    j = lax.broadcasted_iota(jnp.int32, (c, c), 1)
    lower = i >= j
    strict = i > j
    eye = (i == j).astype(F32)
    blocks = []
    b = 1
    while b < c:
        same = (i // (2 * b)) == (j // (2 * b))
        blocks.append(same & ((i % (2 * b)) >= b) & ((j % (2 * b)) < b))
        b *= 2
    return lower, strict, eye, blocks


def _unit_lower_inverse(low, eye, blocks):
    t = eye - jnp.where(blocks[0], low, 0.0)
    for m in blocks[1:]:
        lo = jnp.where(m, low, 0.0)
        t = t - _nn(t, _nn(lo, t, HI), HI)
    return t


@jax.custom_vjp
def _known_inverse(low, tinv):
    return tinv


def _known_inverse_fwd(low, tinv):
    return tinv, tinv


def _known_inverse_bwd(tinv, dt):
    return -_tn(tinv, _nt(dt, tinv, HI), HI), jnp.zeros_like(tinv)


_known_inverse.defvjp(_known_inverse_fwd, _known_inverse_bwd)


def _gdn_chunk(q, k, v, gc, bb, s, masks, tinv=None):
    lower, strict, eye, blocks = masks
    qs = q * (GDN_D ** -0.5)
    gct = jnp.swapaxes(gc, -1, -2)
    decay = jnp.exp(jnp.where(lower, gc - gct, -1e30))
    kb = k * bb
    low = jnp.where(strict, _nt(kb, k, HI) * decay, 0.0)
    tinv = _unit_lower_inverse(low, eye, blocks) if tinv is None else _known_inverse(low, tinv)
    eg = jnp.exp(gc)
    w = _nn(tinv, kb * eg, HI)
    u = _nn(tinv, v * bb, HI)
    attn = _nt(qs, k, HI) * decay
    last = lax.broadcasted_iota(jnp.int32, gc.shape[-2:], 0) == GDN_CHUNK - 1
    g_end = jnp.sum(jnp.where(last, gc, 0.0), axis=-2, keepdims=True)
    k_dec = k * jnp.exp(g_end - gc)
    v_new = u - _nn(w, s, HI)
    o = _nn(qs * eg, s, HI) + _nn(attn, v_new, HI)
    s_new = s * jnp.exp(g_end) + _tn(k_dec, v_new, HI)
    return o, s_new, tinv


GDN_GROUP = 8
GDN_GROUPS = N_HEADS // GDN_GROUP


def _group_heads(ref):
    return jnp.stack([ref[:, pl.ds(j * SLOT, GDN_D)] for j in range(GDN_GROUP)])


def _ungroup_heads(ref, val):
    pad = jnp.zeros((GDN_CHUNK, SLOT - GDN_D), F32)
    for j in range(GDN_GROUP):
        ref[:, pl.ds(j * SLOT, GDN_D)] = val[j]
        ref[:, pl.ds(j * SLOT + GDN_D, SLOT - GDN_D)] = pad


def _gdn_fwd(qkv, gb, bb, carry=None):
    t = qkv.shape[0]
    n_chunks = t // GDN_CHUNK
    d = GDN_D

    def body(q_ref, k_ref, v_ref, g_ref, b_ref, o_ref, keep_ref, s_ref):
        @pl.when(pl.program_id(1) == 0)
        def _():
            s_ref[...] = jnp.zeros_like(s_ref)

        s = s_ref[...]
        keep_ref[:, 0, 0] = s
        o, s_new, tinv = _gdn_chunk(*[_group_heads(r) for r in (q_ref, k_ref, v_ref, g_ref, b_ref)], s, _chunk_masks())
        keep_ref[:, 0, 1] = tinv
        s_ref[...] = s_new
        _ungroup_heads(o_ref, o)

    def spec(kind=0):
        return pl.BlockSpec((GDN_CHUNK, GDN_GROUP * SLOT), lambda h, n: (n, kind * GDN_GROUPS + h))

    return _call_carrying(
        body, carry, (qkv, qkv, qkv, gb, bb), name="gdn_fwd",
        grid=(GDN_GROUPS, n_chunks),
        in_specs=[spec(0), spec(1), spec(2), spec(), spec()],
        out_specs=[spec(), pl.BlockSpec((GDN_GROUP, 1, 2, d, d), lambda h, n: (h, n, 0, 0, 0))],
        out_shape=[jax.ShapeDtypeStruct((t, N_HEADS * SLOT), F32), jax.ShapeDtypeStruct((N_HEADS, n_chunks, 2, d, d), F32)],
        scratch_shapes=[pltpu.VMEM((GDN_GROUP, d, d), F32)],
        compiler_params=pltpu.CompilerParams(dimension_semantics=("arbitrary", "arbitrary")),
    )


def _gdn_bwd(qkv, gb, bb, keep, do, carry=None):
    t = qkv.shape[0]
    n_chunks = t // GDN_CHUNK
    d = GDN_D

    def body(q_ref, k_ref, v_ref, g_ref, b_ref, keep_ref, do_ref, dqkv_ref, dg_ref, db_ref, ds_ref):
        @pl.when(pl.program_id(1) == 0)
        def _():
            ds_ref[...] = jnp.zeros_like(ds_ref)

        masks = _chunk_masks()
        tinv = keep_ref[:, 0, 1]
        _, pull = jax.vjp(lambda *a: _gdn_chunk(*a, masks, tinv)[:2],
                          *[_group_heads(r) for r in (q_ref, k_ref, v_ref, g_ref, b_ref)], keep_ref[:, 0, 0])
        dq, dk, dv, dg, db, ds = pull((_group_heads(do_ref), ds_ref[...]))
        ds_ref[...] = ds
        for i, val in enumerate((dq, dk, dv)):
            _ungroup_heads(dqkv_ref.at[i], val)
        _ungroup_heads(dg_ref, dg)
        _ungroup_heads(db_ref, db)

    def spec(kind=0):
        return pl.BlockSpec((GDN_CHUNK, GDN_GROUP * SLOT), lambda h, n: (n_chunks - 1 - n, kind * GDN_GROUPS + h))

    return _call_carrying(
        body, carry, (qkv, qkv, qkv, gb, bb, keep, do), name="gdn_bwd",
        grid=(GDN_GROUPS, n_chunks),
        in_specs=[spec(0), spec(1), spec(2), spec(), spec(),
                  pl.BlockSpec((GDN_GROUP, 1, 2, d, d), lambda h, n: (h, n_chunks - 1 - n, 0, 0, 0)), spec()],
        out_specs=[pl.BlockSpec((3, GDN_CHUNK, GDN_GROUP * SLOT), lambda h, n: (0, n_chunks - 1 - n, h)), spec(), spec()],
        out_shape=[jax.ShapeDtypeStruct((3, t, N_HEADS * SLOT), F32)] + [jax.ShapeDtypeStruct((t, N_HEADS * SLOT), F32)] * 2,
        scratch_shapes=[pltpu.VMEM((GDN_GROUP, d, d), F32)],
        compiler_params=pltpu.CompilerParams(dimension_semantics=("arbitrary", "arbitrary")),
    )


def _rowwise(name, fn, rows, consts, outs, sums=(), tm=512):
    rows = [x if isinstance(x, tuple) else (x, x.shape[1], 0) for x in rows]
    t = rows[0][0].shape[0]
    tm = min(tm, t)
    steps = t // tm
    n_r, n_c, n_o, n_s = len(rows), len(consts), len(outs), len(sums)

    def window(width, block):
        return pl.BlockSpec((tm, width), lambda i: (i, block))

    def body(*refs):
        r, c = refs[:n_r], refs[n_r:n_r + n_c]
        o, s = refs[n_r + n_c:n_r + n_c + n_o], refs[n_r + n_c + n_o:]
        vals, tot = fn([x[...] for x in r], [x[...] for x in c])
        for ref, val in zip(o, vals):
            ref[...] = val.astype(ref.dtype)
        if n_s:
            @pl.when(pl.program_id(0) == 0)
            def _():
                for ref in s:
                    ref[...] = jnp.zeros_like(ref)

            for ref, val in zip(s, tot):
                ref[...] += val

    return pl.pallas_call(
        body, name=name,
        grid=(steps,),
        in_specs=[window(w, b) for _, w, b in rows] + [pl.BlockSpec(x.shape, lambda i: (0, 0)) for x in consts],
        out_specs=[pl.BlockSpec((tm, w), lambda i: (i, 0)) for w, _ in outs]
        + [pl.BlockSpec((1, w), lambda i: (0, 0)) for w in sums],
        out_shape=[jax.ShapeDtypeStruct((t, w), dt) for w, dt in outs]
        + [jax.ShapeDtypeStruct((1, w), F32) for w in sums],
        compiler_params=pltpu.CompilerParams(dimension_semantics=("arbitrary",)),
    )(*[x for x, _, _ in rows], *consts)


def _tile(dim, target):
    if dim <= target:
        return dim
    best = None
    for cand in range(128, target + 1, 128):
        if dim % cand == 0:
            best = cand
    assert best is not None, (dim, target)
    return best


def _matmul(name, a, b, mode, out_dtype=F32, tm=1024, tn=1024, tk=2048, carry=None):
    if mode == "nn":
        (m, k), n = a.shape, b.shape[1]
    elif mode == "nt":
        (m, k), n = a.shape, b.shape[0]
    else:
        (k, m), n = a.shape, b.shape[1]
    tm, tn, tk = _tile(m, tm), _tile(n, tn), _tile(k, tk)
    k_steps = k // tk
    product = {"nn": _nn, "nt": _nt, "tn": _tn}[mode]

    def body(a_ref, b_ref, o_ref, acc_ref):
        part = product(a_ref[...].astype(BF16), b_ref[...].astype(BF16))
        if k_steps == 1:
            o_ref[...] = part.astype(o_ref.dtype)
        else:
            kk = pl.program_id(2)

            @pl.when(kk == 0)
            def _():
                acc_ref[...] = part

            @pl.when(kk > 0)
            def _():
                acc_ref[...] += part

            @pl.when(kk == k_steps - 1)
            def _():
                o_ref[...] = acc_ref[...].astype(o_ref.dtype)

    a_spec = pl.BlockSpec((tk, tm), lambda i, j, kk: (kk, i)) if mode == "tn" else pl.BlockSpec((tm, tk), lambda i, j, kk: (i, kk))
    b_spec = pl.BlockSpec((tn, tk), lambda i, j, kk: (j, kk)) if mode == "nt" else pl.BlockSpec((tk, tn), lambda i, j, kk: (kk, j))
    (out,), carried = _call_carrying(
        body, carry, (a, b), name=name,
        grid=(m // tm, n // tn, k_steps),
        in_specs=[a_spec, b_spec],
        out_specs=[pl.BlockSpec((tm, tn), lambda i, j, kk: (i, j))],
        out_shape=[jax.ShapeDtypeStruct((m, n), out_dtype)],
        scratch_shapes=[pltpu.VMEM((tm, tn) if k_steps > 1 else (8, 128), F32)],
        compiler_params=pltpu.CompilerParams(dimension_semantics=("arbitrary", "arbitrary", "arbitrary")),
    )
    return out if carry is None else (out, carried)


FFN_TM = 512
FFN_BWD_TM = 256
FFN_BLOCKS = 4
FFN_GATE, FFN_UP, FFN_DOWN = 0, 1, 2


def _ffn_weight_specs(ffn_w, first):
    _, _, rows, dm = ffn_w.shape

    def spec(k):
        return pl.BlockSpec((FFN_BLOCKS, None, rows, dm), lambda i, j: (j, first + k, 0, 0))

    return [spec(FFN_GATE), spec(FFN_UP), spec(FFN_DOWN)], FFN_BLOCKS * rows


def _ffn_fwd(name, x, g_pre, ffn_w, first, g_post, carry=None):
    t, dm = x.shape
    tm = min(FFN_TM, t)
    w_specs, tf = _ffn_weight_specs(ffn_w, first)
    f_steps = N_DEV // FFN_BLOCKS

    def body(x_ref, gpre_ref, wg_ref, wu_ref, wd_ref, gpost_ref, h_ref, y_ref, hg_ref, hu_ref, xn_ref, acc_ref):
        j = pl.program_id(1)

        @pl.when(j == 0)
        def _():
            xn_ref[...] = _rms(x_ref[...], gpre_ref[...], dm).astype(BF16)
            acc_ref[...] = jnp.zeros_like(acc_ref)

        xn = xn_ref[...]
        wg, wu, wd = (r[...].reshape(tf, dm) for r in (wg_ref, wu_ref, wd_ref))
        hg, hu = _nt(xn, wg), _nt(xn, wu)
        hg_ref[...] = hg.astype(BF16)
        hu_ref[...] = hu.astype(BF16)
        a = _silu(hg) * hu
        acc_ref[...] += _nn(a.astype(BF16), wd)

        @pl.when(j == f_steps - 1)
        def _():
            h = acc_ref[...]
            h_ref[...] = h
            y_ref[...] = x_ref[...] + 0.5 * _rms(h, gpost_ref[...], dm)

    row = pl.BlockSpec((tm, dm), lambda i, j: (i, 0))
    vec = pl.BlockSpec((1, dm), lambda i, j: (0, 0))
    wide = pl.BlockSpec((tm, tf), lambda i, j: (i, j))
    return _call_carrying(
        body, carry, (x, g_pre, ffn_w, ffn_w, ffn_w, g_post), name=name,
        grid=(t // tm, f_steps),
        in_specs=[row, vec, *w_specs, vec],
        out_specs=[row, row, wide, wide],
        out_shape=[jax.ShapeDtypeStruct((t, dm), F32)] * 2 + [jax.ShapeDtypeStruct((t, f_steps * tf), BF16)] * 2,
        scratch_shapes=[pltpu.VMEM((tm, dm), BF16), pltpu.VMEM((tm, dm), F32)],
        compiler_params=pltpu.CompilerParams(dimension_semantics=("arbitrary", "arbitrary")),
    )


def _ffn_bwd(name, x, h, hg, hu, dy, g_pre, ffn_w, first, g_post, carry=None):
    t, dm = x.shape
    tm = min(FFN_BWD_TM, t)
    w_specs, tf = _ffn_weight_specs(ffn_w, first)
    f_steps = N_DEV // FFN_BLOCKS
    f = f_steps * tf

    def post(hv, g):
        return 0.5 * _rms(hv, g, dm)

    def pre(xv, g):
        return _rms(xv, g, dm)

    def body(x_ref, h_ref, dy_ref, hg_ref, hu_ref, gpre_ref, wg_ref, wu_ref, wd_ref, gpost_ref,
             dx_ref, xn_ref, dh_ref, a_ref, dhg_ref, dhu_ref, dgpre_ref, dgpost_ref, acc_ref):
        i, j = pl.program_id(0), pl.program_id(1)

        @pl.when((i == 0) & (j == 0))
        def _():
            dgpre_ref[...] = jnp.zeros_like(dgpre_ref)
            dgpost_ref[...] = jnp.zeros_like(dgpost_ref)

        @pl.when(j == 0)
        def _():
            xn_ref[...] = pre(x_ref[...], gpre_ref[...]).astype(BF16)
            _, pull = jax.vjp(post, h_ref[...], gpost_ref[...])
            dh, dg = pull(dy_ref[...])
            dh_ref[...] = dh.astype(BF16)
            dgpost_ref[...] += dg
            acc_ref[...] = jnp.zeros_like(acc_ref)

        wg, wu, wd = (r[...].reshape(tf, dm) for r in (wg_ref, wu_ref, wd_ref))
        hg, hu = hg_ref[...].astype(F32), hu_ref[...].astype(F32)
        da = _nt(dh_ref[...], wd)
        sig = _sigmoid(hg)
        act = hg * sig
        dhu = (da * act).astype(BF16)
        dhg = (da * hu * (sig * (1.0 + hg * (1.0 - sig)))).astype(BF16)
        a_ref[...] = (act * hu).astype(BF16)
        dhg_ref[...] = dhg
        dhu_ref[...] = dhu
        acc_ref[...] += _nn(dhg, wg) + _nn(dhu, wu)

        @pl.when(j == f_steps - 1)
        def _():
            _, pull = jax.vjp(pre, x_ref[...], gpre_ref[...])
            dx, dg = pull(acc_ref[...])
            dx_ref[...] = dy_ref[...] + dx
            dgpre_ref[...] += dg

    row = pl.BlockSpec((tm, dm), lambda i, j: (i, 0))
    vec = pl.BlockSpec((1, dm), lambda i, j: (0, 0))
    wide = pl.BlockSpec((tm, tf), lambda i, j: (i, j))
    return _call_carrying(
        body, carry, (x, h, dy, hg, hu, g_pre, ffn_w, ffn_w, ffn_w, g_post), name=name,
        grid=(t // tm, f_steps),
        in_specs=[row, row, row, wide, wide, vec, *w_specs, vec],
        out_specs=[row, row, row, wide, wide, wide, vec, vec],
        out_shape=[jax.ShapeDtypeStruct((t, dm), F32), jax.ShapeDtypeStruct((t, dm), BF16), jax.ShapeDtypeStruct((t, dm), BF16),
                   jax.ShapeDtypeStruct((t, f), BF16), jax.ShapeDtypeStruct((t, f), BF16), jax.ShapeDtypeStruct((t, f), BF16),
                   jax.ShapeDtypeStruct((1, dm), F32), jax.ShapeDtypeStruct((1, dm), F32)],
        scratch_shapes=[pltpu.VMEM((tm, dm), F32)],
        compiler_params=pltpu.CompilerParams(dimension_semantics=("arbitrary", "arbitrary")),
    )


ATT_T = 512
ATT_GROUP = 2
ATT_SCALE = (MLA_NOPE + MLA_ROPE) ** -0.5


def _stack_slots(ref, group):
    return jnp.stack([ref[:, pl.ds(j * SLOT, SLOT)] for j in range(group)])


def _unstack_slots(ref, val):
    for j in range(val.shape[0]):
        ref[:, pl.ds(j * SLOT, SLOT)] = val[j].astype(ref.dtype)


def _scores(q, k, diagonal):
    s = _nt(q, k) * ATT_SCALE
    if diagonal:
        row = lax.broadcasted_iota(jnp.int32, s.shape[1:], 0)
        col = lax.broadcasted_iota(jnp.int32, s.shape[1:], 1)
        s = jnp.where(col <= row, s, -1e30)
    return s


def _attn_pairs(steps, q_major):
    pairs = ([(qi, ki) for qi in range(steps) for ki in range(qi + 1)] if q_major
             else [(qi, ki) for ki in range(steps) for qi in range(ki, steps)])
    return jnp.array([p[0] for p in pairs], jnp.int32), jnp.array([p[1] for p in pairs], jnp.int32)


def _attn_specs(tile):
    width = ATT_GROUP * SLOT
    return (pl.BlockSpec((tile, width), lambda h, p, qt, kt: (qt[p], h)),
            pl.BlockSpec((tile, width), lambda h, p, qt, kt: (kt[p], h)))


def _attn_fwd(q, k, v):
    t = q.shape[0]
    tile = min(ATT_T, t)
    steps = t // tile
    g = ATT_GROUP

    strip = min(SLOT, tile)

    def body(qt_ref, kt_ref, q_ref, k_ref, v_ref, o_ref, lse_ref, m_ref, l_ref, alpha_ref, acc_ref, s_ref, p_ref):
        qi, ki = qt_ref[pl.program_id(1)], kt_ref[pl.program_id(1)]

        @pl.when(ki == 0)
        def _():
            m_ref[...] = jnp.full_like(m_ref, -1e30)
            l_ref[...] = jnp.zeros_like(l_ref)
            acc_ref[...] = jnp.zeros_like(acc_ref)

        def step(diagonal):
            s_ref[...] = _nt(_stack_slots(k_ref, g), _stack_slots(q_ref, g))
            for j in range(tile // strip):
                c = pl.ds(j * strip, strip)
                s = s_ref[:, :, c] * ATT_SCALE
                if diagonal:
                    key = lax.broadcasted_iota(jnp.int32, s.shape[1:], 0)
                    query = lax.broadcasted_iota(jnp.int32, s.shape[1:], 1) + j * strip
                    s = jnp.where(key <= query, s, -1e30)
                m_old = m_ref[:, :, c]
                m_new = jnp.maximum(m_old, jnp.max(s, axis=1, keepdims=True))
                p = jnp.exp(s - m_new)
                alpha = jnp.exp(m_old - m_new)
                l_ref[:, :, c] = alpha * l_ref[:, :, c] + jnp.sum(p, axis=1, keepdims=True)
                alpha_ref[:, :, c] = alpha
                m_ref[:, :, c] = m_new
                p_ref[:, :, c] = p.astype(BF16)
            acc_ref[...] = acc_ref[...] * alpha_ref[...] + _tn(_stack_slots(v_ref, g), p_ref[...])

        @pl.when(ki < qi)
        def _():
            step(False)

        @pl.when(ki == qi)
        def _():
            step(True)
            out = acc_ref[...] / l_ref[...]
            lse = jnp.broadcast_to(m_ref[...] + jnp.log(l_ref[...]), out.shape)
            for j in range(g):
                o_ref[:, pl.ds(j * SLOT, SLOT)] = out[j].T
                lse_ref[:, pl.ds(j * SLOT, SLOT)] = lse[j].T

    q_spec, k_spec = _attn_specs(tile)
    tables = _attn_pairs(steps, True)
    return pl.pallas_call(
        body, name="attn_fwd",
        grid_spec=pltpu.PrefetchScalarGridSpec(
            num_scalar_prefetch=2, grid=(N_HEADS // g, tables[0].shape[0]),
            in_specs=[q_spec, k_spec, k_spec], out_specs=[q_spec, q_spec],
            scratch_shapes=[pltpu.VMEM((g, 1, tile), F32), pltpu.VMEM((g, 1, tile), F32), pltpu.VMEM((g, 1, tile), F32),
                            pltpu.VMEM((g, SLOT, tile), F32), pltpu.VMEM((g, tile, tile), F32), pltpu.VMEM((g, tile, tile), BF16)]),
        out_shape=[jax.ShapeDtypeStruct((t, N_HEADS * SLOT), F32)] * 2,
        compiler_params=pltpu.CompilerParams(dimension_semantics=("parallel", "arbitrary")),
    )(*tables, q, k, v)


def _attn_grad_scores(q, k, v, do, lse_ref, delta_ref, diagonal):
    g = ATT_GROUP
    p = jnp.exp(_scores(q, k, diagonal) - _stack_slots(lse_ref, g)[:, :, 0:1])
    dp = _nt(do, v)
    return p, p * (dp - _stack_slots(delta_ref, g)[:, :, 0:1]) * ATT_SCALE


def _attn_bwd(q, k, v, do, lse, delta):
    t = q.shape[0]
    tile = min(ATT_T, t)
    steps = t // tile
    g = ATT_GROUP

    def body(qt_ref, kt_ref, q_ref, k_ref, v_ref, do_ref, lse_ref, delta_ref, dq_ref, dk_ref, dv_ref, dk_acc, dv_acc):
        qi, ki = qt_ref[pl.program_id(1)], kt_ref[pl.program_id(1)]

        @pl.when(pl.program_id(1) == 0)
        def _():
            dq_ref[...] = jnp.zeros_like(dq_ref)

        def step(diagonal):
            qq, kk = _stack_slots(q_ref, g), _stack_slots(k_ref, g)
            do_b = _stack_slots(do_ref, g).astype(BF16)
            p, ds = _attn_grad_scores(qq, kk, _stack_slots(v_ref, g), do_b, lse_ref, delta_ref, diagonal)
            ds = ds.astype(BF16)
            dv_acc[...] += _tn(p.astype(BF16), do_b)
            dk_acc[...] += _tn(ds, qq)
            dq = _nn(ds, kk)
            rows = pl.ds(pl.multiple_of(qi * tile, tile), tile)
            for j in range(g):
                dq_ref[rows, pl.ds(j * SLOT, SLOT)] += dq[j]

        @pl.when(qi == ki)
        def _():
            dk_acc[...] = jnp.zeros_like(dk_acc)
            dv_acc[...] = jnp.zeros_like(dv_acc)
            step(True)

        @pl.when(qi > ki)
        def _():
            step(False)

        @pl.when(qi == steps - 1)
        def _():
            _unstack_slots(dk_ref, dk_acc[...])
            _unstack_slots(dv_ref, dv_acc[...])

    q_spec, k_spec = _attn_specs(tile)
    tables = _attn_pairs(steps, False)
    return pl.pallas_call(
        body, name="attn_bwd",
        grid_spec=pltpu.PrefetchScalarGridSpec(
            num_scalar_prefetch=2, grid=(N_HEADS // g, tables[0].shape[0]),
            in_specs=[q_spec, k_spec, k_spec, q_spec, q_spec, q_spec],
            out_specs=[pl.BlockSpec((t, g * SLOT), lambda h, p, qt, kt: (0, h)), k_spec, k_spec],
            scratch_shapes=[pltpu.VMEM((g, tile, SLOT), F32), pltpu.VMEM((g, tile, SLOT), F32)]),
        out_shape=[jax.ShapeDtypeStruct((t, N_HEADS * SLOT), F32)] * 3,
        compiler_params=pltpu.CompilerParams(dimension_semantics=("parallel", "arbitrary")),
    )(*tables, q, k, v, do, lse, delta)


CONV_PAD = 8


def _fill_padded(ref, val):
    t = val.shape[0]
    zeros = jnp.zeros((CONV_PAD, val.shape[1]), val.dtype)
    ref[pl.ds(0, CONV_PAD)] = zeros
    ref[pl.ds(CONV_PAD + t, CONV_PAD)] = zeros
    ref[pl.ds(CONV_PAD, t)] = val


def _shifted(ref, s):
    return ref[pl.ds(CONV_PAD - s, ref.shape[0] - 2 * CONV_PAD)]


def _l2norm(x):
    return x * lax.rsqrt(jnp.sum(x * x, axis=-1, keepdims=True) + EPS)


def _conv_pre(x_pad, w):
    y = w[GDN_CONV - 1:GDN_CONV, :] * _shifted(x_pad, 0)
    for s in range(1, GDN_CONV):
        y = y + w[GDN_CONV - 1 - s:GDN_CONV - s, :] * _shifted(x_pad, s)
    return y


def _gdn_conv_fwd(x, w):
    t, width = x.shape

    def body(x_ref, w_ref, o_ref, x_pad):
        _fill_padded(x_pad, x_ref[...])
        act = _silu(_conv_pre(x_pad, w_ref[...]))
        normed = pl.program_id(0) < 2 * N_HEADS
        o_ref[...] = jnp.where(normed, _l2norm(act), act)

    return pl.pallas_call(
        body, name="gdn_conv_fwd",
        grid=(width // SLOT,),
        in_specs=[pl.BlockSpec((t, SLOT), lambda j: (0, j)), pl.BlockSpec((GDN_CONV, SLOT), lambda j: (0, j))],
        out_specs=pl.BlockSpec((t, SLOT), lambda j: (0, j)),
        out_shape=jax.ShapeDtypeStruct((t, width), F32),
        scratch_shapes=[pltpu.VMEM((t + 2 * CONV_PAD, SLOT), F32)],
        compiler_params=pltpu.CompilerParams(dimension_semantics=("parallel",)),
    )(x, w)


def _gdn_conv_bwd(x, w, dout):
    t, width = x.shape

    def body(x_ref, w_ref, do_ref, dx_ref, dw_ref, x_pad, dy_pad):
        wv = w_ref[...]
        _fill_padded(x_pad, x_ref[...])
        y = _conv_pre(x_pad, wv)
        sig = _sigmoid(y)
        act = y * sig
        _, pull = jax.vjp(_l2norm, act)
        normed = pl.program_id(0) < 2 * N_HEADS
        dact = jnp.where(normed, pull(do_ref[0])[0], do_ref[0])
        dy = dact * (sig * (1.0 + y * (1.0 - sig)))
        _fill_padded(dy_pad, dy)
        dx = wv[GDN_CONV - 1:GDN_CONV, :] * dy
        for s in range(1, GDN_CONV):
            dx = dx + wv[GDN_CONV - 1 - s:GDN_CONV - s, :] * _shifted(dy_pad, -s)
        dx_ref[...] = dx.astype(BF16)
        for s in range(GDN_CONV):
            dw_ref[GDN_CONV - 1 - s:GDN_CONV - s, :] = jnp.sum(dy * _shifted(x_pad, s), axis=0, keepdims=True)

    col = pl.BlockSpec((t, SLOT), lambda j: (0, j))
    tap = pl.BlockSpec((GDN_CONV, SLOT), lambda j: (0, j))
    return pl.pallas_call(
        body, name="gdn_conv_bwd",
        grid=(width // SLOT,),
        in_specs=[col, tap, pl.BlockSpec((1, t, SLOT), lambda j: (j // N_HEADS, 0, j % N_HEADS))],
        out_specs=[col, tap],
        out_shape=[jax.ShapeDtypeStruct((t, width), BF16), jax.ShapeDtypeStruct((GDN_CONV, width), F32)],
        scratch_shapes=[pltpu.VMEM((t + 2 * CONV_PAD, SLOT), F32)] * 2,
        compiler_params=pltpu.CompilerParams(dimension_semantics=("parallel",)),
    )(x, w, dout)


def _softplus(x):
    e = jnp.exp(-jnp.abs(x))
    u = 1.0 + e
    log1p = jnp.where(u == 1.0, e, jnp.log(u) * e / jnp.where(u == 1.0, 1.0, u - 1.0))
    return jnp.maximum(x, 0.0) + log1p


def _chunk_running_sum(x, reverse=False):
    tm = x.shape[0]
    at = lax.broadcasted_iota(jnp.int32, x.shape, 0) % GDN_CHUNK
    step = 1
    while step < GDN_CHUNK:
        if reverse:
            x = x + jnp.where(at < GDN_CHUNK - step, pltpu.roll(x, tm - step, 0), 0.0)
        else:
            x = x + jnp.where(at >= step, pltpu.roll(x, step, 0), 0.0)
        step *= 2
    return x


def _gates_fwd(ab, a_log, dt_bias):
    def fn(rows, consts):
        (abv,), (alog, dtb) = rows, consts
        g = _chunk_running_sum(-jnp.exp(alog) * _softplus(abv + dtb))
        beta = _sigmoid(abv)
        shape = (abv.shape[0], SLOT)
        g_slots = [jnp.broadcast_to(g[:, h:h + 1], shape) for h in range(N_HEADS)]
        b_slots = [jnp.broadcast_to(beta[:, N_HEADS + h:N_HEADS + h + 1], shape) for h in range(N_HEADS)]
        return [jnp.concatenate(g_slots, axis=1), jnp.concatenate(b_slots, axis=1)], []

    width = N_HEADS * SLOT
    return _rowwise("gdn_gates_fwd", fn, [ab], [a_log, dt_bias], [(width, F32), (width, F32)])


def _gates_bwd(ab, a_log, dt_bias, dg, dbeta):
    def fn(rows, consts):
        (abv, dgv, dbv), (alog, dtb) = rows, consts
        lane = lax.broadcasted_iota(jnp.int32, abv.shape, 1)
        dg_tok = jnp.zeros_like(abv)
        db_tok = jnp.zeros_like(abv)
        for h in range(N_HEADS):
            dg_tok = dg_tok + jnp.where(lane == h, jnp.sum(dgv[:, h * SLOT:(h + 1) * SLOT], axis=1, keepdims=True), 0.0)
            db_tok = db_tok + jnp.where(lane == N_HEADS + h, jnp.sum(dbv[:, h * SLOT:(h + 1) * SLOT], axis=1, keepdims=True), 0.0)
        dg_tok = _chunk_running_sum(dg_tok, reverse=True)
        xa = abv + dtb
        g = -jnp.exp(alog) * _softplus(xa)
        da = dg_tok * (-jnp.exp(alog)) * _sigmoid(xa)
        beta = _sigmoid(abv)
        dab = jnp.where(lane < N_HEADS, da, db_tok * beta * (1.0 - beta))
        dab = jnp.where(lane < 2 * N_HEADS, dab, 0.0)
        d_alog = jnp.sum(jnp.where(lane < N_HEADS, dg_tok * g, 0.0), axis=0, keepdims=True)
        d_dtb = jnp.sum(jnp.where(lane < N_HEADS, da, 0.0), axis=0, keepdims=True)
        return [dab], [d_alog, d_dtb]

    return _rowwise("gdn_gates_bwd", fn, [ab, dg, dbeta], [a_log, dt_bias], [(SLOT, F32)], sums=[SLOT, SLOT])


ROPE_HALF = MLA_ROPE // 2


def _rope_tables(positions):
    freqs = ROPE_THETA ** (-jnp.arange(ROPE_HALF, dtype=F32) / ROPE_HALF)
    ang = positions.astype(F32)[:, None] * freqs
    cos, sin = jnp.cos(ang), jnp.sin(ang)
    t = positions.shape[0]
    ones, zeros = jnp.ones((t, MLA_NOPE), F32), jnp.zeros((t, MLA_NOPE), F32)
    tail = jnp.zeros((t, SLOT - MLA_NOPE - MLA_ROPE), F32)
    half0 = jnp.zeros((t, ROPE_HALF), F32)
    same = jnp.concatenate([ones, cos, cos, tail], axis=1)
    from_low = jnp.concatenate([zeros, half0, sin, tail], axis=1)
    from_high = jnp.concatenate([zeros, -sin, half0, tail], axis=1)
    return same, from_low, from_high


def _rope(x, tabs):
    same, from_low, from_high = tabs
    width = x.shape[1]
    return x * same + pltpu.roll(x, ROPE_HALF, 1) * from_low + pltpu.roll(x, width - ROPE_HALF, 1) * from_high


def _rope_transposed(dy, tabs):
    same, from_low, from_high = tabs
    width = dy.shape[1]
    return dy * same + pltpu.roll(dy * from_low, width - ROPE_HALF, 1) + pltpu.roll(dy * from_high, ROPE_HALF, 1)


def _tile_slots(tab):
    return jnp.concatenate([tab] * N_HEADS, axis=1)


A_WIDTH = MLA_Q_RANK + MLA_KV_RANK + 2 * SLOT
A_KPE = MLA_Q_RANK + MLA_KV_RANK
A_AB = A_KPE + SLOT
WIDE = N_HEADS * SLOT


def _mla_front_fwd(proj_a, tabs, g_q, g_kv, w_uq, w_kv):
    def fn(rows, consts):
        pa, *tb = rows
        gq, gkv, wuq, wkv = consts
        cqn = _rms(pa[:, :MLA_Q_RANK], gq, MLA_Q_RANK).astype(BF16)
        ckvn = _rms(pa[:, MLA_Q_RANK:A_KPE], gkv, MLA_KV_RANK).astype(BF16)
        kv = _nt(ckvn, wkv)
        q = _rope(_nt(cqn, wuq), [_tile_slots(x) for x in tb])
        k = kv[:, :WIDE] + _tile_slots(_rope(pa[:, A_KPE:A_AB], tb))
        return [cqn, ckvn, q, k, kv[:, WIDE:]], []

    return _rowwise("mla_front_fwd", fn, [proj_a, *tabs], [g_q, g_kv, w_uq, w_kv],
                    [(MLA_Q_RANK, BF16), (MLA_KV_RANK, BF16)] + [(WIDE, BF16)] * 3)


def _mla_front_bwd(proj_a, tabs, g_q, g_kv, w_uq, w_kv, dq, dk, dv, dab):
    def fn(rows, consts):
        pa, t0, t1, t2, dqv, dkv, dvv, da = rows
        gq, gkv, wuq, wkv = consts
        tb = (t0, t1, t2)
        dq_p = _rope_transposed(dqv, [_tile_slots(x) for x in tb]).astype(BF16)
        dkv_p = jnp.concatenate([dkv, dvv], axis=1).astype(BF16)
        dkpe = dkv[:, :SLOT]
        for h in range(1, N_HEADS):
            dkpe = dkpe + dkv[:, h * SLOT:(h + 1) * SLOT]
        _, pull_q = jax.vjp(lambda x, g: _rms(x, g, MLA_Q_RANK), pa[:, :MLA_Q_RANK], gq)
        _, pull_kv = jax.vjp(lambda x, g: _rms(x, g, MLA_KV_RANK), pa[:, MLA_Q_RANK:A_KPE], gkv)
        dcq, dgq = pull_q(_nn(dq_p, wuq))
        dckv, dgkv = pull_kv(_nn(dkv_p, wkv))
        return [jnp.concatenate([dcq, dckv, _rope_transposed(dkpe, tb), da], axis=1), dq_p, dkv_p], [dgq, dgkv]

    return _rowwise("mla_front_bwd", fn, [proj_a, *tabs, dq, dk, dv, dab], [g_q, g_kv, w_uq, w_kv],
                    [(A_WIDTH, BF16), (WIDE, BF16), (2 * WIDE, BF16)], sums=[MLA_Q_RANK, MLA_KV_RANK])


def _slot_sum(x):
    parts = [jnp.broadcast_to(jnp.sum(x[:, h * SLOT:(h + 1) * SLOT], axis=1, keepdims=True), (x.shape[0], SLOT))
             for h in range(N_HEADS)]
    return jnp.concatenate(parts, axis=1)


def _mix_join(o_mla, o_gdn, gate, g_mla, g_gdn):
    mla = _rms(o_mla, g_mla, N_HEADS * MLA_V)
    gdn = o_gdn * lax.rsqrt(_slot_sum(o_gdn * o_gdn) * (1.0 / GDN_D) + EPS) * g_gdn * _silu(gate)
    return mla, gdn


MIX_TM = 256


def _mix_fwd(o_mla, o_gdn, gate, x, g_mla, g_gdn, w_out, g_post):
    dm = x.shape[1]

    def fn(rows, consts):
        om, og, gt, xv = rows
        gm, gg, wo, gp = consts
        cat = jnp.concatenate(_mix_join(om, og, gt, gm, gg), axis=1).astype(BF16)
        mixed = _nn(cat, wo)
        return [cat, mixed, xv + _rms(mixed, gp, dm)], []

    return _rowwise("mix_fwd", fn, [o_mla, o_gdn, gate, x], [g_mla, g_gdn, w_out, g_post],
                    [(2 * WIDE, BF16), (dm, F32), (dm, F32)], tm=MIX_TM)


def _mix_bwd(o_mla, o_gdn, gate, mixed, dy, g_mla, g_gdn, w_out, g_post):
    dm = mixed.shape[1]

    def fn(rows, consts):
        om, og, gt, mx, dyv = rows
        gm, gg, wo, gp = consts
        _, pull_post = jax.vjp(lambda hv, gv: _rms(hv, gv, dm), mx, gp)
        dmixed, dgp = pull_post(dyv)
        dmixed = dmixed.astype(BF16)
        dc = _nt(dmixed, wo)
        _, pull = jax.vjp(lambda x, g: _rms(x, g, N_HEADS * MLA_V), om, gm)
        dom, dgm = pull(dc[:, :WIDE])
        dn_out = dc[:, WIDE:]
        r = lax.rsqrt(_slot_sum(og * og) * (1.0 / GDN_D) + EPS)
        sig = _sigmoid(gt)
        normed = og * r
        dn = dn_out * gg * (gt * sig)
        dog = r * dn - normed * (r * r) * _slot_sum(dn * og) * (1.0 / GDN_D)
        dgt = dn_out * normed * gg * (sig * (1.0 + gt * (1.0 - sig)))
        dgg = jnp.sum(dn_out * normed * (gt * sig), axis=0, keepdims=True)
        return [dmixed, dom, _slot_sum(dom * om), dog, dgt], [dgp, dgm, dgg]

    return _rowwise("mix_bwd", fn, [o_mla, o_gdn, gate, mixed, dy], [g_mla, g_gdn, w_out, g_post],
                    [(dm, BF16), (WIDE, F32), (WIDE, F32), (WIDE, F32), (WIDE, BF16)], sums=[dm, WIDE, WIDE], tm=MIX_TM)


def _proj_fwd(x, g, weights):
    dm = x.shape[1]

    def fn(rows, consts):
        hn = _rms(rows[0], consts[0], dm).astype(BF16)
        return [hn] + [_nt(hn, wv) for wv in consts[1:]], []

    return _rowwise("proj_fwd", fn, [x], [g, *weights], [(dm, BF16)] + [(wv.shape[0], F32) for wv in weights], tm=MIX_TM)


def _proj_bwd(x, g, weights, cots, dy):
    dm = x.shape[1]
    n = len(weights)

    def fn(rows, consts):
        xv, dyv, *parts = rows
        dn = _nn(parts[0], consts[1])
        for p, wv in zip(parts[1:], consts[2:]):
            dn = dn + _nn(p, wv)
        _, pull = jax.vjp(lambda a, gv: _rms(a, gv, dm), xv, consts[0])
        dx, dg = pull(dn)
        return [dyv + dx], [dg]

    assert len(cots) == n
    return _rowwise("proj_bwd", fn, [x, dy, *cots], [g, *weights], [(dm, F32)], sums=[dm], tm=MIX_TM)


def _loss_fwd(y, target):
    dm = y.shape[1]

    def fn(rows, consts):
        err = rows[0] - rows[1]
        sq = err * err
        lanes = sq[:, :SLOT]
        for j in range(1, dm // SLOT):
            lanes = lanes + sq[:, j * SLOT:(j + 1) * SLOT]
        return [err * (1.0 / dm)], [jnp.sum(lanes, axis=0, keepdims=True) * (0.5 / dm)]

    return _rowwise("loss", fn, [y, target], [], [(dm, F32)], sums=[SLOT])


W_IN_CUTS = (0, 256, 384, 416, 1952, 1960, 1968, 2480)


def _heads_out(w, per_head, axis=-1):
    axis = axis % w.ndim
    shape = w.shape
    n = shape[axis] // per_head
    w = w.reshape(shape[:axis] + (n, per_head) + shape[axis + 1:])
    pad = [(0, 0)] * w.ndim
    pad[axis + 1] = (0, SLOT - per_head)
    return jnp.pad(w, pad).reshape(shape[:axis] + (n * SLOT,) + shape[axis + 1:])


def _heads_in(w, per_head, axis=-1):
    axis = axis % w.ndim
    shape = w.shape
    n = shape[axis] // SLOT
    w = w.reshape(shape[:axis] + (n, SLOT) + shape[axis + 1:])
    w = lax.slice_in_dim(w, 0, per_head, axis=axis + 1)
    return w.reshape(shape[:axis] + (n * per_head,) + shape[axis + 1:])


def _pad_lanes(v, lo, width=SLOT):
    return jnp.pad(v, [(0, 0)] * (v.ndim - 1) + [(lo, width - lo - v.shape[-1])])


def _pad_rows(v, lo, rows=SLOT):
    return jnp.pad(v, [(lo, rows - lo - v.shape[0])] + [(0, 0)] * (v.ndim - 1))


def _layout_weights(w):
    c = W_IN_CUTS
    w_in = w["w_in_t"]
    p = {}
    p["w_a"] = jnp.concatenate([w_in[c[0]:c[2]], _pad_rows(w_in[c[2]:c[3]], MLA_NOPE), _pad_rows(w_in[c[4]:c[6]], 0)], axis=0)
    p["w_qkv"] = _heads_out(w_in[c[3]:c[4]], GDN_D, axis=0)
    p["w_gate"] = _heads_out(w_in[c[6]:c[7]], GDN_D, axis=0)
    p["w_uq"] = _heads_out(w["uq_t"], MLA_NOPE + MLA_ROPE, axis=0)
    ukv = w["ukv_t"].reshape(N_HEADS, MLA_NOPE + MLA_V, MLA_KV_RANK)
    p["w_kv"] = jnp.concatenate([_heads_out(ukv[:, :MLA_NOPE].reshape(-1, MLA_KV_RANK), MLA_NOPE, axis=0),
                                 _heads_out(ukv[:, MLA_NOPE:].reshape(-1, MLA_KV_RANK), MLA_V, axis=0)], axis=0)
    p["conv"] = _heads_out(w["gdn_conv_w"], GDN_D)
    p["g_mla_out"] = _heads_out(w["mla_out_g"], MLA_V)
    p["g_gdn"] = jnp.tile(_pad_lanes(w["gdn_norm_g"], 0), (1, N_HEADS))
    p["a_log"] = _pad_lanes(w["gdn_a_log"], 0)
    p["dt_bias"] = _pad_lanes(w["gdn_dt_bias"], 0)
    return p


def _unlayout_grads(d):
    c = W_IN_CUTS
    g = {}
    da = d["w_a"]
    kpe0 = A_KPE + MLA_NOPE
    g["w_in_t"] = jnp.concatenate([da[:A_KPE], da[kpe0:kpe0 + MLA_ROPE], _heads_in(d["w_qkv"], GDN_D, axis=0),
                                   da[A_AB:A_AB + 2 * N_HEADS], _heads_in(d["w_gate"], GDN_D, axis=0)], axis=0)
    assert g["w_in_t"].shape[0] == c[-1]
    g["uq_t"] = _heads_in(d["w_uq"], MLA_NOPE + MLA_ROPE, axis=0)
    dk = _heads_in(d["w_kv"][:WIDE], MLA_NOPE, axis=0).reshape(N_HEADS, MLA_NOPE, MLA_KV_RANK)
    dv = _heads_in(d["w_kv"][WIDE:], MLA_V, axis=0).reshape(N_HEADS, MLA_V, MLA_KV_RANK)
    g["ukv_t"] = jnp.concatenate([dk, dv], axis=1).reshape(-1, MLA_KV_RANK)
    g["w_out"] = _heads_in(d["w_out"], GDN_D, axis=0)
    g["gdn_conv_w"] = _heads_in(d["conv"], GDN_D)
    g["mla_out_g"] = _heads_in(d["g_mla_out"], MLA_V)
    g["gdn_norm_g"] = jnp.sum(d["g_gdn"].reshape(N_HEADS, SLOT), axis=0, keepdims=True)[:, :GDN_D]
    g["gdn_a_log"] = d["a_log"][:, :N_HEADS]
    g["gdn_dt_bias"] = d["dt_bias"][:, :N_HEADS]
    return g


def _weight_grad(name, cots, acts, out_dtype=F32, tm=1024, tn=1024, tk=2048, carry=None):
    return _matmul(name, cots, acts, "tn", out_dtype=out_dtype, tm=tm, tn=tn, tk=tk, carry=carry)


def _by_device(a):
    return a.astype(BF16).reshape((N_DEV, a.shape[0] // N_DEV) + a.shape[1:])


def _rows_of(blocks):
    return blocks.reshape((-1,) + blocks.shape[2:])


def _local_step(x, positions, target, w, mid, late):
    tabs = _rope_tables(positions)

    (h1, x1, hg1, hu1), gathered = _ffn_fwd("ffn1_fwd", x, w["ffn1_pre_g"], w["ffn1"], 0, w["ffn1_post_g"], carry=mid)
    w = dict(w, w_in_t=_rows_of(gathered[0]), uq_t=_rows_of(gathered[1]), ukv_t=_rows_of(gathered[2]))
    p = _layout_weights(w)
    in_weights = [p["w_a"], p["w_qkv"], p["w_gate"]]
    hn, proj_a, proj_qkv, proj_gate = _proj_fwd(x1, w["mix_pre_g"], in_weights)
    cqn, ckvn, q, k, v = _mla_front_fwd(proj_a, tabs, w["mla_q_norm_g"], w["mla_kv_norm_g"], p["w_uq"], p["w_kv"])
    o_mla, lse = _attn_fwd(q, k, v)
    ab = (proj_a, SLOT, A_AB // SLOT)
    qkv_n = _gdn_conv_fwd(proj_qkv, p["conv"])
    gb, bb = _gates_fwd(ab, p["a_log"], p["dt_bias"])
    (o_gdn, keep), (ffn2, w_out) = _gdn_fwd(qkv_n, gb, bb, carry=late)
    p["w_out"] = _heads_out(_rows_of(w_out), GDN_D, axis=0)
    cat, mixed, x2 = _mix_fwd(o_mla, o_gdn, proj_gate, x1, p["g_mla_out"], p["g_gdn"], p["w_out"], w["mix_post_g"])
    (h2, y, hg2, hu2), _ = _ffn_fwd("ffn2_fwd", x2, w["ffn2_pre_g"], ffn2, 0, w["ffn2_post_g"])
    dy, loss_lanes = _loss_fwd(y, target)

    g = {}
    (dx2, xn2, dh2, a2, dhg2, dhu2, g["ffn2_pre_g"], g["ffn2_post_g"]), _ = _ffn_bwd(
        "ffn2_bwd", x2, h2, hg2, hu2, dy, w["ffn2_pre_g"], ffn2, 0, w["ffn2_post_g"])
    ffn2_grads = _Scatter([_by_device(_weight_grad("ffn2_dw_gate", dhg2, xn2, BF16, tm=1408)),
                           _by_device(_weight_grad("ffn2_dw_up", dhu2, xn2, BF16, tm=1408)),
                           _by_device(_weight_grad("ffn2_dw_down", a2, dh2, BF16, tm=1408))])
    d = {}
    dmixed, do_mla, delta, do_gdn, dgate, g["mix_post_g"], d["g_mla_out"], d["g_gdn"] = _mix_bwd(
        o_mla, o_gdn, proj_gate, mixed, dx2, p["g_mla_out"], p["g_gdn"], p["w_out"], w["mix_post_g"])
    d["w_out"] = _weight_grad("mix_out_dw", cat, dmixed)
    dq, dk, dv = _attn_bwd(q, k, v, do_mla, lse, delta)
    (dqkv_n, dgb, dbb), landed_ffn2 = _gdn_bwd(qkv_n, gb, bb, keep, do_gdn, carry=ffn2_grads)
    dab, d["a_log"], d["dt_bias"] = _gates_bwd(ab, p["a_log"], p["dt_bias"], dgb, dbb)
    dproj_qkv, d["conv"] = _gdn_conv_bwd(proj_qkv, p["conv"], dqkv_n)
    dproj_a, dq_p, dkv_p, g["mla_q_norm_g"], g["mla_kv_norm_g"] = _mla_front_bwd(
        proj_a, tabs, w["mla_q_norm_g"], w["mla_kv_norm_g"], p["w_uq"], p["w_kv"], dq, dk, dv, dab)
    d["w_uq"] = _weight_grad("mla_q_dw", dq_p, cqn)
    d["w_kv"] = _weight_grad("mla_kv_dw", dkv_p, ckvn)
    d["w_a"] = _weight_grad("proj_a_dw", dproj_a, hn, tm=640)
    d["w_qkv"] = _weight_grad("proj_qkv_dw", dproj_qkv, hn)
    d["w_gate"] = _weight_grad("proj_gate_dw", dgate, hn)
    dx1, g["mix_pre_g"] = _proj_bwd(x1, w["mix_pre_g"], in_weights, [dproj_a, dproj_qkv, dgate], dx2)
    g.update(_unlayout_grads(d))
    others = [t for t, _ in OTHER.values()]
    (dx, xn1, dh1, a1, dhg1, dhu1, g["ffn1_pre_g"], g["ffn1_post_g"]), landed_others = _ffn_bwd(
        "ffn1_bwd", x, h1, hg1, hu1, dx1, w["ffn1_pre_g"], w["ffn1"], 0, w["ffn1_post_g"], carry=_Scatter([_by_device(g.pop(t)) for t in others]))
    dw_down = _weight_grad("ffn1_dw_down", a1, dh1, BF16, tm=1408)
    dw_gate, (landed_down,) = _weight_grad("ffn1_dw_gate", dhg1, xn1, BF16, tm=1408, carry=_Scatter([_by_device(dw_down)]))
    dw_up, (landed_gate,) = _weight_grad("ffn1_dw_up", dhu1, xn1, BF16, tm=1408, carry=_Scatter([_by_device(dw_gate)]))
    (landed_up,) = _exchange("scatter_last", _Scatter([_by_device(dw_up)]))
    landed = dict(zip(list(FFN_NAMES) + list(OTHER),
                      [landed_gate, landed_up, landed_down] + list(landed_ffn2) + list(landed_others)))
    return loss_lanes, dx, g, landed


MESH_AXES = ("x", "y", "c")
N_LINKS = N_DEV - 1


def _place():
    return tuple(lax.axis_index(a) for a in MESH_AXES)


def _block_of(dev):
    x, y, c = dev
    return 4 * x + 2 * y + c


def _remote_copy(src, dst, sems, k, to):
    send_sems, recv_sems = sems
    return pltpu.make_async_remote_copy(src_ref=src, dst_ref=dst, send_sem=send_sems.at[k], recv_sem=recv_sems.at[k],
                                        device_id=to, device_id_type=pl.DeviceIdType.MESH)


class _Exchange:
    def __init__(self, arrays):
        self.arrays = list(arrays)
        self.n = len(self.arrays)
        self.specs = [pl.BlockSpec(memory_space=pl.ANY)] * self.n
        self.scratch = [pltpu.SemaphoreType.DMA((self.n * N_LINKS,)), pltpu.SemaphoreType.DMA((self.n * N_LINKS,)),
                        pltpu.SemaphoreType.DMA((self.n,))]

    def split(self, refs):
        n = self.n
        return refs[:n], refs[n:2 * n], (refs[2 * n], refs[2 * n + 1]), refs[2 * n + 2]


class _Gather(_Exchange):
    def out_shape(self):
        return [jax.ShapeDtypeStruct((N_DEV,) + a.shape, a.dtype) for a in self.arrays]

    def _plan(self, ins, outs, sems, local_sems):
        x, y, c = _place()
        me, sibling = (x, y, c), (x, y, 1 - c)
        chips = [(1 - x, y), (x, 1 - y), (1 - x, 1 - y)]

        def copy(a, k, block, to, mine=False):
            src = ins[a] if mine else outs[a].at[_block_of(block)]
            return _remote_copy(src, outs[a].at[_block_of(block)], sems, a * N_LINKS + k, to)

        local = [pltpu.make_async_copy(ins[a], outs[a].at[_block_of(me)], local_sems.at[a]) for a in range(self.n)]
        first = []
        for a in range(self.n):
            first.append(copy(a, 0, me, sibling, mine=True))
            first += [copy(a, 1 + j, me, (*chip, c), mine=True) for j, chip in enumerate(chips)]
        return me, sibling, chips, c, copy, local, first

    def start(self, ins, outs, sems, local_sems):
        *_, local, first = self._plan(ins, outs, sems, local_sems)
        for cp in local + first:
            cp.start()

    def finish(self, ins, outs, sems, local_sems):
        me, sibling, chips, c, copy, local, first = self._plan(ins, outs, sems, local_sems)
        passed = []
        for j, chip in enumerate(chips):
            for a in range(self.n):
                copy(a, 1 + j, (*chip, c), me).wait_recv()
                passed.append(copy(a, 4 + j, (*chip, c), sibling))
                passed[-1].start()
        for a in range(self.n):
            copy(a, 0, sibling, me).wait_recv()
            for j, chip in enumerate(chips):
                copy(a, 4 + j, (*chip, 1 - c), me).wait_recv()
        for cp in first + passed:
            cp.wait_send()
        for cp in local:
            cp.wait()


class _Scatter(_Exchange):
    def out_shape(self):
        return [jax.ShapeDtypeStruct(a.shape, a.dtype) for a in self.arrays]

    def _plan(self, ins, outs, sems, local_sems):
        x, y, c = _place()
        me = _block_of((x, y, c))

        def peer(r):
            return (1 - x if r & 4 else x, 1 - y if r & 2 else y, 1 - c if r & 1 else c)

        local = [pltpu.make_async_copy(ins[a].at[me], outs[a].at[me], local_sems.at[a]) for a in range(self.n)]
        sends = [_remote_copy(ins[a].at[_block_of(peer(r))], outs[a].at[me], sems, a * N_LINKS + r - 1, peer(r))
                 for a in range(self.n) for r in range(1, N_DEV)]
        arrivals = [_remote_copy(ins[a].at[me], outs[a].at[_block_of(peer(r))], sems, a * N_LINKS + r - 1, peer(r))
                    for a in range(self.n) for r in range(1, N_DEV)]
        return local, sends, arrivals

    def start(self, ins, outs, sems, local_sems):
        local, sends, _ = self._plan(ins, outs, sems, local_sems)
        for cp in local + sends:
            cp.start()

    def finish(self, ins, outs, sems, local_sems):
        local, sends, arrivals = self._plan(ins, outs, sems, local_sems)
        for cp in arrivals:
            cp.wait_recv()
        for cp in sends:
            cp.wait_send()
        for cp in local:
            cp.wait()


def _exchange(name, plan):
    def body(*refs):
        parts = plan.split(refs)
        plan.start(*parts)
        plan.finish(*parts)

    return pl.pallas_call(
        body, name=name,
        in_specs=plan.specs,
        out_specs=plan.specs,
        out_shape=plan.out_shape(),
        scratch_shapes=plan.scratch,
    )(*plan.arrays)


def _call_carrying(body, plan, operands, *, name, grid, in_specs, out_specs, out_shape, scratch_shapes, compiler_params):
    if plan is None:
        outs = pl.pallas_call(body, name=name, grid=grid, in_specs=in_specs, out_specs=out_specs, out_shape=out_shape,
                              scratch_shapes=scratch_shapes, compiler_params=compiler_params)(*operands)
        return outs, []
    n_i, n_o, n_s, k = len(in_specs), len(out_specs), len(scratch_shapes), plan.n

    def whole(*refs):
        cut = [n_i, n_i + k, n_i + k + n_o, n_i + 2 * k + n_o, n_i + 2 * k + n_o + n_s]
        own_in, ex_in, own_out, ex_out, own_scr, ex_scr = (refs[a:b] for a, b in zip([0] + cut, cut + [len(refs)]))
        parts = plan.split(ex_in + ex_out + ex_scr)
        first = last = True
        for axis, size in enumerate(grid):
            first = first & (pl.program_id(axis) == 0)
            last = last & (pl.program_id(axis) == size - 1)

        @pl.when(first)
        def _():
            plan.start(*parts)

        body(*own_in, *own_out, *own_scr)

        @pl.when(last)
        def _():
            plan.finish(*parts)

    outs = pl.pallas_call(
        whole, name=name, grid=grid,
        in_specs=list(in_specs) + plan.specs, out_specs=list(out_specs) + plan.specs,
        out_shape=list(out_shape) + plan.out_shape(), scratch_shapes=list(scratch_shapes) + plan.scratch,
        compiler_params=compiler_params,
    )(*operands, *plan.arrays)
    return outs[:n_o], outs[n_o:]


def _row_tile(rows, target=256):
    best = rows
    for cand in range(16, min(rows, target) + 1, 16):
        if rows % cand == 0:
            best = cand
    return best


def _sum_blocks(name, blocks):
    rows, width = blocks.shape[-2:]
    tm = _row_tile(rows)

    def body(x_ref, o_ref):
        acc = x_ref[0].astype(F32)
        for d in range(1, N_DEV):
            acc = acc + x_ref[d].astype(F32)
        o_ref[...] = acc

    return pl.pallas_call(
        body, name=name,
        grid=(rows // tm,),
        in_specs=[pl.BlockSpec((N_DEV, tm, width), lambda i: (0, i, 0))],
        out_specs=pl.BlockSpec((tm, width), lambda i: (i, 0)),
        out_shape=jax.ShapeDtypeStruct((rows, width), F32),
        compiler_params=pltpu.CompilerParams(dimension_semantics=("parallel",)),
    )(blocks)


def _all_reduce_small(name, vec):
    rows, width = vec.shape

    def body(x_ref, o_ref, all_ref, send_sems, recv_sems):
        x, y, c = _place()
        me = _block_of((x, y, c))
        all_ref[me] = x_ref[...]

        def peer(r):
            return (1 - x if r & 4 else x, 1 - y if r & 2 else y, 1 - c if r & 1 else c)

        def copy(r, block):
            return _remote_copy(x_ref, all_ref.at[block], (send_sems, recv_sems), r - 1, peer(r))

        sends = [copy(r, me) for r in range(1, N_DEV)]
        for cp in sends:
            cp.start()
        for r in range(1, N_DEV):
            copy(r, _block_of(peer(r))).wait_recv()
        for cp in sends:
            cp.wait_send()
        acc = all_ref[0]
        for d in range(1, N_DEV):
            acc = acc + all_ref[d]
        o_ref[...] = acc

    return pl.pallas_call(
        body, name=name,
        in_specs=[pl.BlockSpec(memory_space=pltpu.VMEM)],
        out_specs=pl.BlockSpec(memory_space=pltpu.VMEM),
        out_shape=jax.ShapeDtypeStruct((rows, width), F32),
        scratch_shapes=[pltpu.VMEM((N_DEV, rows, width), F32), pltpu.SemaphoreType.DMA((N_LINKS,)), pltpu.SemaphoreType.DMA((N_LINKS,))],
    )(vec)


def _adamw(name, w, g, m, v):
    def fn(rows, consts):
        wv, gv, mv, vv = rows
        m2 = ADAM_B1 * mv + (1.0 - ADAM_B1) * gv
        v2 = ADAM_B2 * vv + (1.0 - ADAM_B2) * jnp.square(gv)
        m_hat = m2 / (1.0 - ADAM_B1 ** ADAM_STEP)
        v_hat = v2 / (1.0 - ADAM_B2 ** ADAM_STEP)
        return [-ADAM_LR * (m_hat / (jnp.sqrt(v_hat) + ADAM_EPS) + ADAM_WD * wv), m2, v2], []

    return _rowwise(name, fn, [w, g, m, v], [], [(w.shape[1], F32)] * 3, tm=_row_tile(w.shape[0]))


ROW = 1024
FFN_NAMES = ("ffn1_w_gate", "ffn1_w_up", "ffn1_w_down", "ffn2_w_gate", "ffn2_w_up", "ffn2_w_down")
OTHER = {"w_in": ("w_in_t", True), "mla_w_uq": ("uq_t", True), "mla_w_ukv": ("ukv_t", True), "w_out": ("w_out", False)}
BY_COLUMNS = ("ffn1_w_gate", "ffn1_w_up", "ffn2_w_gate", "ffn2_w_up", "w_in", "mla_w_uq", "mla_w_ukv")
SMALL = {
    "ffn1_pre_g": (1024, 1024), "ffn1_post_g": (1024, 1024), "mix_pre_g": (1024, 1024), "mla_q_norm_g": (256, 256),
    "mla_kv_norm_g": (128, 128), "mla_out_g": (512, 512), "gdn_a_log": (8, 128), "gdn_dt_bias": (8, 128),
    "gdn_norm_g": (64, 128), "mix_post_g": (1024, 1024), "ffn2_pre_g": (1024, 1024), "ffn2_post_g": (1024, 1024),
}
CONV_SHAPE = (GDN_CONV, 3 * N_HEADS * GDN_D)
CONV_SHARD = (GDN_CONV, CONV_SHAPE[1] // N_DEV)
CONV_LANES = CONV_SHAPE[0] * CONV_SHAPE[1]
SMALL_ROWS = 8
REDUCE_ROWS = 16


def _pack_small(vecs, conv, rows):
    parts = [_pad_lanes(vecs[n].reshape(1, -1), 0, r) for n, (_, r) in SMALL.items()]
    parts.append(conv.reshape(1, -1))
    flat = jnp.concatenate(parts, axis=1)
    return _pad_lanes(flat, 0, rows * ROW).reshape(rows, ROW)


def _unpack_small(buf):
    flat = buf.reshape(1, -1)
    out, at = {}, 0
    for n, (w, r) in SMALL.items():
        out[n] = flat[:, at:at + w]
        at += r
    return out, flat[0, at:]


def kernel(x, positions, ffn1_pre_g, ffn1_w_gate, ffn1_w_up, ffn1_w_down, ffn1_post_g, mix_pre_g, w_in, mla_q_norm_g, mla_w_uq, mla_kv_norm_g, mla_w_ukv, mla_out_g, gdn_conv_w, gdn_a_log, gdn_dt_bias, gdn_norm_g, w_out, mix_post_g, ffn2_pre_g, ffn2_w_gate, ffn2_w_up, ffn2_w_down, ffn2_post_g, loss_target, m_ffn1_pre_g, m_ffn1_w_gate, m_ffn1_w_up, m_ffn1_w_down, m_ffn1_post_g, m_mix_pre_g, m_w_in, m_mla_q_norm_g, m_mla_w_uq, m_mla_kv_norm_g, m_mla_w_ukv, m_mla_out_g, m_gdn_conv_w, m_gdn_a_log, m_gdn_dt_bias, m_gdn_norm_g, m_w_out, m_mix_post_g, m_ffn2_pre_g, m_ffn2_w_gate, m_ffn2_w_up, m_ffn2_w_down, m_ffn2_post_g, v_ffn1_pre_g, v_ffn1_w_gate, v_ffn1_w_up, v_ffn1_w_down, v_ffn1_post_g, v_mix_pre_g, v_w_in, v_mla_q_norm_g, v_mla_w_uq, v_mla_kv_norm_g, v_mla_w_ukv, v_mla_out_g, v_gdn_conv_w, v_gdn_a_log, v_gdn_dt_bias, v_gdn_norm_g, v_w_out, v_mix_post_g, v_ffn2_pre_g, v_ffn2_w_gate, v_ffn2_w_up, v_ffn2_w_down, v_ffn2_post_g):
    given = dict(locals())
    order = ["ffn1_pre_g", "ffn1_w_gate", "ffn1_w_up", "ffn1_w_down", "ffn1_post_g", "mix_pre_g", "w_in", "mla_q_norm_g",
             "mla_w_uq", "mla_kv_norm_g", "mla_w_ukv", "mla_out_g", "gdn_conv_w", "gdn_a_log", "gdn_dt_bias", "gdn_norm_g",
             "w_out", "mix_post_g", "ffn2_pre_g", "ffn2_w_gate", "ffn2_w_up", "ffn2_w_down", "ffn2_post_g"]
    assert sorted(order) == sorted(list(FFN_NAMES) + list(OTHER) + list(SMALL) + ["gdn_conv_w"])

    def drop_depth(a):
        return a[0] if a.ndim == 3 else a

    wts = {n: drop_depth(given[n]) for n in order}
    mom = {n: drop_depth(given["m_" + n]) for n in order}
    var = {n: drop_depth(given["v_" + n]) for n in order}
    me = _block_of(_place())

    def wire(n):
        return (wts[n].T if n in BY_COLUMNS else wts[n]).astype(BF16)

    (ffn1,) = _exchange("gather_first", _Gather([jnp.stack([wire(n) for n in FFN_NAMES[:3]])]))
    mid = _Gather([wire(n) for n in ("w_in", "mla_w_uq", "mla_w_ukv")])
    late = _Gather([jnp.stack([wire(n) for n in FFN_NAMES[3:]]), wire("w_out")])
    conv_at = lax.dynamic_update_slice(jnp.zeros((N_DEV, CONV_SHARD[0] * CONV_SHARD[1]), F32),
                                       wts["gdn_conv_w"].reshape(1, -1), (me, 0))
    conv_all = _all_reduce_small("gather_conv", _pad_lanes(conv_at.reshape(1, -1), 0, SMALL_ROWS * ROW).reshape(SMALL_ROWS, ROW))
    full = {n: wts[n] for n in SMALL}
    full["ffn1"] = ffn1
    full["gdn_conv_w"] = conv_all.reshape(-1)[:CONV_LANES].reshape((N_DEV,) + CONV_SHARD).transpose(1, 0, 2).reshape(CONV_SHAPE)

    loss_lanes, dx, grads, landed = _local_step(x[0], positions[0], loss_target[0], full, mid, late)
    loss = lax.psum(jnp.sum(loss_lanes), MESH_AXES)

    sums = {n: _sum_blocks("sum_" + n, blocks) for n, blocks in landed.items()}
    grad = {n: (sums[n].T if n in BY_COLUMNS else sums[n]) for n in sums}
    small_sum = _all_reduce_small("reduce_small", _pack_small(grads, grads["gdn_conv_w"].reshape(-1), REDUCE_ROWS))
    small_grad, conv_grad_full = _unpack_small(small_sum)
    grad.update(small_grad)
    grad["gdn_conv_w"] = lax.dynamic_slice(conv_grad_full[:CONV_LANES].reshape(CONV_SHAPE), (0, me * CONV_SHARD[1]), CONV_SHARD)

    outs = {"grad": grad, "delta": {}, "new_m": {}, "new_v": {}}
    for n in list(FFN_NAMES) + list(OTHER):
        outs["delta"][n], outs["new_m"][n], outs["new_v"][n] = _adamw("adamw_" + n, wts[n], grad[n], mom[n], var[n])
    small = [_pack_small(s, s["gdn_conv_w"].reshape(-1), SMALL_ROWS) for s in (wts, grad, mom, var)]
    for kind, s in zip(("delta", "new_m", "new_v"), _adamw("adamw_small", *small)):
        vecs, conv = _unpack_small(s)
        outs[kind].update(vecs)
        outs[kind]["gdn_conv_w"] = conv[:CONV_SHARD[0] * CONV_SHARD[1]].reshape(CONV_SHARD)
    result = [loss, dx[None]]
    for kind in ("grad", "delta", "new_m", "new_v"):
        result += [outs[kind][n].reshape(given[n].shape) for n in order]
    return tuple(result)
```

```python
import jax
import jax.numpy as jnp
from jax import lax
from jax.experimental import pallas as pl
from jax.experimental.pallas import tpu as pltpu

F32 = jnp.float32
BF16 = jnp.bfloat16
HI = lax.Precision.HIGH

N_DEV = 8
D_MODEL = 1024
D_FF = 2816
N_HEADS = 8
SLOT = 128
MLA_Q_RANK = 256
MLA_KV_RANK = 128
MLA_NOPE = 64
MLA_ROPE = 32
MLA_V = 64
GDN_D = 64
GDN_CONV = 4
GDN_CHUNK = 64
ROPE_THETA = 10000.0
EPS = 1e-6
ADAM_LR, ADAM_B1, ADAM_B2, ADAM_EPS, ADAM_WD, ADAM_STEP = 0.001, 0.9, 0.999, 1e-08, 0.01, 10


def _dot(a, b, ca, cb, precision=None):
    lead = a.ndim - 2
    batch = tuple(range(lead))
    return lax.dot_general(a, b, (((lead + ca,), (lead + cb,)), (batch, batch)), precision=precision,
                           preferred_element_type=F32)


def _nn(a, b, precision=None):
    return _dot(a, b, 1, 0, precision)


def _nt(a, b, precision=None):
    return _dot(a, b, 1, 1, precision)


def _tn(a, b, precision=None):
    return _dot(a, b, 0, 0, precision)


def _sigmoid(x):
    return 1.0 / (1.0 + jnp.exp(-x))


def _silu(x):
    return x * _sigmoid(x)


def _rms(x, g, n):
    ms = jnp.sum(x * x, axis=-1, keepdims=True) * (1.0 / n)
    return x * lax.rsqrt(ms + EPS) * g


def _chunk_masks():
    c = GDN_CHUNK
    i = lax.broadcasted_iota(jnp.int32, (c, c), 0)
    j = lax.broadcasted_iota(jnp.int32, (c, c), 1)
    lower = i >= j
    strict = i > j
    eye = (i == j).astype(F32)
    blocks = []
    b = 1
    while b < c:
        same = (i // (2 * b)) == (j // (2 * b))
        blocks.append(same & ((i % (2 * b)) >= b) & ((j % (2 * b)) < b))
        b *= 2
    return lower, strict, eye, blocks


def _unit_lower_inverse(low, eye, blocks):
    t = eye - jnp.where(blocks[0], low, 0.0)
    for m in blocks[1:]:
        lo = jnp.where(m, low, 0.0)
        t = t - _nn(t, _nn(lo, t, HI), HI)
    return t


@jax.custom_vjp
def _known_inverse(low, tinv):
    return tinv


def _known_inverse_fwd(low, tinv):
    return tinv, tinv


def _known_inverse_bwd(tinv, dt):
    return -_tn(tinv, _nt(dt, tinv, HI), HI), jnp.zeros_like(tinv)


_known_inverse.defvjp(_known_inverse_fwd, _known_inverse_bwd)


def _gdn_chunk(q, k, v, gc, bb, s, masks, tinv=None):
    lower, strict, eye, blocks = masks
    qs = q * (GDN_D ** -0.5)
    gct = jnp.swapaxes(gc, -1, -2)
    decay = jnp.exp(jnp.where(lower, gc - gct, -1e30))
    kb = k * bb
    low = jnp.where(strict, _nt(kb, k, HI) * decay, 0.0)
    tinv = _unit_lower_inverse(low, eye, blocks) if tinv is None else _known_inverse(low, tinv)
    eg = jnp.exp(gc)
    w = _nn(tinv, kb * eg, HI)
    u = _nn(tinv, v * bb, HI)
    attn = _nt(qs, k, HI) * decay
    last = lax.broadcasted_iota(jnp.int32, gc.shape[-2:], 0) == GDN_CHUNK - 1
    g_end = jnp.sum(jnp.where(last, gc, 0.0), axis=-2, keepdims=True)
    k_dec = k * jnp.exp(g_end - gc)
    v_new = u - _nn(w, s, HI)
    o = _nn(qs * eg, s, HI) + _nn(attn, v_new, HI)
    s_new = s * jnp.exp(g_end) + _tn(k_dec, v_new, HI)
    return o, s_new, tinv


GDN_GROUP = 8
GDN_GROUPS = N_HEADS // GDN_GROUP


def _group_heads(ref):
    return jnp.stack([ref[:, pl.ds(j * SLOT, GDN_D)] for j in range(GDN_GROUP)])


def _ungroup_heads(ref, val):
    pad = jnp.zeros((GDN_CHUNK, SLOT - GDN_D), F32)
    for j in range(GDN_GROUP):
        ref[:, pl.ds(j * SLOT, GDN_D)] = val[j]
        ref[:, pl.ds(j * SLOT + GDN_D, SLOT - GDN_D)] = pad


def _gdn_fwd(qkv, gb, bb, carry=None):
    t = qkv.shape[0]
    n_chunks = t // GDN_CHUNK
    d = GDN_D

    def body(q_ref, k_ref, v_ref, g_ref, b_ref, o_ref, keep_ref, s_ref):
        @pl.when(pl.program_id(1) == 0)
        def _():
            s_ref[...] = jnp.zeros_like(s_ref)

        s = s_ref[...]
        keep_ref[:, 0, 0] = s
        o, s_new, tinv = _gdn_chunk(*[_group_heads(r) for r in (q_ref, k_ref, v_ref, g_ref, b_ref)], s, _chunk_masks())
        keep_ref[:, 0, 1] = tinv
        s_ref[...] = s_new
        _ungroup_heads(o_ref, o)

    def spec(kind=0):
        return pl.BlockSpec((GDN_CHUNK, GDN_GROUP * SLOT), lambda h, n: (n, kind * GDN_GROUPS + h))

    return _call_carrying(
        body, carry, (qkv, qkv, qkv, gb, bb), name="gdn_fwd",
        grid=(GDN_GROUPS, n_chunks),
        in_specs=[spec(0), spec(1), spec(2), spec(), spec()],
        out_specs=[spec(), pl.BlockSpec((GDN_GROUP, 1, 2, d, d), lambda h, n: (h, n, 0, 0, 0))],
        out_shape=[jax.ShapeDtypeStruct((t, N_HEADS * SLOT), F32), jax.ShapeDtypeStruct((N_HEADS, n_chunks, 2, d, d), F32)],
        scratch_shapes=[pltpu.VMEM((GDN_GROUP, d, d), F32)],
        compiler_params=pltpu.CompilerParams(dimension_semantics=("arbitrary", "arbitrary")),
    )


def _gdn_bwd(qkv, gb, bb, keep, do, carry=None):
    t = qkv.shape[0]
    n_chunks = t // GDN_CHUNK
    d = GDN_D

    def body(q_ref, k_ref, v_ref, g_ref, b_ref, keep_ref, do_ref, dqkv_ref, dg_ref, db_ref, ds_ref):
        @pl.when(pl.program_id(1) == 0)
        def _():
            ds_ref[...] = jnp.zeros_like(ds_ref)

        masks = _chunk_masks()
        tinv = keep_ref[:, 0, 1]
        _, pull = jax.vjp(lambda *a: _gdn_chunk(*a, masks, tinv)[:2],
                          *[_group_heads(r) for r in (q_ref, k_ref, v_ref, g_ref, b_ref)], keep_ref[:, 0, 0])
        dq, dk, dv, dg, db, ds = pull((_group_heads(do_ref), ds_ref[...]))
        ds_ref[...] = ds
        for i, val in enumerate((dq, dk, dv)):
            _ungroup_heads(dqkv_ref.at[i], val)
        _ungroup_heads(dg_ref, dg)
        _ungroup_heads(db_ref, db)

    def spec(kind=0):
        return pl.BlockSpec((GDN_CHUNK, GDN_GROUP * SLOT), lambda h, n: (n_chunks - 1 - n, kind * GDN_GROUPS + h))

    return _call_carrying(
        body, carry, (qkv, qkv, qkv, gb, bb, keep, do), name="gdn_bwd",
        grid=(GDN_GROUPS, n_chunks),
        in_specs=[spec(0), spec(1), spec(2), spec(), spec(),
                  pl.BlockSpec((GDN_GROUP, 1, 2, d, d), lambda h, n: (h, n_chunks - 1 - n, 0, 0, 0)), spec()],
        out_specs=[pl.BlockSpec((3, GDN_CHUNK, GDN_GROUP * SLOT), lambda h, n: (0, n_chunks - 1 - n, h)), spec(), spec()],
        out_shape=[jax.ShapeDtypeStruct((3, t, N_HEADS * SLOT), F32)] + [jax.ShapeDtypeStruct((t, N_HEADS * SLOT), F32)] * 2,
        scratch_shapes=[pltpu.VMEM((GDN_GROUP, d, d), F32)],
        compiler_params=pltpu.CompilerParams(dimension_semantics=("arbitrary", "arbitrary")),
    )


def _rowwise(name, fn, rows, consts, outs, sums=(), tm=512):
    rows = [x if isinstance(x, tuple) else (x, x.shape[1], 0) for x in rows]
    t = rows[0][0].shape[0]
    tm = min(tm, t)
    steps = t // tm
    n_r, n_c, n_o, n_s = len(rows), len(consts), len(outs), len(sums)

    def window(width, block):
        return pl.BlockSpec((tm, width), lambda i: (i, block))

    def body(*refs):
        r, c = refs[:n_r], refs[n_r:n_r + n_c]
        o, s = refs[n_r + n_c:n_r + n_c + n_o], refs[n_r + n_c + n_o:]
        vals, tot = fn([x[...] for x in r], [x[...] for x in c])
        for ref, val in zip(o, vals):
            ref[...] = val.astype(ref.dtype)
        if n_s:
            @pl.when(pl.program_id(0) == 0)
            def _():
                for ref in s:
                    ref[...] = jnp.zeros_like(ref)

            for ref, val in zip(s, tot):
                ref[...] += val

    return pl.pallas_call(
        body, name=name,
        grid=(steps,),
        in_specs=[window(w, b) for _, w, b in rows] + [pl.BlockSpec(x.shape, lambda i: (0, 0)) for x in consts],
        out_specs=[pl.BlockSpec((tm, w), lambda i: (i, 0)) for w, _ in outs]
        + [pl.BlockSpec((1, w), lambda i: (0, 0)) for w in sums],
        out_shape=[jax.ShapeDtypeStruct((t, w), dt) for w, dt in outs]
        + [jax.ShapeDtypeStruct((1, w), F32) for w in sums],
        compiler_params=pltpu.CompilerParams(dimension_semantics=("arbitrary",)),
    )(*[x for x, _, _ in rows], *consts)


def _tile(dim, target):
    if dim <= target:
        return dim
    best = None
    for cand in range(128, target + 1, 128):
        if dim % cand == 0:
            best = cand
    assert best is not None, (dim, target)
    return best


def _matmul(name, a, b, mode, out_dtype=F32, tm=1024, tn=1024, tk=2048, carry=None):
    if mode == "nn":
        (m, k), n = a.shape, b.shape[1]
    elif mode == "nt":
        (m, k), n = a.shape, b.shape[0]
    else:
        (k, m), n = a.shape, b.shape[1]
    tm, tn, tk = _tile(m, tm), _tile(n, tn), _tile(k, tk)
    k_steps = k // tk
    product = {"nn": _nn, "nt": _nt, "tn": _tn}[mode]

    def body(a_ref, b_ref, o_ref, acc_ref):
        part = product(a_ref[...].astype(BF16), b_ref[...].astype(BF16))
        if k_steps == 1:
            o_ref[...] = part.astype(o_ref.dtype)
        else:
            kk = pl.program_id(2)

            @pl.when(kk == 0)
            def _():
                acc_ref[...] = part

            @pl.when(kk > 0)
            def _():
                acc_ref[...] += part

            @pl.when(kk == k_steps - 1)
            def _():
                o_ref[...] = acc_ref[...].astype(o_ref.dtype)

    a_spec = pl.BlockSpec((tk, tm), lambda i, j, kk: (kk, i)) if mode == "tn" else pl.BlockSpec((tm, tk), lambda i, j, kk: (i, kk))
    b_spec = pl.BlockSpec((tn, tk), lambda i, j, kk: (j, kk)) if mode == "nt" else pl.BlockSpec((tk, tn), lambda i, j, kk: (kk, j))
    (out,), carried = _call_carrying(
        body, carry, (a, b), name=name,
        grid=(m // tm, n // tn, k_steps),
        in_specs=[a_spec, b_spec],
        out_specs=[pl.BlockSpec((tm, tn), lambda i, j, kk: (i, j))],
        out_shape=[jax.ShapeDtypeStruct((m, n), out_dtype)],
        scratch_shapes=[pltpu.VMEM((tm, tn) if k_steps > 1 else (8, 128), F32)],
        compiler_params=pltpu.CompilerParams(dimension_semantics=("arbitrary", "arbitrary", "arbitrary")),
    )
    return out if carry is None else (out, carried)


FFN_TM = 512
FFN_BWD_TM = 256
FFN_BLOCKS = 4
FFN_GATE, FFN_UP, FFN_DOWN = 0, 1, 2


def _ffn_weight_specs(ffn_w, first):
    _, _, rows, dm = ffn_w.shape

    def spec(k):
        return pl.BlockSpec((FFN_BLOCKS, None, rows, dm), lambda i, j: (j, first + k, 0, 0))

    return [spec(FFN_GATE), spec(FFN_UP), spec(FFN_DOWN)], FFN_BLOCKS * rows


def _ffn_fwd(name, x, g_pre, ffn_w, first, g_post, carry=None):
    t, dm = x.shape
    tm = min(FFN_TM, t)
    w_specs, tf = _ffn_weight_specs(ffn_w, first)
    f_steps = N_DEV // FFN_BLOCKS

    def body(x_ref, gpre_ref, wg_ref, wu_ref, wd_ref, gpost_ref, h_ref, y_ref, hg_ref, hu_ref, xn_ref, acc_ref):
        j = pl.program_id(1)

        @pl.when(j == 0)
        def _():
            xn_ref[...] = _rms(x_ref[...], gpre_ref[...], dm).astype(BF16)
            acc_ref[...] = jnp.zeros_like(acc_ref)

        xn = xn_ref[...]
        wg, wu, wd = (r[...].reshape(tf, dm) for r in (wg_ref, wu_ref, wd_ref))
        hg, hu = _nt(xn, wg), _nt(xn, wu)
        hg_ref[...] = hg.astype(BF16)
        hu_ref[...] = hu.astype(BF16)
        a = _silu(hg) * hu
        acc_ref[...] += _nn(a.astype(BF16), wd)

        @pl.when(j == f_steps - 1)
        def _():
            h = acc_ref[...]
            h_ref[...] = h
            y_ref[...] = x_ref[...] + 0.5 * _rms(h, gpost_ref[...], dm)

    row = pl.BlockSpec((tm, dm), lambda i, j: (i, 0))
    vec = pl.BlockSpec((1, dm), lambda i, j: (0, 0))
    wide = pl.BlockSpec((tm, tf), lambda i, j: (i, j))
    return _call_carrying(
        body, carry, (x, g_pre, ffn_w, ffn_w, ffn_w, g_post), name=name,
        grid=(t // tm, f_steps),
        in_specs=[row, vec, *w_specs, vec],
        out_specs=[row, row, wide, wide],
        out_shape=[jax.ShapeDtypeStruct((t, dm), F32)] * 2 + [jax.ShapeDtypeStruct((t, f_steps * tf), BF16)] * 2,
        scratch_shapes=[pltpu.VMEM((tm, dm), BF16), pltpu.VMEM((tm, dm), F32)],
        compiler_params=pltpu.CompilerParams(dimension_semantics=("arbitrary", "arbitrary")),
    )


def _ffn_bwd(name, x, h, hg, hu, dy, g_pre, ffn_w, first, g_post, carry=None):
    t, dm = x.shape
    tm = min(FFN_BWD_TM, t)
    w_specs, tf = _ffn_weight_specs(ffn_w, first)
    f_steps = N_DEV // FFN_BLOCKS
    f = f_steps * tf

    def post(hv, g):
        return 0.5 * _rms(hv, g, dm)

    def pre(xv, g):
        return _rms(xv, g, dm)

    def body(x_ref, h_ref, dy_ref, hg_ref, hu_ref, gpre_ref, wg_ref, wu_ref, wd_ref, gpost_ref,
             dx_ref, xn_ref, dh_ref, a_ref, dhg_ref, dhu_ref, dgpre_ref, dgpost_ref, acc_ref):
        i, j = pl.program_id(0), pl.program_id(1)

        @pl.when((i == 0) & (j == 0))
        def _():
            dgpre_ref[...] = jnp.zeros_like(dgpre_ref)
            dgpost_ref[...] = jnp.zeros_like(dgpost_ref)

        @pl.when(j == 0)
        def _():
            xn_ref[...] = pre(x_ref[...], gpre_ref[...]).astype(BF16)
            _, pull = jax.vjp(post, h_ref[...], gpost_ref[...])
            dh, dg = pull(dy_ref[...])
            dh_ref[...] = dh.astype(BF16)
            dgpost_ref[...] += dg
            acc_ref[...] = jnp.zeros_like(acc_ref)

        wg, wu, wd = (r[...].reshape(tf, dm) for r in (wg_ref, wu_ref, wd_ref))
        hg, hu = hg_ref[...].astype(F32), hu_ref[...].astype(F32)
        da = _nt(dh_ref[...], wd)
        sig = _sigmoid(hg)
        act = hg * sig
        dhu = (da * act).astype(BF16)
        dhg = (da * hu * (sig * (1.0 + hg * (1.0 - sig)))).astype(BF16)
        a_ref[...] = (act * hu).astype(BF16)
        dhg_ref[...] = dhg
        dhu_ref[...] = dhu
        acc_ref[...] += _nn(dhg, wg) + _nn(dhu, wu)

        @pl.when(j == f_steps - 1)
        def _():
            _, pull = jax.vjp(pre, x_ref[...], gpre_ref[...])
            dx, dg = pull(acc_ref[...])
            dx_ref[...] = dy_ref[...] + dx
            dgpre_ref[...] += dg

    row = pl.BlockSpec((tm, dm), lambda i, j: (i, 0))
    vec = pl.BlockSpec((1, dm), lambda i, j: (0, 0))
    wide = pl.BlockSpec((tm, tf), lambda i, j: (i, j))
    return _call_carrying(
        body, carry, (x, h, dy, hg, hu, g_pre, ffn_w, ffn_w, ffn_w, g_post), name=name,
        grid=(t // tm, f_steps),
        in_specs=[row, row, row, wide, wide, vec, *w_specs, vec],
        out_specs=[row, row, row, wide, wide, wide, vec, vec],
        out_shape=[jax.ShapeDtypeStruct((t, dm), F32), jax.ShapeDtypeStruct((t, dm), BF16), jax.ShapeDtypeStruct((t, dm), BF16),
                   jax.ShapeDtypeStruct((t, f), BF16), jax.ShapeDtypeStruct((t, f), BF16), jax.ShapeDtypeStruct((t, f), BF16),
                   jax.ShapeDtypeStruct((1, dm), F32), jax.ShapeDtypeStruct((1, dm), F32)],
        scratch_shapes=[pltpu.VMEM((tm, dm), F32)],
        compiler_params=pltpu.CompilerParams(dimension_semantics=("arbitrary", "arbitrary")),
    )


ATT_T = 512
ATT_GROUP = 4
ATT_GROUP_FWD = 8
ATT_SCALE = (MLA_NOPE + MLA_ROPE) ** -0.5


def _stack_slots(ref, group):
    return jnp.stack([ref[:, pl.ds(j * SLOT, SLOT)] for j in range(group)])


def _unstack_slots(ref, val):
    for j in range(val.shape[0]):
        ref[:, pl.ds(j * SLOT, SLOT)] = val[j].astype(ref.dtype)


def _scores(q, k, diagonal):
    s = _nt(q, k) * ATT_SCALE
    if diagonal:
        row = lax.broadcasted_iota(jnp.int32, s.shape[1:], 0)
        col = lax.broadcasted_iota(jnp.int32, s.shape[1:], 1)
        s = jnp.where(col <= row, s, -1e30)
    return s


def _attn_pairs(steps, q_major):
    pairs = ([(qi, ki) for qi in range(steps) for ki in range(qi + 1)] if q_major
             else [(qi, ki) for ki in range(steps) for qi in range(ki, steps)])
    return jnp.array([p[0] for p in pairs], jnp.int32), jnp.array([p[1] for p in pairs], jnp.int32)


def _attn_specs(tile, group):
    width = group * SLOT
    return (pl.BlockSpec((tile, width), lambda h, p, qt, kt: (qt[p], h)),
            pl.BlockSpec((tile, width), lambda h, p, qt, kt: (kt[p], h)))


def _attn_fwd(q, k, v):
    t = q.shape[0]
    tile = min(ATT_T, t)
    steps = t // tile
    g = ATT_GROUP_FWD

    strip = min(SLOT, tile)

    def body(qt_ref, kt_ref, q_ref, k_ref, v_ref, o_ref, lse_ref, m_ref, l_ref, alpha_ref, acc_ref, s_ref, p_ref):
        qi, ki = qt_ref[pl.program_id(1)], kt_ref[pl.program_id(1)]

        @pl.when(ki == 0)
        def _():
            m_ref[...] = jnp.full_like(m_ref, -1e30)
            l_ref[...] = jnp.zeros_like(l_ref)
            acc_ref[...] = jnp.zeros_like(acc_ref)

        def step(diagonal):
            s_ref[...] = _nt(_stack_slots(k_ref, g), _stack_slots(q_ref, g))
            for j in range(tile // strip):
                c = pl.ds(j * strip, strip)
                s = s_ref[:, :, c] * ATT_SCALE
                if diagonal:
                    key = lax.broadcasted_iota(jnp.int32, s.shape[1:], 0)
                    query = lax.broadcasted_iota(jnp.int32, s.shape[1:], 1) + j * strip
                    s = jnp.where(key <= query, s, -1e30)
                m_old = m_ref[:, :, c]
                m_new = jnp.maximum(m_old, jnp.max(s, axis=1, keepdims=True))
                p = jnp.exp(s - m_new)
                alpha = jnp.exp(m_old - m_new)
                l_ref[:, :, c] = alpha * l_ref[:, :, c] + jnp.sum(p, axis=1, keepdims=True)
                alpha_ref[:, :, c] = alpha
                m_ref[:, :, c] = m_new
                p_ref[:, :, c] = p.astype(BF16)
            acc_ref[...] = acc_ref[...] * alpha_ref[...] + _tn(_stack_slots(v_ref, g), p_ref[...])

        @pl.when(ki < qi)
        def _():
            step(False)

        @pl.when(ki == qi)
        def _():
            step(True)
            out = acc_ref[...] / l_ref[...]
            lse = jnp.broadcast_to(m_ref[...] + jnp.log(l_ref[...]), out.shape)
            for j in range(g):
                o_ref[:, pl.ds(j * SLOT, SLOT)] = out[j].T
                lse_ref[:, pl.ds(j * SLOT, SLOT)] = lse[j].T

    q_spec, k_spec = _attn_specs(tile, g)
    tables = _attn_pairs(steps, True)
    return pl.pallas_call(
        body, name="attn_fwd",
        grid_spec=pltpu.PrefetchScalarGridSpec(
            num_scalar_prefetch=2, grid=(N_HEADS // g, tables[0].shape[0]),
            in_specs=[q_spec, k_spec, k_spec], out_specs=[q_spec, q_spec],
            scratch_shapes=[pltpu.VMEM((g, 1, tile), F32), pltpu.VMEM((g, 1, tile), F32), pltpu.VMEM((g, 1, tile), F32),
                            pltpu.VMEM((g, SLOT, tile), F32), pltpu.VMEM((g, tile, tile), F32), pltpu.VMEM((g, tile, tile), BF16)]),
        out_shape=[jax.ShapeDtypeStruct((t, N_HEADS * SLOT), F32)] * 2,
        compiler_params=pltpu.CompilerParams(dimension_semantics=("parallel", "arbitrary")),
    )(*tables, q, k, v)


def _attn_grad_scores(q, k, v, do, lse_ref, delta_ref, diagonal):
    g = ATT_GROUP
    p = jnp.exp(_scores(q, k, diagonal) - _stack_slots(lse_ref, g)[:, :, 0:1])
    dp = _nt(do, v)
    return p, p * (dp - _stack_slots(delta_ref, g)[:, :, 0:1]) * ATT_SCALE


def _attn_bwd(q, k, v, do, lse, delta):
    t = q.shape[0]
    tile = min(ATT_T, t)
    steps = t // tile
    g = ATT_GROUP

    def body(qt_ref, kt_ref, q_ref, k_ref, v_ref, do_ref, lse_ref, delta_ref, dq_ref, dk_ref, dv_ref, dk_acc, dv_acc):
        qi, ki = qt_ref[pl.program_id(1)], kt_ref[pl.program_id(1)]

        @pl.when(pl.program_id(1) == 0)
        def _():
            dq_ref[...] = jnp.zeros_like(dq_ref)

        def step(diagonal):
            qq, kk = _stack_slots(q_ref, g), _stack_slots(k_ref, g)
            do_b = _stack_slots(do_ref, g).astype(BF16)
            p, ds = _attn_grad_scores(qq, kk, _stack_slots(v_ref, g), do_b, lse_ref, delta_ref, diagonal)
            ds = ds.astype(BF16)
            dv_acc[...] += _tn(p.astype(BF16), do_b)
            dk_acc[...] += _tn(ds, qq)
            dq = _nn(ds, kk)
            rows = pl.ds(pl.multiple_of(qi * tile, tile), tile)
            for j in range(g):
                dq_ref[rows, pl.ds(j * SLOT, SLOT)] += dq[j]

        @pl.when(qi == ki)
        def _():
            dk_acc[...] = jnp.zeros_like(dk_acc)
            dv_acc[...] = jnp.zeros_like(dv_acc)
            step(True)

        @pl.when(qi > ki)
        def _():
            step(False)

        @pl.when(qi == steps - 1)
        def _():
            _unstack_slots(dk_ref, dk_acc[...])
            _unstack_slots(dv_ref, dv_acc[...])

    q_spec, k_spec = _attn_specs(tile, g)
    tables = _attn_pairs(steps, False)
    return pl.pallas_call(
        body, name="attn_bwd",
        grid_spec=pltpu.PrefetchScalarGridSpec(
            num_scalar_prefetch=2, grid=(N_HEADS // g, tables[0].shape[0]),
            in_specs=[q_spec, k_spec, k_spec, q_spec, q_spec, q_spec],
            out_specs=[pl.BlockSpec((t, g * SLOT), lambda h, p, qt, kt: (0, h)), k_spec, k_spec],
            scratch_shapes=[pltpu.VMEM((g, tile, SLOT), F32), pltpu.VMEM((g, tile, SLOT), F32)]),
        out_shape=[jax.ShapeDtypeStruct((t, N_HEADS * SLOT), F32)] * 3,
        compiler_params=pltpu.CompilerParams(dimension_semantics=("parallel", "arbitrary")),
    )(*tables, q, k, v, do, lse, delta)


CONV_PAD = 8


def _fill_padded(ref, val):
    t = val.shape[0]
    zeros = jnp.zeros((CONV_PAD, val.shape[1]), val.dtype)
    ref[pl.ds(0, CONV_PAD)] = zeros
    ref[pl.ds(CONV_PAD + t, CONV_PAD)] = zeros
    ref[pl.ds(CONV_PAD, t)] = val


def _shifted(ref, s):
    return ref[pl.ds(CONV_PAD - s, ref.shape[0] - 2 * CONV_PAD)]


def _l2norm(x):
    return x * lax.rsqrt(jnp.sum(x * x, axis=-1, keepdims=True) + EPS)


def _conv_pre(x_pad, w):
    y = w[GDN_CONV - 1:GDN_CONV, :] * _shifted(x_pad, 0)
    for s in range(1, GDN_CONV):
        y = y + w[GDN_CONV - 1 - s:GDN_CONV - s, :] * _shifted(x_pad, s)
    return y


def _gdn_conv_fwd(x, w):
    t, width = x.shape

    def body(x_ref, w_ref, o_ref, x_pad):
        _fill_padded(x_pad, x_ref[...])
        act = _silu(_conv_pre(x_pad, w_ref[...]))
        normed = pl.program_id(0) < 2 * N_HEADS
        o_ref[...] = jnp.where(normed, _l2norm(act), act)

    return pl.pallas_call(
        body, name="gdn_conv_fwd",
        grid=(width // SLOT,),
        in_specs=[pl.BlockSpec((t, SLOT), lambda j: (0, j)), pl.BlockSpec((GDN_CONV, SLOT), lambda j: (0, j))],
        out_specs=pl.BlockSpec((t, SLOT), lambda j: (0, j)),
        out_shape=jax.ShapeDtypeStruct((t, width), F32),
        scratch_shapes=[pltpu.VMEM((t + 2 * CONV_PAD, SLOT), F32)],
        compiler_params=pltpu.CompilerParams(dimension_semantics=("parallel",)),
    )(x, w)


def _gdn_conv_bwd(x, w, dout):
    t, width = x.shape

    def body(x_ref, w_ref, do_ref, dx_ref, dw_ref, x_pad, dy_pad):
        wv = w_ref[...]
        _fill_padded(x_pad, x_ref[...])
        y = _conv_pre(x_pad, wv)
        sig = _sigmoid(y)
        act = y * sig
        _, pull = jax.vjp(_l2norm, act)
        normed = pl.program_id(0) < 2 * N_HEADS
        dact = jnp.where(normed, pull(do_ref[0])[0], do_ref[0])
        dy = dact * (sig * (1.0 + y * (1.0 - sig)))
        _fill_padded(dy_pad, dy)
        dx = wv[GDN_CONV - 1:GDN_CONV, :] * dy
        for s in range(1, GDN_CONV):
            dx = dx + wv[GDN_CONV - 1 - s:GDN_CONV - s, :] * _shifted(dy_pad, -s)
        dx_ref[...] = dx.astype(BF16)
        for s in range(GDN_CONV):
            dw_ref[GDN_CONV - 1 - s:GDN_CONV - s, :] = jnp.sum(dy * _shifted(x_pad, s), axis=0, keepdims=True)

    col = pl.BlockSpec((t, SLOT), lambda j: (0, j))
    tap = pl.BlockSpec((GDN_CONV, SLOT), lambda j: (0, j))
    return pl.pallas_call(
        body, name="gdn_conv_bwd",
        grid=(width // SLOT,),
        in_specs=[col, tap, pl.BlockSpec((1, t, SLOT), lambda j: (j // N_HEADS, 0, j % N_HEADS))],
        out_specs=[col, tap],
        out_shape=[jax.ShapeDtypeStruct((t, width), BF16), jax.ShapeDtypeStruct((GDN_CONV, width), F32)],
        scratch_shapes=[pltpu.VMEM((t + 2 * CONV_PAD, SLOT), F32)] * 2,
        compiler_params=pltpu.CompilerParams(dimension_semantics=("parallel",)),
    )(x, w, dout)


def _softplus(x):
    e = jnp.exp(-jnp.abs(x))
    u = 1.0 + e
    log1p = jnp.where(u == 1.0, e, jnp.log(u) * e / jnp.where(u == 1.0, 1.0, u - 1.0))
    return jnp.maximum(x, 0.0) + log1p


def _chunk_running_sum(x, reverse=False):
    tm = x.shape[0]
    at = lax.broadcasted_iota(jnp.int32, x.shape, 0) % GDN_CHUNK
    step = 1
    while step < GDN_CHUNK:
        if reverse:
            x = x + jnp.where(at < GDN_CHUNK - step, pltpu.roll(x, tm - step, 0), 0.0)
        else:
            x = x + jnp.where(at >= step, pltpu.roll(x, step, 0), 0.0)
        step *= 2
    return x


def _gates_fwd(ab, a_log, dt_bias):
    def fn(rows, consts):
        (abv,), (alog, dtb) = rows, consts
        g = _chunk_running_sum(-jnp.exp(alog) * _softplus(abv + dtb))
        beta = _sigmoid(abv)
        shape = (abv.shape[0], SLOT)
        g_slots = [jnp.broadcast_to(g[:, h:h + 1], shape) for h in range(N_HEADS)]
        b_slots = [jnp.broadcast_to(beta[:, N_HEADS + h:N_HEADS + h + 1], shape) for h in range(N_HEADS)]
        return [jnp.concatenate(g_slots, axis=1), jnp.concatenate(b_slots, axis=1)], []

    width = N_HEADS * SLOT
    return _rowwise("gdn_gates_fwd", fn, [ab], [a_log, dt_bias], [(width, F32), (width, F32)])


def _gates_bwd(ab, a_log, dt_bias, dg, dbeta):
    def fn(rows, consts):
        (abv, dgv, dbv), (alog, dtb) = rows, consts
        lane = lax.broadcasted_iota(jnp.int32, abv.shape, 1)
        dg_tok = jnp.zeros_like(abv)
        db_tok = jnp.zeros_like(abv)
        for h in range(N_HEADS):
            dg_tok = dg_tok + jnp.where(lane == h, jnp.sum(dgv[:, h * SLOT:(h + 1) * SLOT], axis=1, keepdims=True), 0.0)
            db_tok = db_tok + jnp.where(lane == N_HEADS + h, jnp.sum(dbv[:, h * SLOT:(h + 1) * SLOT], axis=1, keepdims=True), 0.0)
        dg_tok = _chunk_running_sum(dg_tok, reverse=True)
        xa = abv + dtb
        g = -jnp.exp(alog) * _softplus(xa)
        da = dg_tok * (-jnp.exp(alog)) * _sigmoid(xa)
        beta = _sigmoid(abv)
        dab = jnp.where(lane < N_HEADS, da, db_tok * beta * (1.0 - beta))
        dab = jnp.where(lane < 2 * N_HEADS, dab, 0.0)
        d_alog = jnp.sum(jnp.where(lane < N_HEADS, dg_tok * g, 0.0), axis=0, keepdims=True)
        d_dtb = jnp.sum(jnp.where(lane < N_HEADS, da, 0.0), axis=0, keepdims=True)
        return [dab], [d_alog, d_dtb]

    return _rowwise("gdn_gates_bwd", fn, [ab, dg, dbeta], [a_log, dt_bias], [(SLOT, F32)], sums=[SLOT, SLOT])


ROPE_HALF = MLA_ROPE // 2


def _rope_tables(positions):
    freqs = ROPE_THETA ** (-jnp.arange(ROPE_HALF, dtype=F32) / ROPE_HALF)
    ang = positions.astype(F32)[:, None] * freqs
    cos, sin = jnp.cos(ang), jnp.sin(ang)
    t = positions.shape[0]
    ones, zeros = jnp.ones((t, MLA_NOPE), F32), jnp.zeros((t, MLA_NOPE), F32)
    tail = jnp.zeros((t, SLOT - MLA_NOPE - MLA_ROPE), F32)
    half0 = jnp.zeros((t, ROPE_HALF), F32)
    same = jnp.concatenate([ones, cos, cos, tail], axis=1)
    from_low = jnp.concatenate([zeros, half0, sin, tail], axis=1)
    from_high = jnp.concatenate([zeros, -sin, half0, tail], axis=1)
    return same, from_low, from_high


def _rope(x, tabs):
    same, from_low, from_high = tabs
    width = x.shape[1]
    return x * same + pltpu.roll(x, ROPE_HALF, 1) * from_low + pltpu.roll(x, width - ROPE_HALF, 1) * from_high


def _rope_transposed(dy, tabs):
    same, from_low, from_high = tabs
    width = dy.shape[1]
    return dy * same + pltpu.roll(dy * from_low, width - ROPE_HALF, 1) + pltpu.roll(dy * from_high, ROPE_HALF, 1)


def _tile_slots(tab):
    return jnp.concatenate([tab] * N_HEADS, axis=1)


A_WIDTH = MLA_Q_RANK + MLA_KV_RANK + 2 * SLOT
A_KPE = MLA_Q_RANK + MLA_KV_RANK
A_AB = A_KPE + SLOT
WIDE = N_HEADS * SLOT


def _mla_front_fwd(proj_a, tabs, g_q, g_kv, w_uq, w_kv):
    def fn(rows, consts):
        pa, *tb = rows
        gq, gkv, wuq, wkv = consts
        cqn = _rms(pa[:, :MLA_Q_RANK], gq, MLA_Q_RANK).astype(BF16)
        ckvn = _rms(pa[:, MLA_Q_RANK:A_KPE], gkv, MLA_KV_RANK).astype(BF16)
        kv = _nt(ckvn, wkv)
        q = _rope(_nt(cqn, wuq), [_tile_slots(x) for x in tb])
        k = kv[:, :WIDE] + _tile_slots(_rope(pa[:, A_KPE:A_AB], tb))
        return [cqn, ckvn, q, k, kv[:, WIDE:]], []

    return _rowwise("mla_front_fwd", fn, [proj_a, *tabs], [g_q, g_kv, w_uq, w_kv],
                    [(MLA_Q_RANK, BF16), (MLA_KV_RANK, BF16)] + [(WIDE, BF16)] * 3)


def _mla_front_bwd(proj_a, tabs, g_q, g_kv, w_uq, w_kv, dq, dk, dv, dab):
    def fn(rows, consts):
        pa, t0, t1, t2, dqv, dkv, dvv, da = rows
        gq, gkv, wuq, wkv = consts
        tb = (t0, t1, t2)
        dq_p = _rope_transposed(dqv, [_tile_slots(x) for x in tb]).astype(BF16)
        dkv_p = jnp.concatenate([dkv, dvv], axis=1).astype(BF16)
        dkpe = dkv[:, :SLOT]
        for h in range(1, N_HEADS):
            dkpe = dkpe + dkv[:, h * SLOT:(h + 1) * SLOT]
        _, pull_q = jax.vjp(lambda x, g: _rms(x, g, MLA_Q_RANK), pa[:, :MLA_Q_RANK], gq)
        _, pull_kv = jax.vjp(lambda x, g: _rms(x, g, MLA_KV_RANK), pa[:, MLA_Q_RANK:A_KPE], gkv)
        dcq, dgq = pull_q(_nn(dq_p, wuq))
        dckv, dgkv = pull_kv(_nn(dkv_p, wkv))
        return [jnp.concatenate([dcq, dckv, _rope_transposed(dkpe, tb), da], axis=1), dq_p, dkv_p], [dgq, dgkv]

    return _rowwise("mla_front_bwd", fn, [proj_a, *tabs, dq, dk, dv, dab], [g_q, g_kv, w_uq, w_kv],
                    [(A_WIDTH, BF16), (WIDE, BF16), (2 * WIDE, BF16)], sums=[MLA_Q_RANK, MLA_KV_RANK])


def _slot_sum(x):
    parts = [jnp.broadcast_to(jnp.sum(x[:, h * SLOT:(h + 1) * SLOT], axis=1, keepdims=True), (x.shape[0], SLOT))
             for h in range(N_HEADS)]
    return jnp.concatenate(parts, axis=1)


def _mix_join(o_mla, o_gdn, gate, g_mla, g_gdn):
    mla = _rms(o_mla, g_mla, N_HEADS * MLA_V)
    gdn = o_gdn * lax.rsqrt(_slot_sum(o_gdn * o_gdn) * (1.0 / GDN_D) + EPS) * g_gdn * _silu(gate)
    return mla, gdn


MIX_TM = 256


def _mix_fwd(o_mla, o_gdn, gate, x, g_mla, g_gdn, w_out, g_post):
    dm = x.shape[1]

    def fn(rows, consts):
        om, og, gt, xv = rows
        gm, gg, wo, gp = consts
        cat = jnp.concatenate(_mix_join(om, og, gt, gm, gg), axis=1).astype(BF16)
        mixed = _nn(cat, wo)
        return [cat, mixed, xv + _rms(mixed, gp, dm)], []

    return _rowwise("mix_fwd", fn, [o_mla, o_gdn, gate, x], [g_mla, g_gdn, w_out, g_post],
                    [(2 * WIDE, BF16), (dm, F32), (dm, F32)], tm=MIX_TM)


def _mix_bwd(o_mla, o_gdn, gate, mixed, dy, g_mla, g_gdn, w_out, g_post):
    dm = mixed.shape[1]

    def fn(rows, consts):
        om, og, gt, mx, dyv = rows
        gm, gg, wo, gp = consts
        _, pull_post = jax.vjp(lambda hv, gv: _rms(hv, gv, dm), mx, gp)
        dmixed, dgp = pull_post(dyv)
        dmixed = dmixed.astype(BF16)
        dc = _nt(dmixed, wo)
        _, pull = jax.vjp(lambda x, g: _rms(x, g, N_HEADS * MLA_V), om, gm)
        dom, dgm = pull(dc[:, :WIDE])
        dn_out = dc[:, WIDE:]
        r = lax.rsqrt(_slot_sum(og * og) * (1.0 / GDN_D) + EPS)
        sig = _sigmoid(gt)
        normed = og * r
        dn = dn_out * gg * (gt * sig)
        dog = r * dn - normed * (r * r) * _slot_sum(dn * og) * (1.0 / GDN_D)
        dgt = dn_out * normed * gg * (sig * (1.0 + gt * (1.0 - sig)))
        dgg = jnp.sum(dn_out * normed * (gt * sig), axis=0, keepdims=True)
        return [dmixed, dom, _slot_sum(dom * om), dog, dgt], [dgp, dgm, dgg]

    return _rowwise("mix_bwd", fn, [o_mla, o_gdn, gate, mixed, dy], [g_mla, g_gdn, w_out, g_post],
                    [(dm, BF16), (WIDE, F32), (WIDE, F32), (WIDE, F32), (WIDE, BF16)], sums=[dm, WIDE, WIDE], tm=MIX_TM)


def _proj_fwd(x, g, weights):
    dm = x.shape[1]

    def fn(rows, consts):
        hn = _rms(rows[0], consts[0], dm).astype(BF16)
        return [hn] + [_nt(hn, wv) for wv in consts[1:]], []

    return _rowwise("proj_fwd", fn, [x], [g, *weights], [(dm, BF16)] + [(wv.shape[0], F32) for wv in weights], tm=MIX_TM)


def _proj_bwd(x, g, weights, cots, dy):
    dm = x.shape[1]
    n = len(weights)

    def fn(rows, consts):
        xv, dyv, *parts = rows
        dn = _nn(parts[0], consts[1])
        for p, wv in zip(parts[1:], consts[2:]):
            dn = dn + _nn(p, wv)
        _, pull = jax.vjp(lambda a, gv: _rms(a, gv, dm), xv, consts[0])
        dx, dg = pull(dn)
        return [dyv + dx], [dg]

    assert len(cots) == n
    return _rowwise("proj_bwd", fn, [x, dy, *cots], [g, *weights], [(dm, F32)], sums=[dm], tm=MIX_TM)


def _loss_fwd(y, target):
    dm = y.shape[1]

    def fn(rows, consts):
        err = rows[0] - rows[1]
        sq = err * err
        lanes = sq[:, :SLOT]
        for j in range(1, dm // SLOT):
            lanes = lanes + sq[:, j * SLOT:(j + 1) * SLOT]
        return [err * (1.0 / dm)], [jnp.sum(lanes, axis=0, keepdims=True) * (0.5 / dm)]

    return _rowwise("loss", fn, [y, target], [], [(dm, F32)], sums=[SLOT])


W_IN_CUTS = (0, 256, 384, 416, 1952, 1960, 1968, 2480)


def _heads_out(w, per_head, axis=-1):
    axis = axis % w.ndim
    shape = w.shape
    n = shape[axis] // per_head
    w = w.reshape(shape[:axis] + (n, per_head) + shape[axis + 1:])
    pad = [(0, 0)] * w.ndim
    pad[axis + 1] = (0, SLOT - per_head)
    return jnp.pad(w, pad).reshape(shape[:axis] + (n * SLOT,) + shape[axis + 1:])


def _heads_in(w, per_head, axis=-1):
    axis = axis % w.ndim
    shape = w.shape
    n = shape[axis] // SLOT
    w = w.reshape(shape[:axis] + (n, SLOT) + shape[axis + 1:])
    w = lax.slice_in_dim(w, 0, per_head, axis=axis + 1)
    return w.reshape(shape[:axis] + (n * per_head,) + shape[axis + 1:])


def _pad_lanes(v, lo, width=SLOT):
    return jnp.pad(v, [(0, 0)] * (v.ndim - 1) + [(lo, width - lo - v.shape[-1])])


def _pad_rows(v, lo, rows=SLOT):
    return jnp.pad(v, [(lo, rows - lo - v.shape[0])] + [(0, 0)] * (v.ndim - 1))


def _layout_weights(w):
    c = W_IN_CUTS
    w_in = w["w_in_t"]
    p = {}
    p["w_a"] = jnp.concatenate([w_in[c[0]:c[2]], _pad_rows(w_in[c[2]:c[3]], MLA_NOPE), _pad_rows(w_in[c[4]:c[6]], 0)], axis=0)
    p["w_qkv"] = _heads_out(w_in[c[3]:c[4]], GDN_D, axis=0)
    p["w_gate"] = _heads_out(w_in[c[6]:c[7]], GDN_D, axis=0)
    p["w_uq"] = _heads_out(w["uq_t"], MLA_NOPE + MLA_ROPE, axis=0)
    ukv = w["ukv_t"].reshape(N_HEADS, MLA_NOPE + MLA_V, MLA_KV_RANK)
    p["w_kv"] = jnp.concatenate([_heads_out(ukv[:, :MLA_NOPE].reshape(-1, MLA_KV_RANK), MLA_NOPE, axis=0),
                                 _heads_out(ukv[:, MLA_NOPE:].reshape(-1, MLA_KV_RANK), MLA_V, axis=0)], axis=0)
    p["conv"] = _heads_out(w["gdn_conv_w"], GDN_D)
    p["g_mla_out"] = _heads_out(w["mla_out_g"], MLA_V)
    p["g_gdn"] = jnp.tile(_pad_lanes(w["gdn_norm_g"], 0), (1, N_HEADS))
    p["a_log"] = _pad_lanes(w["gdn_a_log"], 0)
    p["dt_bias"] = _pad_lanes(w["gdn_dt_bias"], 0)
    return p


def _unlayout_grads(d):
    c = W_IN_CUTS
    g = {}
    da = d["w_a"]
    kpe0 = A_KPE + MLA_NOPE
    g["w_in_t"] = jnp.concatenate([da[:A_KPE], da[kpe0:kpe0 + MLA_ROPE], _heads_in(d["w_qkv"], GDN_D, axis=0),
                                   da[A_AB:A_AB + 2 * N_HEADS], _heads_in(d["w_gate"], GDN_D, axis=0)], axis=0)
    assert g["w_in_t"].shape[0] == c[-1]
    g["uq_t"] = _heads_in(d["w_uq"], MLA_NOPE + MLA_ROPE, axis=0)
    dk = _heads_in(d["w_kv"][:WIDE], MLA_NOPE, axis=0).reshape(N_HEADS, MLA_NOPE, MLA_KV_RANK)
    dv = _heads_in(d["w_kv"][WIDE:], MLA_V, axis=0).reshape(N_HEADS, MLA_V, MLA_KV_RANK)
    g["ukv_t"] = jnp.concatenate([dk, dv], axis=1).reshape(-1, MLA_KV_RANK)
    g["w_out"] = _heads_in(d["w_out"], GDN_D, axis=0)
    g["gdn_conv_w"] = _heads_in(d["conv"], GDN_D)
    g["mla_out_g"] = _heads_in(d["g_mla_out"], MLA_V)
    g["gdn_norm_g"] = jnp.sum(d["g_gdn"].reshape(N_HEADS, SLOT), axis=0, keepdims=True)[:, :GDN_D]
    g["gdn_a_log"] = d["a_log"][:, :N_HEADS]
    g["gdn_dt_bias"] = d["dt_bias"][:, :N_HEADS]
    return g


def _weight_grad(name, cots, acts, out_dtype=F32, tm=1024, tn=1024, tk=2048, carry=None):
    return _matmul(name, cots, acts, "tn", out_dtype=out_dtype, tm=tm, tn=tn, tk=tk, carry=carry)


def _by_device(a):
    return a.astype(BF16).reshape((N_DEV, a.shape[0] // N_DEV) + a.shape[1:])


def _rows_of(blocks):
    return blocks.reshape((-1,) + blocks.shape[2:])


def _local_step(x, positions, target, w, mid, late):
    tabs = _rope_tables(positions)

    (h1, x1, hg1, hu1), gathered = _ffn_fwd("ffn1_fwd", x, w["ffn1_pre_g"], w["ffn1"], 0, w["ffn1_post_g"], carry=mid)
    w = dict(w, w_in_t=_rows_of(gathered[0]), uq_t=_rows_of(gathered[1]), ukv_t=_rows_of(gathered[2]))
    p = _layout_weights(w)
    in_weights = [p["w_a"], p["w_qkv"], p["w_gate"]]
    hn, proj_a, proj_qkv, proj_gate = _proj_fwd(x1, w["mix_pre_g"], in_weights)
    cqn, ckvn, q, k, v = _mla_front_fwd(proj_a, tabs, w["mla_q_norm_g"], w["mla_kv_norm_g"], p["w_uq"], p["w_kv"])
    o_mla, lse = _attn_fwd(q, k, v)
    ab = (proj_a, SLOT, A_AB // SLOT)
    qkv_n = _gdn_conv_fwd(proj_qkv, p["conv"])
    gb, bb = _gates_fwd(ab, p["a_log"], p["dt_bias"])
    (o_gdn, keep), (ffn2, w_out) = _gdn_fwd(qkv_n, gb, bb, carry=late)
    p["w_out"] = _heads_out(_rows_of(w_out), GDN_D, axis=0)
    cat, mixed, x2 = _mix_fwd(o_mla, o_gdn, proj_gate, x1, p["g_mla_out"], p["g_gdn"], p["w_out"], w["mix_post_g"])
    (h2, y, hg2, hu2), _ = _ffn_fwd("ffn2_fwd", x2, w["ffn2_pre_g"], ffn2, 0, w["ffn2_post_g"])
    dy, loss_lanes = _loss_fwd(y, target)

    g = {}
    (dx2, xn2, dh2, a2, dhg2, dhu2, g["ffn2_pre_g"], g["ffn2_post_g"]), _ = _ffn_bwd(
        "ffn2_bwd", x2, h2, hg2, hu2, dy, w["ffn2_pre_g"], ffn2, 0, w["ffn2_post_g"])
    ffn2_grads = _Scatter([_by_device(_weight_grad("ffn2_dw_gate", dhg2, xn2, BF16, tm=1408)),
                           _by_device(_weight_grad("ffn2_dw_up", dhu2, xn2, BF16, tm=1408)),
                           _by_device(_weight_grad("ffn2_dw_down", a2, dh2, BF16, tm=1408))])
    d = {}
    dmixed, do_mla, delta, do_gdn, dgate, g["mix_post_g"], d["g_mla_out"], d["g_gdn"] = _mix_bwd(
        o_mla, o_gdn, proj_gate, mixed, dx2, p["g_mla_out"], p["g_gdn"], p["w_out"], w["mix_post_g"])
    d["w_out"] = _weight_grad("mix_out_dw", cat, dmixed)
    dq, dk, dv = _attn_bwd(q, k, v, do_mla, lse, delta)
    (dqkv_n, dgb, dbb), landed_ffn2 = _gdn_bwd(qkv_n, gb, bb, keep, do_gdn, carry=ffn2_grads)
    dab, d["a_log"], d["dt_bias"] = _gates_bwd(ab, p["a_log"], p["dt_bias"], dgb, dbb)
    dproj_qkv, d["conv"] = _gdn_conv_bwd(proj_qkv, p["conv"], dqkv_n)
    dproj_a, dq_p, dkv_p, g["mla_q_norm_g"], g["mla_kv_norm_g"] = _mla_front_bwd(
        proj_a, tabs, w["mla_q_norm_g"], w["mla_kv_norm_g"], p["w_uq"], p["w_kv"], dq, dk, dv, dab)
    d["w_uq"] = _weight_grad("mla_q_dw", dq_p, cqn)
    d["w_kv"] = _weight_grad("mla_kv_dw", dkv_p, ckvn)
    d["w_a"] = _weight_grad("proj_a_dw", dproj_a, hn, tm=640)
    d["w_qkv"] = _weight_grad("proj_qkv_dw", dproj_qkv, hn)
    d["w_gate"] = _weight_grad("proj_gate_dw", dgate, hn)
    dx1, g["mix_pre_g"] = _proj_bwd(x1, w["mix_pre_g"], in_weights, [dproj_a, dproj_qkv, dgate], dx2)
    g.update(_unlayout_grads(d))
    others = [t for t, _ in OTHER.values()]
    (dx, xn1, dh1, a1, dhg1, dhu1, g["ffn1_pre_g"], g["ffn1_post_g"]), landed_others = _ffn_bwd(
        "ffn1_bwd", x, h1, hg1, hu1, dx1, w["ffn1_pre_g"], w["ffn1"], 0, w["ffn1_post_g"], carry=_Scatter([_by_device(g.pop(t)) for t in others]))
    dw_down = _weight_grad("ffn1_dw_down", a1, dh1, BF16, tm=1408)
    dw_gate, (landed_down,) = _weight_grad("ffn1_dw_gate", dhg1, xn1, BF16, tm=1408, carry=_Scatter([_by_device(dw_down)]))
    dw_up, (landed_gate,) = _weight_grad("ffn1_dw_up", dhu1, xn1, BF16, tm=1408, carry=_Scatter([_by_device(dw_gate)]))
    (landed_up,) = _exchange("scatter_last", _Scatter([_by_device(dw_up)]))
    landed = dict(zip(list(FFN_NAMES) + list(OTHER),
                      [landed_gate, landed_up, landed_down] + list(landed_ffn2) + list(landed_others)))
    return loss_lanes, dx, g, landed


MESH_AXES = ("x", "y", "c")
N_LINKS = N_DEV - 1


def _place():
    return tuple(lax.axis_index(a) for a in MESH_AXES)


def _block_of(dev):
    x, y, c = dev
    return 4 * x + 2 * y + c


def _remote_copy(src, dst, sems, k, to):
    send_sems, recv_sems = sems
    return pltpu.make_async_remote_copy(src_ref=src, dst_ref=dst, send_sem=send_sems.at[k], recv_sem=recv_sems.at[k],
                                        device_id=to, device_id_type=pl.DeviceIdType.MESH)


class _Exchange:
    def __init__(self, arrays):
        self.arrays = list(arrays)
        self.n = len(self.arrays)
        self.specs = [pl.BlockSpec(memory_space=pl.ANY)] * self.n
        self.scratch = [pltpu.SemaphoreType.DMA((self.n * N_LINKS,)), pltpu.SemaphoreType.DMA((self.n * N_LINKS,)),
                        pltpu.SemaphoreType.DMA((self.n,))]

    def split(self, refs):
        n = self.n
        return refs[:n], refs[n:2 * n], (refs[2 * n], refs[2 * n + 1]), refs[2 * n + 2]


class _Gather(_Exchange):
    def out_shape(self):
        return [jax.ShapeDtypeStruct((N_DEV,) + a.shape, a.dtype) for a in self.arrays]

    def _plan(self, ins, outs, sems, local_sems):
        x, y, c = _place()
        me, sibling = (x, y, c), (x, y, 1 - c)
        chips = [(1 - x, y), (x, 1 - y), (1 - x, 1 - y)]

        def copy(a, k, block, to, mine=False):
            src = ins[a] if mine else outs[a].at[_block_of(block)]
            return _remote_copy(src, outs[a].at[_block_of(block)], sems, a * N_LINKS + k, to)

        local = [pltpu.make_async_copy(ins[a], outs[a].at[_block_of(me)], local_sems.at[a]) for a in range(self.n)]
        first = []
        for a in range(self.n):
            first.append(copy(a, 0, me, sibling, mine=True))
            first += [copy(a, 1 + j, me, (*chip, c), mine=True) for j, chip in enumerate(chips)]
        return me, sibling, chips, c, copy, local, first

    def start(self, ins, outs, sems, local_sems):
        *_, local, first = self._plan(ins, outs, sems, local_sems)
        for cp in local + first:
            cp.start()

    def finish(self, ins, outs, sems, local_sems):
        me, sibling, chips, c, copy, local, first = self._plan(ins, outs, sems, local_sems)
        passed = []
        for j, chip in enumerate(chips):
            for a in range(self.n):
                copy(a, 1 + j, (*chip, c), me).wait_recv()
                passed.append(copy(a, 4 + j, (*chip, c), sibling))
                passed[-1].start()
        for a in range(self.n):
            copy(a, 0, sibling, me).wait_recv()
            for j, chip in enumerate(chips):
                copy(a, 4 + j, (*chip, 1 - c), me).wait_recv()
        for cp in first + passed:
            cp.wait_send()
        for cp in local:
            cp.wait()


class _Scatter(_Exchange):
    def out_shape(self):
        return [jax.ShapeDtypeStruct(a.shape, a.dtype) for a in self.arrays]

    def _plan(self, ins, outs, sems, local_sems):
        x, y, c = _place()
        me = _block_of((x, y, c))

        def peer(r):
            return (1 - x if r & 4 else x, 1 - y if r & 2 else y, 1 - c if r & 1 else c)

        local = [pltpu.make_async_copy(ins[a].at[me], outs[a].at[me], local_sems.at[a]) for a in range(self.n)]
        sends = [_remote_copy(ins[a].at[_block_of(peer(r))], outs[a].at[me], sems, a * N_LINKS + r - 1, peer(r))
                 for a in range(self.n) for r in range(1, N_DEV)]
        arrivals = [_remote_copy(ins[a].at[me], outs[a].at[_block_of(peer(r))], sems, a * N_LINKS + r - 1, peer(r))
                    for a in range(self.n) for r in range(1, N_DEV)]
        return local, sends, arrivals

    def start(self, ins, outs, sems, local_sems):
        local, sends, _ = self._plan(ins, outs, sems, local_sems)
        for cp in local + sends:
            cp.start()

    def finish(self, ins, outs, sems, local_sems):
        local, sends, arrivals = self._plan(ins, outs, sems, local_sems)
        for cp in arrivals:
            cp.wait_recv()
        for cp in sends:
            cp.wait_send()
        for cp in local:
            cp.wait()


def _exchange(name, plan):
    def body(*refs):
        parts = plan.split(refs)
        plan.start(*parts)
        plan.finish(*parts)

    return pl.pallas_call(
        body, name=name,
        in_specs=plan.specs,
        out_specs=plan.specs,
        out_shape=plan.out_shape(),
        scratch_shapes=plan.scratch,
    )(*plan.arrays)


def _call_carrying(body, plan, operands, *, name, grid, in_specs, out_specs, out_shape, scratch_shapes, compiler_params):
    if plan is None:
        outs = pl.pallas_call(body, name=name, grid=grid, in_specs=in_specs, out_specs=out_specs, out_shape=out_shape,
                              scratch_shapes=scratch_shapes, compiler_params=compiler_params)(*operands)
        return outs, []
    n_i, n_o, n_s, k = len(in_specs), len(out_specs), len(scratch_shapes), plan.n

    def whole(*refs):
        cut = [n_i, n_i + k, n_i + k + n_o, n_i + 2 * k + n_o, n_i + 2 * k + n_o + n_s]
        own_in, ex_in, own_out, ex_out, own_scr, ex_scr = (refs[a:b] for a, b in zip([0] + cut, cut + [len(refs)]))
        parts = plan.split(ex_in + ex_out + ex_scr)
        first = last = True
        for axis, size in enumerate(grid):
            first = first & (pl.program_id(axis) == 0)
            last = last & (pl.program_id(axis) == size - 1)

        @pl.when(first)
        def _():
            plan.start(*parts)

        body(*own_in, *own_out, *own_scr)

        @pl.when(last)
        def _():
            plan.finish(*parts)

    outs = pl.pallas_call(
        whole, name=name, grid=grid,
        in_specs=list(in_specs) + plan.specs, out_specs=list(out_specs) + plan.specs,
        out_shape=list(out_shape) + plan.out_shape(), scratch_shapes=list(scratch_shapes) + plan.scratch,
        compiler_params=compiler_params,
    )(*operands, *plan.arrays)
    return outs[:n_o], outs[n_o:]


def _row_tile(rows, target=256):
    best = rows
    for cand in range(16, min(rows, target) + 1, 16):
        if rows % cand == 0:
            best = cand
    return best


def _sum_blocks(name, blocks):
    rows, width = blocks.shape[-2:]
    tm = _row_tile(rows)

    def body(x_ref, o_ref):
        acc = x_ref[0].astype(F32)
        for d in range(1, N_DEV):
            acc = acc + x_ref[d].astype(F32)
        o_ref[...] = acc

    return pl.pallas_call(
        body, name=name,
        grid=(rows // tm,),
        in_specs=[pl.BlockSpec((N_DEV, tm, width), lambda i: (0, i, 0))],
        out_specs=pl.BlockSpec((tm, width), lambda i: (i, 0)),
        out_shape=jax.ShapeDtypeStruct((rows, width), F32),
        compiler_params=pltpu.CompilerParams(dimension_semantics=("parallel",)),
    )(blocks)


def _all_reduce_small(name, vec):
    rows, width = vec.shape

    def body(x_ref, o_ref, all_ref, send_sems, recv_sems):
        x, y, c = _place()
        me = _block_of((x, y, c))
        all_ref[me] = x_ref[...]

        def peer(r):
            return (1 - x if r & 4 else x, 1 - y if r & 2 else y, 1 - c if r & 1 else c)

        def copy(r, block):
            return _remote_copy(x_ref, all_ref.at[block], (send_sems, recv_sems), r - 1, peer(r))

        sends = [copy(r, me) for r in range(1, N_DEV)]
        for cp in sends:
            cp.start()
        for r in range(1, N_DEV):
            copy(r, _block_of(peer(r))).wait_recv()
        for cp in sends:
            cp.wait_send()
        acc = all_ref[0]
        for d in range(1, N_DEV):
            acc = acc + all_ref[d]
        o_ref[...] = acc

    return pl.pallas_call(
        body, name=name,
        in_specs=[pl.BlockSpec(memory_space=pltpu.VMEM)],
        out_specs=pl.BlockSpec(memory_space=pltpu.VMEM),
        out_shape=jax.ShapeDtypeStruct((rows, width), F32),
        scratch_shapes=[pltpu.VMEM((N_DEV, rows, width), F32), pltpu.SemaphoreType.DMA((N_LINKS,)), pltpu.SemaphoreType.DMA((N_LINKS,))],
    )(vec)


def _adamw(name, w, g, m, v):
    def fn(rows, consts):
        wv, gv, mv, vv = rows
        m2 = ADAM_B1 * mv + (1.0 - ADAM_B1) * gv
        v2 = ADAM_B2 * vv + (1.0 - ADAM_B2) * jnp.square(gv)
        m_hat = m2 / (1.0 - ADAM_B1 ** ADAM_STEP)
        v_hat = v2 / (1.0 - ADAM_B2 ** ADAM_STEP)
        return [-ADAM_LR * (m_hat / (jnp.sqrt(v_hat) + ADAM_EPS) + ADAM_WD * wv), m2, v2], []

    return _rowwise(name, fn, [w, g, m, v], [], [(w.shape[1], F32)] * 3, tm=_row_tile(w.shape[0]))


ROW = 1024
FFN_NAMES = ("ffn1_w_gate", "ffn1_w_up", "ffn1_w_down", "ffn2_w_gate", "ffn2_w_up", "ffn2_w_down")
OTHER = {"w_in": ("w_in_t", True), "mla_w_uq": ("uq_t", True), "mla_w_ukv": ("ukv_t", True), "w_out": ("w_out", False)}
BY_COLUMNS = ("ffn1_w_gate", "ffn1_w_up", "ffn2_w_gate", "ffn2_w_up", "w_in", "mla_w_uq", "mla_w_ukv")
SMALL = {
    "ffn1_pre_g": (1024, 1024), "ffn1_post_g": (1024, 1024), "mix_pre_g": (1024, 1024), "mla_q_norm_g": (256, 256),
    "mla_kv_norm_g": (128, 128), "mla_out_g": (512, 512), "gdn_a_log": (8, 128), "gdn_dt_bias": (8, 128),
    "gdn_norm_g": (64, 128), "mix_post_g": (1024, 1024), "ffn2_pre_g": (1024, 1024), "ffn2_post_g": (1024, 1024),
}
CONV_SHAPE = (GDN_CONV, 3 * N_HEADS * GDN_D)
CONV_SHARD = (GDN_CONV, CONV_SHAPE[1] // N_DEV)
CONV_LANES = CONV_SHAPE[0] * CONV_SHAPE[1]
SMALL_ROWS = 8
REDUCE_ROWS = 16


def _pack_small(vecs, conv, rows):
    parts = [_pad_lanes(vecs[n].reshape(1, -1), 0, r) for n, (_, r) in SMALL.items()]
    parts.append(conv.reshape(1, -1))
    flat = jnp.concatenate(parts, axis=1)
    return _pad_lanes(flat, 0, rows * ROW).reshape(rows, ROW)


def _unpack_small(buf):
    flat = buf.reshape(1, -1)
    out, at = {}, 0
    for n, (w, r) in SMALL.items():
        out[n] = flat[:, at:at + w]
        at += r
    return out, flat[0, at:]


def kernel(x, positions, ffn1_pre_g, ffn1_w_gate, ffn1_w_up, ffn1_w_down, ffn1_post_g, mix_pre_g, w_in, mla_q_norm_g, mla_w_uq, mla_kv_norm_g, mla_w_ukv, mla_out_g, gdn_conv_w, gdn_a_log, gdn_dt_bias, gdn_norm_g, w_out, mix_post_g, ffn2_pre_g, ffn2_w_gate, ffn2_w_up, ffn2_w_down, ffn2_post_g, loss_target, m_ffn1_pre_g, m_ffn1_w_gate, m_ffn1_w_up, m_ffn1_w_down, m_ffn1_post_g, m_mix_pre_g, m_w_in, m_mla_q_norm_g, m_mla_w_uq, m_mla_kv_norm_g, m_mla_w_ukv, m_mla_out_g, m_gdn_conv_w, m_gdn_a_log, m_gdn_dt_bias, m_gdn_norm_g, m_w_out, m_mix_post_g, m_ffn2_pre_g, m_ffn2_w_gate, m_ffn2_w_up, m_ffn2_w_down, m_ffn2_post_g, v_ffn1_pre_g, v_ffn1_w_gate, v_ffn1_w_up, v_ffn1_w_down, v_ffn1_post_g, v_mix_pre_g, v_w_in, v_mla_q_norm_g, v_mla_w_uq, v_mla_kv_norm_g, v_mla_w_ukv, v_mla_out_g, v_gdn_conv_w, v_gdn_a_log, v_gdn_dt_bias, v_gdn_norm_g, v_w_out, v_mix_post_g, v_ffn2_pre_g, v_ffn2_w_gate, v_ffn2_w_up, v_ffn2_w_down, v_ffn2_post_g):
    given = dict(locals())
    order = ["ffn1_pre_g", "ffn1_w_gate", "ffn1_w_up", "ffn1_w_down", "ffn1_post_g", "mix_pre_g", "w_in", "mla_q_norm_g",
             "mla_w_uq", "mla_kv_norm_g", "mla_w_ukv", "mla_out_g", "gdn_conv_w", "gdn_a_log", "gdn_dt_bias", "gdn_norm_g",
             "w_out", "mix_post_g", "ffn2_pre_g", "ffn2_w_gate", "ffn2_w_up", "ffn2_w_down", "ffn2_post_g"]
    assert sorted(order) == sorted(list(FFN_NAMES) + list(OTHER) + list(SMALL) + ["gdn_conv_w"])

    def drop_depth(a):
        return a[0] if a.ndim == 3 else a

    wts = {n: drop_depth(given[n]) for n in order}
    mom = {n: drop_depth(given["m_" + n]) for n in order}
    var = {n: drop_depth(given["v_" + n]) for n in order}
    me = _block_of(_place())

    def wire(n):
        return (wts[n].T if n in BY_COLUMNS else wts[n]).astype(BF16)

    (ffn1,) = _exchange("gather_first", _Gather([jnp.stack([wire(n) for n in FFN_NAMES[:3]])]))
    mid = _Gather([wire(n) for n in ("w_in", "mla_w_uq", "mla_w_ukv")])
    late = _Gather([jnp.stack([wire(n) for n in FFN_NAMES[3:]]), wire("w_out")])
    conv_at = lax.dynamic_update_slice(jnp.zeros((N_DEV, CONV_SHARD[0] * CONV_SHARD[1]), F32),
                                       wts["gdn_conv_w"].reshape(1, -1), (me, 0))
    conv_all = _all_reduce_small("gather_conv", _pad_lanes(conv_at.reshape(1, -1), 0, SMALL_ROWS * ROW).reshape(SMALL_ROWS, ROW))
    full = {n: wts[n] for n in SMALL}
    full["ffn1"] = ffn1
    full["gdn_conv_w"] = conv_all.reshape(-1)[:CONV_LANES].reshape((N_DEV,) + CONV_SHARD).transpose(1, 0, 2).reshape(CONV_SHAPE)

    loss_lanes, dx, grads, landed = _local_step(x[0], positions[0], loss_target[0], full, mid, late)
    loss = lax.psum(jnp.sum(loss_lanes), MESH_AXES)

    sums = {n: _sum_blocks("sum_" + n, blocks) for n, blocks in landed.items()}
    grad = {n: (sums[n].T if n in BY_COLUMNS else sums[n]) for n in sums}
    small_sum = _all_reduce_small("reduce_small", _pack_small(grads, grads["gdn_conv_w"].reshape(-1), REDUCE_ROWS))
    small_grad, conv_grad_full = _unpack_small(small_sum)
    grad.update(small_grad)
    grad["gdn_conv_w"] = lax.dynamic_slice(conv_grad_full[:CONV_LANES].reshape(CONV_SHAPE), (0, me * CONV_SHARD[1]), CONV_SHARD)

    outs = {"grad": grad, "delta": {}, "new_m": {}, "new_v": {}}
    for n in list(FFN_NAMES) + list(OTHER):
        outs["delta"][n], outs["new_m"][n], outs["new_v"][n] = _adamw("adamw_" + n, wts[n], grad[n], mom[n], var[n])
    small = [_pack_small(s, s["gdn_conv_w"].reshape(-1), SMALL_ROWS) for s in (wts, grad, mom, var)]
    for kind, s in zip(("delta", "new_m", "new_v"), _adamw("adamw_small", *small)):
        vecs, conv = _unpack_small(s)
        outs[kind].update(vecs)
        outs[kind]["gdn_conv_w"] = conv[:CONV_SHARD[0] * CONV_SHARD[1]].reshape(CONV_SHARD)
    result = [loss, dx[None]]
    for kind in ("grad", "delta", "new_m", "new_v"):
        result += [outs[kind][n].reshape(given[n].shape) for n in order]
    return tuple(result)
```

```python
import jax
import jax.numpy as jnp
from jax import lax
from jax.experimental import pallas as pl
from jax.experimental.pallas import tpu as pltpu

F32 = jnp.float32
BF16 = jnp.bfloat16
HI = lax.Precision.HIGH

N_DEV = 8
N_HEADS = 8
SLOT = 128
MLA_Q_RANK = 256
MLA_KV_RANK = 128
MLA_NOPE = 64
MLA_ROPE = 32
MLA_V = 64
GDN_D = 64
GDN_CONV = 4
GDN_CHUNK = 64
ROPE_THETA = 10000.0
EPS = 1e-6
ADAM_LR, ADAM_B1, ADAM_B2, ADAM_EPS, ADAM_WD, ADAM_STEP = 0.001, 0.9, 0.999, 1e-08, 0.01, 10


def _dot(a, b, ca, cb, precision=None):
    lead = a.ndim - 2
    batch = tuple(range(lead))
    return lax.dot_general(a, b, (((lead + ca,), (lead + cb,)), (batch, batch)), precision=precision,
                           preferred_element_type=F32)


def _nn(a, b, precision=None):
    return _dot(a, b, 1, 0, precision)


def _nt(a, b, precision=None):
    return _dot(a, b, 1, 1, precision)


def _tn(a, b, precision=None):
    return _dot(a, b, 0, 0, precision)


def _sigmoid(x):
    return 1.0 / (1.0 + jnp.exp(-x))


def _silu(x):
    return x * _sigmoid(x)


def _rms(x, g, n):
    ms = jnp.sum(x * x, axis=-1, keepdims=True) * (1.0 / n)
    return x * lax.rsqrt(ms + EPS) * g


def _chunk_masks():
    c = GDN_CHUNK
    i = lax.broadcasted_iota(jnp.int32, (c, c), 0)
    j = lax.broadcasted_iota(jnp.int32, (c, c), 1)
    lower = i >= j
    strict = i > j
    eye = (i == j).astype(F32)
    blocks = []
    b = 1
    while b < c:
        same = (i // (2 * b)) == (j // (2 * b))
        blocks.append(same & ((i % (2 * b)) >= b) & ((j % (2 * b)) < b))
        b *= 2
    return lower, strict, eye, blocks


def _unit_lower_inverse(low, eye, blocks):
    t = eye - jnp.where(blocks[0], low, 0.0)
    for m in blocks[1:]:
        lo = jnp.where(m, low, 0.0)
        t = t - _nn(t, _nn(lo, t, HI), HI)
    return t


@jax.custom_vjp
def _known_inverse(low, tinv):
    return tinv


def _known_inverse_fwd(low, tinv):
    return tinv, tinv


def _known_inverse_bwd(tinv, dt):
    return -_tn(tinv, _nt(dt, tinv, HI), HI), jnp.zeros_like(tinv)


_known_inverse.defvjp(_known_inverse_fwd, _known_inverse_bwd)

_PRODUCTS = {"nn": _nn, "nt": _nt, "tn": _tn}


@jax.custom_vjp
def _known_nn(a, b, c):
    return c


@jax.custom_vjp
def _known_nt(a, b, c):
    return c


@jax.custom_vjp
def _known_tn(a, b, c):
    return c


def _known_fwd(a, b, c):
    return c, (a, b, c)


_known_nn.defvjp(_known_fwd, lambda r, dc: (_nt(dc, r[1], HI), _tn(r[0], dc, HI), jnp.zeros_like(r[2])))
_known_nt.defvjp(_known_fwd, lambda r, dc: (_nn(dc, r[1], HI), _tn(dc, r[0], HI), jnp.zeros_like(r[2])))
_known_tn.defvjp(_known_fwd, lambda r, dc: (_nt(r[1], dc, HI), _nn(r[0], dc, HI), jnp.zeros_like(r[2])))
_KNOWN = {"nn": _known_nn, "nt": _known_nt, "tn": _known_tn}
GDN_PRODUCTS = 8
GDN_KEPT = 2 + GDN_PRODUCTS


def _gdn_chunk(q, k, v, gc, bb, s, masks, known=None):
    lower, strict, eye, blocks = masks
    made = []

    def product(kind, a, b):
        c = _PRODUCTS[kind](a, b, HI) if known is None else _KNOWN[kind](a, b, known[1 + len(made)])
        made.append(c)
        return c

    qs = q * (GDN_D ** -0.5)
    gct = jnp.swapaxes(gc, -1, -2)
    decay = jnp.exp(jnp.where(lower, gc - gct, -1e30))
    kb = k * bb
    low = jnp.where(strict, product("nt", kb, k) * decay, 0.0)
    tinv = _unit_lower_inverse(low, eye, blocks) if known is None else _known_inverse(low, known[0])
    eg = jnp.exp(gc)
    w = product("nn", tinv, kb * eg)
    u = product("nn", tinv, v * bb)
    attn = product("nt", qs, k) * decay
    last = lax.broadcasted_iota(jnp.int32, gc.shape[-2:], 0) == GDN_CHUNK - 1
    g_end = jnp.sum(jnp.where(last, gc, 0.0), axis=-2, keepdims=True)
    k_dec = k * jnp.exp(g_end - gc)
    v_new = u - product("nn", w, s)
    o = product("nn", qs * eg, s) + product("nn", attn, v_new)
    s_new = s * jnp.exp(g_end) + product("tn", k_dec, v_new)
    assert len(made) == GDN_PRODUCTS
    return o, s_new, [tinv] + made


GDN_GROUP = 8
GDN_GROUPS = N_HEADS // GDN_GROUP


def _group_heads(ref):
    return jnp.stack([ref[:, pl.ds(j * SLOT, GDN_D)] for j in range(GDN_GROUP)])


def _ungroup_heads(ref, val):
    pad = jnp.zeros((GDN_CHUNK, SLOT - GDN_D), F32)
    for j in range(GDN_GROUP):
        ref[:, pl.ds(j * SLOT, GDN_D)] = val[j]
        ref[:, pl.ds(j * SLOT + GDN_D, SLOT - GDN_D)] = pad


def _gdn_fwd(qkv, gb, bb, carry=None):
    t = qkv.shape[0]
    n_chunks = t // GDN_CHUNK
    d = GDN_D

    def body(q_ref, k_ref, v_ref, g_ref, b_ref, o_ref, keep_ref, s_ref):
        @pl.when(pl.program_id(1) == 0)
        def _():
            s_ref[...] = jnp.zeros_like(s_ref)

        s = s_ref[...]
        keep_ref[:, 0, 0] = s
        o, s_new, made = _gdn_chunk(*[_group_heads(r) for r in (q_ref, k_ref, v_ref, g_ref, b_ref)], s, _chunk_masks())
        for i, val in enumerate(made):
            keep_ref[:, 0, 1 + i] = val
        s_ref[...] = s_new
        _ungroup_heads(o_ref, o)

    def spec(kind=0):
        return pl.BlockSpec((GDN_CHUNK, GDN_GROUP * SLOT), lambda h, n: (n, kind * GDN_GROUPS + h))

    return _call_carrying(
        body, carry, (qkv, qkv, qkv, gb, bb), name="gdn_fwd",
        grid=(GDN_GROUPS, n_chunks),
        in_specs=[spec(0), spec(1), spec(2), spec(), spec()],
        out_specs=[spec(), pl.BlockSpec((GDN_GROUP, 1, GDN_KEPT, d, d), lambda h, n: (h, n, 0, 0, 0))],
        out_shape=[jax.ShapeDtypeStruct((t, N_HEADS * SLOT), F32), jax.ShapeDtypeStruct((N_HEADS, n_chunks, GDN_KEPT, d, d), F32)],
        scratch_shapes=[pltpu.VMEM((GDN_GROUP, d, d), F32)],
        compiler_params=pltpu.CompilerParams(dimension_semantics=("arbitrary", "arbitrary")),
    )


def _gdn_bwd(qkv, gb, bb, keep, do, carry=None):
    t = qkv.shape[0]
    n_chunks = t // GDN_CHUNK
    d = GDN_D

    def body(q_ref, k_ref, v_ref, g_ref, b_ref, keep_ref, do_ref, dqkv_ref, dg_ref, db_ref, ds_ref):
        @pl.when(pl.program_id(1) == 0)
        def _():
            ds_ref[...] = jnp.zeros_like(ds_ref)

        masks = _chunk_masks()
        known = [keep_ref[:, 0, 1 + i] for i in range(GDN_KEPT - 1)]
        _, pull = jax.vjp(lambda *a: _gdn_chunk(*a, masks, known)[:2],
                          *[_group_heads(r) for r in (q_ref, k_ref, v_ref, g_ref, b_ref)], keep_ref[:, 0, 0])
        dq, dk, dv, dg, db, ds = pull((_group_heads(do_ref), ds_ref[...]))
        ds_ref[...] = ds
        for i, val in enumerate((dq, dk, dv)):
            _ungroup_heads(dqkv_ref.at[i], val)
        _ungroup_heads(dg_ref, dg)
        _ungroup_heads(db_ref, db)

    def spec(kind=0):
        return pl.BlockSpec((GDN_CHUNK, GDN_GROUP * SLOT), lambda h, n: (n_chunks - 1 - n, kind * GDN_GROUPS + h))

    return _call_carrying(
        body, carry, (qkv, qkv, qkv, gb, bb, keep, do), name="gdn_bwd",
        grid=(GDN_GROUPS, n_chunks),
        in_specs=[spec(0), spec(1), spec(2), spec(), spec(),
                  pl.BlockSpec((GDN_GROUP, 1, GDN_KEPT, d, d), lambda h, n: (h, n_chunks - 1 - n, 0, 0, 0)), spec()],
        out_specs=[pl.BlockSpec((3, GDN_CHUNK, GDN_GROUP * SLOT), lambda h, n: (0, n_chunks - 1 - n, h)), spec(), spec()],
        out_shape=[jax.ShapeDtypeStruct((3, t, N_HEADS * SLOT), F32)] + [jax.ShapeDtypeStruct((t, N_HEADS * SLOT), F32)] * 2,
        scratch_shapes=[pltpu.VMEM((GDN_GROUP, d, d), F32)],
        compiler_params=pltpu.CompilerParams(dimension_semantics=("arbitrary", "arbitrary")),
    )


def _rowwise(name, fn, rows, consts, outs, sums=(), tm=512):
    rows = [x if isinstance(x, tuple) else (x, x.shape[1], 0) for x in rows]
    t = rows[0][0].shape[0]
    tm = min(tm, t)
    steps = t // tm
    n_r, n_c, n_o, n_s = len(rows), len(consts), len(outs), len(sums)

    def window(width, block):
        return pl.BlockSpec((tm, width), lambda i: (i, block))

    def body(*refs):
        r, c = refs[:n_r], refs[n_r:n_r + n_c]
        o, s = refs[n_r + n_c:n_r + n_c + n_o], refs[n_r + n_c + n_o:]
        vals, tot = fn([x[...] for x in r], [x[...] for x in c])
        for ref, val in zip(o, vals):
            ref[...] = val.astype(ref.dtype)
        if n_s:
            @pl.when(pl.program_id(0) == 0)
            def _():
                for ref in s:
                    ref[...] = jnp.zeros_like(ref)

            for ref, val in zip(s, tot):
                ref[...] += val

    return pl.pallas_call(
        body, name=name,
        grid=(steps,),
        in_specs=[window(w, b) for _, w, b in rows] + [pl.BlockSpec(x.shape, lambda i: (0, 0)) for x in consts],
        out_specs=[pl.BlockSpec((tm, w), lambda i: (i, 0)) for w, _ in outs]
        + [pl.BlockSpec((1, w), lambda i: (0, 0)) for w in sums],
        out_shape=[jax.ShapeDtypeStruct((t, w), dt) for w, dt in outs]
        + [jax.ShapeDtypeStruct((1, w), F32) for w in sums],
        compiler_params=pltpu.CompilerParams(dimension_semantics=("arbitrary",)),
    )(*[x for x, _, _ in rows], *consts)


def _tile(dim, target):
    if dim <= target:
        return dim
    best = None
    for cand in range(128, target + 1, 128):
        if dim % cand == 0:
            best = cand
    assert best is not None, (dim, target)
    return best


def _matmul(name, a, b, mode, out_dtype=F32, tm=1024, tn=1024, tk=2048, carry=None):
    if mode == "nn":
        (m, k), n = a.shape, b.shape[1]
    elif mode == "nt":
        (m, k), n = a.shape, b.shape[0]
    else:
        (k, m), n = a.shape, b.shape[1]
    tm, tn, tk = _tile(m, tm), _tile(n, tn), _tile(k, tk)
    k_steps = k // tk
    product = {"nn": _nn, "nt": _nt, "tn": _tn}[mode]

    def body(a_ref, b_ref, o_ref, acc_ref):
        part = product(a_ref[...].astype(BF16), b_ref[...].astype(BF16))
        if k_steps == 1:
            o_ref[...] = part.astype(o_ref.dtype)
        else:
            kk = pl.program_id(2)

            @pl.when(kk == 0)
            def _():
                acc_ref[...] = part

            @pl.when(kk > 0)
            def _():
                acc_ref[...] += part

            @pl.when(kk == k_steps - 1)
            def _():
                o_ref[...] = acc_ref[...].astype(o_ref.dtype)

    a_spec = pl.BlockSpec((tk, tm), lambda i, j, kk: (kk, i)) if mode == "tn" else pl.BlockSpec((tm, tk), lambda i, j, kk: (i, kk))
    b_spec = pl.BlockSpec((tn, tk), lambda i, j, kk: (j, kk)) if mode == "nt" else pl.BlockSpec((tk, tn), lambda i, j, kk: (kk, j))
    (out,), carried = _call_carrying(
        body, carry, (a, b), name=name,
        grid=(m // tm, n // tn, k_steps),
        in_specs=[a_spec, b_spec],
        out_specs=[pl.BlockSpec((tm, tn), lambda i, j, kk: (i, j))],
        out_shape=[jax.ShapeDtypeStruct((m, n), out_dtype)],
        scratch_shapes=[pltpu.VMEM((tm, tn) if k_steps > 1 else (8, 128), F32)],
        compiler_params=pltpu.CompilerParams(dimension_semantics=("arbitrary", "arbitrary", "arbitrary")),
    )
    return out if carry is None else (out, carried)


FFN_TM = 512
FFN_BWD_TM = 256
FFN_BLOCKS = 4
FFN_GATE, FFN_UP, FFN_DOWN = 0, 1, 2


def _ffn_weight_specs(ffn_w, first):
    _, _, rows, dm = ffn_w.shape

    def spec(k):
        return pl.BlockSpec((FFN_BLOCKS, None, rows, dm), lambda i, j: (j, first + k, 0, 0))

    return [spec(FFN_GATE), spec(FFN_UP), spec(FFN_DOWN)], FFN_BLOCKS * rows


def _ffn_fwd(name, x, g_pre, ffn_w, first, g_post, carry=None):
    t, dm = x.shape
    tm = min(FFN_TM, t)
    w_specs, tf = _ffn_weight_specs(ffn_w, first)
    f_steps = N_DEV // FFN_BLOCKS

    def body(x_ref, gpre_ref, wg_ref, wu_ref, wd_ref, gpost_ref, h_ref, y_ref, hg_ref, hu_ref, xn_ref, acc_ref):
        j = pl.program_id(1)

        @pl.when(j == 0)
        def _():
            xn_ref[...] = _rms(x_ref[...], gpre_ref[...], dm).astype(BF16)
            acc_ref[...] = jnp.zeros_like(acc_ref)

        xn = xn_ref[...]
        wg, wu, wd = (r[...].reshape(tf, dm) for r in (wg_ref, wu_ref, wd_ref))
        hg, hu = _nt(xn, wg), _nt(xn, wu)
        hg_ref[...] = hg.astype(BF16)
        hu_ref[...] = hu.astype(BF16)
        a = _silu(hg) * hu
        acc_ref[...] += _nn(a.astype(BF16), wd)

        @pl.when(j == f_steps - 1)
        def _():
            h = acc_ref[...]
            h_ref[...] = h
            y_ref[...] = x_ref[...] + 0.5 * _rms(h, gpost_ref[...], dm)

    row = pl.BlockSpec((tm, dm), lambda i, j: (i, 0))
    vec = pl.BlockSpec((1, dm), lambda i, j: (0, 0))
    wide = pl.BlockSpec((tm, tf), lambda i, j: (i, j))
    return _call_carrying(
        body, carry, (x, g_pre, ffn_w, ffn_w, ffn_w, g_post), name=name,
        grid=(t // tm, f_steps),
        in_specs=[row, vec, *w_specs, vec],
        out_specs=[row, row, wide, wide],
        out_shape=[jax.ShapeDtypeStruct((t, dm), F32)] * 2 + [jax.ShapeDtypeStruct((t, f_steps * tf), BF16)] * 2,
        scratch_shapes=[pltpu.VMEM((tm, dm), BF16), pltpu.VMEM((tm, dm), F32)],
        compiler_params=pltpu.CompilerParams(dimension_semantics=("arbitrary", "arbitrary")),
    )


def _ffn_bwd(name, x, h, hg, hu, dy, g_pre, ffn_w, first, g_post, carry=None):
    t, dm = x.shape
    tm = min(FFN_BWD_TM, t)
    w_specs, tf = _ffn_weight_specs(ffn_w, first)
    f_steps = N_DEV // FFN_BLOCKS
    f = f_steps * tf

    def post(hv, g):
        return 0.5 * _rms(hv, g, dm)

    def pre(xv, g):
        return _rms(xv, g, dm)

    def body(x_ref, h_ref, dy_ref, hg_ref, hu_ref, gpre_ref, wg_ref, wu_ref, wd_ref, gpost_ref,
             dx_ref, xn_ref, dh_ref, a_ref, dhg_ref, dhu_ref, dgpre_ref, dgpost_ref, acc_ref):
        i, j = pl.program_id(0), pl.program_id(1)

        @pl.when((i == 0) & (j == 0))
        def _():
            dgpre_ref[...] = jnp.zeros_like(dgpre_ref)
            dgpost_ref[...] = jnp.zeros_like(dgpost_ref)

        @pl.when(j == 0)
        def _():
            xn_ref[...] = pre(x_ref[...], gpre_ref[...]).astype(BF16)
            _, pull = jax.vjp(post, h_ref[...], gpost_ref[...])
            dh, dg = pull(dy_ref[...])
            dh_ref[...] = dh.astype(BF16)
            dgpost_ref[...] += dg
            acc_ref[...] = jnp.zeros_like(acc_ref)

        wg, wu, wd = (r[...].reshape(tf, dm) for r in (wg_ref, wu_ref, wd_ref))
        hg, hu = hg_ref[...].astype(F32), hu_ref[...].astype(F32)
        da = _nt(dh_ref[...], wd)
        sig = _sigmoid(hg)
        act = hg * sig
        dhu = (da * act).astype(BF16)
        dhg = (da * hu * (sig * (1.0 + hg * (1.0 - sig)))).astype(BF16)
        a_ref[...] = (act * hu).astype(BF16)
        dhg_ref[...] = dhg
        dhu_ref[...] = dhu
        acc_ref[...] += _nn(dhg, wg) + _nn(dhu, wu)

        @pl.when(j == f_steps - 1)
        def _():
            _, pull = jax.vjp(pre, x_ref[...], gpre_ref[...])
            dx, dg = pull(acc_ref[...])
            dx_ref[...] = dy_ref[...] + dx
            dgpre_ref[...] += dg

    row = pl.BlockSpec((tm, dm), lambda i, j: (i, 0))
    vec = pl.BlockSpec((1, dm), lambda i, j: (0, 0))
    wide = pl.BlockSpec((tm, tf), lambda i, j: (i, j))
    return _call_carrying(
        body, carry, (x, h, dy, hg, hu, g_pre, ffn_w, ffn_w, ffn_w, g_post), name=name,
        grid=(t // tm, f_steps),
        in_specs=[row, row, row, wide, wide, vec, *w_specs, vec],
        out_specs=[row, row, row, wide, wide, wide, vec, vec],
        out_shape=[jax.ShapeDtypeStruct((t, dm), F32), jax.ShapeDtypeStruct((t, dm), BF16), jax.ShapeDtypeStruct((t, dm), BF16),
                   jax.ShapeDtypeStruct((t, f), BF16), jax.ShapeDtypeStruct((t, f), BF16), jax.ShapeDtypeStruct((t, f), BF16),
                   jax.ShapeDtypeStruct((1, dm), F32), jax.ShapeDtypeStruct((1, dm), F32)],
        scratch_shapes=[pltpu.VMEM((tm, dm), F32)],
        compiler_params=pltpu.CompilerParams(dimension_semantics=("arbitrary", "arbitrary")),
    )


ATT_T = 512
ATT_GROUP = 4
ATT_GROUP_FWD = 8
ATT_SCALE = (MLA_NOPE + MLA_ROPE) ** -0.5


def _stack_slots(ref, group):
    return jnp.stack([ref[:, pl.ds(j * SLOT, SLOT)] for j in range(group)])


def _unstack_slots(ref, val):
    for j in range(val.shape[0]):
        ref[:, pl.ds(j * SLOT, SLOT)] = val[j].astype(ref.dtype)


def _scores(q, k, diagonal):
    s = _nt(q, k) * ATT_SCALE
    if diagonal:
        row = lax.broadcasted_iota(jnp.int32, s.shape[1:], 0)
        col = lax.broadcasted_iota(jnp.int32, s.shape[1:], 1)
        s = jnp.where(col <= row, s, -1e30)
    return s


def _attn_pairs(steps, q_major):
    pairs = ([(qi, ki) for qi in range(steps) for ki in range(qi + 1)] if q_major
             else [(qi, ki) for ki in range(steps) for qi in range(ki, steps)])
    return jnp.array([p[0] for p in pairs], jnp.int32), jnp.array([p[1] for p in pairs], jnp.int32)


def _attn_specs(tile, group):
    width = group * SLOT
    return (pl.BlockSpec((tile, width), lambda h, p, qt, kt: (qt[p], h)),
            pl.BlockSpec((tile, width), lambda h, p, qt, kt: (kt[p], h)))


def _attn_fwd(q, k, v):
    t = q.shape[0]
    tile = min(ATT_T, t)
    steps = t // tile
    g = ATT_GROUP_FWD

    strip = min(SLOT, tile)

    def body(qt_ref, kt_ref, q_ref, k_ref, v_ref, o_ref, lse_ref, m_ref, l_ref, alpha_ref, acc_ref, s_ref, p_ref):
        qi, ki = qt_ref[pl.program_id(1)], kt_ref[pl.program_id(1)]

        @pl.when(ki == 0)
        def _():
            m_ref[...] = jnp.full_like(m_ref, -1e30)
            l_ref[...] = jnp.zeros_like(l_ref)
            acc_ref[...] = jnp.zeros_like(acc_ref)

        def step(diagonal):
            s_ref[...] = _nt(_stack_slots(k_ref, g), _stack_slots(q_ref, g))
            for j in range(tile // strip):
                c = pl.ds(j * strip, strip)
                s = s_ref[:, :, c] * ATT_SCALE
                if diagonal:
                    key = lax.broadcasted_iota(jnp.int32, s.shape[1:], 0)
                    query = lax.broadcasted_iota(jnp.int32, s.shape[1:], 1) + j * strip
                    s = jnp.where(key <= query, s, -1e30)
                m_old = m_ref[:, :, c]
                m_new = jnp.maximum(m_old, jnp.max(s, axis=1, keepdims=True))
                p = jnp.exp(s - m_new)
                alpha = jnp.exp(m_old - m_new)
                l_ref[:, :, c] = alpha * l_ref[:, :, c] + jnp.sum(p, axis=1, keepdims=True)
                alpha_ref[:, :, c] = alpha
                m_ref[:, :, c] = m_new
                p_ref[:, :, c] = p.astype(BF16)
            acc_ref[...] = acc_ref[...] * alpha_ref[...] + _tn(_stack_slots(v_ref, g), p_ref[...])

        @pl.when(ki < qi)
        def _():
            step(False)

        @pl.when(ki == qi)
        def _():
            step(True)
            out = acc_ref[...] / l_ref[...]
            lse = jnp.broadcast_to(m_ref[...] + jnp.log(l_ref[...]), out.shape)
            for j in range(g):
                o_ref[:, pl.ds(j * SLOT, SLOT)] = out[j].T
                lse_ref[:, pl.ds(j * SLOT, SLOT)] = lse[j].T

    q_spec, k_spec = _attn_specs(tile, g)
    tables = _attn_pairs(steps, True)
    return pl.pallas_call(
        body, name="attn_fwd",
        grid_spec=pltpu.PrefetchScalarGridSpec(
            num_scalar_prefetch=2, grid=(N_HEADS // g, tables[0].shape[0]),
            in_specs=[q_spec, k_spec, k_spec], out_specs=[q_spec, q_spec],
            scratch_shapes=[pltpu.VMEM((g, 1, tile), F32), pltpu.VMEM((g, 1, tile), F32), pltpu.VMEM((g, 1, tile), F32),
                            pltpu.VMEM((g, SLOT, tile), F32), pltpu.VMEM((g, tile, tile), F32), pltpu.VMEM((g, tile, tile), BF16)]),
        out_shape=[jax.ShapeDtypeStruct((t, N_HEADS * SLOT), F32)] * 2,
        compiler_params=pltpu.CompilerParams(dimension_semantics=("parallel", "arbitrary")),
    )(*tables, q, k, v)


def _attn_grad_scores(q, k, v, do, lse_ref, delta_ref, diagonal):
    g = ATT_GROUP
    p = jnp.exp(_scores(q, k, diagonal) - _stack_slots(lse_ref, g)[:, :, 0:1])
    dp = _nt(do, v)
    return p, p * (dp - _stack_slots(delta_ref, g)[:, :, 0:1]) * ATT_SCALE


def _attn_bwd(q, k, v, do, lse, delta):
    t = q.shape[0]
    tile = min(ATT_T, t)
    steps = t // tile
    g = ATT_GROUP

    def body(qt_ref, kt_ref, q_ref, k_ref, v_ref, do_ref, lse_ref, delta_ref, dq_ref, dk_ref, dv_ref, dk_acc, dv_acc):
        qi, ki = qt_ref[pl.program_id(1)], kt_ref[pl.program_id(1)]

        @pl.when(pl.program_id(1) == 0)
        def _():
            dq_ref[...] = jnp.zeros_like(dq_ref)

        def step(diagonal):
            qq, kk = _stack_slots(q_ref, g), _stack_slots(k_ref, g)
            do_b = _stack_slots(do_ref, g).astype(BF16)
            p, ds = _attn_grad_scores(qq, kk, _stack_slots(v_ref, g), do_b, lse_ref, delta_ref, diagonal)
            ds = ds.astype(BF16)
            dv_acc[...] += _tn(p.astype(BF16), do_b)
            dk_acc[...] += _tn(ds, qq)
            dq = _nn(ds, kk)
            rows = pl.ds(pl.multiple_of(qi * tile, tile), tile)
            for j in range(g):
                dq_ref[rows, pl.ds(j * SLOT, SLOT)] += dq[j]

        @pl.when(qi == ki)
        def _():
            dk_acc[...] = jnp.zeros_like(dk_acc)
            dv_acc[...] = jnp.zeros_like(dv_acc)
            step(True)

        @pl.when(qi > ki)
        def _():
            step(False)

        @pl.when(qi == steps - 1)
        def _():
            _unstack_slots(dk_ref, dk_acc[...])
            _unstack_slots(dv_ref, dv_acc[...])

    q_spec, k_spec = _attn_specs(tile, g)
    tables = _attn_pairs(steps, False)
    return pl.pallas_call(
        body, name="attn_bwd",
        grid_spec=pltpu.PrefetchScalarGridSpec(
            num_scalar_prefetch=2, grid=(N_HEADS // g, tables[0].shape[0]),
            in_specs=[q_spec, k_spec, k_spec, q_spec, q_spec, q_spec],
            out_specs=[pl.BlockSpec((t, g * SLOT), lambda h, p, qt, kt: (0, h)), k_spec, k_spec],
            scratch_shapes=[pltpu.VMEM((g, tile, SLOT), F32), pltpu.VMEM((g, tile, SLOT), F32)]),
        out_shape=[jax.ShapeDtypeStruct((t, N_HEADS * SLOT), F32)] * 3,
        compiler_params=pltpu.CompilerParams(dimension_semantics=("parallel", "arbitrary")),
    )(*tables, q, k, v, do, lse, delta)


CONV_PAD = 8


def _fill_padded(ref, val):
    t = val.shape[0]
    zeros = jnp.zeros((CONV_PAD, val.shape[1]), val.dtype)
    ref[pl.ds(0, CONV_PAD)] = zeros
    ref[pl.ds(CONV_PAD + t, CONV_PAD)] = zeros
    ref[pl.ds(CONV_PAD, t)] = val


def _shifted(ref, s):
    return ref[pl.ds(CONV_PAD - s, ref.shape[0] - 2 * CONV_PAD)]


def _l2norm(x):
    return x * lax.rsqrt(jnp.sum(x * x, axis=-1, keepdims=True) + EPS)


def _conv_pre(x_pad, w):
    y = w[GDN_CONV - 1:GDN_CONV, :] * _shifted(x_pad, 0)
    for s in range(1, GDN_CONV):
        y = y + w[GDN_CONV - 1 - s:GDN_CONV - s, :] * _shifted(x_pad, s)
    return y


def _gdn_conv_fwd(x, w):
    t, width = x.shape

    def body(x_ref, w_ref, o_ref, x_pad):
        _fill_padded(x_pad, x_ref[...])
        act = _silu(_conv_pre(x_pad, w_ref[...]))
        normed = pl.program_id(0) < 2 * N_HEADS
        o_ref[...] = jnp.where(normed, _l2norm(act), act)

    return pl.pallas_call(
        body, name="gdn_conv_fwd",
        grid=(width // SLOT,),
        in_specs=[pl.BlockSpec((t, SLOT), lambda j: (0, j)), pl.BlockSpec((GDN_CONV, SLOT), lambda j: (0, j))],
        out_specs=pl.BlockSpec((t, SLOT), lambda j: (0, j)),
        out_shape=jax.ShapeDtypeStruct((t, width), F32),
        scratch_shapes=[pltpu.VMEM((t + 2 * CONV_PAD, SLOT), F32)],
        compiler_params=pltpu.CompilerParams(dimension_semantics=("parallel",)),
    )(x, w)


def _gdn_conv_bwd(x, w, dout):
    t, width = x.shape

    def body(x_ref, w_ref, do_ref, dx_ref, dw_ref, x_pad, dy_pad):
        wv = w_ref[...]
        _fill_padded(x_pad, x_ref[...])
        y = _conv_pre(x_pad, wv)
        sig = _sigmoid(y)
        act = y * sig
        _, pull = jax.vjp(_l2norm, act)
        normed = pl.program_id(0) < 2 * N_HEADS
        dact = jnp.where(normed, pull(do_ref[0])[0], do_ref[0])
        dy = dact * (sig * (1.0 + y * (1.0 - sig)))
        _fill_padded(dy_pad, dy)
        dx = wv[GDN_CONV - 1:GDN_CONV, :] * dy
        for s in range(1, GDN_CONV):
            dx = dx + wv[GDN_CONV - 1 - s:GDN_CONV - s, :] * _shifted(dy_pad, -s)
        dx_ref[...] = dx.astype(BF16)
        for s in range(GDN_CONV):
            dw_ref[GDN_CONV - 1 - s:GDN_CONV - s, :] = jnp.sum(dy * _shifted(x_pad, s), axis=0, keepdims=True)

    col = pl.BlockSpec((t, SLOT), lambda j: (0, j))
    tap = pl.BlockSpec((GDN_CONV, SLOT), lambda j: (0, j))
    return pl.pallas_call(
        body, name="gdn_conv_bwd",
        grid=(width // SLOT,),
        in_specs=[col, tap, pl.BlockSpec((1, t, SLOT), lambda j: (j // N_HEADS, 0, j % N_HEADS))],
        out_specs=[col, tap],
        out_shape=[jax.ShapeDtypeStruct((t, width), BF16), jax.ShapeDtypeStruct((GDN_CONV, width), F32)],
        scratch_shapes=[pltpu.VMEM((t + 2 * CONV_PAD, SLOT), F32)] * 2,
        compiler_params=pltpu.CompilerParams(dimension_semantics=("parallel",)),
    )(x, w, dout)


def _softplus(x):
    e = jnp.exp(-jnp.abs(x))
    u = 1.0 + e
    log1p = jnp.where(u == 1.0, e, jnp.log(u) * e / jnp.where(u == 1.0, 1.0, u - 1.0))
    return jnp.maximum(x, 0.0) + log1p


def _chunk_running_sum(x, reverse=False):
    tm = x.shape[0]
    at = lax.broadcasted_iota(jnp.int32, x.shape, 0) % GDN_CHUNK
    step = 1
    while step < GDN_CHUNK:
        if reverse:
            x = x + jnp.where(at < GDN_CHUNK - step, pltpu.roll(x, tm - step, 0), 0.0)
        else:
            x = x + jnp.where(at >= step, pltpu.roll(x, step, 0), 0.0)
        step *= 2
    return x


def _gates_fwd(ab, a_log, dt_bias):
    def fn(rows, consts):
        (abv,), (alog, dtb) = rows, consts
        g = _chunk_running_sum(-jnp.exp(alog) * _softplus(abv + dtb))
        beta = _sigmoid(abv)
        shape = (abv.shape[0], SLOT)
        g_slots = [jnp.broadcast_to(g[:, h:h + 1], shape) for h in range(N_HEADS)]
        b_slots = [jnp.broadcast_to(beta[:, N_HEADS + h:N_HEADS + h + 1], shape) for h in range(N_HEADS)]
        return [jnp.concatenate(g_slots, axis=1), jnp.concatenate(b_slots, axis=1)], []

    width = N_HEADS * SLOT
    return _rowwise("gdn_gates_fwd", fn, [ab], [a_log, dt_bias], [(width, F32), (width, F32)])


def _gates_bwd(ab, a_log, dt_bias, dg, dbeta):
    def fn(rows, consts):
        (abv, dgv, dbv), (alog, dtb) = rows, consts
        lane = lax.broadcasted_iota(jnp.int32, abv.shape, 1)
        dg_tok = jnp.zeros_like(abv)
        db_tok = jnp.zeros_like(abv)
        for h in range(N_HEADS):
            dg_tok = dg_tok + jnp.where(lane == h, jnp.sum(dgv[:, h * SLOT:(h + 1) * SLOT], axis=1, keepdims=True), 0.0)
            db_tok = db_tok + jnp.where(lane == N_HEADS + h, jnp.sum(dbv[:, h * SLOT:(h + 1) * SLOT], axis=1, keepdims=True), 0.0)
        dg_tok = _chunk_running_sum(dg_tok, reverse=True)
        xa = abv + dtb
        g = -jnp.exp(alog) * _softplus(xa)
        da = dg_tok * (-jnp.exp(alog)) * _sigmoid(xa)
        beta = _sigmoid(abv)
        dab = jnp.where(lane < N_HEADS, da, db_tok * beta * (1.0 - beta))
        dab = jnp.where(lane < 2 * N_HEADS, dab, 0.0)
        d_alog = jnp.sum(jnp.where(lane < N_HEADS, dg_tok * g, 0.0), axis=0, keepdims=True)
        d_dtb = jnp.sum(jnp.where(lane < N_HEADS, da, 0.0), axis=0, keepdims=True)
        return [dab], [d_alog, d_dtb]

    return _rowwise("gdn_gates_bwd", fn, [ab, dg, dbeta], [a_log, dt_bias], [(SLOT, F32)], sums=[SLOT, SLOT])


ROPE_HALF = MLA_ROPE // 2


def _rope_tables(positions):
    freqs = ROPE_THETA ** (-jnp.arange(ROPE_HALF, dtype=F32) / ROPE_HALF)
    ang = positions.astype(F32)[:, None] * freqs
    cos, sin = jnp.cos(ang), jnp.sin(ang)
    t = positions.shape[0]
    ones, zeros = jnp.ones((t, MLA_NOPE), F32), jnp.zeros((t, MLA_NOPE), F32)
    tail = jnp.zeros((t, SLOT - MLA_NOPE - MLA_ROPE), F32)
    half0 = jnp.zeros((t, ROPE_HALF), F32)
    same = jnp.concatenate([ones, cos, cos, tail], axis=1)
    from_low = jnp.concatenate([zeros, half0, sin, tail], axis=1)
    from_high = jnp.concatenate([zeros, -sin, half0, tail], axis=1)
    return same, from_low, from_high


def _rope(x, tabs):
    same, from_low, from_high = tabs
    width = x.shape[1]
    return x * same + pltpu.roll(x, ROPE_HALF, 1) * from_low + pltpu.roll(x, width - ROPE_HALF, 1) * from_high


def _rope_transposed(dy, tabs):
    same, from_low, from_high = tabs
    width = dy.shape[1]
    return dy * same + pltpu.roll(dy * from_low, width - ROPE_HALF, 1) + pltpu.roll(dy * from_high, ROPE_HALF, 1)


def _tile_slots(tab):
    return jnp.concatenate([tab] * N_HEADS, axis=1)


A_WIDTH = MLA_Q_RANK + MLA_KV_RANK + 2 * SLOT
A_KPE = MLA_Q_RANK + MLA_KV_RANK
A_AB = A_KPE + SLOT
WIDE = N_HEADS * SLOT


def _mla_front_fwd(proj_a, tabs, g_q, g_kv, w_uq, w_kv):
    def fn(rows, consts):
        pa, *tb = rows
        gq, gkv, wuq, wkv = consts
        cqn = _rms(pa[:, :MLA_Q_RANK], gq, MLA_Q_RANK).astype(BF16)
        ckvn = _rms(pa[:, MLA_Q_RANK:A_KPE], gkv, MLA_KV_RANK).astype(BF16)
        kv = _nt(ckvn, wkv)
        q = _rope(_nt(cqn, wuq), [_tile_slots(x) for x in tb])
        k = kv[:, :WIDE] + _tile_slots(_rope(pa[:, A_KPE:A_AB], tb))
        return [cqn, ckvn, q, k, kv[:, WIDE:]], []

    return _rowwise("mla_front_fwd", fn, [proj_a, *tabs], [g_q, g_kv, w_uq, w_kv],
                    [(MLA_Q_RANK, BF16), (MLA_KV_RANK, BF16)] + [(WIDE, BF16)] * 3)


def _mla_front_bwd(proj_a, tabs, g_q, g_kv, w_uq, w_kv, dq, dk, dv, dab):
    def fn(rows, consts):
        pa, t0, t1, t2, dqv, dkv, dvv, da = rows
        gq, gkv, wuq, wkv = consts
        tb = (t0, t1, t2)
        dq_p = _rope_transposed(dqv, [_tile_slots(x) for x in tb]).astype(BF16)
        dkv_p = jnp.concatenate([dkv, dvv], axis=1).astype(BF16)
        dkpe = dkv[:, :SLOT]
        for h in range(1, N_HEADS):
            dkpe = dkpe + dkv[:, h * SLOT:(h + 1) * SLOT]
        _, pull_q = jax.vjp(lambda x, g: _rms(x, g, MLA_Q_RANK), pa[:, :MLA_Q_RANK], gq)
        _, pull_kv = jax.vjp(lambda x, g: _rms(x, g, MLA_KV_RANK), pa[:, MLA_Q_RANK:A_KPE], gkv)
        dcq, dgq = pull_q(_nn(dq_p, wuq))
        dckv, dgkv = pull_kv(_nn(dkv_p, wkv))
        return [jnp.concatenate([dcq, dckv, _rope_transposed(dkpe, tb), da], axis=1), dq_p, dkv_p], [dgq, dgkv]

    return _rowwise("mla_front_bwd", fn, [proj_a, *tabs, dq, dk, dv, dab], [g_q, g_kv, w_uq, w_kv],
                    [(A_WIDTH, BF16), (WIDE, BF16), (2 * WIDE, BF16)], sums=[MLA_Q_RANK, MLA_KV_RANK])


def _slot_sum(x):
    parts = [jnp.broadcast_to(jnp.sum(x[:, h * SLOT:(h + 1) * SLOT], axis=1, keepdims=True), (x.shape[0], SLOT))
             for h in range(N_HEADS)]
    return jnp.concatenate(parts, axis=1)


def _mix_join(o_mla, o_gdn, gate, g_mla, g_gdn):
    mla = _rms(o_mla, g_mla, N_HEADS * MLA_V)
    gdn = o_gdn * lax.rsqrt(_slot_sum(o_gdn * o_gdn) * (1.0 / GDN_D) + EPS) * g_gdn * _silu(gate)
    return mla, gdn


MIX_TM = 256


def _mix_fwd(o_mla, o_gdn, gate, x, g_mla, g_gdn, w_out, g_post):
    dm = x.shape[1]

    def fn(rows, consts):
        om, og, gt, xv = rows
        gm, gg, wo, gp = consts
        cat = jnp.concatenate(_mix_join(om, og, gt, gm, gg), axis=1).astype(BF16)
        mixed = _nn(cat, wo)
        return [cat, mixed, xv + _rms(mixed, gp, dm)], []

    return _rowwise("mix_fwd", fn, [o_mla, o_gdn, gate, x], [g_mla, g_gdn, w_out, g_post],
                    [(2 * WIDE, BF16), (dm, F32), (dm, F32)], tm=MIX_TM)


def _mix_bwd(o_mla, o_gdn, gate, mixed, dy, g_mla, g_gdn, w_out, g_post):
    dm = mixed.shape[1]

    def fn(rows, consts):
        om, og, gt, mx, dyv = rows
        gm, gg, wo, gp = consts
        _, pull_post = jax.vjp(lambda hv, gv: _rms(hv, gv, dm), mx, gp)
        dmixed, dgp = pull_post(dyv)
        dmixed = dmixed.astype(BF16)
        dc = _nt(dmixed, wo)
        _, pull = jax.vjp(lambda x, g: _rms(x, g, N_HEADS * MLA_V), om, gm)
        dom, dgm = pull(dc[:, :WIDE])
        dn_out = dc[:, WIDE:]
        r = lax.rsqrt(_slot_sum(og * og) * (1.0 / GDN_D) + EPS)
        sig = _sigmoid(gt)
        normed = og * r
        dn = dn_out * gg * (gt * sig)
        dog = r * dn - normed * (r * r) * _slot_sum(dn * og) * (1.0 / GDN_D)
        dgt = dn_out * normed * gg * (sig * (1.0 + gt * (1.0 - sig)))
        dgg = jnp.sum(dn_out * normed * (gt * sig), axis=0, keepdims=True)
        return [dmixed, dom, _slot_sum(dom * om), dog, dgt], [dgp, dgm, dgg]

    return _rowwise("mix_bwd", fn, [o_mla, o_gdn, gate, mixed, dy], [g_mla, g_gdn, w_out, g_post],
                    [(dm, BF16), (WIDE, F32), (WIDE, F32), (WIDE, F32), (WIDE, BF16)], sums=[dm, WIDE, WIDE], tm=MIX_TM)


def _proj_fwd(x, g, weights):
    dm = x.shape[1]

    def fn(rows, consts):
        hn = _rms(rows[0], consts[0], dm).astype(BF16)
        return [hn] + [_nt(hn, wv) for wv in consts[1:]], []

    return _rowwise("proj_fwd", fn, [x], [g, *weights], [(dm, BF16)] + [(wv.shape[0], F32) for wv in weights], tm=MIX_TM)


def _proj_bwd(x, g, weights, cots, dy):
    dm = x.shape[1]
    n = len(weights)

    def fn(rows, consts):
        xv, dyv, *parts = rows
        dn = _nn(parts[0], consts[1])
        for p, wv in zip(parts[1:], consts[2:]):
            dn = dn + _nn(p, wv)
        _, pull = jax.vjp(lambda a, gv: _rms(a, gv, dm), xv, consts[0])
        dx, dg = pull(dn)
        return [dyv + dx], [dg]

    assert len(cots) == n
    return _rowwise("proj_bwd", fn, [x, dy, *cots], [g, *weights], [(dm, F32)], sums=[dm], tm=MIX_TM)


def _loss_fwd(y, target):
    dm = y.shape[1]

    def fn(rows, consts):
        err = rows[0] - rows[1]
        sq = err * err
        lanes = sq[:, :SLOT]
        for j in range(1, dm // SLOT):
            lanes = lanes + sq[:, j * SLOT:(j + 1) * SLOT]
        return [err * (1.0 / dm)], [jnp.sum(lanes, axis=0, keepdims=True) * (0.5 / dm)]

    return _rowwise("loss", fn, [y, target], [], [(dm, F32)], sums=[SLOT])


W_IN_CUTS = (0, 256, 384, 416, 1952, 1960, 1968, 2480)


def _heads_out(w, per_head, axis=-1):
    axis = axis % w.ndim
    shape = w.shape
    n = shape[axis] // per_head
    w = w.reshape(shape[:axis] + (n, per_head) + shape[axis + 1:])
    pad = [(0, 0)] * w.ndim
    pad[axis + 1] = (0, SLOT - per_head)
    return jnp.pad(w, pad).reshape(shape[:axis] + (n * SLOT,) + shape[axis + 1:])


def _heads_in(w, per_head, axis=-1):
    axis = axis % w.ndim
    shape = w.shape
    n = shape[axis] // SLOT
    w = w.reshape(shape[:axis] + (n, SLOT) + shape[axis + 1:])
    w = lax.slice_in_dim(w, 0, per_head, axis=axis + 1)
    return w.reshape(shape[:axis] + (n * per_head,) + shape[axis + 1:])


def _pad_lanes(v, lo, width=SLOT):
    return jnp.pad(v, [(0, 0)] * (v.ndim - 1) + [(lo, width - lo - v.shape[-1])])


def _pad_rows(v, lo, rows=SLOT):
    return jnp.pad(v, [(lo, rows - lo - v.shape[0])] + [(0, 0)] * (v.ndim - 1))


def _layout_weights(w):
    c = W_IN_CUTS
    w_in = w["w_in_t"]
    p = {}
    p["w_a"] = jnp.concatenate([w_in[c[0]:c[2]], _pad_rows(w_in[c[2]:c[3]], MLA_NOPE), _pad_rows(w_in[c[4]:c[6]], 0)], axis=0)
    p["w_qkv"] = _heads_out(w_in[c[3]:c[4]], GDN_D, axis=0)
    p["w_gate"] = _heads_out(w_in[c[6]:c[7]], GDN_D, axis=0)
    p["w_uq"] = _heads_out(w["uq_t"], MLA_NOPE + MLA_ROPE, axis=0)
    ukv = w["ukv_t"].reshape(N_HEADS, MLA_NOPE + MLA_V, MLA_KV_RANK)
    p["w_kv"] = jnp.concatenate([_heads_out(ukv[:, :MLA_NOPE].reshape(-1, MLA_KV_RANK), MLA_NOPE, axis=0),
                                 _heads_out(ukv[:, MLA_NOPE:].reshape(-1, MLA_KV_RANK), MLA_V, axis=0)], axis=0)
    p["conv"] = _heads_out(w["gdn_conv_w"], GDN_D)
    p["g_mla_out"] = _heads_out(w["mla_out_g"], MLA_V)
    p["g_gdn"] = jnp.tile(_pad_lanes(w["gdn_norm_g"], 0), (1, N_HEADS))
    p["a_log"] = _pad_lanes(w["gdn_a_log"], 0)
    p["dt_bias"] = _pad_lanes(w["gdn_dt_bias"], 0)
    return p


def _unlayout_grads(d):
    c = W_IN_CUTS
    g = {}
    da = d["w_a"]
    kpe0 = A_KPE + MLA_NOPE
    g["w_in_t"] = jnp.concatenate([da[:A_KPE], da[kpe0:kpe0 + MLA_ROPE], _heads_in(d["w_qkv"], GDN_D, axis=0),
                                   da[A_AB:A_AB + 2 * N_HEADS], _heads_in(d["w_gate"], GDN_D, axis=0)], axis=0)
    assert g["w_in_t"].shape[0] == c[-1]
    g["uq_t"] = _heads_in(d["w_uq"], MLA_NOPE + MLA_ROPE, axis=0)
    dk = _heads_in(d["w_kv"][:WIDE], MLA_NOPE, axis=0).reshape(N_HEADS, MLA_NOPE, MLA_KV_RANK)
    dv = _heads_in(d["w_kv"][WIDE:], MLA_V, axis=0).reshape(N_HEADS, MLA_V, MLA_KV_RANK)
    g["ukv_t"] = jnp.concatenate([dk, dv], axis=1).reshape(-1, MLA_KV_RANK)
    g["w_out"] = _heads_in(d["w_out"], GDN_D, axis=0)
    g["gdn_conv_w"] = _heads_in(d["conv"], GDN_D)
    g["mla_out_g"] = _heads_in(d["g_mla_out"], MLA_V)
    g["gdn_norm_g"] = jnp.sum(d["g_gdn"].reshape(N_HEADS, SLOT), axis=0, keepdims=True)[:, :GDN_D]
    g["gdn_a_log"] = d["a_log"][:, :N_HEADS]
    g["gdn_dt_bias"] = d["dt_bias"][:, :N_HEADS]
    return g


def _weight_grad(name, cots, acts, out_dtype=F32, tm=1024, tn=1024, tk=2048, carry=None):
    return _matmul(name, cots, acts, "tn", out_dtype=out_dtype, tm=tm, tn=tn, tk=tk, carry=carry)


def _by_device(a):
    return a.astype(BF16).reshape((N_DEV, a.shape[0] // N_DEV) + a.shape[1:])


def _rows_of(blocks):
    return blocks.reshape((-1,) + blocks.shape[2:])


def _local_step(x, positions, target, w, mid, late):
    tabs = _rope_tables(positions)

    (h1, x1, hg1, hu1), gathered = _ffn_fwd("ffn1_fwd", x, w["ffn1_pre_g"], w["ffn1"], 0, w["ffn1_post_g"], carry=mid)
    w = dict(w, w_in_t=_rows_of(gathered[0]), uq_t=_rows_of(gathered[1]), ukv_t=_rows_of(gathered[2]))
    p = _layout_weights(w)
    in_weights = [p["w_a"], p["w_qkv"], p["w_gate"]]
    hn, proj_a, proj_qkv, proj_gate = _proj_fwd(x1, w["mix_pre_g"], in_weights)
    cqn, ckvn, q, k, v = _mla_front_fwd(proj_a, tabs, w["mla_q_norm_g"], w["mla_kv_norm_g"], p["w_uq"], p["w_kv"])
    o_mla, lse = _attn_fwd(q, k, v)
    ab = (proj_a, SLOT, A_AB // SLOT)
    qkv_n = _gdn_conv_fwd(proj_qkv, p["conv"])
    gb, bb = _gates_fwd(ab, p["a_log"], p["dt_bias"])
    (o_gdn, keep), (ffn2, w_out) = _gdn_fwd(qkv_n, gb, bb, carry=late)
    p["w_out"] = _heads_out(_rows_of(w_out), GDN_D, axis=0)
    cat, mixed, x2 = _mix_fwd(o_mla, o_gdn, proj_gate, x1, p["g_mla_out"], p["g_gdn"], p["w_out"], w["mix_post_g"])
    (h2, y, hg2, hu2), _ = _ffn_fwd("ffn2_fwd", x2, w["ffn2_pre_g"], ffn2, 0, w["ffn2_post_g"])
    dy, loss_lanes = _loss_fwd(y, target)

    g = {}
    (dx2, xn2, dh2, a2, dhg2, dhu2, g["ffn2_pre_g"], g["ffn2_post_g"]), _ = _ffn_bwd(
        "ffn2_bwd", x2, h2, hg2, hu2, dy, w["ffn2_pre_g"], ffn2, 0, w["ffn2_post_g"])
    ffn2_grads = _Scatter([_by_device(_weight_grad("ffn2_dw_gate", dhg2, xn2, BF16, tm=1408)),
                           _by_device(_weight_grad("ffn2_dw_up", dhu2, xn2, BF16, tm=1408)),
                           _by_device(_weight_grad("ffn2_dw_down", a2, dh2, BF16, tm=1408))])
    d = {}
    dmixed, do_mla, delta, do_gdn, dgate, g["mix_post_g"], d["g_mla_out"], d["g_gdn"] = _mix_bwd(
        o_mla, o_gdn, proj_gate, mixed, dx2, p["g_mla_out"], p["g_gdn"], p["w_out"], w["mix_post_g"])
    d["w_out"] = _weight_grad("mix_out_dw", cat, dmixed)
    dq, dk, dv = _attn_bwd(q, k, v, do_mla, lse, delta)
    (dqkv_n, dgb, dbb), landed_ffn2 = _gdn_bwd(qkv_n, gb, bb, keep, do_gdn, carry=ffn2_grads)
    dab, d["a_log"], d["dt_bias"] = _gates_bwd(ab, p["a_log"], p["dt_bias"], dgb, dbb)
    dproj_qkv, d["conv"] = _gdn_conv_bwd(proj_qkv, p["conv"], dqkv_n)
    dproj_a, dq_p, dkv_p, g["mla_q_norm_g"], g["mla_kv_norm_g"] = _mla_front_bwd(
        proj_a, tabs, w["mla_q_norm_g"], w["mla_kv_norm_g"], p["w_uq"], p["w_kv"], dq, dk, dv, dab)
    d["w_uq"] = _weight_grad("mla_q_dw", dq_p, cqn)
    d["w_kv"] = _weight_grad("mla_kv_dw", dkv_p, ckvn)
    d["w_a"] = _weight_grad("proj_a_dw", dproj_a, hn, tm=640)
    d["w_qkv"] = _weight_grad("proj_qkv_dw", dproj_qkv, hn)
    d["w_gate"] = _weight_grad("proj_gate_dw", dgate, hn)
    dx1, g["mix_pre_g"] = _proj_bwd(x1, w["mix_pre_g"], in_weights, [dproj_a, dproj_qkv, dgate], dx2)
    g.update(_unlayout_grads(d))
    others = list(OTHER.values())
    (dx, xn1, dh1, a1, dhg1, dhu1, g["ffn1_pre_g"], g["ffn1_post_g"]), landed_others = _ffn_bwd(
        "ffn1_bwd", x, h1, hg1, hu1, dx1, w["ffn1_pre_g"], w["ffn1"], 0, w["ffn1_post_g"], carry=_Scatter([_by_device(g.pop(t)) for t in others]))
    dw_down = _weight_grad("ffn1_dw_down", a1, dh1, BF16, tm=1408)
    dw_gate, (landed_down,) = _weight_grad("ffn1_dw_gate", dhg1, xn1, BF16, tm=1408, carry=_Scatter([_by_device(dw_down)]))
    dw_up, (landed_gate,) = _weight_grad("ffn1_dw_up", dhu1, xn1, BF16, tm=1408, carry=_Scatter([_by_device(dw_gate)]))
    (landed_up,) = _exchange("scatter_last", _Scatter([_by_device(dw_up)]))
    landed = dict(zip(list(FFN_NAMES) + list(OTHER),
                      [landed_gate, landed_up, landed_down] + list(landed_ffn2) + list(landed_others)))
    return loss_lanes, dx, g, landed


MESH_AXES = ("x", "y", "c")
N_LINKS = N_DEV - 1


def _place():
    return tuple(lax.axis_index(a) for a in MESH_AXES)


def _block_of(dev):
    x, y, c = dev
    return 4 * x + 2 * y + c


def _remote_copy(src, dst, sems, k, to):
    send_sems, recv_sems = sems
    return pltpu.make_async_remote_copy(src_ref=src, dst_ref=dst, send_sem=send_sems.at[k], recv_sem=recv_sems.at[k],
                                        device_id=to, device_id_type=pl.DeviceIdType.MESH)


class _Exchange:
    def __init__(self, arrays):
        self.arrays = list(arrays)
        self.n = len(self.arrays)
        self.specs = [pl.BlockSpec(memory_space=pl.ANY)] * self.n
        self.scratch = [pltpu.SemaphoreType.DMA((self.n * N_LINKS,)), pltpu.SemaphoreType.DMA((self.n * N_LINKS,)),
                        pltpu.SemaphoreType.DMA((self.n,))]

    def split(self, refs):
        n = self.n
        return refs[:n], refs[n:2 * n], (refs[2 * n], refs[2 * n + 1]), refs[2 * n + 2]


class _Gather(_Exchange):
    def out_shape(self):
        return [jax.ShapeDtypeStruct((N_DEV,) + a.shape, a.dtype) for a in self.arrays]

    def _plan(self, ins, outs, sems, local_sems):
        x, y, c = _place()
        me, sibling = (x, y, c), (x, y, 1 - c)
        chips = [(1 - x, y), (x, 1 - y), (1 - x, 1 - y)]

        def copy(a, k, block, to, mine=False):
            src = ins[a] if mine else outs[a].at[_block_of(block)]
            return _remote_copy(src, outs[a].at[_block_of(block)], sems, a * N_LINKS + k, to)

        local = [pltpu.make_async_copy(ins[a], outs[a].at[_block_of(me)], local_sems.at[a]) for a in range(self.n)]
        first = []
        for a in range(self.n):
            first.append(copy(a, 0, me, sibling, mine=True))
            first += [copy(a, 1 + j, me, (*chip, c), mine=True) for j, chip in enumerate(chips)]
        return me, sibling, chips, c, copy, local, first

    def start(self, ins, outs, sems, local_sems):
        *_, local, first = self._plan(ins, outs, sems, local_sems)
        for cp in local + first:
            cp.start()

    def finish(self, ins, outs, sems, local_sems):
        me, sibling, chips, c, copy, local, first = self._plan(ins, outs, sems, local_sems)
        passed = []
        for j, chip in enumerate(chips):
            for a in range(self.n):
                copy(a, 1 + j, (*chip, c), me).wait_recv()
                passed.append(copy(a, 4 + j, (*chip, c), sibling))
                passed[-1].start()
        for a in range(self.n):
            copy(a, 0, sibling, me).wait_recv()
            for j, chip in enumerate(chips):
                copy(a, 4 + j, (*chip, 1 - c), me).wait_recv()
        for cp in first + passed:
            cp.wait_send()
        for cp in local:
            cp.wait()


class _Scatter(_Exchange):
    def out_shape(self):
        return [jax.ShapeDtypeStruct(a.shape, a.dtype) for a in self.arrays]

    def _plan(self, ins, outs, sems, local_sems):
        x, y, c = _place()
        me = _block_of((x, y, c))

        def peer(r):
            return (1 - x if r & 4 else x, 1 - y if r & 2 else y, 1 - c if r & 1 else c)

        local = [pltpu.make_async_copy(ins[a].at[me], outs[a].at[me], local_sems.at[a]) for a in range(self.n)]
        sends = [_remote_copy(ins[a].at[_block_of(peer(r))], outs[a].at[me], sems, a * N_LINKS + r - 1, peer(r))
                 for a in range(self.n) for r in range(1, N_DEV)]
        arrivals = [_remote_copy(ins[a].at[me], outs[a].at[_block_of(peer(r))], sems, a * N_LINKS + r - 1, peer(r))
                    for a in range(self.n) for r in range(1, N_DEV)]
        return local, sends, arrivals

    def start(self, ins, outs, sems, local_sems):
        local, sends, _ = self._plan(ins, outs, sems, local_sems)
        for cp in local + sends:
            cp.start()

    def finish(self, ins, outs, sems, local_sems):
        local, sends, arrivals = self._plan(ins, outs, sems, local_sems)
        for cp in arrivals:
            cp.wait_recv()
        for cp in sends:
            cp.wait_send()
        for cp in local:
            cp.wait()


def _exchange(name, plan):
    def body(*refs):
        parts = plan.split(refs)
        plan.start(*parts)
        plan.finish(*parts)

    return pl.pallas_call(
        body, name=name,
        in_specs=plan.specs,
        out_specs=plan.specs,
        out_shape=plan.out_shape(),
        scratch_shapes=plan.scratch,
    )(*plan.arrays)


def _call_carrying(body, plan, operands, *, name, grid, in_specs, out_specs, out_shape, scratch_shapes, compiler_params):
    if plan is None:
        outs = pl.pallas_call(body, name=name, grid=grid, in_specs=in_specs, out_specs=out_specs, out_shape=out_shape,
                              scratch_shapes=scratch_shapes, compiler_params=compiler_params)(*operands)
        return outs, []
    n_i, n_o, n_s, k = len(in_specs), len(out_specs), len(scratch_shapes), plan.n

    def whole(*refs):
        cut = [n_i, n_i + k, n_i + k + n_o, n_i + 2 * k + n_o, n_i + 2 * k + n_o + n_s]
        own_in, ex_in, own_out, ex_out, own_scr, ex_scr = (refs[a:b] for a, b in zip([0] + cut, cut + [len(refs)]))
        parts = plan.split(ex_in + ex_out + ex_scr)
        first = last = True
        for axis, size in enumerate(grid):
            first = first & (pl.program_id(axis) == 0)
            last = last & (pl.program_id(axis) == size - 1)

        @pl.when(first)
        def _():
            plan.start(*parts)

        body(*own_in, *own_out, *own_scr)

        @pl.when(last)
        def _():
            plan.finish(*parts)

    outs = pl.pallas_call(
        whole, name=name, grid=grid,
        in_specs=list(in_specs) + plan.specs, out_specs=list(out_specs) + plan.specs,
        out_shape=list(out_shape) + plan.out_shape(), scratch_shapes=list(scratch_shapes) + plan.scratch,
        compiler_params=compiler_params,
    )(*operands, *plan.arrays)
    return outs[:n_o], outs[n_o:]


def _row_tile(rows, target=256):
    best = rows
    for cand in range(16, min(rows, target) + 1, 16):
        if rows % cand == 0:
            best = cand
    return best


def _sum_blocks(name, blocks):
    rows, width = blocks.shape[-2:]
    tm = _row_tile(rows)

    def body(x_ref, o_ref):
        acc = x_ref[0].astype(F32)
        for d in range(1, N_DEV):
            acc = acc + x_ref[d].astype(F32)
        o_ref[...] = acc

    return pl.pallas_call(
        body, name=name,
        grid=(rows // tm,),
        in_specs=[pl.BlockSpec((N_DEV, tm, width), lambda i: (0, i, 0))],
        out_specs=pl.BlockSpec((tm, width), lambda i: (i, 0)),
        out_shape=jax.ShapeDtypeStruct((rows, width), F32),
        compiler_params=pltpu.CompilerParams(dimension_semantics=("parallel",)),
    )(blocks)


def _all_reduce_small(name, vec):
    rows, width = vec.shape

    def body(x_ref, o_ref, all_ref, send_sems, recv_sems):
        x, y, c = _place()
        me = _block_of((x, y, c))
        all_ref[me] = x_ref[...]

        def peer(r):
            return (1 - x if r & 4 else x, 1 - y if r & 2 else y, 1 - c if r & 1 else c)

        def copy(r, block):
            return _remote_copy(x_ref, all_ref.at[block], (send_sems, recv_sems), r - 1, peer(r))

        sends = [copy(r, me) for r in range(1, N_DEV)]
        for cp in sends:
            cp.start()
        for r in range(1, N_DEV):
            copy(r, _block_of(peer(r))).wait_recv()
        for cp in sends:
            cp.wait_send()
        acc = all_ref[0]
        for d in range(1, N_DEV):
            acc = acc + all_ref[d]
        o_ref[...] = acc

    return pl.pallas_call(
        body, name=name,
        in_specs=[pl.BlockSpec(memory_space=pltpu.VMEM)],
        out_specs=pl.BlockSpec(memory_space=pltpu.VMEM),
        out_shape=jax.ShapeDtypeStruct((rows, width), F32),
        scratch_shapes=[pltpu.VMEM((N_DEV, rows, width), F32), pltpu.SemaphoreType.DMA((N_LINKS,)), pltpu.SemaphoreType.DMA((N_LINKS,))],
    )(vec)


def _adamw(name, w, g, m, v):
    def fn(rows, consts):
        wv, gv, mv, vv = rows
        m2 = ADAM_B1 * mv + (1.0 - ADAM_B1) * gv
        v2 = ADAM_B2 * vv + (1.0 - ADAM_B2) * jnp.square(gv)
        m_hat = m2 / (1.0 - ADAM_B1 ** ADAM_STEP)
        v_hat = v2 / (1.0 - ADAM_B2 ** ADAM_STEP)
        return [-ADAM_LR * (m_hat / (jnp.sqrt(v_hat) + ADAM_EPS) + ADAM_WD * wv), m2, v2], []

    return _rowwise(name, fn, [w, g, m, v], [], [(w.shape[1], F32)] * 3, tm=_row_tile(w.shape[0]))


ROW = 1024
FFN_NAMES = ("ffn1_w_gate", "ffn1_w_up", "ffn1_w_down", "ffn2_w_gate", "ffn2_w_up", "ffn2_w_down")
OTHER = {"w_in": "w_in_t", "mla_w_uq": "uq_t", "mla_w_ukv": "ukv_t", "w_out": "w_out"}
BY_COLUMNS = ("ffn1_w_gate", "ffn1_w_up", "ffn2_w_gate", "ffn2_w_up", "w_in", "mla_w_uq", "mla_w_ukv")
SMALL = {
    "ffn1_pre_g": (1024, 1024), "ffn1_post_g": (1024, 1024), "mix_pre_g": (1024, 1024), "mla_q_norm_g": (256, 256),
    "mla_kv_norm_g": (128, 128), "mla_out_g": (512, 512), "gdn_a_log": (8, 128), "gdn_dt_bias": (8, 128),
    "gdn_norm_g": (64, 128), "mix_post_g": (1024, 1024), "ffn2_pre_g": (1024, 1024), "ffn2_post_g": (1024, 1024),
}
CONV_SHAPE = (GDN_CONV, 3 * N_HEADS * GDN_D)
CONV_SHARD = (GDN_CONV, CONV_SHAPE[1] // N_DEV)
CONV_LANES = CONV_SHAPE[0] * CONV_SHAPE[1]
SMALL_ROWS = 8
REDUCE_ROWS = 16


def _pack_small(vecs, conv, rows):
    parts = [_pad_lanes(vecs[n].reshape(1, -1), 0, r) for n, (_, r) in SMALL.items()]
    parts.append(conv.reshape(1, -1))
    flat = jnp.concatenate(parts, axis=1)
    return _pad_lanes(flat, 0, rows * ROW).reshape(rows, ROW)


def _unpack_small(buf):
    flat = buf.reshape(1, -1)
    out, at = {}, 0
    for n, (w, r) in SMALL.items():
        out[n] = flat[:, at:at + w]
        at += r
    return out, flat[0, at:]


def kernel(x, positions, ffn1_pre_g, ffn1_w_gate, ffn1_w_up, ffn1_w_down, ffn1_post_g, mix_pre_g, w_in, mla_q_norm_g, mla_w_uq, mla_kv_norm_g, mla_w_ukv, mla_out_g, gdn_conv_w, gdn_a_log, gdn_dt_bias, gdn_norm_g, w_out, mix_post_g, ffn2_pre_g, ffn2_w_gate, ffn2_w_up, ffn2_w_down, ffn2_post_g, loss_target, m_ffn1_pre_g, m_ffn1_w_gate, m_ffn1_w_up, m_ffn1_w_down, m_ffn1_post_g, m_mix_pre_g, m_w_in, m_mla_q_norm_g, m_mla_w_uq, m_mla_kv_norm_g, m_mla_w_ukv, m_mla_out_g, m_gdn_conv_w, m_gdn_a_log, m_gdn_dt_bias, m_gdn_norm_g, m_w_out, m_mix_post_g, m_ffn2_pre_g, m_ffn2_w_gate, m_ffn2_w_up, m_ffn2_w_down, m_ffn2_post_g, v_ffn1_pre_g, v_ffn1_w_gate, v_ffn1_w_up, v_ffn1_w_down, v_ffn1_post_g, v_mix_pre_g, v_w_in, v_mla_q_norm_g, v_mla_w_uq, v_mla_kv_norm_g, v_mla_w_ukv, v_mla_out_g, v_gdn_conv_w, v_gdn_a_log, v_gdn_dt_bias, v_gdn_norm_g, v_w_out, v_mix_post_g, v_ffn2_pre_g, v_ffn2_w_gate, v_ffn2_w_up, v_ffn2_w_down, v_ffn2_post_g):
    given = dict(locals())
    order = ["ffn1_pre_g", "ffn1_w_gate", "ffn1_w_up", "ffn1_w_down", "ffn1_post_g", "mix_pre_g", "w_in", "mla_q_norm_g",
             "mla_w_uq", "mla_kv_norm_g", "mla_w_ukv", "mla_out_g", "gdn_conv_w", "gdn_a_log", "gdn_dt_bias", "gdn_norm_g",
             "w_out", "mix_post_g", "ffn2_pre_g", "ffn2_w_gate", "ffn2_w_up", "ffn2_w_down", "ffn2_post_g"]
    assert sorted(order) == sorted(list(FFN_NAMES) + list(OTHER) + list(SMALL) + ["gdn_conv_w"])

    def drop_depth(a):
        return a[0] if a.ndim == 3 else a

    wts = {n: drop_depth(given[n]) for n in order}
    mom = {n: drop_depth(given["m_" + n]) for n in order}
    var = {n: drop_depth(given["v_" + n]) for n in order}
    me = _block_of(_place())

    def wire(n):
        return (wts[n].T if n in BY_COLUMNS else wts[n]).astype(BF16)

    (ffn1,) = _exchange("gather_first", _Gather([jnp.stack([wire(n) for n in FFN_NAMES[:3]])]))
    mid = _Gather([wire(n) for n in ("w_in", "mla_w_uq", "mla_w_ukv")])
    late = _Gather([jnp.stack([wire(n) for n in FFN_NAMES[3:]]), wire("w_out")])
    conv_at = lax.dynamic_update_slice(jnp.zeros((N_DEV, CONV_SHARD[0] * CONV_SHARD[1]), F32),
                                       wts["gdn_conv_w"].reshape(1, -1), (me, 0))
    conv_all = _all_reduce_small("gather_conv", _pad_lanes(conv_at.reshape(1, -1), 0, SMALL_ROWS * ROW).reshape(SMALL_ROWS, ROW))
    full = {n: wts[n] for n in SMALL}
    full["ffn1"] = ffn1
    full["gdn_conv_w"] = conv_all.reshape(-1)[:CONV_LANES].reshape((N_DEV,) + CONV_SHARD).transpose(1, 0, 2).reshape(CONV_SHAPE)

    loss_lanes, dx, grads, landed = _local_step(x[0], positions[0], loss_target[0], full, mid, late)
    loss = lax.psum(jnp.sum(loss_lanes), MESH_AXES)

    sums = {n: _sum_blocks("sum_" + n, blocks) for n, blocks in landed.items()}
    grad = {n: (sums[n].T if n in BY_COLUMNS else sums[n]) for n in sums}
    small_sum = _all_reduce_small("reduce_small", _pack_small(grads, grads["gdn_conv_w"].reshape(-1), REDUCE_ROWS))
    small_grad, conv_grad_full = _unpack_small(small_sum)
    grad.update(small_grad)
    grad["gdn_conv_w"] = lax.dynamic_slice(conv_grad_full[:CONV_LANES].reshape(CONV_SHAPE), (0, me * CONV_SHARD[1]), CONV_SHARD)

    outs = {"grad": grad, "delta": {}, "new_m": {}, "new_v": {}}
    for n in list(FFN_NAMES) + list(OTHER):
        outs["delta"][n], outs["new_m"][n], outs["new_v"][n] = _adamw("adamw_" + n, wts[n], grad[n], mom[n], var[n])
    small = [_pack_small(s, s["gdn_conv_w"].reshape(-1), SMALL_ROWS) for s in (wts, grad, mom, var)]
    for kind, s in zip(("delta", "new_m", "new_v"), _adamw("adamw_small", *small)):
        vecs, conv = _unpack_small(s)
        outs[kind].update(vecs)
        outs[kind]["gdn_conv_w"] = conv[:CONV_SHARD[0] * CONV_SHARD[1]].reshape(CONV_SHARD)
    result = [loss, dx[None]]
    for kind in ("grad", "delta", "new_m", "new_v"):
        result += [outs[kind][n].reshape(given[n].shape) for n in order]
    return tuple(result)
```

```python
import jax
import jax.numpy as jnp
from jax import lax
from jax.experimental import pallas as pl
from jax.experimental.pallas import tpu as pltpu

F32 = jnp.float32
BF16 = jnp.bfloat16
HI = lax.Precision.HIGH

N_DEV = 8
N_HEADS = 8
SLOT = 128
MLA_Q_RANK = 256
MLA_KV_RANK = 128
MLA_NOPE = 64
MLA_ROPE = 32
MLA_V = 64
GDN_D = 64
GDN_CONV = 4
GDN_CHUNK = 64
ROPE_THETA = 10000.0
EPS = 1e-6
ADAM_LR, ADAM_B1, ADAM_B2, ADAM_EPS, ADAM_WD, ADAM_STEP = 0.001, 0.9, 0.999, 1e-08, 0.01, 10


def _dot(a, b, ca, cb, precision=None):
    lead = a.ndim - 2
    batch = tuple(range(lead))
    return lax.dot_general(a, b, (((lead + ca,), (lead + cb,)), (batch, batch)), precision=precision,
                           preferred_element_type=F32)


def _nn(a, b, precision=None):
    return _dot(a, b, 1, 0, precision)


def _nt(a, b, precision=None):
    return _dot(a, b, 1, 1, precision)


def _tn(a, b, precision=None):
    return _dot(a, b, 0, 0, precision)


def _sigmoid(x):
    return 1.0 / (1.0 + jnp.exp(-x))


def _silu(x):
    return x * _sigmoid(x)


def _rms(x, g, n):
    ms = jnp.sum(x * x, axis=-1, keepdims=True) * (1.0 / n)
    return x * lax.rsqrt(ms + EPS) * g


def _chunk_masks():
    c = GDN_CHUNK
    i = lax.broadcasted_iota(jnp.int32, (c, c), 0)
    j = lax.broadcasted_iota(jnp.int32, (c, c), 1)
    lower = i >= j
    strict = i > j
    eye = (i == j).astype(F32)
    blocks = []
    b = 1
    while b < c:
        same = (i // (2 * b)) == (j // (2 * b))
        blocks.append(same & ((i % (2 * b)) >= b) & ((j % (2 * b)) < b))
        b *= 2
    return lower, strict, eye, blocks


def _unit_lower_inverse(low, eye, blocks):
    t = eye - jnp.where(blocks[0], low, 0.0)
    for m in blocks[1:]:
        lo = jnp.where(m, low, 0.0)
        t = t - _nn(t, _nn(lo, t, HI), HI)
    return t


@jax.custom_vjp
def _known_inverse(low, tinv):
    return tinv


def _known_inverse_fwd(low, tinv):
    return tinv, tinv


def _known_inverse_bwd(tinv, dt):
    return -_tn(tinv, _nt(dt, tinv, HI), HI), jnp.zeros_like(tinv)


_known_inverse.defvjp(_known_inverse_fwd, _known_inverse_bwd)

_PRODUCTS = {"nn": _nn, "nt": _nt, "tn": _tn}


@jax.custom_vjp
def _known_nn(a, b, c):
    return c


@jax.custom_vjp
def _known_nt(a, b, c):
    return c


@jax.custom_vjp
def _known_tn(a, b, c):
    return c


def _known_fwd(a, b, c):
    return c, (a, b, c)


_known_nn.defvjp(_known_fwd, lambda r, dc: (_nt(dc, r[1], HI), _tn(r[0], dc, HI), jnp.zeros_like(r[2])))
_known_nt.defvjp(_known_fwd, lambda r, dc: (_nn(dc, r[1], HI), _tn(dc, r[0], HI), jnp.zeros_like(r[2])))
_known_tn.defvjp(_known_fwd, lambda r, dc: (_nt(r[1], dc, HI), _nn(r[0], dc, HI), jnp.zeros_like(r[2])))
_KNOWN = {"nn": _known_nn, "nt": _known_nt, "tn": _known_tn}
GDN_PRODUCTS = 8
GDN_KEPT = 2 + GDN_PRODUCTS


def _gdn_chunk(q, k, v, gc, bb, s, masks, known=None):
    lower, strict, eye, blocks = masks
    made = []

    def product(kind, a, b):
        c = _PRODUCTS[kind](a, b, HI) if known is None else _KNOWN[kind](a, b, known[1 + len(made)])
        made.append(c)
        return c

    qs = q * (GDN_D ** -0.5)
    gct = jnp.swapaxes(gc, -1, -2)
    decay = jnp.exp(jnp.where(lower, gc - gct, -1e30))
    kb = k * bb
    low = jnp.where(strict, product("nt", kb, k) * decay, 0.0)
    tinv = _unit_lower_inverse(low, eye, blocks) if known is None else _known_inverse(low, known[0])
    eg = jnp.exp(gc)
    w = product("nn", tinv, kb * eg)
    u = product("nn", tinv, v * bb)
    attn = product("nt", qs, k) * decay
    last = lax.broadcasted_iota(jnp.int32, gc.shape[-2:], 0) == GDN_CHUNK - 1
    g_end = jnp.sum(jnp.where(last, gc, 0.0), axis=-2, keepdims=True)
    k_dec = k * jnp.exp(g_end - gc)
    v_new = u - product("nn", w, s)
    o = product("nn", qs * eg, s) + product("nn", attn, v_new)
    s_new = s * jnp.exp(g_end) + product("tn", k_dec, v_new)
    assert len(made) == GDN_PRODUCTS
    return o, s_new, [tinv] + made


GDN_GROUP = 8
GDN_GROUPS = N_HEADS // GDN_GROUP


def _group_heads(ref):
    return jnp.stack([ref[:, pl.ds(j * SLOT, GDN_D)] for j in range(GDN_GROUP)])


def _ungroup_heads(ref, val):
    pad = jnp.zeros((GDN_CHUNK, SLOT - GDN_D), F32)
    for j in range(GDN_GROUP):
        ref[:, pl.ds(j * SLOT, GDN_D)] = val[j]
        ref[:, pl.ds(j * SLOT + GDN_D, SLOT - GDN_D)] = pad


def _gdn_fwd(qkv, gb, bb, carry=None):
    t = qkv.shape[0]
    n_chunks = t // GDN_CHUNK
    d = GDN_D

    def body(q_ref, k_ref, v_ref, g_ref, b_ref, o_ref, keep_ref, s_ref):
        @pl.when(pl.program_id(1) == 0)
        def _():
            s_ref[...] = jnp.zeros_like(s_ref)

        s = s_ref[...]
        keep_ref[:, 0, 0] = s
        o, s_new, made = _gdn_chunk(*[_group_heads(r) for r in (q_ref, k_ref, v_ref, g_ref, b_ref)], s, _chunk_masks())
        for i, val in enumerate(made):
            keep_ref[:, 0, 1 + i] = val
        s_ref[...] = s_new
        _ungroup_heads(o_ref, o)

    def spec(kind=0):
        return pl.BlockSpec((GDN_CHUNK, GDN_GROUP * SLOT), lambda h, n: (n, kind * GDN_GROUPS + h))

    return _call_carrying(
        body, carry, (qkv, qkv, qkv, gb, bb), name="gdn_fwd",
        grid=(GDN_GROUPS, n_chunks),
        in_specs=[spec(0), spec(1), spec(2), spec(), spec()],
        out_specs=[spec(), pl.BlockSpec((GDN_GROUP, 1, GDN_KEPT, d, d), lambda h, n: (h, n, 0, 0, 0))],
        out_shape=[jax.ShapeDtypeStruct((t, N_HEADS * SLOT), F32), jax.ShapeDtypeStruct((N_HEADS, n_chunks, GDN_KEPT, d, d), F32)],
        scratch_shapes=[pltpu.VMEM((GDN_GROUP, d, d), F32)],
        compiler_params=pltpu.CompilerParams(dimension_semantics=("arbitrary", "arbitrary")),
    )


def _gdn_bwd(qkv, gb, bb, keep, do, carry=None):
    t = qkv.shape[0]
    n_chunks = t // GDN_CHUNK
    d = GDN_D

    def body(q_ref, k_ref, v_ref, g_ref, b_ref, keep_ref, do_ref, dqkv_ref, dg_ref, db_ref, ds_ref):
        @pl.when(pl.program_id(1) == 0)
        def _():
            ds_ref[...] = jnp.zeros_like(ds_ref)

        masks = _chunk_masks()
        known = [keep_ref[:, 0, 1 + i] for i in range(GDN_KEPT - 1)]
        _, pull = jax.vjp(lambda *a: _gdn_chunk(*a, masks, known)[:2],
                          *[_group_heads(r) for r in (q_ref, k_ref, v_ref, g_ref, b_ref)], keep_ref[:, 0, 0])
        dq, dk, dv, dg, db, ds = pull((_group_heads(do_ref), ds_ref[...]))
        ds_ref[...] = ds
        for i, val in enumerate((dq, dk, dv)):
            _ungroup_heads(dqkv_ref.at[i], val)
        _ungroup_heads(dg_ref, dg)
        _ungroup_heads(db_ref, db)

    def spec(kind=0):
        return pl.BlockSpec((GDN_CHUNK, GDN_GROUP * SLOT), lambda h, n: (n_chunks - 1 - n, kind * GDN_GROUPS + h))

    return _call_carrying(
        body, carry, (qkv, qkv, qkv, gb, bb, keep, do), name="gdn_bwd",
        grid=(GDN_GROUPS, n_chunks),
        in_specs=[spec(0), spec(1), spec(2), spec(), spec(),
                  pl.BlockSpec((GDN_GROUP, 1, GDN_KEPT, d, d), lambda h, n: (h, n_chunks - 1 - n, 0, 0, 0)), spec()],
        out_specs=[pl.BlockSpec((3, GDN_CHUNK, GDN_GROUP * SLOT), lambda h, n: (0, n_chunks - 1 - n, h)), spec(), spec()],
        out_shape=[jax.ShapeDtypeStruct((3, t, N_HEADS * SLOT), F32)] + [jax.ShapeDtypeStruct((t, N_HEADS * SLOT), F32)] * 2,
        scratch_shapes=[pltpu.VMEM((GDN_GROUP, d, d), F32)],
        compiler_params=pltpu.CompilerParams(dimension_semantics=("arbitrary", "arbitrary")),
    )


def _rowwise(name, fn, rows, consts, outs, sums=(), tm=512):
    rows = [x if isinstance(x, tuple) else (x, x.shape[1], 0) for x in rows]
    t = rows[0][0].shape[0]
    tm = min(tm, t)
    steps = t // tm
    n_r, n_c, n_o, n_s = len(rows), len(consts), len(outs), len(sums)

    def window(width, block):
        return pl.BlockSpec((tm, width), lambda i: (i, block))

    def body(*refs):
        r, c = refs[:n_r], refs[n_r:n_r + n_c]
        o, s = refs[n_r + n_c:n_r + n_c + n_o], refs[n_r + n_c + n_o:]
        vals, tot = fn([x[...] for x in r], [x[...] for x in c])
        for ref, val in zip(o, vals):
            ref[...] = val.astype(ref.dtype)
        if n_s:
            @pl.when(pl.program_id(0) == 0)
            def _():
                for ref in s:
                    ref[...] = jnp.zeros_like(ref)

            for ref, val in zip(s, tot):
                ref[...] += val

    return pl.pallas_call(
        body, name=name,
        grid=(steps,),
        in_specs=[window(w, b) for _, w, b in rows] + [pl.BlockSpec(x.shape, lambda i: (0, 0)) for x in consts],
        out_specs=[pl.BlockSpec((tm, w), lambda i: (i, 0)) for w, _ in outs]
        + [pl.BlockSpec((1, w), lambda i: (0, 0)) for w in sums],
        out_shape=[jax.ShapeDtypeStruct((t, w), dt) for w, dt in outs]
        + [jax.ShapeDtypeStruct((1, w), F32) for w in sums],
        compiler_params=pltpu.CompilerParams(dimension_semantics=("arbitrary",)),
    )(*[x for x, _, _ in rows], *consts)


def _tile(dim, target):
    if dim <= target:
        return dim
    best = None
    for cand in range(128, target + 1, 128):
        if dim % cand == 0:
            best = cand
    assert best is not None, (dim, target)
    return best


def _matmul(name, a, b, mode, out_dtype=F32, tm=1024, tn=1024, tk=2048, after=None):
    if mode == "nn":
        (m, k), n = a.shape, b.shape[1]
    elif mode == "nt":
        (m, k), n = a.shape, b.shape[0]
    else:
        (k, m), n = a.shape, b.shape[1]
    tm, tn, tk = _tile(m, tm), _tile(n, tn), _tile(k, tk)
    k_steps = k // tk
    product = {"nn": _nn, "nt": _nt, "tn": _tn}[mode]

    def body(a_ref, b_ref, *rest):
        o_ref, acc_ref = rest[-2:]
        part = product(a_ref[...].astype(BF16), b_ref[...].astype(BF16))
        if k_steps == 1:
            o_ref[...] = part.astype(o_ref.dtype)
        else:
            kk = pl.program_id(2)

            @pl.when(kk == 0)
            def _():
                acc_ref[...] = part

            @pl.when(kk > 0)
            def _():
                acc_ref[...] += part

            @pl.when(kk == k_steps - 1)
            def _():
                o_ref[...] = acc_ref[...].astype(o_ref.dtype)

    a_spec = pl.BlockSpec((tk, tm), lambda i, j, kk: (kk, i)) if mode == "tn" else pl.BlockSpec((tm, tk), lambda i, j, kk: (i, kk))
    b_spec = pl.BlockSpec((tn, tk), lambda i, j, kk: (j, kk)) if mode == "nt" else pl.BlockSpec((tk, tn), lambda i, j, kk: (kk, j))
    ordered = [] if after is None else [after]
    return pl.pallas_call(
        body, name=name,
        grid=(m // tm, n // tn, k_steps),
        in_specs=[a_spec, b_spec] + [pl.BlockSpec(x.shape, lambda i, j, kk: (0, 0)) for x in ordered],
        out_specs=pl.BlockSpec((tm, tn), lambda i, j, kk: (i, j)),
        out_shape=jax.ShapeDtypeStruct((m, n), out_dtype),
        scratch_shapes=[pltpu.VMEM((tm, tn) if k_steps > 1 else (8, 128), F32)],
        compiler_params=pltpu.CompilerParams(dimension_semantics=("parallel", "parallel", "arbitrary")),
    )(a, b, *ordered)


FFN_TM = 512
FFN_BWD_TM = 256
FFN_BLOCKS = 4
FFN_GATE, FFN_UP, FFN_DOWN = 0, 1, 2


def _ffn_weight_specs(ffn_w, first):
    _, _, rows, dm = ffn_w.shape

    def spec(k):
        return pl.BlockSpec((FFN_BLOCKS, None, rows, dm), lambda i, j: (j, first + k, 0, 0))

    return [spec(FFN_GATE), spec(FFN_UP), spec(FFN_DOWN)], FFN_BLOCKS * rows


def _ffn_fwd(name, x, g_pre, ffn_w, first, g_post, carry=None):
    t, dm = x.shape
    tm = min(FFN_TM, t)
    w_specs, tf = _ffn_weight_specs(ffn_w, first)
    f_steps = N_DEV // FFN_BLOCKS

    def body(x_ref, gpre_ref, wg_ref, wu_ref, wd_ref, gpost_ref, h_ref, y_ref, hg_ref, hu_ref, xn_ref, acc_ref):
        j = pl.program_id(1)

        @pl.when(j == 0)
        def _():
            xn_ref[...] = _rms(x_ref[...], gpre_ref[...], dm).astype(BF16)
            acc_ref[...] = jnp.zeros_like(acc_ref)

        xn = xn_ref[...]
        wg, wu, wd = (r[...].reshape(tf, dm) for r in (wg_ref, wu_ref, wd_ref))
        hg, hu = _nt(xn, wg), _nt(xn, wu)
        hg_ref[...] = hg.astype(BF16)
        hu_ref[...] = hu.astype(BF16)
        a = _silu(hg) * hu
        acc_ref[...] += _nn(a.astype(BF16), wd)

        @pl.when(j == f_steps - 1)
        def _():
            h = acc_ref[...]
            h_ref[...] = h
            y_ref[...] = x_ref[...] + 0.5 * _rms(h, gpost_ref[...], dm)

    row = pl.BlockSpec((tm, dm), lambda i, j: (i, 0))
    vec = pl.BlockSpec((1, dm), lambda i, j: (0, 0))
    wide = pl.BlockSpec((tm, tf), lambda i, j: (i, j))
    return _call_carrying(
        body, carry, (x, g_pre, ffn_w, ffn_w, ffn_w, g_post), name=name,
        grid=(t // tm, f_steps),
        in_specs=[row, vec, *w_specs, vec],
        out_specs=[row, row, wide, wide],
        out_shape=[jax.ShapeDtypeStruct((t, dm), F32)] * 2 + [jax.ShapeDtypeStruct((t, f_steps * tf), BF16)] * 2,
        scratch_shapes=[pltpu.VMEM((tm, dm), BF16), pltpu.VMEM((tm, dm), F32)],
        compiler_params=pltpu.CompilerParams(dimension_semantics=("arbitrary", "arbitrary")),
    )


def _ffn_bwd(name, x, h, hg, hu, dy, g_pre, ffn_w, first, g_post, carry=None):
    t, dm = x.shape
    tm = min(FFN_BWD_TM, t)
    w_specs, tf = _ffn_weight_specs(ffn_w, first)
    f_steps = N_DEV // FFN_BLOCKS
    f = f_steps * tf

    def post(hv, g):
        return 0.5 * _rms(hv, g, dm)

    def pre(xv, g):
        return _rms(xv, g, dm)

    def body(x_ref, h_ref, dy_ref, hg_ref, hu_ref, gpre_ref, wg_ref, wu_ref, wd_ref, gpost_ref,
             dx_ref, xn_ref, dh_ref, a_ref, dhg_ref, dhu_ref, dgpre_ref, dgpost_ref, acc_ref):
        i, j = pl.program_id(0), pl.program_id(1)

        @pl.when((i == 0) & (j == 0))
        def _():
            dgpre_ref[...] = jnp.zeros_like(dgpre_ref)
            dgpost_ref[...] = jnp.zeros_like(dgpost_ref)

        @pl.when(j == 0)
        def _():
            xn_ref[...] = pre(x_ref[...], gpre_ref[...]).astype(BF16)
            _, pull = jax.vjp(post, h_ref[...], gpost_ref[...])
            dh, dg = pull(dy_ref[...])
            dh_ref[...] = dh.astype(BF16)
            dgpost_ref[...] += dg
            acc_ref[...] = jnp.zeros_like(acc_ref)

        wg, wu, wd = (r[...].reshape(tf, dm) for r in (wg_ref, wu_ref, wd_ref))
        hg, hu = hg_ref[...].astype(F32), hu_ref[...].astype(F32)
        da = _nt(dh_ref[...], wd)
        sig = _sigmoid(hg)
        act = hg * sig
        dhu = (da * act).astype(BF16)
        dhg = (da * hu * (sig * (1.0 + hg * (1.0 - sig)))).astype(BF16)
        a_ref[...] = (act * hu).astype(BF16)
        dhg_ref[...] = dhg
        dhu_ref[...] = dhu
        acc_ref[...] += _nn(dhg, wg) + _nn(dhu, wu)

        @pl.when(j == f_steps - 1)
        def _():
            _, pull = jax.vjp(pre, x_ref[...], gpre_ref[...])
            dx, dg = pull(acc_ref[...])
            dx_ref[...] = dy_ref[...] + dx
            dgpre_ref[...] += dg

    row = pl.BlockSpec((tm, dm), lambda i, j: (i, 0))
    vec = pl.BlockSpec((1, dm), lambda i, j: (0, 0))
    wide = pl.BlockSpec((tm, tf), lambda i, j: (i, j))
    return _call_carrying(
        body, carry, (x, h, dy, hg, hu, g_pre, ffn_w, ffn_w, ffn_w, g_post), name=name,
        grid=(t // tm, f_steps),
        in_specs=[row, row, row, wide, wide, vec, *w_specs, vec],
        out_specs=[row, row, row, wide, wide, wide, vec, vec],
        out_shape=[jax.ShapeDtypeStruct((t, dm), F32), jax.ShapeDtypeStruct((t, dm), BF16), jax.ShapeDtypeStruct((t, dm), BF16),
                   jax.ShapeDtypeStruct((t, f), BF16), jax.ShapeDtypeStruct((t, f), BF16), jax.ShapeDtypeStruct((t, f), BF16),
                   jax.ShapeDtypeStruct((1, dm), F32), jax.ShapeDtypeStruct((1, dm), F32)],
        scratch_shapes=[pltpu.VMEM((tm, dm), F32)],
        compiler_params=pltpu.CompilerParams(dimension_semantics=("arbitrary", "arbitrary")),
    )


ATT_T = 512
ATT_GROUP = 4
ATT_GROUP_FWD = 8
ATT_SCALE = (MLA_NOPE + MLA_ROPE) ** -0.5


def _stack_slots(ref, group):
    return jnp.stack([ref[:, pl.ds(j * SLOT, SLOT)] for j in range(group)])


def _unstack_slots(ref, val):
    for j in range(val.shape[0]):
        ref[:, pl.ds(j * SLOT, SLOT)] = val[j].astype(ref.dtype)


def _scores(q, k, diagonal):
    s = _nt(q, k) * ATT_SCALE
    if diagonal:
        row = lax.broadcasted_iota(jnp.int32, s.shape[1:], 0)
        col = lax.broadcasted_iota(jnp.int32, s.shape[1:], 1)
        s = jnp.where(col <= row, s, -1e30)
    return s


def _attn_pairs(steps, q_major):
    pairs = ([(qi, ki) for qi in range(steps) for ki in range(qi + 1)] if q_major
             else [(qi, ki) for ki in range(steps) for qi in range(ki, steps)])
    return jnp.array([p[0] for p in pairs], jnp.int32), jnp.array([p[1] for p in pairs], jnp.int32)


def _attn_specs(tile, group):
    width = group * SLOT
    return (pl.BlockSpec((tile, width), lambda h, p, qt, kt: (qt[p], h)),
            pl.BlockSpec((tile, width), lambda h, p, qt, kt: (kt[p], h)))


def _attn_fwd(q, k, v):
    t = q.shape[0]
    tile = min(ATT_T, t)
    steps = t // tile
    g = ATT_GROUP_FWD

    strip = min(SLOT, tile)

    def body(qt_ref, kt_ref, q_ref, k_ref, v_ref, o_ref, lse_ref, m_ref, l_ref, alpha_ref, acc_ref, s_ref, p_ref):
        qi, ki = qt_ref[pl.program_id(1)], kt_ref[pl.program_id(1)]

        @pl.when(ki == 0)
        def _():
            m_ref[...] = jnp.full_like(m_ref, -1e30)
            l_ref[...] = jnp.zeros_like(l_ref)
            acc_ref[...] = jnp.zeros_like(acc_ref)

        def step(diagonal):
            s_ref[...] = _nt(_stack_slots(k_ref, g), _stack_slots(q_ref, g))
            for j in range(tile // strip):
                c = pl.ds(j * strip, strip)
                s = s_ref[:, :, c] * ATT_SCALE
                if diagonal:
                    key = lax.broadcasted_iota(jnp.int32, s.shape[1:], 0)
                    query = lax.broadcasted_iota(jnp.int32, s.shape[1:], 1) + j * strip
                    s = jnp.where(key <= query, s, -1e30)
                m_old = m_ref[:, :, c]
                m_new = jnp.maximum(m_old, jnp.max(s, axis=1, keepdims=True))
                p = jnp.exp(s - m_new)
                alpha = jnp.exp(m_old - m_new)
                l_ref[:, :, c] = alpha * l_ref[:, :, c] + jnp.sum(p, axis=1, keepdims=True)
                alpha_ref[:, :, c] = alpha
                m_ref[:, :, c] = m_new
                p_ref[:, :, c] = p.astype(BF16)
            acc_ref[...] = acc_ref[...] * alpha_ref[...] + _tn(_stack_slots(v_ref, g), p_ref[...])

        @pl.when(ki < qi)
        def _():
            step(False)

        @pl.when(ki == qi)
        def _():
            step(True)
            out = acc_ref[...] / l_ref[...]
            lse = jnp.broadcast_to(m_ref[...] + jnp.log(l_ref[...]), out.shape)
            for j in range(g):
                o_ref[:, pl.ds(j * SLOT, SLOT)] = out[j].T
                lse_ref[:, pl.ds(j * SLOT, SLOT)] = lse[j].T

    q_spec, k_spec = _attn_specs(tile, g)
    tables = _attn_pairs(steps, True)
    return pl.pallas_call(
        body, name="attn_fwd",
        grid_spec=pltpu.PrefetchScalarGridSpec(
            num_scalar_prefetch=2, grid=(N_HEADS // g, tables[0].shape[0]),
            in_specs=[q_spec, k_spec, k_spec], out_specs=[q_spec, q_spec],
            scratch_shapes=[pltpu.VMEM((g, 1, tile), F32), pltpu.VMEM((g, 1, tile), F32), pltpu.VMEM((g, 1, tile), F32),
                            pltpu.VMEM((g, SLOT, tile), F32), pltpu.VMEM((g, tile, tile), F32), pltpu.VMEM((g, tile, tile), BF16)]),
        out_shape=[jax.ShapeDtypeStruct((t, N_HEADS * SLOT), F32)] * 2,
        compiler_params=pltpu.CompilerParams(dimension_semantics=("parallel", "arbitrary")),
    )(*tables, q, k, v)


def _attn_grad_scores(q, k, v, do, lse_ref, delta_ref, diagonal):
    g = ATT_GROUP
    p = jnp.exp(_scores(q, k, diagonal) - _stack_slots(lse_ref, g)[:, :, 0:1])
    dp = _nt(do, v)
    return p, p * (dp - _stack_slots(delta_ref, g)[:, :, 0:1]) * ATT_SCALE


def _attn_bwd(q, k, v, do, lse, delta):
    t = q.shape[0]
    tile = min(ATT_T, t)
    steps = t // tile
    g = ATT_GROUP

    def body(qt_ref, kt_ref, q_ref, k_ref, v_ref, do_ref, lse_ref, delta_ref, dq_ref, dk_ref, dv_ref, dk_acc, dv_acc):
        qi, ki = qt_ref[pl.program_id(1)], kt_ref[pl.program_id(1)]

        @pl.when(pl.program_id(1) == 0)
        def _():
            dq_ref[...] = jnp.zeros_like(dq_ref)

        def step(diagonal):
            qq, kk = _stack_slots(q_ref, g), _stack_slots(k_ref, g)
            do_b = _stack_slots(do_ref, g).astype(BF16)
            p, ds = _attn_grad_scores(qq, kk, _stack_slots(v_ref, g), do_b, lse_ref, delta_ref, diagonal)
            ds = ds.astype(BF16)
            dv_acc[...] += _tn(p.astype(BF16), do_b)
            dk_acc[...] += _tn(ds, qq)
            dq = _nn(ds, kk)
            rows = pl.ds(pl.multiple_of(qi * tile, tile), tile)
            for j in range(g):
                dq_ref[rows, pl.ds(j * SLOT, SLOT)] += dq[j]

        @pl.when(qi == ki)
        def _():
            dk_acc[...] = jnp.zeros_like(dk_acc)
            dv_acc[...] = jnp.zeros_like(dv_acc)
            step(True)

        @pl.when(qi > ki)
        def _():
            step(False)

        @pl.when(qi == steps - 1)
        def _():
            _unstack_slots(dk_ref, dk_acc[...])
            _unstack_slots(dv_ref, dv_acc[...])

    q_spec, k_spec = _attn_specs(tile, g)
    tables = _attn_pairs(steps, False)
    return pl.pallas_call(
        body, name="attn_bwd",
        grid_spec=pltpu.PrefetchScalarGridSpec(
            num_scalar_prefetch=2, grid=(N_HEADS // g, tables[0].shape[0]),
            in_specs=[q_spec, k_spec, k_spec, q_spec, q_spec, q_spec],
            out_specs=[pl.BlockSpec((t, g * SLOT), lambda h, p, qt, kt: (0, h)), k_spec, k_spec],
            scratch_shapes=[pltpu.VMEM((g, tile, SLOT), F32), pltpu.VMEM((g, tile, SLOT), F32)]),
        out_shape=[jax.ShapeDtypeStruct((t, N_HEADS * SLOT), F32)] * 3,
        compiler_params=pltpu.CompilerParams(dimension_semantics=("parallel", "arbitrary")),
    )(*tables, q, k, v, do, lse, delta)


CONV_PAD = 8


def _fill_padded(ref, val):
    t = val.shape[0]
    zeros = jnp.zeros((CONV_PAD, val.shape[1]), val.dtype)
    ref[pl.ds(0, CONV_PAD)] = zeros
    ref[pl.ds(CONV_PAD + t, CONV_PAD)] = zeros
    ref[pl.ds(CONV_PAD, t)] = val


def _shifted(ref, s):
    return ref[pl.ds(CONV_PAD - s, ref.shape[0] - 2 * CONV_PAD)]


def _l2norm(x):
    return x * lax.rsqrt(jnp.sum(x * x, axis=-1, keepdims=True) + EPS)


def _conv_pre(x_pad, w):
    y = w[GDN_CONV - 1:GDN_CONV, :] * _shifted(x_pad, 0)
    for s in range(1, GDN_CONV):
        y = y + w[GDN_CONV - 1 - s:GDN_CONV - s, :] * _shifted(x_pad, s)
    return y


def _gdn_conv_fwd(x, w):
    t, width = x.shape

    def body(x_ref, w_ref, o_ref, x_pad):
        _fill_padded(x_pad, x_ref[...])
        act = _silu(_conv_pre(x_pad, w_ref[...]))
        normed = pl.program_id(0) < 2 * N_HEADS
        o_ref[...] = jnp.where(normed, _l2norm(act), act)

    return pl.pallas_call(
        body, name="gdn_conv_fwd",
        grid=(width // SLOT,),
        in_specs=[pl.BlockSpec((t, SLOT), lambda j: (0, j)), pl.BlockSpec((GDN_CONV, SLOT), lambda j: (0, j))],
        out_specs=pl.BlockSpec((t, SLOT), lambda j: (0, j)),
        out_shape=jax.ShapeDtypeStruct((t, width), F32),
        scratch_shapes=[pltpu.VMEM((t + 2 * CONV_PAD, SLOT), F32)],
        compiler_params=pltpu.CompilerParams(dimension_semantics=("parallel",)),
    )(x, w)


def _gdn_conv_bwd(x, w, dout):
    t, width = x.shape

    def body(x_ref, w_ref, do_ref, dx_ref, dw_ref, x_pad, dy_pad):
        wv = w_ref[...]
        _fill_padded(x_pad, x_ref[...])
        y = _conv_pre(x_pad, wv)
        sig = _sigmoid(y)
        act = y * sig
        _, pull = jax.vjp(_l2norm, act)
        normed = pl.program_id(0) < 2 * N_HEADS
        dact = jnp.where(normed, pull(do_ref[0])[0], do_ref[0])
        dy = dact * (sig * (1.0 + y * (1.0 - sig)))
        _fill_padded(dy_pad, dy)
        dx = wv[GDN_CONV - 1:GDN_CONV, :] * dy
        for s in range(1, GDN_CONV):
            dx = dx + wv[GDN_CONV - 1 - s:GDN_CONV - s, :] * _shifted(dy_pad, -s)
        dx_ref[...] = dx.astype(BF16)
        for s in range(GDN_CONV):
            dw_ref[GDN_CONV - 1 - s:GDN_CONV - s, :] = jnp.sum(dy * _shifted(x_pad, s), axis=0, keepdims=True)

    col = pl.BlockSpec((t, SLOT), lambda j: (0, j))
    tap = pl.BlockSpec((GDN_CONV, SLOT), lambda j: (0, j))
    return pl.pallas_call(
        body, name="gdn_conv_bwd",
        grid=(width // SLOT,),
        in_specs=[col, tap, pl.BlockSpec((1, t, SLOT), lambda j: (j // N_HEADS, 0, j % N_HEADS))],
        out_specs=[col, tap],
        out_shape=[jax.ShapeDtypeStruct((t, width), BF16), jax.ShapeDtypeStruct((GDN_CONV, width), F32)],
        scratch_shapes=[pltpu.VMEM((t + 2 * CONV_PAD, SLOT), F32)] * 2,
        compiler_params=pltpu.CompilerParams(dimension_semantics=("parallel",)),
    )(x, w, dout)


def _softplus(x):
    e = jnp.exp(-jnp.abs(x))
    u = 1.0 + e
    log1p = jnp.where(u == 1.0, e, jnp.log(u) * e / jnp.where(u == 1.0, 1.0, u - 1.0))
    return jnp.maximum(x, 0.0) + log1p


def _chunk_running_sum(x, reverse=False):
    tm = x.shape[0]
    at = lax.broadcasted_iota(jnp.int32, x.shape, 0) % GDN_CHUNK
    step = 1
    while step < GDN_CHUNK:
        if reverse:
            x = x + jnp.where(at < GDN_CHUNK - step, pltpu.roll(x, tm - step, 0), 0.0)
        else:
            x = x + jnp.where(at >= step, pltpu.roll(x, step, 0), 0.0)
        step *= 2
    return x


def _gates_fwd(ab, a_log, dt_bias):
    def fn(rows, consts):
        (abv,), (alog, dtb) = rows, consts
        g = _chunk_running_sum(-jnp.exp(alog) * _softplus(abv + dtb))
        beta = _sigmoid(abv)
        shape = (abv.shape[0], SLOT)
        g_slots = [jnp.broadcast_to(g[:, h:h + 1], shape) for h in range(N_HEADS)]
        b_slots = [jnp.broadcast_to(beta[:, N_HEADS + h:N_HEADS + h + 1], shape) for h in range(N_HEADS)]
        return [jnp.concatenate(g_slots, axis=1), jnp.concatenate(b_slots, axis=1)], []

    width = N_HEADS * SLOT
    return _rowwise("gdn_gates_fwd", fn, [ab], [a_log, dt_bias], [(width, F32), (width, F32)])


def _gates_bwd(ab, a_log, dt_bias, dg, dbeta):
    def fn(rows, consts):
        (abv, dgv, dbv), (alog, dtb) = rows, consts
        lane = lax.broadcasted_iota(jnp.int32, abv.shape, 1)
        dg_tok = jnp.zeros_like(abv)
        db_tok = jnp.zeros_like(abv)
        for h in range(N_HEADS):
            dg_tok = dg_tok + jnp.where(lane == h, jnp.sum(dgv[:, h * SLOT:(h + 1) * SLOT], axis=1, keepdims=True), 0.0)
            db_tok = db_tok + jnp.where(lane == N_HEADS + h, jnp.sum(dbv[:, h * SLOT:(h + 1) * SLOT], axis=1, keepdims=True), 0.0)
        dg_tok = _chunk_running_sum(dg_tok, reverse=True)
        xa = abv + dtb
        g = -jnp.exp(alog) * _softplus(xa)
        da = dg_tok * (-jnp.exp(alog)) * _sigmoid(xa)
        beta = _sigmoid(abv)
        dab = jnp.where(lane < N_HEADS, da, db_tok * beta * (1.0 - beta))
        dab = jnp.where(lane < 2 * N_HEADS, dab, 0.0)
        d_alog = jnp.sum(jnp.where(lane < N_HEADS, dg_tok * g, 0.0), axis=0, keepdims=True)
        d_dtb = jnp.sum(jnp.where(lane < N_HEADS, da, 0.0), axis=0, keepdims=True)
        return [dab], [d_alog, d_dtb]

    return _rowwise("gdn_gates_bwd", fn, [ab, dg, dbeta], [a_log, dt_bias], [(SLOT, F32)], sums=[SLOT, SLOT])


ROPE_HALF = MLA_ROPE // 2


def _rope_tables(positions):
    freqs = ROPE_THETA ** (-jnp.arange(ROPE_HALF, dtype=F32) / ROPE_HALF)
    ang = positions.astype(F32)[:, None] * freqs
    cos, sin = jnp.cos(ang), jnp.sin(ang)
    t = positions.shape[0]
    ones, zeros = jnp.ones((t, MLA_NOPE), F32), jnp.zeros((t, MLA_NOPE), F32)
    tail = jnp.zeros((t, SLOT - MLA_NOPE - MLA_ROPE), F32)
    half0 = jnp.zeros((t, ROPE_HALF), F32)
    same = jnp.concatenate([ones, cos, cos, tail], axis=1)
    from_low = jnp.concatenate([zeros, half0, sin, tail], axis=1)
    from_high = jnp.concatenate([zeros, -sin, half0, tail], axis=1)
    return same, from_low, from_high


def _rope(x, tabs):
    same, from_low, from_high = tabs
    width = x.shape[1]
    return x * same + pltpu.roll(x, ROPE_HALF, 1) * from_low + pltpu.roll(x, width - ROPE_HALF, 1) * from_high


def _rope_transposed(dy, tabs):
    same, from_low, from_high = tabs
    width = dy.shape[1]
    return dy * same + pltpu.roll(dy * from_low, width - ROPE_HALF, 1) + pltpu.roll(dy * from_high, ROPE_HALF, 1)


def _tile_slots(tab):
    return jnp.concatenate([tab] * N_HEADS, axis=1)


A_WIDTH = MLA_Q_RANK + MLA_KV_RANK + 2 * SLOT
A_KPE = MLA_Q_RANK + MLA_KV_RANK
A_AB = A_KPE + SLOT
WIDE = N_HEADS * SLOT


def _mla_front_fwd(proj_a, tabs, g_q, g_kv, w_uq, w_kv):
    def fn(rows, consts):
        pa, *tb = rows
        gq, gkv, wuq, wkv = consts
        cqn = _rms(pa[:, :MLA_Q_RANK], gq, MLA_Q_RANK).astype(BF16)
        ckvn = _rms(pa[:, MLA_Q_RANK:A_KPE], gkv, MLA_KV_RANK).astype(BF16)
        kv = _nt(ckvn, wkv)
        q = _rope(_nt(cqn, wuq), [_tile_slots(x) for x in tb])
        k = kv[:, :WIDE] + _tile_slots(_rope(pa[:, A_KPE:A_AB], tb))
        return [cqn, ckvn, q, k, kv[:, WIDE:]], []

    return _rowwise("mla_front_fwd", fn, [proj_a, *tabs], [g_q, g_kv, w_uq, w_kv],
                    [(MLA_Q_RANK, BF16), (MLA_KV_RANK, BF16)] + [(WIDE, BF16)] * 3)


def _mla_front_bwd(proj_a, tabs, g_q, g_kv, w_uq, w_kv, dq, dk, dv, dab):
    def fn(rows, consts):
        pa, t0, t1, t2, dqv, dkv, dvv, da = rows
        gq, gkv, wuq, wkv = consts
        tb = (t0, t1, t2)
        dq_p = _rope_transposed(dqv, [_tile_slots(x) for x in tb]).astype(BF16)
        dkv_p = jnp.concatenate([dkv, dvv], axis=1).astype(BF16)
        dkpe = dkv[:, :SLOT]
        for h in range(1, N_HEADS):
            dkpe = dkpe + dkv[:, h * SLOT:(h + 1) * SLOT]
        _, pull_q = jax.vjp(lambda x, g: _rms(x, g, MLA_Q_RANK), pa[:, :MLA_Q_RANK], gq)
        _, pull_kv = jax.vjp(lambda x, g: _rms(x, g, MLA_KV_RANK), pa[:, MLA_Q_RANK:A_KPE], gkv)
        dcq, dgq = pull_q(_nn(dq_p, wuq))
        dckv, dgkv = pull_kv(_nn(dkv_p, wkv))
        return [jnp.concatenate([dcq, dckv, _rope_transposed(dkpe, tb), da], axis=1), dq_p, dkv_p], [dgq, dgkv]

    return _rowwise("mla_front_bwd", fn, [proj_a, *tabs, dq, dk, dv, dab], [g_q, g_kv, w_uq, w_kv],
                    [(A_WIDTH, BF16), (WIDE, BF16), (2 * WIDE, BF16)], sums=[MLA_Q_RANK, MLA_KV_RANK])


def _slot_sum(x):
    parts = [jnp.broadcast_to(jnp.sum(x[:, h * SLOT:(h + 1) * SLOT], axis=1, keepdims=True), (x.shape[0], SLOT))
             for h in range(N_HEADS)]
    return jnp.concatenate(parts, axis=1)


def _mix_join(o_mla, o_gdn, gate, g_mla, g_gdn):
    mla = _rms(o_mla, g_mla, N_HEADS * MLA_V)
    gdn = o_gdn * lax.rsqrt(_slot_sum(o_gdn * o_gdn) * (1.0 / GDN_D) + EPS) * g_gdn * _silu(gate)
    return mla, gdn


MIX_TM = 256


def _mix_fwd(o_mla, o_gdn, gate, x, g_mla, g_gdn, w_out, g_post):
    dm = x.shape[1]

    def fn(rows, consts):
        om, og, gt, xv = rows
        gm, gg, wo, gp = consts
        cat = jnp.concatenate(_mix_join(om, og, gt, gm, gg), axis=1).astype(BF16)
        mixed = _nn(cat, wo)
        return [cat, mixed, xv + _rms(mixed, gp, dm)], []

    return _rowwise("mix_fwd", fn, [o_mla, o_gdn, gate, x], [g_mla, g_gdn, w_out, g_post],
                    [(2 * WIDE, BF16), (dm, F32), (dm, F32)], tm=MIX_TM)


def _mix_bwd(o_mla, o_gdn, gate, mixed, dy, g_mla, g_gdn, w_out, g_post):
    dm = mixed.shape[1]

    def fn(rows, consts):
        om, og, gt, mx, dyv = rows
        gm, gg, wo, gp = consts
        _, pull_post = jax.vjp(lambda hv, gv: _rms(hv, gv, dm), mx, gp)
        dmixed, dgp = pull_post(dyv)
        dmixed = dmixed.astype(BF16)
        dc = _nt(dmixed, wo)
        _, pull = jax.vjp(lambda x, g: _rms(x, g, N_HEADS * MLA_V), om, gm)
        dom, dgm = pull(dc[:, :WIDE])
        dn_out = dc[:, WIDE:]
        r = lax.rsqrt(_slot_sum(og * og) * (1.0 / GDN_D) + EPS)
        sig = _sigmoid(gt)
        normed = og * r
        dn = dn_out * gg * (gt * sig)
        dog = r * dn - normed * (r * r) * _slot_sum(dn * og) * (1.0 / GDN_D)
        dgt = dn_out * normed * gg * (sig * (1.0 + gt * (1.0 - sig)))
        dgg = jnp.sum(dn_out * normed * (gt * sig), axis=0, keepdims=True)
        return [dmixed, dom, _slot_sum(dom * om), dog, dgt], [dgp, dgm, dgg]

    return _rowwise("mix_bwd", fn, [o_mla, o_gdn, gate, mixed, dy], [g_mla, g_gdn, w_out, g_post],
                    [(dm, BF16), (WIDE, F32), (WIDE, F32), (WIDE, F32), (WIDE, BF16)], sums=[dm, WIDE, WIDE], tm=MIX_TM)


def _proj_fwd(x, g, weights):
    dm = x.shape[1]

    def fn(rows, consts):
        hn = _rms(rows[0], consts[0], dm).astype(BF16)
        return [hn] + [_nt(hn, wv) for wv in consts[1:]], []

    return _rowwise("proj_fwd", fn, [x], [g, *weights], [(dm, BF16)] + [(wv.shape[0], F32) for wv in weights], tm=MIX_TM)


def _proj_bwd(x, g, weights, cots, dy):
    dm = x.shape[1]
    n = len(weights)

    def fn(rows, consts):
        xv, dyv, *parts = rows
        dn = _nn(parts[0], consts[1])
        for p, wv in zip(parts[1:], consts[2:]):
            dn = dn + _nn(p, wv)
        _, pull = jax.vjp(lambda a, gv: _rms(a, gv, dm), xv, consts[0])
        dx, dg = pull(dn)
        return [dyv + dx], [dg]

    assert len(cots) == n
    return _rowwise("proj_bwd", fn, [x, dy, *cots], [g, *weights], [(dm, F32)], sums=[dm], tm=MIX_TM)


def _loss_fwd(y, target):
    dm = y.shape[1]

    def fn(rows, consts):
        err = rows[0] - rows[1]
        sq = err * err
        lanes = sq[:, :SLOT]
        for j in range(1, dm // SLOT):
            lanes = lanes + sq[:, j * SLOT:(j + 1) * SLOT]
        return [err * (1.0 / dm)], [jnp.sum(lanes, axis=0, keepdims=True) * (0.5 / dm)]

    return _rowwise("loss", fn, [y, target], [], [(dm, F32)], sums=[SLOT])


W_IN_CUTS = (0, 256, 384, 416, 1952, 1960, 1968, 2480)


def _heads_out(w, per_head, axis=-1):
    axis = axis % w.ndim
    shape = w.shape
    n = shape[axis] // per_head
    w = w.reshape(shape[:axis] + (n, per_head) + shape[axis + 1:])
    pad = [(0, 0)] * w.ndim
    pad[axis + 1] = (0, SLOT - per_head)
    return jnp.pad(w, pad).reshape(shape[:axis] + (n * SLOT,) + shape[axis + 1:])


def _heads_in(w, per_head, axis=-1):
    axis = axis % w.ndim
    shape = w.shape
    n = shape[axis] // SLOT
    w = w.reshape(shape[:axis] + (n, SLOT) + shape[axis + 1:])
    w = lax.slice_in_dim(w, 0, per_head, axis=axis + 1)
    return w.reshape(shape[:axis] + (n * per_head,) + shape[axis + 1:])


def _pad_lanes(v, lo, width=SLOT):
    return jnp.pad(v, [(0, 0)] * (v.ndim - 1) + [(lo, width - lo - v.shape[-1])])


def _pad_rows(v, lo, rows=SLOT):
    return jnp.pad(v, [(lo, rows - lo - v.shape[0])] + [(0, 0)] * (v.ndim - 1))


def _layout_weights(w):
    c = W_IN_CUTS
    w_in = w["w_in_t"]
    p = {}
    p["w_a"] = jnp.concatenate([w_in[c[0]:c[2]], _pad_rows(w_in[c[2]:c[3]], MLA_NOPE), _pad_rows(w_in[c[4]:c[6]], 0)], axis=0)
    p["w_qkv"] = _heads_out(w_in[c[3]:c[4]], GDN_D, axis=0)
    p["w_gate"] = _heads_out(w_in[c[6]:c[7]], GDN_D, axis=0)
    p["w_uq"] = _heads_out(w["uq_t"], MLA_NOPE + MLA_ROPE, axis=0)
    ukv = w["ukv_t"].reshape(N_HEADS, MLA_NOPE + MLA_V, MLA_KV_RANK)
    p["w_kv"] = jnp.concatenate([_heads_out(ukv[:, :MLA_NOPE].reshape(-1, MLA_KV_RANK), MLA_NOPE, axis=0),
                                 _heads_out(ukv[:, MLA_NOPE:].reshape(-1, MLA_KV_RANK), MLA_V, axis=0)], axis=0)
    p["conv"] = _heads_out(w["gdn_conv_w"], GDN_D)
    p["g_mla_out"] = _heads_out(w["mla_out_g"], MLA_V)
    p["g_gdn"] = jnp.tile(_pad_lanes(w["gdn_norm_g"], 0), (1, N_HEADS))
    p["a_log"] = _pad_lanes(w["gdn_a_log"], 0)
    p["dt_bias"] = _pad_lanes(w["gdn_dt_bias"], 0)
    return p


def _unlayout_grads(d):
    c = W_IN_CUTS
    g = {}
    da = d["w_a"]
    kpe0 = A_KPE + MLA_NOPE
    g["w_in_t"] = jnp.concatenate([da[:A_KPE], da[kpe0:kpe0 + MLA_ROPE], _heads_in(d["w_qkv"], GDN_D, axis=0),
                                   da[A_AB:A_AB + 2 * N_HEADS], _heads_in(d["w_gate"], GDN_D, axis=0)], axis=0)
    assert g["w_in_t"].shape[0] == c[-1]
    g["uq_t"] = _heads_in(d["w_uq"], MLA_NOPE + MLA_ROPE, axis=0)
    dk = _heads_in(d["w_kv"][:WIDE], MLA_NOPE, axis=0).reshape(N_HEADS, MLA_NOPE, MLA_KV_RANK)
    dv = _heads_in(d["w_kv"][WIDE:], MLA_V, axis=0).reshape(N_HEADS, MLA_V, MLA_KV_RANK)
    g["ukv_t"] = jnp.concatenate([dk, dv], axis=1).reshape(-1, MLA_KV_RANK)
    g["w_out"] = _heads_in(d["w_out"], GDN_D, axis=0)
    g["gdn_conv_w"] = _heads_in(d["conv"], GDN_D)
    g["mla_out_g"] = _heads_in(d["g_mla_out"], MLA_V)
    g["gdn_norm_g"] = jnp.sum(d["g_gdn"].reshape(N_HEADS, SLOT), axis=0, keepdims=True)[:, :GDN_D]
    g["gdn_a_log"] = d["a_log"][:, :N_HEADS]
    g["gdn_dt_bias"] = d["dt_bias"][:, :N_HEADS]
    return g


def _weight_grad(name, cots, acts, out_dtype=F32, tm=1024, tn=1024, tk=2048, after=None):
    return _matmul(name, cots, acts, "tn", out_dtype=out_dtype, tm=tm, tn=tn, tk=tk, after=after)


def _by_device(a):
    return a.astype(BF16).reshape((N_DEV, a.shape[0] // N_DEV) + a.shape[1:])


def _rows_of(blocks):
    return blocks.reshape((-1,) + blocks.shape[2:])


def _local_step(x, positions, target, w, mid, late):
    tabs = _rope_tables(positions)

    (h1, x1, hg1, hu1), gathered = _ffn_fwd("ffn1_fwd", x, w["ffn1_pre_g"], w["ffn1"], 0, w["ffn1_post_g"], carry=mid)
    w = dict(w, w_in_t=_rows_of(gathered[0]), uq_t=_rows_of(gathered[1]), ukv_t=_rows_of(gathered[2]))
    p = _layout_weights(w)
    in_weights = [p["w_a"], p["w_qkv"], p["w_gate"]]
    hn, proj_a, proj_qkv, proj_gate = _proj_fwd(x1, w["mix_pre_g"], in_weights)
    cqn, ckvn, q, k, v = _mla_front_fwd(proj_a, tabs, w["mla_q_norm_g"], w["mla_kv_norm_g"], p["w_uq"], p["w_kv"])
    o_mla, lse = _attn_fwd(q, k, v)
    ab = (proj_a, SLOT, A_AB // SLOT)
    qkv_n = _gdn_conv_fwd(proj_qkv, p["conv"])
    gb, bb = _gates_fwd(ab, p["a_log"], p["dt_bias"])
    (o_gdn, keep), (ffn2, w_out) = _gdn_fwd(qkv_n, gb, bb, carry=late)
    p["w_out"] = _heads_out(_rows_of(w_out), GDN_D, axis=0)
    cat, mixed, x2 = _mix_fwd(o_mla, o_gdn, proj_gate, x1, p["g_mla_out"], p["g_gdn"], p["w_out"], w["mix_post_g"])
    (h2, y, hg2, hu2), _ = _ffn_fwd("ffn2_fwd", x2, w["ffn2_pre_g"], ffn2, 0, w["ffn2_post_g"])
    dy, loss_lanes = _loss_fwd(y, target)

    g = {}
    (dx2, xn2, dh2, a2, dhg2, dhu2, g["ffn2_pre_g"], g["ffn2_post_g"]), _ = _ffn_bwd(
        "ffn2_bwd", x2, h2, hg2, hu2, dy, w["ffn2_pre_g"], ffn2, 0, w["ffn2_post_g"])
    ffn2_grads = _Scatter([_by_device(_weight_grad("ffn2_dw_gate", dhg2, xn2, BF16, tm=1408)),
                           _by_device(_weight_grad("ffn2_dw_up", dhu2, xn2, BF16, tm=1408)),
                           _by_device(_weight_grad("ffn2_dw_down", a2, dh2, BF16, tm=1408))])
    d = {}
    dmixed, do_mla, delta, do_gdn, dgate, g["mix_post_g"], d["g_mla_out"], d["g_gdn"] = _mix_bwd(
        o_mla, o_gdn, proj_gate, mixed, dx2, p["g_mla_out"], p["g_gdn"], p["w_out"], w["mix_post_g"])
    d["w_out"] = _weight_grad("mix_out_dw", cat, dmixed)
    dq, dk, dv = _attn_bwd(q, k, v, do_mla, lse, delta)
    (dqkv_n, dgb, dbb), landed_ffn2 = _gdn_bwd(qkv_n, gb, bb, keep, do_gdn, carry=ffn2_grads)
    dab, d["a_log"], d["dt_bias"] = _gates_bwd(ab, p["a_log"], p["dt_bias"], dgb, dbb)
    dproj_qkv, d["conv"] = _gdn_conv_bwd(proj_qkv, p["conv"], dqkv_n)
    dproj_a, dq_p, dkv_p, g["mla_q_norm_g"], g["mla_kv_norm_g"] = _mla_front_bwd(
        proj_a, tabs, w["mla_q_norm_g"], w["mla_kv_norm_g"], p["w_uq"], p["w_kv"], dq, dk, dv, dab)
    d["w_uq"] = _weight_grad("mla_q_dw", dq_p, cqn)
    d["w_kv"] = _weight_grad("mla_kv_dw", dkv_p, ckvn)
    d["w_a"] = _weight_grad("proj_a_dw", dproj_a, hn, tm=640)
    d["w_qkv"] = _weight_grad("proj_qkv_dw", dproj_qkv, hn)
    d["w_gate"] = _weight_grad("proj_gate_dw", dgate, hn)
    dx1, g["mix_pre_g"] = _proj_bwd(x1, w["mix_pre_g"], in_weights, [dproj_a, dproj_qkv, dgate], dx2)
    g.update(_unlayout_grads(d))
    others = list(OTHER.values())
    (dx, xn1, dh1, a1, dhg1, dhu1, g["ffn1_pre_g"], g["ffn1_post_g"]), landed_others = _ffn_bwd(
        "ffn1_bwd", x, h1, hg1, hu1, dx1, w["ffn1_pre_g"], w["ffn1"], 0, w["ffn1_post_g"], carry=_Scatter([_by_device(g.pop(t)) for t in others]))
    landed = dict(zip(list(FFN_NAMES[3:]) + list(OTHER), list(landed_ffn2) + list(landed_others)))
    begun, token = {}, None
    for name, cots, acts in (("ffn1_w_down", a1, dh1), ("ffn1_w_gate", dhg1, xn1), ("ffn1_w_up", dhu1, xn1)):
        blocks = _by_device(_weight_grad(name + "_grad", cots, acts, BF16, tm=1408, after=token))
        begun[name], token = _scatter_begin("scatter_" + name + "_begin", blocks)
        begun[name] = (begun[name], blocks)
    return loss_lanes, dx, g, landed, begun, token


MESH_AXES = ("x", "y", "c")
N_LINKS = N_DEV - 1


def _place():
    return tuple(lax.axis_index(a) for a in MESH_AXES)


def _block_of(dev):
    x, y, c = dev
    return 4 * x + 2 * y + c


def _remote_copy(src, dst, sems, k, to):
    send_sems, recv_sems = sems
    return pltpu.make_async_remote_copy(src_ref=src, dst_ref=dst, send_sem=send_sems.at[k], recv_sem=recv_sems.at[k],
                                        device_id=to, device_id_type=pl.DeviceIdType.MESH)


class _Exchange:
    def __init__(self, arrays):
        self.arrays = list(arrays)
        self.n = len(self.arrays)
        self.specs = [pl.BlockSpec(memory_space=pl.ANY)] * self.n
        self.scratch = [pltpu.SemaphoreType.DMA((self.n * N_LINKS,)), pltpu.SemaphoreType.DMA((self.n * N_LINKS,)),
                        pltpu.SemaphoreType.DMA((self.n,))]

    def split(self, refs):
        n = self.n
        return refs[:n], refs[n:2 * n], (refs[2 * n], refs[2 * n + 1]), refs[2 * n + 2]


class _Gather(_Exchange):
    def out_shape(self):
        return [jax.ShapeDtypeStruct((N_DEV,) + a.shape, a.dtype) for a in self.arrays]

    def _plan(self, ins, outs, sems, local_sems):
        x, y, c = _place()
        me, sibling = (x, y, c), (x, y, 1 - c)
        chips = [(1 - x, y), (x, 1 - y), (1 - x, 1 - y)]

        def copy(a, k, block, to, mine=False):
            src = ins[a] if mine else outs[a].at[_block_of(block)]
            return _remote_copy(src, outs[a].at[_block_of(block)], sems, a * N_LINKS + k, to)

        local = [pltpu.make_async_copy(ins[a], outs[a].at[_block_of(me)], local_sems.at[a]) for a in range(self.n)]
        first = []
        for a in range(self.n):
            first.append(copy(a, 0, me, sibling, mine=True))
            first += [copy(a, 1 + j, me, (*chip, c), mine=True) for j, chip in enumerate(chips)]
        return me, sibling, chips, c, copy, local, first

    def start(self, ins, outs, sems, local_sems):
        *_, local, first = self._plan(ins, outs, sems, local_sems)
        for cp in local + first:
            cp.start()

    def finish(self, ins, outs, sems, local_sems):
        me, sibling, chips, c, copy, local, first = self._plan(ins, outs, sems, local_sems)
        passed = []
        for j, chip in enumerate(chips):
            for a in range(self.n):
                copy(a, 1 + j, (*chip, c), me).wait_recv()
                passed.append(copy(a, 4 + j, (*chip, c), sibling))
                passed[-1].start()
        for a in range(self.n):
            copy(a, 0, sibling, me).wait_recv()
            for j, chip in enumerate(chips):
                copy(a, 4 + j, (*chip, 1 - c), me).wait_recv()
        for cp in first + passed:
            cp.wait_send()
        for cp in local:
            cp.wait()


class _Scatter(_Exchange):
    def out_shape(self):
        return [jax.ShapeDtypeStruct(a.shape, a.dtype) for a in self.arrays]

    def _plan(self, ins, outs, sems, local_sems):
        x, y, c = _place()
        me = _block_of((x, y, c))

        def peer(r):
            return (1 - x if r & 4 else x, 1 - y if r & 2 else y, 1 - c if r & 1 else c)

        local = [pltpu.make_async_copy(ins[a].at[me], outs[a].at[me], local_sems.at[a]) for a in range(self.n)]
        sends = [_remote_copy(ins[a].at[_block_of(peer(r))], outs[a].at[me], sems, a * N_LINKS + r - 1, peer(r))
                 for a in range(self.n) for r in range(1, N_DEV)]
        arrivals = [_remote_copy(ins[a].at[me], outs[a].at[_block_of(peer(r))], sems, a * N_LINKS + r - 1, peer(r))
                    for a in range(self.n) for r in range(1, N_DEV)]
        return local, sends, arrivals

    def start(self, ins, outs, sems, local_sems):
        local, sends, _ = self._plan(ins, outs, sems, local_sems)
        for cp in local + sends:
            cp.start()

    def finish(self, ins, outs, sems, local_sems):
        local, sends, arrivals = self._plan(ins, outs, sems, local_sems)
        for cp in arrivals:
            cp.wait_recv()
        for cp in sends:
            cp.wait_send()
        for cp in local:
            cp.wait()


def _exchange(name, plan):
    def body(*refs):
        parts = plan.split(refs)
        plan.start(*parts)
        plan.finish(*parts)

    return pl.pallas_call(
        body, name=name,
        in_specs=plan.specs,
        out_specs=plan.specs,
        out_shape=plan.out_shape(),
        scratch_shapes=plan.scratch,
    )(*plan.arrays)


def _call_carrying(body, plan, operands, *, name, grid, in_specs, out_specs, out_shape, scratch_shapes, compiler_params):
    if plan is None:
        outs = pl.pallas_call(body, name=name, grid=grid, in_specs=in_specs, out_specs=out_specs, out_shape=out_shape,
                              scratch_shapes=scratch_shapes, compiler_params=compiler_params)(*operands)
        return outs, []
    n_i, n_o, n_s, k = len(in_specs), len(out_specs), len(scratch_shapes), plan.n

    def whole(*refs):
        cut = [n_i, n_i + k, n_i + k + n_o, n_i + 2 * k + n_o, n_i + 2 * k + n_o + n_s]
        own_in, ex_in, own_out, ex_out, own_scr, ex_scr = (refs[a:b] for a, b in zip([0] + cut, cut + [len(refs)]))
        parts = plan.split(ex_in + ex_out + ex_scr)
        first = last = True
        for axis, size in enumerate(grid):
            first = first & (pl.program_id(axis) == 0)
            last = last & (pl.program_id(axis) == size - 1)

        @pl.when(first)
        def _():
            plan.start(*parts)

        body(*own_in, *own_out, *own_scr)

        @pl.when(last)
        def _():
            plan.finish(*parts)

    outs = pl.pallas_call(
        whole, name=name, grid=grid,
        in_specs=list(in_specs) + plan.specs, out_specs=list(out_specs) + plan.specs,
        out_shape=list(out_shape) + plan.out_shape(), scratch_shapes=list(scratch_shapes) + plan.scratch,
        compiler_params=compiler_params,
    )(*operands, *plan.arrays)
    return outs[:n_o], outs[n_o:]


def _row_tile(rows, target=256):
    best = rows
    for cand in range(16, min(rows, target) + 1, 16):
        if rows % cand == 0:
            best = cand
    return best


def _sum_blocks(name, blocks, after=None):
    rows, width = blocks.shape[-2:]
    tm = _row_tile(rows)

    def body(x_ref, *rest):
        acc = x_ref[0].astype(F32)
        for d in range(1, N_DEV):
            acc = acc + x_ref[d].astype(F32)
        rest[-1][...] = acc

    ordered = [] if after is None else [after]
    return pl.pallas_call(
        body, name=name,
        grid=(rows // tm,),
        in_specs=[pl.BlockSpec((N_DEV, tm, width), lambda i: (0, i, 0))] + [pl.BlockSpec(x.shape, lambda i: (0, 0)) for x in ordered],
        out_specs=pl.BlockSpec((tm, width), lambda i: (i, 0)),
        out_shape=jax.ShapeDtypeStruct((rows, width), F32),
        compiler_params=pltpu.CompilerParams(dimension_semantics=("parallel",)),
    )(blocks, *ordered)


def _split_plan(src_ref, land_ref, sems):
    x, y, c = _place()
    me = _block_of((x, y, c))

    def peer(r):
        return (1 - x if r & 4 else x, 1 - y if r & 2 else y, 1 - c if r & 1 else c)

    sends = [_remote_copy(src_ref.at[_block_of(peer(r))], land_ref.at[me], sems, r - 1, peer(r)) for r in range(1, N_DEV)]
    arrivals = [_remote_copy(src_ref.at[me], land_ref.at[_block_of(peer(r))], sems, r - 1, peer(r)) for r in range(1, N_DEV)]
    return sends, arrivals


def _scatter_begin(name, blocks):
    def body(src_ref, land_ref, send_sems, recv_sems, src_thru, land_thru, token_ref):
        for cp in _split_plan(src_ref, land_ref, (send_sems, recv_sems))[0]:
            cp.start()
        token_ref[...] = jnp.zeros_like(token_ref)

    hbm, sem = pl.BlockSpec(memory_space=pltpu.HBM), pl.BlockSpec(memory_space=pltpu.SEMAPHORE)
    zone = pltpu.HBM(blocks.shape, blocks.dtype)
    *handles, token = pl.pallas_call(
        body, name=name,
        in_specs=(hbm, hbm),
        out_specs=(sem, sem, hbm, hbm, pl.BlockSpec(memory_space=pltpu.VMEM)),
        out_shape=(pltpu.SemaphoreType.DMA((N_LINKS,)), pltpu.SemaphoreType.DMA((N_LINKS,)), zone, zone,
                   jax.ShapeDtypeStruct((8, SLOT), F32)),
        input_output_aliases={0: 2, 1: 3},
        compiler_params=pltpu.CompilerParams(has_side_effects=pltpu.SideEffectType.DATAFLOW_SIDE_EFFECTING),
    )(pltpu.with_memory_space_constraint(blocks, pltpu.HBM),
      pltpu.with_memory_space_constraint(lax.empty(blocks.shape, blocks.dtype), pltpu.HBM))
    return handles, token


def _scatter_end(name, handles, own, after):
    send_sems, recv_sems, src, zone = handles

    def body(src_ref, land_ref, send_sems, recv_sems, after_ref, src_dead, got_ref):
        sends, arrivals = _split_plan(src_ref, land_ref, (send_sems, recv_sems))
        for cp in arrivals:
            cp.wait_recv()
        for cp in sends:
            cp.wait_send()

    hbm, sem = pl.BlockSpec(memory_space=pltpu.HBM), pl.BlockSpec(memory_space=pltpu.SEMAPHORE)
    landed = pl.pallas_call(
        body, name=name,
        in_specs=(hbm, hbm, sem, sem, pl.BlockSpec(memory_space=pl.ANY)),
        out_specs=(hbm, hbm),
        out_shape=(pltpu.HBM(src.shape, src.dtype), pltpu.HBM(zone.shape, zone.dtype)),
        input_output_aliases={0: 0, 1: 1},
        compiler_params=pltpu.CompilerParams(has_side_effects=pltpu.SideEffectType.DATAFLOW_SIDE_EFFECTING),
    )(src, zone, send_sems, recv_sems, after)[1]
    return lax.dynamic_update_slice(landed, own, (_block_of(_place()),) + (0,) * (own.ndim - 1))


def _all_reduce_small(name, vec):
    rows, width = vec.shape

    def body(x_ref, o_ref, all_ref, send_sems, recv_sems):
        x, y, c = _place()
        me = _block_of((x, y, c))
        all_ref[me] = x_ref[...]

        def peer(r):
            return (1 - x if r & 4 else x, 1 - y if r & 2 else y, 1 - c if r & 1 else c)

        def copy(r, block):
            return _remote_copy(x_ref, all_ref.at[block], (send_sems, recv_sems), r - 1, peer(r))

        sends = [copy(r, me) for r in range(1, N_DEV)]
        for cp in sends:
            cp.start()
        for r in range(1, N_DEV):
            copy(r, _block_of(peer(r))).wait_recv()
        for cp in sends:
            cp.wait_send()
        acc = all_ref[0]
        for d in range(1, N_DEV):
            acc = acc + all_ref[d]
        o_ref[...] = acc

    return pl.pallas_call(
        body, name=name,
        in_specs=[pl.BlockSpec(memory_space=pltpu.VMEM)],
        out_specs=pl.BlockSpec(memory_space=pltpu.VMEM),
        out_shape=jax.ShapeDtypeStruct((rows, width), F32),
        scratch_shapes=[pltpu.VMEM((N_DEV, rows, width), F32), pltpu.SemaphoreType.DMA((N_LINKS,)), pltpu.SemaphoreType.DMA((N_LINKS,))],
    )(vec)


def _adamw(name, w, g, m, v):
    def fn(rows, consts):
        wv, gv, mv, vv = rows
        m2 = ADAM_B1 * mv + (1.0 - ADAM_B1) * gv
        v2 = ADAM_B2 * vv + (1.0 - ADAM_B2) * jnp.square(gv)
        m_hat = m2 / (1.0 - ADAM_B1 ** ADAM_STEP)
        v_hat = v2 / (1.0 - ADAM_B2 ** ADAM_STEP)
        return [-ADAM_LR * (m_hat / (jnp.sqrt(v_hat) + ADAM_EPS) + ADAM_WD * wv), m2, v2], []

    return _rowwise(name, fn, [w, g, m, v], [], [(w.shape[1], F32)] * 3, tm=_row_tile(w.shape[0]))


ROW = 1024
FFN_NAMES = ("ffn1_w_gate", "ffn1_w_up", "ffn1_w_down", "ffn2_w_gate", "ffn2_w_up", "ffn2_w_down")
OTHER = {"w_in": "w_in_t", "mla_w_uq": "uq_t", "mla_w_ukv": "ukv_t", "w_out": "w_out"}
BY_COLUMNS = ("ffn1_w_gate", "ffn1_w_up", "ffn2_w_gate", "ffn2_w_up", "w_in", "mla_w_uq", "mla_w_ukv")
SMALL = {
    "ffn1_pre_g": (1024, 1024), "ffn1_post_g": (1024, 1024), "mix_pre_g": (1024, 1024), "mla_q_norm_g": (256, 256),
    "mla_kv_norm_g": (128, 128), "mla_out_g": (512, 512), "gdn_a_log": (8, 128), "gdn_dt_bias": (8, 128),
    "gdn_norm_g": (64, 128), "mix_post_g": (1024, 1024), "ffn2_pre_g": (1024, 1024), "ffn2_post_g": (1024, 1024),
}
CONV_SHAPE = (GDN_CONV, 3 * N_HEADS * GDN_D)
CONV_SHARD = (GDN_CONV, CONV_SHAPE[1] // N_DEV)
CONV_LANES = CONV_SHAPE[0] * CONV_SHAPE[1]
SMALL_ROWS = 8
REDUCE_ROWS = 16


def _pack_small(vecs, conv, rows):
    parts = [_pad_lanes(vecs[n].reshape(1, -1), 0, r) for n, (_, r) in SMALL.items()]
    parts.append(conv.reshape(1, -1))
    flat = jnp.concatenate(parts, axis=1)
    return _pad_lanes(flat, 0, rows * ROW).reshape(rows, ROW)


def _unpack_small(buf):
    flat = buf.reshape(1, -1)
    out, at = {}, 0
    for n, (w, r) in SMALL.items():
        out[n] = flat[:, at:at + w]
        at += r
    return out, flat[0, at:]


def kernel(x, positions, ffn1_pre_g, ffn1_w_gate, ffn1_w_up, ffn1_w_down, ffn1_post_g, mix_pre_g, w_in, mla_q_norm_g, mla_w_uq, mla_kv_norm_g, mla_w_ukv, mla_out_g, gdn_conv_w, gdn_a_log, gdn_dt_bias, gdn_norm_g, w_out, mix_post_g, ffn2_pre_g, ffn2_w_gate, ffn2_w_up, ffn2_w_down, ffn2_post_g, loss_target, m_ffn1_pre_g, m_ffn1_w_gate, m_ffn1_w_up, m_ffn1_w_down, m_ffn1_post_g, m_mix_pre_g, m_w_in, m_mla_q_norm_g, m_mla_w_uq, m_mla_kv_norm_g, m_mla_w_ukv, m_mla_out_g, m_gdn_conv_w, m_gdn_a_log, m_gdn_dt_bias, m_gdn_norm_g, m_w_out, m_mix_post_g, m_ffn2_pre_g, m_ffn2_w_gate, m_ffn2_w_up, m_ffn2_w_down, m_ffn2_post_g, v_ffn1_pre_g, v_ffn1_w_gate, v_ffn1_w_up, v_ffn1_w_down, v_ffn1_post_g, v_mix_pre_g, v_w_in, v_mla_q_norm_g, v_mla_w_uq, v_mla_kv_norm_g, v_mla_w_ukv, v_mla_out_g, v_gdn_conv_w, v_gdn_a_log, v_gdn_dt_bias, v_gdn_norm_g, v_w_out, v_mix_post_g, v_ffn2_pre_g, v_ffn2_w_gate, v_ffn2_w_up, v_ffn2_w_down, v_ffn2_post_g):
    given = dict(locals())
    order = ["ffn1_pre_g", "ffn1_w_gate", "ffn1_w_up", "ffn1_w_down", "ffn1_post_g", "mix_pre_g", "w_in", "mla_q_norm_g",
             "mla_w_uq", "mla_kv_norm_g", "mla_w_ukv", "mla_out_g", "gdn_conv_w", "gdn_a_log", "gdn_dt_bias", "gdn_norm_g",
             "w_out", "mix_post_g", "ffn2_pre_g", "ffn2_w_gate", "ffn2_w_up", "ffn2_w_down", "ffn2_post_g"]
    assert sorted(order) == sorted(list(FFN_NAMES) + list(OTHER) + list(SMALL) + ["gdn_conv_w"])

    def drop_depth(a):
        return a[0] if a.ndim == 3 else a

    wts = {n: drop_depth(given[n]) for n in order}
    mom = {n: drop_depth(given["m_" + n]) for n in order}
    var = {n: drop_depth(given["v_" + n]) for n in order}
    me = _block_of(_place())

    def wire(n):
        return (wts[n].T if n in BY_COLUMNS else wts[n]).astype(BF16)

    (ffn1,) = _exchange("gather_first", _Gather([jnp.stack([wire(n) for n in FFN_NAMES[:3]])]))
    mid = _Gather([wire(n) for n in ("w_in", "mla_w_uq", "mla_w_ukv")])
    late = _Gather([jnp.stack([wire(n) for n in FFN_NAMES[3:]]), wire("w_out")])
    conv_at = lax.dynamic_update_slice(jnp.zeros((N_DEV, CONV_SHARD[0] * CONV_SHARD[1]), F32),
                                       wts["gdn_conv_w"].reshape(1, -1), (me, 0))
    conv_all = _all_reduce_small("gather_conv", _pad_lanes(conv_at.reshape(1, -1), 0, SMALL_ROWS * ROW).reshape(SMALL_ROWS, ROW))
    full = {n: wts[n] for n in SMALL}
    full["ffn1"] = ffn1
    full["gdn_conv_w"] = conv_all.reshape(-1)[:CONV_LANES].reshape((N_DEV,) + CONV_SHARD).transpose(1, 0, 2).reshape(CONV_SHAPE)

    loss_lanes, dx, grads, landed, begun, token = _local_step(x[0], positions[0], loss_target[0], full, mid, late)
    loss = lax.psum(jnp.sum(loss_lanes), MESH_AXES)

    grad, outs = {}, {"delta": {}, "new_m": {}, "new_v": {}}

    def finish(n, blocks, after=None):
        total = _sum_blocks("sum_" + n, blocks, after=after)
        grad[n] = total.T if n in BY_COLUMNS else total
        outs["delta"][n], outs["new_m"][n], outs["new_v"][n] = _adamw("adamw_" + n, wts[n], grad[n], mom[n], var[n])

    for n, blocks in landed.items():
        finish(n, blocks, after=token)
    fillers = list(landed)
    for i, (n, (handles, blocks)) in enumerate(begun.items()):
        own = lax.dynamic_slice_in_dim(blocks, me, 1, axis=0)
        finish(n, _scatter_end("scatter_" + n + "_end", handles, own, after=outs["new_v"][fillers[-1 - i]]))

    small_sum = _all_reduce_small("reduce_small", _pack_small(grads, grads["gdn_conv_w"].reshape(-1), REDUCE_ROWS))
    small_grad, conv_grad_full = _unpack_small(small_sum)
    grad.update(small_grad)
    grad["gdn_conv_w"] = lax.dynamic_slice(conv_grad_full[:CONV_LANES].reshape(CONV_SHAPE), (0, me * CONV_SHARD[1]), CONV_SHARD)
    outs["grad"] = grad
    small = [_pack_small(s, s["gdn_conv_w"].reshape(-1), SMALL_ROWS) for s in (wts, grad, mom, var)]
    for kind, s in zip(("delta", "new_m", "new_v"), _adamw("adamw_small", *small)):
        vecs, conv = _unpack_small(s)
        outs[kind].update(vecs)
        outs[kind]["gdn_conv_w"] = conv[:CONV_SHARD[0] * CONV_SHARD[1]].reshape(CONV_SHARD)
    result = [loss, dx[None]]
    for kind in ("grad", "delta", "new_m", "new_v"):
        result += [outs[kind][n].reshape(given[n].shape) for n in order]
    return tuple(result)
```

```python
import jax
import jax.numpy as jnp
from jax import lax
from jax.experimental import pallas as pl
from jax.experimental.pallas import tpu as pltpu

F32 = jnp.float32
BF16 = jnp.bfloat16
HI = lax.Precision.HIGH

N_DEV = 8
N_HEADS = 8
SLOT = 128
MLA_Q_RANK = 256
MLA_KV_RANK = 128
MLA_NOPE = 64
MLA_ROPE = 32
MLA_V = 64
GDN_D = 64
GDN_CONV = 4
GDN_CHUNK = 64
ROPE_THETA = 10000.0
EPS = 1e-6
ADAM_LR, ADAM_B1, ADAM_B2, ADAM_EPS, ADAM_WD, ADAM_STEP = 0.001, 0.9, 0.999, 1e-08, 0.01, 10


def _dot(a, b, ca, cb, precision=None):
    lead = a.ndim - 2
    batch = tuple(range(lead))
    return lax.dot_general(a, b, (((lead + ca,), (lead + cb,)), (batch, batch)), precision=precision,
                           preferred_element_type=F32)


def _nn(a, b, precision=None):
    return _dot(a, b, 1, 0, precision)


def _nt(a, b, precision=None):
    return _dot(a, b, 1, 1, precision)


def _tn(a, b, precision=None):
    return _dot(a, b, 0, 0, precision)


def _sigmoid(x):
    return 1.0 / (1.0 + jnp.exp(-x))


def _silu(x):
    return x * _sigmoid(x)


def _rms(x, g, n):
    ms = jnp.sum(x * x, axis=-1, keepdims=True) * (1.0 / n)
    return x * lax.rsqrt(ms + EPS) * g


def _chunk_masks():
    c = GDN_CHUNK
    i = lax.broadcasted_iota(jnp.int32, (c, c), 0)
    j = lax.broadcasted_iota(jnp.int32, (c, c), 1)
    lower = i >= j
    strict = i > j
    eye = (i == j).astype(F32)
    blocks = []
    b = 1
    while b < c:
        same = (i // (2 * b)) == (j // (2 * b))
        blocks.append(same & ((i % (2 * b)) >= b) & ((j % (2 * b)) < b))
        b *= 2
    return lower, strict, eye, blocks


def _unit_lower_inverse(low, eye, blocks):
    t = eye - jnp.where(blocks[0], low, 0.0)
    for m in blocks[1:]:
        lo = jnp.where(m, low, 0.0)
        t = t - _nn(t, _nn(lo, t, HI), HI)
    return t


@jax.custom_vjp
def _known_inverse(low, tinv):
    return tinv


def _known_inverse_fwd(low, tinv):
    return tinv, tinv


def _known_inverse_bwd(tinv, dt):
    return -_tn(tinv, _nt(dt, tinv, HI), HI), jnp.zeros_like(tinv)


_known_inverse.defvjp(_known_inverse_fwd, _known_inverse_bwd)

_PRODUCTS = {"nn": _nn, "nt": _nt, "tn": _tn}


@jax.custom_vjp
def _known_nn(a, b, c):
    return c


@jax.custom_vjp
def _known_nt(a, b, c):
    return c


@jax.custom_vjp
def _known_tn(a, b, c):
    return c


def _known_fwd(a, b, c):
    return c, (a, b, c)


_known_nn.defvjp(_known_fwd, lambda r, dc: (_nt(dc, r[1], HI), _tn(r[0], dc, HI), jnp.zeros_like(r[2])))
_known_nt.defvjp(_known_fwd, lambda r, dc: (_nn(dc, r[1], HI), _tn(dc, r[0], HI), jnp.zeros_like(r[2])))
_known_tn.defvjp(_known_fwd, lambda r, dc: (_nt(r[1], dc, HI), _nn(r[0], dc, HI), jnp.zeros_like(r[2])))
_KNOWN = {"nn": _known_nn, "nt": _known_nt, "tn": _known_tn}
GDN_PRODUCTS = 8
GDN_KEPT = 2 + GDN_PRODUCTS


def _gdn_chunk(q, k, v, gc, bb, s, masks, known=None):
    lower, strict, eye, blocks = masks
    made = []

    def product(kind, a, b):
        c = _PRODUCTS[kind](a, b, HI) if known is None else _KNOWN[kind](a, b, known[1 + len(made)])
        made.append(c)
        return c

    qs = q * (GDN_D ** -0.5)
    gct = jnp.swapaxes(gc, -1, -2)
    decay = jnp.exp(jnp.where(lower, gc - gct, -1e30))
    kb = k * bb
    low = jnp.where(strict, product("nt", kb, k) * decay, 0.0)
    tinv = _unit_lower_inverse(low, eye, blocks) if known is None else _known_inverse(low, known[0])
    eg = jnp.exp(gc)
    w = product("nn", tinv, kb * eg)
    u = product("nn", tinv, v * bb)
    attn = product("nt", qs, k) * decay
    last = lax.broadcasted_iota(jnp.int32, gc.shape[-2:], 0) == GDN_CHUNK - 1
    g_end = jnp.sum(jnp.where(last, gc, 0.0), axis=-2, keepdims=True)
    k_dec = k * jnp.exp(g_end - gc)
    v_new = u - product("nn", w, s)
    o = product("nn", qs * eg, s) + product("nn", attn, v_new)
    s_new = s * jnp.exp(g_end) + product("tn", k_dec, v_new)
    assert len(made) == GDN_PRODUCTS
    return o, s_new, [tinv] + made


GDN_GROUP = 8
GDN_GROUPS = N_HEADS // GDN_GROUP


def _group_heads(ref):
    return jnp.stack([ref[:, pl.ds(j * SLOT, GDN_D)] for j in range(GDN_GROUP)])


def _ungroup_heads(ref, val):
    pad = jnp.zeros((GDN_CHUNK, SLOT - GDN_D), F32)
    for j in range(GDN_GROUP):
        ref[:, pl.ds(j * SLOT, GDN_D)] = val[j]
        ref[:, pl.ds(j * SLOT + GDN_D, SLOT - GDN_D)] = pad


def _gdn_fwd(qkv, gb, bb, carry=None):
    t = qkv.shape[0]
    n_chunks = t // GDN_CHUNK
    d = GDN_D

    def body(q_ref, k_ref, v_ref, g_ref, b_ref, o_ref, keep_ref, s_ref):
        @pl.when(pl.program_id(1) == 0)
        def _():
            s_ref[...] = jnp.zeros_like(s_ref)

        s = s_ref[...]
        keep_ref[:, 0, 0] = s
        o, s_new, made = _gdn_chunk(*[_group_heads(r) for r in (q_ref, k_ref, v_ref, g_ref, b_ref)], s, _chunk_masks())
        for i, val in enumerate(made):
            keep_ref[:, 0, 1 + i] = val
        s_ref[...] = s_new
        _ungroup_heads(o_ref, o)

    def spec(kind=0):
        return pl.BlockSpec((GDN_CHUNK, GDN_GROUP * SLOT), lambda h, n: (n, kind * GDN_GROUPS + h))

    return _call_carrying(
        body, carry, (qkv, qkv, qkv, gb, bb), name="gdn_fwd",
        grid=(GDN_GROUPS, n_chunks),
        in_specs=[spec(0), spec(1), spec(2), spec(), spec()],
        out_specs=[spec(), pl.BlockSpec((GDN_GROUP, 1, GDN_KEPT, d, d), lambda h, n: (h, n, 0, 0, 0))],
        out_shape=[jax.ShapeDtypeStruct((t, N_HEADS * SLOT), F32), jax.ShapeDtypeStruct((N_HEADS, n_chunks, GDN_KEPT, d, d), F32)],
        scratch_shapes=[pltpu.VMEM((GDN_GROUP, d, d), F32)],
        compiler_params=pltpu.CompilerParams(dimension_semantics=("arbitrary", "arbitrary")),
    )


def _gdn_bwd(qkv, gb, bb, keep, do, carry=None):
    t = qkv.shape[0]
    n_chunks = t // GDN_CHUNK
    d = GDN_D

    def body(q_ref, k_ref, v_ref, g_ref, b_ref, keep_ref, do_ref, dqkv_ref, dg_ref, db_ref, ds_ref):
        @pl.when(pl.program_id(1) == 0)
        def _():
            ds_ref[...] = jnp.zeros_like(ds_ref)

        masks = _chunk_masks()
        known = [keep_ref[:, 0, 1 + i] for i in range(GDN_KEPT - 1)]
        _, pull = jax.vjp(lambda *a: _gdn_chunk(*a, masks, known)[:2],
                          *[_group_heads(r) for r in (q_ref, k_ref, v_ref, g_ref, b_ref)], keep_ref[:, 0, 0])
        dq, dk, dv, dg, db, ds = pull((_group_heads(do_ref), ds_ref[...]))
        ds_ref[...] = ds
        for i, val in enumerate((dq, dk, dv)):
            _ungroup_heads(dqkv_ref.at[i], val)
        _ungroup_heads(dg_ref, dg)
        _ungroup_heads(db_ref, db)

    def spec(kind=0):
        return pl.BlockSpec((GDN_CHUNK, GDN_GROUP * SLOT), lambda h, n: (n_chunks - 1 - n, kind * GDN_GROUPS + h))

    return _call_carrying(
        body, carry, (qkv, qkv, qkv, gb, bb, keep, do), name="gdn_bwd",
        grid=(GDN_GROUPS, n_chunks),
        in_specs=[spec(0), spec(1), spec(2), spec(), spec(),
                  pl.BlockSpec((GDN_GROUP, 1, GDN_KEPT, d, d), lambda h, n: (h, n_chunks - 1 - n, 0, 0, 0)), spec()],
        out_specs=[pl.BlockSpec((3, GDN_CHUNK, GDN_GROUP * SLOT), lambda h, n: (0, n_chunks - 1 - n, h)), spec(), spec()],
        out_shape=[jax.ShapeDtypeStruct((3, t, N_HEADS * SLOT), F32)] + [jax.ShapeDtypeStruct((t, N_HEADS * SLOT), F32)] * 2,
        scratch_shapes=[pltpu.VMEM((GDN_GROUP, d, d), F32)],
        compiler_params=pltpu.CompilerParams(dimension_semantics=("arbitrary", "arbitrary")),
    )


def _rowwise(name, fn, rows, consts, outs, sums=(), tm=512):
    rows = [x if isinstance(x, tuple) else (x, x.shape[1], 0) for x in rows]
    t = rows[0][0].shape[0]
    tm = min(tm, t)
    steps = t // tm
    n_r, n_c, n_o, n_s = len(rows), len(consts), len(outs), len(sums)

    def window(width, block):
        return pl.BlockSpec((tm, width), lambda i: (i, block))

    def body(*refs):
        r, c = refs[:n_r], refs[n_r:n_r + n_c]
        o, s = refs[n_r + n_c:n_r + n_c + n_o], refs[n_r + n_c + n_o:]
        vals, tot = fn([x[...] for x in r], [x[...] for x in c])
        for ref, val in zip(o, vals):
            ref[...] = val.astype(ref.dtype)
        if n_s:
            @pl.when(pl.program_id(0) == 0)
            def _():
                for ref in s:
                    ref[...] = jnp.zeros_like(ref)

            for ref, val in zip(s, tot):
                ref[...] += val

    return pl.pallas_call(
        body, name=name,
        grid=(steps,),
        in_specs=[window(w, b) for _, w, b in rows] + [pl.BlockSpec(x.shape, lambda i: (0, 0)) for x in consts],
        out_specs=[pl.BlockSpec((tm, w), lambda i: (i, 0)) for w, _ in outs]
        + [pl.BlockSpec((1, w), lambda i: (0, 0)) for w in sums],
        out_shape=[jax.ShapeDtypeStruct((t, w), dt) for w, dt in outs]
        + [jax.ShapeDtypeStruct((1, w), F32) for w in sums],
        compiler_params=pltpu.CompilerParams(dimension_semantics=("arbitrary",)),
    )(*[x for x, _, _ in rows], *consts)


def _tile(dim, target):
    if dim <= target:
        return dim
    best = None
    for cand in range(128, target + 1, 128):
        if dim % cand == 0:
            best = cand
    assert best is not None, (dim, target)
    return best


def _matmul(name, a, b, mode, out_dtype=F32, tm=1024, tn=1024, tk=2048, after=None):
    if mode == "nn":
        (m, k), n = a.shape, b.shape[1]
    elif mode == "nt":
        (m, k), n = a.shape, b.shape[0]
    else:
        (k, m), n = a.shape, b.shape[1]
    tm, tn, tk = _tile(m, tm), _tile(n, tn), _tile(k, tk)
    k_steps = k // tk
    product = {"nn": _nn, "nt": _nt, "tn": _tn}[mode]

    def body(a_ref, b_ref, *rest):
        o_ref, acc_ref = rest[-2:]
        part = product(a_ref[...].astype(BF16), b_ref[...].astype(BF16))
        if k_steps == 1:
            o_ref[...] = part.astype(o_ref.dtype)
        else:
            kk = pl.program_id(2)

            @pl.when(kk == 0)
            def _():
                acc_ref[...] = part

            @pl.when(kk > 0)
            def _():
                acc_ref[...] += part

            @pl.when(kk == k_steps - 1)
            def _():
                o_ref[...] = acc_ref[...].astype(o_ref.dtype)

    a_spec = pl.BlockSpec((tk, tm), lambda i, j, kk: (kk, i)) if mode == "tn" else pl.BlockSpec((tm, tk), lambda i, j, kk: (i, kk))
    b_spec = pl.BlockSpec((tn, tk), lambda i, j, kk: (j, kk)) if mode == "nt" else pl.BlockSpec((tk, tn), lambda i, j, kk: (kk, j))
    ordered = [] if after is None else [after]
    return pl.pallas_call(
        body, name=name,
        grid=(m // tm, n // tn, k_steps),
        in_specs=[a_spec, b_spec] + [pl.BlockSpec(memory_space=pl.ANY)] * len(ordered),
        out_specs=pl.BlockSpec((tm, tn), lambda i, j, kk: (i, j)),
        out_shape=jax.ShapeDtypeStruct((m, n), out_dtype),
        scratch_shapes=[pltpu.VMEM((tm, tn) if k_steps > 1 else (8, 128), F32)],
        compiler_params=pltpu.CompilerParams(dimension_semantics=("parallel", "parallel", "arbitrary")),
    )(a, b, *ordered)


FFN_TM = 512
FFN_BWD_TM = 256
FFN_BLOCKS = 4
FFN_GATE, FFN_UP, FFN_DOWN = 0, 1, 2


def _ffn_weight_specs(ffn_w, first):
    _, _, rows, dm = ffn_w.shape

    def spec(k):
        return pl.BlockSpec((FFN_BLOCKS, None, rows, dm), lambda i, j: (j, first + k, 0, 0))

    return [spec(FFN_GATE), spec(FFN_UP), spec(FFN_DOWN)], FFN_BLOCKS * rows


def _ffn_fwd(name, x, g_pre, ffn_w, first, g_post, carry=None):
    t, dm = x.shape
    tm = min(FFN_TM, t)
    w_specs, tf = _ffn_weight_specs(ffn_w, first)
    f_steps = N_DEV // FFN_BLOCKS

    def body(x_ref, gpre_ref, wg_ref, wu_ref, wd_ref, gpost_ref, h_ref, y_ref, hg_ref, hu_ref, xn_ref, acc_ref):
        j = pl.program_id(1)

        @pl.when(j == 0)
        def _():
            xn_ref[...] = _rms(x_ref[...], gpre_ref[...], dm).astype(BF16)
            acc_ref[...] = jnp.zeros_like(acc_ref)

        xn = xn_ref[...]
        wg, wu, wd = (r[...].reshape(tf, dm) for r in (wg_ref, wu_ref, wd_ref))
        hg, hu = _nt(xn, wg), _nt(xn, wu)
        hg_ref[...] = hg.astype(BF16)
        hu_ref[...] = hu.astype(BF16)
        a = _silu(hg) * hu
        acc_ref[...] += _nn(a.astype(BF16), wd)

        @pl.when(j == f_steps - 1)
        def _():
            h = acc_ref[...]
            h_ref[...] = h
            y_ref[...] = x_ref[...] + 0.5 * _rms(h, gpost_ref[...], dm)

    row = pl.BlockSpec((tm, dm), lambda i, j: (i, 0))
    vec = pl.BlockSpec((1, dm), lambda i, j: (0, 0))
    wide = pl.BlockSpec((tm, tf), lambda i, j: (i, j))
    return _call_carrying(
        body, carry, (x, g_pre, ffn_w, ffn_w, ffn_w, g_post), name=name,
        grid=(t // tm, f_steps),
        in_specs=[row, vec, *w_specs, vec],
        out_specs=[row, row, wide, wide],
        out_shape=[jax.ShapeDtypeStruct((t, dm), F32)] * 2 + [jax.ShapeDtypeStruct((t, f_steps * tf), BF16)] * 2,
        scratch_shapes=[pltpu.VMEM((tm, dm), BF16), pltpu.VMEM((tm, dm), F32)],
        compiler_params=pltpu.CompilerParams(dimension_semantics=("arbitrary", "arbitrary")),
    )


def _ffn_bwd(name, x, h, hg, hu, dy, g_pre, ffn_w, first, g_post, carry=None):
    t, dm = x.shape
    tm = min(FFN_BWD_TM, t)
    w_specs, tf = _ffn_weight_specs(ffn_w, first)
    f_steps = N_DEV // FFN_BLOCKS
    f = f_steps * tf

    def post(hv, g):
        return 0.5 * _rms(hv, g, dm)

    def pre(xv, g):
        return _rms(xv, g, dm)

    def body(x_ref, h_ref, dy_ref, hg_ref, hu_ref, gpre_ref, wg_ref, wu_ref, wd_ref, gpost_ref,
             dx_ref, xn_ref, dh_ref, a_ref, dhg_ref, dhu_ref, dgpre_ref, dgpost_ref, acc_ref):
        i, j = pl.program_id(0), pl.program_id(1)

        @pl.when((i == 0) & (j == 0))
        def _():
            dgpre_ref[...] = jnp.zeros_like(dgpre_ref)
            dgpost_ref[...] = jnp.zeros_like(dgpost_ref)

        @pl.when(j == 0)
        def _():
            xn_ref[...] = pre(x_ref[...], gpre_ref[...]).astype(BF16)
            _, pull = jax.vjp(post, h_ref[...], gpost_ref[...])
            dh, dg = pull(dy_ref[...])
            dh_ref[...] = dh.astype(BF16)
            dgpost_ref[...] += dg
            acc_ref[...] = jnp.zeros_like(acc_ref)

        wg, wu, wd = (r[...].reshape(tf, dm) for r in (wg_ref, wu_ref, wd_ref))
        hg, hu = hg_ref[...].astype(F32), hu_ref[...].astype(F32)
        da = _nt(dh_ref[...], wd)
        sig = _sigmoid(hg)
        act = hg * sig
        dhu = (da * act).astype(BF16)
        dhg = (da * hu * (sig * (1.0 + hg * (1.0 - sig)))).astype(BF16)
        a_ref[...] = (act * hu).astype(BF16)
        dhg_ref[...] = dhg
        dhu_ref[...] = dhu
        acc_ref[...] += _nn(dhg, wg) + _nn(dhu, wu)

        @pl.when(j == f_steps - 1)
        def _():
            _, pull = jax.vjp(pre, x_ref[...], gpre_ref[...])
            dx, dg = pull(acc_ref[...])
            dx_ref[...] = dy_ref[...] + dx
            dgpre_ref[...] += dg

    row = pl.BlockSpec((tm, dm), lambda i, j: (i, 0))
    vec = pl.BlockSpec((1, dm), lambda i, j: (0, 0))
    wide = pl.BlockSpec((tm, tf), lambda i, j: (i, j))
    return _call_carrying(
        body, carry, (x, h, dy, hg, hu, g_pre, ffn_w, ffn_w, ffn_w, g_post), name=name,
        grid=(t // tm, f_steps),
        in_specs=[row, row, row, wide, wide, vec, *w_specs, vec],
        out_specs=[row, row, row, wide, wide, wide, vec, vec],
        out_shape=[jax.ShapeDtypeStruct((t, dm), F32), jax.ShapeDtypeStruct((t, dm), BF16), jax.ShapeDtypeStruct((t, dm), BF16),
                   jax.ShapeDtypeStruct((t, f), BF16), jax.ShapeDtypeStruct((t, f), BF16), jax.ShapeDtypeStruct((t, f), BF16),
                   jax.ShapeDtypeStruct((1, dm), F32), jax.ShapeDtypeStruct((1, dm), F32)],
        scratch_shapes=[pltpu.VMEM((tm, dm), F32)],
        compiler_params=pltpu.CompilerParams(dimension_semantics=("arbitrary", "arbitrary")),
    )


ATT_T = 512
ATT_GROUP = 4
ATT_GROUP_FWD = 8
ATT_SCALE = (MLA_NOPE + MLA_ROPE) ** -0.5


def _stack_slots(ref, group):
    return jnp.stack([ref[:, pl.ds(j * SLOT, SLOT)] for j in range(group)])


def _unstack_slots(ref, val):
    for j in range(val.shape[0]):
        ref[:, pl.ds(j * SLOT, SLOT)] = val[j].astype(ref.dtype)


def _scores(q, k, diagonal):
    s = _nt(q, k) * ATT_SCALE
    if diagonal:
        row = lax.broadcasted_iota(jnp.int32, s.shape[1:], 0)
        col = lax.broadcasted_iota(jnp.int32, s.shape[1:], 1)
        s = jnp.where(col <= row, s, -1e30)
    return s


def _attn_pairs(steps, q_major):
    pairs = ([(qi, ki) for qi in range(steps) for ki in range(qi + 1)] if q_major
             else [(qi, ki) for ki in range(steps) for qi in range(ki, steps)])
    return jnp.array([p[0] for p in pairs], jnp.int32), jnp.array([p[1] for p in pairs], jnp.int32)


def _attn_specs(tile, group):
    width = group * SLOT
    return (pl.BlockSpec((tile, width), lambda h, p, qt, kt: (qt[p], h)),
            pl.BlockSpec((tile, width), lambda h, p, qt, kt: (kt[p], h)))


def _attn_fwd(q, k, v):
    t = q.shape[0]
    tile = min(ATT_T, t)
    steps = t // tile
    g = ATT_GROUP_FWD

    strip = min(SLOT, tile)

    def body(qt_ref, kt_ref, q_ref, k_ref, v_ref, o_ref, lse_ref, m_ref, l_ref, alpha_ref, acc_ref, s_ref, p_ref):
        qi, ki = qt_ref[pl.program_id(1)], kt_ref[pl.program_id(1)]

        @pl.when(ki == 0)
        def _():
            m_ref[...] = jnp.full_like(m_ref, -1e30)
            l_ref[...] = jnp.zeros_like(l_ref)
            acc_ref[...] = jnp.zeros_like(acc_ref)

        def step(diagonal):
            s_ref[...] = _nt(_stack_slots(k_ref, g), _stack_slots(q_ref, g))
            for j in range(tile // strip):
                c = pl.ds(j * strip, strip)
                s = s_ref[:, :, c] * ATT_SCALE
                if diagonal:
                    key = lax.broadcasted_iota(jnp.int32, s.shape[1:], 0)
                    query = lax.broadcasted_iota(jnp.int32, s.shape[1:], 1) + j * strip
                    s = jnp.where(key <= query, s, -1e30)
                m_old = m_ref[:, :, c]
                m_new = jnp.maximum(m_old, jnp.max(s, axis=1, keepdims=True))
                p = jnp.exp(s - m_new)
                alpha = jnp.exp(m_old - m_new)
                l_ref[:, :, c] = alpha * l_ref[:, :, c] + jnp.sum(p, axis=1, keepdims=True)
                alpha_ref[:, :, c] = alpha
                m_ref[:, :, c] = m_new
                p_ref[:, :, c] = p.astype(BF16)
            acc_ref[...] = acc_ref[...] * alpha_ref[...] + _tn(_stack_slots(v_ref, g), p_ref[...])

        @pl.when(ki < qi)
        def _():
            step(False)

        @pl.when(ki == qi)
        def _():
            step(True)
            out = acc_ref[...] / l_ref[...]
            lse = jnp.broadcast_to(m_ref[...] + jnp.log(l_ref[...]), out.shape)
            for j in range(g):
                o_ref[:, pl.ds(j * SLOT, SLOT)] = out[j].T
                lse_ref[:, pl.ds(j * SLOT, SLOT)] = lse[j].T

    q_spec, k_spec = _attn_specs(tile, g)
    tables = _attn_pairs(steps, True)
    return pl.pallas_call(
        body, name="attn_fwd",
        grid_spec=pltpu.PrefetchScalarGridSpec(
            num_scalar_prefetch=2, grid=(N_HEADS // g, tables[0].shape[0]),
            in_specs=[q_spec, k_spec, k_spec], out_specs=[q_spec, q_spec],
            scratch_shapes=[pltpu.VMEM((g, 1, tile), F32), pltpu.VMEM((g, 1, tile), F32), pltpu.VMEM((g, 1, tile), F32),
                            pltpu.VMEM((g, SLOT, tile), F32), pltpu.VMEM((g, tile, tile), F32), pltpu.VMEM((g, tile, tile), BF16)]),
        out_shape=[jax.ShapeDtypeStruct((t, N_HEADS * SLOT), F32)] * 2,
        compiler_params=pltpu.CompilerParams(dimension_semantics=("parallel", "arbitrary")),
    )(*tables, q, k, v)


def _attn_grad_scores(q, k, v, do, lse_ref, delta_ref, diagonal):
    g = ATT_GROUP
    p = jnp.exp(_scores(q, k, diagonal) - _stack_slots(lse_ref, g)[:, :, 0:1])
    dp = _nt(do, v)
    return p, p * (dp - _stack_slots(delta_ref, g)[:, :, 0:1]) * ATT_SCALE


def _attn_bwd(q, k, v, do, lse, delta):
    t = q.shape[0]
    tile = min(ATT_T, t)
    steps = t // tile
    g = ATT_GROUP

    def body(qt_ref, kt_ref, q_ref, k_ref, v_ref, do_ref, lse_ref, delta_ref, dq_ref, dk_ref, dv_ref, dk_acc, dv_acc):
        qi, ki = qt_ref[pl.program_id(1)], kt_ref[pl.program_id(1)]

        @pl.when(pl.program_id(1) == 0)
        def _():
            dq_ref[...] = jnp.zeros_like(dq_ref)

        def step(diagonal):
            qq, kk = _stack_slots(q_ref, g), _stack_slots(k_ref, g)
            do_b = _stack_slots(do_ref, g).astype(BF16)
            p, ds = _attn_grad_scores(qq, kk, _stack_slots(v_ref, g), do_b, lse_ref, delta_ref, diagonal)
            ds = ds.astype(BF16)
            dv_acc[...] += _tn(p.astype(BF16), do_b)
            dk_acc[...] += _tn(ds, qq)
            dq = _nn(ds, kk)
            rows = pl.ds(pl.multiple_of(qi * tile, tile), tile)
            for j in range(g):
                dq_ref[rows, pl.ds(j * SLOT, SLOT)] += dq[j]

        @pl.when(qi == ki)
        def _():
            dk_acc[...] = jnp.zeros_like(dk_acc)
            dv_acc[...] = jnp.zeros_like(dv_acc)
            step(True)

        @pl.when(qi > ki)
        def _():
            step(False)

        @pl.when(qi == steps - 1)
        def _():
            _unstack_slots(dk_ref, dk_acc[...])
            _unstack_slots(dv_ref, dv_acc[...])

    q_spec, k_spec = _attn_specs(tile, g)
    tables = _attn_pairs(steps, False)
    return pl.pallas_call(
        body, name="attn_bwd",
        grid_spec=pltpu.PrefetchScalarGridSpec(
            num_scalar_prefetch=2, grid=(N_HEADS // g, tables[0].shape[0]),
            in_specs=[q_spec, k_spec, k_spec, q_spec, q_spec, q_spec],
            out_specs=[pl.BlockSpec((t, g * SLOT), lambda h, p, qt, kt: (0, h)), k_spec, k_spec],
            scratch_shapes=[pltpu.VMEM((g, tile, SLOT), F32), pltpu.VMEM((g, tile, SLOT), F32)]),
        out_shape=[jax.ShapeDtypeStruct((t, N_HEADS * SLOT), F32)] * 3,
        compiler_params=pltpu.CompilerParams(dimension_semantics=("parallel", "arbitrary")),
    )(*tables, q, k, v, do, lse, delta)


CONV_PAD = 8


def _fill_padded(ref, val):
    t = val.shape[0]
    zeros = jnp.zeros((CONV_PAD, val.shape[1]), val.dtype)
    ref[pl.ds(0, CONV_PAD)] = zeros
    ref[pl.ds(CONV_PAD + t, CONV_PAD)] = zeros
    ref[pl.ds(CONV_PAD, t)] = val


def _shifted(ref, s):
    return ref[pl.ds(CONV_PAD - s, ref.shape[0] - 2 * CONV_PAD)]


def _l2norm(x):
    return x * lax.rsqrt(jnp.sum(x * x, axis=-1, keepdims=True) + EPS)


def _conv_pre(x_pad, w):
    y = w[GDN_CONV - 1:GDN_CONV, :] * _shifted(x_pad, 0)
    for s in range(1, GDN_CONV):
        y = y + w[GDN_CONV - 1 - s:GDN_CONV - s, :] * _shifted(x_pad, s)
    return y


def _gdn_conv_fwd(x, w):
    t, width = x.shape

    def body(x_ref, w_ref, o_ref, x_pad):
        _fill_padded(x_pad, x_ref[...])
        act = _silu(_conv_pre(x_pad, w_ref[...]))
        normed = pl.program_id(0) < 2 * N_HEADS
        o_ref[...] = jnp.where(normed, _l2norm(act), act)

    return pl.pallas_call(
        body, name="gdn_conv_fwd",
        grid=(width // SLOT,),
        in_specs=[pl.BlockSpec((t, SLOT), lambda j: (0, j)), pl.BlockSpec((GDN_CONV, SLOT), lambda j: (0, j))],
        out_specs=pl.BlockSpec((t, SLOT), lambda j: (0, j)),
        out_shape=jax.ShapeDtypeStruct((t, width), F32),
        scratch_shapes=[pltpu.VMEM((t + 2 * CONV_PAD, SLOT), F32)],
        compiler_params=pltpu.CompilerParams(dimension_semantics=("parallel",)),
    )(x, w)


def _gdn_conv_bwd(x, w, dout):
    t, width = x.shape

    def body(x_ref, w_ref, do_ref, dx_ref, dw_ref, x_pad, dy_pad):
        wv = w_ref[...]
        _fill_padded(x_pad, x_ref[...])
        y = _conv_pre(x_pad, wv)
        sig = _sigmoid(y)
        act = y * sig
        _, pull = jax.vjp(_l2norm, act)
        normed = pl.program_id(0) < 2 * N_HEADS
        dact = jnp.where(normed, pull(do_ref[0])[0], do_ref[0])
        dy = dact * (sig * (1.0 + y * (1.0 - sig)))
        _fill_padded(dy_pad, dy)
        dx = wv[GDN_CONV - 1:GDN_CONV, :] * dy
        for s in range(1, GDN_CONV):
            dx = dx + wv[GDN_CONV - 1 - s:GDN_CONV - s, :] * _shifted(dy_pad, -s)
        dx_ref[...] = dx.astype(BF16)
        for s in range(GDN_CONV):
            dw_ref[GDN_CONV - 1 - s:GDN_CONV - s, :] = jnp.sum(dy * _shifted(x_pad, s), axis=0, keepdims=True)

    col = pl.BlockSpec((t, SLOT), lambda j: (0, j))
    tap = pl.BlockSpec((GDN_CONV, SLOT), lambda j: (0, j))
    return pl.pallas_call(
        body, name="gdn_conv_bwd",
        grid=(width // SLOT,),
        in_specs=[col, tap, pl.BlockSpec((1, t, SLOT), lambda j: (j // N_HEADS, 0, j % N_HEADS))],
        out_specs=[col, tap],
        out_shape=[jax.ShapeDtypeStruct((t, width), BF16), jax.ShapeDtypeStruct((GDN_CONV, width), F32)],
        scratch_shapes=[pltpu.VMEM((t + 2 * CONV_PAD, SLOT), F32)] * 2,
        compiler_params=pltpu.CompilerParams(dimension_semantics=("parallel",)),
    )(x, w, dout)


def _softplus(x):
    e = jnp.exp(-jnp.abs(x))
    u = 1.0 + e
    log1p = jnp.where(u == 1.0, e, jnp.log(u) * e / jnp.where(u == 1.0, 1.0, u - 1.0))
    return jnp.maximum(x, 0.0) + log1p


def _chunk_running_sum(x, reverse=False):
    tm = x.shape[0]
    at = lax.broadcasted_iota(jnp.int32, x.shape, 0) % GDN_CHUNK
    step = 1
    while step < GDN_CHUNK:
        if reverse:
            x = x + jnp.where(at < GDN_CHUNK - step, pltpu.roll(x, tm - step, 0), 0.0)
        else:
            x = x + jnp.where(at >= step, pltpu.roll(x, step, 0), 0.0)
        step *= 2
    return x


def _gates_fwd(ab, a_log, dt_bias):
    def fn(rows, consts):
        (abv,), (alog, dtb) = rows, consts
        g = _chunk_running_sum(-jnp.exp(alog) * _softplus(abv + dtb))
        beta = _sigmoid(abv)
        shape = (abv.shape[0], SLOT)
        g_slots = [jnp.broadcast_to(g[:, h:h + 1], shape) for h in range(N_HEADS)]
        b_slots = [jnp.broadcast_to(beta[:, N_HEADS + h:N_HEADS + h + 1], shape) for h in range(N_HEADS)]
        return [jnp.concatenate(g_slots, axis=1), jnp.concatenate(b_slots, axis=1)], []

    width = N_HEADS * SLOT
    return _rowwise("gdn_gates_fwd", fn, [ab], [a_log, dt_bias], [(width, F32), (width, F32)])


def _gates_bwd(ab, a_log, dt_bias, dg, dbeta):
    def fn(rows, consts):
        (abv, dgv, dbv), (alog, dtb) = rows, consts
        lane = lax.broadcasted_iota(jnp.int32, abv.shape, 1)
        dg_tok = jnp.zeros_like(abv)
        db_tok = jnp.zeros_like(abv)
        for h in range(N_HEADS):
            dg_tok = dg_tok + jnp.where(lane == h, jnp.sum(dgv[:, h * SLOT:(h + 1) * SLOT], axis=1, keepdims=True), 0.0)
            db_tok = db_tok + jnp.where(lane == N_HEADS + h, jnp.sum(dbv[:, h * SLOT:(h + 1) * SLOT], axis=1, keepdims=True), 0.0)
        dg_tok = _chunk_running_sum(dg_tok, reverse=True)
        xa = abv + dtb
        g = -jnp.exp(alog) * _softplus(xa)
        da = dg_tok * (-jnp.exp(alog)) * _sigmoid(xa)
        beta = _sigmoid(abv)
        dab = jnp.where(lane < N_HEADS, da, db_tok * beta * (1.0 - beta))
        dab = jnp.where(lane < 2 * N_HEADS, dab, 0.0)
        d_alog = jnp.sum(jnp.where(lane < N_HEADS, dg_tok * g, 0.0), axis=0, keepdims=True)
        d_dtb = jnp.sum(jnp.where(lane < N_HEADS, da, 0.0), axis=0, keepdims=True)
        return [dab], [d_alog, d_dtb]

    return _rowwise("gdn_gates_bwd", fn, [ab, dg, dbeta], [a_log, dt_bias], [(SLOT, F32)], sums=[SLOT, SLOT])


ROPE_HALF = MLA_ROPE // 2


def _rope_tables(positions):
    freqs = ROPE_THETA ** (-jnp.arange(ROPE_HALF, dtype=F32) / ROPE_HALF)
    ang = positions.astype(F32)[:, None] * freqs
    cos, sin = jnp.cos(ang), jnp.sin(ang)
    t = positions.shape[0]
    ones, zeros = jnp.ones((t, MLA_NOPE), F32), jnp.zeros((t, MLA_NOPE), F32)
    tail = jnp.zeros((t, SLOT - MLA_NOPE - MLA_ROPE), F32)
    half0 = jnp.zeros((t, ROPE_HALF), F32)
    same = jnp.concatenate([ones, cos, cos, tail], axis=1)
    from_low = jnp.concatenate([zeros, half0, sin, tail], axis=1)
    from_high = jnp.concatenate([zeros, -sin, half0, tail], axis=1)
    return same, from_low, from_high


def _rope(x, tabs):
    same, from_low, from_high = tabs
    width = x.shape[1]
    return x * same + pltpu.roll(x, ROPE_HALF, 1) * from_low + pltpu.roll(x, width - ROPE_HALF, 1) * from_high


def _rope_transposed(dy, tabs):
    same, from_low, from_high = tabs
    width = dy.shape[1]
    return dy * same + pltpu.roll(dy * from_low, width - ROPE_HALF, 1) + pltpu.roll(dy * from_high, ROPE_HALF, 1)


def _tile_slots(tab):
    return jnp.concatenate([tab] * N_HEADS, axis=1)


A_WIDTH = MLA_Q_RANK + MLA_KV_RANK + 2 * SLOT
A_KPE = MLA_Q_RANK + MLA_KV_RANK
A_AB = A_KPE + SLOT
WIDE = N_HEADS * SLOT


def _mla_front_fwd(proj_a, tabs, g_q, g_kv, w_uq, w_kv):
    def fn(rows, consts):
        pa, *tb = rows
        gq, gkv, wuq, wkv = consts
        cqn = _rms(pa[:, :MLA_Q_RANK], gq, MLA_Q_RANK).astype(BF16)
        ckvn = _rms(pa[:, MLA_Q_RANK:A_KPE], gkv, MLA_KV_RANK).astype(BF16)
        kv = _nt(ckvn, wkv)
        q = _rope(_nt(cqn, wuq), [_tile_slots(x) for x in tb])
        k = kv[:, :WIDE] + _tile_slots(_rope(pa[:, A_KPE:A_AB], tb))
        return [cqn, ckvn, q, k, kv[:, WIDE:]], []

    return _rowwise("mla_front_fwd", fn, [proj_a, *tabs], [g_q, g_kv, w_uq, w_kv],
                    [(MLA_Q_RANK, BF16), (MLA_KV_RANK, BF16)] + [(WIDE, BF16)] * 3)


def _mla_front_bwd(proj_a, tabs, g_q, g_kv, w_uq, w_kv, dq, dk, dv, dab):
    def fn(rows, consts):
        pa, t0, t1, t2, dqv, dkv, dvv, da = rows
        gq, gkv, wuq, wkv = consts
        tb = (t0, t1, t2)
        dq_p = _rope_transposed(dqv, [_tile_slots(x) for x in tb]).astype(BF16)
        dkv_p = jnp.concatenate([dkv, dvv], axis=1).astype(BF16)
        dkpe = dkv[:, :SLOT]
        for h in range(1, N_HEADS):
            dkpe = dkpe + dkv[:, h * SLOT:(h + 1) * SLOT]
        _, pull_q = jax.vjp(lambda x, g: _rms(x, g, MLA_Q_RANK), pa[:, :MLA_Q_RANK], gq)
        _, pull_kv = jax.vjp(lambda x, g: _rms(x, g, MLA_KV_RANK), pa[:, MLA_Q_RANK:A_KPE], gkv)
        dcq, dgq = pull_q(_nn(dq_p, wuq))
        dckv, dgkv = pull_kv(_nn(dkv_p, wkv))
        return [jnp.concatenate([dcq, dckv, _rope_transposed(dkpe, tb), da], axis=1), dq_p, dkv_p], [dgq, dgkv]

    return _rowwise("mla_front_bwd", fn, [proj_a, *tabs, dq, dk, dv, dab], [g_q, g_kv, w_uq, w_kv],
                    [(A_WIDTH, BF16), (WIDE, BF16), (2 * WIDE, BF16)], sums=[MLA_Q_RANK, MLA_KV_RANK])


def _slot_sum(x):
    parts = [jnp.broadcast_to(jnp.sum(x[:, h * SLOT:(h + 1) * SLOT], axis=1, keepdims=True), (x.shape[0], SLOT))
             for h in range(N_HEADS)]
    return jnp.concatenate(parts, axis=1)


def _mix_join(o_mla, o_gdn, gate, g_mla, g_gdn):
    mla = _rms(o_mla, g_mla, N_HEADS * MLA_V)
    gdn = o_gdn * lax.rsqrt(_slot_sum(o_gdn * o_gdn) * (1.0 / GDN_D) + EPS) * g_gdn * _silu(gate)
    return mla, gdn


MIX_TM = 256


def _mix_fwd(o_mla, o_gdn, gate, x, g_mla, g_gdn, w_out, g_post):
    dm = x.shape[1]

    def fn(rows, consts):
        om, og, gt, xv = rows
        gm, gg, wo, gp = consts
        cat = jnp.concatenate(_mix_join(om, og, gt, gm, gg), axis=1).astype(BF16)
        mixed = _nn(cat, wo)
        return [cat, mixed, xv + _rms(mixed, gp, dm)], []

    return _rowwise("mix_fwd", fn, [o_mla, o_gdn, gate, x], [g_mla, g_gdn, w_out, g_post],
                    [(2 * WIDE, BF16), (dm, F32), (dm, F32)], tm=MIX_TM)


def _mix_bwd(o_mla, o_gdn, gate, mixed, dy, g_mla, g_gdn, w_out, g_post):
    dm = mixed.shape[1]

    def fn(rows, consts):
        om, og, gt, mx, dyv = rows
        gm, gg, wo, gp = consts
        _, pull_post = jax.vjp(lambda hv, gv: _rms(hv, gv, dm), mx, gp)
        dmixed, dgp = pull_post(dyv)
        dmixed = dmixed.astype(BF16)
        dc = _nt(dmixed, wo)
        _, pull = jax.vjp(lambda x, g: _rms(x, g, N_HEADS * MLA_V), om, gm)
        dom, dgm = pull(dc[:, :WIDE])
        dn_out = dc[:, WIDE:]
        r = lax.rsqrt(_slot_sum(og * og) * (1.0 / GDN_D) + EPS)
        sig = _sigmoid(gt)
        normed = og * r
        dn = dn_out * gg * (gt * sig)
        dog = r * dn - normed * (r * r) * _slot_sum(dn * og) * (1.0 / GDN_D)
        dgt = dn_out * normed * gg * (sig * (1.0 + gt * (1.0 - sig)))
        dgg = jnp.sum(dn_out * normed * (gt * sig), axis=0, keepdims=True)
        return [dmixed, dom, _slot_sum(dom * om), dog, dgt], [dgp, dgm, dgg]

    return _rowwise("mix_bwd", fn, [o_mla, o_gdn, gate, mixed, dy], [g_mla, g_gdn, w_out, g_post],
                    [(dm, BF16), (WIDE, F32), (WIDE, F32), (WIDE, F32), (WIDE, BF16)], sums=[dm, WIDE, WIDE], tm=MIX_TM)


def _proj_fwd(x, g, weights):
    dm = x.shape[1]

    def fn(rows, consts):
        hn = _rms(rows[0], consts[0], dm).astype(BF16)
        return [hn] + [_nt(hn, wv) for wv in consts[1:]], []

    return _rowwise("proj_fwd", fn, [x], [g, *weights], [(dm, BF16)] + [(wv.shape[0], F32) for wv in weights], tm=MIX_TM)


def _proj_bwd(x, g, weights, cots, dy):
    dm = x.shape[1]
    n = len(weights)

    def fn(rows, consts):
        xv, dyv, *parts = rows
        dn = _nn(parts[0], consts[1])
        for p, wv in zip(parts[1:], consts[2:]):
            dn = dn + _nn(p, wv)
        _, pull = jax.vjp(lambda a, gv: _rms(a, gv, dm), xv, consts[0])
        dx, dg = pull(dn)
        return [dyv + dx], [dg]

    assert len(cots) == n
    return _rowwise("proj_bwd", fn, [x, dy, *cots], [g, *weights], [(dm, F32)], sums=[dm], tm=MIX_TM)


def _loss_fwd(y, target):
    dm = y.shape[1]

    def fn(rows, consts):
        err = rows[0] - rows[1]
        sq = err * err
        lanes = sq[:, :SLOT]
        for j in range(1, dm // SLOT):
            lanes = lanes + sq[:, j * SLOT:(j + 1) * SLOT]
        return [err * (1.0 / dm)], [jnp.sum(lanes, axis=0, keepdims=True) * (0.5 / dm)]

    return _rowwise("loss", fn, [y, target], [], [(dm, F32)], sums=[SLOT])


W_IN_CUTS = (0, 256, 384, 416, 1952, 1960, 1968, 2480)


def _heads_out(w, per_head, axis=-1):
    axis = axis % w.ndim
    shape = w.shape
    n = shape[axis] // per_head
    w = w.reshape(shape[:axis] + (n, per_head) + shape[axis + 1:])
    pad = [(0, 0)] * w.ndim
    pad[axis + 1] = (0, SLOT - per_head)
    return jnp.pad(w, pad).reshape(shape[:axis] + (n * SLOT,) + shape[axis + 1:])


def _heads_in(w, per_head, axis=-1):
    axis = axis % w.ndim
    shape = w.shape
    n = shape[axis] // SLOT
    w = w.reshape(shape[:axis] + (n, SLOT) + shape[axis + 1:])
    w = lax.slice_in_dim(w, 0, per_head, axis=axis + 1)
    return w.reshape(shape[:axis] + (n * per_head,) + shape[axis + 1:])


def _pad_lanes(v, lo, width=SLOT):
    return jnp.pad(v, [(0, 0)] * (v.ndim - 1) + [(lo, width - lo - v.shape[-1])])


def _pad_rows(v, lo, rows=SLOT):
    return jnp.pad(v, [(lo, rows - lo - v.shape[0])] + [(0, 0)] * (v.ndim - 1))


def _layout_weights(w):
    c = W_IN_CUTS
    w_in = w["w_in_t"]
    p = {}
    p["w_a"] = jnp.concatenate([w_in[c[0]:c[2]], _pad_rows(w_in[c[2]:c[3]], MLA_NOPE), _pad_rows(w_in[c[4]:c[6]], 0)], axis=0)
    p["w_qkv"] = _heads_out(w_in[c[3]:c[4]], GDN_D, axis=0)
    p["w_gate"] = _heads_out(w_in[c[6]:c[7]], GDN_D, axis=0)
    p["w_uq"] = _heads_out(w["uq_t"], MLA_NOPE + MLA_ROPE, axis=0)
    ukv = w["ukv_t"].reshape(N_HEADS, MLA_NOPE + MLA_V, MLA_KV_RANK)
    p["w_kv"] = jnp.concatenate([_heads_out(ukv[:, :MLA_NOPE].reshape(-1, MLA_KV_RANK), MLA_NOPE, axis=0),
                                 _heads_out(ukv[:, MLA_NOPE:].reshape(-1, MLA_KV_RANK), MLA_V, axis=0)], axis=0)
    p["conv"] = _heads_out(w["gdn_conv_w"], GDN_D)
    p["g_mla_out"] = _heads_out(w["mla_out_g"], MLA_V)
    p["g_gdn"] = jnp.tile(_pad_lanes(w["gdn_norm_g"], 0), (1, N_HEADS))
    p["a_log"] = _pad_lanes(w["gdn_a_log"], 0)
    p["dt_bias"] = _pad_lanes(w["gdn_dt_bias"], 0)
    return p


def _unlayout_grads(d):
    c = W_IN_CUTS
    g = {}
    da = d["w_a"]
    kpe0 = A_KPE + MLA_NOPE
    g["w_in_t"] = jnp.concatenate([da[:A_KPE], da[kpe0:kpe0 + MLA_ROPE], _heads_in(d["w_qkv"], GDN_D, axis=0),
                                   da[A_AB:A_AB + 2 * N_HEADS], _heads_in(d["w_gate"], GDN_D, axis=0)], axis=0)
    assert g["w_in_t"].shape[0] == c[-1]
    g["uq_t"] = _heads_in(d["w_uq"], MLA_NOPE + MLA_ROPE, axis=0)
    dk = _heads_in(d["w_kv"][:WIDE], MLA_NOPE, axis=0).reshape(N_HEADS, MLA_NOPE, MLA_KV_RANK)
    dv = _heads_in(d["w_kv"][WIDE:], MLA_V, axis=0).reshape(N_HEADS, MLA_V, MLA_KV_RANK)
    g["ukv_t"] = jnp.concatenate([dk, dv], axis=1).reshape(-1, MLA_KV_RANK)
    g["w_out"] = _heads_in(d["w_out"], GDN_D, axis=0)
    g["gdn_conv_w"] = _heads_in(d["conv"], GDN_D)
    g["mla_out_g"] = _heads_in(d["g_mla_out"], MLA_V)
    g["gdn_norm_g"] = jnp.sum(d["g_gdn"].reshape(N_HEADS, SLOT), axis=0, keepdims=True)[:, :GDN_D]
    g["gdn_a_log"] = d["a_log"][:, :N_HEADS]
    g["gdn_dt_bias"] = d["dt_bias"][:, :N_HEADS]
    return g


def _weight_grad(name, cots, acts, out_dtype=F32, tm=1024, tn=1024, tk=2048, after=None):
    return _matmul(name, cots, acts, "tn", out_dtype=out_dtype, tm=tm, tn=tn, tk=tk, after=after)


def _by_device(a):
    return a.astype(BF16).reshape((N_DEV, a.shape[0] // N_DEV) + a.shape[1:])


def _rows_of(blocks):
    return blocks.reshape((-1,) + blocks.shape[2:])


def _local_step(x, positions, target, w, mid, late):
    tabs = _rope_tables(positions)

    (h1, x1, hg1, hu1), gathered = _ffn_fwd("ffn1_fwd", x, w["ffn1_pre_g"], w["ffn1"], 0, w["ffn1_post_g"], carry=mid)
    w = dict(w, w_in_t=_rows_of(gathered[0]), uq_t=_rows_of(gathered[1]), ukv_t=_rows_of(gathered[2]))
    p = _layout_weights(w)
    in_weights = [p["w_a"], p["w_qkv"], p["w_gate"]]
    hn, proj_a, proj_qkv, proj_gate = _proj_fwd(x1, w["mix_pre_g"], in_weights)
    cqn, ckvn, q, k, v = _mla_front_fwd(proj_a, tabs, w["mla_q_norm_g"], w["mla_kv_norm_g"], p["w_uq"], p["w_kv"])
    o_mla, lse = _attn_fwd(q, k, v)
    ab = (proj_a, SLOT, A_AB // SLOT)
    qkv_n = _gdn_conv_fwd(proj_qkv, p["conv"])
    gb, bb = _gates_fwd(ab, p["a_log"], p["dt_bias"])
    (o_gdn, keep), (ffn2, w_out) = _gdn_fwd(qkv_n, gb, bb, carry=late)
    p["w_out"] = _heads_out(_rows_of(w_out), GDN_D, axis=0)
    cat, mixed, x2 = _mix_fwd(o_mla, o_gdn, proj_gate, x1, p["g_mla_out"], p["g_gdn"], p["w_out"], w["mix_post_g"])
    (h2, y, hg2, hu2), _ = _ffn_fwd("ffn2_fwd", x2, w["ffn2_pre_g"], ffn2, 0, w["ffn2_post_g"])
    dy, loss_lanes = _loss_fwd(y, target)

    g = {}
    (dx2, xn2, dh2, a2, dhg2, dhu2, g["ffn2_pre_g"], g["ffn2_post_g"]), _ = _ffn_bwd(
        "ffn2_bwd", x2, h2, hg2, hu2, dy, w["ffn2_pre_g"], ffn2, 0, w["ffn2_post_g"])
    ffn2_grads = _Scatter([_by_device(_weight_grad("ffn2_dw_gate", dhg2, xn2, BF16, tm=1408)),
                           _by_device(_weight_grad("ffn2_dw_up", dhu2, xn2, BF16, tm=1408)),
                           _by_device(_weight_grad("ffn2_dw_down", a2, dh2, BF16, tm=1408))])
    d = {}
    dmixed, do_mla, delta, do_gdn, dgate, g["mix_post_g"], d["g_mla_out"], d["g_gdn"] = _mix_bwd(
        o_mla, o_gdn, proj_gate, mixed, dx2, p["g_mla_out"], p["g_gdn"], p["w_out"], w["mix_post_g"])
    d["w_out"] = _weight_grad("mix_out_dw", cat, dmixed)
    dq, dk, dv = _attn_bwd(q, k, v, do_mla, lse, delta)
    (dqkv_n, dgb, dbb), landed_ffn2 = _gdn_bwd(qkv_n, gb, bb, keep, do_gdn, carry=ffn2_grads)
    dab, d["a_log"], d["dt_bias"] = _gates_bwd(ab, p["a_log"], p["dt_bias"], dgb, dbb)
    dproj_qkv, d["conv"] = _gdn_conv_bwd(proj_qkv, p["conv"], dqkv_n)
    dproj_a, dq_p, dkv_p, g["mla_q_norm_g"], g["mla_kv_norm_g"] = _mla_front_bwd(
        proj_a, tabs, w["mla_q_norm_g"], w["mla_kv_norm_g"], p["w_uq"], p["w_kv"], dq, dk, dv, dab)
    d["w_uq"] = _weight_grad("mla_q_dw", dq_p, cqn)
    d["w_kv"] = _weight_grad("mla_kv_dw", dkv_p, ckvn)
    d["w_a"] = _weight_grad("proj_a_dw", dproj_a, hn, tm=640)
    d["w_qkv"] = _weight_grad("proj_qkv_dw", dproj_qkv, hn)
    d["w_gate"] = _weight_grad("proj_gate_dw", dgate, hn)
    dx1, g["mix_pre_g"] = _proj_bwd(x1, w["mix_pre_g"], in_weights, [dproj_a, dproj_qkv, dgate], dx2)
    g.update(_unlayout_grads(d))
    others = list(OTHER.values())
    (dx, xn1, dh1, a1, dhg1, dhu1, g["ffn1_pre_g"], g["ffn1_post_g"]), landed_others = _ffn_bwd(
        "ffn1_bwd", x, h1, hg1, hu1, dx1, w["ffn1_pre_g"], w["ffn1"], 0, w["ffn1_post_g"], carry=_Scatter([_by_device(g.pop(t)) for t in others]))
    landed = dict(zip(list(FFN_NAMES[3:]) + list(OTHER), list(landed_ffn2) + list(landed_others)))
    begun, token = {}, None
    for name, cots, acts in (("ffn1_w_down", a1, dh1), ("ffn1_w_gate", dhg1, xn1), ("ffn1_w_up", dhu1, xn1)):
        blocks = _by_device(_weight_grad(name + "_grad", cots, acts, BF16, tm=1408, after=token))
        begun[name], token = _scatter_begin("scatter_" + name + "_begin", blocks)
    return loss_lanes, dx, g, landed, begun, token


MESH_AXES = ("x", "y", "c")
N_LINKS = N_DEV - 1


def _place():
    return tuple(lax.axis_index(a) for a in MESH_AXES)


def _block_of(dev):
    x, y, c = dev
    return 4 * x + 2 * y + c


def _remote_copy(src, dst, sems, k, to):
    send_sems, recv_sems = sems
    return pltpu.make_async_remote_copy(src_ref=src, dst_ref=dst, send_sem=send_sems.at[k], recv_sem=recv_sems.at[k],
                                        device_id=to, device_id_type=pl.DeviceIdType.MESH)


class _Exchange:
    def __init__(self, arrays):
        self.arrays = list(arrays)
        self.n = len(self.arrays)
        self.specs = [pl.BlockSpec(memory_space=pl.ANY)] * self.n
        self.scratch = [pltpu.SemaphoreType.DMA((self.n * N_LINKS,)), pltpu.SemaphoreType.DMA((self.n * N_LINKS,)),
                        pltpu.SemaphoreType.DMA((self.n,))]

    def split(self, refs):
        n = self.n
        return refs[:n], refs[n:2 * n], (refs[2 * n], refs[2 * n + 1]), refs[2 * n + 2]


class _Gather(_Exchange):
    def out_shape(self):
        return [jax.ShapeDtypeStruct((N_DEV,) + a.shape, a.dtype) for a in self.arrays]

    def _plan(self, ins, outs, sems, local_sems):
        x, y, c = _place()
        me, sibling = (x, y, c), (x, y, 1 - c)
        chips = [(1 - x, y), (x, 1 - y), (1 - x, 1 - y)]

        def copy(a, k, block, to, mine=False):
            src = ins[a] if mine else outs[a].at[_block_of(block)]
            return _remote_copy(src, outs[a].at[_block_of(block)], sems, a * N_LINKS + k, to)

        local = [pltpu.make_async_copy(ins[a], outs[a].at[_block_of(me)], local_sems.at[a]) for a in range(self.n)]
        first = []
        for a in range(self.n):
            first.append(copy(a, 0, me, sibling, mine=True))
            first += [copy(a, 1 + j, me, (*chip, c), mine=True) for j, chip in enumerate(chips)]
        return me, sibling, chips, c, copy, local, first

    def start(self, ins, outs, sems, local_sems):
        *_, local, first = self._plan(ins, outs, sems, local_sems)
        for cp in local + first:
            cp.start()

    def finish(self, ins, outs, sems, local_sems):
        me, sibling, chips, c, copy, local, first = self._plan(ins, outs, sems, local_sems)
        passed = []
        for j, chip in enumerate(chips):
            for a in range(self.n):
                copy(a, 1 + j, (*chip, c), me).wait_recv()
                passed.append(copy(a, 4 + j, (*chip, c), sibling))
                passed[-1].start()
        for a in range(self.n):
            copy(a, 0, sibling, me).wait_recv()
            for j, chip in enumerate(chips):
                copy(a, 4 + j, (*chip, 1 - c), me).wait_recv()
        for cp in first + passed:
            cp.wait_send()
        for cp in local:
            cp.wait()


class _Scatter(_Exchange):
    def out_shape(self):
        return [jax.ShapeDtypeStruct(a.shape, a.dtype) for a in self.arrays]

    def _plan(self, ins, outs, sems, local_sems):
        x, y, c = _place()
        me = _block_of((x, y, c))

        def peer(r):
            return (1 - x if r & 4 else x, 1 - y if r & 2 else y, 1 - c if r & 1 else c)

        local = [pltpu.make_async_copy(ins[a].at[me], outs[a].at[me], local_sems.at[a]) for a in range(self.n)]
        sends = [_remote_copy(ins[a].at[_block_of(peer(r))], outs[a].at[me], sems, a * N_LINKS + r - 1, peer(r))
                 for a in range(self.n) for r in range(1, N_DEV)]
        arrivals = [_remote_copy(ins[a].at[me], outs[a].at[_block_of(peer(r))], sems, a * N_LINKS + r - 1, peer(r))
                    for a in range(self.n) for r in range(1, N_DEV)]
        return local, sends, arrivals

    def start(self, ins, outs, sems, local_sems):
        local, sends, _ = self._plan(ins, outs, sems, local_sems)
        for cp in local + sends:
            cp.start()

    def finish(self, ins, outs, sems, local_sems):
        local, sends, arrivals = self._plan(ins, outs, sems, local_sems)
        for cp in arrivals:
            cp.wait_recv()
        for cp in sends:
            cp.wait_send()
        for cp in local:
            cp.wait()


def _exchange(name, plan):
    def body(*refs):
        parts = plan.split(refs)
        plan.start(*parts)
        plan.finish(*parts)

    return pl.pallas_call(
        body, name=name,
        in_specs=plan.specs,
        out_specs=plan.specs,
        out_shape=plan.out_shape(),
        scratch_shapes=plan.scratch,
    )(*plan.arrays)


def _call_carrying(body, plan, operands, *, name, grid, in_specs, out_specs, out_shape, scratch_shapes, compiler_params):
    if plan is None:
        outs = pl.pallas_call(body, name=name, grid=grid, in_specs=in_specs, out_specs=out_specs, out_shape=out_shape,
                              scratch_shapes=scratch_shapes, compiler_params=compiler_params)(*operands)
        return outs, []
    n_i, n_o, n_s, k = len(in_specs), len(out_specs), len(scratch_shapes), plan.n

    def whole(*refs):
        cut = [n_i, n_i + k, n_i + k + n_o, n_i + 2 * k + n_o, n_i + 2 * k + n_o + n_s]
        own_in, ex_in, own_out, ex_out, own_scr, ex_scr = (refs[a:b] for a, b in zip([0] + cut, cut + [len(refs)]))
        parts = plan.split(ex_in + ex_out + ex_scr)
        first = last = True
        for axis, size in enumerate(grid):
            first = first & (pl.program_id(axis) == 0)
            last = last & (pl.program_id(axis) == size - 1)

        @pl.when(first)
        def _():
            plan.start(*parts)

        body(*own_in, *own_out, *own_scr)

        @pl.when(last)
        def _():
            plan.finish(*parts)

    outs = pl.pallas_call(
        whole, name=name, grid=grid,
        in_specs=list(in_specs) + plan.specs, out_specs=list(out_specs) + plan.specs,
        out_shape=list(out_shape) + plan.out_shape(), scratch_shapes=list(scratch_shapes) + plan.scratch,
        compiler_params=compiler_params,
    )(*operands, *plan.arrays)
    return outs[:n_o], outs[n_o:]


def _row_tile(rows, target=256):
    best = rows
    for cand in range(16, min(rows, target) + 1, 16):
        if rows % cand == 0:
            best = cand
    return best


def _sum_blocks(name, blocks, after=None):
    rows, width = blocks.shape[-2:]
    tm = _row_tile(rows)

    def body(x_ref, *rest):
        acc = x_ref[0].astype(F32)
        for d in range(1, N_DEV):
            acc = acc + x_ref[d].astype(F32)
        rest[-1][...] = acc

    ordered = [] if after is None else [after]
    return pl.pallas_call(
        body, name=name,
        grid=(rows // tm,),
        in_specs=[pl.BlockSpec((N_DEV, tm, width), lambda i: (0, i, 0))] + [pl.BlockSpec(memory_space=pl.ANY)] * len(ordered),
        out_specs=pl.BlockSpec((tm, width), lambda i: (i, 0)),
        out_shape=jax.ShapeDtypeStruct((rows, width), F32),
        compiler_params=pltpu.CompilerParams(dimension_semantics=("parallel",)),
    )(blocks, *ordered)


def _split_plan(src_ref, land_ref, sems):
    x, y, c = _place()
    me = _block_of((x, y, c))

    def peer(r):
        return (1 - x if r & 4 else x, 1 - y if r & 2 else y, 1 - c if r & 1 else c)

    sends = [_remote_copy(src_ref.at[_block_of(peer(r))], land_ref.at[me], sems, r - 1, peer(r)) for r in range(1, N_DEV)]
    arrivals = [_remote_copy(src_ref.at[me], land_ref.at[_block_of(peer(r))], sems, r - 1, peer(r)) for r in range(1, N_DEV)]
    return sends, arrivals


def _scatter_begin(name, blocks):
    def body(src_ref, land_ref, send_sems, recv_sems, src_thru, land_thru, token_ref):
        for cp in _split_plan(src_ref, land_ref, (send_sems, recv_sems))[0]:
            cp.start()
        token_ref[...] = jnp.zeros_like(token_ref)

    hbm, sem = pl.BlockSpec(memory_space=pltpu.HBM), pl.BlockSpec(memory_space=pltpu.SEMAPHORE)
    zone = pltpu.HBM(blocks.shape, blocks.dtype)
    *handles, token = pl.pallas_call(
        body, name=name,
        in_specs=(hbm, hbm),
        out_specs=(sem, sem, hbm, hbm, pl.BlockSpec(memory_space=pltpu.VMEM)),
        out_shape=(pltpu.SemaphoreType.DMA((N_LINKS,)), pltpu.SemaphoreType.DMA((N_LINKS,)), zone, zone,
                   jax.ShapeDtypeStruct((8, SLOT), F32)),
        input_output_aliases={0: 2, 1: 3},
        compiler_params=pltpu.CompilerParams(has_side_effects=pltpu.SideEffectType.DATAFLOW_SIDE_EFFECTING),
    )(pltpu.with_memory_space_constraint(blocks, pltpu.HBM),
      pltpu.with_memory_space_constraint(lax.empty(blocks.shape, blocks.dtype), pltpu.HBM))
    return handles, token


def _scatter_end(name, handles, after):
    send_sems, recv_sems, src, zone = handles

    def body(src_ref, land_ref, send_sems, recv_sems, after_ref, src_dead, got_ref):
        sends, arrivals = _split_plan(src_ref, land_ref, (send_sems, recv_sems))
        for cp in arrivals:
            cp.wait_recv()
        for cp in sends:
            cp.wait_send()

    hbm, sem = pl.BlockSpec(memory_space=pltpu.HBM), pl.BlockSpec(memory_space=pltpu.SEMAPHORE)
    sent, landed = pl.pallas_call(
        body, name=name,
        in_specs=(hbm, hbm, sem, sem, pl.BlockSpec(memory_space=pl.ANY)),
        out_specs=(hbm, hbm),
        out_shape=(pltpu.HBM(src.shape, src.dtype), pltpu.HBM(zone.shape, zone.dtype)),
        input_output_aliases={0: 0, 1: 1},
        compiler_params=pltpu.CompilerParams(has_side_effects=pltpu.SideEffectType.DATAFLOW_SIDE_EFFECTING),
    )(src, zone, send_sems, recv_sems, after)
    me = _block_of(_place())
    return lax.dynamic_update_slice_in_dim(landed, lax.dynamic_slice_in_dim(sent, me, 1, axis=0), me, axis=0)


def _all_reduce_small(name, vec):
    rows, width = vec.shape

    def body(x_ref, o_ref, all_ref, send_sems, recv_sems):
        x, y, c = _place()
        me = _block_of((x, y, c))
        all_ref[me] = x_ref[...]

        def peer(r):
            return (1 - x if r & 4 else x, 1 - y if r & 2 else y, 1 - c if r & 1 else c)

        def copy(r, block):
            return _remote_copy(x_ref, all_ref.at[block], (send_sems, recv_sems), r - 1, peer(r))

        sends = [copy(r, me) for r in range(1, N_DEV)]
        for cp in sends:
            cp.start()
        for r in range(1, N_DEV):
            copy(r, _block_of(peer(r))).wait_recv()
        for cp in sends:
            cp.wait_send()
        acc = all_ref[0]
        for d in range(1, N_DEV):
            acc = acc + all_ref[d]
        o_ref[...] = acc

    return pl.pallas_call(
        body, name=name,
        in_specs=[pl.BlockSpec(memory_space=pltpu.VMEM)],
        out_specs=pl.BlockSpec(memory_space=pltpu.VMEM),
        out_shape=jax.ShapeDtypeStruct((rows, width), F32),
        scratch_shapes=[pltpu.VMEM((N_DEV, rows, width), F32), pltpu.SemaphoreType.DMA((N_LINKS,)), pltpu.SemaphoreType.DMA((N_LINKS,))],
    )(vec)


def _adamw(name, w, g, m, v):
    def fn(rows, consts):
        wv, gv, mv, vv = rows
        m2 = ADAM_B1 * mv + (1.0 - ADAM_B1) * gv
        v2 = ADAM_B2 * vv + (1.0 - ADAM_B2) * jnp.square(gv)
        m_hat = m2 / (1.0 - ADAM_B1 ** ADAM_STEP)
        v_hat = v2 / (1.0 - ADAM_B2 ** ADAM_STEP)
        return [-ADAM_LR * (m_hat / (jnp.sqrt(v_hat) + ADAM_EPS) + ADAM_WD * wv), m2, v2], []

    return _rowwise(name, fn, [w, g, m, v], [], [(w.shape[1], F32)] * 3, tm=_row_tile(w.shape[0]))


ROW = 1024
FFN_NAMES = ("ffn1_w_gate", "ffn1_w_up", "ffn1_w_down", "ffn2_w_gate", "ffn2_w_up", "ffn2_w_down")
OTHER = {"w_in": "w_in_t", "mla_w_uq": "uq_t", "mla_w_ukv": "ukv_t", "w_out": "w_out"}
BY_COLUMNS = ("ffn1_w_gate", "ffn1_w_up", "ffn2_w_gate", "ffn2_w_up", "w_in", "mla_w_uq", "mla_w_ukv")
SMALL = {
    "ffn1_pre_g": (1024, 1024), "ffn1_post_g": (1024, 1024), "mix_pre_g": (1024, 1024), "mla_q_norm_g": (256, 256),
    "mla_kv_norm_g": (128, 128), "mla_out_g": (512, 512), "gdn_a_log": (8, 128), "gdn_dt_bias": (8, 128),
    "gdn_norm_g": (64, 128), "mix_post_g": (1024, 1024), "ffn2_pre_g": (1024, 1024), "ffn2_post_g": (1024, 1024),
}
CONV_SHAPE = (GDN_CONV, 3 * N_HEADS * GDN_D)
CONV_SHARD = (GDN_CONV, CONV_SHAPE[1] // N_DEV)
CONV_LANES = CONV_SHAPE[0] * CONV_SHAPE[1]
SMALL_ROWS = 8
REDUCE_ROWS = 16


def _pack_small(vecs, conv, rows):
    parts = [_pad_lanes(vecs[n].reshape(1, -1), 0, r) for n, (_, r) in SMALL.items()]
    parts.append(conv.reshape(1, -1))
    flat = jnp.concatenate(parts, axis=1)
    return _pad_lanes(flat, 0, rows * ROW).reshape(rows, ROW)


def _unpack_small(buf):
    flat = buf.reshape(1, -1)
    out, at = {}, 0
    for n, (w, r) in SMALL.items():
        out[n] = flat[:, at:at + w]
        at += r
    return out, flat[0, at:]


def kernel(x, positions, ffn1_pre_g, ffn1_w_gate, ffn1_w_up, ffn1_w_down, ffn1_post_g, mix_pre_g, w_in, mla_q_norm_g, mla_w_uq, mla_kv_norm_g, mla_w_ukv, mla_out_g, gdn_conv_w, gdn_a_log, gdn_dt_bias, gdn_norm_g, w_out, mix_post_g, ffn2_pre_g, ffn2_w_gate, ffn2_w_up, ffn2_w_down, ffn2_post_g, loss_target, m_ffn1_pre_g, m_ffn1_w_gate, m_ffn1_w_up, m_ffn1_w_down, m_ffn1_post_g, m_mix_pre_g, m_w_in, m_mla_q_norm_g, m_mla_w_uq, m_mla_kv_norm_g, m_mla_w_ukv, m_mla_out_g, m_gdn_conv_w, m_gdn_a_log, m_gdn_dt_bias, m_gdn_norm_g, m_w_out, m_mix_post_g, m_ffn2_pre_g, m_ffn2_w_gate, m_ffn2_w_up, m_ffn2_w_down, m_ffn2_post_g, v_ffn1_pre_g, v_ffn1_w_gate, v_ffn1_w_up, v_ffn1_w_down, v_ffn1_post_g, v_mix_pre_g, v_w_in, v_mla_q_norm_g, v_mla_w_uq, v_mla_kv_norm_g, v_mla_w_ukv, v_mla_out_g, v_gdn_conv_w, v_gdn_a_log, v_gdn_dt_bias, v_gdn_norm_g, v_w_out, v_mix_post_g, v_ffn2_pre_g, v_ffn2_w_gate, v_ffn2_w_up, v_ffn2_w_down, v_ffn2_post_g):
    given = dict(locals())
    order = ["ffn1_pre_g", "ffn1_w_gate", "ffn1_w_up", "ffn1_w_down", "ffn1_post_g", "mix_pre_g", "w_in", "mla_q_norm_g",
             "mla_w_uq", "mla_kv_norm_g", "mla_w_ukv", "mla_out_g", "gdn_conv_w", "gdn_a_log", "gdn_dt_bias", "gdn_norm_g",
             "w_out", "mix_post_g", "ffn2_pre_g", "ffn2_w_gate", "ffn2_w_up", "ffn2_w_down", "ffn2_post_g"]
    assert sorted(order) == sorted(list(FFN_NAMES) + list(OTHER) + list(SMALL) + ["gdn_conv_w"])

    def drop_depth(a):
        return a[0] if a.ndim == 3 else a

    wts = {n: drop_depth(given[n]) for n in order}
    mom = {n: drop_depth(given["m_" + n]) for n in order}
    var = {n: drop_depth(given["v_" + n]) for n in order}
    me = _block_of(_place())

    def wire(n):
        return (wts[n].T if n in BY_COLUMNS else wts[n]).astype(BF16)

    (ffn1,) = _exchange("gather_first", _Gather([jnp.stack([wire(n) for n in FFN_NAMES[:3]])]))
    mid = _Gather([wire(n) for n in ("w_in", "mla_w_uq", "mla_w_ukv")])
    late = _Gather([jnp.stack([wire(n) for n in FFN_NAMES[3:]]), wire("w_out")])
    conv_at = lax.dynamic_update_slice(jnp.zeros((N_DEV, CONV_SHARD[0] * CONV_SHARD[1]), F32),
                                       wts["gdn_conv_w"].reshape(1, -1), (me, 0))
    conv_all = _all_reduce_small("gather_conv", _pad_lanes(conv_at.reshape(1, -1), 0, SMALL_ROWS * ROW).reshape(SMALL_ROWS, ROW))
    full = {n: wts[n] for n in SMALL}
    full["ffn1"] = ffn1
    full["gdn_conv_w"] = conv_all.reshape(-1)[:CONV_LANES].reshape((N_DEV,) + CONV_SHARD).transpose(1, 0, 2).reshape(CONV_SHAPE)

    loss_lanes, dx, grads, landed, begun, token = _local_step(x[0], positions[0], loss_target[0], full, mid, late)
    loss = lax.psum(jnp.sum(loss_lanes), MESH_AXES)

    grad, outs = {}, {"delta": {}, "new_m": {}, "new_v": {}}

    def finish(n, blocks, after=None):
        total = _sum_blocks("sum_" + n, blocks, after=after)
        grad[n] = total.T if n in BY_COLUMNS else total
        outs["delta"][n], outs["new_m"][n], outs["new_v"][n] = _adamw("adamw_" + n, wts[n], grad[n], mom[n], var[n])

    for n, blocks in landed.items():
        finish(n, blocks, after=token)
        token = outs["new_v"][n]
    for n, handles in begun.items():
        finish(n, _scatter_end("scatter_" + n + "_end", handles, after=token))
        token = outs["new_v"][n]

    small_sum = _all_reduce_small("reduce_small", _pack_small(grads, grads["gdn_conv_w"].reshape(-1), REDUCE_ROWS))
    small_grad, conv_grad_full = _unpack_small(small_sum)
    grad.update(small_grad)
    grad["gdn_conv_w"] = lax.dynamic_slice(conv_grad_full[:CONV_LANES].reshape(CONV_SHAPE), (0, me * CONV_SHARD[1]), CONV_SHARD)
    outs["grad"] = grad
    small = [_pack_small(s, s["gdn_conv_w"].reshape(-1), SMALL_ROWS) for s in (wts, grad, mom, var)]
    for kind, s in zip(("delta", "new_m", "new_v"), _adamw("adamw_small", *small)):
        vecs, conv = _unpack_small(s)
        outs[kind].update(vecs)
        outs[kind]["gdn_conv_w"] = conv[:CONV_SHARD[0] * CONV_SHARD[1]].reshape(CONV_SHARD)
    result = [loss, dx[None]]
    for kind in ("grad", "delta", "new_m", "new_v"):
        result += [outs[kind][n].reshape(given[n].shape) for n in order]
    return tuple(result)
```

```python
import jax
import jax.numpy as jnp
from jax import lax
from jax.experimental import pallas as pl
from jax.experimental.pallas import tpu as pltpu

F32 = jnp.float32
BF16 = jnp.bfloat16
HI = lax.Precision.HIGH

N_DEV = 8
N_HEADS = 8
SLOT = 128
MLA_Q_RANK = 256
MLA_KV_RANK = 128
MLA_NOPE = 64
MLA_ROPE = 32
MLA_V = 64
GDN_D = 64
GDN_CONV = 4
GDN_CHUNK = 64
ROPE_THETA = 10000.0
EPS = 1e-6
ADAM_LR, ADAM_B1, ADAM_B2, ADAM_EPS, ADAM_WD, ADAM_STEP = 0.001, 0.9, 0.999, 1e-08, 0.01, 10


def _dot(a, b, ca, cb, precision=None):
    lead = a.ndim - 2
    batch = tuple(range(lead))
    return lax.dot_general(a, b, (((lead + ca,), (lead + cb,)), (batch, batch)), precision=precision,
                           preferred_element_type=F32)


def _nn(a, b, precision=None):
    return _dot(a, b, 1, 0, precision)


def _nt(a, b, precision=None):
    return _dot(a, b, 1, 1, precision)


def _tn(a, b, precision=None):
    return _dot(a, b, 0, 0, precision)


def _sigmoid(x):
    return 1.0 / (1.0 + jnp.exp(-x))


def _silu(x):
    return x * _sigmoid(x)


def _rms(x, g, n):
    ms = jnp.sum(x * x, axis=-1, keepdims=True) * (1.0 / n)
    return x * lax.rsqrt(ms + EPS) * g


def _chunk_masks():
    c = GDN_CHUNK
    i = lax.broadcasted_iota(jnp.int32, (c, c), 0)
    j = lax.broadcasted_iota(jnp.int32, (c, c), 1)
    lower = i >= j
    strict = i > j
    eye = (i == j).astype(F32)
    blocks = []
    b = 1
    while b < c:
        same = (i // (2 * b)) == (j // (2 * b))
        blocks.append(same & ((i % (2 * b)) >= b) & ((j % (2 * b)) < b))
        b *= 2
    return lower, strict, eye, blocks


def _unit_lower_inverse(low, eye, blocks):
    t = eye - jnp.where(blocks[0], low, 0.0)
    for m in blocks[1:]:
        lo = jnp.where(m, low, 0.0)
        t = t - _nn(t, _nn(lo, t, HI), HI)
    return t


@jax.custom_vjp
def _known_inverse(low, tinv):
    return tinv


def _known_inverse_fwd(low, tinv):
    return tinv, tinv


def _known_inverse_bwd(tinv, dt):
    return -_tn(tinv, _nt(dt, tinv, HI), HI), jnp.zeros_like(tinv)


_known_inverse.defvjp(_known_inverse_fwd, _known_inverse_bwd)

_PRODUCTS = {"nn": _nn, "nt": _nt, "tn": _tn}


@jax.custom_vjp
def _known_nn(a, b, c):
    return c


@jax.custom_vjp
def _known_nt(a, b, c):
    return c


@jax.custom_vjp
def _known_tn(a, b, c):
    return c


def _known_fwd(a, b, c):
    return c, (a, b, c)


_known_nn.defvjp(_known_fwd, lambda r, dc: (_nt(dc, r[1], HI), _tn(r[0], dc, HI), jnp.zeros_like(r[2])))
_known_nt.defvjp(_known_fwd, lambda r, dc: (_nn(dc, r[1], HI), _tn(dc, r[0], HI), jnp.zeros_like(r[2])))
_known_tn.defvjp(_known_fwd, lambda r, dc: (_nt(r[1], dc, HI), _nn(r[0], dc, HI), jnp.zeros_like(r[2])))
_KNOWN = {"nn": _known_nn, "nt": _known_nt, "tn": _known_tn}
GDN_PRODUCTS = 8
GDN_KEPT = 2 + GDN_PRODUCTS


def _gdn_chunk(q, k, v, gc, bb, s, masks, known=None):
    lower, strict, eye, blocks = masks
    made = []

    def product(kind, a, b):
        c = _PRODUCTS[kind](a, b, HI) if known is None else _KNOWN[kind](a, b, known[1 + len(made)])
        made.append(c)
        return c

    qs = q * (GDN_D ** -0.5)
    gct = jnp.swapaxes(gc, -1, -2)
    decay = jnp.exp(jnp.where(lower, gc - gct, -1e30))
    kb = k * bb
    low = jnp.where(strict, product("nt", kb, k) * decay, 0.0)
    tinv = _unit_lower_inverse(low, eye, blocks) if known is None else _known_inverse(low, known[0])
    eg = jnp.exp(gc)
    w = product("nn", tinv, kb * eg)
    u = product("nn", tinv, v * bb)
    attn = product("nt", qs, k) * decay
    last = lax.broadcasted_iota(jnp.int32, gc.shape[-2:], 0) == GDN_CHUNK - 1
    g_end = jnp.sum(jnp.where(last, gc, 0.0), axis=-2, keepdims=True)
    k_dec = k * jnp.exp(g_end - gc)
    v_new = u - product("nn", w, s)
    o = product("nn", qs * eg, s) + product("nn", attn, v_new)
    s_new = s * jnp.exp(g_end) + product("tn", k_dec, v_new)
    assert len(made) == GDN_PRODUCTS
    return o, s_new, [tinv] + made


GDN_GROUP = 8
GDN_GROUPS = N_HEADS // GDN_GROUP


def _group_heads(ref):
    return jnp.stack([ref[:, pl.ds(j * SLOT, GDN_D)] for j in range(GDN_GROUP)])


def _ungroup_heads(ref, val):
    pad = jnp.zeros((GDN_CHUNK, SLOT - GDN_D), F32)
    for j in range(GDN_GROUP):
        ref[:, pl.ds(j * SLOT, GDN_D)] = val[j]
        ref[:, pl.ds(j * SLOT + GDN_D, SLOT - GDN_D)] = pad


def _gdn_fwd(qkv, gb, bb, carry=None):
    t = qkv.shape[0]
    n_chunks = t // GDN_CHUNK
    d = GDN_D

    def body(q_ref, k_ref, v_ref, g_ref, b_ref, o_ref, keep_ref, s_ref):
        @pl.when(pl.program_id(1) == 0)
        def _():
            s_ref[...] = jnp.zeros_like(s_ref)

        s = s_ref[...]
        keep_ref[:, 0, 0] = s
        o, s_new, made = _gdn_chunk(*[_group_heads(r) for r in (q_ref, k_ref, v_ref, g_ref, b_ref)], s, _chunk_masks())
        for i, val in enumerate(made):
            keep_ref[:, 0, 1 + i] = val
        s_ref[...] = s_new
        _ungroup_heads(o_ref, o)

    def spec(kind=0):
        return pl.BlockSpec((GDN_CHUNK, GDN_GROUP * SLOT), lambda h, n: (n, kind * GDN_GROUPS + h))

    return _call_carrying(
        body, carry, (qkv, qkv, qkv, gb, bb), name="gdn_fwd",
        grid=(GDN_GROUPS, n_chunks),
        in_specs=[spec(0), spec(1), spec(2), spec(), spec()],
        out_specs=[spec(), pl.BlockSpec((GDN_GROUP, 1, GDN_KEPT, d, d), lambda h, n: (h, n, 0, 0, 0))],
        out_shape=[jax.ShapeDtypeStruct((t, N_HEADS * SLOT), F32), jax.ShapeDtypeStruct((N_HEADS, n_chunks, GDN_KEPT, d, d), F32)],
        scratch_shapes=[pltpu.VMEM((GDN_GROUP, d, d), F32)],
        compiler_params=pltpu.CompilerParams(dimension_semantics=("arbitrary", "arbitrary")),
    )


def _gdn_bwd(qkv, gb, bb, keep, do, carry=None):
    t = qkv.shape[0]
    n_chunks = t // GDN_CHUNK
    d = GDN_D

    def body(q_ref, k_ref, v_ref, g_ref, b_ref, keep_ref, do_ref, dqkv_ref, dg_ref, db_ref, ds_ref):
        @pl.when(pl.program_id(1) == 0)
        def _():
            ds_ref[...] = jnp.zeros_like(ds_ref)

        masks = _chunk_masks()
        known = [keep_ref[:, 0, 1 + i] for i in range(GDN_KEPT - 1)]
        _, pull = jax.vjp(lambda *a: _gdn_chunk(*a, masks, known)[:2],
                          *[_group_heads(r) for r in (q_ref, k_ref, v_ref, g_ref, b_ref)], keep_ref[:, 0, 0])
        dq, dk, dv, dg, db, ds = pull((_group_heads(do_ref), ds_ref[...]))
        ds_ref[...] = ds
        for i, val in enumerate((dq, dk, dv)):
            _ungroup_heads(dqkv_ref.at[i], val)
        _ungroup_heads(dg_ref, dg)
        _ungroup_heads(db_ref, db)

    def spec(kind=0):
        return pl.BlockSpec((GDN_CHUNK, GDN_GROUP * SLOT), lambda h, n: (n_chunks - 1 - n, kind * GDN_GROUPS + h))

    return _call_carrying(
        body, carry, (qkv, qkv, qkv, gb, bb, keep, do), name="gdn_bwd",
        grid=(GDN_GROUPS, n_chunks),
        in_specs=[spec(0), spec(1), spec(2), spec(), spec(),
                  pl.BlockSpec((GDN_GROUP, 1, GDN_KEPT, d, d), lambda h, n: (h, n_chunks - 1 - n, 0, 0, 0)), spec()],
        out_specs=[pl.BlockSpec((3, GDN_CHUNK, GDN_GROUP * SLOT), lambda h, n: (0, n_chunks - 1 - n, h)), spec(), spec()],
        out_shape=[jax.ShapeDtypeStruct((3, t, N_HEADS * SLOT), F32)] + [jax.ShapeDtypeStruct((t, N_HEADS * SLOT), F32)] * 2,
        scratch_shapes=[pltpu.VMEM((GDN_GROUP, d, d), F32)],
        compiler_params=pltpu.CompilerParams(dimension_semantics=("arbitrary", "arbitrary")),
    )


def _rowwise(name, fn, rows, consts, outs, sums=(), tm=512):
    rows = [x if isinstance(x, tuple) else (x, x.shape[1], 0) for x in rows]
    t = rows[0][0].shape[0]
    tm = min(tm, t)
    steps = t // tm
    n_r, n_c, n_o, n_s = len(rows), len(consts), len(outs), len(sums)

    def window(width, block):
        return pl.BlockSpec((tm, width), lambda i: (i, block))

    def body(*refs):
        r, c = refs[:n_r], refs[n_r:n_r + n_c]
        o, s = refs[n_r + n_c:n_r + n_c + n_o], refs[n_r + n_c + n_o:]
        vals, tot = fn([x[...] for x in r], [x[...] for x in c])
        for ref, val in zip(o, vals):
            ref[...] = val.astype(ref.dtype)
        if n_s:
            @pl.when(pl.program_id(0) == 0)
            def _():
                for ref in s:
                    ref[...] = jnp.zeros_like(ref)

            for ref, val in zip(s, tot):
                ref[...] += val

    return pl.pallas_call(
        body, name=name,
        grid=(steps,),
        in_specs=[window(w, b) for _, w, b in rows] + [pl.BlockSpec(x.shape, lambda i: (0, 0)) for x in consts],
        out_specs=[pl.BlockSpec((tm, w), lambda i: (i, 0)) for w, _ in outs]
        + [pl.BlockSpec((1, w), lambda i: (0, 0)) for w in sums],
        out_shape=[jax.ShapeDtypeStruct((t, w), dt) for w, dt in outs]
        + [jax.ShapeDtypeStruct((1, w), F32) for w in sums],
        compiler_params=pltpu.CompilerParams(dimension_semantics=("arbitrary",)),
    )(*[x for x, _, _ in rows], *consts)


def _tile(dim, target):
    if dim <= target:
        return dim
    best = None
    for cand in range(128, target + 1, 128):
        if dim % cand == 0:
            best = cand
    assert best is not None, (dim, target)
    return best


def _matmul(name, a, b, mode, out_dtype=F32, tm=1024, tn=1024, tk=2048, after=None):
    if mode == "nn":
        (m, k), n = a.shape, b.shape[1]
    elif mode == "nt":
        (m, k), n = a.shape, b.shape[0]
    else:
        (k, m), n = a.shape, b.shape[1]
    tm, tn, tk = _tile(m, tm), _tile(n, tn), _tile(k, tk)
    k_steps = k // tk
    product = {"nn": _nn, "nt": _nt, "tn": _tn}[mode]

    def body(a_ref, b_ref, *rest):
        o_ref, acc_ref = rest[-2:]
        part = product(a_ref[...].astype(BF16), b_ref[...].astype(BF16))
        if k_steps == 1:
            o_ref[...] = part.astype(o_ref.dtype)
        else:
            kk = pl.program_id(2)

            @pl.when(kk == 0)
            def _():
                acc_ref[...] = part

            @pl.when(kk > 0)
            def _():
                acc_ref[...] += part

            @pl.when(kk == k_steps - 1)
            def _():
                o_ref[...] = acc_ref[...].astype(o_ref.dtype)

    a_spec = pl.BlockSpec((tk, tm), lambda i, j, kk: (kk, i)) if mode == "tn" else pl.BlockSpec((tm, tk), lambda i, j, kk: (i, kk))
    b_spec = pl.BlockSpec((tn, tk), lambda i, j, kk: (j, kk)) if mode == "nt" else pl.BlockSpec((tk, tn), lambda i, j, kk: (kk, j))
    ordered = [] if after is None else [after]
    return pl.pallas_call(
        body, name=name,
        grid=(m // tm, n // tn, k_steps),
        in_specs=[a_spec, b_spec] + [pl.BlockSpec(memory_space=pl.ANY)] * len(ordered),
        out_specs=pl.BlockSpec((tm, tn), lambda i, j, kk: (i, j)),
        out_shape=jax.ShapeDtypeStruct((m, n), out_dtype),
        scratch_shapes=[pltpu.VMEM((tm, tn) if k_steps > 1 else (8, 128), F32)],
        compiler_params=pltpu.CompilerParams(dimension_semantics=("parallel", "parallel", "arbitrary")),
    )(a, b, *ordered)


FFN_TM = 512
FFN_BWD_TM = 256
FFN_BLOCKS = 4
FFN_GATE, FFN_UP, FFN_DOWN = 0, 1, 2


def _ffn_weight_specs(ffn_w, first):
    _, _, rows, dm = ffn_w.shape

    def spec(k):
        return pl.BlockSpec((FFN_BLOCKS, None, rows, dm), lambda i, j: (j, first + k, 0, 0))

    return [spec(FFN_GATE), spec(FFN_UP), spec(FFN_DOWN)], FFN_BLOCKS * rows


def _ffn_fwd(name, x, g_pre, ffn_w, first, g_post, carry=None):
    t, dm = x.shape
    tm = min(FFN_TM, t)
    w_specs, tf = _ffn_weight_specs(ffn_w, first)
    f_steps = N_DEV // FFN_BLOCKS

    def body(x_ref, gpre_ref, wg_ref, wu_ref, wd_ref, gpost_ref, h_ref, y_ref, hg_ref, hu_ref, xn_ref, acc_ref):
        j = pl.program_id(1)

        @pl.when(j == 0)
        def _():
            xn_ref[...] = _rms(x_ref[...], gpre_ref[...], dm).astype(BF16)
            acc_ref[...] = jnp.zeros_like(acc_ref)

        xn = xn_ref[...]
        wg, wu, wd = (r[...].reshape(tf, dm) for r in (wg_ref, wu_ref, wd_ref))
        hg, hu = _nt(xn, wg), _nt(xn, wu)
        hg_ref[...] = hg.astype(BF16)
        hu_ref[...] = hu.astype(BF16)
        a = _silu(hg) * hu
        acc_ref[...] += _nn(a.astype(BF16), wd)

        @pl.when(j == f_steps - 1)
        def _():
            h = acc_ref[...]
            h_ref[...] = h
            y_ref[...] = x_ref[...] + 0.5 * _rms(h, gpost_ref[...], dm)

    row = pl.BlockSpec((tm, dm), lambda i, j: (i, 0))
    vec = pl.BlockSpec((1, dm), lambda i, j: (0, 0))
    wide = pl.BlockSpec((tm, tf), lambda i, j: (i, j))
    return _call_carrying(
        body, carry, (x, g_pre, ffn_w, ffn_w, ffn_w, g_post), name=name,
        grid=(t // tm, f_steps),
        in_specs=[row, vec, *w_specs, vec],
        out_specs=[row, row, wide, wide],
        out_shape=[jax.ShapeDtypeStruct((t, dm), F32)] * 2 + [jax.ShapeDtypeStruct((t, f_steps * tf), BF16)] * 2,
        scratch_shapes=[pltpu.VMEM((tm, dm), BF16), pltpu.VMEM((tm, dm), F32)],
        compiler_params=pltpu.CompilerParams(dimension_semantics=("arbitrary", "arbitrary")),
    )


def _ffn_bwd(name, x, h, hg, hu, dy, g_pre, ffn_w, first, g_post, carry=None):
    t, dm = x.shape
    tm = min(FFN_BWD_TM, t)
    w_specs, tf = _ffn_weight_specs(ffn_w, first)
    f_steps = N_DEV // FFN_BLOCKS
    f = f_steps * tf

    def post(hv, g):
        return 0.5 * _rms(hv, g, dm)

    def pre(xv, g):
        return _rms(xv, g, dm)

    def body(x_ref, h_ref, dy_ref, hg_ref, hu_ref, gpre_ref, wg_ref, wu_ref, wd_ref, gpost_ref,
             dx_ref, xn_ref, dh_ref, a_ref, dhg_ref, dhu_ref, dgpre_ref, dgpost_ref, acc_ref):
        i, j = pl.program_id(0), pl.program_id(1)

        @pl.when((i == 0) & (j == 0))
        def _():
            dgpre_ref[...] = jnp.zeros_like(dgpre_ref)
            dgpost_ref[...] = jnp.zeros_like(dgpost_ref)

        @pl.when(j == 0)
        def _():
            xn_ref[...] = pre(x_ref[...], gpre_ref[...]).astype(BF16)
            _, pull = jax.vjp(post, h_ref[...], gpost_ref[...])
            dh, dg = pull(dy_ref[...])
            dh_ref[...] = dh.astype(BF16)
            dgpost_ref[...] += dg
            acc_ref[...] = jnp.zeros_like(acc_ref)

        wg, wu, wd = (r[...].reshape(tf, dm) for r in (wg_ref, wu_ref, wd_ref))
        hg, hu = hg_ref[...].astype(F32), hu_ref[...].astype(F32)
        da = _nt(dh_ref[...], wd)
        sig = _sigmoid(hg)
        act = hg * sig
        dhu = (da * act).astype(BF16)
        dhg = (da * hu * (sig * (1.0 + hg * (1.0 - sig)))).astype(BF16)
        a_ref[...] = (act * hu).astype(BF16)
        dhg_ref[...] = dhg
        dhu_ref[...] = dhu
        acc_ref[...] += _nn(dhg, wg) + _nn(dhu, wu)

        @pl.when(j == f_steps - 1)
        def _():
            _, pull = jax.vjp(pre, x_ref[...], gpre_ref[...])
            dx, dg = pull(acc_ref[...])
            dx_ref[...] = dy_ref[...] + dx
            dgpre_ref[...] += dg

    row = pl.BlockSpec((tm, dm), lambda i, j: (i, 0))
    vec = pl.BlockSpec((1, dm), lambda i, j: (0, 0))
    wide = pl.BlockSpec((tm, tf), lambda i, j: (i, j))
    return _call_carrying(
        body, carry, (x, h, dy, hg, hu, g_pre, ffn_w, ffn_w, ffn_w, g_post), name=name,
        grid=(t // tm, f_steps),
        in_specs=[row, row, row, wide, wide, vec, *w_specs, vec],
        out_specs=[row, row, row, wide, wide, wide, vec, vec],
        out_shape=[jax.ShapeDtypeStruct((t, dm), F32), jax.ShapeDtypeStruct((t, dm), BF16), jax.ShapeDtypeStruct((t, dm), BF16),
                   jax.ShapeDtypeStruct((t, f), BF16), jax.ShapeDtypeStruct((t, f), BF16), jax.ShapeDtypeStruct((t, f), BF16),
                   jax.ShapeDtypeStruct((1, dm), F32), jax.ShapeDtypeStruct((1, dm), F32)],
        scratch_shapes=[pltpu.VMEM((tm, dm), F32)],
        compiler_params=pltpu.CompilerParams(dimension_semantics=("arbitrary", "arbitrary")),
    )


ATT_T = 512
ATT_GROUP = 4
ATT_GROUP_FWD = 8
ATT_SCALE = (MLA_NOPE + MLA_ROPE) ** -0.5


def _stack_slots(ref, group):
    return jnp.stack([ref[:, pl.ds(j * SLOT, SLOT)] for j in range(group)])


def _unstack_slots(ref, val):
    for j in range(val.shape[0]):
        ref[:, pl.ds(j * SLOT, SLOT)] = val[j].astype(ref.dtype)


def _scores(q, k, diagonal):
    s = _nt(q, k) * ATT_SCALE
    if diagonal:
        row = lax.broadcasted_iota(jnp.int32, s.shape[1:], 0)
        col = lax.broadcasted_iota(jnp.int32, s.shape[1:], 1)
        s = jnp.where(col <= row, s, -1e30)
    return s


def _attn_pairs(steps, q_major):
    pairs = ([(qi, ki) for qi in range(steps) for ki in range(qi + 1)] if q_major
             else [(qi, ki) for ki in range(steps) for qi in range(ki, steps)])
    return jnp.array([p[0] for p in pairs], jnp.int32), jnp.array([p[1] for p in pairs], jnp.int32)


def _attn_specs(tile, group):
    width = group * SLOT
    return (pl.BlockSpec((tile, width), lambda h, p, qt, kt: (qt[p], h)),
            pl.BlockSpec((tile, width), lambda h, p, qt, kt: (kt[p], h)))


def _attn_fwd(q, k, v):
    t = q.shape[0]
    tile = min(ATT_T, t)
    steps = t // tile
    g = ATT_GROUP_FWD

    strip = min(SLOT, tile)

    def body(qt_ref, kt_ref, q_ref, k_ref, v_ref, o_ref, lse_ref, m_ref, l_ref, alpha_ref, acc_ref, s_ref, p_ref):
        qi, ki = qt_ref[pl.program_id(1)], kt_ref[pl.program_id(1)]

        @pl.when(ki == 0)
        def _():
            m_ref[...] = jnp.full_like(m_ref, -1e30)
            l_ref[...] = jnp.zeros_like(l_ref)
            acc_ref[...] = jnp.zeros_like(acc_ref)

        def step(diagonal):
            s_ref[...] = _nt(_stack_slots(k_ref, g), _stack_slots(q_ref, g))
            for j in range(tile // strip):
                c = pl.ds(j * strip, strip)
                s = s_ref[:, :, c] * ATT_SCALE
                if diagonal:
                    key = lax.broadcasted_iota(jnp.int32, s.shape[1:], 0)
                    query = lax.broadcasted_iota(jnp.int32, s.shape[1:], 1) + j * strip
                    s = jnp.where(key <= query, s, -1e30)
                m_old = m_ref[:, :, c]
                m_new = jnp.maximum(m_old, jnp.max(s, axis=1, keepdims=True))
                p = jnp.exp(s - m_new)
                alpha = jnp.exp(m_old - m_new)
                l_ref[:, :, c] = alpha * l_ref[:, :, c] + jnp.sum(p, axis=1, keepdims=True)
                alpha_ref[:, :, c] = alpha
                m_ref[:, :, c] = m_new
                p_ref[:, :, c] = p.astype(BF16)
            acc_ref[...] = acc_ref[...] * alpha_ref[...] + _tn(_stack_slots(v_ref, g), p_ref[...])

        @pl.when(ki < qi)
        def _():
            step(False)

        @pl.when(ki == qi)
        def _():
            step(True)
            out = acc_ref[...] / l_ref[...]
            lse = jnp.broadcast_to(m_ref[...] + jnp.log(l_ref[...]), out.shape)
            for j in range(g):
                o_ref[:, pl.ds(j * SLOT, SLOT)] = out[j].T
                lse_ref[:, pl.ds(j * SLOT, SLOT)] = lse[j].T

    q_spec, k_spec = _attn_specs(tile, g)
    tables = _attn_pairs(steps, True)
    return pl.pallas_call(
        body, name="attn_fwd",
        grid_spec=pltpu.PrefetchScalarGridSpec(
            num_scalar_prefetch=2, grid=(N_HEADS // g, tables[0].shape[0]),
            in_specs=[q_spec, k_spec, k_spec], out_specs=[q_spec, q_spec],
            scratch_shapes=[pltpu.VMEM((g, 1, tile), F32), pltpu.VMEM((g, 1, tile), F32), pltpu.VMEM((g, 1, tile), F32),
                            pltpu.VMEM((g, SLOT, tile), F32), pltpu.VMEM((g, tile, tile), F32), pltpu.VMEM((g, tile, tile), BF16)]),
        out_shape=[jax.ShapeDtypeStruct((t, N_HEADS * SLOT), F32)] * 2,
        compiler_params=pltpu.CompilerParams(dimension_semantics=("parallel", "arbitrary")),
    )(*tables, q, k, v)


def _attn_grad_scores(q, k, v, do, lse_ref, delta_ref, diagonal):
    g = ATT_GROUP
    p = jnp.exp(_scores(q, k, diagonal) - _stack_slots(lse_ref, g)[:, :, 0:1])
    dp = _nt(do, v)
    return p, p * (dp - _stack_slots(delta_ref, g)[:, :, 0:1]) * ATT_SCALE


def _attn_bwd(q, k, v, do, lse, delta):
    t = q.shape[0]
    tile = min(ATT_T, t)
    steps = t // tile
    g = ATT_GROUP

    def body(qt_ref, kt_ref, q_ref, k_ref, v_ref, do_ref, lse_ref, delta_ref, dq_ref, dk_ref, dv_ref, dk_acc, dv_acc):
        qi, ki = qt_ref[pl.program_id(1)], kt_ref[pl.program_id(1)]

        @pl.when(pl.program_id(1) == 0)
        def _():
            dq_ref[...] = jnp.zeros_like(dq_ref)

        def step(diagonal):
            qq, kk = _stack_slots(q_ref, g), _stack_slots(k_ref, g)
            do_b = _stack_slots(do_ref, g).astype(BF16)
            p, ds = _attn_grad_scores(qq, kk, _stack_slots(v_ref, g), do_b, lse_ref, delta_ref, diagonal)
            ds = ds.astype(BF16)
            dv_acc[...] += _tn(p.astype(BF16), do_b)
            dk_acc[...] += _tn(ds, qq)
            dq = _nn(ds, kk)
            rows = pl.ds(pl.multiple_of(qi * tile, tile), tile)
            for j in range(g):
                dq_ref[rows, pl.ds(j * SLOT, SLOT)] += dq[j]

        @pl.when(qi == ki)
        def _():
            dk_acc[...] = jnp.zeros_like(dk_acc)
            dv_acc[...] = jnp.zeros_like(dv_acc)
            step(True)

        @pl.when(qi > ki)
        def _():
            step(False)

        @pl.when(qi == steps - 1)
        def _():
            _unstack_slots(dk_ref, dk_acc[...])
            _unstack_slots(dv_ref, dv_acc[...])

    q_spec, k_spec = _attn_specs(tile, g)
    tables = _attn_pairs(steps, False)
    return pl.pallas_call(
        body, name="attn_bwd",
        grid_spec=pltpu.PrefetchScalarGridSpec(
            num_scalar_prefetch=2, grid=(N_HEADS // g, tables[0].shape[0]),
            in_specs=[q_spec, k_spec, k_spec, q_spec, q_spec, q_spec],
            out_specs=[pl.BlockSpec((t, g * SLOT), lambda h, p, qt, kt: (0, h)), k_spec, k_spec],
            scratch_shapes=[pltpu.VMEM((g, tile, SLOT), F32), pltpu.VMEM((g, tile, SLOT), F32)]),
        out_shape=[jax.ShapeDtypeStruct((t, N_HEADS * SLOT), F32)] * 3,
        compiler_params=pltpu.CompilerParams(dimension_semantics=("parallel", "arbitrary")),
    )(*tables, q, k, v, do, lse, delta)


CONV_PAD = 8


def _fill_padded(ref, val):
    t = val.shape[0]
    zeros = jnp.zeros((CONV_PAD, val.shape[1]), val.dtype)
    ref[pl.ds(0, CONV_PAD)] = zeros
    ref[pl.ds(CONV_PAD + t, CONV_PAD)] = zeros
    ref[pl.ds(CONV_PAD, t)] = val


def _shifted(ref, s):
    return ref[pl.ds(CONV_PAD - s, ref.shape[0] - 2 * CONV_PAD)]


def _l2norm(x):
    return x * lax.rsqrt(jnp.sum(x * x, axis=-1, keepdims=True) + EPS)


def _conv_pre(x_pad, w):
    y = w[GDN_CONV - 1:GDN_CONV, :] * _shifted(x_pad, 0)
    for s in range(1, GDN_CONV):
        y = y + w[GDN_CONV - 1 - s:GDN_CONV - s, :] * _shifted(x_pad, s)
    return y


def _gdn_conv_fwd(x, w):
    t, width = x.shape

    def body(x_ref, w_ref, o_ref, x_pad):
        _fill_padded(x_pad, x_ref[...])
        act = _silu(_conv_pre(x_pad, w_ref[...]))
        normed = pl.program_id(0) < 2 * N_HEADS
        o_ref[...] = jnp.where(normed, _l2norm(act), act)

    return pl.pallas_call(
        body, name="gdn_conv_fwd",
        grid=(width // SLOT,),
        in_specs=[pl.BlockSpec((t, SLOT), lambda j: (0, j)), pl.BlockSpec((GDN_CONV, SLOT), lambda j: (0, j))],
        out_specs=pl.BlockSpec((t, SLOT), lambda j: (0, j)),
        out_shape=jax.ShapeDtypeStruct((t, width), F32),
        scratch_shapes=[pltpu.VMEM((t + 2 * CONV_PAD, SLOT), F32)],
        compiler_params=pltpu.CompilerParams(dimension_semantics=("parallel",)),
    )(x, w)


def _gdn_conv_bwd(x, w, dout):
    t, width = x.shape

    def body(x_ref, w_ref, do_ref, dx_ref, dw_ref, x_pad, dy_pad):
        wv = w_ref[...]
        _fill_padded(x_pad, x_ref[...])
        y = _conv_pre(x_pad, wv)
        sig = _sigmoid(y)
        act = y * sig
        _, pull = jax.vjp(_l2norm, act)
        normed = pl.program_id(0) < 2 * N_HEADS
        dact = jnp.where(normed, pull(do_ref[0])[0], do_ref[0])
        dy = dact * (sig * (1.0 + y * (1.0 - sig)))
        _fill_padded(dy_pad, dy)
        dx = wv[GDN_CONV - 1:GDN_CONV, :] * dy
        for s in range(1, GDN_CONV):
            dx = dx + wv[GDN_CONV - 1 - s:GDN_CONV - s, :] * _shifted(dy_pad, -s)
        dx_ref[...] = dx.astype(BF16)
        for s in range(GDN_CONV):
            dw_ref[GDN_CONV - 1 - s:GDN_CONV - s, :] = jnp.sum(dy * _shifted(x_pad, s), axis=0, keepdims=True)

    col = pl.BlockSpec((t, SLOT), lambda j: (0, j))
    tap = pl.BlockSpec((GDN_CONV, SLOT), lambda j: (0, j))
    return pl.pallas_call(
        body, name="gdn_conv_bwd",
        grid=(width // SLOT,),
        in_specs=[col, tap, pl.BlockSpec((1, t, SLOT), lambda j: (j // N_HEADS, 0, j % N_HEADS))],
        out_specs=[col, tap],
        out_shape=[jax.ShapeDtypeStruct((t, width), BF16), jax.ShapeDtypeStruct((GDN_CONV, width), F32)],
        scratch_shapes=[pltpu.VMEM((t + 2 * CONV_PAD, SLOT), F32)] * 2,
        compiler_params=pltpu.CompilerParams(dimension_semantics=("parallel",)),
    )(x, w, dout)


def _softplus(x):
    e = jnp.exp(-jnp.abs(x))
    u = 1.0 + e
    log1p = jnp.where(u == 1.0, e, jnp.log(u) * e / jnp.where(u == 1.0, 1.0, u - 1.0))
    return jnp.maximum(x, 0.0) + log1p


def _chunk_running_sum(x, reverse=False):
    tm = x.shape[0]
    at = lax.broadcasted_iota(jnp.int32, x.shape, 0) % GDN_CHUNK
    step = 1
    while step < GDN_CHUNK:
        if reverse:
            x = x + jnp.where(at < GDN_CHUNK - step, pltpu.roll(x, tm - step, 0), 0.0)
        else:
            x = x + jnp.where(at >= step, pltpu.roll(x, step, 0), 0.0)
        step *= 2
    return x


def _gates_fwd(ab, a_log, dt_bias):
    def fn(rows, consts):
        (abv,), (alog, dtb) = rows, consts
        g = _chunk_running_sum(-jnp.exp(alog) * _softplus(abv + dtb))
        beta = _sigmoid(abv)
        shape = (abv.shape[0], SLOT)
        g_slots = [jnp.broadcast_to(g[:, h:h + 1], shape) for h in range(N_HEADS)]
        b_slots = [jnp.broadcast_to(beta[:, N_HEADS + h:N_HEADS + h + 1], shape) for h in range(N_HEADS)]
        return [jnp.concatenate(g_slots, axis=1), jnp.concatenate(b_slots, axis=1)], []

    width = N_HEADS * SLOT
    return _rowwise("gdn_gates_fwd", fn, [ab], [a_log, dt_bias], [(width, F32), (width, F32)])


def _gates_bwd(ab, a_log, dt_bias, dg, dbeta):
    def fn(rows, consts):
        (abv, dgv, dbv), (alog, dtb) = rows, consts
        lane = lax.broadcasted_iota(jnp.int32, abv.shape, 1)
        dg_tok = jnp.zeros_like(abv)
        db_tok = jnp.zeros_like(abv)
        for h in range(N_HEADS):
            dg_tok = dg_tok + jnp.where(lane == h, jnp.sum(dgv[:, h * SLOT:(h + 1) * SLOT], axis=1, keepdims=True), 0.0)
            db_tok = db_tok + jnp.where(lane == N_HEADS + h, jnp.sum(dbv[:, h * SLOT:(h + 1) * SLOT], axis=1, keepdims=True), 0.0)
        dg_tok = _chunk_running_sum(dg_tok, reverse=True)
        xa = abv + dtb
        g = -jnp.exp(alog) * _softplus(xa)
        da = dg_tok * (-jnp.exp(alog)) * _sigmoid(xa)
        beta = _sigmoid(abv)
        dab = jnp.where(lane < N_HEADS, da, db_tok * beta * (1.0 - beta))
        dab = jnp.where(lane < 2 * N_HEADS, dab, 0.0)
        d_alog = jnp.sum(jnp.where(lane < N_HEADS, dg_tok * g, 0.0), axis=0, keepdims=True)
        d_dtb = jnp.sum(jnp.where(lane < N_HEADS, da, 0.0), axis=0, keepdims=True)
        return [dab], [d_alog, d_dtb]

    return _rowwise("gdn_gates_bwd", fn, [ab, dg, dbeta], [a_log, dt_bias], [(SLOT, F32)], sums=[SLOT, SLOT])


ROPE_HALF = MLA_ROPE // 2


def _rope_tables(positions):
    freqs = ROPE_THETA ** (-jnp.arange(ROPE_HALF, dtype=F32) / ROPE_HALF)
    ang = positions.astype(F32)[:, None] * freqs
    cos, sin = jnp.cos(ang), jnp.sin(ang)
    t = positions.shape[0]
    ones, zeros = jnp.ones((t, MLA_NOPE), F32), jnp.zeros((t, MLA_NOPE), F32)
    tail = jnp.zeros((t, SLOT - MLA_NOPE - MLA_ROPE), F32)
    half0 = jnp.zeros((t, ROPE_HALF), F32)
    same = jnp.concatenate([ones, cos, cos, tail], axis=1)
    from_low = jnp.concatenate([zeros, half0, sin, tail], axis=1)
    from_high = jnp.concatenate([zeros, -sin, half0, tail], axis=1)
    return same, from_low, from_high


def _rope(x, tabs):
    same, from_low, from_high = tabs
    width = x.shape[1]
    return x * same + pltpu.roll(x, ROPE_HALF, 1) * from_low + pltpu.roll(x, width - ROPE_HALF, 1) * from_high


def _rope_transposed(dy, tabs):
    same, from_low, from_high = tabs
    width = dy.shape[1]
    return dy * same + pltpu.roll(dy * from_low, width - ROPE_HALF, 1) + pltpu.roll(dy * from_high, ROPE_HALF, 1)


def _tile_slots(tab):
    return jnp.concatenate([tab] * N_HEADS, axis=1)


A_WIDTH = MLA_Q_RANK + MLA_KV_RANK + 2 * SLOT
A_KPE = MLA_Q_RANK + MLA_KV_RANK
A_AB = A_KPE + SLOT
WIDE = N_HEADS * SLOT


def _mla_front_fwd(proj_a, tabs, g_q, g_kv, w_uq, w_kv):
    def fn(rows, consts):
        pa, *tb = rows
        gq, gkv, wuq, wkv = consts
        cqn = _rms(pa[:, :MLA_Q_RANK], gq, MLA_Q_RANK).astype(BF16)
        ckvn = _rms(pa[:, MLA_Q_RANK:A_KPE], gkv, MLA_KV_RANK).astype(BF16)
        kv = _nt(ckvn, wkv)
        q = _rope(_nt(cqn, wuq), [_tile_slots(x) for x in tb])
        k = kv[:, :WIDE] + _tile_slots(_rope(pa[:, A_KPE:A_AB], tb))
        return [cqn, ckvn, q, k, kv[:, WIDE:]], []

    return _rowwise("mla_front_fwd", fn, [proj_a, *tabs], [g_q, g_kv, w_uq, w_kv],
                    [(MLA_Q_RANK, BF16), (MLA_KV_RANK, BF16)] + [(WIDE, BF16)] * 3)


def _mla_front_bwd(proj_a, tabs, g_q, g_kv, w_uq, w_kv, dq, dk, dv, dab):
    def fn(rows, consts):
        pa, t0, t1, t2, dqv, dkv, dvv, da = rows
        gq, gkv, wuq, wkv = consts
        tb = (t0, t1, t2)
        dq_p = _rope_transposed(dqv, [_tile_slots(x) for x in tb]).astype(BF16)
        dkv_p = jnp.concatenate([dkv, dvv], axis=1).astype(BF16)
        dkpe = dkv[:, :SLOT]
        for h in range(1, N_HEADS):
            dkpe = dkpe + dkv[:, h * SLOT:(h + 1) * SLOT]
        _, pull_q = jax.vjp(lambda x, g: _rms(x, g, MLA_Q_RANK), pa[:, :MLA_Q_RANK], gq)
        _, pull_kv = jax.vjp(lambda x, g: _rms(x, g, MLA_KV_RANK), pa[:, MLA_Q_RANK:A_KPE], gkv)
        dcq, dgq = pull_q(_nn(dq_p, wuq))
        dckv, dgkv = pull_kv(_nn(dkv_p, wkv))
        return [jnp.concatenate([dcq, dckv, _rope_transposed(dkpe, tb), da], axis=1), dq_p, dkv_p], [dgq, dgkv]

    return _rowwise("mla_front_bwd", fn, [proj_a, *tabs, dq, dk, dv, dab], [g_q, g_kv, w_uq, w_kv],
                    [(A_WIDTH, BF16), (WIDE, BF16), (2 * WIDE, BF16)], sums=[MLA_Q_RANK, MLA_KV_RANK])


def _slot_sum(x):
    parts = [jnp.broadcast_to(jnp.sum(x[:, h * SLOT:(h + 1) * SLOT], axis=1, keepdims=True), (x.shape[0], SLOT))
             for h in range(N_HEADS)]
    return jnp.concatenate(parts, axis=1)


def _mix_join(o_mla, o_gdn, gate, g_mla, g_gdn):
    mla = _rms(o_mla, g_mla, N_HEADS * MLA_V)
    gdn = o_gdn * lax.rsqrt(_slot_sum(o_gdn * o_gdn) * (1.0 / GDN_D) + EPS) * g_gdn * _silu(gate)
    return mla, gdn


MIX_TM = 256


def _mix_fwd(o_mla, o_gdn, gate, x, g_mla, g_gdn, w_out, g_post):
    dm = x.shape[1]

    def fn(rows, consts):
        om, og, gt, xv = rows
        gm, gg, wo, gp = consts
        cat = jnp.concatenate(_mix_join(om, og, gt, gm, gg), axis=1).astype(BF16)
        mixed = _nn(cat, wo)
        return [cat, mixed, xv + _rms(mixed, gp, dm)], []

    return _rowwise("mix_fwd", fn, [o_mla, o_gdn, gate, x], [g_mla, g_gdn, w_out, g_post],
                    [(2 * WIDE, BF16), (dm, F32), (dm, F32)], tm=MIX_TM)


def _mix_bwd(o_mla, o_gdn, gate, mixed, dy, g_mla, g_gdn, w_out, g_post):
    dm = mixed.shape[1]

    def fn(rows, consts):
        om, og, gt, mx, dyv = rows
        gm, gg, wo, gp = consts
        _, pull_post = jax.vjp(lambda hv, gv: _rms(hv, gv, dm), mx, gp)
        dmixed, dgp = pull_post(dyv)
        dmixed = dmixed.astype(BF16)
        dc = _nt(dmixed, wo)
        _, pull = jax.vjp(lambda x, g: _rms(x, g, N_HEADS * MLA_V), om, gm)
        dom, dgm = pull(dc[:, :WIDE])
        dn_out = dc[:, WIDE:]
        r = lax.rsqrt(_slot_sum(og * og) * (1.0 / GDN_D) + EPS)
        sig = _sigmoid(gt)
        normed = og * r
        dn = dn_out * gg * (gt * sig)
        dog = r * dn - normed * (r * r) * _slot_sum(dn * og) * (1.0 / GDN_D)
        dgt = dn_out * normed * gg * (sig * (1.0 + gt * (1.0 - sig)))
        dgg = jnp.sum(dn_out * normed * (gt * sig), axis=0, keepdims=True)
        return [dmixed, dom, _slot_sum(dom * om), dog, dgt], [dgp, dgm, dgg]

    return _rowwise("mix_bwd", fn, [o_mla, o_gdn, gate, mixed, dy], [g_mla, g_gdn, w_out, g_post],
                    [(dm, BF16), (WIDE, F32), (WIDE, F32), (WIDE, F32), (WIDE, BF16)], sums=[dm, WIDE, WIDE], tm=MIX_TM)


def _proj_fwd(x, g, weights):
    dm = x.shape[1]

    def fn(rows, consts):
        hn = _rms(rows[0], consts[0], dm).astype(BF16)
        return [hn] + [_nt(hn, wv) for wv in consts[1:]], []

    return _rowwise("proj_fwd", fn, [x], [g, *weights], [(dm, BF16)] + [(wv.shape[0], F32) for wv in weights], tm=MIX_TM)


def _proj_bwd(x, g, weights, cots, dy):
    dm = x.shape[1]
    n = len(weights)

    def fn(rows, consts):
        xv, dyv, *parts = rows
        dn = _nn(parts[0], consts[1])
        for p, wv in zip(parts[1:], consts[2:]):
            dn = dn + _nn(p, wv)
        _, pull = jax.vjp(lambda a, gv: _rms(a, gv, dm), xv, consts[0])
        dx, dg = pull(dn)
        return [dyv + dx], [dg]

    assert len(cots) == n
    return _rowwise("proj_bwd", fn, [x, dy, *cots], [g, *weights], [(dm, F32)], sums=[dm], tm=MIX_TM)


def _loss_fwd(y, target):
    dm = y.shape[1]

    def fn(rows, consts):
        err = rows[0] - rows[1]
        sq = err * err
        lanes = sq[:, :SLOT]
        for j in range(1, dm // SLOT):
            lanes = lanes + sq[:, j * SLOT:(j + 1) * SLOT]
        return [err * (1.0 / dm)], [jnp.sum(lanes, axis=0, keepdims=True) * (0.5 / dm)]

    return _rowwise("loss", fn, [y, target], [], [(dm, F32)], sums=[SLOT])


W_IN_CUTS = (0, 256, 384, 416, 1952, 1960, 1968, 2480)


def _heads_out(w, per_head, axis=-1):
    axis = axis % w.ndim
    shape = w.shape
    n = shape[axis] // per_head
    w = w.reshape(shape[:axis] + (n, per_head) + shape[axis + 1:])
    pad = [(0, 0)] * w.ndim
    pad[axis + 1] = (0, SLOT - per_head)
    return jnp.pad(w, pad).reshape(shape[:axis] + (n * SLOT,) + shape[axis + 1:])


def _heads_in(w, per_head, axis=-1):
    axis = axis % w.ndim
    shape = w.shape
    n = shape[axis] // SLOT
    w = w.reshape(shape[:axis] + (n, SLOT) + shape[axis + 1:])
    w = lax.slice_in_dim(w, 0, per_head, axis=axis + 1)
    return w.reshape(shape[:axis] + (n * per_head,) + shape[axis + 1:])


def _pad_lanes(v, lo, width=SLOT):
    return jnp.pad(v, [(0, 0)] * (v.ndim - 1) + [(lo, width - lo - v.shape[-1])])


def _pad_rows(v, lo, rows=SLOT):
    return jnp.pad(v, [(lo, rows - lo - v.shape[0])] + [(0, 0)] * (v.ndim - 1))


def _layout_weights(w):
    c = W_IN_CUTS
    w_in = w["w_in_t"]
    p = {}
    p["w_a"] = jnp.concatenate([w_in[c[0]:c[2]], _pad_rows(w_in[c[2]:c[3]], MLA_NOPE), _pad_rows(w_in[c[4]:c[6]], 0)], axis=0)
    p["w_qkv"] = _heads_out(w_in[c[3]:c[4]], GDN_D, axis=0)
    p["w_gate"] = _heads_out(w_in[c[6]:c[7]], GDN_D, axis=0)
    p["w_uq"] = _heads_out(w["uq_t"], MLA_NOPE + MLA_ROPE, axis=0)
    ukv = w["ukv_t"].reshape(N_HEADS, MLA_NOPE + MLA_V, MLA_KV_RANK)
    p["w_kv"] = jnp.concatenate([_heads_out(ukv[:, :MLA_NOPE].reshape(-1, MLA_KV_RANK), MLA_NOPE, axis=0),
                                 _heads_out(ukv[:, MLA_NOPE:].reshape(-1, MLA_KV_RANK), MLA_V, axis=0)], axis=0)
    p["conv"] = _heads_out(w["gdn_conv_w"], GDN_D)
    p["g_mla_out"] = _heads_out(w["mla_out_g"], MLA_V)
    p["g_gdn"] = jnp.tile(_pad_lanes(w["gdn_norm_g"], 0), (1, N_HEADS))
    p["a_log"] = _pad_lanes(w["gdn_a_log"], 0)
    p["dt_bias"] = _pad_lanes(w["gdn_dt_bias"], 0)
    return p


def _unlayout_grads(d):
    c = W_IN_CUTS
    g = {}
    da = d["w_a"]
    kpe0 = A_KPE + MLA_NOPE
    g["w_in_t"] = jnp.concatenate([da[:A_KPE], da[kpe0:kpe0 + MLA_ROPE], _heads_in(d["w_qkv"], GDN_D, axis=0),
                                   da[A_AB:A_AB + 2 * N_HEADS], _heads_in(d["w_gate"], GDN_D, axis=0)], axis=0)
    assert g["w_in_t"].shape[0] == c[-1]
    g["uq_t"] = _heads_in(d["w_uq"], MLA_NOPE + MLA_ROPE, axis=0)
    dk = _heads_in(d["w_kv"][:WIDE], MLA_NOPE, axis=0).reshape(N_HEADS, MLA_NOPE, MLA_KV_RANK)
    dv = _heads_in(d["w_kv"][WIDE:], MLA_V, axis=0).reshape(N_HEADS, MLA_V, MLA_KV_RANK)
    g["ukv_t"] = jnp.concatenate([dk, dv], axis=1).reshape(-1, MLA_KV_RANK)
    g["w_out"] = _heads_in(d["w_out"], GDN_D, axis=0)
    g["gdn_conv_w"] = _heads_in(d["conv"], GDN_D)
    g["mla_out_g"] = _heads_in(d["g_mla_out"], MLA_V)
    g["gdn_norm_g"] = jnp.sum(d["g_gdn"].reshape(N_HEADS, SLOT), axis=0, keepdims=True)[:, :GDN_D]
    g["gdn_a_log"] = d["a_log"][:, :N_HEADS]
    g["gdn_dt_bias"] = d["dt_bias"][:, :N_HEADS]
    return g


def _weight_grad(name, cots, acts, out_dtype=F32, tm=1024, tn=1024, tk=2048, after=None):
    return _matmul(name, cots, acts, "tn", out_dtype=out_dtype, tm=tm, tn=tn, tk=tk, after=after)


def _by_device(a):
    return a.astype(BF16).reshape((N_DEV, a.shape[0] // N_DEV) + a.shape[1:])


def _rows_of(blocks):
    return blocks.reshape((-1,) + blocks.shape[2:])


def _local_step(x, positions, target, w, mid, late):
    tabs = _rope_tables(positions)

    (h1, x1, hg1, hu1), gathered = _ffn_fwd("ffn1_fwd", x, w["ffn1_pre_g"], w["ffn1"], 0, w["ffn1_post_g"], carry=mid)
    w = dict(w, w_in_t=_rows_of(gathered[0]), uq_t=_rows_of(gathered[1]), ukv_t=_rows_of(gathered[2]),
             gdn_conv_w=gathered[3].transpose(1, 0, 2).reshape(CONV_SHAPE))
    p = _layout_weights(w)
    in_weights = [p["w_a"], p["w_qkv"], p["w_gate"]]
    hn, proj_a, proj_qkv, proj_gate = _proj_fwd(x1, w["mix_pre_g"], in_weights)
    cqn, ckvn, q, k, v = _mla_front_fwd(proj_a, tabs, w["mla_q_norm_g"], w["mla_kv_norm_g"], p["w_uq"], p["w_kv"])
    o_mla, lse = _attn_fwd(q, k, v)
    ab = (proj_a, SLOT, A_AB // SLOT)
    qkv_n = _gdn_conv_fwd(proj_qkv, p["conv"])
    gb, bb = _gates_fwd(ab, p["a_log"], p["dt_bias"])
    (o_gdn, keep), (ffn2, w_out) = _gdn_fwd(qkv_n, gb, bb, carry=late)
    p["w_out"] = _heads_out(_rows_of(w_out), GDN_D, axis=0)
    cat, mixed, x2 = _mix_fwd(o_mla, o_gdn, proj_gate, x1, p["g_mla_out"], p["g_gdn"], p["w_out"], w["mix_post_g"])
    (h2, y, hg2, hu2), _ = _ffn_fwd("ffn2_fwd", x2, w["ffn2_pre_g"], ffn2, 0, w["ffn2_post_g"])
    dy, loss_lanes = _loss_fwd(y, target)

    g = {}
    (dx2, xn2, dh2, a2, dhg2, dhu2, g["ffn2_pre_g"], g["ffn2_post_g"]), _ = _ffn_bwd(
        "ffn2_bwd", x2, h2, hg2, hu2, dy, w["ffn2_pre_g"], ffn2, 0, w["ffn2_post_g"])
    ffn2_grads = _Scatter([_by_device(_weight_grad("ffn2_dw_gate", dhg2, xn2, BF16, tm=1408)),
                           _by_device(_weight_grad("ffn2_dw_up", dhu2, xn2, BF16, tm=1408)),
                           _by_device(_weight_grad("ffn2_dw_down", a2, dh2, BF16, tm=1408))])
    d = {}
    dmixed, do_mla, delta, do_gdn, dgate, g["mix_post_g"], d["g_mla_out"], d["g_gdn"] = _mix_bwd(
        o_mla, o_gdn, proj_gate, mixed, dx2, p["g_mla_out"], p["g_gdn"], p["w_out"], w["mix_post_g"])
    d["w_out"] = _weight_grad("mix_out_dw", cat, dmixed)
    dq, dk, dv = _attn_bwd(q, k, v, do_mla, lse, delta)
    (dqkv_n, dgb, dbb), landed_ffn2 = _gdn_bwd(qkv_n, gb, bb, keep, do_gdn, carry=ffn2_grads)
    dab, d["a_log"], d["dt_bias"] = _gates_bwd(ab, p["a_log"], p["dt_bias"], dgb, dbb)
    dproj_qkv, d["conv"] = _gdn_conv_bwd(proj_qkv, p["conv"], dqkv_n)
    dproj_a, dq_p, dkv_p, g["mla_q_norm_g"], g["mla_kv_norm_g"] = _mla_front_bwd(
        proj_a, tabs, w["mla_q_norm_g"], w["mla_kv_norm_g"], p["w_uq"], p["w_kv"], dq, dk, dv, dab)
    d["w_uq"] = _weight_grad("mla_q_dw", dq_p, cqn)
    d["w_kv"] = _weight_grad("mla_kv_dw", dkv_p, ckvn)
    d["w_a"] = _weight_grad("proj_a_dw", dproj_a, hn, tm=640)
    d["w_qkv"] = _weight_grad("proj_qkv_dw", dproj_qkv, hn)
    d["w_gate"] = _weight_grad("proj_gate_dw", dgate, hn)
    dx1, g["mix_pre_g"] = _proj_bwd(x1, w["mix_pre_g"], in_weights, [dproj_a, dproj_qkv, dgate], dx2)
    g.update(_unlayout_grads(d))
    others = list(OTHER.values())
    (dx, xn1, dh1, a1, dhg1, dhu1, g["ffn1_pre_g"], g["ffn1_post_g"]), landed_others = _ffn_bwd(
        "ffn1_bwd", x, h1, hg1, hu1, dx1, w["ffn1_pre_g"], w["ffn1"], 0, w["ffn1_post_g"], carry=_Scatter([_by_device(g.pop(t)) for t in others]))
    landed = dict(zip(list(FFN_NAMES[3:]) + list(OTHER), list(landed_ffn2) + list(landed_others)))
    packed = _pack_small(g, g["gdn_conv_w"].reshape(-1), REDUCE_ROWS)
    begun = {}
    begun["small"], token = _scatter_begin("reduce_small_begin", jnp.broadcast_to(packed, (N_DEV,) + packed.shape))
    for name, cots, acts in (("ffn1_w_down", a1, dh1), ("ffn1_w_gate", dhg1, xn1), ("ffn1_w_up", dhu1, xn1)):
        blocks = _by_device(_weight_grad(name + "_grad", cots, acts, BF16, tm=1408, after=token))
        begun[name], token = _scatter_begin("scatter_" + name + "_begin", blocks)
    return loss_lanes, dx, g, landed, begun, token


MESH_AXES = ("x", "y", "c")
N_LINKS = N_DEV - 1


def _place():
    return tuple(lax.axis_index(a) for a in MESH_AXES)


def _block_of(dev):
    x, y, c = dev
    return 4 * x + 2 * y + c


def _remote_copy(src, dst, sems, k, to):
    send_sems, recv_sems = sems
    return pltpu.make_async_remote_copy(src_ref=src, dst_ref=dst, send_sem=send_sems.at[k], recv_sem=recv_sems.at[k],
                                        device_id=to, device_id_type=pl.DeviceIdType.MESH)


class _Exchange:
    def __init__(self, arrays):
        self.arrays = list(arrays)
        self.n = len(self.arrays)
        self.specs = [pl.BlockSpec(memory_space=pl.ANY)] * self.n
        self.scratch = [pltpu.SemaphoreType.DMA((self.n * N_LINKS,)), pltpu.SemaphoreType.DMA((self.n * N_LINKS,)),
                        pltpu.SemaphoreType.DMA((self.n,))]

    def split(self, refs):
        n = self.n
        return refs[:n], refs[n:2 * n], (refs[2 * n], refs[2 * n + 1]), refs[2 * n + 2]


class _Gather(_Exchange):
    def out_shape(self):
        return [jax.ShapeDtypeStruct((N_DEV,) + a.shape, a.dtype) for a in self.arrays]

    def _plan(self, ins, outs, sems, local_sems):
        x, y, c = _place()
        me, sibling = (x, y, c), (x, y, 1 - c)
        chips = [(1 - x, y), (x, 1 - y), (1 - x, 1 - y)]

        def copy(a, k, block, to, mine=False):
            src = ins[a] if mine else outs[a].at[_block_of(block)]
            return _remote_copy(src, outs[a].at[_block_of(block)], sems, a * N_LINKS + k, to)

        local = [pltpu.make_async_copy(ins[a], outs[a].at[_block_of(me)], local_sems.at[a]) for a in range(self.n)]
        first = []
        for a in range(self.n):
            first.append(copy(a, 0, me, sibling, mine=True))
            first += [copy(a, 1 + j, me, (*chip, c), mine=True) for j, chip in enumerate(chips)]
        return me, sibling, chips, c, copy, local, first

    def start(self, ins, outs, sems, local_sems):
        *_, local, first = self._plan(ins, outs, sems, local_sems)
        for cp in local + first:
            cp.start()

    def finish(self, ins, outs, sems, local_sems):
        me, sibling, chips, c, copy, local, first = self._plan(ins, outs, sems, local_sems)
        passed = []
        for j, chip in enumerate(chips):
            for a in range(self.n):
                copy(a, 1 + j, (*chip, c), me).wait_recv()
                passed.append(copy(a, 4 + j, (*chip, c), sibling))
                passed[-1].start()
        for a in range(self.n):
            copy(a, 0, sibling, me).wait_recv()
            for j, chip in enumerate(chips):
                copy(a, 4 + j, (*chip, 1 - c), me).wait_recv()
        for cp in first + passed:
            cp.wait_send()
        for cp in local:
            cp.wait()


class _Scatter(_Exchange):
    def out_shape(self):
        return [jax.ShapeDtypeStruct(a.shape, a.dtype) for a in self.arrays]

    def _plan(self, ins, outs, sems, local_sems):
        x, y, c = _place()
        me = _block_of((x, y, c))

        def peer(r):
            return (1 - x if r & 4 else x, 1 - y if r & 2 else y, 1 - c if r & 1 else c)

        local = [pltpu.make_async_copy(ins[a].at[me], outs[a].at[me], local_sems.at[a]) for a in range(self.n)]
        sends = [_remote_copy(ins[a].at[_block_of(peer(r))], outs[a].at[me], sems, a * N_LINKS + r - 1, peer(r))
                 for a in range(self.n) for r in range(1, N_DEV)]
        arrivals = [_remote_copy(ins[a].at[me], outs[a].at[_block_of(peer(r))], sems, a * N_LINKS + r - 1, peer(r))
                    for a in range(self.n) for r in range(1, N_DEV)]
        return local, sends, arrivals

    def start(self, ins, outs, sems, local_sems):
        local, sends, _ = self._plan(ins, outs, sems, local_sems)
        for cp in local + sends:
            cp.start()

    def finish(self, ins, outs, sems, local_sems):
        local, sends, arrivals = self._plan(ins, outs, sems, local_sems)
        for cp in arrivals:
            cp.wait_recv()
        for cp in sends:
            cp.wait_send()
        for cp in local:
            cp.wait()


def _exchange(name, plan):
    def body(*refs):
        parts = plan.split(refs)
        plan.start(*parts)
        plan.finish(*parts)

    return pl.pallas_call(
        body, name=name,
        in_specs=plan.specs,
        out_specs=plan.specs,
        out_shape=plan.out_shape(),
        scratch_shapes=plan.scratch,
    )(*plan.arrays)


def _call_carrying(body, plan, operands, *, name, grid, in_specs, out_specs, out_shape, scratch_shapes, compiler_params):
    if plan is None:
        outs = pl.pallas_call(body, name=name, grid=grid, in_specs=in_specs, out_specs=out_specs, out_shape=out_shape,
                              scratch_shapes=scratch_shapes, compiler_params=compiler_params)(*operands)
        return outs, []
    n_i, n_o, n_s, k = len(in_specs), len(out_specs), len(scratch_shapes), plan.n

    def whole(*refs):
        cut = [n_i, n_i + k, n_i + k + n_o, n_i + 2 * k + n_o, n_i + 2 * k + n_o + n_s]
        own_in, ex_in, own_out, ex_out, own_scr, ex_scr = (refs[a:b] for a, b in zip([0] + cut, cut + [len(refs)]))
        parts = plan.split(ex_in + ex_out + ex_scr)
        first = last = True
        for axis, size in enumerate(grid):
            first = first & (pl.program_id(axis) == 0)
            last = last & (pl.program_id(axis) == size - 1)

        @pl.when(first)
        def _():
            plan.start(*parts)

        body(*own_in, *own_out, *own_scr)

        @pl.when(last)
        def _():
            plan.finish(*parts)

    outs = pl.pallas_call(
        whole, name=name, grid=grid,
        in_specs=list(in_specs) + plan.specs, out_specs=list(out_specs) + plan.specs,
        out_shape=list(out_shape) + plan.out_shape(), scratch_shapes=list(scratch_shapes) + plan.scratch,
        compiler_params=compiler_params,
    )(*operands, *plan.arrays)
    return outs[:n_o], outs[n_o:]


def _row_tile(rows, target=256):
    best = rows
    for cand in range(16, min(rows, target) + 1, 16):
        if rows % cand == 0:
            best = cand
    return best


def _sum_blocks(name, blocks, after=None):
    rows, width = blocks.shape[-2:]
    tm = _row_tile(rows)

    def body(x_ref, *rest):
        acc = x_ref[0].astype(F32)
        for d in range(1, N_DEV):
            acc = acc + x_ref[d].astype(F32)
        rest[-1][...] = acc

    ordered = [] if after is None else [after]
    return pl.pallas_call(
        body, name=name,
        grid=(rows // tm,),
        in_specs=[pl.BlockSpec((N_DEV, tm, width), lambda i: (0, i, 0))] + [pl.BlockSpec(memory_space=pl.ANY)] * len(ordered),
        out_specs=pl.BlockSpec((tm, width), lambda i: (i, 0)),
        out_shape=jax.ShapeDtypeStruct((rows, width), F32),
        compiler_params=pltpu.CompilerParams(dimension_semantics=("parallel",)),
    )(blocks, *ordered)


def _split_plan(src_ref, land_ref, sems):
    x, y, c = _place()
    me = _block_of((x, y, c))

    def peer(r):
        return (1 - x if r & 4 else x, 1 - y if r & 2 else y, 1 - c if r & 1 else c)

    sends = [_remote_copy(src_ref.at[_block_of(peer(r))], land_ref.at[me], sems, r - 1, peer(r)) for r in range(1, N_DEV)]
    arrivals = [_remote_copy(src_ref.at[me], land_ref.at[_block_of(peer(r))], sems, r - 1, peer(r)) for r in range(1, N_DEV)]
    return sends, arrivals


def _scatter_begin(name, blocks):
    def body(src_ref, land_ref, send_sems, recv_sems, src_thru, land_thru, token_ref):
        for cp in _split_plan(src_ref, land_ref, (send_sems, recv_sems))[0]:
            cp.start()
        token_ref[...] = jnp.zeros_like(token_ref)

    hbm, sem = pl.BlockSpec(memory_space=pltpu.HBM), pl.BlockSpec(memory_space=pltpu.SEMAPHORE)
    zone = pltpu.HBM(blocks.shape, blocks.dtype)
    *handles, token = pl.pallas_call(
        body, name=name,
        in_specs=(hbm, hbm),
        out_specs=(sem, sem, hbm, hbm, pl.BlockSpec(memory_space=pltpu.VMEM)),
        out_shape=(pltpu.SemaphoreType.DMA((N_LINKS,)), pltpu.SemaphoreType.DMA((N_LINKS,)), zone, zone,
                   jax.ShapeDtypeStruct((8, SLOT), F32)),
        input_output_aliases={0: 2, 1: 3},
        compiler_params=pltpu.CompilerParams(has_side_effects=pltpu.SideEffectType.DATAFLOW_SIDE_EFFECTING),
    )(pltpu.with_memory_space_constraint(blocks, pltpu.HBM),
      pltpu.with_memory_space_constraint(lax.empty(blocks.shape, blocks.dtype), pltpu.HBM))
    return handles, token


def _scatter_end(name, handles, after):
    send_sems, recv_sems, src, zone = handles

    def body(src_ref, land_ref, send_sems, recv_sems, after_ref, src_dead, got_ref):
        sends, arrivals = _split_plan(src_ref, land_ref, (send_sems, recv_sems))
        for cp in arrivals:
            cp.wait_recv()
        for cp in sends:
            cp.wait_send()

    hbm, sem = pl.BlockSpec(memory_space=pltpu.HBM), pl.BlockSpec(memory_space=pltpu.SEMAPHORE)
    sent, landed = pl.pallas_call(
        body, name=name,
        in_specs=(hbm, hbm, sem, sem, pl.BlockSpec(memory_space=pl.ANY)),
        out_specs=(hbm, hbm),
        out_shape=(pltpu.HBM(src.shape, src.dtype), pltpu.HBM(zone.shape, zone.dtype)),
        input_output_aliases={0: 0, 1: 1},
        compiler_params=pltpu.CompilerParams(has_side_effects=pltpu.SideEffectType.DATAFLOW_SIDE_EFFECTING),
    )(src, zone, send_sems, recv_sems, after)
    me = _block_of(_place())
    return lax.dynamic_update_slice_in_dim(landed, lax.dynamic_slice_in_dim(sent, me, 1, axis=0), me, axis=0)


def _adamw(name, w, g, m, v):
    def fn(rows, consts):
        wv, gv, mv, vv = rows
        m2 = ADAM_B1 * mv + (1.0 - ADAM_B1) * gv
        v2 = ADAM_B2 * vv + (1.0 - ADAM_B2) * jnp.square(gv)
        m_hat = m2 / (1.0 - ADAM_B1 ** ADAM_STEP)
        v_hat = v2 / (1.0 - ADAM_B2 ** ADAM_STEP)
        return [-ADAM_LR * (m_hat / (jnp.sqrt(v_hat) + ADAM_EPS) + ADAM_WD * wv), m2, v2], []

    return _rowwise(name, fn, [w, g, m, v], [], [(w.shape[1], F32)] * 3, tm=_row_tile(w.shape[0]))


ROW = 1024
FFN_NAMES = ("ffn1_w_gate", "ffn1_w_up", "ffn1_w_down", "ffn2_w_gate", "ffn2_w_up", "ffn2_w_down")
OTHER = {"w_in": "w_in_t", "mla_w_uq": "uq_t", "mla_w_ukv": "ukv_t", "w_out": "w_out"}
BY_COLUMNS = ("ffn1_w_gate", "ffn1_w_up", "ffn2_w_gate", "ffn2_w_up", "w_in", "mla_w_uq", "mla_w_ukv")
SMALL = {
    "ffn1_pre_g": (1024, 1024), "ffn1_post_g": (1024, 1024), "mix_pre_g": (1024, 1024), "mla_q_norm_g": (256, 256),
    "mla_kv_norm_g": (128, 128), "mla_out_g": (512, 512), "gdn_a_log": (8, 128), "gdn_dt_bias": (8, 128),
    "gdn_norm_g": (64, 128), "mix_post_g": (1024, 1024), "ffn2_pre_g": (1024, 1024), "ffn2_post_g": (1024, 1024),
}
CONV_SHAPE = (GDN_CONV, 3 * N_HEADS * GDN_D)
CONV_SHARD = (GDN_CONV, CONV_SHAPE[1] // N_DEV)
CONV_LANES = CONV_SHAPE[0] * CONV_SHAPE[1]
SMALL_ROWS = 8
REDUCE_ROWS = 16


def _pack_small(vecs, conv, rows):
    parts = [_pad_lanes(vecs[n].reshape(1, -1), 0, r) for n, (_, r) in SMALL.items()]
    parts.append(conv.reshape(1, -1))
    flat = jnp.concatenate(parts, axis=1)
    return _pad_lanes(flat, 0, rows * ROW).reshape(rows, ROW)


def _unpack_small(buf):
    flat = buf.reshape(1, -1)
    out, at = {}, 0
    for n, (w, r) in SMALL.items():
        out[n] = flat[:, at:at + w]
        at += r
    return out, flat[0, at:]


def kernel(x, positions, ffn1_pre_g, ffn1_w_gate, ffn1_w_up, ffn1_w_down, ffn1_post_g, mix_pre_g, w_in, mla_q_norm_g, mla_w_uq, mla_kv_norm_g, mla_w_ukv, mla_out_g, gdn_conv_w, gdn_a_log, gdn_dt_bias, gdn_norm_g, w_out, mix_post_g, ffn2_pre_g, ffn2_w_gate, ffn2_w_up, ffn2_w_down, ffn2_post_g, loss_target, m_ffn1_pre_g, m_ffn1_w_gate, m_ffn1_w_up, m_ffn1_w_down, m_ffn1_post_g, m_mix_pre_g, m_w_in, m_mla_q_norm_g, m_mla_w_uq, m_mla_kv_norm_g, m_mla_w_ukv, m_mla_out_g, m_gdn_conv_w, m_gdn_a_log, m_gdn_dt_bias, m_gdn_norm_g, m_w_out, m_mix_post_g, m_ffn2_pre_g, m_ffn2_w_gate, m_ffn2_w_up, m_ffn2_w_down, m_ffn2_post_g, v_ffn1_pre_g, v_ffn1_w_gate, v_ffn1_w_up, v_ffn1_w_down, v_ffn1_post_g, v_mix_pre_g, v_w_in, v_mla_q_norm_g, v_mla_w_uq, v_mla_kv_norm_g, v_mla_w_ukv, v_mla_out_g, v_gdn_conv_w, v_gdn_a_log, v_gdn_dt_bias, v_gdn_norm_g, v_w_out, v_mix_post_g, v_ffn2_pre_g, v_ffn2_w_gate, v_ffn2_w_up, v_ffn2_w_down, v_ffn2_post_g):
    given = dict(locals())
    order = ["ffn1_pre_g", "ffn1_w_gate", "ffn1_w_up", "ffn1_w_down", "ffn1_post_g", "mix_pre_g", "w_in", "mla_q_norm_g",
             "mla_w_uq", "mla_kv_norm_g", "mla_w_ukv", "mla_out_g", "gdn_conv_w", "gdn_a_log", "gdn_dt_bias", "gdn_norm_g",
             "w_out", "mix_post_g", "ffn2_pre_g", "ffn2_w_gate", "ffn2_w_up", "ffn2_w_down", "ffn2_post_g"]
    assert sorted(order) == sorted(list(FFN_NAMES) + list(OTHER) + list(SMALL) + ["gdn_conv_w"])

    def drop_depth(a):
        return a[0] if a.ndim == 3 else a

    wts = {n: drop_depth(given[n]) for n in order}
    mom = {n: drop_depth(given["m_" + n]) for n in order}
    var = {n: drop_depth(given["v_" + n]) for n in order}
    me = _block_of(_place())

    def wire(n):
        return (wts[n].T if n in BY_COLUMNS else wts[n]).astype(BF16)

    (ffn1,) = _exchange("gather_first", _Gather([jnp.stack([wire(n) for n in FFN_NAMES[:3]])]))
    mid = _Gather([wire(n) for n in ("w_in", "mla_w_uq", "mla_w_ukv")] + [wts["gdn_conv_w"]])
    late = _Gather([jnp.stack([wire(n) for n in FFN_NAMES[3:]]), wire("w_out")])
    full = {n: wts[n] for n in SMALL}
    full["ffn1"] = ffn1

    loss_lanes, dx, grads, landed, begun, token = _local_step(x[0], positions[0], loss_target[0], full, mid, late)
    loss = lax.psum(jnp.sum(loss_lanes), MESH_AXES)

    grad, outs = {}, {"delta": {}, "new_m": {}, "new_v": {}}

    def finish(n, blocks, after=None):
        total = _sum_blocks("sum_" + n, blocks, after=after)
        grad[n] = total.T if n in BY_COLUMNS else total
        outs["delta"][n], outs["new_m"][n], outs["new_v"][n] = _adamw("adamw_" + n, wts[n], grad[n], mom[n], var[n])

    for n, blocks in landed.items():
        finish(n, blocks, after=token)
        token = outs["new_v"][n]
    small_handles = begun.pop("small")
    for n, handles in begun.items():
        finish(n, _scatter_end("scatter_" + n + "_end", handles, after=token))
        token = outs["new_v"][n]

    small_sum = _sum_blocks("sum_small", _scatter_end("reduce_small_end", small_handles, after=token))
    small_grad, conv_grad_full = _unpack_small(small_sum)
    grad.update(small_grad)
    grad["gdn_conv_w"] = lax.dynamic_slice(conv_grad_full[:CONV_LANES].reshape(CONV_SHAPE), (0, me * CONV_SHARD[1]), CONV_SHARD)
    outs["grad"] = grad
    small = [_pack_small(s, s["gdn_conv_w"].reshape(-1), SMALL_ROWS) for s in (wts, grad, mom, var)]
    for kind, s in zip(("delta", "new_m", "new_v"), _adamw("adamw_small", *small)):
        vecs, conv = _unpack_small(s)
        outs[kind].update(vecs)
        outs[kind]["gdn_conv_w"] = conv[:CONV_SHARD[0] * CONV_SHARD[1]].reshape(CONV_SHARD)
    result = [loss, dx[None]]
    for kind in ("grad", "delta", "new_m", "new_v"):
        result += [outs[kind][n].reshape(given[n].shape) for n in order]
    return tuple(result)
```

```python
import jax
import jax.numpy as jnp
from jax import lax
from jax.experimental import pallas as pl
from jax.experimental.pallas import tpu as pltpu

F32 = jnp.float32
BF16 = jnp.bfloat16
HI = lax.Precision.HIGH

N_DEV = 8
N_HEADS = 8
SLOT = 128
MLA_Q_RANK = 256
MLA_KV_RANK = 128
MLA_NOPE = 64
MLA_ROPE = 32
MLA_V = 64
GDN_D = 64
GDN_CONV = 4
GDN_CHUNK = 64
ROPE_THETA = 10000.0
EPS = 1e-6
ADAM_LR, ADAM_B1, ADAM_B2, ADAM_EPS, ADAM_WD, ADAM_STEP = 0.001, 0.9, 0.999, 1e-08, 0.01, 10


def _dot(a, b, ca, cb, precision=None):
    lead = a.ndim - 2
    batch = tuple(range(lead))
    return lax.dot_general(a, b, (((lead + ca,), (lead + cb,)), (batch, batch)), precision=precision,
                           preferred_element_type=F32)


def _nn(a, b, precision=None):
    return _dot(a, b, 1, 0, precision)


def _nt(a, b, precision=None):
    return _dot(a, b, 1, 1, precision)


def _tn(a, b, precision=None):
    return _dot(a, b, 0, 0, precision)


def _sigmoid(x):
    return 1.0 / (1.0 + jnp.exp(-x))


def _silu(x):
    return x * _sigmoid(x)


def _rms(x, g, n):
    ms = jnp.sum(x * x, axis=-1, keepdims=True) * (1.0 / n)
    return x * lax.rsqrt(ms + EPS) * g


def _chunk_masks():
    c = GDN_CHUNK
    i = lax.broadcasted_iota(jnp.int32, (c, c), 0)
    j = lax.broadcasted_iota(jnp.int32, (c, c), 1)
    lower = i >= j
    strict = i > j
    eye = (i == j).astype(F32)
    blocks = []
    b = 1
    while b < c:
        same = (i // (2 * b)) == (j // (2 * b))
        blocks.append(same & ((i % (2 * b)) >= b) & ((j % (2 * b)) < b))
        b *= 2
    return lower, strict, eye, blocks


def _unit_lower_inverse(low, eye, blocks):
    t = eye - jnp.where(blocks[0], low, 0.0)
    for m in blocks[1:]:
        lo = jnp.where(m, low, 0.0)
        t = t - _nn(t, _nn(lo, t, HI), HI)
    return t


@jax.custom_vjp
def _known_inverse(low, tinv):
    return tinv


def _known_inverse_fwd(low, tinv):
    return tinv, tinv


def _known_inverse_bwd(tinv, dt):
    return -_tn(tinv, _nt(dt, tinv, HI), HI), jnp.zeros_like(tinv)


_known_inverse.defvjp(_known_inverse_fwd, _known_inverse_bwd)

_PRODUCTS = {"nn": _nn, "nt": _nt, "tn": _tn}


@jax.custom_vjp
def _known_nn(a, b, c):
    return c


@jax.custom_vjp
def _known_nt(a, b, c):
    return c


@jax.custom_vjp
def _known_tn(a, b, c):
    return c


def _known_fwd(a, b, c):
    return c, (a, b, c)


_known_nn.defvjp(_known_fwd, lambda r, dc: (_nt(dc, r[1], HI), _tn(r[0], dc, HI), jnp.zeros_like(r[2])))
_known_nt.defvjp(_known_fwd, lambda r, dc: (_nn(dc, r[1], HI), _tn(dc, r[0], HI), jnp.zeros_like(r[2])))
_known_tn.defvjp(_known_fwd, lambda r, dc: (_nt(r[1], dc, HI), _nn(r[0], dc, HI), jnp.zeros_like(r[2])))
_KNOWN = {"nn": _known_nn, "nt": _known_nt, "tn": _known_tn}
GDN_PRODUCTS = 8
GDN_KEPT = 2 + GDN_PRODUCTS


def _gdn_chunk(q, k, v, gc, bb, s, masks, known=None):
    lower, strict, eye, blocks = masks
    made = []

    def product(kind, a, b):
        c = _PRODUCTS[kind](a, b, HI) if known is None else _KNOWN[kind](a, b, known[1 + len(made)])
        made.append(c)
        return c

    qs = q * (GDN_D ** -0.5)
    gct = jnp.swapaxes(gc, -1, -2)
    decay = jnp.exp(jnp.where(lower, gc - gct, -1e30))
    kb = k * bb
    low = jnp.where(strict, product("nt", kb, k) * decay, 0.0)
    tinv = _unit_lower_inverse(low, eye, blocks) if known is None else _known_inverse(low, known[0])
    eg = jnp.exp(gc)
    w = product("nn", tinv, kb * eg)
    u = product("nn", tinv, v * bb)
    attn = product("nt", qs, k) * decay
    last = lax.broadcasted_iota(jnp.int32, gc.shape[-2:], 0) == GDN_CHUNK - 1
    g_end = jnp.sum(jnp.where(last, gc, 0.0), axis=-2, keepdims=True)
    k_dec = k * jnp.exp(g_end - gc)
    v_new = u - product("nn", w, s)
    o = product("nn", qs * eg, s) + product("nn", attn, v_new)
    s_new = s * jnp.exp(g_end) + product("tn", k_dec, v_new)
    assert len(made) == GDN_PRODUCTS
    return o, s_new, [tinv] + made


GDN_GROUP = 8
GDN_GROUPS = N_HEADS // GDN_GROUP


def _group_heads(ref):
    return jnp.stack([ref[:, pl.ds(j * SLOT, GDN_D)] for j in range(GDN_GROUP)])


def _ungroup_heads(ref, val):
    pad = jnp.zeros((GDN_CHUNK, SLOT - GDN_D), F32)
    for j in range(GDN_GROUP):
        ref[:, pl.ds(j * SLOT, GDN_D)] = val[j]
        ref[:, pl.ds(j * SLOT + GDN_D, SLOT - GDN_D)] = pad


def _gdn_fwd(qkv, gb, bb, carry=None):
    t = qkv.shape[0]
    n_chunks = t // GDN_CHUNK
    d = GDN_D

    def body(q_ref, k_ref, v_ref, g_ref, b_ref, o_ref, keep_ref, s_ref):
        @pl.when(pl.program_id(1) == 0)
        def _():
            s_ref[...] = jnp.zeros_like(s_ref)

        s = s_ref[...]
        keep_ref[:, 0, 0] = s
        o, s_new, made = _gdn_chunk(*[_group_heads(r) for r in (q_ref, k_ref, v_ref, g_ref, b_ref)], s, _chunk_masks())
        for i, val in enumerate(made):
            keep_ref[:, 0, 1 + i] = val
        s_ref[...] = s_new
        _ungroup_heads(o_ref, o)

    def spec(kind=0):
        return pl.BlockSpec((GDN_CHUNK, GDN_GROUP * SLOT), lambda h, n: (n, kind * GDN_GROUPS + h))

    return _call_carrying(
        body, carry, (qkv, qkv, qkv, gb, bb), name="gdn_fwd",
        grid=(GDN_GROUPS, n_chunks),
        in_specs=[spec(0), spec(1), spec(2), spec(), spec()],
        out_specs=[spec(), pl.BlockSpec((GDN_GROUP, 1, GDN_KEPT, d, d), lambda h, n: (h, n, 0, 0, 0))],
        out_shape=[jax.ShapeDtypeStruct((t, N_HEADS * SLOT), F32), jax.ShapeDtypeStruct((N_HEADS, n_chunks, GDN_KEPT, d, d), F32)],
        scratch_shapes=[pltpu.VMEM((GDN_GROUP, d, d), F32)],
        compiler_params=pltpu.CompilerParams(dimension_semantics=("arbitrary", "arbitrary")),
    )


def _gdn_bwd(qkv, gb, bb, keep, do, carry=None):
    t = qkv.shape[0]
    n_chunks = t // GDN_CHUNK
    d = GDN_D

    def body(q_ref, k_ref, v_ref, g_ref, b_ref, keep_ref, do_ref, dqkv_ref, dg_ref, db_ref, ds_ref):
        @pl.when(pl.program_id(1) == 0)
        def _():
            ds_ref[...] = jnp.zeros_like(ds_ref)

        masks = _chunk_masks()
        known = [keep_ref[:, 0, 1 + i] for i in range(GDN_KEPT - 1)]
        _, pull = jax.vjp(lambda *a: _gdn_chunk(*a, masks, known)[:2],
                          *[_group_heads(r) for r in (q_ref, k_ref, v_ref, g_ref, b_ref)], keep_ref[:, 0, 0])
        dq, dk, dv, dg, db, ds = pull((_group_heads(do_ref), ds_ref[...]))
        ds_ref[...] = ds
        for i, val in enumerate((dq, dk, dv)):
            _ungroup_heads(dqkv_ref.at[i], val)
        _ungroup_heads(dg_ref, dg)
        _ungroup_heads(db_ref, db)

    def spec(kind=0):
        return pl.BlockSpec((GDN_CHUNK, GDN_GROUP * SLOT), lambda h, n: (n_chunks - 1 - n, kind * GDN_GROUPS + h))

    return _call_carrying(
        body, carry, (qkv, qkv, qkv, gb, bb, keep, do), name="gdn_bwd",
        grid=(GDN_GROUPS, n_chunks),
        in_specs=[spec(0), spec(1), spec(2), spec(), spec(),
                  pl.BlockSpec((GDN_GROUP, 1, GDN_KEPT, d, d), lambda h, n: (h, n_chunks - 1 - n, 0, 0, 0)), spec()],
        out_specs=[pl.BlockSpec((3, GDN_CHUNK, GDN_GROUP * SLOT), lambda h, n: (0, n_chunks - 1 - n, h)), spec(), spec()],
        out_shape=[jax.ShapeDtypeStruct((3, t, N_HEADS * SLOT), F32)] + [jax.ShapeDtypeStruct((t, N_HEADS * SLOT), F32)] * 2,
        scratch_shapes=[pltpu.VMEM((GDN_GROUP, d, d), F32)],
        compiler_params=pltpu.CompilerParams(dimension_semantics=("arbitrary", "arbitrary")),
    )


def _rowwise(name, fn, rows, consts, outs, sums=(), tm=512):
    rows = [x if isinstance(x, tuple) else (x, x.shape[1], 0) for x in rows]
    t = rows[0][0].shape[0]
    tm = min(tm, t)
    steps = t // tm
    n_r, n_c, n_o, n_s = len(rows), len(consts), len(outs), len(sums)

    def window(width, block):
        return pl.BlockSpec((tm, width), lambda i: (i, block))

    def body(*refs):
        r, c = refs[:n_r], refs[n_r:n_r + n_c]
        o, s = refs[n_r + n_c:n_r + n_c + n_o], refs[n_r + n_c + n_o:]
        vals, tot = fn([x[...] for x in r], [x[...] for x in c])
        for ref, val in zip(o, vals):
            ref[...] = val.astype(ref.dtype)
        if n_s:
            @pl.when(pl.program_id(0) == 0)
            def _():
                for ref in s:
                    ref[...] = jnp.zeros_like(ref)

            for ref, val in zip(s, tot):
                ref[...] += val

    return pl.pallas_call(
        body, name=name,
        grid=(steps,),
        in_specs=[window(w, b) for _, w, b in rows] + [pl.BlockSpec(x.shape, lambda i: (0, 0)) for x in consts],
        out_specs=[pl.BlockSpec((tm, w), lambda i: (i, 0)) for w, _ in outs]
        + [pl.BlockSpec((1, w), lambda i: (0, 0)) for w in sums],
        out_shape=[jax.ShapeDtypeStruct((t, w), dt) for w, dt in outs]
        + [jax.ShapeDtypeStruct((1, w), F32) for w in sums],
        compiler_params=pltpu.CompilerParams(dimension_semantics=("arbitrary",)),
    )(*[x for x, _, _ in rows], *consts)


def _tile(dim, target):
    if dim <= target:
        return dim
    best = None
    for cand in range(128, target + 1, 128):
        if dim % cand == 0:
            best = cand
    assert best is not None, (dim, target)
    return best


def _matmul(name, a, b, mode, out_dtype=F32, tm=1024, tn=1024, tk=2048, after=None):
    if mode == "nn":
        (m, k), n = a.shape, b.shape[1]
    elif mode == "nt":
        (m, k), n = a.shape, b.shape[0]
    else:
        (k, m), n = a.shape, b.shape[1]
    tm, tn, tk = _tile(m, tm), _tile(n, tn), _tile(k, tk)
    k_steps = k // tk
    product = {"nn": _nn, "nt": _nt, "tn": _tn}[mode]

    def body(a_ref, b_ref, *rest):
        o_ref, acc_ref = rest[-2:]
        part = product(a_ref[...].astype(BF16), b_ref[...].astype(BF16))
        if k_steps == 1:
            o_ref[...] = part.astype(o_ref.dtype)
        else:
            kk = pl.program_id(2)

            @pl.when(kk == 0)
            def _():
                acc_ref[...] = part

            @pl.when(kk > 0)
            def _():
                acc_ref[...] += part

            @pl.when(kk == k_steps - 1)
            def _():
                o_ref[...] = acc_ref[...].astype(o_ref.dtype)

    a_spec = pl.BlockSpec((tk, tm), lambda i, j, kk: (kk, i)) if mode == "tn" else pl.BlockSpec((tm, tk), lambda i, j, kk: (i, kk))
    b_spec = pl.BlockSpec((tn, tk), lambda i, j, kk: (j, kk)) if mode == "nt" else pl.BlockSpec((tk, tn), lambda i, j, kk: (kk, j))
    ordered = [] if after is None else [after]
    return pl.pallas_call(
        body, name=name,
        grid=(m // tm, n // tn, k_steps),
        in_specs=[a_spec, b_spec] + [pl.BlockSpec(memory_space=pl.ANY)] * len(ordered),
        out_specs=pl.BlockSpec((tm, tn), lambda i, j, kk: (i, j)),
        out_shape=jax.ShapeDtypeStruct((m, n), out_dtype),
        scratch_shapes=[pltpu.VMEM((tm, tn) if k_steps > 1 else (8, 128), F32)],
        compiler_params=pltpu.CompilerParams(dimension_semantics=("parallel", "parallel", "arbitrary")),
    )(a, b, *ordered)


FFN_TM = 512
FFN_BWD_TM = 256
FFN_BLOCKS = 4
FFN_GATE, FFN_UP, FFN_DOWN = 0, 1, 2


def _ffn_weight_specs(ffn_w, first):
    _, _, rows, dm = ffn_w.shape

    def spec(k):
        return pl.BlockSpec((FFN_BLOCKS, None, rows, dm), lambda i, j: (j, first + k, 0, 0))

    return [spec(FFN_GATE), spec(FFN_UP), spec(FFN_DOWN)], FFN_BLOCKS * rows


def _ffn_fwd(name, x, g_pre, ffn_w, first, g_post, carry=None):
    t, dm = x.shape
    tm = min(FFN_TM, t)
    w_specs, tf = _ffn_weight_specs(ffn_w, first)
    f_steps = N_DEV // FFN_BLOCKS

    def body(x_ref, gpre_ref, wg_ref, wu_ref, wd_ref, gpost_ref, h_ref, y_ref, hg_ref, hu_ref, xn_ref, acc_ref):
        j = pl.program_id(1)

        @pl.when(j == 0)
        def _():
            xn_ref[...] = _rms(x_ref[...], gpre_ref[...], dm).astype(BF16)
            acc_ref[...] = jnp.zeros_like(acc_ref)

        xn = xn_ref[...]
        wg, wu, wd = (r[...].reshape(tf, dm) for r in (wg_ref, wu_ref, wd_ref))
        hg, hu = _nt(xn, wg), _nt(xn, wu)
        hg_ref[...] = hg.astype(BF16)
        hu_ref[...] = hu.astype(BF16)
        a = _silu(hg) * hu
        acc_ref[...] += _nn(a.astype(BF16), wd)

        @pl.when(j == f_steps - 1)
        def _():
            h = acc_ref[...]
            h_ref[...] = h
            y_ref[...] = x_ref[...] + 0.5 * _rms(h, gpost_ref[...], dm)

    row = pl.BlockSpec((tm, dm), lambda i, j: (i, 0))
    vec = pl.BlockSpec((1, dm), lambda i, j: (0, 0))
    wide = pl.BlockSpec((tm, tf), lambda i, j: (i, j))
    return _call_carrying(
        body, carry, (x, g_pre, ffn_w, ffn_w, ffn_w, g_post), name=name,
        grid=(t // tm, f_steps),
        in_specs=[row, vec, *w_specs, vec],
        out_specs=[row, row, wide, wide],
        out_shape=[jax.ShapeDtypeStruct((t, dm), F32)] * 2 + [jax.ShapeDtypeStruct((t, f_steps * tf), BF16)] * 2,
        scratch_shapes=[pltpu.VMEM((tm, dm), BF16), pltpu.VMEM((tm, dm), F32)],
        compiler_params=pltpu.CompilerParams(dimension_semantics=("arbitrary", "arbitrary")),
    )


def _ffn_bwd(name, x, h, hg, hu, dy, g_pre, ffn_w, first, g_post, carry=None):
    t, dm = x.shape
    tm = min(FFN_BWD_TM, t)
    w_specs, tf = _ffn_weight_specs(ffn_w, first)
    f_steps = N_DEV // FFN_BLOCKS
    f = f_steps * tf

    def post(hv, g):
        return 0.5 * _rms(hv, g, dm)

    def pre(xv, g):
        return _rms(xv, g, dm)

    def body(x_ref, h_ref, dy_ref, hg_ref, hu_ref, gpre_ref, wg_ref, wu_ref, wd_ref, gpost_ref,
             dx_ref, xn_ref, dh_ref, a_ref, dhg_ref, dhu_ref, dgpre_ref, dgpost_ref, acc_ref):
        i, j = pl.program_id(0), pl.program_id(1)

        @pl.when((i == 0) & (j == 0))
        def _():
            dgpre_ref[...] = jnp.zeros_like(dgpre_ref)
            dgpost_ref[...] = jnp.zeros_like(dgpost_ref)

        @pl.when(j == 0)
        def _():
            xn_ref[...] = pre(x_ref[...], gpre_ref[...]).astype(BF16)
            _, pull = jax.vjp(post, h_ref[...], gpost_ref[...])
            dh, dg = pull(dy_ref[...])
            dh_ref[...] = dh.astype(BF16)
            dgpost_ref[...] += dg
            acc_ref[...] = jnp.zeros_like(acc_ref)

        wg, wu, wd = (r[...].reshape(tf, dm) for r in (wg_ref, wu_ref, wd_ref))
        hg, hu = hg_ref[...].astype(F32), hu_ref[...].astype(F32)
        da = _nt(dh_ref[...], wd)
        sig = _sigmoid(hg)
        act = hg * sig
        dhu = (da * act).astype(BF16)
        dhg = (da * hu * (sig * (1.0 + hg * (1.0 - sig)))).astype(BF16)
        a_ref[...] = (act * hu).astype(BF16)
        dhg_ref[...] = dhg
        dhu_ref[...] = dhu
        acc_ref[...] += _nn(dhg, wg) + _nn(dhu, wu)

        @pl.when(j == f_steps - 1)
        def _():
            _, pull = jax.vjp(pre, x_ref[...], gpre_ref[...])
            dx, dg = pull(acc_ref[...])
            dx_ref[...] = dy_ref[...] + dx
            dgpre_ref[...] += dg

    row = pl.BlockSpec((tm, dm), lambda i, j: (i, 0))
    vec = pl.BlockSpec((1, dm), lambda i, j: (0, 0))
    wide = pl.BlockSpec((tm, tf), lambda i, j: (i, j))
    return _call_carrying(
        body, carry, (x, h, dy, hg, hu, g_pre, ffn_w, ffn_w, ffn_w, g_post), name=name,
        grid=(t // tm, f_steps),
        in_specs=[row, row, row, wide, wide, vec, *w_specs, vec],
        out_specs=[row, row, row, wide, wide, wide, vec, vec],
        out_shape=[jax.ShapeDtypeStruct((t, dm), F32), jax.ShapeDtypeStruct((t, dm), BF16), jax.ShapeDtypeStruct((t, dm), BF16),
                   jax.ShapeDtypeStruct((t, f), BF16), jax.ShapeDtypeStruct((t, f), BF16), jax.ShapeDtypeStruct((t, f), BF16),
                   jax.ShapeDtypeStruct((1, dm), F32), jax.ShapeDtypeStruct((1, dm), F32)],
        scratch_shapes=[pltpu.VMEM((tm, dm), F32)],
        compiler_params=pltpu.CompilerParams(dimension_semantics=("arbitrary", "arbitrary")),
    )


ATT_T = 512
ATT_GROUP = 4
ATT_GROUP_FWD = 8
ATT_SCALE = (MLA_NOPE + MLA_ROPE) ** -0.5


def _stack_slots(ref, group):
    return jnp.stack([ref[:, pl.ds(j * SLOT, SLOT)] for j in range(group)])


def _unstack_slots(ref, val):
    for j in range(val.shape[0]):
        ref[:, pl.ds(j * SLOT, SLOT)] = val[j].astype(ref.dtype)


def _scores(q, k, diagonal):
    s = _nt(q, k) * ATT_SCALE
    if diagonal:
        row = lax.broadcasted_iota(jnp.int32, s.shape[1:], 0)
        col = lax.broadcasted_iota(jnp.int32, s.shape[1:], 1)
        s = jnp.where(col <= row, s, -1e30)
    return s


def _attn_pairs(steps, q_major):
    pairs = ([(qi, ki) for qi in range(steps) for ki in range(qi + 1)] if q_major
             else [(qi, ki) for ki in range(steps) for qi in range(ki, steps)])
    return jnp.array([p[0] for p in pairs], jnp.int32), jnp.array([p[1] for p in pairs], jnp.int32)


def _attn_specs(tile, group):
    width = group * SLOT
    return (pl.BlockSpec((tile, width), lambda h, p, qt, kt: (qt[p], h)),
            pl.BlockSpec((tile, width), lambda h, p, qt, kt: (kt[p], h)))


def _attn_fwd(q, k, v):
    t = q.shape[0]
    tile = min(ATT_T, t)
    steps = t // tile
    g = ATT_GROUP_FWD

    strip = min(SLOT, tile)

    def body(qt_ref, kt_ref, q_ref, k_ref, v_ref, o_ref, lse_ref, m_ref, l_ref, alpha_ref, acc_ref, s_ref, p_ref):
        qi, ki = qt_ref[pl.program_id(1)], kt_ref[pl.program_id(1)]

        @pl.when(ki == 0)
        def _():
            m_ref[...] = jnp.full_like(m_ref, -1e30)
            l_ref[...] = jnp.zeros_like(l_ref)
            acc_ref[...] = jnp.zeros_like(acc_ref)

        def step(diagonal):
            s_ref[...] = _nt(_stack_slots(k_ref, g), _stack_slots(q_ref, g))
            for j in range(tile // strip):
                c = pl.ds(j * strip, strip)
                s = s_ref[:, :, c] * ATT_SCALE
                if diagonal:
                    key = lax.broadcasted_iota(jnp.int32, s.shape[1:], 0)
                    query = lax.broadcasted_iota(jnp.int32, s.shape[1:], 1) + j * strip
                    s = jnp.where(key <= query, s, -1e30)
                m_old = m_ref[:, :, c]
                m_new = jnp.maximum(m_old, jnp.max(s, axis=1, keepdims=True))
                p = jnp.exp(s - m_new)
                alpha = jnp.exp(m_old - m_new)
                l_ref[:, :, c] = alpha * l_ref[:, :, c] + jnp.sum(p, axis=1, keepdims=True)
                alpha_ref[:, :, c] = alpha
                m_ref[:, :, c] = m_new
                p_ref[:, :, c] = p.astype(BF16)
            acc_ref[...] = acc_ref[...] * alpha_ref[...] + _tn(_stack_slots(v_ref, g), p_ref[...])

        @pl.when(ki < qi)
        def _():
            step(False)

        @pl.when(ki == qi)
        def _():
            step(True)
            out = acc_ref[...] / l_ref[...]
            lse = jnp.broadcast_to(m_ref[...] + jnp.log(l_ref[...]), out.shape)
            for j in range(g):
                o_ref[:, pl.ds(j * SLOT, SLOT)] = out[j].T
                lse_ref[:, pl.ds(j * SLOT, SLOT)] = lse[j].T

    q_spec, k_spec = _attn_specs(tile, g)
    tables = _attn_pairs(steps, True)
    return pl.pallas_call(
        body, name="attn_fwd",
        grid_spec=pltpu.PrefetchScalarGridSpec(
            num_scalar_prefetch=2, grid=(N_HEADS // g, tables[0].shape[0]),
            in_specs=[q_spec, k_spec, k_spec], out_specs=[q_spec, q_spec],
            scratch_shapes=[pltpu.VMEM((g, 1, tile), F32), pltpu.VMEM((g, 1, tile), F32), pltpu.VMEM((g, 1, tile), F32),
                            pltpu.VMEM((g, SLOT, tile), F32), pltpu.VMEM((g, tile, tile), F32), pltpu.VMEM((g, tile, tile), BF16)]),
        out_shape=[jax.ShapeDtypeStruct((t, N_HEADS * SLOT), F32)] * 2,
        compiler_params=pltpu.CompilerParams(dimension_semantics=("parallel", "arbitrary")),
    )(*tables, q, k, v)


def _attn_grad_scores(q, k, v, do, lse_ref, delta_ref, diagonal):
    g = ATT_GROUP
    p = jnp.exp(_scores(q, k, diagonal) - _stack_slots(lse_ref, g)[:, :, 0:1])
    dp = _nt(do, v)
    return p, p * (dp - _stack_slots(delta_ref, g)[:, :, 0:1]) * ATT_SCALE


def _attn_bwd(q, k, v, do, lse, delta):
    t = q.shape[0]
    tile = min(ATT_T, t)
    steps = t // tile
    g = ATT_GROUP

    def body(qt_ref, kt_ref, q_ref, k_ref, v_ref, do_ref, lse_ref, delta_ref, dq_ref, dk_ref, dv_ref, dk_acc, dv_acc):
        qi, ki = qt_ref[pl.program_id(1)], kt_ref[pl.program_id(1)]

        @pl.when(pl.program_id(1) == 0)
        def _():
            dq_ref[...] = jnp.zeros_like(dq_ref)

        def step(diagonal):
            qq, kk = _stack_slots(q_ref, g), _stack_slots(k_ref, g)
            do_b = _stack_slots(do_ref, g).astype(BF16)
            p, ds = _attn_grad_scores(qq, kk, _stack_slots(v_ref, g), do_b, lse_ref, delta_ref, diagonal)
            ds = ds.astype(BF16)
            dv_acc[...] += _tn(p.astype(BF16), do_b)
            dk_acc[...] += _tn(ds, qq)
            dq = _nn(ds, kk)
            rows = pl.ds(pl.multiple_of(qi * tile, tile), tile)
            for j in range(g):
                dq_ref[rows, pl.ds(j * SLOT, SLOT)] += dq[j]

        @pl.when(qi == ki)
        def _():
            dk_acc[...] = jnp.zeros_like(dk_acc)
            dv_acc[...] = jnp.zeros_like(dv_acc)
            step(True)

        @pl.when(qi > ki)
        def _():
            step(False)

        @pl.when(qi == steps - 1)
        def _():
            _unstack_slots(dk_ref, dk_acc[...])
            _unstack_slots(dv_ref, dv_acc[...])

    q_spec, k_spec = _attn_specs(tile, g)
    tables = _attn_pairs(steps, False)
    return pl.pallas_call(
        body, name="attn_bwd",
        grid_spec=pltpu.PrefetchScalarGridSpec(
            num_scalar_prefetch=2, grid=(N_HEADS // g, tables[0].shape[0]),
            in_specs=[q_spec, k_spec, k_spec, q_spec, q_spec, q_spec],
            out_specs=[pl.BlockSpec((t, g * SLOT), lambda h, p, qt, kt: (0, h)), k_spec, k_spec],
            scratch_shapes=[pltpu.VMEM((g, tile, SLOT), F32), pltpu.VMEM((g, tile, SLOT), F32)]),
        out_shape=[jax.ShapeDtypeStruct((t, N_HEADS * SLOT), F32)] * 3,
        compiler_params=pltpu.CompilerParams(dimension_semantics=("parallel", "arbitrary")),
    )(*tables, q, k, v, do, lse, delta)


CONV_PAD = 8


def _fill_padded(ref, val):
    t = val.shape[0]
    zeros = jnp.zeros((CONV_PAD, val.shape[1]), val.dtype)
    ref[pl.ds(0, CONV_PAD)] = zeros
    ref[pl.ds(CONV_PAD + t, CONV_PAD)] = zeros
    ref[pl.ds(CONV_PAD, t)] = val


def _shifted(ref, s):
    return ref[pl.ds(CONV_PAD - s, ref.shape[0] - 2 * CONV_PAD)]


def _l2norm(x):
    return x * lax.rsqrt(jnp.sum(x * x, axis=-1, keepdims=True) + EPS)


def _conv_pre(x_pad, w):
    y = w[GDN_CONV - 1:GDN_CONV, :] * _shifted(x_pad, 0)
    for s in range(1, GDN_CONV):
        y = y + w[GDN_CONV - 1 - s:GDN_CONV - s, :] * _shifted(x_pad, s)
    return y


def _gdn_conv_fwd(x, w):
    t, width = x.shape

    def body(x_ref, w_ref, o_ref, x_pad):
        _fill_padded(x_pad, x_ref[...])
        act = _silu(_conv_pre(x_pad, w_ref[...]))
        normed = pl.program_id(0) < 2 * N_HEADS
        o_ref[...] = jnp.where(normed, _l2norm(act), act)

    return pl.pallas_call(
        body, name="gdn_conv_fwd",
        grid=(width // SLOT,),
        in_specs=[pl.BlockSpec((t, SLOT), lambda j: (0, j)), pl.BlockSpec((GDN_CONV, SLOT), lambda j: (0, j))],
        out_specs=pl.BlockSpec((t, SLOT), lambda j: (0, j)),
        out_shape=jax.ShapeDtypeStruct((t, width), F32),
        scratch_shapes=[pltpu.VMEM((t + 2 * CONV_PAD, SLOT), F32)],
        compiler_params=pltpu.CompilerParams(dimension_semantics=("parallel",)),
    )(x, w)


def _gdn_conv_bwd(x, w, dout):
    t, width = x.shape

    def body(x_ref, w_ref, do_ref, dx_ref, dw_ref, x_pad, dy_pad):
        wv = w_ref[...]
        _fill_padded(x_pad, x_ref[...])
        y = _conv_pre(x_pad, wv)
        sig = _sigmoid(y)
        act = y * sig
        _, pull = jax.vjp(_l2norm, act)
        normed = pl.program_id(0) < 2 * N_HEADS
        dact = jnp.where(normed, pull(do_ref[0])[0], do_ref[0])
        dy = dact * (sig * (1.0 + y * (1.0 - sig)))
        _fill_padded(dy_pad, dy)
        dx = wv[GDN_CONV - 1:GDN_CONV, :] * dy
        for s in range(1, GDN_CONV):
            dx = dx + wv[GDN_CONV - 1 - s:GDN_CONV - s, :] * _shifted(dy_pad, -s)
        dx_ref[...] = dx.astype(BF16)
        for s in range(GDN_CONV):
            dw_ref[GDN_CONV - 1 - s:GDN_CONV - s, :] = jnp.sum(dy * _shifted(x_pad, s), axis=0, keepdims=True)

    col = pl.BlockSpec((t, SLOT), lambda j: (0, j))
    tap = pl.BlockSpec((GDN_CONV, SLOT), lambda j: (0, j))
    return pl.pallas_call(
        body, name="gdn_conv_bwd",
        grid=(width // SLOT,),
        in_specs=[col, tap, pl.BlockSpec((1, t, SLOT), lambda j: (j // N_HEADS, 0, j % N_HEADS))],
        out_specs=[col, tap],
        out_shape=[jax.ShapeDtypeStruct((t, width), BF16), jax.ShapeDtypeStruct((GDN_CONV, width), F32)],
        scratch_shapes=[pltpu.VMEM((t + 2 * CONV_PAD, SLOT), F32)] * 2,
        compiler_params=pltpu.CompilerParams(dimension_semantics=("parallel",)),
    )(x, w, dout)


def _softplus(x):
    e = jnp.exp(-jnp.abs(x))
    u = 1.0 + e
    log1p = jnp.where(u == 1.0, e, jnp.log(u) * e / jnp.where(u == 1.0, 1.0, u - 1.0))
    return jnp.maximum(x, 0.0) + log1p


def _chunk_running_sum(x, reverse=False):
    tm = x.shape[0]
    at = lax.broadcasted_iota(jnp.int32, x.shape, 0) % GDN_CHUNK
    step = 1
    while step < GDN_CHUNK:
        if reverse:
            x = x + jnp.where(at < GDN_CHUNK - step, pltpu.roll(x, tm - step, 0), 0.0)
        else:
            x = x + jnp.where(at >= step, pltpu.roll(x, step, 0), 0.0)
        step *= 2
    return x


def _gates_fwd(ab, a_log, dt_bias):
    def fn(rows, consts):
        (abv,), (alog, dtb) = rows, consts
        g = _chunk_running_sum(-jnp.exp(alog) * _softplus(abv + dtb))
        beta = _sigmoid(abv)
        shape = (abv.shape[0], SLOT)
        g_slots = [jnp.broadcast_to(g[:, h:h + 1], shape) for h in range(N_HEADS)]
        b_slots = [jnp.broadcast_to(beta[:, N_HEADS + h:N_HEADS + h + 1], shape) for h in range(N_HEADS)]
        return [jnp.concatenate(g_slots, axis=1), jnp.concatenate(b_slots, axis=1)], []

    width = N_HEADS * SLOT
    return _rowwise("gdn_gates_fwd", fn, [ab], [a_log, dt_bias], [(width, F32), (width, F32)])


def _gates_bwd(ab, a_log, dt_bias, dg, dbeta):
    def fn(rows, consts):
        (abv, dgv, dbv), (alog, dtb) = rows, consts
        lane = lax.broadcasted_iota(jnp.int32, abv.shape, 1)
        dg_tok = jnp.zeros_like(abv)
        db_tok = jnp.zeros_like(abv)
        for h in range(N_HEADS):
            dg_tok = dg_tok + jnp.where(lane == h, jnp.sum(dgv[:, h * SLOT:(h + 1) * SLOT], axis=1, keepdims=True), 0.0)
            db_tok = db_tok + jnp.where(lane == N_HEADS + h, jnp.sum(dbv[:, h * SLOT:(h + 1) * SLOT], axis=1, keepdims=True), 0.0)
        dg_tok = _chunk_running_sum(dg_tok, reverse=True)
        xa = abv + dtb
        g = -jnp.exp(alog) * _softplus(xa)
        da = dg_tok * (-jnp.exp(alog)) * _sigmoid(xa)
        beta = _sigmoid(abv)
        dab = jnp.where(lane < N_HEADS, da, db_tok * beta * (1.0 - beta))
        dab = jnp.where(lane < 2 * N_HEADS, dab, 0.0)
        d_alog = jnp.sum(jnp.where(lane < N_HEADS, dg_tok * g, 0.0), axis=0, keepdims=True)
        d_dtb = jnp.sum(jnp.where(lane < N_HEADS, da, 0.0), axis=0, keepdims=True)
        return [dab], [d_alog, d_dtb]

    return _rowwise("gdn_gates_bwd", fn, [ab, dg, dbeta], [a_log, dt_bias], [(SLOT, F32)], sums=[SLOT, SLOT])


ROPE_HALF = MLA_ROPE // 2


def _rope_tables(positions):
    freqs = ROPE_THETA ** (-jnp.arange(ROPE_HALF, dtype=F32) / ROPE_HALF)
    ang = positions.astype(F32)[:, None] * freqs
    cos, sin = jnp.cos(ang), jnp.sin(ang)
    t = positions.shape[0]
    ones, zeros = jnp.ones((t, MLA_NOPE), F32), jnp.zeros((t, MLA_NOPE), F32)
    tail = jnp.zeros((t, SLOT - MLA_NOPE - MLA_ROPE), F32)
    half0 = jnp.zeros((t, ROPE_HALF), F32)
    same = jnp.concatenate([ones, cos, cos, tail], axis=1)
    from_low = jnp.concatenate([zeros, half0, sin, tail], axis=1)
    from_high = jnp.concatenate([zeros, -sin, half0, tail], axis=1)
    return same, from_low, from_high


def _rope(x, tabs):
    same, from_low, from_high = tabs
    width = x.shape[1]
    return x * same + pltpu.roll(x, ROPE_HALF, 1) * from_low + pltpu.roll(x, width - ROPE_HALF, 1) * from_high


def _rope_transposed(dy, tabs):
    same, from_low, from_high = tabs
    width = dy.shape[1]
    return dy * same + pltpu.roll(dy * from_low, width - ROPE_HALF, 1) + pltpu.roll(dy * from_high, ROPE_HALF, 1)


def _tile_slots(tab):
    return jnp.concatenate([tab] * N_HEADS, axis=1)


A_WIDTH = MLA_Q_RANK + MLA_KV_RANK + 2 * SLOT
A_KPE = MLA_Q_RANK + MLA_KV_RANK
A_AB = A_KPE + SLOT
WIDE = N_HEADS * SLOT


def _mla_front_fwd(proj_a, tabs, g_q, g_kv, w_uq, w_kv):
    def fn(rows, consts):
        pa, *tb = rows
        gq, gkv, wuq, wkv = consts
        cqn = _rms(pa[:, :MLA_Q_RANK], gq, MLA_Q_RANK).astype(BF16)
        ckvn = _rms(pa[:, MLA_Q_RANK:A_KPE], gkv, MLA_KV_RANK).astype(BF16)
        kv = _nt(ckvn, wkv)
        q = _rope(_nt(cqn, wuq), [_tile_slots(x) for x in tb])
        k = kv[:, :WIDE] + _tile_slots(_rope(pa[:, A_KPE:A_AB], tb))
        return [cqn, ckvn, q, k, kv[:, WIDE:]], []

    return _rowwise("mla_front_fwd", fn, [proj_a, *tabs], [g_q, g_kv, w_uq, w_kv],
                    [(MLA_Q_RANK, BF16), (MLA_KV_RANK, BF16)] + [(WIDE, BF16)] * 3)


def _mla_front_bwd(proj_a, tabs, g_q, g_kv, w_uq, w_kv, dq, dk, dv, dab):
    def fn(rows, consts):
        pa, t0, t1, t2, dqv, dkv, dvv, da = rows
        gq, gkv, wuq, wkv = consts
        tb = (t0, t1, t2)
        dq_p = _rope_transposed(dqv, [_tile_slots(x) for x in tb]).astype(BF16)
        dkv_p = jnp.concatenate([dkv, dvv], axis=1).astype(BF16)
        dkpe = dkv[:, :SLOT]
        for h in range(1, N_HEADS):
            dkpe = dkpe + dkv[:, h * SLOT:(h + 1) * SLOT]
        _, pull_q = jax.vjp(lambda x, g: _rms(x, g, MLA_Q_RANK), pa[:, :MLA_Q_RANK], gq)
        _, pull_kv = jax.vjp(lambda x, g: _rms(x, g, MLA_KV_RANK), pa[:, MLA_Q_RANK:A_KPE], gkv)
        dcq, dgq = pull_q(_nn(dq_p, wuq))
        dckv, dgkv = pull_kv(_nn(dkv_p, wkv))
        return [jnp.concatenate([dcq, dckv, _rope_transposed(dkpe, tb), da], axis=1), dq_p, dkv_p], [dgq, dgkv]

    return _rowwise("mla_front_bwd", fn, [proj_a, *tabs, dq, dk, dv, dab], [g_q, g_kv, w_uq, w_kv],
                    [(A_WIDTH, BF16), (WIDE, BF16), (2 * WIDE, BF16)], sums=[MLA_Q_RANK, MLA_KV_RANK])


def _slot_sum(x):
    parts = [jnp.broadcast_to(jnp.sum(x[:, h * SLOT:(h + 1) * SLOT], axis=1, keepdims=True), (x.shape[0], SLOT))
             for h in range(N_HEADS)]
    return jnp.concatenate(parts, axis=1)


def _mix_join(o_mla, o_gdn, gate, g_mla, g_gdn):
    mla = _rms(o_mla, g_mla, N_HEADS * MLA_V)
    gdn = o_gdn * lax.rsqrt(_slot_sum(o_gdn * o_gdn) * (1.0 / GDN_D) + EPS) * g_gdn * _silu(gate)
    return mla, gdn


MIX_TM = 256


def _mix_fwd(o_mla, o_gdn, gate, x, g_mla, g_gdn, w_out, g_post):
    dm = x.shape[1]

    def fn(rows, consts):
        om, og, gt, xv = rows
        gm, gg, wo, gp = consts
        cat = jnp.concatenate(_mix_join(om, og, gt, gm, gg), axis=1).astype(BF16)
        mixed = _nn(cat, wo)
        return [cat, mixed, xv + _rms(mixed, gp, dm)], []

    return _rowwise("mix_fwd", fn, [o_mla, o_gdn, gate, x], [g_mla, g_gdn, w_out, g_post],
                    [(2 * WIDE, BF16), (dm, F32), (dm, F32)], tm=MIX_TM)


def _mix_bwd(o_mla, o_gdn, gate, mixed, dy, g_mla, g_gdn, w_out, g_post):
    dm = mixed.shape[1]

    def fn(rows, consts):
        om, og, gt, mx, dyv = rows
        gm, gg, wo, gp = consts
        _, pull_post = jax.vjp(lambda hv, gv: _rms(hv, gv, dm), mx, gp)
        dmixed, dgp = pull_post(dyv)
        dmixed = dmixed.astype(BF16)
        dc = _nt(dmixed, wo)
        _, pull = jax.vjp(lambda x, g: _rms(x, g, N_HEADS * MLA_V), om, gm)
        dom, dgm = pull(dc[:, :WIDE])
        dn_out = dc[:, WIDE:]
        r = lax.rsqrt(_slot_sum(og * og) * (1.0 / GDN_D) + EPS)
        sig = _sigmoid(gt)
        normed = og * r
        dn = dn_out * gg * (gt * sig)
        dog = r * dn - normed * (r * r) * _slot_sum(dn * og) * (1.0 / GDN_D)
        dgt = dn_out * normed * gg * (sig * (1.0 + gt * (1.0 - sig)))
        dgg = jnp.sum(dn_out * normed * (gt * sig), axis=0, keepdims=True)
        return [dmixed, dom, _slot_sum(dom * om), dog, dgt], [dgp, dgm, dgg]

    return _rowwise("mix_bwd", fn, [o_mla, o_gdn, gate, mixed, dy], [g_mla, g_gdn, w_out, g_post],
                    [(dm, BF16), (WIDE, F32), (WIDE, F32), (WIDE, F32), (WIDE, BF16)], sums=[dm, WIDE, WIDE], tm=MIX_TM)


def _proj_fwd(x, g, weights):
    dm = x.shape[1]

    def fn(rows, consts):
        hn = _rms(rows[0], consts[0], dm).astype(BF16)
        return [hn] + [_nt(hn, wv) for wv in consts[1:]], []

    return _rowwise("proj_fwd", fn, [x], [g, *weights], [(dm, BF16)] + [(wv.shape[0], F32) for wv in weights], tm=MIX_TM)


def _proj_bwd(x, g, weights, cots, dy):
    dm = x.shape[1]
    n = len(weights)

    def fn(rows, consts):
        xv, dyv, *parts = rows
        dn = _nn(parts[0], consts[1])
        for p, wv in zip(parts[1:], consts[2:]):
            dn = dn + _nn(p, wv)
        _, pull = jax.vjp(lambda a, gv: _rms(a, gv, dm), xv, consts[0])
        dx, dg = pull(dn)
        return [dyv + dx], [dg]

    assert len(cots) == n
    return _rowwise("proj_bwd", fn, [x, dy, *cots], [g, *weights], [(dm, F32)], sums=[dm], tm=MIX_TM)


def _loss_fwd(y, target):
    dm = y.shape[1]

    def fn(rows, consts):
        err = rows[0] - rows[1]
        sq = err * err
        lanes = sq[:, :SLOT]
        for j in range(1, dm // SLOT):
            lanes = lanes + sq[:, j * SLOT:(j + 1) * SLOT]
        return [err * (1.0 / dm)], [jnp.sum(lanes, axis=0, keepdims=True) * (0.5 / dm)]

    return _rowwise("loss", fn, [y, target], [], [(dm, F32)], sums=[SLOT])


W_IN_CUTS = (0, 256, 384, 416, 1952, 1960, 1968, 2480)


def _heads_out(w, per_head, axis=-1):
    axis = axis % w.ndim
    shape = w.shape
    n = shape[axis] // per_head
    w = w.reshape(shape[:axis] + (n, per_head) + shape[axis + 1:])
    pad = [(0, 0)] * w.ndim
    pad[axis + 1] = (0, SLOT - per_head)
    return jnp.pad(w, pad).reshape(shape[:axis] + (n * SLOT,) + shape[axis + 1:])


def _heads_in(w, per_head, axis=-1):
    axis = axis % w.ndim
    shape = w.shape
    n = shape[axis] // SLOT
    w = w.reshape(shape[:axis] + (n, SLOT) + shape[axis + 1:])
    w = lax.slice_in_dim(w, 0, per_head, axis=axis + 1)
    return w.reshape(shape[:axis] + (n * per_head,) + shape[axis + 1:])


def _pad_lanes(v, lo, width=SLOT):
    return jnp.pad(v, [(0, 0)] * (v.ndim - 1) + [(lo, width - lo - v.shape[-1])])


def _pad_rows(v, lo, rows=SLOT):
    return jnp.pad(v, [(lo, rows - lo - v.shape[0])] + [(0, 0)] * (v.ndim - 1))


def _layout_weights(w):
    c = W_IN_CUTS
    w_in = w["w_in_t"]
    p = {}
    p["w_a"] = jnp.concatenate([w_in[c[0]:c[2]], _pad_rows(w_in[c[2]:c[3]], MLA_NOPE), _pad_rows(w_in[c[4]:c[6]], 0)], axis=0)
    p["w_qkv"] = _heads_out(w_in[c[3]:c[4]], GDN_D, axis=0)
    p["w_gate"] = _heads_out(w_in[c[6]:c[7]], GDN_D, axis=0)
    p["w_uq"] = _heads_out(w["uq_t"], MLA_NOPE + MLA_ROPE, axis=0)
    ukv = w["ukv_t"].reshape(N_HEADS, MLA_NOPE + MLA_V, MLA_KV_RANK)
    p["w_kv"] = jnp.concatenate([_heads_out(ukv[:, :MLA_NOPE].reshape(-1, MLA_KV_RANK), MLA_NOPE, axis=0),
                                 _heads_out(ukv[:, MLA_NOPE:].reshape(-1, MLA_KV_RANK), MLA_V, axis=0)], axis=0)
    p["conv"] = _heads_out(w["gdn_conv_w"], GDN_D)
    p["g_mla_out"] = _heads_out(w["mla_out_g"], MLA_V)
    p["g_gdn"] = jnp.tile(_pad_lanes(w["gdn_norm_g"], 0), (1, N_HEADS))
    p["a_log"] = _pad_lanes(w["gdn_a_log"], 0)
    p["dt_bias"] = _pad_lanes(w["gdn_dt_bias"], 0)
    return p


def _unlayout_grads(d):
    c = W_IN_CUTS
    g = {}
    da = d["w_a"]
    kpe0 = A_KPE + MLA_NOPE
    g["w_in_t"] = jnp.concatenate([da[:A_KPE], da[kpe0:kpe0 + MLA_ROPE], _heads_in(d["w_qkv"], GDN_D, axis=0),
                                   da[A_AB:A_AB + 2 * N_HEADS], _heads_in(d["w_gate"], GDN_D, axis=0)], axis=0)
    assert g["w_in_t"].shape[0] == c[-1]
    g["uq_t"] = _heads_in(d["w_uq"], MLA_NOPE + MLA_ROPE, axis=0)
    dk = _heads_in(d["w_kv"][:WIDE], MLA_NOPE, axis=0).reshape(N_HEADS, MLA_NOPE, MLA_KV_RANK)
    dv = _heads_in(d["w_kv"][WIDE:], MLA_V, axis=0).reshape(N_HEADS, MLA_V, MLA_KV_RANK)
    g["ukv_t"] = jnp.concatenate([dk, dv], axis=1).reshape(-1, MLA_KV_RANK)
    g["w_out"] = _heads_in(d["w_out"], GDN_D, axis=0)
    g["gdn_conv_w"] = _heads_in(d["conv"], GDN_D)
    g["mla_out_g"] = _heads_in(d["g_mla_out"], MLA_V)
    g["gdn_norm_g"] = jnp.sum(d["g_gdn"].reshape(N_HEADS, SLOT), axis=0, keepdims=True)[:, :GDN_D]
    g["gdn_a_log"] = d["a_log"][:, :N_HEADS]
    g["gdn_dt_bias"] = d["dt_bias"][:, :N_HEADS]
    return g


def _weight_grad(name, cots, acts, out_dtype=F32, tm=1024, tn=1024, tk=2048, after=None):
    return _matmul(name, cots, acts, "tn", out_dtype=out_dtype, tm=tm, tn=tn, tk=tk, after=after)


def _by_device(a):
    return a.astype(BF16).reshape((N_DEV, a.shape[0] // N_DEV) + a.shape[1:])


def _rows_of(blocks):
    return blocks.reshape((-1,) + blocks.shape[2:])


def _local_step(x, positions, target, w, mid, late):
    tabs = _rope_tables(positions)

    (h1, x1, hg1, hu1), gathered = _ffn_fwd("ffn1_fwd", x, w["ffn1_pre_g"], w["ffn1"], 0, w["ffn1_post_g"], carry=mid)
    w = dict(w, w_in_t=_rows_of(gathered[0]), uq_t=_rows_of(gathered[1]), ukv_t=_rows_of(gathered[2]),
             gdn_conv_w=gathered[3].transpose(1, 0, 2).reshape(CONV_SHAPE))
    p = _layout_weights(w)
    in_weights = [p["w_a"], p["w_qkv"], p["w_gate"]]
    hn, proj_a, proj_qkv, proj_gate = _proj_fwd(x1, w["mix_pre_g"], in_weights)
    cqn, ckvn, q, k, v = _mla_front_fwd(proj_a, tabs, w["mla_q_norm_g"], w["mla_kv_norm_g"], p["w_uq"], p["w_kv"])
    o_mla, lse = _attn_fwd(q, k, v)
    ab = (proj_a, SLOT, A_AB // SLOT)
    qkv_n = _gdn_conv_fwd(proj_qkv, p["conv"])
    gb, bb = _gates_fwd(ab, p["a_log"], p["dt_bias"])
    (o_gdn, keep), (ffn2, w_out) = _gdn_fwd(qkv_n, gb, bb, carry=late)
    p["w_out"] = _heads_out(_rows_of(w_out), GDN_D, axis=0)
    cat, mixed, x2 = _mix_fwd(o_mla, o_gdn, proj_gate, x1, p["g_mla_out"], p["g_gdn"], p["w_out"], w["mix_post_g"])
    (h2, y, hg2, hu2), _ = _ffn_fwd("ffn2_fwd", x2, w["ffn2_pre_g"], ffn2, 0, w["ffn2_post_g"])
    dy, loss_lanes = _loss_fwd(y, target)

    g = {}
    (dx2, xn2, dh2, a2, dhg2, dhu2, g["ffn2_pre_g"], g["ffn2_post_g"]), _ = _ffn_bwd(
        "ffn2_bwd", x2, h2, hg2, hu2, dy, w["ffn2_pre_g"], ffn2, 0, w["ffn2_post_g"])
    ffn2_grads = _Scatter([_by_device(_weight_grad("ffn2_dw_gate", dhg2, xn2, BF16, tm=1408)),
                           _by_device(_weight_grad("ffn2_dw_up", dhu2, xn2, BF16, tm=1408)),
                           _by_device(_weight_grad("ffn2_dw_down", a2, dh2, BF16, tm=1408))])
    d = {}
    dmixed, do_mla, delta, do_gdn, dgate, g["mix_post_g"], d["g_mla_out"], d["g_gdn"] = _mix_bwd(
        o_mla, o_gdn, proj_gate, mixed, dx2, p["g_mla_out"], p["g_gdn"], p["w_out"], w["mix_post_g"])
    d["w_out"] = _weight_grad("mix_out_dw", cat, dmixed)
    dq, dk, dv = _attn_bwd(q, k, v, do_mla, lse, delta)
    (dqkv_n, dgb, dbb), landed_ffn2 = _gdn_bwd(qkv_n, gb, bb, keep, do_gdn, carry=ffn2_grads)
    dab, d["a_log"], d["dt_bias"] = _gates_bwd(ab, p["a_log"], p["dt_bias"], dgb, dbb)
    dproj_qkv, d["conv"] = _gdn_conv_bwd(proj_qkv, p["conv"], dqkv_n)
    dproj_a, dq_p, dkv_p, g["mla_q_norm_g"], g["mla_kv_norm_g"] = _mla_front_bwd(
        proj_a, tabs, w["mla_q_norm_g"], w["mla_kv_norm_g"], p["w_uq"], p["w_kv"], dq, dk, dv, dab)
    d["w_uq"] = _weight_grad("mla_q_dw", dq_p, cqn)
    d["w_kv"] = _weight_grad("mla_kv_dw", dkv_p, ckvn)
    d["w_a"] = _weight_grad("proj_a_dw", dproj_a, hn, tm=640)
    d["w_qkv"] = _weight_grad("proj_qkv_dw", dproj_qkv, hn)
    d["w_gate"] = _weight_grad("proj_gate_dw", dgate, hn)
    dx1, g["mix_pre_g"] = _proj_bwd(x1, w["mix_pre_g"], in_weights, [dproj_a, dproj_qkv, dgate], dx2)
    g.update(_unlayout_grads(d))
    others = list(OTHER.values())
    (dx, xn1, dh1, a1, dhg1, dhu1, g["ffn1_pre_g"], g["ffn1_post_g"]), landed_others = _ffn_bwd(
        "ffn1_bwd", x, h1, hg1, hu1, dx1, w["ffn1_pre_g"], w["ffn1"], 0, w["ffn1_post_g"], carry=_Scatter([_by_device(g.pop(t)) for t in others]))
    landed = dict(zip(list(FFN_NAMES[3:]) + list(OTHER), list(landed_ffn2) + list(landed_others)))
    packed = _pack_small(g, g["gdn_conv_w"].reshape(-1), REDUCE_ROWS)
    packed = packed.at[REDUCE_ROWS - 1, ROW - 1].set(jnp.sum(loss_lanes))
    begun = {}
    begun["small"], token = _scatter_begin("reduce_small_begin", jnp.broadcast_to(packed, (N_DEV,) + packed.shape))
    for name, cots, acts in (("ffn1_w_down", a1, dh1), ("ffn1_w_gate", dhg1, xn1), ("ffn1_w_up", dhu1, xn1)):
        blocks = _by_device(_weight_grad(name + "_grad", cots, acts, BF16, tm=1408, after=token))
        begun[name], token = _scatter_begin("scatter_" + name + "_begin", blocks)
    return dx, g, landed, begun, token


MESH_AXES = ("x", "y", "c")
N_LINKS = N_DEV - 1


def _place():
    return tuple(lax.axis_index(a) for a in MESH_AXES)


def _block_of(dev):
    x, y, c = dev
    return 4 * x + 2 * y + c


def _remote_copy(src, dst, sems, k, to):
    send_sems, recv_sems = sems
    return pltpu.make_async_remote_copy(src_ref=src, dst_ref=dst, send_sem=send_sems.at[k], recv_sem=recv_sems.at[k],
                                        device_id=to, device_id_type=pl.DeviceIdType.MESH)


class _Exchange:
    def __init__(self, arrays):
        self.arrays = list(arrays)
        self.n = len(self.arrays)
        self.specs = [pl.BlockSpec(memory_space=pl.ANY)] * self.n
        self.scratch = [pltpu.SemaphoreType.DMA((self.n * N_LINKS,)), pltpu.SemaphoreType.DMA((self.n * N_LINKS,)),
                        pltpu.SemaphoreType.DMA((self.n,))]

    def split(self, refs):
        n = self.n
        return refs[:n], refs[n:2 * n], (refs[2 * n], refs[2 * n + 1]), refs[2 * n + 2]


class _Gather(_Exchange):
    def out_shape(self):
        return [jax.ShapeDtypeStruct((N_DEV,) + a.shape, a.dtype) for a in self.arrays]

    def _plan(self, ins, outs, sems, local_sems):
        x, y, c = _place()
        me, sibling = (x, y, c), (x, y, 1 - c)
        chips = [(1 - x, y), (x, 1 - y), (1 - x, 1 - y)]

        def copy(a, k, block, to, mine=False):
            src = ins[a] if mine else outs[a].at[_block_of(block)]
            return _remote_copy(src, outs[a].at[_block_of(block)], sems, a * N_LINKS + k, to)

        local = [pltpu.make_async_copy(ins[a], outs[a].at[_block_of(me)], local_sems.at[a]) for a in range(self.n)]
        first = []
        for a in range(self.n):
            first.append(copy(a, 0, me, sibling, mine=True))
            first += [copy(a, 1 + j, me, (*chip, c), mine=True) for j, chip in enumerate(chips)]
        return me, sibling, chips, c, copy, local, first

    def start(self, ins, outs, sems, local_sems):
        *_, local, first = self._plan(ins, outs, sems, local_sems)
        for cp in local + first:
            cp.start()

    def finish(self, ins, outs, sems, local_sems):
        me, sibling, chips, c, copy, local, first = self._plan(ins, outs, sems, local_sems)
        passed = []
        for j, chip in enumerate(chips):
            for a in range(self.n):
                copy(a, 1 + j, (*chip, c), me).wait_recv()
                passed.append(copy(a, 4 + j, (*chip, c), sibling))
                passed[-1].start()
        for a in range(self.n):
            copy(a, 0, sibling, me).wait_recv()
            for j, chip in enumerate(chips):
                copy(a, 4 + j, (*chip, 1 - c), me).wait_recv()
        for cp in first + passed:
            cp.wait_send()
        for cp in local:
            cp.wait()


class _Scatter(_Exchange):
    def out_shape(self):
        return [jax.ShapeDtypeStruct(a.shape, a.dtype) for a in self.arrays]

    def _plan(self, ins, outs, sems, local_sems):
        x, y, c = _place()
        me = _block_of((x, y, c))

        def peer(r):
            return (1 - x if r & 4 else x, 1 - y if r & 2 else y, 1 - c if r & 1 else c)

        local = [pltpu.make_async_copy(ins[a].at[me], outs[a].at[me], local_sems.at[a]) for a in range(self.n)]
        sends = [_remote_copy(ins[a].at[_block_of(peer(r))], outs[a].at[me], sems, a * N_LINKS + r - 1, peer(r))
                 for a in range(self.n) for r in range(1, N_DEV)]
        arrivals = [_remote_copy(ins[a].at[me], outs[a].at[_block_of(peer(r))], sems, a * N_LINKS + r - 1, peer(r))
                    for a in range(self.n) for r in range(1, N_DEV)]
        return local, sends, arrivals

    def start(self, ins, outs, sems, local_sems):
        local, sends, _ = self._plan(ins, outs, sems, local_sems)
        for cp in local + sends:
            cp.start()

    def finish(self, ins, outs, sems, local_sems):
        local, sends, arrivals = self._plan(ins, outs, sems, local_sems)
        for cp in arrivals:
            cp.wait_recv()
        for cp in sends:
            cp.wait_send()
        for cp in local:
            cp.wait()


def _exchange(name, plan):
    def body(*refs):
        parts = plan.split(refs)
        plan.start(*parts)
        plan.finish(*parts)

    return pl.pallas_call(
        body, name=name,
        in_specs=plan.specs,
        out_specs=plan.specs,
        out_shape=plan.out_shape(),
        scratch_shapes=plan.scratch,
    )(*plan.arrays)


def _call_carrying(body, plan, operands, *, name, grid, in_specs, out_specs, out_shape, scratch_shapes, compiler_params):
    if plan is None:
        outs = pl.pallas_call(body, name=name, grid=grid, in_specs=in_specs, out_specs=out_specs, out_shape=out_shape,
                              scratch_shapes=scratch_shapes, compiler_params=compiler_params)(*operands)
        return outs, []
    n_i, n_o, n_s, k = len(in_specs), len(out_specs), len(scratch_shapes), plan.n

    def whole(*refs):
        cut = [n_i, n_i + k, n_i + k + n_o, n_i + 2 * k + n_o, n_i + 2 * k + n_o + n_s]
        own_in, ex_in, own_out, ex_out, own_scr, ex_scr = (refs[a:b] for a, b in zip([0] + cut, cut + [len(refs)]))
        parts = plan.split(ex_in + ex_out + ex_scr)
        first = last = True
        for axis, size in enumerate(grid):
            first = first & (pl.program_id(axis) == 0)
            last = last & (pl.program_id(axis) == size - 1)

        @pl.when(first)
        def _():
            plan.start(*parts)

        body(*own_in, *own_out, *own_scr)

        @pl.when(last)
        def _():
            plan.finish(*parts)

    outs = pl.pallas_call(
        whole, name=name, grid=grid,
        in_specs=list(in_specs) + plan.specs, out_specs=list(out_specs) + plan.specs,
        out_shape=list(out_shape) + plan.out_shape(), scratch_shapes=list(scratch_shapes) + plan.scratch,
        compiler_params=compiler_params,
    )(*operands, *plan.arrays)
    return outs[:n_o], outs[n_o:]


def _row_tile(rows, target=256):
    best = rows
    for cand in range(16, min(rows, target) + 1, 16):
        if rows % cand == 0:
            best = cand
    return best


def _sum_blocks(name, blocks, after=None):
    rows, width = blocks.shape[-2:]
    tm = _row_tile(rows)

    def body(x_ref, *rest):
        acc = x_ref[0].astype(F32)
        for d in range(1, N_DEV):
            acc = acc + x_ref[d].astype(F32)
        rest[-1][...] = acc

    ordered = [] if after is None else [after]
    return pl.pallas_call(
        body, name=name,
        grid=(rows // tm,),
        in_specs=[pl.BlockSpec((N_DEV, tm, width), lambda i: (0, i, 0))] + [pl.BlockSpec(memory_space=pl.ANY)] * len(ordered),
        out_specs=pl.BlockSpec((tm, width), lambda i: (i, 0)),
        out_shape=jax.ShapeDtypeStruct((rows, width), F32),
        compiler_params=pltpu.CompilerParams(dimension_semantics=("parallel",)),
    )(blocks, *ordered)


def _split_plan(src_ref, land_ref, sems):
    x, y, c = _place()
    me = _block_of((x, y, c))

    def peer(r):
        return (1 - x if r & 4 else x, 1 - y if r & 2 else y, 1 - c if r & 1 else c)

    sends = [_remote_copy(src_ref.at[_block_of(peer(r))], land_ref.at[me], sems, r - 1, peer(r)) for r in range(1, N_DEV)]
    arrivals = [_remote_copy(src_ref.at[me], land_ref.at[_block_of(peer(r))], sems, r - 1, peer(r)) for r in range(1, N_DEV)]
    return sends, arrivals


def _scatter_begin(name, blocks):
    def body(src_ref, land_ref, send_sems, recv_sems, src_thru, land_thru, token_ref):
        for cp in _split_plan(src_ref, land_ref, (send_sems, recv_sems))[0]:
            cp.start()
        token_ref[...] = jnp.zeros_like(token_ref)

    hbm, sem = pl.BlockSpec(memory_space=pltpu.HBM), pl.BlockSpec(memory_space=pltpu.SEMAPHORE)
    zone = pltpu.HBM(blocks.shape, blocks.dtype)
    *handles, token = pl.pallas_call(
        body, name=name,
        in_specs=(hbm, hbm),
        out_specs=(sem, sem, hbm, hbm, pl.BlockSpec(memory_space=pltpu.VMEM)),
        out_shape=(pltpu.SemaphoreType.DMA((N_LINKS,)), pltpu.SemaphoreType.DMA((N_LINKS,)), zone, zone,
                   jax.ShapeDtypeStruct((8, SLOT), F32)),
        input_output_aliases={0: 2, 1: 3},
        compiler_params=pltpu.CompilerParams(has_side_effects=pltpu.SideEffectType.DATAFLOW_SIDE_EFFECTING),
    )(pltpu.with_memory_space_constraint(blocks, pltpu.HBM),
      pltpu.with_memory_space_constraint(lax.empty(blocks.shape, blocks.dtype), pltpu.HBM))
    return handles, token


def _scatter_end(name, handles, after):
    send_sems, recv_sems, src, zone = handles

    def body(src_ref, land_ref, send_sems, recv_sems, after_ref, src_dead, got_ref):
        sends, arrivals = _split_plan(src_ref, land_ref, (send_sems, recv_sems))
        for cp in arrivals:
            cp.wait_recv()
        for cp in sends:
            cp.wait_send()

    hbm, sem = pl.BlockSpec(memory_space=pltpu.HBM), pl.BlockSpec(memory_space=pltpu.SEMAPHORE)
    sent, landed = pl.pallas_call(
        body, name=name,
        in_specs=(hbm, hbm, sem, sem, pl.BlockSpec(memory_space=pl.ANY)),
        out_specs=(hbm, hbm),
        out_shape=(pltpu.HBM(src.shape, src.dtype), pltpu.HBM(zone.shape, zone.dtype)),
        input_output_aliases={0: 0, 1: 1},
        compiler_params=pltpu.CompilerParams(has_side_effects=pltpu.SideEffectType.DATAFLOW_SIDE_EFFECTING),
    )(src, zone, send_sems, recv_sems, after)
    me = _block_of(_place())
    return lax.dynamic_update_slice_in_dim(landed, lax.dynamic_slice_in_dim(sent, me, 1, axis=0), me, axis=0)


def _adamw(name, w, g, m, v):
    def fn(rows, consts):
        wv, gv, mv, vv = rows
        m2 = ADAM_B1 * mv + (1.0 - ADAM_B1) * gv
        v2 = ADAM_B2 * vv + (1.0 - ADAM_B2) * jnp.square(gv)
        m_hat = m2 / (1.0 - ADAM_B1 ** ADAM_STEP)
        v_hat = v2 / (1.0 - ADAM_B2 ** ADAM_STEP)
        return [-ADAM_LR * (m_hat / (jnp.sqrt(v_hat) + ADAM_EPS) + ADAM_WD * wv), m2, v2], []

    return _rowwise(name, fn, [w, g, m, v], [], [(w.shape[1], F32)] * 3, tm=_row_tile(w.shape[0]))


ROW = 1024
FFN_NAMES = ("ffn1_w_gate", "ffn1_w_up", "ffn1_w_down", "ffn2_w_gate", "ffn2_w_up", "ffn2_w_down")
OTHER = {"w_in": "w_in_t", "mla_w_uq": "uq_t", "mla_w_ukv": "ukv_t", "w_out": "w_out"}
BY_COLUMNS = ("ffn1_w_gate", "ffn1_w_up", "ffn2_w_gate", "ffn2_w_up", "w_in", "mla_w_uq", "mla_w_ukv")
SMALL = {
    "ffn1_pre_g": (1024, 1024), "ffn1_post_g": (1024, 1024), "mix_pre_g": (1024, 1024), "mla_q_norm_g": (256, 256),
    "mla_kv_norm_g": (128, 128), "mla_out_g": (512, 512), "gdn_a_log": (8, 128), "gdn_dt_bias": (8, 128),
    "gdn_norm_g": (64, 128), "mix_post_g": (1024, 1024), "ffn2_pre_g": (1024, 1024), "ffn2_post_g": (1024, 1024),
}
CONV_SHAPE = (GDN_CONV, 3 * N_HEADS * GDN_D)
CONV_SHARD = (GDN_CONV, CONV_SHAPE[1] // N_DEV)
CONV_LANES = CONV_SHAPE[0] * CONV_SHAPE[1]
SMALL_ROWS = 8
REDUCE_ROWS = 16


def _pack_small(vecs, conv, rows):
    parts = [_pad_lanes(vecs[n].reshape(1, -1), 0, r) for n, (_, r) in SMALL.items()]
    parts.append(conv.reshape(1, -1))
    flat = jnp.concatenate(parts, axis=1)
    return _pad_lanes(flat, 0, rows * ROW).reshape(rows, ROW)


def _unpack_small(buf):
    flat = buf.reshape(1, -1)
    out, at = {}, 0
    for n, (w, r) in SMALL.items():
        out[n] = flat[:, at:at + w]
        at += r
    return out, flat[0, at:]


def kernel(x, positions, ffn1_pre_g, ffn1_w_gate, ffn1_w_up, ffn1_w_down, ffn1_post_g, mix_pre_g, w_in, mla_q_norm_g, mla_w_uq, mla_kv_norm_g, mla_w_ukv, mla_out_g, gdn_conv_w, gdn_a_log, gdn_dt_bias, gdn_norm_g, w_out, mix_post_g, ffn2_pre_g, ffn2_w_gate, ffn2_w_up, ffn2_w_down, ffn2_post_g, loss_target, m_ffn1_pre_g, m_ffn1_w_gate, m_ffn1_w_up, m_ffn1_w_down, m_ffn1_post_g, m_mix_pre_g, m_w_in, m_mla_q_norm_g, m_mla_w_uq, m_mla_kv_norm_g, m_mla_w_ukv, m_mla_out_g, m_gdn_conv_w, m_gdn_a_log, m_gdn_dt_bias, m_gdn_norm_g, m_w_out, m_mix_post_g, m_ffn2_pre_g, m_ffn2_w_gate, m_ffn2_w_up, m_ffn2_w_down, m_ffn2_post_g, v_ffn1_pre_g, v_ffn1_w_gate, v_ffn1_w_up, v_ffn1_w_down, v_ffn1_post_g, v_mix_pre_g, v_w_in, v_mla_q_norm_g, v_mla_w_uq, v_mla_kv_norm_g, v_mla_w_ukv, v_mla_out_g, v_gdn_conv_w, v_gdn_a_log, v_gdn_dt_bias, v_gdn_norm_g, v_w_out, v_mix_post_g, v_ffn2_pre_g, v_ffn2_w_gate, v_ffn2_w_up, v_ffn2_w_down, v_ffn2_post_g):
    given = dict(locals())
    order = ["ffn1_pre_g", "ffn1_w_gate", "ffn1_w_up", "ffn1_w_down", "ffn1_post_g", "mix_pre_g", "w_in", "mla_q_norm_g",
             "mla_w_uq", "mla_kv_norm_g", "mla_w_ukv", "mla_out_g", "gdn_conv_w", "gdn_a_log", "gdn_dt_bias", "gdn_norm_g",
             "w_out", "mix_post_g", "ffn2_pre_g", "ffn2_w_gate", "ffn2_w_up", "ffn2_w_down", "ffn2_post_g"]
    assert sorted(order) == sorted(list(FFN_NAMES) + list(OTHER) + list(SMALL) + ["gdn_conv_w"])

    def drop_depth(a):
        return a[0] if a.ndim == 3 else a

    wts = {n: drop_depth(given[n]) for n in order}
    mom = {n: drop_depth(given["m_" + n]) for n in order}
    var = {n: drop_depth(given["v_" + n]) for n in order}
    me = _block_of(_place())

    def wire(n):
        return (wts[n].T if n in BY_COLUMNS else wts[n]).astype(BF16)

    (ffn1,) = _exchange("gather_first", _Gather([jnp.stack([wire(n) for n in FFN_NAMES[:3]])]))
    mid = _Gather([wire(n) for n in ("w_in", "mla_w_uq", "mla_w_ukv")] + [wts["gdn_conv_w"]])
    late = _Gather([jnp.stack([wire(n) for n in FFN_NAMES[3:]]), wire("w_out")])
    full = {n: wts[n] for n in SMALL}
    full["ffn1"] = ffn1

    dx, grads, landed, begun, token = _local_step(x[0], positions[0], loss_target[0], full, mid, late)

    grad, outs = {}, {"delta": {}, "new_m": {}, "new_v": {}}

    def finish(n, blocks, after=None):
        total = _sum_blocks("sum_" + n, blocks, after=after)
        grad[n] = total.T if n in BY_COLUMNS else total
        outs["delta"][n], outs["new_m"][n], outs["new_v"][n] = _adamw("adamw_" + n, wts[n], grad[n], mom[n], var[n])

    for n, blocks in landed.items():
        finish(n, blocks, after=token)
        token = outs["new_v"][n]
    small_handles = begun.pop("small")
    for n, handles in begun.items():
        finish(n, _scatter_end("scatter_" + n + "_end", handles, after=token))
        token = outs["new_v"][n]

    small_sum = _sum_blocks("sum_small", _scatter_end("reduce_small_end", small_handles, after=token))
    loss = small_sum[REDUCE_ROWS - 1, ROW - 1]
    small_grad, conv_grad_full = _unpack_small(small_sum)
    grad.update(small_grad)
    grad["gdn_conv_w"] = lax.dynamic_slice(conv_grad_full[:CONV_LANES].reshape(CONV_SHAPE), (0, me * CONV_SHARD[1]), CONV_SHARD)
    outs["grad"] = grad
    small = [_pack_small(s, s["gdn_conv_w"].reshape(-1), SMALL_ROWS) for s in (wts, grad, mom, var)]
    for kind, s in zip(("delta", "new_m", "new_v"), _adamw("adamw_small", *small)):
        vecs, conv = _unpack_small(s)
        outs[kind].update(vecs)
        outs[kind]["gdn_conv_w"] = conv[:CONV_SHARD[0] * CONV_SHARD[1]].reshape(CONV_SHARD)
    result = [loss, dx[None]]
    for kind in ("grad", "delta", "new_m", "new_v"):
        result += [outs[kind][n].reshape(given[n].shape) for n in order]
    return tuple(result)
```

```python
import jax
import jax.numpy as jnp
from jax import lax
from jax.experimental import pallas as pl
from jax.experimental.pallas import tpu as pltpu

F32 = jnp.float32
BF16 = jnp.bfloat16
HI = lax.Precision.HIGH

N_DEV = 8
N_HEADS = 8
SLOT = 128
MLA_Q_RANK = 256
MLA_KV_RANK = 128
MLA_NOPE = 64
MLA_ROPE = 32
MLA_V = 64
GDN_D = 64
GDN_CONV = 4
GDN_CHUNK = 64
ROPE_THETA = 10000.0
EPS = 1e-6
ADAM_LR, ADAM_B1, ADAM_B2, ADAM_EPS, ADAM_WD, ADAM_STEP = 0.001, 0.9, 0.999, 1e-08, 0.01, 10


def _dot(a, b, ca, cb, precision=None):
    lead = a.ndim - 2
    batch = tuple(range(lead))
    return lax.dot_general(a, b, (((lead + ca,), (lead + cb,)), (batch, batch)), precision=precision,
                           preferred_element_type=F32)


def _nn(a, b, precision=None):
    return _dot(a, b, 1, 0, precision)


def _nt(a, b, precision=None):
    return _dot(a, b, 1, 1, precision)


def _tn(a, b, precision=None):
    return _dot(a, b, 0, 0, precision)


def _sigmoid(x):
    return 1.0 / (1.0 + jnp.exp(-x))


def _silu(x):
    return x * _sigmoid(x)


def _rms(x, g, n):
    ms = jnp.sum(x * x, axis=-1, keepdims=True) * (1.0 / n)
    return x * lax.rsqrt(ms + EPS) * g


def _chunk_masks():
    c = GDN_CHUNK
    i = lax.broadcasted_iota(jnp.int32, (c, c), 0)
    j = lax.broadcasted_iota(jnp.int32, (c, c), 1)
    lower = i >= j
    strict = i > j
    eye = (i == j).astype(F32)
    blocks = []
    b = 1
    while b < c:
        same = (i // (2 * b)) == (j // (2 * b))
        blocks.append(same & ((i % (2 * b)) >= b) & ((j % (2 * b)) < b))
        b *= 2
    return lower, strict, eye, blocks


def _unit_lower_inverse(low, eye, blocks):
    t = eye - jnp.where(blocks[0], low, 0.0)
    for m in blocks[1:]:
        lo = jnp.where(m, low, 0.0)
        t = t - _nn(t, _nn(lo, t, HI), HI)
    return t


@jax.custom_vjp
def _known_inverse(low, tinv):
    return tinv


def _known_inverse_fwd(low, tinv):
    return tinv, tinv


def _known_inverse_bwd(tinv, dt):
    return -_tn(tinv, _nt(dt, tinv, HI), HI), jnp.zeros_like(tinv)


_known_inverse.defvjp(_known_inverse_fwd, _known_inverse_bwd)

_PRODUCTS = {"nn": _nn, "nt": _nt, "tn": _tn}


@jax.custom_vjp
def _known_nn(a, b, c):
    return c


@jax.custom_vjp
def _known_nt(a, b, c):
    return c


@jax.custom_vjp
def _known_tn(a, b, c):
    return c


def _known_fwd(a, b, c):
    return c, (a, b, c)


_known_nn.defvjp(_known_fwd, lambda r, dc: (_nt(dc, r[1], HI), _tn(r[0], dc, HI), jnp.zeros_like(r[2])))
_known_nt.defvjp(_known_fwd, lambda r, dc: (_nn(dc, r[1], HI), _tn(dc, r[0], HI), jnp.zeros_like(r[2])))
_known_tn.defvjp(_known_fwd, lambda r, dc: (_nt(r[1], dc, HI), _nn(r[0], dc, HI), jnp.zeros_like(r[2])))
_KNOWN = {"nn": _known_nn, "nt": _known_nt, "tn": _known_tn}
GDN_PRODUCTS = 8
GDN_KEPT = 2 + GDN_PRODUCTS


def _gdn_chunk(q, k, v, gc, bb, s, masks, known=None):
    lower, strict, eye, blocks = masks
    made = []

    def product(kind, a, b):
        c = _PRODUCTS[kind](a, b, HI) if known is None else _KNOWN[kind](a, b, known[1 + len(made)])
        made.append(c)
        return c

    qs = q * (GDN_D ** -0.5)
    gct = jnp.swapaxes(gc, -1, -2)
    decay = jnp.exp(jnp.where(lower, gc - gct, -1e30))
    kb = k * bb
    low = jnp.where(strict, product("nt", kb, k) * decay, 0.0)
    tinv = _unit_lower_inverse(low, eye, blocks) if known is None else _known_inverse(low, known[0])
    eg = jnp.exp(gc)
    w = product("nn", tinv, kb * eg)
    u = product("nn", tinv, v * bb)
    attn = product("nt", qs, k) * decay
    last = lax.broadcasted_iota(jnp.int32, gc.shape[-2:], 0) == GDN_CHUNK - 1
    g_end = jnp.sum(jnp.where(last, gc, 0.0), axis=-2, keepdims=True)
    k_dec = k * jnp.exp(g_end - gc)
    v_new = u - product("nn", w, s)
    o = product("nn", qs * eg, s) + product("nn", attn, v_new)
    s_new = s * jnp.exp(g_end) + product("tn", k_dec, v_new)
    assert len(made) == GDN_PRODUCTS
    return o, s_new, [tinv] + made


GDN_GROUP = 8
GDN_GROUPS = N_HEADS // GDN_GROUP


def _group_heads(ref):
    return jnp.stack([ref[:, pl.ds(j * SLOT, GDN_D)] for j in range(GDN_GROUP)])


def _ungroup_heads(ref, val):
    pad = jnp.zeros((GDN_CHUNK, SLOT - GDN_D), F32)
    for j in range(GDN_GROUP):
        ref[:, pl.ds(j * SLOT, GDN_D)] = val[j]
        ref[:, pl.ds(j * SLOT + GDN_D, SLOT - GDN_D)] = pad


def _gdn_fwd(qkv, gb, bb, carry=None):
    t = qkv.shape[0]
    n_chunks = t // GDN_CHUNK
    d = GDN_D

    def body(q_ref, k_ref, v_ref, g_ref, b_ref, o_ref, keep_ref, s_ref):
        @pl.when(pl.program_id(1) == 0)
        def _():
            s_ref[...] = jnp.zeros_like(s_ref)

        s = s_ref[...]
        keep_ref[:, 0, 0] = s
        o, s_new, made = _gdn_chunk(*[_group_heads(r) for r in (q_ref, k_ref, v_ref, g_ref, b_ref)], s, _chunk_masks())
        for i, val in enumerate(made):
            keep_ref[:, 0, 1 + i] = val
        s_ref[...] = s_new
        _ungroup_heads(o_ref, o)

    def spec(kind=0):
        return pl.BlockSpec((GDN_CHUNK, GDN_GROUP * SLOT), lambda h, n: (n, kind * GDN_GROUPS + h))

    return _call_carrying(
        body, carry, (qkv, qkv, qkv, gb, bb), name="gdn_fwd",
        grid=(GDN_GROUPS, n_chunks),
        in_specs=[spec(0), spec(1), spec(2), spec(), spec()],
        out_specs=[spec(), pl.BlockSpec((GDN_GROUP, 1, GDN_KEPT, d, d), lambda h, n: (h, n, 0, 0, 0))],
        out_shape=[jax.ShapeDtypeStruct((t, N_HEADS * SLOT), F32), jax.ShapeDtypeStruct((N_HEADS, n_chunks, GDN_KEPT, d, d), F32)],
        scratch_shapes=[pltpu.VMEM((GDN_GROUP, d, d), F32)],
        compiler_params=pltpu.CompilerParams(dimension_semantics=("arbitrary", "arbitrary")),
    )


def _gdn_bwd(qkv, gb, bb, keep, do, carry=None):
    t = qkv.shape[0]
    n_chunks = t // GDN_CHUNK
    d = GDN_D

    def body(q_ref, k_ref, v_ref, g_ref, b_ref, keep_ref, do_ref, dqkv_ref, dg_ref, db_ref, ds_ref):
        @pl.when(pl.program_id(1) == 0)
        def _():
            ds_ref[...] = jnp.zeros_like(ds_ref)

        masks = _chunk_masks()
        known = [keep_ref[:, 0, 1 + i] for i in range(GDN_KEPT - 1)]
        _, pull = jax.vjp(lambda *a: _gdn_chunk(*a, masks, known)[:2],
                          *[_group_heads(r) for r in (q_ref, k_ref, v_ref, g_ref, b_ref)], keep_ref[:, 0, 0])
        dq, dk, dv, dg, db, ds = pull((_group_heads(do_ref), ds_ref[...]))
        ds_ref[...] = ds
        for i, val in enumerate((dq, dk, dv)):
            _ungroup_heads(dqkv_ref.at[i], val)
        _ungroup_heads(dg_ref, dg)
        _ungroup_heads(db_ref, db)

    def spec(kind=0):
        return pl.BlockSpec((GDN_CHUNK, GDN_GROUP * SLOT), lambda h, n: (n_chunks - 1 - n, kind * GDN_GROUPS + h))

    return _call_carrying(
        body, carry, (qkv, qkv, qkv, gb, bb, keep, do), name="gdn_bwd",
        grid=(GDN_GROUPS, n_chunks),
        in_specs=[spec(0), spec(1), spec(2), spec(), spec(),
                  pl.BlockSpec((GDN_GROUP, 1, GDN_KEPT, d, d), lambda h, n: (h, n_chunks - 1 - n, 0, 0, 0)), spec()],
        out_specs=[pl.BlockSpec((3, GDN_CHUNK, GDN_GROUP * SLOT), lambda h, n: (0, n_chunks - 1 - n, h)), spec(), spec()],
        out_shape=[jax.ShapeDtypeStruct((3, t, N_HEADS * SLOT), F32)] + [jax.ShapeDtypeStruct((t, N_HEADS * SLOT), F32)] * 2,
        scratch_shapes=[pltpu.VMEM((GDN_GROUP, d, d), F32)],
        compiler_params=pltpu.CompilerParams(dimension_semantics=("arbitrary", "arbitrary")),
    )


def _rowwise(name, fn, rows, consts, outs, sums=(), tm=512):
    rows = [x if isinstance(x, tuple) else (x, x.shape[1], 0) for x in rows]
    t = rows[0][0].shape[0]
    tm = min(tm, t)
    steps = t // tm
    n_r, n_c, n_o, n_s = len(rows), len(consts), len(outs), len(sums)

    def window(width, block):
        return pl.BlockSpec((tm, width), lambda i: (i, block))

    def body(*refs):
        r, c = refs[:n_r], refs[n_r:n_r + n_c]
        o, s = refs[n_r + n_c:n_r + n_c + n_o], refs[n_r + n_c + n_o:]
        vals, tot = fn([x[...] for x in r], [x[...] for x in c])
        for ref, val in zip(o, vals):
            ref[...] = val.astype(ref.dtype)
        if n_s:
            @pl.when(pl.program_id(0) == 0)
            def _():
                for ref in s:
                    ref[...] = jnp.zeros_like(ref)

            for ref, val in zip(s, tot):
                ref[...] += val

    return pl.pallas_call(
        body, name=name,
        grid=(steps,),
        in_specs=[window(w, b) for _, w, b in rows] + [pl.BlockSpec(x.shape, lambda i: (0, 0)) for x in consts],
        out_specs=[pl.BlockSpec((tm, w), lambda i: (i, 0)) for w, _ in outs]
        + [pl.BlockSpec((1, w), lambda i: (0, 0)) for w in sums],
        out_shape=[jax.ShapeDtypeStruct((t, w), dt) for w, dt in outs]
        + [jax.ShapeDtypeStruct((1, w), F32) for w in sums],
        compiler_params=pltpu.CompilerParams(dimension_semantics=("arbitrary",)),
    )(*[x for x, _, _ in rows], *consts)


def _tile(dim, target):
    if dim <= target:
        return dim
    best = None
    for cand in range(128, target + 1, 128):
        if dim % cand == 0:
            best = cand
    assert best is not None, (dim, target)
    return best


def _matmul(name, a, b, mode, out_dtype=F32, tm=1024, tn=1024, tk=2048, after=None):
    if mode == "nn":
        (m, k), n = a.shape, b.shape[1]
    elif mode == "nt":
        (m, k), n = a.shape, b.shape[0]
    else:
        (k, m), n = a.shape, b.shape[1]
    tm, tn, tk = _tile(m, tm), _tile(n, tn), _tile(k, tk)
    k_steps = k // tk
    product = {"nn": _nn, "nt": _nt, "tn": _tn}[mode]

    def body(a_ref, b_ref, *rest):
        o_ref, acc_ref = rest[-2:]
        part = product(a_ref[...].astype(BF16), b_ref[...].astype(BF16))
        if k_steps == 1:
            o_ref[...] = part.astype(o_ref.dtype)
        else:
            kk = pl.program_id(2)

            @pl.when(kk == 0)
            def _():
                acc_ref[...] = part

            @pl.when(kk > 0)
            def _():
                acc_ref[...] += part

            @pl.when(kk == k_steps - 1)
            def _():
                o_ref[...] = acc_ref[...].astype(o_ref.dtype)

    a_spec = pl.BlockSpec((tk, tm), lambda i, j, kk: (kk, i)) if mode == "tn" else pl.BlockSpec((tm, tk), lambda i, j, kk: (i, kk))
    b_spec = pl.BlockSpec((tn, tk), lambda i, j, kk: (j, kk)) if mode == "nt" else pl.BlockSpec((tk, tn), lambda i, j, kk: (kk, j))
    ordered = [] if after is None else [after]
    return pl.pallas_call(
        body, name=name,
        grid=(m // tm, n // tn, k_steps),
        in_specs=[a_spec, b_spec] + [pl.BlockSpec(memory_space=pl.ANY)] * len(ordered),
        out_specs=pl.BlockSpec((tm, tn), lambda i, j, kk: (i, j)),
        out_shape=jax.ShapeDtypeStruct((m, n), out_dtype),
        scratch_shapes=[pltpu.VMEM((tm, tn) if k_steps > 1 else (8, 128), F32)],
        compiler_params=pltpu.CompilerParams(dimension_semantics=("parallel", "parallel", "arbitrary")),
    )(a, b, *ordered)


FFN_TM = 512
FFN_BWD_TM = 256
FFN_BLOCKS = 4
FFN_GATE, FFN_UP, FFN_DOWN = 0, 1, 2


def _ffn_weight_specs(ffn_w, first):
    _, _, rows, dm = ffn_w.shape

    def spec(k):
        return pl.BlockSpec((FFN_BLOCKS, None, rows, dm), lambda i, j: (j, first + k, 0, 0))

    return [spec(FFN_GATE), spec(FFN_UP), spec(FFN_DOWN)], FFN_BLOCKS * rows


def _ffn_fwd(name, x, g_pre, ffn_w, first, g_post, carry=None):
    t, dm = x.shape
    tm = min(FFN_TM, t)
    w_specs, tf = _ffn_weight_specs(ffn_w, first)
    f_steps = N_DEV // FFN_BLOCKS

    def body(x_ref, gpre_ref, wg_ref, wu_ref, wd_ref, gpost_ref, h_ref, y_ref, hg_ref, hu_ref, xn_ref, acc_ref):
        j = pl.program_id(1)

        @pl.when(j == 0)
        def _():
            xn_ref[...] = _rms(x_ref[...], gpre_ref[...], dm).astype(BF16)
            acc_ref[...] = jnp.zeros_like(acc_ref)

        xn = xn_ref[...]
        wg, wu, wd = (r[...].reshape(tf, dm) for r in (wg_ref, wu_ref, wd_ref))
        hg, hu = _nt(xn, wg), _nt(xn, wu)
        hg_ref[...] = hg.astype(BF16)
        hu_ref[...] = hu.astype(BF16)
        a = _silu(hg) * hu
        acc_ref[...] += _nn(a.astype(BF16), wd)

        @pl.when(j == f_steps - 1)
        def _():
            h = acc_ref[...]
            h_ref[...] = h
            y_ref[...] = x_ref[...] + 0.5 * _rms(h, gpost_ref[...], dm)

    row = pl.BlockSpec((tm, dm), lambda i, j: (i, 0))
    vec = pl.BlockSpec((1, dm), lambda i, j: (0, 0))
    wide = pl.BlockSpec((tm, tf), lambda i, j: (i, j))
    return _call_carrying(
        body, carry, (x, g_pre, ffn_w, ffn_w, ffn_w, g_post), name=name,
        grid=(t // tm, f_steps),
        in_specs=[row, vec, *w_specs, vec],
        out_specs=[row, row, wide, wide],
        out_shape=[jax.ShapeDtypeStruct((t, dm), F32)] * 2 + [jax.ShapeDtypeStruct((t, f_steps * tf), BF16)] * 2,
        scratch_shapes=[pltpu.VMEM((tm, dm), BF16), pltpu.VMEM((tm, dm), F32)],
        compiler_params=pltpu.CompilerParams(dimension_semantics=("arbitrary", "arbitrary")),
    )


def _ffn_bwd(name, x, h, hg, hu, dy, g_pre, ffn_w, first, g_post, carry=None):
    t, dm = x.shape
    tm = min(FFN_BWD_TM, t)
    w_specs, tf = _ffn_weight_specs(ffn_w, first)
    f_steps = N_DEV // FFN_BLOCKS
    f = f_steps * tf

    def post(hv, g):
        return 0.5 * _rms(hv, g, dm)

    def pre(xv, g):
        return _rms(xv, g, dm)

    def body(x_ref, h_ref, dy_ref, hg_ref, hu_ref, gpre_ref, wg_ref, wu_ref, wd_ref, gpost_ref,
             dx_ref, xn_ref, dh_ref, a_ref, dhg_ref, dhu_ref, dgpre_ref, dgpost_ref, acc_ref):
        i, j = pl.program_id(0), pl.program_id(1)

        @pl.when((i == 0) & (j == 0))
        def _():
            dgpre_ref[...] = jnp.zeros_like(dgpre_ref)
            dgpost_ref[...] = jnp.zeros_like(dgpost_ref)

        @pl.when(j == 0)
        def _():
            xn_ref[...] = pre(x_ref[...], gpre_ref[...]).astype(BF16)
            _, pull = jax.vjp(post, h_ref[...], gpost_ref[...])
            dh, dg = pull(dy_ref[...])
            dh_ref[...] = dh.astype(BF16)
            dgpost_ref[...] += dg
            acc_ref[...] = jnp.zeros_like(acc_ref)

        wg, wu, wd = (r[...].reshape(tf, dm) for r in (wg_ref, wu_ref, wd_ref))
        hg, hu = hg_ref[...].astype(F32), hu_ref[...].astype(F32)
        da = _nt(dh_ref[...], wd)
        sig = _sigmoid(hg)
        act = hg * sig
        dhu = (da * act).astype(BF16)
        dhg = (da * hu * (sig * (1.0 + hg * (1.0 - sig)))).astype(BF16)
        a_ref[...] = (act * hu).astype(BF16)
        dhg_ref[...] = dhg
        dhu_ref[...] = dhu
        acc_ref[...] += _nn(dhg, wg) + _nn(dhu, wu)

        @pl.when(j == f_steps - 1)
        def _():
            _, pull = jax.vjp(pre, x_ref[...], gpre_ref[...])
            dx, dg = pull(acc_ref[...])
            dx_ref[...] = dy_ref[...] + dx
            dgpre_ref[...] += dg

    row = pl.BlockSpec((tm, dm), lambda i, j: (i, 0))
    vec = pl.BlockSpec((1, dm), lambda i, j: (0, 0))
    wide = pl.BlockSpec((tm, tf), lambda i, j: (i, j))
    return _call_carrying(
        body, carry, (x, h, dy, hg, hu, g_pre, ffn_w, ffn_w, ffn_w, g_post), name=name,
        grid=(t // tm, f_steps),
        in_specs=[row, row, row, wide, wide, vec, *w_specs, vec],
        out_specs=[row, row, row, wide, wide, wide, vec, vec],
        out_shape=[jax.ShapeDtypeStruct((t, dm), F32), jax.ShapeDtypeStruct((t, dm), BF16), jax.ShapeDtypeStruct((t, dm), BF16),
                   jax.ShapeDtypeStruct((t, f), BF16), jax.ShapeDtypeStruct((t, f), BF16), jax.ShapeDtypeStruct((t, f), BF16),
                   jax.ShapeDtypeStruct((1, dm), F32), jax.ShapeDtypeStruct((1, dm), F32)],
        scratch_shapes=[pltpu.VMEM((tm, dm), F32)],
        compiler_params=pltpu.CompilerParams(dimension_semantics=("arbitrary", "arbitrary")),
    )


ATT_T = 512
ATT_GROUP = 4
ATT_GROUP_FWD = 8
ATT_SCALE = (MLA_NOPE + MLA_ROPE) ** -0.5


def _stack_slots(ref, group):
    return jnp.stack([ref[:, pl.ds(j * SLOT, SLOT)] for j in range(group)])


def _unstack_slots(ref, val):
    for j in range(val.shape[0]):
        ref[:, pl.ds(j * SLOT, SLOT)] = val[j].astype(ref.dtype)


def _scores(q, k, diagonal):
    s = _nt(q, k) * ATT_SCALE
    if diagonal:
        row = lax.broadcasted_iota(jnp.int32, s.shape[1:], 0)
        col = lax.broadcasted_iota(jnp.int32, s.shape[1:], 1)
        s = jnp.where(col <= row, s, -1e30)
    return s


def _attn_pairs(steps, q_major):
    pairs = ([(qi, ki) for qi in range(steps) for ki in range(qi + 1)] if q_major
             else [(qi, ki) for ki in range(steps) for qi in range(ki, steps)])
    return jnp.array([p[0] for p in pairs], jnp.int32), jnp.array([p[1] for p in pairs], jnp.int32)


def _attn_specs(tile, group):
    width = group * SLOT
    return (pl.BlockSpec((tile, width), lambda h, p, qt, kt: (qt[p], h)),
            pl.BlockSpec((tile, width), lambda h, p, qt, kt: (kt[p], h)))


def _attn_fwd(q, k, v):
    t = q.shape[0]
    tile = min(ATT_T, t)
    steps = t // tile
    g = ATT_GROUP_FWD

    strip = min(SLOT, tile)

    def body(qt_ref, kt_ref, q_ref, k_ref, v_ref, o_ref, lse_ref, m_ref, l_ref, alpha_ref, acc_ref, s_ref, p_ref):
        qi, ki = qt_ref[pl.program_id(1)], kt_ref[pl.program_id(1)]

        @pl.when(ki == 0)
        def _():
            m_ref[...] = jnp.full_like(m_ref, -1e30)
            l_ref[...] = jnp.zeros_like(l_ref)
            acc_ref[...] = jnp.zeros_like(acc_ref)

        def step(diagonal):
            s_ref[...] = _nt(_stack_slots(k_ref, g), _stack_slots(q_ref, g))
            for j in range(tile // strip):
                c = pl.ds(j * strip, strip)
                s = s_ref[:, :, c] * ATT_SCALE
                if diagonal:
                    key = lax.broadcasted_iota(jnp.int32, s.shape[1:], 0)
                    query = lax.broadcasted_iota(jnp.int32, s.shape[1:], 1) + j * strip
                    s = jnp.where(key <= query, s, -1e30)
                m_old = m_ref[:, :, c]
                m_new = jnp.maximum(m_old, jnp.max(s, axis=1, keepdims=True))
                p = jnp.exp(s - m_new)
                alpha = jnp.exp(m_old - m_new)
                l_ref[:, :, c] = alpha * l_ref[:, :, c] + jnp.sum(p, axis=1, keepdims=True)
                alpha_ref[:, :, c] = alpha
                m_ref[:, :, c] = m_new
                p_ref[:, :, c] = p.astype(BF16)
            acc_ref[...] = acc_ref[...] * alpha_ref[...] + _tn(_stack_slots(v_ref, g), p_ref[...])

        @pl.when(ki < qi)
        def _():
            step(False)

        @pl.when(ki == qi)
        def _():
            step(True)
            out = acc_ref[...] / l_ref[...]
            lse = jnp.broadcast_to(m_ref[...] + jnp.log(l_ref[...]), out.shape)
            for j in range(g):
                o_ref[:, pl.ds(j * SLOT, SLOT)] = out[j].T
                lse_ref[:, pl.ds(j * SLOT, SLOT)] = lse[j].T

    q_spec, k_spec = _attn_specs(tile, g)
    tables = _attn_pairs(steps, True)
    return pl.pallas_call(
        body, name="attn_fwd",
        grid_spec=pltpu.PrefetchScalarGridSpec(
            num_scalar_prefetch=2, grid=(N_HEADS // g, tables[0].shape[0]),
            in_specs=[q_spec, k_spec, k_spec], out_specs=[q_spec, q_spec],
            scratch_shapes=[pltpu.VMEM((g, 1, tile), F32), pltpu.VMEM((g, 1, tile), F32), pltpu.VMEM((g, 1, tile), F32),
                            pltpu.VMEM((g, SLOT, tile), F32), pltpu.VMEM((g, tile, tile), F32), pltpu.VMEM((g, tile, tile), BF16)]),
        out_shape=[jax.ShapeDtypeStruct((t, N_HEADS * SLOT), F32)] * 2,
        compiler_params=pltpu.CompilerParams(dimension_semantics=("parallel", "arbitrary")),
    )(*tables, q, k, v)


def _attn_grad_scores(q, k, v, do, lse_ref, delta_ref, diagonal):
    g = ATT_GROUP
    p = jnp.exp(_scores(q, k, diagonal) - _stack_slots(lse_ref, g)[:, :, 0:1])
    dp = _nt(do, v)
    return p, p * (dp - _stack_slots(delta_ref, g)[:, :, 0:1]) * ATT_SCALE


def _attn_bwd(q, k, v, do, lse, delta):
    t = q.shape[0]
    tile = min(ATT_T, t)
    steps = t // tile
    g = ATT_GROUP

    def body(qt_ref, kt_ref, q_ref, k_ref, v_ref, do_ref, lse_ref, delta_ref, dq_ref, dk_ref, dv_ref, dk_acc, dv_acc):
        qi, ki = qt_ref[pl.program_id(1)], kt_ref[pl.program_id(1)]

        @pl.when(pl.program_id(1) == 0)
        def _():
            dq_ref[...] = jnp.zeros_like(dq_ref)

        def step(diagonal):
            qq, kk = _stack_slots(q_ref, g), _stack_slots(k_ref, g)
            do_b = _stack_slots(do_ref, g).astype(BF16)
            p, ds = _attn_grad_scores(qq, kk, _stack_slots(v_ref, g), do_b, lse_ref, delta_ref, diagonal)
            ds = ds.astype(BF16)
            dv_acc[...] += _tn(p.astype(BF16), do_b)
            dk_acc[...] += _tn(ds, qq)
            dq = _nn(ds, kk)
            rows = pl.ds(pl.multiple_of(qi * tile, tile), tile)
            for j in range(g):
                dq_ref[rows, pl.ds(j * SLOT, SLOT)] += dq[j]

        @pl.when(qi == ki)
        def _():
            dk_acc[...] = jnp.zeros_like(dk_acc)
            dv_acc[...] = jnp.zeros_like(dv_acc)
            step(True)

        @pl.when(qi > ki)
        def _():
            step(False)

        @pl.when(qi == steps - 1)
        def _():
            _unstack_slots(dk_ref, dk_acc[...])
            _unstack_slots(dv_ref, dv_acc[...])

    q_spec, k_spec = _attn_specs(tile, g)
    tables = _attn_pairs(steps, False)
    return pl.pallas_call(
        body, name="attn_bwd",
        grid_spec=pltpu.PrefetchScalarGridSpec(
            num_scalar_prefetch=2, grid=(N_HEADS // g, tables[0].shape[0]),
            in_specs=[q_spec, k_spec, k_spec, q_spec, q_spec, q_spec],
            out_specs=[pl.BlockSpec((t, g * SLOT), lambda h, p, qt, kt: (0, h)), k_spec, k_spec],
            scratch_shapes=[pltpu.VMEM((g, tile, SLOT), F32), pltpu.VMEM((g, tile, SLOT), F32)]),
        out_shape=[jax.ShapeDtypeStruct((t, N_HEADS * SLOT), F32)] * 3,
        compiler_params=pltpu.CompilerParams(dimension_semantics=("parallel", "arbitrary")),
    )(*tables, q, k, v, do, lse, delta)


CONV_PAD = 8


def _fill_padded(ref, val):
    t = val.shape[0]
    zeros = jnp.zeros((CONV_PAD, val.shape[1]), val.dtype)
    ref[pl.ds(0, CONV_PAD)] = zeros
    ref[pl.ds(CONV_PAD + t, CONV_PAD)] = zeros
    ref[pl.ds(CONV_PAD, t)] = val


def _shifted(ref, s):
    return ref[pl.ds(CONV_PAD - s, ref.shape[0] - 2 * CONV_PAD)]


def _l2norm(x):
    return x * lax.rsqrt(jnp.sum(x * x, axis=-1, keepdims=True) + EPS)


def _conv_pre(x_pad, w):
    y = w[GDN_CONV - 1:GDN_CONV, :] * _shifted(x_pad, 0)
    for s in range(1, GDN_CONV):
        y = y + w[GDN_CONV - 1 - s:GDN_CONV - s, :] * _shifted(x_pad, s)
    return y


def _gdn_conv_fwd(x, w):
    t, width = x.shape

    def body(x_ref, w_ref, o_ref, x_pad):
        _fill_padded(x_pad, x_ref[...])
        act = _silu(_conv_pre(x_pad, w_ref[...]))
        normed = pl.program_id(0) < 2 * N_HEADS
        o_ref[...] = jnp.where(normed, _l2norm(act), act)

    return pl.pallas_call(
        body, name="gdn_conv_fwd",
        grid=(width // SLOT,),
        in_specs=[pl.BlockSpec((t, SLOT), lambda j: (0, j)), pl.BlockSpec((GDN_CONV, SLOT), lambda j: (0, j))],
        out_specs=pl.BlockSpec((t, SLOT), lambda j: (0, j)),
        out_shape=jax.ShapeDtypeStruct((t, width), F32),
        scratch_shapes=[pltpu.VMEM((t + 2 * CONV_PAD, SLOT), F32)],
        compiler_params=pltpu.CompilerParams(dimension_semantics=("parallel",)),
    )(x, w)


def _gdn_conv_bwd(x, w, dout):
    t, width = x.shape

    def body(x_ref, w_ref, do_ref, dx_ref, dw_ref, x_pad, dy_pad):
        wv = w_ref[...]
        _fill_padded(x_pad, x_ref[...])
        y = _conv_pre(x_pad, wv)
        sig = _sigmoid(y)
        act = y * sig
        _, pull = jax.vjp(_l2norm, act)
        normed = pl.program_id(0) < 2 * N_HEADS
        dact = jnp.where(normed, pull(do_ref[0])[0], do_ref[0])
        dy = dact * (sig * (1.0 + y * (1.0 - sig)))
        _fill_padded(dy_pad, dy)
        dx = wv[GDN_CONV - 1:GDN_CONV, :] * dy
        for s in range(1, GDN_CONV):
            dx = dx + wv[GDN_CONV - 1 - s:GDN_CONV - s, :] * _shifted(dy_pad, -s)
        dx_ref[...] = dx.astype(BF16)
        for s in range(GDN_CONV):
            dw_ref[GDN_CONV - 1 - s:GDN_CONV - s, :] = jnp.sum(dy * _shifted(x_pad, s), axis=0, keepdims=True)

    col = pl.BlockSpec((t, SLOT), lambda j: (0, j))
    tap = pl.BlockSpec((GDN_CONV, SLOT), lambda j: (0, j))
    return pl.pallas_call(
        body, name="gdn_conv_bwd",
        grid=(width // SLOT,),
        in_specs=[col, tap, pl.BlockSpec((1, t, SLOT), lambda j: (j // N_HEADS, 0, j % N_HEADS))],
        out_specs=[col, tap],
        out_shape=[jax.ShapeDtypeStruct((t, width), BF16), jax.ShapeDtypeStruct((GDN_CONV, width), F32)],
        scratch_shapes=[pltpu.VMEM((t + 2 * CONV_PAD, SLOT), F32)] * 2,
        compiler_params=pltpu.CompilerParams(dimension_semantics=("parallel",)),
    )(x, w, dout)


def _softplus(x):
    e = jnp.exp(-jnp.abs(x))
    u = 1.0 + e
    log1p = jnp.where(u == 1.0, e, jnp.log(u) * e / jnp.where(u == 1.0, 1.0, u - 1.0))
    return jnp.maximum(x, 0.0) + log1p


def _chunk_running_sum(x, reverse=False):
    tm = x.shape[0]
    at = lax.broadcasted_iota(jnp.int32, x.shape, 0) % GDN_CHUNK
    step = 1
    while step < GDN_CHUNK:
        if reverse:
            x = x + jnp.where(at < GDN_CHUNK - step, pltpu.roll(x, tm - step, 0), 0.0)
        else:
            x = x + jnp.where(at >= step, pltpu.roll(x, step, 0), 0.0)
        step *= 2
    return x


def _gates_fwd(ab, a_log, dt_bias):
    def fn(rows, consts):
        (abv,), (alog, dtb) = rows, consts
        g = _chunk_running_sum(-jnp.exp(alog) * _softplus(abv + dtb))
        beta = _sigmoid(abv)
        shape = (abv.shape[0], SLOT)
        g_slots = [jnp.broadcast_to(g[:, h:h + 1], shape) for h in range(N_HEADS)]
        b_slots = [jnp.broadcast_to(beta[:, N_HEADS + h:N_HEADS + h + 1], shape) for h in range(N_HEADS)]
        return [jnp.concatenate(g_slots, axis=1), jnp.concatenate(b_slots, axis=1)], []

    width = N_HEADS * SLOT
    return _rowwise("gdn_gates_fwd", fn, [ab], [a_log, dt_bias], [(width, F32), (width, F32)])


def _gates_bwd(ab, a_log, dt_bias, dg, dbeta):
    def fn(rows, consts):
        (abv, dgv, dbv), (alog, dtb) = rows, consts
        lane = lax.broadcasted_iota(jnp.int32, abv.shape, 1)
        dg_tok = jnp.zeros_like(abv)
        db_tok = jnp.zeros_like(abv)
        for h in range(N_HEADS):
            dg_tok = dg_tok + jnp.where(lane == h, jnp.sum(dgv[:, h * SLOT:(h + 1) * SLOT], axis=1, keepdims=True), 0.0)
            db_tok = db_tok + jnp.where(lane == N_HEADS + h, jnp.sum(dbv[:, h * SLOT:(h + 1) * SLOT], axis=1, keepdims=True), 0.0)
        dg_tok = _chunk_running_sum(dg_tok, reverse=True)
        xa = abv + dtb
        g = -jnp.exp(alog) * _softplus(xa)
        da = dg_tok * (-jnp.exp(alog)) * _sigmoid(xa)
        beta = _sigmoid(abv)
        dab = jnp.where(lane < N_HEADS, da, db_tok * beta * (1.0 - beta))
        dab = jnp.where(lane < 2 * N_HEADS, dab, 0.0)
        d_alog = jnp.sum(jnp.where(lane < N_HEADS, dg_tok * g, 0.0), axis=0, keepdims=True)
        d_dtb = jnp.sum(jnp.where(lane < N_HEADS, da, 0.0), axis=0, keepdims=True)
        return [dab], [d_alog, d_dtb]

    return _rowwise("gdn_gates_bwd", fn, [ab, dg, dbeta], [a_log, dt_bias], [(SLOT, F32)], sums=[SLOT, SLOT])


ROPE_HALF = MLA_ROPE // 2


def _rope_freqs():
    freqs = ROPE_THETA ** (-jnp.arange(ROPE_HALF, dtype=F32) / ROPE_HALF)
    return _pad_lanes(jnp.concatenate([freqs, freqs])[None, :], MLA_NOPE)


def _rope_tables(pos, freqs):
    ang = pos * freqs
    cos, sin = jnp.cos(ang), jnp.sin(ang)
    lane = lax.broadcasted_iota(jnp.int32, ang.shape, 1)
    low = (lane >= MLA_NOPE) & (lane < MLA_NOPE + ROPE_HALF)
    high = (lane >= MLA_NOPE + ROPE_HALF) & (lane < MLA_NOPE + MLA_ROPE)
    same = jnp.where(lane < MLA_NOPE, 1.0, jnp.where(low | high, cos, 0.0))
    return same, jnp.where(high, sin, 0.0), jnp.where(low, -sin, 0.0)


def _rope(x, tabs):
    same, from_low, from_high = tabs
    width = x.shape[1]
    return x * same + pltpu.roll(x, ROPE_HALF, 1) * from_low + pltpu.roll(x, width - ROPE_HALF, 1) * from_high


def _rope_transposed(dy, tabs):
    same, from_low, from_high = tabs
    width = dy.shape[1]
    return dy * same + pltpu.roll(dy * from_low, width - ROPE_HALF, 1) + pltpu.roll(dy * from_high, ROPE_HALF, 1)


def _tile_slots(tab):
    return jnp.concatenate([tab] * N_HEADS, axis=1)


A_WIDTH = MLA_Q_RANK + MLA_KV_RANK + 2 * SLOT
A_KPE = MLA_Q_RANK + MLA_KV_RANK
A_AB = A_KPE + SLOT
WIDE = N_HEADS * SLOT


def _mla_front_fwd(proj_a, tabs, g_q, g_kv, w_uq, w_kv):
    def fn(rows, consts):
        pa, pos = rows
        gq, gkv, wuq, wkv, freqs = consts
        tb = _rope_tables(pos, freqs)
        cqn = _rms(pa[:, :MLA_Q_RANK], gq, MLA_Q_RANK).astype(BF16)
        ckvn = _rms(pa[:, MLA_Q_RANK:A_KPE], gkv, MLA_KV_RANK).astype(BF16)
        kv = _nt(ckvn, wkv)
        q = _rope(_nt(cqn, wuq), [_tile_slots(x) for x in tb])
        k = kv[:, :WIDE] + _tile_slots(_rope(pa[:, A_KPE:A_AB], tb))
        return [cqn, ckvn, q, k, kv[:, WIDE:]], []

    return _rowwise("mla_front_fwd", fn, [proj_a, tabs[0]], [g_q, g_kv, w_uq, w_kv, tabs[1]],
                    [(MLA_Q_RANK, BF16), (MLA_KV_RANK, BF16)] + [(WIDE, BF16)] * 3)


def _mla_front_bwd(proj_a, tabs, g_q, g_kv, w_uq, w_kv, dq, dk, dv, dab):
    def fn(rows, consts):
        pa, pos, dqv, dkv, dvv, da = rows
        gq, gkv, wuq, wkv, freqs = consts
        tb = _rope_tables(pos, freqs)
        dq_p = _rope_transposed(dqv, [_tile_slots(x) for x in tb]).astype(BF16)
        dkv_p = jnp.concatenate([dkv, dvv], axis=1).astype(BF16)
        dkpe = dkv[:, :SLOT]
        for h in range(1, N_HEADS):
            dkpe = dkpe + dkv[:, h * SLOT:(h + 1) * SLOT]
        _, pull_q = jax.vjp(lambda x, g: _rms(x, g, MLA_Q_RANK), pa[:, :MLA_Q_RANK], gq)
        _, pull_kv = jax.vjp(lambda x, g: _rms(x, g, MLA_KV_RANK), pa[:, MLA_Q_RANK:A_KPE], gkv)
        dcq, dgq = pull_q(_nn(dq_p, wuq))
        dckv, dgkv = pull_kv(_nn(dkv_p, wkv))
        return [jnp.concatenate([dcq, dckv, _rope_transposed(dkpe, tb), da], axis=1), dq_p, dkv_p], [dgq, dgkv]

    return _rowwise("mla_front_bwd", fn, [proj_a, tabs[0], dq, dk, dv, dab], [g_q, g_kv, w_uq, w_kv, tabs[1]],
                    [(A_WIDTH, BF16), (WIDE, BF16), (2 * WIDE, BF16)], sums=[MLA_Q_RANK, MLA_KV_RANK])


def _slot_sum(x):
    parts = [jnp.broadcast_to(jnp.sum(x[:, h * SLOT:(h + 1) * SLOT], axis=1, keepdims=True), (x.shape[0], SLOT))
             for h in range(N_HEADS)]
    return jnp.concatenate(parts, axis=1)


def _mix_join(o_mla, o_gdn, gate, g_mla, g_gdn):
    mla = _rms(o_mla, g_mla, N_HEADS * MLA_V)
    gdn = o_gdn * lax.rsqrt(_slot_sum(o_gdn * o_gdn) * (1.0 / GDN_D) + EPS) * g_gdn * _silu(gate)
    return mla, gdn


MIX_TM = 256


def _mix_fwd(o_mla, o_gdn, gate, x, g_mla, g_gdn, w_out, g_post):
    dm = x.shape[1]

    def fn(rows, consts):
        om, og, gt, xv = rows
        gm, gg, wo, gp = consts
        cat = jnp.concatenate(_mix_join(om, og, gt, gm, gg), axis=1).astype(BF16)
        mixed = _nn(cat, wo)
        return [cat, mixed, xv + _rms(mixed, gp, dm)], []

    return _rowwise("mix_fwd", fn, [o_mla, o_gdn, gate, x], [g_mla, g_gdn, w_out, g_post],
                    [(2 * WIDE, BF16), (dm, F32), (dm, F32)], tm=MIX_TM)


def _mix_bwd(o_mla, o_gdn, gate, mixed, dy, g_mla, g_gdn, w_out, g_post):
    dm = mixed.shape[1]

    def fn(rows, consts):
        om, og, gt, mx, dyv = rows
        gm, gg, wo, gp = consts
        _, pull_post = jax.vjp(lambda hv, gv: _rms(hv, gv, dm), mx, gp)
        dmixed, dgp = pull_post(dyv)
        dmixed = dmixed.astype(BF16)
        dc = _nt(dmixed, wo)
        _, pull = jax.vjp(lambda x, g: _rms(x, g, N_HEADS * MLA_V), om, gm)
        dom, dgm = pull(dc[:, :WIDE])
        dn_out = dc[:, WIDE:]
        r = lax.rsqrt(_slot_sum(og * og) * (1.0 / GDN_D) + EPS)
        sig = _sigmoid(gt)
        normed = og * r
        dn = dn_out * gg * (gt * sig)
        dog = r * dn - normed * (r * r) * _slot_sum(dn * og) * (1.0 / GDN_D)
        dgt = dn_out * normed * gg * (sig * (1.0 + gt * (1.0 - sig)))
        dgg = jnp.sum(dn_out * normed * (gt * sig), axis=0, keepdims=True)
        return [dmixed, dom, _slot_sum(dom * om), dog, dgt], [dgp, dgm, dgg]

    return _rowwise("mix_bwd", fn, [o_mla, o_gdn, gate, mixed, dy], [g_mla, g_gdn, w_out, g_post],
                    [(dm, BF16), (WIDE, F32), (WIDE, F32), (WIDE, F32), (WIDE, BF16)], sums=[dm, WIDE, WIDE], tm=MIX_TM)


def _proj_fwd(x, g, weights):
    dm = x.shape[1]

    def fn(rows, consts):
        hn = _rms(rows[0], consts[0], dm).astype(BF16)
        return [hn] + [_nt(hn, wv) for wv in consts[1:]], []

    return _rowwise("proj_fwd", fn, [x], [g, *weights], [(dm, BF16)] + [(wv.shape[0], F32) for wv in weights], tm=MIX_TM)


def _proj_bwd(x, g, weights, cots, dy):
    dm = x.shape[1]
    n = len(weights)

    def fn(rows, consts):
        xv, dyv, *parts = rows
        dn = _nn(parts[0], consts[1])
        for p, wv in zip(parts[1:], consts[2:]):
            dn = dn + _nn(p, wv)
        _, pull = jax.vjp(lambda a, gv: _rms(a, gv, dm), xv, consts[0])
        dx, dg = pull(dn)
        return [dyv + dx], [dg]

    assert len(cots) == n
    return _rowwise("proj_bwd", fn, [x, dy, *cots], [g, *weights], [(dm, F32)], sums=[dm], tm=MIX_TM)


def _loss_fwd(y, target):
    dm = y.shape[1]

    def fn(rows, consts):
        err = rows[0] - rows[1]
        sq = err * err
        lanes = sq[:, :SLOT]
        for j in range(1, dm // SLOT):
            lanes = lanes + sq[:, j * SLOT:(j + 1) * SLOT]
        return [err * (1.0 / dm)], [jnp.sum(lanes, axis=0, keepdims=True) * (0.5 / dm)]

    return _rowwise("loss", fn, [y, target], [], [(dm, F32)], sums=[SLOT])


W_IN_CUTS = (0, 256, 384, 416, 1952, 1960, 1968, 2480)


def _heads_out(w, per_head, axis=-1):
    axis = axis % w.ndim
    shape = w.shape
    n = shape[axis] // per_head
    w = w.reshape(shape[:axis] + (n, per_head) + shape[axis + 1:])
    pad = [(0, 0)] * w.ndim
    pad[axis + 1] = (0, SLOT - per_head)
    return jnp.pad(w, pad).reshape(shape[:axis] + (n * SLOT,) + shape[axis + 1:])


def _heads_in(w, per_head, axis=-1):
    axis = axis % w.ndim
    shape = w.shape
    n = shape[axis] // SLOT
    w = w.reshape(shape[:axis] + (n, SLOT) + shape[axis + 1:])
    w = lax.slice_in_dim(w, 0, per_head, axis=axis + 1)
    return w.reshape(shape[:axis] + (n * per_head,) + shape[axis + 1:])


def _pad_lanes(v, lo, width=SLOT):
    return jnp.pad(v, [(0, 0)] * (v.ndim - 1) + [(lo, width - lo - v.shape[-1])])


def _pad_rows(v, lo, rows=SLOT):
    return jnp.pad(v, [(lo, rows - lo - v.shape[0])] + [(0, 0)] * (v.ndim - 1))


def _layout_weights(w):
    c = W_IN_CUTS
    w_in = w["w_in_t"]
    p = {}
    p["w_a"] = jnp.concatenate([w_in[c[0]:c[2]], _pad_rows(w_in[c[2]:c[3]], MLA_NOPE), _pad_rows(w_in[c[4]:c[6]], 0)], axis=0)
    p["w_qkv"] = _heads_out(w_in[c[3]:c[4]], GDN_D, axis=0)
    p["w_gate"] = _heads_out(w_in[c[6]:c[7]], GDN_D, axis=0)
    p["w_uq"] = _heads_out(w["uq_t"], MLA_NOPE + MLA_ROPE, axis=0)
    ukv = w["ukv_t"].reshape(N_HEADS, MLA_NOPE + MLA_V, MLA_KV_RANK)
    p["w_kv"] = jnp.concatenate([_heads_out(ukv[:, :MLA_NOPE].reshape(-1, MLA_KV_RANK), MLA_NOPE, axis=0),
                                 _heads_out(ukv[:, MLA_NOPE:].reshape(-1, MLA_KV_RANK), MLA_V, axis=0)], axis=0)
    p["conv"] = _heads_out(w["gdn_conv_w"], GDN_D)
    p["g_mla_out"] = _heads_out(w["mla_out_g"], MLA_V)
    p["g_gdn"] = jnp.tile(_pad_lanes(w["gdn_norm_g"], 0), (1, N_HEADS))
    p["a_log"] = _pad_lanes(w["gdn_a_log"], 0)
    p["dt_bias"] = _pad_lanes(w["gdn_dt_bias"], 0)
    return p


def _unlayout_grads(d):
    c = W_IN_CUTS
    g = {}
    da = d["w_a"]
    kpe0 = A_KPE + MLA_NOPE
    g["w_in_t"] = jnp.concatenate([da[:A_KPE], da[kpe0:kpe0 + MLA_ROPE], _heads_in(d["w_qkv"], GDN_D, axis=0),
                                   da[A_AB:A_AB + 2 * N_HEADS], _heads_in(d["w_gate"], GDN_D, axis=0)], axis=0)
    assert g["w_in_t"].shape[0] == c[-1]
    g["uq_t"] = _heads_in(d["w_uq"], MLA_NOPE + MLA_ROPE, axis=0)
    dk = _heads_in(d["w_kv"][:WIDE], MLA_NOPE, axis=0).reshape(N_HEADS, MLA_NOPE, MLA_KV_RANK)
    dv = _heads_in(d["w_kv"][WIDE:], MLA_V, axis=0).reshape(N_HEADS, MLA_V, MLA_KV_RANK)
    g["ukv_t"] = jnp.concatenate([dk, dv], axis=1).reshape(-1, MLA_KV_RANK)
    g["w_out"] = _heads_in(d["w_out"], GDN_D, axis=0)
    g["gdn_conv_w"] = _heads_in(d["conv"], GDN_D)
    g["mla_out_g"] = _heads_in(d["g_mla_out"], MLA_V)
    g["gdn_norm_g"] = jnp.sum(d["g_gdn"].reshape(N_HEADS, SLOT), axis=0, keepdims=True)[:, :GDN_D]
    g["gdn_a_log"] = d["a_log"][:, :N_HEADS]
    g["gdn_dt_bias"] = d["dt_bias"][:, :N_HEADS]
    return g


def _weight_grad(name, cots, acts, out_dtype=F32, tm=1024, tn=1024, tk=2048, after=None):
    return _matmul(name, cots, acts, "tn", out_dtype=out_dtype, tm=tm, tn=tn, tk=tk, after=after)


def _by_device(a):
    return a.astype(BF16).reshape((N_DEV, a.shape[0] // N_DEV) + a.shape[1:])


def _rows_of(blocks):
    return blocks.reshape((-1,) + blocks.shape[2:])


def _local_step(x, positions, target, w, mid, late):
    tabs = (positions.astype(F32)[:, None], _rope_freqs())

    (h1, x1, hg1, hu1), gathered = _ffn_fwd("ffn1_fwd", x, w["ffn1_pre_g"], w["ffn1"], 0, w["ffn1_post_g"], carry=mid)
    w = dict(w, w_in_t=_rows_of(gathered[0]), uq_t=_rows_of(gathered[1]), ukv_t=_rows_of(gathered[2]),
             gdn_conv_w=gathered[3].transpose(1, 0, 2).reshape(CONV_SHAPE))
    p = _layout_weights(w)
    in_weights = [p["w_a"], p["w_qkv"], p["w_gate"]]
    hn, proj_a, proj_qkv, proj_gate = _proj_fwd(x1, w["mix_pre_g"], in_weights)
    cqn, ckvn, q, k, v = _mla_front_fwd(proj_a, tabs, w["mla_q_norm_g"], w["mla_kv_norm_g"], p["w_uq"], p["w_kv"])
    o_mla, lse = _attn_fwd(q, k, v)
    ab = (proj_a, SLOT, A_AB // SLOT)
    qkv_n = _gdn_conv_fwd(proj_qkv, p["conv"])
    gb, bb = _gates_fwd(ab, p["a_log"], p["dt_bias"])
    (o_gdn, keep), (ffn2, w_out) = _gdn_fwd(qkv_n, gb, bb, carry=late)
    p["w_out"] = _heads_out(_rows_of(w_out), GDN_D, axis=0)
    cat, mixed, x2 = _mix_fwd(o_mla, o_gdn, proj_gate, x1, p["g_mla_out"], p["g_gdn"], p["w_out"], w["mix_post_g"])
    (h2, y, hg2, hu2), _ = _ffn_fwd("ffn2_fwd", x2, w["ffn2_pre_g"], ffn2, 0, w["ffn2_post_g"])
    dy, loss_lanes = _loss_fwd(y, target)

    g = {}
    (dx2, xn2, dh2, a2, dhg2, dhu2, g["ffn2_pre_g"], g["ffn2_post_g"]), _ = _ffn_bwd(
        "ffn2_bwd", x2, h2, hg2, hu2, dy, w["ffn2_pre_g"], ffn2, 0, w["ffn2_post_g"])
    ffn2_grads = _Scatter([_by_device(_weight_grad("ffn2_dw_gate", dhg2, xn2, BF16, tm=1408)),
                           _by_device(_weight_grad("ffn2_dw_up", dhu2, xn2, BF16, tm=1408)),
                           _by_device(_weight_grad("ffn2_dw_down", a2, dh2, BF16, tm=1408))])
    d = {}
    dmixed, do_mla, delta, do_gdn, dgate, g["mix_post_g"], d["g_mla_out"], d["g_gdn"] = _mix_bwd(
        o_mla, o_gdn, proj_gate, mixed, dx2, p["g_mla_out"], p["g_gdn"], p["w_out"], w["mix_post_g"])
    d["w_out"] = _weight_grad("mix_out_dw", cat, dmixed)
    dq, dk, dv = _attn_bwd(q, k, v, do_mla, lse, delta)
    (dqkv_n, dgb, dbb), landed_ffn2 = _gdn_bwd(qkv_n, gb, bb, keep, do_gdn, carry=ffn2_grads)
    dab, d["a_log"], d["dt_bias"] = _gates_bwd(ab, p["a_log"], p["dt_bias"], dgb, dbb)
    dproj_qkv, d["conv"] = _gdn_conv_bwd(proj_qkv, p["conv"], dqkv_n)
    dproj_a, dq_p, dkv_p, g["mla_q_norm_g"], g["mla_kv_norm_g"] = _mla_front_bwd(
        proj_a, tabs, w["mla_q_norm_g"], w["mla_kv_norm_g"], p["w_uq"], p["w_kv"], dq, dk, dv, dab)
    d["w_uq"] = _weight_grad("mla_q_dw", dq_p, cqn)
    d["w_kv"] = _weight_grad("mla_kv_dw", dkv_p, ckvn)
    d["w_a"] = _weight_grad("proj_a_dw", dproj_a, hn, tm=640)
    d["w_qkv"] = _weight_grad("proj_qkv_dw", dproj_qkv, hn)
    d["w_gate"] = _weight_grad("proj_gate_dw", dgate, hn)
    dx1, g["mix_pre_g"] = _proj_bwd(x1, w["mix_pre_g"], in_weights, [dproj_a, dproj_qkv, dgate], dx2)
    g.update(_unlayout_grads(d))
    others = list(OTHER.values())
    (dx, xn1, dh1, a1, dhg1, dhu1, g["ffn1_pre_g"], g["ffn1_post_g"]), landed_others = _ffn_bwd(
        "ffn1_bwd", x, h1, hg1, hu1, dx1, w["ffn1_pre_g"], w["ffn1"], 0, w["ffn1_post_g"], carry=_Scatter([_by_device(g.pop(t)) for t in others]))
    landed = dict(zip(list(FFN_NAMES[3:]) + list(OTHER), list(landed_ffn2) + list(landed_others)))
    packed = _pack_small(g, g["gdn_conv_w"].reshape(-1), REDUCE_ROWS)
    packed = packed.at[REDUCE_ROWS - 1, ROW - 1].set(jnp.sum(loss_lanes))
    begun = {}
    begun["small"], token = _scatter_begin("reduce_small_begin", jnp.broadcast_to(packed, (N_DEV,) + packed.shape))
    for name, cots, acts in (("ffn1_w_down", a1, dh1), ("ffn1_w_gate", dhg1, xn1), ("ffn1_w_up", dhu1, xn1)):
        blocks = _by_device(_weight_grad(name + "_grad", cots, acts, BF16, tm=1408, after=token))
        begun[name], token = _scatter_begin("scatter_" + name + "_begin", blocks)
    return dx, g, landed, begun, token


MESH_AXES = ("x", "y", "c")
N_LINKS = N_DEV - 1


def _place():
    return tuple(lax.axis_index(a) for a in MESH_AXES)


def _block_of(dev):
    x, y, c = dev
    return 4 * x + 2 * y + c


def _remote_copy(src, dst, sems, k, to):
    send_sems, recv_sems = sems
    return pltpu.make_async_remote_copy(src_ref=src, dst_ref=dst, send_sem=send_sems.at[k], recv_sem=recv_sems.at[k],
                                        device_id=to, device_id_type=pl.DeviceIdType.MESH)


class _Exchange:
    def __init__(self, arrays):
        self.arrays = list(arrays)
        self.n = len(self.arrays)
        self.specs = [pl.BlockSpec(memory_space=pl.ANY)] * self.n
        self.scratch = [pltpu.SemaphoreType.DMA((self.n * N_LINKS,)), pltpu.SemaphoreType.DMA((self.n * N_LINKS,)),
                        pltpu.SemaphoreType.DMA((self.n,))]

    def split(self, refs):
        n = self.n
        return refs[:n], refs[n:2 * n], (refs[2 * n], refs[2 * n + 1]), refs[2 * n + 2]


class _Gather(_Exchange):
    def out_shape(self):
        return [jax.ShapeDtypeStruct((N_DEV,) + a.shape, a.dtype) for a in self.arrays]

    def _plan(self, ins, outs, sems, local_sems):
        x, y, c = _place()
        me, sibling = (x, y, c), (x, y, 1 - c)
        chips = [(1 - x, y), (x, 1 - y), (1 - x, 1 - y)]

        def copy(a, k, block, to, mine=False):
            src = ins[a] if mine else outs[a].at[_block_of(block)]
            return _remote_copy(src, outs[a].at[_block_of(block)], sems, a * N_LINKS + k, to)

        local = [pltpu.make_async_copy(ins[a], outs[a].at[_block_of(me)], local_sems.at[a]) for a in range(self.n)]
        first = []
        for a in range(self.n):
            first.append(copy(a, 0, me, sibling, mine=True))
            first += [copy(a, 1 + j, me, (*chip, c), mine=True) for j, chip in enumerate(chips)]
        return me, sibling, chips, c, copy, local, first

    def start(self, ins, outs, sems, local_sems):
        *_, local, first = self._plan(ins, outs, sems, local_sems)
        for cp in local + first:
            cp.start()

    def finish(self, ins, outs, sems, local_sems):
        me, sibling, chips, c, copy, local, first = self._plan(ins, outs, sems, local_sems)
        passed = []
        for j, chip in enumerate(chips):
            for a in range(self.n):
                copy(a, 1 + j, (*chip, c), me).wait_recv()
                passed.append(copy(a, 4 + j, (*chip, c), sibling))
                passed[-1].start()
        for a in range(self.n):
            copy(a, 0, sibling, me).wait_recv()
            for j, chip in enumerate(chips):
                copy(a, 4 + j, (*chip, 1 - c), me).wait_recv()
        for cp in first + passed:
            cp.wait_send()
        for cp in local:
            cp.wait()


class _Scatter(_Exchange):
    def out_shape(self):
        return [jax.ShapeDtypeStruct(a.shape, a.dtype) for a in self.arrays]

    def _plan(self, ins, outs, sems, local_sems):
        x, y, c = _place()
        me = _block_of((x, y, c))

        def peer(r):
            return (1 - x if r & 4 else x, 1 - y if r & 2 else y, 1 - c if r & 1 else c)

        local = [pltpu.make_async_copy(ins[a].at[me], outs[a].at[me], local_sems.at[a]) for a in range(self.n)]
        sends = [_remote_copy(ins[a].at[_block_of(peer(r))], outs[a].at[me], sems, a * N_LINKS + r - 1, peer(r))
                 for a in range(self.n) for r in range(1, N_DEV)]
        arrivals = [_remote_copy(ins[a].at[me], outs[a].at[_block_of(peer(r))], sems, a * N_LINKS + r - 1, peer(r))
                    for a in range(self.n) for r in range(1, N_DEV)]
        return local, sends, arrivals

    def start(self, ins, outs, sems, local_sems):
        local, sends, _ = self._plan(ins, outs, sems, local_sems)
        for cp in local + sends:
            cp.start()

    def finish(self, ins, outs, sems, local_sems):
        local, sends, arrivals = self._plan(ins, outs, sems, local_sems)
        for cp in arrivals:
            cp.wait_recv()
        for cp in sends:
            cp.wait_send()
        for cp in local:
            cp.wait()


def _exchange(name, plan):
    def body(*refs):
        parts = plan.split(refs)
        plan.start(*parts)
        plan.finish(*parts)

    return pl.pallas_call(
        body, name=name,
        in_specs=plan.specs,
        out_specs=plan.specs,
        out_shape=plan.out_shape(),
        scratch_shapes=plan.scratch,
    )(*plan.arrays)


def _call_carrying(body, plan, operands, *, name, grid, in_specs, out_specs, out_shape, scratch_shapes, compiler_params):
    if plan is None:
        outs = pl.pallas_call(body, name=name, grid=grid, in_specs=in_specs, out_specs=out_specs, out_shape=out_shape,
                              scratch_shapes=scratch_shapes, compiler_params=compiler_params)(*operands)
        return outs, []
    n_i, n_o, n_s, k = len(in_specs), len(out_specs), len(scratch_shapes), plan.n

    def whole(*refs):
        cut = [n_i, n_i + k, n_i + k + n_o, n_i + 2 * k + n_o, n_i + 2 * k + n_o + n_s]
        own_in, ex_in, own_out, ex_out, own_scr, ex_scr = (refs[a:b] for a, b in zip([0] + cut, cut + [len(refs)]))
        parts = plan.split(ex_in + ex_out + ex_scr)
        first = last = True
        for axis, size in enumerate(grid):
            first = first & (pl.program_id(axis) == 0)
            last = last & (pl.program_id(axis) == size - 1)

        @pl.when(first)
        def _():
            plan.start(*parts)

        body(*own_in, *own_out, *own_scr)

        @pl.when(last)
        def _():
            plan.finish(*parts)

    outs = pl.pallas_call(
        whole, name=name, grid=grid,
        in_specs=list(in_specs) + plan.specs, out_specs=list(out_specs) + plan.specs,
        out_shape=list(out_shape) + plan.out_shape(), scratch_shapes=list(scratch_shapes) + plan.scratch,
        compiler_params=compiler_params,
    )(*operands, *plan.arrays)
    return outs[:n_o], outs[n_o:]


def _row_tile(rows, target=256):
    best = rows
    for cand in range(16, min(rows, target) + 1, 16):
        if rows % cand == 0:
            best = cand
    return best


def _sum_blocks(name, blocks, after=None):
    rows, width = blocks.shape[-2:]
    tm = _row_tile(rows)

    def body(x_ref, *rest):
        acc = x_ref[0].astype(F32)
        for d in range(1, N_DEV):
            acc = acc + x_ref[d].astype(F32)
        rest[-1][...] = acc

    ordered = [] if after is None else [after]
    return pl.pallas_call(
        body, name=name,
        grid=(rows // tm,),
        in_specs=[pl.BlockSpec((N_DEV, tm, width), lambda i: (0, i, 0))] + [pl.BlockSpec(memory_space=pl.ANY)] * len(ordered),
        out_specs=pl.BlockSpec((tm, width), lambda i: (i, 0)),
        out_shape=jax.ShapeDtypeStruct((rows, width), F32),
        compiler_params=pltpu.CompilerParams(dimension_semantics=("parallel",)),
    )(blocks, *ordered)


def _split_plan(src_ref, land_ref, sems):
    x, y, c = _place()
    me = _block_of((x, y, c))

    def peer(r):
        return (1 - x if r & 4 else x, 1 - y if r & 2 else y, 1 - c if r & 1 else c)

    sends = [_remote_copy(src_ref.at[_block_of(peer(r))], land_ref.at[me], sems, r - 1, peer(r)) for r in range(1, N_DEV)]
    arrivals = [_remote_copy(src_ref.at[me], land_ref.at[_block_of(peer(r))], sems, r - 1, peer(r)) for r in range(1, N_DEV)]
    return sends, arrivals


def _scatter_begin(name, blocks):
    def body(src_ref, land_ref, send_sems, recv_sems, src_thru, land_thru, token_ref):
        for cp in _split_plan(src_ref, land_ref, (send_sems, recv_sems))[0]:
            cp.start()
        token_ref[...] = jnp.zeros_like(token_ref)

    hbm, sem = pl.BlockSpec(memory_space=pltpu.HBM), pl.BlockSpec(memory_space=pltpu.SEMAPHORE)
    zone = pltpu.HBM(blocks.shape, blocks.dtype)
    *handles, token = pl.pallas_call(
        body, name=name,
        in_specs=(hbm, hbm),
        out_specs=(sem, sem, hbm, hbm, pl.BlockSpec(memory_space=pltpu.VMEM)),
        out_shape=(pltpu.SemaphoreType.DMA((N_LINKS,)), pltpu.SemaphoreType.DMA((N_LINKS,)), zone, zone,
                   jax.ShapeDtypeStruct((8, SLOT), F32)),
        input_output_aliases={0: 2, 1: 3},
        compiler_params=pltpu.CompilerParams(has_side_effects=pltpu.SideEffectType.DATAFLOW_SIDE_EFFECTING),
    )(pltpu.with_memory_space_constraint(blocks, pltpu.HBM),
      pltpu.with_memory_space_constraint(lax.empty(blocks.shape, blocks.dtype), pltpu.HBM))
    return handles, token


def _scatter_end(name, handles, after):
    send_sems, recv_sems, src, zone = handles

    def body(src_ref, land_ref, send_sems, recv_sems, after_ref, src_dead, got_ref):
        sends, arrivals = _split_plan(src_ref, land_ref, (send_sems, recv_sems))
        for cp in arrivals:
            cp.wait_recv()
        for cp in sends:
            cp.wait_send()

    hbm, sem = pl.BlockSpec(memory_space=pltpu.HBM), pl.BlockSpec(memory_space=pltpu.SEMAPHORE)
    sent, landed = pl.pallas_call(
        body, name=name,
        in_specs=(hbm, hbm, sem, sem, pl.BlockSpec(memory_space=pl.ANY)),
        out_specs=(hbm, hbm),
        out_shape=(pltpu.HBM(src.shape, src.dtype), pltpu.HBM(zone.shape, zone.dtype)),
        input_output_aliases={0: 0, 1: 1},
        compiler_params=pltpu.CompilerParams(has_side_effects=pltpu.SideEffectType.DATAFLOW_SIDE_EFFECTING),
    )(src, zone, send_sems, recv_sems, after)
    me = _block_of(_place())
    return lax.dynamic_update_slice_in_dim(landed, lax.dynamic_slice_in_dim(sent, me, 1, axis=0), me, axis=0)


def _adamw(name, w, g, m, v):
    def fn(rows, consts):
        wv, gv, mv, vv = rows
        m2 = ADAM_B1 * mv + (1.0 - ADAM_B1) * gv
        v2 = ADAM_B2 * vv + (1.0 - ADAM_B2) * jnp.square(gv)
        m_hat = m2 / (1.0 - ADAM_B1 ** ADAM_STEP)
        v_hat = v2 / (1.0 - ADAM_B2 ** ADAM_STEP)
        return [-ADAM_LR * (m_hat / (jnp.sqrt(v_hat) + ADAM_EPS) + ADAM_WD * wv), m2, v2], []

    return _rowwise(name, fn, [w, g, m, v], [], [(w.shape[1], F32)] * 3, tm=_row_tile(w.shape[0]))


ROW = 1024
FFN_NAMES = ("ffn1_w_gate", "ffn1_w_up", "ffn1_w_down", "ffn2_w_gate", "ffn2_w_up", "ffn2_w_down")
OTHER = {"w_in": "w_in_t", "mla_w_uq": "uq_t", "mla_w_ukv": "ukv_t", "w_out": "w_out"}
BY_COLUMNS = ("ffn1_w_gate", "ffn1_w_up", "ffn2_w_gate", "ffn2_w_up", "w_in", "mla_w_uq", "mla_w_ukv")
SMALL = {
    "ffn1_pre_g": (1024, 1024), "ffn1_post_g": (1024, 1024), "mix_pre_g": (1024, 1024), "mla_q_norm_g": (256, 256),
    "mla_kv_norm_g": (128, 128), "mla_out_g": (512, 512), "gdn_a_log": (8, 128), "gdn_dt_bias": (8, 128),
    "gdn_norm_g": (64, 128), "mix_post_g": (1024, 1024), "ffn2_pre_g": (1024, 1024), "ffn2_post_g": (1024, 1024),
}
CONV_SHAPE = (GDN_CONV, 3 * N_HEADS * GDN_D)
CONV_SHARD = (GDN_CONV, CONV_SHAPE[1] // N_DEV)
CONV_LANES = CONV_SHAPE[0] * CONV_SHAPE[1]
SMALL_ROWS = 8
REDUCE_ROWS = 16


def _pack_small(vecs, conv, rows):
    parts = [_pad_lanes(vecs[n].reshape(1, -1), 0, r) for n, (_, r) in SMALL.items()]
    parts.append(conv.reshape(1, -1))
    flat = jnp.concatenate(parts, axis=1)
    return _pad_lanes(flat, 0, rows * ROW).reshape(rows, ROW)


def _unpack_small(buf):
    flat = buf.reshape(1, -1)
    out, at = {}, 0
    for n, (w, r) in SMALL.items():
        out[n] = flat[:, at:at + w]
        at += r
    return out, flat[0, at:]


def kernel(x, positions, ffn1_pre_g, ffn1_w_gate, ffn1_w_up, ffn1_w_down, ffn1_post_g, mix_pre_g, w_in, mla_q_norm_g, mla_w_uq, mla_kv_norm_g, mla_w_ukv, mla_out_g, gdn_conv_w, gdn_a_log, gdn_dt_bias, gdn_norm_g, w_out, mix_post_g, ffn2_pre_g, ffn2_w_gate, ffn2_w_up, ffn2_w_down, ffn2_post_g, loss_target, m_ffn1_pre_g, m_ffn1_w_gate, m_ffn1_w_up, m_ffn1_w_down, m_ffn1_post_g, m_mix_pre_g, m_w_in, m_mla_q_norm_g, m_mla_w_uq, m_mla_kv_norm_g, m_mla_w_ukv, m_mla_out_g, m_gdn_conv_w, m_gdn_a_log, m_gdn_dt_bias, m_gdn_norm_g, m_w_out, m_mix_post_g, m_ffn2_pre_g, m_ffn2_w_gate, m_ffn2_w_up, m_ffn2_w_down, m_ffn2_post_g, v_ffn1_pre_g, v_ffn1_w_gate, v_ffn1_w_up, v_ffn1_w_down, v_ffn1_post_g, v_mix_pre_g, v_w_in, v_mla_q_norm_g, v_mla_w_uq, v_mla_kv_norm_g, v_mla_w_ukv, v_mla_out_g, v_gdn_conv_w, v_gdn_a_log, v_gdn_dt_bias, v_gdn_norm_g, v_w_out, v_mix_post_g, v_ffn2_pre_g, v_ffn2_w_gate, v_ffn2_w_up, v_ffn2_w_down, v_ffn2_post_g):
    given = dict(locals())
    order = ["ffn1_pre_g", "ffn1_w_gate", "ffn1_w_up", "ffn1_w_down", "ffn1_post_g", "mix_pre_g", "w_in", "mla_q_norm_g",
             "mla_w_uq", "mla_kv_norm_g", "mla_w_ukv", "mla_out_g", "gdn_conv_w", "gdn_a_log", "gdn_dt_bias", "gdn_norm_g",
             "w_out", "mix_post_g", "ffn2_pre_g", "ffn2_w_gate", "ffn2_w_up", "ffn2_w_down", "ffn2_post_g"]
    assert sorted(order) == sorted(list(FFN_NAMES) + list(OTHER) + list(SMALL) + ["gdn_conv_w"])

    def drop_depth(a):
        return a[0] if a.ndim == 3 else a

    wts = {n: drop_depth(given[n]) for n in order}
    mom = {n: drop_depth(given["m_" + n]) for n in order}
    var = {n: drop_depth(given["v_" + n]) for n in order}
    me = _block_of(_place())

    def wire(n):
        return (wts[n].T if n in BY_COLUMNS else wts[n]).astype(BF16)

    (ffn1,) = _exchange("gather_first", _Gather([jnp.stack([wire(n) for n in FFN_NAMES[:3]])]))
    mid = _Gather([wire(n) for n in ("w_in", "mla_w_uq", "mla_w_ukv")] + [wts["gdn_conv_w"]])
    late = _Gather([jnp.stack([wire(n) for n in FFN_NAMES[3:]]), wire("w_out")])
    full = {n: wts[n] for n in SMALL}
    full["ffn1"] = ffn1

    dx, grads, landed, begun, token = _local_step(x[0], positions[0], loss_target[0], full, mid, late)

    grad, outs = {}, {"delta": {}, "new_m": {}, "new_v": {}}

    def finish(n, blocks, after=None):
        total = _sum_blocks("sum_" + n, blocks, after=after)
        grad[n] = total.T if n in BY_COLUMNS else total
        outs["delta"][n], outs["new_m"][n], outs["new_v"][n] = _adamw("adamw_" + n, wts[n], grad[n], mom[n], var[n])

    for n, blocks in landed.items():
        finish(n, blocks, after=token)
        token = outs["new_v"][n]
    small_handles = begun.pop("small")
    for n, handles in begun.items():
        finish(n, _scatter_end("scatter_" + n + "_end", handles, after=token))
        token = outs["new_v"][n]

    small_sum = _sum_blocks("sum_small", _scatter_end("reduce_small_end", small_handles, after=token))
    loss = small_sum[REDUCE_ROWS - 1, ROW - 1]
    small_grad, conv_grad_full = _unpack_small(small_sum)
    grad.update(small_grad)
    grad["gdn_conv_w"] = lax.dynamic_slice(conv_grad_full[:CONV_LANES].reshape(CONV_SHAPE), (0, me * CONV_SHARD[1]), CONV_SHARD)
    outs["grad"] = grad
    small = [_pack_small(s, s["gdn_conv_w"].reshape(-1), SMALL_ROWS) for s in (wts, grad, mom, var)]
    for kind, s in zip(("delta", "new_m", "new_v"), _adamw("adamw_small", *small)):
        vecs, conv = _unpack_small(s)
        outs[kind].update(vecs)
        outs[kind]["gdn_conv_w"] = conv[:CONV_SHARD[0] * CONV_SHARD[1]].reshape(CONV_SHARD)
    result = [loss, dx[None]]
    for kind in ("grad", "delta", "new_m", "new_v"):
        result += [outs[kind][n].reshape(given[n].shape) for n in order]
    return tuple(result)
```

```python
import jax
import jax.numpy as jnp
from jax import lax
from jax.experimental import pallas as pl
from jax.experimental.pallas import tpu as pltpu

F32 = jnp.float32
BF16 = jnp.bfloat16
HI = lax.Precision.HIGH

N_DEV = 8
N_HEADS = 8
SLOT = 128
MLA_Q_RANK = 256
MLA_KV_RANK = 128
MLA_NOPE = 64
MLA_ROPE = 32
MLA_V = 64
GDN_D = 64
GDN_CONV = 4
GDN_CHUNK = 64
ROPE_THETA = 10000.0
EPS = 1e-6
ADAM_LR, ADAM_B1, ADAM_B2, ADAM_EPS, ADAM_WD, ADAM_STEP = 0.001, 0.9, 0.999, 1e-08, 0.01, 10


def _dot(a, b, ca, cb, precision=None):
    lead = a.ndim - 2
    batch = tuple(range(lead))
    return lax.dot_general(a, b, (((lead + ca,), (lead + cb,)), (batch, batch)), precision=precision,
                           preferred_element_type=F32)


def _nn(a, b, precision=None):
    return _dot(a, b, 1, 0, precision)


def _nt(a, b, precision=None):
    return _dot(a, b, 1, 1, precision)


def _tn(a, b, precision=None):
    return _dot(a, b, 0, 0, precision)


def _sigmoid(x):
    return 1.0 / (1.0 + jnp.exp(-x))


def _silu(x):
    return x * _sigmoid(x)


def _rms(x, g, n):
    ms = jnp.sum(x * x, axis=-1, keepdims=True) * (1.0 / n)
    return x * lax.rsqrt(ms + EPS) * g


def _chunk_masks():
    c = GDN_CHUNK
    i = lax.broadcasted_iota(jnp.int32, (c, c), 0)
    j = lax.broadcasted_iota(jnp.int32, (c, c), 1)
    lower = i >= j
    strict = i > j
    eye = (i == j).astype(F32)
    blocks = []
    b = 1
    while b < c:
        same = (i // (2 * b)) == (j // (2 * b))
        blocks.append(same & ((i % (2 * b)) >= b) & ((j % (2 * b)) < b))
        b *= 2
    return lower, strict, eye, blocks


def _unit_lower_inverse(low, eye, blocks):
    t = eye - jnp.where(blocks[0], low, 0.0)
    for m in blocks[1:]:
        lo = jnp.where(m, low, 0.0)
        t = t - _nn(t, _nn(lo, t, HI), HI)
    return t


@jax.custom_vjp
def _known_inverse(low, tinv):
    return tinv


def _known_inverse_fwd(low, tinv):
    return tinv, tinv


def _known_inverse_bwd(tinv, dt):
    return -_tn(tinv, _nt(dt, tinv, HI), HI), jnp.zeros_like(tinv)


_known_inverse.defvjp(_known_inverse_fwd, _known_inverse_bwd)

_PRODUCTS = {"nn": _nn, "nt": _nt, "tn": _tn}


@jax.custom_vjp
def _known_nn(a, b, c):
    return c


@jax.custom_vjp
def _known_nt(a, b, c):
    return c


@jax.custom_vjp
def _known_tn(a, b, c):
    return c


def _known_fwd(a, b, c):
    return c, (a, b, c)


_known_nn.defvjp(_known_fwd, lambda r, dc: (_nt(dc, r[1], HI), _tn(r[0], dc, HI), jnp.zeros_like(r[2])))
_known_nt.defvjp(_known_fwd, lambda r, dc: (_nn(dc, r[1], HI), _tn(dc, r[0], HI), jnp.zeros_like(r[2])))
_known_tn.defvjp(_known_fwd, lambda r, dc: (_nt(r[1], dc, HI), _nn(r[0], dc, HI), jnp.zeros_like(r[2])))
_KNOWN = {"nn": _known_nn, "nt": _known_nt, "tn": _known_tn}
GDN_PRODUCTS = 8
GDN_KEPT = 2 + GDN_PRODUCTS


def _gdn_chunk(q, k, v, gc, bb, s, masks, known=None):
    lower, strict, eye, blocks = masks
    made = []

    def product(kind, a, b):
        c = _PRODUCTS[kind](a, b, HI) if known is None else _KNOWN[kind](a, b, known[1 + len(made)])
        made.append(c)
        return c

    qs = q * (GDN_D ** -0.5)
    gct = jnp.swapaxes(gc, -1, -2)
    decay = jnp.exp(jnp.where(lower, gc - gct, -1e30))
    kb = k * bb
    low = jnp.where(strict, product("nt", kb, k) * decay, 0.0)
    tinv = _unit_lower_inverse(low, eye, blocks) if known is None else _known_inverse(low, known[0])
    eg = jnp.exp(gc)
    w = product("nn", tinv, kb * eg)
    u = product("nn", tinv, v * bb)
    attn = product("nt", qs, k) * decay
    last = lax.broadcasted_iota(jnp.int32, gc.shape[-2:], 0) == GDN_CHUNK - 1
    g_end = jnp.sum(jnp.where(last, gc, 0.0), axis=-2, keepdims=True)
    k_dec = k * jnp.exp(g_end - gc)
    v_new = u - product("nn", w, s)
    o = product("nn", qs * eg, s) + product("nn", attn, v_new)
    s_new = s * jnp.exp(g_end) + product("tn", k_dec, v_new)
    assert len(made) == GDN_PRODUCTS
    return o, s_new, [tinv] + made


GDN_GROUP = 8
GDN_GROUPS = N_HEADS // GDN_GROUP


def _group_heads(ref):
    return jnp.stack([ref[:, pl.ds(j * SLOT, GDN_D)] for j in range(GDN_GROUP)])


def _ungroup_heads(ref, val):
    pad = jnp.zeros((GDN_CHUNK, SLOT - GDN_D), F32)
    for j in range(GDN_GROUP):
        ref[:, pl.ds(j * SLOT, GDN_D)] = val[j]
        ref[:, pl.ds(j * SLOT + GDN_D, SLOT - GDN_D)] = pad


def _gdn_fwd(qkv, gb, bb, carry=None):
    t = qkv.shape[0]
    n_chunks = t // GDN_CHUNK
    d = GDN_D

    def body(q_ref, k_ref, v_ref, g_ref, b_ref, o_ref, keep_ref, s_ref):
        @pl.when(pl.program_id(1) == 0)
        def _():
            s_ref[...] = jnp.zeros_like(s_ref)

        s = s_ref[...]
        keep_ref[:, 0, 0] = s
        o, s_new, made = _gdn_chunk(*[_group_heads(r) for r in (q_ref, k_ref, v_ref, g_ref, b_ref)], s, _chunk_masks())
        for i, val in enumerate(made):
            keep_ref[:, 0, 1 + i] = val
        s_ref[...] = s_new
        _ungroup_heads(o_ref, o)

    def spec(kind=0):
        return pl.BlockSpec((GDN_CHUNK, GDN_GROUP * SLOT), lambda h, n: (n, kind * GDN_GROUPS + h))

    return _call_carrying(
        body, carry, (qkv, qkv, qkv, gb, bb), name="gdn_fwd",
        grid=(GDN_GROUPS, n_chunks),
        in_specs=[spec(0), spec(1), spec(2), spec(), spec()],
        out_specs=[spec(), pl.BlockSpec((GDN_GROUP, 1, GDN_KEPT, d, d), lambda h, n: (h, n, 0, 0, 0))],
        out_shape=[jax.ShapeDtypeStruct((t, N_HEADS * SLOT), F32), jax.ShapeDtypeStruct((N_HEADS, n_chunks, GDN_KEPT, d, d), F32)],
        scratch_shapes=[pltpu.VMEM((GDN_GROUP, d, d), F32)],
        compiler_params=pltpu.CompilerParams(dimension_semantics=("arbitrary", "arbitrary")),
    )


def _gdn_bwd(qkv, gb, bb, keep, do, carry=None):
    t = qkv.shape[0]
    n_chunks = t // GDN_CHUNK
    d = GDN_D

    def body(q_ref, k_ref, v_ref, g_ref, b_ref, keep_ref, do_ref, dqkv_ref, dg_ref, db_ref, ds_ref):
        @pl.when(pl.program_id(1) == 0)
        def _():
            ds_ref[...] = jnp.zeros_like(ds_ref)

        masks = _chunk_masks()
        known = [keep_ref[:, 0, 1 + i] for i in range(GDN_KEPT - 1)]
        _, pull = jax.vjp(lambda *a: _gdn_chunk(*a, masks, known)[:2],
                          *[_group_heads(r) for r in (q_ref, k_ref, v_ref, g_ref, b_ref)], keep_ref[:, 0, 0])
        dq, dk, dv, dg, db, ds = pull((_group_heads(do_ref), ds_ref[...]))
        ds_ref[...] = ds
        for i, val in enumerate((dq, dk, dv)):
            _ungroup_heads(dqkv_ref.at[i], val)
        _ungroup_heads(dg_ref, dg)
        _ungroup_heads(db_ref, db)

    def spec(kind=0):
        return pl.BlockSpec((GDN_CHUNK, GDN_GROUP * SLOT), lambda h, n: (n_chunks - 1 - n, kind * GDN_GROUPS + h))

    return _call_carrying(
        body, carry, (qkv, qkv, qkv, gb, bb, keep, do), name="gdn_bwd",
        grid=(GDN_GROUPS, n_chunks),
        in_specs=[spec(0), spec(1), spec(2), spec(), spec(),
                  pl.BlockSpec((GDN_GROUP, 1, GDN_KEPT, d, d), lambda h, n: (h, n_chunks - 1 - n, 0, 0, 0)), spec()],
        out_specs=[pl.BlockSpec((3, GDN_CHUNK, GDN_GROUP * SLOT), lambda h, n: (0, n_chunks - 1 - n, h)), spec(), spec()],
        out_shape=[jax.ShapeDtypeStruct((3, t, N_HEADS * SLOT), F32)] + [jax.ShapeDtypeStruct((t, N_HEADS * SLOT), F32)] * 2,
        scratch_shapes=[pltpu.VMEM((GDN_GROUP, d, d), F32)],
        compiler_params=pltpu.CompilerParams(dimension_semantics=("arbitrary", "arbitrary")),
    )


def _rowwise(name, fn, rows, consts, outs, sums=(), tm=512):
    rows = [x if isinstance(x, tuple) else (x, x.shape[1], 0) for x in rows]
    t = rows[0][0].shape[0]
    tm = min(tm, t)
    steps = t // tm
    n_r, n_c, n_o, n_s = len(rows), len(consts), len(outs), len(sums)

    def window(width, block):
        return pl.BlockSpec((tm, width), lambda i: (i, block))

    def body(*refs):
        r, c = refs[:n_r], refs[n_r:n_r + n_c]
        o, s = refs[n_r + n_c:n_r + n_c + n_o], refs[n_r + n_c + n_o:]
        vals, tot = fn([x[...] for x in r], [x[...] for x in c])
        for ref, val in zip(o, vals):
            ref[...] = val.astype(ref.dtype)
        if n_s:
            @pl.when(pl.program_id(0) == 0)
            def _():
                for ref in s:
                    ref[...] = jnp.zeros_like(ref)

            for ref, val in zip(s, tot):
                ref[...] += val

    return pl.pallas_call(
        body, name=name,
        grid=(steps,),
        in_specs=[window(w, b) for _, w, b in rows] + [pl.BlockSpec(x.shape, lambda i: (0, 0)) for x in consts],
        out_specs=[pl.BlockSpec((tm, w), lambda i: (i, 0)) for w, _ in outs]
        + [pl.BlockSpec((1, w), lambda i: (0, 0)) for w in sums],
        out_shape=[jax.ShapeDtypeStruct((t, w), dt) for w, dt in outs]
        + [jax.ShapeDtypeStruct((1, w), F32) for w in sums],
        compiler_params=pltpu.CompilerParams(dimension_semantics=("arbitrary",)),
    )(*[x for x, _, _ in rows], *consts)


def _tile(dim, target):
    if dim <= target:
        return dim
    best = None
    for cand in range(128, target + 1, 128):
        if dim % cand == 0:
            best = cand
    assert best is not None, (dim, target)
    return best


def _matmul(name, a, b, mode, out_dtype=F32, tm=1024, tn=1024, tk=2048, after=None):
    if mode == "nn":
        (m, k), n = a.shape, b.shape[1]
    elif mode == "nt":
        (m, k), n = a.shape, b.shape[0]
    else:
        (k, m), n = a.shape, b.shape[1]
    tm, tn, tk = _tile(m, tm), _tile(n, tn), _tile(k, tk)
    k_steps = k // tk
    product = {"nn": _nn, "nt": _nt, "tn": _tn}[mode]

    def body(a_ref, b_ref, *rest):
        o_ref, acc_ref = rest[-2:]
        part = product(a_ref[...].astype(BF16), b_ref[...].astype(BF16))
        if k_steps == 1:
            o_ref[...] = part.astype(o_ref.dtype)
        else:
            kk = pl.program_id(2)

            @pl.when(kk == 0)
            def _():
                acc_ref[...] = part

            @pl.when(kk > 0)
            def _():
                acc_ref[...] += part

            @pl.when(kk == k_steps - 1)
            def _():
                o_ref[...] = acc_ref[...].astype(o_ref.dtype)

    a_spec = pl.BlockSpec((tk, tm), lambda i, j, kk: (kk, i)) if mode == "tn" else pl.BlockSpec((tm, tk), lambda i, j, kk: (i, kk))
    b_spec = pl.BlockSpec((tn, tk), lambda i, j, kk: (j, kk)) if mode == "nt" else pl.BlockSpec((tk, tn), lambda i, j, kk: (kk, j))
    ordered = [] if after is None else [after]
    return pl.pallas_call(
        body, name=name,
        grid=(m // tm, n // tn, k_steps),
        in_specs=[a_spec, b_spec] + [pl.BlockSpec(memory_space=pl.ANY)] * len(ordered),
        out_specs=pl.BlockSpec((tm, tn), lambda i, j, kk: (i, j)),
        out_shape=jax.ShapeDtypeStruct((m, n), out_dtype),
        scratch_shapes=[pltpu.VMEM((tm, tn) if k_steps > 1 else (8, 128), F32)],
        compiler_params=pltpu.CompilerParams(dimension_semantics=("parallel", "parallel", "arbitrary")),
    )(a, b, *ordered)


FFN_TM = 512
FFN_BWD_TM = 256
FFN_BLOCKS = 4
FFN_GATE, FFN_UP, FFN_DOWN = 0, 1, 2


def _ffn_weight_specs(ffn_w, first):
    _, _, rows, dm = ffn_w.shape

    def spec(k):
        return pl.BlockSpec((FFN_BLOCKS, None, rows, dm), lambda i, j: (j, first + k, 0, 0))

    return [spec(FFN_GATE), spec(FFN_UP), spec(FFN_DOWN)], FFN_BLOCKS * rows


def _ffn_fwd(name, x, g_pre, ffn_w, first, g_post, carry=None):
    t, dm = x.shape
    tm = min(FFN_TM, t)
    w_specs, tf = _ffn_weight_specs(ffn_w, first)
    f_steps = N_DEV // FFN_BLOCKS

    def body(x_ref, gpre_ref, wg_ref, wu_ref, wd_ref, gpost_ref, h_ref, y_ref, hg_ref, hu_ref, xn_ref, acc_ref):
        j = pl.program_id(1)

        @pl.when(j == 0)
        def _():
            xn_ref[...] = _rms(x_ref[...], gpre_ref[...], dm).astype(BF16)
            acc_ref[...] = jnp.zeros_like(acc_ref)

        xn = xn_ref[...]
        wg, wu, wd = (r[...].reshape(tf, dm) for r in (wg_ref, wu_ref, wd_ref))
        hg, hu = _nt(xn, wg), _nt(xn, wu)
        hg_ref[...] = hg.astype(BF16)
        hu_ref[...] = hu.astype(BF16)
        a = _silu(hg) * hu
        acc_ref[...] += _nn(a.astype(BF16), wd)

        @pl.when(j == f_steps - 1)
        def _():
            h = acc_ref[...]
            h_ref[...] = h
            y_ref[...] = x_ref[...] + 0.5 * _rms(h, gpost_ref[...], dm)

    row = pl.BlockSpec((tm, dm), lambda i, j: (i, 0))
    vec = pl.BlockSpec((1, dm), lambda i, j: (0, 0))
    wide = pl.BlockSpec((tm, tf), lambda i, j: (i, j))
    return _call_carrying(
        body, carry, (x, g_pre, ffn_w, ffn_w, ffn_w, g_post), name=name,
        grid=(t // tm, f_steps),
        in_specs=[row, vec, *w_specs, vec],
        out_specs=[row, row, wide, wide],
        out_shape=[jax.ShapeDtypeStruct((t, dm), F32)] * 2 + [jax.ShapeDtypeStruct((t, f_steps * tf), BF16)] * 2,
        scratch_shapes=[pltpu.VMEM((tm, dm), BF16), pltpu.VMEM((tm, dm), F32)],
        compiler_params=pltpu.CompilerParams(dimension_semantics=("arbitrary", "arbitrary")),
    )


def _ffn_bwd(name, x, h, hg, hu, dy, g_pre, ffn_w, first, g_post, carry=None):
    t, dm = x.shape
    tm = min(FFN_BWD_TM, t)
    w_specs, tf = _ffn_weight_specs(ffn_w, first)
    f_steps = N_DEV // FFN_BLOCKS
    f = f_steps * tf

    def post(hv, g):
        return 0.5 * _rms(hv, g, dm)

    def pre(xv, g):
        return _rms(xv, g, dm)

    def body(x_ref, h_ref, dy_ref, hg_ref, hu_ref, gpre_ref, wg_ref, wu_ref, wd_ref, gpost_ref,
             dx_ref, xn_ref, dh_ref, a_ref, dhg_ref, dhu_ref, dgpre_ref, dgpost_ref, acc_ref):
        i, j = pl.program_id(0), pl.program_id(1)

        @pl.when((i == 0) & (j == 0))
        def _():
            dgpre_ref[...] = jnp.zeros_like(dgpre_ref)
            dgpost_ref[...] = jnp.zeros_like(dgpost_ref)

        @pl.when(j == 0)
        def _():
            xn_ref[...] = pre(x_ref[...], gpre_ref[...]).astype(BF16)
            _, pull = jax.vjp(post, h_ref[...], gpost_ref[...])
            dh, dg = pull(dy_ref[...])
            dh_ref[...] = dh.astype(BF16)
            dgpost_ref[...] += dg
            acc_ref[...] = jnp.zeros_like(acc_ref)

        wg, wu, wd = (r[...].reshape(tf, dm) for r in (wg_ref, wu_ref, wd_ref))
        hg, hu = hg_ref[...].astype(F32), hu_ref[...].astype(F32)
        da = _nt(dh_ref[...], wd)
        sig = _sigmoid(hg)
        act = hg * sig
        dhu = (da * act).astype(BF16)
        dhg = (da * hu * (sig * (1.0 + hg * (1.0 - sig)))).astype(BF16)
        a_ref[...] = (act * hu).astype(BF16)
        dhg_ref[...] = dhg
        dhu_ref[...] = dhu
        acc_ref[...] += _nn(dhg, wg) + _nn(dhu, wu)

        @pl.when(j == f_steps - 1)
        def _():
            _, pull = jax.vjp(pre, x_ref[...], gpre_ref[...])
            dx, dg = pull(acc_ref[...])
            dx_ref[...] = dy_ref[...] + dx
            dgpre_ref[...] += dg

    row = pl.BlockSpec((tm, dm), lambda i, j: (i, 0))
    vec = pl.BlockSpec((1, dm), lambda i, j: (0, 0))
    wide = pl.BlockSpec((tm, tf), lambda i, j: (i, j))
    return _call_carrying(
        body, carry, (x, h, dy, hg, hu, g_pre, ffn_w, ffn_w, ffn_w, g_post), name=name,
        grid=(t // tm, f_steps),
        in_specs=[row, row, row, wide, wide, vec, *w_specs, vec],
        out_specs=[row, row, row, wide, wide, wide, vec, vec],
        out_shape=[jax.ShapeDtypeStruct((t, dm), F32), jax.ShapeDtypeStruct((t, dm), BF16), jax.ShapeDtypeStruct((t, dm), BF16),
                   jax.ShapeDtypeStruct((t, f), BF16), jax.ShapeDtypeStruct((t, f), BF16), jax.ShapeDtypeStruct((t, f), BF16),
                   jax.ShapeDtypeStruct((1, dm), F32), jax.ShapeDtypeStruct((1, dm), F32)],
        scratch_shapes=[pltpu.VMEM((tm, dm), F32)],
        compiler_params=pltpu.CompilerParams(dimension_semantics=("arbitrary", "arbitrary")),
    )


ATT_T = 512
ATT_GROUP = 4
ATT_GROUP_FWD = 8
ATT_SCALE = (MLA_NOPE + MLA_ROPE) ** -0.5


def _stack_slots(ref, group):
    return jnp.stack([ref[:, pl.ds(j * SLOT, SLOT)] for j in range(group)])


def _unstack_slots(ref, val):
    for j in range(val.shape[0]):
        ref[:, pl.ds(j * SLOT, SLOT)] = val[j].astype(ref.dtype)


def _scores(q, k, diagonal):
    s = _nt(q, k) * ATT_SCALE
    if diagonal:
        row = lax.broadcasted_iota(jnp.int32, s.shape[1:], 0)
        col = lax.broadcasted_iota(jnp.int32, s.shape[1:], 1)
        s = jnp.where(col <= row, s, -1e30)
    return s


def _attn_pairs(steps, q_major):
    pairs = ([(qi, ki) for qi in range(steps) for ki in range(qi + 1)] if q_major
             else [(qi, ki) for ki in range(steps) for qi in range(ki, steps)])
    return jnp.array([p[0] for p in pairs], jnp.int32), jnp.array([p[1] for p in pairs], jnp.int32)


def _attn_specs(tile, group):
    width = group * SLOT
    return (pl.BlockSpec((tile, width), lambda h, p, qt, kt: (qt[p], h)),
            pl.BlockSpec((tile, width), lambda h, p, qt, kt: (kt[p], h)))


def _attn_fwd(q, k, v):
    t = q.shape[0]
    tile = min(ATT_T, t)
    steps = t // tile
    g = ATT_GROUP_FWD

    strip = min(SLOT, tile)

    def body(qt_ref, kt_ref, q_ref, k_ref, v_ref, o_ref, lse_ref, m_ref, l_ref, alpha_ref, acc_ref, s_ref, p_ref):
        qi, ki = qt_ref[pl.program_id(1)], kt_ref[pl.program_id(1)]

        @pl.when(ki == 0)
        def _():
            m_ref[...] = jnp.full_like(m_ref, -1e30)
            l_ref[...] = jnp.zeros_like(l_ref)
            acc_ref[...] = jnp.zeros_like(acc_ref)

        def step(diagonal):
            s_ref[...] = _nt(_stack_slots(k_ref, g), _stack_slots(q_ref, g))
            for j in range(tile // strip):
                c = pl.ds(j * strip, strip)
                s = s_ref[:, :, c] * ATT_SCALE
                if diagonal:
                    key = lax.broadcasted_iota(jnp.int32, s.shape[1:], 0)
                    query = lax.broadcasted_iota(jnp.int32, s.shape[1:], 1) + j * strip
                    s = jnp.where(key <= query, s, -1e30)
                m_old = m_ref[:, :, c]
                m_new = jnp.maximum(m_old, jnp.max(s, axis=1, keepdims=True))
                p = jnp.exp(s - m_new)
                alpha = jnp.exp(m_old - m_new)
                l_ref[:, :, c] = alpha * l_ref[:, :, c] + jnp.sum(p, axis=1, keepdims=True)
                alpha_ref[:, :, c] = alpha
                m_ref[:, :, c] = m_new
                p_ref[:, :, c] = p.astype(BF16)
            acc_ref[...] = acc_ref[...] * alpha_ref[...] + _tn(_stack_slots(v_ref, g), p_ref[...])

        @pl.when(ki < qi)
        def _():
            step(False)

        @pl.when(ki == qi)
        def _():
            step(True)
            out = acc_ref[...] / l_ref[...]
            lse = jnp.broadcast_to(m_ref[...] + jnp.log(l_ref[...]), out.shape)
            for j in range(g):
                o_ref[:, pl.ds(j * SLOT, SLOT)] = out[j].T
                lse_ref[:, pl.ds(j * SLOT, SLOT)] = lse[j].T

    q_spec, k_spec = _attn_specs(tile, g)
    tables = _attn_pairs(steps, True)
    return pl.pallas_call(
        body, name="attn_fwd",
        grid_spec=pltpu.PrefetchScalarGridSpec(
            num_scalar_prefetch=2, grid=(N_HEADS // g, tables[0].shape[0]),
            in_specs=[q_spec, k_spec, k_spec], out_specs=[q_spec, q_spec],
            scratch_shapes=[pltpu.VMEM((g, 1, tile), F32), pltpu.VMEM((g, 1, tile), F32), pltpu.VMEM((g, 1, tile), F32),
                            pltpu.VMEM((g, SLOT, tile), F32), pltpu.VMEM((g, tile, tile), F32), pltpu.VMEM((g, tile, tile), BF16)]),
        out_shape=[jax.ShapeDtypeStruct((t, N_HEADS * SLOT), F32)] * 2,
        compiler_params=pltpu.CompilerParams(dimension_semantics=("parallel", "arbitrary")),
    )(*tables, q, k, v)


def _attn_grad_scores(q, k, v, do, lse_ref, delta_ref, diagonal):
    g = ATT_GROUP
    p = jnp.exp(_scores(q, k, diagonal) - _stack_slots(lse_ref, g)[:, :, 0:1])
    dp = _nt(do, v)
    return p, p * (dp - _stack_slots(delta_ref, g)[:, :, 0:1]) * ATT_SCALE


def _attn_bwd(q, k, v, do, lse, delta):
    t = q.shape[0]
    tile = min(ATT_T, t)
    steps = t // tile
    g = ATT_GROUP

    def body(qt_ref, kt_ref, q_ref, k_ref, v_ref, do_ref, lse_ref, delta_ref, dq_ref, dk_ref, dv_ref, dk_acc, dv_acc):
        qi, ki = qt_ref[pl.program_id(1)], kt_ref[pl.program_id(1)]

        @pl.when(pl.program_id(1) == 0)
        def _():
            dq_ref[...] = jnp.zeros_like(dq_ref)

        def step(diagonal):
            qq, kk = _stack_slots(q_ref, g), _stack_slots(k_ref, g)
            do_b = _stack_slots(do_ref, g).astype(BF16)
            p, ds = _attn_grad_scores(qq, kk, _stack_slots(v_ref, g), do_b, lse_ref, delta_ref, diagonal)
            ds = ds.astype(BF16)
            dv_acc[...] += _tn(p.astype(BF16), do_b)
            dk_acc[...] += _tn(ds, qq)
            dq = _nn(ds, kk)
            rows = pl.ds(pl.multiple_of(qi * tile, tile), tile)
            for j in range(g):
                dq_ref[rows, pl.ds(j * SLOT, SLOT)] += dq[j]

        @pl.when(qi == ki)
        def _():
            dk_acc[...] = jnp.zeros_like(dk_acc)
            dv_acc[...] = jnp.zeros_like(dv_acc)
            step(True)

        @pl.when(qi > ki)
        def _():
            step(False)

        @pl.when(qi == steps - 1)
        def _():
            _unstack_slots(dk_ref, dk_acc[...])
            _unstack_slots(dv_ref, dv_acc[...])

    q_spec, k_spec = _attn_specs(tile, g)
    tables = _attn_pairs(steps, False)
    return pl.pallas_call(
        body, name="attn_bwd",
        grid_spec=pltpu.PrefetchScalarGridSpec(
            num_scalar_prefetch=2, grid=(N_HEADS // g, tables[0].shape[0]),
            in_specs=[q_spec, k_spec, k_spec, q_spec, q_spec, q_spec],
            out_specs=[pl.BlockSpec((t, g * SLOT), lambda h, p, qt, kt: (0, h)), k_spec, k_spec],
            scratch_shapes=[pltpu.VMEM((g, tile, SLOT), F32), pltpu.VMEM((g, tile, SLOT), F32)]),
        out_shape=[jax.ShapeDtypeStruct((t, N_HEADS * SLOT), F32)] * 3,
        compiler_params=pltpu.CompilerParams(dimension_semantics=("parallel", "arbitrary")),
    )(*tables, q, k, v, do, lse, delta)


CONV_PAD = 8


def _fill_padded(ref, val):
    t = val.shape[0]
    zeros = jnp.zeros((CONV_PAD, val.shape[1]), val.dtype)
    ref[pl.ds(0, CONV_PAD)] = zeros
    ref[pl.ds(CONV_PAD + t, CONV_PAD)] = zeros
    ref[pl.ds(CONV_PAD, t)] = val


def _shifted(ref, s):
    return ref[pl.ds(CONV_PAD - s, ref.shape[0] - 2 * CONV_PAD)]


def _l2norm(x):
    return x * lax.rsqrt(jnp.sum(x * x, axis=-1, keepdims=True) + EPS)


def _conv_pre(x_pad, w):
    y = w[GDN_CONV - 1:GDN_CONV, :] * _shifted(x_pad, 0)
    for s in range(1, GDN_CONV):
        y = y + w[GDN_CONV - 1 - s:GDN_CONV - s, :] * _shifted(x_pad, s)
    return y


def _gdn_conv_fwd(x, w):
    t, width = x.shape

    def body(x_ref, w_ref, o_ref, x_pad):
        _fill_padded(x_pad, x_ref[...])
        act = _silu(_conv_pre(x_pad, w_ref[...]))
        normed = pl.program_id(0) < 2 * N_HEADS
        o_ref[...] = jnp.where(normed, _l2norm(act), act)

    return pl.pallas_call(
        body, name="gdn_conv_fwd",
        grid=(width // SLOT,),
        in_specs=[pl.BlockSpec((t, SLOT), lambda j: (0, j)), pl.BlockSpec((GDN_CONV, SLOT), lambda j: (0, j))],
        out_specs=pl.BlockSpec((t, SLOT), lambda j: (0, j)),
        out_shape=jax.ShapeDtypeStruct((t, width), F32),
        scratch_shapes=[pltpu.VMEM((t + 2 * CONV_PAD, SLOT), F32)],
        compiler_params=pltpu.CompilerParams(dimension_semantics=("parallel",)),
    )(x, w)


def _gdn_conv_bwd(x, w, dout):
    t, width = x.shape

    def body(x_ref, w_ref, do_ref, dx_ref, dw_ref, x_pad, dy_pad):
        wv = w_ref[...]
        _fill_padded(x_pad, x_ref[...])
        y = _conv_pre(x_pad, wv)
        sig = _sigmoid(y)
        act = y * sig
        _, pull = jax.vjp(_l2norm, act)
        normed = pl.program_id(0) < 2 * N_HEADS
        dact = jnp.where(normed, pull(do_ref[0])[0], do_ref[0])
        dy = dact * (sig * (1.0 + y * (1.0 - sig)))
        _fill_padded(dy_pad, dy)
        dx = wv[GDN_CONV - 1:GDN_CONV, :] * dy
        for s in range(1, GDN_CONV):
            dx = dx + wv[GDN_CONV - 1 - s:GDN_CONV - s, :] * _shifted(dy_pad, -s)
        dx_ref[...] = dx.astype(BF16)
        for s in range(GDN_CONV):
            dw_ref[GDN_CONV - 1 - s:GDN_CONV - s, :] = jnp.sum(dy * _shifted(x_pad, s), axis=0, keepdims=True)

    col = pl.BlockSpec((t, SLOT), lambda j: (0, j))
    tap = pl.BlockSpec((GDN_CONV, SLOT), lambda j: (0, j))
    return pl.pallas_call(
        body, name="gdn_conv_bwd",
        grid=(width // SLOT,),
        in_specs=[col, tap, pl.BlockSpec((1, t, SLOT), lambda j: (j // N_HEADS, 0, j % N_HEADS))],
        out_specs=[col, tap],
        out_shape=[jax.ShapeDtypeStruct((t, width), BF16), jax.ShapeDtypeStruct((GDN_CONV, width), F32)],
        scratch_shapes=[pltpu.VMEM((t + 2 * CONV_PAD, SLOT), F32)] * 2,
        compiler_params=pltpu.CompilerParams(dimension_semantics=("parallel",)),
    )(x, w, dout)


def _softplus(x):
    e = jnp.exp(-jnp.abs(x))
    u = 1.0 + e
    log1p = jnp.where(u == 1.0, e, jnp.log(u) * e / jnp.where(u == 1.0, 1.0, u - 1.0))
    return jnp.maximum(x, 0.0) + log1p


def _chunk_running_sum(x, reverse=False):
    tm = x.shape[0]
    at = lax.broadcasted_iota(jnp.int32, x.shape, 0) % GDN_CHUNK
    step = 1
    while step < GDN_CHUNK:
        if reverse:
            x = x + jnp.where(at < GDN_CHUNK - step, pltpu.roll(x, tm - step, 0), 0.0)
        else:
            x = x + jnp.where(at >= step, pltpu.roll(x, step, 0), 0.0)
        step *= 2
    return x


def _gates_fwd(ab, a_log, dt_bias):
    def fn(rows, consts):
        (abv,), (alog, dtb) = rows, consts
        g = _chunk_running_sum(-jnp.exp(alog) * _softplus(abv + dtb))
        beta = _sigmoid(abv)
        shape = (abv.shape[0], SLOT)
        g_slots = [jnp.broadcast_to(g[:, h:h + 1], shape) for h in range(N_HEADS)]
        b_slots = [jnp.broadcast_to(beta[:, N_HEADS + h:N_HEADS + h + 1], shape) for h in range(N_HEADS)]
        return [jnp.concatenate(g_slots, axis=1), jnp.concatenate(b_slots, axis=1)], []

    width = N_HEADS * SLOT
    return _rowwise("gdn_gates_fwd", fn, [ab], [a_log, dt_bias], [(width, F32), (width, F32)])


def _gates_bwd(ab, a_log, dt_bias, dg, dbeta):
    def fn(rows, consts):
        (abv, dgv, dbv), (alog, dtb) = rows, consts
        lane = lax.broadcasted_iota(jnp.int32, abv.shape, 1)
        dg_tok = jnp.zeros_like(abv)
        db_tok = jnp.zeros_like(abv)
        for h in range(N_HEADS):
            dg_tok = dg_tok + jnp.where(lane == h, jnp.sum(dgv[:, h * SLOT:(h + 1) * SLOT], axis=1, keepdims=True), 0.0)
            db_tok = db_tok + jnp.where(lane == N_HEADS + h, jnp.sum(dbv[:, h * SLOT:(h + 1) * SLOT], axis=1, keepdims=True), 0.0)
        dg_tok = _chunk_running_sum(dg_tok, reverse=True)
        xa = abv + dtb
        g = -jnp.exp(alog) * _softplus(xa)
        da = dg_tok * (-jnp.exp(alog)) * _sigmoid(xa)
        beta = _sigmoid(abv)
        dab = jnp.where(lane < N_HEADS, da, db_tok * beta * (1.0 - beta))
        dab = jnp.where(lane < 2 * N_HEADS, dab, 0.0)
        d_alog = jnp.sum(jnp.where(lane < N_HEADS, dg_tok * g, 0.0), axis=0, keepdims=True)
        d_dtb = jnp.sum(jnp.where(lane < N_HEADS, da, 0.0), axis=0, keepdims=True)
        return [dab], [d_alog, d_dtb]

    return _rowwise("gdn_gates_bwd", fn, [ab, dg, dbeta], [a_log, dt_bias], [(SLOT, F32)], sums=[SLOT, SLOT])


ROPE_HALF = MLA_ROPE // 2


def _rope_tables(positions):
    freqs = ROPE_THETA ** (-jnp.arange(ROPE_HALF, dtype=F32) / ROPE_HALF)
    ang = positions.astype(F32)[:, None] * freqs
    cos, sin = jnp.cos(ang), jnp.sin(ang)
    t = positions.shape[0]
    ones, zeros = jnp.ones((t, MLA_NOPE), F32), jnp.zeros((t, MLA_NOPE), F32)
    tail = jnp.zeros((t, SLOT - MLA_NOPE - MLA_ROPE), F32)
    half0 = jnp.zeros((t, ROPE_HALF), F32)
    same = jnp.concatenate([ones, cos, cos, tail], axis=1)
    from_low = jnp.concatenate([zeros, half0, sin, tail], axis=1)
    from_high = jnp.concatenate([zeros, -sin, half0, tail], axis=1)
    return same, from_low, from_high


def _rope(x, tabs):
    same, from_low, from_high = tabs
    width = x.shape[1]
    return x * same + pltpu.roll(x, ROPE_HALF, 1) * from_low + pltpu.roll(x, width - ROPE_HALF, 1) * from_high


def _rope_transposed(dy, tabs):
    same, from_low, from_high = tabs
    width = dy.shape[1]
    return dy * same + pltpu.roll(dy * from_low, width - ROPE_HALF, 1) + pltpu.roll(dy * from_high, ROPE_HALF, 1)


def _tile_slots(tab):
    return jnp.concatenate([tab] * N_HEADS, axis=1)


A_WIDTH = MLA_Q_RANK + MLA_KV_RANK + 2 * SLOT
A_KPE = MLA_Q_RANK + MLA_KV_RANK
A_AB = A_KPE + SLOT
WIDE = N_HEADS * SLOT


def _mla_front_fwd(proj_a, tabs, g_q, g_kv, w_uq, w_kv):
    def fn(rows, consts):
        pa, *tb = rows
        gq, gkv, wuq, wkv = consts
        cqn = _rms(pa[:, :MLA_Q_RANK], gq, MLA_Q_RANK).astype(BF16)
        ckvn = _rms(pa[:, MLA_Q_RANK:A_KPE], gkv, MLA_KV_RANK).astype(BF16)
        kv = _nt(ckvn, wkv)
        q = _rope(_nt(cqn, wuq), [_tile_slots(x) for x in tb])
        k = kv[:, :WIDE] + _tile_slots(_rope(pa[:, A_KPE:A_AB], tb))
        return [cqn, ckvn, q, k, kv[:, WIDE:]], []

    return _rowwise("mla_front_fwd", fn, [proj_a, *tabs], [g_q, g_kv, w_uq, w_kv],
                    [(MLA_Q_RANK, BF16), (MLA_KV_RANK, BF16)] + [(WIDE, BF16)] * 3)


def _mla_front_bwd(proj_a, tabs, g_q, g_kv, w_uq, w_kv, dq, dk, dv, dab):
    def fn(rows, consts):
        pa, t0, t1, t2, dqv, dkv, dvv, da = rows
        gq, gkv, wuq, wkv = consts
        tb = (t0, t1, t2)
        dq_p = _rope_transposed(dqv, [_tile_slots(x) for x in tb]).astype(BF16)
        dkv_p = jnp.concatenate([dkv, dvv], axis=1).astype(BF16)
        dkpe = dkv[:, :SLOT]
        for h in range(1, N_HEADS):
            dkpe = dkpe + dkv[:, h * SLOT:(h + 1) * SLOT]
        _, pull_q = jax.vjp(lambda x, g: _rms(x, g, MLA_Q_RANK), pa[:, :MLA_Q_RANK], gq)
        _, pull_kv = jax.vjp(lambda x, g: _rms(x, g, MLA_KV_RANK), pa[:, MLA_Q_RANK:A_KPE], gkv)
        dcq, dgq = pull_q(_nn(dq_p, wuq))
        dckv, dgkv = pull_kv(_nn(dkv_p, wkv))
        return [jnp.concatenate([dcq, dckv, _rope_transposed(dkpe, tb), da], axis=1), dq_p, dkv_p], [dgq, dgkv]

    return _rowwise("mla_front_bwd", fn, [proj_a, *tabs, dq, dk, dv, dab], [g_q, g_kv, w_uq, w_kv],
                    [(A_WIDTH, BF16), (WIDE, BF16), (2 * WIDE, BF16)], sums=[MLA_Q_RANK, MLA_KV_RANK])


def _slot_sum(x):
    parts = [jnp.broadcast_to(jnp.sum(x[:, h * SLOT:(h + 1) * SLOT], axis=1, keepdims=True), (x.shape[0], SLOT))
             for h in range(N_HEADS)]
    return jnp.concatenate(parts, axis=1)


def _mix_join(o_mla, o_gdn, gate, g_mla, g_gdn):
    mla = _rms(o_mla, g_mla, N_HEADS * MLA_V)
    gdn = o_gdn * lax.rsqrt(_slot_sum(o_gdn * o_gdn) * (1.0 / GDN_D) + EPS) * g_gdn * _silu(gate)
    return mla, gdn


MIX_TM = 256


def _mix_fwd(o_mla, o_gdn, gate, x, g_mla, g_gdn, w_out, g_post):
    dm = x.shape[1]

    def fn(rows, consts):
        om, og, gt, xv = rows
        gm, gg, wo, gp = consts
        cat = jnp.concatenate(_mix_join(om, og, gt, gm, gg), axis=1).astype(BF16)
        mixed = _nn(cat, wo)
        return [cat, mixed, xv + _rms(mixed, gp, dm)], []

    return _rowwise("mix_fwd", fn, [o_mla, o_gdn, gate, x], [g_mla, g_gdn, w_out, g_post],
                    [(2 * WIDE, BF16), (dm, F32), (dm, F32)], tm=MIX_TM)


def _mix_bwd(o_mla, o_gdn, gate, mixed, dy, g_mla, g_gdn, w_out, g_post):
    dm = mixed.shape[1]

    def fn(rows, consts):
        om, og, gt, mx, dyv = rows
        gm, gg, wo, gp = consts
        _, pull_post = jax.vjp(lambda hv, gv: _rms(hv, gv, dm), mx, gp)
        dmixed, dgp = pull_post(dyv)
        dmixed = dmixed.astype(BF16)
        dc = _nt(dmixed, wo)
        _, pull = jax.vjp(lambda x, g: _rms(x, g, N_HEADS * MLA_V), om, gm)
        dom, dgm = pull(dc[:, :WIDE])
        dn_out = dc[:, WIDE:]
        r = lax.rsqrt(_slot_sum(og * og) * (1.0 / GDN_D) + EPS)
        sig = _sigmoid(gt)
        normed = og * r
        dn = dn_out * gg * (gt * sig)
        dog = r * dn - normed * (r * r) * _slot_sum(dn * og) * (1.0 / GDN_D)
        dgt = dn_out * normed * gg * (sig * (1.0 + gt * (1.0 - sig)))
        dgg = jnp.sum(dn_out * normed * (gt * sig), axis=0, keepdims=True)
        return [dmixed, dom, _slot_sum(dom * om), dog, dgt], [dgp, dgm, dgg]

    return _rowwise("mix_bwd", fn, [o_mla, o_gdn, gate, mixed, dy], [g_mla, g_gdn, w_out, g_post],
                    [(dm, BF16), (WIDE, F32), (WIDE, F32), (WIDE, F32), (WIDE, BF16)], sums=[dm, WIDE, WIDE], tm=MIX_TM)


def _proj_fwd(x, g, weights):
    dm = x.shape[1]

    def fn(rows, consts):
        hn = _rms(rows[0], consts[0], dm).astype(BF16)
        return [hn] + [_nt(hn, wv) for wv in consts[1:]], []

    return _rowwise("proj_fwd", fn, [x], [g, *weights], [(dm, BF16)] + [(wv.shape[0], F32) for wv in weights], tm=MIX_TM)


def _proj_bwd(x, g, weights, cots, dy):
    dm = x.shape[1]
    n = len(weights)

    def fn(rows, consts):
        xv, dyv, *parts = rows
        dn = _nn(parts[0], consts[1])
        for p, wv in zip(parts[1:], consts[2:]):
            dn = dn + _nn(p, wv)
        _, pull = jax.vjp(lambda a, gv: _rms(a, gv, dm), xv, consts[0])
        dx, dg = pull(dn)
        return [dyv + dx], [dg]

    assert len(cots) == n
    return _rowwise("proj_bwd", fn, [x, dy, *cots], [g, *weights], [(dm, F32)], sums=[dm], tm=MIX_TM)


def _loss_fwd(y, target):
    dm = y.shape[1]

    def fn(rows, consts):
        err = rows[0] - rows[1]
        sq = err * err
        lanes = sq[:, :SLOT]
        for j in range(1, dm // SLOT):
            lanes = lanes + sq[:, j * SLOT:(j + 1) * SLOT]
        return [err * (1.0 / dm)], [jnp.sum(lanes, axis=0, keepdims=True) * (0.5 / dm)]

    return _rowwise("loss", fn, [y, target], [], [(dm, F32)], sums=[SLOT])


W_IN_CUTS = (0, 256, 384, 416, 1952, 1960, 1968, 2480)


def _heads_out(w, per_head, axis=-1):
    axis = axis % w.ndim
    shape = w.shape
    n = shape[axis] // per_head
    w = w.reshape(shape[:axis] + (n, per_head) + shape[axis + 1:])
    pad = [(0, 0)] * w.ndim
    pad[axis + 1] = (0, SLOT - per_head)
    return jnp.pad(w, pad).reshape(shape[:axis] + (n * SLOT,) + shape[axis + 1:])


def _heads_in(w, per_head, axis=-1):
    axis = axis % w.ndim
    shape = w.shape
    n = shape[axis] // SLOT
    w = w.reshape(shape[:axis] + (n, SLOT) + shape[axis + 1:])
    w = lax.slice_in_dim(w, 0, per_head, axis=axis + 1)
    return w.reshape(shape[:axis] + (n * per_head,) + shape[axis + 1:])


def _pad_lanes(v, lo, width=SLOT):
    return jnp.pad(v, [(0, 0)] * (v.ndim - 1) + [(lo, width - lo - v.shape[-1])])


def _pad_rows(v, lo, rows=SLOT):
    return jnp.pad(v, [(lo, rows - lo - v.shape[0])] + [(0, 0)] * (v.ndim - 1))


def _layout_weights(w):
    c = W_IN_CUTS
    w_in = w["w_in_t"]
    p = {}
    p["w_a"] = jnp.concatenate([w_in[c[0]:c[2]], _pad_rows(w_in[c[2]:c[3]], MLA_NOPE), _pad_rows(w_in[c[4]:c[6]], 0)], axis=0)
    p["w_qkv"] = _heads_out(w_in[c[3]:c[4]], GDN_D, axis=0)
    p["w_gate"] = _heads_out(w_in[c[6]:c[7]], GDN_D, axis=0)
    p["w_uq"] = _heads_out(w["uq_t"], MLA_NOPE + MLA_ROPE, axis=0)
    ukv = w["ukv_t"].reshape(N_HEADS, MLA_NOPE + MLA_V, MLA_KV_RANK)
    p["w_kv"] = jnp.concatenate([_heads_out(ukv[:, :MLA_NOPE].reshape(-1, MLA_KV_RANK), MLA_NOPE, axis=0),
                                 _heads_out(ukv[:, MLA_NOPE:].reshape(-1, MLA_KV_RANK), MLA_V, axis=0)], axis=0)
    p["conv"] = _heads_out(w["gdn_conv_w"], GDN_D)
    p["g_mla_out"] = _heads_out(w["mla_out_g"], MLA_V)
    p["g_gdn"] = jnp.tile(_pad_lanes(w["gdn_norm_g"], 0), (1, N_HEADS))
    p["a_log"] = _pad_lanes(w["gdn_a_log"], 0)
    p["dt_bias"] = _pad_lanes(w["gdn_dt_bias"], 0)
    return p


def _unlayout_grads(d):
    c = W_IN_CUTS
    g = {}
    da = d["w_a"]
    kpe0 = A_KPE + MLA_NOPE
    g["w_in_t"] = jnp.concatenate([da[:A_KPE], da[kpe0:kpe0 + MLA_ROPE], _heads_in(d["w_qkv"], GDN_D, axis=0),
                                   da[A_AB:A_AB + 2 * N_HEADS], _heads_in(d["w_gate"], GDN_D, axis=0)], axis=0)
    assert g["w_in_t"].shape[0] == c[-1]
    g["uq_t"] = _heads_in(d["w_uq"], MLA_NOPE + MLA_ROPE, axis=0)
    dk = _heads_in(d["w_kv"][:WIDE], MLA_NOPE, axis=0).reshape(N_HEADS, MLA_NOPE, MLA_KV_RANK)
    dv = _heads_in(d["w_kv"][WIDE:], MLA_V, axis=0).reshape(N_HEADS, MLA_V, MLA_KV_RANK)
    g["ukv_t"] = jnp.concatenate([dk, dv], axis=1).reshape(-1, MLA_KV_RANK)
    g["w_out"] = _heads_in(d["w_out"], GDN_D, axis=0)
    g["gdn_conv_w"] = _heads_in(d["conv"], GDN_D)
    g["mla_out_g"] = _heads_in(d["g_mla_out"], MLA_V)
    g["gdn_norm_g"] = jnp.sum(d["g_gdn"].reshape(N_HEADS, SLOT), axis=0, keepdims=True)[:, :GDN_D]
    g["gdn_a_log"] = d["a_log"][:, :N_HEADS]
    g["gdn_dt_bias"] = d["dt_bias"][:, :N_HEADS]
    return g


def _weight_grad(name, cots, acts, out_dtype=F32, tm=1024, tn=1024, tk=2048, after=None):
    return _matmul(name, cots, acts, "tn", out_dtype=out_dtype, tm=tm, tn=tn, tk=tk, after=after)


def _by_device(a):
    return a.astype(BF16).reshape((N_DEV, a.shape[0] // N_DEV) + a.shape[1:])


def _rows_of(blocks):
    return blocks.reshape((-1,) + blocks.shape[2:])


def _local_step(x, positions, target, w, mid, late):
    tabs = _rope_tables(positions)

    (h1, x1, hg1, hu1), gathered = _ffn_fwd("ffn1_fwd", x, w["ffn1_pre_g"], w["ffn1"], 0, w["ffn1_post_g"], carry=mid)
    w = dict(w, w_in_t=_rows_of(gathered[0]), uq_t=_rows_of(gathered[1]), ukv_t=_rows_of(gathered[2]),
             gdn_conv_w=gathered[3].transpose(1, 0, 2).reshape(CONV_SHAPE))
    p = _layout_weights(w)
    in_weights = [p["w_a"], p["w_qkv"], p["w_gate"]]
    hn, proj_a, proj_qkv, proj_gate = _proj_fwd(x1, w["mix_pre_g"], in_weights)
    cqn, ckvn, q, k, v = _mla_front_fwd(proj_a, tabs, w["mla_q_norm_g"], w["mla_kv_norm_g"], p["w_uq"], p["w_kv"])
    o_mla, lse = _attn_fwd(q, k, v)
    ab = (proj_a, SLOT, A_AB // SLOT)
    qkv_n = _gdn_conv_fwd(proj_qkv, p["conv"])
    gb, bb = _gates_fwd(ab, p["a_log"], p["dt_bias"])
    (o_gdn, keep), (ffn2, w_out) = _gdn_fwd(qkv_n, gb, bb, carry=late)
    p["w_out"] = _heads_out(_rows_of(w_out), GDN_D, axis=0)
    cat, mixed, x2 = _mix_fwd(o_mla, o_gdn, proj_gate, x1, p["g_mla_out"], p["g_gdn"], p["w_out"], w["mix_post_g"])
    (h2, y, hg2, hu2), _ = _ffn_fwd("ffn2_fwd", x2, w["ffn2_pre_g"], ffn2, 0, w["ffn2_post_g"])
    dy, loss_lanes = _loss_fwd(y, target)

    g = {}
    (dx2, xn2, dh2, a2, dhg2, dhu2, g["ffn2_pre_g"], g["ffn2_post_g"]), _ = _ffn_bwd(
        "ffn2_bwd", x2, h2, hg2, hu2, dy, w["ffn2_pre_g"], ffn2, 0, w["ffn2_post_g"])
    ffn2_grads = _Scatter([_by_device(_weight_grad("ffn2_dw_gate", dhg2, xn2, BF16, tm=1408)),
                           _by_device(_weight_grad("ffn2_dw_up", dhu2, xn2, BF16, tm=1408)),
                           _by_device(_weight_grad("ffn2_dw_down", a2, dh2, BF16, tm=1408))])
    d = {}
    dmixed, do_mla, delta, do_gdn, dgate, g["mix_post_g"], d["g_mla_out"], d["g_gdn"] = _mix_bwd(
        o_mla, o_gdn, proj_gate, mixed, dx2, p["g_mla_out"], p["g_gdn"], p["w_out"], w["mix_post_g"])
    d["w_out"] = _weight_grad("mix_out_dw", cat, dmixed)
    dq, dk, dv = _attn_bwd(q, k, v, do_mla, lse, delta)
    (dqkv_n, dgb, dbb), landed_ffn2 = _gdn_bwd(qkv_n, gb, bb, keep, do_gdn, carry=ffn2_grads)
    dab, d["a_log"], d["dt_bias"] = _gates_bwd(ab, p["a_log"], p["dt_bias"], dgb, dbb)
    dproj_qkv, d["conv"] = _gdn_conv_bwd(proj_qkv, p["conv"], dqkv_n)
    dproj_a, dq_p, dkv_p, g["mla_q_norm_g"], g["mla_kv_norm_g"] = _mla_front_bwd(
        proj_a, tabs, w["mla_q_norm_g"], w["mla_kv_norm_g"], p["w_uq"], p["w_kv"], dq, dk, dv, dab)
    d["w_uq"] = _weight_grad("mla_q_dw", dq_p, cqn)
    d["w_kv"] = _weight_grad("mla_kv_dw", dkv_p, ckvn)
    d["w_a"] = _weight_grad("proj_a_dw", dproj_a, hn, tm=640)
    d["w_qkv"] = _weight_grad("proj_qkv_dw", dproj_qkv, hn)
    d["w_gate"] = _weight_grad("proj_gate_dw", dgate, hn)
    dx1, g["mix_pre_g"] = _proj_bwd(x1, w["mix_pre_g"], in_weights, [dproj_a, dproj_qkv, dgate], dx2)
    g.update(_unlayout_grads(d))
    others = list(OTHER.values())
    (dx, xn1, dh1, a1, dhg1, dhu1, g["ffn1_pre_g"], g["ffn1_post_g"]), landed_others = _ffn_bwd(
        "ffn1_bwd", x, h1, hg1, hu1, dx1, w["ffn1_pre_g"], w["ffn1"], 0, w["ffn1_post_g"], carry=_Scatter([_by_device(g.pop(t)) for t in others]))
    landed = dict(zip(list(FFN_NAMES[3:]) + list(OTHER), list(landed_ffn2) + list(landed_others)))
    packed = _pack_small(g, g["gdn_conv_w"].reshape(-1), REDUCE_ROWS)
    packed = packed.at[REDUCE_ROWS - 1, ROW - 1].set(jnp.sum(loss_lanes))
    begun = {}
    begun["small"], token = _scatter_begin("reduce_small_begin", jnp.broadcast_to(packed, (N_DEV,) + packed.shape))
    for name, cots, acts in (("ffn1_w_down", a1, dh1), ("ffn1_w_gate", dhg1, xn1), ("ffn1_w_up", dhu1, xn1)):
        blocks = _by_device(_weight_grad(name + "_grad", cots, acts, BF16, tm=1408, after=token))
        begun[name], token = _scatter_begin("scatter_" + name + "_begin", blocks)
    return dx, g, landed, begun, token


MESH_AXES = ("x", "y", "c")
N_LINKS = N_DEV - 1


def _place():
    return tuple(lax.axis_index(a) for a in MESH_AXES)


def _block_of(dev):
    x, y, c = dev
    return 4 * x + 2 * y + c


def _remote_copy(src, dst, sems, k, to):
    send_sems, recv_sems = sems
    return pltpu.make_async_remote_copy(src_ref=src, dst_ref=dst, send_sem=send_sems.at[k], recv_sem=recv_sems.at[k],
                                        device_id=to, device_id_type=pl.DeviceIdType.MESH)


class _Exchange:
    def __init__(self, arrays):
        self.arrays = list(arrays)
        self.n = len(self.arrays)
        self.specs = [pl.BlockSpec(memory_space=pl.ANY)] * self.n
        self.scratch = [pltpu.SemaphoreType.DMA((self.n * N_LINKS,)), pltpu.SemaphoreType.DMA((self.n * N_LINKS,)),
                        pltpu.SemaphoreType.DMA((self.n,))]

    def split(self, refs):
        n = self.n
        return refs[:n], refs[n:2 * n], (refs[2 * n], refs[2 * n + 1]), refs[2 * n + 2]


class _Gather(_Exchange):
    def out_shape(self):
        return [jax.ShapeDtypeStruct((N_DEV,) + a.shape, a.dtype) for a in self.arrays]

    def _plan(self, ins, outs, sems, local_sems):
        x, y, c = _place()
        me, sibling = (x, y, c), (x, y, 1 - c)
        chips = [(1 - x, y), (x, 1 - y), (1 - x, 1 - y)]

        def copy(a, k, block, to, mine=False):
            src = ins[a] if mine else outs[a].at[_block_of(block)]
            return _remote_copy(src, outs[a].at[_block_of(block)], sems, a * N_LINKS + k, to)

        local = [pltpu.make_async_copy(ins[a], outs[a].at[_block_of(me)], local_sems.at[a]) for a in range(self.n)]
        first = []
        for a in range(self.n):
            first.append(copy(a, 0, me, sibling, mine=True))
            first += [copy(a, 1 + j, me, (*chip, c), mine=True) for j, chip in enumerate(chips)]
        return me, sibling, chips, c, copy, local, first

    def start(self, ins, outs, sems, local_sems):
        *_, local, first = self._plan(ins, outs, sems, local_sems)
        for cp in local + first:
            cp.start()

    def finish(self, ins, outs, sems, local_sems):
        me, sibling, chips, c, copy, local, first = self._plan(ins, outs, sems, local_sems)
        passed = []
        for j, chip in enumerate(chips):
            for a in range(self.n):
                copy(a, 1 + j, (*chip, c), me).wait_recv()
                passed.append(copy(a, 4 + j, (*chip, c), sibling))
                passed[-1].start()
        for a in range(self.n):
            copy(a, 0, sibling, me).wait_recv()
            for j, chip in enumerate(chips):
                copy(a, 4 + j, (*chip, 1 - c), me).wait_recv()
        for cp in first + passed:
            cp.wait_send()
        for cp in local:
            cp.wait()


class _Scatter(_Exchange):
    def out_shape(self):
        return [jax.ShapeDtypeStruct(a.shape, a.dtype) for a in self.arrays]

    def _plan(self, ins, outs, sems, local_sems):
        x, y, c = _place()
        me = _block_of((x, y, c))

        def peer(r):
            return (1 - x if r & 4 else x, 1 - y if r & 2 else y, 1 - c if r & 1 else c)

        local = [pltpu.make_async_copy(ins[a].at[me], outs[a].at[me], local_sems.at[a]) for a in range(self.n)]
        sends = [_remote_copy(ins[a].at[_block_of(peer(r))], outs[a].at[me], sems, a * N_LINKS + r - 1, peer(r))
                 for a in range(self.n) for r in range(1, N_DEV)]
        arrivals = [_remote_copy(ins[a].at[me], outs[a].at[_block_of(peer(r))], sems, a * N_LINKS + r - 1, peer(r))
                    for a in range(self.n) for r in range(1, N_DEV)]
        return local, sends, arrivals

    def start(self, ins, outs, sems, local_sems):
        local, sends, _ = self._plan(ins, outs, sems, local_sems)
        for cp in local + sends:
            cp.start()

    def finish(self, ins, outs, sems, local_sems):
        local, sends, arrivals = self._plan(ins, outs, sems, local_sems)
        for cp in arrivals:
            cp.wait_recv()
        for cp in sends:
            cp.wait_send()
        for cp in local:
            cp.wait()


def _exchange(name, plan):
    def body(*refs):
        parts = plan.split(refs)
        plan.start(*parts)
        plan.finish(*parts)

    return pl.pallas_call(
        body, name=name,
        in_specs=plan.specs,
        out_specs=plan.specs,
        out_shape=plan.out_shape(),
        scratch_shapes=plan.scratch,
    )(*plan.arrays)


def _call_carrying(body, plan, operands, *, name, grid, in_specs, out_specs, out_shape, scratch_shapes, compiler_params):
    if plan is None:
        outs = pl.pallas_call(body, name=name, grid=grid, in_specs=in_specs, out_specs=out_specs, out_shape=out_shape,
                              scratch_shapes=scratch_shapes, compiler_params=compiler_params)(*operands)
        return outs, []
    n_i, n_o, n_s, k = len(in_specs), len(out_specs), len(scratch_shapes), plan.n

    def whole(*refs):
        cut = [n_i, n_i + k, n_i + k + n_o, n_i + 2 * k + n_o, n_i + 2 * k + n_o + n_s]
        own_in, ex_in, own_out, ex_out, own_scr, ex_scr = (refs[a:b] for a, b in zip([0] + cut, cut + [len(refs)]))
        parts = plan.split(ex_in + ex_out + ex_scr)
        first = last = True
        for axis, size in enumerate(grid):
            first = first & (pl.program_id(axis) == 0)
            last = last & (pl.program_id(axis) == size - 1)

        @pl.when(first)
        def _():
            plan.start(*parts)

        body(*own_in, *own_out, *own_scr)

        @pl.when(last)
        def _():
            plan.finish(*parts)

    outs = pl.pallas_call(
        whole, name=name, grid=grid,
        in_specs=list(in_specs) + plan.specs, out_specs=list(out_specs) + plan.specs,
        out_shape=list(out_shape) + plan.out_shape(), scratch_shapes=list(scratch_shapes) + plan.scratch,
        compiler_params=compiler_params,
    )(*operands, *plan.arrays)
    return outs[:n_o], outs[n_o:]


def _row_tile(rows, target=256):
    best = rows
    for cand in range(16, min(rows, target) + 1, 16):
        if rows % cand == 0:
            best = cand
    return best


def _sum_blocks(name, blocks, after=None):
    rows, width = blocks.shape[-2:]
    tm = _row_tile(rows)

    def body(x_ref, *rest):
        acc = x_ref[0].astype(F32)
        for d in range(1, N_DEV):
            acc = acc + x_ref[d].astype(F32)
        rest[-1][...] = acc

    ordered = [] if after is None else [after]
    return pl.pallas_call(
        body, name=name,
        grid=(rows // tm,),
        in_specs=[pl.BlockSpec((N_DEV, tm, width), lambda i: (0, i, 0))] + [pl.BlockSpec(memory_space=pl.ANY)] * len(ordered),
        out_specs=pl.BlockSpec((tm, width), lambda i: (i, 0)),
        out_shape=jax.ShapeDtypeStruct((rows, width), F32),
        compiler_params=pltpu.CompilerParams(dimension_semantics=("parallel",)),
    )(blocks, *ordered)


def _split_plan(src_ref, land_ref, sems):
    x, y, c = _place()
    me = _block_of((x, y, c))

    def peer(r):
        return (1 - x if r & 4 else x, 1 - y if r & 2 else y, 1 - c if r & 1 else c)

    sends = [_remote_copy(src_ref.at[_block_of(peer(r))], land_ref.at[me], sems, r - 1, peer(r)) for r in range(1, N_DEV)]
    arrivals = [_remote_copy(src_ref.at[me], land_ref.at[_block_of(peer(r))], sems, r - 1, peer(r)) for r in range(1, N_DEV)]
    return sends, arrivals


def _scatter_begin(name, blocks):
    def body(src_ref, land_ref, send_sems, recv_sems, src_thru, land_thru, token_ref):
        for cp in _split_plan(src_ref, land_ref, (send_sems, recv_sems))[0]:
            cp.start()
        token_ref[...] = jnp.zeros_like(token_ref)

    hbm, sem = pl.BlockSpec(memory_space=pltpu.HBM), pl.BlockSpec(memory_space=pltpu.SEMAPHORE)
    zone = pltpu.HBM(blocks.shape, blocks.dtype)
    *handles, token = pl.pallas_call(
        body, name=name,
        in_specs=(hbm, hbm),
        out_specs=(sem, sem, hbm, hbm, pl.BlockSpec(memory_space=pltpu.VMEM)),
        out_shape=(pltpu.SemaphoreType.DMA((N_LINKS,)), pltpu.SemaphoreType.DMA((N_LINKS,)), zone, zone,
                   jax.ShapeDtypeStruct((8, SLOT), F32)),
        input_output_aliases={0: 2, 1: 3},
        compiler_params=pltpu.CompilerParams(has_side_effects=pltpu.SideEffectType.DATAFLOW_SIDE_EFFECTING),
    )(pltpu.with_memory_space_constraint(blocks, pltpu.HBM),
      pltpu.with_memory_space_constraint(lax.empty(blocks.shape, blocks.dtype), pltpu.HBM))
    return handles, token


def _scatter_end(name, handles, after):
    send_sems, recv_sems, src, zone = handles

    def body(src_ref, land_ref, send_sems, recv_sems, after_ref, src_dead, got_ref):
        sends, arrivals = _split_plan(src_ref, land_ref, (send_sems, recv_sems))
        for cp in arrivals:
            cp.wait_recv()
        for cp in sends:
            cp.wait_send()

    hbm, sem = pl.BlockSpec(memory_space=pltpu.HBM), pl.BlockSpec(memory_space=pltpu.SEMAPHORE)
    sent, landed = pl.pallas_call(
        body, name=name,
        in_specs=(hbm, hbm, sem, sem, pl.BlockSpec(memory_space=pl.ANY)),
        out_specs=(hbm, hbm),
        out_shape=(pltpu.HBM(src.shape, src.dtype), pltpu.HBM(zone.shape, zone.dtype)),
        input_output_aliases={0: 0, 1: 1},
        compiler_params=pltpu.CompilerParams(has_side_effects=pltpu.SideEffectType.DATAFLOW_SIDE_EFFECTING),
    )(src, zone, send_sems, recv_sems, after)
    me = _block_of(_place())
    return lax.dynamic_update_slice_in_dim(landed, lax.dynamic_slice_in_dim(sent, me, 1, axis=0), me, axis=0)


def _adamw(name, w, g, m, v):
    def fn(rows, consts):
        wv, gv, mv, vv = rows
        m2 = ADAM_B1 * mv + (1.0 - ADAM_B1) * gv
        v2 = ADAM_B2 * vv + (1.0 - ADAM_B2) * jnp.square(gv)
        m_hat = m2 / (1.0 - ADAM_B1 ** ADAM_STEP)
        v_hat = v2 / (1.0 - ADAM_B2 ** ADAM_STEP)
        return [-ADAM_LR * (m_hat / (jnp.sqrt(v_hat) + ADAM_EPS) + ADAM_WD * wv), m2, v2], []

    return _rowwise(name, fn, [w, g, m, v], [], [(w.shape[1], F32)] * 3, tm=_row_tile(w.shape[0]))


ROW = 1024
FFN_NAMES = ("ffn1_w_gate", "ffn1_w_up", "ffn1_w_down", "ffn2_w_gate", "ffn2_w_up", "ffn2_w_down")
OTHER = {"w_in": "w_in_t", "mla_w_uq": "uq_t", "mla_w_ukv": "ukv_t", "w_out": "w_out"}
BY_COLUMNS = ("ffn1_w_gate", "ffn1_w_up", "ffn2_w_gate", "ffn2_w_up", "w_in", "mla_w_uq", "mla_w_ukv")
SMALL = {
    "ffn1_pre_g": (1024, 1024), "ffn1_post_g": (1024, 1024), "mix_pre_g": (1024, 1024), "mla_q_norm_g": (256, 256),
    "mla_kv_norm_g": (128, 128), "mla_out_g": (512, 512), "gdn_a_log": (8, 128), "gdn_dt_bias": (8, 128),
    "gdn_norm_g": (64, 128), "mix_post_g": (1024, 1024), "ffn2_pre_g": (1024, 1024), "ffn2_post_g": (1024, 1024),
}
CONV_SHAPE = (GDN_CONV, 3 * N_HEADS * GDN_D)
CONV_SHARD = (GDN_CONV, CONV_SHAPE[1] // N_DEV)
CONV_LANES = CONV_SHAPE[0] * CONV_SHAPE[1]
SMALL_ROWS = 8
REDUCE_ROWS = 16


def _pack_small(vecs, conv, rows):
    parts = [_pad_lanes(vecs[n].reshape(1, -1), 0, r) for n, (_, r) in SMALL.items()]
    parts.append(conv.reshape(1, -1))
    flat = jnp.concatenate(parts, axis=1)
    return _pad_lanes(flat, 0, rows * ROW).reshape(rows, ROW)


def _unpack_small(buf):
    flat = buf.reshape(1, -1)
    out, at = {}, 0
    for n, (w, r) in SMALL.items():
        out[n] = flat[:, at:at + w]
        at += r
    return out, flat[0, at:]


def kernel(x, positions, ffn1_pre_g, ffn1_w_gate, ffn1_w_up, ffn1_w_down, ffn1_post_g, mix_pre_g, w_in, mla_q_norm_g, mla_w_uq, mla_kv_norm_g, mla_w_ukv, mla_out_g, gdn_conv_w, gdn_a_log, gdn_dt_bias, gdn_norm_g, w_out, mix_post_g, ffn2_pre_g, ffn2_w_gate, ffn2_w_up, ffn2_w_down, ffn2_post_g, loss_target, m_ffn1_pre_g, m_ffn1_w_gate, m_ffn1_w_up, m_ffn1_w_down, m_ffn1_post_g, m_mix_pre_g, m_w_in, m_mla_q_norm_g, m_mla_w_uq, m_mla_kv_norm_g, m_mla_w_ukv, m_mla_out_g, m_gdn_conv_w, m_gdn_a_log, m_gdn_dt_bias, m_gdn_norm_g, m_w_out, m_mix_post_g, m_ffn2_pre_g, m_ffn2_w_gate, m_ffn2_w_up, m_ffn2_w_down, m_ffn2_post_g, v_ffn1_pre_g, v_ffn1_w_gate, v_ffn1_w_up, v_ffn1_w_down, v_ffn1_post_g, v_mix_pre_g, v_w_in, v_mla_q_norm_g, v_mla_w_uq, v_mla_kv_norm_g, v_mla_w_ukv, v_mla_out_g, v_gdn_conv_w, v_gdn_a_log, v_gdn_dt_bias, v_gdn_norm_g, v_w_out, v_mix_post_g, v_ffn2_pre_g, v_ffn2_w_gate, v_ffn2_w_up, v_ffn2_w_down, v_ffn2_post_g):
    given = dict(locals())
    order = ["ffn1_pre_g", "ffn1_w_gate", "ffn1_w_up", "ffn1_w_down", "ffn1_post_g", "mix_pre_g", "w_in", "mla_q_norm_g",
             "mla_w_uq", "mla_kv_norm_g", "mla_w_ukv", "mla_out_g", "gdn_conv_w", "gdn_a_log", "gdn_dt_bias", "gdn_norm_g",
             "w_out", "mix_post_g", "ffn2_pre_g", "ffn2_w_gate", "ffn2_w_up", "ffn2_w_down", "ffn2_post_g"]
    assert sorted(order) == sorted(list(FFN_NAMES) + list(OTHER) + list(SMALL) + ["gdn_conv_w"])

    def drop_depth(a):
        return a[0] if a.ndim == 3 else a

    wts = {n: drop_depth(given[n]) for n in order}
    mom = {n: drop_depth(given["m_" + n]) for n in order}
    var = {n: drop_depth(given["v_" + n]) for n in order}
    me = _block_of(_place())

    def wire(n):
        return (wts[n].T if n in BY_COLUMNS else wts[n]).astype(BF16)

    (ffn1,) = _exchange("gather_first", _Gather([jnp.stack([wire(n) for n in FFN_NAMES[:3]])]))
    mid = _Gather([wire(n) for n in ("w_in", "mla_w_uq", "mla_w_ukv")] + [wts["gdn_conv_w"]])
    late = _Gather([jnp.stack([wire(n) for n in FFN_NAMES[3:]]), wire("w_out")])
    full = {n: wts[n] for n in SMALL}
    full["ffn1"] = ffn1

    dx, grads, landed, begun, token = _local_step(x[0], positions[0], loss_target[0], full, mid, late)

    grad, outs = {}, {"delta": {}, "new_m": {}, "new_v": {}}

    def finish(n, blocks, after=None):
        total = _sum_blocks("sum_" + n, blocks, after=after)
        flip = n in BY_COLUMNS and wts[n].shape[1] % SLOT != 0
        turn = (lambda a: a.T) if flip else (lambda a: a)
        grad[n] = total.T if n in BY_COLUMNS else total
        new = _adamw("adamw_" + n, turn(wts[n]), total if flip else grad[n], turn(mom[n]), turn(var[n]))
        outs["delta"][n], outs["new_m"][n], outs["new_v"][n] = (turn(a) for a in new)
        return new[2]

    for n, blocks in landed.items():
        token = finish(n, blocks, after=token)
    small_handles = begun.pop("small")
    for n, handles in begun.items():
        token = finish(n, _scatter_end("scatter_" + n + "_end", handles, after=token))

    small_sum = _sum_blocks("sum_small", _scatter_end("reduce_small_end", small_handles, after=token))
    loss = small_sum[REDUCE_ROWS - 1, ROW - 1]
    small_grad, conv_grad_full = _unpack_small(small_sum)
    grad.update(small_grad)
    grad["gdn_conv_w"] = lax.dynamic_slice(conv_grad_full[:CONV_LANES].reshape(CONV_SHAPE), (0, me * CONV_SHARD[1]), CONV_SHARD)
    outs["grad"] = grad
    small = [_pack_small(s, s["gdn_conv_w"].reshape(-1), SMALL_ROWS) for s in (wts, grad, mom, var)]
    for kind, s in zip(("delta", "new_m", "new_v"), _adamw("adamw_small", *small)):
        vecs, conv = _unpack_small(s)
        outs[kind].update(vecs)
        outs[kind]["gdn_conv_w"] = conv[:CONV_SHARD[0] * CONV_SHARD[1]].reshape(CONV_SHARD)
    result = [loss, dx[None]]
    for kind in ("grad", "delta", "new_m", "new_v"):
        result += [outs[kind][n].reshape(given[n].shape) for n in order]
    return tuple(result)
```

```python
import jax
import jax.numpy as jnp
from jax import lax
from jax.experimental import pallas as pl
from jax.experimental.pallas import tpu as pltpu

F32 = jnp.float32
BF16 = jnp.bfloat16
HI = lax.Precision.HIGH

N_DEV = 8
N_HEADS = 8
SLOT = 128
MLA_Q_RANK = 256
MLA_KV_RANK = 128
MLA_NOPE = 64
MLA_ROPE = 32
MLA_V = 64
GDN_D = 64
GDN_CONV = 4
GDN_CHUNK = 64
ROPE_THETA = 10000.0
EPS = 1e-6
ADAM_LR, ADAM_B1, ADAM_B2, ADAM_EPS, ADAM_WD, ADAM_STEP = 0.001, 0.9, 0.999, 1e-08, 0.01, 10


def _dot(a, b, ca, cb, precision=None):
    lead = a.ndim - 2
    batch = tuple(range(lead))
    return lax.dot_general(a, b, (((lead + ca,), (lead + cb,)), (batch, batch)), precision=precision,
                           preferred_element_type=F32)


def _nn(a, b, precision=None):
    return _dot(a, b, 1, 0, precision)


def _nt(a, b, precision=None):
    return _dot(a, b, 1, 1, precision)


def _tn(a, b, precision=None):
    return _dot(a, b, 0, 0, precision)


def _sigmoid(x):
    return 1.0 / (1.0 + jnp.exp(-x))


def _silu(x):
    return x * _sigmoid(x)


def _rms(x, g, n):
    ms = jnp.sum(x * x, axis=-1, keepdims=True) * (1.0 / n)
    return x * lax.rsqrt(ms + EPS) * g


def _chunk_masks():
    c = GDN_CHUNK
    i = lax.broadcasted_iota(jnp.int32, (c, c), 0)
    j = lax.broadcasted_iota(jnp.int32, (c, c), 1)
    lower = i >= j
    strict = i > j
    eye = (i == j).astype(F32)
    blocks = []
    b = 1
    while b < c:
        same = (i // (2 * b)) == (j // (2 * b))
        blocks.append(same & ((i % (2 * b)) >= b) & ((j % (2 * b)) < b))
        b *= 2
    return lower, strict, eye, blocks


def _unit_lower_inverse(low, eye, blocks):
    t = eye - jnp.where(blocks[0], low, 0.0)
    for m in blocks[1:]:
        lo = jnp.where(m, low, 0.0)
        t = t - _nn(t, _nn(lo, t, HI), HI)
    return t


@jax.custom_vjp
def _known_inverse(low, tinv):
    return tinv


def _known_inverse_fwd(low, tinv):
    return tinv, tinv


def _known_inverse_bwd(tinv, dt):
    return -_tn(tinv, _nt(dt, tinv, HI), HI), jnp.zeros_like(tinv)


_known_inverse.defvjp(_known_inverse_fwd, _known_inverse_bwd)

_PRODUCTS = {"nn": _nn, "nt": _nt, "tn": _tn}


@jax.custom_vjp
def _known_nn(a, b, c):
    return c


@jax.custom_vjp
def _known_nt(a, b, c):
    return c


@jax.custom_vjp
def _known_tn(a, b, c):
    return c


def _known_fwd(a, b, c):
    return c, (a, b, c)


_known_nn.defvjp(_known_fwd, lambda r, dc: (_nt(dc, r[1], HI), _tn(r[0], dc, HI), jnp.zeros_like(r[2])))
_known_nt.defvjp(_known_fwd, lambda r, dc: (_nn(dc, r[1], HI), _tn(dc, r[0], HI), jnp.zeros_like(r[2])))
_known_tn.defvjp(_known_fwd, lambda r, dc: (_nt(r[1], dc, HI), _nn(r[0], dc, HI), jnp.zeros_like(r[2])))
_KNOWN = {"nn": _known_nn, "nt": _known_nt, "tn": _known_tn}
GDN_PRODUCTS = 8
GDN_KEPT = 2 + GDN_PRODUCTS


def _gdn_chunk(q, k, v, gc, bb, s, masks, known=None):
    lower, strict, eye, blocks = masks
    made = []

    def product(kind, a, b):
        c = _PRODUCTS[kind](a, b, HI) if known is None else _KNOWN[kind](a, b, known[1 + len(made)])
        made.append(c)
        return c

    qs = q * (GDN_D ** -0.5)
    gct = jnp.swapaxes(gc, -1, -2)
    decay = jnp.exp(jnp.where(lower, gc - gct, -1e30))
    kb = k * bb
    low = jnp.where(strict, product("nt", kb, k) * decay, 0.0)
    tinv = _unit_lower_inverse(low, eye, blocks) if known is None else _known_inverse(low, known[0])
    eg = jnp.exp(gc)
    w = product("nn", tinv, kb * eg)
    u = product("nn", tinv, v * bb)
    attn = product("nt", qs, k) * decay
    last = lax.broadcasted_iota(jnp.int32, gc.shape[-2:], 0) == GDN_CHUNK - 1
    g_end = jnp.sum(jnp.where(last, gc, 0.0), axis=-2, keepdims=True)
    k_dec = k * jnp.exp(g_end - gc)
    v_new = u - product("nn", w, s)
    o = product("nn", qs * eg, s) + product("nn", attn, v_new)
    s_new = s * jnp.exp(g_end) + product("tn", k_dec, v_new)
    assert len(made) == GDN_PRODUCTS
    return o, s_new, [tinv] + made


GDN_GROUP = 8
GDN_GROUPS = N_HEADS // GDN_GROUP


def _group_heads(ref):
    return jnp.stack([ref[:, pl.ds(j * SLOT, GDN_D)] for j in range(GDN_GROUP)])


def _ungroup_heads(ref, val):
    pad = jnp.zeros((GDN_CHUNK, SLOT - GDN_D), F32)
    for j in range(GDN_GROUP):
        ref[:, pl.ds(j * SLOT, GDN_D)] = val[j]
        ref[:, pl.ds(j * SLOT + GDN_D, SLOT - GDN_D)] = pad


def _gdn_fwd(qkv, gb, bb, carry=None):
    t = qkv.shape[0]
    n_chunks = t // GDN_CHUNK
    d = GDN_D

    def body(q_ref, k_ref, v_ref, g_ref, b_ref, o_ref, keep_ref, s_ref):
        @pl.when(pl.program_id(1) == 0)
        def _():
            s_ref[...] = jnp.zeros_like(s_ref)

        s = s_ref[...]
        keep_ref[:, 0, 0] = s
        o, s_new, made = _gdn_chunk(*[_group_heads(r) for r in (q_ref, k_ref, v_ref, g_ref, b_ref)], s, _chunk_masks())
        for i, val in enumerate(made):
            keep_ref[:, 0, 1 + i] = val
        s_ref[...] = s_new
        _ungroup_heads(o_ref, o)

    def spec(kind=0):
        return pl.BlockSpec((GDN_CHUNK, GDN_GROUP * SLOT), lambda h, n: (n, kind * GDN_GROUPS + h))

    return _call_carrying(
        body, carry, (qkv, qkv, qkv, gb, bb), name="gdn_fwd",
        grid=(GDN_GROUPS, n_chunks),
        in_specs=[spec(0), spec(1), spec(2), spec(), spec()],
        out_specs=[spec(), pl.BlockSpec((GDN_GROUP, 1, GDN_KEPT, d, d), lambda h, n: (h, n, 0, 0, 0))],
        out_shape=[jax.ShapeDtypeStruct((t, N_HEADS * SLOT), F32), jax.ShapeDtypeStruct((N_HEADS, n_chunks, GDN_KEPT, d, d), F32)],
        scratch_shapes=[pltpu.VMEM((GDN_GROUP, d, d), F32)],
        compiler_params=pltpu.CompilerParams(dimension_semantics=("arbitrary", "arbitrary")),
    )


def _gdn_bwd(qkv, gb, bb, keep, do, carry=None):
    t = qkv.shape[0]
    n_chunks = t // GDN_CHUNK
    d = GDN_D

    def body(q_ref, k_ref, v_ref, g_ref, b_ref, keep_ref, do_ref, dqkv_ref, dg_ref, db_ref, ds_ref):
        @pl.when(pl.program_id(1) == 0)
        def _():
            ds_ref[...] = jnp.zeros_like(ds_ref)

        masks = _chunk_masks()
        known = [keep_ref[:, 0, 1 + i] for i in range(GDN_KEPT - 1)]
        _, pull = jax.vjp(lambda *a: _gdn_chunk(*a, masks, known)[:2],
                          *[_group_heads(r) for r in (q_ref, k_ref, v_ref, g_ref, b_ref)], keep_ref[:, 0, 0])
        dq, dk, dv, dg, db, ds = pull((_group_heads(do_ref), ds_ref[...]))
        ds_ref[...] = ds
        for i, val in enumerate((dq, dk, dv)):
            _ungroup_heads(dqkv_ref.at[i], val)
        _ungroup_heads(dg_ref, dg)
        _ungroup_heads(db_ref, db)

    def spec(kind=0):
        return pl.BlockSpec((GDN_CHUNK, GDN_GROUP * SLOT), lambda h, n: (n_chunks - 1 - n, kind * GDN_GROUPS + h))

    return _call_carrying(
        body, carry, (qkv, qkv, qkv, gb, bb, keep, do), name="gdn_bwd",
        grid=(GDN_GROUPS, n_chunks),
        in_specs=[spec(0), spec(1), spec(2), spec(), spec(),
                  pl.BlockSpec((GDN_GROUP, 1, GDN_KEPT, d, d), lambda h, n: (h, n_chunks - 1 - n, 0, 0, 0)), spec()],
        out_specs=[pl.BlockSpec((3, GDN_CHUNK, GDN_GROUP * SLOT), lambda h, n: (0, n_chunks - 1 - n, h)), spec(), spec()],
        out_shape=[jax.ShapeDtypeStruct((3, t, N_HEADS * SLOT), F32)] + [jax.ShapeDtypeStruct((t, N_HEADS * SLOT), F32)] * 2,
        scratch_shapes=[pltpu.VMEM((GDN_GROUP, d, d), F32)],
        compiler_params=pltpu.CompilerParams(dimension_semantics=("arbitrary", "arbitrary")),
    )


def _rowwise(name, fn, rows, consts, outs, sums=(), tm=512):
    rows = [x if isinstance(x, tuple) else (x, x.shape[1], 0) for x in rows]
    t = rows[0][0].shape[0]
    tm = min(tm, t)
    steps = t // tm
    n_r, n_c, n_o, n_s = len(rows), len(consts), len(outs), len(sums)

    def window(width, block):
        return pl.BlockSpec((tm, width), lambda i: (i, block))

    def body(*refs):
        r, c = refs[:n_r], refs[n_r:n_r + n_c]
        o, s = refs[n_r + n_c:n_r + n_c + n_o], refs[n_r + n_c + n_o:]
        vals, tot = fn([x[...] for x in r], [x[...] for x in c])
        for ref, val in zip(o, vals):
            ref[...] = val.astype(ref.dtype)
        if n_s:
            @pl.when(pl.program_id(0) == 0)
            def _():
                for ref in s:
                    ref[...] = jnp.zeros_like(ref)

            for ref, val in zip(s, tot):
                ref[...] += val

    return pl.pallas_call(
        body, name=name,
        grid=(steps,),
        in_specs=[window(w, b) for _, w, b in rows] + [pl.BlockSpec(x.shape, lambda i: (0, 0)) for x in consts],
        out_specs=[pl.BlockSpec((tm, w), lambda i: (i, 0)) for w, _ in outs]
        + [pl.BlockSpec((1, w), lambda i: (0, 0)) for w in sums],
        out_shape=[jax.ShapeDtypeStruct((t, w), dt) for w, dt in outs]
        + [jax.ShapeDtypeStruct((1, w), F32) for w in sums],
        compiler_params=pltpu.CompilerParams(dimension_semantics=("arbitrary",)),
    )(*[x for x, _, _ in rows], *consts)


def _tile(dim, target):
    if dim <= target:
        return dim
    best = None
    for cand in range(128, target + 1, 128):
        if dim % cand == 0:
            best = cand
    assert best is not None, (dim, target)
    return best


def _matmul(name, a, b, mode, out_dtype=F32, tm=1024, tn=1024, tk=2048, after=None):
    if mode == "nn":
        (m, k), n = a.shape, b.shape[1]
    elif mode == "nt":
        (m, k), n = a.shape, b.shape[0]
    else:
        (k, m), n = a.shape, b.shape[1]
    tm, tn, tk = _tile(m, tm), _tile(n, tn), _tile(k, tk)
    k_steps = k // tk
    product = {"nn": _nn, "nt": _nt, "tn": _tn}[mode]

    def body(a_ref, b_ref, *rest):
        o_ref, acc_ref = rest[-2:]
        part = product(a_ref[...].astype(BF16), b_ref[...].astype(BF16))
        if k_steps == 1:
            o_ref[...] = part.astype(o_ref.dtype)
        else:
            kk = pl.program_id(2)

            @pl.when(kk == 0)
            def _():
                acc_ref[...] = part

            @pl.when(kk > 0)
            def _():
                acc_ref[...] += part

            @pl.when(kk == k_steps - 1)
            def _():
                o_ref[...] = acc_ref[...].astype(o_ref.dtype)

    a_spec = pl.BlockSpec((tk, tm), lambda i, j, kk: (kk, i)) if mode == "tn" else pl.BlockSpec((tm, tk), lambda i, j, kk: (i, kk))
    b_spec = pl.BlockSpec((tn, tk), lambda i, j, kk: (j, kk)) if mode == "nt" else pl.BlockSpec((tk, tn), lambda i, j, kk: (kk, j))
    ordered = [] if after is None else [after]
    return pl.pallas_call(
        body, name=name,
        grid=(m // tm, n // tn, k_steps),
        in_specs=[a_spec, b_spec] + [pl.BlockSpec(memory_space=pl.ANY)] * len(ordered),
        out_specs=pl.BlockSpec((tm, tn), lambda i, j, kk: (i, j)),
        out_shape=jax.ShapeDtypeStruct((m, n), out_dtype),
        scratch_shapes=[pltpu.VMEM((tm, tn) if k_steps > 1 else (8, 128), F32)],
        compiler_params=pltpu.CompilerParams(dimension_semantics=("parallel", "parallel", "arbitrary")),
    )(a, b, *ordered)


FFN_TM = 512
FFN_BWD_TM = 256
FFN_BLOCKS = 4
FFN_GATE, FFN_UP, FFN_DOWN = 0, 1, 2


def _ffn_weight_specs(ffn_w, first):
    _, _, rows, dm = ffn_w.shape

    def spec(k):
        return pl.BlockSpec((FFN_BLOCKS, None, rows, dm), lambda i, j: (j, first + k, 0, 0))

    return [spec(FFN_GATE), spec(FFN_UP), spec(FFN_DOWN)], FFN_BLOCKS * rows


def _ffn_fwd(name, x, g_pre, ffn_w, first, g_post, carry=None):
    t, dm = x.shape
    tm = min(FFN_TM, t)
    w_specs, tf = _ffn_weight_specs(ffn_w, first)
    f_steps = N_DEV // FFN_BLOCKS

    def body(x_ref, gpre_ref, wg_ref, wu_ref, wd_ref, gpost_ref, h_ref, y_ref, hg_ref, hu_ref, xn_ref, acc_ref):
        j = pl.program_id(1)

        @pl.when(j == 0)
        def _():
            xn_ref[...] = _rms(x_ref[...], gpre_ref[...], dm).astype(BF16)
            acc_ref[...] = jnp.zeros_like(acc_ref)

        xn = xn_ref[...]
        wg, wu, wd = (r[...].reshape(tf, dm) for r in (wg_ref, wu_ref, wd_ref))
        hg, hu = _nt(xn, wg), _nt(xn, wu)
        hg_ref[...] = hg.astype(BF16)
        hu_ref[...] = hu.astype(BF16)
        a = _silu(hg) * hu
        acc_ref[...] += _nn(a.astype(BF16), wd)

        @pl.when(j == f_steps - 1)
        def _():
            h = acc_ref[...]
            h_ref[...] = h
            y_ref[...] = x_ref[...] + 0.5 * _rms(h, gpost_ref[...], dm)

    row = pl.BlockSpec((tm, dm), lambda i, j: (i, 0))
    vec = pl.BlockSpec((1, dm), lambda i, j: (0, 0))
    wide = pl.BlockSpec((tm, tf), lambda i, j: (i, j))
    return _call_carrying(
        body, carry, (x, g_pre, ffn_w, ffn_w, ffn_w, g_post), name=name,
        grid=(t // tm, f_steps),
        in_specs=[row, vec, *w_specs, vec],
        out_specs=[row, row, wide, wide],
        out_shape=[jax.ShapeDtypeStruct((t, dm), F32)] * 2 + [jax.ShapeDtypeStruct((t, f_steps * tf), BF16)] * 2,
        scratch_shapes=[pltpu.VMEM((tm, dm), BF16), pltpu.VMEM((tm, dm), F32)],
        compiler_params=pltpu.CompilerParams(dimension_semantics=("arbitrary", "arbitrary")),
    )


def _ffn_bwd(name, x, h, hg, hu, dy, g_pre, ffn_w, first, g_post, carry=None):
    t, dm = x.shape
    tm = min(FFN_BWD_TM, t)
    w_specs, tf = _ffn_weight_specs(ffn_w, first)
    f_steps = N_DEV // FFN_BLOCKS
    f = f_steps * tf

    def post(hv, g):
        return 0.5 * _rms(hv, g, dm)

    def pre(xv, g):
        return _rms(xv, g, dm)

    def body(x_ref, h_ref, dy_ref, hg_ref, hu_ref, gpre_ref, wg_ref, wu_ref, wd_ref, gpost_ref,
             dx_ref, xn_ref, dh_ref, a_ref, dhg_ref, dhu_ref, dgpre_ref, dgpost_ref, acc_ref):
        i, j = pl.program_id(0), pl.program_id(1)

        @pl.when((i == 0) & (j == 0))
        def _():
            dgpre_ref[...] = jnp.zeros_like(dgpre_ref)
            dgpost_ref[...] = jnp.zeros_like(dgpost_ref)

        @pl.when(j == 0)
        def _():
            xn_ref[...] = pre(x_ref[...], gpre_ref[...]).astype(BF16)
            _, pull = jax.vjp(post, h_ref[...], gpost_ref[...])
            dh, dg = pull(dy_ref[...])
            dh_ref[...] = dh.astype(BF16)
            dgpost_ref[...] += dg
            acc_ref[...] = jnp.zeros_like(acc_ref)

        wg, wu, wd = (r[...].reshape(tf, dm) for r in (wg_ref, wu_ref, wd_ref))
        hg, hu = hg_ref[...].astype(F32), hu_ref[...].astype(F32)
        da = _nt(dh_ref[...], wd)
        sig = _sigmoid(hg)
        act = hg * sig
        dhu = (da * act).astype(BF16)
        dhg = (da * hu * (sig * (1.0 + hg * (1.0 - sig)))).astype(BF16)
        a_ref[...] = (act * hu).astype(BF16)
        dhg_ref[...] = dhg
        dhu_ref[...] = dhu
        acc_ref[...] += _nn(dhg, wg) + _nn(dhu, wu)

        @pl.when(j == f_steps - 1)
        def _():
            _, pull = jax.vjp(pre, x_ref[...], gpre_ref[...])
            dx, dg = pull(acc_ref[...])
            dx_ref[...] = dy_ref[...] + dx
            dgpre_ref[...] += dg

    row = pl.BlockSpec((tm, dm), lambda i, j: (i, 0))
    vec = pl.BlockSpec((1, dm), lambda i, j: (0, 0))
    wide = pl.BlockSpec((tm, tf), lambda i, j: (i, j))
    return _call_carrying(
        body, carry, (x, h, dy, hg, hu, g_pre, ffn_w, ffn_w, ffn_w, g_post), name=name,
        grid=(t // tm, f_steps),
        in_specs=[row, row, row, wide, wide, vec, *w_specs, vec],
        out_specs=[row, row, row, wide, wide, wide, vec, vec],
        out_shape=[jax.ShapeDtypeStruct((t, dm), F32), jax.ShapeDtypeStruct((t, dm), BF16), jax.ShapeDtypeStruct((t, dm), BF16),
                   jax.ShapeDtypeStruct((t, f), BF16), jax.ShapeDtypeStruct((t, f), BF16), jax.ShapeDtypeStruct((t, f), BF16),
                   jax.ShapeDtypeStruct((1, dm), F32), jax.ShapeDtypeStruct((1, dm), F32)],
        scratch_shapes=[pltpu.VMEM((tm, dm), F32)],
        compiler_params=pltpu.CompilerParams(dimension_semantics=("arbitrary", "arbitrary")),
    )


ATT_T = 512
ATT_GROUP = 4
ATT_GROUP_FWD = 8
ATT_SCALE = (MLA_NOPE + MLA_ROPE) ** -0.5


def _stack_slots(ref, group):
    return jnp.stack([ref[:, pl.ds(j * SLOT, SLOT)] for j in range(group)])


def _unstack_slots(ref, val):
    for j in range(val.shape[0]):
        ref[:, pl.ds(j * SLOT, SLOT)] = val[j].astype(ref.dtype)


def _scores(q, k, diagonal):
    s = _nt(q, k) * ATT_SCALE
    if diagonal:
        row = lax.broadcasted_iota(jnp.int32, s.shape[1:], 0)
        col = lax.broadcasted_iota(jnp.int32, s.shape[1:], 1)
        s = jnp.where(col <= row, s, -1e30)
    return s


def _attn_pairs(steps, q_major):
    pairs = ([(qi, ki) for qi in range(steps) for ki in range(qi + 1)] if q_major
             else [(qi, ki) for ki in range(steps) for qi in range(ki, steps)])
    return jnp.array([p[0] for p in pairs], jnp.int32), jnp.array([p[1] for p in pairs], jnp.int32)


def _attn_specs(tile, group):
    width = group * SLOT
    return (pl.BlockSpec((tile, width), lambda h, p, qt, kt: (qt[p], h)),
            pl.BlockSpec((tile, width), lambda h, p, qt, kt: (kt[p], h)))


def _attn_fwd(q, k, v):
    t = q.shape[0]
    tile = min(ATT_T, t)
    steps = t // tile
    g = ATT_GROUP_FWD

    strip = min(SLOT, tile)

    def body(qt_ref, kt_ref, q_ref, k_ref, v_ref, o_ref, lse_ref, m_ref, l_ref, alpha_ref, acc_ref, s_ref, p_ref):
        qi, ki = qt_ref[pl.program_id(1)], kt_ref[pl.program_id(1)]

        @pl.when(ki == 0)
        def _():
            m_ref[...] = jnp.full_like(m_ref, -1e30)
            l_ref[...] = jnp.zeros_like(l_ref)
            acc_ref[...] = jnp.zeros_like(acc_ref)

        def step(diagonal):
            s_ref[...] = _nt(_stack_slots(k_ref, g), _stack_slots(q_ref, g))
            for j in range(tile // strip):
                c = pl.ds(j * strip, strip)
                s = s_ref[:, :, c] * ATT_SCALE
                if diagonal:
                    key = lax.broadcasted_iota(jnp.int32, s.shape[1:], 0)
                    query = lax.broadcasted_iota(jnp.int32, s.shape[1:], 1) + j * strip
                    s = jnp.where(key <= query, s, -1e30)
                m_old = m_ref[:, :, c]
                m_new = jnp.maximum(m_old, jnp.max(s, axis=1, keepdims=True))
                p = jnp.exp(s - m_new)
                alpha = jnp.exp(m_old - m_new)
                l_ref[:, :, c] = alpha * l_ref[:, :, c] + jnp.sum(p, axis=1, keepdims=True)
                alpha_ref[:, :, c] = alpha
                m_ref[:, :, c] = m_new
                p_ref[:, :, c] = p.astype(BF16)
            acc_ref[...] = acc_ref[...] * alpha_ref[...] + _tn(_stack_slots(v_ref, g), p_ref[...])

        @pl.when(ki < qi)
        def _():
            step(False)

        @pl.when(ki == qi)
        def _():
            step(True)
            out = acc_ref[...] / l_ref[...]
            lse = jnp.broadcast_to(m_ref[...] + jnp.log(l_ref[...]), out.shape)
            for j in range(g):
                o_ref[:, pl.ds(j * SLOT, SLOT)] = out[j].T
                lse_ref[:, pl.ds(j * SLOT, SLOT)] = lse[j].T

    q_spec, k_spec = _attn_specs(tile, g)
    tables = _attn_pairs(steps, True)
    return pl.pallas_call(
        body, name="attn_fwd",
        grid_spec=pltpu.PrefetchScalarGridSpec(
            num_scalar_prefetch=2, grid=(N_HEADS // g, tables[0].shape[0]),
            in_specs=[q_spec, k_spec, k_spec], out_specs=[q_spec, q_spec],
            scratch_shapes=[pltpu.VMEM((g, 1, tile), F32), pltpu.VMEM((g, 1, tile), F32), pltpu.VMEM((g, 1, tile), F32),
                            pltpu.VMEM((g, SLOT, tile), F32), pltpu.VMEM((g, tile, tile), F32), pltpu.VMEM((g, tile, tile), BF16)]),
        out_shape=[jax.ShapeDtypeStruct((t, N_HEADS * SLOT), F32)] * 2,
        compiler_params=pltpu.CompilerParams(dimension_semantics=("parallel", "arbitrary")),
    )(*tables, q, k, v)


def _attn_grad_scores(q, k, v, do, lse_ref, delta_ref, diagonal):
    g = ATT_GROUP
    p = jnp.exp(_scores(q, k, diagonal) - _stack_slots(lse_ref, g)[:, :, 0:1])
    dp = _nt(do, v)
    return p, p * (dp - _stack_slots(delta_ref, g)[:, :, 0:1]) * ATT_SCALE


def _attn_bwd(q, k, v, do, lse, delta):
    t = q.shape[0]
    tile = min(ATT_T, t)
    steps = t // tile
    g = ATT_GROUP

    def body(qt_ref, kt_ref, q_ref, k_ref, v_ref, do_ref, lse_ref, delta_ref, dq_ref, dk_ref, dv_ref, dk_acc, dv_acc):
        qi, ki = qt_ref[pl.program_id(1)], kt_ref[pl.program_id(1)]

        @pl.when(pl.program_id(1) == 0)
        def _():
            dq_ref[...] = jnp.zeros_like(dq_ref)

        def step(diagonal):
            qq, kk = _stack_slots(q_ref, g), _stack_slots(k_ref, g)
            do_b = _stack_slots(do_ref, g).astype(BF16)
            p, ds = _attn_grad_scores(qq, kk, _stack_slots(v_ref, g), do_b, lse_ref, delta_ref, diagonal)
            ds = ds.astype(BF16)
            dv_acc[...] += _tn(p.astype(BF16), do_b)
            dk_acc[...] += _tn(ds, qq)
            dq = _nn(ds, kk)
            rows = pl.ds(pl.multiple_of(qi * tile, tile), tile)
            for j in range(g):
                dq_ref[rows, pl.ds(j * SLOT, SLOT)] += dq[j]

        @pl.when(qi == ki)
        def _():
            dk_acc[...] = jnp.zeros_like(dk_acc)
            dv_acc[...] = jnp.zeros_like(dv_acc)
            step(True)

        @pl.when(qi > ki)
        def _():
            step(False)

        @pl.when(qi == steps - 1)
        def _():
            _unstack_slots(dk_ref, dk_acc[...])
            _unstack_slots(dv_ref, dv_acc[...])

    q_spec, k_spec = _attn_specs(tile, g)
    tables = _attn_pairs(steps, False)
    return pl.pallas_call(
        body, name="attn_bwd",
        grid_spec=pltpu.PrefetchScalarGridSpec(
            num_scalar_prefetch=2, grid=(N_HEADS // g, tables[0].shape[0]),
            in_specs=[q_spec, k_spec, k_spec, q_spec, q_spec, q_spec],
            out_specs=[pl.BlockSpec((t, g * SLOT), lambda h, p, qt, kt: (0, h)), k_spec, k_spec],
            scratch_shapes=[pltpu.VMEM((g, tile, SLOT), F32), pltpu.VMEM((g, tile, SLOT), F32)]),
        out_shape=[jax.ShapeDtypeStruct((t, N_HEADS * SLOT), F32)] * 3,
        compiler_params=pltpu.CompilerParams(dimension_semantics=("parallel", "arbitrary")),
    )(*tables, q, k, v, do, lse, delta)


CONV_PAD = 8


def _fill_padded(ref, val):
    t = val.shape[0]
    zeros = jnp.zeros((CONV_PAD, val.shape[1]), val.dtype)
    ref[pl.ds(0, CONV_PAD)] = zeros
    ref[pl.ds(CONV_PAD + t, CONV_PAD)] = zeros
    ref[pl.ds(CONV_PAD, t)] = val


def _shifted(ref, s):
    return ref[pl.ds(CONV_PAD - s, ref.shape[0] - 2 * CONV_PAD)]


def _l2norm(x):
    return x * lax.rsqrt(jnp.sum(x * x, axis=-1, keepdims=True) + EPS)


def _conv_pre(x_pad, w):
    y = w[GDN_CONV - 1:GDN_CONV, :] * _shifted(x_pad, 0)
    for s in range(1, GDN_CONV):
        y = y + w[GDN_CONV - 1 - s:GDN_CONV - s, :] * _shifted(x_pad, s)
    return y


def _gdn_conv_fwd(x, w):
    t, width = x.shape

    def body(x_ref, w_ref, o_ref, x_pad):
        _fill_padded(x_pad, x_ref[...])
        act = _silu(_conv_pre(x_pad, w_ref[...]))
        normed = pl.program_id(0) < 2 * N_HEADS
        o_ref[...] = jnp.where(normed, _l2norm(act), act)

    return pl.pallas_call(
        body, name="gdn_conv_fwd",
        grid=(width // SLOT,),
        in_specs=[pl.BlockSpec((t, SLOT), lambda j: (0, j)), pl.BlockSpec((GDN_CONV, SLOT), lambda j: (0, j))],
        out_specs=pl.BlockSpec((t, SLOT), lambda j: (0, j)),
        out_shape=jax.ShapeDtypeStruct((t, width), F32),
        scratch_shapes=[pltpu.VMEM((t + 2 * CONV_PAD, SLOT), F32)],
        compiler_params=pltpu.CompilerParams(dimension_semantics=("parallel",)),
    )(x, w)


def _gdn_conv_bwd(x, w, dout):
    t, width = x.shape

    def body(x_ref, w_ref, do_ref, dx_ref, dw_ref, x_pad, dy_pad):
        wv = w_ref[...]
        _fill_padded(x_pad, x_ref[...])
        y = _conv_pre(x_pad, wv)
        sig = _sigmoid(y)
        act = y * sig
        _, pull = jax.vjp(_l2norm, act)
        normed = pl.program_id(0) < 2 * N_HEADS
        dact = jnp.where(normed, pull(do_ref[0])[0], do_ref[0])
        dy = dact * (sig * (1.0 + y * (1.0 - sig)))
        _fill_padded(dy_pad, dy)
        dx = wv[GDN_CONV - 1:GDN_CONV, :] * dy
        for s in range(1, GDN_CONV):
            dx = dx + wv[GDN_CONV - 1 - s:GDN_CONV - s, :] * _shifted(dy_pad, -s)
        dx_ref[...] = dx.astype(BF16)
        for s in range(GDN_CONV):
            dw_ref[GDN_CONV - 1 - s:GDN_CONV - s, :] = jnp.sum(dy * _shifted(x_pad, s), axis=0, keepdims=True)

    col = pl.BlockSpec((t, SLOT), lambda j: (0, j))
    tap = pl.BlockSpec((GDN_CONV, SLOT), lambda j: (0, j))
    return pl.pallas_call(
        body, name="gdn_conv_bwd",
        grid=(width // SLOT,),
        in_specs=[col, tap, pl.BlockSpec((1, t, SLOT), lambda j: (j // N_HEADS, 0, j % N_HEADS))],
        out_specs=[col, tap],
        out_shape=[jax.ShapeDtypeStruct((t, width), BF16), jax.ShapeDtypeStruct((GDN_CONV, width), F32)],
        scratch_shapes=[pltpu.VMEM((t + 2 * CONV_PAD, SLOT), F32)] * 2,
        compiler_params=pltpu.CompilerParams(dimension_semantics=("parallel",)),
    )(x, w, dout)


def _softplus(x):
    e = jnp.exp(-jnp.abs(x))
    u = 1.0 + e
    log1p = jnp.where(u == 1.0, e, jnp.log(u) * e / jnp.where(u == 1.0, 1.0, u - 1.0))
    return jnp.maximum(x, 0.0) + log1p


def _chunk_running_sum(x, reverse=False):
    tm = x.shape[0]
    at = lax.broadcasted_iota(jnp.int32, x.shape, 0) % GDN_CHUNK
    step = 1
    while step < GDN_CHUNK:
        if reverse:
            x = x + jnp.where(at < GDN_CHUNK - step, pltpu.roll(x, tm - step, 0), 0.0)
        else:
            x = x + jnp.where(at >= step, pltpu.roll(x, step, 0), 0.0)
        step *= 2
    return x


def _gates_fwd(ab, a_log, dt_bias):
    def fn(rows, consts):
        (abv,), (alog, dtb) = rows, consts
        g = _chunk_running_sum(-jnp.exp(alog) * _softplus(abv + dtb))
        beta = _sigmoid(abv)
        shape = (abv.shape[0], SLOT)
        g_slots = [jnp.broadcast_to(g[:, h:h + 1], shape) for h in range(N_HEADS)]
        b_slots = [jnp.broadcast_to(beta[:, N_HEADS + h:N_HEADS + h + 1], shape) for h in range(N_HEADS)]
        return [jnp.concatenate(g_slots, axis=1), jnp.concatenate(b_slots, axis=1)], []

    width = N_HEADS * SLOT
    return _rowwise("gdn_gates_fwd", fn, [ab], [a_log, dt_bias], [(width, F32), (width, F32)])


def _gates_bwd(ab, a_log, dt_bias, dg, dbeta):
    def fn(rows, consts):
        (abv, dgv, dbv), (alog, dtb) = rows, consts
        lane = lax.broadcasted_iota(jnp.int32, abv.shape, 1)
        dg_tok = jnp.zeros_like(abv)
        db_tok = jnp.zeros_like(abv)
        for h in range(N_HEADS):
            dg_tok = dg_tok + jnp.where(lane == h, jnp.sum(dgv[:, h * SLOT:(h + 1) * SLOT], axis=1, keepdims=True), 0.0)
            db_tok = db_tok + jnp.where(lane == N_HEADS + h, jnp.sum(dbv[:, h * SLOT:(h + 1) * SLOT], axis=1, keepdims=True), 0.0)
        dg_tok = _chunk_running_sum(dg_tok, reverse=True)
        xa = abv + dtb
        g = -jnp.exp(alog) * _softplus(xa)
        da = dg_tok * (-jnp.exp(alog)) * _sigmoid(xa)
        beta = _sigmoid(abv)
        dab = jnp.where(lane < N_HEADS, da, db_tok * beta * (1.0 - beta))
        dab = jnp.where(lane < 2 * N_HEADS, dab, 0.0)
        d_alog = jnp.sum(jnp.where(lane < N_HEADS, dg_tok * g, 0.0), axis=0, keepdims=True)
        d_dtb = jnp.sum(jnp.where(lane < N_HEADS, da, 0.0), axis=0, keepdims=True)
        return [dab], [d_alog, d_dtb]

    return _rowwise("gdn_gates_bwd", fn, [ab, dg, dbeta], [a_log, dt_bias], [(SLOT, F32)], sums=[SLOT, SLOT])


ROPE_HALF = MLA_ROPE // 2


def _rope_tables(positions):
    freqs = ROPE_THETA ** (-jnp.arange(ROPE_HALF, dtype=F32) / ROPE_HALF)
    ang = positions.astype(F32)[:, None] * freqs
    cos, sin = jnp.cos(ang), jnp.sin(ang)
    t = positions.shape[0]
    ones, zeros = jnp.ones((t, MLA_NOPE), F32), jnp.zeros((t, MLA_NOPE), F32)
    tail = jnp.zeros((t, SLOT - MLA_NOPE - MLA_ROPE), F32)
    half0 = jnp.zeros((t, ROPE_HALF), F32)
    same = jnp.concatenate([ones, cos, cos, tail], axis=1)
    from_low = jnp.concatenate([zeros, half0, sin, tail], axis=1)
    from_high = jnp.concatenate([zeros, -sin, half0, tail], axis=1)
    return same, from_low, from_high


def _rope(x, tabs):
    same, from_low, from_high = tabs
    width = x.shape[1]
    return x * same + pltpu.roll(x, ROPE_HALF, 1) * from_low + pltpu.roll(x, width - ROPE_HALF, 1) * from_high


def _rope_transposed(dy, tabs):
    same, from_low, from_high = tabs
    width = dy.shape[1]
    return dy * same + pltpu.roll(dy * from_low, width - ROPE_HALF, 1) + pltpu.roll(dy * from_high, ROPE_HALF, 1)


def _tile_slots(tab):
    return jnp.concatenate([tab] * N_HEADS, axis=1)


A_WIDTH = MLA_Q_RANK + MLA_KV_RANK + 2 * SLOT
A_KPE = MLA_Q_RANK + MLA_KV_RANK
A_AB = A_KPE + SLOT
WIDE = N_HEADS * SLOT


def _mla_front_fwd(proj_a, tabs, g_q, g_kv, w_uq, w_kv):
    def fn(rows, consts):
        pa, *tb = rows
        gq, gkv, wuq, wkv = consts
        cqn = _rms(pa[:, :MLA_Q_RANK], gq, MLA_Q_RANK).astype(BF16)
        ckvn = _rms(pa[:, MLA_Q_RANK:A_KPE], gkv, MLA_KV_RANK).astype(BF16)
        kv = _nt(ckvn, wkv)
        q = _rope(_nt(cqn, wuq), [_tile_slots(x) for x in tb])
        k = kv[:, :WIDE] + _tile_slots(_rope(pa[:, A_KPE:A_AB], tb))
        return [cqn, ckvn, q, k, kv[:, WIDE:]], []

    return _rowwise("mla_front_fwd", fn, [proj_a, *tabs], [g_q, g_kv, w_uq, w_kv],
                    [(MLA_Q_RANK, BF16), (MLA_KV_RANK, BF16)] + [(WIDE, BF16)] * 3)


def _mla_front_bwd(proj_a, tabs, g_q, g_kv, w_uq, w_kv, dq, dk, dv, dab):
    def fn(rows, consts):
        pa, t0, t1, t2, dqv, dkv, dvv, da = rows
        gq, gkv, wuq, wkv = consts
        tb = (t0, t1, t2)
        dq_p = _rope_transposed(dqv, [_tile_slots(x) for x in tb]).astype(BF16)
        dkv_p = jnp.concatenate([dkv, dvv], axis=1).astype(BF16)
        dkpe = dkv[:, :SLOT]
        for h in range(1, N_HEADS):
            dkpe = dkpe + dkv[:, h * SLOT:(h + 1) * SLOT]
        _, pull_q = jax.vjp(lambda x, g: _rms(x, g, MLA_Q_RANK), pa[:, :MLA_Q_RANK], gq)
        _, pull_kv = jax.vjp(lambda x, g: _rms(x, g, MLA_KV_RANK), pa[:, MLA_Q_RANK:A_KPE], gkv)
        dcq, dgq = pull_q(_nn(dq_p, wuq))
        dckv, dgkv = pull_kv(_nn(dkv_p, wkv))
        return [jnp.concatenate([dcq, dckv, _rope_transposed(dkpe, tb), da], axis=1), dq_p, dkv_p], [dgq, dgkv]

    return _rowwise("mla_front_bwd", fn, [proj_a, *tabs, dq, dk, dv, dab], [g_q, g_kv, w_uq, w_kv],
                    [(A_WIDTH, BF16), (WIDE, BF16), (2 * WIDE, BF16)], sums=[MLA_Q_RANK, MLA_KV_RANK])


def _slot_sum(x):
    parts = [jnp.broadcast_to(jnp.sum(x[:, h * SLOT:(h + 1) * SLOT], axis=1, keepdims=True), (x.shape[0], SLOT))
             for h in range(N_HEADS)]
    return jnp.concatenate(parts, axis=1)


def _mix_join(o_mla, o_gdn, gate, g_mla, g_gdn):
    mla = _rms(o_mla, g_mla, N_HEADS * MLA_V)
    gdn = o_gdn * lax.rsqrt(_slot_sum(o_gdn * o_gdn) * (1.0 / GDN_D) + EPS) * g_gdn * _silu(gate)
    return mla, gdn


MIX_TM = 256


def _mix_fwd(o_mla, o_gdn, gate, x, g_mla, g_gdn, w_out, g_post):
    dm = x.shape[1]

    def fn(rows, consts):
        om, og, gt, xv = rows
        gm, gg, wo, gp = consts
        cat = jnp.concatenate(_mix_join(om, og, gt, gm, gg), axis=1).astype(BF16)
        mixed = _nn(cat, wo)
        return [cat, mixed, xv + _rms(mixed, gp, dm)], []

    return _rowwise("mix_fwd", fn, [o_mla, o_gdn, gate, x], [g_mla, g_gdn, w_out, g_post],
                    [(2 * WIDE, BF16), (dm, F32), (dm, F32)], tm=MIX_TM)


def _mix_bwd(o_mla, o_gdn, gate, mixed, dy, g_mla, g_gdn, w_out, g_post):
    dm = mixed.shape[1]

    def fn(rows, consts):
        om, og, gt, mx, dyv = rows
        gm, gg, wo, gp = consts
        _, pull_post = jax.vjp(lambda hv, gv: _rms(hv, gv, dm), mx, gp)
        dmixed, dgp = pull_post(dyv)
        dmixed = dmixed.astype(BF16)
        dc = _nt(dmixed, wo)
        _, pull = jax.vjp(lambda x, g: _rms(x, g, N_HEADS * MLA_V), om, gm)
        dom, dgm = pull(dc[:, :WIDE])
        dn_out = dc[:, WIDE:]
        r = lax.rsqrt(_slot_sum(og * og) * (1.0 / GDN_D) + EPS)
        sig = _sigmoid(gt)
        normed = og * r
        dn = dn_out * gg * (gt * sig)
        dog = r * dn - normed * (r * r) * _slot_sum(dn * og) * (1.0 / GDN_D)
        dgt = dn_out * normed * gg * (sig * (1.0 + gt * (1.0 - sig)))
        dgg = jnp.sum(dn_out * normed * (gt * sig), axis=0, keepdims=True)
        return [dmixed, dom, _slot_sum(dom * om), dog, dgt], [dgp, dgm, dgg]

    return _rowwise("mix_bwd", fn, [o_mla, o_gdn, gate, mixed, dy], [g_mla, g_gdn, w_out, g_post],
                    [(dm, BF16), (WIDE, F32), (WIDE, F32), (WIDE, F32), (WIDE, BF16)], sums=[dm, WIDE, WIDE], tm=MIX_TM)


def _proj_fwd(x, g, weights):
    dm = x.shape[1]

    def fn(rows, consts):
        hn = _rms(rows[0], consts[0], dm).astype(BF16)
        return [hn] + [_nt(hn, wv) for wv in consts[1:]], []

    return _rowwise("proj_fwd", fn, [x], [g, *weights], [(dm, BF16)] + [(wv.shape[0], F32) for wv in weights], tm=MIX_TM)


def _proj_bwd(x, g, weights, cots, dy):
    dm = x.shape[1]
    n = len(weights)

    def fn(rows, consts):
        xv, dyv, *parts = rows
        dn = _nn(parts[0], consts[1])
        for p, wv in zip(parts[1:], consts[2:]):
            dn = dn + _nn(p, wv)
        _, pull = jax.vjp(lambda a, gv: _rms(a, gv, dm), xv, consts[0])
        dx, dg = pull(dn)
        return [dyv + dx], [dg]

    assert len(cots) == n
    return _rowwise("proj_bwd", fn, [x, dy, *cots], [g, *weights], [(dm, F32)], sums=[dm], tm=MIX_TM)


def _loss_fwd(y, target):
    dm = y.shape[1]

    def fn(rows, consts):
        err = rows[0] - rows[1]
        sq = err * err
        lanes = sq[:, :SLOT]
        for j in range(1, dm // SLOT):
            lanes = lanes + sq[:, j * SLOT:(j + 1) * SLOT]
        return [err * (1.0 / dm)], [jnp.sum(lanes, axis=0, keepdims=True) * (0.5 / dm)]

    return _rowwise("loss", fn, [y, target], [], [(dm, F32)], sums=[SLOT])


W_IN_CUTS = (0, 256, 384, 416, 1952, 1960, 1968, 2480)


def _heads_out(w, per_head, axis=-1):
    axis = axis % w.ndim
    shape = w.shape
    n = shape[axis] // per_head
    w = w.reshape(shape[:axis] + (n, per_head) + shape[axis + 1:])
    pad = [(0, 0)] * w.ndim
    pad[axis + 1] = (0, SLOT - per_head)
    return jnp.pad(w, pad).reshape(shape[:axis] + (n * SLOT,) + shape[axis + 1:])


def _heads_in(w, per_head, axis=-1):
    axis = axis % w.ndim
    shape = w.shape
    n = shape[axis] // SLOT
    w = w.reshape(shape[:axis] + (n, SLOT) + shape[axis + 1:])
    w = lax.slice_in_dim(w, 0, per_head, axis=axis + 1)
    return w.reshape(shape[:axis] + (n * per_head,) + shape[axis + 1:])


def _pad_lanes(v, lo, width=SLOT):
    return jnp.pad(v, [(0, 0)] * (v.ndim - 1) + [(lo, width - lo - v.shape[-1])])


def _pad_rows(v, lo, rows=SLOT):
    return jnp.pad(v, [(lo, rows - lo - v.shape[0])] + [(0, 0)] * (v.ndim - 1))


def _layout_weights(w):
    c = W_IN_CUTS
    w_in = w["w_in_t"]
    p = {}
    p["w_a"] = jnp.concatenate([w_in[c[0]:c[2]], _pad_rows(w_in[c[2]:c[3]], MLA_NOPE), _pad_rows(w_in[c[4]:c[6]], 0)], axis=0)
    p["w_qkv"] = _heads_out(w_in[c[3]:c[4]], GDN_D, axis=0)
    p["w_gate"] = _heads_out(w_in[c[6]:c[7]], GDN_D, axis=0)
    p["w_uq"] = _heads_out(w["uq_t"], MLA_NOPE + MLA_ROPE, axis=0)
    ukv = w["ukv_t"].reshape(N_HEADS, MLA_NOPE + MLA_V, MLA_KV_RANK)
    p["w_kv"] = jnp.concatenate([_heads_out(ukv[:, :MLA_NOPE].reshape(-1, MLA_KV_RANK), MLA_NOPE, axis=0),
                                 _heads_out(ukv[:, MLA_NOPE:].reshape(-1, MLA_KV_RANK), MLA_V, axis=0)], axis=0)
    p["conv"] = _heads_out(w["gdn_conv_w"], GDN_D)
    p["g_mla_out"] = _heads_out(w["mla_out_g"], MLA_V)
    p["g_gdn"] = jnp.tile(_pad_lanes(w["gdn_norm_g"], 0), (1, N_HEADS))
    p["a_log"] = _pad_lanes(w["gdn_a_log"], 0)
    p["dt_bias"] = _pad_lanes(w["gdn_dt_bias"], 0)
    return p


def _unlayout_grads(d):
    c = W_IN_CUTS
    g = {}
    da = d["w_a"]
    kpe0 = A_KPE + MLA_NOPE
    g["w_in_t"] = jnp.concatenate([da[:A_KPE], da[kpe0:kpe0 + MLA_ROPE], _heads_in(d["w_qkv"], GDN_D, axis=0),
                                   da[A_AB:A_AB + 2 * N_HEADS], _heads_in(d["w_gate"], GDN_D, axis=0)], axis=0)
    assert g["w_in_t"].shape[0] == c[-1]
    g["uq_t"] = _heads_in(d["w_uq"], MLA_NOPE + MLA_ROPE, axis=0)
    dk = _heads_in(d["w_kv"][:WIDE], MLA_NOPE, axis=0).reshape(N_HEADS, MLA_NOPE, MLA_KV_RANK)
    dv = _heads_in(d["w_kv"][WIDE:], MLA_V, axis=0).reshape(N_HEADS, MLA_V, MLA_KV_RANK)
    g["ukv_t"] = jnp.concatenate([dk, dv], axis=1).reshape(-1, MLA_KV_RANK)
    g["w_out"] = _heads_in(d["w_out"], GDN_D, axis=0)
    g["gdn_conv_w"] = _heads_in(d["conv"], GDN_D)
    g["mla_out_g"] = _heads_in(d["g_mla_out"], MLA_V)
    g["gdn_norm_g"] = jnp.sum(d["g_gdn"].reshape(N_HEADS, SLOT), axis=0, keepdims=True)[:, :GDN_D]
    g["gdn_a_log"] = d["a_log"][:, :N_HEADS]
    g["gdn_dt_bias"] = d["dt_bias"][:, :N_HEADS]
    return g


def _weight_grad(name, cots, acts, out_dtype=F32, tm=1024, tn=1024, tk=2048, after=None):
    return _matmul(name, cots, acts, "tn", out_dtype=out_dtype, tm=tm, tn=tn, tk=tk, after=after)


def _by_device(a):
    return a.astype(BF16).reshape((N_DEV, a.shape[0] // N_DEV) + a.shape[1:])


def _rows_of(blocks):
    return blocks.reshape((-1,) + blocks.shape[2:])


def _local_step(x, positions, target, w, mid, late):
    tabs = _rope_tables(positions)

    (h1, x1, hg1, hu1), gathered = _ffn_fwd("ffn1_fwd", x, w["ffn1_pre_g"], w["ffn1"], 0, w["ffn1_post_g"], carry=mid)
    w = dict(w, w_in_t=_rows_of(gathered[0]), uq_t=_rows_of(gathered[1]), ukv_t=_rows_of(gathered[2]),
             gdn_conv_w=gathered[3].transpose(1, 0, 2).reshape(CONV_SHAPE))
    p = _layout_weights(w)
    in_weights = [p["w_a"], p["w_qkv"], p["w_gate"]]
    hn, proj_a, proj_qkv, proj_gate = _proj_fwd(x1, w["mix_pre_g"], in_weights)
    cqn, ckvn, q, k, v = _mla_front_fwd(proj_a, tabs, w["mla_q_norm_g"], w["mla_kv_norm_g"], p["w_uq"], p["w_kv"])
    o_mla, lse = _attn_fwd(q, k, v)
    ab = (proj_a, SLOT, A_AB // SLOT)
    qkv_n = _gdn_conv_fwd(proj_qkv, p["conv"])
    gb, bb = _gates_fwd(ab, p["a_log"], p["dt_bias"])
    (o_gdn, keep), (ffn2, w_out) = _gdn_fwd(qkv_n, gb, bb, carry=late)
    p["w_out"] = _heads_out(_rows_of(w_out), GDN_D, axis=0)
    cat, mixed, x2 = _mix_fwd(o_mla, o_gdn, proj_gate, x1, p["g_mla_out"], p["g_gdn"], p["w_out"], w["mix_post_g"])
    (h2, y, hg2, hu2), _ = _ffn_fwd("ffn2_fwd", x2, w["ffn2_pre_g"], ffn2, 0, w["ffn2_post_g"])
    dy, loss_lanes = _loss_fwd(y, target)

    g = {}
    (dx2, xn2, dh2, a2, dhg2, dhu2, g["ffn2_pre_g"], g["ffn2_post_g"]), _ = _ffn_bwd(
        "ffn2_bwd", x2, h2, hg2, hu2, dy, w["ffn2_pre_g"], ffn2, 0, w["ffn2_post_g"])
    ffn2_grads = _Scatter([_by_device(_weight_grad("ffn2_dw_gate", dhg2, xn2, BF16, tm=1408)),
                           _by_device(_weight_grad("ffn2_dw_up", dhu2, xn2, BF16, tm=1408)),
                           _by_device(_weight_grad("ffn2_dw_down", a2, dh2, BF16, tm=1408))])
    d = {}
    dmixed, do_mla, delta, do_gdn, dgate, g["mix_post_g"], d["g_mla_out"], d["g_gdn"] = _mix_bwd(
        o_mla, o_gdn, proj_gate, mixed, dx2, p["g_mla_out"], p["g_gdn"], p["w_out"], w["mix_post_g"])
    d["w_out"] = _weight_grad("mix_out_dw", cat, dmixed, BF16)
    dq, dk, dv = _attn_bwd(q, k, v, do_mla, lse, delta)
    (dqkv_n, dgb, dbb), landed_ffn2 = _gdn_bwd(qkv_n, gb, bb, keep, do_gdn, carry=ffn2_grads)
    dab, d["a_log"], d["dt_bias"] = _gates_bwd(ab, p["a_log"], p["dt_bias"], dgb, dbb)
    dproj_qkv, d["conv"] = _gdn_conv_bwd(proj_qkv, p["conv"], dqkv_n)
    dproj_a, dq_p, dkv_p, g["mla_q_norm_g"], g["mla_kv_norm_g"] = _mla_front_bwd(
        proj_a, tabs, w["mla_q_norm_g"], w["mla_kv_norm_g"], p["w_uq"], p["w_kv"], dq, dk, dv, dab)
    d["w_uq"] = _weight_grad("mla_q_dw", dq_p, cqn, BF16)
    d["w_kv"] = _weight_grad("mla_kv_dw", dkv_p, ckvn, BF16)
    d["w_a"] = _weight_grad("proj_a_dw", dproj_a, hn, BF16, tm=640)
    d["w_qkv"] = _weight_grad("proj_qkv_dw", dproj_qkv, hn, BF16)
    d["w_gate"] = _weight_grad("proj_gate_dw", dgate, hn, BF16)
    dx1, g["mix_pre_g"] = _proj_bwd(x1, w["mix_pre_g"], in_weights, [dproj_a, dproj_qkv, dgate], dx2)
    g.update(_unlayout_grads(d))
    others = list(OTHER.values())
    (dx, xn1, dh1, a1, dhg1, dhu1, g["ffn1_pre_g"], g["ffn1_post_g"]), landed_others = _ffn_bwd(
        "ffn1_bwd", x, h1, hg1, hu1, dx1, w["ffn1_pre_g"], w["ffn1"], 0, w["ffn1_post_g"], carry=_Scatter([_by_device(g.pop(t)) for t in others]))
    landed = dict(zip(list(FFN_NAMES[3:]) + list(OTHER), list(landed_ffn2) + list(landed_others)))
    packed = _pack_small(g, g["gdn_conv_w"].reshape(-1), REDUCE_ROWS)
    packed = packed.at[REDUCE_ROWS - 1, ROW - 1].set(jnp.sum(loss_lanes))
    begun = {}
    begun["small"], token = _scatter_begin("reduce_small_begin", jnp.broadcast_to(packed, (N_DEV,) + packed.shape))
    for name, cots, acts in (("ffn1_w_down", a1, dh1), ("ffn1_w_gate", dhg1, xn1), ("ffn1_w_up", dhu1, xn1)):
        blocks = _by_device(_weight_grad(name + "_grad", cots, acts, BF16, tm=1408, after=token))
        begun[name], token = _scatter_begin("scatter_" + name + "_begin", blocks)
    return dx, g, landed, begun, token


MESH_AXES = ("x", "y", "c")
N_LINKS = N_DEV - 1


def _place():
    return tuple(lax.axis_index(a) for a in MESH_AXES)


def _block_of(dev):
    x, y, c = dev
    return 4 * x + 2 * y + c


def _remote_copy(src, dst, sems, k, to):
    send_sems, recv_sems = sems
    return pltpu.make_async_remote_copy(src_ref=src, dst_ref=dst, send_sem=send_sems.at[k], recv_sem=recv_sems.at[k],
                                        device_id=to, device_id_type=pl.DeviceIdType.MESH)


class _Exchange:
    def __init__(self, arrays):
        self.arrays = list(arrays)
        self.n = len(self.arrays)
        self.specs = [pl.BlockSpec(memory_space=pl.ANY)] * self.n
        self.scratch = [pltpu.SemaphoreType.DMA((self.n * N_LINKS,)), pltpu.SemaphoreType.DMA((self.n * N_LINKS,)),
                        pltpu.SemaphoreType.DMA((self.n,))]

    def split(self, refs):
        n = self.n
        return refs[:n], refs[n:2 * n], (refs[2 * n], refs[2 * n + 1]), refs[2 * n + 2]


class _Gather(_Exchange):
    def out_shape(self):
        return [jax.ShapeDtypeStruct((N_DEV,) + a.shape, a.dtype) for a in self.arrays]

    def _plan(self, ins, outs, sems, local_sems):
        x, y, c = _place()
        me, sibling = (x, y, c), (x, y, 1 - c)
        chips = [(1 - x, y), (x, 1 - y), (1 - x, 1 - y)]

        def copy(a, k, block, to, mine=False):
            src = ins[a] if mine else outs[a].at[_block_of(block)]
            return _remote_copy(src, outs[a].at[_block_of(block)], sems, a * N_LINKS + k, to)

        local = [pltpu.make_async_copy(ins[a], outs[a].at[_block_of(me)], local_sems.at[a]) for a in range(self.n)]
        first = []
        for a in range(self.n):
            first.append(copy(a, 0, me, sibling, mine=True))
            first += [copy(a, 1 + j, me, (*chip, c), mine=True) for j, chip in enumerate(chips)]
        return me, sibling, chips, c, copy, local, first

    def start(self, ins, outs, sems, local_sems):
        *_, local, first = self._plan(ins, outs, sems, local_sems)
        for cp in local + first:
            cp.start()

    def finish(self, ins, outs, sems, local_sems):
        me, sibling, chips, c, copy, local, first = self._plan(ins, outs, sems, local_sems)
        passed = []
        for j, chip in enumerate(chips):
            for a in range(self.n):
                copy(a, 1 + j, (*chip, c), me).wait_recv()
                passed.append(copy(a, 4 + j, (*chip, c), sibling))
                passed[-1].start()
        for a in range(self.n):
            copy(a, 0, sibling, me).wait_recv()
            for j, chip in enumerate(chips):
                copy(a, 4 + j, (*chip, 1 - c), me).wait_recv()
        for cp in first + passed:
            cp.wait_send()
        for cp in local:
            cp.wait()


class _Scatter(_Exchange):
    def out_shape(self):
        return [jax.ShapeDtypeStruct(a.shape, a.dtype) for a in self.arrays]

    def _plan(self, ins, outs, sems, local_sems):
        x, y, c = _place()
        me = _block_of((x, y, c))

        def peer(r):
            return (1 - x if r & 4 else x, 1 - y if r & 2 else y, 1 - c if r & 1 else c)

        local = [pltpu.make_async_copy(ins[a].at[me], outs[a].at[me], local_sems.at[a]) for a in range(self.n)]
        sends = [_remote_copy(ins[a].at[_block_of(peer(r))], outs[a].at[me], sems, a * N_LINKS + r - 1, peer(r))
                 for a in range(self.n) for r in range(1, N_DEV)]
        arrivals = [_remote_copy(ins[a].at[me], outs[a].at[_block_of(peer(r))], sems, a * N_LINKS + r - 1, peer(r))
                    for a in range(self.n) for r in range(1, N_DEV)]
        return local, sends, arrivals

    def start(self, ins, outs, sems, local_sems):
        local, sends, _ = self._plan(ins, outs, sems, local_sems)
        for cp in local + sends:
            cp.start()

    def finish(self, ins, outs, sems, local_sems):
        local, sends, arrivals = self._plan(ins, outs, sems, local_sems)
        for cp in arrivals:
            cp.wait_recv()
        for cp in sends:
            cp.wait_send()
        for cp in local:
            cp.wait()


def _exchange(name, plan):
    def body(*refs):
        parts = plan.split(refs)
        plan.start(*parts)
        plan.finish(*parts)

    return pl.pallas_call(
        body, name=name,
        in_specs=plan.specs,
        out_specs=plan.specs,
        out_shape=plan.out_shape(),
        scratch_shapes=plan.scratch,
    )(*plan.arrays)


def _call_carrying(body, plan, operands, *, name, grid, in_specs, out_specs, out_shape, scratch_shapes, compiler_params):
    if plan is None:
        outs = pl.pallas_call(body, name=name, grid=grid, in_specs=in_specs, out_specs=out_specs, out_shape=out_shape,
                              scratch_shapes=scratch_shapes, compiler_params=compiler_params)(*operands)
        return outs, []
    n_i, n_o, n_s, k = len(in_specs), len(out_specs), len(scratch_shapes), plan.n

    def whole(*refs):
        cut = [n_i, n_i + k, n_i + k + n_o, n_i + 2 * k + n_o, n_i + 2 * k + n_o + n_s]
        own_in, ex_in, own_out, ex_out, own_scr, ex_scr = (refs[a:b] for a, b in zip([0] + cut, cut + [len(refs)]))
        parts = plan.split(ex_in + ex_out + ex_scr)
        first = last = True
        for axis, size in enumerate(grid):
            first = first & (pl.program_id(axis) == 0)
            last = last & (pl.program_id(axis) == size - 1)

        @pl.when(first)
        def _():
            plan.start(*parts)

        body(*own_in, *own_out, *own_scr)

        @pl.when(last)
        def _():
            plan.finish(*parts)

    outs = pl.pallas_call(
        whole, name=name, grid=grid,
        in_specs=list(in_specs) + plan.specs, out_specs=list(out_specs) + plan.specs,
        out_shape=list(out_shape) + plan.out_shape(), scratch_shapes=list(scratch_shapes) + plan.scratch,
        compiler_params=compiler_params,
    )(*operands, *plan.arrays)
    return outs[:n_o], outs[n_o:]


def _row_tile(rows, target=256):
    best = rows
    for cand in range(16, min(rows, target) + 1, 16):
        if rows % cand == 0:
            best = cand
    return best


def _sum_blocks(name, blocks, after=None):
    rows, width = blocks.shape[-2:]
    tm = _row_tile(rows)

    def body(x_ref, *rest):
        acc = x_ref[0].astype(F32)
        for d in range(1, N_DEV):
            acc = acc + x_ref[d].astype(F32)
        rest[-1][...] = acc

    ordered = [] if after is None else [after]
    return pl.pallas_call(
        body, name=name,
        grid=(rows // tm,),
        in_specs=[pl.BlockSpec((N_DEV, tm, width), lambda i: (0, i, 0))] + [pl.BlockSpec(memory_space=pl.ANY)] * len(ordered),
        out_specs=pl.BlockSpec((tm, width), lambda i: (i, 0)),
        out_shape=jax.ShapeDtypeStruct((rows, width), F32),
        compiler_params=pltpu.CompilerParams(dimension_semantics=("parallel",)),
    )(blocks, *ordered)


def _split_plan(src_ref, land_ref, sems):
    x, y, c = _place()
    me = _block_of((x, y, c))

    def peer(r):
        return (1 - x if r & 4 else x, 1 - y if r & 2 else y, 1 - c if r & 1 else c)

    sends = [_remote_copy(src_ref.at[_block_of(peer(r))], land_ref.at[me], sems, r - 1, peer(r)) for r in range(1, N_DEV)]
    arrivals = [_remote_copy(src_ref.at[me], land_ref.at[_block_of(peer(r))], sems, r - 1, peer(r)) for r in range(1, N_DEV)]
    return sends, arrivals


def _scatter_begin(name, blocks):
    def body(src_ref, land_ref, send_sems, recv_sems, src_thru, land_thru, token_ref):
        for cp in _split_plan(src_ref, land_ref, (send_sems, recv_sems))[0]:
            cp.start()
        token_ref[...] = jnp.zeros_like(token_ref)

    hbm, sem = pl.BlockSpec(memory_space=pltpu.HBM), pl.BlockSpec(memory_space=pltpu.SEMAPHORE)
    zone = pltpu.HBM(blocks.shape, blocks.dtype)
    *handles, token = pl.pallas_call(
        body, name=name,
        in_specs=(hbm, hbm),
        out_specs=(sem, sem, hbm, hbm, pl.BlockSpec(memory_space=pltpu.VMEM)),
        out_shape=(pltpu.SemaphoreType.DMA((N_LINKS,)), pltpu.SemaphoreType.DMA((N_LINKS,)), zone, zone,
                   jax.ShapeDtypeStruct((8, SLOT), F32)),
        input_output_aliases={0: 2, 1: 3},
        compiler_params=pltpu.CompilerParams(has_side_effects=pltpu.SideEffectType.DATAFLOW_SIDE_EFFECTING),
    )(pltpu.with_memory_space_constraint(blocks, pltpu.HBM),
      pltpu.with_memory_space_constraint(lax.empty(blocks.shape, blocks.dtype), pltpu.HBM))
    return handles, token


def _scatter_end(name, handles, after):
    send_sems, recv_sems, src, zone = handles

    def body(src_ref, land_ref, send_sems, recv_sems, after_ref, src_dead, got_ref):
        sends, arrivals = _split_plan(src_ref, land_ref, (send_sems, recv_sems))
        for cp in arrivals:
            cp.wait_recv()
        for cp in sends:
            cp.wait_send()

    hbm, sem = pl.BlockSpec(memory_space=pltpu.HBM), pl.BlockSpec(memory_space=pltpu.SEMAPHORE)
    sent, landed = pl.pallas_call(
        body, name=name,
        in_specs=(hbm, hbm, sem, sem, pl.BlockSpec(memory_space=pl.ANY)),
        out_specs=(hbm, hbm),
        out_shape=(pltpu.HBM(src.shape, src.dtype), pltpu.HBM(zone.shape, zone.dtype)),
        input_output_aliases={0: 0, 1: 1},
        compiler_params=pltpu.CompilerParams(has_side_effects=pltpu.SideEffectType.DATAFLOW_SIDE_EFFECTING),
    )(src, zone, send_sems, recv_sems, after)
    me = _block_of(_place())
    return lax.dynamic_update_slice_in_dim(landed, lax.dynamic_slice_in_dim(sent, me, 1, axis=0), me, axis=0)


def _adamw(name, w, g, m, v):
    def fn(rows, consts):
        wv, gv, mv, vv = rows
        m2 = ADAM_B1 * mv + (1.0 - ADAM_B1) * gv
        v2 = ADAM_B2 * vv + (1.0 - ADAM_B2) * jnp.square(gv)
        m_hat = m2 / (1.0 - ADAM_B1 ** ADAM_STEP)
        v_hat = v2 / (1.0 - ADAM_B2 ** ADAM_STEP)
        return [-ADAM_LR * (m_hat / (jnp.sqrt(v_hat) + ADAM_EPS) + ADAM_WD * wv), m2, v2], []

    return _rowwise(name, fn, [w, g, m, v], [], [(w.shape[1], F32)] * 3, tm=_row_tile(w.shape[0]))


ROW = 1024
FFN_NAMES = ("ffn1_w_gate", "ffn1_w_up", "ffn1_w_down", "ffn2_w_gate", "ffn2_w_up", "ffn2_w_down")
OTHER = {"w_in": "w_in_t", "mla_w_uq": "uq_t", "mla_w_ukv": "ukv_t", "w_out": "w_out"}
BY_COLUMNS = ("ffn1_w_gate", "ffn1_w_up", "ffn2_w_gate", "ffn2_w_up", "w_in", "mla_w_uq", "mla_w_ukv")
SMALL = {
    "ffn1_pre_g": (1024, 1024), "ffn1_post_g": (1024, 1024), "mix_pre_g": (1024, 1024), "mla_q_norm_g": (256, 256),
    "mla_kv_norm_g": (128, 128), "mla_out_g": (512, 512), "gdn_a_log": (8, 128), "gdn_dt_bias": (8, 128),
    "gdn_norm_g": (64, 128), "mix_post_g": (1024, 1024), "ffn2_pre_g": (1024, 1024), "ffn2_post_g": (1024, 1024),
}
CONV_SHAPE = (GDN_CONV, 3 * N_HEADS * GDN_D)
CONV_SHARD = (GDN_CONV, CONV_SHAPE[1] // N_DEV)
CONV_LANES = CONV_SHAPE[0] * CONV_SHAPE[1]
SMALL_ROWS = 8
REDUCE_ROWS = 16


def _pack_small(vecs, conv, rows):
    parts = [_pad_lanes(vecs[n].reshape(1, -1), 0, r) for n, (_, r) in SMALL.items()]
    parts.append(conv.reshape(1, -1))
    flat = jnp.concatenate(parts, axis=1)
    return _pad_lanes(flat, 0, rows * ROW).reshape(rows, ROW)


def _unpack_small(buf):
    flat = buf.reshape(1, -1)
    out, at = {}, 0
    for n, (w, r) in SMALL.items():
        out[n] = flat[:, at:at + w]
        at += r
    return out, flat[0, at:]


def kernel(x, positions, ffn1_pre_g, ffn1_w_gate, ffn1_w_up, ffn1_w_down, ffn1_post_g, mix_pre_g, w_in, mla_q_norm_g, mla_w_uq, mla_kv_norm_g, mla_w_ukv, mla_out_g, gdn_conv_w, gdn_a_log, gdn_dt_bias, gdn_norm_g, w_out, mix_post_g, ffn2_pre_g, ffn2_w_gate, ffn2_w_up, ffn2_w_down, ffn2_post_g, loss_target, m_ffn1_pre_g, m_ffn1_w_gate, m_ffn1_w_up, m_ffn1_w_down, m_ffn1_post_g, m_mix_pre_g, m_w_in, m_mla_q_norm_g, m_mla_w_uq, m_mla_kv_norm_g, m_mla_w_ukv, m_mla_out_g, m_gdn_conv_w, m_gdn_a_log, m_gdn_dt_bias, m_gdn_norm_g, m_w_out, m_mix_post_g, m_ffn2_pre_g, m_ffn2_w_gate, m_ffn2_w_up, m_ffn2_w_down, m_ffn2_post_g, v_ffn1_pre_g, v_ffn1_w_gate, v_ffn1_w_up, v_ffn1_w_down, v_ffn1_post_g, v_mix_pre_g, v_w_in, v_mla_q_norm_g, v_mla_w_uq, v_mla_kv_norm_g, v_mla_w_ukv, v_mla_out_g, v_gdn_conv_w, v_gdn_a_log, v_gdn_dt_bias, v_gdn_norm_g, v_w_out, v_mix_post_g, v_ffn2_pre_g, v_ffn2_w_gate, v_ffn2_w_up, v_ffn2_w_down, v_ffn2_post_g):
    given = dict(locals())
    order = ["ffn1_pre_g", "ffn1_w_gate", "ffn1_w_up", "ffn1_w_down", "ffn1_post_g", "mix_pre_g", "w_in", "mla_q_norm_g",
             "mla_w_uq", "mla_kv_norm_g", "mla_w_ukv", "mla_out_g", "gdn_conv_w", "gdn_a_log", "gdn_dt_bias", "gdn_norm_g",
             "w_out", "mix_post_g", "ffn2_pre_g", "ffn2_w_gate", "ffn2_w_up", "ffn2_w_down", "ffn2_post_g"]
    assert sorted(order) == sorted(list(FFN_NAMES) + list(OTHER) + list(SMALL) + ["gdn_conv_w"])

    def drop_depth(a):
        return a[0] if a.ndim == 3 else a

    wts = {n: drop_depth(given[n]) for n in order}
    mom = {n: drop_depth(given["m_" + n]) for n in order}
    var = {n: drop_depth(given["v_" + n]) for n in order}
    me = _block_of(_place())

    def wire(n):
        return (wts[n].T if n in BY_COLUMNS else wts[n]).astype(BF16)

    (ffn1,) = _exchange("gather_first", _Gather([jnp.stack([wire(n) for n in FFN_NAMES[:3]])]))
    mid = _Gather([wire(n) for n in ("w_in", "mla_w_uq", "mla_w_ukv")] + [wts["gdn_conv_w"]])
    late = _Gather([jnp.stack([wire(n) for n in FFN_NAMES[3:]]), wire("w_out")])
    full = {n: wts[n] for n in SMALL}
    full["ffn1"] = ffn1

    dx, grads, landed, begun, token = _local_step(x[0], positions[0], loss_target[0], full, mid, late)

    grad, outs = {}, {"delta": {}, "new_m": {}, "new_v": {}}

    def finish(n, blocks, after=None):
        total = _sum_blocks("sum_" + n, blocks, after=after)
        flip = n in BY_COLUMNS and wts[n].shape[1] % SLOT != 0
        turn = (lambda a: a.T) if flip else (lambda a: a)
        grad[n] = total.T if n in BY_COLUMNS else total
        new = _adamw("adamw_" + n, turn(wts[n]), total if flip else grad[n], turn(mom[n]), turn(var[n]))
        outs["delta"][n], outs["new_m"][n], outs["new_v"][n] = (turn(a) for a in new)
        return new[2]

    for n, blocks in landed.items():
        token = finish(n, blocks, after=token)
    small_handles = begun.pop("small")
    for n, handles in begun.items():
        token = finish(n, _scatter_end("scatter_" + n + "_end", handles, after=token))

    small_sum = _sum_blocks("sum_small", _scatter_end("reduce_small_end", small_handles, after=token))
    loss = small_sum[REDUCE_ROWS - 1, ROW - 1]
    small_grad, conv_grad_full = _unpack_small(small_sum)
    grad.update(small_grad)
    grad["gdn_conv_w"] = lax.dynamic_slice(conv_grad_full[:CONV_LANES].reshape(CONV_SHAPE), (0, me * CONV_SHARD[1]), CONV_SHARD)
    outs["grad"] = grad
    small = [_pack_small(s, s["gdn_conv_w"].reshape(-1), SMALL_ROWS) for s in (wts, grad, mom, var)]
    for kind, s in zip(("delta", "new_m", "new_v"), _adamw("adamw_small", *small)):
        vecs, conv = _unpack_small(s)
        outs[kind].update(vecs)
        outs[kind]["gdn_conv_w"] = conv[:CONV_SHARD[0] * CONV_SHARD[1]].reshape(CONV_SHARD)
    result = [loss, dx[None]]
    for kind in ("grad", "delta", "new_m", "new_v"):
        result += [outs[kind][n].reshape(given[n].shape) for n in order]
    return tuple(result)
```

```python
import jax
import jax.numpy as jnp
from jax import lax
from jax.experimental import pallas as pl
from jax.experimental.pallas import tpu as pltpu

F32 = jnp.float32
BF16 = jnp.bfloat16
HI = lax.Precision.HIGH

N_DEV = 8
N_HEADS = 8
SLOT = 128
MLA_Q_RANK = 256
MLA_KV_RANK = 128
MLA_NOPE = 64
MLA_ROPE = 32
MLA_V = 64
GDN_D = 64
GDN_CONV = 4
GDN_CHUNK = 64
ROPE_THETA = 10000.0
EPS = 1e-6
ADAM_LR, ADAM_B1, ADAM_B2, ADAM_EPS, ADAM_WD, ADAM_STEP = 0.001, 0.9, 0.999, 1e-08, 0.01, 10


def _dot(a, b, ca, cb, precision=None):
    lead = a.ndim - 2
    batch = tuple(range(lead))
    return lax.dot_general(a, b, (((lead + ca,), (lead + cb,)), (batch, batch)), precision=precision,
                           preferred_element_type=F32)


def _nn(a, b, precision=None):
    return _dot(a, b, 1, 0, precision)


def _nt(a, b, precision=None):
    return _dot(a, b, 1, 1, precision)


def _tn(a, b, precision=None):
    return _dot(a, b, 0, 0, precision)


def _sigmoid(x):
    return 1.0 / (1.0 + jnp.exp(-x))


def _silu(x):
    return x * _sigmoid(x)


def _rms(x, g, n):
    ms = jnp.sum(x * x, axis=-1, keepdims=True) * (1.0 / n)
    return x * lax.rsqrt(ms + EPS) * g


def _chunk_masks():
    c = GDN_CHUNK
    i = lax.broadcasted_iota(jnp.int32, (c, c), 0)
    j = lax.broadcasted_iota(jnp.int32, (c, c), 1)
    lower = i >= j
    strict = i > j
    eye = (i == j).astype(F32)
    blocks = []
    b = 1
    while b < c:
        same = (i // (2 * b)) == (j // (2 * b))
        blocks.append(same & ((i % (2 * b)) >= b) & ((j % (2 * b)) < b))
        b *= 2
    return lower, strict, eye, blocks


def _unit_lower_inverse(low, eye, blocks):
    t = eye - jnp.where(blocks[0], low, 0.0)
    for m in blocks[1:]:
        lo = jnp.where(m, low, 0.0)
        t = t - _nn(t, _nn(lo, t, HI), HI)
    return t


@jax.custom_vjp
def _known_inverse(low, tinv):
    return tinv


def _known_inverse_fwd(low, tinv):
    return tinv, tinv


def _known_inverse_bwd(tinv, dt):
    return -_tn(tinv, _nt(dt, tinv, HI), HI), jnp.zeros_like(tinv)


_known_inverse.defvjp(_known_inverse_fwd, _known_inverse_bwd)

_PRODUCTS = {"nn": _nn, "nt": _nt, "tn": _tn}


@jax.custom_vjp
def _known_nn(a, b, c):
    return c


@jax.custom_vjp
def _known_nt(a, b, c):
    return c


@jax.custom_vjp
def _known_tn(a, b, c):
    return c


def _known_fwd(a, b, c):
    return c, (a, b, c)


_known_nn.defvjp(_known_fwd, lambda r, dc: (_nt(dc, r[1], HI), _tn(r[0], dc, HI), jnp.zeros_like(r[2])))
_known_nt.defvjp(_known_fwd, lambda r, dc: (_nn(dc, r[1], HI), _tn(dc, r[0], HI), jnp.zeros_like(r[2])))
_known_tn.defvjp(_known_fwd, lambda r, dc: (_nt(r[1], dc, HI), _nn(r[0], dc, HI), jnp.zeros_like(r[2])))
_KNOWN = {"nn": _known_nn, "nt": _known_nt, "tn": _known_tn}
GDN_PRODUCTS = 8
GDN_KEPT = 2 + GDN_PRODUCTS


def _gdn_chunk(q, k, v, gc, bb, s, masks, known=None):
    lower, strict, eye, blocks = masks
    made = []

    def product(kind, a, b):
        c = _PRODUCTS[kind](a, b, HI) if known is None else _KNOWN[kind](a, b, known[1 + len(made)])
        made.append(c)
        return c

    qs = q * (GDN_D ** -0.5)
    gct = jnp.swapaxes(gc, -1, -2)
    decay = jnp.exp(jnp.where(lower, gc - gct, -1e30))
    kb = k * bb
    low = jnp.where(strict, product("nt", kb, k) * decay, 0.0)
    tinv = _unit_lower_inverse(low, eye, blocks) if known is None else _known_inverse(low, known[0])
    eg = jnp.exp(gc)
    w = product("nn", tinv, kb * eg)
    u = product("nn", tinv, v * bb)
    attn = product("nt", qs, k) * decay
    last = lax.broadcasted_iota(jnp.int32, gc.shape[-2:], 0) == GDN_CHUNK - 1
    g_end = jnp.sum(jnp.where(last, gc, 0.0), axis=-2, keepdims=True)
    k_dec = k * jnp.exp(g_end - gc)
    v_new = u - product("nn", w, s)
    o = product("nn", qs * eg, s) + product("nn", attn, v_new)
    s_new = s * jnp.exp(g_end) + product("tn", k_dec, v_new)
    assert len(made) == GDN_PRODUCTS
    return o, s_new, [tinv] + made


GDN_GROUP = 8
GDN_GROUPS = N_HEADS // GDN_GROUP


def _group_heads(ref):
    return jnp.stack([ref[:, pl.ds(j * SLOT, GDN_D)] for j in range(GDN_GROUP)])


def _ungroup_heads(ref, val):
    pad = jnp.zeros((GDN_CHUNK, SLOT - GDN_D), F32)
    for j in range(GDN_GROUP):
        ref[:, pl.ds(j * SLOT, GDN_D)] = val[j]
        ref[:, pl.ds(j * SLOT + GDN_D, SLOT - GDN_D)] = pad


def _gdn_fwd(qkv, gb, bb, carry=None):
    t = qkv.shape[0]
    n_chunks = t // GDN_CHUNK
    d = GDN_D

    def body(q_ref, k_ref, v_ref, g_ref, b_ref, o_ref, keep_ref, s_ref):
        @pl.when(pl.program_id(1) == 0)
        def _():
            s_ref[...] = jnp.zeros_like(s_ref)

        s = s_ref[...]
        keep_ref[:, 0, 0] = s
        o, s_new, made = _gdn_chunk(*[_group_heads(r) for r in (q_ref, k_ref, v_ref, g_ref, b_ref)], s, _chunk_masks())
        for i, val in enumerate(made):
            keep_ref[:, 0, 1 + i] = val
        s_ref[...] = s_new
        _ungroup_heads(o_ref, o)

    def spec(kind=0):
        return pl.BlockSpec((GDN_CHUNK, GDN_GROUP * SLOT), lambda h, n: (n, kind * GDN_GROUPS + h))

    return _call_carrying(
        body, carry, (qkv, qkv, qkv, gb, bb), name="gdn_fwd",
        grid=(GDN_GROUPS, n_chunks),
        in_specs=[spec(0), spec(1), spec(2), spec(), spec()],
        out_specs=[spec(), pl.BlockSpec((GDN_GROUP, 1, GDN_KEPT, d, d), lambda h, n: (h, n, 0, 0, 0))],
        out_shape=[jax.ShapeDtypeStruct((t, N_HEADS * SLOT), F32), jax.ShapeDtypeStruct((N_HEADS, n_chunks, GDN_KEPT, d, d), F32)],
        scratch_shapes=[pltpu.VMEM((GDN_GROUP, d, d), F32)],
        compiler_params=pltpu.CompilerParams(dimension_semantics=("arbitrary", "arbitrary")),
    )


def _gdn_bwd(qkv, gb, bb, keep, do, carry=None):
    t = qkv.shape[0]
    n_chunks = t // GDN_CHUNK
    d = GDN_D

    def body(q_ref, k_ref, v_ref, g_ref, b_ref, keep_ref, do_ref, dqkv_ref, dg_ref, db_ref, ds_ref):
        @pl.when(pl.program_id(1) == 0)
        def _():
            ds_ref[...] = jnp.zeros_like(ds_ref)

        masks = _chunk_masks()
        known = [keep_ref[:, 0, 1 + i] for i in range(GDN_KEPT - 1)]
        _, pull = jax.vjp(lambda *a: _gdn_chunk(*a, masks, known)[:2],
                          *[_group_heads(r) for r in (q_ref, k_ref, v_ref, g_ref, b_ref)], keep_ref[:, 0, 0])
        dq, dk, dv, dg, db, ds = pull((_group_heads(do_ref), ds_ref[...]))
        ds_ref[...] = ds
        for i, val in enumerate((dq, dk, dv)):
            _ungroup_heads(dqkv_ref.at[i], val)
        _ungroup_heads(dg_ref, dg)
        _ungroup_heads(db_ref, db)

    def spec(kind=0):
        return pl.BlockSpec((GDN_CHUNK, GDN_GROUP * SLOT), lambda h, n: (n_chunks - 1 - n, kind * GDN_GROUPS + h))

    return _call_carrying(
        body, carry, (qkv, qkv, qkv, gb, bb, keep, do), name="gdn_bwd",
        grid=(GDN_GROUPS, n_chunks),
        in_specs=[spec(0), spec(1), spec(2), spec(), spec(),
                  pl.BlockSpec((GDN_GROUP, 1, GDN_KEPT, d, d), lambda h, n: (h, n_chunks - 1 - n, 0, 0, 0)), spec()],
        out_specs=[pl.BlockSpec((3, GDN_CHUNK, GDN_GROUP * SLOT), lambda h, n: (0, n_chunks - 1 - n, h)), spec(), spec()],
        out_shape=[jax.ShapeDtypeStruct((3, t, N_HEADS * SLOT), F32)] + [jax.ShapeDtypeStruct((t, N_HEADS * SLOT), F32)] * 2,
        scratch_shapes=[pltpu.VMEM((GDN_GROUP, d, d), F32)],
        compiler_params=pltpu.CompilerParams(dimension_semantics=("arbitrary", "arbitrary")),
    )


def _rowwise(name, fn, rows, consts, outs, sums=(), tm=512):
    rows = [x if isinstance(x, tuple) else (x, x.shape[1], 0) for x in rows]
    t = rows[0][0].shape[0]
    tm = min(tm, t)
    steps = t // tm
    n_r, n_c, n_o, n_s = len(rows), len(consts), len(outs), len(sums)

    def window(width, block):
        return pl.BlockSpec((tm, width), lambda i: (i, block))

    def body(*refs):
        r, c = refs[:n_r], refs[n_r:n_r + n_c]
        o, s = refs[n_r + n_c:n_r + n_c + n_o], refs[n_r + n_c + n_o:]
        vals, tot = fn([x[...] for x in r], [x[...] for x in c])
        for ref, val in zip(o, vals):
            ref[...] = val.astype(ref.dtype)
        if n_s:
            @pl.when(pl.program_id(0) == 0)
            def _():
                for ref in s:
                    ref[...] = jnp.zeros_like(ref)

            for ref, val in zip(s, tot):
                ref[...] += val

    return pl.pallas_call(
        body, name=name,
        grid=(steps,),
        in_specs=[window(w, b) for _, w, b in rows] + [pl.BlockSpec(x.shape, lambda i: (0, 0)) for x in consts],
        out_specs=[pl.BlockSpec((tm, w), lambda i: (i, 0)) for w, _ in outs]
        + [pl.BlockSpec((1, w), lambda i: (0, 0)) for w in sums],
        out_shape=[jax.ShapeDtypeStruct((t, w), dt) for w, dt in outs]
        + [jax.ShapeDtypeStruct((1, w), F32) for w in sums],
        compiler_params=pltpu.CompilerParams(dimension_semantics=("arbitrary",)),
    )(*[x for x, _, _ in rows], *consts)


def _tile(dim, target):
    if dim <= target:
        return dim
    best = None
    for cand in range(128, target + 1, 128):
        if dim % cand == 0:
            best = cand
    assert best is not None, (dim, target)
    return best


def _matmul(name, a, b, mode, out_dtype=F32, tm=1024, tn=1024, tk=2048, after=None):
    if mode == "nn":
        (m, k), n = a.shape, b.shape[1]
    elif mode == "nt":
        (m, k), n = a.shape, b.shape[0]
    else:
        (k, m), n = a.shape, b.shape[1]
    tm, tn, tk = _tile(m, tm), _tile(n, tn), _tile(k, tk)
    k_steps = k // tk
    product = {"nn": _nn, "nt": _nt, "tn": _tn}[mode]

    def body(a_ref, b_ref, *rest):
        o_ref, acc_ref = rest[-2:]
        part = product(a_ref[...].astype(BF16), b_ref[...].astype(BF16))
        if k_steps == 1:
            o_ref[...] = part.astype(o_ref.dtype)
        else:
            kk = pl.program_id(2)

            @pl.when(kk == 0)
            def _():
                acc_ref[...] = part

            @pl.when(kk > 0)
            def _():
                acc_ref[...] += part

            @pl.when(kk == k_steps - 1)
            def _():
                o_ref[...] = acc_ref[...].astype(o_ref.dtype)

    a_spec = pl.BlockSpec((tk, tm), lambda i, j, kk: (kk, i)) if mode == "tn" else pl.BlockSpec((tm, tk), lambda i, j, kk: (i, kk))
    b_spec = pl.BlockSpec((tn, tk), lambda i, j, kk: (j, kk)) if mode == "nt" else pl.BlockSpec((tk, tn), lambda i, j, kk: (kk, j))
    ordered = [] if after is None else [after]
    return pl.pallas_call(
        body, name=name,
        grid=(m // tm, n // tn, k_steps),
        in_specs=[a_spec, b_spec] + [pl.BlockSpec(memory_space=pl.ANY)] * len(ordered),
        out_specs=pl.BlockSpec((tm, tn), lambda i, j, kk: (i, j)),
        out_shape=jax.ShapeDtypeStruct((m, n), out_dtype),
        scratch_shapes=[pltpu.VMEM((tm, tn) if k_steps > 1 else (8, 128), F32)],
        compiler_params=pltpu.CompilerParams(dimension_semantics=("parallel", "parallel", "arbitrary")),
    )(a, b, *ordered)


FFN_TM = 512
FFN_BWD_TM = 256
FFN_BLOCKS = 4
FFN_GATE, FFN_UP, FFN_DOWN = 0, 1, 2


def _ffn_weight_specs(ffn_w, first):
    _, _, rows, dm = ffn_w.shape

    def spec(k):
        return pl.BlockSpec((FFN_BLOCKS, None, rows, dm), lambda i, j: (j, first + k, 0, 0))

    return [spec(FFN_GATE), spec(FFN_UP), spec(FFN_DOWN)], FFN_BLOCKS * rows


def _ffn_fwd(name, x, g_pre, ffn_w, first, g_post, carry=None):
    t, dm = x.shape
    tm = min(FFN_TM, t)
    w_specs, tf = _ffn_weight_specs(ffn_w, first)
    f_steps = N_DEV // FFN_BLOCKS

    def body(x_ref, gpre_ref, wg_ref, wu_ref, wd_ref, gpost_ref, h_ref, y_ref, hg_ref, hu_ref, xn_ref, acc_ref):
        j = pl.program_id(1)

        @pl.when(j == 0)
        def _():
            xn_ref[...] = _rms(x_ref[...], gpre_ref[...], dm).astype(BF16)
            acc_ref[...] = jnp.zeros_like(acc_ref)

        xn = xn_ref[...]
        wg, wu, wd = (r[...].reshape(tf, dm) for r in (wg_ref, wu_ref, wd_ref))
        hg, hu = _nt(xn, wg), _nt(xn, wu)
        hg_ref[...] = hg.astype(BF16)
        hu_ref[...] = hu.astype(BF16)
        a = _silu(hg) * hu
        acc_ref[...] += _nn(a.astype(BF16), wd)

        @pl.when(j == f_steps - 1)
        def _():
            h = acc_ref[...]
            h_ref[...] = h
            y_ref[...] = x_ref[...] + 0.5 * _rms(h, gpost_ref[...], dm)

    row = pl.BlockSpec((tm, dm), lambda i, j: (i, 0))
    vec = pl.BlockSpec((1, dm), lambda i, j: (0, 0))
    wide = pl.BlockSpec((tm, tf), lambda i, j: (i, j))
    return _call_carrying(
        body, carry, (x, g_pre, ffn_w, ffn_w, ffn_w, g_post), name=name,
        grid=(t // tm, f_steps),
        in_specs=[row, vec, *w_specs, vec],
        out_specs=[row, row, wide, wide],
        out_shape=[jax.ShapeDtypeStruct((t, dm), F32)] * 2 + [jax.ShapeDtypeStruct((t, f_steps * tf), BF16)] * 2,
        scratch_shapes=[pltpu.VMEM((tm, dm), BF16), pltpu.VMEM((tm, dm), F32)],
        compiler_params=pltpu.CompilerParams(dimension_semantics=("arbitrary", "arbitrary")),
    )


def _ffn_bwd(name, x, h, hg, hu, dy, g_pre, ffn_w, first, g_post, carry=None):
    t, dm = x.shape
    tm = min(FFN_BWD_TM, t)
    w_specs, tf = _ffn_weight_specs(ffn_w, first)
    f_steps = N_DEV // FFN_BLOCKS
    f = f_steps * tf

    def post(hv, g):
        return 0.5 * _rms(hv, g, dm)

    def pre(xv, g):
        return _rms(xv, g, dm)

    def body(x_ref, h_ref, dy_ref, hg_ref, hu_ref, gpre_ref, wg_ref, wu_ref, wd_ref, gpost_ref,
             dx_ref, xn_ref, dh_ref, a_ref, dhg_ref, dhu_ref, dgpre_ref, dgpost_ref, acc_ref):
        i, j = pl.program_id(0), pl.program_id(1)

        @pl.when((i == 0) & (j == 0))
        def _():
            dgpre_ref[...] = jnp.zeros_like(dgpre_ref)
            dgpost_ref[...] = jnp.zeros_like(dgpost_ref)

        @pl.when(j == 0)
        def _():
            xn_ref[...] = pre(x_ref[...], gpre_ref[...]).astype(BF16)
            _, pull = jax.vjp(post, h_ref[...], gpost_ref[...])
            dh, dg = pull(dy_ref[...])
            dh_ref[...] = dh.astype(BF16)
            dgpost_ref[...] += dg
            acc_ref[...] = jnp.zeros_like(acc_ref)

        wg, wu, wd = (r[...].reshape(tf, dm) for r in (wg_ref, wu_ref, wd_ref))
        hg, hu = hg_ref[...].astype(F32), hu_ref[...].astype(F32)
        da = _nt(dh_ref[...], wd)
        sig = _sigmoid(hg)
        act = hg * sig
        dhu = (da * act).astype(BF16)
        dhg = (da * hu * (sig * (1.0 + hg * (1.0 - sig)))).astype(BF16)
        a_ref[...] = (act * hu).astype(BF16)
        dhg_ref[...] = dhg
        dhu_ref[...] = dhu
        acc_ref[...] += _nn(dhg, wg) + _nn(dhu, wu)

        @pl.when(j == f_steps - 1)
        def _():
            _, pull = jax.vjp(pre, x_ref[...], gpre_ref[...])
            dx, dg = pull(acc_ref[...])
            dx_ref[...] = dy_ref[...] + dx
            dgpre_ref[...] += dg

    row = pl.BlockSpec((tm, dm), lambda i, j: (i, 0))
    vec = pl.BlockSpec((1, dm), lambda i, j: (0, 0))
    wide = pl.BlockSpec((tm, tf), lambda i, j: (i, j))
    return _call_carrying(
        body, carry, (x, h, dy, hg, hu, g_pre, ffn_w, ffn_w, ffn_w, g_post), name=name,
        grid=(t // tm, f_steps),
        in_specs=[row, row, row, wide, wide, vec, *w_specs, vec],
        out_specs=[row, row, row, wide, wide, wide, vec, vec],
        out_shape=[jax.ShapeDtypeStruct((t, dm), F32), jax.ShapeDtypeStruct((t, dm), BF16), jax.ShapeDtypeStruct((t, dm), BF16),
                   jax.ShapeDtypeStruct((t, f), BF16), jax.ShapeDtypeStruct((t, f), BF16), jax.ShapeDtypeStruct((t, f), BF16),
                   jax.ShapeDtypeStruct((1, dm), F32), jax.ShapeDtypeStruct((1, dm), F32)],
        scratch_shapes=[pltpu.VMEM((tm, dm), F32)],
        compiler_params=pltpu.CompilerParams(dimension_semantics=("arbitrary", "arbitrary")),
    )


ATT_T = 512
ATT_GROUP = 4
ATT_GROUP_FWD = 8
ATT_SCALE = (MLA_NOPE + MLA_ROPE) ** -0.5


def _stack_slots(ref, group):
    return jnp.stack([ref[:, pl.ds(j * SLOT, SLOT)] for j in range(group)])


def _unstack_slots(ref, val):
    for j in range(val.shape[0]):
        ref[:, pl.ds(j * SLOT, SLOT)] = val[j].astype(ref.dtype)


def _scores(q, k, diagonal):
    s = _nt(q, k) * ATT_SCALE
    if diagonal:
        row = lax.broadcasted_iota(jnp.int32, s.shape[1:], 0)
        col = lax.broadcasted_iota(jnp.int32, s.shape[1:], 1)
        s = jnp.where(col <= row, s, -1e30)
    return s


def _attn_pairs(steps, q_major):
    pairs = ([(qi, ki) for qi in range(steps) for ki in range(qi + 1)] if q_major
             else [(qi, ki) for ki in range(steps) for qi in range(ki, steps)])
    return jnp.array([p[0] for p in pairs], jnp.int32), jnp.array([p[1] for p in pairs], jnp.int32)


def _attn_specs(tile, group):
    width = group * SLOT
    return (pl.BlockSpec((tile, width), lambda h, p, qt, kt: (qt[p], h)),
            pl.BlockSpec((tile, width), lambda h, p, qt, kt: (kt[p], h)))


def _attn_fwd(q, k, v):
    t = q.shape[0]
    tile = min(ATT_T, t)
    steps = t // tile
    g = ATT_GROUP_FWD

    strip = min(SLOT, tile)

    def body(qt_ref, kt_ref, q_ref, k_ref, v_ref, o_ref, lse_ref, m_ref, l_ref, alpha_ref, acc_ref, s_ref, p_ref):
        qi, ki = qt_ref[pl.program_id(1)], kt_ref[pl.program_id(1)]

        @pl.when(ki == 0)
        def _():
            m_ref[...] = jnp.full_like(m_ref, -1e30)
            l_ref[...] = jnp.zeros_like(l_ref)
            acc_ref[...] = jnp.zeros_like(acc_ref)

        def step(diagonal):
            s_ref[...] = _nt(_stack_slots(k_ref, g), _stack_slots(q_ref, g))
            for j in range(tile // strip):
                c = pl.ds(j * strip, strip)
                s = s_ref[:, :, c] * ATT_SCALE
                if diagonal:
                    key = lax.broadcasted_iota(jnp.int32, s.shape[1:], 0)
                    query = lax.broadcasted_iota(jnp.int32, s.shape[1:], 1) + j * strip
                    s = jnp.where(key <= query, s, -1e30)
                m_old = m_ref[:, :, c]
                m_new = jnp.maximum(m_old, jnp.max(s, axis=1, keepdims=True))
                p = jnp.exp(s - m_new)
                alpha = jnp.exp(m_old - m_new)
                l_ref[:, :, c] = alpha * l_ref[:, :, c] + jnp.sum(p, axis=1, keepdims=True)
                alpha_ref[:, :, c] = alpha
                m_ref[:, :, c] = m_new
                p_ref[:, :, c] = p.astype(BF16)
            acc_ref[...] = acc_ref[...] * alpha_ref[...] + _tn(_stack_slots(v_ref, g), p_ref[...])

        @pl.when(ki < qi)
        def _():
            step(False)

        @pl.when(ki == qi)
        def _():
            step(True)
            out = acc_ref[...] / l_ref[...]
            lse = jnp.broadcast_to(m_ref[...] + jnp.log(l_ref[...]), out.shape)
            for j in range(g):
                o_ref[:, pl.ds(j * SLOT, SLOT)] = out[j].T
                lse_ref[:, pl.ds(j * SLOT, SLOT)] = lse[j].T

    q_spec, k_spec = _attn_specs(tile, g)
    tables = _attn_pairs(steps, True)
    return pl.pallas_call(
        body, name="attn_fwd",
        grid_spec=pltpu.PrefetchScalarGridSpec(
            num_scalar_prefetch=2, grid=(N_HEADS // g, tables[0].shape[0]),
            in_specs=[q_spec, k_spec, k_spec], out_specs=[q_spec, q_spec],
            scratch_shapes=[pltpu.VMEM((g, 1, tile), F32), pltpu.VMEM((g, 1, tile), F32), pltpu.VMEM((g, 1, tile), F32),
                            pltpu.VMEM((g, SLOT, tile), F32), pltpu.VMEM((g, tile, tile), F32), pltpu.VMEM((g, tile, tile), BF16)]),
        out_shape=[jax.ShapeDtypeStruct((t, N_HEADS * SLOT), F32)] * 2,
        compiler_params=pltpu.CompilerParams(dimension_semantics=("parallel", "arbitrary")),
    )(*tables, q, k, v)


def _attn_grad_scores(q, k, v, do, lse_ref, delta_ref, diagonal):
    g = ATT_GROUP
    p = jnp.exp(_scores(q, k, diagonal) - _stack_slots(lse_ref, g)[:, :, 0:1])
    dp = _nt(do, v)
    return p, p * (dp - _stack_slots(delta_ref, g)[:, :, 0:1]) * ATT_SCALE


def _attn_bwd(q, k, v, do, lse, delta):
    t = q.shape[0]
    tile = min(ATT_T, t)
    steps = t // tile
    g = ATT_GROUP

    def body(qt_ref, kt_ref, q_ref, k_ref, v_ref, do_ref, lse_ref, delta_ref, dq_ref, dk_ref, dv_ref, dk_acc, dv_acc):
        qi, ki = qt_ref[pl.program_id(1)], kt_ref[pl.program_id(1)]

        @pl.when(pl.program_id(1) == 0)
        def _():
            dq_ref[...] = jnp.zeros_like(dq_ref)

        def step(diagonal):
            qq, kk = _stack_slots(q_ref, g), _stack_slots(k_ref, g)
            do_b = _stack_slots(do_ref, g).astype(BF16)
            p, ds = _attn_grad_scores(qq, kk, _stack_slots(v_ref, g), do_b, lse_ref, delta_ref, diagonal)
            ds = ds.astype(BF16)
            dv_acc[...] += _tn(p.astype(BF16), do_b)
            dk_acc[...] += _tn(ds, qq)
            dq = _nn(ds, kk)
            rows = pl.ds(pl.multiple_of(qi * tile, tile), tile)
            for j in range(g):
                dq_ref[rows, pl.ds(j * SLOT, SLOT)] += dq[j]

        @pl.when(qi == ki)
        def _():
            dk_acc[...] = jnp.zeros_like(dk_acc)
            dv_acc[...] = jnp.zeros_like(dv_acc)
            step(True)

        @pl.when(qi > ki)
        def _():
            step(False)

        @pl.when(qi == steps - 1)
        def _():
            _unstack_slots(dk_ref, dk_acc[...])
            _unstack_slots(dv_ref, dv_acc[...])

    q_spec, k_spec = _attn_specs(tile, g)
    tables = _attn_pairs(steps, False)
    return pl.pallas_call(
        body, name="attn_bwd",
        grid_spec=pltpu.PrefetchScalarGridSpec(
            num_scalar_prefetch=2, grid=(N_HEADS // g, tables[0].shape[0]),
            in_specs=[q_spec, k_spec, k_spec, q_spec, q_spec, q_spec],
            out_specs=[pl.BlockSpec((t, g * SLOT), lambda h, p, qt, kt: (0, h)), k_spec, k_spec],
            scratch_shapes=[pltpu.VMEM((g, tile, SLOT), F32), pltpu.VMEM((g, tile, SLOT), F32)]),
        out_shape=[jax.ShapeDtypeStruct((t, N_HEADS * SLOT), F32)] * 3,
        compiler_params=pltpu.CompilerParams(dimension_semantics=("parallel", "arbitrary")),
    )(*tables, q, k, v, do, lse, delta)


CONV_PAD = 8


def _fill_padded(ref, val):
    t = val.shape[0]
    zeros = jnp.zeros((CONV_PAD, val.shape[1]), val.dtype)
    ref[pl.ds(0, CONV_PAD)] = zeros
    ref[pl.ds(CONV_PAD + t, CONV_PAD)] = zeros
    ref[pl.ds(CONV_PAD, t)] = val


def _shifted(ref, s):
    return ref[pl.ds(CONV_PAD - s, ref.shape[0] - 2 * CONV_PAD)]


def _l2norm(x):
    return x * lax.rsqrt(jnp.sum(x * x, axis=-1, keepdims=True) + EPS)


def _conv_pre(x_pad, w):
    y = w[GDN_CONV - 1:GDN_CONV, :] * _shifted(x_pad, 0)
    for s in range(1, GDN_CONV):
        y = y + w[GDN_CONV - 1 - s:GDN_CONV - s, :] * _shifted(x_pad, s)
    return y


def _gdn_conv_fwd(x, w):
    t, width = x.shape

    def body(x_ref, w_ref, o_ref, x_pad):
        _fill_padded(x_pad, x_ref[...])
        act = _silu(_conv_pre(x_pad, w_ref[...]))
        normed = pl.program_id(0) < 2 * N_HEADS
        o_ref[...] = jnp.where(normed, _l2norm(act), act)

    return pl.pallas_call(
        body, name="gdn_conv_fwd",
        grid=(width // SLOT,),
        in_specs=[pl.BlockSpec((t, SLOT), lambda j: (0, j)), pl.BlockSpec((GDN_CONV, SLOT), lambda j: (0, j))],
        out_specs=pl.BlockSpec((t, SLOT), lambda j: (0, j)),
        out_shape=jax.ShapeDtypeStruct((t, width), F32),
        scratch_shapes=[pltpu.VMEM((t + 2 * CONV_PAD, SLOT), F32)],
        compiler_params=pltpu.CompilerParams(dimension_semantics=("parallel",)),
    )(x, w)


def _gdn_conv_bwd(x, w, dout):
    t, width = x.shape

    def body(x_ref, w_ref, do_ref, dx_ref, dw_ref, x_pad, dy_pad):
        wv = w_ref[...]
        _fill_padded(x_pad, x_ref[...])
        y = _conv_pre(x_pad, wv)
        sig = _sigmoid(y)
        act = y * sig
        _, pull = jax.vjp(_l2norm, act)
        normed = pl.program_id(0) < 2 * N_HEADS
        dact = jnp.where(normed, pull(do_ref[0])[0], do_ref[0])
        dy = dact * (sig * (1.0 + y * (1.0 - sig)))
        _fill_padded(dy_pad, dy)
        dx = wv[GDN_CONV - 1:GDN_CONV, :] * dy
        for s in range(1, GDN_CONV):
            dx = dx + wv[GDN_CONV - 1 - s:GDN_CONV - s, :] * _shifted(dy_pad, -s)
        dx_ref[...] = dx.astype(BF16)
        for s in range(GDN_CONV):
            dw_ref[GDN_CONV - 1 - s:GDN_CONV - s, :] = jnp.sum(dy * _shifted(x_pad, s), axis=0, keepdims=True)

    col = pl.BlockSpec((t, SLOT), lambda j: (0, j))
    tap = pl.BlockSpec((GDN_CONV, SLOT), lambda j: (0, j))
    return pl.pallas_call(
        body, name="gdn_conv_bwd",
        grid=(width // SLOT,),
        in_specs=[col, tap, pl.BlockSpec((1, t, SLOT), lambda j: (j // N_HEADS, 0, j % N_HEADS))],
        out_specs=[col, tap],
        out_shape=[jax.ShapeDtypeStruct((t, width), BF16), jax.ShapeDtypeStruct((GDN_CONV, width), F32)],
        scratch_shapes=[pltpu.VMEM((t + 2 * CONV_PAD, SLOT), F32)] * 2,
        compiler_params=pltpu.CompilerParams(dimension_semantics=("parallel",)),
    )(x, w, dout)


def _softplus(x):
    e = jnp.exp(-jnp.abs(x))
    u = 1.0 + e
    log1p = jnp.where(u == 1.0, e, jnp.log(u) * e / jnp.where(u == 1.0, 1.0, u - 1.0))
    return jnp.maximum(x, 0.0) + log1p


def _chunk_running_sum(x, reverse=False):
    tm = x.shape[0]
    at = lax.broadcasted_iota(jnp.int32, x.shape, 0) % GDN_CHUNK
    step = 1
    while step < GDN_CHUNK:
        if reverse:
            x = x + jnp.where(at < GDN_CHUNK - step, pltpu.roll(x, tm - step, 0), 0.0)
        else:
            x = x + jnp.where(at >= step, pltpu.roll(x, step, 0), 0.0)
        step *= 2
    return x


def _gates_fwd(ab, a_log, dt_bias):
    def fn(rows, consts):
        (abv,), (alog, dtb) = rows, consts
        g = _chunk_running_sum(-jnp.exp(alog) * _softplus(abv + dtb))
        beta = _sigmoid(abv)
        shape = (abv.shape[0], SLOT)
        g_slots = [jnp.broadcast_to(g[:, h:h + 1], shape) for h in range(N_HEADS)]
        b_slots = [jnp.broadcast_to(beta[:, N_HEADS + h:N_HEADS + h + 1], shape) for h in range(N_HEADS)]
        return [jnp.concatenate(g_slots, axis=1), jnp.concatenate(b_slots, axis=1)], []

    width = N_HEADS * SLOT
    return _rowwise("gdn_gates_fwd", fn, [ab], [a_log, dt_bias], [(width, F32), (width, F32)])


def _gates_bwd(ab, a_log, dt_bias, dg, dbeta):
    def fn(rows, consts):
        (abv, dgv, dbv), (alog, dtb) = rows, consts
        lane = lax.broadcasted_iota(jnp.int32, abv.shape, 1)
        dg_tok = jnp.zeros_like(abv)
        db_tok = jnp.zeros_like(abv)
        for h in range(N_HEADS):
            dg_tok = dg_tok + jnp.where(lane == h, jnp.sum(dgv[:, h * SLOT:(h + 1) * SLOT], axis=1, keepdims=True), 0.0)
            db_tok = db_tok + jnp.where(lane == N_HEADS + h, jnp.sum(dbv[:, h * SLOT:(h + 1) * SLOT], axis=1, keepdims=True), 0.0)
        dg_tok = _chunk_running_sum(dg_tok, reverse=True)
        xa = abv + dtb
        g = -jnp.exp(alog) * _softplus(xa)
        da = dg_tok * (-jnp.exp(alog)) * _sigmoid(xa)
        beta = _sigmoid(abv)
        dab = jnp.where(lane < N_HEADS, da, db_tok * beta * (1.0 - beta))
        dab = jnp.where(lane < 2 * N_HEADS, dab, 0.0)
        d_alog = jnp.sum(jnp.where(lane < N_HEADS, dg_tok * g, 0.0), axis=0, keepdims=True)
        d_dtb = jnp.sum(jnp.where(lane < N_HEADS, da, 0.0), axis=0, keepdims=True)
        return [dab], [d_alog, d_dtb]

    return _rowwise("gdn_gates_bwd", fn, [ab, dg, dbeta], [a_log, dt_bias], [(SLOT, F32)], sums=[SLOT, SLOT])


ROPE_HALF = MLA_ROPE // 2


def _rope_tables(positions):
    freqs = ROPE_THETA ** (-jnp.arange(ROPE_HALF, dtype=F32) / ROPE_HALF)
    ang = positions.astype(F32)[:, None] * freqs
    cos, sin = jnp.cos(ang), jnp.sin(ang)
    t = positions.shape[0]
    ones, zeros = jnp.ones((t, MLA_NOPE), F32), jnp.zeros((t, MLA_NOPE), F32)
    tail = jnp.zeros((t, SLOT - MLA_NOPE - MLA_ROPE), F32)
    half0 = jnp.zeros((t, ROPE_HALF), F32)
    same = jnp.concatenate([ones, cos, cos, tail], axis=1)
    from_low = jnp.concatenate([zeros, half0, sin, tail], axis=1)
    from_high = jnp.concatenate([zeros, -sin, half0, tail], axis=1)
    return same, from_low, from_high


def _rope(x, tabs):
    same, from_low, from_high = tabs
    width = x.shape[1]
    return x * same + pltpu.roll(x, ROPE_HALF, 1) * from_low + pltpu.roll(x, width - ROPE_HALF, 1) * from_high


def _rope_transposed(dy, tabs):
    same, from_low, from_high = tabs
    width = dy.shape[1]
    return dy * same + pltpu.roll(dy * from_low, width - ROPE_HALF, 1) + pltpu.roll(dy * from_high, ROPE_HALF, 1)


def _tile_slots(tab):
    return jnp.concatenate([tab] * N_HEADS, axis=1)


A_WIDTH = MLA_Q_RANK + MLA_KV_RANK + 2 * SLOT
A_KPE = MLA_Q_RANK + MLA_KV_RANK
A_AB = A_KPE + SLOT
WIDE = N_HEADS * SLOT


def _mla_front_fwd(proj_a, tabs, g_q, g_kv, w_uq, w_kv):
    def fn(rows, consts):
        pa, *tb = rows
        gq, gkv, wuq, wkv = consts
        cqn = _rms(pa[:, :MLA_Q_RANK], gq, MLA_Q_RANK).astype(BF16)
        ckvn = _rms(pa[:, MLA_Q_RANK:A_KPE], gkv, MLA_KV_RANK).astype(BF16)
        kv = _nt(ckvn, wkv)
        q = _rope(_nt(cqn, wuq), [_tile_slots(x) for x in tb])
        k = kv[:, :WIDE] + _tile_slots(_rope(pa[:, A_KPE:A_AB], tb))
        return [cqn, ckvn, q, k, kv[:, WIDE:]], []

    return _rowwise("mla_front_fwd", fn, [proj_a, *tabs], [g_q, g_kv, w_uq, w_kv],
                    [(MLA_Q_RANK, BF16), (MLA_KV_RANK, BF16)] + [(WIDE, BF16)] * 3)


def _mla_front_bwd(proj_a, tabs, g_q, g_kv, w_uq, w_kv, dq, dk, dv, dab):
    def fn(rows, consts):
        pa, t0, t1, t2, dqv, dkv, dvv, da = rows
        gq, gkv, wuq, wkv = consts
        tb = (t0, t1, t2)
        dq_p = _rope_transposed(dqv, [_tile_slots(x) for x in tb]).astype(BF16)
        dkv_p = jnp.concatenate([dkv, dvv], axis=1).astype(BF16)
        dkpe = dkv[:, :SLOT]
        for h in range(1, N_HEADS):
            dkpe = dkpe + dkv[:, h * SLOT:(h + 1) * SLOT]
        _, pull_q = jax.vjp(lambda x, g: _rms(x, g, MLA_Q_RANK), pa[:, :MLA_Q_RANK], gq)
        _, pull_kv = jax.vjp(lambda x, g: _rms(x, g, MLA_KV_RANK), pa[:, MLA_Q_RANK:A_KPE], gkv)
        dcq, dgq = pull_q(_nn(dq_p, wuq))
        dckv, dgkv = pull_kv(_nn(dkv_p, wkv))
        return [jnp.concatenate([dcq, dckv, _rope_transposed(dkpe, tb), da], axis=1), dq_p, dkv_p], [dgq, dgkv]

    return _rowwise("mla_front_bwd", fn, [proj_a, *tabs, dq, dk, dv, dab], [g_q, g_kv, w_uq, w_kv],
                    [(A_WIDTH, BF16), (WIDE, BF16), (2 * WIDE, BF16)], sums=[MLA_Q_RANK, MLA_KV_RANK])


def _slot_sum(x):
    parts = [jnp.broadcast_to(jnp.sum(x[:, h * SLOT:(h + 1) * SLOT], axis=1, keepdims=True), (x.shape[0], SLOT))
             for h in range(N_HEADS)]
    return jnp.concatenate(parts, axis=1)


def _mix_join(o_mla, o_gdn, gate, g_mla, g_gdn):
    mla = _rms(o_mla, g_mla, N_HEADS * MLA_V)
    gdn = o_gdn * lax.rsqrt(_slot_sum(o_gdn * o_gdn) * (1.0 / GDN_D) + EPS) * g_gdn * _silu(gate)
    return mla, gdn


MIX_TM = 256


def _mix_fwd(o_mla, o_gdn, gate, x, g_mla, g_gdn, w_out, g_post):
    dm = x.shape[1]

    def fn(rows, consts):
        om, og, gt, xv = rows
        gm, gg, wo, gp = consts
        cat = jnp.concatenate(_mix_join(om, og, gt, gm, gg), axis=1).astype(BF16)
        mixed = _nn(cat, wo)
        return [cat, mixed, xv + _rms(mixed, gp, dm)], []

    return _rowwise("mix_fwd", fn, [o_mla, o_gdn, gate, x], [g_mla, g_gdn, w_out, g_post],
                    [(2 * WIDE, BF16), (dm, F32), (dm, F32)], tm=MIX_TM)


def _mix_bwd(o_mla, o_gdn, gate, mixed, dy, g_mla, g_gdn, w_out, g_post):
    dm = mixed.shape[1]

    def fn(rows, consts):
        om, og, gt, mx, dyv = rows
        gm, gg, wo, gp = consts
        _, pull_post = jax.vjp(lambda hv, gv: _rms(hv, gv, dm), mx, gp)
        dmixed, dgp = pull_post(dyv)
        dmixed = dmixed.astype(BF16)
        dc = _nt(dmixed, wo)
        _, pull = jax.vjp(lambda x, g: _rms(x, g, N_HEADS * MLA_V), om, gm)
        dom, dgm = pull(dc[:, :WIDE])
        dn_out = dc[:, WIDE:]
        r = lax.rsqrt(_slot_sum(og * og) * (1.0 / GDN_D) + EPS)
        sig = _sigmoid(gt)
        normed = og * r
        dn = dn_out * gg * (gt * sig)
        dog = r * dn - normed * (r * r) * _slot_sum(dn * og) * (1.0 / GDN_D)
        dgt = dn_out * normed * gg * (sig * (1.0 + gt * (1.0 - sig)))
        dgg = jnp.sum(dn_out * normed * (gt * sig), axis=0, keepdims=True)
        return [dmixed, dom, _slot_sum(dom * om), dog, dgt], [dgp, dgm, dgg]

    return _rowwise("mix_bwd", fn, [o_mla, o_gdn, gate, mixed, dy], [g_mla, g_gdn, w_out, g_post],
                    [(dm, BF16), (WIDE, F32), (WIDE, F32), (WIDE, F32), (WIDE, BF16)], sums=[dm, WIDE, WIDE], tm=MIX_TM)


def _proj_fwd(x, g, weights):
    dm = x.shape[1]

    def fn(rows, consts):
        hn = _rms(rows[0], consts[0], dm).astype(BF16)
        return [hn] + [_nt(hn, wv) for wv in consts[1:]], []

    return _rowwise("proj_fwd", fn, [x], [g, *weights], [(dm, BF16)] + [(wv.shape[0], F32) for wv in weights], tm=MIX_TM)


def _proj_bwd(x, g, weights, cots, dy):
    dm = x.shape[1]
    n = len(weights)

    def fn(rows, consts):
        xv, dyv, *parts = rows
        dn = _nn(parts[0], consts[1])
        for p, wv in zip(parts[1:], consts[2:]):
            dn = dn + _nn(p, wv)
        _, pull = jax.vjp(lambda a, gv: _rms(a, gv, dm), xv, consts[0])
        dx, dg = pull(dn)
        return [dyv + dx], [dg]

    assert len(cots) == n
    return _rowwise("proj_bwd", fn, [x, dy, *cots], [g, *weights], [(dm, F32)], sums=[dm], tm=MIX_TM)


def _loss_fwd(y, target):
    dm = y.shape[1]

    def fn(rows, consts):
        err = rows[0] - rows[1]
        sq = err * err
        lanes = sq[:, :SLOT]
        for j in range(1, dm // SLOT):
            lanes = lanes + sq[:, j * SLOT:(j + 1) * SLOT]
        return [err * (1.0 / dm)], [jnp.sum(lanes, axis=0, keepdims=True) * (0.5 / dm)]

    return _rowwise("loss", fn, [y, target], [], [(dm, F32)], sums=[SLOT])


W_IN_CUTS = (0, 256, 384, 416, 1952, 1960, 1968, 2480)


def _heads_out(w, per_head, axis=-1):
    axis = axis % w.ndim
    shape = w.shape
    n = shape[axis] // per_head
    w = w.reshape(shape[:axis] + (n, per_head) + shape[axis + 1:])
    pad = [(0, 0)] * w.ndim
    pad[axis + 1] = (0, SLOT - per_head)
    return jnp.pad(w, pad).reshape(shape[:axis] + (n * SLOT,) + shape[axis + 1:])


def _heads_in(w, per_head, axis=-1):
    axis = axis % w.ndim
    shape = w.shape
    n = shape[axis] // SLOT
    w = w.reshape(shape[:axis] + (n, SLOT) + shape[axis + 1:])
    w = lax.slice_in_dim(w, 0, per_head, axis=axis + 1)
    return w.reshape(shape[:axis] + (n * per_head,) + shape[axis + 1:])


def _pad_lanes(v, lo, width=SLOT):
    return jnp.pad(v, [(0, 0)] * (v.ndim - 1) + [(lo, width - lo - v.shape[-1])])


def _pad_rows(v, lo, rows=SLOT):
    return jnp.pad(v, [(lo, rows - lo - v.shape[0])] + [(0, 0)] * (v.ndim - 1))


def _layout_weights(w):
    c = W_IN_CUTS
    w_in = w["w_in_t"]
    p = {}
    p["w_a"] = jnp.concatenate([w_in[c[0]:c[2]], _pad_rows(w_in[c[2]:c[3]], MLA_NOPE), _pad_rows(w_in[c[4]:c[6]], 0)], axis=0)
    p["w_qkv"] = _heads_out(w_in[c[3]:c[4]], GDN_D, axis=0)
    p["w_gate"] = _heads_out(w_in[c[6]:c[7]], GDN_D, axis=0)
    p["w_uq"] = _heads_out(w["uq_t"], MLA_NOPE + MLA_ROPE, axis=0)
    ukv = w["ukv_t"].reshape(N_HEADS, MLA_NOPE + MLA_V, MLA_KV_RANK)
    p["w_kv"] = jnp.concatenate([_heads_out(ukv[:, :MLA_NOPE].reshape(-1, MLA_KV_RANK), MLA_NOPE, axis=0),
                                 _heads_out(ukv[:, MLA_NOPE:].reshape(-1, MLA_KV_RANK), MLA_V, axis=0)], axis=0)
    p["conv"] = _heads_out(w["gdn_conv_w"], GDN_D)
    p["g_mla_out"] = _heads_out(w["mla_out_g"], MLA_V)
    p["g_gdn"] = jnp.tile(_pad_lanes(w["gdn_norm_g"], 0), (1, N_HEADS))
    p["a_log"] = _pad_lanes(w["gdn_a_log"], 0)
    p["dt_bias"] = _pad_lanes(w["gdn_dt_bias"], 0)
    return p


def _unlayout_grads(d):
    c = W_IN_CUTS
    g = {}
    da = d["w_a"]
    kpe0 = A_KPE + MLA_NOPE
    g["w_in_t"] = jnp.concatenate([da[:A_KPE], da[kpe0:kpe0 + MLA_ROPE], _heads_in(d["w_qkv"], GDN_D, axis=0),
                                   da[A_AB:A_AB + 2 * N_HEADS], _heads_in(d["w_gate"], GDN_D, axis=0)], axis=0)
    assert g["w_in_t"].shape[0] == c[-1]
    g["uq_t"] = _heads_in(d["w_uq"], MLA_NOPE + MLA_ROPE, axis=0)
    dk = _heads_in(d["w_kv"][:WIDE], MLA_NOPE, axis=0).reshape(N_HEADS, MLA_NOPE, MLA_KV_RANK)
    dv = _heads_in(d["w_kv"][WIDE:], MLA_V, axis=0).reshape(N_HEADS, MLA_V, MLA_KV_RANK)
    g["ukv_t"] = jnp.concatenate([dk, dv], axis=1).reshape(-1, MLA_KV_RANK)
    g["w_out"] = _heads_in(d["w_out"], GDN_D, axis=0)
    g["gdn_conv_w"] = _heads_in(d["conv"], GDN_D)
    g["mla_out_g"] = _heads_in(d["g_mla_out"], MLA_V)
    g["gdn_norm_g"] = jnp.sum(d["g_gdn"].reshape(N_HEADS, SLOT), axis=0, keepdims=True)[:, :GDN_D]
    g["gdn_a_log"] = d["a_log"][:, :N_HEADS]
    g["gdn_dt_bias"] = d["dt_bias"][:, :N_HEADS]
    return g


def _weight_grad(name, cots, acts, out_dtype=F32, tm=1024, tn=1024, tk=2048, after=None):
    return _matmul(name, cots, acts, "tn", out_dtype=out_dtype, tm=tm, tn=tn, tk=tk, after=after)


def _by_device(a):
    return a.astype(BF16).reshape((N_DEV, a.shape[0] // N_DEV) + a.shape[1:])


def _rows_of(blocks):
    return blocks.reshape((-1,) + blocks.shape[2:])


def _local_step(x, positions, target, w, mid, late):
    tabs = _rope_tables(positions)

    (h1, x1, hg1, hu1), gathered = _ffn_fwd("ffn1_fwd", x, w["ffn1_pre_g"], w["ffn1"], 0, w["ffn1_post_g"], carry=mid)
    w = dict(w, w_in_t=_rows_of(gathered[0]), uq_t=_rows_of(gathered[1]), ukv_t=_rows_of(gathered[2]),
             gdn_conv_w=gathered[3].transpose(1, 0, 2).reshape(CONV_SHAPE))
    p = _layout_weights(w)
    in_weights = [p["w_a"], p["w_qkv"], p["w_gate"]]
    hn, proj_a, proj_qkv, proj_gate = _proj_fwd(x1, w["mix_pre_g"], in_weights)
    cqn, ckvn, q, k, v = _mla_front_fwd(proj_a, tabs, w["mla_q_norm_g"], w["mla_kv_norm_g"], p["w_uq"], p["w_kv"])
    o_mla, lse = _attn_fwd(q, k, v)
    ab = (proj_a, SLOT, A_AB // SLOT)
    qkv_n = _gdn_conv_fwd(proj_qkv, p["conv"])
    gb, bb = _gates_fwd(ab, p["a_log"], p["dt_bias"])
    (o_gdn, keep), (ffn2, w_out) = _gdn_fwd(qkv_n, gb, bb, carry=late)
    p["w_out"] = _heads_out(_rows_of(w_out), GDN_D, axis=0)
    cat, mixed, x2 = _mix_fwd(o_mla, o_gdn, proj_gate, x1, p["g_mla_out"], p["g_gdn"], p["w_out"], w["mix_post_g"])
    (h2, y, hg2, hu2), _ = _ffn_fwd("ffn2_fwd", x2, w["ffn2_pre_g"], ffn2, 0, w["ffn2_post_g"])
    dy, loss_lanes = _loss_fwd(y, target)

    g = {}
    (dx2, xn2, dh2, a2, dhg2, dhu2, g["ffn2_pre_g"], g["ffn2_post_g"]), _ = _ffn_bwd(
        "ffn2_bwd", x2, h2, hg2, hu2, dy, w["ffn2_pre_g"], ffn2, 0, w["ffn2_post_g"])
    ffn2_grads = _Scatter([_by_device(_weight_grad("ffn2_dw_gate", dhg2, xn2, BF16, tm=1408)),
                           _by_device(_weight_grad("ffn2_dw_up", dhu2, xn2, BF16, tm=1408)),
                           _by_device(_weight_grad("ffn2_dw_down", a2, dh2, BF16, tm=1408))])
    d = {}
    dmixed, do_mla, delta, do_gdn, dgate, g["mix_post_g"], d["g_mla_out"], d["g_gdn"] = _mix_bwd(
        o_mla, o_gdn, proj_gate, mixed, dx2, p["g_mla_out"], p["g_gdn"], p["w_out"], w["mix_post_g"])
    d["w_out"] = _weight_grad("mix_out_dw", cat, dmixed, BF16)
    dq, dk, dv = _attn_bwd(q, k, v, do_mla, lse, delta)
    (dqkv_n, dgb, dbb), landed_ffn2 = _gdn_bwd(qkv_n, gb, bb, keep, do_gdn, carry=ffn2_grads)
    dab, d["a_log"], d["dt_bias"] = _gates_bwd(ab, p["a_log"], p["dt_bias"], dgb, dbb)
    dproj_qkv, d["conv"] = _gdn_conv_bwd(proj_qkv, p["conv"], dqkv_n)
    dproj_a, dq_p, dkv_p, g["mla_q_norm_g"], g["mla_kv_norm_g"] = _mla_front_bwd(
        proj_a, tabs, w["mla_q_norm_g"], w["mla_kv_norm_g"], p["w_uq"], p["w_kv"], dq, dk, dv, dab)
    d["w_uq"] = _weight_grad("mla_q_dw", dq_p, cqn, BF16)
    d["w_kv"] = _weight_grad("mla_kv_dw", dkv_p, ckvn, BF16)
    d["w_a"] = _weight_grad("proj_a_dw", dproj_a, hn, BF16, tm=640)
    d["w_qkv"] = _weight_grad("proj_qkv_dw", dproj_qkv, hn, BF16)
    d["w_gate"] = _weight_grad("proj_gate_dw", dgate, hn, BF16)
    dx1, g["mix_pre_g"] = _proj_bwd(x1, w["mix_pre_g"], in_weights, [dproj_a, dproj_qkv, dgate], dx2)
    g.update(_unlayout_grads(d))
    others = list(OTHER.values())
    (dx, xn1, dh1, a1, dhg1, dhu1, g["ffn1_pre_g"], g["ffn1_post_g"]), landed_others = _ffn_bwd(
        "ffn1_bwd", x, h1, hg1, hu1, dx1, w["ffn1_pre_g"], w["ffn1"], 0, w["ffn1_post_g"], carry=_Scatter([_by_device(g.pop(t)) for t in others]))
    landed = dict(zip(list(FFN_NAMES[3:]) + list(OTHER), list(landed_ffn2) + list(landed_others)))
    packed = _pack_small(g, g["gdn_conv_w"].reshape(-1), REDUCE_ROWS)
    packed = packed.at[REDUCE_ROWS - 1, ROW - 1].set(jnp.sum(loss_lanes))
    begun = {}
    begun["small"], token = _scatter_begin("reduce_small_begin", jnp.broadcast_to(packed, (N_DEV,) + packed.shape))
    for name, cots, acts in (("ffn1_w_down", a1, dh1), ("ffn1_w_gate", dhg1, xn1), ("ffn1_w_up", dhu1, xn1)):
        blocks = _by_device(_weight_grad(name + "_grad", cots, acts, BF16, tm=1408, after=token))
        begun[name], token = _scatter_begin("scatter_" + name + "_begin", blocks)
    return dx, g, landed, begun, token


MESH_AXES = ("x", "y", "c")
N_LINKS = N_DEV - 1


def _place():
    return tuple(lax.axis_index(a) for a in MESH_AXES)


def _block_of(dev):
    x, y, c = dev
    return 4 * x + 2 * y + c


def _remote_copy(src, dst, sems, k, to):
    send_sems, recv_sems = sems
    return pltpu.make_async_remote_copy(src_ref=src, dst_ref=dst, send_sem=send_sems.at[k], recv_sem=recv_sems.at[k],
                                        device_id=to, device_id_type=pl.DeviceIdType.MESH)


class _Exchange:
    def __init__(self, arrays):
        self.arrays = list(arrays)
        self.n = len(self.arrays)
        self.specs = [pl.BlockSpec(memory_space=pl.ANY)] * self.n
        self.scratch = [pltpu.SemaphoreType.DMA((self.n * N_LINKS,)), pltpu.SemaphoreType.DMA((self.n * N_LINKS,)),
                        pltpu.SemaphoreType.DMA((self.n,))]

    def split(self, refs):
        n = self.n
        return refs[:n], refs[n:2 * n], (refs[2 * n], refs[2 * n + 1]), refs[2 * n + 2]


class _Gather(_Exchange):
    def out_shape(self):
        return [jax.ShapeDtypeStruct((N_DEV,) + a.shape, a.dtype) for a in self.arrays]

    def _plan(self, ins, outs, sems, local_sems):
        x, y, c = _place()
        me, sibling = (x, y, c), (x, y, 1 - c)
        chips = [(1 - x, y), (x, 1 - y), (1 - x, 1 - y)]

        def copy(a, k, block, to, mine=False):
            src = ins[a] if mine else outs[a].at[_block_of(block)]
            return _remote_copy(src, outs[a].at[_block_of(block)], sems, a * N_LINKS + k, to)

        local = [pltpu.make_async_copy(ins[a], outs[a].at[_block_of(me)], local_sems.at[a]) for a in range(self.n)]
        first = []
        for a in range(self.n):
            first.append(copy(a, 0, me, sibling, mine=True))
            first += [copy(a, 1 + j, me, (*chip, c), mine=True) for j, chip in enumerate(chips)]
        return me, sibling, chips, c, copy, local, first

    def start(self, ins, outs, sems, local_sems):
        *_, local, first = self._plan(ins, outs, sems, local_sems)
        for cp in local + first:
            cp.start()

    def finish(self, ins, outs, sems, local_sems):
        me, sibling, chips, c, copy, local, first = self._plan(ins, outs, sems, local_sems)
        passed = []
        for j, chip in enumerate(chips):
            for a in range(self.n):
                copy(a, 1 + j, (*chip, c), me).wait_recv()
                passed.append(copy(a, 4 + j, (*chip, c), sibling))
                passed[-1].start()
        for a in range(self.n):
            copy(a, 0, sibling, me).wait_recv()
            for j, chip in enumerate(chips):
                copy(a, 4 + j, (*chip, 1 - c), me).wait_recv()
        for cp in first + passed:
            cp.wait_send()
        for cp in local:
            cp.wait()


class _Scatter(_Exchange):
    def out_shape(self):
        return [jax.ShapeDtypeStruct(a.shape, a.dtype) for a in self.arrays]

    def _plan(self, ins, outs, sems, local_sems):
        x, y, c = _place()
        me = _block_of((x, y, c))

        def peer(r):
            return (1 - x if r & 4 else x, 1 - y if r & 2 else y, 1 - c if r & 1 else c)

        local = [pltpu.make_async_copy(ins[a].at[me], outs[a].at[me], local_sems.at[a]) for a in range(self.n)]
        sends = [_remote_copy(ins[a].at[_block_of(peer(r))], outs[a].at[me], sems, a * N_LINKS + r - 1, peer(r))
                 for a in range(self.n) for r in range(1, N_DEV)]
        arrivals = [_remote_copy(ins[a].at[me], outs[a].at[_block_of(peer(r))], sems, a * N_LINKS + r - 1, peer(r))
                    for a in range(self.n) for r in range(1, N_DEV)]
        return local, sends, arrivals

    def start(self, ins, outs, sems, local_sems):
        local, sends, _ = self._plan(ins, outs, sems, local_sems)
        for cp in local + sends:
            cp.start()

    def finish(self, ins, outs, sems, local_sems):
        local, sends, arrivals = self._plan(ins, outs, sems, local_sems)
        for cp in arrivals:
            cp.wait_recv()
        for cp in sends:
            cp.wait_send()
        for cp in local:
            cp.wait()


def _exchange(name, plan):
    def body(*refs):
        parts = plan.split(refs)
        plan.start(*parts)
        plan.finish(*parts)

    return pl.pallas_call(
        body, name=name,
        in_specs=plan.specs,
        out_specs=plan.specs,
        out_shape=plan.out_shape(),
        scratch_shapes=plan.scratch,
    )(*plan.arrays)


def _call_carrying(body, plan, operands, *, name, grid, in_specs, out_specs, out_shape, scratch_shapes, compiler_params):
    if plan is None:
        outs = pl.pallas_call(body, name=name, grid=grid, in_specs=in_specs, out_specs=out_specs, out_shape=out_shape,
                              scratch_shapes=scratch_shapes, compiler_params=compiler_params)(*operands)
        return outs, []
    n_i, n_o, n_s, k = len(in_specs), len(out_specs), len(scratch_shapes), plan.n

    def whole(*refs):
        cut = [n_i, n_i + k, n_i + k + n_o, n_i + 2 * k + n_o, n_i + 2 * k + n_o + n_s]
        own_in, ex_in, own_out, ex_out, own_scr, ex_scr = (refs[a:b] for a, b in zip([0] + cut, cut + [len(refs)]))
        parts = plan.split(ex_in + ex_out + ex_scr)
        first = last = True
        for axis, size in enumerate(grid):
            first = first & (pl.program_id(axis) == 0)
            last = last & (pl.program_id(axis) == size - 1)

        @pl.when(first)
        def _():
            plan.start(*parts)

        body(*own_in, *own_out, *own_scr)

        @pl.when(last)
        def _():
            plan.finish(*parts)

    outs = pl.pallas_call(
        whole, name=name, grid=grid,
        in_specs=list(in_specs) + plan.specs, out_specs=list(out_specs) + plan.specs,
        out_shape=list(out_shape) + plan.out_shape(), scratch_shapes=list(scratch_shapes) + plan.scratch,
        compiler_params=compiler_params,
    )(*operands, *plan.arrays)
    return outs[:n_o], outs[n_o:]


def _row_tile(rows, target=256):
    best = rows
    for cand in range(16, min(rows, target) + 1, 16):
        if rows % cand == 0:
            best = cand
    return best


def _sum_blocks(name, blocks, after=None):
    rows, width = blocks.shape[-2:]
    tm = _row_tile(rows)

    def body(x_ref, *rest):
        acc = x_ref[0].astype(F32)
        for d in range(1, N_DEV):
            acc = acc + x_ref[d].astype(F32)
        rest[-1][...] = acc

    ordered = [] if after is None else [after]
    return pl.pallas_call(
        body, name=name,
        grid=(rows // tm,),
        in_specs=[pl.BlockSpec((N_DEV, tm, width), lambda i: (0, i, 0))] + [pl.BlockSpec(memory_space=pl.ANY)] * len(ordered),
        out_specs=pl.BlockSpec((tm, width), lambda i: (i, 0)),
        out_shape=jax.ShapeDtypeStruct((rows, width), F32),
        compiler_params=pltpu.CompilerParams(dimension_semantics=("parallel",)),
    )(blocks, *ordered)


def _split_plan(src_ref, land_ref, sems):
    x, y, c = _place()
    me = _block_of((x, y, c))

    def peer(r):
        return (1 - x if r & 4 else x, 1 - y if r & 2 else y, 1 - c if r & 1 else c)

    sends = [_remote_copy(src_ref.at[_block_of(peer(r))], land_ref.at[me], sems, r - 1, peer(r)) for r in range(1, N_DEV)]
    arrivals = [_remote_copy(src_ref.at[me], land_ref.at[_block_of(peer(r))], sems, r - 1, peer(r)) for r in range(1, N_DEV)]
    return sends, arrivals


def _scatter_begin(name, blocks):
    def body(src_ref, land_ref, send_sems, recv_sems, src_thru, land_thru, token_ref):
        for cp in _split_plan(src_ref, land_ref, (send_sems, recv_sems))[0]:
            cp.start()
        token_ref[...] = jnp.zeros_like(token_ref)

    hbm, sem = pl.BlockSpec(memory_space=pltpu.HBM), pl.BlockSpec(memory_space=pltpu.SEMAPHORE)
    zone = pltpu.HBM(blocks.shape, blocks.dtype)
    *handles, token = pl.pallas_call(
        body, name=name,
        in_specs=(hbm, hbm),
        out_specs=(sem, sem, hbm, hbm, pl.BlockSpec(memory_space=pltpu.VMEM)),
        out_shape=(pltpu.SemaphoreType.DMA((N_LINKS,)), pltpu.SemaphoreType.DMA((N_LINKS,)), zone, zone,
                   jax.ShapeDtypeStruct((8, SLOT), F32)),
        input_output_aliases={0: 2, 1: 3},
        compiler_params=pltpu.CompilerParams(has_side_effects=pltpu.SideEffectType.DATAFLOW_SIDE_EFFECTING),
    )(pltpu.with_memory_space_constraint(blocks, pltpu.HBM),
      pltpu.with_memory_space_constraint(lax.empty(blocks.shape, blocks.dtype), pltpu.HBM))
    return handles, token


def _scatter_end(name, handles, after):
    send_sems, recv_sems, src, zone = handles

    def body(src_ref, land_ref, send_sems, recv_sems, after_ref, src_dead, got_ref):
        sends, arrivals = _split_plan(src_ref, land_ref, (send_sems, recv_sems))
        for cp in arrivals:
            cp.wait_recv()
        for cp in sends:
            cp.wait_send()

    hbm, sem = pl.BlockSpec(memory_space=pltpu.HBM), pl.BlockSpec(memory_space=pltpu.SEMAPHORE)
    sent, landed = pl.pallas_call(
        body, name=name,
        in_specs=(hbm, hbm, sem, sem, pl.BlockSpec(memory_space=pl.ANY)),
        out_specs=(hbm, hbm),
        out_shape=(pltpu.HBM(src.shape, src.dtype), pltpu.HBM(zone.shape, zone.dtype)),
        input_output_aliases={0: 0, 1: 1},
        compiler_params=pltpu.CompilerParams(has_side_effects=pltpu.SideEffectType.DATAFLOW_SIDE_EFFECTING),
    )(src, zone, send_sems, recv_sems, after)
    me = _block_of(_place())
    return lax.dynamic_update_slice_in_dim(landed, lax.dynamic_slice_in_dim(sent, me, 1, axis=0), me, axis=0)


def _adamw_values(wv, gv, mv, vv):
    m2 = ADAM_B1 * mv + (1.0 - ADAM_B1) * gv
    v2 = ADAM_B2 * vv + (1.0 - ADAM_B2) * jnp.square(gv)
    m_hat = m2 / (1.0 - ADAM_B1 ** ADAM_STEP)
    v_hat = v2 / (1.0 - ADAM_B2 ** ADAM_STEP)
    return [-ADAM_LR * (m_hat / (jnp.sqrt(v_hat) + ADAM_EPS) + ADAM_WD * wv), m2, v2]


def _adamw(name, w, g, m, v):
    def fn(rows, consts):
        return _adamw_values(*rows), []

    return _rowwise(name, fn, [w, g, m, v], [], [(w.shape[1], F32)] * 3, tm=_row_tile(w.shape[0]))


def _sum_adamw(name, blocks, w, m, v, after=None):
    rows, width = w.shape
    tm = _row_tile(rows)

    def body(x_ref, w_ref, m_ref, v_ref, *rest):
        acc = x_ref[0].astype(F32)
        for d in range(1, N_DEV):
            acc = acc + x_ref[d].astype(F32)
        rest[-4][...] = acc
        for ref, val in zip(rest[-3:], _adamw_values(w_ref[...], acc, m_ref[...], v_ref[...])):
            ref[...] = val

    ordered = [] if after is None else [after]
    tile = pl.BlockSpec((tm, width), lambda i: (i, 0))
    return pl.pallas_call(
        body, name=name,
        grid=(rows // tm,),
        in_specs=[pl.BlockSpec((N_DEV, tm, width), lambda i: (0, i, 0))] + [tile] * 3 + [pl.BlockSpec(memory_space=pl.ANY)] * len(ordered),
        out_specs=[tile] * 4,
        out_shape=[jax.ShapeDtypeStruct((rows, width), F32)] * 4,
        compiler_params=pltpu.CompilerParams(dimension_semantics=("parallel",)),
    )(blocks, w, m, v, *ordered)


ROW = 1024
FFN_NAMES = ("ffn1_w_gate", "ffn1_w_up", "ffn1_w_down", "ffn2_w_gate", "ffn2_w_up", "ffn2_w_down")
OTHER = {"w_in": "w_in_t", "mla_w_uq": "uq_t", "mla_w_ukv": "ukv_t", "w_out": "w_out"}
BY_COLUMNS = ("ffn1_w_gate", "ffn1_w_up", "ffn2_w_gate", "ffn2_w_up", "w_in", "mla_w_uq", "mla_w_ukv")
SMALL = {
    "ffn1_pre_g": (1024, 1024), "ffn1_post_g": (1024, 1024), "mix_pre_g": (1024, 1024), "mla_q_norm_g": (256, 256),
    "mla_kv_norm_g": (128, 128), "mla_out_g": (512, 512), "gdn_a_log": (8, 128), "gdn_dt_bias": (8, 128),
    "gdn_norm_g": (64, 128), "mix_post_g": (1024, 1024), "ffn2_pre_g": (1024, 1024), "ffn2_post_g": (1024, 1024),
}
CONV_SHAPE = (GDN_CONV, 3 * N_HEADS * GDN_D)
CONV_SHARD = (GDN_CONV, CONV_SHAPE[1] // N_DEV)
CONV_LANES = CONV_SHAPE[0] * CONV_SHAPE[1]
SMALL_ROWS = 8
REDUCE_ROWS = 16


def _pack_small(vecs, conv, rows):
    parts = [_pad_lanes(vecs[n].reshape(1, -1), 0, r) for n, (_, r) in SMALL.items()]
    parts.append(conv.reshape(1, -1))
    flat = jnp.concatenate(parts, axis=1)
    return _pad_lanes(flat, 0, rows * ROW).reshape(rows, ROW)


def _unpack_small(buf):
    flat = buf.reshape(1, -1)
    out, at = {}, 0
    for n, (w, r) in SMALL.items():
        out[n] = flat[:, at:at + w]
        at += r
    return out, flat[0, at:]


def kernel(x, positions, ffn1_pre_g, ffn1_w_gate, ffn1_w_up, ffn1_w_down, ffn1_post_g, mix_pre_g, w_in, mla_q_norm_g, mla_w_uq, mla_kv_norm_g, mla_w_ukv, mla_out_g, gdn_conv_w, gdn_a_log, gdn_dt_bias, gdn_norm_g, w_out, mix_post_g, ffn2_pre_g, ffn2_w_gate, ffn2_w_up, ffn2_w_down, ffn2_post_g, loss_target, m_ffn1_pre_g, m_ffn1_w_gate, m_ffn1_w_up, m_ffn1_w_down, m_ffn1_post_g, m_mix_pre_g, m_w_in, m_mla_q_norm_g, m_mla_w_uq, m_mla_kv_norm_g, m_mla_w_ukv, m_mla_out_g, m_gdn_conv_w, m_gdn_a_log, m_gdn_dt_bias, m_gdn_norm_g, m_w_out, m_mix_post_g, m_ffn2_pre_g, m_ffn2_w_gate, m_ffn2_w_up, m_ffn2_w_down, m_ffn2_post_g, v_ffn1_pre_g, v_ffn1_w_gate, v_ffn1_w_up, v_ffn1_w_down, v_ffn1_post_g, v_mix_pre_g, v_w_in, v_mla_q_norm_g, v_mla_w_uq, v_mla_kv_norm_g, v_mla_w_ukv, v_mla_out_g, v_gdn_conv_w, v_gdn_a_log, v_gdn_dt_bias, v_gdn_norm_g, v_w_out, v_mix_post_g, v_ffn2_pre_g, v_ffn2_w_gate, v_ffn2_w_up, v_ffn2_w_down, v_ffn2_post_g):
    given = dict(locals())
    order = ["ffn1_pre_g", "ffn1_w_gate", "ffn1_w_up", "ffn1_w_down", "ffn1_post_g", "mix_pre_g", "w_in", "mla_q_norm_g",
             "mla_w_uq", "mla_kv_norm_g", "mla_w_ukv", "mla_out_g", "gdn_conv_w", "gdn_a_log", "gdn_dt_bias", "gdn_norm_g",
             "w_out", "mix_post_g", "ffn2_pre_g", "ffn2_w_gate", "ffn2_w_up", "ffn2_w_down", "ffn2_post_g"]
    assert sorted(order) == sorted(list(FFN_NAMES) + list(OTHER) + list(SMALL) + ["gdn_conv_w"])

    def drop_depth(a):
        return a[0] if a.ndim == 3 else a

    wts = {n: drop_depth(given[n]) for n in order}
    mom = {n: drop_depth(given["m_" + n]) for n in order}
    var = {n: drop_depth(given["v_" + n]) for n in order}
    me = _block_of(_place())

    def wire(n):
        return (wts[n].T if n in BY_COLUMNS else wts[n]).astype(BF16)

    (ffn1,) = _exchange("gather_first", _Gather([jnp.stack([wire(n) for n in FFN_NAMES[:3]])]))
    mid = _Gather([wire(n) for n in ("w_in", "mla_w_uq", "mla_w_ukv")] + [wts["gdn_conv_w"]])
    late = _Gather([jnp.stack([wire(n) for n in FFN_NAMES[3:]]), wire("w_out")])
    full = {n: wts[n] for n in SMALL}
    full["ffn1"] = ffn1

    dx, grads, landed, begun, token = _local_step(x[0], positions[0], loss_target[0], full, mid, late)

    grad, outs = {}, {"delta": {}, "new_m": {}, "new_v": {}}

    def finish(n, blocks, after=None):
        flip = n in BY_COLUMNS and wts[n].shape[1] % SLOT != 0
        turn = (lambda a: a.T) if flip else (lambda a: a)
        if n in BY_COLUMNS and not flip:
            grad[n] = _sum_blocks("sum_" + n, blocks, after=after).T
            new = _adamw("adamw_" + n, wts[n], grad[n], mom[n], var[n])
        else:
            total, *new = _sum_adamw("update_" + n, blocks, turn(wts[n]), turn(mom[n]), turn(var[n]), after=after)
            grad[n] = turn(total)
        outs["delta"][n], outs["new_m"][n], outs["new_v"][n] = (turn(a) for a in new)
        return new[2]

    for n, blocks in landed.items():
        token = finish(n, blocks, after=token)
    small_handles = begun.pop("small")
    for n, handles in begun.items():
        token = finish(n, _scatter_end("scatter_" + n + "_end", handles, after=token))

    small_sum = _sum_blocks("sum_small", _scatter_end("reduce_small_end", small_handles, after=token))
    loss = small_sum[REDUCE_ROWS - 1, ROW - 1]
    small_grad, conv_grad_full = _unpack_small(small_sum)
    grad.update(small_grad)
    grad["gdn_conv_w"] = lax.dynamic_slice(conv_grad_full[:CONV_LANES].reshape(CONV_SHAPE), (0, me * CONV_SHARD[1]), CONV_SHARD)
    outs["grad"] = grad
    small = [_pack_small(s, s["gdn_conv_w"].reshape(-1), SMALL_ROWS) for s in (wts, grad, mom, var)]
    for kind, s in zip(("delta", "new_m", "new_v"), _adamw("adamw_small", *small)):
        vecs, conv = _unpack_small(s)
        outs[kind].update(vecs)
        outs[kind]["gdn_conv_w"] = conv[:CONV_SHARD[0] * CONV_SHARD[1]].reshape(CONV_SHARD)
    result = [loss, dx[None]]
    for kind in ("grad", "delta", "new_m", "new_v"):
        result += [outs[kind][n].reshape(given[n].shape) for n in order]
    return tuple(result)
```

```python
import jax
import jax.numpy as jnp
from jax import lax
from jax.experimental import pallas as pl
from jax.experimental.pallas import tpu as pltpu

F32 = jnp.float32
BF16 = jnp.bfloat16
HI = lax.Precision.HIGH

N_DEV = 8
N_HEADS = 8
SLOT = 128
MLA_Q_RANK = 256
MLA_KV_RANK = 128
MLA_NOPE = 64
MLA_ROPE = 32
MLA_V = 64
GDN_D = 64
GDN_CONV = 4
GDN_CHUNK = 64
ROPE_THETA = 10000.0
EPS = 1e-6
ADAM_LR, ADAM_B1, ADAM_B2, ADAM_EPS, ADAM_WD, ADAM_STEP = 0.001, 0.9, 0.999, 1e-08, 0.01, 10


def _dot(a, b, ca, cb, precision=None):
    lead = a.ndim - 2
    batch = tuple(range(lead))
    return lax.dot_general(a, b, (((lead + ca,), (lead + cb,)), (batch, batch)), precision=precision,
                           preferred_element_type=F32)


def _nn(a, b, precision=None):
    return _dot(a, b, 1, 0, precision)


def _nt(a, b, precision=None):
    return _dot(a, b, 1, 1, precision)


def _tn(a, b, precision=None):
    return _dot(a, b, 0, 0, precision)


def _sigmoid(x):
    return 1.0 / (1.0 + jnp.exp(-x))


def _silu(x):
    return x * _sigmoid(x)


def _rms(x, g, n):
    ms = jnp.sum(x * x, axis=-1, keepdims=True) * (1.0 / n)
    return x * lax.rsqrt(ms + EPS) * g


def _chunk_masks():
    c = GDN_CHUNK
    i = lax.broadcasted_iota(jnp.int32, (c, c), 0)
    j = lax.broadcasted_iota(jnp.int32, (c, c), 1)
    lower = i >= j
    strict = i > j
    eye = (i == j).astype(F32)
    blocks = []
    b = 1
    while b < c:
        same = (i // (2 * b)) == (j // (2 * b))
        blocks.append(same & ((i % (2 * b)) >= b) & ((j % (2 * b)) < b))
        b *= 2
    return lower, strict, eye, blocks


def _unit_lower_inverse(low, eye, blocks):
    t = eye - jnp.where(blocks[0], low, 0.0)
    for m in blocks[1:]:
        lo = jnp.where(m, low, 0.0)
        t = t - _nn(t, _nn(lo, t, HI), HI)
    return t


@jax.custom_vjp
def _known_inverse(low, tinv):
    return tinv


def _known_inverse_fwd(low, tinv):
    return tinv, tinv


def _known_inverse_bwd(tinv, dt):
    return -_tn(tinv, _nt(dt, tinv, HI), HI), jnp.zeros_like(tinv)


_known_inverse.defvjp(_known_inverse_fwd, _known_inverse_bwd)

_PRODUCTS = {"nn": _nn, "nt": _nt, "tn": _tn}


@jax.custom_vjp
def _known_nn(a, b, c):
    return c


@jax.custom_vjp
def _known_nt(a, b, c):
    return c


@jax.custom_vjp
def _known_tn(a, b, c):
    return c


def _known_fwd(a, b, c):
    return c, (a, b, c)


_known_nn.defvjp(_known_fwd, lambda r, dc: (_nt(dc, r[1], HI), _tn(r[0], dc, HI), jnp.zeros_like(r[2])))
_known_nt.defvjp(_known_fwd, lambda r, dc: (_nn(dc, r[1], HI), _tn(dc, r[0], HI), jnp.zeros_like(r[2])))
_known_tn.defvjp(_known_fwd, lambda r, dc: (_nt(r[1], dc, HI), _nn(r[0], dc, HI), jnp.zeros_like(r[2])))
_KNOWN = {"nn": _known_nn, "nt": _known_nt, "tn": _known_tn}
GDN_PRODUCTS = 8
GDN_KEPT = 2 + GDN_PRODUCTS


def _gdn_chunk(q, k, v, gc, bb, s, masks, known=None):
    lower, strict, eye, blocks = masks
    made = []

    def product(kind, a, b):
        c = _PRODUCTS[kind](a, b, HI) if known is None else _KNOWN[kind](a, b, known[1 + len(made)])
        made.append(c)
        return c

    qs = q * (GDN_D ** -0.5)
    gct = jnp.swapaxes(gc, -1, -2)
    decay = jnp.exp(jnp.where(lower, gc - gct, -1e30))
    kb = k * bb
    low = jnp.where(strict, product("nt", kb, k) * decay, 0.0)
    tinv = _unit_lower_inverse(low, eye, blocks) if known is None else _known_inverse(low, known[0])
    eg = jnp.exp(gc)
    w = product("nn", tinv, kb * eg)
    u = product("nn", tinv, v * bb)
    attn = product("nt", qs, k) * decay
    last = lax.broadcasted_iota(jnp.int32, gc.shape[-2:], 0) == GDN_CHUNK - 1
    g_end = jnp.sum(jnp.where(last, gc, 0.0), axis=-2, keepdims=True)
    k_dec = k * jnp.exp(g_end - gc)
    v_new = u - product("nn", w, s)
    o = product("nn", qs * eg, s) + product("nn", attn, v_new)
    s_new = s * jnp.exp(g_end) + product("tn", k_dec, v_new)
    assert len(made) == GDN_PRODUCTS
    return o, s_new, [tinv] + made


GDN_GROUP = 8
GDN_GROUPS = N_HEADS // GDN_GROUP


def _group_heads(ref):
    return jnp.stack([ref[:, pl.ds(j * SLOT, GDN_D)] for j in range(GDN_GROUP)])


def _ungroup_heads(ref, val):
    pad = jnp.zeros((GDN_CHUNK, SLOT - GDN_D), F32)
    for j in range(GDN_GROUP):
        ref[:, pl.ds(j * SLOT, GDN_D)] = val[j]
        ref[:, pl.ds(j * SLOT + GDN_D, SLOT - GDN_D)] = pad


def _gdn_fwd(qkv, gb, bb, carry=None):
    t = qkv.shape[0]
    n_chunks = t // GDN_CHUNK
    d = GDN_D

    def body(q_ref, k_ref, v_ref, g_ref, b_ref, o_ref, keep_ref, s_ref):
        @pl.when(pl.program_id(1) == 0)
        def _():
            s_ref[...] = jnp.zeros_like(s_ref)

        s = s_ref[...]
        keep_ref[:, 0, 0] = s
        o, s_new, made = _gdn_chunk(*[_group_heads(r) for r in (q_ref, k_ref, v_ref, g_ref, b_ref)], s, _chunk_masks())
        for i, val in enumerate(made):
            keep_ref[:, 0, 1 + i] = val
        s_ref[...] = s_new
        _ungroup_heads(o_ref, o)

    def spec(kind=0):
        return pl.BlockSpec((GDN_CHUNK, GDN_GROUP * SLOT), lambda h, n: (n, kind * GDN_GROUPS + h))

    return _call_carrying(
        body, carry, (qkv, qkv, qkv, gb, bb), name="gdn_fwd",
        grid=(GDN_GROUPS, n_chunks),
        in_specs=[spec(0), spec(1), spec(2), spec(), spec()],
        out_specs=[spec(), pl.BlockSpec((GDN_GROUP, 1, GDN_KEPT, d, d), lambda h, n: (h, n, 0, 0, 0))],
        out_shape=[jax.ShapeDtypeStruct((t, N_HEADS * SLOT), F32), jax.ShapeDtypeStruct((N_HEADS, n_chunks, GDN_KEPT, d, d), F32)],
        scratch_shapes=[pltpu.VMEM((GDN_GROUP, d, d), F32)],
        compiler_params=pltpu.CompilerParams(dimension_semantics=("arbitrary", "arbitrary")),
    )


def _gdn_bwd(qkv, gb, bb, keep, do, carry=None):
    t = qkv.shape[0]
    n_chunks = t // GDN_CHUNK
    d = GDN_D

    def body(q_ref, k_ref, v_ref, g_ref, b_ref, keep_ref, do_ref, dqkv_ref, dg_ref, db_ref, ds_ref):
        @pl.when(pl.program_id(1) == 0)
        def _():
            ds_ref[...] = jnp.zeros_like(ds_ref)

        masks = _chunk_masks()
        known = [keep_ref[:, 0, 1 + i] for i in range(GDN_KEPT - 1)]
        _, pull = jax.vjp(lambda *a: _gdn_chunk(*a, masks, known)[:2],
                          *[_group_heads(r) for r in (q_ref, k_ref, v_ref, g_ref, b_ref)], keep_ref[:, 0, 0])
        dq, dk, dv, dg, db, ds = pull((_group_heads(do_ref), ds_ref[...]))
        ds_ref[...] = ds
        for i, val in enumerate((dq, dk, dv)):
            _ungroup_heads(dqkv_ref.at[i], val)
        _ungroup_heads(dg_ref, dg)
        _ungroup_heads(db_ref, db)

    def spec(kind=0):
        return pl.BlockSpec((GDN_CHUNK, GDN_GROUP * SLOT), lambda h, n: (n_chunks - 1 - n, kind * GDN_GROUPS + h))

    return _call_carrying(
        body, carry, (qkv, qkv, qkv, gb, bb, keep, do), name="gdn_bwd",
        grid=(GDN_GROUPS, n_chunks),
        in_specs=[spec(0), spec(1), spec(2), spec(), spec(),
                  pl.BlockSpec((GDN_GROUP, 1, GDN_KEPT, d, d), lambda h, n: (h, n_chunks - 1 - n, 0, 0, 0)), spec()],
        out_specs=[pl.BlockSpec((3, GDN_CHUNK, GDN_GROUP * SLOT), lambda h, n: (0, n_chunks - 1 - n, h)), spec(), spec()],
        out_shape=[jax.ShapeDtypeStruct((3, t, N_HEADS * SLOT), F32)] + [jax.ShapeDtypeStruct((t, N_HEADS * SLOT), F32)] * 2,
        scratch_shapes=[pltpu.VMEM((GDN_GROUP, d, d), F32)],
        compiler_params=pltpu.CompilerParams(dimension_semantics=("arbitrary", "arbitrary")),
    )


def _rowwise(name, fn, rows, consts, outs, sums=(), tm=512):
    rows = [x if isinstance(x, tuple) else (x, x.shape[1], 0) for x in rows]
    t = rows[0][0].shape[0]
    tm = min(tm, t)
    steps = t // tm
    n_r, n_c, n_o, n_s = len(rows), len(consts), len(outs), len(sums)

    def window(width, block):
        return pl.BlockSpec((tm, width), lambda i: (i, block))

    def body(*refs):
        r, c = refs[:n_r], refs[n_r:n_r + n_c]
        o, s = refs[n_r + n_c:n_r + n_c + n_o], refs[n_r + n_c + n_o:]
        vals, tot = fn([x[...] for x in r], [x[...] for x in c])
        for ref, val in zip(o, vals):
            ref[...] = val.astype(ref.dtype)
        if n_s:
            @pl.when(pl.program_id(0) == 0)
            def _():
                for ref in s:
                    ref[...] = jnp.zeros_like(ref)

            for ref, val in zip(s, tot):
                ref[...] += val

    return pl.pallas_call(
        body, name=name,
        grid=(steps,),
        in_specs=[window(w, b) for _, w, b in rows] + [pl.BlockSpec(x.shape, lambda i: (0, 0)) for x in consts],
        out_specs=[pl.BlockSpec((tm, w), lambda i: (i, 0)) for w, _ in outs]
        + [pl.BlockSpec((1, w), lambda i: (0, 0)) for w in sums],
        out_shape=[jax.ShapeDtypeStruct((t, w), dt) for w, dt in outs]
        + [jax.ShapeDtypeStruct((1, w), F32) for w in sums],
        compiler_params=pltpu.CompilerParams(dimension_semantics=("arbitrary",)),
    )(*[x for x, _, _ in rows], *consts)


def _tile(dim, target):
    if dim <= target:
        return dim
    best = None
    for cand in range(128, target + 1, 128):
        if dim % cand == 0:
            best = cand
    assert best is not None, (dim, target)
    return best


def _matmul(name, a, b, mode, out_dtype=F32, tm=1024, tn=1024, tk=2048, after=None):
    if mode == "nn":
        (m, k), n = a.shape, b.shape[1]
    elif mode == "nt":
        (m, k), n = a.shape, b.shape[0]
    else:
        (k, m), n = a.shape, b.shape[1]
    tm, tn, tk = _tile(m, tm), _tile(n, tn), _tile(k, tk)
    k_steps = k // tk
    product = {"nn": _nn, "nt": _nt, "tn": _tn}[mode]

    def body(a_ref, b_ref, *rest):
        o_ref, acc_ref = rest[-2:]
        part = product(a_ref[...].astype(BF16), b_ref[...].astype(BF16))
        if k_steps == 1:
            o_ref[...] = part.astype(o_ref.dtype)
        else:
            kk = pl.program_id(2)

            @pl.when(kk == 0)
            def _():
                acc_ref[...] = part

            @pl.when(kk > 0)
            def _():
                acc_ref[...] += part

            @pl.when(kk == k_steps - 1)
            def _():
                o_ref[...] = acc_ref[...].astype(o_ref.dtype)

    a_spec = pl.BlockSpec((tk, tm), lambda i, j, kk: (kk, i)) if mode == "tn" else pl.BlockSpec((tm, tk), lambda i, j, kk: (i, kk))
    b_spec = pl.BlockSpec((tn, tk), lambda i, j, kk: (j, kk)) if mode == "nt" else pl.BlockSpec((tk, tn), lambda i, j, kk: (kk, j))
    ordered = [] if after is None else [after]
    return pl.pallas_call(
        body, name=name,
        grid=(m // tm, n // tn, k_steps),
        in_specs=[a_spec, b_spec] + [pl.BlockSpec(memory_space=pl.ANY)] * len(ordered),
        out_specs=pl.BlockSpec((tm, tn), lambda i, j, kk: (i, j)),
        out_shape=jax.ShapeDtypeStruct((m, n), out_dtype),
        scratch_shapes=[pltpu.VMEM((tm, tn) if k_steps > 1 else (8, 128), F32)],
        compiler_params=pltpu.CompilerParams(dimension_semantics=("parallel", "parallel", "arbitrary")),
    )(a, b, *ordered)


FFN_TM = 512
FFN_BWD_TM = 256
FFN_BLOCKS = 4
FFN_GATE, FFN_UP, FFN_DOWN = 0, 1, 2


def _ffn_weight_specs(ffn_w, first):
    _, _, rows, dm = ffn_w.shape

    def spec(k):
        return pl.BlockSpec((FFN_BLOCKS, None, rows, dm), lambda i, j: (j, first + k, 0, 0))

    return [spec(FFN_GATE), spec(FFN_UP), spec(FFN_DOWN)], FFN_BLOCKS * rows


def _ffn_fwd(name, x, g_pre, ffn_w, first, g_post, carry=None, keep_act=False):
    t, dm = x.shape
    tm = min(FFN_TM, t)
    w_specs, tf = _ffn_weight_specs(ffn_w, first)
    f_steps = N_DEV // FFN_BLOCKS

    def body(x_ref, gpre_ref, wg_ref, wu_ref, wd_ref, gpost_ref, h_ref, y_ref, hg_ref, hu_ref, *rest):
        *act_ref, xn_ref, acc_ref = rest
        j = pl.program_id(1)

        @pl.when(j == 0)
        def _():
            xn_ref[...] = _rms(x_ref[...], gpre_ref[...], dm).astype(BF16)
            acc_ref[...] = jnp.zeros_like(acc_ref)

        xn = xn_ref[...]
        wg, wu, wd = (r[...].reshape(tf, dm) for r in (wg_ref, wu_ref, wd_ref))
        hg, hu = _nt(xn, wg), _nt(xn, wu)
        hg_ref[...] = hg.astype(BF16)
        hu_ref[...] = hu.astype(BF16)
        a = (_silu(hg) * hu).astype(BF16)
        for ref in act_ref:
            ref[...] = a
        acc_ref[...] += _nn(a, wd)

        @pl.when(j == f_steps - 1)
        def _():
            h = acc_ref[...]
            h_ref[...] = h
            y_ref[...] = x_ref[...] + 0.5 * _rms(h, gpost_ref[...], dm)

    row = pl.BlockSpec((tm, dm), lambda i, j: (i, 0))
    vec = pl.BlockSpec((1, dm), lambda i, j: (0, 0))
    wide = pl.BlockSpec((tm, tf), lambda i, j: (i, j))
    n_wide = 3 if keep_act else 2
    return _call_carrying(
        body, carry, (x, g_pre, ffn_w, ffn_w, ffn_w, g_post), name=name,
        grid=(t // tm, f_steps),
        in_specs=[row, vec, *w_specs, vec],
        out_specs=[row, row] + [wide] * n_wide,
        out_shape=[jax.ShapeDtypeStruct((t, dm), F32)] * 2 + [jax.ShapeDtypeStruct((t, f_steps * tf), BF16)] * n_wide,
        scratch_shapes=[pltpu.VMEM((tm, dm), BF16), pltpu.VMEM((tm, dm), F32)],
        compiler_params=pltpu.CompilerParams(dimension_semantics=("arbitrary", "arbitrary")),
    )


def _ffn_bwd(name, x, h, hg, hu, dy, g_pre, ffn_w, first, g_post, carry=None, after=None):
    t, dm = x.shape
    tm = min(FFN_BWD_TM, t)
    w_specs, tf = _ffn_weight_specs(ffn_w, first)
    f_steps = N_DEV // FFN_BLOCKS
    f = f_steps * tf

    def post(hv, g):
        return 0.5 * _rms(hv, g, dm)

    def pre(xv, g):
        return _rms(xv, g, dm)

    def body(x_ref, h_ref, dy_ref, hg_ref, hu_ref, gpre_ref, wg_ref, wu_ref, wd_ref, gpost_ref,
             dx_ref, xn_ref, dh_ref, a_ref, dhg_ref, dhu_ref, dgpre_ref, dgpost_ref, acc_ref):
        i, j = pl.program_id(0), pl.program_id(1)

        @pl.when((i == 0) & (j == 0))
        def _():
            dgpre_ref[...] = jnp.zeros_like(dgpre_ref)
            dgpost_ref[...] = jnp.zeros_like(dgpost_ref)

        @pl.when(j == 0)
        def _():
            xn_ref[...] = pre(x_ref[...], gpre_ref[...]).astype(BF16)
            _, pull = jax.vjp(post, h_ref[...], gpost_ref[...])
            dh, dg = pull(dy_ref[...])
            dh_ref[...] = dh.astype(BF16)
            dgpost_ref[...] += dg
            acc_ref[...] = jnp.zeros_like(acc_ref)

        wg, wu, wd = (r[...].reshape(tf, dm) for r in (wg_ref, wu_ref, wd_ref))
        hg, hu = hg_ref[...].astype(F32), hu_ref[...].astype(F32)
        da = _nt(dh_ref[...], wd)
        sig = _sigmoid(hg)
        act = hg * sig
        dhu = (da * act).astype(BF16)
        dhg = (da * hu * (sig * (1.0 + hg * (1.0 - sig)))).astype(BF16)
        a_ref[...] = (act * hu).astype(BF16)
        dhg_ref[...] = dhg
        dhu_ref[...] = dhu
        acc_ref[...] += _nn(dhg, wg) + _nn(dhu, wu)

        @pl.when(j == f_steps - 1)
        def _():
            _, pull = jax.vjp(pre, x_ref[...], gpre_ref[...])
            dx, dg = pull(acc_ref[...])
            dx_ref[...] = dy_ref[...] + dx
            dgpre_ref[...] += dg

    row = pl.BlockSpec((tm, dm), lambda i, j: (i, 0))
    vec = pl.BlockSpec((1, dm), lambda i, j: (0, 0))
    wide = pl.BlockSpec((tm, tf), lambda i, j: (i, j))
    ordered = [] if after is None else [after]

    def after_it(*refs):
        body(*refs[:10], *refs[10 + len(ordered):])

    return _call_carrying(
        after_it, carry, (x, h, dy, hg, hu, g_pre, ffn_w, ffn_w, ffn_w, g_post, *ordered), name=name,
        grid=(t // tm, f_steps),
        in_specs=[row, row, row, wide, wide, vec, *w_specs, vec] + [pl.BlockSpec(memory_space=pl.ANY)] * len(ordered),
        out_specs=[row, row, row, wide, wide, wide, vec, vec],
        out_shape=[jax.ShapeDtypeStruct((t, dm), F32), jax.ShapeDtypeStruct((t, dm), BF16), jax.ShapeDtypeStruct((t, dm), BF16),
                   jax.ShapeDtypeStruct((t, f), BF16), jax.ShapeDtypeStruct((t, f), BF16), jax.ShapeDtypeStruct((t, f), BF16),
                   jax.ShapeDtypeStruct((1, dm), F32), jax.ShapeDtypeStruct((1, dm), F32)],
        scratch_shapes=[pltpu.VMEM((tm, dm), F32)],
        compiler_params=pltpu.CompilerParams(dimension_semantics=("arbitrary", "arbitrary")),
    )


ATT_T = 512
ATT_GROUP = 4
ATT_GROUP_FWD = 8
ATT_SCALE = (MLA_NOPE + MLA_ROPE) ** -0.5


def _stack_slots(ref, group):
    return jnp.stack([ref[:, pl.ds(j * SLOT, SLOT)] for j in range(group)])


def _unstack_slots(ref, val):
    for j in range(val.shape[0]):
        ref[:, pl.ds(j * SLOT, SLOT)] = val[j].astype(ref.dtype)


def _scores(q, k, diagonal):
    s = _nt(q, k) * ATT_SCALE
    if diagonal:
        row = lax.broadcasted_iota(jnp.int32, s.shape[1:], 0)
        col = lax.broadcasted_iota(jnp.int32, s.shape[1:], 1)
        s = jnp.where(col <= row, s, -1e30)
    return s


def _attn_pairs(steps, q_major):
    pairs = ([(qi, ki) for qi in range(steps) for ki in range(qi + 1)] if q_major
             else [(qi, ki) for ki in range(steps) for qi in range(ki, steps)])
    return jnp.array([p[0] for p in pairs], jnp.int32), jnp.array([p[1] for p in pairs], jnp.int32)


def _attn_specs(tile, group):
    width = group * SLOT
    return (pl.BlockSpec((tile, width), lambda h, p, qt, kt: (qt[p], h)),
            pl.BlockSpec((tile, width), lambda h, p, qt, kt: (kt[p], h)))


def _attn_fwd(q, k, v):
    t = q.shape[0]
    tile = min(ATT_T, t)
    steps = t // tile
    g = ATT_GROUP_FWD

    strip = min(SLOT, tile)

    def body(qt_ref, kt_ref, q_ref, k_ref, v_ref, o_ref, lse_ref, m_ref, l_ref, alpha_ref, acc_ref, s_ref, p_ref):
        qi, ki = qt_ref[pl.program_id(1)], kt_ref[pl.program_id(1)]

        @pl.when(ki == 0)
        def _():
            m_ref[...] = jnp.full_like(m_ref, -1e30)
            l_ref[...] = jnp.zeros_like(l_ref)
            acc_ref[...] = jnp.zeros_like(acc_ref)

        def step(diagonal):
            s_ref[...] = _nt(_stack_slots(k_ref, g), _stack_slots(q_ref, g))
            for j in range(tile // strip):
                c = pl.ds(j * strip, strip)
                s = s_ref[:, :, c] * ATT_SCALE
                if diagonal:
                    key = lax.broadcasted_iota(jnp.int32, s.shape[1:], 0)
                    query = lax.broadcasted_iota(jnp.int32, s.shape[1:], 1) + j * strip
                    s = jnp.where(key <= query, s, -1e30)
                m_old = m_ref[:, :, c]
                m_new = jnp.maximum(m_old, jnp.max(s, axis=1, keepdims=True))
                p = jnp.exp(s - m_new)
                alpha = jnp.exp(m_old - m_new)
                l_ref[:, :, c] = alpha * l_ref[:, :, c] + jnp.sum(p, axis=1, keepdims=True)
                alpha_ref[:, :, c] = alpha
                m_ref[:, :, c] = m_new
                p_ref[:, :, c] = p.astype(BF16)
            acc_ref[...] = acc_ref[...] * alpha_ref[...] + _tn(_stack_slots(v_ref, g), p_ref[...])

        @pl.when(ki < qi)
        def _():
            step(False)

        @pl.when(ki == qi)
        def _():
            step(True)
            out = acc_ref[...] / l_ref[...]
            lse = jnp.broadcast_to(m_ref[...] + jnp.log(l_ref[...]), out.shape)
            for j in range(g):
                o_ref[:, pl.ds(j * SLOT, SLOT)] = out[j].T
                lse_ref[:, pl.ds(j * SLOT, SLOT)] = lse[j].T

    q_spec, k_spec = _attn_specs(tile, g)
    tables = _attn_pairs(steps, True)
    return pl.pallas_call(
        body, name="attn_fwd",
        grid_spec=pltpu.PrefetchScalarGridSpec(
            num_scalar_prefetch=2, grid=(N_HEADS // g, tables[0].shape[0]),
            in_specs=[q_spec, k_spec, k_spec], out_specs=[q_spec, q_spec],
            scratch_shapes=[pltpu.VMEM((g, 1, tile), F32), pltpu.VMEM((g, 1, tile), F32), pltpu.VMEM((g, 1, tile), F32),
                            pltpu.VMEM((g, SLOT, tile), F32), pltpu.VMEM((g, tile, tile), F32), pltpu.VMEM((g, tile, tile), BF16)]),
        out_shape=[jax.ShapeDtypeStruct((t, N_HEADS * SLOT), F32)] * 2,
        compiler_params=pltpu.CompilerParams(dimension_semantics=("parallel", "arbitrary")),
    )(*tables, q, k, v)


def _attn_grad_scores(q, k, v, do, lse_ref, delta_ref, diagonal):
    g = ATT_GROUP
    p = jnp.exp(_scores(q, k, diagonal) - _stack_slots(lse_ref, g)[:, :, 0:1])
    dp = _nt(do, v)
    return p, p * (dp - _stack_slots(delta_ref, g)[:, :, 0:1]) * ATT_SCALE


def _attn_bwd(q, k, v, do, lse, delta):
    t = q.shape[0]
    tile = min(ATT_T, t)
    steps = t // tile
    g = ATT_GROUP

    def body(qt_ref, kt_ref, q_ref, k_ref, v_ref, do_ref, lse_ref, delta_ref, dq_ref, dk_ref, dv_ref, dk_acc, dv_acc):
        qi, ki = qt_ref[pl.program_id(1)], kt_ref[pl.program_id(1)]

        @pl.when(pl.program_id(1) == 0)
        def _():
            dq_ref[...] = jnp.zeros_like(dq_ref)

        def step(diagonal):
            qq, kk = _stack_slots(q_ref, g), _stack_slots(k_ref, g)
            do_b = _stack_slots(do_ref, g).astype(BF16)
            p, ds = _attn_grad_scores(qq, kk, _stack_slots(v_ref, g), do_b, lse_ref, delta_ref, diagonal)
            ds = ds.astype(BF16)
            dv_acc[...] += _tn(p.astype(BF16), do_b)
            dk_acc[...] += _tn(ds, qq)
            dq = _nn(ds, kk)
            rows = pl.ds(pl.multiple_of(qi * tile, tile), tile)
            for j in range(g):
                dq_ref[rows, pl.ds(j * SLOT, SLOT)] += dq[j]

        @pl.when(qi == ki)
        def _():
            dk_acc[...] = jnp.zeros_like(dk_acc)
            dv_acc[...] = jnp.zeros_like(dv_acc)
            step(True)

        @pl.when(qi > ki)
        def _():
            step(False)

        @pl.when(qi == steps - 1)
        def _():
            _unstack_slots(dk_ref, dk_acc[...])
            _unstack_slots(dv_ref, dv_acc[...])

    q_spec, k_spec = _attn_specs(tile, g)
    tables = _attn_pairs(steps, False)
    return pl.pallas_call(
        body, name="attn_bwd",
        grid_spec=pltpu.PrefetchScalarGridSpec(
            num_scalar_prefetch=2, grid=(N_HEADS // g, tables[0].shape[0]),
            in_specs=[q_spec, k_spec, k_spec, q_spec, q_spec, q_spec],
            out_specs=[pl.BlockSpec((t, g * SLOT), lambda h, p, qt, kt: (0, h)), k_spec, k_spec],
            scratch_shapes=[pltpu.VMEM((g, tile, SLOT), F32), pltpu.VMEM((g, tile, SLOT), F32)]),
        out_shape=[jax.ShapeDtypeStruct((t, N_HEADS * SLOT), F32)] * 3,
        compiler_params=pltpu.CompilerParams(dimension_semantics=("parallel", "arbitrary")),
    )(*tables, q, k, v, do, lse, delta)


CONV_PAD = 8


def _fill_padded(ref, val):
    t = val.shape[0]
    zeros = jnp.zeros((CONV_PAD, val.shape[1]), val.dtype)
    ref[pl.ds(0, CONV_PAD)] = zeros
    ref[pl.ds(CONV_PAD + t, CONV_PAD)] = zeros
    ref[pl.ds(CONV_PAD, t)] = val


def _shifted(ref, s):
    return ref[pl.ds(CONV_PAD - s, ref.shape[0] - 2 * CONV_PAD)]


def _l2norm(x):
    return x * lax.rsqrt(jnp.sum(x * x, axis=-1, keepdims=True) + EPS)


def _conv_pre(x_pad, w):
    y = w[GDN_CONV - 1:GDN_CONV, :] * _shifted(x_pad, 0)
    for s in range(1, GDN_CONV):
        y = y + w[GDN_CONV - 1 - s:GDN_CONV - s, :] * _shifted(x_pad, s)
    return y


def _gdn_conv_fwd(x, w):
    t, width = x.shape

    def body(x_ref, w_ref, o_ref, x_pad):
        _fill_padded(x_pad, x_ref[...])
        act = _silu(_conv_pre(x_pad, w_ref[...]))
        normed = pl.program_id(0) < 2 * N_HEADS
        o_ref[...] = jnp.where(normed, _l2norm(act), act)

    return pl.pallas_call(
        body, name="gdn_conv_fwd",
        grid=(width // SLOT,),
        in_specs=[pl.BlockSpec((t, SLOT), lambda j: (0, j)), pl.BlockSpec((GDN_CONV, SLOT), lambda j: (0, j))],
        out_specs=pl.BlockSpec((t, SLOT), lambda j: (0, j)),
        out_shape=jax.ShapeDtypeStruct((t, width), F32),
        scratch_shapes=[pltpu.VMEM((t + 2 * CONV_PAD, SLOT), F32)],
        compiler_params=pltpu.CompilerParams(dimension_semantics=("parallel",)),
    )(x, w)


def _gdn_conv_bwd(x, w, dout):
    t, width = x.shape

    def body(x_ref, w_ref, do_ref, dx_ref, dw_ref, x_pad, dy_pad):
        wv = w_ref[...]
        _fill_padded(x_pad, x_ref[...])
        y = _conv_pre(x_pad, wv)
        sig = _sigmoid(y)
        act = y * sig
        _, pull = jax.vjp(_l2norm, act)
        normed = pl.program_id(0) < 2 * N_HEADS
        dact = jnp.where(normed, pull(do_ref[0])[0], do_ref[0])
        dy = dact * (sig * (1.0 + y * (1.0 - sig)))
        _fill_padded(dy_pad, dy)
        dx = wv[GDN_CONV - 1:GDN_CONV, :] * dy
        for s in range(1, GDN_CONV):
            dx = dx + wv[GDN_CONV - 1 - s:GDN_CONV - s, :] * _shifted(dy_pad, -s)
        dx_ref[...] = dx.astype(BF16)
        for s in range(GDN_CONV):
            dw_ref[GDN_CONV - 1 - s:GDN_CONV - s, :] = jnp.sum(dy * _shifted(x_pad, s), axis=0, keepdims=True)

    col = pl.BlockSpec((t, SLOT), lambda j: (0, j))
    tap = pl.BlockSpec((GDN_CONV, SLOT), lambda j: (0, j))
    return pl.pallas_call(
        body, name="gdn_conv_bwd",
        grid=(width // SLOT,),
        in_specs=[col, tap, pl.BlockSpec((1, t, SLOT), lambda j: (j // N_HEADS, 0, j % N_HEADS))],
        out_specs=[col, tap],
        out_shape=[jax.ShapeDtypeStruct((t, width), BF16), jax.ShapeDtypeStruct((GDN_CONV, width), F32)],
        scratch_shapes=[pltpu.VMEM((t + 2 * CONV_PAD, SLOT), F32)] * 2,
        compiler_params=pltpu.CompilerParams(dimension_semantics=("parallel",)),
    )(x, w, dout)


def _softplus(x):
    e = jnp.exp(-jnp.abs(x))
    u = 1.0 + e
    log1p = jnp.where(u == 1.0, e, jnp.log(u) * e / jnp.where(u == 1.0, 1.0, u - 1.0))
    return jnp.maximum(x, 0.0) + log1p


def _chunk_running_sum(x, reverse=False):
    tm = x.shape[0]
    at = lax.broadcasted_iota(jnp.int32, x.shape, 0) % GDN_CHUNK
    step = 1
    while step < GDN_CHUNK:
        if reverse:
            x = x + jnp.where(at < GDN_CHUNK - step, pltpu.roll(x, tm - step, 0), 0.0)
        else:
            x = x + jnp.where(at >= step, pltpu.roll(x, step, 0), 0.0)
        step *= 2
    return x


def _gates_fwd(ab, a_log, dt_bias):
    def fn(rows, consts):
        (abv,), (alog, dtb) = rows, consts
        g = _chunk_running_sum(-jnp.exp(alog) * _softplus(abv + dtb))
        beta = _sigmoid(abv)
        shape = (abv.shape[0], SLOT)
        g_slots = [jnp.broadcast_to(g[:, h:h + 1], shape) for h in range(N_HEADS)]
        b_slots = [jnp.broadcast_to(beta[:, N_HEADS + h:N_HEADS + h + 1], shape) for h in range(N_HEADS)]
        return [jnp.concatenate(g_slots, axis=1), jnp.concatenate(b_slots, axis=1)], []

    width = N_HEADS * SLOT
    return _rowwise("gdn_gates_fwd", fn, [ab], [a_log, dt_bias], [(width, F32), (width, F32)])


def _gates_bwd(ab, a_log, dt_bias, dg, dbeta):
    def fn(rows, consts):
        (abv, dgv, dbv), (alog, dtb) = rows, consts
        lane = lax.broadcasted_iota(jnp.int32, abv.shape, 1)
        dg_tok = jnp.zeros_like(abv)
        db_tok = jnp.zeros_like(abv)
        for h in range(N_HEADS):
            dg_tok = dg_tok + jnp.where(lane == h, jnp.sum(dgv[:, h * SLOT:(h + 1) * SLOT], axis=1, keepdims=True), 0.0)
            db_tok = db_tok + jnp.where(lane == N_HEADS + h, jnp.sum(dbv[:, h * SLOT:(h + 1) * SLOT], axis=1, keepdims=True), 0.0)
        dg_tok = _chunk_running_sum(dg_tok, reverse=True)
        xa = abv + dtb
        g = -jnp.exp(alog) * _softplus(xa)
        da = dg_tok * (-jnp.exp(alog)) * _sigmoid(xa)
        beta = _sigmoid(abv)
        dab = jnp.where(lane < N_HEADS, da, db_tok * beta * (1.0 - beta))
        dab = jnp.where(lane < 2 * N_HEADS, dab, 0.0)
        d_alog = jnp.sum(jnp.where(lane < N_HEADS, dg_tok * g, 0.0), axis=0, keepdims=True)
        d_dtb = jnp.sum(jnp.where(lane < N_HEADS, da, 0.0), axis=0, keepdims=True)
        return [dab], [d_alog, d_dtb]

    return _rowwise("gdn_gates_bwd", fn, [ab, dg, dbeta], [a_log, dt_bias], [(SLOT, F32)], sums=[SLOT, SLOT])


ROPE_HALF = MLA_ROPE // 2


def _rope_tables(positions):
    freqs = ROPE_THETA ** (-jnp.arange(ROPE_HALF, dtype=F32) / ROPE_HALF)
    ang = positions.astype(F32)[:, None] * freqs
    cos, sin = jnp.cos(ang), jnp.sin(ang)
    t = positions.shape[0]
    ones, zeros = jnp.ones((t, MLA_NOPE), F32), jnp.zeros((t, MLA_NOPE), F32)
    tail = jnp.zeros((t, SLOT - MLA_NOPE - MLA_ROPE), F32)
    half0 = jnp.zeros((t, ROPE_HALF), F32)
    same = jnp.concatenate([ones, cos, cos, tail], axis=1)
    from_low = jnp.concatenate([zeros, half0, sin, tail], axis=1)
    from_high = jnp.concatenate([zeros, -sin, half0, tail], axis=1)
    return same, from_low, from_high


def _rope(x, tabs):
    same, from_low, from_high = tabs
    width = x.shape[1]
    return x * same + pltpu.roll(x, ROPE_HALF, 1) * from_low + pltpu.roll(x, width - ROPE_HALF, 1) * from_high


def _rope_transposed(dy, tabs):
    same, from_low, from_high = tabs
    width = dy.shape[1]
    return dy * same + pltpu.roll(dy * from_low, width - ROPE_HALF, 1) + pltpu.roll(dy * from_high, ROPE_HALF, 1)


def _tile_slots(tab):
    return jnp.concatenate([tab] * N_HEADS, axis=1)


A_WIDTH = MLA_Q_RANK + MLA_KV_RANK + 2 * SLOT
A_KPE = MLA_Q_RANK + MLA_KV_RANK
A_AB = A_KPE + SLOT
WIDE = N_HEADS * SLOT


def _mla_front_fwd(proj_a, tabs, g_q, g_kv, w_uq, w_kv):
    def fn(rows, consts):
        pa, *tb = rows
        gq, gkv, wuq, wkv = consts
        cqn = _rms(pa[:, :MLA_Q_RANK], gq, MLA_Q_RANK).astype(BF16)
        ckvn = _rms(pa[:, MLA_Q_RANK:A_KPE], gkv, MLA_KV_RANK).astype(BF16)
        kv = _nt(ckvn, wkv)
        q = _rope(_nt(cqn, wuq), [_tile_slots(x) for x in tb])
        k = kv[:, :WIDE] + _tile_slots(_rope(pa[:, A_KPE:A_AB], tb))
        return [cqn, ckvn, q, k, kv[:, WIDE:]], []

    return _rowwise("mla_front_fwd", fn, [proj_a, *tabs], [g_q, g_kv, w_uq, w_kv],
                    [(MLA_Q_RANK, BF16), (MLA_KV_RANK, BF16)] + [(WIDE, BF16)] * 3)


def _mla_front_bwd(proj_a, tabs, g_q, g_kv, w_uq, w_kv, dq, dk, dv, dab):
    def fn(rows, consts):
        pa, t0, t1, t2, dqv, dkv, dvv, da = rows
        gq, gkv, wuq, wkv = consts
        tb = (t0, t1, t2)
        dq_p = _rope_transposed(dqv, [_tile_slots(x) for x in tb]).astype(BF16)
        dkv_p = jnp.concatenate([dkv, dvv], axis=1).astype(BF16)
        dkpe = dkv[:, :SLOT]
        for h in range(1, N_HEADS):
            dkpe = dkpe + dkv[:, h * SLOT:(h + 1) * SLOT]
        _, pull_q = jax.vjp(lambda x, g: _rms(x, g, MLA_Q_RANK), pa[:, :MLA_Q_RANK], gq)
        _, pull_kv = jax.vjp(lambda x, g: _rms(x, g, MLA_KV_RANK), pa[:, MLA_Q_RANK:A_KPE], gkv)
        dcq, dgq = pull_q(_nn(dq_p, wuq))
        dckv, dgkv = pull_kv(_nn(dkv_p, wkv))
        return [jnp.concatenate([dcq, dckv, _rope_transposed(dkpe, tb), da], axis=1), dq_p, dkv_p], [dgq, dgkv]

    return _rowwise("mla_front_bwd", fn, [proj_a, *tabs, dq, dk, dv, dab], [g_q, g_kv, w_uq, w_kv],
                    [(A_WIDTH, BF16), (WIDE, BF16), (2 * WIDE, BF16)], sums=[MLA_Q_RANK, MLA_KV_RANK])


def _slot_sum(x):
    parts = [jnp.broadcast_to(jnp.sum(x[:, h * SLOT:(h + 1) * SLOT], axis=1, keepdims=True), (x.shape[0], SLOT))
             for h in range(N_HEADS)]
    return jnp.concatenate(parts, axis=1)


def _mix_join(o_mla, o_gdn, gate, g_mla, g_gdn):
    mla = _rms(o_mla, g_mla, N_HEADS * MLA_V)
    gdn = o_gdn * lax.rsqrt(_slot_sum(o_gdn * o_gdn) * (1.0 / GDN_D) + EPS) * g_gdn * _silu(gate)
    return mla, gdn


MIX_TM = 256


def _mix_fwd(o_mla, o_gdn, gate, x, g_mla, g_gdn, w_out, g_post):
    dm = x.shape[1]

    def fn(rows, consts):
        om, og, gt, xv = rows
        gm, gg, wo, gp = consts
        cat = jnp.concatenate(_mix_join(om, og, gt, gm, gg), axis=1).astype(BF16)
        mixed = _nn(cat, wo)
        return [cat, mixed, xv + _rms(mixed, gp, dm)], []

    return _rowwise("mix_fwd", fn, [o_mla, o_gdn, gate, x], [g_mla, g_gdn, w_out, g_post],
                    [(2 * WIDE, BF16), (dm, F32), (dm, F32)], tm=MIX_TM)


def _mix_bwd(o_mla, o_gdn, gate, mixed, dy, g_mla, g_gdn, w_out, g_post):
    dm = mixed.shape[1]

    def fn(rows, consts):
        om, og, gt, mx, dyv = rows
        gm, gg, wo, gp = consts
        _, pull_post = jax.vjp(lambda hv, gv: _rms(hv, gv, dm), mx, gp)
        dmixed, dgp = pull_post(dyv)
        dmixed = dmixed.astype(BF16)
        dc = _nt(dmixed, wo)
        _, pull = jax.vjp(lambda x, g: _rms(x, g, N_HEADS * MLA_V), om, gm)
        dom, dgm = pull(dc[:, :WIDE])
        dn_out = dc[:, WIDE:]
        r = lax.rsqrt(_slot_sum(og * og) * (1.0 / GDN_D) + EPS)
        sig = _sigmoid(gt)
        normed = og * r
        dn = dn_out * gg * (gt * sig)
        dog = r * dn - normed * (r * r) * _slot_sum(dn * og) * (1.0 / GDN_D)
        dgt = dn_out * normed * gg * (sig * (1.0 + gt * (1.0 - sig)))
        dgg = jnp.sum(dn_out * normed * (gt * sig), axis=0, keepdims=True)
        return [dmixed, dom, _slot_sum(dom * om), dog, dgt], [dgp, dgm, dgg]

    return _rowwise("mix_bwd", fn, [o_mla, o_gdn, gate, mixed, dy], [g_mla, g_gdn, w_out, g_post],
                    [(dm, BF16), (WIDE, F32), (WIDE, F32), (WIDE, F32), (WIDE, BF16)], sums=[dm, WIDE, WIDE], tm=MIX_TM)


def _proj_fwd(x, g, weights):
    dm = x.shape[1]

    def fn(rows, consts):
        hn = _rms(rows[0], consts[0], dm).astype(BF16)
        return [hn] + [_nt(hn, wv) for wv in consts[1:]], []

    return _rowwise("proj_fwd", fn, [x], [g, *weights], [(dm, BF16)] + [(wv.shape[0], F32) for wv in weights], tm=MIX_TM)


def _proj_bwd(x, g, weights, cots, dy, h, g_post):
    dm = x.shape[1]
    n = len(weights)

    def fn(rows, consts):
        xv, dyv, hv, *parts = rows
        dn = _nn(parts[0], consts[2])
        for p, wv in zip(parts[1:], consts[3:]):
            dn = dn + _nn(p, wv)
        _, pull = jax.vjp(lambda a, gv: _rms(a, gv, dm), xv, consts[0])
        dx, dg = pull(dn)
        dx = dyv + dx
        _, pull = jax.vjp(lambda a: 0.5 * _rms(a, consts[1], dm), hv)
        return [dx, pull(dx)[0]], [dg]

    assert len(cots) == n
    return _rowwise("proj_bwd", fn, [x, dy, h, *cots], [g, g_post, *weights], [(dm, F32), (dm, BF16)], sums=[dm], tm=MIX_TM)


def _loss_fwd(y, target):
    dm = y.shape[1]

    def fn(rows, consts):
        err = rows[0] - rows[1]
        sq = err * err
        lanes = sq[:, :SLOT]
        for j in range(1, dm // SLOT):
            lanes = lanes + sq[:, j * SLOT:(j + 1) * SLOT]
        return [err * (1.0 / dm)], [jnp.sum(lanes, axis=0, keepdims=True) * (0.5 / dm)]

    return _rowwise("loss", fn, [y, target], [], [(dm, F32)], sums=[SLOT])


W_IN_CUTS = (0, 256, 384, 416, 1952, 1960, 1968, 2480)


def _heads_out(w, per_head, axis=-1):
    axis = axis % w.ndim
    shape = w.shape
    n = shape[axis] // per_head
    w = w.reshape(shape[:axis] + (n, per_head) + shape[axis + 1:])
    pad = [(0, 0)] * w.ndim
    pad[axis + 1] = (0, SLOT - per_head)
    return jnp.pad(w, pad).reshape(shape[:axis] + (n * SLOT,) + shape[axis + 1:])


def _heads_in(w, per_head, axis=-1):
    axis = axis % w.ndim
    shape = w.shape
    n = shape[axis] // SLOT
    w = w.reshape(shape[:axis] + (n, SLOT) + shape[axis + 1:])
    w = lax.slice_in_dim(w, 0, per_head, axis=axis + 1)
    return w.reshape(shape[:axis] + (n * per_head,) + shape[axis + 1:])


def _pad_lanes(v, lo, width=SLOT):
    return jnp.pad(v, [(0, 0)] * (v.ndim - 1) + [(lo, width - lo - v.shape[-1])])


def _pad_rows(v, lo, rows=SLOT):
    return jnp.pad(v, [(lo, rows - lo - v.shape[0])] + [(0, 0)] * (v.ndim - 1))


def _layout_weights(w):
    c = W_IN_CUTS
    w_in = w["w_in_t"]
    p = {}
    p["w_a"] = jnp.concatenate([w_in[c[0]:c[2]], _pad_rows(w_in[c[2]:c[3]], MLA_NOPE), _pad_rows(w_in[c[4]:c[6]], 0)], axis=0)
    p["w_qkv"] = _heads_out(w_in[c[3]:c[4]], GDN_D, axis=0)
    p["w_gate"] = _heads_out(w_in[c[6]:c[7]], GDN_D, axis=0)
    p["w_uq"] = _heads_out(w["uq_t"], MLA_NOPE + MLA_ROPE, axis=0)
    ukv = w["ukv_t"].reshape(N_HEADS, MLA_NOPE + MLA_V, MLA_KV_RANK)
    p["w_kv"] = jnp.concatenate([_heads_out(ukv[:, :MLA_NOPE].reshape(-1, MLA_KV_RANK), MLA_NOPE, axis=0),
                                 _heads_out(ukv[:, MLA_NOPE:].reshape(-1, MLA_KV_RANK), MLA_V, axis=0)], axis=0)
    p["conv"] = _heads_out(w["gdn_conv_w"], GDN_D)
    p["g_mla_out"] = _heads_out(w["mla_out_g"], MLA_V)
    p["g_gdn"] = jnp.tile(_pad_lanes(w["gdn_norm_g"], 0), (1, N_HEADS))
    p["a_log"] = _pad_lanes(w["gdn_a_log"], 0)
    p["dt_bias"] = _pad_lanes(w["gdn_dt_bias"], 0)
    return p


def _unlayout_grads(d):
    c = W_IN_CUTS
    g = {}
    da = d["w_a"]
    kpe0 = A_KPE + MLA_NOPE
    g["w_in_t"] = jnp.concatenate([da[:A_KPE], da[kpe0:kpe0 + MLA_ROPE], _heads_in(d["w_qkv"], GDN_D, axis=0),
                                   da[A_AB:A_AB + 2 * N_HEADS], _heads_in(d["w_gate"], GDN_D, axis=0)], axis=0)
    assert g["w_in_t"].shape[0] == c[-1]
    g["uq_t"] = _heads_in(d["w_uq"], MLA_NOPE + MLA_ROPE, axis=0)
    dk = _heads_in(d["w_kv"][:WIDE], MLA_NOPE, axis=0).reshape(N_HEADS, MLA_NOPE, MLA_KV_RANK)
    dv = _heads_in(d["w_kv"][WIDE:], MLA_V, axis=0).reshape(N_HEADS, MLA_V, MLA_KV_RANK)
    g["ukv_t"] = jnp.concatenate([dk, dv], axis=1).reshape(-1, MLA_KV_RANK)
    g["w_out"] = _heads_in(d["w_out"], GDN_D, axis=0)
    g["gdn_conv_w"] = _heads_in(d["conv"], GDN_D)
    g["mla_out_g"] = _heads_in(d["g_mla_out"], MLA_V)
    g["gdn_norm_g"] = jnp.sum(d["g_gdn"].reshape(N_HEADS, SLOT), axis=0, keepdims=True)[:, :GDN_D]
    g["gdn_a_log"] = d["a_log"][:, :N_HEADS]
    g["gdn_dt_bias"] = d["dt_bias"][:, :N_HEADS]
    return g


def _weight_grad(name, cots, acts, out_dtype=F32, tm=1024, tn=1024, tk=2048, after=None):
    return _matmul(name, cots, acts, "tn", out_dtype=out_dtype, tm=tm, tn=tn, tk=tk, after=after)


def _by_device(a):
    return a.astype(BF16).reshape((N_DEV, a.shape[0] // N_DEV) + a.shape[1:])


def _rows_of(blocks):
    return blocks.reshape((-1,) + blocks.shape[2:])


def _local_step(x, positions, target, w, mid, late):
    tabs = _rope_tables(positions)

    (h1, x1, hg1, hu1, a1), gathered = _ffn_fwd("ffn1_fwd", x, w["ffn1_pre_g"], w["ffn1"], 0, w["ffn1_post_g"], carry=mid,
                                                keep_act=True)
    w = dict(w, w_in_t=_rows_of(gathered[0]), uq_t=_rows_of(gathered[1]), ukv_t=_rows_of(gathered[2]),
             gdn_conv_w=gathered[3].transpose(1, 0, 2).reshape(CONV_SHAPE))
    p = _layout_weights(w)
    in_weights = [p["w_a"], p["w_qkv"], p["w_gate"]]
    hn, proj_a, proj_qkv, proj_gate = _proj_fwd(x1, w["mix_pre_g"], in_weights)
    cqn, ckvn, q, k, v = _mla_front_fwd(proj_a, tabs, w["mla_q_norm_g"], w["mla_kv_norm_g"], p["w_uq"], p["w_kv"])
    o_mla, lse = _attn_fwd(q, k, v)
    ab = (proj_a, SLOT, A_AB // SLOT)
    qkv_n = _gdn_conv_fwd(proj_qkv, p["conv"])
    gb, bb = _gates_fwd(ab, p["a_log"], p["dt_bias"])
    (o_gdn, keep), (ffn2, w_out) = _gdn_fwd(qkv_n, gb, bb, carry=late)
    p["w_out"] = _heads_out(_rows_of(w_out), GDN_D, axis=0)
    cat, mixed, x2 = _mix_fwd(o_mla, o_gdn, proj_gate, x1, p["g_mla_out"], p["g_gdn"], p["w_out"], w["mix_post_g"])
    (h2, y, hg2, hu2), _ = _ffn_fwd("ffn2_fwd", x2, w["ffn2_pre_g"], ffn2, 0, w["ffn2_post_g"])
    dy, loss_lanes = _loss_fwd(y, target)

    g = {}
    (dx2, xn2, dh2, a2, dhg2, dhu2, g["ffn2_pre_g"], g["ffn2_post_g"]), _ = _ffn_bwd(
        "ffn2_bwd", x2, h2, hg2, hu2, dy, w["ffn2_pre_g"], ffn2, 0, w["ffn2_post_g"])
    ffn2_grads = _Scatter([_by_device(_weight_grad("ffn2_dw_gate", dhg2, xn2, BF16, tm=1408)),
                           _by_device(_weight_grad("ffn2_dw_up", dhu2, xn2, BF16, tm=1408)),
                           _by_device(_weight_grad("ffn2_dw_down", a2, dh2, BF16, tm=1408))])
    d = {}
    dmixed, do_mla, delta, do_gdn, dgate, g["mix_post_g"], d["g_mla_out"], d["g_gdn"] = _mix_bwd(
        o_mla, o_gdn, proj_gate, mixed, dx2, p["g_mla_out"], p["g_gdn"], p["w_out"], w["mix_post_g"])
    d["w_out"] = _weight_grad("mix_out_dw", cat, dmixed, BF16)
    dq, dk, dv = _attn_bwd(q, k, v, do_mla, lse, delta)
    (dqkv_n, dgb, dbb), landed_ffn2 = _gdn_bwd(qkv_n, gb, bb, keep, do_gdn, carry=ffn2_grads)
    dab, d["a_log"], d["dt_bias"] = _gates_bwd(ab, p["a_log"], p["dt_bias"], dgb, dbb)
    dproj_qkv, d["conv"] = _gdn_conv_bwd(proj_qkv, p["conv"], dqkv_n)
    dproj_a, dq_p, dkv_p, g["mla_q_norm_g"], g["mla_kv_norm_g"] = _mla_front_bwd(
        proj_a, tabs, w["mla_q_norm_g"], w["mla_kv_norm_g"], p["w_uq"], p["w_kv"], dq, dk, dv, dab)
    d["w_uq"] = _weight_grad("mla_q_dw", dq_p, cqn, BF16)
    d["w_kv"] = _weight_grad("mla_kv_dw", dkv_p, ckvn, BF16)
    d["w_a"] = _weight_grad("proj_a_dw", dproj_a, hn, BF16, tm=640)
    d["w_qkv"] = _weight_grad("proj_qkv_dw", dproj_qkv, hn, BF16)
    d["w_gate"] = _weight_grad("proj_gate_dw", dgate, hn, BF16)
    dx1, dh1, g["mix_pre_g"] = _proj_bwd(x1, w["mix_pre_g"], in_weights, [dproj_a, dproj_qkv, dgate], dx2,
                                        h1, w["ffn1_post_g"])
    g.update(_unlayout_grads(d))
    begun = {}
    blocks = _by_device(_weight_grad("ffn1_w_down_grad", a1, dh1, BF16, tm=1408))
    begun["ffn1_w_down"], token = _scatter_begin("scatter_ffn1_w_down_begin", blocks)
    others = list(OTHER.values())
    (dx, xn1, _, _, dhg1, dhu1, g["ffn1_pre_g"], g["ffn1_post_g"]), landed_others = _ffn_bwd(
        "ffn1_bwd", x, h1, hg1, hu1, dx1, w["ffn1_pre_g"], w["ffn1"], 0, w["ffn1_post_g"],
        carry=_Scatter([_by_device(g.pop(t)) for t in others]), after=token)
    landed = dict(zip(list(FFN_NAMES[3:]) + list(OTHER), list(landed_ffn2) + list(landed_others)))
    packed = _pack_small(g, g["gdn_conv_w"].reshape(-1), REDUCE_ROWS)
    packed = packed.at[REDUCE_ROWS - 1, ROW - 1].set(jnp.sum(loss_lanes))
    begun["small"], token = _scatter_begin("reduce_small_begin", jnp.broadcast_to(packed, (N_DEV,) + packed.shape))
    for name, cots, acts in (("ffn1_w_gate", dhg1, xn1), ("ffn1_w_up", dhu1, xn1)):
        blocks = _by_device(_weight_grad(name + "_grad", cots, acts, BF16, tm=1408, after=token))
        begun[name], token = _scatter_begin("scatter_" + name + "_begin", blocks)
    return dx, g, landed, begun, token


MESH_AXES = ("x", "y", "c")
N_LINKS = N_DEV - 1


def _place():
    return tuple(lax.axis_index(a) for a in MESH_AXES)


def _block_of(dev):
    x, y, c = dev
    return 4 * x + 2 * y + c


def _remote_copy(src, dst, sems, k, to):
    send_sems, recv_sems = sems
    return pltpu.make_async_remote_copy(src_ref=src, dst_ref=dst, send_sem=send_sems.at[k], recv_sem=recv_sems.at[k],
                                        device_id=to, device_id_type=pl.DeviceIdType.MESH)


class _Exchange:
    def __init__(self, arrays):
        self.arrays = list(arrays)
        self.n = len(self.arrays)
        self.specs = [pl.BlockSpec(memory_space=pl.ANY)] * self.n
        self.scratch = [pltpu.SemaphoreType.DMA((self.n * N_LINKS,)), pltpu.SemaphoreType.DMA((self.n * N_LINKS,)),
                        pltpu.SemaphoreType.DMA((self.n,))]

    def split(self, refs):
        n = self.n
        return refs[:n], refs[n:2 * n], (refs[2 * n], refs[2 * n + 1]), refs[2 * n + 2]


class _Gather(_Exchange):
    def out_shape(self):
        return [jax.ShapeDtypeStruct((N_DEV,) + a.shape, a.dtype) for a in self.arrays]

    def _plan(self, ins, outs, sems, local_sems):
        x, y, c = _place()
        me, sibling = (x, y, c), (x, y, 1 - c)
        chips = [(1 - x, y), (x, 1 - y), (1 - x, 1 - y)]

        def copy(a, k, block, to, mine=False):
            src = ins[a] if mine else outs[a].at[_block_of(block)]
            return _remote_copy(src, outs[a].at[_block_of(block)], sems, a * N_LINKS + k, to)

        local = [pltpu.make_async_copy(ins[a], outs[a].at[_block_of(me)], local_sems.at[a]) for a in range(self.n)]
        first = []
        for a in range(self.n):
            first.append(copy(a, 0, me, sibling, mine=True))
            first += [copy(a, 1 + j, me, (*chip, c), mine=True) for j, chip in enumerate(chips)]
        return me, sibling, chips, c, copy, local, first

    def start(self, ins, outs, sems, local_sems):
        *_, local, first = self._plan(ins, outs, sems, local_sems)
        for cp in local + first:
            cp.start()

    def finish(self, ins, outs, sems, local_sems):
        me, sibling, chips, c, copy, local, first = self._plan(ins, outs, sems, local_sems)
        passed = []
        for j, chip in enumerate(chips):
            for a in range(self.n):
                copy(a, 1 + j, (*chip, c), me).wait_recv()
                passed.append(copy(a, 4 + j, (*chip, c), sibling))
                passed[-1].start()
        for a in range(self.n):
            copy(a, 0, sibling, me).wait_recv()
            for j, chip in enumerate(chips):
                copy(a, 4 + j, (*chip, 1 - c), me).wait_recv()
        for cp in first + passed:
            cp.wait_send()
        for cp in local:
            cp.wait()


class _Scatter(_Exchange):
    def out_shape(self):
        return [jax.ShapeDtypeStruct(a.shape, a.dtype) for a in self.arrays]

    def _plan(self, ins, outs, sems, local_sems):
        x, y, c = _place()
        me = _block_of((x, y, c))

        def peer(r):
            return (1 - x if r & 4 else x, 1 - y if r & 2 else y, 1 - c if r & 1 else c)

        local = [pltpu.make_async_copy(ins[a].at[me], outs[a].at[me], local_sems.at[a]) for a in range(self.n)]
        sends = [_remote_copy(ins[a].at[_block_of(peer(r))], outs[a].at[me], sems, a * N_LINKS + r - 1, peer(r))
                 for a in range(self.n) for r in range(1, N_DEV)]
        arrivals = [_remote_copy(ins[a].at[me], outs[a].at[_block_of(peer(r))], sems, a * N_LINKS + r - 1, peer(r))
                    for a in range(self.n) for r in range(1, N_DEV)]
        return local, sends, arrivals

    def start(self, ins, outs, sems, local_sems):
        local, sends, _ = self._plan(ins, outs, sems, local_sems)
        for cp in local + sends:
            cp.start()

    def finish(self, ins, outs, sems, local_sems):
        local, sends, arrivals = self._plan(ins, outs, sems, local_sems)
        for cp in arrivals:
            cp.wait_recv()
        for cp in sends:
            cp.wait_send()
        for cp in local:
            cp.wait()


def _exchange(name, plan):
    def body(*refs):
        parts = plan.split(refs)
        plan.start(*parts)
        plan.finish(*parts)

    return pl.pallas_call(
        body, name=name,
        in_specs=plan.specs,
        out_specs=plan.specs,
        out_shape=plan.out_shape(),
        scratch_shapes=plan.scratch,
    )(*plan.arrays)


def _call_carrying(body, plan, operands, *, name, grid, in_specs, out_specs, out_shape, scratch_shapes, compiler_params):
    if plan is None:
        outs = pl.pallas_call(body, name=name, grid=grid, in_specs=in_specs, out_specs=out_specs, out_shape=out_shape,
                              scratch_shapes=scratch_shapes, compiler_params=compiler_params)(*operands)
        return outs, []
    n_i, n_o, n_s, k = len(in_specs), len(out_specs), len(scratch_shapes), plan.n

    def whole(*refs):
        cut = [n_i, n_i + k, n_i + k + n_o, n_i + 2 * k + n_o, n_i + 2 * k + n_o + n_s]
        own_in, ex_in, own_out, ex_out, own_scr, ex_scr = (refs[a:b] for a, b in zip([0] + cut, cut + [len(refs)]))
        parts = plan.split(ex_in + ex_out + ex_scr)
        first = last = True
        for axis, size in enumerate(grid):
            first = first & (pl.program_id(axis) == 0)
            last = last & (pl.program_id(axis) == size - 1)

        @pl.when(first)
        def _():
            plan.start(*parts)

        body(*own_in, *own_out, *own_scr)

        @pl.when(last)
        def _():
            plan.finish(*parts)

    outs = pl.pallas_call(
        whole, name=name, grid=grid,
        in_specs=list(in_specs) + plan.specs, out_specs=list(out_specs) + plan.specs,
        out_shape=list(out_shape) + plan.out_shape(), scratch_shapes=list(scratch_shapes) + plan.scratch,
        compiler_params=compiler_params,
    )(*operands, *plan.arrays)
    return outs[:n_o], outs[n_o:]


def _row_tile(rows, target=256):
    best = rows
    for cand in range(16, min(rows, target) + 1, 16):
        if rows % cand == 0:
            best = cand
    return best


def _sum_blocks(name, blocks, after=None):
    rows, width = blocks.shape[-2:]
    tm = _row_tile(rows)

    def body(x_ref, *rest):
        acc = x_ref[0].astype(F32)
        for d in range(1, N_DEV):
            acc = acc + x_ref[d].astype(F32)
        rest[-1][...] = acc

    ordered = [] if after is None else [after]
    return pl.pallas_call(
        body, name=name,
        grid=(rows // tm,),
        in_specs=[pl.BlockSpec((N_DEV, tm, width), lambda i: (0, i, 0))] + [pl.BlockSpec(memory_space=pl.ANY)] * len(ordered),
        out_specs=pl.BlockSpec((tm, width), lambda i: (i, 0)),
        out_shape=jax.ShapeDtypeStruct((rows, width), F32),
        compiler_params=pltpu.CompilerParams(dimension_semantics=("parallel",)),
    )(blocks, *ordered)


def _split_plan(src_ref, land_ref, sems):
    x, y, c = _place()
    me = _block_of((x, y, c))

    def peer(r):
        return (1 - x if r & 4 else x, 1 - y if r & 2 else y, 1 - c if r & 1 else c)

    sends = [_remote_copy(src_ref.at[_block_of(peer(r))], land_ref.at[me], sems, r - 1, peer(r)) for r in range(1, N_DEV)]
    arrivals = [_remote_copy(src_ref.at[me], land_ref.at[_block_of(peer(r))], sems, r - 1, peer(r)) for r in range(1, N_DEV)]
    return sends, arrivals


def _scatter_begin(name, blocks):
    def body(src_ref, land_ref, send_sems, recv_sems, src_thru, land_thru, token_ref):
        for cp in _split_plan(src_ref, land_ref, (send_sems, recv_sems))[0]:
            cp.start()
        token_ref[...] = jnp.zeros_like(token_ref)

    hbm, sem = pl.BlockSpec(memory_space=pltpu.HBM), pl.BlockSpec(memory_space=pltpu.SEMAPHORE)
    zone = pltpu.HBM(blocks.shape, blocks.dtype)
    *handles, token = pl.pallas_call(
        body, name=name,
        in_specs=(hbm, hbm),
        out_specs=(sem, sem, hbm, hbm, pl.BlockSpec(memory_space=pltpu.VMEM)),
        out_shape=(pltpu.SemaphoreType.DMA((N_LINKS,)), pltpu.SemaphoreType.DMA((N_LINKS,)), zone, zone,
                   jax.ShapeDtypeStruct((8, SLOT), F32)),
        input_output_aliases={0: 2, 1: 3},
        compiler_params=pltpu.CompilerParams(has_side_effects=pltpu.SideEffectType.DATAFLOW_SIDE_EFFECTING),
    )(pltpu.with_memory_space_constraint(blocks, pltpu.HBM),
      pltpu.with_memory_space_constraint(lax.empty(blocks.shape, blocks.dtype), pltpu.HBM))
    return handles, token


def _scatter_end(name, handles, after):
    send_sems, recv_sems, src, zone = handles

    def body(src_ref, land_ref, send_sems, recv_sems, after_ref, src_dead, got_ref):
        sends, arrivals = _split_plan(src_ref, land_ref, (send_sems, recv_sems))
        for cp in arrivals:
            cp.wait_recv()
        for cp in sends:
            cp.wait_send()

    hbm, sem = pl.BlockSpec(memory_space=pltpu.HBM), pl.BlockSpec(memory_space=pltpu.SEMAPHORE)
    sent, landed = pl.pallas_call(
        body, name=name,
        in_specs=(hbm, hbm, sem, sem, pl.BlockSpec(memory_space=pl.ANY)),
        out_specs=(hbm, hbm),
        out_shape=(pltpu.HBM(src.shape, src.dtype), pltpu.HBM(zone.shape, zone.dtype)),
        input_output_aliases={0: 0, 1: 1},
        compiler_params=pltpu.CompilerParams(has_side_effects=pltpu.SideEffectType.DATAFLOW_SIDE_EFFECTING),
    )(src, zone, send_sems, recv_sems, after)
    me = _block_of(_place())
    return lax.dynamic_update_slice_in_dim(landed, lax.dynamic_slice_in_dim(sent, me, 1, axis=0), me, axis=0)


def _adamw_values(wv, gv, mv, vv):
    m2 = ADAM_B1 * mv + (1.0 - ADAM_B1) * gv
    v2 = ADAM_B2 * vv + (1.0 - ADAM_B2) * jnp.square(gv)
    m_hat = m2 / (1.0 - ADAM_B1 ** ADAM_STEP)
    v_hat = v2 / (1.0 - ADAM_B2 ** ADAM_STEP)
    return [-ADAM_LR * (m_hat / (jnp.sqrt(v_hat) + ADAM_EPS) + ADAM_WD * wv), m2, v2]


def _adamw(name, w, g, m, v):
    def fn(rows, consts):
        return _adamw_values(*rows), []

    return _rowwise(name, fn, [w, g, m, v], [], [(w.shape[1], F32)] * 3, tm=_row_tile(w.shape[0]))


def _sum_adamw(name, blocks, w, m, v, after=None):
    rows, width = w.shape
    tm = _row_tile(rows)

    def body(x_ref, w_ref, m_ref, v_ref, *rest):
        acc = x_ref[0].astype(F32)
        for d in range(1, N_DEV):
            acc = acc + x_ref[d].astype(F32)
        rest[-4][...] = acc
        for ref, val in zip(rest[-3:], _adamw_values(w_ref[...], acc, m_ref[...], v_ref[...])):
            ref[...] = val

    ordered = [] if after is None else [after]
    tile = pl.BlockSpec((tm, width), lambda i: (i, 0))
    return pl.pallas_call(
        body, name=name,
        grid=(rows // tm,),
        in_specs=[pl.BlockSpec((N_DEV, tm, width), lambda i: (0, i, 0))] + [tile] * 3 + [pl.BlockSpec(memory_space=pl.ANY)] * len(ordered),
        out_specs=[tile] * 4,
        out_shape=[jax.ShapeDtypeStruct((rows, width), F32)] * 4,
        compiler_params=pltpu.CompilerParams(dimension_semantics=("parallel",)),
    )(blocks, w, m, v, *ordered)


ROW = 1024
FFN_NAMES = ("ffn1_w_gate", "ffn1_w_up", "ffn1_w_down", "ffn2_w_gate", "ffn2_w_up", "ffn2_w_down")
OTHER = {"w_in": "w_in_t", "mla_w_uq": "uq_t", "mla_w_ukv": "ukv_t", "w_out": "w_out"}
BY_COLUMNS = ("ffn1_w_gate", "ffn1_w_up", "ffn2_w_gate", "ffn2_w_up", "w_in", "mla_w_uq", "mla_w_ukv")
SMALL = {
    "ffn1_pre_g": (1024, 1024), "ffn1_post_g": (1024, 1024), "mix_pre_g": (1024, 1024), "mla_q_norm_g": (256, 256),
    "mla_kv_norm_g": (128, 128), "mla_out_g": (512, 512), "gdn_a_log": (8, 128), "gdn_dt_bias": (8, 128),
    "gdn_norm_g": (64, 128), "mix_post_g": (1024, 1024), "ffn2_pre_g": (1024, 1024), "ffn2_post_g": (1024, 1024),
}
CONV_SHAPE = (GDN_CONV, 3 * N_HEADS * GDN_D)
CONV_SHARD = (GDN_CONV, CONV_SHAPE[1] // N_DEV)
CONV_LANES = CONV_SHAPE[0] * CONV_SHAPE[1]
SMALL_ROWS = 8
REDUCE_ROWS = 16


def _pack_small(vecs, conv, rows):
    parts = [_pad_lanes(vecs[n].reshape(1, -1), 0, r) for n, (_, r) in SMALL.items()]
    parts.append(conv.reshape(1, -1))
    flat = jnp.concatenate(parts, axis=1)
    return _pad_lanes(flat, 0, rows * ROW).reshape(rows, ROW)


def _unpack_small(buf):
    flat = buf.reshape(1, -1)
    out, at = {}, 0
    for n, (w, r) in SMALL.items():
        out[n] = flat[:, at:at + w]
        at += r
    return out, flat[0, at:]


def kernel(x, positions, ffn1_pre_g, ffn1_w_gate, ffn1_w_up, ffn1_w_down, ffn1_post_g, mix_pre_g, w_in, mla_q_norm_g, mla_w_uq, mla_kv_norm_g, mla_w_ukv, mla_out_g, gdn_conv_w, gdn_a_log, gdn_dt_bias, gdn_norm_g, w_out, mix_post_g, ffn2_pre_g, ffn2_w_gate, ffn2_w_up, ffn2_w_down, ffn2_post_g, loss_target, m_ffn1_pre_g, m_ffn1_w_gate, m_ffn1_w_up, m_ffn1_w_down, m_ffn1_post_g, m_mix_pre_g, m_w_in, m_mla_q_norm_g, m_mla_w_uq, m_mla_kv_norm_g, m_mla_w_ukv, m_mla_out_g, m_gdn_conv_w, m_gdn_a_log, m_gdn_dt_bias, m_gdn_norm_g, m_w_out, m_mix_post_g, m_ffn2_pre_g, m_ffn2_w_gate, m_ffn2_w_up, m_ffn2_w_down, m_ffn2_post_g, v_ffn1_pre_g, v_ffn1_w_gate, v_ffn1_w_up, v_ffn1_w_down, v_ffn1_post_g, v_mix_pre_g, v_w_in, v_mla_q_norm_g, v_mla_w_uq, v_mla_kv_norm_g, v_mla_w_ukv, v_mla_out_g, v_gdn_conv_w, v_gdn_a_log, v_gdn_dt_bias, v_gdn_norm_g, v_w_out, v_mix_post_g, v_ffn2_pre_g, v_ffn2_w_gate, v_ffn2_w_up, v_ffn2_w_down, v_ffn2_post_g):
    given = dict(locals())
    order = ["ffn1_pre_g", "ffn1_w_gate", "ffn1_w_up", "ffn1_w_down", "ffn1_post_g", "mix_pre_g", "w_in", "mla_q_norm_g",
             "mla_w_uq", "mla_kv_norm_g", "mla_w_ukv", "mla_out_g", "gdn_conv_w", "gdn_a_log", "gdn_dt_bias", "gdn_norm_g",
             "w_out", "mix_post_g", "ffn2_pre_g", "ffn2_w_gate", "ffn2_w_up", "ffn2_w_down", "ffn2_post_g"]
    assert sorted(order) == sorted(list(FFN_NAMES) + list(OTHER) + list(SMALL) + ["gdn_conv_w"])

    def drop_depth(a):
        return a[0] if a.ndim == 3 else a

    wts = {n: drop_depth(given[n]) for n in order}
    mom = {n: drop_depth(given["m_" + n]) for n in order}
    var = {n: drop_depth(given["v_" + n]) for n in order}
    me = _block_of(_place())

    def wire(n):
        return (wts[n].T if n in BY_COLUMNS else wts[n]).astype(BF16)

    (ffn1,) = _exchange("gather_first", _Gather([jnp.stack([wire(n) for n in FFN_NAMES[:3]])]))
    mid = _Gather([wire(n) for n in ("w_in", "mla_w_uq", "mla_w_ukv")] + [wts["gdn_conv_w"]])
    late = _Gather([jnp.stack([wire(n) for n in FFN_NAMES[3:]]), wire("w_out")])
    full = {n: wts[n] for n in SMALL}
    full["ffn1"] = ffn1

    dx, grads, landed, begun, token = _local_step(x[0], positions[0], loss_target[0], full, mid, late)

    grad, outs = {}, {"delta": {}, "new_m": {}, "new_v": {}}

    def finish(n, blocks, after=None):
        flip = n in BY_COLUMNS and wts[n].shape[1] % SLOT != 0
        turn = (lambda a: a.T) if flip else (lambda a: a)
        if n in BY_COLUMNS and not flip:
            grad[n] = _sum_blocks("sum_" + n, blocks, after=after).T
            new = _adamw("adamw_" + n, wts[n], grad[n], mom[n], var[n])
        else:
            total, *new = _sum_adamw("update_" + n, blocks, turn(wts[n]), turn(mom[n]), turn(var[n]), after=after)
            grad[n] = turn(total)
        outs["delta"][n], outs["new_m"][n], outs["new_v"][n] = (turn(a) for a in new)
        return new[2]

    for n, blocks in landed.items():
        token = finish(n, blocks, after=token)
    small_handles = begun.pop("small")
    for n, handles in begun.items():
        token = finish(n, _scatter_end("scatter_" + n + "_end", handles, after=token))

    small_sum = _sum_blocks("sum_small", _scatter_end("reduce_small_end", small_handles, after=token))
    loss = small_sum[REDUCE_ROWS - 1, ROW - 1]
    small_grad, conv_grad_full = _unpack_small(small_sum)
    grad.update(small_grad)
    grad["gdn_conv_w"] = lax.dynamic_slice(conv_grad_full[:CONV_LANES].reshape(CONV_SHAPE), (0, me * CONV_SHARD[1]), CONV_SHARD)
    outs["grad"] = grad
    small = [_pack_small(s, s["gdn_conv_w"].reshape(-1), SMALL_ROWS) for s in (wts, grad, mom, var)]
    for kind, s in zip(("delta", "new_m", "new_v"), _adamw("adamw_small", *small)):
        vecs, conv = _unpack_small(s)
        outs[kind].update(vecs)
        outs[kind]["gdn_conv_w"] = conv[:CONV_SHARD[0] * CONV_SHARD[1]].reshape(CONV_SHARD)
    result = [loss, dx[None]]
    for kind in ("grad", "delta", "new_m", "new_v"):
        result += [outs[kind][n].reshape(given[n].shape) for n in order]
    return tuple(result)
```

```python
import jax
import jax.numpy as jnp
from jax import lax
from jax.experimental import pallas as pl
from jax.experimental.pallas import tpu as pltpu

F32 = jnp.float32
BF16 = jnp.bfloat16
HI = lax.Precision.HIGH

N_DEV = 8
N_HEADS = 8
SLOT = 128
MLA_Q_RANK = 256
MLA_KV_RANK = 128
MLA_NOPE = 64
MLA_ROPE = 32
MLA_V = 64
GDN_D = 64
GDN_CONV = 4
GDN_CHUNK = 64
ROPE_THETA = 10000.0
EPS = 1e-6
ADAM_LR, ADAM_B1, ADAM_B2, ADAM_EPS, ADAM_WD, ADAM_STEP = 0.001, 0.9, 0.999, 1e-08, 0.01, 10


def _dot(a, b, ca, cb, precision=None):
    lead = a.ndim - 2
    batch = tuple(range(lead))
    return lax.dot_general(a, b, (((lead + ca,), (lead + cb,)), (batch, batch)), precision=precision,
                           preferred_element_type=F32)


def _nn(a, b, precision=None):
    return _dot(a, b, 1, 0, precision)


def _nt(a, b, precision=None):
    return _dot(a, b, 1, 1, precision)


def _tn(a, b, precision=None):
    return _dot(a, b, 0, 0, precision)


def _sigmoid(x):
    return 1.0 / (1.0 + jnp.exp(-x))


def _silu(x):
    return x * _sigmoid(x)


def _rms(x, g, n):
    ms = jnp.sum(x * x, axis=-1, keepdims=True) * (1.0 / n)
    return x * lax.rsqrt(ms + EPS) * g


def _chunk_masks():
    c = GDN_CHUNK
    i = lax.broadcasted_iota(jnp.int32, (c, c), 0)
    j = lax.broadcasted_iota(jnp.int32, (c, c), 1)
    lower = i >= j
    strict = i > j
    eye = (i == j).astype(F32)
    blocks = []
    b = 1
    while b < c:
        same = (i // (2 * b)) == (j // (2 * b))
        blocks.append(same & ((i % (2 * b)) >= b) & ((j % (2 * b)) < b))
        b *= 2
    return lower, strict, eye, blocks


def _unit_lower_inverse(low, eye, blocks):
    t = eye - jnp.where(blocks[0], low, 0.0)
    for m in blocks[1:]:
        lo = jnp.where(m, low, 0.0)
        t = t - _nn(t, _nn(lo, t, HI), HI)
    return t


@jax.custom_vjp
def _known_inverse(low, tinv):
    return tinv


def _known_inverse_fwd(low, tinv):
    return tinv, tinv


def _known_inverse_bwd(tinv, dt):
    return -_tn(tinv, _nt(dt, tinv, HI), HI), jnp.zeros_like(tinv)


_known_inverse.defvjp(_known_inverse_fwd, _known_inverse_bwd)

_PRODUCTS = {"nn": _nn, "nt": _nt, "tn": _tn}


@jax.custom_vjp
def _known_nn(a, b, c):
    return c


@jax.custom_vjp
def _known_nt(a, b, c):
    return c


@jax.custom_vjp
def _known_tn(a, b, c):
    return c


def _known_fwd(a, b, c):
    return c, (a, b, c)


_known_nn.defvjp(_known_fwd, lambda r, dc: (_nt(dc, r[1], HI), _tn(r[0], dc, HI), jnp.zeros_like(r[2])))
_known_nt.defvjp(_known_fwd, lambda r, dc: (_nn(dc, r[1], HI), _tn(dc, r[0], HI), jnp.zeros_like(r[2])))
_known_tn.defvjp(_known_fwd, lambda r, dc: (_nt(r[1], dc, HI), _nn(r[0], dc, HI), jnp.zeros_like(r[2])))
_KNOWN = {"nn": _known_nn, "nt": _known_nt, "tn": _known_tn}
GDN_PRODUCTS = 8
GDN_KEPT = 2 + GDN_PRODUCTS


def _gdn_chunk(q, k, v, gc, bb, s, masks, known=None):
    lower, strict, eye, blocks = masks
    made = []

    def product(kind, a, b):
        c = _PRODUCTS[kind](a, b, HI) if known is None else _KNOWN[kind](a, b, known[1 + len(made)])
        made.append(c)
        return c

    qs = q * (GDN_D ** -0.5)
    gct = jnp.swapaxes(gc, -1, -2)
    decay = jnp.exp(jnp.where(lower, gc - gct, -1e30))
    kb = k * bb
    low = jnp.where(strict, product("nt", kb, k) * decay, 0.0)
    tinv = _unit_lower_inverse(low, eye, blocks) if known is None else _known_inverse(low, known[0])
    eg = jnp.exp(gc)
    w = product("nn", tinv, kb * eg)
    u = product("nn", tinv, v * bb)
    attn = product("nt", qs, k) * decay
    last = lax.broadcasted_iota(jnp.int32, gc.shape[-2:], 0) == GDN_CHUNK - 1
    g_end = jnp.sum(jnp.where(last, gc, 0.0), axis=-2, keepdims=True)
    k_dec = k * jnp.exp(g_end - gc)
    v_new = u - product("nn", w, s)
    o = product("nn", qs * eg, s) + product("nn", attn, v_new)
    s_new = s * jnp.exp(g_end) + product("tn", k_dec, v_new)
    assert len(made) == GDN_PRODUCTS
    return o, s_new, [tinv] + made


GDN_GROUP = 8
GDN_GROUPS = N_HEADS // GDN_GROUP


def _group_heads(ref):
    return jnp.stack([ref[:, pl.ds(j * SLOT, GDN_D)] for j in range(GDN_GROUP)])


def _ungroup_heads(ref, val):
    pad = jnp.zeros((GDN_CHUNK, SLOT - GDN_D), F32)
    for j in range(GDN_GROUP):
        ref[:, pl.ds(j * SLOT, GDN_D)] = val[j]
        ref[:, pl.ds(j * SLOT + GDN_D, SLOT - GDN_D)] = pad


def _gdn_fwd(qkv, gb, bb, carry=None):
    t = qkv.shape[0]
    n_chunks = t // GDN_CHUNK
    d = GDN_D

    def body(q_ref, k_ref, v_ref, g_ref, b_ref, o_ref, keep_ref, s_ref):
        @pl.when(pl.program_id(1) == 0)
        def _():
            s_ref[...] = jnp.zeros_like(s_ref)

        s = s_ref[...]
        keep_ref[:, 0, 0] = s
        o, s_new, made = _gdn_chunk(*[_group_heads(r) for r in (q_ref, k_ref, v_ref, g_ref, b_ref)], s, _chunk_masks())
        for i, val in enumerate(made):
            keep_ref[:, 0, 1 + i] = val
        s_ref[...] = s_new
        _ungroup_heads(o_ref, o)

    def spec(kind=0):
        return pl.BlockSpec((GDN_CHUNK, GDN_GROUP * SLOT), lambda h, n: (n, kind * GDN_GROUPS + h))

    return _call_carrying(
        body, carry, (qkv, qkv, qkv, gb, bb), name="gdn_fwd",
        grid=(GDN_GROUPS, n_chunks),
        in_specs=[spec(0), spec(1), spec(2), spec(), spec()],
        out_specs=[spec(), pl.BlockSpec((GDN_GROUP, 1, GDN_KEPT, d, d), lambda h, n: (h, n, 0, 0, 0))],
        out_shape=[jax.ShapeDtypeStruct((t, N_HEADS * SLOT), F32), jax.ShapeDtypeStruct((N_HEADS, n_chunks, GDN_KEPT, d, d), F32)],
        scratch_shapes=[pltpu.VMEM((GDN_GROUP, d, d), F32)],
        compiler_params=pltpu.CompilerParams(dimension_semantics=("arbitrary", "arbitrary")),
    )


def _gdn_bwd(qkv, gb, bb, keep, do, carry=None):
    t = qkv.shape[0]
    n_chunks = t // GDN_CHUNK
    d = GDN_D

    def body(q_ref, k_ref, v_ref, g_ref, b_ref, keep_ref, do_ref, dqkv_ref, dg_ref, db_ref, ds_ref):
        @pl.when(pl.program_id(1) == 0)
        def _():
            ds_ref[...] = jnp.zeros_like(ds_ref)

        masks = _chunk_masks()
        known = [keep_ref[:, 0, 1 + i] for i in range(GDN_KEPT - 1)]
        _, pull = jax.vjp(lambda *a: _gdn_chunk(*a, masks, known)[:2],
                          *[_group_heads(r) for r in (q_ref, k_ref, v_ref, g_ref, b_ref)], keep_ref[:, 0, 0])
        dq, dk, dv, dg, db, ds = pull((_group_heads(do_ref), ds_ref[...]))
        ds_ref[...] = ds
        for i, val in enumerate((dq, dk, dv)):
            _ungroup_heads(dqkv_ref.at[i], val)
        _ungroup_heads(dg_ref, dg)
        _ungroup_heads(db_ref, db)

    def spec(kind=0):
        return pl.BlockSpec((GDN_CHUNK, GDN_GROUP * SLOT), lambda h, n: (n_chunks - 1 - n, kind * GDN_GROUPS + h))

    return _call_carrying(
        body, carry, (qkv, qkv, qkv, gb, bb, keep, do), name="gdn_bwd",
        grid=(GDN_GROUPS, n_chunks),
        in_specs=[spec(0), spec(1), spec(2), spec(), spec(),
                  pl.BlockSpec((GDN_GROUP, 1, GDN_KEPT, d, d), lambda h, n: (h, n_chunks - 1 - n, 0, 0, 0)), spec()],
        out_specs=[pl.BlockSpec((3, GDN_CHUNK, GDN_GROUP * SLOT), lambda h, n: (0, n_chunks - 1 - n, h)), spec(), spec()],
        out_shape=[jax.ShapeDtypeStruct((3, t, N_HEADS * SLOT), F32)] + [jax.ShapeDtypeStruct((t, N_HEADS * SLOT), F32)] * 2,
        scratch_shapes=[pltpu.VMEM((GDN_GROUP, d, d), F32)],
        compiler_params=pltpu.CompilerParams(dimension_semantics=("arbitrary", "arbitrary")),
    )


def _rowwise(name, fn, rows, consts, outs, sums=(), tm=512):
    rows = [x if isinstance(x, tuple) else (x, x.shape[1], 0) for x in rows]
    t = rows[0][0].shape[0]
    tm = min(tm, t)
    steps = t // tm
    n_r, n_c, n_o, n_s = len(rows), len(consts), len(outs), len(sums)

    def window(width, block):
        return pl.BlockSpec((tm, width), lambda i: (i, block))

    def body(*refs):
        r, c = refs[:n_r], refs[n_r:n_r + n_c]
        o, s = refs[n_r + n_c:n_r + n_c + n_o], refs[n_r + n_c + n_o:]
        vals, tot = fn([x[...] for x in r], [x[...] for x in c])
        for ref, val in zip(o, vals):
            ref[...] = val.astype(ref.dtype)
        if n_s:
            @pl.when(pl.program_id(0) == 0)
            def _():
                for ref in s:
                    ref[...] = jnp.zeros_like(ref)

            for ref, val in zip(s, tot):
                ref[...] += val

    return pl.pallas_call(
        body, name=name,
        grid=(steps,),
        in_specs=[window(w, b) for _, w, b in rows] + [pl.BlockSpec(x.shape, lambda i: (0, 0)) for x in consts],
        out_specs=[pl.BlockSpec((tm, w), lambda i: (i, 0)) for w, _ in outs]
        + [pl.BlockSpec((1, w), lambda i: (0, 0)) for w in sums],
        out_shape=[jax.ShapeDtypeStruct((t, w), dt) for w, dt in outs]
        + [jax.ShapeDtypeStruct((1, w), F32) for w in sums],
        compiler_params=pltpu.CompilerParams(dimension_semantics=("arbitrary",)),
    )(*[x for x, _, _ in rows], *consts)


def _tile(dim, target):
    if dim <= target:
        return dim
    best = None
    for cand in range(128, target + 1, 128):
        if dim % cand == 0:
            best = cand
    assert best is not None, (dim, target)
    return best


def _matmul(name, a, b, mode, out_dtype=F32, tm=1024, tn=1024, tk=2048, after=None):
    if mode == "nn":
        (m, k), n = a.shape, b.shape[1]
    elif mode == "nt":
        (m, k), n = a.shape, b.shape[0]
    else:
        (k, m), n = a.shape, b.shape[1]
    tm, tn, tk = _tile(m, tm), _tile(n, tn), _tile(k, tk)
    k_steps = k // tk
    product = {"nn": _nn, "nt": _nt, "tn": _tn}[mode]

    def body(a_ref, b_ref, *rest):
        o_ref, acc_ref = rest[-2:]
        part = product(a_ref[...].astype(BF16), b_ref[...].astype(BF16))
        if k_steps == 1:
            o_ref[...] = part.astype(o_ref.dtype)
        else:
            kk = pl.program_id(2)

            @pl.when(kk == 0)
            def _():
                acc_ref[...] = part

            @pl.when(kk > 0)
            def _():
                acc_ref[...] += part

            @pl.when(kk == k_steps - 1)
            def _():
                o_ref[...] = acc_ref[...].astype(o_ref.dtype)

    a_spec = pl.BlockSpec((tk, tm), lambda i, j, kk: (kk, i)) if mode == "tn" else pl.BlockSpec((tm, tk), lambda i, j, kk: (i, kk))
    b_spec = pl.BlockSpec((tn, tk), lambda i, j, kk: (j, kk)) if mode == "nt" else pl.BlockSpec((tk, tn), lambda i, j, kk: (kk, j))
    ordered = [] if after is None else [after]
    return pl.pallas_call(
        body, name=name,
        grid=(m // tm, n // tn, k_steps),
        in_specs=[a_spec, b_spec] + [pl.BlockSpec(memory_space=pl.ANY)] * len(ordered),
        out_specs=pl.BlockSpec((tm, tn), lambda i, j, kk: (i, j)),
        out_shape=jax.ShapeDtypeStruct((m, n), out_dtype),
        scratch_shapes=[pltpu.VMEM((tm, tn) if k_steps > 1 else (8, 128), F32)],
        compiler_params=pltpu.CompilerParams(dimension_semantics=("parallel", "parallel", "arbitrary")),
    )(a, b, *ordered)


FFN_TM = 512
FFN_BWD_TM = 256
FFN_BLOCKS = 4
FFN_GATE, FFN_UP, FFN_DOWN = 0, 1, 2


def _ffn_weight_specs(ffn_w, first):
    _, _, rows, dm = ffn_w.shape

    def spec(k):
        return pl.BlockSpec((FFN_BLOCKS, None, rows, dm), lambda i, j: (j, first + k, 0, 0))

    return [spec(FFN_GATE), spec(FFN_UP), spec(FFN_DOWN)], FFN_BLOCKS * rows


def _ffn_fwd(name, x, g_pre, ffn_w, first, g_post, carry=None, keep_act=False):
    t, dm = x.shape
    tm = min(FFN_TM, t)
    w_specs, tf = _ffn_weight_specs(ffn_w, first)
    f_steps = N_DEV // FFN_BLOCKS

    def body(x_ref, gpre_ref, wg_ref, wu_ref, wd_ref, gpost_ref, h_ref, y_ref, hg_ref, hu_ref, *rest):
        *act_ref, xn_ref, acc_ref = rest
        j = pl.program_id(1)

        @pl.when(j == 0)
        def _():
            xn_ref[...] = _rms(x_ref[...], gpre_ref[...], dm).astype(BF16)
            acc_ref[...] = jnp.zeros_like(acc_ref)

        xn = xn_ref[...]
        wg, wu, wd = (r[...].reshape(tf, dm) for r in (wg_ref, wu_ref, wd_ref))
        hg, hu = _nt(xn, wg), _nt(xn, wu)
        hg_ref[...] = hg.astype(BF16)
        hu_ref[...] = hu.astype(BF16)
        a = (_silu(hg) * hu).astype(BF16)
        for ref in act_ref:
            ref[...] = a
        acc_ref[...] += _nn(a, wd)

        @pl.when(j == f_steps - 1)
        def _():
            h = acc_ref[...]
            h_ref[...] = h
            y_ref[...] = x_ref[...] + 0.5 * _rms(h, gpost_ref[...], dm)

    row = pl.BlockSpec((tm, dm), lambda i, j: (i, 0))
    vec = pl.BlockSpec((1, dm), lambda i, j: (0, 0))
    wide = pl.BlockSpec((tm, tf), lambda i, j: (i, j))
    n_wide = 3 if keep_act else 2
    return _call_carrying(
        body, carry, (x, g_pre, ffn_w, ffn_w, ffn_w, g_post), name=name,
        grid=(t // tm, f_steps),
        in_specs=[row, vec, *w_specs, vec],
        out_specs=[row, row] + [wide] * n_wide,
        out_shape=[jax.ShapeDtypeStruct((t, dm), F32)] * 2 + [jax.ShapeDtypeStruct((t, f_steps * tf), BF16)] * n_wide,
        scratch_shapes=[pltpu.VMEM((tm, dm), BF16), pltpu.VMEM((tm, dm), F32)],
        compiler_params=pltpu.CompilerParams(dimension_semantics=("arbitrary", "arbitrary")),
    )


def _ffn_bwd(name, x, h, hg, hu, dy, g_pre, ffn_w, first, g_post, carry=None, after=None, keep_act=True):
    t, dm = x.shape
    tm = min(FFN_BWD_TM, t)
    w_specs, tf = _ffn_weight_specs(ffn_w, first)
    f_steps = N_DEV // FFN_BLOCKS
    f = f_steps * tf

    def post(hv, g):
        return 0.5 * _rms(hv, g, dm)

    def pre(xv, g):
        return _rms(xv, g, dm)

    def body(x_ref, h_ref, dy_ref, hg_ref, hu_ref, gpre_ref, wg_ref, wu_ref, wd_ref, gpost_ref,
             dx_ref, xn_ref, dh_ref, *rest):
        *act_ref, dhg_ref, dhu_ref, dgpre_ref, dgpost_ref, acc_ref = rest
        i, j = pl.program_id(0), pl.program_id(1)

        @pl.when((i == 0) & (j == 0))
        def _():
            dgpre_ref[...] = jnp.zeros_like(dgpre_ref)
            dgpost_ref[...] = jnp.zeros_like(dgpost_ref)

        @pl.when(j == 0)
        def _():
            xn_ref[...] = pre(x_ref[...], gpre_ref[...]).astype(BF16)
            _, pull = jax.vjp(post, h_ref[...], gpost_ref[...])
            dh, dg = pull(dy_ref[...])
            dh_ref[...] = dh.astype(BF16)
            dgpost_ref[...] += dg
            acc_ref[...] = jnp.zeros_like(acc_ref)

        wg, wu, wd = (r[...].reshape(tf, dm) for r in (wg_ref, wu_ref, wd_ref))
        hg, hu = hg_ref[...].astype(F32), hu_ref[...].astype(F32)
        da = _nt(dh_ref[...], wd)
        sig = _sigmoid(hg)
        act = hg * sig
        dhu = (da * act).astype(BF16)
        dhg = (da * hu * (sig * (1.0 + hg * (1.0 - sig)))).astype(BF16)
        for ref in act_ref:
            ref[...] = (act * hu).astype(BF16)
        dhg_ref[...] = dhg
        dhu_ref[...] = dhu
        acc_ref[...] += _nn(dhg, wg) + _nn(dhu, wu)

        @pl.when(j == f_steps - 1)
        def _():
            _, pull = jax.vjp(pre, x_ref[...], gpre_ref[...])
            dx, dg = pull(acc_ref[...])
            dx_ref[...] = dy_ref[...] + dx
            dgpre_ref[...] += dg

    row = pl.BlockSpec((tm, dm), lambda i, j: (i, 0))
    vec = pl.BlockSpec((1, dm), lambda i, j: (0, 0))
    wide = pl.BlockSpec((tm, tf), lambda i, j: (i, j))
    ordered = [] if after is None else [after]
    n_wide = 3 if keep_act else 2

    def after_it(*refs):
        body(*refs[:10], *refs[10 + len(ordered):])

    return _call_carrying(
        after_it, carry, (x, h, dy, hg, hu, g_pre, ffn_w, ffn_w, ffn_w, g_post, *ordered), name=name,
        grid=(t // tm, f_steps),
        in_specs=[row, row, row, wide, wide, vec, *w_specs, vec] + [pl.BlockSpec(memory_space=pl.ANY)] * len(ordered),
        out_specs=[row, row, row] + [wide] * n_wide + [vec, vec],
        out_shape=[jax.ShapeDtypeStruct((t, dm), F32), jax.ShapeDtypeStruct((t, dm), BF16), jax.ShapeDtypeStruct((t, dm), BF16)]
        + [jax.ShapeDtypeStruct((t, f), BF16)] * n_wide + [jax.ShapeDtypeStruct((1, dm), F32)] * 2,
        scratch_shapes=[pltpu.VMEM((tm, dm), F32)],
        compiler_params=pltpu.CompilerParams(dimension_semantics=("arbitrary", "arbitrary")),
    )


ATT_T = 512
ATT_GROUP = 4
ATT_GROUP_FWD = 8
ATT_SCALE = (MLA_NOPE + MLA_ROPE) ** -0.5


def _stack_slots(ref, group):
    return jnp.stack([ref[:, pl.ds(j * SLOT, SLOT)] for j in range(group)])


def _unstack_slots(ref, val):
    for j in range(val.shape[0]):
        ref[:, pl.ds(j * SLOT, SLOT)] = val[j].astype(ref.dtype)


def _scores(q, k, diagonal):
    s = _nt(q, k) * ATT_SCALE
    if diagonal:
        row = lax.broadcasted_iota(jnp.int32, s.shape[1:], 0)
        col = lax.broadcasted_iota(jnp.int32, s.shape[1:], 1)
        s = jnp.where(col <= row, s, -1e30)
    return s


def _attn_pairs(steps, q_major):
    pairs = ([(qi, ki) for qi in range(steps) for ki in range(qi + 1)] if q_major
             else [(qi, ki) for ki in range(steps) for qi in range(ki, steps)])
    return jnp.array([p[0] for p in pairs], jnp.int32), jnp.array([p[1] for p in pairs], jnp.int32)


def _attn_specs(tile, group):
    width = group * SLOT
    return (pl.BlockSpec((tile, width), lambda h, p, qt, kt: (qt[p], h)),
            pl.BlockSpec((tile, width), lambda h, p, qt, kt: (kt[p], h)))


def _attn_fwd(q, k, v):
    t = q.shape[0]
    tile = min(ATT_T, t)
    steps = t // tile
    g = ATT_GROUP_FWD

    strip = min(SLOT, tile)

    def body(qt_ref, kt_ref, q_ref, k_ref, v_ref, o_ref, lse_ref, m_ref, l_ref, alpha_ref, acc_ref, s_ref, p_ref):
        qi, ki = qt_ref[pl.program_id(1)], kt_ref[pl.program_id(1)]

        @pl.when(ki == 0)
        def _():
            m_ref[...] = jnp.full_like(m_ref, -1e30)
            l_ref[...] = jnp.zeros_like(l_ref)
            acc_ref[...] = jnp.zeros_like(acc_ref)

        def step(diagonal):
            s_ref[...] = _nt(_stack_slots(k_ref, g), _stack_slots(q_ref, g))
            for j in range(tile // strip):
                c = pl.ds(j * strip, strip)
                s = s_ref[:, :, c] * ATT_SCALE
                if diagonal:
                    key = lax.broadcasted_iota(jnp.int32, s.shape[1:], 0)
                    query = lax.broadcasted_iota(jnp.int32, s.shape[1:], 1) + j * strip
                    s = jnp.where(key <= query, s, -1e30)
                m_old = m_ref[:, :, c]
                m_new = jnp.maximum(m_old, jnp.max(s, axis=1, keepdims=True))
                p = jnp.exp(s - m_new)
                alpha = jnp.exp(m_old - m_new)
                l_ref[:, :, c] = alpha * l_ref[:, :, c] + jnp.sum(p, axis=1, keepdims=True)
                alpha_ref[:, :, c] = alpha
                m_ref[:, :, c] = m_new
                p_ref[:, :, c] = p.astype(BF16)
            acc_ref[...] = acc_ref[...] * alpha_ref[...] + _tn(_stack_slots(v_ref, g), p_ref[...])

        @pl.when(ki < qi)
        def _():
            step(False)

        @pl.when(ki == qi)
        def _():
            step(True)
            out = acc_ref[...] / l_ref[...]
            lse = jnp.broadcast_to(m_ref[...] + jnp.log(l_ref[...]), out.shape)
            for j in range(g):
                o_ref[:, pl.ds(j * SLOT, SLOT)] = out[j].T
                lse_ref[:, pl.ds(j * SLOT, SLOT)] = lse[j].T

    q_spec, k_spec = _attn_specs(tile, g)
    tables = _attn_pairs(steps, True)
    return pl.pallas_call(
        body, name="attn_fwd",
        grid_spec=pltpu.PrefetchScalarGridSpec(
            num_scalar_prefetch=2, grid=(N_HEADS // g, tables[0].shape[0]),
            in_specs=[q_spec, k_spec, k_spec], out_specs=[q_spec, q_spec],
            scratch_shapes=[pltpu.VMEM((g, 1, tile), F32), pltpu.VMEM((g, 1, tile), F32), pltpu.VMEM((g, 1, tile), F32),
                            pltpu.VMEM((g, SLOT, tile), F32), pltpu.VMEM((g, tile, tile), F32), pltpu.VMEM((g, tile, tile), BF16)]),
        out_shape=[jax.ShapeDtypeStruct((t, N_HEADS * SLOT), F32)] * 2,
        compiler_params=pltpu.CompilerParams(dimension_semantics=("parallel", "arbitrary")),
    )(*tables, q, k, v)


def _attn_grad_scores(q, k, v, do, lse_ref, delta_ref, diagonal):
    g = ATT_GROUP
    p = jnp.exp(_scores(q, k, diagonal) - _stack_slots(lse_ref, g)[:, :, 0:1])
    dp = _nt(do, v)
    return p, p * (dp - _stack_slots(delta_ref, g)[:, :, 0:1]) * ATT_SCALE


def _attn_bwd(q, k, v, do, lse, delta):
    t = q.shape[0]
    tile = min(ATT_T, t)
    steps = t // tile
    g = ATT_GROUP

    def body(qt_ref, kt_ref, q_ref, k_ref, v_ref, do_ref, lse_ref, delta_ref, dq_ref, dk_ref, dv_ref, dk_acc, dv_acc):
        qi, ki = qt_ref[pl.program_id(1)], kt_ref[pl.program_id(1)]

        @pl.when(pl.program_id(1) == 0)
        def _():
            dq_ref[...] = jnp.zeros_like(dq_ref)

        def step(diagonal):
            qq, kk = _stack_slots(q_ref, g), _stack_slots(k_ref, g)
            do_b = _stack_slots(do_ref, g).astype(BF16)
            p, ds = _attn_grad_scores(qq, kk, _stack_slots(v_ref, g), do_b, lse_ref, delta_ref, diagonal)
            ds = ds.astype(BF16)
            dv_acc[...] += _tn(p.astype(BF16), do_b)
            dk_acc[...] += _tn(ds, qq)
            dq = _nn(ds, kk)
            rows = pl.ds(pl.multiple_of(qi * tile, tile), tile)
            for j in range(g):
                dq_ref[rows, pl.ds(j * SLOT, SLOT)] += dq[j]

        @pl.when(qi == ki)
        def _():
            dk_acc[...] = jnp.zeros_like(dk_acc)
            dv_acc[...] = jnp.zeros_like(dv_acc)
            step(True)

        @pl.when(qi > ki)
        def _():
            step(False)

        @pl.when(qi == steps - 1)
        def _():
            _unstack_slots(dk_ref, dk_acc[...])
            _unstack_slots(dv_ref, dv_acc[...])

    q_spec, k_spec = _attn_specs(tile, g)
    tables = _attn_pairs(steps, False)
    return pl.pallas_call(
        body, name="attn_bwd",
        grid_spec=pltpu.PrefetchScalarGridSpec(
            num_scalar_prefetch=2, grid=(N_HEADS // g, tables[0].shape[0]),
            in_specs=[q_spec, k_spec, k_spec, q_spec, q_spec, q_spec],
            out_specs=[pl.BlockSpec((t, g * SLOT), lambda h, p, qt, kt: (0, h)), k_spec, k_spec],
            scratch_shapes=[pltpu.VMEM((g, tile, SLOT), F32), pltpu.VMEM((g, tile, SLOT), F32)]),
        out_shape=[jax.ShapeDtypeStruct((t, N_HEADS * SLOT), F32)] * 3,
        compiler_params=pltpu.CompilerParams(dimension_semantics=("parallel", "arbitrary")),
    )(*tables, q, k, v, do, lse, delta)


CONV_PAD = 8


def _fill_padded(ref, val):
    t = val.shape[0]
    zeros = jnp.zeros((CONV_PAD, val.shape[1]), val.dtype)
    ref[pl.ds(0, CONV_PAD)] = zeros
    ref[pl.ds(CONV_PAD + t, CONV_PAD)] = zeros
    ref[pl.ds(CONV_PAD, t)] = val


def _shifted(ref, s):
    return ref[pl.ds(CONV_PAD - s, ref.shape[0] - 2 * CONV_PAD)]


def _l2norm(x):
    return x * lax.rsqrt(jnp.sum(x * x, axis=-1, keepdims=True) + EPS)


def _conv_pre(x_pad, w):
    y = w[GDN_CONV - 1:GDN_CONV, :] * _shifted(x_pad, 0)
    for s in range(1, GDN_CONV):
        y = y + w[GDN_CONV - 1 - s:GDN_CONV - s, :] * _shifted(x_pad, s)
    return y


def _gdn_conv_fwd(x, w):
    t, width = x.shape

    def body(x_ref, w_ref, o_ref, x_pad):
        _fill_padded(x_pad, x_ref[...])
        act = _silu(_conv_pre(x_pad, w_ref[...]))
        normed = pl.program_id(0) < 2 * N_HEADS
        o_ref[...] = jnp.where(normed, _l2norm(act), act)

    return pl.pallas_call(
        body, name="gdn_conv_fwd",
        grid=(width // SLOT,),
        in_specs=[pl.BlockSpec((t, SLOT), lambda j: (0, j)), pl.BlockSpec((GDN_CONV, SLOT), lambda j: (0, j))],
        out_specs=pl.BlockSpec((t, SLOT), lambda j: (0, j)),
        out_shape=jax.ShapeDtypeStruct((t, width), F32),
        scratch_shapes=[pltpu.VMEM((t + 2 * CONV_PAD, SLOT), F32)],
        compiler_params=pltpu.CompilerParams(dimension_semantics=("parallel",)),
    )(x, w)


def _gdn_conv_bwd(x, w, dout):
    t, width = x.shape

    def body(x_ref, w_ref, do_ref, dx_ref, dw_ref, x_pad, dy_pad):
        wv = w_ref[...]
        _fill_padded(x_pad, x_ref[...])
        y = _conv_pre(x_pad, wv)
        sig = _sigmoid(y)
        act = y * sig
        _, pull = jax.vjp(_l2norm, act)
        normed = pl.program_id(0) < 2 * N_HEADS
        dact = jnp.where(normed, pull(do_ref[0])[0], do_ref[0])
        dy = dact * (sig * (1.0 + y * (1.0 - sig)))
        _fill_padded(dy_pad, dy)
        dx = wv[GDN_CONV - 1:GDN_CONV, :] * dy
        for s in range(1, GDN_CONV):
            dx = dx + wv[GDN_CONV - 1 - s:GDN_CONV - s, :] * _shifted(dy_pad, -s)
        dx_ref[...] = dx.astype(BF16)
        for s in range(GDN_CONV):
            dw_ref[GDN_CONV - 1 - s:GDN_CONV - s, :] = jnp.sum(dy * _shifted(x_pad, s), axis=0, keepdims=True)

    col = pl.BlockSpec((t, SLOT), lambda j: (0, j))
    tap = pl.BlockSpec((GDN_CONV, SLOT), lambda j: (0, j))
    return pl.pallas_call(
        body, name="gdn_conv_bwd",
        grid=(width // SLOT,),
        in_specs=[col, tap, pl.BlockSpec((1, t, SLOT), lambda j: (j // N_HEADS, 0, j % N_HEADS))],
        out_specs=[col, tap],
        out_shape=[jax.ShapeDtypeStruct((t, width), BF16), jax.ShapeDtypeStruct((GDN_CONV, width), F32)],
        scratch_shapes=[pltpu.VMEM((t + 2 * CONV_PAD, SLOT), F32)] * 2,
        compiler_params=pltpu.CompilerParams(dimension_semantics=("parallel",)),
    )(x, w, dout)


def _softplus(x):
    e = jnp.exp(-jnp.abs(x))
    u = 1.0 + e
    log1p = jnp.where(u == 1.0, e, jnp.log(u) * e / jnp.where(u == 1.0, 1.0, u - 1.0))
    return jnp.maximum(x, 0.0) + log1p


def _chunk_running_sum(x, reverse=False):
    tm = x.shape[0]
    at = lax.broadcasted_iota(jnp.int32, x.shape, 0) % GDN_CHUNK
    step = 1
    while step < GDN_CHUNK:
        if reverse:
            x = x + jnp.where(at < GDN_CHUNK - step, pltpu.roll(x, tm - step, 0), 0.0)
        else:
            x = x + jnp.where(at >= step, pltpu.roll(x, step, 0), 0.0)
        step *= 2
    return x


def _gates_fwd(ab, a_log, dt_bias):
    def fn(rows, consts):
        (abv,), (alog, dtb) = rows, consts
        g = _chunk_running_sum(-jnp.exp(alog) * _softplus(abv + dtb))
        beta = _sigmoid(abv)
        shape = (abv.shape[0], SLOT)
        g_slots = [jnp.broadcast_to(g[:, h:h + 1], shape) for h in range(N_HEADS)]
        b_slots = [jnp.broadcast_to(beta[:, N_HEADS + h:N_HEADS + h + 1], shape) for h in range(N_HEADS)]
        return [jnp.concatenate(g_slots, axis=1), jnp.concatenate(b_slots, axis=1)], []

    width = N_HEADS * SLOT
    return _rowwise("gdn_gates_fwd", fn, [ab], [a_log, dt_bias], [(width, F32), (width, F32)])


def _gates_bwd(ab, a_log, dt_bias, dg, dbeta):
    def fn(rows, consts):
        (abv, dgv, dbv), (alog, dtb) = rows, consts
        lane = lax.broadcasted_iota(jnp.int32, abv.shape, 1)
        dg_tok = jnp.zeros_like(abv)
        db_tok = jnp.zeros_like(abv)
        for h in range(N_HEADS):
            dg_tok = dg_tok + jnp.where(lane == h, jnp.sum(dgv[:, h * SLOT:(h + 1) * SLOT], axis=1, keepdims=True), 0.0)
            db_tok = db_tok + jnp.where(lane == N_HEADS + h, jnp.sum(dbv[:, h * SLOT:(h + 1) * SLOT], axis=1, keepdims=True), 0.0)
        dg_tok = _chunk_running_sum(dg_tok, reverse=True)
        xa = abv + dtb
        g = -jnp.exp(alog) * _softplus(xa)
        da = dg_tok * (-jnp.exp(alog)) * _sigmoid(xa)
        beta = _sigmoid(abv)
        dab = jnp.where(lane < N_HEADS, da, db_tok * beta * (1.0 - beta))
        dab = jnp.where(lane < 2 * N_HEADS, dab, 0.0)
        d_alog = jnp.sum(jnp.where(lane < N_HEADS, dg_tok * g, 0.0), axis=0, keepdims=True)
        d_dtb = jnp.sum(jnp.where(lane < N_HEADS, da, 0.0), axis=0, keepdims=True)
        return [dab], [d_alog, d_dtb]

    return _rowwise("gdn_gates_bwd", fn, [ab, dg, dbeta], [a_log, dt_bias], [(SLOT, F32)], sums=[SLOT, SLOT])


ROPE_HALF = MLA_ROPE // 2


def _rope_tables(positions):
    freqs = ROPE_THETA ** (-jnp.arange(ROPE_HALF, dtype=F32) / ROPE_HALF)
    ang = positions.astype(F32)[:, None] * freqs
    cos, sin = jnp.cos(ang), jnp.sin(ang)
    t = positions.shape[0]
    ones, zeros = jnp.ones((t, MLA_NOPE), F32), jnp.zeros((t, MLA_NOPE), F32)
    tail = jnp.zeros((t, SLOT - MLA_NOPE - MLA_ROPE), F32)
    half0 = jnp.zeros((t, ROPE_HALF), F32)
    same = jnp.concatenate([ones, cos, cos, tail], axis=1)
    from_low = jnp.concatenate([zeros, half0, sin, tail], axis=1)
    from_high = jnp.concatenate([zeros, -sin, half0, tail], axis=1)
    return same, from_low, from_high


def _rope(x, tabs):
    same, from_low, from_high = tabs
    width = x.shape[1]
    return x * same + pltpu.roll(x, ROPE_HALF, 1) * from_low + pltpu.roll(x, width - ROPE_HALF, 1) * from_high


def _rope_transposed(dy, tabs):
    same, from_low, from_high = tabs
    width = dy.shape[1]
    return dy * same + pltpu.roll(dy * from_low, width - ROPE_HALF, 1) + pltpu.roll(dy * from_high, ROPE_HALF, 1)


def _tile_slots(tab):
    return jnp.concatenate([tab] * N_HEADS, axis=1)


A_WIDTH = MLA_Q_RANK + MLA_KV_RANK + 2 * SLOT
A_KPE = MLA_Q_RANK + MLA_KV_RANK
A_AB = A_KPE + SLOT
WIDE = N_HEADS * SLOT


def _mla_front_fwd(proj_a, tabs, g_q, g_kv, w_uq, w_kv):
    def fn(rows, consts):
        pa, *tb = rows
        gq, gkv, wuq, wkv = consts
        cqn = _rms(pa[:, :MLA_Q_RANK], gq, MLA_Q_RANK).astype(BF16)
        ckvn = _rms(pa[:, MLA_Q_RANK:A_KPE], gkv, MLA_KV_RANK).astype(BF16)
        kv = _nt(ckvn, wkv)
        q = _rope(_nt(cqn, wuq), [_tile_slots(x) for x in tb])
        k = kv[:, :WIDE] + _tile_slots(_rope(pa[:, A_KPE:A_AB], tb))
        return [cqn, ckvn, q, k, kv[:, WIDE:]], []

    return _rowwise("mla_front_fwd", fn, [proj_a, *tabs], [g_q, g_kv, w_uq, w_kv],
                    [(MLA_Q_RANK, BF16), (MLA_KV_RANK, BF16)] + [(WIDE, BF16)] * 3)


def _mla_front_bwd(proj_a, tabs, g_q, g_kv, w_uq, w_kv, dq, dk, dv, dab):
    def fn(rows, consts):
        pa, t0, t1, t2, dqv, dkv, dvv, da = rows
        gq, gkv, wuq, wkv = consts
        tb = (t0, t1, t2)
        dq_p = _rope_transposed(dqv, [_tile_slots(x) for x in tb]).astype(BF16)
        dkv_p = jnp.concatenate([dkv, dvv], axis=1).astype(BF16)
        dkpe = dkv[:, :SLOT]
        for h in range(1, N_HEADS):
            dkpe = dkpe + dkv[:, h * SLOT:(h + 1) * SLOT]
        _, pull_q = jax.vjp(lambda x, g: _rms(x, g, MLA_Q_RANK), pa[:, :MLA_Q_RANK], gq)
        _, pull_kv = jax.vjp(lambda x, g: _rms(x, g, MLA_KV_RANK), pa[:, MLA_Q_RANK:A_KPE], gkv)
        dcq, dgq = pull_q(_nn(dq_p, wuq))
        dckv, dgkv = pull_kv(_nn(dkv_p, wkv))
        return [jnp.concatenate([dcq, dckv, _rope_transposed(dkpe, tb), da], axis=1), dq_p, dkv_p], [dgq, dgkv]

    return _rowwise("mla_front_bwd", fn, [proj_a, *tabs, dq, dk, dv, dab], [g_q, g_kv, w_uq, w_kv],
                    [(A_WIDTH, BF16), (WIDE, BF16), (2 * WIDE, BF16)], sums=[MLA_Q_RANK, MLA_KV_RANK])


def _slot_sum(x):
    parts = [jnp.broadcast_to(jnp.sum(x[:, h * SLOT:(h + 1) * SLOT], axis=1, keepdims=True), (x.shape[0], SLOT))
             for h in range(N_HEADS)]
    return jnp.concatenate(parts, axis=1)


def _mix_join(o_mla, o_gdn, gate, g_mla, g_gdn):
    mla = _rms(o_mla, g_mla, N_HEADS * MLA_V)
    gdn = o_gdn * lax.rsqrt(_slot_sum(o_gdn * o_gdn) * (1.0 / GDN_D) + EPS) * g_gdn * _silu(gate)
    return mla, gdn


MIX_TM = 256


def _mix_fwd(o_mla, o_gdn, gate, x, g_mla, g_gdn, w_out, g_post):
    dm = x.shape[1]

    def fn(rows, consts):
        om, og, gt, xv = rows
        gm, gg, wo, gp = consts
        cat = jnp.concatenate(_mix_join(om, og, gt, gm, gg), axis=1).astype(BF16)
        mixed = _nn(cat, wo)
        return [cat, mixed, xv + _rms(mixed, gp, dm)], []

    return _rowwise("mix_fwd", fn, [o_mla, o_gdn, gate, x], [g_mla, g_gdn, w_out, g_post],
                    [(2 * WIDE, BF16), (dm, F32), (dm, F32)], tm=MIX_TM)


def _mix_bwd(o_mla, o_gdn, gate, mixed, dy, g_mla, g_gdn, w_out, g_post):
    dm = mixed.shape[1]

    def fn(rows, consts):
        om, og, gt, mx, dyv = rows
        gm, gg, wo, gp = consts
        _, pull_post = jax.vjp(lambda hv, gv: _rms(hv, gv, dm), mx, gp)
        dmixed, dgp = pull_post(dyv)
        dmixed = dmixed.astype(BF16)
        dc = _nt(dmixed, wo)
        _, pull = jax.vjp(lambda x, g: _rms(x, g, N_HEADS * MLA_V), om, gm)
        dom, dgm = pull(dc[:, :WIDE])
        dn_out = dc[:, WIDE:]
        r = lax.rsqrt(_slot_sum(og * og) * (1.0 / GDN_D) + EPS)
        sig = _sigmoid(gt)
        normed = og * r
        dn = dn_out * gg * (gt * sig)
        dog = r * dn - normed * (r * r) * _slot_sum(dn * og) * (1.0 / GDN_D)
        dgt = dn_out * normed * gg * (sig * (1.0 + gt * (1.0 - sig)))
        dgg = jnp.sum(dn_out * normed * (gt * sig), axis=0, keepdims=True)
        return [dmixed, dom, _slot_sum(dom * om), dog, dgt], [dgp, dgm, dgg]

    return _rowwise("mix_bwd", fn, [o_mla, o_gdn, gate, mixed, dy], [g_mla, g_gdn, w_out, g_post],
                    [(dm, BF16), (WIDE, F32), (WIDE, F32), (WIDE, F32), (WIDE, BF16)], sums=[dm, WIDE, WIDE], tm=MIX_TM)


def _proj_fwd(x, g, weights):
    dm = x.shape[1]

    def fn(rows, consts):
        hn = _rms(rows[0], consts[0], dm).astype(BF16)
        return [hn] + [_nt(hn, wv) for wv in consts[1:]], []

    return _rowwise("proj_fwd", fn, [x], [g, *weights], [(dm, BF16)] + [(wv.shape[0], F32) for wv in weights], tm=MIX_TM)


def _proj_bwd(x, g, weights, cots, dy, h, g_post):
    dm = x.shape[1]
    n = len(weights)

    def fn(rows, consts):
        xv, dyv, hv, *parts = rows
        dn = _nn(parts[0], consts[2])
        for p, wv in zip(parts[1:], consts[3:]):
            dn = dn + _nn(p, wv)
        _, pull = jax.vjp(lambda a, gv: _rms(a, gv, dm), xv, consts[0])
        dx, dg = pull(dn)
        dx = dyv + dx
        _, pull = jax.vjp(lambda a: 0.5 * _rms(a, consts[1], dm), hv)
        return [dx, pull(dx)[0]], [dg]

    assert len(cots) == n
    return _rowwise("proj_bwd", fn, [x, dy, h, *cots], [g, g_post, *weights], [(dm, F32), (dm, BF16)], sums=[dm], tm=MIX_TM)


def _loss_fwd(y, target):
    dm = y.shape[1]

    def fn(rows, consts):
        err = rows[0] - rows[1]
        sq = err * err
        lanes = sq[:, :SLOT]
        for j in range(1, dm // SLOT):
            lanes = lanes + sq[:, j * SLOT:(j + 1) * SLOT]
        return [err * (1.0 / dm)], [jnp.sum(lanes, axis=0, keepdims=True) * (0.5 / dm)]

    return _rowwise("loss", fn, [y, target], [], [(dm, F32)], sums=[SLOT])


W_IN_CUTS = (0, 256, 384, 416, 1952, 1960, 1968, 2480)


def _heads_out(w, per_head, axis=-1):
    axis = axis % w.ndim
    shape = w.shape
    n = shape[axis] // per_head
    w = w.reshape(shape[:axis] + (n, per_head) + shape[axis + 1:])
    pad = [(0, 0)] * w.ndim
    pad[axis + 1] = (0, SLOT - per_head)
    return jnp.pad(w, pad).reshape(shape[:axis] + (n * SLOT,) + shape[axis + 1:])


def _heads_in(w, per_head, axis=-1):
    axis = axis % w.ndim
    shape = w.shape
    n = shape[axis] // SLOT
    w = w.reshape(shape[:axis] + (n, SLOT) + shape[axis + 1:])
    w = lax.slice_in_dim(w, 0, per_head, axis=axis + 1)
    return w.reshape(shape[:axis] + (n * per_head,) + shape[axis + 1:])


def _pad_lanes(v, lo, width=SLOT):
    return jnp.pad(v, [(0, 0)] * (v.ndim - 1) + [(lo, width - lo - v.shape[-1])])


def _pad_rows(v, lo, rows=SLOT):
    return jnp.pad(v, [(lo, rows - lo - v.shape[0])] + [(0, 0)] * (v.ndim - 1))


def _layout_weights(w):
    c = W_IN_CUTS
    w_in = w["w_in_t"]
    p = {}
    p["w_a"] = jnp.concatenate([w_in[c[0]:c[2]], _pad_rows(w_in[c[2]:c[3]], MLA_NOPE), _pad_rows(w_in[c[4]:c[6]], 0)], axis=0)
    p["w_qkv"] = _heads_out(w_in[c[3]:c[4]], GDN_D, axis=0)
    p["w_gate"] = _heads_out(w_in[c[6]:c[7]], GDN_D, axis=0)
    p["w_uq"] = _heads_out(w["uq_t"], MLA_NOPE + MLA_ROPE, axis=0)
    ukv = w["ukv_t"].reshape(N_HEADS, MLA_NOPE + MLA_V, MLA_KV_RANK)
    p["w_kv"] = jnp.concatenate([_heads_out(ukv[:, :MLA_NOPE].reshape(-1, MLA_KV_RANK), MLA_NOPE, axis=0),
                                 _heads_out(ukv[:, MLA_NOPE:].reshape(-1, MLA_KV_RANK), MLA_V, axis=0)], axis=0)
    p["conv"] = _heads_out(w["gdn_conv_w"], GDN_D)
    p["g_mla_out"] = _heads_out(w["mla_out_g"], MLA_V)
    p["g_gdn"] = jnp.tile(_pad_lanes(w["gdn_norm_g"], 0), (1, N_HEADS))
    p["a_log"] = _pad_lanes(w["gdn_a_log"], 0)
    p["dt_bias"] = _pad_lanes(w["gdn_dt_bias"], 0)
    return p


def _unlayout_grads(d):
    c = W_IN_CUTS
    g = {}
    da = d["w_a"]
    kpe0 = A_KPE + MLA_NOPE
    g["w_in_t"] = jnp.concatenate([da[:A_KPE], da[kpe0:kpe0 + MLA_ROPE], _heads_in(d["w_qkv"], GDN_D, axis=0),
                                   da[A_AB:A_AB + 2 * N_HEADS], _heads_in(d["w_gate"], GDN_D, axis=0)], axis=0)
    assert g["w_in_t"].shape[0] == c[-1]
    g["uq_t"] = _heads_in(d["w_uq"], MLA_NOPE + MLA_ROPE, axis=0)
    dk = _heads_in(d["w_kv"][:WIDE], MLA_NOPE, axis=0).reshape(N_HEADS, MLA_NOPE, MLA_KV_RANK)
    dv = _heads_in(d["w_kv"][WIDE:], MLA_V, axis=0).reshape(N_HEADS, MLA_V, MLA_KV_RANK)
    g["ukv_t"] = jnp.concatenate([dk, dv], axis=1).reshape(-1, MLA_KV_RANK)
    g["w_out"] = _heads_in(d["w_out"], GDN_D, axis=0)
    g["gdn_conv_w"] = _heads_in(d["conv"], GDN_D)
    g["mla_out_g"] = _heads_in(d["g_mla_out"], MLA_V)
    g["gdn_norm_g"] = jnp.sum(d["g_gdn"].reshape(N_HEADS, SLOT), axis=0, keepdims=True)[:, :GDN_D]
    g["gdn_a_log"] = d["a_log"][:, :N_HEADS]
    g["gdn_dt_bias"] = d["dt_bias"][:, :N_HEADS]
    return g


def _weight_grad(name, cots, acts, out_dtype=F32, tm=1024, tn=1024, tk=2048, after=None):
    return _matmul(name, cots, acts, "tn", out_dtype=out_dtype, tm=tm, tn=tn, tk=tk, after=after)


def _by_device(a):
    return a.astype(BF16).reshape((N_DEV, a.shape[0] // N_DEV) + a.shape[1:])


def _rows_of(blocks):
    return blocks.reshape((-1,) + blocks.shape[2:])


def _local_step(x, positions, target, w, mid, late):
    tabs = _rope_tables(positions)

    (h1, x1, hg1, hu1, a1), gathered = _ffn_fwd("ffn1_fwd", x, w["ffn1_pre_g"], w["ffn1"], 0, w["ffn1_post_g"], carry=mid,
                                                keep_act=True)
    w = dict(w, w_in_t=_rows_of(gathered[0]), uq_t=_rows_of(gathered[1]), ukv_t=_rows_of(gathered[2]),
             gdn_conv_w=gathered[3].transpose(1, 0, 2).reshape(CONV_SHAPE))
    p = _layout_weights(w)
    in_weights = [p["w_a"], p["w_qkv"], p["w_gate"]]
    hn, proj_a, proj_qkv, proj_gate = _proj_fwd(x1, w["mix_pre_g"], in_weights)
    cqn, ckvn, q, k, v = _mla_front_fwd(proj_a, tabs, w["mla_q_norm_g"], w["mla_kv_norm_g"], p["w_uq"], p["w_kv"])
    o_mla, lse = _attn_fwd(q, k, v)
    ab = (proj_a, SLOT, A_AB // SLOT)
    qkv_n = _gdn_conv_fwd(proj_qkv, p["conv"])
    gb, bb = _gates_fwd(ab, p["a_log"], p["dt_bias"])
    (o_gdn, keep), (ffn2, w_out) = _gdn_fwd(qkv_n, gb, bb, carry=late)
    p["w_out"] = _heads_out(_rows_of(w_out), GDN_D, axis=0)
    cat, mixed, x2 = _mix_fwd(o_mla, o_gdn, proj_gate, x1, p["g_mla_out"], p["g_gdn"], p["w_out"], w["mix_post_g"])
    (h2, y, hg2, hu2), _ = _ffn_fwd("ffn2_fwd", x2, w["ffn2_pre_g"], ffn2, 0, w["ffn2_post_g"])
    dy, loss_lanes = _loss_fwd(y, target)

    g = {}
    (dx2, xn2, dh2, a2, dhg2, dhu2, g["ffn2_pre_g"], g["ffn2_post_g"]), _ = _ffn_bwd(
        "ffn2_bwd", x2, h2, hg2, hu2, dy, w["ffn2_pre_g"], ffn2, 0, w["ffn2_post_g"])
    ffn2_grads = _Scatter([_by_device(_weight_grad("ffn2_dw_gate", dhg2, xn2, BF16, tm=1408)),
                           _by_device(_weight_grad("ffn2_dw_up", dhu2, xn2, BF16, tm=1408)),
                           _by_device(_weight_grad("ffn2_dw_down", a2, dh2, BF16, tm=1408))])
    d = {}
    dmixed, do_mla, delta, do_gdn, dgate, g["mix_post_g"], d["g_mla_out"], d["g_gdn"] = _mix_bwd(
        o_mla, o_gdn, proj_gate, mixed, dx2, p["g_mla_out"], p["g_gdn"], p["w_out"], w["mix_post_g"])
    d["w_out"] = _weight_grad("mix_out_dw", cat, dmixed, BF16)
    dq, dk, dv = _attn_bwd(q, k, v, do_mla, lse, delta)
    (dqkv_n, dgb, dbb), landed_ffn2 = _gdn_bwd(qkv_n, gb, bb, keep, do_gdn, carry=ffn2_grads)
    dab, d["a_log"], d["dt_bias"] = _gates_bwd(ab, p["a_log"], p["dt_bias"], dgb, dbb)
    dproj_qkv, d["conv"] = _gdn_conv_bwd(proj_qkv, p["conv"], dqkv_n)
    dproj_a, dq_p, dkv_p, g["mla_q_norm_g"], g["mla_kv_norm_g"] = _mla_front_bwd(
        proj_a, tabs, w["mla_q_norm_g"], w["mla_kv_norm_g"], p["w_uq"], p["w_kv"], dq, dk, dv, dab)
    d["w_uq"] = _weight_grad("mla_q_dw", dq_p, cqn, BF16)
    d["w_kv"] = _weight_grad("mla_kv_dw", dkv_p, ckvn, BF16)
    d["w_a"] = _weight_grad("proj_a_dw", dproj_a, hn, BF16, tm=640)
    d["w_qkv"] = _weight_grad("proj_qkv_dw", dproj_qkv, hn, BF16)
    d["w_gate"] = _weight_grad("proj_gate_dw", dgate, hn, BF16)
    dx1, dh1, g["mix_pre_g"] = _proj_bwd(x1, w["mix_pre_g"], in_weights, [dproj_a, dproj_qkv, dgate], dx2,
                                        h1, w["ffn1_post_g"])
    g.update(_unlayout_grads(d))
    begun = {}
    blocks = _by_device(_weight_grad("ffn1_w_down_grad", a1, dh1, BF16, tm=1408))
    begun["ffn1_w_down"], token = _scatter_begin("scatter_ffn1_w_down_begin", blocks)
    others = list(OTHER.values())
    (dx, xn1, _, dhg1, dhu1, g["ffn1_pre_g"], g["ffn1_post_g"]), landed_others = _ffn_bwd(
        "ffn1_bwd", x, h1, hg1, hu1, dx1, w["ffn1_pre_g"], w["ffn1"], 0, w["ffn1_post_g"],
        carry=_Scatter([_by_device(g.pop(t)) for t in others]), after=token, keep_act=False)
    landed = dict(zip(list(FFN_NAMES[3:]) + list(OTHER), list(landed_ffn2) + list(landed_others)))
    packed = _pack_small(g, g["gdn_conv_w"].reshape(-1), REDUCE_ROWS)
    packed = packed.at[REDUCE_ROWS - 1, ROW - 1].set(jnp.sum(loss_lanes))
    begun["small"], token = _scatter_begin("reduce_small_begin", jnp.broadcast_to(packed, (N_DEV,) + packed.shape))
    for name, cots, acts in (("ffn1_w_gate", dhg1, xn1), ("ffn1_w_up", dhu1, xn1)):
        blocks = _by_device(_weight_grad(name + "_grad", cots, acts, BF16, tm=1408, after=token))
        begun[name], token = _scatter_begin("scatter_" + name + "_begin", blocks)
    return dx, g, landed, begun, token


MESH_AXES = ("x", "y", "c")
N_LINKS = N_DEV - 1


def _place():
    return tuple(lax.axis_index(a) for a in MESH_AXES)


def _block_of(dev):
    x, y, c = dev
    return 4 * x + 2 * y + c


def _remote_copy(src, dst, sems, k, to):
    send_sems, recv_sems = sems
    return pltpu.make_async_remote_copy(src_ref=src, dst_ref=dst, send_sem=send_sems.at[k], recv_sem=recv_sems.at[k],
                                        device_id=to, device_id_type=pl.DeviceIdType.MESH)


class _Exchange:
    def __init__(self, arrays):
        self.arrays = list(arrays)
        self.n = len(self.arrays)
        self.specs = [pl.BlockSpec(memory_space=pl.ANY)] * self.n
        self.scratch = [pltpu.SemaphoreType.DMA((self.n * N_LINKS,)), pltpu.SemaphoreType.DMA((self.n * N_LINKS,)),
                        pltpu.SemaphoreType.DMA((self.n,))]

    def split(self, refs):
        n = self.n
        return refs[:n], refs[n:2 * n], (refs[2 * n], refs[2 * n + 1]), refs[2 * n + 2]


class _Gather(_Exchange):
    def out_shape(self):
        return [jax.ShapeDtypeStruct((N_DEV,) + a.shape, a.dtype) for a in self.arrays]

    def _plan(self, ins, outs, sems, local_sems):
        x, y, c = _place()
        me, sibling = (x, y, c), (x, y, 1 - c)
        chips = [(1 - x, y), (x, 1 - y), (1 - x, 1 - y)]

        def copy(a, k, block, to, mine=False):
            src = ins[a] if mine else outs[a].at[_block_of(block)]
            return _remote_copy(src, outs[a].at[_block_of(block)], sems, a * N_LINKS + k, to)

        local = [pltpu.make_async_copy(ins[a], outs[a].at[_block_of(me)], local_sems.at[a]) for a in range(self.n)]
        first = []
        for a in range(self.n):
            first.append(copy(a, 0, me, sibling, mine=True))
            first += [copy(a, 1 + j, me, (*chip, c), mine=True) for j, chip in enumerate(chips)]
        return me, sibling, chips, c, copy, local, first

    def start(self, ins, outs, sems, local_sems):
        *_, local, first = self._plan(ins, outs, sems, local_sems)
        for cp in local + first:
            cp.start()

    def finish(self, ins, outs, sems, local_sems):
        me, sibling, chips, c, copy, local, first = self._plan(ins, outs, sems, local_sems)
        passed = []
        for j, chip in enumerate(chips):
            for a in range(self.n):
                copy(a, 1 + j, (*chip, c), me).wait_recv()
                passed.append(copy(a, 4 + j, (*chip, c), sibling))
                passed[-1].start()
        for a in range(self.n):
            copy(a, 0, sibling, me).wait_recv()
            for j, chip in enumerate(chips):
                copy(a, 4 + j, (*chip, 1 - c), me).wait_recv()
        for cp in first + passed:
            cp.wait_send()
        for cp in local:
            cp.wait()


class _Scatter(_Exchange):
    def out_shape(self):
        return [jax.ShapeDtypeStruct(a.shape, a.dtype) for a in self.arrays]

    def _plan(self, ins, outs, sems, local_sems):
        x, y, c = _place()
        me = _block_of((x, y, c))

        def peer(r):
            return (1 - x if r & 4 else x, 1 - y if r & 2 else y, 1 - c if r & 1 else c)

        local = [pltpu.make_async_copy(ins[a].at[me], outs[a].at[me], local_sems.at[a]) for a in range(self.n)]
        sends = [_remote_copy(ins[a].at[_block_of(peer(r))], outs[a].at[me], sems, a * N_LINKS + r - 1, peer(r))
                 for a in range(self.n) for r in range(1, N_DEV)]
        arrivals = [_remote_copy(ins[a].at[me], outs[a].at[_block_of(peer(r))], sems, a * N_LINKS + r - 1, peer(r))
                    for a in range(self.n) for r in range(1, N_DEV)]
        return local, sends, arrivals

    def start(self, ins, outs, sems, local_sems):
        local, sends, _ = self._plan(ins, outs, sems, local_sems)
        for cp in local + sends:
            cp.start()

    def finish(self, ins, outs, sems, local_sems):
        local, sends, arrivals = self._plan(ins, outs, sems, local_sems)
        for cp in arrivals:
            cp.wait_recv()
        for cp in sends:
            cp.wait_send()
        for cp in local:
            cp.wait()


def _exchange(name, plan):
    def body(*refs):
        parts = plan.split(refs)
        plan.start(*parts)
        plan.finish(*parts)

    return pl.pallas_call(
        body, name=name,
        in_specs=plan.specs,
        out_specs=plan.specs,
        out_shape=plan.out_shape(),
        scratch_shapes=plan.scratch,
    )(*plan.arrays)


def _call_carrying(body, plan, operands, *, name, grid, in_specs, out_specs, out_shape, scratch_shapes, compiler_params):
    if plan is None:
        outs = pl.pallas_call(body, name=name, grid=grid, in_specs=in_specs, out_specs=out_specs, out_shape=out_shape,
                              scratch_shapes=scratch_shapes, compiler_params=compiler_params)(*operands)
        return outs, []
    n_i, n_o, n_s, k = len(in_specs), len(out_specs), len(scratch_shapes), plan.n

    def whole(*refs):
        cut = [n_i, n_i + k, n_i + k + n_o, n_i + 2 * k + n_o, n_i + 2 * k + n_o + n_s]
        own_in, ex_in, own_out, ex_out, own_scr, ex_scr = (refs[a:b] for a, b in zip([0] + cut, cut + [len(refs)]))
        parts = plan.split(ex_in + ex_out + ex_scr)
        first = last = True
        for axis, size in enumerate(grid):
            first = first & (pl.program_id(axis) == 0)
            last = last & (pl.program_id(axis) == size - 1)

        @pl.when(first)
        def _():
            plan.start(*parts)

        body(*own_in, *own_out, *own_scr)

        @pl.when(last)
        def _():
            plan.finish(*parts)

    outs = pl.pallas_call(
        whole, name=name, grid=grid,
        in_specs=list(in_specs) + plan.specs, out_specs=list(out_specs) + plan.specs,
        out_shape=list(out_shape) + plan.out_shape(), scratch_shapes=list(scratch_shapes) + plan.scratch,
        compiler_params=compiler_params,
    )(*operands, *plan.arrays)
    return outs[:n_o], outs[n_o:]


def _row_tile(rows, target=256):
    best = rows
    for cand in range(16, min(rows, target) + 1, 16):
        if rows % cand == 0:
            best = cand
    return best


def _sum_blocks(name, blocks, after=None):
    rows, width = blocks.shape[-2:]
    tm = _row_tile(rows)

    def body(x_ref, *rest):
        acc = x_ref[0].astype(F32)
        for d in range(1, N_DEV):
            acc = acc + x_ref[d].astype(F32)
        rest[-1][...] = acc

    ordered = [] if after is None else [after]
    return pl.pallas_call(
        body, name=name,
        grid=(rows // tm,),
        in_specs=[pl.BlockSpec((N_DEV, tm, width), lambda i: (0, i, 0))] + [pl.BlockSpec(memory_space=pl.ANY)] * len(ordered),
        out_specs=pl.BlockSpec((tm, width), lambda i: (i, 0)),
        out_shape=jax.ShapeDtypeStruct((rows, width), F32),
        compiler_params=pltpu.CompilerParams(dimension_semantics=("parallel",)),
    )(blocks, *ordered)


def _split_plan(src_ref, land_ref, sems):
    x, y, c = _place()
    me = _block_of((x, y, c))

    def peer(r):
        return (1 - x if r & 4 else x, 1 - y if r & 2 else y, 1 - c if r & 1 else c)

    sends = [_remote_copy(src_ref.at[_block_of(peer(r))], land_ref.at[me], sems, r - 1, peer(r)) for r in range(1, N_DEV)]
    arrivals = [_remote_copy(src_ref.at[me], land_ref.at[_block_of(peer(r))], sems, r - 1, peer(r)) for r in range(1, N_DEV)]
    return sends, arrivals


def _scatter_begin(name, blocks):
    def body(src_ref, land_ref, send_sems, recv_sems, src_thru, land_thru, token_ref):
        for cp in _split_plan(src_ref, land_ref, (send_sems, recv_sems))[0]:
            cp.start()
        token_ref[...] = jnp.zeros_like(token_ref)

    hbm, sem = pl.BlockSpec(memory_space=pltpu.HBM), pl.BlockSpec(memory_space=pltpu.SEMAPHORE)
    zone = pltpu.HBM(blocks.shape, blocks.dtype)
    *handles, token = pl.pallas_call(
        body, name=name,
        in_specs=(hbm, hbm),
        out_specs=(sem, sem, hbm, hbm, pl.BlockSpec(memory_space=pltpu.VMEM)),
        out_shape=(pltpu.SemaphoreType.DMA((N_LINKS,)), pltpu.SemaphoreType.DMA((N_LINKS,)), zone, zone,
                   jax.ShapeDtypeStruct((8, SLOT), F32)),
        input_output_aliases={0: 2, 1: 3},
        compiler_params=pltpu.CompilerParams(has_side_effects=pltpu.SideEffectType.DATAFLOW_SIDE_EFFECTING),
    )(pltpu.with_memory_space_constraint(blocks, pltpu.HBM),
      pltpu.with_memory_space_constraint(lax.empty(blocks.shape, blocks.dtype), pltpu.HBM))
    return handles, token


def _scatter_end(name, handles, after):
    send_sems, recv_sems, src, zone = handles

    def body(src_ref, land_ref, send_sems, recv_sems, after_ref, src_dead, got_ref):
        sends, arrivals = _split_plan(src_ref, land_ref, (send_sems, recv_sems))
        for cp in arrivals:
            cp.wait_recv()
        for cp in sends:
            cp.wait_send()

    hbm, sem = pl.BlockSpec(memory_space=pltpu.HBM), pl.BlockSpec(memory_space=pltpu.SEMAPHORE)
    sent, landed = pl.pallas_call(
        body, name=name,
        in_specs=(hbm, hbm, sem, sem, pl.BlockSpec(memory_space=pl.ANY)),
        out_specs=(hbm, hbm),
        out_shape=(pltpu.HBM(src.shape, src.dtype), pltpu.HBM(zone.shape, zone.dtype)),
        input_output_aliases={0: 0, 1: 1},
        compiler_params=pltpu.CompilerParams(has_side_effects=pltpu.SideEffectType.DATAFLOW_SIDE_EFFECTING),
    )(src, zone, send_sems, recv_sems, after)
    me = _block_of(_place())
    return lax.dynamic_update_slice_in_dim(landed, lax.dynamic_slice_in_dim(sent, me, 1, axis=0), me, axis=0)


def _adamw_values(wv, gv, mv, vv):
    m2 = ADAM_B1 * mv + (1.0 - ADAM_B1) * gv
    v2 = ADAM_B2 * vv + (1.0 - ADAM_B2) * jnp.square(gv)
    m_hat = m2 / (1.0 - ADAM_B1 ** ADAM_STEP)
    v_hat = v2 / (1.0 - ADAM_B2 ** ADAM_STEP)
    return [-ADAM_LR * (m_hat / (jnp.sqrt(v_hat) + ADAM_EPS) + ADAM_WD * wv), m2, v2]


def _adamw(name, w, g, m, v):
    def fn(rows, consts):
        return _adamw_values(*rows), []

    return _rowwise(name, fn, [w, g, m, v], [], [(w.shape[1], F32)] * 3, tm=_row_tile(w.shape[0]))


def _sum_adamw(name, blocks, w, m, v, after=None):
    rows, width = w.shape
    tm = _row_tile(rows)

    def body(x_ref, w_ref, m_ref, v_ref, *rest):
        acc = x_ref[0].astype(F32)
        for d in range(1, N_DEV):
            acc = acc + x_ref[d].astype(F32)
        rest[-4][...] = acc
        for ref, val in zip(rest[-3:], _adamw_values(w_ref[...], acc, m_ref[...], v_ref[...])):
            ref[...] = val

    ordered = [] if after is None else [after]
    tile = pl.BlockSpec((tm, width), lambda i: (i, 0))
    return pl.pallas_call(
        body, name=name,
        grid=(rows // tm,),
        in_specs=[pl.BlockSpec((N_DEV, tm, width), lambda i: (0, i, 0))] + [tile] * 3 + [pl.BlockSpec(memory_space=pl.ANY)] * len(ordered),
        out_specs=[tile] * 4,
        out_shape=[jax.ShapeDtypeStruct((rows, width), F32)] * 4,
        compiler_params=pltpu.CompilerParams(dimension_semantics=("parallel",)),
    )(blocks, w, m, v, *ordered)


ROW = 1024
FFN_NAMES = ("ffn1_w_gate", "ffn1_w_up", "ffn1_w_down", "ffn2_w_gate", "ffn2_w_up", "ffn2_w_down")
OTHER = {"w_in": "w_in_t", "mla_w_uq": "uq_t", "mla_w_ukv": "ukv_t", "w_out": "w_out"}
BY_COLUMNS = ("ffn1_w_gate", "ffn1_w_up", "ffn2_w_gate", "ffn2_w_up", "w_in", "mla_w_uq", "mla_w_ukv")
SMALL = {
    "ffn1_pre_g": (1024, 1024), "ffn1_post_g": (1024, 1024), "mix_pre_g": (1024, 1024), "mla_q_norm_g": (256, 256),
    "mla_kv_norm_g": (128, 128), "mla_out_g": (512, 512), "gdn_a_log": (8, 128), "gdn_dt_bias": (8, 128),
    "gdn_norm_g": (64, 128), "mix_post_g": (1024, 1024), "ffn2_pre_g": (1024, 1024), "ffn2_post_g": (1024, 1024),
}
CONV_SHAPE = (GDN_CONV, 3 * N_HEADS * GDN_D)
CONV_SHARD = (GDN_CONV, CONV_SHAPE[1] // N_DEV)
CONV_LANES = CONV_SHAPE[0] * CONV_SHAPE[1]
SMALL_ROWS = 8
REDUCE_ROWS = 16


def _pack_small(vecs, conv, rows):
    parts = [_pad_lanes(vecs[n].reshape(1, -1), 0, r) for n, (_, r) in SMALL.items()]
    parts.append(conv.reshape(1, -1))
    flat = jnp.concatenate(parts, axis=1)
    return _pad_lanes(flat, 0, rows * ROW).reshape(rows, ROW)


def _unpack_small(buf):
    flat = buf.reshape(1, -1)
    out, at = {}, 0
    for n, (w, r) in SMALL.items():
        out[n] = flat[:, at:at + w]
        at += r
    return out, flat[0, at:]


def kernel(x, positions, ffn1_pre_g, ffn1_w_gate, ffn1_w_up, ffn1_w_down, ffn1_post_g, mix_pre_g, w_in, mla_q_norm_g, mla_w_uq, mla_kv_norm_g, mla_w_ukv, mla_out_g, gdn_conv_w, gdn_a_log, gdn_dt_bias, gdn_norm_g, w_out, mix_post_g, ffn2_pre_g, ffn2_w_gate, ffn2_w_up, ffn2_w_down, ffn2_post_g, loss_target, m_ffn1_pre_g, m_ffn1_w_gate, m_ffn1_w_up, m_ffn1_w_down, m_ffn1_post_g, m_mix_pre_g, m_w_in, m_mla_q_norm_g, m_mla_w_uq, m_mla_kv_norm_g, m_mla_w_ukv, m_mla_out_g, m_gdn_conv_w, m_gdn_a_log, m_gdn_dt_bias, m_gdn_norm_g, m_w_out, m_mix_post_g, m_ffn2_pre_g, m_ffn2_w_gate, m_ffn2_w_up, m_ffn2_w_down, m_ffn2_post_g, v_ffn1_pre_g, v_ffn1_w_gate, v_ffn1_w_up, v_ffn1_w_down, v_ffn1_post_g, v_mix_pre_g, v_w_in, v_mla_q_norm_g, v_mla_w_uq, v_mla_kv_norm_g, v_mla_w_ukv, v_mla_out_g, v_gdn_conv_w, v_gdn_a_log, v_gdn_dt_bias, v_gdn_norm_g, v_w_out, v_mix_post_g, v_ffn2_pre_g, v_ffn2_w_gate, v_ffn2_w_up, v_ffn2_w_down, v_ffn2_post_g):
    given = dict(locals())
    order = ["ffn1_pre_g", "ffn1_w_gate", "ffn1_w_up", "ffn1_w_down", "ffn1_post_g", "mix_pre_g", "w_in", "mla_q_norm_g",
             "mla_w_uq", "mla_kv_norm_g", "mla_w_ukv", "mla_out_g", "gdn_conv_w", "gdn_a_log", "gdn_dt_bias", "gdn_norm_g",
             "w_out", "mix_post_g", "ffn2_pre_g", "ffn2_w_gate", "ffn2_w_up", "ffn2_w_down", "ffn2_post_g"]
    assert sorted(order) == sorted(list(FFN_NAMES) + list(OTHER) + list(SMALL) + ["gdn_conv_w"])

    def drop_depth(a):
        return a[0] if a.ndim == 3 else a

    wts = {n: drop_depth(given[n]) for n in order}
    mom = {n: drop_depth(given["m_" + n]) for n in order}
    var = {n: drop_depth(given["v_" + n]) for n in order}
    me = _block_of(_place())

    def wire(n):
        return (wts[n].T if n in BY_COLUMNS else wts[n]).astype(BF16)

    (ffn1,) = _exchange("gather_first", _Gather([jnp.stack([wire(n) for n in FFN_NAMES[:3]])]))
    mid = _Gather([wire(n) for n in ("w_in", "mla_w_uq", "mla_w_ukv")] + [wts["gdn_conv_w"]])
    late = _Gather([jnp.stack([wire(n) for n in FFN_NAMES[3:]]), wire("w_out")])
    full = {n: wts[n] for n in SMALL}
    full["ffn1"] = ffn1

    dx, grads, landed, begun, token = _local_step(x[0], positions[0], loss_target[0], full, mid, late)

    grad, outs = {}, {"delta": {}, "new_m": {}, "new_v": {}}

    def finish(n, blocks, after=None):
        flip = n in BY_COLUMNS and wts[n].shape[1] % SLOT != 0
        turn = (lambda a: a.T) if flip else (lambda a: a)
        if n in BY_COLUMNS and not flip:
            grad[n] = _sum_blocks("sum_" + n, blocks, after=after).T
            new = _adamw("adamw_" + n, wts[n], grad[n], mom[n], var[n])
        else:
            total, *new = _sum_adamw("update_" + n, blocks, turn(wts[n]), turn(mom[n]), turn(var[n]), after=after)
            grad[n] = turn(total)
        outs["delta"][n], outs["new_m"][n], outs["new_v"][n] = (turn(a) for a in new)
        return new[2]

    for n, blocks in landed.items():
        token = finish(n, blocks, after=token)
    small_handles = begun.pop("small")
    for n, handles in begun.items():
        token = finish(n, _scatter_end("scatter_" + n + "_end", handles, after=token))

    small_sum = _sum_blocks("sum_small", _scatter_end("reduce_small_end", small_handles, after=token))
    loss = small_sum[REDUCE_ROWS - 1, ROW - 1]
    small_grad, conv_grad_full = _unpack_small(small_sum)
    grad.update(small_grad)
    grad["gdn_conv_w"] = lax.dynamic_slice(conv_grad_full[:CONV_LANES].reshape(CONV_SHAPE), (0, me * CONV_SHARD[1]), CONV_SHARD)
    outs["grad"] = grad
    small = [_pack_small(s, s["gdn_conv_w"].reshape(-1), SMALL_ROWS) for s in (wts, grad, mom, var)]
    for kind, s in zip(("delta", "new_m", "new_v"), _adamw("adamw_small", *small)):
        vecs, conv = _unpack_small(s)
        outs[kind].update(vecs)
        outs[kind]["gdn_conv_w"] = conv[:CONV_SHARD[0] * CONV_SHARD[1]].reshape(CONV_SHARD)
    result = [loss, dx[None]]
    for kind in ("grad", "delta", "new_m", "new_v"):
        result += [outs[kind][n].reshape(given[n].shape) for n in order]
    return tuple(result)
```

```python
import jax
import jax.numpy as jnp
from jax import lax
from jax.experimental import pallas as pl
from jax.experimental.pallas import tpu as pltpu

F32 = jnp.float32
BF16 = jnp.bfloat16
HI = lax.Precision.HIGH

N_DEV = 8
N_HEADS = 8
SLOT = 128
MLA_Q_RANK = 256
MLA_KV_RANK = 128
MLA_NOPE = 64
MLA_ROPE = 32
MLA_V = 64
GDN_D = 64
GDN_CONV = 4
GDN_CHUNK = 64
ROPE_THETA = 10000.0
EPS = 1e-6
ADAM_LR, ADAM_B1, ADAM_B2, ADAM_EPS, ADAM_WD, ADAM_STEP = 0.001, 0.9, 0.999, 1e-08, 0.01, 10


def _dot(a, b, ca, cb, precision=None):
    lead = a.ndim - 2
    batch = tuple(range(lead))
    return lax.dot_general(a, b, (((lead + ca,), (lead + cb,)), (batch, batch)), precision=precision,
                           preferred_element_type=F32)


def _nn(a, b, precision=None):
    return _dot(a, b, 1, 0, precision)


def _nt(a, b, precision=None):
    return _dot(a, b, 1, 1, precision)


def _tn(a, b, precision=None):
    return _dot(a, b, 0, 0, precision)


def _sigmoid(x):
    return 1.0 / (1.0 + jnp.exp(-x))


def _silu(x):
    return x * _sigmoid(x)


def _rms(x, g, n):
    ms = jnp.sum(x * x, axis=-1, keepdims=True) * (1.0 / n)
    return x * lax.rsqrt(ms + EPS) * g


def _chunk_masks():
    c = GDN_CHUNK
    i = lax.broadcasted_iota(jnp.int32, (c, c), 0)
    j = lax.broadcasted_iota(jnp.int32, (c, c), 1)
    lower = i >= j
    strict = i > j
    eye = (i == j).astype(F32)
    blocks = []
    b = 1
    while b < c:
        same = (i // (2 * b)) == (j // (2 * b))
        blocks.append(same & ((i % (2 * b)) >= b) & ((j % (2 * b)) < b))
        b *= 2
    return lower, strict, eye, blocks


def _unit_lower_inverse(low, eye, blocks):
    t = eye - jnp.where(blocks[0], low, 0.0)
    for m in blocks[1:]:
        lo = jnp.where(m, low, 0.0)
        t = t - _nn(t, _nn(lo, t, HI), HI)
    return t


@jax.custom_vjp
def _known_inverse(low, tinv):
    return tinv


def _known_inverse_fwd(low, tinv):
    return tinv, tinv


def _known_inverse_bwd(tinv, dt):
    return -_tn(tinv, _nt(dt, tinv, HI), HI), jnp.zeros_like(tinv)


_known_inverse.defvjp(_known_inverse_fwd, _known_inverse_bwd)

_PRODUCTS = {"nn": _nn, "nt": _nt, "tn": _tn}


@jax.custom_vjp
def _known_nn(a, b, c):
    return c


@jax.custom_vjp
def _known_nt(a, b, c):
    return c


@jax.custom_vjp
def _known_tn(a, b, c):
    return c


def _known_fwd(a, b, c):
    return c, (a, b, c)


_known_nn.defvjp(_known_fwd, lambda r, dc: (_nt(dc, r[1], HI), _tn(r[0], dc, HI), jnp.zeros_like(r[2])))
_known_nt.defvjp(_known_fwd, lambda r, dc: (_nn(dc, r[1], HI), _tn(dc, r[0], HI), jnp.zeros_like(r[2])))
_known_tn.defvjp(_known_fwd, lambda r, dc: (_nt(r[1], dc, HI), _nn(r[0], dc, HI), jnp.zeros_like(r[2])))
_KNOWN = {"nn": _known_nn, "nt": _known_nt, "tn": _known_tn}
GDN_PRODUCTS = 8
GDN_KEPT = 2 + GDN_PRODUCTS


def _gdn_chunk(q, k, v, gc, bb, s, masks, known=None):
    lower, strict, eye, blocks = masks
    made = []

    def product(kind, a, b):
        c = _PRODUCTS[kind](a, b, HI) if known is None else _KNOWN[kind](a, b, known[1 + len(made)])
        made.append(c)
        return c

    qs = q * (GDN_D ** -0.5)
    gct = jnp.swapaxes(gc, -1, -2)
    decay = jnp.exp(jnp.where(lower, gc - gct, -1e30))
    kb = k * bb
    low = jnp.where(strict, product("nt", kb, k) * decay, 0.0)
    tinv = _unit_lower_inverse(low, eye, blocks) if known is None else _known_inverse(low, known[0])
    eg = jnp.exp(gc)
    w = product("nn", tinv, kb * eg)
    u = product("nn", tinv, v * bb)
    attn = product("nt", qs, k) * decay
    last = lax.broadcasted_iota(jnp.int32, gc.shape[-2:], 0) == GDN_CHUNK - 1
    g_end = jnp.sum(jnp.where(last, gc, 0.0), axis=-2, keepdims=True)
    k_dec = k * jnp.exp(g_end - gc)
    v_new = u - product("nn", w, s)
    o = product("nn", qs * eg, s) + product("nn", attn, v_new)
    s_new = s * jnp.exp(g_end) + product("tn", k_dec, v_new)
    assert len(made) == GDN_PRODUCTS
    return o, s_new, [tinv] + made


GDN_GROUP = 8
GDN_GROUPS = N_HEADS // GDN_GROUP


def _group_heads(ref):
    return jnp.stack([ref[:, pl.ds(j * SLOT, GDN_D)] for j in range(GDN_GROUP)])


def _ungroup_heads(ref, val):
    pad = jnp.zeros((GDN_CHUNK, SLOT - GDN_D), F32)
    for j in range(GDN_GROUP):
        ref[:, pl.ds(j * SLOT, GDN_D)] = val[j]
        ref[:, pl.ds(j * SLOT + GDN_D, SLOT - GDN_D)] = pad


def _gdn_fwd(qkv, gb, bb, carry=None):
    t = qkv.shape[0]
    n_chunks = t // GDN_CHUNK
    d = GDN_D

    def body(q_ref, k_ref, v_ref, g_ref, b_ref, o_ref, keep_ref, s_ref):
        @pl.when(pl.program_id(1) == 0)
        def _():
            s_ref[...] = jnp.zeros_like(s_ref)

        s = s_ref[...]
        keep_ref[:, 0, 0] = s
        o, s_new, made = _gdn_chunk(*[_group_heads(r) for r in (q_ref, k_ref, v_ref, g_ref, b_ref)], s, _chunk_masks())
        for i, val in enumerate(made):
            keep_ref[:, 0, 1 + i] = val
        s_ref[...] = s_new
        _ungroup_heads(o_ref, o)

    def spec(kind=0):
        return pl.BlockSpec((GDN_CHUNK, GDN_GROUP * SLOT), lambda h, n: (n, kind * GDN_GROUPS + h))

    return _call_carrying(
        body, carry, (qkv, qkv, qkv, gb, bb), name="gdn_fwd",
        grid=(GDN_GROUPS, n_chunks),
        in_specs=[spec(0), spec(1), spec(2), spec(), spec()],
        out_specs=[spec(), pl.BlockSpec((GDN_GROUP, 1, GDN_KEPT, d, d), lambda h, n: (h, n, 0, 0, 0))],
        out_shape=[jax.ShapeDtypeStruct((t, N_HEADS * SLOT), F32), jax.ShapeDtypeStruct((N_HEADS, n_chunks, GDN_KEPT, d, d), F32)],
        scratch_shapes=[pltpu.VMEM((GDN_GROUP, d, d), F32)],
        compiler_params=pltpu.CompilerParams(dimension_semantics=("arbitrary", "arbitrary")),
    )


def _gdn_bwd(qkv, gb, bb, keep, do, carry=None):
    t = qkv.shape[0]
    n_chunks = t // GDN_CHUNK
    d = GDN_D

    def body(q_ref, k_ref, v_ref, g_ref, b_ref, keep_ref, do_ref, dqkv_ref, dg_ref, db_ref, ds_ref):
        @pl.when(pl.program_id(1) == 0)
        def _():
            ds_ref[...] = jnp.zeros_like(ds_ref)

        masks = _chunk_masks()
        known = [keep_ref[:, 0, 1 + i] for i in range(GDN_KEPT - 1)]
        _, pull = jax.vjp(lambda *a: _gdn_chunk(*a, masks, known)[:2],
                          *[_group_heads(r) for r in (q_ref, k_ref, v_ref, g_ref, b_ref)], keep_ref[:, 0, 0])
        dq, dk, dv, dg, db, ds = pull((_group_heads(do_ref), ds_ref[...]))
        ds_ref[...] = ds
        for i, val in enumerate((dq, dk, dv)):
            _ungroup_heads(dqkv_ref.at[i], val)
        _ungroup_heads(dg_ref, dg)
        _ungroup_heads(db_ref, db)

    def spec(kind=0):
        return pl.BlockSpec((GDN_CHUNK, GDN_GROUP * SLOT), lambda h, n: (n_chunks - 1 - n, kind * GDN_GROUPS + h))

    return _call_carrying(
        body, carry, (qkv, qkv, qkv, gb, bb, keep, do), name="gdn_bwd",
        grid=(GDN_GROUPS, n_chunks),
        in_specs=[spec(0), spec(1), spec(2), spec(), spec(),
                  pl.BlockSpec((GDN_GROUP, 1, GDN_KEPT, d, d), lambda h, n: (h, n_chunks - 1 - n, 0, 0, 0)), spec()],
        out_specs=[pl.BlockSpec((3, GDN_CHUNK, GDN_GROUP * SLOT), lambda h, n: (0, n_chunks - 1 - n, h)), spec(), spec()],
        out_shape=[jax.ShapeDtypeStruct((3, t, N_HEADS * SLOT), F32)] + [jax.ShapeDtypeStruct((t, N_HEADS * SLOT), F32)] * 2,
        scratch_shapes=[pltpu.VMEM((GDN_GROUP, d, d), F32)],
        compiler_params=pltpu.CompilerParams(dimension_semantics=("arbitrary", "arbitrary")),
    )


def _rowwise(name, fn, rows, consts, outs, sums=(), tm=512):
    rows = [x if isinstance(x, tuple) else (x, x.shape[1], 0) for x in rows]
    t = rows[0][0].shape[0]
    tm = min(tm, t)
    steps = t // tm
    n_r, n_c, n_o, n_s = len(rows), len(consts), len(outs), len(sums)

    def window(width, block):
        return pl.BlockSpec((tm, width), lambda i: (i, block))

    def body(*refs):
        r, c = refs[:n_r], refs[n_r:n_r + n_c]
        o, s = refs[n_r + n_c:n_r + n_c + n_o], refs[n_r + n_c + n_o:]
        vals, tot = fn([x[...] for x in r], [x[...] for x in c])
        for ref, val in zip(o, vals):
            ref[...] = val.astype(ref.dtype)
        if n_s:
            @pl.when(pl.program_id(0) == 0)
            def _():
                for ref in s:
                    ref[...] = jnp.zeros_like(ref)

            for ref, val in zip(s, tot):
                ref[...] += val

    return pl.pallas_call(
        body, name=name,
        grid=(steps,),
        in_specs=[window(w, b) for _, w, b in rows] + [pl.BlockSpec(x.shape, lambda i: (0, 0)) for x in consts],
        out_specs=[pl.BlockSpec((tm, w), lambda i: (i, 0)) for w, _ in outs]
        + [pl.BlockSpec((1, w), lambda i: (0, 0)) for w in sums],
        out_shape=[jax.ShapeDtypeStruct((t, w), dt) for w, dt in outs]
        + [jax.ShapeDtypeStruct((1, w), F32) for w in sums],
        compiler_params=pltpu.CompilerParams(dimension_semantics=("arbitrary",)),
    )(*[x for x, _, _ in rows], *consts)


def _tile(dim, target):
    if dim <= target:
        return dim
    best = None
    for cand in range(128, target + 1, 128):
        if dim % cand == 0:
            best = cand
    assert best is not None, (dim, target)
    return best


def _matmul(name, a, b, mode, out_dtype=F32, tm=1024, tn=1024, tk=2048, after=None):
    if mode == "nn":
        (m, k), n = a.shape, b.shape[1]
    elif mode == "nt":
        (m, k), n = a.shape, b.shape[0]
    else:
        (k, m), n = a.shape, b.shape[1]
    tm, tn, tk = _tile(m, tm), _tile(n, tn), _tile(k, tk)
    k_steps = k // tk
    product = {"nn": _nn, "nt": _nt, "tn": _tn}[mode]

    def body(a_ref, b_ref, *rest):
        o_ref, acc_ref = rest[-2:]
        part = product(a_ref[...].astype(BF16), b_ref[...].astype(BF16))
        if k_steps == 1:
            o_ref[...] = part.astype(o_ref.dtype)
        else:
            kk = pl.program_id(2)

            @pl.when(kk == 0)
            def _():
                acc_ref[...] = part

            @pl.when(kk > 0)
            def _():
                acc_ref[...] += part

            @pl.when(kk == k_steps - 1)
            def _():
                o_ref[...] = acc_ref[...].astype(o_ref.dtype)

    a_spec = pl.BlockSpec((tk, tm), lambda i, j, kk: (kk, i)) if mode == "tn" else pl.BlockSpec((tm, tk), lambda i, j, kk: (i, kk))
    b_spec = pl.BlockSpec((tn, tk), lambda i, j, kk: (j, kk)) if mode == "nt" else pl.BlockSpec((tk, tn), lambda i, j, kk: (kk, j))
    ordered = [] if after is None else [after]
    return pl.pallas_call(
        body, name=name,
        grid=(m // tm, n // tn, k_steps),
        in_specs=[a_spec, b_spec] + [pl.BlockSpec(memory_space=pl.ANY)] * len(ordered),
        out_specs=pl.BlockSpec((tm, tn), lambda i, j, kk: (i, j)),
        out_shape=jax.ShapeDtypeStruct((m, n), out_dtype),
        scratch_shapes=[pltpu.VMEM((tm, tn) if k_steps > 1 else (8, 128), F32)],
        compiler_params=pltpu.CompilerParams(dimension_semantics=("parallel", "parallel", "arbitrary")),
    )(a, b, *ordered)


FFN_TM = 512
FFN_BWD_TM = 256
FFN_BLOCKS = 4
FFN_GATE, FFN_UP, FFN_DOWN = 0, 1, 2


def _ffn_weight_specs(ffn_w, first):
    _, _, rows, dm = ffn_w.shape

    def spec(k):
        return pl.BlockSpec((FFN_BLOCKS, None, rows, dm), lambda i, j: (j, first + k, 0, 0))

    return [spec(FFN_GATE), spec(FFN_UP), spec(FFN_DOWN)], FFN_BLOCKS * rows


def _ffn_fwd(name, x, g_pre, ffn_w, first, g_post, carry=None):
    t, dm = x.shape
    tm = min(FFN_TM, t)
    w_specs, tf = _ffn_weight_specs(ffn_w, first)
    f_steps = N_DEV // FFN_BLOCKS

    def body(x_ref, gpre_ref, wg_ref, wu_ref, wd_ref, gpost_ref, h_ref, y_ref, hg_ref, hu_ref, a_ref, xn_ref, acc_ref):
        j = pl.program_id(1)

        @pl.when(j == 0)
        def _():
            xn_ref[...] = _rms(x_ref[...], gpre_ref[...], dm).astype(BF16)
            acc_ref[...] = jnp.zeros_like(acc_ref)

        xn = xn_ref[...]
        wg, wu, wd = (r[...].reshape(tf, dm) for r in (wg_ref, wu_ref, wd_ref))
        hg, hu = _nt(xn, wg), _nt(xn, wu)
        hg_ref[...] = hg.astype(BF16)
        hu_ref[...] = hu.astype(BF16)
        a = (_silu(hg) * hu).astype(BF16)
        a_ref[...] = a
        acc_ref[...] += _nn(a, wd)

        @pl.when(j == f_steps - 1)
        def _():
            h = acc_ref[...]
            h_ref[...] = h
            y_ref[...] = x_ref[...] + 0.5 * _rms(h, gpost_ref[...], dm)

    row = pl.BlockSpec((tm, dm), lambda i, j: (i, 0))
    vec = pl.BlockSpec((1, dm), lambda i, j: (0, 0))
    wide = pl.BlockSpec((tm, tf), lambda i, j: (i, j))
    return _call_carrying(
        body, carry, (x, g_pre, ffn_w, ffn_w, ffn_w, g_post), name=name,
        grid=(t // tm, f_steps),
        in_specs=[row, vec, *w_specs, vec],
        out_specs=[row, row, wide, wide, wide],
        out_shape=[jax.ShapeDtypeStruct((t, dm), F32)] * 2 + [jax.ShapeDtypeStruct((t, f_steps * tf), BF16)] * 3,
        scratch_shapes=[pltpu.VMEM((tm, dm), BF16), pltpu.VMEM((tm, dm), F32)],
        compiler_params=pltpu.CompilerParams(dimension_semantics=("arbitrary", "arbitrary")),
    )


def _ffn_bwd(name, x, h, hg, hu, dy, g_pre, ffn_w, first, g_post, carry=None, after=None):
    t, dm = x.shape
    tm = min(FFN_BWD_TM, t)
    w_specs, tf = _ffn_weight_specs(ffn_w, first)
    f_steps = N_DEV // FFN_BLOCKS
    f = f_steps * tf

    def post(hv, g):
        return 0.5 * _rms(hv, g, dm)

    def pre(xv, g):
        return _rms(xv, g, dm)

    def body(x_ref, h_ref, dy_ref, hg_ref, hu_ref, gpre_ref, wg_ref, wu_ref, wd_ref, gpost_ref,
             dx_ref, xn_ref, dh_ref, dhg_ref, dhu_ref, dgpre_ref, dgpost_ref, acc_ref):
        i, j = pl.program_id(0), pl.program_id(1)

        @pl.when((i == 0) & (j == 0))
        def _():
            dgpre_ref[...] = jnp.zeros_like(dgpre_ref)
            dgpost_ref[...] = jnp.zeros_like(dgpost_ref)

        @pl.when(j == 0)
        def _():
            xn_ref[...] = pre(x_ref[...], gpre_ref[...]).astype(BF16)
            _, pull = jax.vjp(post, h_ref[...], gpost_ref[...])
            dh, dg = pull(dy_ref[...])
            dh_ref[...] = dh.astype(BF16)
            dgpost_ref[...] += dg
            acc_ref[...] = jnp.zeros_like(acc_ref)

        wg, wu, wd = (r[...].reshape(tf, dm) for r in (wg_ref, wu_ref, wd_ref))
        hg, hu = hg_ref[...].astype(F32), hu_ref[...].astype(F32)
        da = _nt(dh_ref[...], wd)
        sig = _sigmoid(hg)
        act = hg * sig
        dhu = (da * act).astype(BF16)
        dhg = (da * hu * (sig * (1.0 + hg * (1.0 - sig)))).astype(BF16)
        dhg_ref[...] = dhg
        dhu_ref[...] = dhu
        acc_ref[...] += _nn(dhg, wg) + _nn(dhu, wu)

        @pl.when(j == f_steps - 1)
        def _():
            _, pull = jax.vjp(pre, x_ref[...], gpre_ref[...])
            dx, dg = pull(acc_ref[...])
            dx_ref[...] = dy_ref[...] + dx
            dgpre_ref[...] += dg

    row = pl.BlockSpec((tm, dm), lambda i, j: (i, 0))
    vec = pl.BlockSpec((1, dm), lambda i, j: (0, 0))
    wide = pl.BlockSpec((tm, tf), lambda i, j: (i, j))
    ordered = [] if after is None else [after]

    def after_it(*refs):
        body(*refs[:10], *refs[10 + len(ordered):])

    return _call_carrying(
        after_it, carry, (x, h, dy, hg, hu, g_pre, ffn_w, ffn_w, ffn_w, g_post, *ordered), name=name,
        grid=(t // tm, f_steps),
        in_specs=[row, row, row, wide, wide, vec, *w_specs, vec] + [pl.BlockSpec(memory_space=pl.ANY)] * len(ordered),
        out_specs=[row, row, row, wide, wide, vec, vec],
        out_shape=[jax.ShapeDtypeStruct((t, dm), F32), jax.ShapeDtypeStruct((t, dm), BF16), jax.ShapeDtypeStruct((t, dm), BF16),
                   jax.ShapeDtypeStruct((t, f), BF16), jax.ShapeDtypeStruct((t, f), BF16),
                   jax.ShapeDtypeStruct((1, dm), F32), jax.ShapeDtypeStruct((1, dm), F32)],
        scratch_shapes=[pltpu.VMEM((tm, dm), F32)],
        compiler_params=pltpu.CompilerParams(dimension_semantics=("arbitrary", "arbitrary")),
    )


ATT_T = 512
ATT_GROUP = 4
ATT_GROUP_FWD = 8
ATT_SCALE = (MLA_NOPE + MLA_ROPE) ** -0.5


def _stack_slots(ref, group):
    return jnp.stack([ref[:, pl.ds(j * SLOT, SLOT)] for j in range(group)])


def _unstack_slots(ref, val):
    for j in range(val.shape[0]):
        ref[:, pl.ds(j * SLOT, SLOT)] = val[j].astype(ref.dtype)


def _scores(q, k, diagonal):
    s = _nt(q, k) * ATT_SCALE
    if diagonal:
        row = lax.broadcasted_iota(jnp.int32, s.shape[1:], 0)
        col = lax.broadcasted_iota(jnp.int32, s.shape[1:], 1)
        s = jnp.where(col <= row, s, -1e30)
    return s


def _attn_pairs(steps, q_major):
    pairs = ([(qi, ki) for qi in range(steps) for ki in range(qi + 1)] if q_major
             else [(qi, ki) for ki in range(steps) for qi in range(ki, steps)])
    return jnp.array([p[0] for p in pairs], jnp.int32), jnp.array([p[1] for p in pairs], jnp.int32)


def _attn_specs(tile, group):
    width = group * SLOT
    return (pl.BlockSpec((tile, width), lambda h, p, qt, kt: (qt[p], h)),
            pl.BlockSpec((tile, width), lambda h, p, qt, kt: (kt[p], h)))


def _attn_fwd(q, k, v):
    t = q.shape[0]
    tile = min(ATT_T, t)
    steps = t // tile
    g = ATT_GROUP_FWD

    strip = min(SLOT, tile)

    def body(qt_ref, kt_ref, q_ref, k_ref, v_ref, o_ref, lse_ref, m_ref, l_ref, alpha_ref, acc_ref, s_ref, p_ref):
        qi, ki = qt_ref[pl.program_id(1)], kt_ref[pl.program_id(1)]

        @pl.when(ki == 0)
        def _():
            m_ref[...] = jnp.full_like(m_ref, -1e30)
            l_ref[...] = jnp.zeros_like(l_ref)
            acc_ref[...] = jnp.zeros_like(acc_ref)

        def step(diagonal):
            s_ref[...] = _nt(_stack_slots(k_ref, g), _stack_slots(q_ref, g))
            for j in range(tile // strip):
                c = pl.ds(j * strip, strip)
                s = s_ref[:, :, c] * ATT_SCALE
                if diagonal:
                    key = lax.broadcasted_iota(jnp.int32, s.shape[1:], 0)
                    query = lax.broadcasted_iota(jnp.int32, s.shape[1:], 1) + j * strip
                    s = jnp.where(key <= query, s, -1e30)
                m_old = m_ref[:, :, c]
                m_new = jnp.maximum(m_old, jnp.max(s, axis=1, keepdims=True))
                p = jnp.exp(s - m_new)
                alpha = jnp.exp(m_old - m_new)
                l_ref[:, :, c] = alpha * l_ref[:, :, c] + jnp.sum(p, axis=1, keepdims=True)
                alpha_ref[:, :, c] = alpha
                m_ref[:, :, c] = m_new
                p_ref[:, :, c] = p.astype(BF16)
            acc_ref[...] = acc_ref[...] * alpha_ref[...] + _tn(_stack_slots(v_ref, g), p_ref[...])

        @pl.when(ki < qi)
        def _():
            step(False)

        @pl.when(ki == qi)
        def _():
            step(True)
            out = acc_ref[...] / l_ref[...]
            lse = jnp.broadcast_to(m_ref[...] + jnp.log(l_ref[...]), out.shape)
            for j in range(g):
                o_ref[:, pl.ds(j * SLOT, SLOT)] = out[j].T
                lse_ref[:, pl.ds(j * SLOT, SLOT)] = lse[j].T

    q_spec, k_spec = _attn_specs(tile, g)
    tables = _attn_pairs(steps, True)
    return pl.pallas_call(
        body, name="attn_fwd",
        grid_spec=pltpu.PrefetchScalarGridSpec(
            num_scalar_prefetch=2, grid=(N_HEADS // g, tables[0].shape[0]),
            in_specs=[q_spec, k_spec, k_spec], out_specs=[q_spec, q_spec],
            scratch_shapes=[pltpu.VMEM((g, 1, tile), F32), pltpu.VMEM((g, 1, tile), F32), pltpu.VMEM((g, 1, tile), F32),
                            pltpu.VMEM((g, SLOT, tile), F32), pltpu.VMEM((g, tile, tile), F32), pltpu.VMEM((g, tile, tile), BF16)]),
        out_shape=[jax.ShapeDtypeStruct((t, N_HEADS * SLOT), F32)] * 2,
        compiler_params=pltpu.CompilerParams(dimension_semantics=("parallel", "arbitrary")),
    )(*tables, q, k, v)


def _attn_grad_scores(q, k, v, do, lse_ref, delta_ref, diagonal):
    g = ATT_GROUP
    p = jnp.exp(_scores(q, k, diagonal) - _stack_slots(lse_ref, g)[:, :, 0:1])
    dp = _nt(do, v)
    return p, p * (dp - _stack_slots(delta_ref, g)[:, :, 0:1]) * ATT_SCALE


def _attn_bwd(q, k, v, do, lse, delta):
    t = q.shape[0]
    tile = min(ATT_T, t)
    steps = t // tile
    g = ATT_GROUP

    def body(qt_ref, kt_ref, q_ref, k_ref, v_ref, do_ref, lse_ref, delta_ref, dq_ref, dk_ref, dv_ref, dk_acc, dv_acc):
        qi, ki = qt_ref[pl.program_id(1)], kt_ref[pl.program_id(1)]

        @pl.when(pl.program_id(1) == 0)
        def _():
            dq_ref[...] = jnp.zeros_like(dq_ref)

        def step(diagonal):
            qq, kk = _stack_slots(q_ref, g), _stack_slots(k_ref, g)
            do_b = _stack_slots(do_ref, g).astype(BF16)
            p, ds = _attn_grad_scores(qq, kk, _stack_slots(v_ref, g), do_b, lse_ref, delta_ref, diagonal)
            ds = ds.astype(BF16)
            dv_acc[...] += _tn(p.astype(BF16), do_b)
            dk_acc[...] += _tn(ds, qq)
            dq = _nn(ds, kk)
            rows = pl.ds(pl.multiple_of(qi * tile, tile), tile)
            for j in range(g):
                dq_ref[rows, pl.ds(j * SLOT, SLOT)] += dq[j]

        @pl.when(qi == ki)
        def _():
            dk_acc[...] = jnp.zeros_like(dk_acc)
            dv_acc[...] = jnp.zeros_like(dv_acc)
            step(True)

        @pl.when(qi > ki)
        def _():
            step(False)

        @pl.when(qi == steps - 1)
        def _():
            _unstack_slots(dk_ref, dk_acc[...])
            _unstack_slots(dv_ref, dv_acc[...])

    q_spec, k_spec = _attn_specs(tile, g)
    tables = _attn_pairs(steps, False)
    return pl.pallas_call(
        body, name="attn_bwd",
        grid_spec=pltpu.PrefetchScalarGridSpec(
            num_scalar_prefetch=2, grid=(N_HEADS // g, tables[0].shape[0]),
            in_specs=[q_spec, k_spec, k_spec, q_spec, q_spec, q_spec],
            out_specs=[pl.BlockSpec((t, g * SLOT), lambda h, p, qt, kt: (0, h)), k_spec, k_spec],
            scratch_shapes=[pltpu.VMEM((g, tile, SLOT), F32), pltpu.VMEM((g, tile, SLOT), F32)]),
        out_shape=[jax.ShapeDtypeStruct((t, N_HEADS * SLOT), F32)] * 3,
        compiler_params=pltpu.CompilerParams(dimension_semantics=("parallel", "arbitrary")),
    )(*tables, q, k, v, do, lse, delta)


CONV_PAD = 8


def _fill_padded(ref, val):
    t = val.shape[0]
    zeros = jnp.zeros((CONV_PAD, val.shape[1]), val.dtype)
    ref[pl.ds(0, CONV_PAD)] = zeros
    ref[pl.ds(CONV_PAD + t, CONV_PAD)] = zeros
    ref[pl.ds(CONV_PAD, t)] = val


def _shifted(ref, s):
    return ref[pl.ds(CONV_PAD - s, ref.shape[0] - 2 * CONV_PAD)]


def _l2norm(x):
    return x * lax.rsqrt(jnp.sum(x * x, axis=-1, keepdims=True) + EPS)


def _conv_pre(x_pad, w):
    y = w[GDN_CONV - 1:GDN_CONV, :] * _shifted(x_pad, 0)
    for s in range(1, GDN_CONV):
        y = y + w[GDN_CONV - 1 - s:GDN_CONV - s, :] * _shifted(x_pad, s)
    return y


def _gdn_conv_fwd(x, w):
    t, width = x.shape

    def body(x_ref, w_ref, o_ref, x_pad):
        _fill_padded(x_pad, x_ref[...])
        act = _silu(_conv_pre(x_pad, w_ref[...]))
        normed = pl.program_id(0) < 2 * N_HEADS
        o_ref[...] = jnp.where(normed, _l2norm(act), act)

    return pl.pallas_call(
        body, name="gdn_conv_fwd",
        grid=(width // SLOT,),
        in_specs=[pl.BlockSpec((t, SLOT), lambda j: (0, j)), pl.BlockSpec((GDN_CONV, SLOT), lambda j: (0, j))],
        out_specs=pl.BlockSpec((t, SLOT), lambda j: (0, j)),
        out_shape=jax.ShapeDtypeStruct((t, width), F32),
        scratch_shapes=[pltpu.VMEM((t + 2 * CONV_PAD, SLOT), F32)],
        compiler_params=pltpu.CompilerParams(dimension_semantics=("parallel",)),
    )(x, w)


def _gdn_conv_bwd(x, w, dout):
    t, width = x.shape

    def body(x_ref, w_ref, do_ref, dx_ref, dw_ref, x_pad, dy_pad):
        wv = w_ref[...]
        _fill_padded(x_pad, x_ref[...])
        y = _conv_pre(x_pad, wv)
        sig = _sigmoid(y)
        act = y * sig
        _, pull = jax.vjp(_l2norm, act)
        normed = pl.program_id(0) < 2 * N_HEADS
        dact = jnp.where(normed, pull(do_ref[0])[0], do_ref[0])
        dy = dact * (sig * (1.0 + y * (1.0 - sig)))
        _fill_padded(dy_pad, dy)
        dx = wv[GDN_CONV - 1:GDN_CONV, :] * dy
        for s in range(1, GDN_CONV):
            dx = dx + wv[GDN_CONV - 1 - s:GDN_CONV - s, :] * _shifted(dy_pad, -s)
        dx_ref[...] = dx.astype(BF16)
        for s in range(GDN_CONV):
            dw_ref[GDN_CONV - 1 - s:GDN_CONV - s, :] = jnp.sum(dy * _shifted(x_pad, s), axis=0, keepdims=True)

    col = pl.BlockSpec((t, SLOT), lambda j: (0, j))
    tap = pl.BlockSpec((GDN_CONV, SLOT), lambda j: (0, j))
    return pl.pallas_call(
        body, name="gdn_conv_bwd",
        grid=(width // SLOT,),
        in_specs=[col, tap, pl.BlockSpec((1, t, SLOT), lambda j: (j // N_HEADS, 0, j % N_HEADS))],
        out_specs=[col, tap],
        out_shape=[jax.ShapeDtypeStruct((t, width), BF16), jax.ShapeDtypeStruct((GDN_CONV, width), F32)],
        scratch_shapes=[pltpu.VMEM((t + 2 * CONV_PAD, SLOT), F32)] * 2,
        compiler_params=pltpu.CompilerParams(dimension_semantics=("parallel",)),
    )(x, w, dout)


def _softplus(x):
    e = jnp.exp(-jnp.abs(x))
    u = 1.0 + e
    log1p = jnp.where(u == 1.0, e, jnp.log(u) * e / jnp.where(u == 1.0, 1.0, u - 1.0))
    return jnp.maximum(x, 0.0) + log1p


def _chunk_running_sum(x, reverse=False):
    tm = x.shape[0]
    at = lax.broadcasted_iota(jnp.int32, x.shape, 0) % GDN_CHUNK
    step = 1
    while step < GDN_CHUNK:
        if reverse:
            x = x + jnp.where(at < GDN_CHUNK - step, pltpu.roll(x, tm - step, 0), 0.0)
        else:
            x = x + jnp.where(at >= step, pltpu.roll(x, step, 0), 0.0)
        step *= 2
    return x


def _gates_fwd(ab, a_log, dt_bias):
    def fn(rows, consts):
        (abv,), (alog, dtb) = rows, consts
        g = _chunk_running_sum(-jnp.exp(alog) * _softplus(abv + dtb))
        beta = _sigmoid(abv)
        shape = (abv.shape[0], SLOT)
        g_slots = [jnp.broadcast_to(g[:, h:h + 1], shape) for h in range(N_HEADS)]
        b_slots = [jnp.broadcast_to(beta[:, N_HEADS + h:N_HEADS + h + 1], shape) for h in range(N_HEADS)]
        return [jnp.concatenate(g_slots, axis=1), jnp.concatenate(b_slots, axis=1)], []

    width = N_HEADS * SLOT
    return _rowwise("gdn_gates_fwd", fn, [ab], [a_log, dt_bias], [(width, F32), (width, F32)])


def _gates_bwd(ab, a_log, dt_bias, dg, dbeta):
    def fn(rows, consts):
        (abv, dgv, dbv), (alog, dtb) = rows, consts
        lane = lax.broadcasted_iota(jnp.int32, abv.shape, 1)
        dg_tok = jnp.zeros_like(abv)
        db_tok = jnp.zeros_like(abv)
        for h in range(N_HEADS):
            dg_tok = dg_tok + jnp.where(lane == h, jnp.sum(dgv[:, h * SLOT:(h + 1) * SLOT], axis=1, keepdims=True), 0.0)
            db_tok = db_tok + jnp.where(lane == N_HEADS + h, jnp.sum(dbv[:, h * SLOT:(h + 1) * SLOT], axis=1, keepdims=True), 0.0)
        dg_tok = _chunk_running_sum(dg_tok, reverse=True)
        xa = abv + dtb
        g = -jnp.exp(alog) * _softplus(xa)
        da = dg_tok * (-jnp.exp(alog)) * _sigmoid(xa)
        beta = _sigmoid(abv)
        dab = jnp.where(lane < N_HEADS, da, db_tok * beta * (1.0 - beta))
        dab = jnp.where(lane < 2 * N_HEADS, dab, 0.0)
        d_alog = jnp.sum(jnp.where(lane < N_HEADS, dg_tok * g, 0.0), axis=0, keepdims=True)
        d_dtb = jnp.sum(jnp.where(lane < N_HEADS, da, 0.0), axis=0, keepdims=True)
        return [dab], [d_alog, d_dtb]

    return _rowwise("gdn_gates_bwd", fn, [ab, dg, dbeta], [a_log, dt_bias], [(SLOT, F32)], sums=[SLOT, SLOT])


ROPE_HALF = MLA_ROPE // 2


def _rope_tables(positions):
    freqs = ROPE_THETA ** (-jnp.arange(ROPE_HALF, dtype=F32) / ROPE_HALF)
    ang = positions.astype(F32)[:, None] * freqs
    cos, sin = jnp.cos(ang), jnp.sin(ang)
    t = positions.shape[0]
    ones, zeros = jnp.ones((t, MLA_NOPE), F32), jnp.zeros((t, MLA_NOPE), F32)
    tail = jnp.zeros((t, SLOT - MLA_NOPE - MLA_ROPE), F32)
    half0 = jnp.zeros((t, ROPE_HALF), F32)
    same = jnp.concatenate([ones, cos, cos, tail], axis=1)
    from_low = jnp.concatenate([zeros, half0, sin, tail], axis=1)
    from_high = jnp.concatenate([zeros, -sin, half0, tail], axis=1)
    return same, from_low, from_high


def _rope(x, tabs):
    same, from_low, from_high = tabs
    width = x.shape[1]
    return x * same + pltpu.roll(x, ROPE_HALF, 1) * from_low + pltpu.roll(x, width - ROPE_HALF, 1) * from_high


def _rope_transposed(dy, tabs):
    same, from_low, from_high = tabs
    width = dy.shape[1]
    return dy * same + pltpu.roll(dy * from_low, width - ROPE_HALF, 1) + pltpu.roll(dy * from_high, ROPE_HALF, 1)


def _tile_slots(tab):
    return jnp.concatenate([tab] * N_HEADS, axis=1)


A_WIDTH = MLA_Q_RANK + MLA_KV_RANK + 2 * SLOT
A_KPE = MLA_Q_RANK + MLA_KV_RANK
A_AB = A_KPE + SLOT
WIDE = N_HEADS * SLOT


def _mla_front_fwd(proj_a, tabs, g_q, g_kv, w_uq, w_kv):
    def fn(rows, consts):
        pa, *tb = rows
        gq, gkv, wuq, wkv = consts
        cqn = _rms(pa[:, :MLA_Q_RANK], gq, MLA_Q_RANK).astype(BF16)
        ckvn = _rms(pa[:, MLA_Q_RANK:A_KPE], gkv, MLA_KV_RANK).astype(BF16)
        kv = _nt(ckvn, wkv)
        q = _rope(_nt(cqn, wuq), [_tile_slots(x) for x in tb])
        k = kv[:, :WIDE] + _tile_slots(_rope(pa[:, A_KPE:A_AB], tb))
        return [cqn, ckvn, q, k, kv[:, WIDE:]], []

    return _rowwise("mla_front_fwd", fn, [proj_a, *tabs], [g_q, g_kv, w_uq, w_kv],
                    [(MLA_Q_RANK, BF16), (MLA_KV_RANK, BF16)] + [(WIDE, BF16)] * 3)


def _mla_front_bwd(proj_a, tabs, g_q, g_kv, w_uq, w_kv, dq, dk, dv, dab):
    def fn(rows, consts):
        pa, t0, t1, t2, dqv, dkv, dvv, da = rows
        gq, gkv, wuq, wkv = consts
        tb = (t0, t1, t2)
        dq_p = _rope_transposed(dqv, [_tile_slots(x) for x in tb]).astype(BF16)
        dkv_p = jnp.concatenate([dkv, dvv], axis=1).astype(BF16)
        dkpe = dkv[:, :SLOT]
        for h in range(1, N_HEADS):
            dkpe = dkpe + dkv[:, h * SLOT:(h + 1) * SLOT]
        _, pull_q = jax.vjp(lambda x, g: _rms(x, g, MLA_Q_RANK), pa[:, :MLA_Q_RANK], gq)
        _, pull_kv = jax.vjp(lambda x, g: _rms(x, g, MLA_KV_RANK), pa[:, MLA_Q_RANK:A_KPE], gkv)
        dcq, dgq = pull_q(_nn(dq_p, wuq))
        dckv, dgkv = pull_kv(_nn(dkv_p, wkv))
        return [jnp.concatenate([dcq, dckv, _rope_transposed(dkpe, tb), da], axis=1), dq_p, dkv_p], [dgq, dgkv]

    return _rowwise("mla_front_bwd", fn, [proj_a, *tabs, dq, dk, dv, dab], [g_q, g_kv, w_uq, w_kv],
                    [(A_WIDTH, BF16), (WIDE, BF16), (2 * WIDE, BF16)], sums=[MLA_Q_RANK, MLA_KV_RANK])


def _slot_sum(x):
    parts = [jnp.broadcast_to(jnp.sum(x[:, h * SLOT:(h + 1) * SLOT], axis=1, keepdims=True), (x.shape[0], SLOT))
             for h in range(N_HEADS)]
    return jnp.concatenate(parts, axis=1)


def _mix_join(o_mla, o_gdn, gate, g_mla, g_gdn):
    mla = _rms(o_mla, g_mla, N_HEADS * MLA_V)
    gdn = o_gdn * lax.rsqrt(_slot_sum(o_gdn * o_gdn) * (1.0 / GDN_D) + EPS) * g_gdn * _silu(gate)
    return mla, gdn


MIX_TM = 256


def _mix_fwd(o_mla, o_gdn, gate, x, g_mla, g_gdn, w_out, g_post):
    dm = x.shape[1]

    def fn(rows, consts):
        om, og, gt, xv = rows
        gm, gg, wo, gp = consts
        cat = jnp.concatenate(_mix_join(om, og, gt, gm, gg), axis=1).astype(BF16)
        mixed = _nn(cat, wo)
        return [cat, mixed, xv + _rms(mixed, gp, dm)], []

    return _rowwise("mix_fwd", fn, [o_mla, o_gdn, gate, x], [g_mla, g_gdn, w_out, g_post],
                    [(2 * WIDE, BF16), (dm, F32), (dm, F32)], tm=MIX_TM)


def _mix_bwd(o_mla, o_gdn, gate, mixed, dy, g_mla, g_gdn, w_out, g_post):
    dm = mixed.shape[1]

    def fn(rows, consts):
        om, og, gt, mx, dyv = rows
        gm, gg, wo, gp = consts
        _, pull_post = jax.vjp(lambda hv, gv: _rms(hv, gv, dm), mx, gp)
        dmixed, dgp = pull_post(dyv)
        dmixed = dmixed.astype(BF16)
        dc = _nt(dmixed, wo)
        _, pull = jax.vjp(lambda x, g: _rms(x, g, N_HEADS * MLA_V), om, gm)
        dom, dgm = pull(dc[:, :WIDE])
        dn_out = dc[:, WIDE:]
        r = lax.rsqrt(_slot_sum(og * og) * (1.0 / GDN_D) + EPS)
        sig = _sigmoid(gt)
        normed = og * r
        dn = dn_out * gg * (gt * sig)
        dog = r * dn - normed * (r * r) * _slot_sum(dn * og) * (1.0 / GDN_D)
        dgt = dn_out * normed * gg * (sig * (1.0 + gt * (1.0 - sig)))
        dgg = jnp.sum(dn_out * normed * (gt * sig), axis=0, keepdims=True)
        return [dmixed, dom, _slot_sum(dom * om), dog, dgt], [dgp, dgm, dgg]

    return _rowwise("mix_bwd", fn, [o_mla, o_gdn, gate, mixed, dy], [g_mla, g_gdn, w_out, g_post],
                    [(dm, BF16), (WIDE, F32), (WIDE, F32), (WIDE, F32), (WIDE, BF16)], sums=[dm, WIDE, WIDE], tm=MIX_TM)


def _proj_fwd(x, g, weights):
    dm = x.shape[1]

    def fn(rows, consts):
        hn = _rms(rows[0], consts[0], dm).astype(BF16)
        return [hn] + [_nt(hn, wv) for wv in consts[1:]], []

    return _rowwise("proj_fwd", fn, [x], [g, *weights], [(dm, BF16)] + [(wv.shape[0], F32) for wv in weights], tm=MIX_TM)


def _proj_bwd(x, g, weights, cots, dy, h, g_post):
    dm = x.shape[1]
    n = len(weights)

    def fn(rows, consts):
        xv, dyv, hv, *parts = rows
        dn = _nn(parts[0], consts[2])
        for p, wv in zip(parts[1:], consts[3:]):
            dn = dn + _nn(p, wv)
        _, pull = jax.vjp(lambda a, gv: _rms(a, gv, dm), xv, consts[0])
        dx, dg = pull(dn)
        dx = dyv + dx
        _, pull = jax.vjp(lambda a: 0.5 * _rms(a, consts[1], dm), hv)
        return [dx, pull(dx)[0]], [dg]

    assert len(cots) == n
    return _rowwise("proj_bwd", fn, [x, dy, h, *cots], [g, g_post, *weights], [(dm, F32), (dm, BF16)], sums=[dm], tm=MIX_TM)


def _loss_fwd(y, target):
    dm = y.shape[1]

    def fn(rows, consts):
        err = rows[0] - rows[1]
        sq = err * err
        lanes = sq[:, :SLOT]
        for j in range(1, dm // SLOT):
            lanes = lanes + sq[:, j * SLOT:(j + 1) * SLOT]
        return [err * (1.0 / dm)], [jnp.sum(lanes, axis=0, keepdims=True) * (0.5 / dm)]

    return _rowwise("loss", fn, [y, target], [], [(dm, F32)], sums=[SLOT])


W_IN_CUTS = (0, 256, 384, 416, 1952, 1960, 1968, 2480)


def _heads_out(w, per_head, axis=-1):
    axis = axis % w.ndim
    shape = w.shape
    n = shape[axis] // per_head
    w = w.reshape(shape[:axis] + (n, per_head) + shape[axis + 1:])
    pad = [(0, 0)] * w.ndim
    pad[axis + 1] = (0, SLOT - per_head)
    return jnp.pad(w, pad).reshape(shape[:axis] + (n * SLOT,) + shape[axis + 1:])


def _heads_in(w, per_head, axis=-1):
    axis = axis % w.ndim
    shape = w.shape
    n = shape[axis] // SLOT
    w = w.reshape(shape[:axis] + (n, SLOT) + shape[axis + 1:])
    w = lax.slice_in_dim(w, 0, per_head, axis=axis + 1)
    return w.reshape(shape[:axis] + (n * per_head,) + shape[axis + 1:])


def _pad_lanes(v, lo, width=SLOT):
    return jnp.pad(v, [(0, 0)] * (v.ndim - 1) + [(lo, width - lo - v.shape[-1])])


def _pad_rows(v, lo, rows=SLOT):
    return jnp.pad(v, [(lo, rows - lo - v.shape[0])] + [(0, 0)] * (v.ndim - 1))


def _layout_weights(w):
    c = W_IN_CUTS
    w_in = w["w_in_t"]
    p = {}
    p["w_a"] = jnp.concatenate([w_in[c[0]:c[2]], _pad_rows(w_in[c[2]:c[3]], MLA_NOPE), _pad_rows(w_in[c[4]:c[6]], 0)], axis=0)
    p["w_qkv"] = _heads_out(w_in[c[3]:c[4]], GDN_D, axis=0)
    p["w_gate"] = _heads_out(w_in[c[6]:c[7]], GDN_D, axis=0)
    p["w_uq"] = _heads_out(w["uq_t"], MLA_NOPE + MLA_ROPE, axis=0)
    ukv = w["ukv_t"].reshape(N_HEADS, MLA_NOPE + MLA_V, MLA_KV_RANK)
    p["w_kv"] = jnp.concatenate([_heads_out(ukv[:, :MLA_NOPE].reshape(-1, MLA_KV_RANK), MLA_NOPE, axis=0),
                                 _heads_out(ukv[:, MLA_NOPE:].reshape(-1, MLA_KV_RANK), MLA_V, axis=0)], axis=0)
    p["conv"] = _heads_out(w["gdn_conv_w"], GDN_D)
    p["g_mla_out"] = _heads_out(w["mla_out_g"], MLA_V)
    p["g_gdn"] = jnp.tile(_pad_lanes(w["gdn_norm_g"], 0), (1, N_HEADS))
    p["a_log"] = _pad_lanes(w["gdn_a_log"], 0)
    p["dt_bias"] = _pad_lanes(w["gdn_dt_bias"], 0)
    return p


def _unlayout_grads(d):
    c = W_IN_CUTS
    g = {}
    da = d["w_a"]
    kpe0 = A_KPE + MLA_NOPE
    g["w_in_t"] = jnp.concatenate([da[:A_KPE], da[kpe0:kpe0 + MLA_ROPE], _heads_in(d["w_qkv"], GDN_D, axis=0),
                                   da[A_AB:A_AB + 2 * N_HEADS], _heads_in(d["w_gate"], GDN_D, axis=0)], axis=0)
    assert g["w_in_t"].shape[0] == c[-1]
    g["uq_t"] = _heads_in(d["w_uq"], MLA_NOPE + MLA_ROPE, axis=0)
    dk = _heads_in(d["w_kv"][:WIDE], MLA_NOPE, axis=0).reshape(N_HEADS, MLA_NOPE, MLA_KV_RANK)
    dv = _heads_in(d["w_kv"][WIDE:], MLA_V, axis=0).reshape(N_HEADS, MLA_V, MLA_KV_RANK)
    g["ukv_t"] = jnp.concatenate([dk, dv], axis=1).reshape(-1, MLA_KV_RANK)
    g["w_out"] = _heads_in(d["w_out"], GDN_D, axis=0)
    g["gdn_conv_w"] = _heads_in(d["conv"], GDN_D)
    g["mla_out_g"] = _heads_in(d["g_mla_out"], MLA_V)
    g["gdn_norm_g"] = jnp.sum(d["g_gdn"].reshape(N_HEADS, SLOT), axis=0, keepdims=True)[:, :GDN_D]
    g["gdn_a_log"] = d["a_log"][:, :N_HEADS]
    g["gdn_dt_bias"] = d["dt_bias"][:, :N_HEADS]
    return g


def _weight_grad(name, cots, acts, out_dtype=F32, tm=1024, tn=1024, tk=2048, after=None):
    return _matmul(name, cots, acts, "tn", out_dtype=out_dtype, tm=tm, tn=tn, tk=tk, after=after)


def _by_device(a):
    return a.astype(BF16).reshape((N_DEV, a.shape[0] // N_DEV) + a.shape[1:])


def _rows_of(blocks):
    return blocks.reshape((-1,) + blocks.shape[2:])


def _local_step(x, positions, target, w, mid, late):
    tabs = _rope_tables(positions)

    (h1, x1, hg1, hu1, a1), gathered = _ffn_fwd("ffn1_fwd", x, w["ffn1_pre_g"], w["ffn1"], 0, w["ffn1_post_g"], carry=mid)
    w = dict(w, w_in_t=_rows_of(gathered[0]), uq_t=_rows_of(gathered[1]), ukv_t=_rows_of(gathered[2]),
             gdn_conv_w=gathered[3].transpose(1, 0, 2).reshape(CONV_SHAPE))
    p = _layout_weights(w)
    in_weights = [p["w_a"], p["w_qkv"], p["w_gate"]]
    hn, proj_a, proj_qkv, proj_gate = _proj_fwd(x1, w["mix_pre_g"], in_weights)
    cqn, ckvn, q, k, v = _mla_front_fwd(proj_a, tabs, w["mla_q_norm_g"], w["mla_kv_norm_g"], p["w_uq"], p["w_kv"])
    o_mla, lse = _attn_fwd(q, k, v)
    ab = (proj_a, SLOT, A_AB // SLOT)
    qkv_n = _gdn_conv_fwd(proj_qkv, p["conv"])
    gb, bb = _gates_fwd(ab, p["a_log"], p["dt_bias"])
    (o_gdn, keep), (ffn2, w_out) = _gdn_fwd(qkv_n, gb, bb, carry=late)
    p["w_out"] = _heads_out(_rows_of(w_out), GDN_D, axis=0)
    cat, mixed, x2 = _mix_fwd(o_mla, o_gdn, proj_gate, x1, p["g_mla_out"], p["g_gdn"], p["w_out"], w["mix_post_g"])
    (h2, y, hg2, hu2, a2), _ = _ffn_fwd("ffn2_fwd", x2, w["ffn2_pre_g"], ffn2, 0, w["ffn2_post_g"])
    dy, loss_lanes = _loss_fwd(y, target)

    g = {}
    (dx2, xn2, dh2, dhg2, dhu2, g["ffn2_pre_g"], g["ffn2_post_g"]), _ = _ffn_bwd(
        "ffn2_bwd", x2, h2, hg2, hu2, dy, w["ffn2_pre_g"], ffn2, 0, w["ffn2_post_g"])
    ffn2_grads = _Scatter([_by_device(_weight_grad("ffn2_dw_gate", dhg2, xn2, BF16, tm=1408)),
                           _by_device(_weight_grad("ffn2_dw_up", dhu2, xn2, BF16, tm=1408)),
                           _by_device(_weight_grad("ffn2_dw_down", a2, dh2, BF16, tm=1408))])
    d = {}
    dmixed, do_mla, delta, do_gdn, dgate, g["mix_post_g"], d["g_mla_out"], d["g_gdn"] = _mix_bwd(
        o_mla, o_gdn, proj_gate, mixed, dx2, p["g_mla_out"], p["g_gdn"], p["w_out"], w["mix_post_g"])
    d["w_out"] = _weight_grad("mix_out_dw", cat, dmixed, BF16)
    dq, dk, dv = _attn_bwd(q, k, v, do_mla, lse, delta)
    (dqkv_n, dgb, dbb), landed_ffn2 = _gdn_bwd(qkv_n, gb, bb, keep, do_gdn, carry=ffn2_grads)
    dab, d["a_log"], d["dt_bias"] = _gates_bwd(ab, p["a_log"], p["dt_bias"], dgb, dbb)
    dproj_qkv, d["conv"] = _gdn_conv_bwd(proj_qkv, p["conv"], dqkv_n)
    dproj_a, dq_p, dkv_p, g["mla_q_norm_g"], g["mla_kv_norm_g"] = _mla_front_bwd(
        proj_a, tabs, w["mla_q_norm_g"], w["mla_kv_norm_g"], p["w_uq"], p["w_kv"], dq, dk, dv, dab)
    d["w_uq"] = _weight_grad("mla_q_dw", dq_p, cqn, BF16)
    d["w_kv"] = _weight_grad("mla_kv_dw", dkv_p, ckvn, BF16)
    d["w_a"] = _weight_grad("proj_a_dw", dproj_a, hn, BF16, tm=640)
    d["w_qkv"] = _weight_grad("proj_qkv_dw", dproj_qkv, hn, BF16)
    d["w_gate"] = _weight_grad("proj_gate_dw", dgate, hn, BF16)
    dx1, dh1, g["mix_pre_g"] = _proj_bwd(x1, w["mix_pre_g"], in_weights, [dproj_a, dproj_qkv, dgate], dx2,
                                        h1, w["ffn1_post_g"])
    g.update(_unlayout_grads(d))
    begun = {}
    blocks = _by_device(_weight_grad("ffn1_w_down_grad", a1, dh1, BF16, tm=1408))
    begun["ffn1_w_down"], token = _scatter_begin("scatter_ffn1_w_down_begin", blocks)
    others = list(OTHER.values())
    (dx, xn1, _, dhg1, dhu1, g["ffn1_pre_g"], g["ffn1_post_g"]), landed_others = _ffn_bwd(
        "ffn1_bwd", x, h1, hg1, hu1, dx1, w["ffn1_pre_g"], w["ffn1"], 0, w["ffn1_post_g"],
        carry=_Scatter([_by_device(g.pop(t)) for t in others]), after=token)
    landed = dict(zip(list(FFN_NAMES[3:]) + list(OTHER), list(landed_ffn2) + list(landed_others)))
    packed = _pack_small(g, g["gdn_conv_w"].reshape(-1), REDUCE_ROWS)
    packed = packed.at[REDUCE_ROWS - 1, ROW - 1].set(jnp.sum(loss_lanes))
    begun["small"], token = _scatter_begin("reduce_small_begin", jnp.broadcast_to(packed, (N_DEV,) + packed.shape))
    for name, cots, acts in (("ffn1_w_gate", dhg1, xn1), ("ffn1_w_up", dhu1, xn1)):
        blocks = _by_device(_weight_grad(name + "_grad", cots, acts, BF16, tm=1408, after=token))
        begun[name], token = _scatter_begin("scatter_" + name + "_begin", blocks)
    return dx, g, landed, begun, token


MESH_AXES = ("x", "y", "c")
N_LINKS = N_DEV - 1


def _place():
    return tuple(lax.axis_index(a) for a in MESH_AXES)


def _block_of(dev):
    x, y, c = dev
    return 4 * x + 2 * y + c


def _remote_copy(src, dst, sems, k, to):
    send_sems, recv_sems = sems
    return pltpu.make_async_remote_copy(src_ref=src, dst_ref=dst, send_sem=send_sems.at[k], recv_sem=recv_sems.at[k],
                                        device_id=to, device_id_type=pl.DeviceIdType.MESH)


class _Exchange:
    def __init__(self, arrays):
        self.arrays = list(arrays)
        self.n = len(self.arrays)
        self.specs = [pl.BlockSpec(memory_space=pl.ANY)] * self.n
        self.scratch = [pltpu.SemaphoreType.DMA((self.n * N_LINKS,)), pltpu.SemaphoreType.DMA((self.n * N_LINKS,)),
                        pltpu.SemaphoreType.DMA((self.n,))]

    def split(self, refs):
        n = self.n
        return refs[:n], refs[n:2 * n], (refs[2 * n], refs[2 * n + 1]), refs[2 * n + 2]


class _Gather(_Exchange):
    def out_shape(self):
        return [jax.ShapeDtypeStruct((N_DEV,) + a.shape, a.dtype) for a in self.arrays]

    def _plan(self, ins, outs, sems, local_sems):
        x, y, c = _place()
        me, sibling = (x, y, c), (x, y, 1 - c)
        chips = [(1 - x, y), (x, 1 - y), (1 - x, 1 - y)]

        def copy(a, k, block, to, mine=False):
            src = ins[a] if mine else outs[a].at[_block_of(block)]
            return _remote_copy(src, outs[a].at[_block_of(block)], sems, a * N_LINKS + k, to)

        local = [pltpu.make_async_copy(ins[a], outs[a].at[_block_of(me)], local_sems.at[a]) for a in range(self.n)]
        first = []
        for a in range(self.n):
            first.append(copy(a, 0, me, sibling, mine=True))
            first += [copy(a, 1 + j, me, (*chip, c), mine=True) for j, chip in enumerate(chips)]
        return me, sibling, chips, c, copy, local, first

    def start(self, ins, outs, sems, local_sems):
        *_, local, first = self._plan(ins, outs, sems, local_sems)
        for cp in local + first:
            cp.start()

    def finish(self, ins, outs, sems, local_sems):
        me, sibling, chips, c, copy, local, first = self._plan(ins, outs, sems, local_sems)
        passed = []
        for j, chip in enumerate(chips):
            for a in range(self.n):
                copy(a, 1 + j, (*chip, c), me).wait_recv()
                passed.append(copy(a, 4 + j, (*chip, c), sibling))
                passed[-1].start()
        for a in range(self.n):
            copy(a, 0, sibling, me).wait_recv()
            for j, chip in enumerate(chips):
                copy(a, 4 + j, (*chip, 1 - c), me).wait_recv()
        for cp in first + passed:
            cp.wait_send()
        for cp in local:
            cp.wait()


class _Scatter(_Exchange):
    def out_shape(self):
        return [jax.ShapeDtypeStruct(a.shape, a.dtype) for a in self.arrays]

    def _plan(self, ins, outs, sems, local_sems):
        x, y, c = _place()
        me = _block_of((x, y, c))

        def peer(r):
            return (1 - x if r & 4 else x, 1 - y if r & 2 else y, 1 - c if r & 1 else c)

        local = [pltpu.make_async_copy(ins[a].at[me], outs[a].at[me], local_sems.at[a]) for a in range(self.n)]
        sends = [_remote_copy(ins[a].at[_block_of(peer(r))], outs[a].at[me], sems, a * N_LINKS + r - 1, peer(r))
                 for a in range(self.n) for r in range(1, N_DEV)]
        arrivals = [_remote_copy(ins[a].at[me], outs[a].at[_block_of(peer(r))], sems, a * N_LINKS + r - 1, peer(r))
                    for a in range(self.n) for r in range(1, N_DEV)]
        return local, sends, arrivals

    def start(self, ins, outs, sems, local_sems):
        local, sends, _ = self._plan(ins, outs, sems, local_sems)
        for cp in local + sends:
            cp.start()

    def finish(self, ins, outs, sems, local_sems):
        local, sends, arrivals = self._plan(ins, outs, sems, local_sems)
        for cp in arrivals:
            cp.wait_recv()
        for cp in sends:
            cp.wait_send()
        for cp in local:
            cp.wait()


def _exchange(name, plan):
    def body(*refs):
        parts = plan.split(refs)
        plan.start(*parts)
        plan.finish(*parts)

    return pl.pallas_call(
        body, name=name,
        in_specs=plan.specs,
        out_specs=plan.specs,
        out_shape=plan.out_shape(),
        scratch_shapes=plan.scratch,
    )(*plan.arrays)


def _call_carrying(body, plan, operands, *, name, grid, in_specs, out_specs, out_shape, scratch_shapes, compiler_params):
    if plan is None:
        outs = pl.pallas_call(body, name=name, grid=grid, in_specs=in_specs, out_specs=out_specs, out_shape=out_shape,
                              scratch_shapes=scratch_shapes, compiler_params=compiler_params)(*operands)
        return outs, []
    n_i, n_o, n_s, k = len(in_specs), len(out_specs), len(scratch_shapes), plan.n

    def whole(*refs):
        cut = [n_i, n_i + k, n_i + k + n_o, n_i + 2 * k + n_o, n_i + 2 * k + n_o + n_s]
        own_in, ex_in, own_out, ex_out, own_scr, ex_scr = (refs[a:b] for a, b in zip([0] + cut, cut + [len(refs)]))
        parts = plan.split(ex_in + ex_out + ex_scr)
        first = last = True
        for axis, size in enumerate(grid):
            first = first & (pl.program_id(axis) == 0)
            last = last & (pl.program_id(axis) == size - 1)

        @pl.when(first)
        def _():
            plan.start(*parts)

        body(*own_in, *own_out, *own_scr)

        @pl.when(last)
        def _():
            plan.finish(*parts)

    outs = pl.pallas_call(
        whole, name=name, grid=grid,
        in_specs=list(in_specs) + plan.specs, out_specs=list(out_specs) + plan.specs,
        out_shape=list(out_shape) + plan.out_shape(), scratch_shapes=list(scratch_shapes) + plan.scratch,
        compiler_params=compiler_params,
    )(*operands, *plan.arrays)
    return outs[:n_o], outs[n_o:]


def _row_tile(rows, target=256):
    best = rows
    for cand in range(16, min(rows, target) + 1, 16):
        if rows % cand == 0:
            best = cand
    return best


def _sum_blocks(name, blocks, after=None):
    rows, width = blocks.shape[-2:]
    tm = _row_tile(rows)

    def body(x_ref, *rest):
        acc = x_ref[0].astype(F32)
        for d in range(1, N_DEV):
            acc = acc + x_ref[d].astype(F32)
        rest[-1][...] = acc

    ordered = [] if after is None else [after]
    return pl.pallas_call(
        body, name=name,
        grid=(rows // tm,),
        in_specs=[pl.BlockSpec((N_DEV, tm, width), lambda i: (0, i, 0))] + [pl.BlockSpec(memory_space=pl.ANY)] * len(ordered),
        out_specs=pl.BlockSpec((tm, width), lambda i: (i, 0)),
        out_shape=jax.ShapeDtypeStruct((rows, width), F32),
        compiler_params=pltpu.CompilerParams(dimension_semantics=("parallel",)),
    )(blocks, *ordered)


def _split_plan(src_ref, land_ref, sems):
    x, y, c = _place()
    me = _block_of((x, y, c))

    def peer(r):
        return (1 - x if r & 4 else x, 1 - y if r & 2 else y, 1 - c if r & 1 else c)

    sends = [_remote_copy(src_ref.at[_block_of(peer(r))], land_ref.at[me], sems, r - 1, peer(r)) for r in range(1, N_DEV)]
    arrivals = [_remote_copy(src_ref.at[me], land_ref.at[_block_of(peer(r))], sems, r - 1, peer(r)) for r in range(1, N_DEV)]
    return sends, arrivals


def _scatter_begin(name, blocks):
    def body(src_ref, land_ref, send_sems, recv_sems, src_thru, land_thru, token_ref):
        for cp in _split_plan(src_ref, land_ref, (send_sems, recv_sems))[0]:
            cp.start()
        token_ref[...] = jnp.zeros_like(token_ref)

    hbm, sem = pl.BlockSpec(memory_space=pltpu.HBM), pl.BlockSpec(memory_space=pltpu.SEMAPHORE)
    zone = pltpu.HBM(blocks.shape, blocks.dtype)
    *handles, token = pl.pallas_call(
        body, name=name,
        in_specs=(hbm, hbm),
        out_specs=(sem, sem, hbm, hbm, pl.BlockSpec(memory_space=pltpu.VMEM)),
        out_shape=(pltpu.SemaphoreType.DMA((N_LINKS,)), pltpu.SemaphoreType.DMA((N_LINKS,)), zone, zone,
                   jax.ShapeDtypeStruct((8, SLOT), F32)),
        input_output_aliases={0: 2, 1: 3},
        compiler_params=pltpu.CompilerParams(has_side_effects=pltpu.SideEffectType.DATAFLOW_SIDE_EFFECTING),
    )(pltpu.with_memory_space_constraint(blocks, pltpu.HBM),
      pltpu.with_memory_space_constraint(lax.empty(blocks.shape, blocks.dtype), pltpu.HBM))
    return handles, token


def _scatter_end(name, handles, after):
    send_sems, recv_sems, src, zone = handles

    def body(src_ref, land_ref, send_sems, recv_sems, after_ref, src_dead, got_ref):
        sends, arrivals = _split_plan(src_ref, land_ref, (send_sems, recv_sems))
        for cp in arrivals:
            cp.wait_recv()
        for cp in sends:
            cp.wait_send()

    hbm, sem = pl.BlockSpec(memory_space=pltpu.HBM), pl.BlockSpec(memory_space=pltpu.SEMAPHORE)
    sent, landed = pl.pallas_call(
        body, name=name,
        in_specs=(hbm, hbm, sem, sem, pl.BlockSpec(memory_space=pl.ANY)),
        out_specs=(hbm, hbm),
        out_shape=(pltpu.HBM(src.shape, src.dtype), pltpu.HBM(zone.shape, zone.dtype)),
        input_output_aliases={0: 0, 1: 1},
        compiler_params=pltpu.CompilerParams(has_side_effects=pltpu.SideEffectType.DATAFLOW_SIDE_EFFECTING),
    )(src, zone, send_sems, recv_sems, after)
    me = _block_of(_place())
    return lax.dynamic_update_slice_in_dim(landed, lax.dynamic_slice_in_dim(sent, me, 1, axis=0), me, axis=0)


def _adamw_values(wv, gv, mv, vv):
    m2 = ADAM_B1 * mv + (1.0 - ADAM_B1) * gv
    v2 = ADAM_B2 * vv + (1.0 - ADAM_B2) * jnp.square(gv)
    m_hat = m2 / (1.0 - ADAM_B1 ** ADAM_STEP)
    v_hat = v2 / (1.0 - ADAM_B2 ** ADAM_STEP)
    return [-ADAM_LR * (m_hat / (jnp.sqrt(v_hat) + ADAM_EPS) + ADAM_WD * wv), m2, v2]


def _adamw(name, w, g, m, v):
    def fn(rows, consts):
        return _adamw_values(*rows), []

    return _rowwise(name, fn, [w, g, m, v], [], [(w.shape[1], F32)] * 3, tm=_row_tile(w.shape[0]))


def _sum_adamw(name, blocks, w, m, v, after=None):
    rows, width = w.shape
    tm = _row_tile(rows)

    def body(x_ref, w_ref, m_ref, v_ref, *rest):
        acc = x_ref[0].astype(F32)
        for d in range(1, N_DEV):
            acc = acc + x_ref[d].astype(F32)
        rest[-4][...] = acc
        for ref, val in zip(rest[-3:], _adamw_values(w_ref[...], acc, m_ref[...], v_ref[...])):
            ref[...] = val

    ordered = [] if after is None else [after]
    tile = pl.BlockSpec((tm, width), lambda i: (i, 0))
    return pl.pallas_call(
        body, name=name,
        grid=(rows // tm,),
        in_specs=[pl.BlockSpec((N_DEV, tm, width), lambda i: (0, i, 0))] + [tile] * 3 + [pl.BlockSpec(memory_space=pl.ANY)] * len(ordered),
        out_specs=[tile] * 4,
        out_shape=[jax.ShapeDtypeStruct((rows, width), F32)] * 4,
        compiler_params=pltpu.CompilerParams(dimension_semantics=("parallel",)),
    )(blocks, w, m, v, *ordered)


ROW = 1024
FFN_NAMES = ("ffn1_w_gate", "ffn1_w_up", "ffn1_w_down", "ffn2_w_gate", "ffn2_w_up", "ffn2_w_down")
OTHER = {"w_in": "w_in_t", "mla_w_uq": "uq_t", "mla_w_ukv": "ukv_t", "w_out": "w_out"}
BY_COLUMNS = ("ffn1_w_gate", "ffn1_w_up", "ffn2_w_gate", "ffn2_w_up", "w_in", "mla_w_uq", "mla_w_ukv")
SMALL = {
    "ffn1_pre_g": (1024, 1024), "ffn1_post_g": (1024, 1024), "mix_pre_g": (1024, 1024), "mla_q_norm_g": (256, 256),
    "mla_kv_norm_g": (128, 128), "mla_out_g": (512, 512), "gdn_a_log": (8, 128), "gdn_dt_bias": (8, 128),
    "gdn_norm_g": (64, 128), "mix_post_g": (1024, 1024), "ffn2_pre_g": (1024, 1024), "ffn2_post_g": (1024, 1024),
}
CONV_SHAPE = (GDN_CONV, 3 * N_HEADS * GDN_D)
CONV_SHARD = (GDN_CONV, CONV_SHAPE[1] // N_DEV)
CONV_LANES = CONV_SHAPE[0] * CONV_SHAPE[1]
SMALL_ROWS = 8
REDUCE_ROWS = 16


def _pack_small(vecs, conv, rows):
    parts = [_pad_lanes(vecs[n].reshape(1, -1), 0, r) for n, (_, r) in SMALL.items()]
    parts.append(conv.reshape(1, -1))
    flat = jnp.concatenate(parts, axis=1)
    return _pad_lanes(flat, 0, rows * ROW).reshape(rows, ROW)


def _unpack_small(buf):
    flat = buf.reshape(1, -1)
    out, at = {}, 0
    for n, (w, r) in SMALL.items():
        out[n] = flat[:, at:at + w]
        at += r
    return out, flat[0, at:]


def kernel(x, positions, ffn1_pre_g, ffn1_w_gate, ffn1_w_up, ffn1_w_down, ffn1_post_g, mix_pre_g, w_in, mla_q_norm_g, mla_w_uq, mla_kv_norm_g, mla_w_ukv, mla_out_g, gdn_conv_w, gdn_a_log, gdn_dt_bias, gdn_norm_g, w_out, mix_post_g, ffn2_pre_g, ffn2_w_gate, ffn2_w_up, ffn2_w_down, ffn2_post_g, loss_target, m_ffn1_pre_g, m_ffn1_w_gate, m_ffn1_w_up, m_ffn1_w_down, m_ffn1_post_g, m_mix_pre_g, m_w_in, m_mla_q_norm_g, m_mla_w_uq, m_mla_kv_norm_g, m_mla_w_ukv, m_mla_out_g, m_gdn_conv_w, m_gdn_a_log, m_gdn_dt_bias, m_gdn_norm_g, m_w_out, m_mix_post_g, m_ffn2_pre_g, m_ffn2_w_gate, m_ffn2_w_up, m_ffn2_w_down, m_ffn2_post_g, v_ffn1_pre_g, v_ffn1_w_gate, v_ffn1_w_up, v_ffn1_w_down, v_ffn1_post_g, v_mix_pre_g, v_w_in, v_mla_q_norm_g, v_mla_w_uq, v_mla_kv_norm_g, v_mla_w_ukv, v_mla_out_g, v_gdn_conv_w, v_gdn_a_log, v_gdn_dt_bias, v_gdn_norm_g, v_w_out, v_mix_post_g, v_ffn2_pre_g, v_ffn2_w_gate, v_ffn2_w_up, v_ffn2_w_down, v_ffn2_post_g):
    given = dict(locals())
    order = ["ffn1_pre_g", "ffn1_w_gate", "ffn1_w_up", "ffn1_w_down", "ffn1_post_g", "mix_pre_g", "w_in", "mla_q_norm_g",
             "mla_w_uq", "mla_kv_norm_g", "mla_w_ukv", "mla_out_g", "gdn_conv_w", "gdn_a_log", "gdn_dt_bias", "gdn_norm_g",
             "w_out", "mix_post_g", "ffn2_pre_g", "ffn2_w_gate", "ffn2_w_up", "ffn2_w_down", "ffn2_post_g"]
    assert sorted(order) == sorted(list(FFN_NAMES) + list(OTHER) + list(SMALL) + ["gdn_conv_w"])

    def drop_depth(a):
        return a[0] if a.ndim == 3 else a

    wts = {n: drop_depth(given[n]) for n in order}
    mom = {n: drop_depth(given["m_" + n]) for n in order}
    var = {n: drop_depth(given["v_" + n]) for n in order}
    me = _block_of(_place())

    def wire(n):
        return (wts[n].T if n in BY_COLUMNS else wts[n]).astype(BF16)

    (ffn1,) = _exchange("gather_first", _Gather([jnp.stack([wire(n) for n in FFN_NAMES[:3]])]))
    mid = _Gather([wire(n) for n in ("w_in", "mla_w_uq", "mla_w_ukv")] + [wts["gdn_conv_w"]])
    late = _Gather([jnp.stack([wire(n) for n in FFN_NAMES[3:]]), wire("w_out")])
    full = {n: wts[n] for n in SMALL}
    full["ffn1"] = ffn1

    dx, grads, landed, begun, token = _local_step(x[0], positions[0], loss_target[0], full, mid, late)

    grad, outs = {}, {"delta": {}, "new_m": {}, "new_v": {}}

    def finish(n, blocks, after=None):
        flip = n in BY_COLUMNS and wts[n].shape[1] % SLOT != 0
        turn = (lambda a: a.T) if flip else (lambda a: a)
        if n in BY_COLUMNS and not flip:
            grad[n] = _sum_blocks("sum_" + n, blocks, after=after).T
            new = _adamw("adamw_" + n, wts[n], grad[n], mom[n], var[n])
        else:
            total, *new = _sum_adamw("update_" + n, blocks, turn(wts[n]), turn(mom[n]), turn(var[n]), after=after)
            grad[n] = turn(total)
        outs["delta"][n], outs["new_m"][n], outs["new_v"][n] = (turn(a) for a in new)
        return new[2]

    for n, blocks in landed.items():
        token = finish(n, blocks, after=token)
    small_handles = begun.pop("small")
    for n, handles in begun.items():
        token = finish(n, _scatter_end("scatter_" + n + "_end", handles, after=token))

    small_sum = _sum_blocks("sum_small", _scatter_end("reduce_small_end", small_handles, after=token))
    loss = small_sum[REDUCE_ROWS - 1, ROW - 1]
    small_grad, conv_grad_full = _unpack_small(small_sum)
    grad.update(small_grad)
    grad["gdn_conv_w"] = lax.dynamic_slice(conv_grad_full[:CONV_LANES].reshape(CONV_SHAPE), (0, me * CONV_SHARD[1]), CONV_SHARD)
    outs["grad"] = grad
    small = [_pack_small(s, s["gdn_conv_w"].reshape(-1), SMALL_ROWS) for s in (wts, grad, mom, var)]
    for kind, s in zip(("delta", "new_m", "new_v"), _adamw("adamw_small", *small)):
        vecs, conv = _unpack_small(s)
        outs[kind].update(vecs)
        outs[kind]["gdn_conv_w"] = conv[:CONV_SHARD[0] * CONV_SHARD[1]].reshape(CONV_SHARD)
    result = [loss, dx[None]]
    for kind in ("grad", "delta", "new_m", "new_v"):
        result += [outs[kind][n].reshape(given[n].shape) for n in order]
    return tuple(result)
```

```python
import jax
import jax.numpy as jnp
from jax import lax
from jax.experimental import pallas as pl
from jax.experimental.pallas import tpu as pltpu

F32 = jnp.float32
BF16 = jnp.bfloat16
HI = lax.Precision.HIGH

N_DEV = 8
N_HEADS = 8
SLOT = 128
MLA_Q_RANK = 256
MLA_KV_RANK = 128
MLA_NOPE = 64
MLA_ROPE = 32
MLA_V = 64
GDN_D = 64
GDN_CONV = 4
GDN_CHUNK = 64
ROPE_THETA = 10000.0
EPS = 1e-6
ADAM_LR, ADAM_B1, ADAM_B2, ADAM_EPS, ADAM_WD, ADAM_STEP = 0.001, 0.9, 0.999, 1e-08, 0.01, 10


def _dot(a, b, ca, cb, precision=None):
    lead = a.ndim - 2
    batch = tuple(range(lead))
    return lax.dot_general(a, b, (((lead + ca,), (lead + cb,)), (batch, batch)), precision=precision,
                           preferred_element_type=F32)


def _nn(a, b, precision=None):
    return _dot(a, b, 1, 0, precision)


def _nt(a, b, precision=None):
    return _dot(a, b, 1, 1, precision)


def _tn(a, b, precision=None):
    return _dot(a, b, 0, 0, precision)


def _sigmoid(x):
    return 1.0 / (1.0 + jnp.exp(-x))


def _silu(x):
    return x * _sigmoid(x)


def _rms(x, g, n):
    ms = jnp.sum(x * x, axis=-1, keepdims=True) * (1.0 / n)
    return x * lax.rsqrt(ms + EPS) * g


def _chunk_masks():
    c = GDN_CHUNK
    i = lax.broadcasted_iota(jnp.int32, (c, c), 0)
    j = lax.broadcasted_iota(jnp.int32, (c, c), 1)
    lower = i >= j
    strict = i > j
    eye = (i == j).astype(F32)
    blocks = []
    b = 1
    while b < c:
        same = (i // (2 * b)) == (j // (2 * b))
        blocks.append(same & ((i % (2 * b)) >= b) & ((j % (2 * b)) < b))
        b *= 2
    return lower, strict, eye, blocks


def _unit_lower_inverse(low, eye, blocks):
    t = eye - jnp.where(blocks[0], low, 0.0)
    for m in blocks[1:]:
        lo = jnp.where(m, low, 0.0)
        t = t - _nn(t, _nn(lo, t, HI), HI)
    return t


@jax.custom_vjp
def _known_inverse(low, tinv):
    return tinv


def _known_inverse_fwd(low, tinv):
    return tinv, tinv


def _known_inverse_bwd(tinv, dt):
    return -_tn(tinv, _nt(dt, tinv, HI), HI), jnp.zeros_like(tinv)


_known_inverse.defvjp(_known_inverse_fwd, _known_inverse_bwd)

_PRODUCTS = {"nn": _nn, "nt": _nt, "tn": _tn}


@jax.custom_vjp
def _known_nn(a, b, c):
    return c


@jax.custom_vjp
def _known_nt(a, b, c):
    return c


@jax.custom_vjp
def _known_tn(a, b, c):
    return c


def _known_fwd(a, b, c):
    return c, (a, b, c)


_known_nn.defvjp(_known_fwd, lambda r, dc: (_nt(dc, r[1], HI), _tn(r[0], dc, HI), jnp.zeros_like(r[2])))
_known_nt.defvjp(_known_fwd, lambda r, dc: (_nn(dc, r[1], HI), _tn(dc, r[0], HI), jnp.zeros_like(r[2])))
_known_tn.defvjp(_known_fwd, lambda r, dc: (_nt(r[1], dc, HI), _nn(r[0], dc, HI), jnp.zeros_like(r[2])))
_KNOWN = {"nn": _known_nn, "nt": _known_nt, "tn": _known_tn}
GDN_PRODUCTS = 8
GDN_KEPT = 2 + GDN_PRODUCTS


def _gdn_chunk(q, k, v, gc, bb, s, masks, known=None):
    lower, strict, eye, blocks = masks
    made = []

    def product(kind, a, b):
        c = _PRODUCTS[kind](a, b, HI) if known is None else _KNOWN[kind](a, b, known[1 + len(made)])
        made.append(c)
        return c

    qs = q * (GDN_D ** -0.5)
    gct = jnp.swapaxes(gc, -1, -2)
    decay = jnp.exp(jnp.where(lower, gc - gct, -1e30))
    kb = k * bb
    low = jnp.where(strict, product("nt", kb, k) * decay, 0.0)
    tinv = _unit_lower_inverse(low, eye, blocks) if known is None else _known_inverse(low, known[0])
    eg = jnp.exp(gc)
    w = product("nn", tinv, kb * eg)
    u = product("nn", tinv, v * bb)
    attn = product("nt", qs, k) * decay
    last = lax.broadcasted_iota(jnp.int32, gc.shape[-2:], 0) == GDN_CHUNK - 1
    g_end = jnp.sum(jnp.where(last, gc, 0.0), axis=-2, keepdims=True)
    k_dec = k * jnp.exp(g_end - gc)
    v_new = u - product("nn", w, s)
    o = product("nn", qs * eg, s) + product("nn", attn, v_new)
    s_new = s * jnp.exp(g_end) + product("tn", k_dec, v_new)
    assert len(made) == GDN_PRODUCTS
    return o, s_new, [tinv] + made


GDN_GROUP = 8
GDN_GROUPS = N_HEADS // GDN_GROUP


def _group_heads(ref):
    return jnp.stack([ref[:, pl.ds(j * SLOT, GDN_D)] for j in range(GDN_GROUP)])


def _ungroup_heads(ref, val):
    pad = jnp.zeros((GDN_CHUNK, SLOT - GDN_D), F32)
    for j in range(GDN_GROUP):
        ref[:, pl.ds(j * SLOT, GDN_D)] = val[j]
        ref[:, pl.ds(j * SLOT + GDN_D, SLOT - GDN_D)] = pad


def _gdn_fwd(qkv, gb, bb, carry=None):
    t = qkv.shape[0]
    n_chunks = t // GDN_CHUNK
    d = GDN_D

    def body(q_ref, k_ref, v_ref, g_ref, b_ref, o_ref, keep_ref, s_ref):
        @pl.when(pl.program_id(1) == 0)
        def _():
            s_ref[...] = jnp.zeros_like(s_ref)

        s = s_ref[...]
        keep_ref[:, 0, 0] = s
        o, s_new, made = _gdn_chunk(*[_group_heads(r) for r in (q_ref, k_ref, v_ref, g_ref, b_ref)], s, _chunk_masks())
        for i, val in enumerate(made):
            keep_ref[:, 0, 1 + i] = val
        s_ref[...] = s_new
        _ungroup_heads(o_ref, o)

    def spec(kind=0):
        return pl.BlockSpec((GDN_CHUNK, GDN_GROUP * SLOT), lambda h, n: (n, kind * GDN_GROUPS + h))

    return _call_carrying(
        body, carry, (qkv, qkv, qkv, gb, bb), name="gdn_fwd",
        grid=(GDN_GROUPS, n_chunks),
        in_specs=[spec(0), spec(1), spec(2), spec(), spec()],
        out_specs=[spec(), pl.BlockSpec((GDN_GROUP, 1, GDN_KEPT, d, d), lambda h, n: (h, n, 0, 0, 0))],
        out_shape=[jax.ShapeDtypeStruct((t, N_HEADS * SLOT), F32), jax.ShapeDtypeStruct((N_HEADS, n_chunks, GDN_KEPT, d, d), F32)],
        scratch_shapes=[pltpu.VMEM((GDN_GROUP, d, d), F32)],
        compiler_params=pltpu.CompilerParams(dimension_semantics=("arbitrary", "arbitrary")),
    )


def _gdn_bwd(qkv, gb, bb, keep, do, carry=None):
    t = qkv.shape[0]
    n_chunks = t // GDN_CHUNK
    d = GDN_D

    def body(q_ref, k_ref, v_ref, g_ref, b_ref, keep_ref, do_ref, dqkv_ref, dg_ref, db_ref, ds_ref):
        @pl.when(pl.program_id(1) == 0)
        def _():
            ds_ref[...] = jnp.zeros_like(ds_ref)

        masks = _chunk_masks()
        known = [keep_ref[:, 0, 1 + i] for i in range(GDN_KEPT - 1)]
        _, pull = jax.vjp(lambda *a: _gdn_chunk(*a, masks, known)[:2],
                          *[_group_heads(r) for r in (q_ref, k_ref, v_ref, g_ref, b_ref)], keep_ref[:, 0, 0])
        dq, dk, dv, dg, db, ds = pull((_group_heads(do_ref), ds_ref[...]))
        ds_ref[...] = ds
        for i, val in enumerate((dq, dk, dv)):
            _ungroup_heads(dqkv_ref.at[i], val)
        _ungroup_heads(dg_ref, dg)
        _ungroup_heads(db_ref, db)

    def spec(kind=0):
        return pl.BlockSpec((GDN_CHUNK, GDN_GROUP * SLOT), lambda h, n: (n_chunks - 1 - n, kind * GDN_GROUPS + h))

    return _call_carrying(
        body, carry, (qkv, qkv, qkv, gb, bb, keep, do), name="gdn_bwd",
        grid=(GDN_GROUPS, n_chunks),
        in_specs=[spec(0), spec(1), spec(2), spec(), spec(),
                  pl.BlockSpec((GDN_GROUP, 1, GDN_KEPT, d, d), lambda h, n: (h, n_chunks - 1 - n, 0, 0, 0)), spec()],
        out_specs=[pl.BlockSpec((3, GDN_CHUNK, GDN_GROUP * SLOT), lambda h, n: (0, n_chunks - 1 - n, h)), spec(), spec()],
        out_shape=[jax.ShapeDtypeStruct((3, t, N_HEADS * SLOT), F32)] + [jax.ShapeDtypeStruct((t, N_HEADS * SLOT), F32)] * 2,
        scratch_shapes=[pltpu.VMEM((GDN_GROUP, d, d), F32)],
        compiler_params=pltpu.CompilerParams(dimension_semantics=("arbitrary", "arbitrary")),
    )


def _rowwise(name, fn, rows, consts, outs, sums=(), tm=512):
    rows = [x if isinstance(x, tuple) else (x, x.shape[1], 0) for x in rows]
    t = rows[0][0].shape[0]
    tm = min(tm, t)
    steps = t // tm
    n_r, n_c, n_o, n_s = len(rows), len(consts), len(outs), len(sums)

    def window(width, block):
        return pl.BlockSpec((tm, width), lambda i: (i, block))

    def body(*refs):
        r, c = refs[:n_r], refs[n_r:n_r + n_c]
        o, s = refs[n_r + n_c:n_r + n_c + n_o], refs[n_r + n_c + n_o:]
        vals, tot = fn([x[...] for x in r], [x[...] for x in c])
        for ref, val in zip(o, vals):
            ref[...] = val.astype(ref.dtype)
        if n_s:
            @pl.when(pl.program_id(0) == 0)
            def _():
                for ref in s:
                    ref[...] = jnp.zeros_like(ref)

            for ref, val in zip(s, tot):
                ref[...] += val

    return pl.pallas_call(
        body, name=name,
        grid=(steps,),
        in_specs=[window(w, b) for _, w, b in rows] + [pl.BlockSpec(x.shape, lambda i: (0, 0)) for x in consts],
        out_specs=[pl.BlockSpec((tm, w), lambda i: (i, 0)) for w, _ in outs]
        + [pl.BlockSpec((1, w), lambda i: (0, 0)) for w in sums],
        out_shape=[jax.ShapeDtypeStruct((t, w), dt) for w, dt in outs]
        + [jax.ShapeDtypeStruct((1, w), F32) for w in sums],
        compiler_params=pltpu.CompilerParams(dimension_semantics=("arbitrary",)),
    )(*[x for x, _, _ in rows], *consts)


def _tile(dim, target):
    if dim <= target:
        return dim
    best = None
    for cand in range(128, target + 1, 128):
        if dim % cand == 0:
            best = cand
    assert best is not None, (dim, target)
    return best


def _matmul(name, a, b, mode, out_dtype=F32, tm=1024, tn=1024, tk=2048, after=None):
    if mode == "nn":
        (m, k), n = a.shape, b.shape[1]
    elif mode == "nt":
        (m, k), n = a.shape, b.shape[0]
    else:
        (k, m), n = a.shape, b.shape[1]
    tm, tn, tk = _tile(m, tm), _tile(n, tn), _tile(k, tk)
    k_steps = k // tk
    product = {"nn": _nn, "nt": _nt, "tn": _tn}[mode]

    def body(a_ref, b_ref, *rest):
        o_ref, acc_ref = rest[-2:]
        part = product(a_ref[...].astype(BF16), b_ref[...].astype(BF16))
        if k_steps == 1:
            o_ref[...] = part.astype(o_ref.dtype)
        else:
            kk = pl.program_id(2)

            @pl.when(kk == 0)
            def _():
                acc_ref[...] = part

            @pl.when(kk > 0)
            def _():
                acc_ref[...] += part

            @pl.when(kk == k_steps - 1)
            def _():
                o_ref[...] = acc_ref[...].astype(o_ref.dtype)

    a_spec = pl.BlockSpec((tk, tm), lambda i, j, kk: (kk, i)) if mode == "tn" else pl.BlockSpec((tm, tk), lambda i, j, kk: (i, kk))
    b_spec = pl.BlockSpec((tn, tk), lambda i, j, kk: (j, kk)) if mode == "nt" else pl.BlockSpec((tk, tn), lambda i, j, kk: (kk, j))
    ordered = [] if after is None else [after]
    return pl.pallas_call(
        body, name=name,
        grid=(m // tm, n // tn, k_steps),
        in_specs=[a_spec, b_spec] + [pl.BlockSpec(memory_space=pl.ANY)] * len(ordered),
        out_specs=pl.BlockSpec((tm, tn), lambda i, j, kk: (i, j)),
        out_shape=jax.ShapeDtypeStruct((m, n), out_dtype),
        scratch_shapes=[pltpu.VMEM((tm, tn) if k_steps > 1 else (8, 128), F32)],
        compiler_params=pltpu.CompilerParams(dimension_semantics=("parallel", "parallel", "arbitrary")),
    )(a, b, *ordered)


FFN_TM = 512
FFN_BWD_TM = 256
FFN_BLOCKS = 4
FFN_GATE, FFN_UP, FFN_DOWN = 0, 1, 2


def _ffn_weight_specs(ffn_w, first):
    _, _, rows, dm = ffn_w.shape

    def spec(k):
        return pl.BlockSpec((FFN_BLOCKS, None, rows, dm), lambda i, j: (j, first + k, 0, 0))

    return [spec(FFN_GATE), spec(FFN_UP), spec(FFN_DOWN)], FFN_BLOCKS * rows


def _ffn_fwd(name, x, g_pre, ffn_w, first, g_post, carry=None):
    t, dm = x.shape
    tm = min(FFN_TM, t)
    w_specs, tf = _ffn_weight_specs(ffn_w, first)
    f_steps = N_DEV // FFN_BLOCKS

    def body(x_ref, gpre_ref, wg_ref, wu_ref, wd_ref, gpost_ref, h_ref, y_ref, hg_ref, hu_ref, a_ref, xn_ref, acc_ref):
        j = pl.program_id(1)

        @pl.when(j == 0)
        def _():
            xn_ref[...] = _rms(x_ref[...], gpre_ref[...], dm).astype(BF16)
            acc_ref[...] = jnp.zeros_like(acc_ref)

        xn = xn_ref[...]
        wg, wu, wd = (r[...].reshape(tf, dm) for r in (wg_ref, wu_ref, wd_ref))
        hg, hu = _nt(xn, wg), _nt(xn, wu)
        hg_ref[...] = hg.astype(BF16)
        hu_ref[...] = hu.astype(BF16)
        a = (_silu(hg) * hu).astype(BF16)
        a_ref[...] = a
        acc_ref[...] += _nn(a, wd)

        @pl.when(j == f_steps - 1)
        def _():
            h = acc_ref[...]
            h_ref[...] = h
            y_ref[...] = x_ref[...] + 0.5 * _rms(h, gpost_ref[...], dm)

    row = pl.BlockSpec((tm, dm), lambda i, j: (i, 0))
    vec = pl.BlockSpec((1, dm), lambda i, j: (0, 0))
    wide = pl.BlockSpec((tm, tf), lambda i, j: (i, j))
    return _call_carrying(
        body, carry, (x, g_pre, ffn_w, ffn_w, ffn_w, g_post), name=name,
        grid=(t // tm, f_steps),
        in_specs=[row, vec, *w_specs, vec],
        out_specs=[row, row, wide, wide, wide],
        out_shape=[jax.ShapeDtypeStruct((t, dm), F32)] * 2 + [jax.ShapeDtypeStruct((t, f_steps * tf), BF16)] * 3,
        scratch_shapes=[pltpu.VMEM((tm, dm), BF16), pltpu.VMEM((tm, dm), F32)],
        compiler_params=pltpu.CompilerParams(dimension_semantics=("arbitrary", "arbitrary")),
    )


def _ffn_bwd(name, x, h, hg, hu, dy, g_pre, ffn_w, first, g_post, carry=None, after=None):
    t, dm = x.shape
    tm = min(FFN_BWD_TM, t)
    w_specs, tf = _ffn_weight_specs(ffn_w, first)
    f_steps = N_DEV // FFN_BLOCKS
    f = f_steps * tf

    def post(hv, g):
        return 0.5 * _rms(hv, g, dm)

    def pre(xv, g):
        return _rms(xv, g, dm)

    def body(x_ref, h_ref, dy_ref, hg_ref, hu_ref, gpre_ref, wg_ref, wu_ref, wd_ref, gpost_ref,
             dx_ref, xn_ref, dh_ref, dhg_ref, dhu_ref, dgpre_ref, dgpost_ref, acc_ref):
        i, j = pl.program_id(0), pl.program_id(1)

        @pl.when((i == 0) & (j == 0))
        def _():
            dgpre_ref[...] = jnp.zeros_like(dgpre_ref)
            dgpost_ref[...] = jnp.zeros_like(dgpost_ref)

        @pl.when(j == 0)
        def _():
            xn_ref[...] = pre(x_ref[...], gpre_ref[...]).astype(BF16)
            _, pull = jax.vjp(post, h_ref[...], gpost_ref[...])
            dh, dg = pull(dy_ref[...])
            dh_ref[...] = dh.astype(BF16)
            dgpost_ref[...] += dg
            acc_ref[...] = jnp.zeros_like(acc_ref)

        wg, wu, wd = (r[...].reshape(tf, dm) for r in (wg_ref, wu_ref, wd_ref))
        hg, hu = hg_ref[...].astype(F32), hu_ref[...].astype(F32)
        da = _nt(dh_ref[...], wd)
        sig = _sigmoid(hg)
        act = hg * sig
        dhu = (da * act).astype(BF16)
        dhg = (da * hu * (sig * (1.0 + hg * (1.0 - sig)))).astype(BF16)
        dhg_ref[...] = dhg
        dhu_ref[...] = dhu
        acc_ref[...] += _nn(dhg, wg) + _nn(dhu, wu)

        @pl.when(j == f_steps - 1)
        def _():
            _, pull = jax.vjp(pre, x_ref[...], gpre_ref[...])
            dx, dg = pull(acc_ref[...])
            dx_ref[...] = dy_ref[...] + dx
            dgpre_ref[...] += dg

    row = pl.BlockSpec((tm, dm), lambda i, j: (i, 0))
    vec = pl.BlockSpec((1, dm), lambda i, j: (0, 0))
    wide = pl.BlockSpec((tm, tf), lambda i, j: (i, j))
    ordered = [] if after is None else [after]

    def after_it(*refs):
        body(*refs[:10], *refs[10 + len(ordered):])

    return _call_carrying(
        after_it, carry, (x, h, dy, hg, hu, g_pre, ffn_w, ffn_w, ffn_w, g_post, *ordered), name=name,
        grid=(t // tm, f_steps),
        in_specs=[row, row, row, wide, wide, vec, *w_specs, vec] + [pl.BlockSpec(memory_space=pl.ANY)] * len(ordered),
        out_specs=[row, row, row, wide, wide, vec, vec],
        out_shape=[jax.ShapeDtypeStruct((t, dm), F32), jax.ShapeDtypeStruct((t, dm), BF16), jax.ShapeDtypeStruct((t, dm), BF16),
                   jax.ShapeDtypeStruct((t, f), BF16), jax.ShapeDtypeStruct((t, f), BF16),
                   jax.ShapeDtypeStruct((1, dm), F32), jax.ShapeDtypeStruct((1, dm), F32)],
        scratch_shapes=[pltpu.VMEM((tm, dm), F32)],
        compiler_params=pltpu.CompilerParams(dimension_semantics=("arbitrary", "arbitrary")),
    )


ATT_T = 512
ATT_GROUP = 4
ATT_GROUP_FWD = 8
ATT_SCALE = (MLA_NOPE + MLA_ROPE) ** -0.5


def _stack_slots(ref, group):
    return jnp.stack([ref[:, pl.ds(j * SLOT, SLOT)] for j in range(group)])


def _unstack_slots(ref, val):
    for j in range(val.shape[0]):
        ref[:, pl.ds(j * SLOT, SLOT)] = val[j].astype(ref.dtype)


def _scores(q, k, diagonal):
    s = _nt(q, k) * ATT_SCALE
    if diagonal:
        row = lax.broadcasted_iota(jnp.int32, s.shape[1:], 0)
        col = lax.broadcasted_iota(jnp.int32, s.shape[1:], 1)
        s = jnp.where(col <= row, s, -1e30)
    return s


def _attn_pairs(steps, q_major):
    pairs = ([(qi, ki) for qi in range(steps) for ki in range(qi + 1)] if q_major
             else [(qi, ki) for ki in range(steps) for qi in range(ki, steps)])
    return jnp.array([p[0] for p in pairs], jnp.int32), jnp.array([p[1] for p in pairs], jnp.int32)


def _attn_specs(tile, group):
    width = group * SLOT
    return (pl.BlockSpec((tile, width), lambda h, p, qt, kt: (qt[p], h)),
            pl.BlockSpec((tile, width), lambda h, p, qt, kt: (kt[p], h)))


def _attn_fwd(q, k, v):
    t = q.shape[0]
    tile = min(ATT_T, t)
    steps = t // tile
    g = ATT_GROUP_FWD

    strip = min(SLOT, tile)

    def body(qt_ref, kt_ref, q_ref, k_ref, v_ref, o_ref, lse_ref, m_ref, l_ref, alpha_ref, acc_ref, s_ref, p_ref):
        qi, ki = qt_ref[pl.program_id(1)], kt_ref[pl.program_id(1)]

        @pl.when(ki == 0)
        def _():
            m_ref[...] = jnp.full_like(m_ref, -1e30)
            l_ref[...] = jnp.zeros_like(l_ref)
            acc_ref[...] = jnp.zeros_like(acc_ref)

        def step(diagonal):
            s_ref[...] = _nt(_stack_slots(k_ref, g), _stack_slots(q_ref, g))
            for j in range(tile // strip):
                c = pl.ds(j * strip, strip)
                s = s_ref[:, :, c] * ATT_SCALE
                if diagonal:
                    key = lax.broadcasted_iota(jnp.int32, s.shape[1:], 0)
                    query = lax.broadcasted_iota(jnp.int32, s.shape[1:], 1) + j * strip
                    s = jnp.where(key <= query, s, -1e30)
                m_old = m_ref[:, :, c]
                m_new = jnp.maximum(m_old, jnp.max(s, axis=1, keepdims=True))
                p = jnp.exp(s - m_new)
                alpha = jnp.exp(m_old - m_new)
                l_ref[:, :, c] = alpha * l_ref[:, :, c] + jnp.sum(p, axis=1, keepdims=True)
                alpha_ref[:, :, c] = alpha
                m_ref[:, :, c] = m_new
                p_ref[:, :, c] = p.astype(BF16)
            acc_ref[...] = acc_ref[...] * alpha_ref[...] + _tn(_stack_slots(v_ref, g), p_ref[...])

        @pl.when(ki < qi)
        def _():
            step(False)

        @pl.when(ki == qi)
        def _():
            step(True)
            out = acc_ref[...] / l_ref[...]
            lse = jnp.broadcast_to(m_ref[...] + jnp.log(l_ref[...]), out.shape)
            for j in range(g):
                o_ref[:, pl.ds(j * SLOT, SLOT)] = out[j].T
                lse_ref[:, pl.ds(j * SLOT, SLOT)] = lse[j].T

    q_spec, k_spec = _attn_specs(tile, g)
    tables = _attn_pairs(steps, True)
    return pl.pallas_call(
        body, name="attn_fwd",
        grid_spec=pltpu.PrefetchScalarGridSpec(
            num_scalar_prefetch=2, grid=(N_HEADS // g, tables[0].shape[0]),
            in_specs=[q_spec, k_spec, k_spec], out_specs=[q_spec, q_spec],
            scratch_shapes=[pltpu.VMEM((g, 1, tile), F32), pltpu.VMEM((g, 1, tile), F32), pltpu.VMEM((g, 1, tile), F32),
                            pltpu.VMEM((g, SLOT, tile), F32), pltpu.VMEM((g, tile, tile), F32), pltpu.VMEM((g, tile, tile), BF16)]),
        out_shape=[jax.ShapeDtypeStruct((t, N_HEADS * SLOT), F32)] * 2,
        compiler_params=pltpu.CompilerParams(dimension_semantics=("parallel", "arbitrary")),
    )(*tables, q, k, v)


def _attn_grad_scores(q, k, v, do, lse_ref, delta_ref, diagonal):
    g = ATT_GROUP
    p = jnp.exp(_scores(q, k, diagonal) - _stack_slots(lse_ref, g)[:, :, 0:1])
    dp = _nt(do, v)
    return p, p * (dp - _stack_slots(delta_ref, g)[:, :, 0:1]) * ATT_SCALE


def _attn_bwd(q, k, v, do, lse, delta):
    t = q.shape[0]
    tile = min(ATT_T, t)
    steps = t // tile
    g = ATT_GROUP

    def body(qt_ref, kt_ref, q_ref, k_ref, v_ref, do_ref, lse_ref, delta_ref, dq_ref, dk_ref, dv_ref, dk_acc, dv_acc):
        qi, ki = qt_ref[pl.program_id(1)], kt_ref[pl.program_id(1)]

        @pl.when(pl.program_id(1) == 0)
        def _():
            dq_ref[...] = jnp.zeros_like(dq_ref)

        def step(diagonal):
            qq, kk = _stack_slots(q_ref, g), _stack_slots(k_ref, g)
            do_b = _stack_slots(do_ref, g).astype(BF16)
            p, ds = _attn_grad_scores(qq, kk, _stack_slots(v_ref, g), do_b, lse_ref, delta_ref, diagonal)
            ds = ds.astype(BF16)
            dv_acc[...] += _tn(p.astype(BF16), do_b)
            dk_acc[...] += _tn(ds, qq)
            dq = _nn(ds, kk)
            rows = pl.ds(pl.multiple_of(qi * tile, tile), tile)
            for j in range(g):
                dq_ref[rows, pl.ds(j * SLOT, SLOT)] += dq[j]

        @pl.when(qi == ki)
        def _():
            dk_acc[...] = jnp.zeros_like(dk_acc)
            dv_acc[...] = jnp.zeros_like(dv_acc)
            step(True)

        @pl.when(qi > ki)
        def _():
            step(False)

        @pl.when(qi == steps - 1)
        def _():
            _unstack_slots(dk_ref, dk_acc[...])
            _unstack_slots(dv_ref, dv_acc[...])

    q_spec, k_spec = _attn_specs(tile, g)
    tables = _attn_pairs(steps, False)
    return pl.pallas_call(
        body, name="attn_bwd",
        grid_spec=pltpu.PrefetchScalarGridSpec(
            num_scalar_prefetch=2, grid=(N_HEADS // g, tables[0].shape[0]),
            in_specs=[q_spec, k_spec, k_spec, q_spec, q_spec, q_spec],
            out_specs=[pl.BlockSpec((t, g * SLOT), lambda h, p, qt, kt: (0, h)), k_spec, k_spec],
            scratch_shapes=[pltpu.VMEM((g, tile, SLOT), F32), pltpu.VMEM((g, tile, SLOT), F32)]),
        out_shape=[jax.ShapeDtypeStruct((t, N_HEADS * SLOT), F32)] * 3,
        compiler_params=pltpu.CompilerParams(dimension_semantics=("parallel", "arbitrary")),
    )(*tables, q, k, v, do, lse, delta)


CONV_PAD = 8


def _fill_padded(ref, val):
    t = val.shape[0]
    zeros = jnp.zeros((CONV_PAD, val.shape[1]), val.dtype)
    ref[pl.ds(0, CONV_PAD)] = zeros
    ref[pl.ds(CONV_PAD + t, CONV_PAD)] = zeros
    ref[pl.ds(CONV_PAD, t)] = val


def _shifted(ref, s):
    return ref[pl.ds(CONV_PAD - s, ref.shape[0] - 2 * CONV_PAD)]


def _l2norm(x):
    return x * lax.rsqrt(jnp.sum(x * x, axis=-1, keepdims=True) + EPS)


def _conv_pre(x_pad, w):
    y = w[GDN_CONV - 1:GDN_CONV, :] * _shifted(x_pad, 0)
    for s in range(1, GDN_CONV):
        y = y + w[GDN_CONV - 1 - s:GDN_CONV - s, :] * _shifted(x_pad, s)
    return y


def _gdn_conv_fwd(x, w):
    t, width = x.shape

    def body(x_ref, w_ref, o_ref, x_pad):
        _fill_padded(x_pad, x_ref[...])
        act = _silu(_conv_pre(x_pad, w_ref[...]))
        normed = pl.program_id(0) < 2 * N_HEADS
        o_ref[...] = jnp.where(normed, _l2norm(act), act)

    return pl.pallas_call(
        body, name="gdn_conv_fwd",
        grid=(width // SLOT,),
        in_specs=[pl.BlockSpec((t, SLOT), lambda j: (0, j)), pl.BlockSpec((GDN_CONV, SLOT), lambda j: (0, j))],
        out_specs=pl.BlockSpec((t, SLOT), lambda j: (0, j)),
        out_shape=jax.ShapeDtypeStruct((t, width), F32),
        scratch_shapes=[pltpu.VMEM((t + 2 * CONV_PAD, SLOT), F32)],
        compiler_params=pltpu.CompilerParams(dimension_semantics=("parallel",)),
    )(x, w)


def _gdn_conv_bwd(x, w, dout):
    t, width = x.shape

    def body(x_ref, w_ref, do_ref, dx_ref, dw_ref, x_pad, dy_pad):
        wv = w_ref[...]
        _fill_padded(x_pad, x_ref[...])
        y = _conv_pre(x_pad, wv)
        sig = _sigmoid(y)
        act = y * sig
        _, pull = jax.vjp(_l2norm, act)
        normed = pl.program_id(0) < 2 * N_HEADS
        dact = jnp.where(normed, pull(do_ref[0])[0], do_ref[0])
        dy = dact * (sig * (1.0 + y * (1.0 - sig)))
        _fill_padded(dy_pad, dy)
        dx = wv[GDN_CONV - 1:GDN_CONV, :] * dy
        for s in range(1, GDN_CONV):
            dx = dx + wv[GDN_CONV - 1 - s:GDN_CONV - s, :] * _shifted(dy_pad, -s)
        dx_ref[...] = dx.astype(BF16)
        for s in range(GDN_CONV):
            dw_ref[GDN_CONV - 1 - s:GDN_CONV - s, :] = jnp.sum(dy * _shifted(x_pad, s), axis=0, keepdims=True)

    col = pl.BlockSpec((t, SLOT), lambda j: (0, j))
    tap = pl.BlockSpec((GDN_CONV, SLOT), lambda j: (0, j))
    return pl.pallas_call(
        body, name="gdn_conv_bwd",
        grid=(width // SLOT,),
        in_specs=[col, tap, pl.BlockSpec((1, t, SLOT), lambda j: (j // N_HEADS, 0, j % N_HEADS))],
        out_specs=[col, tap],
        out_shape=[jax.ShapeDtypeStruct((t, width), BF16), jax.ShapeDtypeStruct((GDN_CONV, width), F32)],
        scratch_shapes=[pltpu.VMEM((t + 2 * CONV_PAD, SLOT), F32)] * 2,
        compiler_params=pltpu.CompilerParams(dimension_semantics=("parallel",)),
    )(x, w, dout)


def _softplus(x):
    e = jnp.exp(-jnp.abs(x))
    u = 1.0 + e
    log1p = jnp.where(u == 1.0, e, jnp.log(u) * e / jnp.where(u == 1.0, 1.0, u - 1.0))
    return jnp.maximum(x, 0.0) + log1p


def _chunk_running_sum(x, reverse=False):
    tm = x.shape[0]
    at = lax.broadcasted_iota(jnp.int32, x.shape, 0) % GDN_CHUNK
    step = 1
    while step < GDN_CHUNK:
        if reverse:
            x = x + jnp.where(at < GDN_CHUNK - step, pltpu.roll(x, tm - step, 0), 0.0)
        else:
            x = x + jnp.where(at >= step, pltpu.roll(x, step, 0), 0.0)
        step *= 2
    return x


def _gates_fwd(ab, a_log, dt_bias):
    def fn(rows, consts):
        (abv,), (alog, dtb) = rows, consts
        g = _chunk_running_sum(-jnp.exp(alog) * _softplus(abv + dtb))
        beta = _sigmoid(abv)
        shape = (abv.shape[0], SLOT)
        g_slots = [jnp.broadcast_to(g[:, h:h + 1], shape) for h in range(N_HEADS)]
        b_slots = [jnp.broadcast_to(beta[:, N_HEADS + h:N_HEADS + h + 1], shape) for h in range(N_HEADS)]
        return [jnp.concatenate(g_slots, axis=1), jnp.concatenate(b_slots, axis=1)], []

    width = N_HEADS * SLOT
    return _rowwise("gdn_gates_fwd", fn, [ab], [a_log, dt_bias], [(width, F32), (width, F32)])


def _gates_bwd(ab, a_log, dt_bias, dg, dbeta):
    def fn(rows, consts):
        (abv, dgv, dbv), (alog, dtb) = rows, consts
        lane = lax.broadcasted_iota(jnp.int32, abv.shape, 1)
        dg_tok = jnp.zeros_like(abv)
        db_tok = jnp.zeros_like(abv)
        for h in range(N_HEADS):
            dg_tok = dg_tok + jnp.where(lane == h, jnp.sum(dgv[:, h * SLOT:(h + 1) * SLOT], axis=1, keepdims=True), 0.0)
            db_tok = db_tok + jnp.where(lane == N_HEADS + h, jnp.sum(dbv[:, h * SLOT:(h + 1) * SLOT], axis=1, keepdims=True), 0.0)
        dg_tok = _chunk_running_sum(dg_tok, reverse=True)
        xa = abv + dtb
        g = -jnp.exp(alog) * _softplus(xa)
        da = dg_tok * (-jnp.exp(alog)) * _sigmoid(xa)
        beta = _sigmoid(abv)
        dab = jnp.where(lane < N_HEADS, da, db_tok * beta * (1.0 - beta))
        dab = jnp.where(lane < 2 * N_HEADS, dab, 0.0)
        d_alog = jnp.sum(jnp.where(lane < N_HEADS, dg_tok * g, 0.0), axis=0, keepdims=True)
        d_dtb = jnp.sum(jnp.where(lane < N_HEADS, da, 0.0), axis=0, keepdims=True)
        return [dab], [d_alog, d_dtb]

    return _rowwise("gdn_gates_bwd", fn, [ab, dg, dbeta], [a_log, dt_bias], [(SLOT, F32)], sums=[SLOT, SLOT])


ROPE_HALF = MLA_ROPE // 2


def _rope_tables(positions):
    freqs = ROPE_THETA ** (-jnp.arange(ROPE_HALF, dtype=F32) / ROPE_HALF)
    ang = positions.astype(F32)[:, None] * freqs
    cos, sin = jnp.cos(ang), jnp.sin(ang)
    t = positions.shape[0]
    ones, zeros = jnp.ones((t, MLA_NOPE), F32), jnp.zeros((t, MLA_NOPE), F32)
    tail = jnp.zeros((t, SLOT - MLA_NOPE - MLA_ROPE), F32)
    half0 = jnp.zeros((t, ROPE_HALF), F32)
    same = jnp.concatenate([ones, cos, cos, tail], axis=1)
    from_low = jnp.concatenate([zeros, half0, sin, tail], axis=1)
    from_high = jnp.concatenate([zeros, -sin, half0, tail], axis=1)
    return same, from_low, from_high


def _rope(x, tabs):
    same, from_low, from_high = tabs
    width = x.shape[1]
    return x * same + pltpu.roll(x, ROPE_HALF, 1) * from_low + pltpu.roll(x, width - ROPE_HALF, 1) * from_high


def _rope_transposed(dy, tabs):
    same, from_low, from_high = tabs
    width = dy.shape[1]
    return dy * same + pltpu.roll(dy * from_low, width - ROPE_HALF, 1) + pltpu.roll(dy * from_high, ROPE_HALF, 1)


def _tile_slots(tab):
    return jnp.concatenate([tab] * N_HEADS, axis=1)


A_WIDTH = MLA_Q_RANK + MLA_KV_RANK + 2 * SLOT
A_KPE = MLA_Q_RANK + MLA_KV_RANK
A_AB = A_KPE + SLOT
WIDE = N_HEADS * SLOT


def _mla_front_fwd(proj_a, tabs, g_q, g_kv, w_uq, w_kv):
    def fn(rows, consts):
        pa, *tb = rows
        gq, gkv, wuq, wkv = consts
        cqn = _rms(pa[:, :MLA_Q_RANK], gq, MLA_Q_RANK).astype(BF16)
        ckvn = _rms(pa[:, MLA_Q_RANK:A_KPE], gkv, MLA_KV_RANK).astype(BF16)
        kv = _nt(ckvn, wkv)
        q = _rope(_nt(cqn, wuq), [_tile_slots(x) for x in tb])
        k = kv[:, :WIDE] + _tile_slots(_rope(pa[:, A_KPE:A_AB], tb))
        return [cqn, ckvn, q, k, kv[:, WIDE:]], []

    return _rowwise("mla_front_fwd", fn, [proj_a, *tabs], [g_q, g_kv, w_uq, w_kv],
                    [(MLA_Q_RANK, BF16), (MLA_KV_RANK, BF16)] + [(WIDE, BF16)] * 3)


def _mla_front_bwd(proj_a, tabs, g_q, g_kv, w_uq, w_kv, dq, dk, dv, dab):
    def fn(rows, consts):
        pa, t0, t1, t2, dqv, dkv, dvv, da = rows
        gq, gkv, wuq, wkv = consts
        tb = (t0, t1, t2)
        dq_p = _rope_transposed(dqv, [_tile_slots(x) for x in tb]).astype(BF16)
        dkv_p = jnp.concatenate([dkv, dvv], axis=1).astype(BF16)
        dkpe = dkv[:, :SLOT]
        for h in range(1, N_HEADS):
            dkpe = dkpe + dkv[:, h * SLOT:(h + 1) * SLOT]
        _, pull_q = jax.vjp(lambda x, g: _rms(x, g, MLA_Q_RANK), pa[:, :MLA_Q_RANK], gq)
        _, pull_kv = jax.vjp(lambda x, g: _rms(x, g, MLA_KV_RANK), pa[:, MLA_Q_RANK:A_KPE], gkv)
        dcq, dgq = pull_q(_nn(dq_p, wuq))
        dckv, dgkv = pull_kv(_nn(dkv_p, wkv))
        return [jnp.concatenate([dcq, dckv, _rope_transposed(dkpe, tb), da], axis=1), dq_p, dkv_p], [dgq, dgkv]

    return _rowwise("mla_front_bwd", fn, [proj_a, *tabs, dq, dk, dv, dab], [g_q, g_kv, w_uq, w_kv],
                    [(A_WIDTH, BF16), (WIDE, BF16), (2 * WIDE, BF16)], sums=[MLA_Q_RANK, MLA_KV_RANK])


def _slot_sum(x):
    parts = [jnp.broadcast_to(jnp.sum(x[:, h * SLOT:(h + 1) * SLOT], axis=1, keepdims=True), (x.shape[0], SLOT))
             for h in range(N_HEADS)]
    return jnp.concatenate(parts, axis=1)


def _mix_join(o_mla, o_gdn, gate, g_mla, g_gdn):
    mla = _rms(o_mla, g_mla, N_HEADS * MLA_V)
    gdn = o_gdn * lax.rsqrt(_slot_sum(o_gdn * o_gdn) * (1.0 / GDN_D) + EPS) * g_gdn * _silu(gate)
    return mla, gdn


MIX_TM = 256


def _mix_fwd(o_mla, o_gdn, gate, x, g_mla, g_gdn, w_out, g_post):
    dm = x.shape[1]

    def fn(rows, consts):
        om, og, gt, xv = rows
        gm, gg, wo, gp = consts
        cat = jnp.concatenate(_mix_join(om, og, gt, gm, gg), axis=1).astype(BF16)
        mixed = _nn(cat, wo)
        return [cat, mixed, xv + _rms(mixed, gp, dm)], []

    return _rowwise("mix_fwd", fn, [o_mla, o_gdn, gate, x], [g_mla, g_gdn, w_out, g_post],
                    [(2 * WIDE, BF16), (dm, F32), (dm, F32)], tm=MIX_TM)


def _mix_bwd(o_mla, o_gdn, gate, mixed, dy, g_mla, g_gdn, w_out, g_post):
    dm = mixed.shape[1]

    def fn(rows, consts):
        om, og, gt, mx, dyv = rows
        gm, gg, wo, gp = consts
        _, pull_post = jax.vjp(lambda hv, gv: _rms(hv, gv, dm), mx, gp)
        dmixed, dgp = pull_post(dyv)
        dmixed = dmixed.astype(BF16)
        dc = _nt(dmixed, wo)
        _, pull = jax.vjp(lambda x, g: _rms(x, g, N_HEADS * MLA_V), om, gm)
        dom, dgm = pull(dc[:, :WIDE])
        dn_out = dc[:, WIDE:]
        r = lax.rsqrt(_slot_sum(og * og) * (1.0 / GDN_D) + EPS)
        sig = _sigmoid(gt)
        normed = og * r
        dn = dn_out * gg * (gt * sig)
        dog = r * dn - normed * (r * r) * _slot_sum(dn * og) * (1.0 / GDN_D)
        dgt = dn_out * normed * gg * (sig * (1.0 + gt * (1.0 - sig)))
        dgg = jnp.sum(dn_out * normed * (gt * sig), axis=0, keepdims=True)
        return [dmixed, dom, _slot_sum(dom * om), dog, dgt], [dgp, dgm, dgg]

    return _rowwise("mix_bwd", fn, [o_mla, o_gdn, gate, mixed, dy], [g_mla, g_gdn, w_out, g_post],
                    [(dm, BF16), (WIDE, F32), (WIDE, F32), (WIDE, F32), (WIDE, BF16)], sums=[dm, WIDE, WIDE], tm=MIX_TM)


def _proj_fwd(x, g, weights):
    dm = x.shape[1]

    def fn(rows, consts):
        hn = _rms(rows[0], consts[0], dm).astype(BF16)
        return [hn] + [_nt(hn, wv) for wv in consts[1:]], []

    return _rowwise("proj_fwd", fn, [x], [g, *weights], [(dm, BF16)] + [(wv.shape[0], F32) for wv in weights], tm=MIX_TM)


def _proj_bwd(x, g, weights, cots, dy, h, g_post):
    dm = x.shape[1]
    n = len(weights)

    def fn(rows, consts):
        xv, dyv, hv, *parts = rows
        dn = _nn(parts[0], consts[2])
        for p, wv in zip(parts[1:], consts[3:]):
            dn = dn + _nn(p, wv)
        _, pull = jax.vjp(lambda a, gv: _rms(a, gv, dm), xv, consts[0])
        dx, dg = pull(dn)
        dx = dyv + dx
        _, pull = jax.vjp(lambda a: 0.5 * _rms(a, consts[1], dm), hv)
        return [dx, pull(dx)[0]], [dg]

    assert len(cots) == n
    return _rowwise("proj_bwd", fn, [x, dy, h, *cots], [g, g_post, *weights], [(dm, F32), (dm, BF16)], sums=[dm], tm=MIX_TM)


def _loss_fwd(y, target):
    dm = y.shape[1]

    def fn(rows, consts):
        err = rows[0] - rows[1]
        sq = err * err
        lanes = sq[:, :SLOT]
        for j in range(1, dm // SLOT):
            lanes = lanes + sq[:, j * SLOT:(j + 1) * SLOT]
        return [err * (1.0 / dm)], [jnp.sum(lanes, axis=0, keepdims=True) * (0.5 / dm)]

    return _rowwise("loss", fn, [y, target], [], [(dm, F32)], sums=[SLOT])


W_IN_CUTS = (0, 256, 384, 416, 1952, 1960, 1968, 2480)


def _heads_out(w, per_head, axis=-1):
    axis = axis % w.ndim
    shape = w.shape
    n = shape[axis] // per_head
    w = w.reshape(shape[:axis] + (n, per_head) + shape[axis + 1:])
    pad = [(0, 0)] * w.ndim
    pad[axis + 1] = (0, SLOT - per_head)
    return jnp.pad(w, pad).reshape(shape[:axis] + (n * SLOT,) + shape[axis + 1:])


def _heads_in(w, per_head, axis=-1):
    axis = axis % w.ndim
    shape = w.shape
    n = shape[axis] // SLOT
    w = w.reshape(shape[:axis] + (n, SLOT) + shape[axis + 1:])
    w = lax.slice_in_dim(w, 0, per_head, axis=axis + 1)
    return w.reshape(shape[:axis] + (n * per_head,) + shape[axis + 1:])


def _pad_lanes(v, lo, width=SLOT):
    return jnp.pad(v, [(0, 0)] * (v.ndim - 1) + [(lo, width - lo - v.shape[-1])])


def _pad_rows(v, lo, rows=SLOT):
    return jnp.pad(v, [(lo, rows - lo - v.shape[0])] + [(0, 0)] * (v.ndim - 1))


def _layout_weights(w):
    c = W_IN_CUTS
    w_in = w["w_in_t"]
    p = {}
    p["w_a"] = jnp.concatenate([w_in[c[0]:c[2]], _pad_rows(w_in[c[2]:c[3]], MLA_NOPE), _pad_rows(w_in[c[4]:c[6]], 0)], axis=0)
    p["w_qkv"] = _heads_out(w_in[c[3]:c[4]], GDN_D, axis=0)
    p["w_gate"] = _heads_out(w_in[c[6]:c[7]], GDN_D, axis=0)
    p["w_uq"] = _heads_out(w["uq_t"], MLA_NOPE + MLA_ROPE, axis=0)
    ukv = w["ukv_t"].reshape(N_HEADS, MLA_NOPE + MLA_V, MLA_KV_RANK)
    p["w_kv"] = jnp.concatenate([_heads_out(ukv[:, :MLA_NOPE].reshape(-1, MLA_KV_RANK), MLA_NOPE, axis=0),
                                 _heads_out(ukv[:, MLA_NOPE:].reshape(-1, MLA_KV_RANK), MLA_V, axis=0)], axis=0)
    p["conv"] = _heads_out(w["gdn_conv_w"], GDN_D)
    p["g_mla_out"] = _heads_out(w["mla_out_g"], MLA_V)
    p["g_gdn"] = jnp.tile(_pad_lanes(w["gdn_norm_g"], 0), (1, N_HEADS))
    p["a_log"] = _pad_lanes(w["gdn_a_log"], 0)
    p["dt_bias"] = _pad_lanes(w["gdn_dt_bias"], 0)
    return p


def _unlayout_grads(d):
    c = W_IN_CUTS
    g = {}
    da = d["w_a"]
    kpe0 = A_KPE + MLA_NOPE
    g["w_in_t"] = jnp.concatenate([da[:A_KPE], da[kpe0:kpe0 + MLA_ROPE], _heads_in(d["w_qkv"], GDN_D, axis=0),
                                   da[A_AB:A_AB + 2 * N_HEADS], _heads_in(d["w_gate"], GDN_D, axis=0)], axis=0)
    assert g["w_in_t"].shape[0] == c[-1]
    g["uq_t"] = _heads_in(d["w_uq"], MLA_NOPE + MLA_ROPE, axis=0)
    dk = _heads_in(d["w_kv"][:WIDE], MLA_NOPE, axis=0).reshape(N_HEADS, MLA_NOPE, MLA_KV_RANK)
    dv = _heads_in(d["w_kv"][WIDE:], MLA_V, axis=0).reshape(N_HEADS, MLA_V, MLA_KV_RANK)
    g["ukv_t"] = jnp.concatenate([dk, dv], axis=1).reshape(-1, MLA_KV_RANK)
    g["w_out"] = _heads_in(d["w_out"], GDN_D, axis=0)
    g["gdn_conv_w"] = _heads_in(d["conv"], GDN_D)
    g["mla_out_g"] = _heads_in(d["g_mla_out"], MLA_V)
    g["gdn_norm_g"] = jnp.sum(d["g_gdn"].reshape(N_HEADS, SLOT), axis=0, keepdims=True)[:, :GDN_D]
    g["gdn_a_log"] = d["a_log"][:, :N_HEADS]
    g["gdn_dt_bias"] = d["dt_bias"][:, :N_HEADS]
    return g


def _weight_grad(name, cots, acts, out_dtype=F32, tm=1024, tn=1024, tk=2048, after=None):
    return _matmul(name, cots, acts, "tn", out_dtype=out_dtype, tm=tm, tn=tn, tk=tk, after=after)


def _by_device(a):
    return a.astype(BF16).reshape((N_DEV, a.shape[0] // N_DEV) + a.shape[1:])


def _rows_of(blocks):
    return blocks.reshape((-1,) + blocks.shape[2:])


def _local_step(x, positions, target, w, mid, late):
    tabs = _rope_tables(positions)

    (h1, x1, hg1, hu1, a1), gathered = _ffn_fwd("ffn1_fwd", x, w["ffn1_pre_g"], w["ffn1"], 0, w["ffn1_post_g"], carry=mid)
    w = dict(w, w_in_t=_rows_of(gathered[0]), uq_t=_rows_of(gathered[1]), ukv_t=_rows_of(gathered[2]),
             gdn_conv_w=gathered[3].transpose(1, 0, 2).reshape(CONV_SHAPE))
    p = _layout_weights(w)
    in_weights = [p["w_a"], p["w_qkv"], p["w_gate"]]
    hn, proj_a, proj_qkv, proj_gate = _proj_fwd(x1, w["mix_pre_g"], in_weights)
    cqn, ckvn, q, k, v = _mla_front_fwd(proj_a, tabs, w["mla_q_norm_g"], w["mla_kv_norm_g"], p["w_uq"], p["w_kv"])
    o_mla, lse = _attn_fwd(q, k, v)
    ab = (proj_a, SLOT, A_AB // SLOT)
    qkv_n = _gdn_conv_fwd(proj_qkv, p["conv"])
    gb, bb = _gates_fwd(ab, p["a_log"], p["dt_bias"])
    (o_gdn, keep), (ffn2, w_out) = _gdn_fwd(qkv_n, gb, bb, carry=late)
    p["w_out"] = _heads_out(_rows_of(w_out), GDN_D, axis=0)
    cat, mixed, x2 = _mix_fwd(o_mla, o_gdn, proj_gate, x1, p["g_mla_out"], p["g_gdn"], p["w_out"], w["mix_post_g"])
    (h2, y, hg2, hu2, a2), _ = _ffn_fwd("ffn2_fwd", x2, w["ffn2_pre_g"], ffn2, 0, w["ffn2_post_g"])
    dy, loss_lanes = _loss_fwd(y, target)

    g = {}
    (dx2, xn2, dh2, dhg2, dhu2, g["ffn2_pre_g"], g["ffn2_post_g"]), _ = _ffn_bwd(
        "ffn2_bwd", x2, h2, hg2, hu2, dy, w["ffn2_pre_g"], ffn2, 0, w["ffn2_post_g"])
    ffn2_grads = _Scatter([_by_device(_weight_grad("ffn2_dw_gate", dhg2, xn2, BF16, tm=1408)),
                           _by_device(_weight_grad("ffn2_dw_up", dhu2, xn2, BF16, tm=1408)),
                           _by_device(_weight_grad("ffn2_dw_down", a2, dh2, BF16, tm=1408))])
    d = {}
    dmixed, do_mla, delta, do_gdn, dgate, g["mix_post_g"], d["g_mla_out"], d["g_gdn"] = _mix_bwd(
        o_mla, o_gdn, proj_gate, mixed, dx2, p["g_mla_out"], p["g_gdn"], p["w_out"], w["mix_post_g"])
    d["w_out"] = _weight_grad("mix_out_dw", cat, dmixed, BF16)
    dq, dk, dv = _attn_bwd(q, k, v, do_mla, lse, delta)
    (dqkv_n, dgb, dbb), landed_ffn2 = _gdn_bwd(qkv_n, gb, bb, keep, do_gdn, carry=ffn2_grads)
    dab, d["a_log"], d["dt_bias"] = _gates_bwd(ab, p["a_log"], p["dt_bias"], dgb, dbb)
    dproj_qkv, d["conv"] = _gdn_conv_bwd(proj_qkv, p["conv"], dqkv_n)
    dproj_a, dq_p, dkv_p, g["mla_q_norm_g"], g["mla_kv_norm_g"] = _mla_front_bwd(
        proj_a, tabs, w["mla_q_norm_g"], w["mla_kv_norm_g"], p["w_uq"], p["w_kv"], dq, dk, dv, dab)
    d["w_uq"] = _weight_grad("mla_q_dw", dq_p, cqn, BF16)
    d["w_kv"] = _weight_grad("mla_kv_dw", dkv_p, ckvn, BF16)
    d["w_a"] = _weight_grad("proj_a_dw", dproj_a, hn, BF16, tm=640)
    d["w_qkv"] = _weight_grad("proj_qkv_dw", dproj_qkv, hn, BF16)
    d["w_gate"] = _weight_grad("proj_gate_dw", dgate, hn, BF16)
    dx1, dh1, g["mix_pre_g"] = _proj_bwd(x1, w["mix_pre_g"], in_weights, [dproj_a, dproj_qkv, dgate], dx2,
                                        h1, w["ffn1_post_g"])
    g.update(_unlayout_grads(d))
    begun = {}
    blocks = _by_device(_weight_grad("ffn1_w_down_grad", a1, dh1, BF16, tm=1408))
    begun["ffn1_w_down"], token = _scatter_begin("scatter_ffn1_w_down_begin", blocks)
    others = list(OTHER.values())
    (dx, xn1, _, dhg1, dhu1, g["ffn1_pre_g"], g["ffn1_post_g"]), landed_others = _ffn_bwd(
        "ffn1_bwd", x, h1, hg1, hu1, dx1, w["ffn1_pre_g"], w["ffn1"], 0, w["ffn1_post_g"],
        carry=_Scatter([_by_device(g.pop(t)) for t in others]), after=token)
    landed = dict(zip(list(FFN_NAMES[3:]) + list(OTHER), list(landed_ffn2) + list(landed_others)))
    for name, cots, acts in (("ffn1_w_gate", dhg1, xn1), ("ffn1_w_up", dhu1, xn1)):
        blocks = _by_device(_weight_grad(name + "_grad", cots, acts, BF16, tm=1408, after=token))
        begun[name], token = _scatter_begin("scatter_" + name + "_begin", blocks)
    packed = _pack_small(g, g["gdn_conv_w"].reshape(-1), REDUCE_ROWS)
    packed = packed.at[REDUCE_ROWS - 1, ROW - 1].set(jnp.sum(loss_lanes))
    begun["small"], small_token = _scatter_begin("reduce_small_begin", jnp.broadcast_to(packed, (N_DEV,) + packed.shape))
    return dx, g, landed, begun, token + small_token


MESH_AXES = ("x", "y", "c")
N_LINKS = N_DEV - 1


def _place():
    return tuple(lax.axis_index(a) for a in MESH_AXES)


def _block_of(dev):
    x, y, c = dev
    return 4 * x + 2 * y + c


def _remote_copy(src, dst, sems, k, to):
    send_sems, recv_sems = sems
    return pltpu.make_async_remote_copy(src_ref=src, dst_ref=dst, send_sem=send_sems.at[k], recv_sem=recv_sems.at[k],
                                        device_id=to, device_id_type=pl.DeviceIdType.MESH)


class _Exchange:
    def __init__(self, arrays):
        self.arrays = list(arrays)
        self.n = len(self.arrays)
        self.specs = [pl.BlockSpec(memory_space=pl.ANY)] * self.n
        self.scratch = [pltpu.SemaphoreType.DMA((self.n * N_LINKS,)), pltpu.SemaphoreType.DMA((self.n * N_LINKS,)),
                        pltpu.SemaphoreType.DMA((self.n,))]

    def split(self, refs):
        n = self.n
        return refs[:n], refs[n:2 * n], (refs[2 * n], refs[2 * n + 1]), refs[2 * n + 2]


class _Gather(_Exchange):
    def out_shape(self):
        return [jax.ShapeDtypeStruct((N_DEV,) + a.shape, a.dtype) for a in self.arrays]

    def _plan(self, ins, outs, sems, local_sems):
        x, y, c = _place()
        me, sibling = (x, y, c), (x, y, 1 - c)
        chips = [(1 - x, y), (x, 1 - y), (1 - x, 1 - y)]

        def copy(a, k, block, to, mine=False):
            src = ins[a] if mine else outs[a].at[_block_of(block)]
            return _remote_copy(src, outs[a].at[_block_of(block)], sems, a * N_LINKS + k, to)

        local = [pltpu.make_async_copy(ins[a], outs[a].at[_block_of(me)], local_sems.at[a]) for a in range(self.n)]
        first = []
        for a in range(self.n):
            first.append(copy(a, 0, me, sibling, mine=True))
            first += [copy(a, 1 + j, me, (*chip, c), mine=True) for j, chip in enumerate(chips)]
        return me, sibling, chips, c, copy, local, first

    def start(self, ins, outs, sems, local_sems):
        *_, local, first = self._plan(ins, outs, sems, local_sems)
        for cp in local + first:
            cp.start()

    def finish(self, ins, outs, sems, local_sems):
        me, sibling, chips, c, copy, local, first = self._plan(ins, outs, sems, local_sems)
        passed = []
        for j, chip in enumerate(chips):
            for a in range(self.n):
                copy(a, 1 + j, (*chip, c), me).wait_recv()
                passed.append(copy(a, 4 + j, (*chip, c), sibling))
                passed[-1].start()
        for a in range(self.n):
            copy(a, 0, sibling, me).wait_recv()
            for j, chip in enumerate(chips):
                copy(a, 4 + j, (*chip, 1 - c), me).wait_recv()
        for cp in first + passed:
            cp.wait_send()
        for cp in local:
            cp.wait()


class _Scatter(_Exchange):
    def out_shape(self):
        return [jax.ShapeDtypeStruct(a.shape, a.dtype) for a in self.arrays]

    def _plan(self, ins, outs, sems, local_sems):
        x, y, c = _place()
        me = _block_of((x, y, c))

        def peer(r):
            return (1 - x if r & 4 else x, 1 - y if r & 2 else y, 1 - c if r & 1 else c)

        local = [pltpu.make_async_copy(ins[a].at[me], outs[a].at[me], local_sems.at[a]) for a in range(self.n)]
        sends = [_remote_copy(ins[a].at[_block_of(peer(r))], outs[a].at[me], sems, a * N_LINKS + r - 1, peer(r))
                 for a in range(self.n) for r in range(1, N_DEV)]
        arrivals = [_remote_copy(ins[a].at[me], outs[a].at[_block_of(peer(r))], sems, a * N_LINKS + r - 1, peer(r))
                    for a in range(self.n) for r in range(1, N_DEV)]
        return local, sends, arrivals

    def start(self, ins, outs, sems, local_sems):
        local, sends, _ = self._plan(ins, outs, sems, local_sems)
        for cp in local + sends:
            cp.start()

    def finish(self, ins, outs, sems, local_sems):
        local, sends, arrivals = self._plan(ins, outs, sems, local_sems)
        for cp in arrivals:
            cp.wait_recv()
        for cp in sends:
            cp.wait_send()
        for cp in local:
            cp.wait()


def _exchange(name, plan):
    def body(*refs):
        parts = plan.split(refs)
        plan.start(*parts)
        plan.finish(*parts)

    return pl.pallas_call(
        body, name=name,
        in_specs=plan.specs,
        out_specs=plan.specs,
        out_shape=plan.out_shape(),
        scratch_shapes=plan.scratch,
    )(*plan.arrays)


def _call_carrying(body, plan, operands, *, name, grid, in_specs, out_specs, out_shape, scratch_shapes, compiler_params):
    if plan is None:
        outs = pl.pallas_call(body, name=name, grid=grid, in_specs=in_specs, out_specs=out_specs, out_shape=out_shape,
                              scratch_shapes=scratch_shapes, compiler_params=compiler_params)(*operands)
        return outs, []
    n_i, n_o, n_s, k = len(in_specs), len(out_specs), len(scratch_shapes), plan.n

    def whole(*refs):
        cut = [n_i, n_i + k, n_i + k + n_o, n_i + 2 * k + n_o, n_i + 2 * k + n_o + n_s]
        own_in, ex_in, own_out, ex_out, own_scr, ex_scr = (refs[a:b] for a, b in zip([0] + cut, cut + [len(refs)]))
        parts = plan.split(ex_in + ex_out + ex_scr)
        first = last = True
        for axis, size in enumerate(grid):
            first = first & (pl.program_id(axis) == 0)
            last = last & (pl.program_id(axis) == size - 1)

        @pl.when(first)
        def _():
            plan.start(*parts)

        body(*own_in, *own_out, *own_scr)

        @pl.when(last)
        def _():
            plan.finish(*parts)

    outs = pl.pallas_call(
        whole, name=name, grid=grid,
        in_specs=list(in_specs) + plan.specs, out_specs=list(out_specs) + plan.specs,
        out_shape=list(out_shape) + plan.out_shape(), scratch_shapes=list(scratch_shapes) + plan.scratch,
        compiler_params=compiler_params,
    )(*operands, *plan.arrays)
    return outs[:n_o], outs[n_o:]


def _row_tile(rows, target=256):
    best = rows
    for cand in range(16, min(rows, target) + 1, 16):
        if rows % cand == 0:
            best = cand
    return best


def _sum_blocks(name, blocks, after=None):
    rows, width = blocks.shape[-2:]
    tm = _row_tile(rows)

    def body(x_ref, *rest):
        acc = x_ref[0].astype(F32)
        for d in range(1, N_DEV):
            acc = acc + x_ref[d].astype(F32)
        rest[-1][...] = acc

    ordered = [] if after is None else [after]
    return pl.pallas_call(
        body, name=name,
        grid=(rows // tm,),
        in_specs=[pl.BlockSpec((N_DEV, tm, width), lambda i: (0, i, 0))] + [pl.BlockSpec(memory_space=pl.ANY)] * len(ordered),
        out_specs=pl.BlockSpec((tm, width), lambda i: (i, 0)),
        out_shape=jax.ShapeDtypeStruct((rows, width), F32),
        compiler_params=pltpu.CompilerParams(dimension_semantics=("parallel",)),
    )(blocks, *ordered)


def _split_plan(src_ref, land_ref, sems):
    x, y, c = _place()
    me = _block_of((x, y, c))

    def peer(r):
        return (1 - x if r & 4 else x, 1 - y if r & 2 else y, 1 - c if r & 1 else c)

    sends = [_remote_copy(src_ref.at[_block_of(peer(r))], land_ref.at[me], sems, r - 1, peer(r)) for r in range(1, N_DEV)]
    arrivals = [_remote_copy(src_ref.at[me], land_ref.at[_block_of(peer(r))], sems, r - 1, peer(r)) for r in range(1, N_DEV)]
    return sends, arrivals


def _scatter_begin(name, blocks):
    def body(src_ref, land_ref, send_sems, recv_sems, src_thru, land_thru, token_ref):
        for cp in _split_plan(src_ref, land_ref, (send_sems, recv_sems))[0]:
            cp.start()
        token_ref[...] = jnp.zeros_like(token_ref)

    hbm, sem = pl.BlockSpec(memory_space=pltpu.HBM), pl.BlockSpec(memory_space=pltpu.SEMAPHORE)
    zone = pltpu.HBM(blocks.shape, blocks.dtype)
    *handles, token = pl.pallas_call(
        body, name=name,
        in_specs=(hbm, hbm),
        out_specs=(sem, sem, hbm, hbm, pl.BlockSpec(memory_space=pltpu.VMEM)),
        out_shape=(pltpu.SemaphoreType.DMA((N_LINKS,)), pltpu.SemaphoreType.DMA((N_LINKS,)), zone, zone,
                   jax.ShapeDtypeStruct((8, SLOT), F32)),
        input_output_aliases={0: 2, 1: 3},
        compiler_params=pltpu.CompilerParams(has_side_effects=pltpu.SideEffectType.DATAFLOW_SIDE_EFFECTING),
    )(pltpu.with_memory_space_constraint(blocks, pltpu.HBM),
      pltpu.with_memory_space_constraint(lax.empty(blocks.shape, blocks.dtype), pltpu.HBM))
    return handles, token


def _scatter_end(name, handles, after):
    send_sems, recv_sems, src, zone = handles

    def body(src_ref, land_ref, send_sems, recv_sems, after_ref, src_dead, got_ref):
        sends, arrivals = _split_plan(src_ref, land_ref, (send_sems, recv_sems))
        for cp in arrivals:
            cp.wait_recv()
        for cp in sends:
            cp.wait_send()

    hbm, sem = pl.BlockSpec(memory_space=pltpu.HBM), pl.BlockSpec(memory_space=pltpu.SEMAPHORE)
    sent, landed = pl.pallas_call(
        body, name=name,
        in_specs=(hbm, hbm, sem, sem, pl.BlockSpec(memory_space=pl.ANY)),
        out_specs=(hbm, hbm),
        out_shape=(pltpu.HBM(src.shape, src.dtype), pltpu.HBM(zone.shape, zone.dtype)),
        input_output_aliases={0: 0, 1: 1},
        compiler_params=pltpu.CompilerParams(has_side_effects=pltpu.SideEffectType.DATAFLOW_SIDE_EFFECTING),
    )(src, zone, send_sems, recv_sems, after)
    me = _block_of(_place())
    return lax.dynamic_update_slice_in_dim(landed, lax.dynamic_slice_in_dim(sent, me, 1, axis=0), me, axis=0)


def _adamw_values(wv, gv, mv, vv):
    m2 = ADAM_B1 * mv + (1.0 - ADAM_B1) * gv
    v2 = ADAM_B2 * vv + (1.0 - ADAM_B2) * jnp.square(gv)
    m_hat = m2 / (1.0 - ADAM_B1 ** ADAM_STEP)
    v_hat = v2 / (1.0 - ADAM_B2 ** ADAM_STEP)
    return [-ADAM_LR * (m_hat / (jnp.sqrt(v_hat) + ADAM_EPS) + ADAM_WD * wv), m2, v2]


def _adamw(name, w, g, m, v):
    def fn(rows, consts):
        return _adamw_values(*rows), []

    return _rowwise(name, fn, [w, g, m, v], [], [(w.shape[1], F32)] * 3, tm=_row_tile(w.shape[0]))


def _sum_adamw(name, blocks, w, m, v, after=None):
    rows, width = w.shape
    tm = _row_tile(rows)

    def body(x_ref, w_ref, m_ref, v_ref, *rest):
        acc = x_ref[0].astype(F32)
        for d in range(1, N_DEV):
            acc = acc + x_ref[d].astype(F32)
        rest[-4][...] = acc
        for ref, val in zip(rest[-3:], _adamw_values(w_ref[...], acc, m_ref[...], v_ref[...])):
            ref[...] = val

    ordered = [] if after is None else [after]
    tile = pl.BlockSpec((tm, width), lambda i: (i, 0))
    return pl.pallas_call(
        body, name=name,
        grid=(rows // tm,),
        in_specs=[pl.BlockSpec((N_DEV, tm, width), lambda i: (0, i, 0))] + [tile] * 3 + [pl.BlockSpec(memory_space=pl.ANY)] * len(ordered),
        out_specs=[tile] * 4,
        out_shape=[jax.ShapeDtypeStruct((rows, width), F32)] * 4,
        compiler_params=pltpu.CompilerParams(dimension_semantics=("parallel",)),
    )(blocks, w, m, v, *ordered)


ROW = 1024
FFN_NAMES = ("ffn1_w_gate", "ffn1_w_up", "ffn1_w_down", "ffn2_w_gate", "ffn2_w_up", "ffn2_w_down")
OTHER = {"w_in": "w_in_t", "mla_w_uq": "uq_t", "mla_w_ukv": "ukv_t", "w_out": "w_out"}
BY_COLUMNS = ("ffn1_w_gate", "ffn1_w_up", "ffn2_w_gate", "ffn2_w_up", "w_in", "mla_w_uq", "mla_w_ukv")
SMALL = {
    "ffn1_pre_g": (1024, 1024), "ffn1_post_g": (1024, 1024), "mix_pre_g": (1024, 1024), "mla_q_norm_g": (256, 256),
    "mla_kv_norm_g": (128, 128), "mla_out_g": (512, 512), "gdn_a_log": (8, 128), "gdn_dt_bias": (8, 128),
    "gdn_norm_g": (64, 128), "mix_post_g": (1024, 1024), "ffn2_pre_g": (1024, 1024), "ffn2_post_g": (1024, 1024),
}
CONV_SHAPE = (GDN_CONV, 3 * N_HEADS * GDN_D)
CONV_SHARD = (GDN_CONV, CONV_SHAPE[1] // N_DEV)
CONV_LANES = CONV_SHAPE[0] * CONV_SHAPE[1]
SMALL_ROWS = 8
REDUCE_ROWS = 16


def _pack_small(vecs, conv, rows):
    parts = [_pad_lanes(vecs[n].reshape(1, -1), 0, r) for n, (_, r) in SMALL.items()]
    parts.append(conv.reshape(1, -1))
    flat = jnp.concatenate(parts, axis=1)
    return _pad_lanes(flat, 0, rows * ROW).reshape(rows, ROW)


def _unpack_small(buf):
    flat = buf.reshape(1, -1)
    out, at = {}, 0
    for n, (w, r) in SMALL.items():
        out[n] = flat[:, at:at + w]
        at += r
    return out, flat[0, at:]


def kernel(x, positions, ffn1_pre_g, ffn1_w_gate, ffn1_w_up, ffn1_w_down, ffn1_post_g, mix_pre_g, w_in, mla_q_norm_g, mla_w_uq, mla_kv_norm_g, mla_w_ukv, mla_out_g, gdn_conv_w, gdn_a_log, gdn_dt_bias, gdn_norm_g, w_out, mix_post_g, ffn2_pre_g, ffn2_w_gate, ffn2_w_up, ffn2_w_down, ffn2_post_g, loss_target, m_ffn1_pre_g, m_ffn1_w_gate, m_ffn1_w_up, m_ffn1_w_down, m_ffn1_post_g, m_mix_pre_g, m_w_in, m_mla_q_norm_g, m_mla_w_uq, m_mla_kv_norm_g, m_mla_w_ukv, m_mla_out_g, m_gdn_conv_w, m_gdn_a_log, m_gdn_dt_bias, m_gdn_norm_g, m_w_out, m_mix_post_g, m_ffn2_pre_g, m_ffn2_w_gate, m_ffn2_w_up, m_ffn2_w_down, m_ffn2_post_g, v_ffn1_pre_g, v_ffn1_w_gate, v_ffn1_w_up, v_ffn1_w_down, v_ffn1_post_g, v_mix_pre_g, v_w_in, v_mla_q_norm_g, v_mla_w_uq, v_mla_kv_norm_g, v_mla_w_ukv, v_mla_out_g, v_gdn_conv_w, v_gdn_a_log, v_gdn_dt_bias, v_gdn_norm_g, v_w_out, v_mix_post_g, v_ffn2_pre_g, v_ffn2_w_gate, v_ffn2_w_up, v_ffn2_w_down, v_ffn2_post_g):
    given = dict(locals())
    order = ["ffn1_pre_g", "ffn1_w_gate", "ffn1_w_up", "ffn1_w_down", "ffn1_post_g", "mix_pre_g", "w_in", "mla_q_norm_g",
             "mla_w_uq", "mla_kv_norm_g", "mla_w_ukv", "mla_out_g", "gdn_conv_w", "gdn_a_log", "gdn_dt_bias", "gdn_norm_g",
             "w_out", "mix_post_g", "ffn2_pre_g", "ffn2_w_gate", "ffn2_w_up", "ffn2_w_down", "ffn2_post_g"]
    assert sorted(order) == sorted(list(FFN_NAMES) + list(OTHER) + list(SMALL) + ["gdn_conv_w"])

    def drop_depth(a):
        return a[0] if a.ndim == 3 else a

    wts = {n: drop_depth(given[n]) for n in order}
    mom = {n: drop_depth(given["m_" + n]) for n in order}
    var = {n: drop_depth(given["v_" + n]) for n in order}
    me = _block_of(_place())

    def wire(n):
        return (wts[n].T if n in BY_COLUMNS else wts[n]).astype(BF16)

    (ffn1,) = _exchange("gather_first", _Gather([jnp.stack([wire(n) for n in FFN_NAMES[:3]])]))
    mid = _Gather([wire(n) for n in ("w_in", "mla_w_uq", "mla_w_ukv")] + [wts["gdn_conv_w"]])
    late = _Gather([jnp.stack([wire(n) for n in FFN_NAMES[3:]]), wire("w_out")])
    full = {n: wts[n] for n in SMALL}
    full["ffn1"] = ffn1

    dx, grads, landed, begun, token = _local_step(x[0], positions[0], loss_target[0], full, mid, late)

    grad, outs = {}, {"delta": {}, "new_m": {}, "new_v": {}}

    def finish(n, blocks, after=None):
        flip = n in BY_COLUMNS and wts[n].shape[1] % SLOT != 0
        turn = (lambda a: a.T) if flip else (lambda a: a)
        if n in BY_COLUMNS and not flip:
            grad[n] = _sum_blocks("sum_" + n, blocks, after=after).T
            new = _adamw("adamw_" + n, wts[n], grad[n], mom[n], var[n])
        else:
            total, *new = _sum_adamw("update_" + n, blocks, turn(wts[n]), turn(mom[n]), turn(var[n]), after=after)
            grad[n] = turn(total)
        outs["delta"][n], outs["new_m"][n], outs["new_v"][n] = (turn(a) for a in new)
        return new[2]

    for n, blocks in landed.items():
        token = finish(n, blocks, after=token)
    small_handles = begun.pop("small")
    for n, handles in begun.items():
        token = finish(n, _scatter_end("scatter_" + n + "_end", handles, after=token))

    small_sum = _sum_blocks("sum_small", _scatter_end("reduce_small_end", small_handles, after=token))
    loss = small_sum[REDUCE_ROWS - 1, ROW - 1]
    small_grad, conv_grad_full = _unpack_small(small_sum)
    grad.update(small_grad)
    grad["gdn_conv_w"] = lax.dynamic_slice(conv_grad_full[:CONV_LANES].reshape(CONV_SHAPE), (0, me * CONV_SHARD[1]), CONV_SHARD)
    outs["grad"] = grad
    small = [_pack_small(s, s["gdn_conv_w"].reshape(-1), SMALL_ROWS) for s in (wts, grad, mom, var)]
    for kind, s in zip(("delta", "new_m", "new_v"), _adamw("adamw_small", *small)):
        vecs, conv = _unpack_small(s)
        outs[kind].update(vecs)
        outs[kind]["gdn_conv_w"] = conv[:CONV_SHARD[0] * CONV_SHARD[1]].reshape(CONV_SHARD)
    result = [loss, dx[None]]
    for kind in ("grad", "delta", "new_m", "new_v"):
        result += [outs[kind][n].reshape(given[n].shape) for n in order]
    return tuple(result)
```

```python
import jax
import jax.numpy as jnp
from jax import lax
from jax.experimental import pallas as pl
from jax.experimental.pallas import tpu as pltpu

F32 = jnp.float32
BF16 = jnp.bfloat16
HI = lax.Precision.HIGH

N_DEV = 8
N_HEADS = 8
SLOT = 128
MLA_Q_RANK = 256
MLA_KV_RANK = 128
MLA_NOPE = 64
MLA_ROPE = 32
MLA_V = 64
GDN_D = 64
GDN_CONV = 4
GDN_CHUNK = 64
ROPE_THETA = 10000.0
EPS = 1e-6
ADAM_LR, ADAM_B1, ADAM_B2, ADAM_EPS, ADAM_WD, ADAM_STEP = 0.001, 0.9, 0.999, 1e-08, 0.01, 10


def _dot(a, b, ca, cb, precision=None):
    lead = a.ndim - 2
    batch = tuple(range(lead))
    return lax.dot_general(a, b, (((lead + ca,), (lead + cb,)), (batch, batch)), precision=precision,
                           preferred_element_type=F32)


def _nn(a, b, precision=None):
    return _dot(a, b, 1, 0, precision)


def _nt(a, b, precision=None):
    return _dot(a, b, 1, 1, precision)


def _tn(a, b, precision=None):
    return _dot(a, b, 0, 0, precision)


def _sigmoid(x):
    return 1.0 / (1.0 + jnp.exp(-x))


def _silu(x):
    return x * _sigmoid(x)


def _rms(x, g, n):
    ms = jnp.sum(x * x, axis=-1, keepdims=True) * (1.0 / n)
    return x * lax.rsqrt(ms + EPS) * g


def _chunk_masks():
    c = GDN_CHUNK
    i = lax.broadcasted_iota(jnp.int32, (c, c), 0)
    j = lax.broadcasted_iota(jnp.int32, (c, c), 1)
    lower = i >= j
    strict = i > j
    eye = (i == j).astype(F32)
    blocks = []
    b = 1
    while b < c:
        same = (i // (2 * b)) == (j // (2 * b))
        blocks.append(same & ((i % (2 * b)) >= b) & ((j % (2 * b)) < b))
        b *= 2
    return lower, strict, eye, blocks


def _unit_lower_inverse(low, eye, blocks):
    t = eye - jnp.where(blocks[0], low, 0.0)
    for m in blocks[1:]:
        lo = jnp.where(m, low, 0.0)
        t = t - _nn(t, _nn(lo, t, HI), HI)
    return t


@jax.custom_vjp
def _known_inverse(low, tinv):
    return tinv


def _known_inverse_fwd(low, tinv):
    return tinv, tinv


def _known_inverse_bwd(tinv, dt):
    return -_tn(tinv, _nt(dt, tinv, HI), HI), jnp.zeros_like(tinv)


_known_inverse.defvjp(_known_inverse_fwd, _known_inverse_bwd)

_PRODUCTS = {"nn": _nn, "nt": _nt, "tn": _tn}


@jax.custom_vjp
def _known_nn(a, b, c):
    return c


@jax.custom_vjp
def _known_nt(a, b, c):
    return c


@jax.custom_vjp
def _known_tn(a, b, c):
    return c


def _known_fwd(a, b, c):
    return c, (a, b, c)


_known_nn.defvjp(_known_fwd, lambda r, dc: (_nt(dc, r[1], HI), _tn(r[0], dc, HI), jnp.zeros_like(r[2])))
_known_nt.defvjp(_known_fwd, lambda r, dc: (_nn(dc, r[1], HI), _tn(dc, r[0], HI), jnp.zeros_like(r[2])))
_known_tn.defvjp(_known_fwd, lambda r, dc: (_nt(r[1], dc, HI), _nn(r[0], dc, HI), jnp.zeros_like(r[2])))
_KNOWN = {"nn": _known_nn, "nt": _known_nt, "tn": _known_tn}
GDN_PRODUCTS = 8
GDN_KEPT = 2 + GDN_PRODUCTS


def _gdn_chunk(q, k, v, gc, bb, s, masks, known=None):
    lower, strict, eye, blocks = masks
    made = []

    def product(kind, a, b):
        c = _PRODUCTS[kind](a, b, HI) if known is None else _KNOWN[kind](a, b, known[1 + len(made)])
        made.append(c)
        return c

    qs = q * (GDN_D ** -0.5)
    gct = jnp.swapaxes(gc, -1, -2)
    decay = jnp.exp(jnp.where(lower, gc - gct, -1e30))
    kb = k * bb
    low = jnp.where(strict, product("nt", kb, k) * decay, 0.0)
    tinv = _unit_lower_inverse(low, eye, blocks) if known is None else _known_inverse(low, known[0])
    eg = jnp.exp(gc)
    w = product("nn", tinv, kb * eg)
    u = product("nn", tinv, v * bb)
    attn = product("nt", qs, k) * decay
    last = lax.broadcasted_iota(jnp.int32, gc.shape[-2:], 0) == GDN_CHUNK - 1
    g_end = jnp.sum(jnp.where(last, gc, 0.0), axis=-2, keepdims=True)
    k_dec = k * jnp.exp(g_end - gc)
    v_new = u - product("nn", w, s)
    o = product("nn", qs * eg, s) + product("nn", attn, v_new)
    s_new = s * jnp.exp(g_end) + product("tn", k_dec, v_new)
    assert len(made) == GDN_PRODUCTS
    return o, s_new, [tinv] + made


GDN_GROUP = 8
GDN_GROUPS = N_HEADS // GDN_GROUP


def _group_heads(ref):
    return jnp.stack([ref[:, pl.ds(j * SLOT, GDN_D)] for j in range(GDN_GROUP)])


def _ungroup_heads(ref, val):
    pad = jnp.zeros((GDN_CHUNK, SLOT - GDN_D), F32)
    for j in range(GDN_GROUP):
        ref[:, pl.ds(j * SLOT, GDN_D)] = val[j]
        ref[:, pl.ds(j * SLOT + GDN_D, SLOT - GDN_D)] = pad


def _gdn_fwd(qkv, gb, bb, carry=None):
    t = qkv.shape[0]
    n_chunks = t // GDN_CHUNK
    d = GDN_D

    def body(q_ref, k_ref, v_ref, g_ref, b_ref, o_ref, keep_ref, s_ref):
        @pl.when(pl.program_id(1) == 0)
        def _():
            s_ref[...] = jnp.zeros_like(s_ref)

        s = s_ref[...]
        keep_ref[:, 0, 0] = s
        o, s_new, made = _gdn_chunk(*[_group_heads(r) for r in (q_ref, k_ref, v_ref, g_ref, b_ref)], s, _chunk_masks())
        for i, val in enumerate(made):
            keep_ref[:, 0, 1 + i] = val
        s_ref[...] = s_new
        _ungroup_heads(o_ref, o)

    def spec(kind=0):
        return pl.BlockSpec((GDN_CHUNK, GDN_GROUP * SLOT), lambda h, n: (n, kind * GDN_GROUPS + h))

    return _call_carrying(
        body, carry, (qkv, qkv, qkv, gb, bb), name="gdn_fwd",
        grid=(GDN_GROUPS, n_chunks),
        in_specs=[spec(0), spec(1), spec(2), spec(), spec()],
        out_specs=[spec(), pl.BlockSpec((GDN_GROUP, 1, GDN_KEPT, d, d), lambda h, n: (h, n, 0, 0, 0))],
        out_shape=[jax.ShapeDtypeStruct((t, N_HEADS * SLOT), F32), jax.ShapeDtypeStruct((N_HEADS, n_chunks, GDN_KEPT, d, d), F32)],
        scratch_shapes=[pltpu.VMEM((GDN_GROUP, d, d), F32)],
        compiler_params=pltpu.CompilerParams(dimension_semantics=("arbitrary", "arbitrary")),
    )


def _gdn_bwd(qkv, gb, bb, keep, do, carry=None):
    t = qkv.shape[0]
    n_chunks = t // GDN_CHUNK
    d = GDN_D

    def body(q_ref, k_ref, v_ref, g_ref, b_ref, keep_ref, do_ref, dqkv_ref, dg_ref, db_ref, ds_ref):
        @pl.when(pl.program_id(1) == 0)
        def _():
            ds_ref[...] = jnp.zeros_like(ds_ref)

        masks = _chunk_masks()
        known = [keep_ref[:, 0, 1 + i] for i in range(GDN_KEPT - 1)]
        _, pull = jax.vjp(lambda *a: _gdn_chunk(*a, masks, known)[:2],
                          *[_group_heads(r) for r in (q_ref, k_ref, v_ref, g_ref, b_ref)], keep_ref[:, 0, 0])
        dq, dk, dv, dg, db, ds = pull((_group_heads(do_ref), ds_ref[...]))
        ds_ref[...] = ds
        for i, val in enumerate((dq, dk, dv)):
            _ungroup_heads(dqkv_ref.at[i], val)
        _ungroup_heads(dg_ref, dg)
        _ungroup_heads(db_ref, db)

    def spec(kind=0):
        return pl.BlockSpec((GDN_CHUNK, GDN_GROUP * SLOT), lambda h, n: (n_chunks - 1 - n, kind * GDN_GROUPS + h))

    return _call_carrying(
        body, carry, (qkv, qkv, qkv, gb, bb, keep, do), name="gdn_bwd",
        grid=(GDN_GROUPS, n_chunks),
        in_specs=[spec(0), spec(1), spec(2), spec(), spec(),
                  pl.BlockSpec((GDN_GROUP, 1, GDN_KEPT, d, d), lambda h, n: (h, n_chunks - 1 - n, 0, 0, 0)), spec()],
        out_specs=[pl.BlockSpec((3, GDN_CHUNK, GDN_GROUP * SLOT), lambda h, n: (0, n_chunks - 1 - n, h)), spec(), spec()],
        out_shape=[jax.ShapeDtypeStruct((3, t, N_HEADS * SLOT), F32)] + [jax.ShapeDtypeStruct((t, N_HEADS * SLOT), F32)] * 2,
        scratch_shapes=[pltpu.VMEM((GDN_GROUP, d, d), F32)],
        compiler_params=pltpu.CompilerParams(dimension_semantics=("arbitrary", "arbitrary")),
    )


def _rowwise(name, fn, rows, consts, outs, sums=(), tm=512):
    rows = [x if isinstance(x, tuple) else (x, x.shape[1], 0) for x in rows]
    t = rows[0][0].shape[0]
    tm = min(tm, t)
    steps = t // tm
    n_r, n_c, n_o, n_s = len(rows), len(consts), len(outs), len(sums)

    def window(width, block):
        return pl.BlockSpec((tm, width), lambda i: (i, block))

    def body(*refs):
        r, c = refs[:n_r], refs[n_r:n_r + n_c]
        o, s = refs[n_r + n_c:n_r + n_c + n_o], refs[n_r + n_c + n_o:]
        vals, tot = fn([x[...] for x in r], [x[...] for x in c])
        for ref, val in zip(o, vals):
            ref[...] = val.astype(ref.dtype)
        if n_s:
            @pl.when(pl.program_id(0) == 0)
            def _():
                for ref in s:
                    ref[...] = jnp.zeros_like(ref)

            for ref, val in zip(s, tot):
                ref[...] += val

    return pl.pallas_call(
        body, name=name,
        grid=(steps,),
        in_specs=[window(w, b) for _, w, b in rows] + [pl.BlockSpec(x.shape, lambda i: (0, 0)) for x in consts],
        out_specs=[pl.BlockSpec((tm, w), lambda i: (i, 0)) for w, _ in outs]
        + [pl.BlockSpec((1, w), lambda i: (0, 0)) for w in sums],
        out_shape=[jax.ShapeDtypeStruct((t, w), dt) for w, dt in outs]
        + [jax.ShapeDtypeStruct((1, w), F32) for w in sums],
        compiler_params=pltpu.CompilerParams(dimension_semantics=("arbitrary",)),
    )(*[x for x, _, _ in rows], *consts)


def _tile(dim, target):
    if dim <= target:
        return dim
    best = None
    for cand in range(128, target + 1, 128):
        if dim % cand == 0:
            best = cand
    assert best is not None, (dim, target)
    return best


def _matmul(name, a, b, mode, out_dtype=F32, tm=1024, tn=1024, tk=2048, after=None):
    if mode == "nn":
        (m, k), n = a.shape, b.shape[1]
    elif mode == "nt":
        (m, k), n = a.shape, b.shape[0]
    else:
        (k, m), n = a.shape, b.shape[1]
    tm, tn, tk = _tile(m, tm), _tile(n, tn), _tile(k, tk)
    k_steps = k // tk
    product = {"nn": _nn, "nt": _nt, "tn": _tn}[mode]

    def body(a_ref, b_ref, *rest):
        o_ref, acc_ref = rest[-2:]
        part = product(a_ref[...].astype(BF16), b_ref[...].astype(BF16))
        if k_steps == 1:
            o_ref[...] = part.astype(o_ref.dtype)
        else:
            kk = pl.program_id(2)

            @pl.when(kk == 0)
            def _():
                acc_ref[...] = part

            @pl.when(kk > 0)
            def _():
                acc_ref[...] += part

            @pl.when(kk == k_steps - 1)
            def _():
                o_ref[...] = acc_ref[...].astype(o_ref.dtype)

    a_spec = pl.BlockSpec((tk, tm), lambda i, j, kk: (kk, i)) if mode == "tn" else pl.BlockSpec((tm, tk), lambda i, j, kk: (i, kk))
    b_spec = pl.BlockSpec((tn, tk), lambda i, j, kk: (j, kk)) if mode == "nt" else pl.BlockSpec((tk, tn), lambda i, j, kk: (kk, j))
    ordered = [] if after is None else [after]
    return pl.pallas_call(
        body, name=name,
        grid=(m // tm, n // tn, k_steps),
        in_specs=[a_spec, b_spec] + [pl.BlockSpec(memory_space=pl.ANY)] * len(ordered),
        out_specs=pl.BlockSpec((tm, tn), lambda i, j, kk: (i, j)),
        out_shape=jax.ShapeDtypeStruct((m, n), out_dtype),
        scratch_shapes=[pltpu.VMEM((tm, tn) if k_steps > 1 else (8, 128), F32)],
        compiler_params=pltpu.CompilerParams(dimension_semantics=("parallel", "parallel", "arbitrary")),
    )(a, b, *ordered)


FFN_TM = 512
FFN_BWD_TM = 256
FFN_BLOCKS = 4
FFN_GATE, FFN_UP, FFN_DOWN = 0, 1, 2


def _ffn_weight_specs(ffn_w, first):
    _, _, rows, dm = ffn_w.shape

    def spec(k):
        return pl.BlockSpec((FFN_BLOCKS, None, rows, dm), lambda i, j: (j, first + k, 0, 0))

    return [spec(FFN_GATE), spec(FFN_UP), spec(FFN_DOWN)], FFN_BLOCKS * rows


def _ffn_fwd(name, x, g_pre, ffn_w, first, g_post, carry=None):
    t, dm = x.shape
    tm = min(FFN_TM, t)
    w_specs, tf = _ffn_weight_specs(ffn_w, first)
    f_steps = N_DEV // FFN_BLOCKS

    def body(x_ref, gpre_ref, wg_ref, wu_ref, wd_ref, gpost_ref, h_ref, y_ref, hg_ref, hu_ref, a_ref, xn_ref, acc_ref):
        j = pl.program_id(1)

        @pl.when(j == 0)
        def _():
            xn_ref[...] = _rms(x_ref[...], gpre_ref[...], dm).astype(BF16)
            acc_ref[...] = jnp.zeros_like(acc_ref)

        xn = xn_ref[...]
        wg, wu, wd = (r[...].reshape(tf, dm) for r in (wg_ref, wu_ref, wd_ref))
        hg, hu = _nt(xn, wg), _nt(xn, wu)
        hg_ref[...] = hg.astype(BF16)
        hu_ref[...] = hu.astype(BF16)
        a = (_silu(hg) * hu).astype(BF16)
        a_ref[...] = a
        acc_ref[...] += _nn(a, wd)

        @pl.when(j == f_steps - 1)
        def _():
            h = acc_ref[...]
            h_ref[...] = h
            y_ref[...] = x_ref[...] + 0.5 * _rms(h, gpost_ref[...], dm)

    row = pl.BlockSpec((tm, dm), lambda i, j: (i, 0))
    vec = pl.BlockSpec((1, dm), lambda i, j: (0, 0))
    wide = pl.BlockSpec((tm, tf), lambda i, j: (i, j))
    return _call_carrying(
        body, carry, (x, g_pre, ffn_w, ffn_w, ffn_w, g_post), name=name,
        grid=(t // tm, f_steps),
        in_specs=[row, vec, *w_specs, vec],
        out_specs=[row, row, wide, wide, wide],
        out_shape=[jax.ShapeDtypeStruct((t, dm), F32)] * 2 + [jax.ShapeDtypeStruct((t, f_steps * tf), BF16)] * 3,
        scratch_shapes=[pltpu.VMEM((tm, dm), BF16), pltpu.VMEM((tm, dm), F32)],
        compiler_params=pltpu.CompilerParams(dimension_semantics=("arbitrary", "arbitrary")),
    )


def _ffn_bwd(name, x, h, hg, hu, dy, g_pre, ffn_w, first, g_post, carry=None):
    t, dm = x.shape
    tm = min(FFN_BWD_TM, t)
    w_specs, tf = _ffn_weight_specs(ffn_w, first)
    f_steps = N_DEV // FFN_BLOCKS
    f = f_steps * tf

    def post(hv, g):
        return 0.5 * _rms(hv, g, dm)

    def pre(xv, g):
        return _rms(xv, g, dm)

    def body(x_ref, h_ref, dy_ref, hg_ref, hu_ref, gpre_ref, wg_ref, wu_ref, wd_ref, gpost_ref,
             dx_ref, xn_ref, dh_ref, dhg_ref, dhu_ref, dgpre_ref, dgpost_ref, acc_ref):
        i, j = pl.program_id(0), pl.program_id(1)

        @pl.when((i == 0) & (j == 0))
        def _():
            dgpre_ref[...] = jnp.zeros_like(dgpre_ref)
            dgpost_ref[...] = jnp.zeros_like(dgpost_ref)

        @pl.when(j == 0)
        def _():
            xn_ref[...] = pre(x_ref[...], gpre_ref[...]).astype(BF16)
            _, pull = jax.vjp(post, h_ref[...], gpost_ref[...])
            dh, dg = pull(dy_ref[...])
            dh_ref[...] = dh.astype(BF16)
            dgpost_ref[...] += dg
            acc_ref[...] = jnp.zeros_like(acc_ref)

        wg, wu, wd = (r[...].reshape(tf, dm) for r in (wg_ref, wu_ref, wd_ref))
        hg, hu = hg_ref[...].astype(F32), hu_ref[...].astype(F32)
        da = _nt(dh_ref[...], wd)
        sig = _sigmoid(hg)
        act = hg * sig
        dhu = (da * act).astype(BF16)
        dhg = (da * hu * (sig * (1.0 + hg * (1.0 - sig)))).astype(BF16)
        dhg_ref[...] = dhg
        dhu_ref[...] = dhu
        acc_ref[...] += _nn(dhg, wg) + _nn(dhu, wu)

        @pl.when(j == f_steps - 1)
        def _():
            _, pull = jax.vjp(pre, x_ref[...], gpre_ref[...])
            dx, dg = pull(acc_ref[...])
            dx_ref[...] = dy_ref[...] + dx
            dgpre_ref[...] += dg

    row = pl.BlockSpec((tm, dm), lambda i, j: (i, 0))
    vec = pl.BlockSpec((1, dm), lambda i, j: (0, 0))
    wide = pl.BlockSpec((tm, tf), lambda i, j: (i, j))
    return _call_carrying(
        body, carry, (x, h, dy, hg, hu, g_pre, ffn_w, ffn_w, ffn_w, g_post), name=name,
        grid=(t // tm, f_steps),
        in_specs=[row, row, row, wide, wide, vec, *w_specs, vec],
        out_specs=[row, row, row, wide, wide, vec, vec],
        out_shape=[jax.ShapeDtypeStruct((t, dm), F32), jax.ShapeDtypeStruct((t, dm), BF16), jax.ShapeDtypeStruct((t, dm), BF16),
                   jax.ShapeDtypeStruct((t, f), BF16), jax.ShapeDtypeStruct((t, f), BF16),
                   jax.ShapeDtypeStruct((1, dm), F32), jax.ShapeDtypeStruct((1, dm), F32)],
        scratch_shapes=[pltpu.VMEM((tm, dm), F32)],
        compiler_params=pltpu.CompilerParams(dimension_semantics=("arbitrary", "arbitrary")),
    )


ATT_T = 512
ATT_GROUP = 4
ATT_GROUP_FWD = 8
ATT_SCALE = (MLA_NOPE + MLA_ROPE) ** -0.5


def _stack_slots(ref, group):
    return jnp.stack([ref[:, pl.ds(j * SLOT, SLOT)] for j in range(group)])


def _unstack_slots(ref, val):
    for j in range(val.shape[0]):
        ref[:, pl.ds(j * SLOT, SLOT)] = val[j].astype(ref.dtype)


def _scores(q, k, diagonal):
    s = _nt(q, k) * ATT_SCALE
    if diagonal:
        row = lax.broadcasted_iota(jnp.int32, s.shape[1:], 0)
        col = lax.broadcasted_iota(jnp.int32, s.shape[1:], 1)
        s = jnp.where(col <= row, s, -1e30)
    return s


def _attn_pairs(steps, q_major):
    pairs = ([(qi, ki) for qi in range(steps) for ki in range(qi + 1)] if q_major
             else [(qi, ki) for ki in range(steps) for qi in range(ki, steps)])
    return jnp.array([p[0] for p in pairs], jnp.int32), jnp.array([p[1] for p in pairs], jnp.int32)


def _attn_specs(tile, group):
    width = group * SLOT
    return (pl.BlockSpec((tile, width), lambda h, p, qt, kt: (qt[p], h)),
            pl.BlockSpec((tile, width), lambda h, p, qt, kt: (kt[p], h)))


def _attn_fwd(q, k, v):
    t = q.shape[0]
    tile = min(ATT_T, t)
    steps = t // tile
    g = ATT_GROUP_FWD

    strip = min(SLOT, tile)

    def body(qt_ref, kt_ref, q_ref, k_ref, v_ref, o_ref, lse_ref, m_ref, l_ref, alpha_ref, acc_ref, s_ref, p_ref):
        qi, ki = qt_ref[pl.program_id(1)], kt_ref[pl.program_id(1)]

        @pl.when(ki == 0)
        def _():
            m_ref[...] = jnp.full_like(m_ref, -1e30)
            l_ref[...] = jnp.zeros_like(l_ref)
            acc_ref[...] = jnp.zeros_like(acc_ref)

        def step(diagonal):
            s_ref[...] = _nt(_stack_slots(k_ref, g), _stack_slots(q_ref, g))
            for j in range(tile // strip):
                c = pl.ds(j * strip, strip)
                s = s_ref[:, :, c] * ATT_SCALE
                if diagonal:
                    key = lax.broadcasted_iota(jnp.int32, s.shape[1:], 0)
                    query = lax.broadcasted_iota(jnp.int32, s.shape[1:], 1) + j * strip
                    s = jnp.where(key <= query, s, -1e30)
                m_old = m_ref[:, :, c]
                m_new = jnp.maximum(m_old, jnp.max(s, axis=1, keepdims=True))
                p = jnp.exp(s - m_new)
                alpha = jnp.exp(m_old - m_new)
                l_ref[:, :, c] = alpha * l_ref[:, :, c] + jnp.sum(p, axis=1, keepdims=True)
                alpha_ref[:, :, c] = alpha
                m_ref[:, :, c] = m_new
                p_ref[:, :, c] = p.astype(BF16)
            acc_ref[...] = acc_ref[...] * alpha_ref[...] + _tn(_stack_slots(v_ref, g), p_ref[...])

        @pl.when(ki < qi)
        def _():
            step(False)

        @pl.when(ki == qi)
        def _():
            step(True)
            out = acc_ref[...] / l_ref[...]
            lse = jnp.broadcast_to(m_ref[...] + jnp.log(l_ref[...]), out.shape)
            for j in range(g):
                o_ref[:, pl.ds(j * SLOT, SLOT)] = out[j].T
                lse_ref[:, pl.ds(j * SLOT, SLOT)] = lse[j].T

    q_spec, k_spec = _attn_specs(tile, g)
    tables = _attn_pairs(steps, True)
    return pl.pallas_call(
        body, name="attn_fwd",
        grid_spec=pltpu.PrefetchScalarGridSpec(
            num_scalar_prefetch=2, grid=(N_HEADS // g, tables[0].shape[0]),
            in_specs=[q_spec, k_spec, k_spec], out_specs=[q_spec, q_spec],
            scratch_shapes=[pltpu.VMEM((g, 1, tile), F32), pltpu.VMEM((g, 1, tile), F32), pltpu.VMEM((g, 1, tile), F32),
                            pltpu.VMEM((g, SLOT, tile), F32), pltpu.VMEM((g, tile, tile), F32), pltpu.VMEM((g, tile, tile), BF16)]),
        out_shape=[jax.ShapeDtypeStruct((t, N_HEADS * SLOT), F32)] * 2,
        compiler_params=pltpu.CompilerParams(dimension_semantics=("parallel", "arbitrary")),
    )(*tables, q, k, v)


def _attn_grad_scores(q, k, v, do, lse_ref, delta_ref, diagonal):
    g = ATT_GROUP
    p = jnp.exp(_scores(q, k, diagonal) - _stack_slots(lse_ref, g)[:, :, 0:1])
    dp = _nt(do, v)
    return p, p * (dp - _stack_slots(delta_ref, g)[:, :, 0:1]) * ATT_SCALE


def _attn_bwd(q, k, v, do, lse, delta):
    t = q.shape[0]
    tile = min(ATT_T, t)
    steps = t // tile
    g = ATT_GROUP

    def body(qt_ref, kt_ref, q_ref, k_ref, v_ref, do_ref, lse_ref, delta_ref, dq_ref, dk_ref, dv_ref, dk_acc, dv_acc):
        qi, ki = qt_ref[pl.program_id(1)], kt_ref[pl.program_id(1)]

        @pl.when(pl.program_id(1) == 0)
        def _():
            dq_ref[...] = jnp.zeros_like(dq_ref)

        def step(diagonal):
            qq, kk = _stack_slots(q_ref, g), _stack_slots(k_ref, g)
            do_b = _stack_slots(do_ref, g).astype(BF16)
            p, ds = _attn_grad_scores(qq, kk, _stack_slots(v_ref, g), do_b, lse_ref, delta_ref, diagonal)
            ds = ds.astype(BF16)
            dv_acc[...] += _tn(p.astype(BF16), do_b)
            dk_acc[...] += _tn(ds, qq)
            dq = _nn(ds, kk)
            rows = pl.ds(pl.multiple_of(qi * tile, tile), tile)
            for j in range(g):
                dq_ref[rows, pl.ds(j * SLOT, SLOT)] += dq[j]

        @pl.when(qi == ki)
        def _():
            dk_acc[...] = jnp.zeros_like(dk_acc)
            dv_acc[...] = jnp.zeros_like(dv_acc)
            step(True)

        @pl.when(qi > ki)
        def _():
            step(False)

        @pl.when(qi == steps - 1)
        def _():
            _unstack_slots(dk_ref, dk_acc[...])
            _unstack_slots(dv_ref, dv_acc[...])

    q_spec, k_spec = _attn_specs(tile, g)
    tables = _attn_pairs(steps, False)
    return pl.pallas_call(
        body, name="attn_bwd",
        grid_spec=pltpu.PrefetchScalarGridSpec(
            num_scalar_prefetch=2, grid=(N_HEADS // g, tables[0].shape[0]),
            in_specs=[q_spec, k_spec, k_spec, q_spec, q_spec, q_spec],
            out_specs=[pl.BlockSpec((t, g * SLOT), lambda h, p, qt, kt: (0, h)), k_spec, k_spec],
            scratch_shapes=[pltpu.VMEM((g, tile, SLOT), F32), pltpu.VMEM((g, tile, SLOT), F32)]),
        out_shape=[jax.ShapeDtypeStruct((t, N_HEADS * SLOT), F32)] * 3,
        compiler_params=pltpu.CompilerParams(dimension_semantics=("parallel", "arbitrary")),
    )(*tables, q, k, v, do, lse, delta)


CONV_PAD = 8


def _fill_padded(ref, val):
    t = val.shape[0]
    zeros = jnp.zeros((CONV_PAD, val.shape[1]), val.dtype)
    ref[pl.ds(0, CONV_PAD)] = zeros
    ref[pl.ds(CONV_PAD + t, CONV_PAD)] = zeros
    ref[pl.ds(CONV_PAD, t)] = val


def _shifted(ref, s):
    return ref[pl.ds(CONV_PAD - s, ref.shape[0] - 2 * CONV_PAD)]


def _l2norm(x):
    return x * lax.rsqrt(jnp.sum(x * x, axis=-1, keepdims=True) + EPS)


def _conv_pre(x_pad, w):
    y = w[GDN_CONV - 1:GDN_CONV, :] * _shifted(x_pad, 0)
    for s in range(1, GDN_CONV):
        y = y + w[GDN_CONV - 1 - s:GDN_CONV - s, :] * _shifted(x_pad, s)
    return y


def _gdn_conv_fwd(x, w):
    t, width = x.shape

    def body(x_ref, w_ref, o_ref, x_pad):
        _fill_padded(x_pad, x_ref[...])
        act = _silu(_conv_pre(x_pad, w_ref[...]))
        normed = pl.program_id(0) < 2 * N_HEADS
        o_ref[...] = jnp.where(normed, _l2norm(act), act)

    return pl.pallas_call(
        body, name="gdn_conv_fwd",
        grid=(width // SLOT,),
        in_specs=[pl.BlockSpec((t, SLOT), lambda j: (0, j)), pl.BlockSpec((GDN_CONV, SLOT), lambda j: (0, j))],
        out_specs=pl.BlockSpec((t, SLOT), lambda j: (0, j)),
        out_shape=jax.ShapeDtypeStruct((t, width), F32),
        scratch_shapes=[pltpu.VMEM((t + 2 * CONV_PAD, SLOT), F32)],
        compiler_params=pltpu.CompilerParams(dimension_semantics=("parallel",)),
    )(x, w)


def _gdn_conv_bwd(x, w, dout):
    t, width = x.shape

    def body(x_ref, w_ref, do_ref, dx_ref, dw_ref, x_pad, dy_pad):
        wv = w_ref[...]
        _fill_padded(x_pad, x_ref[...])
        y = _conv_pre(x_pad, wv)
        sig = _sigmoid(y)
        act = y * sig
        _, pull = jax.vjp(_l2norm, act)
        normed = pl.program_id(0) < 2 * N_HEADS
        dact = jnp.where(normed, pull(do_ref[0])[0], do_ref[0])
        dy = dact * (sig * (1.0 + y * (1.0 - sig)))
        _fill_padded(dy_pad, dy)
        dx = wv[GDN_CONV - 1:GDN_CONV, :] * dy
        for s in range(1, GDN_CONV):
            dx = dx + wv[GDN_CONV - 1 - s:GDN_CONV - s, :] * _shifted(dy_pad, -s)
        dx_ref[...] = dx.astype(BF16)
        for s in range(GDN_CONV):
            dw_ref[GDN_CONV - 1 - s:GDN_CONV - s, :] = jnp.sum(dy * _shifted(x_pad, s), axis=0, keepdims=True)

    col = pl.BlockSpec((t, SLOT), lambda j: (0, j))
    tap = pl.BlockSpec((GDN_CONV, SLOT), lambda j: (0, j))
    return pl.pallas_call(
        body, name="gdn_conv_bwd",
        grid=(width // SLOT,),
        in_specs=[col, tap, pl.BlockSpec((1, t, SLOT), lambda j: (j // N_HEADS, 0, j % N_HEADS))],
        out_specs=[col, tap],
        out_shape=[jax.ShapeDtypeStruct((t, width), BF16), jax.ShapeDtypeStruct((GDN_CONV, width), F32)],
        scratch_shapes=[pltpu.VMEM((t + 2 * CONV_PAD, SLOT), F32)] * 2,
        compiler_params=pltpu.CompilerParams(dimension_semantics=("parallel",)),
    )(x, w, dout)


def _softplus(x):
    e = jnp.exp(-jnp.abs(x))
    u = 1.0 + e
    log1p = jnp.where(u == 1.0, e, jnp.log(u) * e / jnp.where(u == 1.0, 1.0, u - 1.0))
    return jnp.maximum(x, 0.0) + log1p


def _chunk_running_sum(x, reverse=False):
    tm = x.shape[0]
    at = lax.broadcasted_iota(jnp.int32, x.shape, 0) % GDN_CHUNK
    step = 1
    while step < GDN_CHUNK:
        if reverse:
            x = x + jnp.where(at < GDN_CHUNK - step, pltpu.roll(x, tm - step, 0), 0.0)
        else:
            x = x + jnp.where(at >= step, pltpu.roll(x, step, 0), 0.0)
        step *= 2
    return x


def _gates_fwd(ab, a_log, dt_bias):
    def fn(rows, consts):
        (abv,), (alog, dtb) = rows, consts
        g = _chunk_running_sum(-jnp.exp(alog) * _softplus(abv + dtb))
        beta = _sigmoid(abv)
        shape = (abv.shape[0], SLOT)
        g_slots = [jnp.broadcast_to(g[:, h:h + 1], shape) for h in range(N_HEADS)]
        b_slots = [jnp.broadcast_to(beta[:, N_HEADS + h:N_HEADS + h + 1], shape) for h in range(N_HEADS)]
        return [jnp.concatenate(g_slots, axis=1), jnp.concatenate(b_slots, axis=1)], []

    width = N_HEADS * SLOT
    return _rowwise("gdn_gates_fwd", fn, [ab], [a_log, dt_bias], [(width, F32), (width, F32)])


def _gates_bwd(ab, a_log, dt_bias, dg, dbeta):
    def fn(rows, consts):
        (abv, dgv, dbv), (alog, dtb) = rows, consts
        lane = lax.broadcasted_iota(jnp.int32, abv.shape, 1)
        dg_tok = jnp.zeros_like(abv)
        db_tok = jnp.zeros_like(abv)
        for h in range(N_HEADS):
            dg_tok = dg_tok + jnp.where(lane == h, jnp.sum(dgv[:, h * SLOT:(h + 1) * SLOT], axis=1, keepdims=True), 0.0)
            db_tok = db_tok + jnp.where(lane == N_HEADS + h, jnp.sum(dbv[:, h * SLOT:(h + 1) * SLOT], axis=1, keepdims=True), 0.0)
        dg_tok = _chunk_running_sum(dg_tok, reverse=True)
        xa = abv + dtb
        g = -jnp.exp(alog) * _softplus(xa)
        da = dg_tok * (-jnp.exp(alog)) * _sigmoid(xa)
        beta = _sigmoid(abv)
        dab = jnp.where(lane < N_HEADS, da, db_tok * beta * (1.0 - beta))
        dab = jnp.where(lane < 2 * N_HEADS, dab, 0.0)
        d_alog = jnp.sum(jnp.where(lane < N_HEADS, dg_tok * g, 0.0), axis=0, keepdims=True)
        d_dtb = jnp.sum(jnp.where(lane < N_HEADS, da, 0.0), axis=0, keepdims=True)
        return [dab], [d_alog, d_dtb]

    return _rowwise("gdn_gates_bwd", fn, [ab, dg, dbeta], [a_log, dt_bias], [(SLOT, F32)], sums=[SLOT, SLOT])


ROPE_HALF = MLA_ROPE // 2


def _rope_tables(positions):
    freqs = ROPE_THETA ** (-jnp.arange(ROPE_HALF, dtype=F32) / ROPE_HALF)
    ang = positions.astype(F32)[:, None] * freqs
    cos, sin = jnp.cos(ang), jnp.sin(ang)
    t = positions.shape[0]
    ones, zeros = jnp.ones((t, MLA_NOPE), F32), jnp.zeros((t, MLA_NOPE), F32)
    tail = jnp.zeros((t, SLOT - MLA_NOPE - MLA_ROPE), F32)
    half0 = jnp.zeros((t, ROPE_HALF), F32)
    same = jnp.concatenate([ones, cos, cos, tail], axis=1)
    from_low = jnp.concatenate([zeros, half0, sin, tail], axis=1)
    from_high = jnp.concatenate([zeros, -sin, half0, tail], axis=1)
    return same, from_low, from_high


def _rope(x, tabs):
    same, from_low, from_high = tabs
    width = x.shape[1]
    return x * same + pltpu.roll(x, ROPE_HALF, 1) * from_low + pltpu.roll(x, width - ROPE_HALF, 1) * from_high


def _rope_transposed(dy, tabs):
    same, from_low, from_high = tabs
    width = dy.shape[1]
    return dy * same + pltpu.roll(dy * from_low, width - ROPE_HALF, 1) + pltpu.roll(dy * from_high, ROPE_HALF, 1)


def _tile_slots(tab):
    return jnp.concatenate([tab] * N_HEADS, axis=1)


A_WIDTH = MLA_Q_RANK + MLA_KV_RANK + 2 * SLOT
A_KPE = MLA_Q_RANK + MLA_KV_RANK
A_AB = A_KPE + SLOT
WIDE = N_HEADS * SLOT


def _mla_front_fwd(proj_a, tabs, g_q, g_kv, w_uq, w_kv):
    def fn(rows, consts):
        pa, *tb = rows
        gq, gkv, wuq, wkv = consts
        cqn = _rms(pa[:, :MLA_Q_RANK], gq, MLA_Q_RANK).astype(BF16)
        ckvn = _rms(pa[:, MLA_Q_RANK:A_KPE], gkv, MLA_KV_RANK).astype(BF16)
        kv = _nt(ckvn, wkv)
        q = _rope(_nt(cqn, wuq), [_tile_slots(x) for x in tb])
        k = kv[:, :WIDE] + _tile_slots(_rope(pa[:, A_KPE:A_AB], tb))
        return [cqn, ckvn, q, k, kv[:, WIDE:]], []

    return _rowwise("mla_front_fwd", fn, [proj_a, *tabs], [g_q, g_kv, w_uq, w_kv],
                    [(MLA_Q_RANK, BF16), (MLA_KV_RANK, BF16)] + [(WIDE, BF16)] * 3)


def _mla_front_bwd(proj_a, tabs, g_q, g_kv, w_uq, w_kv, dq, dk, dv, dab):
    def fn(rows, consts):
        pa, t0, t1, t2, dqv, dkv, dvv, da = rows
        gq, gkv, wuq, wkv = consts
        tb = (t0, t1, t2)
        dq_p = _rope_transposed(dqv, [_tile_slots(x) for x in tb]).astype(BF16)
        dkv_p = jnp.concatenate([dkv, dvv], axis=1).astype(BF16)
        dkpe = dkv[:, :SLOT]
        for h in range(1, N_HEADS):
            dkpe = dkpe + dkv[:, h * SLOT:(h + 1) * SLOT]
        _, pull_q = jax.vjp(lambda x, g: _rms(x, g, MLA_Q_RANK), pa[:, :MLA_Q_RANK], gq)
        _, pull_kv = jax.vjp(lambda x, g: _rms(x, g, MLA_KV_RANK), pa[:, MLA_Q_RANK:A_KPE], gkv)
        dcq, dgq = pull_q(_nn(dq_p, wuq))
        dckv, dgkv = pull_kv(_nn(dkv_p, wkv))
        return [jnp.concatenate([dcq, dckv, _rope_transposed(dkpe, tb), da], axis=1), dq_p, dkv_p], [dgq, dgkv]

    return _rowwise("mla_front_bwd", fn, [proj_a, *tabs, dq, dk, dv, dab], [g_q, g_kv, w_uq, w_kv],
                    [(A_WIDTH, BF16), (WIDE, BF16), (2 * WIDE, BF16)], sums=[MLA_Q_RANK, MLA_KV_RANK])


def _slot_sum(x):
    parts = [jnp.broadcast_to(jnp.sum(x[:, h * SLOT:(h + 1) * SLOT], axis=1, keepdims=True), (x.shape[0], SLOT))
             for h in range(N_HEADS)]
    return jnp.concatenate(parts, axis=1)


def _mix_join(o_mla, o_gdn, gate, g_mla, g_gdn):
    mla = _rms(o_mla, g_mla, N_HEADS * MLA_V)
    gdn = o_gdn * lax.rsqrt(_slot_sum(o_gdn * o_gdn) * (1.0 / GDN_D) + EPS) * g_gdn * _silu(gate)
    return mla, gdn


MIX_TM = 256


def _mix_fwd(o_mla, o_gdn, gate, x, g_mla, g_gdn, w_out, g_post):
    dm = x.shape[1]

    def fn(rows, consts):
        om, og, gt, xv = rows
        gm, gg, wo, gp = consts
        cat = jnp.concatenate(_mix_join(om, og, gt, gm, gg), axis=1).astype(BF16)
        mixed = _nn(cat, wo)
        return [cat, mixed, xv + _rms(mixed, gp, dm)], []

    return _rowwise("mix_fwd", fn, [o_mla, o_gdn, gate, x], [g_mla, g_gdn, w_out, g_post],
                    [(2 * WIDE, BF16), (dm, F32), (dm, F32)], tm=MIX_TM)


def _mix_bwd(o_mla, o_gdn, gate, mixed, dy, g_mla, g_gdn, w_out, g_post):
    dm = mixed.shape[1]

    def fn(rows, consts):
        om, og, gt, mx, dyv = rows
        gm, gg, wo, gp = consts
        _, pull_post = jax.vjp(lambda hv, gv: _rms(hv, gv, dm), mx, gp)
        dmixed, dgp = pull_post(dyv)
        dmixed = dmixed.astype(BF16)
        dc = _nt(dmixed, wo)
        _, pull = jax.vjp(lambda x, g: _rms(x, g, N_HEADS * MLA_V), om, gm)
        dom, dgm = pull(dc[:, :WIDE])
        dn_out = dc[:, WIDE:]
        r = lax.rsqrt(_slot_sum(og * og) * (1.0 / GDN_D) + EPS)
        sig = _sigmoid(gt)
        normed = og * r
        dn = dn_out * gg * (gt * sig)
        dog = r * dn - normed * (r * r) * _slot_sum(dn * og) * (1.0 / GDN_D)
        dgt = dn_out * normed * gg * (sig * (1.0 + gt * (1.0 - sig)))
        dgg = jnp.sum(dn_out * normed * (gt * sig), axis=0, keepdims=True)
        return [dmixed, dom, _slot_sum(dom * om), dog, dgt], [dgp, dgm, dgg]

    return _rowwise("mix_bwd", fn, [o_mla, o_gdn, gate, mixed, dy], [g_mla, g_gdn, w_out, g_post],
                    [(dm, BF16), (WIDE, F32), (WIDE, F32), (WIDE, F32), (WIDE, BF16)], sums=[dm, WIDE, WIDE], tm=MIX_TM)


def _proj_fwd(x, g, weights):
    dm = x.shape[1]

    def fn(rows, consts):
        hn = _rms(rows[0], consts[0], dm).astype(BF16)
        return [hn] + [_nt(hn, wv) for wv in consts[1:]], []

    return _rowwise("proj_fwd", fn, [x], [g, *weights], [(dm, BF16)] + [(wv.shape[0], F32) for wv in weights], tm=MIX_TM)


def _proj_bwd(x, g, weights, cots, dy, h, g_post):
    dm = x.shape[1]
    n = len(weights)

    def fn(rows, consts):
        xv, dyv, hv, *parts = rows
        dn = _nn(parts[0], consts[2])
        for p, wv in zip(parts[1:], consts[3:]):
            dn = dn + _nn(p, wv)
        _, pull = jax.vjp(lambda a, gv: _rms(a, gv, dm), xv, consts[0])
        dx, dg = pull(dn)
        dx = dyv + dx
        _, pull = jax.vjp(lambda a: 0.5 * _rms(a, consts[1], dm), hv)
        return [dx, pull(dx)[0]], [dg]

    assert len(cots) == n
    return _rowwise("proj_bwd", fn, [x, dy, h, *cots], [g, g_post, *weights], [(dm, F32), (dm, BF16)], sums=[dm], tm=MIX_TM)


def _loss_fwd(y, target):
    dm = y.shape[1]

    def fn(rows, consts):
        err = rows[0] - rows[1]
        sq = err * err
        lanes = sq[:, :SLOT]
        for j in range(1, dm // SLOT):
            lanes = lanes + sq[:, j * SLOT:(j + 1) * SLOT]
        return [err * (1.0 / dm)], [jnp.sum(lanes, axis=0, keepdims=True) * (0.5 / dm)]

    return _rowwise("loss", fn, [y, target], [], [(dm, F32)], sums=[SLOT])


W_IN_CUTS = (0, 256, 384, 416, 1952, 1960, 1968, 2480)


def _heads_out(w, per_head, axis=-1):
    axis = axis % w.ndim
    shape = w.shape
    n = shape[axis] // per_head
    w = w.reshape(shape[:axis] + (n, per_head) + shape[axis + 1:])
    pad = [(0, 0)] * w.ndim
    pad[axis + 1] = (0, SLOT - per_head)
    return jnp.pad(w, pad).reshape(shape[:axis] + (n * SLOT,) + shape[axis + 1:])


def _heads_in(w, per_head, axis=-1):
    axis = axis % w.ndim
    shape = w.shape
    n = shape[axis] // SLOT
    w = w.reshape(shape[:axis] + (n, SLOT) + shape[axis + 1:])
    w = lax.slice_in_dim(w, 0, per_head, axis=axis + 1)
    return w.reshape(shape[:axis] + (n * per_head,) + shape[axis + 1:])


def _pad_lanes(v, lo, width=SLOT):
    return jnp.pad(v, [(0, 0)] * (v.ndim - 1) + [(lo, width - lo - v.shape[-1])])


def _pad_rows(v, lo, rows=SLOT):
    return jnp.pad(v, [(lo, rows - lo - v.shape[0])] + [(0, 0)] * (v.ndim - 1))


def _layout_weights(w):
    c = W_IN_CUTS
    w_in = w["w_in_t"]
    p = {}
    p["w_a"] = jnp.concatenate([w_in[c[0]:c[2]], _pad_rows(w_in[c[2]:c[3]], MLA_NOPE), _pad_rows(w_in[c[4]:c[6]], 0)], axis=0)
    p["w_qkv"] = _heads_out(w_in[c[3]:c[4]], GDN_D, axis=0)
    p["w_gate"] = _heads_out(w_in[c[6]:c[7]], GDN_D, axis=0)
    p["w_uq"] = _heads_out(w["uq_t"], MLA_NOPE + MLA_ROPE, axis=0)
    ukv = w["ukv_t"].reshape(N_HEADS, MLA_NOPE + MLA_V, MLA_KV_RANK)
    p["w_kv"] = jnp.concatenate([_heads_out(ukv[:, :MLA_NOPE].reshape(-1, MLA_KV_RANK), MLA_NOPE, axis=0),
                                 _heads_out(ukv[:, MLA_NOPE:].reshape(-1, MLA_KV_RANK), MLA_V, axis=0)], axis=0)
    p["conv"] = _heads_out(w["gdn_conv_w"], GDN_D)
    p["g_mla_out"] = _heads_out(w["mla_out_g"], MLA_V)
    p["g_gdn"] = jnp.tile(_pad_lanes(w["gdn_norm_g"], 0), (1, N_HEADS))
    p["a_log"] = _pad_lanes(w["gdn_a_log"], 0)
    p["dt_bias"] = _pad_lanes(w["gdn_dt_bias"], 0)
    return p


def _unlayout_grads(d):
    c = W_IN_CUTS
    g = {}
    da = d["w_a"]
    kpe0 = A_KPE + MLA_NOPE
    g["w_in_t"] = jnp.concatenate([da[:A_KPE], da[kpe0:kpe0 + MLA_ROPE], _heads_in(d["w_qkv"], GDN_D, axis=0),
                                   da[A_AB:A_AB + 2 * N_HEADS], _heads_in(d["w_gate"], GDN_D, axis=0)], axis=0)
    assert g["w_in_t"].shape[0] == c[-1]
    g["uq_t"] = _heads_in(d["w_uq"], MLA_NOPE + MLA_ROPE, axis=0)
    dk = _heads_in(d["w_kv"][:WIDE], MLA_NOPE, axis=0).reshape(N_HEADS, MLA_NOPE, MLA_KV_RANK)
    dv = _heads_in(d["w_kv"][WIDE:], MLA_V, axis=0).reshape(N_HEADS, MLA_V, MLA_KV_RANK)
    g["ukv_t"] = jnp.concatenate([dk, dv], axis=1).reshape(-1, MLA_KV_RANK)
    g["w_out"] = _heads_in(d["w_out"], GDN_D, axis=0)
    g["gdn_conv_w"] = _heads_in(d["conv"], GDN_D)
    g["mla_out_g"] = _heads_in(d["g_mla_out"], MLA_V)
    g["gdn_norm_g"] = jnp.sum(d["g_gdn"].reshape(N_HEADS, SLOT), axis=0, keepdims=True)[:, :GDN_D]
    g["gdn_a_log"] = d["a_log"][:, :N_HEADS]
    g["gdn_dt_bias"] = d["dt_bias"][:, :N_HEADS]
    return g


def _weight_grad(name, cots, acts, out_dtype=F32, tm=1024, tn=1024, tk=2048, after=None):
    return _matmul(name, cots, acts, "tn", out_dtype=out_dtype, tm=tm, tn=tn, tk=tk, after=after)


def _by_device(a):
    return a.astype(BF16).reshape((N_DEV, a.shape[0] // N_DEV) + a.shape[1:])


def _rows_of(blocks):
    return blocks.reshape((-1,) + blocks.shape[2:])


def _local_step(x, positions, target, w, mid, late):
    tabs = _rope_tables(positions)

    (h1, x1, hg1, hu1, a1), gathered = _ffn_fwd("ffn1_fwd", x, w["ffn1_pre_g"], w["ffn1"], 0, w["ffn1_post_g"], carry=mid)
    w = dict(w, w_in_t=_rows_of(gathered[0]), uq_t=_rows_of(gathered[1]), ukv_t=_rows_of(gathered[2]),
             gdn_conv_w=gathered[3].transpose(1, 0, 2).reshape(CONV_SHAPE))
    p = _layout_weights(w)
    in_weights = [p["w_a"], p["w_qkv"], p["w_gate"]]
    hn, proj_a, proj_qkv, proj_gate = _proj_fwd(x1, w["mix_pre_g"], in_weights)
    cqn, ckvn, q, k, v = _mla_front_fwd(proj_a, tabs, w["mla_q_norm_g"], w["mla_kv_norm_g"], p["w_uq"], p["w_kv"])
    o_mla, lse = _attn_fwd(q, k, v)
    ab = (proj_a, SLOT, A_AB // SLOT)
    qkv_n = _gdn_conv_fwd(proj_qkv, p["conv"])
    gb, bb = _gates_fwd(ab, p["a_log"], p["dt_bias"])
    (o_gdn, keep), (ffn2, w_out) = _gdn_fwd(qkv_n, gb, bb, carry=late)
    p["w_out"] = _heads_out(_rows_of(w_out), GDN_D, axis=0)
    cat, mixed, x2 = _mix_fwd(o_mla, o_gdn, proj_gate, x1, p["g_mla_out"], p["g_gdn"], p["w_out"], w["mix_post_g"])
    (h2, y, hg2, hu2, a2), _ = _ffn_fwd("ffn2_fwd", x2, w["ffn2_pre_g"], ffn2, 0, w["ffn2_post_g"])
    dy, loss_lanes = _loss_fwd(y, target)

    g = {}
    (dx2, xn2, dh2, dhg2, dhu2, g["ffn2_pre_g"], g["ffn2_post_g"]), _ = _ffn_bwd(
        "ffn2_bwd", x2, h2, hg2, hu2, dy, w["ffn2_pre_g"], ffn2, 0, w["ffn2_post_g"])
    ffn2_grads = _Scatter([_by_device(_weight_grad("ffn2_dw_gate", dhg2, xn2, BF16, tm=1408)),
                           _by_device(_weight_grad("ffn2_dw_up", dhu2, xn2, BF16, tm=1408)),
                           _by_device(_weight_grad("ffn2_dw_down", a2, dh2, BF16, tm=1408))])
    d = {}
    dmixed, do_mla, delta, do_gdn, dgate, g["mix_post_g"], d["g_mla_out"], d["g_gdn"] = _mix_bwd(
        o_mla, o_gdn, proj_gate, mixed, dx2, p["g_mla_out"], p["g_gdn"], p["w_out"], w["mix_post_g"])
    d["w_out"] = _weight_grad("mix_out_dw", cat, dmixed, BF16)
    dq, dk, dv = _attn_bwd(q, k, v, do_mla, lse, delta)
    (dqkv_n, dgb, dbb), landed_ffn2 = _gdn_bwd(qkv_n, gb, bb, keep, do_gdn, carry=ffn2_grads)
    dab, d["a_log"], d["dt_bias"] = _gates_bwd(ab, p["a_log"], p["dt_bias"], dgb, dbb)
    dproj_qkv, d["conv"] = _gdn_conv_bwd(proj_qkv, p["conv"], dqkv_n)
    dproj_a, dq_p, dkv_p, g["mla_q_norm_g"], g["mla_kv_norm_g"] = _mla_front_bwd(
        proj_a, tabs, w["mla_q_norm_g"], w["mla_kv_norm_g"], p["w_uq"], p["w_kv"], dq, dk, dv, dab)
    d["w_uq"] = _weight_grad("mla_q_dw", dq_p, cqn, BF16)
    d["w_kv"] = _weight_grad("mla_kv_dw", dkv_p, ckvn, BF16)
    d["w_a"] = _weight_grad("proj_a_dw", dproj_a, hn, BF16, tm=640)
    d["w_qkv"] = _weight_grad("proj_qkv_dw", dproj_qkv, hn, BF16)
    d["w_gate"] = _weight_grad("proj_gate_dw", dgate, hn, BF16)
    dx1, dh1, g["mix_pre_g"] = _proj_bwd(x1, w["mix_pre_g"], in_weights, [dproj_a, dproj_qkv, dgate], dx2,
                                        h1, w["ffn1_post_g"])
    g.update(_unlayout_grads(d))
    behind = [_by_device(g.pop(t)) for t in OTHER.values()]
    behind.append(_by_device(_weight_grad("ffn1_w_down_grad", a1, dh1, BF16, tm=1408)))
    (dx, xn1, _, dhg1, dhu1, g["ffn1_pre_g"], g["ffn1_post_g"]), landed_ffn1 = _ffn_bwd(
        "ffn1_bwd", x, h1, hg1, hu1, dx1, w["ffn1_pre_g"], w["ffn1"], 0, w["ffn1_post_g"], carry=_Scatter(behind))
    landed = dict(zip(list(FFN_NAMES[3:]) + list(OTHER) + ["ffn1_w_down"], list(landed_ffn2) + list(landed_ffn1)))
    begun, token = {}, None
    for name, cots, acts in (("ffn1_w_gate", dhg1, xn1), ("ffn1_w_up", dhu1, xn1)):
        blocks = _by_device(_weight_grad(name + "_grad", cots, acts, BF16, tm=1408, after=token))
        begun[name], token = _scatter_begin("scatter_" + name + "_begin", blocks)
    packed = _pack_small(g, g["gdn_conv_w"].reshape(-1), REDUCE_ROWS)
    packed = packed.at[REDUCE_ROWS - 1, ROW - 1].set(jnp.sum(loss_lanes))
    begun["small"], small_token = _scatter_begin("reduce_small_begin", jnp.broadcast_to(packed, (N_DEV,) + packed.shape))
    return dx, g, landed, begun, token + small_token


MESH_AXES = ("x", "y", "c")
N_LINKS = N_DEV - 1


def _place():
    return tuple(lax.axis_index(a) for a in MESH_AXES)


def _block_of(dev):
    x, y, c = dev
    return 4 * x + 2 * y + c


def _remote_copy(src, dst, sems, k, to):
    send_sems, recv_sems = sems
    return pltpu.make_async_remote_copy(src_ref=src, dst_ref=dst, send_sem=send_sems.at[k], recv_sem=recv_sems.at[k],
                                        device_id=to, device_id_type=pl.DeviceIdType.MESH)


class _Exchange:
    def __init__(self, arrays):
        self.arrays = list(arrays)
        self.n = len(self.arrays)
        self.specs = [pl.BlockSpec(memory_space=pl.ANY)] * self.n
        self.scratch = [pltpu.SemaphoreType.DMA((self.n * N_LINKS,)), pltpu.SemaphoreType.DMA((self.n * N_LINKS,)),
                        pltpu.SemaphoreType.DMA((self.n,))]

    def split(self, refs):
        n = self.n
        return refs[:n], refs[n:2 * n], (refs[2 * n], refs[2 * n + 1]), refs[2 * n + 2]


class _Gather(_Exchange):
    def out_shape(self):
        return [jax.ShapeDtypeStruct((N_DEV,) + a.shape, a.dtype) for a in self.arrays]

    def _plan(self, ins, outs, sems, local_sems):
        x, y, c = _place()
        me, sibling = (x, y, c), (x, y, 1 - c)
        chips = [(1 - x, y), (x, 1 - y), (1 - x, 1 - y)]

        def copy(a, k, block, to, mine=False):
            src = ins[a] if mine else outs[a].at[_block_of(block)]
            return _remote_copy(src, outs[a].at[_block_of(block)], sems, a * N_LINKS + k, to)

        local = [pltpu.make_async_copy(ins[a], outs[a].at[_block_of(me)], local_sems.at[a]) for a in range(self.n)]
        first = []
        for a in range(self.n):
            first.append(copy(a, 0, me, sibling, mine=True))
            first += [copy(a, 1 + j, me, (*chip, c), mine=True) for j, chip in enumerate(chips)]
        return me, sibling, chips, c, copy, local, first

    def start(self, ins, outs, sems, local_sems):
        *_, local, first = self._plan(ins, outs, sems, local_sems)
        for cp in local + first:
            cp.start()

    def finish(self, ins, outs, sems, local_sems):
        me, sibling, chips, c, copy, local, first = self._plan(ins, outs, sems, local_sems)
        passed = []
        for j, chip in enumerate(chips):
            for a in range(self.n):
                copy(a, 1 + j, (*chip, c), me).wait_recv()
                passed.append(copy(a, 4 + j, (*chip, c), sibling))
                passed[-1].start()
        for a in range(self.n):
            copy(a, 0, sibling, me).wait_recv()
            for j, chip in enumerate(chips):
                copy(a, 4 + j, (*chip, 1 - c), me).wait_recv()
        for cp in first + passed:
            cp.wait_send()
        for cp in local:
            cp.wait()


class _Scatter(_Exchange):
    def out_shape(self):
        return [jax.ShapeDtypeStruct(a.shape, a.dtype) for a in self.arrays]

    def _plan(self, ins, outs, sems, local_sems):
        x, y, c = _place()
        me = _block_of((x, y, c))

        def peer(r):
            return (1 - x if r & 4 else x, 1 - y if r & 2 else y, 1 - c if r & 1 else c)

        local = [pltpu.make_async_copy(ins[a].at[me], outs[a].at[me], local_sems.at[a]) for a in range(self.n)]
        sends = [_remote_copy(ins[a].at[_block_of(peer(r))], outs[a].at[me], sems, a * N_LINKS + r - 1, peer(r))
                 for a in range(self.n) for r in range(1, N_DEV)]
        arrivals = [_remote_copy(ins[a].at[me], outs[a].at[_block_of(peer(r))], sems, a * N_LINKS + r - 1, peer(r))
                    for a in range(self.n) for r in range(1, N_DEV)]
        return local, sends, arrivals

    def start(self, ins, outs, sems, local_sems):
        local, sends, _ = self._plan(ins, outs, sems, local_sems)
        for cp in local + sends:
            cp.start()

    def finish(self, ins, outs, sems, local_sems):
        local, sends, arrivals = self._plan(ins, outs, sems, local_sems)
        for cp in arrivals:
            cp.wait_recv()
        for cp in sends:
            cp.wait_send()
        for cp in local:
            cp.wait()


def _exchange(name, plan):
    def body(*refs):
        parts = plan.split(refs)
        plan.start(*parts)
        plan.finish(*parts)

    return pl.pallas_call(
        body, name=name,
        in_specs=plan.specs,
        out_specs=plan.specs,
        out_shape=plan.out_shape(),
        scratch_shapes=plan.scratch,
    )(*plan.arrays)


def _call_carrying(body, plan, operands, *, name, grid, in_specs, out_specs, out_shape, scratch_shapes, compiler_params):
    if plan is None:
        outs = pl.pallas_call(body, name=name, grid=grid, in_specs=in_specs, out_specs=out_specs, out_shape=out_shape,
                              scratch_shapes=scratch_shapes, compiler_params=compiler_params)(*operands)
        return outs, []
    n_i, n_o, n_s, k = len(in_specs), len(out_specs), len(scratch_shapes), plan.n

    def whole(*refs):
        cut = [n_i, n_i + k, n_i + k + n_o, n_i + 2 * k + n_o, n_i + 2 * k + n_o + n_s]
        own_in, ex_in, own_out, ex_out, own_scr, ex_scr = (refs[a:b] for a, b in zip([0] + cut, cut + [len(refs)]))
        parts = plan.split(ex_in + ex_out + ex_scr)
        first = last = True
        for axis, size in enumerate(grid):
            first = first & (pl.program_id(axis) == 0)
            last = last & (pl.program_id(axis) == size - 1)

        @pl.when(first)
        def _():
            plan.start(*parts)

        body(*own_in, *own_out, *own_scr)

        @pl.when(last)
        def _():
            plan.finish(*parts)

    outs = pl.pallas_call(
        whole, name=name, grid=grid,
        in_specs=list(in_specs) + plan.specs, out_specs=list(out_specs) + plan.specs,
        out_shape=list(out_shape) + plan.out_shape(), scratch_shapes=list(scratch_shapes) + plan.scratch,
        compiler_params=compiler_params,
    )(*operands, *plan.arrays)
    return outs[:n_o], outs[n_o:]


def _row_tile(rows, target=256):
    best = rows
    for cand in range(16, min(rows, target) + 1, 16):
        if rows % cand == 0:
            best = cand
    return best


def _sum_blocks(name, blocks, after=None):
    rows, width = blocks.shape[-2:]
    tm = _row_tile(rows)

    def body(x_ref, *rest):
        acc = x_ref[0].astype(F32)
        for d in range(1, N_DEV):
            acc = acc + x_ref[d].astype(F32)
        rest[-1][...] = acc

    ordered = [] if after is None else [after]
    return pl.pallas_call(
        body, name=name,
        grid=(rows // tm,),
        in_specs=[pl.BlockSpec((N_DEV, tm, width), lambda i: (0, i, 0))] + [pl.BlockSpec(memory_space=pl.ANY)] * len(ordered),
        out_specs=pl.BlockSpec((tm, width), lambda i: (i, 0)),
        out_shape=jax.ShapeDtypeStruct((rows, width), F32),
        compiler_params=pltpu.CompilerParams(dimension_semantics=("parallel",)),
    )(blocks, *ordered)


def _split_plan(src_ref, land_ref, sems):
    x, y, c = _place()
    me = _block_of((x, y, c))

    def peer(r):
        return (1 - x if r & 4 else x, 1 - y if r & 2 else y, 1 - c if r & 1 else c)

    sends = [_remote_copy(src_ref.at[_block_of(peer(r))], land_ref.at[me], sems, r - 1, peer(r)) for r in range(1, N_DEV)]
    arrivals = [_remote_copy(src_ref.at[me], land_ref.at[_block_of(peer(r))], sems, r - 1, peer(r)) for r in range(1, N_DEV)]
    return sends, arrivals


def _scatter_begin(name, blocks):
    def body(src_ref, land_ref, send_sems, recv_sems, src_thru, land_thru, token_ref):
        for cp in _split_plan(src_ref, land_ref, (send_sems, recv_sems))[0]:
            cp.start()
        token_ref[...] = jnp.zeros_like(token_ref)

    hbm, sem = pl.BlockSpec(memory_space=pltpu.HBM), pl.BlockSpec(memory_space=pltpu.SEMAPHORE)
    zone = pltpu.HBM(blocks.shape, blocks.dtype)
    *handles, token = pl.pallas_call(
        body, name=name,
        in_specs=(hbm, hbm),
        out_specs=(sem, sem, hbm, hbm, pl.BlockSpec(memory_space=pltpu.VMEM)),
        out_shape=(pltpu.SemaphoreType.DMA((N_LINKS,)), pltpu.SemaphoreType.DMA((N_LINKS,)), zone, zone,
                   jax.ShapeDtypeStruct((8, SLOT), F32)),
        input_output_aliases={0: 2, 1: 3},
        compiler_params=pltpu.CompilerParams(has_side_effects=pltpu.SideEffectType.DATAFLOW_SIDE_EFFECTING),
    )(pltpu.with_memory_space_constraint(blocks, pltpu.HBM),
      pltpu.with_memory_space_constraint(lax.empty(blocks.shape, blocks.dtype), pltpu.HBM))
    return handles, token


def _scatter_end(name, handles, after):
    send_sems, recv_sems, src, zone = handles

    def body(src_ref, land_ref, send_sems, recv_sems, after_ref, src_dead, got_ref):
        sends, arrivals = _split_plan(src_ref, land_ref, (send_sems, recv_sems))
        for cp in arrivals:
            cp.wait_recv()
        for cp in sends:
            cp.wait_send()

    hbm, sem = pl.BlockSpec(memory_space=pltpu.HBM), pl.BlockSpec(memory_space=pltpu.SEMAPHORE)
    sent, landed = pl.pallas_call(
        body, name=name,
        in_specs=(hbm, hbm, sem, sem, pl.BlockSpec(memory_space=pl.ANY)),
        out_specs=(hbm, hbm),
        out_shape=(pltpu.HBM(src.shape, src.dtype), pltpu.HBM(zone.shape, zone.dtype)),
        input_output_aliases={0: 0, 1: 1},
        compiler_params=pltpu.CompilerParams(has_side_effects=pltpu.SideEffectType.DATAFLOW_SIDE_EFFECTING),
    )(src, zone, send_sems, recv_sems, after)
    me = _block_of(_place())
    return lax.dynamic_update_slice_in_dim(landed, lax.dynamic_slice_in_dim(sent, me, 1, axis=0), me, axis=0)


def _adamw_values(wv, gv, mv, vv):
    m2 = ADAM_B1 * mv + (1.0 - ADAM_B1) * gv
    v2 = ADAM_B2 * vv + (1.0 - ADAM_B2) * jnp.square(gv)
    m_hat = m2 / (1.0 - ADAM_B1 ** ADAM_STEP)
    v_hat = v2 / (1.0 - ADAM_B2 ** ADAM_STEP)
    return [-ADAM_LR * (m_hat / (jnp.sqrt(v_hat) + ADAM_EPS) + ADAM_WD * wv), m2, v2]


def _adamw(name, w, g, m, v):
    def fn(rows, consts):
        return _adamw_values(*rows), []

    return _rowwise(name, fn, [w, g, m, v], [], [(w.shape[1], F32)] * 3, tm=_row_tile(w.shape[0]))


def _sum_adamw(name, blocks, w, m, v, after=None):
    rows, width = w.shape
    tm = _row_tile(rows)

    def body(x_ref, w_ref, m_ref, v_ref, *rest):
        acc = x_ref[0].astype(F32)
        for d in range(1, N_DEV):
            acc = acc + x_ref[d].astype(F32)
        rest[-4][...] = acc
        for ref, val in zip(rest[-3:], _adamw_values(w_ref[...], acc, m_ref[...], v_ref[...])):
            ref[...] = val

    ordered = [] if after is None else [after]
    tile = pl.BlockSpec((tm, width), lambda i: (i, 0))
    return pl.pallas_call(
        body, name=name,
        grid=(rows // tm,),
        in_specs=[pl.BlockSpec((N_DEV, tm, width), lambda i: (0, i, 0))] + [tile] * 3 + [pl.BlockSpec(memory_space=pl.ANY)] * len(ordered),
        out_specs=[tile] * 4,
        out_shape=[jax.ShapeDtypeStruct((rows, width), F32)] * 4,
        compiler_params=pltpu.CompilerParams(dimension_semantics=("parallel",)),
    )(blocks, w, m, v, *ordered)


ROW = 1024
FFN_NAMES = ("ffn1_w_gate", "ffn1_w_up", "ffn1_w_down", "ffn2_w_gate", "ffn2_w_up", "ffn2_w_down")
OTHER = {"w_in": "w_in_t", "mla_w_uq": "uq_t", "mla_w_ukv": "ukv_t", "w_out": "w_out"}
BY_COLUMNS = ("ffn1_w_gate", "ffn1_w_up", "ffn2_w_gate", "ffn2_w_up", "w_in", "mla_w_uq", "mla_w_ukv")
SMALL = {
    "ffn1_pre_g": (1024, 1024), "ffn1_post_g": (1024, 1024), "mix_pre_g": (1024, 1024), "mla_q_norm_g": (256, 256),
    "mla_kv_norm_g": (128, 128), "mla_out_g": (512, 512), "gdn_a_log": (8, 128), "gdn_dt_bias": (8, 128),
    "gdn_norm_g": (64, 128), "mix_post_g": (1024, 1024), "ffn2_pre_g": (1024, 1024), "ffn2_post_g": (1024, 1024),
}
CONV_SHAPE = (GDN_CONV, 3 * N_HEADS * GDN_D)
CONV_SHARD = (GDN_CONV, CONV_SHAPE[1] // N_DEV)
CONV_LANES = CONV_SHAPE[0] * CONV_SHAPE[1]
SMALL_ROWS = 8
REDUCE_ROWS = 16


def _pack_small(vecs, conv, rows):
    parts = [_pad_lanes(vecs[n].reshape(1, -1), 0, r) for n, (_, r) in SMALL.items()]
    parts.append(conv.reshape(1, -1))
    flat = jnp.concatenate(parts, axis=1)
    return _pad_lanes(flat, 0, rows * ROW).reshape(rows, ROW)


def _unpack_small(buf):
    flat = buf.reshape(1, -1)
    out, at = {}, 0
    for n, (w, r) in SMALL.items():
        out[n] = flat[:, at:at + w]
        at += r
    return out, flat[0, at:]


def kernel(x, positions, ffn1_pre_g, ffn1_w_gate, ffn1_w_up, ffn1_w_down, ffn1_post_g, mix_pre_g, w_in, mla_q_norm_g, mla_w_uq, mla_kv_norm_g, mla_w_ukv, mla_out_g, gdn_conv_w, gdn_a_log, gdn_dt_bias, gdn_norm_g, w_out, mix_post_g, ffn2_pre_g, ffn2_w_gate, ffn2_w_up, ffn2_w_down, ffn2_post_g, loss_target, m_ffn1_pre_g, m_ffn1_w_gate, m_ffn1_w_up, m_ffn1_w_down, m_ffn1_post_g, m_mix_pre_g, m_w_in, m_mla_q_norm_g, m_mla_w_uq, m_mla_kv_norm_g, m_mla_w_ukv, m_mla_out_g, m_gdn_conv_w, m_gdn_a_log, m_gdn_dt_bias, m_gdn_norm_g, m_w_out, m_mix_post_g, m_ffn2_pre_g, m_ffn2_w_gate, m_ffn2_w_up, m_ffn2_w_down, m_ffn2_post_g, v_ffn1_pre_g, v_ffn1_w_gate, v_ffn1_w_up, v_ffn1_w_down, v_ffn1_post_g, v_mix_pre_g, v_w_in, v_mla_q_norm_g, v_mla_w_uq, v_mla_kv_norm_g, v_mla_w_ukv, v_mla_out_g, v_gdn_conv_w, v_gdn_a_log, v_gdn_dt_bias, v_gdn_norm_g, v_w_out, v_mix_post_g, v_ffn2_pre_g, v_ffn2_w_gate, v_ffn2_w_up, v_ffn2_w_down, v_ffn2_post_g):
    given = dict(locals())
    order = ["ffn1_pre_g", "ffn1_w_gate", "ffn1_w_up", "ffn1_w_down", "ffn1_post_g", "mix_pre_g", "w_in", "mla_q_norm_g",
             "mla_w_uq", "mla_kv_norm_g", "mla_w_ukv", "mla_out_g", "gdn_conv_w", "gdn_a_log", "gdn_dt_bias", "gdn_norm_g",
             "w_out", "mix_post_g", "ffn2_pre_g", "ffn2_w_gate", "ffn2_w_up", "ffn2_w_down", "ffn2_post_g"]
    assert sorted(order) == sorted(list(FFN_NAMES) + list(OTHER) + list(SMALL) + ["gdn_conv_w"])

    def drop_depth(a):
        return a[0] if a.ndim == 3 else a

    wts = {n: drop_depth(given[n]) for n in order}
    mom = {n: drop_depth(given["m_" + n]) for n in order}
    var = {n: drop_depth(given["v_" + n]) for n in order}
    me = _block_of(_place())

    def wire(n):
        return (wts[n].T if n in BY_COLUMNS else wts[n]).astype(BF16)

    (ffn1,) = _exchange("gather_first", _Gather([jnp.stack([wire(n) for n in FFN_NAMES[:3]])]))
    mid = _Gather([wire(n) for n in ("w_in", "mla_w_uq", "mla_w_ukv")] + [wts["gdn_conv_w"]])
    late = _Gather([jnp.stack([wire(n) for n in FFN_NAMES[3:]]), wire("w_out")])
    full = {n: wts[n] for n in SMALL}
    full["ffn1"] = ffn1

    dx, grads, landed, begun, token = _local_step(x[0], positions[0], loss_target[0], full, mid, late)

    grad, outs = {}, {"delta": {}, "new_m": {}, "new_v": {}}

    def finish(n, blocks, after=None):
        flip = n in BY_COLUMNS and wts[n].shape[1] % SLOT != 0
        turn = (lambda a: a.T) if flip else (lambda a: a)
        if n in BY_COLUMNS and not flip:
            grad[n] = _sum_blocks("sum_" + n, blocks, after=after).T
            new = _adamw("adamw_" + n, wts[n], grad[n], mom[n], var[n])
        else:
            total, *new = _sum_adamw("update_" + n, blocks, turn(wts[n]), turn(mom[n]), turn(var[n]), after=after)
            grad[n] = turn(total)
        outs["delta"][n], outs["new_m"][n], outs["new_v"][n] = (turn(a) for a in new)
        return new[2]

    for n, blocks in landed.items():
        token = finish(n, blocks, after=token)
    small_handles = begun.pop("small")
    for n, handles in begun.items():
        token = finish(n, _scatter_end("scatter_" + n + "_end", handles, after=token))

    small_sum = _sum_blocks("sum_small", _scatter_end("reduce_small_end", small_handles, after=token))
    loss = small_sum[REDUCE_ROWS - 1, ROW - 1]
    small_grad, conv_grad_full = _unpack_small(small_sum)
    grad.update(small_grad)
    grad["gdn_conv_w"] = lax.dynamic_slice(conv_grad_full[:CONV_LANES].reshape(CONV_SHAPE), (0, me * CONV_SHARD[1]), CONV_SHARD)
    outs["grad"] = grad
    small = [_pack_small(s, s["gdn_conv_w"].reshape(-1), SMALL_ROWS) for s in (wts, grad, mom, var)]
    for kind, s in zip(("delta", "new_m", "new_v"), _adamw("adamw_small", *small)):
        vecs, conv = _unpack_small(s)
        outs[kind].update(vecs)
        outs[kind]["gdn_conv_w"] = conv[:CONV_SHARD[0] * CONV_SHARD[1]].reshape(CONV_SHARD)
    result = [loss, dx[None]]
    for kind in ("grad", "delta", "new_m", "new_v"):
        result += [outs[kind][n].reshape(given[n].shape) for n in order]
    return tuple(result)
```

```python
import jax
import jax.numpy as jnp
from jax import lax
from jax.experimental import pallas as pl
from jax.experimental.pallas import tpu as pltpu

F32 = jnp.float32
BF16 = jnp.bfloat16
HI = lax.Precision.HIGH

N_DEV = 8
N_HEADS = 8
SLOT = 128
MLA_Q_RANK = 256
MLA_KV_RANK = 128
MLA_NOPE = 64
MLA_ROPE = 32
MLA_V = 64
GDN_D = 64
GDN_CONV = 4
GDN_CHUNK = 64
ROPE_THETA = 10000.0
EPS = 1e-6
ADAM_LR, ADAM_B1, ADAM_B2, ADAM_EPS, ADAM_WD, ADAM_STEP = 0.001, 0.9, 0.999, 1e-08, 0.01, 10


def _dot(a, b, ca, cb, precision=None):
    lead = a.ndim - 2
    batch = tuple(range(lead))
    return lax.dot_general(a, b, (((lead + ca,), (lead + cb,)), (batch, batch)), precision=precision,
                           preferred_element_type=F32)


def _nn(a, b, precision=None):
    return _dot(a, b, 1, 0, precision)


def _nt(a, b, precision=None):
    return _dot(a, b, 1, 1, precision)


def _tn(a, b, precision=None):
    return _dot(a, b, 0, 0, precision)


def _sigmoid(x):
    return 1.0 / (1.0 + jnp.exp(-x))


def _silu(x):
    return x * _sigmoid(x)


def _rms(x, g, n):
    ms = jnp.sum(x * x, axis=-1, keepdims=True) * (1.0 / n)
    return x * lax.rsqrt(ms + EPS) * g


def _chunk_masks():
    c = GDN_CHUNK
    i = lax.broadcasted_iota(jnp.int32, (c, c), 0)
    j = lax.broadcasted_iota(jnp.int32, (c, c), 1)
    lower = i >= j
    strict = i > j
    eye = (i == j).astype(F32)
    blocks = []
    b = 1
    while b < c:
        same = (i // (2 * b)) == (j // (2 * b))
        blocks.append(same & ((i % (2 * b)) >= b) & ((j % (2 * b)) < b))
        b *= 2
    return lower, strict, eye, blocks


def _unit_lower_inverse(low, eye, blocks):
    t = eye - jnp.where(blocks[0], low, 0.0)
    for m in blocks[1:]:
        lo = jnp.where(m, low, 0.0)
        t = t - _nn(t, _nn(lo, t, HI), HI)
    return t


@jax.custom_vjp
def _known_inverse(low, tinv):
    return tinv


def _known_inverse_fwd(low, tinv):
    return tinv, tinv


def _known_inverse_bwd(tinv, dt):
    return -_tn(tinv, _nt(dt, tinv, HI), HI), jnp.zeros_like(tinv)


_known_inverse.defvjp(_known_inverse_fwd, _known_inverse_bwd)

_PRODUCTS = {"nn": _nn, "nt": _nt, "tn": _tn}


@jax.custom_vjp
def _known_nn(a, b, c):
    return c


@jax.custom_vjp
def _known_nt(a, b, c):
    return c


@jax.custom_vjp
def _known_tn(a, b, c):
    return c


def _known_fwd(a, b, c):
    return c, (a, b, c)


_known_nn.defvjp(_known_fwd, lambda r, dc: (_nt(dc, r[1], HI), _tn(r[0], dc, HI), jnp.zeros_like(r[2])))
_known_nt.defvjp(_known_fwd, lambda r, dc: (_nn(dc, r[1], HI), _tn(dc, r[0], HI), jnp.zeros_like(r[2])))
_known_tn.defvjp(_known_fwd, lambda r, dc: (_nt(r[1], dc, HI), _nn(r[0], dc, HI), jnp.zeros_like(r[2])))
_KNOWN = {"nn": _known_nn, "nt": _known_nt, "tn": _known_tn}
GDN_PRODUCTS = 8
GDN_KEPT = 2 + GDN_PRODUCTS


def _gdn_chunk(q, k, v, gc, bb, s, masks, known=None):
    lower, strict, eye, blocks = masks
    made = []

    def product(kind, a, b):
        c = _PRODUCTS[kind](a, b, HI) if known is None else _KNOWN[kind](a, b, known[1 + len(made)])
        made.append(c)
        return c

    qs = q * (GDN_D ** -0.5)
    gct = jnp.swapaxes(gc, -1, -2)
    decay = jnp.exp(jnp.where(lower, gc - gct, -1e30))
    kb = k * bb
    low = jnp.where(strict, product("nt", kb, k) * decay, 0.0)
    tinv = _unit_lower_inverse(low, eye, blocks) if known is None else _known_inverse(low, known[0])
    eg = jnp.exp(gc)
    w = product("nn", tinv, kb * eg)
    u = product("nn", tinv, v * bb)
    attn = product("nt", qs, k) * decay
    last = lax.broadcasted_iota(jnp.int32, gc.shape[-2:], 0) == GDN_CHUNK - 1
    g_end = jnp.sum(jnp.where(last, gc, 0.0), axis=-2, keepdims=True)
    k_dec = k * jnp.exp(g_end - gc)
    v_new = u - product("nn", w, s)
    o = product("nn", qs * eg, s) + product("nn", attn, v_new)
    s_new = s * jnp.exp(g_end) + product("tn", k_dec, v_new)
    assert len(made) == GDN_PRODUCTS
    return o, s_new, [tinv] + made


GDN_GROUP = 8
GDN_GROUPS = N_HEADS // GDN_GROUP


def _group_heads(ref):
    return jnp.stack([ref[:, pl.ds(j * SLOT, GDN_D)] for j in range(GDN_GROUP)])


def _ungroup_heads(ref, val):
    pad = jnp.zeros((GDN_CHUNK, SLOT - GDN_D), F32)
    for j in range(GDN_GROUP):
        ref[:, pl.ds(j * SLOT, GDN_D)] = val[j]
        ref[:, pl.ds(j * SLOT + GDN_D, SLOT - GDN_D)] = pad


def _gdn_fwd(qkv, gb, bb, carry=None):
    t = qkv.shape[0]
    n_chunks = t // GDN_CHUNK
    d = GDN_D

    def body(q_ref, k_ref, v_ref, g_ref, b_ref, o_ref, keep_ref, s_ref):
        @pl.when(pl.program_id(1) == 0)
        def _():
            s_ref[...] = jnp.zeros_like(s_ref)

        s = s_ref[...]
        keep_ref[:, 0, 0] = s
        o, s_new, made = _gdn_chunk(*[_group_heads(r) for r in (q_ref, k_ref, v_ref, g_ref, b_ref)], s, _chunk_masks())
        for i, val in enumerate(made):
            keep_ref[:, 0, 1 + i] = val
        s_ref[...] = s_new
        _ungroup_heads(o_ref, o)

    def spec(kind=0):
        return pl.BlockSpec((GDN_CHUNK, GDN_GROUP * SLOT), lambda h, n: (n, kind * GDN_GROUPS + h))

    return _call_carrying(
        body, carry, (qkv, qkv, qkv, gb, bb), name="gdn_fwd",
        grid=(GDN_GROUPS, n_chunks),
        in_specs=[spec(0), spec(1), spec(2), spec(), spec()],
        out_specs=[spec(), pl.BlockSpec((GDN_GROUP, 1, GDN_KEPT, d, d), lambda h, n: (h, n, 0, 0, 0))],
        out_shape=[jax.ShapeDtypeStruct((t, N_HEADS * SLOT), F32), jax.ShapeDtypeStruct((N_HEADS, n_chunks, GDN_KEPT, d, d), F32)],
        scratch_shapes=[pltpu.VMEM((GDN_GROUP, d, d), F32)],
        compiler_params=pltpu.CompilerParams(dimension_semantics=("arbitrary", "arbitrary")),
    )


def _gdn_bwd(qkv, gb, bb, keep, do, carry=None):
    t = qkv.shape[0]
    n_chunks = t // GDN_CHUNK
    d = GDN_D

    def body(q_ref, k_ref, v_ref, g_ref, b_ref, keep_ref, do_ref, dqkv_ref, dg_ref, db_ref, ds_ref):
        @pl.when(pl.program_id(1) == 0)
        def _():
            ds_ref[...] = jnp.zeros_like(ds_ref)

        masks = _chunk_masks()
        known = [keep_ref[:, 0, 1 + i] for i in range(GDN_KEPT - 1)]
        _, pull = jax.vjp(lambda *a: _gdn_chunk(*a, masks, known)[:2],
                          *[_group_heads(r) for r in (q_ref, k_ref, v_ref, g_ref, b_ref)], keep_ref[:, 0, 0])
        dq, dk, dv, dg, db, ds = pull((_group_heads(do_ref), ds_ref[...]))
        ds_ref[...] = ds
        for i, val in enumerate((dq, dk, dv)):
            _ungroup_heads(dqkv_ref.at[i], val)
        _ungroup_heads(dg_ref, dg)
        _ungroup_heads(db_ref, db)

    def spec(kind=0):
        return pl.BlockSpec((GDN_CHUNK, GDN_GROUP * SLOT), lambda h, n: (n_chunks - 1 - n, kind * GDN_GROUPS + h))

    return _call_carrying(
        body, carry, (qkv, qkv, qkv, gb, bb, keep, do), name="gdn_bwd",
        grid=(GDN_GROUPS, n_chunks),
        in_specs=[spec(0), spec(1), spec(2), spec(), spec(),
                  pl.BlockSpec((GDN_GROUP, 1, GDN_KEPT, d, d), lambda h, n: (h, n_chunks - 1 - n, 0, 0, 0)), spec()],
        out_specs=[pl.BlockSpec((3, GDN_CHUNK, GDN_GROUP * SLOT), lambda h, n: (0, n_chunks - 1 - n, h)), spec(), spec()],
        out_shape=[jax.ShapeDtypeStruct((3, t, N_HEADS * SLOT), F32)] + [jax.ShapeDtypeStruct((t, N_HEADS * SLOT), F32)] * 2,
        scratch_shapes=[pltpu.VMEM((GDN_GROUP, d, d), F32)],
        compiler_params=pltpu.CompilerParams(dimension_semantics=("arbitrary", "arbitrary")),
    )


def _rowwise(name, fn, rows, consts, outs, sums=(), tm=512):
    rows = [x if isinstance(x, tuple) else (x, x.shape[1], 0) for x in rows]
    t = rows[0][0].shape[0]
    tm = min(tm, t)
    steps = t // tm
    n_r, n_c, n_o, n_s = len(rows), len(consts), len(outs), len(sums)

    def window(width, block):
        return pl.BlockSpec((tm, width), lambda i: (i, block))

    def body(*refs):
        r, c = refs[:n_r], refs[n_r:n_r + n_c]
        o, s = refs[n_r + n_c:n_r + n_c + n_o], refs[n_r + n_c + n_o:]
        vals, tot = fn([x[...] for x in r], [x[...] for x in c])
        for ref, val in zip(o, vals):
            ref[...] = val.astype(ref.dtype)
        if n_s:
            @pl.when(pl.program_id(0) == 0)
            def _():
                for ref in s:
                    ref[...] = jnp.zeros_like(ref)

            for ref, val in zip(s, tot):
                ref[...] += val

    return pl.pallas_call(
        body, name=name,
        grid=(steps,),
        in_specs=[window(w, b) for _, w, b in rows] + [pl.BlockSpec(x.shape, lambda i: (0, 0)) for x in consts],
        out_specs=[pl.BlockSpec((tm, w), lambda i: (i, 0)) for w, _ in outs]
        + [pl.BlockSpec((1, w), lambda i: (0, 0)) for w in sums],
        out_shape=[jax.ShapeDtypeStruct((t, w), dt) for w, dt in outs]
        + [jax.ShapeDtypeStruct((1, w), F32) for w in sums],
        compiler_params=pltpu.CompilerParams(dimension_semantics=("arbitrary",)),
    )(*[x for x, _, _ in rows], *consts)


def _tile(dim, target):
    if dim <= target:
        return dim
    best = None
    for cand in range(128, target + 1, 128):
        if dim % cand == 0:
            best = cand
    assert best is not None, (dim, target)
    return best


def _matmul(name, a, b, mode, out_dtype=F32, tm=1024, tn=1024, tk=2048, after=None):
    if mode == "nn":
        (m, k), n = a.shape, b.shape[1]
    elif mode == "nt":
        (m, k), n = a.shape, b.shape[0]
    else:
        (k, m), n = a.shape, b.shape[1]
    tm, tn, tk = _tile(m, tm), _tile(n, tn), _tile(k, tk)
    k_steps = k // tk
    product = {"nn": _nn, "nt": _nt, "tn": _tn}[mode]

    def body(a_ref, b_ref, *rest):
        o_ref, acc_ref = rest[-2:]
        part = product(a_ref[...].astype(BF16), b_ref[...].astype(BF16))
        if k_steps == 1:
            o_ref[...] = part.astype(o_ref.dtype)
        else:
            kk = pl.program_id(2)

            @pl.when(kk == 0)
            def _():
                acc_ref[...] = part

            @pl.when(kk > 0)
            def _():
                acc_ref[...] += part

            @pl.when(kk == k_steps - 1)
            def _():
                o_ref[...] = acc_ref[...].astype(o_ref.dtype)

    a_spec = pl.BlockSpec((tk, tm), lambda i, j, kk: (kk, i)) if mode == "tn" else pl.BlockSpec((tm, tk), lambda i, j, kk: (i, kk))
    b_spec = pl.BlockSpec((tn, tk), lambda i, j, kk: (j, kk)) if mode == "nt" else pl.BlockSpec((tk, tn), lambda i, j, kk: (kk, j))
    ordered = [] if after is None else [after]
    return pl.pallas_call(
        body, name=name,
        grid=(m // tm, n // tn, k_steps),
        in_specs=[a_spec, b_spec] + [pl.BlockSpec(memory_space=pl.ANY)] * len(ordered),
        out_specs=pl.BlockSpec((tm, tn), lambda i, j, kk: (i, j)),
        out_shape=jax.ShapeDtypeStruct((m, n), out_dtype),
        scratch_shapes=[pltpu.VMEM((tm, tn) if k_steps > 1 else (8, 128), F32)],
        compiler_params=pltpu.CompilerParams(dimension_semantics=("parallel", "parallel", "arbitrary")),
    )(a, b, *ordered)


FFN_TM = 512
FFN_BWD_TM = 256
FFN_BLOCKS = 4
FFN_GATE, FFN_UP, FFN_DOWN = 0, 1, 2


def _ffn_weight_specs(ffn_w, first):
    _, _, rows, dm = ffn_w.shape

    def spec(k):
        return pl.BlockSpec((FFN_BLOCKS, None, rows, dm), lambda i, j: (j, first + k, 0, 0))

    return [spec(FFN_GATE), spec(FFN_UP), spec(FFN_DOWN)], FFN_BLOCKS * rows


def _ffn_fwd(name, x, g_pre, ffn_w, first, g_post, carry=None, target=None):
    t, dm = x.shape
    tm = min(FFN_TM, t)
    w_specs, tf = _ffn_weight_specs(ffn_w, first)
    f_steps = N_DEV // FFN_BLOCKS
    n_t = 0 if target is None else 1

    def body(x_ref, gpre_ref, wg_ref, wu_ref, wd_ref, gpost_ref, *rest):
        tgt_ref, loss_ref = rest[:n_t], rest[n_t + 5:2 * n_t + 5]
        h_ref, y_ref, hg_ref, hu_ref, a_ref = rest[n_t:n_t + 5]
        xn_ref, acc_ref = rest[-2:]
        i, j = pl.program_id(0), pl.program_id(1)

        @pl.when((i == 0) & (j == 0))
        def _():
            for ref in loss_ref:
                ref[...] = jnp.zeros_like(ref)

        @pl.when(j == 0)
        def _():
            xn_ref[...] = _rms(x_ref[...], gpre_ref[...], dm).astype(BF16)
            acc_ref[...] = jnp.zeros_like(acc_ref)

        xn = xn_ref[...]
        wg, wu, wd = (r[...].reshape(tf, dm) for r in (wg_ref, wu_ref, wd_ref))
        hg, hu = _nt(xn, wg), _nt(xn, wu)
        hg_ref[...] = hg.astype(BF16)
        hu_ref[...] = hu.astype(BF16)
        a = (_silu(hg) * hu).astype(BF16)
        a_ref[...] = a
        acc_ref[...] += _nn(a, wd)

        @pl.when(j == f_steps - 1)
        def _():
            h = acc_ref[...]
            h_ref[...] = h
            y = x_ref[...] + 0.5 * _rms(h, gpost_ref[...], dm)
            if n_t:
                err = y - tgt_ref[0][...]
                sq = err * err
                lanes = sq[:, :SLOT]
                for k in range(1, dm // SLOT):
                    lanes = lanes + sq[:, k * SLOT:(k + 1) * SLOT]
                loss_ref[0][...] += jnp.sum(lanes, axis=0, keepdims=True) * (0.5 / dm)
                y = err * (1.0 / dm)
            y_ref[...] = y

    row = pl.BlockSpec((tm, dm), lambda i, j: (i, 0))
    vec = pl.BlockSpec((1, dm), lambda i, j: (0, 0))
    wide = pl.BlockSpec((tm, tf), lambda i, j: (i, j))
    targets = [] if target is None else [target]
    return _call_carrying(
        body, carry, (x, g_pre, ffn_w, ffn_w, ffn_w, g_post, *targets), name=name,
        grid=(t // tm, f_steps),
        in_specs=[row, vec, *w_specs, vec] + [row] * n_t,
        out_specs=[row, row, wide, wide, wide] + [pl.BlockSpec((1, SLOT), lambda i, j: (0, 0))] * n_t,
        out_shape=[jax.ShapeDtypeStruct((t, dm), F32)] * 2 + [jax.ShapeDtypeStruct((t, f_steps * tf), BF16)] * 3
        + [jax.ShapeDtypeStruct((1, SLOT), F32)] * n_t,
        scratch_shapes=[pltpu.VMEM((tm, dm), BF16), pltpu.VMEM((tm, dm), F32)],
        compiler_params=pltpu.CompilerParams(dimension_semantics=("arbitrary", "arbitrary")),
    )


def _ffn_bwd(name, x, h, hg, hu, dy, g_pre, ffn_w, first, g_post, carry=None):
    t, dm = x.shape
    tm = min(FFN_BWD_TM, t)
    w_specs, tf = _ffn_weight_specs(ffn_w, first)
    f_steps = N_DEV // FFN_BLOCKS
    f = f_steps * tf

    def post(hv, g):
        return 0.5 * _rms(hv, g, dm)

    def pre(xv, g):
        return _rms(xv, g, dm)

    def body(x_ref, h_ref, dy_ref, hg_ref, hu_ref, gpre_ref, wg_ref, wu_ref, wd_ref, gpost_ref,
             dx_ref, xn_ref, dh_ref, dhg_ref, dhu_ref, dgpre_ref, dgpost_ref, acc_ref):
        i, j = pl.program_id(0), pl.program_id(1)

        @pl.when((i == 0) & (j == 0))
        def _():
            dgpre_ref[...] = jnp.zeros_like(dgpre_ref)
            dgpost_ref[...] = jnp.zeros_like(dgpost_ref)

        @pl.when(j == 0)
        def _():
            xn_ref[...] = pre(x_ref[...], gpre_ref[...]).astype(BF16)
            _, pull = jax.vjp(post, h_ref[...], gpost_ref[...])
            dh, dg = pull(dy_ref[...])
            dh_ref[...] = dh.astype(BF16)
            dgpost_ref[...] += dg
            acc_ref[...] = jnp.zeros_like(acc_ref)

        wg, wu, wd = (r[...].reshape(tf, dm) for r in (wg_ref, wu_ref, wd_ref))
        hg, hu = hg_ref[...].astype(F32), hu_ref[...].astype(F32)
        da = _nt(dh_ref[...], wd)
        sig = _sigmoid(hg)
        act = hg * sig
        dhu = (da * act).astype(BF16)
        dhg = (da * hu * (sig * (1.0 + hg * (1.0 - sig)))).astype(BF16)
        dhg_ref[...] = dhg
        dhu_ref[...] = dhu
        acc_ref[...] += _nn(dhg, wg) + _nn(dhu, wu)

        @pl.when(j == f_steps - 1)
        def _():
            _, pull = jax.vjp(pre, x_ref[...], gpre_ref[...])
            dx, dg = pull(acc_ref[...])
            dx_ref[...] = dy_ref[...] + dx
            dgpre_ref[...] += dg

    row = pl.BlockSpec((tm, dm), lambda i, j: (i, 0))
    vec = pl.BlockSpec((1, dm), lambda i, j: (0, 0))
    wide = pl.BlockSpec((tm, tf), lambda i, j: (i, j))
    return _call_carrying(
        body, carry, (x, h, dy, hg, hu, g_pre, ffn_w, ffn_w, ffn_w, g_post), name=name,
        grid=(t // tm, f_steps),
        in_specs=[row, row, row, wide, wide, vec, *w_specs, vec],
        out_specs=[row, row, row, wide, wide, vec, vec],
        out_shape=[jax.ShapeDtypeStruct((t, dm), F32), jax.ShapeDtypeStruct((t, dm), BF16), jax.ShapeDtypeStruct((t, dm), BF16),
                   jax.ShapeDtypeStruct((t, f), BF16), jax.ShapeDtypeStruct((t, f), BF16),
                   jax.ShapeDtypeStruct((1, dm), F32), jax.ShapeDtypeStruct((1, dm), F32)],
        scratch_shapes=[pltpu.VMEM((tm, dm), F32)],
        compiler_params=pltpu.CompilerParams(dimension_semantics=("arbitrary", "arbitrary")),
    )


ATT_T = 512
ATT_GROUP = 4
ATT_GROUP_FWD = 8
ATT_SCALE = (MLA_NOPE + MLA_ROPE) ** -0.5


def _stack_slots(ref, group):
    return jnp.stack([ref[:, pl.ds(j * SLOT, SLOT)] for j in range(group)])


def _unstack_slots(ref, val):
    for j in range(val.shape[0]):
        ref[:, pl.ds(j * SLOT, SLOT)] = val[j].astype(ref.dtype)


def _scores(q, k, diagonal):
    s = _nt(q, k) * ATT_SCALE
    if diagonal:
        row = lax.broadcasted_iota(jnp.int32, s.shape[1:], 0)
        col = lax.broadcasted_iota(jnp.int32, s.shape[1:], 1)
        s = jnp.where(col <= row, s, -1e30)
    return s


def _attn_pairs(steps, q_major):
    pairs = ([(qi, ki) for qi in range(steps) for ki in range(qi + 1)] if q_major
             else [(qi, ki) for ki in range(steps) for qi in range(ki, steps)])
    return jnp.array([p[0] for p in pairs], jnp.int32), jnp.array([p[1] for p in pairs], jnp.int32)


def _attn_specs(tile, group):
    width = group * SLOT
    return (pl.BlockSpec((tile, width), lambda h, p, qt, kt: (qt[p], h)),
            pl.BlockSpec((tile, width), lambda h, p, qt, kt: (kt[p], h)))


def _attn_fwd(q, k, v):
    t = q.shape[0]
    tile = min(ATT_T, t)
    steps = t // tile
    g = ATT_GROUP_FWD

    strip = min(SLOT, tile)

    def body(qt_ref, kt_ref, q_ref, k_ref, v_ref, o_ref, lse_ref, m_ref, l_ref, alpha_ref, acc_ref, s_ref, p_ref):
        qi, ki = qt_ref[pl.program_id(1)], kt_ref[pl.program_id(1)]

        @pl.when(ki == 0)
        def _():
            m_ref[...] = jnp.full_like(m_ref, -1e30)
            l_ref[...] = jnp.zeros_like(l_ref)
            acc_ref[...] = jnp.zeros_like(acc_ref)

        def step(diagonal):
            s_ref[...] = _nt(_stack_slots(k_ref, g), _stack_slots(q_ref, g))
            for j in range(tile // strip):
                c = pl.ds(j * strip, strip)
                s = s_ref[:, :, c] * ATT_SCALE
                if diagonal:
                    key = lax.broadcasted_iota(jnp.int32, s.shape[1:], 0)
                    query = lax.broadcasted_iota(jnp.int32, s.shape[1:], 1) + j * strip
                    s = jnp.where(key <= query, s, -1e30)
                m_old = m_ref[:, :, c]
                m_new = jnp.maximum(m_old, jnp.max(s, axis=1, keepdims=True))
                p = jnp.exp(s - m_new)
                alpha = jnp.exp(m_old - m_new)
                l_ref[:, :, c] = alpha * l_ref[:, :, c] + jnp.sum(p, axis=1, keepdims=True)
                alpha_ref[:, :, c] = alpha
                m_ref[:, :, c] = m_new
                p_ref[:, :, c] = p.astype(BF16)
            acc_ref[...] = acc_ref[...] * alpha_ref[...] + _tn(_stack_slots(v_ref, g), p_ref[...])

        @pl.when(ki < qi)
        def _():
            step(False)

        @pl.when(ki == qi)
        def _():
            step(True)
            out = acc_ref[...] / l_ref[...]
            lse = jnp.broadcast_to(m_ref[...] + jnp.log(l_ref[...]), out.shape)
            for j in range(g):
                o_ref[:, pl.ds(j * SLOT, SLOT)] = out[j].T
                lse_ref[:, pl.ds(j * SLOT, SLOT)] = lse[j].T

    q_spec, k_spec = _attn_specs(tile, g)
    tables = _attn_pairs(steps, True)
    return pl.pallas_call(
        body, name="attn_fwd",
        grid_spec=pltpu.PrefetchScalarGridSpec(
            num_scalar_prefetch=2, grid=(N_HEADS // g, tables[0].shape[0]),
            in_specs=[q_spec, k_spec, k_spec], out_specs=[q_spec, q_spec],
            scratch_shapes=[pltpu.VMEM((g, 1, tile), F32), pltpu.VMEM((g, 1, tile), F32), pltpu.VMEM((g, 1, tile), F32),
                            pltpu.VMEM((g, SLOT, tile), F32), pltpu.VMEM((g, tile, tile), F32), pltpu.VMEM((g, tile, tile), BF16)]),
        out_shape=[jax.ShapeDtypeStruct((t, N_HEADS * SLOT), F32)] * 2,
        compiler_params=pltpu.CompilerParams(dimension_semantics=("parallel", "arbitrary")),
    )(*tables, q, k, v)


def _attn_grad_scores(q, k, v, do, lse_ref, delta_ref, diagonal):
    g = ATT_GROUP
    p = jnp.exp(_scores(q, k, diagonal) - _stack_slots(lse_ref, g)[:, :, 0:1])
    dp = _nt(do, v)
    return p, p * (dp - _stack_slots(delta_ref, g)[:, :, 0:1]) * ATT_SCALE


def _attn_bwd(q, k, v, do, lse, delta):
    t = q.shape[0]
    tile = min(ATT_T, t)
    steps = t // tile
    g = ATT_GROUP

    def body(qt_ref, kt_ref, q_ref, k_ref, v_ref, do_ref, lse_ref, delta_ref, dq_ref, dk_ref, dv_ref, dk_acc, dv_acc):
        qi, ki = qt_ref[pl.program_id(1)], kt_ref[pl.program_id(1)]

        @pl.when(pl.program_id(1) == 0)
        def _():
            dq_ref[...] = jnp.zeros_like(dq_ref)

        def step(diagonal):
            qq, kk = _stack_slots(q_ref, g), _stack_slots(k_ref, g)
            do_b = _stack_slots(do_ref, g).astype(BF16)
            p, ds = _attn_grad_scores(qq, kk, _stack_slots(v_ref, g), do_b, lse_ref, delta_ref, diagonal)
            ds = ds.astype(BF16)
            dv_acc[...] += _tn(p.astype(BF16), do_b)
            dk_acc[...] += _tn(ds, qq)
            dq = _nn(ds, kk)
            rows = pl.ds(pl.multiple_of(qi * tile, tile), tile)
            for j in range(g):
                dq_ref[rows, pl.ds(j * SLOT, SLOT)] += dq[j]

        @pl.when(qi == ki)
        def _():
            dk_acc[...] = jnp.zeros_like(dk_acc)
            dv_acc[...] = jnp.zeros_like(dv_acc)
            step(True)

        @pl.when(qi > ki)
        def _():
            step(False)

        @pl.when(qi == steps - 1)
        def _():
            _unstack_slots(dk_ref, dk_acc[...])
            _unstack_slots(dv_ref, dv_acc[...])

    q_spec, k_spec = _attn_specs(tile, g)
    tables = _attn_pairs(steps, False)
    return pl.pallas_call(
        body, name="attn_bwd",
        grid_spec=pltpu.PrefetchScalarGridSpec(
            num_scalar_prefetch=2, grid=(N_HEADS // g, tables[0].shape[0]),
            in_specs=[q_spec, k_spec, k_spec, q_spec, q_spec, q_spec],
            out_specs=[pl.BlockSpec((t, g * SLOT), lambda h, p, qt, kt: (0, h)), k_spec, k_spec],
            scratch_shapes=[pltpu.VMEM((g, tile, SLOT), F32), pltpu.VMEM((g, tile, SLOT), F32)]),
        out_shape=[jax.ShapeDtypeStruct((t, N_HEADS * SLOT), F32)] * 3,
        compiler_params=pltpu.CompilerParams(dimension_semantics=("parallel", "arbitrary")),
    )(*tables, q, k, v, do, lse, delta)


CONV_PAD = 8


def _fill_padded(ref, val):
    t = val.shape[0]
    zeros = jnp.zeros((CONV_PAD, val.shape[1]), val.dtype)
    ref[pl.ds(0, CONV_PAD)] = zeros
    ref[pl.ds(CONV_PAD + t, CONV_PAD)] = zeros
    ref[pl.ds(CONV_PAD, t)] = val


def _shifted(ref, s):
    return ref[pl.ds(CONV_PAD - s, ref.shape[0] - 2 * CONV_PAD)]


def _l2norm(x):
    return x * lax.rsqrt(jnp.sum(x * x, axis=-1, keepdims=True) + EPS)


def _conv_pre(x_pad, w):
    y = w[GDN_CONV - 1:GDN_CONV, :] * _shifted(x_pad, 0)
    for s in range(1, GDN_CONV):
        y = y + w[GDN_CONV - 1 - s:GDN_CONV - s, :] * _shifted(x_pad, s)
    return y


def _gdn_conv_fwd(x, w):
    t, width = x.shape

    def body(x_ref, w_ref, o_ref, x_pad):
        _fill_padded(x_pad, x_ref[...])
        act = _silu(_conv_pre(x_pad, w_ref[...]))
        normed = pl.program_id(0) < 2 * N_HEADS
        o_ref[...] = jnp.where(normed, _l2norm(act), act)

    return pl.pallas_call(
        body, name="gdn_conv_fwd",
        grid=(width // SLOT,),
        in_specs=[pl.BlockSpec((t, SLOT), lambda j: (0, j)), pl.BlockSpec((GDN_CONV, SLOT), lambda j: (0, j))],
        out_specs=pl.BlockSpec((t, SLOT), lambda j: (0, j)),
        out_shape=jax.ShapeDtypeStruct((t, width), F32),
        scratch_shapes=[pltpu.VMEM((t + 2 * CONV_PAD, SLOT), F32)],
        compiler_params=pltpu.CompilerParams(dimension_semantics=("parallel",)),
    )(x, w)


def _gdn_conv_bwd(x, w, dout):
    t, width = x.shape

    def body(x_ref, w_ref, do_ref, dx_ref, dw_ref, x_pad, dy_pad):
        wv = w_ref[...]
        _fill_padded(x_pad, x_ref[...])
        y = _conv_pre(x_pad, wv)
        sig = _sigmoid(y)
        act = y * sig
        _, pull = jax.vjp(_l2norm, act)
        normed = pl.program_id(0) < 2 * N_HEADS
        dact = jnp.where(normed, pull(do_ref[0])[0], do_ref[0])
        dy = dact * (sig * (1.0 + y * (1.0 - sig)))
        _fill_padded(dy_pad, dy)
        dx = wv[GDN_CONV - 1:GDN_CONV, :] * dy
        for s in range(1, GDN_CONV):
            dx = dx + wv[GDN_CONV - 1 - s:GDN_CONV - s, :] * _shifted(dy_pad, -s)
        dx_ref[...] = dx.astype(BF16)
        for s in range(GDN_CONV):
            dw_ref[GDN_CONV - 1 - s:GDN_CONV - s, :] = jnp.sum(dy * _shifted(x_pad, s), axis=0, keepdims=True)

    col = pl.BlockSpec((t, SLOT), lambda j: (0, j))
    tap = pl.BlockSpec((GDN_CONV, SLOT), lambda j: (0, j))
    return pl.pallas_call(
        body, name="gdn_conv_bwd",
        grid=(width // SLOT,),
        in_specs=[col, tap, pl.BlockSpec((1, t, SLOT), lambda j: (j // N_HEADS, 0, j % N_HEADS))],
        out_specs=[col, tap],
        out_shape=[jax.ShapeDtypeStruct((t, width), BF16), jax.ShapeDtypeStruct((GDN_CONV, width), F32)],
        scratch_shapes=[pltpu.VMEM((t + 2 * CONV_PAD, SLOT), F32)] * 2,
        compiler_params=pltpu.CompilerParams(dimension_semantics=("parallel",)),
    )(x, w, dout)


def _softplus(x):
    e = jnp.exp(-jnp.abs(x))
    u = 1.0 + e
    log1p = jnp.where(u == 1.0, e, jnp.log(u) * e / jnp.where(u == 1.0, 1.0, u - 1.0))
    return jnp.maximum(x, 0.0) + log1p


def _chunk_running_sum(x, reverse=False):
    tm = x.shape[0]
    at = lax.broadcasted_iota(jnp.int32, x.shape, 0) % GDN_CHUNK
    step = 1
    while step < GDN_CHUNK:
        if reverse:
            x = x + jnp.where(at < GDN_CHUNK - step, pltpu.roll(x, tm - step, 0), 0.0)
        else:
            x = x + jnp.where(at >= step, pltpu.roll(x, step, 0), 0.0)
        step *= 2
    return x


def _gates_fwd(ab, a_log, dt_bias):
    def fn(rows, consts):
        (abv,), (alog, dtb) = rows, consts
        g = _chunk_running_sum(-jnp.exp(alog) * _softplus(abv + dtb))
        beta = _sigmoid(abv)
        shape = (abv.shape[0], SLOT)
        g_slots = [jnp.broadcast_to(g[:, h:h + 1], shape) for h in range(N_HEADS)]
        b_slots = [jnp.broadcast_to(beta[:, N_HEADS + h:N_HEADS + h + 1], shape) for h in range(N_HEADS)]
        return [jnp.concatenate(g_slots, axis=1), jnp.concatenate(b_slots, axis=1)], []

    width = N_HEADS * SLOT
    return _rowwise("gdn_gates_fwd", fn, [ab], [a_log, dt_bias], [(width, F32), (width, F32)])


def _gates_bwd(ab, a_log, dt_bias, dg, dbeta):
    def fn(rows, consts):
        (abv, dgv, dbv), (alog, dtb) = rows, consts
        lane = lax.broadcasted_iota(jnp.int32, abv.shape, 1)
        dg_tok = jnp.zeros_like(abv)
        db_tok = jnp.zeros_like(abv)
        for h in range(N_HEADS):
            dg_tok = dg_tok + jnp.where(lane == h, jnp.sum(dgv[:, h * SLOT:(h + 1) * SLOT], axis=1, keepdims=True), 0.0)
            db_tok = db_tok + jnp.where(lane == N_HEADS + h, jnp.sum(dbv[:, h * SLOT:(h + 1) * SLOT], axis=1, keepdims=True), 0.0)
        dg_tok = _chunk_running_sum(dg_tok, reverse=True)
        xa = abv + dtb
        g = -jnp.exp(alog) * _softplus(xa)
        da = dg_tok * (-jnp.exp(alog)) * _sigmoid(xa)
        beta = _sigmoid(abv)
        dab = jnp.where(lane < N_HEADS, da, db_tok * beta * (1.0 - beta))
        dab = jnp.where(lane < 2 * N_HEADS, dab, 0.0)
        d_alog = jnp.sum(jnp.where(lane < N_HEADS, dg_tok * g, 0.0), axis=0, keepdims=True)
        d_dtb = jnp.sum(jnp.where(lane < N_HEADS, da, 0.0), axis=0, keepdims=True)
        return [dab], [d_alog, d_dtb]

    return _rowwise("gdn_gates_bwd", fn, [ab, dg, dbeta], [a_log, dt_bias], [(SLOT, F32)], sums=[SLOT, SLOT])


ROPE_HALF = MLA_ROPE // 2


def _rope_tables(positions):
    freqs = ROPE_THETA ** (-jnp.arange(ROPE_HALF, dtype=F32) / ROPE_HALF)
    ang = positions.astype(F32)[:, None] * freqs
    cos, sin = jnp.cos(ang), jnp.sin(ang)
    t = positions.shape[0]
    ones, zeros = jnp.ones((t, MLA_NOPE), F32), jnp.zeros((t, MLA_NOPE), F32)
    tail = jnp.zeros((t, SLOT - MLA_NOPE - MLA_ROPE), F32)
    half0 = jnp.zeros((t, ROPE_HALF), F32)
    same = jnp.concatenate([ones, cos, cos, tail], axis=1)
    from_low = jnp.concatenate([zeros, half0, sin, tail], axis=1)
    from_high = jnp.concatenate([zeros, -sin, half0, tail], axis=1)
    return same, from_low, from_high


def _rope(x, tabs):
    same, from_low, from_high = tabs
    width = x.shape[1]
    return x * same + pltpu.roll(x, ROPE_HALF, 1) * from_low + pltpu.roll(x, width - ROPE_HALF, 1) * from_high


def _rope_transposed(dy, tabs):
    same, from_low, from_high = tabs
    width = dy.shape[1]
    return dy * same + pltpu.roll(dy * from_low, width - ROPE_HALF, 1) + pltpu.roll(dy * from_high, ROPE_HALF, 1)


def _tile_slots(tab):
    return jnp.concatenate([tab] * N_HEADS, axis=1)


A_WIDTH = MLA_Q_RANK + MLA_KV_RANK + 2 * SLOT
A_KPE = MLA_Q_RANK + MLA_KV_RANK
A_AB = A_KPE + SLOT
WIDE = N_HEADS * SLOT


def _mla_front_fwd(proj_a, tabs, g_q, g_kv, w_uq, w_kv):
    def fn(rows, consts):
        pa, *tb = rows
        gq, gkv, wuq, wkv = consts
        cqn = _rms(pa[:, :MLA_Q_RANK], gq, MLA_Q_RANK).astype(BF16)
        ckvn = _rms(pa[:, MLA_Q_RANK:A_KPE], gkv, MLA_KV_RANK).astype(BF16)
        kv = _nt(ckvn, wkv)
        q = _rope(_nt(cqn, wuq), [_tile_slots(x) for x in tb])
        k = kv[:, :WIDE] + _tile_slots(_rope(pa[:, A_KPE:A_AB], tb))
        return [cqn, ckvn, q, k, kv[:, WIDE:]], []

    return _rowwise("mla_front_fwd", fn, [proj_a, *tabs], [g_q, g_kv, w_uq, w_kv],
                    [(MLA_Q_RANK, BF16), (MLA_KV_RANK, BF16)] + [(WIDE, BF16)] * 3)


def _mla_front_bwd(proj_a, tabs, g_q, g_kv, w_uq, w_kv, dq, dk, dv, dab):
    def fn(rows, consts):
        pa, t0, t1, t2, dqv, dkv, dvv, da = rows
        gq, gkv, wuq, wkv = consts
        tb = (t0, t1, t2)
        dq_p = _rope_transposed(dqv, [_tile_slots(x) for x in tb]).astype(BF16)
        dkv_p = jnp.concatenate([dkv, dvv], axis=1).astype(BF16)
        dkpe = dkv[:, :SLOT]
        for h in range(1, N_HEADS):
            dkpe = dkpe + dkv[:, h * SLOT:(h + 1) * SLOT]
        _, pull_q = jax.vjp(lambda x, g: _rms(x, g, MLA_Q_RANK), pa[:, :MLA_Q_RANK], gq)
        _, pull_kv = jax.vjp(lambda x, g: _rms(x, g, MLA_KV_RANK), pa[:, MLA_Q_RANK:A_KPE], gkv)
        dcq, dgq = pull_q(_nn(dq_p, wuq))
        dckv, dgkv = pull_kv(_nn(dkv_p, wkv))
        return [jnp.concatenate([dcq, dckv, _rope_transposed(dkpe, tb), da], axis=1), dq_p, dkv_p], [dgq, dgkv]

    return _rowwise("mla_front_bwd", fn, [proj_a, *tabs, dq, dk, dv, dab], [g_q, g_kv, w_uq, w_kv],
                    [(A_WIDTH, BF16), (WIDE, BF16), (2 * WIDE, BF16)], sums=[MLA_Q_RANK, MLA_KV_RANK])


def _slot_sum(x):
    parts = [jnp.broadcast_to(jnp.sum(x[:, h * SLOT:(h + 1) * SLOT], axis=1, keepdims=True), (x.shape[0], SLOT))
             for h in range(N_HEADS)]
    return jnp.concatenate(parts, axis=1)


def _mix_join(o_mla, o_gdn, gate, g_mla, g_gdn):
    mla = _rms(o_mla, g_mla, N_HEADS * MLA_V)
    gdn = o_gdn * lax.rsqrt(_slot_sum(o_gdn * o_gdn) * (1.0 / GDN_D) + EPS) * g_gdn * _silu(gate)
    return mla, gdn


MIX_TM = 256


def _mix_fwd(o_mla, o_gdn, gate, x, g_mla, g_gdn, w_out, g_post):
    dm = x.shape[1]

    def fn(rows, consts):
        om, og, gt, xv = rows
        gm, gg, wo, gp = consts
        cat = jnp.concatenate(_mix_join(om, og, gt, gm, gg), axis=1).astype(BF16)
        mixed = _nn(cat, wo)
        return [cat, mixed, xv + _rms(mixed, gp, dm)], []

    return _rowwise("mix_fwd", fn, [o_mla, o_gdn, gate, x], [g_mla, g_gdn, w_out, g_post],
                    [(2 * WIDE, BF16), (dm, F32), (dm, F32)], tm=MIX_TM)


def _mix_bwd(o_mla, o_gdn, gate, mixed, dy, g_mla, g_gdn, w_out, g_post):
    dm = mixed.shape[1]

    def fn(rows, consts):
        om, og, gt, mx, dyv = rows
        gm, gg, wo, gp = consts
        _, pull_post = jax.vjp(lambda hv, gv: _rms(hv, gv, dm), mx, gp)
        dmixed, dgp = pull_post(dyv)
        dmixed = dmixed.astype(BF16)
        dc = _nt(dmixed, wo)
        _, pull = jax.vjp(lambda x, g: _rms(x, g, N_HEADS * MLA_V), om, gm)
        dom, dgm = pull(dc[:, :WIDE])
        dn_out = dc[:, WIDE:]
        r = lax.rsqrt(_slot_sum(og * og) * (1.0 / GDN_D) + EPS)
        sig = _sigmoid(gt)
        normed = og * r
        dn = dn_out * gg * (gt * sig)
        dog = r * dn - normed * (r * r) * _slot_sum(dn * og) * (1.0 / GDN_D)
        dgt = dn_out * normed * gg * (sig * (1.0 + gt * (1.0 - sig)))
        dgg = jnp.sum(dn_out * normed * (gt * sig), axis=0, keepdims=True)
        return [dmixed, dom, _slot_sum(dom * om), dog, dgt], [dgp, dgm, dgg]

    return _rowwise("mix_bwd", fn, [o_mla, o_gdn, gate, mixed, dy], [g_mla, g_gdn, w_out, g_post],
                    [(dm, BF16), (WIDE, F32), (WIDE, F32), (WIDE, F32), (WIDE, BF16)], sums=[dm, WIDE, WIDE], tm=MIX_TM)


def _proj_fwd(x, g, weights):
    dm = x.shape[1]

    def fn(rows, consts):
        hn = _rms(rows[0], consts[0], dm).astype(BF16)
        return [hn] + [_nt(hn, wv) for wv in consts[1:]], []

    return _rowwise("proj_fwd", fn, [x], [g, *weights], [(dm, BF16)] + [(wv.shape[0], F32) for wv in weights], tm=MIX_TM)


def _proj_bwd(x, g, weights, cots, dy, h, g_post):
    dm = x.shape[1]
    n = len(weights)

    def fn(rows, consts):
        xv, dyv, hv, *parts = rows
        dn = _nn(parts[0], consts[2])
        for p, wv in zip(parts[1:], consts[3:]):
            dn = dn + _nn(p, wv)
        _, pull = jax.vjp(lambda a, gv: _rms(a, gv, dm), xv, consts[0])
        dx, dg = pull(dn)
        dx = dyv + dx
        _, pull = jax.vjp(lambda a: 0.5 * _rms(a, consts[1], dm), hv)
        return [dx, pull(dx)[0]], [dg]

    assert len(cots) == n
    return _rowwise("proj_bwd", fn, [x, dy, h, *cots], [g, g_post, *weights], [(dm, F32), (dm, BF16)], sums=[dm], tm=MIX_TM)


W_IN_CUTS = (0, 256, 384, 416, 1952, 1960, 1968, 2480)


def _heads_out(w, per_head, axis=-1):
    axis = axis % w.ndim
    shape = w.shape
    n = shape[axis] // per_head
    w = w.reshape(shape[:axis] + (n, per_head) + shape[axis + 1:])
    pad = [(0, 0)] * w.ndim
    pad[axis + 1] = (0, SLOT - per_head)
    return jnp.pad(w, pad).reshape(shape[:axis] + (n * SLOT,) + shape[axis + 1:])


def _heads_in(w, per_head, axis=-1):
    axis = axis % w.ndim
    shape = w.shape
    n = shape[axis] // SLOT
    w = w.reshape(shape[:axis] + (n, SLOT) + shape[axis + 1:])
    w = lax.slice_in_dim(w, 0, per_head, axis=axis + 1)
    return w.reshape(shape[:axis] + (n * per_head,) + shape[axis + 1:])


def _pad_lanes(v, lo, width=SLOT):
    return jnp.pad(v, [(0, 0)] * (v.ndim - 1) + [(lo, width - lo - v.shape[-1])])


def _pad_rows(v, lo, rows=SLOT):
    return jnp.pad(v, [(lo, rows - lo - v.shape[0])] + [(0, 0)] * (v.ndim - 1))


def _layout_weights(w):
    c = W_IN_CUTS
    w_in = w["w_in_t"]
    p = {}
    p["w_a"] = jnp.concatenate([w_in[c[0]:c[2]], _pad_rows(w_in[c[2]:c[3]], MLA_NOPE), _pad_rows(w_in[c[4]:c[6]], 0)], axis=0)
    p["w_qkv"] = _heads_out(w_in[c[3]:c[4]], GDN_D, axis=0)
    p["w_gate"] = _heads_out(w_in[c[6]:c[7]], GDN_D, axis=0)
    p["w_uq"] = _heads_out(w["uq_t"], MLA_NOPE + MLA_ROPE, axis=0)
    ukv = w["ukv_t"].reshape(N_HEADS, MLA_NOPE + MLA_V, MLA_KV_RANK)
    p["w_kv"] = jnp.concatenate([_heads_out(ukv[:, :MLA_NOPE].reshape(-1, MLA_KV_RANK), MLA_NOPE, axis=0),
                                 _heads_out(ukv[:, MLA_NOPE:].reshape(-1, MLA_KV_RANK), MLA_V, axis=0)], axis=0)
    p["conv"] = _heads_out(w["gdn_conv_w"], GDN_D)
    p["g_mla_out"] = _heads_out(w["mla_out_g"], MLA_V)
    p["g_gdn"] = jnp.tile(_pad_lanes(w["gdn_norm_g"], 0), (1, N_HEADS))
    p["a_log"] = _pad_lanes(w["gdn_a_log"], 0)
    p["dt_bias"] = _pad_lanes(w["gdn_dt_bias"], 0)
    return p


def _unlayout_grads(d):
    c = W_IN_CUTS
    g = {}
    da = d["w_a"]
    kpe0 = A_KPE + MLA_NOPE
    g["w_in_t"] = jnp.concatenate([da[:A_KPE], da[kpe0:kpe0 + MLA_ROPE], _heads_in(d["w_qkv"], GDN_D, axis=0),
                                   da[A_AB:A_AB + 2 * N_HEADS], _heads_in(d["w_gate"], GDN_D, axis=0)], axis=0)
    assert g["w_in_t"].shape[0] == c[-1]
    g["uq_t"] = _heads_in(d["w_uq"], MLA_NOPE + MLA_ROPE, axis=0)
    dk = _heads_in(d["w_kv"][:WIDE], MLA_NOPE, axis=0).reshape(N_HEADS, MLA_NOPE, MLA_KV_RANK)
    dv = _heads_in(d["w_kv"][WIDE:], MLA_V, axis=0).reshape(N_HEADS, MLA_V, MLA_KV_RANK)
    g["ukv_t"] = jnp.concatenate([dk, dv], axis=1).reshape(-1, MLA_KV_RANK)
    g["w_out"] = _heads_in(d["w_out"], GDN_D, axis=0)
    g["gdn_conv_w"] = _heads_in(d["conv"], GDN_D)
    g["mla_out_g"] = _heads_in(d["g_mla_out"], MLA_V)
    g["gdn_norm_g"] = jnp.sum(d["g_gdn"].reshape(N_HEADS, SLOT), axis=0, keepdims=True)[:, :GDN_D]
    g["gdn_a_log"] = d["a_log"][:, :N_HEADS]
    g["gdn_dt_bias"] = d["dt_bias"][:, :N_HEADS]
    return g


def _weight_grad(name, cots, acts, out_dtype=F32, tm=1024, tn=1024, tk=2048, after=None):
    return _matmul(name, cots, acts, "tn", out_dtype=out_dtype, tm=tm, tn=tn, tk=tk, after=after)


def _by_device(a):
    return a.astype(BF16).reshape((N_DEV, a.shape[0] // N_DEV) + a.shape[1:])


def _rows_of(blocks):
    return blocks.reshape((-1,) + blocks.shape[2:])


def _local_step(x, positions, target, w, mid, late):
    tabs = _rope_tables(positions)

    (h1, x1, hg1, hu1, a1), gathered = _ffn_fwd("ffn1_fwd", x, w["ffn1_pre_g"], w["ffn1"], 0, w["ffn1_post_g"], carry=mid)
    w = dict(w, w_in_t=_rows_of(gathered[0]), uq_t=_rows_of(gathered[1]), ukv_t=_rows_of(gathered[2]),
             gdn_conv_w=gathered[3].transpose(1, 0, 2).reshape(CONV_SHAPE))
    p = _layout_weights(w)
    in_weights = [p["w_a"], p["w_qkv"], p["w_gate"]]
    hn, proj_a, proj_qkv, proj_gate = _proj_fwd(x1, w["mix_pre_g"], in_weights)
    cqn, ckvn, q, k, v = _mla_front_fwd(proj_a, tabs, w["mla_q_norm_g"], w["mla_kv_norm_g"], p["w_uq"], p["w_kv"])
    o_mla, lse = _attn_fwd(q, k, v)
    ab = (proj_a, SLOT, A_AB // SLOT)
    qkv_n = _gdn_conv_fwd(proj_qkv, p["conv"])
    gb, bb = _gates_fwd(ab, p["a_log"], p["dt_bias"])
    (o_gdn, keep), (ffn2, w_out) = _gdn_fwd(qkv_n, gb, bb, carry=late)
    p["w_out"] = _heads_out(_rows_of(w_out), GDN_D, axis=0)
    cat, mixed, x2 = _mix_fwd(o_mla, o_gdn, proj_gate, x1, p["g_mla_out"], p["g_gdn"], p["w_out"], w["mix_post_g"])
    (h2, dy, hg2, hu2, a2, loss_lanes), _ = _ffn_fwd("ffn2_fwd", x2, w["ffn2_pre_g"], ffn2, 0, w["ffn2_post_g"], target=target)

    g = {}
    (dx2, xn2, dh2, dhg2, dhu2, g["ffn2_pre_g"], g["ffn2_post_g"]), _ = _ffn_bwd(
        "ffn2_bwd", x2, h2, hg2, hu2, dy, w["ffn2_pre_g"], ffn2, 0, w["ffn2_post_g"])
    ffn2_grads = _Scatter([_by_device(_weight_grad("ffn2_dw_gate", dhg2, xn2, BF16, tm=1408)),
                           _by_device(_weight_grad("ffn2_dw_up", dhu2, xn2, BF16, tm=1408)),
                           _by_device(_weight_grad("ffn2_dw_down", a2, dh2, BF16, tm=1408))])
    d = {}
    dmixed, do_mla, delta, do_gdn, dgate, g["mix_post_g"], d["g_mla_out"], d["g_gdn"] = _mix_bwd(
        o_mla, o_gdn, proj_gate, mixed, dx2, p["g_mla_out"], p["g_gdn"], p["w_out"], w["mix_post_g"])
    d["w_out"] = _weight_grad("mix_out_dw", cat, dmixed, BF16)
    dq, dk, dv = _attn_bwd(q, k, v, do_mla, lse, delta)
    (dqkv_n, dgb, dbb), landed_ffn2 = _gdn_bwd(qkv_n, gb, bb, keep, do_gdn, carry=ffn2_grads)
    dab, d["a_log"], d["dt_bias"] = _gates_bwd(ab, p["a_log"], p["dt_bias"], dgb, dbb)
    dproj_qkv, d["conv"] = _gdn_conv_bwd(proj_qkv, p["conv"], dqkv_n)
    dproj_a, dq_p, dkv_p, g["mla_q_norm_g"], g["mla_kv_norm_g"] = _mla_front_bwd(
        proj_a, tabs, w["mla_q_norm_g"], w["mla_kv_norm_g"], p["w_uq"], p["w_kv"], dq, dk, dv, dab)
    d["w_uq"] = _weight_grad("mla_q_dw", dq_p, cqn, BF16)
    d["w_kv"] = _weight_grad("mla_kv_dw", dkv_p, ckvn, BF16)
    d["w_a"] = _weight_grad("proj_a_dw", dproj_a, hn, BF16, tm=640)
    d["w_qkv"] = _weight_grad("proj_qkv_dw", dproj_qkv, hn, BF16)
    d["w_gate"] = _weight_grad("proj_gate_dw", dgate, hn, BF16)
    dx1, dh1, g["mix_pre_g"] = _proj_bwd(x1, w["mix_pre_g"], in_weights, [dproj_a, dproj_qkv, dgate], dx2,
                                        h1, w["ffn1_post_g"])
    g.update(_unlayout_grads(d))
    behind = [_by_device(g.pop(t)) for t in OTHER.values()]
    behind.append(_by_device(_weight_grad("ffn1_w_down_grad", a1, dh1, BF16, tm=1408)))
    (dx, xn1, _, dhg1, dhu1, g["ffn1_pre_g"], g["ffn1_post_g"]), landed_ffn1 = _ffn_bwd(
        "ffn1_bwd", x, h1, hg1, hu1, dx1, w["ffn1_pre_g"], w["ffn1"], 0, w["ffn1_post_g"], carry=_Scatter(behind))
    landed = dict(zip(list(FFN_NAMES[3:]) + list(OTHER) + ["ffn1_w_down"], list(landed_ffn2) + list(landed_ffn1)))
    begun, token = {}, None
    for name, cots, acts in (("ffn1_w_gate", dhg1, xn1), ("ffn1_w_up", dhu1, xn1)):
        blocks = _by_device(_weight_grad(name + "_grad", cots, acts, BF16, tm=1408, after=token))
        begun[name], token = _scatter_begin("scatter_" + name + "_begin", blocks)
    packed = _pack_small(g, g["gdn_conv_w"].reshape(-1), REDUCE_ROWS)
    packed = packed.at[REDUCE_ROWS - 1, ROW - 1].set(jnp.sum(loss_lanes))
    begun["small"], small_token = _scatter_begin("reduce_small_begin", jnp.broadcast_to(packed, (N_DEV,) + packed.shape))
    return dx, g, landed, begun, token + small_token


MESH_AXES = ("x", "y", "c")
N_LINKS = N_DEV - 1


def _place():
    return tuple(lax.axis_index(a) for a in MESH_AXES)


def _block_of(dev):
    x, y, c = dev
    return 4 * x + 2 * y + c


def _remote_copy(src, dst, sems, k, to):
    send_sems, recv_sems = sems
    return pltpu.make_async_remote_copy(src_ref=src, dst_ref=dst, send_sem=send_sems.at[k], recv_sem=recv_sems.at[k],
                                        device_id=to, device_id_type=pl.DeviceIdType.MESH)


class _Exchange:
    def __init__(self, arrays):
        self.arrays = list(arrays)
        self.n = len(self.arrays)
        self.specs = [pl.BlockSpec(memory_space=pl.ANY)] * self.n
        self.scratch = [pltpu.SemaphoreType.DMA((self.n * N_LINKS,)), pltpu.SemaphoreType.DMA((self.n * N_LINKS,)),
                        pltpu.SemaphoreType.DMA((self.n,))]

    def split(self, refs):
        n = self.n
        return refs[:n], refs[n:2 * n], (refs[2 * n], refs[2 * n + 1]), refs[2 * n + 2]


class _Gather(_Exchange):
    def out_shape(self):
        return [jax.ShapeDtypeStruct((N_DEV,) + a.shape, a.dtype) for a in self.arrays]

    def _plan(self, ins, outs, sems, local_sems):
        x, y, c = _place()
        me, sibling = (x, y, c), (x, y, 1 - c)
        chips = [(1 - x, y), (x, 1 - y), (1 - x, 1 - y)]

        def copy(a, k, block, to, mine=False):
            src = ins[a] if mine else outs[a].at[_block_of(block)]
            return _remote_copy(src, outs[a].at[_block_of(block)], sems, a * N_LINKS + k, to)

        local = [pltpu.make_async_copy(ins[a], outs[a].at[_block_of(me)], local_sems.at[a]) for a in range(self.n)]
        first = []
        for a in range(self.n):
            first.append(copy(a, 0, me, sibling, mine=True))
            first += [copy(a, 1 + j, me, (*chip, c), mine=True) for j, chip in enumerate(chips)]
        return me, sibling, chips, c, copy, local, first

    def start(self, ins, outs, sems, local_sems):
        *_, local, first = self._plan(ins, outs, sems, local_sems)
        for cp in local + first:
            cp.start()

    def finish(self, ins, outs, sems, local_sems):
        me, sibling, chips, c, copy, local, first = self._plan(ins, outs, sems, local_sems)
        passed = []
        for j, chip in enumerate(chips):
            for a in range(self.n):
                copy(a, 1 + j, (*chip, c), me).wait_recv()
                passed.append(copy(a, 4 + j, (*chip, c), sibling))
                passed[-1].start()
        for a in range(self.n):
            copy(a, 0, sibling, me).wait_recv()
            for j, chip in enumerate(chips):
                copy(a, 4 + j, (*chip, 1 - c), me).wait_recv()
        for cp in first + passed:
            cp.wait_send()
        for cp in local:
            cp.wait()


class _Scatter(_Exchange):
    def out_shape(self):
        return [jax.ShapeDtypeStruct(a.shape, a.dtype) for a in self.arrays]

    def _plan(self, ins, outs, sems, local_sems):
        x, y, c = _place()
        me = _block_of((x, y, c))

        def peer(r):
            return (1 - x if r & 4 else x, 1 - y if r & 2 else y, 1 - c if r & 1 else c)

        local = [pltpu.make_async_copy(ins[a].at[me], outs[a].at[me], local_sems.at[a]) for a in range(self.n)]
        sends = [_remote_copy(ins[a].at[_block_of(peer(r))], outs[a].at[me], sems, a * N_LINKS + r - 1, peer(r))
                 for a in range(self.n) for r in range(1, N_DEV)]
        arrivals = [_remote_copy(ins[a].at[me], outs[a].at[_block_of(peer(r))], sems, a * N_LINKS + r - 1, peer(r))
                    for a in range(self.n) for r in range(1, N_DEV)]
        return local, sends, arrivals

    def start(self, ins, outs, sems, local_sems):
        local, sends, _ = self._plan(ins, outs, sems, local_sems)
        for cp in local + sends:
            cp.start()

    def finish(self, ins, outs, sems, local_sems):
        local, sends, arrivals = self._plan(ins, outs, sems, local_sems)
        for cp in arrivals:
            cp.wait_recv()
        for cp in sends:
            cp.wait_send()
        for cp in local:
            cp.wait()


def _exchange(name, plan):
    def body(*refs):
        parts = plan.split(refs)
        plan.start(*parts)
        plan.finish(*parts)

    return pl.pallas_call(
        body, name=name,
        in_specs=plan.specs,
        out_specs=plan.specs,
        out_shape=plan.out_shape(),
        scratch_shapes=plan.scratch,
    )(*plan.arrays)


def _call_carrying(body, plan, operands, *, name, grid, in_specs, out_specs, out_shape, scratch_shapes, compiler_params):
    if plan is None:
        outs = pl.pallas_call(body, name=name, grid=grid, in_specs=in_specs, out_specs=out_specs, out_shape=out_shape,
                              scratch_shapes=scratch_shapes, compiler_params=compiler_params)(*operands)
        return outs, []
    n_i, n_o, n_s, k = len(in_specs), len(out_specs), len(scratch_shapes), plan.n

    def whole(*refs):
        cut = [n_i, n_i + k, n_i + k + n_o, n_i + 2 * k + n_o, n_i + 2 * k + n_o + n_s]
        own_in, ex_in, own_out, ex_out, own_scr, ex_scr = (refs[a:b] for a, b in zip([0] + cut, cut + [len(refs)]))
        parts = plan.split(ex_in + ex_out + ex_scr)
        first = last = True
        for axis, size in enumerate(grid):
            first = first & (pl.program_id(axis) == 0)
            last = last & (pl.program_id(axis) == size - 1)

        @pl.when(first)
        def _():
            plan.start(*parts)

        body(*own_in, *own_out, *own_scr)

        @pl.when(last)
        def _():
            plan.finish(*parts)

    outs = pl.pallas_call(
        whole, name=name, grid=grid,
        in_specs=list(in_specs) + plan.specs, out_specs=list(out_specs) + plan.specs,
        out_shape=list(out_shape) + plan.out_shape(), scratch_shapes=list(scratch_shapes) + plan.scratch,
        compiler_params=compiler_params,
    )(*operands, *plan.arrays)
    return outs[:n_o], outs[n_o:]


def _row_tile(rows, target=256):
    best = rows
    for cand in range(16, min(rows, target) + 1, 16):
        if rows % cand == 0:
            best = cand
    return best


def _sum_blocks(name, blocks, after=None):
    rows, width = blocks.shape[-2:]
    tm = _row_tile(rows)

    def body(x_ref, *rest):
        acc = x_ref[0].astype(F32)
        for d in range(1, N_DEV):
            acc = acc + x_ref[d].astype(F32)
        rest[-1][...] = acc

    ordered = [] if after is None else [after]
    return pl.pallas_call(
        body, name=name,
        grid=(rows // tm,),
        in_specs=[pl.BlockSpec((N_DEV, tm, width), lambda i: (0, i, 0))] + [pl.BlockSpec(memory_space=pl.ANY)] * len(ordered),
        out_specs=pl.BlockSpec((tm, width), lambda i: (i, 0)),
        out_shape=jax.ShapeDtypeStruct((rows, width), F32),
        compiler_params=pltpu.CompilerParams(dimension_semantics=("parallel",)),
    )(blocks, *ordered)


def _split_plan(src_ref, land_ref, sems):
    x, y, c = _place()
    me = _block_of((x, y, c))

    def peer(r):
        return (1 - x if r & 4 else x, 1 - y if r & 2 else y, 1 - c if r & 1 else c)

    sends = [_remote_copy(src_ref.at[_block_of(peer(r))], land_ref.at[me], sems, r - 1, peer(r)) for r in range(1, N_DEV)]
    arrivals = [_remote_copy(src_ref.at[me], land_ref.at[_block_of(peer(r))], sems, r - 1, peer(r)) for r in range(1, N_DEV)]
    return sends, arrivals


def _scatter_begin(name, blocks):
    def body(src_ref, land_ref, send_sems, recv_sems, src_thru, land_thru, token_ref):
        for cp in _split_plan(src_ref, land_ref, (send_sems, recv_sems))[0]:
            cp.start()
        token_ref[...] = jnp.zeros_like(token_ref)

    hbm, sem = pl.BlockSpec(memory_space=pltpu.HBM), pl.BlockSpec(memory_space=pltpu.SEMAPHORE)
    zone = pltpu.HBM(blocks.shape, blocks.dtype)
    *handles, token = pl.pallas_call(
        body, name=name,
        in_specs=(hbm, hbm),
        out_specs=(sem, sem, hbm, hbm, pl.BlockSpec(memory_space=pltpu.VMEM)),
        out_shape=(pltpu.SemaphoreType.DMA((N_LINKS,)), pltpu.SemaphoreType.DMA((N_LINKS,)), zone, zone,
                   jax.ShapeDtypeStruct((8, SLOT), F32)),
        input_output_aliases={0: 2, 1: 3},
        compiler_params=pltpu.CompilerParams(has_side_effects=pltpu.SideEffectType.DATAFLOW_SIDE_EFFECTING),
    )(pltpu.with_memory_space_constraint(blocks, pltpu.HBM),
      pltpu.with_memory_space_constraint(lax.empty(blocks.shape, blocks.dtype), pltpu.HBM))
    return handles, token


def _scatter_end(name, handles, after):
    send_sems, recv_sems, src, zone = handles

    def body(src_ref, land_ref, send_sems, recv_sems, after_ref, src_dead, got_ref):
        sends, arrivals = _split_plan(src_ref, land_ref, (send_sems, recv_sems))
        for cp in arrivals:
            cp.wait_recv()
        for cp in sends:
            cp.wait_send()

    hbm, sem = pl.BlockSpec(memory_space=pltpu.HBM), pl.BlockSpec(memory_space=pltpu.SEMAPHORE)
    sent, landed = pl.pallas_call(
        body, name=name,
        in_specs=(hbm, hbm, sem, sem, pl.BlockSpec(memory_space=pl.ANY)),
        out_specs=(hbm, hbm),
        out_shape=(pltpu.HBM(src.shape, src.dtype), pltpu.HBM(zone.shape, zone.dtype)),
        input_output_aliases={0: 0, 1: 1},
        compiler_params=pltpu.CompilerParams(has_side_effects=pltpu.SideEffectType.DATAFLOW_SIDE_EFFECTING),
    )(src, zone, send_sems, recv_sems, after)
    me = _block_of(_place())
    return lax.dynamic_update_slice_in_dim(landed, lax.dynamic_slice_in_dim(sent, me, 1, axis=0), me, axis=0)


def _adamw_values(wv, gv, mv, vv):
    m2 = ADAM_B1 * mv + (1.0 - ADAM_B1) * gv
    v2 = ADAM_B2 * vv + (1.0 - ADAM_B2) * jnp.square(gv)
    m_hat = m2 / (1.0 - ADAM_B1 ** ADAM_STEP)
    v_hat = v2 / (1.0 - ADAM_B2 ** ADAM_STEP)
    return [-ADAM_LR * (m_hat / (jnp.sqrt(v_hat) + ADAM_EPS) + ADAM_WD * wv), m2, v2]


def _adamw(name, w, g, m, v):
    def fn(rows, consts):
        return _adamw_values(*rows), []

    return _rowwise(name, fn, [w, g, m, v], [], [(w.shape[1], F32)] * 3, tm=_row_tile(w.shape[0]))


def _sum_adamw(name, blocks, w, m, v, after=None):
    rows, width = w.shape
    tm = _row_tile(rows)

    def body(x_ref, w_ref, m_ref, v_ref, *rest):
        acc = x_ref[0].astype(F32)
        for d in range(1, N_DEV):
            acc = acc + x_ref[d].astype(F32)
        rest[-4][...] = acc
        for ref, val in zip(rest[-3:], _adamw_values(w_ref[...], acc, m_ref[...], v_ref[...])):
            ref[...] = val

    ordered = [] if after is None else [after]
    tile = pl.BlockSpec((tm, width), lambda i: (i, 0))
    return pl.pallas_call(
        body, name=name,
        grid=(rows // tm,),
        in_specs=[pl.BlockSpec((N_DEV, tm, width), lambda i: (0, i, 0))] + [tile] * 3 + [pl.BlockSpec(memory_space=pl.ANY)] * len(ordered),
        out_specs=[tile] * 4,
        out_shape=[jax.ShapeDtypeStruct((rows, width), F32)] * 4,
        compiler_params=pltpu.CompilerParams(dimension_semantics=("parallel",)),
    )(blocks, w, m, v, *ordered)


ROW = 1024
FFN_NAMES = ("ffn1_w_gate", "ffn1_w_up", "ffn1_w_down", "ffn2_w_gate", "ffn2_w_up", "ffn2_w_down")
OTHER = {"w_in": "w_in_t", "mla_w_uq": "uq_t", "mla_w_ukv": "ukv_t", "w_out": "w_out"}
BY_COLUMNS = ("ffn1_w_gate", "ffn1_w_up", "ffn2_w_gate", "ffn2_w_up", "w_in", "mla_w_uq", "mla_w_ukv")
SMALL = {
    "ffn1_pre_g": (1024, 1024), "ffn1_post_g": (1024, 1024), "mix_pre_g": (1024, 1024), "mla_q_norm_g": (256, 256),
    "mla_kv_norm_g": (128, 128), "mla_out_g": (512, 512), "gdn_a_log": (8, 128), "gdn_dt_bias": (8, 128),
    "gdn_norm_g": (64, 128), "mix_post_g": (1024, 1024), "ffn2_pre_g": (1024, 1024), "ffn2_post_g": (1024, 1024),
}
CONV_SHAPE = (GDN_CONV, 3 * N_HEADS * GDN_D)
CONV_SHARD = (GDN_CONV, CONV_SHAPE[1] // N_DEV)
CONV_LANES = CONV_SHAPE[0] * CONV_SHAPE[1]
SMALL_ROWS = 8
REDUCE_ROWS = 16


def _pack_small(vecs, conv, rows):
    parts = [_pad_lanes(vecs[n].reshape(1, -1), 0, r) for n, (_, r) in SMALL.items()]
    parts.append(conv.reshape(1, -1))
    flat = jnp.concatenate(parts, axis=1)
    return _pad_lanes(flat, 0, rows * ROW).reshape(rows, ROW)


def _unpack_small(buf):
    flat = buf.reshape(1, -1)
    out, at = {}, 0
    for n, (w, r) in SMALL.items():
        out[n] = flat[:, at:at + w]
        at += r
    return out, flat[0, at:]


def kernel(x, positions, ffn1_pre_g, ffn1_w_gate, ffn1_w_up, ffn1_w_down, ffn1_post_g, mix_pre_g, w_in, mla_q_norm_g, mla_w_uq, mla_kv_norm_g, mla_w_ukv, mla_out_g, gdn_conv_w, gdn_a_log, gdn_dt_bias, gdn_norm_g, w_out, mix_post_g, ffn2_pre_g, ffn2_w_gate, ffn2_w_up, ffn2_w_down, ffn2_post_g, loss_target, m_ffn1_pre_g, m_ffn1_w_gate, m_ffn1_w_up, m_ffn1_w_down, m_ffn1_post_g, m_mix_pre_g, m_w_in, m_mla_q_norm_g, m_mla_w_uq, m_mla_kv_norm_g, m_mla_w_ukv, m_mla_out_g, m_gdn_conv_w, m_gdn_a_log, m_gdn_dt_bias, m_gdn_norm_g, m_w_out, m_mix_post_g, m_ffn2_pre_g, m_ffn2_w_gate, m_ffn2_w_up, m_ffn2_w_down, m_ffn2_post_g, v_ffn1_pre_g, v_ffn1_w_gate, v_ffn1_w_up, v_ffn1_w_down, v_ffn1_post_g, v_mix_pre_g, v_w_in, v_mla_q_norm_g, v_mla_w_uq, v_mla_kv_norm_g, v_mla_w_ukv, v_mla_out_g, v_gdn_conv_w, v_gdn_a_log, v_gdn_dt_bias, v_gdn_norm_g, v_w_out, v_mix_post_g, v_ffn2_pre_g, v_ffn2_w_gate, v_ffn2_w_up, v_ffn2_w_down, v_ffn2_post_g):
    given = dict(locals())
    order = ["ffn1_pre_g", "ffn1_w_gate", "ffn1_w_up", "ffn1_w_down", "ffn1_post_g", "mix_pre_g", "w_in", "mla_q_norm_g",
             "mla_w_uq", "mla_kv_norm_g", "mla_w_ukv", "mla_out_g", "gdn_conv_w", "gdn_a_log", "gdn_dt_bias", "gdn_norm_g",
             "w_out", "mix_post_g", "ffn2_pre_g", "ffn2_w_gate", "ffn2_w_up", "ffn2_w_down", "ffn2_post_g"]
    assert sorted(order) == sorted(list(FFN_NAMES) + list(OTHER) + list(SMALL) + ["gdn_conv_w"])

    def drop_depth(a):
        return a[0] if a.ndim == 3 else a

    wts = {n: drop_depth(given[n]) for n in order}
    mom = {n: drop_depth(given["m_" + n]) for n in order}
    var = {n: drop_depth(given["v_" + n]) for n in order}
    me = _block_of(_place())

    def wire(n):
        return (wts[n].T if n in BY_COLUMNS else wts[n]).astype(BF16)

    (ffn1,) = _exchange("gather_first", _Gather([jnp.stack([wire(n) for n in FFN_NAMES[:3]])]))
    mid = _Gather([wire(n) for n in ("w_in", "mla_w_uq", "mla_w_ukv")] + [wts["gdn_conv_w"]])
    late = _Gather([jnp.stack([wire(n) for n in FFN_NAMES[3:]]), wire("w_out")])
    full = {n: wts[n] for n in SMALL}
    full["ffn1"] = ffn1

    dx, grads, landed, begun, token = _local_step(x[0], positions[0], loss_target[0], full, mid, late)

    grad, outs = {}, {"delta": {}, "new_m": {}, "new_v": {}}

    def finish(n, blocks, after=None):
        flip = n in BY_COLUMNS and wts[n].shape[1] % SLOT != 0
        turn = (lambda a: a.T) if flip else (lambda a: a)
        if n in BY_COLUMNS and not flip:
            grad[n] = _sum_blocks("sum_" + n, blocks, after=after).T
            new = _adamw("adamw_" + n, wts[n], grad[n], mom[n], var[n])
        else:
            total, *new = _sum_adamw("update_" + n, blocks, turn(wts[n]), turn(mom[n]), turn(var[n]), after=after)
            grad[n] = turn(total)
        outs["delta"][n], outs["new_m"][n], outs["new_v"][n] = (turn(a) for a in new)
        return new[2]

    for n, blocks in landed.items():
        token = finish(n, blocks, after=token)
    small_handles = begun.pop("small")
    for n, handles in begun.items():
        token = finish(n, _scatter_end("scatter_" + n + "_end", handles, after=token))

    small_sum = _sum_blocks("sum_small", _scatter_end("reduce_small_end", small_handles, after=token))
    loss = small_sum[REDUCE_ROWS - 1, ROW - 1]
    small_grad, conv_grad_full = _unpack_small(small_sum)
    grad.update(small_grad)
    grad["gdn_conv_w"] = lax.dynamic_slice(conv_grad_full[:CONV_LANES].reshape(CONV_SHAPE), (0, me * CONV_SHARD[1]), CONV_SHARD)
    outs["grad"] = grad
    small = [_pack_small(s, s["gdn_conv_w"].reshape(-1), SMALL_ROWS) for s in (wts, grad, mom, var)]
    for kind, s in zip(("delta", "new_m", "new_v"), _adamw("adamw_small", *small)):
        vecs, conv = _unpack_small(s)
        outs[kind].update(vecs)
        outs[kind]["gdn_conv_w"] = conv[:CONV_SHARD[0] * CONV_SHARD[1]].reshape(CONV_SHARD)
    result = [loss, dx[None]]
    for kind in ("grad", "delta", "new_m", "new_v"):
        result += [outs[kind][n].reshape(given[n].shape) for n in order]
    return tuple(result)
```

```python
import jax
import jax.numpy as jnp
from jax import lax
from jax.experimental import pallas as pl
from jax.experimental.pallas import tpu as pltpu

F32 = jnp.float32
BF16 = jnp.bfloat16
HI = lax.Precision.HIGH

N_DEV = 8
N_HEADS = 8
SLOT = 128
MLA_Q_RANK = 256
MLA_KV_RANK = 128
MLA_NOPE = 64
MLA_ROPE = 32
MLA_V = 64
GDN_D = 64
GDN_CONV = 4
GDN_CHUNK = 64
ROPE_THETA = 10000.0
EPS = 1e-6
ADAM_LR, ADAM_B1, ADAM_B2, ADAM_EPS, ADAM_WD, ADAM_STEP = 0.001, 0.9, 0.999, 1e-08, 0.01, 10


def _dot(a, b, ca, cb, precision=None):
    lead = a.ndim - 2
    batch = tuple(range(lead))
    return lax.dot_general(a, b, (((lead + ca,), (lead + cb,)), (batch, batch)), precision=precision,
                           preferred_element_type=F32)


def _nn(a, b, precision=None):
    return _dot(a, b, 1, 0, precision)


def _nt(a, b, precision=None):
    return _dot(a, b, 1, 1, precision)


def _tn(a, b, precision=None):
    return _dot(a, b, 0, 0, precision)


def _sigmoid(x):
    return 1.0 / (1.0 + jnp.exp(-x))


def _silu(x):
    return x * _sigmoid(x)


def _rms(x, g, n):
    ms = jnp.sum(x * x, axis=-1, keepdims=True) * (1.0 / n)
    return x * lax.rsqrt(ms + EPS) * g


def _chunk_masks():
    c = GDN_CHUNK
    i = lax.broadcasted_iota(jnp.int32, (c, c), 0)
    j = lax.broadcasted_iota(jnp.int32, (c, c), 1)
    lower = i >= j
    strict = i > j
    eye = (i == j).astype(F32)
    blocks = []
    b = 1
    while b < c:
        same = (i // (2 * b)) == (j // (2 * b))
        blocks.append(same & ((i % (2 * b)) >= b) & ((j % (2 * b)) < b))
        b *= 2
    return lower, strict, eye, blocks


def _unit_lower_inverse(low, eye, blocks):
    t = eye - jnp.where(blocks[0], low, 0.0)
    for m in blocks[1:]:
        lo = jnp.where(m, low, 0.0)
        t = t - _nn(t, _nn(lo, t, HI), HI)
    return t


@jax.custom_vjp
def _known_inverse(low, tinv):
    return tinv


def _known_inverse_fwd(low, tinv):
    return tinv, tinv


def _known_inverse_bwd(tinv, dt):
    return -_tn(tinv, _nt(dt, tinv, HI), HI), jnp.zeros_like(tinv)


_known_inverse.defvjp(_known_inverse_fwd, _known_inverse_bwd)

_PRODUCTS = {"nn": _nn, "nt": _nt, "tn": _tn}


@jax.custom_vjp
def _known_nn(a, b, c):
    return c


@jax.custom_vjp
def _known_nt(a, b, c):
    return c


@jax.custom_vjp
def _known_tn(a, b, c):
    return c


def _known_fwd(a, b, c):
    return c, (a, b, c)


_known_nn.defvjp(_known_fwd, lambda r, dc: (_nt(dc, r[1], HI), _tn(r[0], dc, HI), jnp.zeros_like(r[2])))
_known_nt.defvjp(_known_fwd, lambda r, dc: (_nn(dc, r[1], HI), _tn(dc, r[0], HI), jnp.zeros_like(r[2])))
_known_tn.defvjp(_known_fwd, lambda r, dc: (_nt(r[1], dc, HI), _nn(r[0], dc, HI), jnp.zeros_like(r[2])))
_KNOWN = {"nn": _known_nn, "nt": _known_nt, "tn": _known_tn}
GDN_PRODUCTS = 8
GDN_KEPT = 2 + GDN_PRODUCTS


def _gdn_chunk(q, k, v, gc, bb, s, masks, known=None):
    lower, strict, eye, blocks = masks
    made = []

    def product(kind, a, b):
        c = _PRODUCTS[kind](a, b, HI) if known is None else _KNOWN[kind](a, b, known[1 + len(made)])
        made.append(c)
        return c

    qs = q * (GDN_D ** -0.5)
    gct = jnp.swapaxes(gc, -1, -2)
    decay = jnp.exp(jnp.where(lower, gc - gct, -1e30))
    kb = k * bb
    low = jnp.where(strict, product("nt", kb, k) * decay, 0.0)
    tinv = _unit_lower_inverse(low, eye, blocks) if known is None else _known_inverse(low, known[0])
    eg = jnp.exp(gc)
    w = product("nn", tinv, kb * eg)
    u = product("nn", tinv, v * bb)
    attn = product("nt", qs, k) * decay
    last = lax.broadcasted_iota(jnp.int32, gc.shape[-2:], 0) == GDN_CHUNK - 1
    g_end = jnp.sum(jnp.where(last, gc, 0.0), axis=-2, keepdims=True)
    k_dec = k * jnp.exp(g_end - gc)
    v_new = u - product("nn", w, s)
    o = product("nn", qs * eg, s) + product("nn", attn, v_new)
    s_new = s * jnp.exp(g_end) + product("tn", k_dec, v_new)
    assert len(made) == GDN_PRODUCTS
    return o, s_new, [tinv] + made


GDN_GROUP = 8
GDN_GROUPS = N_HEADS // GDN_GROUP


def _group_heads(ref):
    return jnp.stack([ref[:, pl.ds(j * SLOT, GDN_D)] for j in range(GDN_GROUP)])


def _ungroup_heads(ref, val):
    pad = jnp.zeros((GDN_CHUNK, SLOT - GDN_D), F32)
    for j in range(GDN_GROUP):
        ref[:, pl.ds(j * SLOT, GDN_D)] = val[j]
        ref[:, pl.ds(j * SLOT + GDN_D, SLOT - GDN_D)] = pad


def _gdn_fwd(qkv, gb, bb, carry=None):
    t = qkv.shape[0]
    n_chunks = t // GDN_CHUNK
    d = GDN_D

    def body(q_ref, k_ref, v_ref, g_ref, b_ref, o_ref, keep_ref, s_ref):
        @pl.when(pl.program_id(1) == 0)
        def _():
            s_ref[...] = jnp.zeros_like(s_ref)

        s = s_ref[...]
        keep_ref[:, 0, 0] = s
        o, s_new, made = _gdn_chunk(*[_group_heads(r) for r in (q_ref, k_ref, v_ref, g_ref, b_ref)], s, _chunk_masks())
        for i, val in enumerate(made):
            keep_ref[:, 0, 1 + i] = val
        s_ref[...] = s_new
        _ungroup_heads(o_ref, o)

    def spec(kind=0):
        return pl.BlockSpec((GDN_CHUNK, GDN_GROUP * SLOT), lambda h, n: (n, kind * GDN_GROUPS + h))

    return _call_carrying(
        body, carry, (qkv, qkv, qkv, gb, bb), name="gdn_fwd",
        grid=(GDN_GROUPS, n_chunks),
        in_specs=[spec(0), spec(1), spec(2), spec(), spec()],
        out_specs=[spec(), pl.BlockSpec((GDN_GROUP, 1, GDN_KEPT, d, d), lambda h, n: (h, n, 0, 0, 0))],
        out_shape=[jax.ShapeDtypeStruct((t, N_HEADS * SLOT), F32), jax.ShapeDtypeStruct((N_HEADS, n_chunks, GDN_KEPT, d, d), F32)],
        scratch_shapes=[pltpu.VMEM((GDN_GROUP, d, d), F32)],
        compiler_params=pltpu.CompilerParams(dimension_semantics=("arbitrary", "arbitrary")),
    )


def _gdn_bwd(qkv, gb, bb, keep, do, carry=None):
    t = qkv.shape[0]
    n_chunks = t // GDN_CHUNK
    d = GDN_D

    def body(q_ref, k_ref, v_ref, g_ref, b_ref, keep_ref, do_ref, dqkv_ref, dg_ref, db_ref, ds_ref):
        @pl.when(pl.program_id(1) == 0)
        def _():
            ds_ref[...] = jnp.zeros_like(ds_ref)

        masks = _chunk_masks()
        known = [keep_ref[:, 0, 1 + i] for i in range(GDN_KEPT - 1)]
        _, pull = jax.vjp(lambda *a: _gdn_chunk(*a, masks, known)[:2],
                          *[_group_heads(r) for r in (q_ref, k_ref, v_ref, g_ref, b_ref)], keep_ref[:, 0, 0])
        dq, dk, dv, dg, db, ds = pull((_group_heads(do_ref), ds_ref[...]))
        ds_ref[...] = ds
        for i, val in enumerate((dq, dk, dv)):
            _ungroup_heads(dqkv_ref.at[i], val)
        _ungroup_heads(dg_ref, dg)
        _ungroup_heads(db_ref, db)

    def spec(kind=0):
        return pl.BlockSpec((GDN_CHUNK, GDN_GROUP * SLOT), lambda h, n: (n_chunks - 1 - n, kind * GDN_GROUPS + h))

    return _call_carrying(
        body, carry, (qkv, qkv, qkv, gb, bb, keep, do), name="gdn_bwd",
        grid=(GDN_GROUPS, n_chunks),
        in_specs=[spec(0), spec(1), spec(2), spec(), spec(),
                  pl.BlockSpec((GDN_GROUP, 1, GDN_KEPT, d, d), lambda h, n: (h, n_chunks - 1 - n, 0, 0, 0)), spec()],
        out_specs=[pl.BlockSpec((3, GDN_CHUNK, GDN_GROUP * SLOT), lambda h, n: (0, n_chunks - 1 - n, h)), spec(), spec()],
        out_shape=[jax.ShapeDtypeStruct((3, t, N_HEADS * SLOT), F32)] + [jax.ShapeDtypeStruct((t, N_HEADS * SLOT), F32)] * 2,
        scratch_shapes=[pltpu.VMEM((GDN_GROUP, d, d), F32)],
        compiler_params=pltpu.CompilerParams(dimension_semantics=("arbitrary", "arbitrary")),
    )


def _rowwise(name, fn, rows, consts, outs, sums=(), tm=512):
    rows = [x if isinstance(x, tuple) else (x, x.shape[1], 0) for x in rows]
    t = rows[0][0].shape[0]
    tm = min(tm, t)
    steps = t // tm
    n_r, n_c, n_o, n_s = len(rows), len(consts), len(outs), len(sums)

    def window(width, block):
        return pl.BlockSpec((tm, width), lambda i: (i, block))

    def body(*refs):
        r, c = refs[:n_r], refs[n_r:n_r + n_c]
        o, s = refs[n_r + n_c:n_r + n_c + n_o], refs[n_r + n_c + n_o:]
        vals, tot = fn([x[...] for x in r], [x[...] for x in c])
        for ref, val in zip(o, vals):
            ref[...] = val.astype(ref.dtype)
        if n_s:
            @pl.when(pl.program_id(0) == 0)
            def _():
                for ref in s:
                    ref[...] = jnp.zeros_like(ref)

            for ref, val in zip(s, tot):
                ref[...] += val

    return pl.pallas_call(
        body, name=name,
        grid=(steps,),
        in_specs=[window(w, b) for _, w, b in rows] + [pl.BlockSpec(x.shape, lambda i: (0, 0)) for x in consts],
        out_specs=[pl.BlockSpec((tm, w), lambda i: (i, 0)) for w, _ in outs]
        + [pl.BlockSpec((1, w), lambda i: (0, 0)) for w in sums],
        out_shape=[jax.ShapeDtypeStruct((t, w), dt) for w, dt in outs]
        + [jax.ShapeDtypeStruct((1, w), F32) for w in sums],
        compiler_params=pltpu.CompilerParams(dimension_semantics=("arbitrary",)),
    )(*[x for x, _, _ in rows], *consts)


def _tile(dim, target):
    if dim <= target:
        return dim
    best = None
    for cand in range(128, target + 1, 128):
        if dim % cand == 0:
            best = cand
    assert best is not None, (dim, target)
    return best


def _matmul(name, a, b, mode, out_dtype=F32, tm=1024, tn=1024, tk=2048, after=None):
    if mode == "nn":
        (m, k), n = a.shape, b.shape[1]
    elif mode == "nt":
        (m, k), n = a.shape, b.shape[0]
    else:
        (k, m), n = a.shape, b.shape[1]
    tm, tn, tk = _tile(m, tm), _tile(n, tn), _tile(k, tk)
    k_steps = k // tk
    product = {"nn": _nn, "nt": _nt, "tn": _tn}[mode]

    def body(a_ref, b_ref, *rest):
        o_ref, acc_ref = rest[-2:]
        part = product(a_ref[...].astype(BF16), b_ref[...].astype(BF16))
        if k_steps == 1:
            o_ref[...] = part.astype(o_ref.dtype)
        else:
            kk = pl.program_id(2)

            @pl.when(kk == 0)
            def _():
                acc_ref[...] = part

            @pl.when(kk > 0)
            def _():
                acc_ref[...] += part

            @pl.when(kk == k_steps - 1)
            def _():
                o_ref[...] = acc_ref[...].astype(o_ref.dtype)

    a_spec = pl.BlockSpec((tk, tm), lambda i, j, kk: (kk, i)) if mode == "tn" else pl.BlockSpec((tm, tk), lambda i, j, kk: (i, kk))
    b_spec = pl.BlockSpec((tn, tk), lambda i, j, kk: (j, kk)) if mode == "nt" else pl.BlockSpec((tk, tn), lambda i, j, kk: (kk, j))
    ordered = [] if after is None else [after]
    return pl.pallas_call(
        body, name=name,
        grid=(m // tm, n // tn, k_steps),
        in_specs=[a_spec, b_spec] + [pl.BlockSpec(memory_space=pl.ANY)] * len(ordered),
        out_specs=pl.BlockSpec((tm, tn), lambda i, j, kk: (i, j)),
        out_shape=jax.ShapeDtypeStruct((m, n), out_dtype),
        scratch_shapes=[pltpu.VMEM((tm, tn) if k_steps > 1 else (8, 128), F32)],
        compiler_params=pltpu.CompilerParams(dimension_semantics=("parallel", "parallel", "arbitrary")),
    )(a, b, *ordered)


FFN_TM = 512
FFN_BWD_TM = 256
FFN_BLOCKS = 4
FFN_GATE, FFN_UP, FFN_DOWN = 0, 1, 2


def _ffn_weight_specs(ffn_w, first):
    _, _, rows, dm = ffn_w.shape

    def spec(k):
        return pl.BlockSpec((FFN_BLOCKS, None, rows, dm), lambda i, j: (j, first + k, 0, 0))

    return [spec(FFN_GATE), spec(FFN_UP), spec(FFN_DOWN)], FFN_BLOCKS * rows


def _ffn_fwd(name, x, g_pre, ffn_w, first, g_post, carry=None, target=None):
    t, dm = x.shape
    tm = min(FFN_TM, t)
    w_specs, tf = _ffn_weight_specs(ffn_w, first)
    f_steps = N_DEV // FFN_BLOCKS
    n_t = 0 if target is None else 1

    def body(x_ref, gpre_ref, wg_ref, wu_ref, wd_ref, gpost_ref, *rest):
        tgt_ref, loss_ref = rest[:n_t], rest[n_t + 5:2 * n_t + 5]
        h_ref, y_ref, hg_ref, hu_ref, a_ref = rest[n_t:n_t + 5]
        xn_ref, acc_ref = rest[-2:]
        i, j = pl.program_id(0), pl.program_id(1)

        @pl.when((i == 0) & (j == 0))
        def _():
            for ref in loss_ref:
                ref[...] = jnp.zeros_like(ref)

        @pl.when(j == 0)
        def _():
            xn_ref[...] = _rms(x_ref[...], gpre_ref[...], dm).astype(BF16)
            acc_ref[...] = jnp.zeros_like(acc_ref)

        xn = xn_ref[...]
        wg, wu, wd = (r[...].reshape(tf, dm) for r in (wg_ref, wu_ref, wd_ref))
        hg, hu = _nt(xn, wg), _nt(xn, wu)
        hg_ref[...] = hg.astype(BF16)
        hu_ref[...] = hu.astype(BF16)
        a = (_silu(hg) * hu).astype(BF16)
        a_ref[...] = a
        acc_ref[...] += _nn(a, wd)

        @pl.when(j == f_steps - 1)
        def _():
            h = acc_ref[...]
            h_ref[...] = h
            y = x_ref[...] + 0.5 * _rms(h, gpost_ref[...], dm)
            if n_t:
                err = y - tgt_ref[0][...]
                sq = err * err
                lanes = sq[:, :SLOT]
                for k in range(1, dm // SLOT):
                    lanes = lanes + sq[:, k * SLOT:(k + 1) * SLOT]
                loss_ref[0][...] += jnp.sum(lanes, axis=0, keepdims=True) * (0.5 / dm)
                y = err * (1.0 / dm)
            y_ref[...] = y

    row = pl.BlockSpec((tm, dm), lambda i, j: (i, 0))
    vec = pl.BlockSpec((1, dm), lambda i, j: (0, 0))
    wide = pl.BlockSpec((tm, tf), lambda i, j: (i, j))
    targets = [] if target is None else [target]
    return _call_carrying(
        body, carry, (x, g_pre, ffn_w, ffn_w, ffn_w, g_post, *targets), name=name,
        grid=(t // tm, f_steps),
        in_specs=[row, vec, *w_specs, vec] + [row] * n_t,
        out_specs=[row, row, wide, wide, wide] + [pl.BlockSpec((1, SLOT), lambda i, j: (0, 0))] * n_t,
        out_shape=[jax.ShapeDtypeStruct((t, dm), F32)] * 2 + [jax.ShapeDtypeStruct((t, f_steps * tf), BF16)] * 3
        + [jax.ShapeDtypeStruct((1, SLOT), F32)] * n_t,
        scratch_shapes=[pltpu.VMEM((tm, dm), BF16), pltpu.VMEM((tm, dm), F32)],
        compiler_params=pltpu.CompilerParams(dimension_semantics=("arbitrary", "arbitrary")),
    )


def _ffn_bwd(name, x, h, hg, hu, dy, g_pre, ffn_w, first, g_post, carry=None):
    t, dm = x.shape
    tm = min(FFN_BWD_TM, t)
    w_specs, tf = _ffn_weight_specs(ffn_w, first)
    f_steps = N_DEV // FFN_BLOCKS
    f = f_steps * tf

    def post(hv, g):
        return 0.5 * _rms(hv, g, dm)

    def pre(xv, g):
        return _rms(xv, g, dm)

    def body(x_ref, h_ref, dy_ref, hg_ref, hu_ref, gpre_ref, wg_ref, wu_ref, wd_ref, gpost_ref,
             dx_ref, xn_ref, dh_ref, dhg_ref, dhu_ref, dgpre_ref, dgpost_ref, acc_ref):
        i, j = pl.program_id(0), pl.program_id(1)

        @pl.when((i == 0) & (j == 0))
        def _():
            dgpre_ref[...] = jnp.zeros_like(dgpre_ref)
            dgpost_ref[...] = jnp.zeros_like(dgpost_ref)

        @pl.when(j == 0)
        def _():
            xn_ref[...] = pre(x_ref[...], gpre_ref[...]).astype(BF16)
            _, pull = jax.vjp(post, h_ref[...], gpost_ref[...])
            dh, dg = pull(dy_ref[...])
            dh_ref[...] = dh.astype(BF16)
            dgpost_ref[...] += dg
            acc_ref[...] = jnp.zeros_like(acc_ref)

        wg, wu, wd = (r[...].reshape(tf, dm) for r in (wg_ref, wu_ref, wd_ref))
        hg, hu = hg_ref[...].astype(F32), hu_ref[...].astype(F32)
        da = _nt(dh_ref[...], wd)
        sig = _sigmoid(hg)
        act = hg * sig
        dhu = (da * act).astype(BF16)
        dhg = (da * hu * (sig * (1.0 + hg * (1.0 - sig)))).astype(BF16)
        dhg_ref[...] = dhg
        dhu_ref[...] = dhu
        acc_ref[...] += _nn(dhg, wg) + _nn(dhu, wu)

        @pl.when(j == f_steps - 1)
        def _():
            _, pull = jax.vjp(pre, x_ref[...], gpre_ref[...])
            dx, dg = pull(acc_ref[...])
            dx_ref[...] = dy_ref[...] + dx
            dgpre_ref[...] += dg

    row = pl.BlockSpec((tm, dm), lambda i, j: (i, 0))
    vec = pl.BlockSpec((1, dm), lambda i, j: (0, 0))
    wide = pl.BlockSpec((tm, tf), lambda i, j: (i, j))
    return _call_carrying(
        body, carry, (x, h, dy, hg, hu, g_pre, ffn_w, ffn_w, ffn_w, g_post), name=name,
        grid=(t // tm, f_steps),
        in_specs=[row, row, row, wide, wide, vec, *w_specs, vec],
        out_specs=[row, row, row, wide, wide, vec, vec],
        out_shape=[jax.ShapeDtypeStruct((t, dm), F32), jax.ShapeDtypeStruct((t, dm), BF16), jax.ShapeDtypeStruct((t, dm), BF16),
                   jax.ShapeDtypeStruct((t, f), BF16), jax.ShapeDtypeStruct((t, f), BF16),
                   jax.ShapeDtypeStruct((1, dm), F32), jax.ShapeDtypeStruct((1, dm), F32)],
        scratch_shapes=[pltpu.VMEM((tm, dm), F32)],
        compiler_params=pltpu.CompilerParams(dimension_semantics=("arbitrary", "arbitrary")),
    )


ATT_T = 512
ATT_GROUP = 4
ATT_GROUP_FWD = 8
ATT_SCALE = (MLA_NOPE + MLA_ROPE) ** -0.5


def _stack_slots(ref, group):
    return jnp.stack([ref[:, pl.ds(j * SLOT, SLOT)] for j in range(group)])


def _unstack_slots(ref, val):
    for j in range(val.shape[0]):
        ref[:, pl.ds(j * SLOT, SLOT)] = val[j].astype(ref.dtype)


def _scores(q, k, diagonal):
    s = _nt(q, k) * ATT_SCALE
    if diagonal:
        row = lax.broadcasted_iota(jnp.int32, s.shape[1:], 0)
        col = lax.broadcasted_iota(jnp.int32, s.shape[1:], 1)
        s = jnp.where(col <= row, s, -1e30)
    return s


def _attn_pairs(steps, q_major):
    pairs = ([(qi, ki) for qi in range(steps) for ki in range(qi + 1)] if q_major
             else [(qi, ki) for ki in range(steps) for qi in range(ki, steps)])
    return jnp.array([p[0] for p in pairs], jnp.int32), jnp.array([p[1] for p in pairs], jnp.int32)


def _attn_specs(tile, group):
    width = group * SLOT
    return (pl.BlockSpec((tile, width), lambda h, p, qt, kt: (qt[p], h)),
            pl.BlockSpec((tile, width), lambda h, p, qt, kt: (kt[p], h)))


def _attn_fwd(q, k, v):
    t = q.shape[0]
    tile = min(ATT_T, t)
    steps = t // tile
    g = ATT_GROUP_FWD

    strip = min(SLOT, tile)

    def body(qt_ref, kt_ref, q_ref, k_ref, v_ref, o_ref, lse_ref, m_ref, l_ref, alpha_ref, acc_ref, s_ref, p_ref):
        qi, ki = qt_ref[pl.program_id(1)], kt_ref[pl.program_id(1)]

        @pl.when(ki == 0)
        def _():
            m_ref[...] = jnp.full_like(m_ref, -1e30)
            l_ref[...] = jnp.zeros_like(l_ref)
            acc_ref[...] = jnp.zeros_like(acc_ref)

        def step(diagonal):
            s_ref[...] = _nt(_stack_slots(k_ref, g), _stack_slots(q_ref, g))
            for j in range(tile // strip):
                c = pl.ds(j * strip, strip)
                s = s_ref[:, :, c] * ATT_SCALE
                if diagonal:
                    key = lax.broadcasted_iota(jnp.int32, s.shape[1:], 0)
                    query = lax.broadcasted_iota(jnp.int32, s.shape[1:], 1) + j * strip
                    s = jnp.where(key <= query, s, -1e30)
                m_old = m_ref[:, :, c]
                m_new = jnp.maximum(m_old, jnp.max(s, axis=1, keepdims=True))
                p = jnp.exp(s - m_new)
                alpha = jnp.exp(m_old - m_new)
                l_ref[:, :, c] = alpha * l_ref[:, :, c] + jnp.sum(p, axis=1, keepdims=True)
                alpha_ref[:, :, c] = alpha
                m_ref[:, :, c] = m_new
                p_ref[:, :, c] = p.astype(BF16)
            acc_ref[...] = acc_ref[...] * alpha_ref[...] + _tn(_stack_slots(v_ref, g), p_ref[...])

        @pl.when(ki < qi)
        def _():
            step(False)

        @pl.when(ki == qi)
        def _():
            step(True)
            out = acc_ref[...] / l_ref[...]
            lse = jnp.broadcast_to(m_ref[...] + jnp.log(l_ref[...]), out.shape)
            for j in range(g):
                o_ref[:, pl.ds(j * SLOT, SLOT)] = out[j].T
                lse_ref[:, pl.ds(j * SLOT, SLOT)] = lse[j].T

    q_spec, k_spec = _attn_specs(tile, g)
    tables = _attn_pairs(steps, True)
    return pl.pallas_call(
        body, name="attn_fwd",
        grid_spec=pltpu.PrefetchScalarGridSpec(
            num_scalar_prefetch=2, grid=(N_HEADS // g, tables[0].shape[0]),
            in_specs=[q_spec, k_spec, k_spec], out_specs=[q_spec, q_spec],
            scratch_shapes=[pltpu.VMEM((g, 1, tile), F32), pltpu.VMEM((g, 1, tile), F32), pltpu.VMEM((g, 1, tile), F32),
                            pltpu.VMEM((g, SLOT, tile), F32), pltpu.VMEM((g, tile, tile), F32), pltpu.VMEM((g, tile, tile), BF16)]),
        out_shape=[jax.ShapeDtypeStruct((t, N_HEADS * SLOT), F32)] * 2,
        compiler_params=pltpu.CompilerParams(dimension_semantics=("parallel", "arbitrary")),
    )(*tables, q, k, v)


def _attn_grad_scores(q, k, v, do, lse_ref, delta_ref, diagonal):
    g = ATT_GROUP
    p = jnp.exp(_scores(q, k, diagonal) - _stack_slots(lse_ref, g)[:, :, 0:1])
    dp = _nt(do, v)
    return p, p * (dp - _stack_slots(delta_ref, g)[:, :, 0:1]) * ATT_SCALE


def _attn_bwd(q, k, v, do, lse, delta):
    t = q.shape[0]
    tile = min(ATT_T, t)
    steps = t // tile
    g = ATT_GROUP

    def body(qt_ref, kt_ref, q_ref, k_ref, v_ref, do_ref, lse_ref, delta_ref, dq_ref, dk_ref, dv_ref, dk_acc, dv_acc):
        qi, ki = qt_ref[pl.program_id(1)], kt_ref[pl.program_id(1)]

        @pl.when(pl.program_id(1) == 0)
        def _():
            dq_ref[...] = jnp.zeros_like(dq_ref)

        def step(diagonal):
            qq, kk = _stack_slots(q_ref, g), _stack_slots(k_ref, g)
            do_b = _stack_slots(do_ref, g).astype(BF16)
            p, ds = _attn_grad_scores(qq, kk, _stack_slots(v_ref, g), do_b, lse_ref, delta_ref, diagonal)
            ds = ds.astype(BF16)
            dv_acc[...] += _tn(p.astype(BF16), do_b)
            dk_acc[...] += _tn(ds, qq)
            dq = _nn(ds, kk)
            rows = pl.ds(pl.multiple_of(qi * tile, tile), tile)
            for j in range(g):
                dq_ref[rows, pl.ds(j * SLOT, SLOT)] += dq[j]

        @pl.when(qi == ki)
        def _():
            dk_acc[...] = jnp.zeros_like(dk_acc)
            dv_acc[...] = jnp.zeros_like(dv_acc)
            step(True)

        @pl.when(qi > ki)
        def _():
            step(False)

        @pl.when(qi == steps - 1)
        def _():
            _unstack_slots(dk_ref, dk_acc[...])
            _unstack_slots(dv_ref, dv_acc[...])

    q_spec, k_spec = _attn_specs(tile, g)
    tables = _attn_pairs(steps, False)
    return pl.pallas_call(
        body, name="attn_bwd",
        grid_spec=pltpu.PrefetchScalarGridSpec(
            num_scalar_prefetch=2, grid=(N_HEADS // g, tables[0].shape[0]),
            in_specs=[q_spec, k_spec, k_spec, q_spec, q_spec, q_spec],
            out_specs=[pl.BlockSpec((t, g * SLOT), lambda h, p, qt, kt: (0, h)), k_spec, k_spec],
            scratch_shapes=[pltpu.VMEM((g, tile, SLOT), F32), pltpu.VMEM((g, tile, SLOT), F32)]),
        out_shape=[jax.ShapeDtypeStruct((t, N_HEADS * SLOT), F32)] * 3,
        compiler_params=pltpu.CompilerParams(dimension_semantics=("parallel", "arbitrary")),
    )(*tables, q, k, v, do, lse, delta)


CONV_PAD = 8


def _fill_padded(ref, val):
    t = val.shape[0]
    zeros = jnp.zeros((CONV_PAD, val.shape[1]), val.dtype)
    ref[pl.ds(0, CONV_PAD)] = zeros
    ref[pl.ds(CONV_PAD + t, CONV_PAD)] = zeros
    ref[pl.ds(CONV_PAD, t)] = val


def _shifted(ref, s):
    return ref[pl.ds(CONV_PAD - s, ref.shape[0] - 2 * CONV_PAD)]


def _l2norm(x):
    return x * lax.rsqrt(jnp.sum(x * x, axis=-1, keepdims=True) + EPS)


def _conv_pre(x_pad, w):
    y = w[GDN_CONV - 1:GDN_CONV, :] * _shifted(x_pad, 0)
    for s in range(1, GDN_CONV):
        y = y + w[GDN_CONV - 1 - s:GDN_CONV - s, :] * _shifted(x_pad, s)
    return y


def _gdn_conv_fwd(x, w):
    t, width = x.shape

    def body(x_ref, w_ref, o_ref, x_pad):
        _fill_padded(x_pad, x_ref[...])
        act = _silu(_conv_pre(x_pad, w_ref[...]))
        normed = pl.program_id(0) < 2 * N_HEADS
        o_ref[...] = jnp.where(normed, _l2norm(act), act)

    return pl.pallas_call(
        body, name="gdn_conv_fwd",
        grid=(width // SLOT,),
        in_specs=[pl.BlockSpec((t, SLOT), lambda j: (0, j)), pl.BlockSpec((GDN_CONV, SLOT), lambda j: (0, j))],
        out_specs=pl.BlockSpec((t, SLOT), lambda j: (0, j)),
        out_shape=jax.ShapeDtypeStruct((t, width), F32),
        scratch_shapes=[pltpu.VMEM((t + 2 * CONV_PAD, SLOT), F32)],
        compiler_params=pltpu.CompilerParams(dimension_semantics=("parallel",)),
    )(x, w)


def _gdn_conv_bwd(x, w, dout):
    t, width = x.shape

    def body(x_ref, w_ref, do_ref, dx_ref, dw_ref, x_pad, dy_pad):
        wv = w_ref[...]
        _fill_padded(x_pad, x_ref[...])
        y = _conv_pre(x_pad, wv)
        sig = _sigmoid(y)
        act = y * sig
        _, pull = jax.vjp(_l2norm, act)
        normed = pl.program_id(0) < 2 * N_HEADS
        dact = jnp.where(normed, pull(do_ref[0])[0], do_ref[0])
        dy = dact * (sig * (1.0 + y * (1.0 - sig)))
        _fill_padded(dy_pad, dy)
        dx = wv[GDN_CONV - 1:GDN_CONV, :] * dy
        for s in range(1, GDN_CONV):
            dx = dx + wv[GDN_CONV - 1 - s:GDN_CONV - s, :] * _shifted(dy_pad, -s)
        dx_ref[...] = dx.astype(BF16)
        for s in range(GDN_CONV):
            dw_ref[GDN_CONV - 1 - s:GDN_CONV - s, :] = jnp.sum(dy * _shifted(x_pad, s), axis=0, keepdims=True)

    col = pl.BlockSpec((t, SLOT), lambda j: (0, j))
    tap = pl.BlockSpec((GDN_CONV, SLOT), lambda j: (0, j))
    return pl.pallas_call(
        body, name="gdn_conv_bwd",
        grid=(width // SLOT,),
        in_specs=[col, tap, pl.BlockSpec((1, t, SLOT), lambda j: (j // N_HEADS, 0, j % N_HEADS))],
        out_specs=[col, tap],
        out_shape=[jax.ShapeDtypeStruct((t, width), BF16), jax.ShapeDtypeStruct((GDN_CONV, width), F32)],
        scratch_shapes=[pltpu.VMEM((t + 2 * CONV_PAD, SLOT), F32)] * 2,
        compiler_params=pltpu.CompilerParams(dimension_semantics=("parallel",)),
    )(x, w, dout)


def _softplus(x):
    e = jnp.exp(-jnp.abs(x))
    u = 1.0 + e
    log1p = jnp.where(u == 1.0, e, jnp.log(u) * e / jnp.where(u == 1.0, 1.0, u - 1.0))
    return jnp.maximum(x, 0.0) + log1p


def _chunk_running_sum(x, reverse=False):
    tm = x.shape[0]
    at = lax.broadcasted_iota(jnp.int32, x.shape, 0) % GDN_CHUNK
    step = 1
    while step < GDN_CHUNK:
        if reverse:
            x = x + jnp.where(at < GDN_CHUNK - step, pltpu.roll(x, tm - step, 0), 0.0)
        else:
            x = x + jnp.where(at >= step, pltpu.roll(x, step, 0), 0.0)
        step *= 2
    return x


def _gates_fwd(ab, a_log, dt_bias):
    def fn(rows, consts):
        (abv,), (alog, dtb) = rows, consts
        g = _chunk_running_sum(-jnp.exp(alog) * _softplus(abv + dtb))
        beta = _sigmoid(abv)
        shape = (abv.shape[0], SLOT)
        g_slots = [jnp.broadcast_to(g[:, h:h + 1], shape) for h in range(N_HEADS)]
        b_slots = [jnp.broadcast_to(beta[:, N_HEADS + h:N_HEADS + h + 1], shape) for h in range(N_HEADS)]
        return [jnp.concatenate(g_slots, axis=1), jnp.concatenate(b_slots, axis=1)], []

    width = N_HEADS * SLOT
    return _rowwise("gdn_gates_fwd", fn, [ab], [a_log, dt_bias], [(width, F32), (width, F32)])


def _gates_bwd(ab, a_log, dt_bias, dg, dbeta):
    def fn(rows, consts):
        (abv, dgv, dbv), (alog, dtb) = rows, consts
        lane = lax.broadcasted_iota(jnp.int32, abv.shape, 1)
        dg_tok = jnp.zeros_like(abv)
        db_tok = jnp.zeros_like(abv)
        for h in range(N_HEADS):
            dg_tok = dg_tok + jnp.where(lane == h, jnp.sum(dgv[:, h * SLOT:(h + 1) * SLOT], axis=1, keepdims=True), 0.0)
            db_tok = db_tok + jnp.where(lane == N_HEADS + h, jnp.sum(dbv[:, h * SLOT:(h + 1) * SLOT], axis=1, keepdims=True), 0.0)
        dg_tok = _chunk_running_sum(dg_tok, reverse=True)
        xa = abv + dtb
        g = -jnp.exp(alog) * _softplus(xa)
        da = dg_tok * (-jnp.exp(alog)) * _sigmoid(xa)
        beta = _sigmoid(abv)
        dab = jnp.where(lane < N_HEADS, da, db_tok * beta * (1.0 - beta))
        dab = jnp.where(lane < 2 * N_HEADS, dab, 0.0)
        d_alog = jnp.sum(jnp.where(lane < N_HEADS, dg_tok * g, 0.0), axis=0, keepdims=True)
        d_dtb = jnp.sum(jnp.where(lane < N_HEADS, da, 0.0), axis=0, keepdims=True)
        return [dab], [d_alog, d_dtb]

    return _rowwise("gdn_gates_bwd", fn, [ab, dg, dbeta], [a_log, dt_bias], [(SLOT, F32)], sums=[SLOT, SLOT])


ROPE_HALF = MLA_ROPE // 2


def _rope_tables(positions):
    freqs = ROPE_THETA ** (-jnp.arange(ROPE_HALF, dtype=F32) / ROPE_HALF)
    ang = positions.astype(F32)[:, None] * _pad_lanes(jnp.concatenate([freqs, freqs])[None, :], MLA_NOPE)
    cos, sin = jnp.cos(ang), jnp.sin(ang)
    lane = lax.broadcasted_iota(jnp.int32, ang.shape, 1)
    low = (lane >= MLA_NOPE) & (lane < MLA_NOPE + ROPE_HALF)
    high = (lane >= MLA_NOPE + ROPE_HALF) & (lane < MLA_NOPE + MLA_ROPE)
    same = jnp.where(lane < MLA_NOPE, 1.0, jnp.where(low | high, cos, 0.0))
    return same, jnp.where(high, sin, 0.0), jnp.where(low, -sin, 0.0)


def _rope(x, tabs):
    same, from_low, from_high = tabs
    width = x.shape[1]
    return x * same + pltpu.roll(x, ROPE_HALF, 1) * from_low + pltpu.roll(x, width - ROPE_HALF, 1) * from_high


def _rope_transposed(dy, tabs):
    same, from_low, from_high = tabs
    width = dy.shape[1]
    return dy * same + pltpu.roll(dy * from_low, width - ROPE_HALF, 1) + pltpu.roll(dy * from_high, ROPE_HALF, 1)


def _tile_slots(tab):
    return jnp.concatenate([tab] * N_HEADS, axis=1)


A_WIDTH = MLA_Q_RANK + MLA_KV_RANK + 2 * SLOT
A_KPE = MLA_Q_RANK + MLA_KV_RANK
A_AB = A_KPE + SLOT
WIDE = N_HEADS * SLOT


def _mla_front_fwd(proj_a, tabs, g_q, g_kv, w_uq, w_kv):
    def fn(rows, consts):
        pa, *tb = rows
        gq, gkv, wuq, wkv = consts
        cqn = _rms(pa[:, :MLA_Q_RANK], gq, MLA_Q_RANK).astype(BF16)
        ckvn = _rms(pa[:, MLA_Q_RANK:A_KPE], gkv, MLA_KV_RANK).astype(BF16)
        kv = _nt(ckvn, wkv)
        q = _rope(_nt(cqn, wuq), [_tile_slots(x) for x in tb])
        k = kv[:, :WIDE] + _tile_slots(_rope(pa[:, A_KPE:A_AB], tb))
        return [cqn, ckvn, q, k, kv[:, WIDE:]], []

    return _rowwise("mla_front_fwd", fn, [proj_a, *tabs], [g_q, g_kv, w_uq, w_kv],
                    [(MLA_Q_RANK, BF16), (MLA_KV_RANK, BF16)] + [(WIDE, BF16)] * 3)


def _mla_front_bwd(proj_a, tabs, g_q, g_kv, w_uq, w_kv, dq, dk, dv, dab):
    def fn(rows, consts):
        pa, t0, t1, t2, dqv, dkv, dvv, da = rows
        gq, gkv, wuq, wkv = consts
        tb = (t0, t1, t2)
        dq_p = _rope_transposed(dqv, [_tile_slots(x) for x in tb]).astype(BF16)
        dkv_p = jnp.concatenate([dkv, dvv], axis=1).astype(BF16)
        dkpe = dkv[:, :SLOT]
        for h in range(1, N_HEADS):
            dkpe = dkpe + dkv[:, h * SLOT:(h + 1) * SLOT]
        _, pull_q = jax.vjp(lambda x, g: _rms(x, g, MLA_Q_RANK), pa[:, :MLA_Q_RANK], gq)
        _, pull_kv = jax.vjp(lambda x, g: _rms(x, g, MLA_KV_RANK), pa[:, MLA_Q_RANK:A_KPE], gkv)
        dcq, dgq = pull_q(_nn(dq_p, wuq))
        dckv, dgkv = pull_kv(_nn(dkv_p, wkv))
        return [jnp.concatenate([dcq, dckv, _rope_transposed(dkpe, tb), da], axis=1), dq_p, dkv_p], [dgq, dgkv]

    return _rowwise("mla_front_bwd", fn, [proj_a, *tabs, dq, dk, dv, dab], [g_q, g_kv, w_uq, w_kv],
                    [(A_WIDTH, BF16), (WIDE, BF16), (2 * WIDE, BF16)], sums=[MLA_Q_RANK, MLA_KV_RANK])


def _slot_sum(x):
    parts = [jnp.broadcast_to(jnp.sum(x[:, h * SLOT:(h + 1) * SLOT], axis=1, keepdims=True), (x.shape[0], SLOT))
             for h in range(N_HEADS)]
    return jnp.concatenate(parts, axis=1)


def _mix_join(o_mla, o_gdn, gate, g_mla, g_gdn):
    mla = _rms(o_mla, g_mla, N_HEADS * MLA_V)
    gdn = o_gdn * lax.rsqrt(_slot_sum(o_gdn * o_gdn) * (1.0 / GDN_D) + EPS) * g_gdn * _silu(gate)
    return mla, gdn


MIX_TM = 256


def _mix_fwd(o_mla, o_gdn, gate, x, g_mla, g_gdn, w_out, g_post):
    dm = x.shape[1]

    def fn(rows, consts):
        om, og, gt, xv = rows
        gm, gg, wo, gp = consts
        cat = jnp.concatenate(_mix_join(om, og, gt, gm, gg), axis=1).astype(BF16)
        mixed = _nn(cat, wo)
        return [cat, mixed, xv + _rms(mixed, gp, dm)], []

    return _rowwise("mix_fwd", fn, [o_mla, o_gdn, gate, x], [g_mla, g_gdn, w_out, g_post],
                    [(2 * WIDE, BF16), (dm, F32), (dm, F32)], tm=MIX_TM)


def _mix_bwd(o_mla, o_gdn, gate, mixed, dy, g_mla, g_gdn, w_out, g_post):
    dm = mixed.shape[1]

    def fn(rows, consts):
        om, og, gt, mx, dyv = rows
        gm, gg, wo, gp = consts
        _, pull_post = jax.vjp(lambda hv, gv: _rms(hv, gv, dm), mx, gp)
        dmixed, dgp = pull_post(dyv)
        dmixed = dmixed.astype(BF16)
        dc = _nt(dmixed, wo)
        _, pull = jax.vjp(lambda x, g: _rms(x, g, N_HEADS * MLA_V), om, gm)
        dom, dgm = pull(dc[:, :WIDE])
        dn_out = dc[:, WIDE:]
        r = lax.rsqrt(_slot_sum(og * og) * (1.0 / GDN_D) + EPS)
        sig = _sigmoid(gt)
        normed = og * r
        dn = dn_out * gg * (gt * sig)
        dog = r * dn - normed * (r * r) * _slot_sum(dn * og) * (1.0 / GDN_D)
        dgt = dn_out * normed * gg * (sig * (1.0 + gt * (1.0 - sig)))
        dgg = jnp.sum(dn_out * normed * (gt * sig), axis=0, keepdims=True)
        return [dmixed, dom, _slot_sum(dom * om), dog, dgt], [dgp, dgm, dgg]

    return _rowwise("mix_bwd", fn, [o_mla, o_gdn, gate, mixed, dy], [g_mla, g_gdn, w_out, g_post],
                    [(dm, BF16), (WIDE, F32), (WIDE, F32), (WIDE, F32), (WIDE, BF16)], sums=[dm, WIDE, WIDE], tm=MIX_TM)


def _proj_fwd(x, g, weights):
    dm = x.shape[1]

    def fn(rows, consts):
        hn = _rms(rows[0], consts[0], dm).astype(BF16)
        return [hn] + [_nt(hn, wv) for wv in consts[1:]], []

    return _rowwise("proj_fwd", fn, [x], [g, *weights], [(dm, BF16)] + [(wv.shape[0], F32) for wv in weights], tm=MIX_TM)


def _proj_bwd(x, g, weights, cots, dy, h, g_post):
    dm = x.shape[1]
    n = len(weights)

    def fn(rows, consts):
        xv, dyv, hv, *parts = rows
        dn = _nn(parts[0], consts[2])
        for p, wv in zip(parts[1:], consts[3:]):
            dn = dn + _nn(p, wv)
        _, pull = jax.vjp(lambda a, gv: _rms(a, gv, dm), xv, consts[0])
        dx, dg = pull(dn)
        dx = dyv + dx
        _, pull = jax.vjp(lambda a: 0.5 * _rms(a, consts[1], dm), hv)
        return [dx, pull(dx)[0]], [dg]

    assert len(cots) == n
    return _rowwise("proj_bwd", fn, [x, dy, h, *cots], [g, g_post, *weights], [(dm, F32), (dm, BF16)], sums=[dm], tm=MIX_TM)


W_IN_CUTS = (0, 256, 384, 416, 1952, 1960, 1968, 2480)


def _heads_out(w, per_head, axis=-1):
    axis = axis % w.ndim
    shape = w.shape
    n = shape[axis] // per_head
    w = w.reshape(shape[:axis] + (n, per_head) + shape[axis + 1:])
    pad = [(0, 0)] * w.ndim
    pad[axis + 1] = (0, SLOT - per_head)
    return jnp.pad(w, pad).reshape(shape[:axis] + (n * SLOT,) + shape[axis + 1:])


def _heads_in(w, per_head, axis=-1):
    axis = axis % w.ndim
    shape = w.shape
    n = shape[axis] // SLOT
    w = w.reshape(shape[:axis] + (n, SLOT) + shape[axis + 1:])
    w = lax.slice_in_dim(w, 0, per_head, axis=axis + 1)
    return w.reshape(shape[:axis] + (n * per_head,) + shape[axis + 1:])


def _pad_lanes(v, lo, width=SLOT):
    return jnp.pad(v, [(0, 0)] * (v.ndim - 1) + [(lo, width - lo - v.shape[-1])])


def _pad_rows(v, lo, rows=SLOT):
    return jnp.pad(v, [(lo, rows - lo - v.shape[0])] + [(0, 0)] * (v.ndim - 1))


def _layout_weights(w):
    c = W_IN_CUTS
    w_in = w["w_in_t"]
    p = {}
    p["w_a"] = jnp.concatenate([w_in[c[0]:c[2]], _pad_rows(w_in[c[2]:c[3]], MLA_NOPE), _pad_rows(w_in[c[4]:c[6]], 0)], axis=0)
    p["w_qkv"] = _heads_out(w_in[c[3]:c[4]], GDN_D, axis=0)
    p["w_gate"] = _heads_out(w_in[c[6]:c[7]], GDN_D, axis=0)
    p["w_uq"] = _heads_out(w["uq_t"], MLA_NOPE + MLA_ROPE, axis=0)
    ukv = w["ukv_t"].reshape(N_HEADS, MLA_NOPE + MLA_V, MLA_KV_RANK)
    p["w_kv"] = jnp.concatenate([_heads_out(ukv[:, :MLA_NOPE].reshape(-1, MLA_KV_RANK), MLA_NOPE, axis=0),
                                 _heads_out(ukv[:, MLA_NOPE:].reshape(-1, MLA_KV_RANK), MLA_V, axis=0)], axis=0)
    p["conv"] = _heads_out(w["gdn_conv_w"], GDN_D)
    p["g_mla_out"] = _heads_out(w["mla_out_g"], MLA_V)
    p["g_gdn"] = jnp.tile(_pad_lanes(w["gdn_norm_g"], 0), (1, N_HEADS))
    p["a_log"] = _pad_lanes(w["gdn_a_log"], 0)
    p["dt_bias"] = _pad_lanes(w["gdn_dt_bias"], 0)
    return p


def _unlayout_grads(d):
    c = W_IN_CUTS
    g = {}
    da = d["w_a"]
    kpe0 = A_KPE + MLA_NOPE
    g["w_in_t"] = jnp.concatenate([da[:A_KPE], da[kpe0:kpe0 + MLA_ROPE], _heads_in(d["w_qkv"], GDN_D, axis=0),
                                   da[A_AB:A_AB + 2 * N_HEADS], _heads_in(d["w_gate"], GDN_D, axis=0)], axis=0)
    assert g["w_in_t"].shape[0] == c[-1]
    g["uq_t"] = _heads_in(d["w_uq"], MLA_NOPE + MLA_ROPE, axis=0)
    dk = _heads_in(d["w_kv"][:WIDE], MLA_NOPE, axis=0).reshape(N_HEADS, MLA_NOPE, MLA_KV_RANK)
    dv = _heads_in(d["w_kv"][WIDE:], MLA_V, axis=0).reshape(N_HEADS, MLA_V, MLA_KV_RANK)
    g["ukv_t"] = jnp.concatenate([dk, dv], axis=1).reshape(-1, MLA_KV_RANK)
    g["w_out"] = _heads_in(d["w_out"], GDN_D, axis=0)
    g["gdn_conv_w"] = _heads_in(d["conv"], GDN_D)
    g["mla_out_g"] = _heads_in(d["g_mla_out"], MLA_V)
    g["gdn_norm_g"] = jnp.sum(d["g_gdn"].reshape(N_HEADS, SLOT), axis=0, keepdims=True)[:, :GDN_D]
    g["gdn_a_log"] = d["a_log"][:, :N_HEADS]
    g["gdn_dt_bias"] = d["dt_bias"][:, :N_HEADS]
    return g


def _weight_grad(name, cots, acts, out_dtype=F32, tm=1024, tn=1024, tk=2048, after=None):
    return _matmul(name, cots, acts, "tn", out_dtype=out_dtype, tm=tm, tn=tn, tk=tk, after=after)


def _by_device(a):
    return a.astype(BF16).reshape((N_DEV, a.shape[0] // N_DEV) + a.shape[1:])


def _rows_of(blocks):
    return blocks.reshape((-1,) + blocks.shape[2:])


def _local_step(x, positions, target, w, mid, late):
    tabs = _rope_tables(positions)

    (h1, x1, hg1, hu1, a1), gathered = _ffn_fwd("ffn1_fwd", x, w["ffn1_pre_g"], w["ffn1"], 0, w["ffn1_post_g"], carry=mid)
    w = dict(w, w_in_t=_rows_of(gathered[0]), uq_t=_rows_of(gathered[1]), ukv_t=_rows_of(gathered[2]),
             gdn_conv_w=gathered[3].transpose(1, 0, 2).reshape(CONV_SHAPE))
    p = _layout_weights(w)
    in_weights = [p["w_a"], p["w_qkv"], p["w_gate"]]
    hn, proj_a, proj_qkv, proj_gate = _proj_fwd(x1, w["mix_pre_g"], in_weights)
    cqn, ckvn, q, k, v = _mla_front_fwd(proj_a, tabs, w["mla_q_norm_g"], w["mla_kv_norm_g"], p["w_uq"], p["w_kv"])
    o_mla, lse = _attn_fwd(q, k, v)
    ab = (proj_a, SLOT, A_AB // SLOT)
    qkv_n = _gdn_conv_fwd(proj_qkv, p["conv"])
    gb, bb = _gates_fwd(ab, p["a_log"], p["dt_bias"])
    (o_gdn, keep), (ffn2, w_out) = _gdn_fwd(qkv_n, gb, bb, carry=late)
    p["w_out"] = _heads_out(_rows_of(w_out), GDN_D, axis=0)
    cat, mixed, x2 = _mix_fwd(o_mla, o_gdn, proj_gate, x1, p["g_mla_out"], p["g_gdn"], p["w_out"], w["mix_post_g"])
    (h2, dy, hg2, hu2, a2, loss_lanes), _ = _ffn_fwd("ffn2_fwd", x2, w["ffn2_pre_g"], ffn2, 0, w["ffn2_post_g"], target=target)

    g = {}
    (dx2, xn2, dh2, dhg2, dhu2, g["ffn2_pre_g"], g["ffn2_post_g"]), _ = _ffn_bwd(
        "ffn2_bwd", x2, h2, hg2, hu2, dy, w["ffn2_pre_g"], ffn2, 0, w["ffn2_post_g"])
    ffn2_grads = _Scatter([_by_device(_weight_grad("ffn2_dw_gate", dhg2, xn2, BF16, tm=1408)),
                           _by_device(_weight_grad("ffn2_dw_up", dhu2, xn2, BF16, tm=1408)),
                           _by_device(_weight_grad("ffn2_dw_down", a2, dh2, BF16, tm=1408))])
    d = {}
    dmixed, do_mla, delta, do_gdn, dgate, g["mix_post_g"], d["g_mla_out"], d["g_gdn"] = _mix_bwd(
        o_mla, o_gdn, proj_gate, mixed, dx2, p["g_mla_out"], p["g_gdn"], p["w_out"], w["mix_post_g"])
    d["w_out"] = _weight_grad("mix_out_dw", cat, dmixed, BF16)
    dq, dk, dv = _attn_bwd(q, k, v, do_mla, lse, delta)
    (dqkv_n, dgb, dbb), landed_ffn2 = _gdn_bwd(qkv_n, gb, bb, keep, do_gdn, carry=ffn2_grads)
    dab, d["a_log"], d["dt_bias"] = _gates_bwd(ab, p["a_log"], p["dt_bias"], dgb, dbb)
    dproj_qkv, d["conv"] = _gdn_conv_bwd(proj_qkv, p["conv"], dqkv_n)
    dproj_a, dq_p, dkv_p, g["mla_q_norm_g"], g["mla_kv_norm_g"] = _mla_front_bwd(
        proj_a, tabs, w["mla_q_norm_g"], w["mla_kv_norm_g"], p["w_uq"], p["w_kv"], dq, dk, dv, dab)
    d["w_uq"] = _weight_grad("mla_q_dw", dq_p, cqn, BF16)
    d["w_kv"] = _weight_grad("mla_kv_dw", dkv_p, ckvn, BF16)
    d["w_a"] = _weight_grad("proj_a_dw", dproj_a, hn, BF16, tm=640)
    d["w_qkv"] = _weight_grad("proj_qkv_dw", dproj_qkv, hn, BF16)
    d["w_gate"] = _weight_grad("proj_gate_dw", dgate, hn, BF16)
    dx1, dh1, g["mix_pre_g"] = _proj_bwd(x1, w["mix_pre_g"], in_weights, [dproj_a, dproj_qkv, dgate], dx2,
                                        h1, w["ffn1_post_g"])
    g.update(_unlayout_grads(d))
    behind = [_by_device(g.pop(t)) for t in OTHER.values()]
    behind.append(_by_device(_weight_grad("ffn1_w_down_grad", a1, dh1, BF16, tm=1408)))
    (dx, xn1, _, dhg1, dhu1, g["ffn1_pre_g"], g["ffn1_post_g"]), landed_ffn1 = _ffn_bwd(
        "ffn1_bwd", x, h1, hg1, hu1, dx1, w["ffn1_pre_g"], w["ffn1"], 0, w["ffn1_post_g"], carry=_Scatter(behind))
    landed = dict(zip(list(FFN_NAMES[3:]) + list(OTHER) + ["ffn1_w_down"], list(landed_ffn2) + list(landed_ffn1)))
    begun, token = {}, None
    for name, cots, acts in (("ffn1_w_gate", dhg1, xn1), ("ffn1_w_up", dhu1, xn1)):
        blocks = _by_device(_weight_grad(name + "_grad", cots, acts, BF16, tm=1408, after=token))
        begun[name], token = _scatter_begin("scatter_" + name + "_begin", blocks)
    packed = _pack_small(g, g["gdn_conv_w"].reshape(-1), REDUCE_ROWS)
    packed = packed.at[REDUCE_ROWS - 1, ROW - 1].set(jnp.sum(loss_lanes))
    begun["small"], small_token = _scatter_begin("reduce_small_begin", jnp.broadcast_to(packed, (N_DEV,) + packed.shape))
    return dx, g, landed, begun, token + small_token


MESH_AXES = ("x", "y", "c")
N_LINKS = N_DEV - 1


def _place():
    return tuple(lax.axis_index(a) for a in MESH_AXES)


def _block_of(dev):
    x, y, c = dev
    return 4 * x + 2 * y + c


def _remote_copy(src, dst, sems, k, to):
    send_sems, recv_sems = sems
    return pltpu.make_async_remote_copy(src_ref=src, dst_ref=dst, send_sem=send_sems.at[k], recv_sem=recv_sems.at[k],
                                        device_id=to, device_id_type=pl.DeviceIdType.MESH)


class _Exchange:
    def __init__(self, arrays):
        self.arrays = list(arrays)
        self.n = len(self.arrays)
        self.specs = [pl.BlockSpec(memory_space=pl.ANY)] * self.n
        self.scratch = [pltpu.SemaphoreType.DMA((self.n * N_LINKS,)), pltpu.SemaphoreType.DMA((self.n * N_LINKS,)),
                        pltpu.SemaphoreType.DMA((self.n,))]

    def split(self, refs):
        n = self.n
        return refs[:n], refs[n:2 * n], (refs[2 * n], refs[2 * n + 1]), refs[2 * n + 2]


class _Gather(_Exchange):
    def out_shape(self):
        return [jax.ShapeDtypeStruct((N_DEV,) + a.shape, a.dtype) for a in self.arrays]

    def _plan(self, ins, outs, sems, local_sems):
        x, y, c = _place()
        me, sibling = (x, y, c), (x, y, 1 - c)
        chips = [(1 - x, y), (x, 1 - y), (1 - x, 1 - y)]

        def copy(a, k, block, to, mine=False):
            src = ins[a] if mine else outs[a].at[_block_of(block)]
            return _remote_copy(src, outs[a].at[_block_of(block)], sems, a * N_LINKS + k, to)

        local = [pltpu.make_async_copy(ins[a], outs[a].at[_block_of(me)], local_sems.at[a]) for a in range(self.n)]
        first = []
        for a in range(self.n):
            first.append(copy(a, 0, me, sibling, mine=True))
            first += [copy(a, 1 + j, me, (*chip, c), mine=True) for j, chip in enumerate(chips)]
        return me, sibling, chips, c, copy, local, first

    def start(self, ins, outs, sems, local_sems):
        *_, local, first = self._plan(ins, outs, sems, local_sems)
        for cp in local + first:
            cp.start()

    def finish(self, ins, outs, sems, local_sems):
        me, sibling, chips, c, copy, local, first = self._plan(ins, outs, sems, local_sems)
        passed = []
        for j, chip in enumerate(chips):
            for a in range(self.n):
                copy(a, 1 + j, (*chip, c), me).wait_recv()
                passed.append(copy(a, 4 + j, (*chip, c), sibling))
                passed[-1].start()
        for a in range(self.n):
            copy(a, 0, sibling, me).wait_recv()
            for j, chip in enumerate(chips):
                copy(a, 4 + j, (*chip, 1 - c), me).wait_recv()
        for cp in first + passed:
            cp.wait_send()
        for cp in local:
            cp.wait()


class _Scatter(_Exchange):
    def out_shape(self):
        return [jax.ShapeDtypeStruct(a.shape, a.dtype) for a in self.arrays]

    def _plan(self, ins, outs, sems, local_sems):
        x, y, c = _place()
        me = _block_of((x, y, c))

        def peer(r):
            return (1 - x if r & 4 else x, 1 - y if r & 2 else y, 1 - c if r & 1 else c)

        local = [pltpu.make_async_copy(ins[a].at[me], outs[a].at[me], local_sems.at[a]) for a in range(self.n)]
        sends = [_remote_copy(ins[a].at[_block_of(peer(r))], outs[a].at[me], sems, a * N_LINKS + r - 1, peer(r))
                 for a in range(self.n) for r in range(1, N_DEV)]
        arrivals = [_remote_copy(ins[a].at[me], outs[a].at[_block_of(peer(r))], sems, a * N_LINKS + r - 1, peer(r))
                    for a in range(self.n) for r in range(1, N_DEV)]
        return local, sends, arrivals

    def start(self, ins, outs, sems, local_sems):
        local, sends, _ = self._plan(ins, outs, sems, local_sems)
        for cp in local + sends:
            cp.start()

    def finish(self, ins, outs, sems, local_sems):
        local, sends, arrivals = self._plan(ins, outs, sems, local_sems)
        for cp in arrivals:
            cp.wait_recv()
        for cp in sends:
            cp.wait_send()
        for cp in local:
            cp.wait()


def _exchange(name, plan):
    def body(*refs):
        parts = plan.split(refs)
        plan.start(*parts)
        plan.finish(*parts)

    return pl.pallas_call(
        body, name=name,
        in_specs=plan.specs,
        out_specs=plan.specs,
        out_shape=plan.out_shape(),
        scratch_shapes=plan.scratch,
    )(*plan.arrays)


def _call_carrying(body, plan, operands, *, name, grid, in_specs, out_specs, out_shape, scratch_shapes, compiler_params):
    if plan is None:
        outs = pl.pallas_call(body, name=name, grid=grid, in_specs=in_specs, out_specs=out_specs, out_shape=out_shape,
                              scratch_shapes=scratch_shapes, compiler_params=compiler_params)(*operands)
        return outs, []
    n_i, n_o, n_s, k = len(in_specs), len(out_specs), len(scratch_shapes), plan.n

    def whole(*refs):
        cut = [n_i, n_i + k, n_i + k + n_o, n_i + 2 * k + n_o, n_i + 2 * k + n_o + n_s]
        own_in, ex_in, own_out, ex_out, own_scr, ex_scr = (refs[a:b] for a, b in zip([0] + cut, cut + [len(refs)]))
        parts = plan.split(ex_in + ex_out + ex_scr)
        first = last = True
        for axis, size in enumerate(grid):
            first = first & (pl.program_id(axis) == 0)
            last = last & (pl.program_id(axis) == size - 1)

        @pl.when(first)
        def _():
            plan.start(*parts)

        body(*own_in, *own_out, *own_scr)

        @pl.when(last)
        def _():
            plan.finish(*parts)

    outs = pl.pallas_call(
        whole, name=name, grid=grid,
        in_specs=list(in_specs) + plan.specs, out_specs=list(out_specs) + plan.specs,
        out_shape=list(out_shape) + plan.out_shape(), scratch_shapes=list(scratch_shapes) + plan.scratch,
        compiler_params=compiler_params,
    )(*operands, *plan.arrays)
    return outs[:n_o], outs[n_o:]


def _row_tile(rows, target=256):
    best = rows
    for cand in range(16, min(rows, target) + 1, 16):
        if rows % cand == 0:
            best = cand
    return best


def _sum_blocks(name, blocks, after=None):
    rows, width = blocks.shape[-2:]
    tm = _row_tile(rows)

    def body(x_ref, *rest):
        acc = x_ref[0].astype(F32)
        for d in range(1, N_DEV):
            acc = acc + x_ref[d].astype(F32)
        rest[-1][...] = acc

    ordered = [] if after is None else [after]
    return pl.pallas_call(
        body, name=name,
        grid=(rows // tm,),
        in_specs=[pl.BlockSpec((N_DEV, tm, width), lambda i: (0, i, 0))] + [pl.BlockSpec(memory_space=pl.ANY)] * len(ordered),
        out_specs=pl.BlockSpec((tm, width), lambda i: (i, 0)),
        out_shape=jax.ShapeDtypeStruct((rows, width), F32),
        compiler_params=pltpu.CompilerParams(dimension_semantics=("parallel",)),
    )(blocks, *ordered)


def _split_plan(src_ref, land_ref, sems):
    x, y, c = _place()
    me = _block_of((x, y, c))

    def peer(r):
        return (1 - x if r & 4 else x, 1 - y if r & 2 else y, 1 - c if r & 1 else c)

    sends = [_remote_copy(src_ref.at[_block_of(peer(r))], land_ref.at[me], sems, r - 1, peer(r)) for r in range(1, N_DEV)]
    arrivals = [_remote_copy(src_ref.at[me], land_ref.at[_block_of(peer(r))], sems, r - 1, peer(r)) for r in range(1, N_DEV)]
    return sends, arrivals


def _scatter_begin(name, blocks):
    def body(src_ref, land_ref, send_sems, recv_sems, src_thru, land_thru, token_ref):
        for cp in _split_plan(src_ref, land_ref, (send_sems, recv_sems))[0]:
            cp.start()
        token_ref[...] = jnp.zeros_like(token_ref)

    hbm, sem = pl.BlockSpec(memory_space=pltpu.HBM), pl.BlockSpec(memory_space=pltpu.SEMAPHORE)
    zone = pltpu.HBM(blocks.shape, blocks.dtype)
    *handles, token = pl.pallas_call(
        body, name=name,
        in_specs=(hbm, hbm),
        out_specs=(sem, sem, hbm, hbm, pl.BlockSpec(memory_space=pltpu.VMEM)),
        out_shape=(pltpu.SemaphoreType.DMA((N_LINKS,)), pltpu.SemaphoreType.DMA((N_LINKS,)), zone, zone,
                   jax.ShapeDtypeStruct((8, SLOT), F32)),
        input_output_aliases={0: 2, 1: 3},
        compiler_params=pltpu.CompilerParams(has_side_effects=pltpu.SideEffectType.DATAFLOW_SIDE_EFFECTING),
    )(pltpu.with_memory_space_constraint(blocks, pltpu.HBM),
      pltpu.with_memory_space_constraint(lax.empty(blocks.shape, blocks.dtype), pltpu.HBM))
    return handles, token


def _scatter_end(name, handles, after):
    send_sems, recv_sems, src, zone = handles

    def body(src_ref, land_ref, send_sems, recv_sems, after_ref, src_dead, got_ref):
        sends, arrivals = _split_plan(src_ref, land_ref, (send_sems, recv_sems))
        for cp in arrivals:
            cp.wait_recv()
        for cp in sends:
            cp.wait_send()

    hbm, sem = pl.BlockSpec(memory_space=pltpu.HBM), pl.BlockSpec(memory_space=pltpu.SEMAPHORE)
    sent, landed = pl.pallas_call(
        body, name=name,
        in_specs=(hbm, hbm, sem, sem, pl.BlockSpec(memory_space=pl.ANY)),
        out_specs=(hbm, hbm),
        out_shape=(pltpu.HBM(src.shape, src.dtype), pltpu.HBM(zone.shape, zone.dtype)),
        input_output_aliases={0: 0, 1: 1},
        compiler_params=pltpu.CompilerParams(has_side_effects=pltpu.SideEffectType.DATAFLOW_SIDE_EFFECTING),
    )(src, zone, send_sems, recv_sems, after)
    me = _block_of(_place())
    return lax.dynamic_update_slice_in_dim(landed, lax.dynamic_slice_in_dim(sent, me, 1, axis=0), me, axis=0)


def _adamw_values(wv, gv, mv, vv):
    m2 = ADAM_B1 * mv + (1.0 - ADAM_B1) * gv
    v2 = ADAM_B2 * vv + (1.0 - ADAM_B2) * jnp.square(gv)
    m_hat = m2 / (1.0 - ADAM_B1 ** ADAM_STEP)
    v_hat = v2 / (1.0 - ADAM_B2 ** ADAM_STEP)
    return [-ADAM_LR * (m_hat / (jnp.sqrt(v_hat) + ADAM_EPS) + ADAM_WD * wv), m2, v2]


def _adamw(name, w, g, m, v):
    def fn(rows, consts):
        return _adamw_values(*rows), []

    return _rowwise(name, fn, [w, g, m, v], [], [(w.shape[1], F32)] * 3, tm=_row_tile(w.shape[0]))


def _sum_adamw(name, blocks, w, m, v, after=None):
    rows, width = w.shape
    tm = _row_tile(rows)

    def body(x_ref, w_ref, m_ref, v_ref, *rest):
        acc = x_ref[0].astype(F32)
        for d in range(1, N_DEV):
            acc = acc + x_ref[d].astype(F32)
        rest[-4][...] = acc
        for ref, val in zip(rest[-3:], _adamw_values(w_ref[...], acc, m_ref[...], v_ref[...])):
            ref[...] = val

    ordered = [] if after is None else [after]
    tile = pl.BlockSpec((tm, width), lambda i: (i, 0))
    return pl.pallas_call(
        body, name=name,
        grid=(rows // tm,),
        in_specs=[pl.BlockSpec((N_DEV, tm, width), lambda i: (0, i, 0))] + [tile] * 3 + [pl.BlockSpec(memory_space=pl.ANY)] * len(ordered),
        out_specs=[tile] * 4,
        out_shape=[jax.ShapeDtypeStruct((rows, width), F32)] * 4,
        compiler_params=pltpu.CompilerParams(dimension_semantics=("parallel",)),
    )(blocks, w, m, v, *ordered)


ROW = 1024
FFN_NAMES = ("ffn1_w_gate", "ffn1_w_up", "ffn1_w_down", "ffn2_w_gate", "ffn2_w_up", "ffn2_w_down")
OTHER = {"w_in": "w_in_t", "mla_w_uq": "uq_t", "mla_w_ukv": "ukv_t", "w_out": "w_out"}
BY_COLUMNS = ("ffn1_w_gate", "ffn1_w_up", "ffn2_w_gate", "ffn2_w_up", "w_in", "mla_w_uq", "mla_w_ukv")
SMALL = {
    "ffn1_pre_g": (1024, 1024), "ffn1_post_g": (1024, 1024), "mix_pre_g": (1024, 1024), "mla_q_norm_g": (256, 256),
    "mla_kv_norm_g": (128, 128), "mla_out_g": (512, 512), "gdn_a_log": (8, 128), "gdn_dt_bias": (8, 128),
    "gdn_norm_g": (64, 128), "mix_post_g": (1024, 1024), "ffn2_pre_g": (1024, 1024), "ffn2_post_g": (1024, 1024),
}
CONV_SHAPE = (GDN_CONV, 3 * N_HEADS * GDN_D)
CONV_SHARD = (GDN_CONV, CONV_SHAPE[1] // N_DEV)
CONV_LANES = CONV_SHAPE[0] * CONV_SHAPE[1]
SMALL_ROWS = 8
REDUCE_ROWS = 16


def _pack_small(vecs, conv, rows):
    parts = [_pad_lanes(vecs[n].reshape(1, -1), 0, r) for n, (_, r) in SMALL.items()]
    parts.append(conv.reshape(1, -1))
    flat = jnp.concatenate(parts, axis=1)
    return _pad_lanes(flat, 0, rows * ROW).reshape(rows, ROW)


def _unpack_small(buf):
    flat = buf.reshape(1, -1)
    out, at = {}, 0
    for n, (w, r) in SMALL.items():
        out[n] = flat[:, at:at + w]
        at += r
    return out, flat[0, at:]


def kernel(x, positions, ffn1_pre_g, ffn1_w_gate, ffn1_w_up, ffn1_w_down, ffn1_post_g, mix_pre_g, w_in, mla_q_norm_g, mla_w_uq, mla_kv_norm_g, mla_w_ukv, mla_out_g, gdn_conv_w, gdn_a_log, gdn_dt_bias, gdn_norm_g, w_out, mix_post_g, ffn2_pre_g, ffn2_w_gate, ffn2_w_up, ffn2_w_down, ffn2_post_g, loss_target, m_ffn1_pre_g, m_ffn1_w_gate, m_ffn1_w_up, m_ffn1_w_down, m_ffn1_post_g, m_mix_pre_g, m_w_in, m_mla_q_norm_g, m_mla_w_uq, m_mla_kv_norm_g, m_mla_w_ukv, m_mla_out_g, m_gdn_conv_w, m_gdn_a_log, m_gdn_dt_bias, m_gdn_norm_g, m_w_out, m_mix_post_g, m_ffn2_pre_g, m_ffn2_w_gate, m_ffn2_w_up, m_ffn2_w_down, m_ffn2_post_g, v_ffn1_pre_g, v_ffn1_w_gate, v_ffn1_w_up, v_ffn1_w_down, v_ffn1_post_g, v_mix_pre_g, v_w_in, v_mla_q_norm_g, v_mla_w_uq, v_mla_kv_norm_g, v_mla_w_ukv, v_mla_out_g, v_gdn_conv_w, v_gdn_a_log, v_gdn_dt_bias, v_gdn_norm_g, v_w_out, v_mix_post_g, v_ffn2_pre_g, v_ffn2_w_gate, v_ffn2_w_up, v_ffn2_w_down, v_ffn2_post_g):
    given = dict(locals())
    order = ["ffn1_pre_g", "ffn1_w_gate", "ffn1_w_up", "ffn1_w_down", "ffn1_post_g", "mix_pre_g", "w_in", "mla_q_norm_g",
             "mla_w_uq", "mla_kv_norm_g", "mla_w_ukv", "mla_out_g", "gdn_conv_w", "gdn_a_log", "gdn_dt_bias", "gdn_norm_g",
             "w_out", "mix_post_g", "ffn2_pre_g", "ffn2_w_gate", "ffn2_w_up", "ffn2_w_down", "ffn2_post_g"]
    assert sorted(order) == sorted(list(FFN_NAMES) + list(OTHER) + list(SMALL) + ["gdn_conv_w"])

    def drop_depth(a):
        return a[0] if a.ndim == 3 else a

    wts = {n: drop_depth(given[n]) for n in order}
    mom = {n: drop_depth(given["m_" + n]) for n in order}
    var = {n: drop_depth(given["v_" + n]) for n in order}
    me = _block_of(_place())

    def wire(n):
        return (wts[n].T if n in BY_COLUMNS else wts[n]).astype(BF16)

    (ffn1,) = _exchange("gather_first", _Gather([jnp.stack([wire(n) for n in FFN_NAMES[:3]])]))
    mid = _Gather([wire(n) for n in ("w_in", "mla_w_uq", "mla_w_ukv")] + [wts["gdn_conv_w"]])
    late = _Gather([jnp.stack([wire(n) for n in FFN_NAMES[3:]]), wire("w_out")])
    full = {n: wts[n] for n in SMALL}
    full["ffn1"] = ffn1

    dx, grads, landed, begun, token = _local_step(x[0], positions[0], loss_target[0], full, mid, late)

    grad, outs = {}, {"delta": {}, "new_m": {}, "new_v": {}}

    def finish(n, blocks, after=None):
        flip = n in BY_COLUMNS and wts[n].shape[1] % SLOT != 0
        turn = (lambda a: a.T) if flip else (lambda a: a)
        if n in BY_COLUMNS and not flip:
            grad[n] = _sum_blocks("sum_" + n, blocks, after=after).T
            new = _adamw("adamw_" + n, wts[n], grad[n], mom[n], var[n])
        else:
            total, *new = _sum_adamw("update_" + n, blocks, turn(wts[n]), turn(mom[n]), turn(var[n]), after=after)
            grad[n] = turn(total)
        outs["delta"][n], outs["new_m"][n], outs["new_v"][n] = (turn(a) for a in new)
        return new[2]

    for n, blocks in landed.items():
        token = finish(n, blocks, after=token)
    small_handles = begun.pop("small")
    for n, handles in begun.items():
        token = finish(n, _scatter_end("scatter_" + n + "_end", handles, after=token))

    small_sum = _sum_blocks("sum_small", _scatter_end("reduce_small_end", small_handles, after=token))
    loss = small_sum[REDUCE_ROWS - 1, ROW - 1]
    small_grad, conv_grad_full = _unpack_small(small_sum)
    grad.update(small_grad)
    grad["gdn_conv_w"] = lax.dynamic_slice(conv_grad_full[:CONV_LANES].reshape(CONV_SHAPE), (0, me * CONV_SHARD[1]), CONV_SHARD)
    outs["grad"] = grad
    small = [_pack_small(s, s["gdn_conv_w"].reshape(-1), SMALL_ROWS) for s in (wts, grad, mom, var)]
    for kind, s in zip(("delta", "new_m", "new_v"), _adamw("adamw_small", *small)):
        vecs, conv = _unpack_small(s)
        outs[kind].update(vecs)
        outs[kind]["gdn_conv_w"] = conv[:CONV_SHARD[0] * CONV_SHARD[1]].reshape(CONV_SHARD)
    result = [loss, dx[None]]
    for kind in ("grad", "delta", "new_m", "new_v"):
        result += [outs[kind][n].reshape(given[n].shape) for n in order]
    return tuple(result)
```
